```python
import math
import jax, jax.numpy as jnp
from jax import lax
import numpy as np

D_MODEL = 1024
BATCH = 16
SEQ = 2048
DEPTH = 1

CTX_LEN = 256
GRID_W = 64
RET_HEADS = 4
RET_DK = 64
RET_DV = 128
RET_CHUNK = 128
MLA_HEADS = 4
MLA_NOPE = 128
MLA_ROPE = 64
MLA_V = 128
Q_LORA = 384
KV_LORA = 256
D_MIX = RET_HEADS * RET_DV + MLA_HEADS * MLA_V
D_FF = 4 * D_MODEL
ROPE_BASE = 10000.0
Q_BLOCK = 128
EPS = 1e-6
IN_SPLITS = (RET_HEADS * RET_DK, RET_HEADS * RET_DK, RET_HEADS * RET_DV, RET_HEADS * RET_DV,
             Q_LORA, KV_LORA, MLA_ROPE)
IN_COLS = sum(IN_SPLITS)
SPLIT_POINTS = tuple(int(v) for v in np.cumsum(IN_SPLITS)[:-1])

kernel_name = "hymba_retention_mla_adaln_prefix_block"


def rms_norm(x, g):
    x32 = x.astype(jnp.float32)
    y = x32 * lax.rsqrt(jnp.mean(x32 * x32, axis=-1, keepdims=True) + EPS)
    return (y * g.astype(jnp.float32)).astype(x.dtype)


def modulate(h, shift, scale):
    return h * (1 + scale) + shift


def axial_rope_tables(rows, dim):
    row = jnp.repeat(jnp.arange(rows, dtype=jnp.float32), GRID_W)
    col = jnp.tile(jnp.arange(GRID_W, dtype=jnp.float32), rows)
    n_freq = dim // 4
    freq = ROPE_BASE ** (-jnp.arange(n_freq, dtype=jnp.float32) / n_freq)
    ang = jnp.concatenate([row[:, None] * freq, col[:, None] * freq], axis=-1)
    return jnp.cos(ang)[:, None, :], jnp.sin(ang)[:, None, :]


def apply_rope(x, cos, sin):
    half = x.shape[-1] // 2
    x1, x2 = x[..., :half], x[..., half:]
    return jnp.concatenate([x1 * cos - x2 * sin, x2 * cos + x1 * sin], axis=-1).astype(x.dtype)


def retention_chunked(q, k, v, log_gamma, s0):
    B, L, H, dk = q.shape
    dv = v.shape[-1]
    n = L // RET_CHUNK
    qc = q.reshape(B, n, RET_CHUNK, H, dk)
    kc = k.reshape(B, n, RET_CHUNK, H, dk)
    vc = v.reshape(B, n, RET_CHUNK, H, dv)
    pos = jnp.arange(RET_CHUNK, dtype=jnp.float32)
    diff = pos[:, None] - pos[None, :]
    dec = jnp.where(diff >= 0, jnp.exp(log_gamma[:, None, None] * jnp.maximum(diff, 0.0)), 0.0)
    scores = jnp.einsum('bnihd,bnjhd->bnhij', qc, kc) * dec
    o_intra = jnp.einsum('bnhij,bnjhe->bnihe', scores, vc)
    w_k = jnp.exp(log_gamma[:, None] * (RET_CHUNK - 1 - pos)[None, :])
    kv = jnp.einsum('bnjhd,hj,bnjhe->bnhde', kc, w_k, vc).astype(jnp.float32)
    chunk_decay = jnp.exp(log_gamma * RET_CHUNK)[:, None, None]

    def step(s, kv_n):
        return chunk_decay * s + kv_n, s

    _, s_prev = lax.scan(step, s0.astype(jnp.float32), jnp.moveaxis(kv, 1, 0))
    w_q = jnp.exp(log_gamma[:, None] * (pos + 1.0)[None, :])
    o_cross = jnp.einsum('bnihd,hi,nbhde->bnihe', qc, w_q, s_prev)
    return (o_intra + o_cross).reshape(B, L, H, dv)


def retention_final_state(k, v, log_gamma):
    L = k.shape[1]
    w = jnp.exp(log_gamma[:, None] * (L - 1 - jnp.arange(L, dtype=jnp.float32))[None, :])
    return jnp.einsum('blhd,hl,blhe->bhde', k, w, v).astype(jnp.float32)


def retention_mix(q, k, v, gate, lg_f, lg_b, g_ret, s_f, s_b):
    B, L, H, dv = v.shape
    o_f = retention_chunked(q, k, v, lg_f, s_f)
    o_b = retention_chunked(q[:, ::-1], k[:, ::-1], v[:, ::-1], lg_b, s_b)[:, ::-1]
    o = (o_f + o_b).astype(jnp.float32)
    mu = jnp.mean(o, axis=-1, keepdims=True)
    var = jnp.mean(jnp.square(o - mu), axis=-1, keepdims=True)
    o = (o - mu) * lax.rsqrt(var + EPS) * g_ret.astype(jnp.float32).reshape(H, dv)
    return (o.reshape(B, L, H * dv) * jax.nn.silu(gate.astype(jnp.float32))).astype(gate.dtype)


def attend(q, k, v):
    s = jnp.einsum('bqhd,bkhd->bhqk', q, k).astype(jnp.float32) * (1.0 / math.sqrt(q.shape[-1]))
    p = jax.nn.softmax(s, axis=-1)
    return jnp.einsum('bhqk,bkhe->bqhe', p.astype(v.dtype), v)


def blocked_attention(q, k, v):
    B, L, H, d = q.shape
    qb = q.reshape(B, L // Q_BLOCK, Q_BLOCK, H, d).swapaxes(0, 1)
    out = lax.map(lambda qi: attend(qi, k, v), qb)
    return out.swapaxes(0, 1).reshape(B, L, H, v.shape[-1])


def head_group_inputs(h, w_in, g_q, w_uq, g_kv, w_ukv):
    B, L, _ = h.shape
    r_q, r_k, r_v, r_g, c_q, c_kv, k_pe = jnp.split(h @ w_in, SPLIT_POINTS, axis=-1)
    r_q = r_q.reshape(B, L, RET_HEADS, RET_DK)
    r_k = r_k.reshape(B, L, RET_HEADS, RET_DK) * (RET_DK ** -0.5)
    r_v = r_v.reshape(B, L, RET_HEADS, RET_DV)
    q = (rms_norm(c_q, g_q) @ w_uq).reshape(B, L, MLA_HEADS, MLA_NOPE + MLA_ROPE)
    kv = (rms_norm(c_kv, g_kv) @ w_ukv).reshape(B, L, MLA_HEADS, MLA_NOPE + MLA_V)
    q_nope, q_pe = q[..., :MLA_NOPE], q[..., MLA_NOPE:]
    k_nope, m_v = kv[..., :MLA_NOPE], kv[..., MLA_NOPE:]
    k_pe = k_pe[:, :, None, :]
    return r_q, r_k, r_v, r_g, q_nope, q_pe, k_nope, k_pe, m_v


def mla_qk(q_nope, q_pe, k_nope, k_pe):
    q = jnp.concatenate([q_nope, q_pe], axis=-1)
    k = jnp.concatenate([k_nope, jnp.broadcast_to(k_pe, k_nope.shape[:-1] + (MLA_ROPE,))], axis=-1)
    return q, k


def sq_relu_mlp(h, w1, w2):
    return jnp.square(jax.nn.relu(h @ w1)) @ w2


def _fwd_setup_inputs(seed: int = 0) -> dict:
    key = jax.random.key(seed)
    ks = jax.random.split(key, 24)
    f32 = jnp.float32

    def nrm(k, shape, scale):
        return jax.random.normal(k, shape, f32) * scale

    base_logit = jnp.log(2.0 ** (5.0 + jnp.arange(RET_HEADS, dtype=f32)) - 1.0)
    return {
        "x": nrm(ks[0], (BATCH, SEQ, D_MODEL), 1.0),
        "c": nrm(ks[1], (BATCH, D_MODEL), 1.0),
        "ctx": nrm(ks[2], (BATCH, CTX_LEN, D_MODEL), 1.0),
        "c_ctx": nrm(ks[3], (D_MODEL,), 1.0),
        "w_ada": nrm(ks[4], (DEPTH, D_MODEL, 6 * D_MODEL), 0.5 * D_MODEL ** -0.5),
        "b_ada": nrm(ks[5], (DEPTH, 6 * D_MODEL), 0.01),
        "g_attn": 1.0 + nrm(ks[6], (DEPTH, D_MODEL), 0.05),
        "g_ffn": 1.0 + nrm(ks[7], (DEPTH, D_MODEL), 0.05),
        "w_in": nrm(ks[8], (DEPTH, D_MODEL, IN_COLS), D_MODEL ** -0.5),
        "ret_decay_fwd": base_logit + nrm(ks[9], (DEPTH, RET_HEADS), 0.1),
        "ret_decay_bwd": base_logit + nrm(ks[10], (DEPTH, RET_HEADS), 0.1),
        "g_ret": 1.0 + nrm(ks[11], (DEPTH, RET_HEADS * RET_DV), 0.05),
        "g_q_lora": 1.0 + nrm(ks[12], (DEPTH, Q_LORA), 0.05),
        "w_uq": nrm(ks[13], (DEPTH, Q_LORA, MLA_HEADS * (MLA_NOPE + MLA_ROPE)), Q_LORA ** -0.5),
        "g_kv_lora": 1.0 + nrm(ks[14], (DEPTH, KV_LORA), 0.05),
        "w_ukv": nrm(ks[15], (DEPTH, KV_LORA, MLA_HEADS * (MLA_NOPE + MLA_V)), KV_LORA ** -0.5),
        "w_out": nrm(ks[16], (DEPTH, D_MIX, D_MODEL), D_MIX ** -0.5),
        "w_ff1": nrm(ks[17], (DEPTH, D_MODEL, D_FF), D_MODEL ** -0.5),
        "w_ff2": nrm(ks[18], (DEPTH, D_FF, D_MODEL), D_FF ** -0.5),
        "g_final": 1.0 + nrm(ks[19], (D_MODEL,), 0.05),
    }


def _fwd_reference(x, c, ctx, c_ctx, w_ada, b_ada, g_attn, g_ffn, w_in, ret_decay_fwd, ret_decay_bwd,
              g_ret, g_q_lora, w_uq, g_kv_lora, w_ukv, w_out, w_ff1, w_ff2, g_final):
    B, L, _ = x.shape
    rows = L // GRID_W
    cos, sin = axial_rope_tables(rows, RET_DK)
    for l in range(DEPTH):
        mod = jax.nn.silu(c) @ w_ada[l] + b_ada[l]
        mod_c = jax.nn.silu(c_ctx) @ w_ada[l] + b_ada[l]
        sh_a, sc_a, gt_a, sh_f, sc_f, gt_f = [m[:, None, :] for m in jnp.split(mod, 6, axis=-1)]
        csh_a, csc_a, cgt_a, csh_f, csc_f, cgt_f = jnp.split(mod_c, 6, axis=-1)

        h = modulate(rms_norm(x, g_attn[l]), sh_a, sc_a)
        hc = modulate(rms_norm(ctx, g_attn[l]), csh_a, csc_a)
        rq, rk, rv, rg, qn, qp, kn, kp, mv = head_group_inputs(h, w_in[l], g_q_lora[l], w_uq[l],
                                                                g_kv_lora[l], w_ukv[l])
        rqc, rkc, rvc, rgc, qnc, qpc, knc, kpc, mvc = head_group_inputs(hc, w_in[l], g_q_lora[l], w_uq[l],
                                                                        g_kv_lora[l], w_ukv[l])
        rq, rk = apply_rope(rq, cos, sin), apply_rope(rk, cos, sin)
        qp, kp = apply_rope(qp, cos, sin), apply_rope(kp, cos, sin)

        lg_f = jax.nn.log_sigmoid(ret_decay_fwd[l].astype(jnp.float32))
        lg_b = jax.nn.log_sigmoid(ret_decay_bwd[l].astype(jnp.float32))
        s_f = retention_final_state(rkc, rvc, lg_f)
        s_b = retention_final_state(rkc[:, ::-1], rvc[:, ::-1], lg_b)
        y_ret = retention_mix(rq, rk, rv, rg, lg_f, lg_b, g_ret[l], s_f, s_b)

        q_m, k_m = mla_qk(qn, qp, kn, kp)
        q_mc, k_mc = mla_qk(qnc, qpc, knc, kpc)
        y_mla = blocked_attention(q_m, jnp.concatenate([k_mc, k_m], axis=1),
                                  jnp.concatenate([mvc, mv], axis=1)).reshape(B, L, MLA_HEADS * MLA_V)

        x_mid = x + gt_a * (jnp.concatenate([y_ret, y_mla], axis=-1) @ w_out[l])

        if l + 1 < DEPTH:
            zero_state = jnp.zeros((B, RET_HEADS, RET_DK, RET_DV), jnp.float32)
            y_ret_c = retention_mix(rqc, rkc, rvc, rgc, lg_f, lg_b, g_ret[l], zero_state, zero_state)
            y_mla_c = attend(q_mc, k_mc, mvc).reshape(B, CTX_LEN, MLA_HEADS * MLA_V)
            ctx = ctx + cgt_a * (jnp.concatenate([y_ret_c, y_mla_c], axis=-1) @ w_out[l])
            ctx = ctx + cgt_f * sq_relu_mlp(modulate(rms_norm(ctx, g_ffn[l]), csh_f, csc_f),
                                            w_ff1[l], w_ff2[l])

        x = x_mid + gt_f * sq_relu_mlp(modulate(rms_norm(x_mid, g_ffn[l]), sh_f, sc_f),
                                       w_ff1[l], w_ff2[l])
    return rms_norm(x, g_final)


import jax as _jax
import jax.numpy as _jnp

TWIN_FORMAT = 'train_step'
FWD_PARAMS = ['x', 'c', 'ctx', 'c_ctx', 'w_ada', 'b_ada', 'g_attn', 'g_ffn', 'w_in', 'ret_decay_fwd', 'ret_decay_bwd', 'g_ret', 'g_q_lora', 'w_uq', 'g_kv_lora', 'w_ukv', 'w_out', 'w_ff1', 'w_ff2', 'g_final']
TWIN_WEIGHTS = ['c_ctx', 'w_ada', 'b_ada', 'g_attn', 'g_ffn', 'w_in', 'ret_decay_fwd', 'ret_decay_bwd', 'g_ret', 'g_q_lora', 'w_uq', 'g_kv_lora', 'w_ukv', 'w_out', 'w_ff1', 'w_ff2', 'g_final']
TWIN_DIFF_INPUT = 'x'
TWIN_INPUTS = ['x', 'c', 'ctx', 'c_ctx', 'w_ada', 'b_ada', 'g_attn', 'g_ffn', 'w_in', 'ret_decay_fwd', 'ret_decay_bwd', 'g_ret', 'g_q_lora', 'w_uq', 'g_kv_lora', 'w_ukv', 'w_out', 'w_ff1', 'w_ff2', 'g_final', 'loss_target', 'm_c_ctx', 'm_w_ada', 'm_b_ada', 'm_g_attn', 'm_g_ffn', 'm_w_in', 'm_ret_decay_fwd', 'm_ret_decay_bwd', 'm_g_ret', 'm_g_q_lora', 'm_w_uq', 'm_g_kv_lora', 'm_w_ukv', 'm_w_out', 'm_w_ff1', 'm_w_ff2', 'm_g_final', 'v_c_ctx', 'v_w_ada', 'v_b_ada', 'v_g_attn', 'v_g_ffn', 'v_w_in', 'v_ret_decay_fwd', 'v_ret_decay_bwd', 'v_g_ret', 'v_g_q_lora', 'v_w_uq', 'v_g_kv_lora', 'v_w_ukv', 'v_w_out', 'v_w_ff1', 'v_w_ff2', 'v_g_final']
TWIN_OUTPUTS = ['loss', 'grad_x', 'grad_c_ctx', 'grad_w_ada', 'grad_b_ada', 'grad_g_attn', 'grad_g_ffn', 'grad_w_in', 'grad_ret_decay_fwd', 'grad_ret_decay_bwd', 'grad_g_ret', 'grad_g_q_lora', 'grad_w_uq', 'grad_g_kv_lora', 'grad_w_ukv', 'grad_w_out', 'grad_w_ff1', 'grad_w_ff2', 'grad_g_final', 'delta_c_ctx', 'delta_w_ada', 'delta_b_ada', 'delta_g_attn', 'delta_g_ffn', 'delta_w_in', 'delta_ret_decay_fwd', 'delta_ret_decay_bwd', 'delta_g_ret', 'delta_g_q_lora', 'delta_w_uq', 'delta_g_kv_lora', 'delta_w_ukv', 'delta_w_out', 'delta_w_ff1', 'delta_w_ff2', 'delta_g_final', 'new_m_c_ctx', 'new_m_w_ada', 'new_m_b_ada', 'new_m_g_attn', 'new_m_g_ffn', 'new_m_w_in', 'new_m_ret_decay_fwd', 'new_m_ret_decay_bwd', 'new_m_g_ret', 'new_m_g_q_lora', 'new_m_w_uq', 'new_m_g_kv_lora', 'new_m_w_ukv', 'new_m_w_out', 'new_m_w_ff1', 'new_m_w_ff2', 'new_m_g_final', 'new_v_c_ctx', 'new_v_w_ada', 'new_v_b_ada', 'new_v_g_attn', 'new_v_g_ffn', 'new_v_w_in', 'new_v_ret_decay_fwd', 'new_v_ret_decay_bwd', 'new_v_g_ret', 'new_v_g_q_lora', 'new_v_w_uq', 'new_v_g_kv_lora', 'new_v_w_ukv', 'new_v_w_out', 'new_v_w_ff1', 'new_v_w_ff2', 'new_v_g_final']
TWIN_LEAF_KINDS = {'loss': 'loss', 'grad_x': 'grad_x', 'grad_c_ctx': 'grad_w', 'grad_w_ada': 'grad_w', 'grad_b_ada': 'grad_w', 'grad_g_attn': 'grad_w', 'grad_g_ffn': 'grad_w', 'grad_w_in': 'grad_w', 'grad_ret_decay_fwd': 'grad_w', 'grad_ret_decay_bwd': 'grad_w', 'grad_g_ret': 'grad_w', 'grad_g_q_lora': 'grad_w', 'grad_w_uq': 'grad_w', 'grad_g_kv_lora': 'grad_w', 'grad_w_ukv': 'grad_w', 'grad_w_out': 'grad_w', 'grad_w_ff1': 'grad_w', 'grad_w_ff2': 'grad_w', 'grad_g_final': 'grad_w', 'delta_c_ctx': 'delta_w', 'delta_w_ada': 'delta_w', 'delta_b_ada': 'delta_w', 'delta_g_attn': 'delta_w', 'delta_g_ffn': 'delta_w', 'delta_w_in': 'delta_w', 'delta_ret_decay_fwd': 'delta_w', 'delta_ret_decay_bwd': 'delta_w', 'delta_g_ret': 'delta_w', 'delta_g_q_lora': 'delta_w', 'delta_w_uq': 'delta_w', 'delta_g_kv_lora': 'delta_w', 'delta_w_ukv': 'delta_w', 'delta_w_out': 'delta_w', 'delta_w_ff1': 'delta_w', 'delta_w_ff2': 'delta_w', 'delta_g_final': 'delta_w', 'new_m_c_ctx': 'new_m', 'new_m_w_ada': 'new_m', 'new_m_b_ada': 'new_m', 'new_m_g_attn': 'new_m', 'new_m_g_ffn': 'new_m', 'new_m_w_in': 'new_m', 'new_m_ret_decay_fwd': 'new_m', 'new_m_ret_decay_bwd': 'new_m', 'new_m_g_ret': 'new_m', 'new_m_g_q_lora': 'new_m', 'new_m_w_uq': 'new_m', 'new_m_g_kv_lora': 'new_m', 'new_m_w_ukv': 'new_m', 'new_m_w_out': 'new_m', 'new_m_w_ff1': 'new_m', 'new_m_w_ff2': 'new_m', 'new_m_g_final': 'new_m', 'new_v_c_ctx': 'new_v', 'new_v_w_ada': 'new_v', 'new_v_b_ada': 'new_v', 'new_v_g_attn': 'new_v', 'new_v_g_ffn': 'new_v', 'new_v_w_in': 'new_v', 'new_v_ret_decay_fwd': 'new_v', 'new_v_ret_decay_bwd': 'new_v', 'new_v_g_ret': 'new_v', 'new_v_g_q_lora': 'new_v', 'new_v_w_uq': 'new_v', 'new_v_g_kv_lora': 'new_v', 'new_v_w_ukv': 'new_v', 'new_v_w_out': 'new_v', 'new_v_w_ff1': 'new_v', 'new_v_w_ff2': 'new_v', 'new_v_g_final': 'new_v'}


def _forward(args):
    return _fwd_reference(*[args[k] for k in FWD_PARAMS])


def _output_shape():
    out = _jax.eval_shape(lambda: _forward(_fwd_setup_inputs(0)))
    return out.shape, out.dtype

N_MICROBATCH = 1
ADAM_LR = 0.001
ADAM_B1 = 0.9
ADAM_B2 = 0.999
ADAM_EPS = 1e-08
ADAM_WD = 0.01
ADAM_STEP = 10
PER_EXAMPLE_BATCH_AXIS = {'x': 0, 'c': 0, 'ctx': 0, 'loss_target': 0}
SHARED_INPUTS = []
_WEIGHT_DTYPES = {'c_ctx': _jnp.float32, 'w_ada': _jnp.float32, 'b_ada': _jnp.float32, 'g_attn': _jnp.float32, 'g_ffn': _jnp.float32, 'w_in': _jnp.float32, 'ret_decay_fwd': _jnp.float32, 'ret_decay_bwd': _jnp.float32, 'g_ret': _jnp.float32, 'g_q_lora': _jnp.float32, 'w_uq': _jnp.float32, 'g_kv_lora': _jnp.float32, 'w_ukv': _jnp.float32, 'w_out': _jnp.float32, 'w_ff1': _jnp.float32, 'w_ff2': _jnp.float32, 'g_final': _jnp.float32}
MOMENT_SCALE = {'c_ctx': 1.486459e-02, 'w_ada': 1.692355e-01, 'b_ada': 3.092276e-01, 'g_attn': 4.946364e-02, 'g_ffn': 8.254370e-02, 'w_in': 3.954884e-02, 'ret_decay_fwd': 2.156234e-01, 'ret_decay_bwd': 1.933502e-01, 'g_ret': 3.622585e-02, 'g_q_lora': 6.594313e-03, 'w_uq': 4.409190e-03, 'g_kv_lora': 3.233343e-02, 'w_ukv': 1.652764e-02, 'w_out': 2.904256e-02, 'w_ff1': 4.439181e-02, 'w_ff2': 1.014609e-01, 'g_final': 3.227993e+01}


def _to_microbatches(a, axis):
    t = _jnp.moveaxis(a, axis, 0)
    t = t.reshape((N_MICROBATCH, t.shape[0] // N_MICROBATCH) + t.shape[1:])
    return _jnp.moveaxis(t, 1, axis + 1)


def setup_inputs(seed: int = 0) -> dict:
    inp = _fwd_setup_inputs(seed)
    key = _jax.random.fold_in(_jax.random.key(seed), 7919)
    shape, _ = _output_shape()
    out = dict(inp)
    out["loss_target"] = _jax.random.normal(_jax.random.fold_in(key, 0), shape, _jnp.float32)
    for i, name in enumerate(TWIN_WEIGHTS):
        w = inp[name].astype(_jnp.float32)
        if MOMENT_SCALE is None:
            s = _jnp.sqrt(_jnp.mean(_jnp.square(w)) + 1e-30)
        else:
            s = MOMENT_SCALE[name]
        km, kv = _jax.random.split(_jax.random.fold_in(key, i + 1))
        out[name] = w
        out["m_" + name] = s * _jax.random.normal(km, w.shape, _jnp.float32)
        out["v_" + name] = (s * s) * _jax.random.uniform(kv, w.shape, _jnp.float32, 0.5, 1.5)
    if N_MICROBATCH > 1:
        for name, axis in PER_EXAMPLE_BATCH_AXIS.items():
            out[name] = _to_microbatches(out[name], axis)
    return {'x': out['x'], 'c': out['c'], 'ctx': out['ctx'], 'c_ctx': out['c_ctx'], 'w_ada': out['w_ada'], 'b_ada': out['b_ada'], 'g_attn': out['g_attn'], 'g_ffn': out['g_ffn'], 'w_in': out['w_in'], 'ret_decay_fwd': out['ret_decay_fwd'], 'ret_decay_bwd': out['ret_decay_bwd'], 'g_ret': out['g_ret'], 'g_q_lora': out['g_q_lora'], 'w_uq': out['w_uq'], 'g_kv_lora': out['g_kv_lora'], 'w_ukv': out['w_ukv'], 'w_out': out['w_out'], 'w_ff1': out['w_ff1'], 'w_ff2': out['w_ff2'], 'g_final': out['g_final'], 'loss_target': out['loss_target'], 'm_c_ctx': out['m_c_ctx'], 'm_w_ada': out['m_w_ada'], 'm_b_ada': out['m_b_ada'], 'm_g_attn': out['m_g_attn'], 'm_g_ffn': out['m_g_ffn'], 'm_w_in': out['m_w_in'], 'm_ret_decay_fwd': out['m_ret_decay_fwd'], 'm_ret_decay_bwd': out['m_ret_decay_bwd'], 'm_g_ret': out['m_g_ret'], 'm_g_q_lora': out['m_g_q_lora'], 'm_w_uq': out['m_w_uq'], 'm_g_kv_lora': out['m_g_kv_lora'], 'm_w_ukv': out['m_w_ukv'], 'm_w_out': out['m_w_out'], 'm_w_ff1': out['m_w_ff1'], 'm_w_ff2': out['m_w_ff2'], 'm_g_final': out['m_g_final'], 'v_c_ctx': out['v_c_ctx'], 'v_w_ada': out['v_w_ada'], 'v_b_ada': out['v_b_ada'], 'v_g_attn': out['v_g_attn'], 'v_g_ffn': out['v_g_ffn'], 'v_w_in': out['v_w_in'], 'v_ret_decay_fwd': out['v_ret_decay_fwd'], 'v_ret_decay_bwd': out['v_ret_decay_bwd'], 'v_g_ret': out['v_g_ret'], 'v_g_q_lora': out['v_g_q_lora'], 'v_w_uq': out['v_w_uq'], 'v_g_kv_lora': out['v_g_kv_lora'], 'v_w_ukv': out['v_w_ukv'], 'v_w_out': out['v_w_out'], 'v_w_ff1': out['v_w_ff1'], 'v_w_ff2': out['v_w_ff2'], 'v_g_final': out['v_g_final']}


def _loss(weights, diff, rest, loss_target):
    with _jax.named_scope("forward"):
        args = {**rest, TWIN_DIFF_INPUT: diff, **{k: w.astype(_WEIGHT_DTYPES[k]) for k, w in weights.items()}}
        y = _forward(args)
    with _jax.named_scope("loss_head"):
        err = _jnp.square(y.astype(_jnp.float32) - loss_target)
        return 0.5 * _jnp.sum(_jnp.mean(err, axis=-1)) if err.ndim else 0.5 * err


def _adamw(w, g, m, v):
    m = ADAM_B1 * m + (1.0 - ADAM_B1) * g
    v = ADAM_B2 * v + (1.0 - ADAM_B2) * _jnp.square(g)
    m_hat = m / (1.0 - ADAM_B1 ** ADAM_STEP)
    v_hat = v / (1.0 - ADAM_B2 ** ADAM_STEP)
    delta = -ADAM_LR * (m_hat / (_jnp.sqrt(v_hat) + ADAM_EPS) + ADAM_WD * w)
    return delta, m, v


def reference(x, c, ctx, c_ctx, w_ada, b_ada, g_attn, g_ffn, w_in, ret_decay_fwd, ret_decay_bwd, g_ret, g_q_lora, w_uq, g_kv_lora, w_ukv, w_out, w_ff1, w_ff2, g_final, loss_target, m_c_ctx, m_w_ada, m_b_ada, m_g_attn, m_g_ffn, m_w_in, m_ret_decay_fwd, m_ret_decay_bwd, m_g_ret, m_g_q_lora, m_w_uq, m_g_kv_lora, m_w_ukv, m_w_out, m_w_ff1, m_w_ff2, m_g_final, v_c_ctx, v_w_ada, v_b_ada, v_g_attn, v_g_ffn, v_w_in, v_ret_decay_fwd, v_ret_decay_bwd, v_g_ret, v_g_q_lora, v_w_uq, v_g_kv_lora, v_w_ukv, v_w_out, v_w_ff1, v_w_ff2, v_g_final):
    given = dict(x=x, c=c, ctx=ctx, c_ctx=c_ctx, w_ada=w_ada, b_ada=b_ada, g_attn=g_attn, g_ffn=g_ffn, w_in=w_in, ret_decay_fwd=ret_decay_fwd, ret_decay_bwd=ret_decay_bwd, g_ret=g_ret, g_q_lora=g_q_lora, w_uq=w_uq, g_kv_lora=g_kv_lora, w_ukv=w_ukv, w_out=w_out, w_ff1=w_ff1, w_ff2=w_ff2, g_final=g_final, loss_target=loss_target, m_c_ctx=m_c_ctx, m_w_ada=m_w_ada, m_b_ada=m_b_ada, m_g_attn=m_g_attn, m_g_ffn=m_g_ffn, m_w_in=m_w_in, m_ret_decay_fwd=m_ret_decay_fwd, m_ret_decay_bwd=m_ret_decay_bwd, m_g_ret=m_g_ret, m_g_q_lora=m_g_q_lora, m_w_uq=m_w_uq, m_g_kv_lora=m_g_kv_lora, m_w_ukv=m_w_ukv, m_w_out=m_w_out, m_w_ff1=m_w_ff1, m_w_ff2=m_w_ff2, m_g_final=m_g_final, v_c_ctx=v_c_ctx, v_w_ada=v_w_ada, v_b_ada=v_b_ada, v_g_attn=v_g_attn, v_g_ffn=v_g_ffn, v_w_in=v_w_in, v_ret_decay_fwd=v_ret_decay_fwd, v_ret_decay_bwd=v_ret_decay_bwd, v_g_ret=v_g_ret, v_g_q_lora=v_g_q_lora, v_w_uq=v_w_uq, v_g_kv_lora=v_g_kv_lora, v_w_ukv=v_w_ukv, v_w_out=v_w_out, v_w_ff1=v_w_ff1, v_w_ff2=v_w_ff2, v_g_final=v_g_final)
    weights = {n: given[n] for n in TWIN_WEIGHTS}
    shared = {n: given[n] for n in SHARED_INPUTS}
    per_example = {n: given[n] for n in ['x', 'c', 'ctx']}
    grad_fn = _jax.value_and_grad(_loss, argnums=(0, 1))

    def one_microbatch(ex, loss_target):
        ex = dict(ex)
        diff = ex.pop(TWIN_DIFF_INPUT)
        return grad_fn(weights, diff, {**shared, **ex}, loss_target)

    if N_MICROBATCH == 1:
        loss, (grad_w, grad_x) = one_microbatch(per_example, given["loss_target"])
    else:
        def body(carry, xs):
            loss_sum, grad_sum = carry
            l_k, (gw_k, gx_k) = one_microbatch(xs[0], xs[1])
            with _jax.named_scope("update"):
                return (loss_sum + l_k, _jax.tree.map(_jnp.add, grad_sum, gw_k)), gx_k

        init = (_jnp.zeros((), _jnp.float32), _jax.tree.map(_jnp.zeros_like, weights))
        (loss, grad_w), grad_x = _jax.lax.scan(body, init, (per_example, given["loss_target"]))
    with _jax.named_scope("update"):
        delta_w, new_m, new_v = {}, {}, {}
        for n in TWIN_WEIGHTS:
            delta_w[n], new_m[n], new_v[n] = _adamw(weights[n], grad_w[n], given["m_" + n], given["v_" + n])
    return (loss, grad_x, *[grad_w[n] for n in TWIN_WEIGHTS], *[delta_w[n] for n in TWIN_WEIGHTS],
            *[new_m[n] for n in TWIN_WEIGHTS], *[new_v[n] for n in TWIN_WEIGHTS])
```

```python
import functools
import math

import jax
import jax.numpy as jnp
from jax import lax
from jax.experimental import pallas as pl
from jax.experimental.pallas import tpu as pltpu

F32 = jnp.float32
BF16 = jnp.bfloat16
MESH = pl.DeviceIdType.MESH

EPS = 1e-6
D_MODEL = 1024
D_FF = 4096
HEADS = 4
RET_DK = 64
RET_DV = 128
MLA_NOPE = 128
MLA_ROPE = 64
MLA_HEAD = 256
Q_LORA = 384
KV_LORA = 256
GRID_W = 64
ROPE_BASE = 10000.0
IN_COLS = 2240
IN_PAD = 2304
PG_COLS = 1152
N_CHIPS = 4
N_DEV = 8
LANES = 128
ADAM_LR = 0.001
ADAM_B1 = 0.9
ADAM_B2 = 0.999
ADAM_EPS = 1e-08
ADAM_WD = 0.01
ADAM_STEP = 10
VMEM_LIMIT = 56 * 1024 * 1024


def _dot(a, b):
    return jnp.dot(a, b, preferred_element_type=F32)


def _dot_nt(a, b):
    return lax.dot_general(a, b, (((1,), (1,)), ((), ())), preferred_element_type=F32)


def _dot_tn(a, b):
    return lax.dot_general(a, b, (((0,), (0,)), ((), ())), preferred_element_type=F32)


def _params(sem=None, vmem=None):
    return pltpu.CompilerParams(dimension_semantics=sem, vmem_limit_bytes=vmem)


def _full(shape):
    n = len(shape)
    return pl.BlockSpec(shape, lambda *_: (0,) * n)


def _rope(x, cos, sin):
    w = x.shape[-1]
    lo = (lax.broadcasted_iota(jnp.int32, (1, w), 1) % 64) < 32
    swapped = jnp.where(lo, pltpu.roll(x, w - 32, 1), pltpu.roll(x, 32, 1))
    return x * cos + swapped * sin


def _rope_t(g, cos, sin):
    w = g.shape[-1]
    lo = (lax.broadcasted_iota(jnp.int32, (1, w), 1) % 64) < 32
    t = g * sin
    swapped = jnp.where(lo, pltpu.roll(t, w - 32, 1), pltpu.roll(t, 32, 1))
    return g * cos + swapped


def _rope_tables(seq, tm):
    rows = seq // GRID_W
    row = jnp.repeat(jnp.arange(rows, dtype=F32), GRID_W)
    col = jnp.tile(jnp.arange(GRID_W, dtype=F32), rows)
    n_freq = RET_DK // 4
    freq = ROPE_BASE ** (-jnp.arange(n_freq, dtype=F32) / n_freq)
    ang = jnp.concatenate([row[:, None] * freq, col[:, None] * freq], axis=-1)
    cos, sin = jnp.cos(ang), jnp.sin(ang)
    cos_t = jnp.tile(jnp.concatenate([cos, cos], -1), (1, HEADS))
    sin_t = jnp.tile(jnp.concatenate([-sin, sin], -1), (1, HEADS))
    cos_t = jnp.concatenate([cos_t, jnp.ones((tm, 4 * RET_DK), F32)], 0)
    sin_t = jnp.concatenate([sin_t, jnp.zeros((tm, 4 * RET_DK), F32)], 0)
    return cos_t, sin_t


def _cast_bf16(w, name):
    r, c = w.shape
    rb = math.gcd(r, 256)

    def body(w_ref, o_ref):
        o_ref[...] = w_ref[...].astype(BF16)

    return pl.pallas_call(
        body, name=name, grid=(r // rb,),
        in_specs=[pl.BlockSpec((rb, c), lambda i: (i, 0))],
        out_specs=pl.BlockSpec((rb, c), lambda i: (i, 0)),
        out_shape=jax.ShapeDtypeStruct((r, c), BF16),
        compiler_params=_params(("parallel",)),
    )(w)


def _adamw(w, g, m, v, name):
    r, c = w.shape
    rb = r
    for cand in (256, 128, 64, 32, 16, 8):
        if r % cand == 0 and cand * c * 4 <= (1 << 20):
            rb = cand
            break
    if r * c * 4 <= (1 << 20):
        rb = r

    def body(w_ref, g_ref, m_ref, v_ref, d_ref, mo_ref, vo_ref):
        gg = g_ref[...]
        mn = ADAM_B1 * m_ref[...] + (1.0 - ADAM_B1) * gg
        vn = ADAM_B2 * v_ref[...] + (1.0 - ADAM_B2) * (gg * gg)
        m_hat = mn / (1.0 - ADAM_B1 ** ADAM_STEP)
        v_hat = vn / (1.0 - ADAM_B2 ** ADAM_STEP)
        d_ref[...] = -ADAM_LR * (m_hat / (jnp.sqrt(v_hat) + ADAM_EPS) + ADAM_WD * w_ref[...])
        mo_ref[...] = mn
        vo_ref[...] = vn

    spec = pl.BlockSpec((rb, c), lambda i: (i, 0))
    shp = jax.ShapeDtypeStruct((r, c), F32)
    return pl.pallas_call(
        body, name=name, grid=(r // rb,), in_specs=[spec] * 4, out_specs=[spec] * 3, out_shape=[shp] * 3,
        compiler_params=_params(("parallel",)),
    )(w, g, m, v)


def _decay_prep(dec):
    def body(d_ref, lg_ref, sg_ref):
        d = d_ref[...]
        lg_ref[...] = jnp.minimum(d, 0.0) - jnp.log(1.0 + jnp.exp(-jnp.abs(d)))
        sg_ref[...] = 1.0 / (1.0 + jnp.exp(d))

    shp = jax.ShapeDtypeStruct(dec.shape, F32)
    return pl.pallas_call(body, name="decay_prep", out_shape=[shp, shp])(dec)


def _mod_fwd(a_in, w_ada, b_sh):
    rows, d = a_in.shape
    n = w_ada.shape[1]
    bn = 512

    def body(a_ref, w_ref, b_ref, o_ref):
        a = a_ref[...]
        s = (a / (1.0 + jnp.exp(-a))).astype(BF16)
        o_ref[...] = _dot(s, w_ref[...].astype(BF16)) + b_ref[...]

    return pl.pallas_call(
        body, name="mod_fwd", grid=(n // bn,),
        in_specs=[_full((rows, d)), pl.BlockSpec((d, bn), lambda j: (0, j)), pl.BlockSpec((1, bn), lambda j: (0, j))],
        out_specs=pl.BlockSpec((rows, bn), lambda j: (0, j)),
        out_shape=jax.ShapeDtypeStruct((rows, n), F32),
        compiler_params=_params(("parallel",)),
    )(a_in, w_ada, b_sh)


def _mod_bwd(a_in, dm, w_ada):
    rows, d = a_in.shape
    n = w_ada.shape[1]
    bn = 512
    nb = n // bn

    def body(a_ref, dm_ref, w_ref, gw_ref, da_ref):
        j = pl.program_id(0)
        a = a_ref[...]
        s = (a / (1.0 + jnp.exp(-a))).astype(BF16)
        dmb = dm_ref[...].astype(BF16)
        gw_ref[...] = _dot_tn(s, dmb)
        part = _dot_nt(dmb, w_ref[...].astype(BF16))

        @pl.when(j == 0)
        def _():
            da_ref[...] = part

        @pl.when(j > 0)
        def _():
            da_ref[...] += part

    return pl.pallas_call(
        body, name="mod_bwd", grid=(nb,),
        in_specs=[_full((rows, d)), pl.BlockSpec((rows, bn), lambda j: (0, j)), pl.BlockSpec((d, bn), lambda j: (0, j))],
        out_specs=[pl.BlockSpec((d, bn), lambda j: (0, j)), _full((rows, d))],
        out_shape=[jax.ShapeDtypeStruct((d, n), F32), jax.ShapeDtypeStruct((rows, d), F32)],
        compiler_params=_params(("arbitrary",)),
    )(a_in, dm, w_ada)


def _pre_fwd(x2, ctx2, modv, g_attn, w_in, g_q, g_kv, w_uq, w_ukv, cos_t, sin_t, *, seq, tm):
    t_lat, d = x2.shape
    t_ctx = ctx2.shape[0]
    nl, nc = t_lat // tm, t_ctx // tm
    n_all = t_lat + t_ctx
    tpe = seq // tm
    nex = t_lat // seq

    def body(x_ref, c_ref, mod_ref, g_ref, win_ref, gq_ref, gkv_ref, wuq_ref, wukv_ref, cos_ref, sin_ref,
             h_ref, pg_ref, rq_ref, rk_ref, rv_ref, nq_ref, nkv_ref, q_ref, k_ref, v_ref):
        i = pl.program_id(0)
        xt = jnp.where(i < nl, x_ref[...], c_ref[...])
        sh = mod_ref[0, 0:1, :]
        sc = mod_ref[0, 1:2, :]
        r = lax.rsqrt(jnp.mean(xt * xt, axis=-1, keepdims=True) + EPS)
        hb = ((xt * r) * g_ref[...] * (1.0 + sc) + sh).astype(BF16)
        h_ref[...] = hb
        p = _dot(hb, win_ref[...])
        cos = cos_ref[...]
        sin = sin_ref[...]
        rq_ref[...] = _rope(p[:, 0:256], cos, sin).astype(BF16)
        rk_ref[...] = _rope(p[:, 256:512] * (RET_DK ** -0.5), cos, sin).astype(BF16)
        rv_ref[...] = p[:, 512:1024].astype(BF16)
        pg_ref[...] = p[:, 1024:2176]
        cq = p[:, 1536:1920]
        ckv = p[:, 1920:2176]
        nqb = (cq * lax.rsqrt(jnp.mean(cq * cq, axis=-1, keepdims=True) + EPS) * gq_ref[...]).astype(BF16)
        nkvb = (ckv * lax.rsqrt(jnp.mean(ckv * ckv, axis=-1, keepdims=True) + EPS) * gkv_ref[...]).astype(BF16)
        nq_ref[...] = nqb
        nkv_ref[...] = nkvb
        cos1 = cos[:, 0:LANES]
        sin1 = sin[:, 0:LANES]
        kpe = _rope(p[:, 2176:2304], cos1, sin1).astype(BF16)
        for hd in range(HEADS):
            o = hd * MLA_HEAD
            qh = _dot(nqb, wuq_ref[hd])
            q_ref[:, o:o + 128] = qh[:, 0:128].astype(BF16)
            q_ref[:, o + 128:o + 256] = _rope(qh[:, 128:256], cos1, sin1).astype(BF16)
            kvh = _dot(nkvb, wukv_ref[hd])
            k_ref[:, o:o + 128] = kvh[:, 0:128].astype(BF16)
            k_ref[:, o + 128:o + 256] = kpe
            v_ref[:, hd * 128:(hd + 1) * 128] = kvh[:, 128:256].astype(BF16)

    def tile(width):
        return pl.BlockSpec((tm, width), lambda i: (i, 0))

    widths = (d, PG_COLS, 256, 256, 512, Q_LORA, KV_LORA, HEADS * MLA_HEAD, HEADS * MLA_HEAD, HEADS * 128)
    dtypes = (BF16, F32, BF16, BF16, BF16, BF16, BF16, BF16, BF16, BF16)
    tab = pl.BlockSpec((tm, 256), lambda i: (jnp.where(i < nl, i % tpe, tpe), 0))
    return pl.pallas_call(
        body, name="pre_fwd", grid=(nl + nc,),
        in_specs=[
            pl.BlockSpec((tm, d), lambda i: (jnp.minimum(i, nl - 1), 0)),
            pl.BlockSpec((tm, d), lambda i: (jnp.maximum(i - nl, 0), 0)),
            pl.BlockSpec((1, 8, d), lambda i: (jnp.minimum(i // tpe, nex), 0, 0)),
            _full((1, d)), _full(w_in.shape), _full((1, Q_LORA)), _full((1, KV_LORA)),
            _full(w_uq.shape), _full(w_ukv.shape), tab, tab,
        ],
        out_specs=[tile(w) for w in widths],
        out_shape=[jax.ShapeDtypeStruct((n_all, w), dt) for w, dt in zip(widths, dtypes)],
        compiler_params=_params(("parallel",), VMEM_LIMIT),
    )(x2, ctx2, modv, g_attn, w_in, g_q, g_kv, w_uq, w_ukv, cos_t, sin_t)


def _post(yret, ymla, x2, tgt2, modv, g_ffn, g_fin, w_out, w_ff1, w_ff2, *, seq, tm):
    t_lat, d = x2.shape
    nl = t_lat // tm
    tpe = seq // tm
    nex = t_lat // seq
    n_slab = w_ff1.shape[0]
    fs = w_ff1.shape[2]

    def body(yr_ref, ym_ref, x_ref, t_ref, mod_ref, gf_ref, gl_ref, wo_ref, w1_ref, w2_ref,
             mix_ref, a_ref, du_ref, h2_ref, df_ref, dmo_ref, dmix_ref, dxm_ref, st_ref, ru_ref):
        i = pl.program_id(0)
        gt_a = mod_ref[0, 2:3, :]
        sh_f = mod_ref[0, 3:4, :]
        sc_f = mod_ref[0, 4:5, :]
        gt_f = mod_ref[0, 5:6, :]
        g_ffn_v = gf_ref[...]
        g_fin_v = gl_ref[...]
        yr = yr_ref[...]
        ym = ym_ref[...]
        mix_ref[:, 0:512] = yr
        mix_ref[:, 512:1024] = ym
        op = _dot(yr, wo_ref[0:512, :]) + _dot(ym, wo_ref[512:1024, :])
        x_mid = x_ref[...] + gt_a * op
        r2 = lax.rsqrt(jnp.mean(x_mid * x_mid, axis=-1, keepdims=True) + EPS)
        xh2 = x_mid * r2
        h2b = (xh2 * g_ffn_v * (1.0 + sc_f) + sh_f).astype(BF16)
        h2_ref[...] = h2b
        f = jnp.zeros((tm, d), F32)
        for s in range(n_slab):
            ru = jnp.maximum(_dot(h2b, w1_ref[s]), 0.0)
            ru_ref[:, s * fs:(s + 1) * fs] = ru
            ab = (ru * ru).astype(BF16)
            a_ref[:, s * fs:(s + 1) * fs] = ab
            f = f + _dot(ab, w2_ref[s * fs:(s + 1) * fs, :])
        x_out = x_mid + gt_f * f
        r3 = lax.rsqrt(jnp.mean(x_out * x_out, axis=-1, keepdims=True) + EPS)
        xh3 = x_out * r3
        err = xh3 * g_fin_v - t_ref[...]
        dy = err * (1.0 / d)
        dxh3 = dy * g_fin_v
        dx_out = r3 * (dxh3 - xh3 * jnp.mean(dxh3 * xh3, axis=-1, keepdims=True))
        dfb = (dx_out * gt_f).astype(BF16)
        df_ref[...] = dfb
        dh2 = jnp.zeros((tm, d), F32)
        for s in range(n_slab):
            da = _dot_nt(dfb, w2_ref[s * fs:(s + 1) * fs, :])
            dub = (da * (2.0 * ru_ref[:, s * fs:(s + 1) * fs])).astype(BF16)
            du_ref[:, s * fs:(s + 1) * fs] = dub
            dh2 = dh2 + _dot_nt(dub, w1_ref[s])
        dxh2 = dh2 * (1.0 + sc_f) * g_ffn_v
        dx_mid = dx_out + r2 * (dxh2 - xh2 * jnp.mean(dxh2 * xh2, axis=-1, keepdims=True))
        dxm_ref[...] = dx_mid
        dmob = (dx_mid * gt_a).astype(BF16)
        dmo_ref[...] = dmob
        dmix_ref[...] = _dot_nt(dmob, wo_ref[...]).astype(BF16)

        def rsum(v):
            return jnp.sum(v, axis=0, keepdims=True)

        stats = jnp.concatenate([
            rsum(dh2), rsum(dh2 * xh2 * g_ffn_v), rsum(dx_out * f), rsum(dx_mid * op),
            rsum(dh2 * (1.0 + sc_f) * xh2), rsum(dy * xh3), rsum(err * err), jnp.zeros((1, d), F32)], axis=0)

        @pl.when(i % tpe == 0)
        def _():
            st_ref[0] = stats

        @pl.when(i % tpe != 0)
        def _():
            st_ref[0] += stats

    def tile(width):
        return pl.BlockSpec((tm, width), lambda i: (i, 0))

    widths = (d, D_FF, D_FF, d, d, d, d, d)
    dtypes = (BF16, BF16, BF16, BF16, BF16, BF16, BF16, F32)
    const = pl.Buffered(1)
    return pl.pallas_call(
        body, name="post", grid=(nl,),
        in_specs=[
            tile(512), tile(512), tile(d), tile(d),
            pl.BlockSpec((1, 8, d), lambda i: (i // tpe, 0, 0)),
            _full((1, d)), _full((1, d)),
            pl.BlockSpec(w_out.shape, lambda i: (0, 0), pipeline_mode=const),
            pl.BlockSpec(w_ff1.shape, lambda i: (0, 0, 0), pipeline_mode=const),
            pl.BlockSpec(w_ff2.shape, lambda i: (0, 0), pipeline_mode=const),
        ],
        out_specs=[tile(w) for w in widths] + [pl.BlockSpec((1, 8, d), lambda i: (i // tpe, 0, 0))],
        out_shape=[jax.ShapeDtypeStruct((t_lat, w), dt) for w, dt in zip(widths, dtypes)]
        + [jax.ShapeDtypeStruct((nex, 8, d), F32)],
        scratch_shapes=[pltpu.VMEM((tm, D_FF), F32)],
        compiler_params=_params(("arbitrary",), VMEM_LIMIT),
    )(yret, ymla, x2, tgt2, modv, g_ffn, g_fin, w_out, w_ff1, w_ff2)


def _pre_bwd(x2, ctx2, modv, g_attn, pg, drq, drk, dkc_r, drv, dvc_r, drg, dq_m, dkl, dkc, dvl, dvc, dxm,
             w_in, g_q, g_kv, w_uq, w_ukv, cos_t, sin_t, *, seq, tm):
    t_lat, d = x2.shape
    t_ctx = ctx2.shape[0]
    nl, nc = t_lat // tm, t_ctx // tm
    n_all = t_lat + t_ctx
    tpe = seq // tm
    nex = t_lat // seq

    def body(x_ref, c_ref, mod_ref, g_ref, pg_ref, drq_ref, drk_ref, dkcr_ref, drv_ref, dvcr_ref, drg_ref,
             dq_ref, dkl_ref, dkc_ref, dvl_ref, dvc_ref, dxm_ref, win_ref, gq_ref, gkv_ref, wuq_ref, wukv_ref,
             cos_ref, sin_ref, dpb_ref, dqf_ref, dkvf_ref, gx_ref, st_ref):
        i = pl.program_id(0)
        lat = i < nl
        latf = lat.astype(F32)
        cos = cos_ref[...]
        sin = sin_ref[...]
        cos1 = cos[:, 0:LANES]
        sin1 = sin[:, 0:LANES]
        d_rq = _rope_t(drq_ref[...] * latf, cos, sin)
        d_rk = _rope_t(jnp.where(lat, drk_ref[...], dkcr_ref[...]), cos, sin) * (RET_DK ** -0.5)
        d_rv = jnp.where(lat, drv_ref[...], dvcr_ref[...])
        d_rg = drg_ref[...] * latf
        dq_all = dq_ref[...] * latf
        dk_all = jnp.where(lat, dkl_ref[...], dkc_ref[...])
        dv_all = jnp.where(lat, dvl_ref[...], dvc_ref[...])
        dnq = jnp.zeros((tm, Q_LORA), F32)
        dnkv = jnp.zeros((tm, KV_LORA), F32)
        dkpe = jnp.zeros((tm, LANES), F32)
        for hd in range(HEADS):
            o = hd * MLA_HEAD
            dqh = jnp.concatenate([dq_all[:, o:o + 128], _rope_t(dq_all[:, o + 128:o + 256], cos1, sin1)],
                                  axis=1).astype(BF16)
            dqf_ref[:, o:o + 256] = dqh
            dnq = dnq + _dot_nt(dqh, wuq_ref[hd])
            dkpe = dkpe + dk_all[:, o + 128:o + 256]
            dkvh = jnp.concatenate([dk_all[:, o:o + 128], dv_all[:, hd * 128:(hd + 1) * 128]], axis=1).astype(BF16)
            dkvf_ref[:, o:o + 256] = dkvh
            dnkv = dnkv + _dot_nt(dkvh, wukv_ref[hd])
        d_kpe = _rope_t(dkpe, cos1, sin1)
        pgv = pg_ref[...]
        cq = pgv[:, 512:896]
        ckv = pgv[:, 896:1152]
        rq_ = lax.rsqrt(jnp.mean(cq * cq, axis=-1, keepdims=True) + EPS)
        cqh = cq * rq_
        dcqh = dnq * gq_ref[...]
        d_cq = rq_ * (dcqh - cqh * jnp.mean(dcqh * cqh, axis=-1, keepdims=True))
        rkv_ = lax.rsqrt(jnp.mean(ckv * ckv, axis=-1, keepdims=True) + EPS)
        ckvh = ckv * rkv_
        dckvh = dnkv * gkv_ref[...]
        d_ckv = rkv_ * (dckvh - ckvh * jnp.mean(dckvh * ckvh, axis=-1, keepdims=True))
        dpb = jnp.concatenate([d_rq, d_rk, d_rv, d_rg, d_cq, d_ckv, d_kpe], axis=1).astype(BF16)
        dpb_ref[...] = dpb
        dh = _dot_nt(dpb, win_ref[...])
        xt = jnp.where(lat, x_ref[...], c_ref[...])
        sc = mod_ref[0, 1:2, :]
        g = g_ref[...]
        r = lax.rsqrt(jnp.mean(xt * xt, axis=-1, keepdims=True) + EPS)
        xh = xt * r
        dxh = dh * (1.0 + sc) * g
        dx = r * (dxh - xh * jnp.mean(dxh * xh, axis=-1, keepdims=True))

        @pl.when(lat)
        def _():
            gx_ref[...] = dxm_ref[...] + dx

        def rsum(v):
            return jnp.sum(v, axis=0, keepdims=True)

        def widen(v):
            return jnp.concatenate([v, jnp.zeros((1, d - v.shape[1]), F32)], axis=1)

        stats = jnp.concatenate([
            rsum(dh), rsum(dh * xh * g), rsum(dh * (1.0 + sc) * xh), widen(rsum(dnq * cqh)), widen(rsum(dnkv * ckvh)),
            jnp.zeros((3, d), F32)], axis=0)
        first = jnp.logical_or(jnp.logical_and(lat, i % tpe == 0), i == nl)

        @pl.when(first)
        def _():
            st_ref[0] = stats

        @pl.when(jnp.logical_not(first))
        def _():
            st_ref[0] += stats

    def lat_tile(width):
        return pl.BlockSpec((tm, width), lambda i: (jnp.minimum(i, nl - 1), 0))

    def ctx_tile(width):
        return pl.BlockSpec((tm, width), lambda i: (jnp.maximum(i - nl, 0), 0))

    def tile(width):
        return pl.BlockSpec((tm, width), lambda i: (i, 0))

    tab = pl.BlockSpec((tm, 256), lambda i: (jnp.where(i < nl, i % tpe, tpe), 0))
    ex = pl.BlockSpec((1, 8, d), lambda i: (jnp.minimum(i // tpe, nex), 0, 0))
    return pl.pallas_call(
        body, name="pre_bwd", grid=(nl + nc,),
        in_specs=[
            lat_tile(d), ctx_tile(d), ex, _full((1, d)), tile(PG_COLS),
            lat_tile(256), lat_tile(256), ctx_tile(256), lat_tile(512), ctx_tile(512), lat_tile(512),
            lat_tile(1024), lat_tile(1024), ctx_tile(1024), lat_tile(512), ctx_tile(512), lat_tile(d),
            _full(w_in.shape), _full((1, Q_LORA)), _full((1, KV_LORA)), _full(w_uq.shape), _full(w_ukv.shape),
            tab, tab,
        ],
        out_specs=[tile(IN_PAD), tile(1024), tile(1024), lat_tile(d), ex],
        out_shape=[
            jax.ShapeDtypeStruct((n_all, IN_PAD), BF16), jax.ShapeDtypeStruct((n_all, 1024), BF16),
            jax.ShapeDtypeStruct((n_all, 1024), BF16), jax.ShapeDtypeStruct((t_lat, d), F32),
            jax.ShapeDtypeStruct((nex + 1, 8, d), F32),
        ],
        compiler_params=_params(("arbitrary",), VMEM_LIMIT),
    )(x2, ctx2, modv, g_attn, pg, drq, drk, dkc_r, drv, dvc_r, drg, dq_m, dkl, dkc, dvl, dvc, dxm,
      w_in, g_q, g_kv, w_uq, w_ukv, cos_t, sin_t)


def _softmax_parts(qb, kl, kc):
    scale = 1.0 / math.sqrt(MLA_NOPE + MLA_ROPE)
    s = _dot_nt(qb, kl) * scale
    sc = _dot_nt(qb, kc) * scale
    m = jnp.maximum(jnp.max(s, axis=-1, keepdims=True), jnp.max(sc, axis=-1, keepdims=True))
    p = jnp.exp(s - m)
    pc = jnp.exp(sc - m)
    inv = 1.0 / (jnp.sum(p, axis=-1, keepdims=True) + jnp.sum(pc, axis=-1, keepdims=True))
    return p, pc, inv, scale


def _mla_specs(t_lat, seq, ctx_len, tq):
    nqt = seq // tq
    cb = t_lat // ctx_len
    q = pl.BlockSpec((tq, MLA_HEAD), lambda b, h, j: (b * nqt + j, h))
    kl = pl.BlockSpec((seq, MLA_HEAD), lambda b, h, j: (b, h))
    kc = pl.BlockSpec((ctx_len, MLA_HEAD), lambda b, h, j: (cb + b, h))
    vl = pl.BlockSpec((seq, 128), lambda b, h, j: (b, h))
    vc = pl.BlockSpec((ctx_len, 128), lambda b, h, j: (cb + b, h))
    o = pl.BlockSpec((tq, 128), lambda b, h, j: (b * nqt + j, h))
    return q, kl, kc, vl, vc, o


def _mla_fwd(q, k, v, *, t_lat, seq, ctx_len, tq):
    nex = t_lat // seq

    def body(q_ref, kl_ref, kc_ref, vl_ref, vc_ref, o_ref):
        p, pc, inv, _ = _softmax_parts(q_ref[...], kl_ref[...], kc_ref[...])
        o = _dot(p.astype(BF16), vl_ref[...]) + _dot(pc.astype(BF16), vc_ref[...])
        o_ref[...] = (o * inv).astype(BF16)

    qs, kl, kc, vl, vc, os_ = _mla_specs(t_lat, seq, ctx_len, tq)
    return pl.pallas_call(
        body, name="mla_fwd", grid=(nex, HEADS, seq // tq),
        in_specs=[qs, kl, kc, vl, vc], out_specs=os_,
        out_shape=jax.ShapeDtypeStruct((t_lat, HEADS * 128), BF16),
        compiler_params=_params(("parallel", "parallel", "arbitrary"), VMEM_LIMIT),
    )(q, k, k, v, v)


def _mla_bwd(q, k, v, ymla, dmix, *, t_lat, seq, ctx_len, tq):
    nex = t_lat // seq
    nqt = seq // tq
    t_ctx = nex * ctx_len

    def body(q_ref, kl_ref, kc_ref, vl_ref, vc_ref, o_ref, do_ref, dq_ref, dkl_ref, dkc_ref, dvl_ref, dvc_ref):
        j = pl.program_id(2)
        qb = q_ref[...]
        p, pc, inv, scale = _softmax_parts(qb, kl_ref[...], kc_ref[...])
        p = p * inv
        pc = pc * inv
        dob = do_ref[...]
        delta = jnp.sum(dob.astype(F32) * o_ref[...].astype(F32), axis=-1, keepdims=True)
        ds = (p * (_dot_nt(dob, vl_ref[...]) - delta) * scale).astype(BF16)
        dsc = (pc * (_dot_nt(dob, vc_ref[...]) - delta) * scale).astype(BF16)
        dq_ref[...] = _dot(ds, kl_ref[...]) + _dot(dsc, kc_ref[...])
        pb = p.astype(BF16)
        pcb = pc.astype(BF16)

        @pl.when(j == 0)
        def _():
            dkl_ref[...] = _dot_tn(ds, qb)
            dkc_ref[...] = _dot_tn(dsc, qb)
            dvl_ref[...] = _dot_tn(pb, dob)
            dvc_ref[...] = _dot_tn(pcb, dob)

        @pl.when(j > 0)
        def _():
            dkl_ref[...] += _dot_tn(ds, qb)
            dkc_ref[...] += _dot_tn(dsc, qb)
            dvl_ref[...] += _dot_tn(pb, dob)
            dvc_ref[...] += _dot_tn(pcb, dob)

    qs, kl, kc, vl, vc, os_ = _mla_specs(t_lat, seq, ctx_len, tq)
    do_spec = pl.BlockSpec((tq, 128), lambda b, h, j: (b * nqt + j, HEADS + h))
    return pl.pallas_call(
        body, name="mla_bwd", grid=(nex, HEADS, nqt),
        in_specs=[qs, kl, kc, vl, vc, os_, do_spec],
        out_specs=[
            qs,
            pl.BlockSpec((seq, MLA_HEAD), lambda b, h, j: (b, h)),
            pl.BlockSpec((ctx_len, MLA_HEAD), lambda b, h, j: (b, h)),
            pl.BlockSpec((seq, 128), lambda b, h, j: (b, h)),
            pl.BlockSpec((ctx_len, 128), lambda b, h, j: (b, h)),
        ],
        out_shape=[
            jax.ShapeDtypeStruct((t_lat, HEADS * MLA_HEAD), F32),
            jax.ShapeDtypeStruct((t_lat, HEADS * MLA_HEAD), F32),
            jax.ShapeDtypeStruct((t_ctx, HEADS * MLA_HEAD), F32),
            jax.ShapeDtypeStruct((t_lat, HEADS * 128), F32),
            jax.ShapeDtypeStruct((t_ctx, HEADS * 128), F32),
        ],
        compiler_params=_params(("parallel", "parallel", "arbitrary"), VMEM_LIMIT),
    )(q, k, k, v, v, ymla, dmix)


def _decay_terms(lg, chunk, forward):
    ii = lax.broadcasted_iota(jnp.int32, (chunk, chunk), 0)
    jj = lax.broadcasted_iota(jnp.int32, (chunk, chunk), 1)
    diff = (ii - jj) if forward else (jj - ii)
    dist = jnp.maximum(diff, 0).astype(F32)
    dmat = jnp.where(diff >= 0, jnp.exp(lg * dist), 0.0)
    pos = lax.broadcasted_iota(jnp.int32, (chunk, 1), 0).astype(F32)
    if forward:
        e_q = pos + 1.0
        e_k = (chunk - 1.0) - pos
    else:
        e_q = chunk - pos
        e_k = pos
    wq = jnp.exp(lg * e_q)
    wk = jnp.exp(lg * e_k)
    cd = jnp.exp(jnp.full((1, 1), lg * chunk, F32))
    return dmat, dist, wq, wk, e_q, e_k, cd


def _ctx_weights(lg, ctx_len, forward):
    pos = lax.broadcasted_iota(jnp.int32, (ctx_len, 1), 0).astype(F32)
    e = ((ctx_len - 1.0) - pos) if forward else pos
    return jnp.exp(lg * e), e


def _ret_specs(t_lat, seq, ctx_len):
    cb = t_lat // ctx_len
    qk = pl.BlockSpec((seq, 128), lambda b, h: (b, h // 2))
    v = pl.BlockSpec((seq, 128), lambda b, h: (b, h))
    kc = pl.BlockSpec((ctx_len, 128), lambda b, h: (cb + b, h // 2))
    vc = pl.BlockSpec((ctx_len, 128), lambda b, h: (cb + b, h))
    return qk, v, kc, vc


def _head_mask(h):
    lane = lax.broadcasted_iota(jnp.int32, (1, 128), 1)
    return (lane // RET_DK) == (h % 2)


def _ret_fwd(rq, rk, rv, pg, lg, g_ret, *, t_lat, seq, ctx_len, chunk):
    nex = t_lat // seq
    n_chunk = seq // chunk

    def body(q_ref, k_ref, v_ref, kc_ref, vc_ref, rg_ref, lg_ref, g_ref, y_ref, o_ref):
        h = pl.program_id(1)
        hm = _head_mask(h)
        gain = g_ref[...]
        kcm = jnp.where(hm, kc_ref[...].astype(F32), 0.0)
        vcb = vc_ref[...]

        def run(forward):
            lgd = lg_ref[0 if forward else 1, h]
            dmat, _, wq, wk, _, _, cd = _decay_terms(lgd, chunk, forward)
            wc, _ = _ctx_weights(lgd, ctx_len, forward)
            s0 = _dot_tn((kcm * wc).astype(BF16), vcb)

            def step(t, s):
                n = t if forward else n_chunk - 1 - t
                sl = pl.ds(pl.multiple_of(n * chunk, chunk), chunk)
                qm = jnp.where(hm, q_ref[sl, :], jnp.zeros((), BF16))
                kf = jnp.where(hm, k_ref[sl, :].astype(F32), 0.0)
                vb = v_ref[sl, :]
                a = _dot_nt(qm, kf.astype(BF16)) * dmat
                o = _dot(a.astype(BF16), vb) + wq * _dot(qm, s.astype(BF16))
                if forward:
                    o_ref[sl, :] = o
                else:
                    o = o_ref[sl, :] + o
                    o_ref[sl, :] = o
                    mu = jnp.mean(o, axis=-1, keepdims=True)
                    oc = o - mu
                    var = jnp.mean(oc * oc, axis=-1, keepdims=True)
                    on = oc * lax.rsqrt(var + EPS) * gain
                    rg = rg_ref[sl, :]
                    y_ref[sl, :] = (on * (rg / (1.0 + jnp.exp(-rg)))).astype(BF16)
                return cd * s + _dot_tn((kf * wk).astype(BF16), vb)

            lax.fori_loop(0, n_chunk, step, s0)

        run(True)
        run(False)

    qk, v, kc, vc = _ret_specs(t_lat, seq, ctx_len)
    return pl.pallas_call(
        body, name="ret_fwd", grid=(nex, HEADS),
        in_specs=[qk, qk, v, kc, vc, v, pl.BlockSpec(memory_space=pltpu.SMEM), pl.BlockSpec((1, 128), lambda b, h: (0, h))],
        out_specs=[v, v],
        out_shape=[jax.ShapeDtypeStruct((t_lat, HEADS * RET_DV), BF16), jax.ShapeDtypeStruct((t_lat, HEADS * RET_DV), F32)],
        compiler_params=_params(("parallel", "arbitrary"), VMEM_LIMIT),
    )(rq, rk, rv, rk, rv, pg, lg, g_ret)


def _ret_bwd(rq, rk, rv, pg, osum, dmix, lg, g_ret, *, t_lat, seq, ctx_len, chunk):
    nex = t_lat // seq
    n_chunk = seq // chunk
    t_ctx = nex * ctx_len

    def body(q_ref, k_ref, v_ref, kc_ref, vc_ref, rg_ref, o_ref, dy_ref, lg_ref, g_ref,
             dq_ref, dk_ref, dv_ref, dkc_ref, dvc_ref, drg_ref, st_ref, do_s, s_st):
        h = pl.program_id(1)
        hm = _head_mask(h)
        gain = g_ref[...]
        kcm = jnp.where(hm, kc_ref[...].astype(F32), 0.0)
        vcb = vc_ref[...]

        def norm_step(n, dgain):
            sl = pl.ds(pl.multiple_of(n * chunk, chunk), chunk)
            o = o_ref[sl, :]
            mu = jnp.mean(o, axis=-1, keepdims=True)
            oc = o - mu
            rstd = lax.rsqrt(jnp.mean(oc * oc, axis=-1, keepdims=True) + EPS)
            ohat = oc * rstd
            rg = rg_ref[sl, :]
            sg = 1.0 / (1.0 + jnp.exp(-rg))
            dy = dy_ref[sl, :].astype(F32)
            don = dy * (rg * sg)
            drg_ref[sl, :] = dy * (ohat * gain) * (sg * (1.0 + rg * (1.0 - sg)))
            dohat = don * gain
            do_s[sl, :] = rstd * (dohat - jnp.mean(dohat, axis=-1, keepdims=True)
                                  - ohat * jnp.mean(dohat * ohat, axis=-1, keepdims=True))
            return dgain + jnp.sum(don * ohat, axis=0, keepdims=True)

        dgain = lax.fori_loop(0, n_chunk, norm_step, jnp.zeros((1, 128), F32))

        @pl.when(h % 2 == 0)
        def _():
            dq_ref[...] = jnp.zeros(dq_ref.shape, F32)
            dk_ref[...] = jnp.zeros(dk_ref.shape, F32)
            dkc_ref[...] = jnp.zeros(dkc_ref.shape, F32)

        dv_ref[...] = jnp.zeros(dv_ref.shape, F32)

        def run(forward):
            lgd = lg_ref[0 if forward else 1, h]
            dmat, dist, wq, wk, e_q, e_k, cd = _decay_terms(lgd, chunk, forward)
            wc, e_c = _ctx_weights(lgd, ctx_len, forward)
            s0 = _dot_tn((kcm * wc).astype(BF16), vcb)

            def state_step(t, s):
                n = t if forward else n_chunk - 1 - t
                sl = pl.ds(pl.multiple_of(n * chunk, chunk), chunk)
                s_st[n] = s
                kf = jnp.where(hm, k_ref[sl, :].astype(F32), 0.0)
                return cd * s + _dot_tn((kf * wk).astype(BF16), v_ref[sl, :])

            lax.fori_loop(0, n_chunk, state_step, s0)

            def grad_step(t, carry):
                g_next, dlg = carry
                n = (n_chunk - 1 - t) if forward else t
                sl = pl.ds(pl.multiple_of(n * chunk, chunk), chunk)
                qm = jnp.where(hm, q_ref[sl, :], jnp.zeros((), BF16))
                kf = jnp.where(hm, k_ref[sl, :].astype(F32), 0.0)
                kb = kf.astype(BF16)
                vb = v_ref[sl, :]
                do = do_s[sl, :]
                dob = do.astype(BF16)
                s_n = s_st[n]
                s_nb = s_n.astype(BF16)
                gb = g_next.astype(BF16)
                dk_cross = wk * _dot_nt(vb, gb)
                dv_cross = _dot((kf * wk).astype(BF16), gb)
                a = _dot_nt(qm, kb) * dmat
                da_raw = _dot_nt(dob, vb)
                dab = (da_raw * dmat).astype(BF16)
                ab = a.astype(BF16)
                o_cross = wq * _dot(qm, s_nb)
                dq_ref[sl, :] += _dot(dab, kb) + wq * _dot_nt(dob, s_nb)
                dk_ref[sl, :] += _dot_tn(dab, qm) + dk_cross
                dv_ref[sl, :] += _dot_tn(ab, dob) + dv_cross
                dlg = (dlg + chunk * cd * jnp.sum(g_next * s_n, keepdims=True)
                       + jnp.sum(e_k * jnp.sum(kf * dk_cross, axis=-1, keepdims=True), keepdims=True)
                       + jnp.sum(dist * a * da_raw, keepdims=True)
                       + jnp.sum(e_q * jnp.sum(o_cross * do, axis=-1, keepdims=True), keepdims=True))
                g_new = cd * g_next + _dot_tn((qm.astype(F32) * wq).astype(BF16), dob)
                return g_new, dlg

            ds0, dlg = lax.fori_loop(0, n_chunk, grad_step, (jnp.zeros((128, 128), F32), jnp.zeros((1, 1), F32)))
            ds0b = ds0.astype(BF16)
            dkc_part = wc * _dot_nt(vcb, ds0b)
            dkc_ref[...] += dkc_part
            dvc_part = _dot((kcm * wc).astype(BF16), ds0b)
            dlg = dlg + jnp.sum(e_c * jnp.sum(kcm * dkc_part, axis=-1, keepdims=True), keepdims=True)
            return dvc_part, dlg

        dvc_f, dlg_f = run(True)
        dvc_b, dlg_b = run(False)
        dvc_ref[...] = dvc_f + dvc_b
        st_ref[0] = jnp.concatenate([
            dgain, jnp.broadcast_to(dlg_f, (1, 128)), jnp.broadcast_to(dlg_b, (1, 128)), jnp.zeros((5, 128), F32)], axis=0)

    qk, v, kc, vc = _ret_specs(t_lat, seq, ctx_len)
    dy_spec = v
    return pl.pallas_call(
        body, name="ret_bwd", grid=(nex, HEADS),
        in_specs=[qk, qk, v, kc, vc, v, v, dy_spec, pl.BlockSpec(memory_space=pltpu.SMEM),
                  pl.BlockSpec((1, 128), lambda b, h: (0, h))],
        out_specs=[
            qk, qk, v,
            pl.BlockSpec((ctx_len, 128), lambda b, h: (b, h // 2)),
            pl.BlockSpec((ctx_len, 128), lambda b, h: (b, h)),
            v,
            pl.BlockSpec((1, 8, 128), lambda b, h: (b, 0, h)),
        ],
        out_shape=[
            jax.ShapeDtypeStruct((t_lat, 256), F32), jax.ShapeDtypeStruct((t_lat, 256), F32),
            jax.ShapeDtypeStruct((t_lat, 512), F32), jax.ShapeDtypeStruct((t_ctx, 256), F32),
            jax.ShapeDtypeStruct((t_ctx, 512), F32), jax.ShapeDtypeStruct((t_lat, 512), F32),
            jax.ShapeDtypeStruct((nex, 8, 512), F32),
        ],
        scratch_shapes=[pltpu.VMEM((seq, 128), F32), pltpu.VMEM((n_chunk, 128, 128), F32)],
        compiler_params=_params(("parallel", "arbitrary"), VMEM_LIMIT),
    )(rq, rk, rv, rk, rv, pg, osum, dmix, lg, g_ret)


def _matmul_tn(a, b, *, bm, bn, bk, chip_major, name):
    tk, m = a.shape
    n = b.shape[1]

    def body(a_ref, b_ref, o_ref):
        k = pl.program_id(2)
        part = _dot_tn(a_ref[...], b_ref[...])

        @pl.when(k == 0)
        def _():
            o_ref[...] = part

        @pl.when(k > 0)
        def _():
            o_ref[...] += part

    if chip_major:
        out_spec = pl.BlockSpec((None, bm, bn), lambda i, j, k: (j, i, 0))
        out_shape = jax.ShapeDtypeStruct((n // bn, m, bn), F32)
    else:
        out_spec = pl.BlockSpec((bm, bn), lambda i, j, k: (i, j))
        out_shape = jax.ShapeDtypeStruct((m, n), F32)
    return pl.pallas_call(
        body, name=name, grid=(m // bm, n // bn, tk // bk),
        in_specs=[pl.BlockSpec((bk, bm), lambda i, j, k: (k, i)), pl.BlockSpec((bk, bn), lambda i, j, k: (k, j))],
        out_specs=out_spec, out_shape=out_shape,
        compiler_params=_params(("parallel", "parallel", "arbitrary"), VMEM_LIMIT),
    )(a, b)


def _local_step(x, ctx, tgt, modv, lg, g_attn, g_ffn, g_fin, g_ret, g_q, g_kv, w_in, w_uq, w_ukv, w_out, w_ff1, w_ff2,
                *, tm=256, tq=256, chunk=128):
    nex, seq, d = x.shape
    ctx_len = ctx.shape[1]
    t_lat = nex * seq
    x2 = x.reshape(t_lat, d)
    ctx2 = ctx.reshape(nex * ctx_len, d)
    tgt2 = tgt.reshape(t_lat, d)
    cos_t, sin_t = _rope_tables(seq, tm)
    dims = dict(t_lat=t_lat, seq=seq, ctx_len=ctx_len)

    hb, pg, rq, rk, rv, nq, nkv, q, k, v = _pre_fwd(x2, ctx2, modv, g_attn, w_in, g_q, g_kv, w_uq, w_ukv, cos_t, sin_t,
                                                    seq=seq, tm=tm)
    yret, osum = _ret_fwd(rq, rk, rv, pg, lg, g_ret, chunk=chunk, **dims)
    ymla = _mla_fwd(q, k, v, tq=tq, **dims)
    mix, act, du, h2, df, dmo, dmix, dxm, st_post = _post(yret, ymla, x2, tgt2, modv, g_ffn, g_fin, w_out, w_ff1, w_ff2,
                                                         seq=seq, tm=tm)
    dq_m, dkl, dkc, dvl, dvc = _mla_bwd(q, k, v, ymla, dmix, tq=tq, **dims)
    drq, drk, drv, dkc_r, dvc_r, drg, st_ret = _ret_bwd(rq, rk, rv, pg, osum, dmix, lg, g_ret, chunk=chunk, **dims)
    dpb, dqf, dkvf, gx, st_pre = _pre_bwd(x2, ctx2, modv, g_attn, pg, drq, drk, dkc_r, drv, dvc_r, drg, dq_m, dkl, dkc, dvl,
                                          dvc, dxm, w_in, g_q, g_kv, w_uq, w_ukv, cos_t, sin_t, seq=seq, tm=tm)
    bk = 512
    gw_ff2 = _matmul_tn(act, df, bm=512, bn=1024, bk=bk, chip_major=False, name="gw_ff2")
    gw_ff1 = _matmul_tn(h2, du, bm=512, bn=1024, bk=bk, chip_major=True, name="gw_ff1")
    gw_out = _matmul_tn(mix, dmo, bm=512, bn=1024, bk=bk, chip_major=False, name="gw_out")
    gw_in = _matmul_tn(hb, dpb, bm=512, bn=768, bk=bk, chip_major=False, name="gw_in")
    gw_uq = _matmul_tn(nq, dqf, bm=Q_LORA, bn=MLA_HEAD, bk=bk, chip_major=True, name="gw_uq")
    gw_ukv = _matmul_tn(nkv, dkvf, bm=KV_LORA, bn=256, bk=bk, chip_major=True, name="gw_ukv")
    grads = dict(w_in=gw_in, w_uq=gw_uq, w_ukv=gw_ukv, w_out=gw_out, w_ff1=gw_ff1, w_ff2=gw_ff2)
    return gx.reshape(nex, seq, d), grads, st_post, st_ret, st_pre


_ANY = pl.BlockSpec(memory_space=pl.ANY)
_VMEM = pl.BlockSpec(memory_space=pltpu.VMEM)
_OFFSETS = tuple((dx, dy, dc) for dx in (0, 1) for dy in (0, 1) for dc in (0, 1))[1:]
_CHIP_OFFSETS = ((1, 0), (0, 1), (1, 1))


def _place():
    return lax.axis_index("x"), lax.axis_index("y"), lax.axis_index("c")


def _flip(v, d):
    return 1 - v if d else v


def _allgather8(a, name):
    r, c = a.shape

    def body(a_ref, o_ref, send, recv, lsem):
        x, y, z = _place()
        me = 4 * x + 2 * y + z
        mine = pltpu.make_async_copy(a_ref, o_ref.at[me], lsem)
        mine.start()
        copies = []
        for k, (dx, dy, dc) in enumerate(_OFFSETS):
            cp = pltpu.make_async_remote_copy(
                src_ref=a_ref, dst_ref=o_ref.at[me], send_sem=send.at[k], recv_sem=recv.at[k],
                device_id=(_flip(x, dx), _flip(y, dy), _flip(z, dc)), device_id_type=MESH)
            cp.start()
            copies.append(cp)
        for k, (dx, dy, dc) in enumerate(_OFFSETS):
            peer = 4 * _flip(x, dx) + 2 * _flip(y, dy) + _flip(z, dc)
            pltpu.make_async_remote_copy(
                src_ref=a_ref, dst_ref=o_ref.at[peer], send_sem=send.at[k], recv_sem=recv.at[k],
                device_id=(_flip(x, dx), _flip(y, dy), _flip(z, dc)), device_id_type=MESH).wait_recv()
        for cp in copies:
            cp.wait_send()
        mine.wait()

    return pl.pallas_call(
        body, name=name, in_specs=[_VMEM], out_specs=_VMEM,
        out_shape=jax.ShapeDtypeStruct((N_DEV, r, c), a.dtype),
        scratch_shapes=[pltpu.SemaphoreType.DMA((7,)), pltpu.SemaphoreType.DMA((7,)), pltpu.SemaphoreType.DMA],
    )(a)


def _allgather_weights(ws):
    n = len(ws)

    def body(*refs):
        w_refs, o_refs = refs[:n], refs[n:2 * n]
        send, recv, fsend, frecv, lsem = refs[2 * n:]
        x, y, z = _place()
        chip = 2 * x + y
        sib = (x, y, 1 - z)
        local, first = [], []
        for a in range(n):
            r2 = w_refs[a].shape[0] // 2
            cp = pltpu.make_async_copy(w_refs[a], o_refs[a].at[chip], lsem.at[a])
            cp.start()
            local.append(cp)
            mine = pl.ds(z * r2, r2)
            for k, (dx, dy) in enumerate(_CHIP_OFFSETS):
                cp = pltpu.make_async_remote_copy(
                    src_ref=w_refs[a].at[mine], dst_ref=o_refs[a].at[chip, mine], send_sem=send.at[a, k],
                    recv_sem=recv.at[a, k], device_id=(_flip(x, dx), _flip(y, dy), z), device_id_type=MESH)
                cp.start()
                first.append(cp)
        passed = []
        for a in range(n):
            r2 = w_refs[a].shape[0] // 2
            mine = pl.ds(z * r2, r2)
            for k, (dx, dy) in enumerate(_CHIP_OFFSETS):
                other = 2 * _flip(x, dx) + _flip(y, dy)
                landed = o_refs[a].at[other, mine]
                pltpu.make_async_remote_copy(
                    src_ref=w_refs[a].at[mine], dst_ref=landed, send_sem=send.at[a, k], recv_sem=recv.at[a, k],
                    device_id=(_flip(x, dx), _flip(y, dy), z), device_id_type=MESH).wait_recv()
                cp = pltpu.make_async_remote_copy(
                    src_ref=landed, dst_ref=landed, send_sem=fsend.at[a, k], recv_sem=frecv.at[a, k],
                    device_id=sib, device_id_type=MESH)
                cp.start()
                passed.append(cp)
        for a in range(n):
            r2 = w_refs[a].shape[0] // 2
            theirs = pl.ds((1 - z) * r2, r2)
            for k, (dx, dy) in enumerate(_CHIP_OFFSETS):
                other = 2 * _flip(x, dx) + _flip(y, dy)
                got = o_refs[a].at[other, theirs]
                pltpu.make_async_remote_copy(
                    src_ref=got, dst_ref=got, send_sem=fsend.at[a, k], recv_sem=frecv.at[a, k],
                    device_id=sib, device_id_type=MESH).wait_recv()
        for cp in first + passed:
            cp.wait_send()
        for cp in local:
            cp.wait()

    return pl.pallas_call(
        body, name="ag_weights", in_specs=[_ANY] * n, out_specs=[_ANY] * n,
        out_shape=[jax.ShapeDtypeStruct((N_CHIPS,) + w.shape, w.dtype) for w in ws],
        scratch_shapes=[pltpu.SemaphoreType.DMA((n, 3))] * 4 + [pltpu.SemaphoreType.DMA((n,))],
    )(*ws)


def _rs_exchange(gs):
    n = len(gs)

    def body(*refs):
        g_refs, r_refs = refs[:n], refs[n:2 * n]
        send, recv = refs[2 * n:]
        x, y, z = _place()
        copies = []
        for a in range(n):
            r2 = g_refs[a].shape[1] // 2
            cp = pltpu.make_async_remote_copy(
                src_ref=g_refs[a].at[:, pl.ds((1 - z) * r2, r2)], dst_ref=r_refs[a], send_sem=send.at[a],
                recv_sem=recv.at[a], device_id=(x, y, 1 - z), device_id_type=MESH)
            cp.start()
            copies.append(cp)
        for cp in copies:
            cp.wait()

    return pl.pallas_call(
        body, name="rs_exchange", in_specs=[_ANY] * n, out_specs=[_ANY] * n,
        out_shape=[jax.ShapeDtypeStruct((g.shape[0], g.shape[1] // 2, g.shape[2]), g.dtype) for g in gs],
        scratch_shapes=[pltpu.SemaphoreType.DMA((n,))] * 2,
    )(*gs)


def _add_half(g, recv, core, name):
    s, r, c = g.shape
    r2 = r // 2
    rb = r2
    for cand in (256, 128, 64):
        if r2 % cand == 0:
            rb = cand
            break
    g4 = g.reshape(s, 2, r2, c)

    def body(core_ref, g_ref, r_ref, o_ref):
        o_ref[...] = (g_ref[...] + r_ref[...]).astype(BF16)

    return pl.pallas_call(
        body, name=name,
        grid_spec=pltpu.PrefetchScalarGridSpec(
            num_scalar_prefetch=1, grid=(s, r2 // rb),
            in_specs=[pl.BlockSpec((None, None, rb, c), lambda i, j, cr: (i, cr[0], j, 0)),
                      pl.BlockSpec((None, rb, c), lambda i, j, cr: (i, j, 0))],
            out_specs=pl.BlockSpec((None, rb, c), lambda i, j, cr: (i, j, 0))),
        out_shape=jax.ShapeDtypeStruct((s, r2, c), BF16),
        compiler_params=_params(("parallel", "parallel")),
    )(core, g4, recv)


def _rs_scatter(ps):
    n = len(ps)

    def body(*refs):
        p_refs, o_refs = refs[:n], refs[n:2 * n]
        send, recv, lsem = refs[2 * n:]
        x, y, z = _place()
        chip = 2 * x + y
        local, sent = [], []
        for a in range(n):
            cp = pltpu.make_async_copy(p_refs[a].at[chip], o_refs[a].at[chip], lsem.at[a])
            cp.start()
            local.append(cp)
            for k, (dx, dy) in enumerate(_CHIP_OFFSETS):
                other = 2 * _flip(x, dx) + _flip(y, dy)
                cp = pltpu.make_async_remote_copy(
                    src_ref=p_refs[a].at[other], dst_ref=o_refs[a].at[chip], send_sem=send.at[a, k],
                    recv_sem=recv.at[a, k], device_id=(_flip(x, dx), _flip(y, dy), z), device_id_type=MESH)
                cp.start()
                sent.append(cp)
        for a in range(n):
            for k, (dx, dy) in enumerate(_CHIP_OFFSETS):
                other = 2 * _flip(x, dx) + _flip(y, dy)
                pltpu.make_async_remote_copy(
                    src_ref=p_refs[a].at[other], dst_ref=o_refs[a].at[other], send_sem=send.at[a, k],
                    recv_sem=recv.at[a, k], device_id=(_flip(x, dx), _flip(y, dy), z), device_id_type=MESH).wait_recv()
        for cp in sent:
            cp.wait_send()
        for cp in local:
            cp.wait()

    return pl.pallas_call(
        body, name="rs_scatter", in_specs=[_ANY] * n, out_specs=[_ANY] * n,
        out_shape=[jax.ShapeDtypeStruct(p.shape, p.dtype) for p in ps],
        scratch_shapes=[pltpu.SemaphoreType.DMA((n, 3))] * 2 + [pltpu.SemaphoreType.DMA((n,))],
    )(*ps)


def _sum_chips(p, name):
    s, r2, c = p.shape
    rb = r2
    for cand in (256, 128, 64):
        if r2 % cand == 0:
            rb = cand
            break

    def body(p_ref, o_ref):
        acc = p_ref[0].astype(F32)
        for q in range(1, s):
            acc = acc + p_ref[q].astype(F32)
        o_ref[...] = acc

    return pl.pallas_call(
        body, name=name, grid=(r2 // rb,),
        in_specs=[pl.BlockSpec((s, rb, c), lambda i: (0, i, 0))],
        out_specs=pl.BlockSpec((rb, c), lambda i: (i, 0)),
        out_shape=jax.ShapeDtypeStruct((r2, c), F32),
        compiler_params=_params(("parallel",)),
    )(p)


def _rs_share(hs):
    n = len(hs)

    def body(*refs):
        h_refs, o_refs = refs[:n], refs[n:2 * n]
        send, recv, lsem = refs[2 * n:]
        x, y, z = _place()
        local, sent = [], []
        for a in range(n):
            cp = pltpu.make_async_copy(h_refs[a], o_refs[a].at[z], lsem.at[a])
            cp.start()
            local.append(cp)
            cp = pltpu.make_async_remote_copy(
                src_ref=h_refs[a], dst_ref=o_refs[a].at[z], send_sem=send.at[a], recv_sem=recv.at[a],
                device_id=(x, y, 1 - z), device_id_type=MESH)
            cp.start()
            sent.append(cp)
        for a in range(n):
            pltpu.make_async_remote_copy(
                src_ref=h_refs[a], dst_ref=o_refs[a].at[1 - z], send_sem=send.at[a], recv_sem=recv.at[a],
                device_id=(x, y, 1 - z), device_id_type=MESH).wait_recv()
        for cp in sent:
            cp.wait_send()
        for cp in local:
            cp.wait()

    return pl.pallas_call(
        body, name="rs_share", in_specs=[_ANY] * n, out_specs=[_ANY] * n,
        out_shape=[jax.ShapeDtypeStruct((2,) + h.shape, h.dtype) for h in hs],
        scratch_shapes=[pltpu.SemaphoreType.DMA((n,))] * 3,
    )(*hs)


SMALL_ROWS = 32
PACK_ROWS = 16


def _pack_small(st_post, st_ret, st_pre):
    d = st_post.shape[2]

    def body(po_ref, re_ref, pr_ref, o_ref):
        o_ref[...] = jnp.zeros(o_ref.shape, F32)
        o_ref[0:1, :] = pr_ref[0, 2:3, :] + pr_ref[1, 2:3, :] + pr_ref[2, 2:3, :]
        o_ref[1:2, :] = po_ref[0, 4:5, :] + po_ref[1, 4:5, :]
        o_ref[2:3, :] = po_ref[0, 5:6, :] + po_ref[1, 5:6, :]
        o_ref[3:4, 0:512] = re_ref[0, 0:1, :] + re_ref[1, 0:1, :]
        o_ref[4:5, :] = pr_ref[0, 3:4, :] + pr_ref[1, 3:4, :] + pr_ref[2, 3:4, :]
        o_ref[5:6, :] = pr_ref[0, 4:5, :] + pr_ref[1, 4:5, :] + pr_ref[2, 4:5, :]
        o_ref[6:7, 0:512] = re_ref[0, 1:2, :] + re_ref[1, 1:2, :]
        o_ref[6:7, 512:1024] = re_ref[0, 2:3, :] + re_ref[1, 2:3, :]
        o_ref[7:8, :] = po_ref[0, 6:7, :] + po_ref[1, 6:7, :]
        o_ref[8:9, :] = pr_ref[2, 0:1, :]
        o_ref[9:10, :] = pr_ref[2, 1:2, :]
        for e in range(2):
            b = 12 + 6 * e
            o_ref[b:b + 1, :] = pr_ref[e, 0:1, :]
            o_ref[b + 1:b + 2, :] = pr_ref[e, 1:2, :]
            o_ref[b + 2:b + 3, :] = po_ref[e, 3:4, :]
            o_ref[b + 3:b + 4, :] = po_ref[e, 0:1, :]
            o_ref[b + 4:b + 5, :] = po_ref[e, 1:2, :]
            o_ref[b + 5:b + 6, :] = po_ref[e, 2:3, :]

    return pl.pallas_call(body, name="pack_small", out_shape=jax.ShapeDtypeStruct((SMALL_ROWS, d), F32))(st_post, st_ret, st_pre)


def _small_reduce(gathered):
    d = gathered.shape[2]

    def body(g_ref, o_ref):
        tot = g_ref[0, 0:PACK_ROWS, :]
        for dev in range(1, N_DEV):
            tot = tot + g_ref[dev, 0:PACK_ROWS, :]
        o_ref[...] = tot
        for j in range(6):
            acc = g_ref[0, 12 + j:13 + j, :] + g_ref[0, 18 + j:19 + j, :]
            for dev in range(1, N_DEV):
                acc = acc + g_ref[dev, 12 + j:13 + j, :] + g_ref[dev, 18 + j:19 + j, :]
            if j < 2:
                acc = acc + o_ref[8 + j:9 + j, :]
            o_ref[10 + j:11 + j, :] = acc

    return pl.pallas_call(body, name="small_reduce", out_shape=jax.ShapeDtypeStruct((PACK_ROWS, d), F32))(gathered)


def _small_final(tot, dcc, c_ctx_row, sg_row, pw, pm, pv):
    d = tot.shape[1]

    def body(t_ref, dcc_ref, cc_ref, sg_ref, w_ref, m_ref, v_ref, g_ref, d_ref, mo_ref, vo_ref, l_ref):
        g_ref[0:6, :] = t_ref[0:6, :]
        g_ref[6:7, :] = t_ref[6:7, :] * sg_ref[...]
        cc = cc_ref[...]
        s = 1.0 / (1.0 + jnp.exp(-cc))
        dsilu = dcc_ref[0, 0:1, :] + dcc_ref[2, 0:1, :] + dcc_ref[4, 0:1, :] + dcc_ref[6, 0:1, :]
        g_ref[7:8, :] = dsilu * (s * (1.0 + cc * (1.0 - s)))
        g_ref[8:14, :] = t_ref[10:16, :]
        g_ref[14:16, :] = jnp.zeros((2, d), F32)
        gg = g_ref[...]
        mn = ADAM_B1 * m_ref[...] + (1.0 - ADAM_B1) * gg
        vn = ADAM_B2 * v_ref[...] + (1.0 - ADAM_B2) * (gg * gg)
        m_hat = mn / (1.0 - ADAM_B1 ** ADAM_STEP)
        v_hat = vn / (1.0 - ADAM_B2 ** ADAM_STEP)
        d_ref[...] = -ADAM_LR * (m_hat / (jnp.sqrt(v_hat) + ADAM_EPS) + ADAM_WD * w_ref[...])
        mo_ref[...] = mn
        vo_ref[...] = vn
        l_ref[...] = jnp.broadcast_to((0.5 / d) * jnp.sum(t_ref[7:8, :], keepdims=True), l_ref.shape)

    shp = jax.ShapeDtypeStruct((PACK_ROWS, d), F32)
    return pl.pallas_call(
        body, name="small_final", out_shape=[shp, shp, shp, shp, jax.ShapeDtypeStruct((8, LANES), F32)],
    )(tot, dcc, c_ctx_row, sg_row, pw, pm, pv)


_SMALL = (("g_attn", 0, 1024), ("g_ffn", 1, 1024), ("g_final", 2, 1024), ("g_ret", 3, 512), ("g_q_lora", 4, 384),
          ("g_kv_lora", 5, 256), ("c_ctx", 7, 1024))


def _pack_rows(vals):
    d = D_MODEL
    rows = [jnp.zeros((1, d), F32) for _ in range(PACK_ROWS)]
    for name, row, width in _SMALL:
        rows[row] = jnp.pad(vals[name].reshape(1, width), ((0, 0), (0, d - width)))
    dec = jnp.concatenate([vals["ret_decay_fwd"].reshape(-1), vals["ret_decay_bwd"].reshape(-1)])
    rows[6] = jnp.repeat(dec, LANES).reshape(1, d)
    b = vals["b_ada"].reshape(6, d)
    for j in range(6):
        rows[8 + j] = b[j:j + 1]
    return jnp.concatenate(rows, axis=0)


def _unpack_rows(packed, like):
    out = {}
    for name, row, width in _SMALL:
        out[name] = packed[row, :width].reshape(like[name].shape)
    dec = packed[6].reshape(8, LANES)[:, 0]
    out["ret_decay_fwd"] = dec[:4].reshape(like["ret_decay_fwd"].shape)
    out["ret_decay_bwd"] = dec[4:].reshape(like["ret_decay_bwd"].shape)
    out["b_ada"] = packed[8:14].reshape(like["b_ada"].shape)
    return out


_WEIGHTS = ("c_ctx", "w_ada", "b_ada", "g_attn", "g_ffn", "w_in", "ret_decay_fwd", "ret_decay_bwd", "g_ret", "g_q_lora",
            "w_uq", "g_kv_lora", "w_ukv", "w_out", "w_ff1", "w_ff2", "g_final")
_BIG = ("w_in", "w_uq", "w_ukv", "w_out", "w_ff1", "w_ff2")


def kernel(x, c, ctx, c_ctx, w_ada, b_ada, g_attn, g_ffn, w_in, ret_decay_fwd, ret_decay_bwd, g_ret, g_q_lora, w_uq, g_kv_lora, w_ukv, w_out, w_ff1, w_ff2, g_final, loss_target, m_c_ctx, m_w_ada, m_b_ada, m_g_attn, m_g_ffn, m_w_in, m_ret_decay_fwd, m_ret_decay_bwd, m_g_ret, m_g_q_lora, m_w_uq, m_g_kv_lora, m_w_ukv, m_w_out, m_w_ff1, m_w_ff2, m_g_final, v_c_ctx, v_w_ada, v_b_ada, v_g_attn, v_g_ffn, v_w_in, v_ret_decay_fwd, v_ret_decay_bwd, v_g_ret, v_g_q_lora, v_w_uq, v_g_kv_lora, v_w_ukv, v_w_out, v_w_ff1, v_w_ff2, v_g_final):
    w = dict(c_ctx=c_ctx, w_ada=w_ada, b_ada=b_ada, g_attn=g_attn, g_ffn=g_ffn, w_in=w_in, ret_decay_fwd=ret_decay_fwd,
             ret_decay_bwd=ret_decay_bwd, g_ret=g_ret, g_q_lora=g_q_lora, w_uq=w_uq, g_kv_lora=g_kv_lora, w_ukv=w_ukv,
             w_out=w_out, w_ff1=w_ff1, w_ff2=w_ff2, g_final=g_final)
    m = dict(c_ctx=m_c_ctx, w_ada=m_w_ada, b_ada=m_b_ada, g_attn=m_g_attn, g_ffn=m_g_ffn, w_in=m_w_in,
             ret_decay_fwd=m_ret_decay_fwd, ret_decay_bwd=m_ret_decay_bwd, g_ret=m_g_ret, g_q_lora=m_g_q_lora, w_uq=m_w_uq,
             g_kv_lora=m_g_kv_lora, w_ukv=m_w_ukv, w_out=m_w_out, w_ff1=m_w_ff1, w_ff2=m_w_ff2, g_final=m_g_final)
    v = dict(c_ctx=v_c_ctx, w_ada=v_w_ada, b_ada=v_b_ada, g_attn=v_g_attn, g_ffn=v_g_ffn, w_in=v_w_in,
             ret_decay_fwd=v_ret_decay_fwd, ret_decay_bwd=v_ret_decay_bwd, g_ret=v_g_ret, g_q_lora=v_g_q_lora, w_uq=v_w_uq,
             g_kv_lora=v_g_kv_lora, w_ukv=v_w_ukv, w_out=v_w_out, w_ff1=v_w_ff1, w_ff2=v_w_ff2, g_final=v_g_final)
    xi, yi, ci = lax.axis_index("x"), lax.axis_index("y"), lax.axis_index("c")
    chip = 2 * xi + yi
    dev = 2 * chip + ci
    nex, seq, d = x.shape
    n_ada = w_ada.shape[2]

    c_all = _allgather8(jnp.pad(c, ((0, 8 - nex), (0, 0))), "ag_c")[:, :nex].reshape(N_DEV * nex, d)
    a_in = jnp.concatenate([c_all, c_ctx.reshape(1, d), jnp.zeros((7, d), F32)], axis=0)
    b_sh = lax.dynamic_slice(b_ada, (0, chip * n_ada), (1, n_ada))
    mod_sh = _mod_fwd(a_in, w_ada[0], b_sh)
    mod_all = _allgather8(mod_sh, "ag_mod")[0::2].transpose(1, 0, 2).reshape(a_in.shape[0], N_CHIPS * n_ada)
    mod_me = lax.dynamic_slice(mod_all, (nex * dev, 0), (nex, N_CHIPS * n_ada)).reshape(nex, 6, d)
    mod_c = mod_all[N_DEV * nex].reshape(1, 6, d)
    modv = jnp.pad(jnp.concatenate([mod_me, mod_c], axis=0), ((0, 0), (0, 2), (0, 0)))

    dec = jnp.zeros((8, LANES), F32).at[0, :HEADS].set(ret_decay_fwd[0]).at[1, :HEADS].set(ret_decay_bwd[0])
    lg8, sg8 = _decay_prep(dec)
    lg = lg8[:2, :HEADS]

    shard = {k: w[k][0] for k in _BIG}
    shard["w_uq"] = jnp.pad(shard["w_uq"], ((0, 0), (0, MLA_HEAD - MLA_NOPE - MLA_ROPE)))
    full = dict(zip(_BIG, _allgather_weights([_cast_bf16(shard[k], "cast_" + k) for k in _BIG])))
    w_in_k = jnp.pad(full["w_in"].transpose(1, 0, 2).reshape(d, IN_COLS), ((0, 0), (0, IN_PAD - IN_COLS)))
    w_uq_k = full["w_uq"]
    w_out_k = full["w_out"].reshape(d, d)
    w_ff2_k = full["w_ff2"].reshape(D_FF, d)

    gx, gw, st_post, st_ret, st_pre = _local_step(
        x, ctx, loss_target, modv, lg, g_attn, g_ffn, g_final.reshape(1, d), g_ret, g_q_lora, g_kv_lora,
        w_in_k, w_uq_k, full["w_ukv"], w_out_k, full["w_ff1"], w_ff2_k)

    per_chip = IN_COLS // N_CHIPS
    g4 = [
        gw["w_in"][:, :IN_COLS].reshape(d, N_CHIPS, per_chip).transpose(1, 0, 2),
        gw["w_uq"][:, :, :MLA_NOPE + MLA_ROPE],
        gw["w_ukv"],
        gw["w_out"].reshape(N_CHIPS, d // N_CHIPS, d),
        gw["w_ff1"],
        gw["w_ff2"].reshape(N_CHIPS, D_FF // N_CHIPS, d),
    ]
    got = _rs_exchange(g4)
    core = ci.reshape(1).astype(jnp.int32)
    partial = [_add_half(g, r, core, "add_half_" + k) for g, r, k in zip(g4, got, _BIG)]
    landed = _rs_scatter(partial)
    halves = [_sum_chips(p, "sum_chips_" + k) for p, k in zip(landed, _BIG)]
    shared = _rs_share(halves)
    grad = {k: s.reshape(w[k].shape) for k, s in zip(_BIG, shared)}

    gathered = _allgather8(_pack_small(st_post, st_ret, st_pre), "ag_small")
    tot = _small_reduce(gathered)
    dm = jnp.concatenate([
        gathered[:, 12:24].reshape(N_DEV * nex, 6 * d),
        jnp.concatenate([tot[8:10].reshape(1, 2 * d), jnp.zeros((1, 4 * d), F32)], axis=1),
        jnp.zeros((7, 6 * d), F32)], axis=0)
    dm_sh = lax.dynamic_slice(dm, (0, chip * n_ada), (dm.shape[0], n_ada))
    g_ada, da = _mod_bwd(a_in, dm_sh, w_ada[0])
    grad["w_ada"] = g_ada.reshape(w_ada.shape)
    dcc = _allgather8(da[N_DEV * nex:], "ag_dcc")
    sg_row = jnp.repeat(jnp.concatenate([sg8[0, :HEADS], sg8[1, :HEADS]]), LANES).reshape(1, d)
    g_small, d_small, m_small, v_small, loss8 = _small_final(
        tot, dcc, c_ctx.reshape(1, d), sg_row, _pack_rows(w), _pack_rows(m), _pack_rows(v))
    grad.update(_unpack_rows(g_small, w))
    delta = _unpack_rows(d_small, w)
    new_m = _unpack_rows(m_small, w)
    new_v = _unpack_rows(v_small, w)
    for k in _BIG + ("w_ada",):
        shp = w[k].shape
        dk, mk, vk = _adamw(w[k].reshape(shp[1:]), grad[k].reshape(shp[1:]), m[k].reshape(shp[1:]), v[k].reshape(shp[1:]),
                            "adamw_" + k)
        delta[k], new_m[k], new_v[k] = dk.reshape(shp), mk.reshape(shp), vk.reshape(shp)
    return (loss8[0, 0], gx, *[grad[k] for k in _WEIGHTS], *[delta[k] for k in _WEIGHTS],
            *[new_m[k] for k in _WEIGHTS], *[new_v[k] for k in _WEIGHTS])
```

```python
import functools
import math

import jax
import jax.numpy as jnp
from jax import lax
from jax.experimental import pallas as pl
from jax.experimental.pallas import tpu as pltpu

F32 = jnp.float32
BF16 = jnp.bfloat16
MESH = pl.DeviceIdType.MESH

EPS = 1e-6
D_MODEL = 1024
D_FF = 4096
HEADS = 4
RET_DK = 64
RET_DV = 128
MLA_NOPE = 128
MLA_ROPE = 64
MLA_HEAD = 256
Q_LORA = 384
KV_LORA = 256
GRID_W = 64
ROPE_BASE = 10000.0
IN_COLS = 2240
IN_PAD = 2304
PG_COLS = 1152
N_CHIPS = 4
N_DEV = 8
LANES = 128
ADAM_LR = 0.001
ADAM_B1 = 0.9
ADAM_B2 = 0.999
ADAM_EPS = 1e-08
ADAM_WD = 0.01
ADAM_STEP = 10
VMEM_LIMIT = 56 * 1024 * 1024


def _dot(a, b):
    return jnp.dot(a, b, preferred_element_type=F32)


def _dot_nt(a, b):
    return lax.dot_general(a, b, (((1,), (1,)), ((), ())), preferred_element_type=F32)


def _dot_tn(a, b):
    return lax.dot_general(a, b, (((0,), (0,)), ((), ())), preferred_element_type=F32)


def _params(sem=None, vmem=None):
    return pltpu.CompilerParams(dimension_semantics=sem, vmem_limit_bytes=vmem)


def _full(shape):
    n = len(shape)
    return pl.BlockSpec(shape, lambda *_: (0,) * n)


def _rope(x, cos, sin):
    w = x.shape[-1]
    lo = (lax.broadcasted_iota(jnp.int32, (1, w), 1) % 64) < 32
    swapped = jnp.where(lo, pltpu.roll(x, w - 32, 1), pltpu.roll(x, 32, 1))
    return x * cos + swapped * sin


def _rope_t(g, cos, sin):
    w = g.shape[-1]
    lo = (lax.broadcasted_iota(jnp.int32, (1, w), 1) % 64) < 32
    t = g * sin
    swapped = jnp.where(lo, pltpu.roll(t, w - 32, 1), pltpu.roll(t, 32, 1))
    return g * cos + swapped


def _rope_tables(seq, tm):
    rows = seq // GRID_W
    row = jnp.repeat(jnp.arange(rows, dtype=F32), GRID_W)
    col = jnp.tile(jnp.arange(GRID_W, dtype=F32), rows)
    n_freq = RET_DK // 4
    freq = ROPE_BASE ** (-jnp.arange(n_freq, dtype=F32) / n_freq)
    ang = jnp.concatenate([row[:, None] * freq, col[:, None] * freq], axis=-1)
    cos, sin = jnp.cos(ang), jnp.sin(ang)
    cos_t = jnp.tile(jnp.concatenate([cos, cos], -1), (1, HEADS))
    sin_t = jnp.tile(jnp.concatenate([-sin, sin], -1), (1, HEADS))
    cos_t = jnp.concatenate([cos_t, jnp.ones((tm, 4 * RET_DK), F32)], 0)
    sin_t = jnp.concatenate([sin_t, jnp.zeros((tm, 4 * RET_DK), F32)], 0)
    return cos_t, sin_t


def _adam_math(w, g, m, v):
    mn = ADAM_B1 * m + (1.0 - ADAM_B1) * g
    vn = ADAM_B2 * v + (1.0 - ADAM_B2) * (g * g)
    m_hat = mn / (1.0 - ADAM_B1 ** ADAM_STEP)
    v_hat = vn / (1.0 - ADAM_B2 ** ADAM_STEP)
    return -ADAM_LR * (m_hat / (jnp.sqrt(v_hat) + ADAM_EPS) + ADAM_WD * w), mn, vn


def _cast_into_slot(w, slot, name):
    r, c = w.shape
    rb = math.gcd(r, 256)

    def body(s_ref, w_ref, o_ref):
        o_ref[...] = w_ref[...].astype(BF16)

    return pl.pallas_call(
        body, name=name,
        grid_spec=pltpu.PrefetchScalarGridSpec(
            num_scalar_prefetch=1, grid=(r // rb,),
            in_specs=[pl.BlockSpec((rb, c), lambda i, s: (i, 0))],
            out_specs=pl.BlockSpec((None, rb, c), lambda i, s: (s[0], i, 0))),
        out_shape=jax.ShapeDtypeStruct((N_CHIPS, r, c), BF16),
        compiler_params=_params(("parallel",)),
    )(slot, w)


def _adamw_halves(w, mine, theirs, m, v, core, name):
    r, c = w.shape
    r2 = r // 2
    rb = 8
    for cand in (256, 128, 64, 32, 16):
        if r2 % cand == 0 and cand * c * 4 <= (1 << 20):
            rb = cand
            break
    nbh = r2 // rb

    def body(z_ref, w_ref, a_ref, b_ref, m_ref, v_ref, g_ref, d_ref, mo_ref, vo_ref):
        here = (pl.program_id(0) // nbh) == z_ref[0]
        gg = jnp.where(here, a_ref[...], b_ref[...])
        g_ref[...] = gg
        d_ref[...], mo_ref[...], vo_ref[...] = _adam_math(w_ref[...], gg, m_ref[...], v_ref[...])

    spec = pl.BlockSpec((rb, c), lambda i, z: (i, 0))
    a_spec = pl.BlockSpec((rb, c), lambda i, z: (jnp.clip(i - z[0] * nbh, 0, nbh - 1), 0))
    b_spec = pl.BlockSpec((rb, c), lambda i, z: (jnp.clip(i - (1 - z[0]) * nbh, 0, nbh - 1), 0))
    shp = jax.ShapeDtypeStruct((r, c), F32)
    return pl.pallas_call(
        body, name=name,
        grid_spec=pltpu.PrefetchScalarGridSpec(
            num_scalar_prefetch=1, grid=(r // rb,), in_specs=[spec, a_spec, b_spec, spec, spec], out_specs=[spec] * 4),
        out_shape=[shp] * 4,
        compiler_params=_params(("parallel",)),
    )(core, w, mine, theirs, m, v)


def _adamw(w, g, m, v, name):
    r, c = w.shape
    rb = r
    for cand in (256, 128, 64, 32, 16, 8):
        if r % cand == 0 and cand * c * 4 <= (1 << 20):
            rb = cand
            break
    if r * c * 4 <= (1 << 20):
        rb = r

    def body(w_ref, g_ref, m_ref, v_ref, d_ref, mo_ref, vo_ref):
        d_ref[...], mo_ref[...], vo_ref[...] = _adam_math(w_ref[...], g_ref[...], m_ref[...], v_ref[...])

    spec = pl.BlockSpec((rb, c), lambda i: (i, 0))
    shp = jax.ShapeDtypeStruct((r, c), F32)
    return pl.pallas_call(
        body, name=name, grid=(r // rb,), in_specs=[spec] * 4, out_specs=[spec] * 3, out_shape=[shp] * 3,
        compiler_params=_params(("parallel",)),
    )(w, g, m, v)


def _decay_prep(dec):
    def body(d_ref, lg_ref, sg_ref):
        d = d_ref[...]
        lg_ref[...] = jnp.minimum(d, 0.0) - jnp.log(1.0 + jnp.exp(-jnp.abs(d)))
        sg_ref[...] = 1.0 / (1.0 + jnp.exp(d))

    shp = jax.ShapeDtypeStruct(dec.shape, F32)
    return pl.pallas_call(body, name="decay_prep", out_shape=[shp, shp])(dec)


def _mod_fwd(a_in, w_ada, b_sh):
    rows, d = a_in.shape
    n = w_ada.shape[1]
    bn = 512

    def body(a_ref, w_ref, b_ref, o_ref):
        a = a_ref[...]
        s = (a / (1.0 + jnp.exp(-a))).astype(BF16)
        o_ref[...] = _dot(s, w_ref[...].astype(BF16)) + b_ref[...]

    return pl.pallas_call(
        body, name="mod_fwd", grid=(n // bn,),
        in_specs=[_full((rows, d)), pl.BlockSpec((d, bn), lambda j: (0, j)), pl.BlockSpec((1, bn), lambda j: (0, j))],
        out_specs=pl.BlockSpec((rows, bn), lambda j: (0, j)),
        out_shape=jax.ShapeDtypeStruct((rows, n), F32),
        compiler_params=_params(("parallel",)),
    )(a_in, w_ada, b_sh)


def _mod_bwd(a_in, dm, w_ada):
    rows, d = a_in.shape
    n = w_ada.shape[1]
    bn = 512
    nb = n // bn

    def body(a_ref, dm_ref, w_ref, gw_ref, da_ref):
        j = pl.program_id(0)
        a = a_ref[...]
        s = (a / (1.0 + jnp.exp(-a))).astype(BF16)
        dmb = dm_ref[...].astype(BF16)
        gw_ref[...] = _dot_tn(s, dmb)
        part = _dot_nt(dmb, w_ref[...].astype(BF16))

        @pl.when(j == 0)
        def _():
            da_ref[...] = part

        @pl.when(j > 0)
        def _():
            da_ref[...] += part

    return pl.pallas_call(
        body, name="mod_bwd", grid=(nb,),
        in_specs=[_full((rows, d)), pl.BlockSpec((rows, bn), lambda j: (0, j)), pl.BlockSpec((d, bn), lambda j: (0, j))],
        out_specs=[pl.BlockSpec((d, bn), lambda j: (0, j)), _full((rows, d))],
        out_shape=[jax.ShapeDtypeStruct((d, n), F32), jax.ShapeDtypeStruct((rows, d), F32)],
        compiler_params=_params(("arbitrary",)),
    )(a_in, dm, w_ada)


def _pre_fwd(x2, ctx2, modv, g_attn, w_in, g_q, g_kv, w_uq, w_ukv, cos_t, sin_t, *, seq, tm):
    t_lat, d = x2.shape
    t_ctx = ctx2.shape[0]
    nl, nc = t_lat // tm, t_ctx // tm
    n_all = t_lat + t_ctx
    tpe = seq // tm
    nex = t_lat // seq

    def body(x_ref, c_ref, mod_ref, g_ref, win_ref, gq_ref, gkv_ref, wuq_ref, wukv_ref, cos_ref, sin_ref,
             h_ref, pg_ref, rq_ref, rk_ref, rv_ref, nq_ref, nkv_ref, q_ref, k_ref, v_ref):
        i = pl.program_id(0)
        xt = jnp.where(i < nl, x_ref[...], c_ref[...])
        sh = mod_ref[0, 0:1, :]
        sc = mod_ref[0, 1:2, :]
        r = lax.rsqrt(jnp.mean(xt * xt, axis=-1, keepdims=True) + EPS)
        hb = ((xt * r) * g_ref[...] * (1.0 + sc) + sh).astype(BF16)
        h_ref[...] = hb
        p = _dot(hb, win_ref[...])
        cos = cos_ref[...]
        sin = sin_ref[...]
        rq_ref[...] = _rope(p[:, 0:256], cos, sin).astype(BF16)
        rk_ref[...] = _rope(p[:, 256:512] * (RET_DK ** -0.5), cos, sin).astype(BF16)
        rv_ref[...] = p[:, 512:1024].astype(BF16)
        pg_ref[...] = p[:, 1024:2176]
        cq = p[:, 1536:1920]
        ckv = p[:, 1920:2176]
        nqb = (cq * lax.rsqrt(jnp.mean(cq * cq, axis=-1, keepdims=True) + EPS) * gq_ref[...]).astype(BF16)
        nkvb = (ckv * lax.rsqrt(jnp.mean(ckv * ckv, axis=-1, keepdims=True) + EPS) * gkv_ref[...]).astype(BF16)
        nq_ref[...] = nqb
        nkv_ref[...] = nkvb
        cos1 = cos[:, 0:LANES]
        sin1 = sin[:, 0:LANES]
        kpe = _rope(p[:, 2176:2304], cos1, sin1).astype(BF16)
        for hd in range(HEADS):
            o = hd * MLA_HEAD
            qh = _dot(nqb, wuq_ref[hd])
            q_ref[:, o:o + 128] = qh[:, 0:128].astype(BF16)
            q_ref[:, o + 128:o + 256] = _rope(qh[:, 128:256], cos1, sin1).astype(BF16)
            kvh = _dot(nkvb, wukv_ref[hd])
            k_ref[:, o:o + 128] = kvh[:, 0:128].astype(BF16)
            k_ref[:, o + 128:o + 256] = kpe
            v_ref[:, hd * 128:(hd + 1) * 128] = kvh[:, 128:256].astype(BF16)

    def tile(width):
        return pl.BlockSpec((tm, width), lambda i: (i, 0))

    widths = (d, PG_COLS, 256, 256, 512, Q_LORA, KV_LORA, HEADS * MLA_HEAD, HEADS * MLA_HEAD, HEADS * 128)
    dtypes = (BF16, F32, BF16, BF16, BF16, BF16, BF16, BF16, BF16, BF16)
    tab = pl.BlockSpec((tm, 256), lambda i: (jnp.where(i < nl, i % tpe, tpe), 0))
    return pl.pallas_call(
        body, name="pre_fwd", grid=(nl + nc,),
        in_specs=[
            pl.BlockSpec((tm, d), lambda i: (jnp.minimum(i, nl - 1), 0)),
            pl.BlockSpec((tm, d), lambda i: (jnp.maximum(i - nl, 0), 0)),
            pl.BlockSpec((1, 8, d), lambda i: (jnp.minimum(i // tpe, nex), 0, 0)),
            _full((1, d)), _full(w_in.shape), _full((1, Q_LORA)), _full((1, KV_LORA)),
            _full(w_uq.shape), _full(w_ukv.shape), tab, tab,
        ],
        out_specs=[tile(w) for w in widths],
        out_shape=[jax.ShapeDtypeStruct((n_all, w), dt) for w, dt in zip(widths, dtypes)],
        compiler_params=_params(("parallel",), VMEM_LIMIT),
    )(x2, ctx2, modv, g_attn, w_in, g_q, g_kv, w_uq, w_ukv, cos_t, sin_t)


def _post(yret, ymla, x2, tgt2, modv, g_ffn, g_fin, w_out, w_ff1, w_ff2, *, seq, tm):
    t_lat, d = x2.shape
    nl = t_lat // tm
    tpe = seq // tm
    nex = t_lat // seq
    n_slab = w_ff1.shape[0]
    fs = w_ff1.shape[2]

    def body(yr_ref, ym_ref, x_ref, t_ref, mod_ref, gf_ref, gl_ref, wo_ref, w1_ref, w2_ref,
             mix_ref, a_ref, du_ref, h2_ref, df_ref, dmo_ref, dmix_ref, dxm_ref, st_ref, ru_ref):
        i = pl.program_id(0)
        gt_a = mod_ref[0, 2:3, :]
        sh_f = mod_ref[0, 3:4, :]
        sc_f = mod_ref[0, 4:5, :]
        gt_f = mod_ref[0, 5:6, :]
        g_ffn_v = gf_ref[...]
        g_fin_v = gl_ref[...]
        yr = yr_ref[...]
        ym = ym_ref[...]
        mix_ref[:, 0:512] = yr
        mix_ref[:, 512:1024] = ym
        op = _dot(yr, wo_ref[0:512, :]) + _dot(ym, wo_ref[512:1024, :])
        x_mid = x_ref[...] + gt_a * op
        r2 = lax.rsqrt(jnp.mean(x_mid * x_mid, axis=-1, keepdims=True) + EPS)
        xh2 = x_mid * r2
        h2b = (xh2 * g_ffn_v * (1.0 + sc_f) + sh_f).astype(BF16)
        h2_ref[...] = h2b
        f = jnp.zeros((tm, d), F32)
        for s in range(n_slab):
            ru = jnp.maximum(_dot(h2b, w1_ref[s]), 0.0)
            ru_ref[:, s * fs:(s + 1) * fs] = ru
            ab = (ru * ru).astype(BF16)
            a_ref[:, s * fs:(s + 1) * fs] = ab
            f = f + _dot(ab, w2_ref[s * fs:(s + 1) * fs, :])
        x_out = x_mid + gt_f * f
        r3 = lax.rsqrt(jnp.mean(x_out * x_out, axis=-1, keepdims=True) + EPS)
        xh3 = x_out * r3
        err = xh3 * g_fin_v - t_ref[...]
        dy = err * (1.0 / d)
        dxh3 = dy * g_fin_v
        dx_out = r3 * (dxh3 - xh3 * jnp.mean(dxh3 * xh3, axis=-1, keepdims=True))
        dfb = (dx_out * gt_f).astype(BF16)
        df_ref[...] = dfb
        dh2 = jnp.zeros((tm, d), F32)
        for s in range(n_slab):
            da = _dot_nt(dfb, w2_ref[s * fs:(s + 1) * fs, :])
            dub = (da * (2.0 * ru_ref[:, s * fs:(s + 1) * fs])).astype(BF16)
            du_ref[:, s * fs:(s + 1) * fs] = dub
            dh2 = dh2 + _dot_nt(dub, w1_ref[s])
        dxh2 = dh2 * (1.0 + sc_f) * g_ffn_v
        dx_mid = dx_out + r2 * (dxh2 - xh2 * jnp.mean(dxh2 * xh2, axis=-1, keepdims=True))
        dxm_ref[...] = dx_mid
        dmob = (dx_mid * gt_a).astype(BF16)
        dmo_ref[...] = dmob
        dmix_ref[...] = _dot_nt(dmob, wo_ref[...]).astype(BF16)

        def rsum(v):
            return jnp.sum(v, axis=0, keepdims=True)

        stats = jnp.concatenate([
            rsum(dh2), rsum(dh2 * xh2 * g_ffn_v), rsum(dx_out * f), rsum(dx_mid * op),
            rsum(dh2 * (1.0 + sc_f) * xh2), rsum(dy * xh3), rsum(err * err), jnp.zeros((1, d), F32)], axis=0)

        @pl.when(i % tpe == 0)
        def _():
            st_ref[0] = stats

        @pl.when(i % tpe != 0)
        def _():
            st_ref[0] += stats

    def tile(width):
        return pl.BlockSpec((tm, width), lambda i: (i, 0))

    widths = (d, D_FF, D_FF, d, d, d, d, d)
    dtypes = (BF16, BF16, BF16, BF16, BF16, BF16, BF16, F32)
    const = pl.Buffered(1)
    return pl.pallas_call(
        body, name="post", grid=(nl,),
        in_specs=[
            tile(512), tile(512), tile(d), tile(d),
            pl.BlockSpec((1, 8, d), lambda i: (i // tpe, 0, 0)),
            _full((1, d)), _full((1, d)),
            pl.BlockSpec(w_out.shape, lambda i: (0, 0), pipeline_mode=const),
            pl.BlockSpec(w_ff1.shape, lambda i: (0, 0, 0), pipeline_mode=const),
            pl.BlockSpec(w_ff2.shape, lambda i: (0, 0), pipeline_mode=const),
        ],
        out_specs=[tile(w) for w in widths] + [pl.BlockSpec((1, 8, d), lambda i: (i // tpe, 0, 0))],
        out_shape=[jax.ShapeDtypeStruct((t_lat, w), dt) for w, dt in zip(widths, dtypes)]
        + [jax.ShapeDtypeStruct((nex, 8, d), F32)],
        scratch_shapes=[pltpu.VMEM((tm, D_FF), F32)],
        compiler_params=_params(("arbitrary",), VMEM_LIMIT),
    )(yret, ymla, x2, tgt2, modv, g_ffn, g_fin, w_out, w_ff1, w_ff2)


def _pre_bwd(x2, ctx2, modv, g_attn, pg, drq, drk, dkc_r, drv, dvc_r, drg, dq_m, dkl, dkc, dvl, dvc, dxm,
             w_in, g_q, g_kv, w_uq, w_ukv, cos_t, sin_t, *, seq, tm):
    t_lat, d = x2.shape
    t_ctx = ctx2.shape[0]
    nl, nc = t_lat // tm, t_ctx // tm
    n_all = t_lat + t_ctx
    tpe = seq // tm
    nex = t_lat // seq

    def body(x_ref, c_ref, mod_ref, g_ref, pg_ref, drq_ref, drk_ref, dkcr_ref, drv_ref, dvcr_ref, drg_ref,
             dq_ref, dkl_ref, dkc_ref, dvl_ref, dvc_ref, dxm_ref, win_ref, gq_ref, gkv_ref, wuq_ref, wukv_ref,
             cos_ref, sin_ref, dpb_ref, dqf_ref, dkvf_ref, gx_ref, st_ref):
        i = pl.program_id(0)
        lat = i < nl
        latf = lat.astype(F32)
        cos = cos_ref[...]
        sin = sin_ref[...]
        cos1 = cos[:, 0:LANES]
        sin1 = sin[:, 0:LANES]
        d_rq = _rope_t(drq_ref[...] * latf, cos, sin)
        d_rk = _rope_t(jnp.where(lat, drk_ref[...], dkcr_ref[...]), cos, sin) * (RET_DK ** -0.5)
        d_rv = jnp.where(lat, drv_ref[...], dvcr_ref[...])
        d_rg = drg_ref[...] * latf
        dq_all = dq_ref[...] * latf
        dk_all = jnp.where(lat, dkl_ref[...], dkc_ref[...])
        dv_all = jnp.where(lat, dvl_ref[...], dvc_ref[...])
        dnq = jnp.zeros((tm, Q_LORA), F32)
        dnkv = jnp.zeros((tm, KV_LORA), F32)
        dkpe = jnp.zeros((tm, LANES), F32)
        for hd in range(HEADS):
            o = hd * MLA_HEAD
            dqh = jnp.concatenate([dq_all[:, o:o + 128], _rope_t(dq_all[:, o + 128:o + 256], cos1, sin1)],
                                  axis=1).astype(BF16)
            dqf_ref[:, o:o + 256] = dqh
            dnq = dnq + _dot_nt(dqh, wuq_ref[hd])
            dkpe = dkpe + dk_all[:, o + 128:o + 256]
            dkvh = jnp.concatenate([dk_all[:, o:o + 128], dv_all[:, hd * 128:(hd + 1) * 128]], axis=1).astype(BF16)
            dkvf_ref[:, o:o + 256] = dkvh
            dnkv = dnkv + _dot_nt(dkvh, wukv_ref[hd])
        d_kpe = _rope_t(dkpe, cos1, sin1)
        pgv = pg_ref[...]
        cq = pgv[:, 512:896]
        ckv = pgv[:, 896:1152]
        rq_ = lax.rsqrt(jnp.mean(cq * cq, axis=-1, keepdims=True) + EPS)
        cqh = cq * rq_
        dcqh = dnq * gq_ref[...]
        d_cq = rq_ * (dcqh - cqh * jnp.mean(dcqh * cqh, axis=-1, keepdims=True))
        rkv_ = lax.rsqrt(jnp.mean(ckv * ckv, axis=-1, keepdims=True) + EPS)
        ckvh = ckv * rkv_
        dckvh = dnkv * gkv_ref[...]
        d_ckv = rkv_ * (dckvh - ckvh * jnp.mean(dckvh * ckvh, axis=-1, keepdims=True))
        dpb = jnp.concatenate([d_rq, d_rk, d_rv, d_rg, d_cq, d_ckv, d_kpe], axis=1).astype(BF16)
        dpb_ref[...] = dpb
        dh = _dot_nt(dpb, win_ref[...])
        xt = jnp.where(lat, x_ref[...], c_ref[...])
        sc = mod_ref[0, 1:2, :]
        g = g_ref[...]
        r = lax.rsqrt(jnp.mean(xt * xt, axis=-1, keepdims=True) + EPS)
        xh = xt * r
        dxh = dh * (1.0 + sc) * g
        dx = r * (dxh - xh * jnp.mean(dxh * xh, axis=-1, keepdims=True))

        @pl.when(lat)
        def _():
            gx_ref[...] = dxm_ref[...] + dx

        def rsum(v):
            return jnp.sum(v, axis=0, keepdims=True)

        def widen(v):
            return jnp.concatenate([v, jnp.zeros((1, d - v.shape[1]), F32)], axis=1)

        stats = jnp.concatenate([
            rsum(dh), rsum(dh * xh * g), rsum(dh * (1.0 + sc) * xh), widen(rsum(dnq * cqh)), widen(rsum(dnkv * ckvh)),
            jnp.zeros((3, d), F32)], axis=0)
        first = jnp.logical_or(jnp.logical_and(lat, i % tpe == 0), i == nl)

        @pl.when(first)
        def _():
            st_ref[0] = stats

        @pl.when(jnp.logical_not(first))
        def _():
            st_ref[0] += stats

    def lat_tile(width):
        return pl.BlockSpec((tm, width), lambda i: (jnp.minimum(i, nl - 1), 0))

    def ctx_tile(width):
        return pl.BlockSpec((tm, width), lambda i: (jnp.maximum(i - nl, 0), 0))

    def tile(width):
        return pl.BlockSpec((tm, width), lambda i: (i, 0))

    tab = pl.BlockSpec((tm, 256), lambda i: (jnp.where(i < nl, i % tpe, tpe), 0))
    ex = pl.BlockSpec((1, 8, d), lambda i: (jnp.minimum(i // tpe, nex), 0, 0))
    return pl.pallas_call(
        body, name="pre_bwd", grid=(nl + nc,),
        in_specs=[
            lat_tile(d), ctx_tile(d), ex, _full((1, d)), tile(PG_COLS),
            lat_tile(256), lat_tile(256), ctx_tile(256), lat_tile(512), ctx_tile(512), lat_tile(512),
            lat_tile(1024), lat_tile(1024), ctx_tile(1024), lat_tile(512), ctx_tile(512), lat_tile(d),
            _full(w_in.shape), _full((1, Q_LORA)), _full((1, KV_LORA)), _full(w_uq.shape), _full(w_ukv.shape),
            tab, tab,
        ],
        out_specs=[tile(IN_PAD), tile(1024), tile(1024), lat_tile(d), ex],
        out_shape=[
            jax.ShapeDtypeStruct((n_all, IN_PAD), BF16), jax.ShapeDtypeStruct((n_all, 1024), BF16),
            jax.ShapeDtypeStruct((n_all, 1024), BF16), jax.ShapeDtypeStruct((t_lat, d), F32),
            jax.ShapeDtypeStruct((nex + 1, 8, d), F32),
        ],
        compiler_params=_params(("arbitrary",), VMEM_LIMIT),
    )(x2, ctx2, modv, g_attn, pg, drq, drk, dkc_r, drv, dvc_r, drg, dq_m, dkl, dkc, dvl, dvc, dxm,
      w_in, g_q, g_kv, w_uq, w_ukv, cos_t, sin_t)


def _softmax_parts(qb, kl, kc):
    scale = 1.0 / math.sqrt(MLA_NOPE + MLA_ROPE)
    s = _dot_nt(qb, kl) * scale
    sc = _dot_nt(qb, kc) * scale
    m = jnp.maximum(jnp.max(s, axis=-1, keepdims=True), jnp.max(sc, axis=-1, keepdims=True))
    p = jnp.exp(s - m)
    pc = jnp.exp(sc - m)
    inv = 1.0 / (jnp.sum(p, axis=-1, keepdims=True) + jnp.sum(pc, axis=-1, keepdims=True))
    return p, pc, inv, scale


def _mla_specs(t_lat, seq, ctx_len, tq):
    nqt = seq // tq
    cb = t_lat // ctx_len
    q = pl.BlockSpec((tq, MLA_HEAD), lambda b, h, j: (b * nqt + j, h))
    kl = pl.BlockSpec((seq, MLA_HEAD), lambda b, h, j: (b, h))
    kc = pl.BlockSpec((ctx_len, MLA_HEAD), lambda b, h, j: (cb + b, h))
    vl = pl.BlockSpec((seq, 128), lambda b, h, j: (b, h))
    vc = pl.BlockSpec((ctx_len, 128), lambda b, h, j: (cb + b, h))
    o = pl.BlockSpec((tq, 128), lambda b, h, j: (b * nqt + j, h))
    return q, kl, kc, vl, vc, o


def _mla_fwd(q, k, v, *, t_lat, seq, ctx_len, tq):
    nex = t_lat // seq

    def body(q_ref, kl_ref, kc_ref, vl_ref, vc_ref, o_ref):
        p, pc, inv, _ = _softmax_parts(q_ref[...], kl_ref[...], kc_ref[...])
        o = _dot(p.astype(BF16), vl_ref[...]) + _dot(pc.astype(BF16), vc_ref[...])
        o_ref[...] = (o * inv).astype(BF16)

    qs, kl, kc, vl, vc, os_ = _mla_specs(t_lat, seq, ctx_len, tq)
    return pl.pallas_call(
        body, name="mla_fwd", grid=(nex, HEADS, seq // tq),
        in_specs=[qs, kl, kc, vl, vc], out_specs=os_,
        out_shape=jax.ShapeDtypeStruct((t_lat, HEADS * 128), BF16),
        compiler_params=_params(("parallel", "parallel", "arbitrary"), VMEM_LIMIT),
    )(q, k, k, v, v)


def _mla_bwd(q, k, v, ymla, dmix, *, t_lat, seq, ctx_len, tq):
    nex = t_lat // seq
    nqt = seq // tq
    t_ctx = nex * ctx_len

    def body(q_ref, kl_ref, kc_ref, vl_ref, vc_ref, o_ref, do_ref, dq_ref, dkl_ref, dkc_ref, dvl_ref, dvc_ref):
        j = pl.program_id(2)
        qb = q_ref[...]
        p, pc, inv, scale = _softmax_parts(qb, kl_ref[...], kc_ref[...])
        p = p * inv
        pc = pc * inv
        dob = do_ref[...]
        delta = jnp.sum(dob.astype(F32) * o_ref[...].astype(F32), axis=-1, keepdims=True)
        ds = (p * (_dot_nt(dob, vl_ref[...]) - delta) * scale).astype(BF16)
        dsc = (pc * (_dot_nt(dob, vc_ref[...]) - delta) * scale).astype(BF16)
        dq_ref[...] = _dot(ds, kl_ref[...]) + _dot(dsc, kc_ref[...])
        pb = p.astype(BF16)
        pcb = pc.astype(BF16)

        @pl.when(j == 0)
        def _():
            dkl_ref[...] = _dot_tn(ds, qb)
            dkc_ref[...] = _dot_tn(dsc, qb)
            dvl_ref[...] = _dot_tn(pb, dob)
            dvc_ref[...] = _dot_tn(pcb, dob)

        @pl.when(j > 0)
        def _():
            dkl_ref[...] += _dot_tn(ds, qb)
            dkc_ref[...] += _dot_tn(dsc, qb)
            dvl_ref[...] += _dot_tn(pb, dob)
            dvc_ref[...] += _dot_tn(pcb, dob)

    qs, kl, kc, vl, vc, os_ = _mla_specs(t_lat, seq, ctx_len, tq)
    do_spec = pl.BlockSpec((tq, 128), lambda b, h, j: (b * nqt + j, HEADS + h))
    return pl.pallas_call(
        body, name="mla_bwd", grid=(nex, HEADS, nqt),
        in_specs=[qs, kl, kc, vl, vc, os_, do_spec],
        out_specs=[
            qs,
            pl.BlockSpec((seq, MLA_HEAD), lambda b, h, j: (b, h)),
            pl.BlockSpec((ctx_len, MLA_HEAD), lambda b, h, j: (b, h)),
            pl.BlockSpec((seq, 128), lambda b, h, j: (b, h)),
            pl.BlockSpec((ctx_len, 128), lambda b, h, j: (b, h)),
        ],
        out_shape=[
            jax.ShapeDtypeStruct((t_lat, HEADS * MLA_HEAD), F32),
            jax.ShapeDtypeStruct((t_lat, HEADS * MLA_HEAD), F32),
            jax.ShapeDtypeStruct((t_ctx, HEADS * MLA_HEAD), F32),
            jax.ShapeDtypeStruct((t_lat, HEADS * 128), F32),
            jax.ShapeDtypeStruct((t_ctx, HEADS * 128), F32),
        ],
        compiler_params=_params(("parallel", "parallel", "arbitrary"), VMEM_LIMIT),
    )(q, k, k, v, v, ymla, dmix)


def _decay_terms(lg, chunk, forward):
    ii = lax.broadcasted_iota(jnp.int32, (chunk, chunk), 0)
    jj = lax.broadcasted_iota(jnp.int32, (chunk, chunk), 1)
    diff = (ii - jj) if forward else (jj - ii)
    dist = jnp.maximum(diff, 0).astype(F32)
    dmat = jnp.where(diff >= 0, jnp.exp(lg * dist), 0.0)
    pos = lax.broadcasted_iota(jnp.int32, (chunk, 1), 0).astype(F32)
    if forward:
        e_q = pos + 1.0
        e_k = (chunk - 1.0) - pos
    else:
        e_q = chunk - pos
        e_k = pos
    wq = jnp.exp(lg * e_q)
    wk = jnp.exp(lg * e_k)
    cd = jnp.exp(jnp.full((1, 1), lg * chunk, F32))
    return dmat, dist, wq, wk, e_q, e_k, cd


def _ctx_weights(lg, ctx_len, forward):
    pos = lax.broadcasted_iota(jnp.int32, (ctx_len, 1), 0).astype(F32)
    e = ((ctx_len - 1.0) - pos) if forward else pos
    return jnp.exp(lg * e), e


def _ret_specs(t_lat, seq, ctx_len):
    cb = t_lat // ctx_len
    qk = pl.BlockSpec((seq, 128), lambda b, h: (b, h // 2))
    v = pl.BlockSpec((seq, 128), lambda b, h: (b, h))
    kc = pl.BlockSpec((ctx_len, 128), lambda b, h: (cb + b, h // 2))
    vc = pl.BlockSpec((ctx_len, 128), lambda b, h: (cb + b, h))
    return qk, v, kc, vc


def _head_mask(h):
    lane = lax.broadcasted_iota(jnp.int32, (1, 128), 1)
    return (lane // RET_DK) == (h % 2)


def _ret_fwd(rq, rk, rv, pg, lg, g_ret, *, t_lat, seq, ctx_len, chunk):
    nex = t_lat // seq
    n_chunk = seq // chunk

    def body(q_ref, k_ref, v_ref, kc_ref, vc_ref, rg_ref, lg_ref, g_ref, y_ref, o_ref):
        h = pl.program_id(1)
        hm = _head_mask(h)
        gain = g_ref[...]
        kcm = jnp.where(hm, kc_ref[...].astype(F32), 0.0)
        vcb = vc_ref[...]

        def run(forward):
            lgd = lg_ref[0 if forward else 1, h]
            dmat, _, wq, wk, _, _, cd = _decay_terms(lgd, chunk, forward)
            wc, _ = _ctx_weights(lgd, ctx_len, forward)
            s0 = _dot_tn((kcm * wc).astype(BF16), vcb)

            def step(t, s):
                n = t if forward else n_chunk - 1 - t
                sl = pl.ds(pl.multiple_of(n * chunk, chunk), chunk)
                qm = jnp.where(hm, q_ref[sl, :], jnp.zeros((), BF16))
                kf = jnp.where(hm, k_ref[sl, :].astype(F32), 0.0)
                vb = v_ref[sl, :]
                a = _dot_nt(qm, kf.astype(BF16)) * dmat
                o = _dot(a.astype(BF16), vb) + wq * _dot(qm, s.astype(BF16))
                if forward:
                    o_ref[sl, :] = o
                else:
                    o = o_ref[sl, :] + o
                    o_ref[sl, :] = o
                    mu = jnp.mean(o, axis=-1, keepdims=True)
                    oc = o - mu
                    var = jnp.mean(oc * oc, axis=-1, keepdims=True)
                    on = oc * lax.rsqrt(var + EPS) * gain
                    rg = rg_ref[sl, :]
                    y_ref[sl, :] = (on * (rg / (1.0 + jnp.exp(-rg)))).astype(BF16)
                return cd * s + _dot_tn((kf * wk).astype(BF16), vb)

            lax.fori_loop(0, n_chunk, step, s0)

        run(True)
        run(False)

    qk, v, kc, vc = _ret_specs(t_lat, seq, ctx_len)
    return pl.pallas_call(
        body, name="ret_fwd", grid=(nex, HEADS),
        in_specs=[qk, qk, v, kc, vc, v, pl.BlockSpec(memory_space=pltpu.SMEM), pl.BlockSpec((1, 128), lambda b, h: (0, h))],
        out_specs=[v, v],
        out_shape=[jax.ShapeDtypeStruct((t_lat, HEADS * RET_DV), BF16), jax.ShapeDtypeStruct((t_lat, HEADS * RET_DV), F32)],
        compiler_params=_params(("parallel", "arbitrary"), VMEM_LIMIT),
    )(rq, rk, rv, rk, rv, pg, lg, g_ret)


def _ret_bwd(rq, rk, rv, pg, osum, dmix, lg, g_ret, *, t_lat, seq, ctx_len, chunk):
    nex = t_lat // seq
    n_chunk = seq // chunk
    t_ctx = nex * ctx_len

    def body(q_ref, k_ref, v_ref, kc_ref, vc_ref, rg_ref, o_ref, dy_ref, lg_ref, g_ref,
             dq_ref, dk_ref, dv_ref, dkc_ref, dvc_ref, drg_ref, st_ref, do_s, s_st):
        h = pl.program_id(1)
        hm = _head_mask(h)
        gain = g_ref[...]
        kcm = jnp.where(hm, kc_ref[...].astype(F32), 0.0)
        vcb = vc_ref[...]

        def norm_step(n, dgain):
            sl = pl.ds(pl.multiple_of(n * chunk, chunk), chunk)
            o = o_ref[sl, :]
            mu = jnp.mean(o, axis=-1, keepdims=True)
            oc = o - mu
            rstd = lax.rsqrt(jnp.mean(oc * oc, axis=-1, keepdims=True) + EPS)
            ohat = oc * rstd
            rg = rg_ref[sl, :]
            sg = 1.0 / (1.0 + jnp.exp(-rg))
            dy = dy_ref[sl, :].astype(F32)
            don = dy * (rg * sg)
            drg_ref[sl, :] = dy * (ohat * gain) * (sg * (1.0 + rg * (1.0 - sg)))
            dohat = don * gain
            do_s[sl, :] = rstd * (dohat - jnp.mean(dohat, axis=-1, keepdims=True)
                                  - ohat * jnp.mean(dohat * ohat, axis=-1, keepdims=True))
            return dgain + jnp.sum(don * ohat, axis=0, keepdims=True)

        dgain = lax.fori_loop(0, n_chunk, norm_step, jnp.zeros((1, 128), F32))

        @pl.when(h % 2 == 0)
        def _():
            dq_ref[...] = jnp.zeros(dq_ref.shape, F32)
            dk_ref[...] = jnp.zeros(dk_ref.shape, F32)
            dkc_ref[...] = jnp.zeros(dkc_ref.shape, F32)

        dv_ref[...] = jnp.zeros(dv_ref.shape, F32)

        def run(forward):
            lgd = lg_ref[0 if forward else 1, h]
            dmat, dist, wq, wk, e_q, e_k, cd = _decay_terms(lgd, chunk, forward)
            wc, e_c = _ctx_weights(lgd, ctx_len, forward)
            s0 = _dot_tn((kcm * wc).astype(BF16), vcb)

            def state_step(t, s):
                n = t if forward else n_chunk - 1 - t
                sl = pl.ds(pl.multiple_of(n * chunk, chunk), chunk)
                s_st[n] = s
                kf = jnp.where(hm, k_ref[sl, :].astype(F32), 0.0)
                return cd * s + _dot_tn((kf * wk).astype(BF16), v_ref[sl, :])

            lax.fori_loop(0, n_chunk, state_step, s0)

            def grad_step(t, carry):
                g_next, dlg = carry
                n = (n_chunk - 1 - t) if forward else t
                sl = pl.ds(pl.multiple_of(n * chunk, chunk), chunk)
                qm = jnp.where(hm, q_ref[sl, :], jnp.zeros((), BF16))
                kf = jnp.where(hm, k_ref[sl, :].astype(F32), 0.0)
                kb = kf.astype(BF16)
                vb = v_ref[sl, :]
                do = do_s[sl, :]
                dob = do.astype(BF16)
                s_n = s_st[n]
                s_nb = s_n.astype(BF16)
                gb = g_next.astype(BF16)
                dk_cross = wk * _dot_nt(vb, gb)
                dv_cross = _dot((kf * wk).astype(BF16), gb)
                a = _dot_nt(qm, kb) * dmat
                da_raw = _dot_nt(dob, vb)
                dab = (da_raw * dmat).astype(BF16)
                ab = a.astype(BF16)
                o_cross = wq * _dot(qm, s_nb)
                dq_ref[sl, :] += _dot(dab, kb) + wq * _dot_nt(dob, s_nb)
                dk_ref[sl, :] += _dot_tn(dab, qm) + dk_cross
                dv_ref[sl, :] += _dot_tn(ab, dob) + dv_cross
                dlg = (dlg + chunk * cd * jnp.sum(g_next * s_n, keepdims=True)
                       + jnp.sum(e_k * jnp.sum(kf * dk_cross, axis=-1, keepdims=True), keepdims=True)
                       + jnp.sum(dist * a * da_raw, keepdims=True)
                       + jnp.sum(e_q * jnp.sum(o_cross * do, axis=-1, keepdims=True), keepdims=True))
                g_new = cd * g_next + _dot_tn((qm.astype(F32) * wq).astype(BF16), dob)
                return g_new, dlg

            ds0, dlg = lax.fori_loop(0, n_chunk, grad_step, (jnp.zeros((128, 128), F32), jnp.zeros((1, 1), F32)))
            ds0b = ds0.astype(BF16)
            dkc_part = wc * _dot_nt(vcb, ds0b)
            dkc_ref[...] += dkc_part
            dvc_part = _dot((kcm * wc).astype(BF16), ds0b)
            dlg = dlg + jnp.sum(e_c * jnp.sum(kcm * dkc_part, axis=-1, keepdims=True), keepdims=True)
            return dvc_part, dlg

        dvc_f, dlg_f = run(True)
        dvc_b, dlg_b = run(False)
        dvc_ref[...] = dvc_f + dvc_b
        st_ref[0] = jnp.concatenate([
            dgain, jnp.broadcast_to(dlg_f, (1, 128)), jnp.broadcast_to(dlg_b, (1, 128)), jnp.zeros((5, 128), F32)], axis=0)

    qk, v, kc, vc = _ret_specs(t_lat, seq, ctx_len)
    dy_spec = v
    return pl.pallas_call(
        body, name="ret_bwd", grid=(nex, HEADS),
        in_specs=[qk, qk, v, kc, vc, v, v, dy_spec, pl.BlockSpec(memory_space=pltpu.SMEM),
                  pl.BlockSpec((1, 128), lambda b, h: (0, h))],
        out_specs=[
            qk, qk, v,
            pl.BlockSpec((ctx_len, 128), lambda b, h: (b, h // 2)),
            pl.BlockSpec((ctx_len, 128), lambda b, h: (b, h)),
            v,
            pl.BlockSpec((1, 8, 128), lambda b, h: (b, 0, h)),
        ],
        out_shape=[
            jax.ShapeDtypeStruct((t_lat, 256), F32), jax.ShapeDtypeStruct((t_lat, 256), F32),
            jax.ShapeDtypeStruct((t_lat, 512), F32), jax.ShapeDtypeStruct((t_ctx, 256), F32),
            jax.ShapeDtypeStruct((t_ctx, 512), F32), jax.ShapeDtypeStruct((t_lat, 512), F32),
            jax.ShapeDtypeStruct((nex, 8, 512), F32),
        ],
        scratch_shapes=[pltpu.VMEM((seq, 128), F32), pltpu.VMEM((n_chunk, 128, 128), F32)],
        compiler_params=_params(("parallel", "arbitrary"), VMEM_LIMIT),
    )(rq, rk, rv, rk, rv, pg, osum, dmix, lg, g_ret)


def _matmul_tn(a, b, *, bm, bn, bk, chip_major, name):
    tk, m = a.shape
    n = b.shape[1]

    def body(a_ref, b_ref, o_ref):
        k = pl.program_id(2)
        part = _dot_tn(a_ref[...], b_ref[...])

        @pl.when(k == 0)
        def _():
            o_ref[...] = part

        @pl.when(k > 0)
        def _():
            o_ref[...] += part

    if chip_major:
        out_spec = pl.BlockSpec((None, bm, bn), lambda i, j, k: (j, i, 0))
        out_shape = jax.ShapeDtypeStruct((n // bn, m, bn), F32)
    else:
        out_spec = pl.BlockSpec((bm, bn), lambda i, j, k: (i, j))
        out_shape = jax.ShapeDtypeStruct((m, n), F32)
    return pl.pallas_call(
        body, name=name, grid=(m // bm, n // bn, tk // bk),
        in_specs=[pl.BlockSpec((bk, bm), lambda i, j, k: (k, i)), pl.BlockSpec((bk, bn), lambda i, j, k: (k, j))],
        out_specs=out_spec, out_shape=out_shape,
        compiler_params=_params(("parallel", "parallel", "arbitrary"), VMEM_LIMIT),
    )(a, b)


def _local_step(x, ctx, tgt, modv, lg, g_attn, g_ffn, g_fin, g_ret, g_q, g_kv, w_in, w_uq, w_ukv, w_out, w_ff1, w_ff2,
                *, tm=256, tq=256, chunk=128):
    nex, seq, d = x.shape
    ctx_len = ctx.shape[1]
    t_lat = nex * seq
    x2 = x.reshape(t_lat, d)
    ctx2 = ctx.reshape(nex * ctx_len, d)
    tgt2 = tgt.reshape(t_lat, d)
    cos_t, sin_t = _rope_tables(seq, tm)
    dims = dict(t_lat=t_lat, seq=seq, ctx_len=ctx_len)

    hb, pg, rq, rk, rv, nq, nkv, q, k, v = _pre_fwd(x2, ctx2, modv, g_attn, w_in, g_q, g_kv, w_uq, w_ukv, cos_t, sin_t,
                                                    seq=seq, tm=tm)
    yret, osum = _ret_fwd(rq, rk, rv, pg, lg, g_ret, chunk=chunk, **dims)
    ymla = _mla_fwd(q, k, v, tq=tq, **dims)
    mix, act, du, h2, df, dmo, dmix, dxm, st_post = _post(yret, ymla, x2, tgt2, modv, g_ffn, g_fin, w_out, w_ff1, w_ff2,
                                                         seq=seq, tm=tm)
    dq_m, dkl, dkc, dvl, dvc = _mla_bwd(q, k, v, ymla, dmix, tq=tq, **dims)
    drq, drk, drv, dkc_r, dvc_r, drg, st_ret = _ret_bwd(rq, rk, rv, pg, osum, dmix, lg, g_ret, chunk=chunk, **dims)
    dpb, dqf, dkvf, gx, st_pre = _pre_bwd(x2, ctx2, modv, g_attn, pg, drq, drk, dkc_r, drv, dvc_r, drg, dq_m, dkl, dkc, dvl,
                                          dvc, dxm, w_in, g_q, g_kv, w_uq, w_ukv, cos_t, sin_t, seq=seq, tm=tm)
    bk = 512
    gw_ff2 = _matmul_tn(act, df, bm=512, bn=1024, bk=bk, chip_major=False, name="gw_ff2")
    gw_ff1 = _matmul_tn(h2, du, bm=512, bn=1024, bk=bk, chip_major=True, name="gw_ff1")
    gw_out = _matmul_tn(mix, dmo, bm=512, bn=1024, bk=bk, chip_major=False, name="gw_out")
    gw_in = _matmul_tn(hb, dpb, bm=512, bn=768, bk=bk, chip_major=False, name="gw_in")
    gw_uq = _matmul_tn(nq, dqf, bm=Q_LORA, bn=MLA_HEAD, bk=bk, chip_major=True, name="gw_uq")
    gw_ukv = _matmul_tn(nkv, dkvf, bm=KV_LORA, bn=256, bk=bk, chip_major=True, name="gw_ukv")
    grads = dict(w_in=gw_in, w_uq=gw_uq, w_ukv=gw_ukv, w_out=gw_out, w_ff1=gw_ff1, w_ff2=gw_ff2)
    return gx.reshape(nex, seq, d), grads, st_post, st_ret, st_pre


_ANY = pl.BlockSpec(memory_space=pl.ANY)
_VMEM = pl.BlockSpec(memory_space=pltpu.VMEM)
_OFFSETS = tuple((dx, dy, dc) for dx in (0, 1) for dy in (0, 1) for dc in (0, 1))[1:]
_CHIP_OFFSETS = ((1, 0), (0, 1), (1, 1))


def _place():
    return lax.axis_index("x"), lax.axis_index("y"), lax.axis_index("c")


def _flip(v, d):
    return 1 - v if d else v


def _allgather8(a, name):
    r, c = a.shape

    def body(a_ref, o_ref, send, recv, lsem):
        x, y, z = _place()
        me = 4 * x + 2 * y + z
        mine = pltpu.make_async_copy(a_ref, o_ref.at[me], lsem)
        mine.start()
        copies = []
        for k, (dx, dy, dc) in enumerate(_OFFSETS):
            cp = pltpu.make_async_remote_copy(
                src_ref=a_ref, dst_ref=o_ref.at[me], send_sem=send.at[k], recv_sem=recv.at[k],
                device_id=(_flip(x, dx), _flip(y, dy), _flip(z, dc)), device_id_type=MESH)
            cp.start()
            copies.append(cp)
        for k, (dx, dy, dc) in enumerate(_OFFSETS):
            peer = 4 * _flip(x, dx) + 2 * _flip(y, dy) + _flip(z, dc)
            pltpu.make_async_remote_copy(
                src_ref=a_ref, dst_ref=o_ref.at[peer], send_sem=send.at[k], recv_sem=recv.at[k],
                device_id=(_flip(x, dx), _flip(y, dy), _flip(z, dc)), device_id_type=MESH).wait_recv()
        for cp in copies:
            cp.wait_send()
        mine.wait()

    return pl.pallas_call(
        body, name=name, in_specs=[_VMEM], out_specs=_VMEM,
        out_shape=jax.ShapeDtypeStruct((N_DEV, r, c), a.dtype),
        scratch_shapes=[pltpu.SemaphoreType.DMA((7,)), pltpu.SemaphoreType.DMA((7,)), pltpu.SemaphoreType.DMA],
    )(a)


def _gather_send(o_refs, send, recv):
    x, y, z = _place()
    chip = 2 * x + y
    for a, o in enumerate(o_refs):
        r2 = o.shape[1] // 2
        mine = o.at[chip, pl.ds(z * r2, r2)]
        for k, (dx, dy) in enumerate(_CHIP_OFFSETS):
            pltpu.make_async_remote_copy(
                src_ref=mine, dst_ref=mine, send_sem=send.at[a, k], recv_sem=recv.at[a, k],
                device_id=(_flip(x, dx), _flip(y, dy), z), device_id_type=MESH).start()


def _gather_finish(o_refs, send, recv, fsend, frecv):
    x, y, z = _place()
    chip = 2 * x + y
    sib = (x, y, 1 - z)
    passed = []
    for a, o in enumerate(o_refs):
        r2 = o.shape[1] // 2
        for k, (dx, dy) in enumerate(_CHIP_OFFSETS):
            other = 2 * _flip(x, dx) + _flip(y, dy)
            landed = o.at[other, pl.ds(z * r2, r2)]
            pltpu.make_async_remote_copy(
                src_ref=landed, dst_ref=landed, send_sem=send.at[a, k], recv_sem=recv.at[a, k],
                device_id=(_flip(x, dx), _flip(y, dy), z), device_id_type=MESH).wait_recv()
            cp = pltpu.make_async_remote_copy(
                src_ref=landed, dst_ref=landed, send_sem=fsend.at[a, k], recv_sem=frecv.at[a, k],
                device_id=sib, device_id_type=MESH)
            cp.start()
            passed.append(cp)
    for a, o in enumerate(o_refs):
        r2 = o.shape[1] // 2
        mine = o.at[chip, pl.ds(z * r2, r2)]
        for k, (dx, dy) in enumerate(_CHIP_OFFSETS):
            other = 2 * _flip(x, dx) + _flip(y, dy)
            got = o.at[other, pl.ds((1 - z) * r2, r2)]
            pltpu.make_async_remote_copy(
                src_ref=got, dst_ref=got, send_sem=fsend.at[a, k], recv_sem=frecv.at[a, k],
                device_id=sib, device_id_type=MESH).wait_recv()
            pltpu.make_async_remote_copy(
                src_ref=mine, dst_ref=mine, send_sem=send.at[a, k], recv_sem=recv.at[a, k],
                device_id=(_flip(x, dx), _flip(y, dy), z), device_id_type=MESH).wait_send()
    for cp in passed:
        cp.wait_send()


def _gather_sems(n):
    return [pltpu.SemaphoreType.DMA((n, 3))] * 4


def _allgather_weights(ws, name):
    n = len(ws)

    def body(*refs):
        o_refs = refs[n:2 * n]
        send, recv, fsend, frecv = refs[2 * n:]
        _gather_send(o_refs, send, recv)
        _gather_finish(o_refs, send, recv, fsend, frecv)

    return pl.pallas_call(
        body, name=name, in_specs=[_ANY] * n, out_specs=[_ANY] * n,
        out_shape=[jax.ShapeDtypeStruct(w.shape, w.dtype) for w in ws],
        input_output_aliases={a: a for a in range(n)},
        scratch_shapes=_gather_sems(n),
    )(*ws)


def _rs_exchange(gs, name):
    n = len(gs)

    def body(*refs):
        g_refs, r_refs = refs[:n], refs[n:2 * n]
        send, recv = refs[2 * n:]
        x, y, z = _place()
        copies = []
        for a in range(n):
            r2 = g_refs[a].shape[1] // 2
            cp = pltpu.make_async_remote_copy(
                src_ref=g_refs[a].at[:, pl.ds((1 - z) * r2, r2)], dst_ref=r_refs[a], send_sem=send.at[a],
                recv_sem=recv.at[a], device_id=(x, y, 1 - z), device_id_type=MESH)
            cp.start()
            copies.append(cp)
        for cp in copies:
            cp.wait()

    return pl.pallas_call(
        body, name=name, in_specs=[_ANY] * n, out_specs=[_ANY] * n,
        out_shape=[jax.ShapeDtypeStruct((g.shape[0], g.shape[1] // 2, g.shape[2]), g.dtype) for g in gs],
        scratch_shapes=[pltpu.SemaphoreType.DMA((n,))] * 2,
    )(*gs)


def _add_half(g, recv, core, name):
    s, r, c = g.shape
    r2 = r // 2
    rb = r2
    for cand in (256, 128, 64):
        if r2 % cand == 0:
            rb = cand
            break
    g4 = g.reshape(s, 2, r2, c)

    def body(core_ref, g_ref, r_ref, o_ref):
        o_ref[...] = (g_ref[...] + r_ref[...]).astype(BF16)

    return pl.pallas_call(
        body, name=name,
        grid_spec=pltpu.PrefetchScalarGridSpec(
            num_scalar_prefetch=1, grid=(s, r2 // rb),
            in_specs=[pl.BlockSpec((None, None, rb, c), lambda i, j, cr: (i, cr[0], j, 0)),
                      pl.BlockSpec((None, rb, c), lambda i, j, cr: (i, j, 0))],
            out_specs=pl.BlockSpec((None, rb, c), lambda i, j, cr: (i, j, 0))),
        out_shape=jax.ShapeDtypeStruct((s, r2, c), BF16),
        compiler_params=_params(("parallel", "parallel")),
    )(core, g4, recv)


def _scatter_copies(p_refs, o_refs, send, recv):
    x, y, z = _place()
    copies = []
    for a, (p, o) in enumerate(zip(p_refs, o_refs)):
        for k, (dx, dy) in enumerate(_CHIP_OFFSETS):
            other = 2 * _flip(x, dx) + _flip(y, dy)
            copies.append(pltpu.make_async_remote_copy(
                src_ref=p.at[other], dst_ref=o.at[k], send_sem=send.at[a, k], recv_sem=recv.at[a, k],
                device_id=(_flip(x, dx), _flip(y, dy), z), device_id_type=MESH))
    return copies


def _rs_scatter(ps, name):
    n = len(ps)

    def body(*refs):
        p_refs, o_refs = refs[:n], refs[n:2 * n]
        send, recv = refs[2 * n:]
        copies = _scatter_copies(p_refs, o_refs, send, recv)
        for cp in copies:
            cp.start()
        for cp in copies:
            cp.wait()

    return pl.pallas_call(
        body, name=name, in_specs=[_ANY] * n, out_specs=[_ANY] * n,
        out_shape=[jax.ShapeDtypeStruct((3,) + p.shape[1:], p.dtype) for p in ps],
        scratch_shapes=[pltpu.SemaphoreType.DMA((n, 3))] * 2,
    )(*ps)


def _sum_chips(p, landed, chip, name):
    _, r2, c = p.shape
    rb = r2
    for cand in (256, 128, 64):
        if r2 % cand == 0:
            rb = cand
            break

    def body(s_ref, p_ref, l_ref, o_ref):
        acc = p_ref[...].astype(F32)
        for k in range(3):
            acc = acc + l_ref[k].astype(F32)
        o_ref[...] = acc

    return pl.pallas_call(
        body, name=name,
        grid_spec=pltpu.PrefetchScalarGridSpec(
            num_scalar_prefetch=1, grid=(r2 // rb,),
            in_specs=[pl.BlockSpec((None, rb, c), lambda i, s: (s[0], i, 0)),
                      pl.BlockSpec((3, rb, c), lambda i, s: (0, i, 0))],
            out_specs=pl.BlockSpec((rb, c), lambda i, s: (i, 0))),
        out_shape=jax.ShapeDtypeStruct((r2, c), F32),
        compiler_params=_params(("parallel",)),
    )(chip, p, landed)


def _rs_swap(hs, name):
    n = len(hs)

    def body(*refs):
        h_refs, o_refs = refs[:n], refs[n:2 * n]
        send, recv = refs[2 * n:]
        x, y, z = _place()
        copies = [pltpu.make_async_remote_copy(
            src_ref=h, dst_ref=o, send_sem=send.at[a], recv_sem=recv.at[a], device_id=(x, y, 1 - z),
            device_id_type=MESH) for a, (h, o) in enumerate(zip(h_refs, o_refs))]
        for cp in copies:
            cp.start()
        for cp in copies:
            cp.wait()

    return pl.pallas_call(
        body, name=name, in_specs=[_ANY] * n, out_specs=[_ANY] * n,
        out_shape=[jax.ShapeDtypeStruct(h.shape, h.dtype) for h in hs],
        scratch_shapes=[pltpu.SemaphoreType.DMA((n,))] * 2,
    )(*hs)


SMALL_ROWS = 32
PACK_ROWS = 16


def _pack_small(st_post, st_ret, st_pre):
    d = st_post.shape[2]

    def body(po_ref, re_ref, pr_ref, o_ref):
        o_ref[...] = jnp.zeros(o_ref.shape, F32)
        o_ref[0:1, :] = pr_ref[0, 2:3, :] + pr_ref[1, 2:3, :] + pr_ref[2, 2:3, :]
        o_ref[1:2, :] = po_ref[0, 4:5, :] + po_ref[1, 4:5, :]
        o_ref[2:3, :] = po_ref[0, 5:6, :] + po_ref[1, 5:6, :]
        o_ref[3:4, 0:512] = re_ref[0, 0:1, :] + re_ref[1, 0:1, :]
        o_ref[4:5, :] = pr_ref[0, 3:4, :] + pr_ref[1, 3:4, :] + pr_ref[2, 3:4, :]
        o_ref[5:6, :] = pr_ref[0, 4:5, :] + pr_ref[1, 4:5, :] + pr_ref[2, 4:5, :]
        lane = lax.broadcasted_iota(jnp.int32, (1, LANES), 1)
        for row, src in ((6, 1), (10, 2)):
            acc = jnp.zeros((1, LANES), F32)
            for hd in range(HEADS):
                grp = re_ref[0, src:src + 1, hd * LANES:(hd + 1) * LANES] + re_ref[1, src:src + 1, hd * LANES:(hd + 1) * LANES]
                acc = acc + jnp.where(lane == hd, grp, 0.0)
            o_ref[row:row + 1, 0:LANES] = acc
        o_ref[7:8, :] = po_ref[0, 6:7, :] + po_ref[1, 6:7, :]
        o_ref[8:9, :] = pr_ref[2, 0:1, :]
        o_ref[9:10, :] = pr_ref[2, 1:2, :]
        for e in range(2):
            b = 12 + 6 * e
            o_ref[b:b + 1, :] = pr_ref[e, 0:1, :]
            o_ref[b + 1:b + 2, :] = pr_ref[e, 1:2, :]
            o_ref[b + 2:b + 3, :] = po_ref[e, 3:4, :]
            o_ref[b + 3:b + 4, :] = po_ref[e, 0:1, :]
            o_ref[b + 4:b + 5, :] = po_ref[e, 1:2, :]
            o_ref[b + 5:b + 6, :] = po_ref[e, 2:3, :]

    return pl.pallas_call(body, name="pack_small", out_shape=jax.ShapeDtypeStruct((SMALL_ROWS, d), F32))(st_post, st_ret, st_pre)


def _small_reduce(gathered):
    d = gathered.shape[2]

    def body(g_ref, o_ref):
        tot = g_ref[0, 0:PACK_ROWS, :]
        for dev in range(1, N_DEV):
            tot = tot + g_ref[dev, 0:PACK_ROWS, :]
        o_ref[0:PACK_ROWS, :] = tot
        for j in range(6):
            acc = g_ref[0, 12 + j:13 + j, :] + g_ref[0, 18 + j:19 + j, :]
            for dev in range(1, N_DEV):
                acc = acc + g_ref[dev, 12 + j:13 + j, :] + g_ref[dev, 18 + j:19 + j, :]
            if j < 2:
                acc = acc + o_ref[8 + j:9 + j, :]
            o_ref[PACK_ROWS + j:PACK_ROWS + j + 1, :] = acc
        o_ref[PACK_ROWS + 6:PACK_ROWS + 8, :] = jnp.zeros((2, d), F32)

    return pl.pallas_call(body, name="small_reduce", out_shape=jax.ShapeDtypeStruct((PACK_ROWS + 8, d), F32))(gathered)


_SMALL = (("g_attn", 0, 1024), ("g_ffn", 1, 1024), ("g_final", 2, 1024), ("g_ret", 3, 512), ("g_q_lora", 4, 384),
          ("g_kv_lora", 5, 256), ("ret_decay_fwd", 6, HEADS), ("ret_decay_bwd", 10, HEADS))
_SMALL_NAMES = tuple(s[0] for s in _SMALL) + ("c_ctx", "b_ada")


def _small_final(tot, dcc, sg8, ws, ms, vs):
    d = tot.shape[1]
    n = len(_SMALL_NAMES)

    def body(*refs):
        t_ref, dcc_ref, sg_ref = refs[0:3]
        w_refs, m_refs, v_refs = refs[3:3 + n], refs[3 + n:3 + 2 * n], refs[3 + 2 * n:3 + 3 * n]
        outs = refs[3 + 3 * n:]
        g_refs, d_refs, mo_refs, vo_refs = outs[0:n], outs[n:2 * n], outs[2 * n:3 * n], outs[3 * n:4 * n]
        l_ref = outs[4 * n]

        def update(i, g, sl=None):
            pick = (lambda r: r[...]) if sl is None else (lambda r: r[:, sl])
            dl, mn, vn = _adam_math(pick(w_refs[i]), g, pick(m_refs[i]), pick(v_refs[i]))
            if sl is None:
                g_refs[i][...], d_refs[i][...], mo_refs[i][...], vo_refs[i][...] = g, dl, mn, vn
            else:
                g_refs[i][:, sl], d_refs[i][:, sl], mo_refs[i][:, sl], vo_refs[i][:, sl] = g, dl, mn, vn

        for i, (name, row, width) in enumerate(_SMALL):
            g = t_ref[row:row + 1, 0:width]
            if name == "ret_decay_fwd":
                g = g * sg_ref[0:1, 0:width]
            elif name == "ret_decay_bwd":
                g = g * sg_ref[1:2, 0:width]
            update(i, g)
        i_cc, i_b = n - 2, n - 1
        cc = w_refs[i_cc][...]
        s = 1.0 / (1.0 + jnp.exp(-cc))
        dsilu = dcc_ref[0, 0:1, :] + dcc_ref[2, 0:1, :] + dcc_ref[4, 0:1, :] + dcc_ref[6, 0:1, :]
        update(i_cc, dsilu * (s * (1.0 + cc * (1.0 - s))))
        for j in range(6):
            update(i_b, t_ref[PACK_ROWS + j:PACK_ROWS + j + 1, :], pl.ds(j * d, d))
        l_ref[...] = jnp.broadcast_to((0.5 / d) * jnp.sum(t_ref[7:8, :], keepdims=True), l_ref.shape)

    shapes = [jax.ShapeDtypeStruct(a.shape, F32) for a in ws]
    outs = pl.pallas_call(
        body, name="small_final", out_shape=shapes * 4 + [jax.ShapeDtypeStruct((8, LANES), F32)],
    )(tot, dcc, sg8, *ws, *ms, *vs)
    return outs[0:n], outs[n:2 * n], outs[2 * n:3 * n], outs[3 * n:4 * n], outs[4 * n]


_WEIGHTS = ("c_ctx", "w_ada", "b_ada", "g_attn", "g_ffn", "w_in", "ret_decay_fwd", "ret_decay_bwd", "g_ret", "g_q_lora",
            "w_uq", "g_kv_lora", "w_ukv", "w_out", "w_ff1", "w_ff2", "g_final")
_BIG = ("w_in", "w_uq", "w_ukv", "w_out", "w_ff1", "w_ff2")


def kernel(x, c, ctx, c_ctx, w_ada, b_ada, g_attn, g_ffn, w_in, ret_decay_fwd, ret_decay_bwd, g_ret, g_q_lora, w_uq, g_kv_lora, w_ukv, w_out, w_ff1, w_ff2, g_final, loss_target, m_c_ctx, m_w_ada, m_b_ada, m_g_attn, m_g_ffn, m_w_in, m_ret_decay_fwd, m_ret_decay_bwd, m_g_ret, m_g_q_lora, m_w_uq, m_g_kv_lora, m_w_ukv, m_w_out, m_w_ff1, m_w_ff2, m_g_final, v_c_ctx, v_w_ada, v_b_ada, v_g_attn, v_g_ffn, v_w_in, v_ret_decay_fwd, v_ret_decay_bwd, v_g_ret, v_g_q_lora, v_w_uq, v_g_kv_lora, v_w_ukv, v_w_out, v_w_ff1, v_w_ff2, v_g_final):
    w = dict(c_ctx=c_ctx, w_ada=w_ada, b_ada=b_ada, g_attn=g_attn, g_ffn=g_ffn, w_in=w_in, ret_decay_fwd=ret_decay_fwd,
             ret_decay_bwd=ret_decay_bwd, g_ret=g_ret, g_q_lora=g_q_lora, w_uq=w_uq, g_kv_lora=g_kv_lora, w_ukv=w_ukv,
             w_out=w_out, w_ff1=w_ff1, w_ff2=w_ff2, g_final=g_final)
    m = dict(c_ctx=m_c_ctx, w_ada=m_w_ada, b_ada=m_b_ada, g_attn=m_g_attn, g_ffn=m_g_ffn, w_in=m_w_in,
             ret_decay_fwd=m_ret_decay_fwd, ret_decay_bwd=m_ret_decay_bwd, g_ret=m_g_ret, g_q_lora=m_g_q_lora, w_uq=m_w_uq,
             g_kv_lora=m_g_kv_lora, w_ukv=m_w_ukv, w_out=m_w_out, w_ff1=m_w_ff1, w_ff2=m_w_ff2, g_final=m_g_final)
    v = dict(c_ctx=v_c_ctx, w_ada=v_w_ada, b_ada=v_b_ada, g_attn=v_g_attn, g_ffn=v_g_ffn, w_in=v_w_in,
             ret_decay_fwd=v_ret_decay_fwd, ret_decay_bwd=v_ret_decay_bwd, g_ret=v_g_ret, g_q_lora=v_g_q_lora, w_uq=v_w_uq,
             g_kv_lora=v_g_kv_lora, w_ukv=v_w_ukv, w_out=v_w_out, w_ff1=v_w_ff1, w_ff2=v_w_ff2, g_final=v_g_final)
    xi, yi, ci = lax.axis_index("x"), lax.axis_index("y"), lax.axis_index("c")
    chip = 2 * xi + yi
    dev = 2 * chip + ci
    nex, seq, d = x.shape
    n_ada = w_ada.shape[2]

    c_all = _allgather8(jnp.pad(c, ((0, 8 - nex), (0, 0))), "ag_c")[:, :nex].reshape(N_DEV * nex, d)
    a_in = jnp.concatenate([c_all, c_ctx.reshape(1, d), jnp.zeros((7, d), F32)], axis=0)
    b_sh = lax.dynamic_slice(b_ada, (0, chip * n_ada), (1, n_ada))
    mod_sh = _mod_fwd(a_in, w_ada[0], b_sh)
    mod_all = _allgather8(mod_sh, "ag_mod")[0::2].transpose(1, 0, 2).reshape(a_in.shape[0], N_CHIPS * n_ada)
    mod_me = lax.dynamic_slice(mod_all, (nex * dev, 0), (nex, N_CHIPS * n_ada)).reshape(nex, 6, d)
    mod_c = mod_all[N_DEV * nex].reshape(1, 6, d)
    modv = jnp.pad(jnp.concatenate([mod_me, mod_c], axis=0), ((0, 0), (0, 2), (0, 0)))

    dec = jnp.zeros((8, LANES), F32).at[0, :HEADS].set(ret_decay_fwd[0]).at[1, :HEADS].set(ret_decay_bwd[0])
    lg8, sg8 = _decay_prep(dec)
    lg = lg8[:2, :HEADS]

    shard = {k: w[k][0] for k in _BIG}
    shard["w_uq"] = jnp.pad(shard["w_uq"], ((0, 0), (0, MLA_HEAD - MLA_NOPE - MLA_ROPE)))
    slot = chip.reshape(1).astype(jnp.int32)
    core = ci.reshape(1).astype(jnp.int32)
    full = dict(zip(_BIG, _allgather_weights([_cast_into_slot(shard[k], slot, "cast_" + k) for k in _BIG], "ag_weights")))
    w_in_k = jnp.pad(full["w_in"].transpose(1, 0, 2).reshape(d, IN_COLS), ((0, 0), (0, IN_PAD - IN_COLS)))
    w_uq_k = full["w_uq"]
    w_out_k = full["w_out"].reshape(d, d)
    w_ff2_k = full["w_ff2"].reshape(D_FF, d)

    gx, gw, st_post, st_ret, st_pre = _local_step(
        x, ctx, loss_target, modv, lg, g_attn, g_ffn, g_final.reshape(1, d), g_ret, g_q_lora, g_kv_lora,
        w_in_k, w_uq_k, full["w_ukv"], w_out_k, full["w_ff1"], w_ff2_k)

    per_chip = IN_COLS // N_CHIPS
    g4 = [
        gw["w_in"][:, :IN_COLS].reshape(d, N_CHIPS, per_chip).transpose(1, 0, 2),
        gw["w_uq"][:, :, :MLA_NOPE + MLA_ROPE],
        gw["w_ukv"],
        gw["w_out"].reshape(N_CHIPS, d // N_CHIPS, d),
        gw["w_ff1"],
        gw["w_ff2"].reshape(N_CHIPS, D_FF // N_CHIPS, d),
    ]
    got = _rs_exchange(g4, "rs_exchange")
    partial = [_add_half(g, r, core, "add_half_" + k) for g, r, k in zip(g4, got, _BIG)]
    landed = _rs_scatter(partial, "rs_scatter")
    mine = [_sum_chips(p, l, slot, "sum_chips_" + k) for p, l, k in zip(partial, landed, _BIG)]
    theirs = _rs_swap(mine, "rs_swap")
    grad, delta, new_m, new_v = {}, {}, {}, {}
    for k, a, b in zip(_BIG, mine, theirs):
        shp = w[k].shape
        outs = _adamw_halves(w[k].reshape(shp[1:]), a, b, m[k].reshape(shp[1:]), v[k].reshape(shp[1:]), core, "adamw_" + k)
        grad[k], delta[k], new_m[k], new_v[k] = [o.reshape(shp) for o in outs]

    gathered = _allgather8(_pack_small(st_post, st_ret, st_pre), "ag_small")
    tot = _small_reduce(gathered)
    dm = jnp.concatenate([
        gathered[:, 12:24].reshape(N_DEV * nex, 6 * d),
        jnp.concatenate([tot[8:10].reshape(1, 2 * d), jnp.zeros((1, 4 * d), F32)], axis=1),
        jnp.zeros((7, 6 * d), F32)], axis=0)
    dm_sh = lax.dynamic_slice(dm, (0, chip * n_ada), (dm.shape[0], n_ada))
    g_ada, da = _mod_bwd(a_in, dm_sh, w_ada[0])
    dcc = _allgather8(da[N_DEV * nex:], "ag_dcc")
    shp = w_ada.shape
    outs = _adamw(w_ada[0], g_ada, m["w_ada"][0], v["w_ada"][0], "adamw_w_ada")
    grad["w_ada"] = g_ada.reshape(shp)
    delta["w_ada"], new_m["w_ada"], new_v["w_ada"] = [o.reshape(shp) for o in outs]
    rows = [{k: t[k].reshape(1, -1) for k in _SMALL_NAMES} for t in (w, m, v)]
    small = _small_final(tot, dcc, sg8, *[[t[k] for k in _SMALL_NAMES] for t in rows])
    for res, outs in zip((grad, delta, new_m, new_v), small[:4]):
        for k, o in zip(_SMALL_NAMES, outs):
            res[k] = o.reshape(w[k].shape)
    return (small[4][0, 0], gx, *[grad[k] for k in _WEIGHTS], *[delta[k] for k in _WEIGHTS],
            *[new_m[k] for k in _WEIGHTS], *[new_v[k] for k in _WEIGHTS])
```

```python
import functools
import math

import jax
import jax.numpy as jnp
from jax import lax
from jax.experimental import pallas as pl
from jax.experimental.pallas import tpu as pltpu

F32 = jnp.float32
BF16 = jnp.bfloat16
MESH = pl.DeviceIdType.MESH

EPS = 1e-6
D_MODEL = 1024
D_FF = 4096
HEADS = 4
RET_DK = 64
RET_DV = 128
MLA_NOPE = 128
MLA_ROPE = 64
MLA_HEAD = 256
Q_LORA = 384
KV_LORA = 256
GRID_W = 64
ROPE_BASE = 10000.0
IN_COLS = 2240
IN_PAD = 2304
PG_COLS = 1152
N_CHIPS = 4
N_DEV = 8
LANES = 128
ADAM_LR = 0.001
ADAM_B1 = 0.9
ADAM_B2 = 0.999
ADAM_EPS = 1e-08
ADAM_WD = 0.01
ADAM_STEP = 10
VMEM_LIMIT = 56 * 1024 * 1024


def _dot(a, b):
    return jnp.dot(a, b, preferred_element_type=F32)


def _dot_nt(a, b):
    return lax.dot_general(a, b, (((1,), (1,)), ((), ())), preferred_element_type=F32)


def _dot_tn(a, b):
    return lax.dot_general(a, b, (((0,), (0,)), ((), ())), preferred_element_type=F32)


def _params(sem=None, vmem=None):
    return pltpu.CompilerParams(dimension_semantics=sem, vmem_limit_bytes=vmem)


def _full(shape):
    n = len(shape)
    return pl.BlockSpec(shape, lambda *_: (0,) * n)


def _rope(x, cos, sin):
    w = x.shape[-1]
    lo = (lax.broadcasted_iota(jnp.int32, (1, w), 1) % 64) < 32
    swapped = jnp.where(lo, pltpu.roll(x, w - 32, 1), pltpu.roll(x, 32, 1))
    return x * cos + swapped * sin


def _rope_t(g, cos, sin):
    w = g.shape[-1]
    lo = (lax.broadcasted_iota(jnp.int32, (1, w), 1) % 64) < 32
    t = g * sin
    swapped = jnp.where(lo, pltpu.roll(t, w - 32, 1), pltpu.roll(t, 32, 1))
    return g * cos + swapped


def _rope_tables(seq, tm):
    rows = seq // GRID_W
    row = jnp.repeat(jnp.arange(rows, dtype=F32), GRID_W)
    col = jnp.tile(jnp.arange(GRID_W, dtype=F32), rows)
    n_freq = RET_DK // 4
    freq = ROPE_BASE ** (-jnp.arange(n_freq, dtype=F32) / n_freq)
    ang = jnp.concatenate([row[:, None] * freq, col[:, None] * freq], axis=-1)
    cos, sin = jnp.cos(ang), jnp.sin(ang)
    cos_t = jnp.tile(jnp.concatenate([cos, cos], -1), (1, HEADS))
    sin_t = jnp.tile(jnp.concatenate([-sin, sin], -1), (1, HEADS))
    cos_t = jnp.concatenate([cos_t, jnp.ones((tm, 4 * RET_DK), F32)], 0)
    sin_t = jnp.concatenate([sin_t, jnp.zeros((tm, 4 * RET_DK), F32)], 0)
    return cos_t, sin_t


def _adam_math(w, g, m, v):
    mn = ADAM_B1 * m + (1.0 - ADAM_B1) * g
    vn = ADAM_B2 * v + (1.0 - ADAM_B2) * (g * g)
    m_hat = mn / (1.0 - ADAM_B1 ** ADAM_STEP)
    v_hat = vn / (1.0 - ADAM_B2 ** ADAM_STEP)
    return -ADAM_LR * (m_hat / (jnp.sqrt(v_hat) + ADAM_EPS) + ADAM_WD * w), mn, vn


def _cast_into_slot(w, slot, name):
    r, c = w.shape
    rb = math.gcd(r, 256)

    def body(s_ref, w_ref, o_ref):
        o_ref[...] = w_ref[...].astype(BF16)

    return pl.pallas_call(
        body, name=name,
        grid_spec=pltpu.PrefetchScalarGridSpec(
            num_scalar_prefetch=1, grid=(r // rb,),
            in_specs=[pl.BlockSpec((rb, c), lambda i, s: (i, 0))],
            out_specs=pl.BlockSpec((None, rb, c), lambda i, s: (s[0], i, 0))),
        out_shape=jax.ShapeDtypeStruct((N_CHIPS, r, c), BF16),
        compiler_params=_params(("parallel",)),
    )(slot, w)


def _adamw_halves(w, mine, theirs, m, v, core, name):
    r, c = w.shape
    r2 = r // 2
    rb = 8
    for cand in (256, 128, 64, 32, 16):
        if r2 % cand == 0 and cand * c * 4 <= (1 << 20):
            rb = cand
            break
    nbh = r2 // rb

    def body(z_ref, w_ref, a_ref, b_ref, m_ref, v_ref, g_ref, d_ref, mo_ref, vo_ref):
        here = (pl.program_id(0) // nbh) == z_ref[0]
        gg = jnp.where(here, a_ref[...], b_ref[...])
        g_ref[...] = gg
        d_ref[...], mo_ref[...], vo_ref[...] = _adam_math(w_ref[...], gg, m_ref[...], v_ref[...])

    spec = pl.BlockSpec((rb, c), lambda i, z: (i, 0))
    a_spec = pl.BlockSpec((rb, c), lambda i, z: (jnp.clip(i - z[0] * nbh, 0, nbh - 1), 0))
    b_spec = pl.BlockSpec((rb, c), lambda i, z: (jnp.clip(i - (1 - z[0]) * nbh, 0, nbh - 1), 0))
    shp = jax.ShapeDtypeStruct((r, c), F32)
    return pl.pallas_call(
        body, name=name,
        grid_spec=pltpu.PrefetchScalarGridSpec(
            num_scalar_prefetch=1, grid=(r // rb,), in_specs=[spec, a_spec, b_spec, spec, spec], out_specs=[spec] * 4),
        out_shape=[shp] * 4,
        compiler_params=_params(("parallel",)),
    )(core, w, mine, theirs, m, v)


def _adamw(w, g, m, v, name):
    r, c = w.shape
    rb = r
    for cand in (256, 128, 64, 32, 16, 8):
        if r % cand == 0 and cand * c * 4 <= (1 << 20):
            rb = cand
            break
    if r * c * 4 <= (1 << 20):
        rb = r

    def body(w_ref, g_ref, m_ref, v_ref, d_ref, mo_ref, vo_ref):
        d_ref[...], mo_ref[...], vo_ref[...] = _adam_math(w_ref[...], g_ref[...], m_ref[...], v_ref[...])

    spec = pl.BlockSpec((rb, c), lambda i: (i, 0))
    shp = jax.ShapeDtypeStruct((r, c), F32)
    return pl.pallas_call(
        body, name=name, grid=(r // rb,), in_specs=[spec] * 4, out_specs=[spec] * 3, out_shape=[shp] * 3,
        compiler_params=_params(("parallel",)),
    )(w, g, m, v)


def _decay_prep(dec):
    def body(d_ref, lg_ref, sg_ref):
        d = d_ref[...]
        lg_ref[...] = jnp.minimum(d, 0.0) - jnp.log(1.0 + jnp.exp(-jnp.abs(d)))
        sg_ref[...] = 1.0 / (1.0 + jnp.exp(d))

    shp = jax.ShapeDtypeStruct(dec.shape, F32)
    return pl.pallas_call(body, name="decay_prep", out_shape=[shp, shp])(dec)


def _mod_fwd(a_in, w_ada, b_sh):
    rows, d = a_in.shape
    n = w_ada.shape[1]
    bn = 512

    def body(a_ref, w_ref, b_ref, o_ref):
        a = a_ref[...]
        s = (a / (1.0 + jnp.exp(-a))).astype(BF16)
        o_ref[...] = _dot(s, w_ref[...].astype(BF16)) + b_ref[...]

    return pl.pallas_call(
        body, name="mod_fwd", grid=(n // bn,),
        in_specs=[_full((rows, d)), pl.BlockSpec((d, bn), lambda j: (0, j)), pl.BlockSpec((1, bn), lambda j: (0, j))],
        out_specs=pl.BlockSpec((rows, bn), lambda j: (0, j)),
        out_shape=jax.ShapeDtypeStruct((rows, n), F32),
        compiler_params=_params(("parallel",)),
    )(a_in, w_ada, b_sh)


def _mod_bwd(a_in, dm, w_ada):
    rows, d = a_in.shape
    n = w_ada.shape[1]
    bn = 512
    nb = n // bn

    def body(a_ref, dm_ref, w_ref, gw_ref, da_ref):
        j = pl.program_id(0)
        a = a_ref[...]
        s = (a / (1.0 + jnp.exp(-a))).astype(BF16)
        dmb = dm_ref[...].astype(BF16)
        gw_ref[...] = _dot_tn(s, dmb)
        part = _dot_nt(dmb, w_ref[...].astype(BF16))

        @pl.when(j == 0)
        def _():
            da_ref[...] = part

        @pl.when(j > 0)
        def _():
            da_ref[...] += part

    return pl.pallas_call(
        body, name="mod_bwd", grid=(nb,),
        in_specs=[_full((rows, d)), pl.BlockSpec((rows, bn), lambda j: (0, j)), pl.BlockSpec((d, bn), lambda j: (0, j))],
        out_specs=[pl.BlockSpec((d, bn), lambda j: (0, j)), _full((rows, d))],
        out_shape=[jax.ShapeDtypeStruct((d, n), F32), jax.ShapeDtypeStruct((rows, d), F32)],
        compiler_params=_params(("arbitrary",)),
    )(a_in, dm, w_ada)


def _pre_fwd(x2, ctx2, modv, g_attn, w_in, g_q, g_kv, w_uq, w_ukv, cos_t, sin_t, *, seq, tm):
    t_lat, d = x2.shape
    t_ctx = ctx2.shape[0]
    nl, nc = t_lat // tm, t_ctx // tm
    n_all = t_lat + t_ctx
    tpe = seq // tm
    nex = t_lat // seq

    def body(x_ref, c_ref, mod_ref, g_ref, win_ref, gq_ref, gkv_ref, wuq_ref, wukv_ref, cos_ref, sin_ref,
             h_ref, pg_ref, rq_ref, rk_ref, rv_ref, nq_ref, nkv_ref, q_ref, k_ref, v_ref):
        i = pl.program_id(0)
        xt = jnp.where(i < nl, x_ref[...], c_ref[...])
        sh = mod_ref[0, 0:1, :]
        sc = mod_ref[0, 1:2, :]
        r = lax.rsqrt(jnp.mean(xt * xt, axis=-1, keepdims=True) + EPS)
        hb = ((xt * r) * g_ref[...] * (1.0 + sc) + sh).astype(BF16)
        h_ref[...] = hb
        p = _dot(hb, win_ref[...])
        cos = cos_ref[...]
        sin = sin_ref[...]
        rq_ref[...] = _rope(p[:, 0:256], cos, sin).astype(BF16)
        rk_ref[...] = _rope(p[:, 256:512] * (RET_DK ** -0.5), cos, sin).astype(BF16)
        rv_ref[...] = p[:, 512:1024].astype(BF16)
        pg_ref[...] = p[:, 1024:2176]
        cq = p[:, 1536:1920]
        ckv = p[:, 1920:2176]
        nqb = (cq * lax.rsqrt(jnp.mean(cq * cq, axis=-1, keepdims=True) + EPS) * gq_ref[...]).astype(BF16)
        nkvb = (ckv * lax.rsqrt(jnp.mean(ckv * ckv, axis=-1, keepdims=True) + EPS) * gkv_ref[...]).astype(BF16)
        nq_ref[...] = nqb
        nkv_ref[...] = nkvb
        cos1 = cos[:, 0:LANES]
        sin1 = sin[:, 0:LANES]
        kpe = _rope(p[:, 2176:2304], cos1, sin1).astype(BF16)
        for hd in range(HEADS):
            o = hd * MLA_HEAD
            qh = _dot(nqb, wuq_ref[hd])
            q_ref[:, o:o + 128] = qh[:, 0:128].astype(BF16)
            q_ref[:, o + 128:o + 256] = _rope(qh[:, 128:256], cos1, sin1).astype(BF16)
            kvh = _dot(nkvb, wukv_ref[hd])
            k_ref[:, o:o + 128] = kvh[:, 0:128].astype(BF16)
            k_ref[:, o + 128:o + 256] = kpe
            v_ref[:, hd * 128:(hd + 1) * 128] = kvh[:, 128:256].astype(BF16)

    def tile(width):
        return pl.BlockSpec((tm, width), lambda i: (i, 0))

    widths = (d, PG_COLS, 256, 256, 512, Q_LORA, KV_LORA, HEADS * MLA_HEAD, HEADS * MLA_HEAD, HEADS * 128)
    dtypes = (BF16, F32, BF16, BF16, BF16, BF16, BF16, BF16, BF16, BF16)
    tab = pl.BlockSpec((tm, 256), lambda i: (jnp.where(i < nl, i % tpe, tpe), 0))
    return pl.pallas_call(
        body, name="pre_fwd", grid=(nl + nc,),
        in_specs=[
            pl.BlockSpec((tm, d), lambda i: (jnp.minimum(i, nl - 1), 0)),
            pl.BlockSpec((tm, d), lambda i: (jnp.maximum(i - nl, 0), 0)),
            pl.BlockSpec((1, 8, d), lambda i: (jnp.minimum(i // tpe, nex), 0, 0)),
            _full((1, d)), _full(w_in.shape), _full((1, Q_LORA)), _full((1, KV_LORA)),
            _full(w_uq.shape), _full(w_ukv.shape), tab, tab,
        ],
        out_specs=[tile(w) for w in widths],
        out_shape=[jax.ShapeDtypeStruct((n_all, w), dt) for w, dt in zip(widths, dtypes)],
        compiler_params=_params(("parallel",), VMEM_LIMIT),
    )(x2, ctx2, modv, g_attn, w_in, g_q, g_kv, w_uq, w_ukv, cos_t, sin_t)


def _post(yret, ymla, x2, tgt2, modv, g_ffn, g_fin, w_out, w_ff1, w_ff2, *, seq, tm):
    t_lat, d = x2.shape
    nl = t_lat // tm
    tpe = seq // tm
    nex = t_lat // seq
    n_slab = w_ff1.shape[0]
    fs = w_ff1.shape[2]

    def body(yr_ref, ym_ref, x_ref, t_ref, mod_ref, gf_ref, gl_ref, wo_ref, w1_ref, w2_ref,
             mix_ref, a_ref, du_ref, h2_ref, df_ref, dmo_ref, dmix_ref, dxm_ref, st_ref, ru_ref):
        i = pl.program_id(0)
        gt_a = mod_ref[0, 2:3, :]
        sh_f = mod_ref[0, 3:4, :]
        sc_f = mod_ref[0, 4:5, :]
        gt_f = mod_ref[0, 5:6, :]
        g_ffn_v = gf_ref[...]
        g_fin_v = gl_ref[...]
        yr = yr_ref[...]
        ym = ym_ref[...]
        mix_ref[:, 0:512] = yr
        mix_ref[:, 512:1024] = ym
        op = _dot(yr, wo_ref[0:512, :]) + _dot(ym, wo_ref[512:1024, :])
        x_mid = x_ref[...] + gt_a * op
        r2 = lax.rsqrt(jnp.mean(x_mid * x_mid, axis=-1, keepdims=True) + EPS)
        xh2 = x_mid * r2
        h2b = (xh2 * g_ffn_v * (1.0 + sc_f) + sh_f).astype(BF16)
        h2_ref[...] = h2b
        f = jnp.zeros((tm, d), F32)
        for s in range(n_slab):
            ru = jnp.maximum(_dot(h2b, w1_ref[s]), 0.0)
            ru_ref[:, s * fs:(s + 1) * fs] = ru
            ab = (ru * ru).astype(BF16)
            a_ref[:, s * fs:(s + 1) * fs] = ab
            f = f + _dot(ab, w2_ref[s * fs:(s + 1) * fs, :])
        x_out = x_mid + gt_f * f
        r3 = lax.rsqrt(jnp.mean(x_out * x_out, axis=-1, keepdims=True) + EPS)
        xh3 = x_out * r3
        err = xh3 * g_fin_v - t_ref[...]
        dy = err * (1.0 / d)
        dxh3 = dy * g_fin_v
        dx_out = r3 * (dxh3 - xh3 * jnp.mean(dxh3 * xh3, axis=-1, keepdims=True))
        dfb = (dx_out * gt_f).astype(BF16)
        df_ref[...] = dfb
        dh2 = jnp.zeros((tm, d), F32)
        for s in range(n_slab):
            da = _dot_nt(dfb, w2_ref[s * fs:(s + 1) * fs, :])
            dub = (da * (2.0 * ru_ref[:, s * fs:(s + 1) * fs])).astype(BF16)
            du_ref[:, s * fs:(s + 1) * fs] = dub
            dh2 = dh2 + _dot_nt(dub, w1_ref[s])
        dxh2 = dh2 * (1.0 + sc_f) * g_ffn_v
        dx_mid = dx_out + r2 * (dxh2 - xh2 * jnp.mean(dxh2 * xh2, axis=-1, keepdims=True))
        dxm_ref[...] = dx_mid
        dmob = (dx_mid * gt_a).astype(BF16)
        dmo_ref[...] = dmob
        dmix_ref[...] = _dot_nt(dmob, wo_ref[...]).astype(BF16)

        def rsum(v):
            return jnp.sum(v, axis=0, keepdims=True)

        stats = jnp.concatenate([
            rsum(dh2), rsum(dh2 * xh2 * g_ffn_v), rsum(dx_out * f), rsum(dx_mid * op),
            rsum(dh2 * (1.0 + sc_f) * xh2), rsum(dy * xh3), rsum(err * err), jnp.zeros((1, d), F32)], axis=0)

        @pl.when(i % tpe == 0)
        def _():
            st_ref[0] = stats

        @pl.when(i % tpe != 0)
        def _():
            st_ref[0] += stats

    def tile(width):
        return pl.BlockSpec((tm, width), lambda i: (i, 0))

    widths = (d, D_FF, D_FF, d, d, d, d, d)
    dtypes = (BF16, BF16, BF16, BF16, BF16, BF16, BF16, F32)
    const = pl.Buffered(1)
    return pl.pallas_call(
        body, name="post", grid=(nl,),
        in_specs=[
            tile(512), tile(512), tile(d), tile(d),
            pl.BlockSpec((1, 8, d), lambda i: (i // tpe, 0, 0)),
            _full((1, d)), _full((1, d)),
            pl.BlockSpec(w_out.shape, lambda i: (0, 0), pipeline_mode=const),
            pl.BlockSpec(w_ff1.shape, lambda i: (0, 0, 0), pipeline_mode=const),
            pl.BlockSpec(w_ff2.shape, lambda i: (0, 0), pipeline_mode=const),
        ],
        out_specs=[tile(w) for w in widths] + [pl.BlockSpec((1, 8, d), lambda i: (i // tpe, 0, 0))],
        out_shape=[jax.ShapeDtypeStruct((t_lat, w), dt) for w, dt in zip(widths, dtypes)]
        + [jax.ShapeDtypeStruct((nex, 8, d), F32)],
        scratch_shapes=[pltpu.VMEM((tm, D_FF), F32)],
        compiler_params=_params(("arbitrary",), VMEM_LIMIT),
    )(yret, ymla, x2, tgt2, modv, g_ffn, g_fin, w_out, w_ff1, w_ff2)


def _pre_bwd(x2, ctx2, modv, g_attn, pg, drq, drk, dkc_r, drv, dvc_r, drg, dq_m, dkl, dkc, dvl, dvc, dxm,
             w_in, g_q, g_kv, w_uq, w_ukv, cos_t, sin_t, *, seq, tm, rider=None):
    t_lat, d = x2.shape
    t_ctx = ctx2.shape[0]
    nl, nc = t_lat // tm, t_ctx // tm
    n_all = t_lat + t_ctx
    tpe = seq // tm
    nex = t_lat // seq

    def body(x_ref, c_ref, mod_ref, g_ref, pg_ref, drq_ref, drk_ref, dkcr_ref, drv_ref, dvcr_ref, drg_ref,
             dq_ref, dkl_ref, dkc_ref, dvl_ref, dvc_ref, dxm_ref, win_ref, gq_ref, gkv_ref, wuq_ref, wukv_ref,
             cos_ref, sin_ref, dpb_ref, dqf_ref, dkvf_ref, gx_ref, st_ref):
        i = pl.program_id(0)
        lat = i < nl
        latf = lat.astype(F32)
        cos = cos_ref[...]
        sin = sin_ref[...]
        cos1 = cos[:, 0:LANES]
        sin1 = sin[:, 0:LANES]
        d_rq = _rope_t(drq_ref[...] * latf, cos, sin)
        d_rk = _rope_t(jnp.where(lat, drk_ref[...], dkcr_ref[...]), cos, sin) * (RET_DK ** -0.5)
        d_rv = jnp.where(lat, drv_ref[...], dvcr_ref[...])
        d_rg = drg_ref[...] * latf
        dq_all = dq_ref[...] * latf
        dk_all = jnp.where(lat, dkl_ref[...], dkc_ref[...])
        dv_all = jnp.where(lat, dvl_ref[...], dvc_ref[...])
        dnq = jnp.zeros((tm, Q_LORA), F32)
        dnkv = jnp.zeros((tm, KV_LORA), F32)
        dkpe = jnp.zeros((tm, LANES), F32)
        for hd in range(HEADS):
            o = hd * MLA_HEAD
            dqh = jnp.concatenate([dq_all[:, o:o + 128], _rope_t(dq_all[:, o + 128:o + 256], cos1, sin1)],
                                  axis=1).astype(BF16)
            dqf_ref[:, o:o + 256] = dqh
            dnq = dnq + _dot_nt(dqh, wuq_ref[hd])
            dkpe = dkpe + dk_all[:, o + 128:o + 256]
            dkvh = jnp.concatenate([dk_all[:, o:o + 128], dv_all[:, hd * 128:(hd + 1) * 128]], axis=1).astype(BF16)
            dkvf_ref[:, o:o + 256] = dkvh
            dnkv = dnkv + _dot_nt(dkvh, wukv_ref[hd])
        d_kpe = _rope_t(dkpe, cos1, sin1)
        pgv = pg_ref[...]
        cq = pgv[:, 512:896]
        ckv = pgv[:, 896:1152]
        rq_ = lax.rsqrt(jnp.mean(cq * cq, axis=-1, keepdims=True) + EPS)
        cqh = cq * rq_
        dcqh = dnq * gq_ref[...]
        d_cq = rq_ * (dcqh - cqh * jnp.mean(dcqh * cqh, axis=-1, keepdims=True))
        rkv_ = lax.rsqrt(jnp.mean(ckv * ckv, axis=-1, keepdims=True) + EPS)
        ckvh = ckv * rkv_
        dckvh = dnkv * gkv_ref[...]
        d_ckv = rkv_ * (dckvh - ckvh * jnp.mean(dckvh * ckvh, axis=-1, keepdims=True))
        dpb = jnp.concatenate([d_rq, d_rk, d_rv, d_rg, d_cq, d_ckv, d_kpe], axis=1).astype(BF16)
        dpb_ref[...] = dpb
        dh = _dot_nt(dpb, win_ref[...])
        xt = jnp.where(lat, x_ref[...], c_ref[...])
        sc = mod_ref[0, 1:2, :]
        g = g_ref[...]
        r = lax.rsqrt(jnp.mean(xt * xt, axis=-1, keepdims=True) + EPS)
        xh = xt * r
        dxh = dh * (1.0 + sc) * g
        dx = r * (dxh - xh * jnp.mean(dxh * xh, axis=-1, keepdims=True))

        @pl.when(lat)
        def _():
            gx_ref[...] = dxm_ref[...] + dx

        def rsum(v):
            return jnp.sum(v, axis=0, keepdims=True)

        def widen(v):
            return jnp.concatenate([v, jnp.zeros((1, d - v.shape[1]), F32)], axis=1)

        stats = jnp.concatenate([
            rsum(dh), rsum(dh * xh * g), rsum(dh * (1.0 + sc) * xh), widen(rsum(dnq * cqh)), widen(rsum(dnkv * ckvh)),
            jnp.zeros((3, d), F32)], axis=0)
        first = jnp.logical_or(jnp.logical_and(lat, i % tpe == 0), i == nl)

        @pl.when(first)
        def _():
            st_ref[0] = stats

        @pl.when(jnp.logical_not(first))
        def _():
            st_ref[0] += stats

    def lat_tile(width):
        return pl.BlockSpec((tm, width), lambda i: (jnp.minimum(i, nl - 1), 0))

    def ctx_tile(width):
        return pl.BlockSpec((tm, width), lambda i: (jnp.maximum(i - nl, 0), 0))

    def tile(width):
        return pl.BlockSpec((tm, width), lambda i: (i, 0))

    tab = pl.BlockSpec((tm, 256), lambda i: (jnp.where(i < nl, i % tpe, tpe), 0))
    ex = pl.BlockSpec((1, 8, d), lambda i: (jnp.minimum(i // tpe, nex), 0, 0))
    return _hosted_call(
        body, (x2, ctx2, modv, g_attn, pg, drq, drk, dkc_r, drv, dvc_r, drg, dq_m, dkl, dkc, dvl, dvc, dxm,
               w_in, g_q, g_kv, w_uq, w_ukv, cos_t, sin_t), name="pre_bwd", grid=(nl + nc,),
        in_specs=[
            lat_tile(d), ctx_tile(d), ex, _full((1, d)), tile(PG_COLS),
            lat_tile(256), lat_tile(256), ctx_tile(256), lat_tile(512), ctx_tile(512), lat_tile(512),
            lat_tile(1024), lat_tile(1024), ctx_tile(1024), lat_tile(512), ctx_tile(512), lat_tile(d),
            _full(w_in.shape), _full((1, Q_LORA)), _full((1, KV_LORA)), _full(w_uq.shape), _full(w_ukv.shape),
            tab, tab,
        ],
        out_specs=[tile(IN_PAD), tile(1024), tile(1024), lat_tile(d), ex],
        out_shape=[
            jax.ShapeDtypeStruct((n_all, IN_PAD), BF16), jax.ShapeDtypeStruct((n_all, 1024), BF16),
            jax.ShapeDtypeStruct((n_all, 1024), BF16), jax.ShapeDtypeStruct((t_lat, d), F32),
            jax.ShapeDtypeStruct((nex + 1, 8, d), F32),
        ],
        sem=("arbitrary",), rider=rider)


def _softmax_parts(qb, kl, kc):
    scale = 1.0 / math.sqrt(MLA_NOPE + MLA_ROPE)
    s = _dot_nt(qb, kl) * scale
    sc = _dot_nt(qb, kc) * scale
    m = jnp.maximum(jnp.max(s, axis=-1, keepdims=True), jnp.max(sc, axis=-1, keepdims=True))
    p = jnp.exp(s - m)
    pc = jnp.exp(sc - m)
    inv = 1.0 / (jnp.sum(p, axis=-1, keepdims=True) + jnp.sum(pc, axis=-1, keepdims=True))
    return p, pc, inv, scale


def _mla_specs(t_lat, seq, ctx_len, tq):
    nqt = seq // tq
    cb = t_lat // ctx_len
    q = pl.BlockSpec((tq, MLA_HEAD), lambda b, h, j: (b * nqt + j, h))
    kl = pl.BlockSpec((seq, MLA_HEAD), lambda b, h, j: (b, h))
    kc = pl.BlockSpec((ctx_len, MLA_HEAD), lambda b, h, j: (cb + b, h))
    vl = pl.BlockSpec((seq, 128), lambda b, h, j: (b, h))
    vc = pl.BlockSpec((ctx_len, 128), lambda b, h, j: (cb + b, h))
    o = pl.BlockSpec((tq, 128), lambda b, h, j: (b * nqt + j, h))
    return q, kl, kc, vl, vc, o


def _mla_fwd(q, k, v, *, t_lat, seq, ctx_len, tq):
    nex = t_lat // seq

    def body(q_ref, kl_ref, kc_ref, vl_ref, vc_ref, o_ref):
        p, pc, inv, _ = _softmax_parts(q_ref[...], kl_ref[...], kc_ref[...])
        o = _dot(p.astype(BF16), vl_ref[...]) + _dot(pc.astype(BF16), vc_ref[...])
        o_ref[...] = (o * inv).astype(BF16)

    qs, kl, kc, vl, vc, os_ = _mla_specs(t_lat, seq, ctx_len, tq)
    return pl.pallas_call(
        body, name="mla_fwd", grid=(nex, HEADS, seq // tq),
        in_specs=[qs, kl, kc, vl, vc], out_specs=os_,
        out_shape=jax.ShapeDtypeStruct((t_lat, HEADS * 128), BF16),
        compiler_params=_params(("parallel", "parallel", "arbitrary"), VMEM_LIMIT),
    )(q, k, k, v, v)


def _mla_bwd(q, k, v, ymla, dmix, *, t_lat, seq, ctx_len, tq, rider=None):
    nex = t_lat // seq
    nqt = seq // tq
    t_ctx = nex * ctx_len

    def body(q_ref, kl_ref, kc_ref, vl_ref, vc_ref, o_ref, do_ref, dq_ref, dkl_ref, dkc_ref, dvl_ref, dvc_ref):
        j = pl.program_id(2)
        qb = q_ref[...]
        p, pc, inv, scale = _softmax_parts(qb, kl_ref[...], kc_ref[...])
        p = p * inv
        pc = pc * inv
        dob = do_ref[...]
        delta = jnp.sum(dob.astype(F32) * o_ref[...].astype(F32), axis=-1, keepdims=True)
        ds = (p * (_dot_nt(dob, vl_ref[...]) - delta) * scale).astype(BF16)
        dsc = (pc * (_dot_nt(dob, vc_ref[...]) - delta) * scale).astype(BF16)
        dq_ref[...] = _dot(ds, kl_ref[...]) + _dot(dsc, kc_ref[...])
        pb = p.astype(BF16)
        pcb = pc.astype(BF16)

        @pl.when(j == 0)
        def _():
            dkl_ref[...] = _dot_tn(ds, qb)
            dkc_ref[...] = _dot_tn(dsc, qb)
            dvl_ref[...] = _dot_tn(pb, dob)
            dvc_ref[...] = _dot_tn(pcb, dob)

        @pl.when(j > 0)
        def _():
            dkl_ref[...] += _dot_tn(ds, qb)
            dkc_ref[...] += _dot_tn(dsc, qb)
            dvl_ref[...] += _dot_tn(pb, dob)
            dvc_ref[...] += _dot_tn(pcb, dob)

    qs, kl, kc, vl, vc, os_ = _mla_specs(t_lat, seq, ctx_len, tq)
    do_spec = pl.BlockSpec((tq, 128), lambda b, h, j: (b * nqt + j, HEADS + h))
    return _hosted_call(
        body, (q, k, k, v, v, ymla, dmix), name="mla_bwd", grid=(nex, HEADS, nqt),
        in_specs=[qs, kl, kc, vl, vc, os_, do_spec],
        out_specs=[
            qs,
            pl.BlockSpec((seq, MLA_HEAD), lambda b, h, j: (b, h)),
            pl.BlockSpec((ctx_len, MLA_HEAD), lambda b, h, j: (b, h)),
            pl.BlockSpec((seq, 128), lambda b, h, j: (b, h)),
            pl.BlockSpec((ctx_len, 128), lambda b, h, j: (b, h)),
        ],
        out_shape=[
            jax.ShapeDtypeStruct((t_lat, HEADS * MLA_HEAD), F32),
            jax.ShapeDtypeStruct((t_lat, HEADS * MLA_HEAD), F32),
            jax.ShapeDtypeStruct((t_ctx, HEADS * MLA_HEAD), F32),
            jax.ShapeDtypeStruct((t_lat, HEADS * 128), F32),
            jax.ShapeDtypeStruct((t_ctx, HEADS * 128), F32),
        ],
        sem=("parallel", "parallel", "arbitrary"), rider=rider)


def _decay_terms(lg, chunk, forward):
    ii = lax.broadcasted_iota(jnp.int32, (chunk, chunk), 0)
    jj = lax.broadcasted_iota(jnp.int32, (chunk, chunk), 1)
    diff = (ii - jj) if forward else (jj - ii)
    dist = jnp.maximum(diff, 0).astype(F32)
    dmat = jnp.where(diff >= 0, jnp.exp(lg * dist), 0.0)
    pos = lax.broadcasted_iota(jnp.int32, (chunk, 1), 0).astype(F32)
    if forward:
        e_q = pos + 1.0
        e_k = (chunk - 1.0) - pos
    else:
        e_q = chunk - pos
        e_k = pos
    wq = jnp.exp(lg * e_q)
    wk = jnp.exp(lg * e_k)
    cd = jnp.exp(jnp.full((1, 1), lg * chunk, F32))
    return dmat, dist, wq, wk, e_q, e_k, cd


def _ctx_weights(lg, ctx_len, forward):
    pos = lax.broadcasted_iota(jnp.int32, (ctx_len, 1), 0).astype(F32)
    e = ((ctx_len - 1.0) - pos) if forward else pos
    return jnp.exp(lg * e), e


def _ret_specs(t_lat, seq, ctx_len):
    cb = t_lat // ctx_len
    qk = pl.BlockSpec((seq, 128), lambda b, h: (b, h // 2))
    v = pl.BlockSpec((seq, 128), lambda b, h: (b, h))
    kc = pl.BlockSpec((ctx_len, 128), lambda b, h: (cb + b, h // 2))
    vc = pl.BlockSpec((ctx_len, 128), lambda b, h: (cb + b, h))
    return qk, v, kc, vc


def _head_mask(h):
    lane = lax.broadcasted_iota(jnp.int32, (1, 128), 1)
    return (lane // RET_DK) == (h % 2)


def _ret_fwd(rq, rk, rv, pg, lg, g_ret, *, t_lat, seq, ctx_len, chunk, rider=None):
    nex = t_lat // seq
    n_chunk = seq // chunk

    def body(q_ref, k_ref, v_ref, kc_ref, vc_ref, rg_ref, lg_ref, g_ref, y_ref, o_ref):
        h = pl.program_id(1)
        hm = _head_mask(h)
        gain = g_ref[...]
        kcm = jnp.where(hm, kc_ref[...].astype(F32), 0.0)
        vcb = vc_ref[...]

        def run(forward):
            lgd = lg_ref[0 if forward else 1, h]
            dmat, _, wq, wk, _, _, cd = _decay_terms(lgd, chunk, forward)
            wc, _ = _ctx_weights(lgd, ctx_len, forward)
            s0 = _dot_tn((kcm * wc).astype(BF16), vcb)

            def step(t, s):
                n = t if forward else n_chunk - 1 - t
                sl = pl.ds(pl.multiple_of(n * chunk, chunk), chunk)
                qm = jnp.where(hm, q_ref[sl, :], jnp.zeros((), BF16))
                kf = jnp.where(hm, k_ref[sl, :].astype(F32), 0.0)
                vb = v_ref[sl, :]
                a = _dot_nt(qm, kf.astype(BF16)) * dmat
                o = _dot(a.astype(BF16), vb) + wq * _dot(qm, s.astype(BF16))
                if forward:
                    o_ref[sl, :] = o
                else:
                    o = o_ref[sl, :] + o
                    o_ref[sl, :] = o
                    mu = jnp.mean(o, axis=-1, keepdims=True)
                    oc = o - mu
                    var = jnp.mean(oc * oc, axis=-1, keepdims=True)
                    on = oc * lax.rsqrt(var + EPS) * gain
                    rg = rg_ref[sl, :]
                    y_ref[sl, :] = (on * (rg / (1.0 + jnp.exp(-rg)))).astype(BF16)
                return cd * s + _dot_tn((kf * wk).astype(BF16), vb)

            lax.fori_loop(0, n_chunk, step, s0)

        run(True)
        run(False)

    qk, v, kc, vc = _ret_specs(t_lat, seq, ctx_len)
    return _hosted_call(
        body, (rq, rk, rv, rk, rv, pg, lg, g_ret), name="ret_fwd", grid=(nex, HEADS),
        in_specs=[qk, qk, v, kc, vc, v, pl.BlockSpec(memory_space=pltpu.SMEM), pl.BlockSpec((1, 128), lambda b, h: (0, h))],
        out_specs=[v, v],
        out_shape=[jax.ShapeDtypeStruct((t_lat, HEADS * RET_DV), BF16), jax.ShapeDtypeStruct((t_lat, HEADS * RET_DV), F32)],
        sem=("parallel", "arbitrary"), rider=rider)


def _ret_bwd(rq, rk, rv, pg, osum, dmix, lg, g_ret, *, t_lat, seq, ctx_len, chunk, rider=None):
    nex = t_lat // seq
    n_chunk = seq // chunk
    t_ctx = nex * ctx_len

    def body(q_ref, k_ref, v_ref, kc_ref, vc_ref, rg_ref, o_ref, dy_ref, lg_ref, g_ref,
             dq_ref, dk_ref, dv_ref, dkc_ref, dvc_ref, drg_ref, st_ref, do_s, s_st):
        h = pl.program_id(1)
        hm = _head_mask(h)
        gain = g_ref[...]
        kcm = jnp.where(hm, kc_ref[...].astype(F32), 0.0)
        vcb = vc_ref[...]

        def norm_step(n, dgain):
            sl = pl.ds(pl.multiple_of(n * chunk, chunk), chunk)
            o = o_ref[sl, :]
            mu = jnp.mean(o, axis=-1, keepdims=True)
            oc = o - mu
            rstd = lax.rsqrt(jnp.mean(oc * oc, axis=-1, keepdims=True) + EPS)
            ohat = oc * rstd
            rg = rg_ref[sl, :]
            sg = 1.0 / (1.0 + jnp.exp(-rg))
            dy = dy_ref[sl, :].astype(F32)
            don = dy * (rg * sg)
            drg_ref[sl, :] = dy * (ohat * gain) * (sg * (1.0 + rg * (1.0 - sg)))
            dohat = don * gain
            do_s[sl, :] = rstd * (dohat - jnp.mean(dohat, axis=-1, keepdims=True)
                                  - ohat * jnp.mean(dohat * ohat, axis=-1, keepdims=True))
            return dgain + jnp.sum(don * ohat, axis=0, keepdims=True)

        dgain = lax.fori_loop(0, n_chunk, norm_step, jnp.zeros((1, 128), F32))

        @pl.when(h % 2 == 0)
        def _():
            dq_ref[...] = jnp.zeros(dq_ref.shape, F32)
            dk_ref[...] = jnp.zeros(dk_ref.shape, F32)
            dkc_ref[...] = jnp.zeros(dkc_ref.shape, F32)

        dv_ref[...] = jnp.zeros(dv_ref.shape, F32)

        def run(forward):
            lgd = lg_ref[0 if forward else 1, h]
            dmat, dist, wq, wk, e_q, e_k, cd = _decay_terms(lgd, chunk, forward)
            wc, e_c = _ctx_weights(lgd, ctx_len, forward)
            s0 = _dot_tn((kcm * wc).astype(BF16), vcb)

            def state_step(t, s):
                n = t if forward else n_chunk - 1 - t
                sl = pl.ds(pl.multiple_of(n * chunk, chunk), chunk)
                s_st[n] = s
                kf = jnp.where(hm, k_ref[sl, :].astype(F32), 0.0)
                return cd * s + _dot_tn((kf * wk).astype(BF16), v_ref[sl, :])

            lax.fori_loop(0, n_chunk, state_step, s0)

            def grad_step(t, carry):
                g_next, dlg = carry
                n = (n_chunk - 1 - t) if forward else t
                sl = pl.ds(pl.multiple_of(n * chunk, chunk), chunk)
                qm = jnp.where(hm, q_ref[sl, :], jnp.zeros((), BF16))
                kf = jnp.where(hm, k_ref[sl, :].astype(F32), 0.0)
                kb = kf.astype(BF16)
                vb = v_ref[sl, :]
                do = do_s[sl, :]
                dob = do.astype(BF16)
                s_n = s_st[n]
                s_nb = s_n.astype(BF16)
                gb = g_next.astype(BF16)
                dk_cross = wk * _dot_nt(vb, gb)
                dv_cross = _dot((kf * wk).astype(BF16), gb)
                a = _dot_nt(qm, kb) * dmat
                da_raw = _dot_nt(dob, vb)
                dab = (da_raw * dmat).astype(BF16)
                ab = a.astype(BF16)
                o_cross = wq * _dot(qm, s_nb)
                dq_ref[sl, :] += _dot(dab, kb) + wq * _dot_nt(dob, s_nb)
                dk_ref[sl, :] += _dot_tn(dab, qm) + dk_cross
                dv_ref[sl, :] += _dot_tn(ab, dob) + dv_cross
                dlg = (dlg + chunk * cd * jnp.sum(g_next * s_n, keepdims=True)
                       + jnp.sum(e_k * jnp.sum(kf * dk_cross, axis=-1, keepdims=True), keepdims=True)
                       + jnp.sum(dist * a * da_raw, keepdims=True)
                       + jnp.sum(e_q * jnp.sum(o_cross * do, axis=-1, keepdims=True), keepdims=True))
                g_new = cd * g_next + _dot_tn((qm.astype(F32) * wq).astype(BF16), dob)
                return g_new, dlg

            ds0, dlg = lax.fori_loop(0, n_chunk, grad_step, (jnp.zeros((128, 128), F32), jnp.zeros((1, 1), F32)))
            ds0b = ds0.astype(BF16)
            dkc_part = wc * _dot_nt(vcb, ds0b)
            dkc_ref[...] += dkc_part
            dvc_part = _dot((kcm * wc).astype(BF16), ds0b)
            dlg = dlg + jnp.sum(e_c * jnp.sum(kcm * dkc_part, axis=-1, keepdims=True), keepdims=True)
            return dvc_part, dlg

        dvc_f, dlg_f = run(True)
        dvc_b, dlg_b = run(False)
        dvc_ref[...] = dvc_f + dvc_b
        st_ref[0] = jnp.concatenate([
            dgain, jnp.broadcast_to(dlg_f, (1, 128)), jnp.broadcast_to(dlg_b, (1, 128)), jnp.zeros((5, 128), F32)], axis=0)

    qk, v, kc, vc = _ret_specs(t_lat, seq, ctx_len)
    dy_spec = v
    return _hosted_call(
        body, (rq, rk, rv, rk, rv, pg, osum, dmix, lg, g_ret), name="ret_bwd", grid=(nex, HEADS),
        in_specs=[qk, qk, v, kc, vc, v, v, dy_spec, pl.BlockSpec(memory_space=pltpu.SMEM),
                  pl.BlockSpec((1, 128), lambda b, h: (0, h))],
        out_specs=[
            qk, qk, v,
            pl.BlockSpec((ctx_len, 128), lambda b, h: (b, h // 2)),
            pl.BlockSpec((ctx_len, 128), lambda b, h: (b, h)),
            v,
            pl.BlockSpec((1, 8, 128), lambda b, h: (b, 0, h)),
        ],
        out_shape=[
            jax.ShapeDtypeStruct((t_lat, 256), F32), jax.ShapeDtypeStruct((t_lat, 256), F32),
            jax.ShapeDtypeStruct((t_lat, 512), F32), jax.ShapeDtypeStruct((t_ctx, 256), F32),
            jax.ShapeDtypeStruct((t_ctx, 512), F32), jax.ShapeDtypeStruct((t_lat, 512), F32),
            jax.ShapeDtypeStruct((nex, 8, 512), F32),
        ],
        scratch_shapes=[pltpu.VMEM((seq, 128), F32), pltpu.VMEM((n_chunk, 128, 128), F32)],
        sem=("parallel", "arbitrary"), rider=rider)


def _matmul_tn(a, b, *, bm, bn, bk, chip_major, name):
    tk, m = a.shape
    n = b.shape[1]

    def body(a_ref, b_ref, o_ref):
        k = pl.program_id(2)
        part = _dot_tn(a_ref[...], b_ref[...])

        @pl.when(k == 0)
        def _():
            o_ref[...] = part

        @pl.when(k > 0)
        def _():
            o_ref[...] += part

    if chip_major:
        out_spec = pl.BlockSpec((None, bm, bn), lambda i, j, k: (j, i, 0))
        out_shape = jax.ShapeDtypeStruct((n // bn, m, bn), F32)
    else:
        out_spec = pl.BlockSpec((bm, bn), lambda i, j, k: (i, j))
        out_shape = jax.ShapeDtypeStruct((m, n), F32)
    return pl.pallas_call(
        body, name=name, grid=(m // bm, n // bn, tk // bk),
        in_specs=[pl.BlockSpec((bk, bm), lambda i, j, k: (k, i)), pl.BlockSpec((bk, bn), lambda i, j, k: (k, j))],
        out_specs=out_spec, out_shape=out_shape,
        compiler_params=_params(("parallel", "parallel", "arbitrary"), VMEM_LIMIT),
    )(a, b)


_LATE = ("w_out", "w_ff1", "w_ff2")
_EARLY = ("w_in", "w_uq", "w_ukv")


def _local_step(x, ctx, tgt, modv, lg, g_attn, g_ffn, g_fin, g_ret, g_q, g_kv, w_in, w_uq, w_ukv, late, place=None,
                *, tm=256, tq=256, chunk=128):
    nex, seq, d = x.shape
    ctx_len = ctx.shape[1]
    t_lat = nex * seq
    x2 = x.reshape(t_lat, d)
    ctx2 = ctx.reshape(nex * ctx_len, d)
    tgt2 = tgt.reshape(t_lat, d)
    cos_t, sin_t = _rope_tables(seq, tm)
    dims = dict(t_lat=t_lat, seq=seq, ctx_len=ctx_len)
    alone = place is None

    hb, pg, rq, rk, rv, nq, nkv, q, k, v = _pre_fwd(x2, ctx2, modv, g_attn, w_in, g_q, g_kv, w_uq, w_ukv, cos_t, sin_t,
                                                    seq=seq, tm=tm)
    (yret, osum), gathered = _ret_fwd(rq, rk, rv, pg, lg, g_ret, chunk=chunk, **dims,
                                      rider=None if alone else _gather_rider(list(late)))
    w_out, w_ff1, w_ff2 = late if alone else gathered
    ymla = _mla_fwd(q, k, v, tq=tq, **dims)
    mix, act, du, h2, df, dmo, dmix, dxm, st_post = _post(yret, ymla, x2, tgt2, modv, g_ffn, g_fin, w_out.reshape(d, d),
                                                         w_ff1, w_ff2.reshape(D_FF, d), seq=seq, tm=tm)
    bk = 512
    g_late = [
        _matmul_tn(mix, dmo, bm=512, bn=1024, bk=bk, chip_major=False, name="gw_out").reshape(N_CHIPS, d // N_CHIPS, d),
        _matmul_tn(h2, du, bm=512, bn=1024, bk=bk, chip_major=True, name="gw_ff1"),
        _matmul_tn(act, df, bm=512, bn=1024, bk=bk, chip_major=False, name="gw_ff2").reshape(N_CHIPS, D_FF // N_CHIPS, d),
    ]
    (dq_m, dkl, dkc, dvl, dvc), got = _mla_bwd(q, k, v, ymla, dmix, tq=tq, **dims,
                                               rider=None if alone else _exchange_rider(g_late))
    if not alone:
        core, slot = place
        part = [_add_half(g, r, core, "add_half_" + n) for g, r, n in zip(g_late, got, _LATE)]
    (drq, drk, drv, dkc_r, dvc_r, drg, st_ret), landed = _ret_bwd(
        rq, rk, rv, pg, osum, dmix, lg, g_ret, chunk=chunk, **dims, rider=None if alone else _scatter_rider(part))
    if not alone:
        mine = [_sum_chips(p, l, slot, "sum_chips_" + n) for p, l, n in zip(part, landed, _LATE)]
    (dpb, dqf, dkvf, gx, st_pre), theirs = _pre_bwd(
        x2, ctx2, modv, g_attn, pg, drq, drk, dkc_r, drv, dvc_r, drg, dq_m, dkl, dkc, dvl, dvc, dxm, w_in, g_q, g_kv,
        w_uq, w_ukv, cos_t, sin_t, seq=seq, tm=tm, rider=None if alone else _swap_rider(mine))
    g_early = [
        _matmul_tn(hb, dpb, bm=512, bn=768, bk=bk, chip_major=False, name="gw_in"),
        _matmul_tn(nq, dqf, bm=Q_LORA, bn=MLA_HEAD, bk=bk, chip_major=True, name="gw_uq"),
        _matmul_tn(nkv, dkvf, bm=KV_LORA, bn=256, bk=bk, chip_major=True, name="gw_ukv"),
    ]
    late_out = g_late if alone else list(zip(mine, theirs))
    return gx.reshape(nex, seq, d), g_early, late_out, st_post, st_ret, st_pre


_ANY = pl.BlockSpec(memory_space=pl.ANY)
_VMEM = pl.BlockSpec(memory_space=pltpu.VMEM)
_OFFSETS = tuple((dx, dy, dc) for dx in (0, 1) for dy in (0, 1) for dc in (0, 1))[1:]
_CHIP_OFFSETS = ((1, 0), (0, 1), (1, 1))


def _place():
    return lax.axis_index("x"), lax.axis_index("y"), lax.axis_index("c")


def _flip(v, d):
    return 1 - v if d else v


def _allgather8(a, name):
    r, c = a.shape

    def body(a_ref, o_ref, send, recv, lsem):
        x, y, z = _place()
        me = 4 * x + 2 * y + z
        mine = pltpu.make_async_copy(a_ref, o_ref.at[me], lsem)
        mine.start()
        copies = []
        for k, (dx, dy, dc) in enumerate(_OFFSETS):
            cp = pltpu.make_async_remote_copy(
                src_ref=a_ref, dst_ref=o_ref.at[me], send_sem=send.at[k], recv_sem=recv.at[k],
                device_id=(_flip(x, dx), _flip(y, dy), _flip(z, dc)), device_id_type=MESH)
            cp.start()
            copies.append(cp)
        for k, (dx, dy, dc) in enumerate(_OFFSETS):
            peer = 4 * _flip(x, dx) + 2 * _flip(y, dy) + _flip(z, dc)
            pltpu.make_async_remote_copy(
                src_ref=a_ref, dst_ref=o_ref.at[peer], send_sem=send.at[k], recv_sem=recv.at[k],
                device_id=(_flip(x, dx), _flip(y, dy), _flip(z, dc)), device_id_type=MESH).wait_recv()
        for cp in copies:
            cp.wait_send()
        mine.wait()

    return pl.pallas_call(
        body, name=name, in_specs=[_VMEM], out_specs=_VMEM,
        out_shape=jax.ShapeDtypeStruct((N_DEV, r, c), a.dtype),
        scratch_shapes=[pltpu.SemaphoreType.DMA((7,)), pltpu.SemaphoreType.DMA((7,)), pltpu.SemaphoreType.DMA],
    )(a)


def _gather_send(o_refs, send, recv):
    x, y, z = _place()
    chip = 2 * x + y
    for a, o in enumerate(o_refs):
        r2 = o.shape[1] // 2
        mine = o.at[chip, pl.ds(z * r2, r2)]
        for k, (dx, dy) in enumerate(_CHIP_OFFSETS):
            pltpu.make_async_remote_copy(
                src_ref=mine, dst_ref=mine, send_sem=send.at[a, k], recv_sem=recv.at[a, k],
                device_id=(_flip(x, dx), _flip(y, dy), z), device_id_type=MESH).start()


def _gather_finish(o_refs, send, recv, fsend, frecv):
    x, y, z = _place()
    chip = 2 * x + y
    sib = (x, y, 1 - z)
    passed = []
    for a, o in enumerate(o_refs):
        r2 = o.shape[1] // 2
        for k, (dx, dy) in enumerate(_CHIP_OFFSETS):
            other = 2 * _flip(x, dx) + _flip(y, dy)
            landed = o.at[other, pl.ds(z * r2, r2)]
            pltpu.make_async_remote_copy(
                src_ref=landed, dst_ref=landed, send_sem=send.at[a, k], recv_sem=recv.at[a, k],
                device_id=(_flip(x, dx), _flip(y, dy), z), device_id_type=MESH).wait_recv()
            cp = pltpu.make_async_remote_copy(
                src_ref=landed, dst_ref=landed, send_sem=fsend.at[a, k], recv_sem=frecv.at[a, k],
                device_id=sib, device_id_type=MESH)
            cp.start()
            passed.append(cp)
    for a, o in enumerate(o_refs):
        r2 = o.shape[1] // 2
        mine = o.at[chip, pl.ds(z * r2, r2)]
        for k, (dx, dy) in enumerate(_CHIP_OFFSETS):
            other = 2 * _flip(x, dx) + _flip(y, dy)
            got = o.at[other, pl.ds((1 - z) * r2, r2)]
            pltpu.make_async_remote_copy(
                src_ref=got, dst_ref=got, send_sem=fsend.at[a, k], recv_sem=frecv.at[a, k],
                device_id=sib, device_id_type=MESH).wait_recv()
            pltpu.make_async_remote_copy(
                src_ref=mine, dst_ref=mine, send_sem=send.at[a, k], recv_sem=recv.at[a, k],
                device_id=(_flip(x, dx), _flip(y, dy), z), device_id_type=MESH).wait_send()
    for cp in passed:
        cp.wait_send()


class _Rider:
    def __init__(self, ins, out_shapes, sems, start, finish, aliases=None):
        self.ins, self.out_shapes, self.sems = list(ins), list(out_shapes), list(sems)
        self.start, self.finish, self.aliases = start, finish, dict(aliases or {})


def _run_rider(rider, name):
    r_in, r_out = len(rider.ins), len(rider.out_shapes)

    def body(*refs):
        ins, outs, sems = refs[:r_in], refs[r_in:r_in + r_out], refs[r_in + r_out:]
        rider.start(ins, outs, sems)
        rider.finish(ins, outs, sems)

    return pl.pallas_call(
        body, name=name, in_specs=[_ANY] * r_in, out_specs=[_ANY] * r_out, out_shape=rider.out_shapes,
        input_output_aliases=rider.aliases, scratch_shapes=rider.sems,
    )(*rider.ins)


def _hosted_call(body, args, *, name, grid, in_specs, out_specs, out_shape, scratch_shapes=(), sem, rider=None):
    scratch_shapes = list(scratch_shapes)
    if rider is None:
        res = pl.pallas_call(
            body, name=name, grid=grid, in_specs=in_specs, out_specs=out_specs, out_shape=out_shape,
            scratch_shapes=scratch_shapes, compiler_params=_params(sem, VMEM_LIMIT))(*args)
        return list(res), []
    n_in, n_out, n_sc = len(in_specs), len(out_specs), len(scratch_shapes)
    r_in, r_out = len(rider.ins), len(rider.out_shapes)
    last = tuple(g - 1 for g in grid)

    def hosted(*refs):
        p = 0
        parts = []
        for cnt in (n_in, r_in, n_out, r_out, n_sc):
            parts.append(refs[p:p + cnt])
            p += cnt
        ins, r_ins, outs, r_outs, scratch = parts
        sems = refs[p:]
        ids = [pl.program_id(a) for a in range(len(grid))]
        is_first = functools.reduce(jnp.logical_and, [i == 0 for i in ids])
        is_last = functools.reduce(jnp.logical_and, [i == e for i, e in zip(ids, last)])

        @pl.when(is_first)
        def _():
            rider.start(r_ins, r_outs, sems)

        body(*ins, *outs, *scratch)

        @pl.when(is_last)
        def _():
            rider.finish(r_ins, r_outs, sems)

    res = pl.pallas_call(
        hosted, name=name, grid=grid, in_specs=list(in_specs) + [_ANY] * r_in, out_specs=list(out_specs) + [_ANY] * r_out,
        out_shape=list(out_shape) + rider.out_shapes, scratch_shapes=scratch_shapes + rider.sems,
        input_output_aliases={n_in + i: n_out + j for i, j in rider.aliases.items()},
        compiler_params=_params(("arbitrary",) * len(grid), VMEM_LIMIT))(*args, *rider.ins)
    return list(res[:n_out]), list(res[n_out:])


def _gather_rider(ws):
    n = len(ws)
    return _Rider(
        ws, [jax.ShapeDtypeStruct(w.shape, w.dtype) for w in ws], [pltpu.SemaphoreType.DMA((n, 3))] * 4,
        lambda ins, outs, sems: _gather_send(outs, sems[0], sems[1]),
        lambda ins, outs, sems: _gather_finish(outs, *sems),
        aliases={a: a for a in range(n)})


def _copies_rider(ins, out_shapes, sem_shape, make):
    def start(r_ins, r_outs, sems):
        for cp in make(r_ins, r_outs, sems[0], sems[1]):
            cp.start()

    def finish(r_ins, r_outs, sems):
        for cp in make(r_ins, r_outs, sems[0], sems[1]):
            cp.wait()

    return _Rider(ins, out_shapes, [pltpu.SemaphoreType.DMA(sem_shape)] * 2, start, finish)


def _exchange_rider(gs):
    def make(g_refs, r_refs, send, recv):
        x, y, z = _place()
        return [pltpu.make_async_remote_copy(
            src_ref=g.at[:, pl.ds((1 - z) * (g.shape[1] // 2), g.shape[1] // 2)], dst_ref=r, send_sem=send.at[a],
            recv_sem=recv.at[a], device_id=(x, y, 1 - z), device_id_type=MESH)
            for a, (g, r) in enumerate(zip(g_refs, r_refs))]

    shapes = [jax.ShapeDtypeStruct((g.shape[0], g.shape[1] // 2, g.shape[2]), g.dtype) for g in gs]
    return _copies_rider(gs, shapes, (len(gs),), make)


def _add_half(g, recv, core, name):
    s, r, c = g.shape
    r2 = r // 2
    rb = r2
    for cand in (256, 128, 64):
        if r2 % cand == 0:
            rb = cand
            break
    g4 = g.reshape(s, 2, r2, c)

    def body(core_ref, g_ref, r_ref, o_ref):
        o_ref[...] = (g_ref[...] + r_ref[...]).astype(BF16)

    return pl.pallas_call(
        body, name=name,
        grid_spec=pltpu.PrefetchScalarGridSpec(
            num_scalar_prefetch=1, grid=(s, r2 // rb),
            in_specs=[pl.BlockSpec((None, None, rb, c), lambda i, j, cr: (i, cr[0], j, 0)),
                      pl.BlockSpec((None, rb, c), lambda i, j, cr: (i, j, 0))],
            out_specs=pl.BlockSpec((None, rb, c), lambda i, j, cr: (i, j, 0))),
        out_shape=jax.ShapeDtypeStruct((s, r2, c), BF16),
        compiler_params=_params(("parallel", "parallel")),
    )(core, g4, recv)


def _scatter_rider(ps):
    def make(p_refs, o_refs, send, recv):
        x, y, z = _place()
        copies = []
        for a, (p, o) in enumerate(zip(p_refs, o_refs)):
            for k, (dx, dy) in enumerate(_CHIP_OFFSETS):
                other = 2 * _flip(x, dx) + _flip(y, dy)
                copies.append(pltpu.make_async_remote_copy(
                    src_ref=p.at[other], dst_ref=o.at[k], send_sem=send.at[a, k], recv_sem=recv.at[a, k],
                    device_id=(_flip(x, dx), _flip(y, dy), z), device_id_type=MESH))
        return copies

    shapes = [jax.ShapeDtypeStruct((3,) + p.shape[1:], p.dtype) for p in ps]
    return _copies_rider(ps, shapes, (len(ps), 3), make)


def _sum_chips(p, landed, chip, name):
    _, r2, c = p.shape
    rb = r2
    for cand in (256, 128, 64):
        if r2 % cand == 0:
            rb = cand
            break

    def body(s_ref, p_ref, l_ref, o_ref):
        acc = p_ref[...].astype(F32)
        for k in range(3):
            acc = acc + l_ref[k].astype(F32)
        o_ref[...] = acc

    return pl.pallas_call(
        body, name=name,
        grid_spec=pltpu.PrefetchScalarGridSpec(
            num_scalar_prefetch=1, grid=(r2 // rb,),
            in_specs=[pl.BlockSpec((None, rb, c), lambda i, s: (s[0], i, 0)),
                      pl.BlockSpec((3, rb, c), lambda i, s: (0, i, 0))],
            out_specs=pl.BlockSpec((rb, c), lambda i, s: (i, 0))),
        out_shape=jax.ShapeDtypeStruct((r2, c), F32),
        compiler_params=_params(("parallel",)),
    )(chip, p, landed)


def _swap_rider(hs):
    def make(h_refs, o_refs, send, recv):
        x, y, z = _place()
        return [pltpu.make_async_remote_copy(
            src_ref=h, dst_ref=o, send_sem=send.at[a], recv_sem=recv.at[a], device_id=(x, y, 1 - z),
            device_id_type=MESH) for a, (h, o) in enumerate(zip(h_refs, o_refs))]

    return _copies_rider(hs, [jax.ShapeDtypeStruct(h.shape, h.dtype) for h in hs], (len(hs),), make)


SMALL_ROWS = 32
PACK_ROWS = 16


def _pack_small(st_post, st_ret, st_pre):
    d = st_post.shape[2]

    def body(po_ref, re_ref, pr_ref, o_ref):
        o_ref[...] = jnp.zeros(o_ref.shape, F32)
        o_ref[0:1, :] = pr_ref[0, 2:3, :] + pr_ref[1, 2:3, :] + pr_ref[2, 2:3, :]
        o_ref[1:2, :] = po_ref[0, 4:5, :] + po_ref[1, 4:5, :]
        o_ref[2:3, :] = po_ref[0, 5:6, :] + po_ref[1, 5:6, :]
        o_ref[3:4, 0:512] = re_ref[0, 0:1, :] + re_ref[1, 0:1, :]
        o_ref[4:5, :] = pr_ref[0, 3:4, :] + pr_ref[1, 3:4, :] + pr_ref[2, 3:4, :]
        o_ref[5:6, :] = pr_ref[0, 4:5, :] + pr_ref[1, 4:5, :] + pr_ref[2, 4:5, :]
        lane = lax.broadcasted_iota(jnp.int32, (1, LANES), 1)
        for row, src in ((6, 1), (10, 2)):
            acc = jnp.zeros((1, LANES), F32)
            for hd in range(HEADS):
                grp = re_ref[0, src:src + 1, hd * LANES:(hd + 1) * LANES] + re_ref[1, src:src + 1, hd * LANES:(hd + 1) * LANES]
                acc = acc + jnp.where(lane == hd, grp, 0.0)
            o_ref[row:row + 1, 0:LANES] = acc
        o_ref[7:8, :] = po_ref[0, 6:7, :] + po_ref[1, 6:7, :]
        o_ref[8:9, :] = pr_ref[2, 0:1, :]
        o_ref[9:10, :] = pr_ref[2, 1:2, :]
        for e in range(2):
            b = 12 + 6 * e
            o_ref[b:b + 1, :] = pr_ref[e, 0:1, :]
            o_ref[b + 1:b + 2, :] = pr_ref[e, 1:2, :]
            o_ref[b + 2:b + 3, :] = po_ref[e, 3:4, :]
            o_ref[b + 3:b + 4, :] = po_ref[e, 0:1, :]
            o_ref[b + 4:b + 5, :] = po_ref[e, 1:2, :]
            o_ref[b + 5:b + 6, :] = po_ref[e, 2:3, :]

    return pl.pallas_call(body, name="pack_small", out_shape=jax.ShapeDtypeStruct((SMALL_ROWS, d), F32))(st_post, st_ret, st_pre)


def _small_reduce(gathered):
    d = gathered.shape[2]

    def body(g_ref, o_ref):
        tot = g_ref[0, 0:PACK_ROWS, :]
        for dev in range(1, N_DEV):
            tot = tot + g_ref[dev, 0:PACK_ROWS, :]
        o_ref[0:PACK_ROWS, :] = tot
        for j in range(6):
            acc = g_ref[0, 12 + j:13 + j, :] + g_ref[0, 18 + j:19 + j, :]
            for dev in range(1, N_DEV):
                acc = acc + g_ref[dev, 12 + j:13 + j, :] + g_ref[dev, 18 + j:19 + j, :]
            if j < 2:
                acc = acc + o_ref[8 + j:9 + j, :]
            o_ref[PACK_ROWS + j:PACK_ROWS + j + 1, :] = acc
        o_ref[PACK_ROWS + 6:PACK_ROWS + 8, :] = jnp.zeros((2, d), F32)

    return pl.pallas_call(body, name="small_reduce", out_shape=jax.ShapeDtypeStruct((PACK_ROWS + 8, d), F32))(gathered)


_SMALL = (("g_attn", 0, 1024), ("g_ffn", 1, 1024), ("g_final", 2, 1024), ("g_ret", 3, 512), ("g_q_lora", 4, 384),
          ("g_kv_lora", 5, 256), ("ret_decay_fwd", 6, HEADS), ("ret_decay_bwd", 10, HEADS))
_SMALL_NAMES = tuple(s[0] for s in _SMALL) + ("c_ctx", "b_ada")


def _small_final(tot, dcc, sg8, ws, ms, vs):
    d = tot.shape[1]
    n = len(_SMALL_NAMES)

    def body(*refs):
        t_ref, dcc_ref, sg_ref = refs[0:3]
        w_refs, m_refs, v_refs = refs[3:3 + n], refs[3 + n:3 + 2 * n], refs[3 + 2 * n:3 + 3 * n]
        outs = refs[3 + 3 * n:]
        g_refs, d_refs, mo_refs, vo_refs = outs[0:n], outs[n:2 * n], outs[2 * n:3 * n], outs[3 * n:4 * n]
        l_ref = outs[4 * n]

        def update(i, g, sl=None):
            pick = (lambda r: r[...]) if sl is None else (lambda r: r[:, sl])
            dl, mn, vn = _adam_math(pick(w_refs[i]), g, pick(m_refs[i]), pick(v_refs[i]))
            if sl is None:
                g_refs[i][...], d_refs[i][...], mo_refs[i][...], vo_refs[i][...] = g, dl, mn, vn
            else:
                g_refs[i][:, sl], d_refs[i][:, sl], mo_refs[i][:, sl], vo_refs[i][:, sl] = g, dl, mn, vn

        for i, (name, row, width) in enumerate(_SMALL):
            g = t_ref[row:row + 1, 0:width]
            if name == "ret_decay_fwd":
                g = g * sg_ref[0:1, 0:width]
            elif name == "ret_decay_bwd":
                g = g * sg_ref[1:2, 0:width]
            update(i, g)
        i_cc, i_b = n - 2, n - 1
        cc = w_refs[i_cc][...]
        s = 1.0 / (1.0 + jnp.exp(-cc))
        dsilu = dcc_ref[0, 0:1, :] + dcc_ref[2, 0:1, :] + dcc_ref[4, 0:1, :] + dcc_ref[6, 0:1, :]
        update(i_cc, dsilu * (s * (1.0 + cc * (1.0 - s))))
        for j in range(6):
            update(i_b, t_ref[PACK_ROWS + j:PACK_ROWS + j + 1, :], pl.ds(j * d, d))
        l_ref[...] = jnp.broadcast_to((0.5 / d) * jnp.sum(t_ref[7:8, :], keepdims=True), l_ref.shape)

    shapes = [jax.ShapeDtypeStruct(a.shape, F32) for a in ws]
    outs = pl.pallas_call(
        body, name="small_final", out_shape=shapes * 4 + [jax.ShapeDtypeStruct((8, LANES), F32)],
    )(tot, dcc, sg8, *ws, *ms, *vs)
    return outs[0:n], outs[n:2 * n], outs[2 * n:3 * n], outs[3 * n:4 * n], outs[4 * n]


_WEIGHTS = ("c_ctx", "w_ada", "b_ada", "g_attn", "g_ffn", "w_in", "ret_decay_fwd", "ret_decay_bwd", "g_ret", "g_q_lora",
            "w_uq", "g_kv_lora", "w_ukv", "w_out", "w_ff1", "w_ff2", "g_final")
_BIG = ("w_in", "w_uq", "w_ukv", "w_out", "w_ff1", "w_ff2")


def kernel(x, c, ctx, c_ctx, w_ada, b_ada, g_attn, g_ffn, w_in, ret_decay_fwd, ret_decay_bwd, g_ret, g_q_lora, w_uq, g_kv_lora, w_ukv, w_out, w_ff1, w_ff2, g_final, loss_target, m_c_ctx, m_w_ada, m_b_ada, m_g_attn, m_g_ffn, m_w_in, m_ret_decay_fwd, m_ret_decay_bwd, m_g_ret, m_g_q_lora, m_w_uq, m_g_kv_lora, m_w_ukv, m_w_out, m_w_ff1, m_w_ff2, m_g_final, v_c_ctx, v_w_ada, v_b_ada, v_g_attn, v_g_ffn, v_w_in, v_ret_decay_fwd, v_ret_decay_bwd, v_g_ret, v_g_q_lora, v_w_uq, v_g_kv_lora, v_w_ukv, v_w_out, v_w_ff1, v_w_ff2, v_g_final):
    w = dict(c_ctx=c_ctx, w_ada=w_ada, b_ada=b_ada, g_attn=g_attn, g_ffn=g_ffn, w_in=w_in, ret_decay_fwd=ret_decay_fwd,
             ret_decay_bwd=ret_decay_bwd, g_ret=g_ret, g_q_lora=g_q_lora, w_uq=w_uq, g_kv_lora=g_kv_lora, w_ukv=w_ukv,
             w_out=w_out, w_ff1=w_ff1, w_ff2=w_ff2, g_final=g_final)
    m = dict(c_ctx=m_c_ctx, w_ada=m_w_ada, b_ada=m_b_ada, g_attn=m_g_attn, g_ffn=m_g_ffn, w_in=m_w_in,
             ret_decay_fwd=m_ret_decay_fwd, ret_decay_bwd=m_ret_decay_bwd, g_ret=m_g_ret, g_q_lora=m_g_q_lora, w_uq=m_w_uq,
             g_kv_lora=m_g_kv_lora, w_ukv=m_w_ukv, w_out=m_w_out, w_ff1=m_w_ff1, w_ff2=m_w_ff2, g_final=m_g_final)
    v = dict(c_ctx=v_c_ctx, w_ada=v_w_ada, b_ada=v_b_ada, g_attn=v_g_attn, g_ffn=v_g_ffn, w_in=v_w_in,
             ret_decay_fwd=v_ret_decay_fwd, ret_decay_bwd=v_ret_decay_bwd, g_ret=v_g_ret, g_q_lora=v_g_q_lora, w_uq=v_w_uq,
             g_kv_lora=v_g_kv_lora, w_ukv=v_w_ukv, w_out=v_w_out, w_ff1=v_w_ff1, w_ff2=v_w_ff2, g_final=v_g_final)
    xi, yi, ci = lax.axis_index("x"), lax.axis_index("y"), lax.axis_index("c")
    chip = 2 * xi + yi
    dev = 2 * chip + ci
    nex, seq, d = x.shape
    n_ada = w_ada.shape[2]

    c_all = _allgather8(jnp.pad(c, ((0, 8 - nex), (0, 0))), "ag_c")[:, :nex].reshape(N_DEV * nex, d)
    a_in = jnp.concatenate([c_all, c_ctx.reshape(1, d), jnp.zeros((7, d), F32)], axis=0)
    b_sh = lax.dynamic_slice(b_ada, (0, chip * n_ada), (1, n_ada))
    mod_sh = _mod_fwd(a_in, w_ada[0], b_sh)
    mod_all = _allgather8(mod_sh, "ag_mod")[0::2].transpose(1, 0, 2).reshape(a_in.shape[0], N_CHIPS * n_ada)
    mod_me = lax.dynamic_slice(mod_all, (nex * dev, 0), (nex, N_CHIPS * n_ada)).reshape(nex, 6, d)
    mod_c = mod_all[N_DEV * nex].reshape(1, 6, d)
    modv = jnp.pad(jnp.concatenate([mod_me, mod_c], axis=0), ((0, 0), (0, 2), (0, 0)))

    dec = jnp.zeros((8, LANES), F32).at[0, :HEADS].set(ret_decay_fwd[0]).at[1, :HEADS].set(ret_decay_bwd[0])
    lg8, sg8 = _decay_prep(dec)
    lg = lg8[:2, :HEADS]

    shard = {k: w[k][0] for k in _BIG}
    shard["w_uq"] = jnp.pad(shard["w_uq"], ((0, 0), (0, MLA_HEAD - MLA_NOPE - MLA_ROPE)))
    slot = chip.reshape(1).astype(jnp.int32)
    core = ci.reshape(1).astype(jnp.int32)
    slots = {k: _cast_into_slot(shard[k], slot, "cast_" + k) for k in _BIG}
    w_in_f, w_uq_k, w_ukv_k = _run_rider(_gather_rider([slots[k] for k in _EARLY]), "ag_early")
    w_in_k = jnp.pad(w_in_f.transpose(1, 0, 2).reshape(d, IN_COLS), ((0, 0), (0, IN_PAD - IN_COLS)))

    gx, g_early, late, st_post, st_ret, st_pre = _local_step(
        x, ctx, loss_target, modv, lg, g_attn, g_ffn, g_final.reshape(1, d), g_ret, g_q_lora, g_kv_lora,
        w_in_k, w_uq_k, w_ukv_k, [slots[k] for k in _LATE], (core, slot))

    per_chip = IN_COLS // N_CHIPS
    g4 = [
        g_early[0][:, :IN_COLS].reshape(d, N_CHIPS, per_chip).transpose(1, 0, 2),
        g_early[1][:, :, :MLA_NOPE + MLA_ROPE],
        g_early[2],
    ]
    got = _run_rider(_exchange_rider(g4), "rs_exchange")
    partial = [_add_half(g, r, core, "add_half_" + k) for g, r, k in zip(g4, got, _EARLY)]
    landed = _run_rider(_scatter_rider(partial), "rs_scatter")
    mine = [_sum_chips(p, l, slot, "sum_chips_" + k) for p, l, k in zip(partial, landed, _EARLY)]
    theirs = _run_rider(_swap_rider(mine), "rs_swap")
    halves = dict(zip(_EARLY, zip(mine, theirs)))
    halves.update(zip(_LATE, late))
    grad, delta, new_m, new_v = {}, {}, {}, {}
    for k in _BIG:
        a, b = halves[k]
        shp = w[k].shape
        outs = _adamw_halves(w[k].reshape(shp[1:]), a, b, m[k].reshape(shp[1:]), v[k].reshape(shp[1:]), core, "adamw_" + k)
        grad[k], delta[k], new_m[k], new_v[k] = [o.reshape(shp) for o in outs]

    gathered = _allgather8(_pack_small(st_post, st_ret, st_pre), "ag_small")
    tot = _small_reduce(gathered)
    dm = jnp.concatenate([
        gathered[:, 12:24].reshape(N_DEV * nex, 6 * d),
        jnp.concatenate([tot[8:10].reshape(1, 2 * d), jnp.zeros((1, 4 * d), F32)], axis=1),
        jnp.zeros((7, 6 * d), F32)], axis=0)
    dm_sh = lax.dynamic_slice(dm, (0, chip * n_ada), (dm.shape[0], n_ada))
    g_ada, da = _mod_bwd(a_in, dm_sh, w_ada[0])
    dcc = _allgather8(da[N_DEV * nex:], "ag_dcc")
    shp = w_ada.shape
    outs = _adamw(w_ada[0], g_ada, m["w_ada"][0], v["w_ada"][0], "adamw_w_ada")
    grad["w_ada"] = g_ada.reshape(shp)
    delta["w_ada"], new_m["w_ada"], new_v["w_ada"] = [o.reshape(shp) for o in outs]
    rows = [{k: t[k].reshape(1, -1) for k in _SMALL_NAMES} for t in (w, m, v)]
    small = _small_final(tot, dcc, sg8, *[[t[k] for k in _SMALL_NAMES] for t in rows])
    for res, outs in zip((grad, delta, new_m, new_v), small[:4]):
        for k, o in zip(_SMALL_NAMES, outs):
            res[k] = o.reshape(w[k].shape)
    return (small[4][0, 0], gx, *[grad[k] for k in _WEIGHTS], *[delta[k] for k in _WEIGHTS],
            *[new_m[k] for k in _WEIGHTS], *[new_v[k] for k in _WEIGHTS])
```

```python
import functools
import math

import jax
import jax.numpy as jnp
from jax import lax
from jax.experimental import pallas as pl
from jax.experimental.pallas import tpu as pltpu

F32 = jnp.float32
BF16 = jnp.bfloat16
MESH = pl.DeviceIdType.MESH

EPS = 1e-6
D_MODEL = 1024
D_FF = 4096
HEADS = 4
RET_DK = 64
RET_DV = 128
MLA_NOPE = 128
MLA_ROPE = 64
MLA_HEAD = 256
Q_LORA = 384
KV_LORA = 256
GRID_W = 64
ROPE_BASE = 10000.0
IN_COLS = 2240
IN_PAD = 2304
PG_COLS = 1152
N_CHIPS = 4
N_DEV = 8
LANES = 128
ADAM_LR = 0.001
ADAM_B1 = 0.9
ADAM_B2 = 0.999
ADAM_EPS = 1e-08
ADAM_WD = 0.01
ADAM_STEP = 10
VMEM_LIMIT = 56 * 1024 * 1024


def _dot(a, b):
    return jnp.dot(a, b, preferred_element_type=F32)


def _dot_nt(a, b):
    return lax.dot_general(a, b, (((1,), (1,)), ((), ())), preferred_element_type=F32)


def _dot_tn(a, b):
    return lax.dot_general(a, b, (((0,), (0,)), ((), ())), preferred_element_type=F32)


def _params(sem=None, vmem=None):
    return pltpu.CompilerParams(dimension_semantics=sem, vmem_limit_bytes=vmem)


def _full(shape):
    n = len(shape)
    return pl.BlockSpec(shape, lambda *_: (0,) * n)


def _rope(x, cos, sin):
    w = x.shape[-1]
    lo = (lax.broadcasted_iota(jnp.int32, (1, w), 1) % 64) < 32
    swapped = jnp.where(lo, pltpu.roll(x, w - 32, 1), pltpu.roll(x, 32, 1))
    return x * cos + swapped * sin


def _rope_t(g, cos, sin):
    w = g.shape[-1]
    lo = (lax.broadcasted_iota(jnp.int32, (1, w), 1) % 64) < 32
    t = g * sin
    swapped = jnp.where(lo, pltpu.roll(t, w - 32, 1), pltpu.roll(t, 32, 1))
    return g * cos + swapped


def _rope_tables(seq, tm):
    rows = seq // GRID_W
    row = jnp.repeat(jnp.arange(rows, dtype=F32), GRID_W)
    col = jnp.tile(jnp.arange(GRID_W, dtype=F32), rows)
    n_freq = RET_DK // 4
    freq = ROPE_BASE ** (-jnp.arange(n_freq, dtype=F32) / n_freq)
    ang = jnp.concatenate([row[:, None] * freq, col[:, None] * freq], axis=-1)
    cos, sin = jnp.cos(ang), jnp.sin(ang)
    cos_t = jnp.tile(jnp.concatenate([cos, cos], -1), (1, HEADS))
    sin_t = jnp.tile(jnp.concatenate([-sin, sin], -1), (1, HEADS))
    cos_t = jnp.concatenate([cos_t, jnp.ones((tm, 4 * RET_DK), F32)], 0)
    sin_t = jnp.concatenate([sin_t, jnp.zeros((tm, 4 * RET_DK), F32)], 0)
    return cos_t, sin_t


def _adam_math(w, g, m, v):
    mn = ADAM_B1 * m + (1.0 - ADAM_B1) * g
    vn = ADAM_B2 * v + (1.0 - ADAM_B2) * (g * g)
    m_hat = mn / (1.0 - ADAM_B1 ** ADAM_STEP)
    v_hat = vn / (1.0 - ADAM_B2 ** ADAM_STEP)
    return -ADAM_LR * (m_hat / (jnp.sqrt(v_hat) + ADAM_EPS) + ADAM_WD * w), mn, vn


def _cast_into_slot(w, slot, name):
    r, c = w.shape
    rb = math.gcd(r, 256)

    def body(s_ref, w_ref, o_ref):
        o_ref[...] = w_ref[...].astype(BF16)

    return pl.pallas_call(
        body, name=name,
        grid_spec=pltpu.PrefetchScalarGridSpec(
            num_scalar_prefetch=1, grid=(r // rb,),
            in_specs=[pl.BlockSpec((rb, c), lambda i, s: (i, 0))],
            out_specs=pl.BlockSpec((None, rb, c), lambda i, s: (s[0], i, 0))),
        out_shape=jax.ShapeDtypeStruct((N_CHIPS, r, c), BF16),
        compiler_params=_params(("parallel",)),
    )(slot, w)


def _adamw_halves(w, mine, theirs, m, v, core, name):
    r, c = w.shape
    r2 = r // 2
    rb = 8
    for cand in (256, 128, 64, 32, 16):
        if r2 % cand == 0 and cand * c * 4 <= (1 << 20):
            rb = cand
            break
    nbh = r2 // rb

    def body(z_ref, w_ref, a_ref, b_ref, m_ref, v_ref, g_ref, d_ref, mo_ref, vo_ref):
        here = (pl.program_id(0) // nbh) == z_ref[0]
        gg = jnp.where(here, a_ref[...], b_ref[...])
        g_ref[...] = gg
        d_ref[...], mo_ref[...], vo_ref[...] = _adam_math(w_ref[...], gg, m_ref[...], v_ref[...])

    spec = pl.BlockSpec((rb, c), lambda i, z: (i, 0))
    a_spec = pl.BlockSpec((rb, c), lambda i, z: (jnp.clip(i - z[0] * nbh, 0, nbh - 1), 0))
    b_spec = pl.BlockSpec((rb, c), lambda i, z: (jnp.clip(i - (1 - z[0]) * nbh, 0, nbh - 1), 0))
    shp = jax.ShapeDtypeStruct((r, c), F32)
    return pl.pallas_call(
        body, name=name,
        grid_spec=pltpu.PrefetchScalarGridSpec(
            num_scalar_prefetch=1, grid=(r // rb,), in_specs=[spec, a_spec, b_spec, spec, spec], out_specs=[spec] * 4),
        out_shape=[shp] * 4,
        compiler_params=_params(("parallel",)),
    )(core, w, mine, theirs, m, v)


def _adamw(w, g, m, v, name):
    r, c = w.shape
    rb = r
    for cand in (256, 128, 64, 32, 16, 8):
        if r % cand == 0 and cand * c * 4 <= (1 << 20):
            rb = cand
            break
    if r * c * 4 <= (1 << 20):
        rb = r

    def body(w_ref, g_ref, m_ref, v_ref, d_ref, mo_ref, vo_ref):
        d_ref[...], mo_ref[...], vo_ref[...] = _adam_math(w_ref[...], g_ref[...], m_ref[...], v_ref[...])

    spec = pl.BlockSpec((rb, c), lambda i: (i, 0))
    shp = jax.ShapeDtypeStruct((r, c), F32)
    return pl.pallas_call(
        body, name=name, grid=(r // rb,), in_specs=[spec] * 4, out_specs=[spec] * 3, out_shape=[shp] * 3,
        compiler_params=_params(("parallel",)),
    )(w, g, m, v)


def _decay_prep(dec):
    def body(d_ref, lg_ref, sg_ref):
        d = d_ref[...]
        lg_ref[...] = jnp.minimum(d, 0.0) - jnp.log(1.0 + jnp.exp(-jnp.abs(d)))
        sg_ref[...] = 1.0 / (1.0 + jnp.exp(d))

    shp = jax.ShapeDtypeStruct(dec.shape, F32)
    return pl.pallas_call(body, name="decay_prep", out_shape=[shp, shp])(dec)


def _mod_fwd(a_in, w_ada, b_sh):
    rows, d = a_in.shape
    n = w_ada.shape[1]
    bn = 512

    def body(a_ref, w_ref, b_ref, o_ref):
        a = a_ref[...]
        s = (a / (1.0 + jnp.exp(-a))).astype(BF16)
        o_ref[...] = _dot(s, w_ref[...].astype(BF16)) + b_ref[...]

    return pl.pallas_call(
        body, name="mod_fwd", grid=(n // bn,),
        in_specs=[_full((rows, d)), pl.BlockSpec((d, bn), lambda j: (0, j)), pl.BlockSpec((1, bn), lambda j: (0, j))],
        out_specs=pl.BlockSpec((rows, bn), lambda j: (0, j)),
        out_shape=jax.ShapeDtypeStruct((rows, n), F32),
        compiler_params=_params(("parallel",)),
    )(a_in, w_ada, b_sh)


def _mod_bwd(a_in, dm, w_ada):
    rows, d = a_in.shape
    n = w_ada.shape[1]
    bn = 512
    nb = n // bn

    def body(a_ref, dm_ref, w_ref, gw_ref, da_ref):
        j = pl.program_id(0)
        a = a_ref[...]
        s = (a / (1.0 + jnp.exp(-a))).astype(BF16)
        dmb = dm_ref[...].astype(BF16)
        gw_ref[...] = _dot_tn(s, dmb)
        part = _dot_nt(dmb, w_ref[...].astype(BF16))

        @pl.when(j == 0)
        def _():
            da_ref[...] = part

        @pl.when(j > 0)
        def _():
            da_ref[...] += part

    return pl.pallas_call(
        body, name="mod_bwd", grid=(nb,),
        in_specs=[_full((rows, d)), pl.BlockSpec((rows, bn), lambda j: (0, j)), pl.BlockSpec((d, bn), lambda j: (0, j))],
        out_specs=[pl.BlockSpec((d, bn), lambda j: (0, j)), _full((rows, d))],
        out_shape=[jax.ShapeDtypeStruct((d, n), F32), jax.ShapeDtypeStruct((rows, d), F32)],
        compiler_params=_params(("arbitrary",)),
    )(a_in, dm, w_ada)


def _pre_fwd(x2, ctx2, modv, g_attn, w_in, g_q, g_kv, w_uq, w_ukv, cos_t, sin_t, *, seq, tm):
    t_lat, d = x2.shape
    t_ctx = ctx2.shape[0]
    nl, nc = t_lat // tm, t_ctx // tm
    n_all = t_lat + t_ctx
    tpe = seq // tm
    nex = t_lat // seq

    def body(x_ref, c_ref, mod_ref, g_ref, win_ref, gq_ref, gkv_ref, wuq_ref, wukv_ref, cos_ref, sin_ref,
             h_ref, pg_ref, rq_ref, rk_ref, rv_ref, nq_ref, nkv_ref, q_ref, k_ref, v_ref):
        i = pl.program_id(0)
        xt = jnp.where(i < nl, x_ref[...], c_ref[...])
        sh = mod_ref[0, 0:1, :]
        sc = mod_ref[0, 1:2, :]
        r = lax.rsqrt(jnp.mean(xt * xt, axis=-1, keepdims=True) + EPS)
        hb = ((xt * r) * g_ref[...] * (1.0 + sc) + sh).astype(BF16)
        h_ref[...] = hb
        p = _dot(hb, win_ref[...])
        cos = cos_ref[...]
        sin = sin_ref[...]
        rq_ref[...] = _rope(p[:, 0:256], cos, sin).astype(BF16)
        rk_ref[...] = _rope(p[:, 256:512] * (RET_DK ** -0.5), cos, sin).astype(BF16)
        rv_ref[...] = p[:, 512:1024].astype(BF16)
        pg_ref[...] = p[:, 1024:2176]
        cq = p[:, 1536:1920]
        ckv = p[:, 1920:2176]
        nqb = (cq * lax.rsqrt(jnp.mean(cq * cq, axis=-1, keepdims=True) + EPS) * gq_ref[...]).astype(BF16)
        nkvb = (ckv * lax.rsqrt(jnp.mean(ckv * ckv, axis=-1, keepdims=True) + EPS) * gkv_ref[...]).astype(BF16)
        nq_ref[...] = nqb
        nkv_ref[...] = nkvb
        cos1 = cos[:, 0:LANES]
        sin1 = sin[:, 0:LANES]
        kpe = _rope(p[:, 2176:2304], cos1, sin1).astype(BF16)
        for hd in range(HEADS):
            o = hd * MLA_HEAD
            qh = _dot(nqb, wuq_ref[hd])
            q_ref[:, o:o + 128] = qh[:, 0:128].astype(BF16)
            q_ref[:, o + 128:o + 256] = _rope(qh[:, 128:256], cos1, sin1).astype(BF16)
            kvh = _dot(nkvb, wukv_ref[hd])
            k_ref[:, o:o + 128] = kvh[:, 0:128].astype(BF16)
            k_ref[:, o + 128:o + 256] = kpe
            v_ref[:, hd * 128:(hd + 1) * 128] = kvh[:, 128:256].astype(BF16)

    def tile(width):
        return pl.BlockSpec((tm, width), lambda i: (i, 0))

    widths = (d, PG_COLS, 256, 256, 512, Q_LORA, KV_LORA, HEADS * MLA_HEAD, HEADS * MLA_HEAD, HEADS * 128)
    dtypes = (BF16, F32, BF16, BF16, BF16, BF16, BF16, BF16, BF16, BF16)
    tab = pl.BlockSpec((tm, 256), lambda i: (jnp.where(i < nl, i % tpe, tpe), 0))
    return pl.pallas_call(
        body, name="pre_fwd", grid=(nl + nc,),
        in_specs=[
            pl.BlockSpec((tm, d), lambda i: (jnp.minimum(i, nl - 1), 0)),
            pl.BlockSpec((tm, d), lambda i: (jnp.maximum(i - nl, 0), 0)),
            pl.BlockSpec((1, 8, d), lambda i: (jnp.minimum(i // tpe, nex), 0, 0)),
            _full((1, d)), _full(w_in.shape), _full((1, Q_LORA)), _full((1, KV_LORA)),
            _full(w_uq.shape), _full(w_ukv.shape), tab, tab,
        ],
        out_specs=[tile(w) for w in widths],
        out_shape=[jax.ShapeDtypeStruct((n_all, w), dt) for w, dt in zip(widths, dtypes)],
        compiler_params=_params(("parallel",), VMEM_LIMIT),
    )(x2, ctx2, modv, g_attn, w_in, g_q, g_kv, w_uq, w_ukv, cos_t, sin_t)


def _post(yret, ymla, x2, tgt2, modv, g_ffn, g_fin, w_out, w_ff1, w_ff2, *, seq, tm):
    t_lat, d = x2.shape
    nl = t_lat // tm
    tpe = seq // tm
    nex = t_lat // seq
    n_slab = w_ff1.shape[0]
    fs = w_ff1.shape[2]

    def body(yr_ref, ym_ref, x_ref, t_ref, mod_ref, gf_ref, gl_ref, wo_ref, w1_ref, w2_ref,
             mix_ref, a_ref, du_ref, h2_ref, df_ref, dmo_ref, dmix_ref, dxm_ref, st_ref, ru_ref):
        i = pl.program_id(0)
        gt_a = mod_ref[0, 2:3, :]
        sh_f = mod_ref[0, 3:4, :]
        sc_f = mod_ref[0, 4:5, :]
        gt_f = mod_ref[0, 5:6, :]
        g_ffn_v = gf_ref[...]
        g_fin_v = gl_ref[...]
        yr = yr_ref[...]
        ym = ym_ref[...]
        mix_ref[:, 0:512] = yr
        mix_ref[:, 512:1024] = ym
        op = _dot(yr, wo_ref[0:512, :]) + _dot(ym, wo_ref[512:1024, :])
        x_mid = x_ref[...] + gt_a * op
        r2 = lax.rsqrt(jnp.mean(x_mid * x_mid, axis=-1, keepdims=True) + EPS)
        xh2 = x_mid * r2
        h2b = (xh2 * g_ffn_v * (1.0 + sc_f) + sh_f).astype(BF16)
        h2_ref[...] = h2b
        f = jnp.zeros((tm, d), F32)
        for s in range(n_slab):
            ru = jnp.maximum(_dot(h2b, w1_ref[s]), 0.0)
            ru_ref[:, s * fs:(s + 1) * fs] = ru
            ab = (ru * ru).astype(BF16)
            a_ref[:, s * fs:(s + 1) * fs] = ab
            f = f + _dot(ab, w2_ref[s * fs:(s + 1) * fs, :])
        x_out = x_mid + gt_f * f
        r3 = lax.rsqrt(jnp.mean(x_out * x_out, axis=-1, keepdims=True) + EPS)
        xh3 = x_out * r3
        err = xh3 * g_fin_v - t_ref[...]
        dy = err * (1.0 / d)
        dxh3 = dy * g_fin_v
        dx_out = r3 * (dxh3 - xh3 * jnp.mean(dxh3 * xh3, axis=-1, keepdims=True))
        dfb = (dx_out * gt_f).astype(BF16)
        df_ref[...] = dfb
        dh2 = jnp.zeros((tm, d), F32)
        for s in range(n_slab):
            da = _dot_nt(dfb, w2_ref[s * fs:(s + 1) * fs, :])
            dub = (da * (2.0 * ru_ref[:, s * fs:(s + 1) * fs])).astype(BF16)
            du_ref[:, s * fs:(s + 1) * fs] = dub
            dh2 = dh2 + _dot_nt(dub, w1_ref[s])
        dxh2 = dh2 * (1.0 + sc_f) * g_ffn_v
        dx_mid = dx_out + r2 * (dxh2 - xh2 * jnp.mean(dxh2 * xh2, axis=-1, keepdims=True))
        dxm_ref[...] = dx_mid
        dmob = (dx_mid * gt_a).astype(BF16)
        dmo_ref[...] = dmob
        dmix_ref[...] = _dot_nt(dmob, wo_ref[...]).astype(BF16)

        def rsum(v):
            return jnp.sum(v, axis=0, keepdims=True)

        stats = jnp.concatenate([
            rsum(dh2), rsum(dh2 * xh2 * g_ffn_v), rsum(dx_out * f), rsum(dx_mid * op),
            rsum(dh2 * (1.0 + sc_f) * xh2), rsum(dy * xh3), rsum(err * err), jnp.zeros((1, d), F32)], axis=0)

        @pl.when(i % tpe == 0)
        def _():
            st_ref[0] = stats

        @pl.when(i % tpe != 0)
        def _():
            st_ref[0] += stats

    def tile(width):
        return pl.BlockSpec((tm, width), lambda i: (i, 0))

    widths = (d, D_FF, D_FF, d, d, d, d, d)
    dtypes = (BF16, BF16, BF16, BF16, BF16, BF16, BF16, F32)
    const = pl.Buffered(1)
    return pl.pallas_call(
        body, name="post", grid=(nl,),
        in_specs=[
            tile(512), tile(512), tile(d), tile(d),
            pl.BlockSpec((1, 8, d), lambda i: (i // tpe, 0, 0)),
            _full((1, d)), _full((1, d)),
            pl.BlockSpec(w_out.shape, lambda i: (0, 0), pipeline_mode=const),
            pl.BlockSpec(w_ff1.shape, lambda i: (0, 0, 0), pipeline_mode=const),
            pl.BlockSpec(w_ff2.shape, lambda i: (0, 0), pipeline_mode=const),
        ],
        out_specs=[tile(w) for w in widths] + [pl.BlockSpec((1, 8, d), lambda i: (i // tpe, 0, 0))],
        out_shape=[jax.ShapeDtypeStruct((t_lat, w), dt) for w, dt in zip(widths, dtypes)]
        + [jax.ShapeDtypeStruct((nex, 8, d), F32)],
        scratch_shapes=[pltpu.VMEM((tm, D_FF), F32)],
        compiler_params=_params(("arbitrary",), VMEM_LIMIT),
    )(yret, ymla, x2, tgt2, modv, g_ffn, g_fin, w_out, w_ff1, w_ff2)


def _pre_bwd(x2, ctx2, modv, g_attn, pg, drq, drk, dkc_r, drv, dvc_r, drg, dq_m, dkl, dkc, dvl, dvc, dxm,
             w_in, g_q, g_kv, w_uq, w_ukv, cos_t, sin_t, *, seq, tm, rider=None):
    t_lat, d = x2.shape
    t_ctx = ctx2.shape[0]
    nl, nc = t_lat // tm, t_ctx // tm
    n_all = t_lat + t_ctx
    tpe = seq // tm
    nex = t_lat // seq

    def body(x_ref, c_ref, mod_ref, g_ref, pg_ref, drq_ref, drk_ref, dkcr_ref, drv_ref, dvcr_ref, drg_ref,
             dq_ref, dkl_ref, dkc_ref, dvl_ref, dvc_ref, dxm_ref, win_ref, gq_ref, gkv_ref, wuq_ref, wukv_ref,
             cos_ref, sin_ref, dpb_ref, dqf_ref, dkvf_ref, gx_ref, st_ref):
        i = pl.program_id(0)
        lat = i < nl
        latf = lat.astype(F32)
        cos = cos_ref[...]
        sin = sin_ref[...]
        cos1 = cos[:, 0:LANES]
        sin1 = sin[:, 0:LANES]
        d_rq = _rope_t(drq_ref[...] * latf, cos, sin)
        d_rk = _rope_t(jnp.where(lat, drk_ref[...], dkcr_ref[...]), cos, sin) * (RET_DK ** -0.5)
        d_rv = jnp.where(lat, drv_ref[...], dvcr_ref[...])
        d_rg = drg_ref[...] * latf
        dq_all = dq_ref[...] * latf
        dk_all = jnp.where(lat, dkl_ref[...], dkc_ref[...])
        dv_all = jnp.where(lat, dvl_ref[...], dvc_ref[...])
        dnq = jnp.zeros((tm, Q_LORA), F32)
        dnkv = jnp.zeros((tm, KV_LORA), F32)
        dkpe = jnp.zeros((tm, LANES), F32)
        for hd in range(HEADS):
            o = hd * MLA_HEAD
            dqh = jnp.concatenate([dq_all[:, o:o + 128], _rope_t(dq_all[:, o + 128:o + 256], cos1, sin1)],
                                  axis=1).astype(BF16)
            dqf_ref[:, o:o + 256] = dqh
            dnq = dnq + _dot_nt(dqh, wuq_ref[hd])
            dkpe = dkpe + dk_all[:, o + 128:o + 256]
            dkvh = jnp.concatenate([dk_all[:, o:o + 128], dv_all[:, hd * 128:(hd + 1) * 128]], axis=1).astype(BF16)
            dkvf_ref[:, o:o + 256] = dkvh
            dnkv = dnkv + _dot_nt(dkvh, wukv_ref[hd])
        d_kpe = _rope_t(dkpe, cos1, sin1)
        pgv = pg_ref[...]
        cq = pgv[:, 512:896]
        ckv = pgv[:, 896:1152]
        rq_ = lax.rsqrt(jnp.mean(cq * cq, axis=-1, keepdims=True) + EPS)
        cqh = cq * rq_
        dcqh = dnq * gq_ref[...]
        d_cq = rq_ * (dcqh - cqh * jnp.mean(dcqh * cqh, axis=-1, keepdims=True))
        rkv_ = lax.rsqrt(jnp.mean(ckv * ckv, axis=-1, keepdims=True) + EPS)
        ckvh = ckv * rkv_
        dckvh = dnkv * gkv_ref[...]
        d_ckv = rkv_ * (dckvh - ckvh * jnp.mean(dckvh * ckvh, axis=-1, keepdims=True))
        dpb = jnp.concatenate([d_rq, d_rk, d_rv, d_rg, d_cq, d_ckv, d_kpe], axis=1).astype(BF16)
        dpb_ref[...] = dpb
        dh = _dot_nt(dpb, win_ref[...])
        xt = jnp.where(lat, x_ref[...], c_ref[...])
        sc = mod_ref[0, 1:2, :]
        g = g_ref[...]
        r = lax.rsqrt(jnp.mean(xt * xt, axis=-1, keepdims=True) + EPS)
        xh = xt * r
        dxh = dh * (1.0 + sc) * g
        dx = r * (dxh - xh * jnp.mean(dxh * xh, axis=-1, keepdims=True))

        @pl.when(lat)
        def _():
            gx_ref[...] = dxm_ref[...] + dx

        def rsum(v):
            return jnp.sum(v, axis=0, keepdims=True)

        def widen(v):
            return jnp.concatenate([v, jnp.zeros((1, d - v.shape[1]), F32)], axis=1)

        stats = jnp.concatenate([
            rsum(dh), rsum(dh * xh * g), rsum(dh * (1.0 + sc) * xh), widen(rsum(dnq * cqh)), widen(rsum(dnkv * ckvh)),
            jnp.zeros((3, d), F32)], axis=0)
        first = jnp.logical_or(jnp.logical_and(lat, i % tpe == 0), i == nl)

        @pl.when(first)
        def _():
            st_ref[0] = stats

        @pl.when(jnp.logical_not(first))
        def _():
            st_ref[0] += stats

    def lat_tile(width):
        return pl.BlockSpec((tm, width), lambda i: (jnp.minimum(i, nl - 1), 0))

    def ctx_tile(width):
        return pl.BlockSpec((tm, width), lambda i: (jnp.maximum(i - nl, 0), 0))

    def tile(width):
        return pl.BlockSpec((tm, width), lambda i: (i, 0))

    tab = pl.BlockSpec((tm, 256), lambda i: (jnp.where(i < nl, i % tpe, tpe), 0))
    ex = pl.BlockSpec((1, 8, d), lambda i: (jnp.minimum(i // tpe, nex), 0, 0))
    return _hosted_call(
        body, (x2, ctx2, modv, g_attn, pg, drq, drk, dkc_r, drv, dvc_r, drg, dq_m, dkl, dkc, dvl, dvc, dxm,
               w_in, g_q, g_kv, w_uq, w_ukv, cos_t, sin_t), name="pre_bwd", grid=(nl + nc,),
        in_specs=[
            lat_tile(d), ctx_tile(d), ex, _full((1, d)), tile(PG_COLS),
            lat_tile(256), lat_tile(256), ctx_tile(256), lat_tile(512), ctx_tile(512), lat_tile(512),
            lat_tile(1024), lat_tile(1024), ctx_tile(1024), lat_tile(512), ctx_tile(512), lat_tile(d),
            _full(w_in.shape), _full((1, Q_LORA)), _full((1, KV_LORA)), _full(w_uq.shape), _full(w_ukv.shape),
            tab, tab,
        ],
        out_specs=[tile(IN_PAD), tile(1024), tile(1024), lat_tile(d), ex],
        out_shape=[
            jax.ShapeDtypeStruct((n_all, IN_PAD), BF16), jax.ShapeDtypeStruct((n_all, 1024), BF16),
            jax.ShapeDtypeStruct((n_all, 1024), BF16), jax.ShapeDtypeStruct((t_lat, d), F32),
            jax.ShapeDtypeStruct((nex + 1, 8, d), F32),
        ],
        sem=("arbitrary",), rider=rider)


MLA_SCALE = 1.0 / math.sqrt(MLA_NOPE + MLA_ROPE)
KEY_BLOCK = 2048


def _mla_specs(t_lat, seq, ctx_len, tq):
    nqt = seq // tq
    cb = t_lat // ctx_len
    q = pl.BlockSpec((tq, MLA_HEAD), lambda b, h, j: (b * nqt + j, h))
    kl = pl.BlockSpec((seq, MLA_HEAD), lambda b, h, j: (b, h))
    kc = pl.BlockSpec((ctx_len, MLA_HEAD), lambda b, h, j: (cb + b, h))
    vl = pl.BlockSpec((seq, 128), lambda b, h, j: (b, h))
    vc = pl.BlockSpec((ctx_len, 128), lambda b, h, j: (cb + b, h))
    o = pl.BlockSpec((tq, 128), lambda b, h, j: (b * nqt + j, h))
    return q, kl, kc, vl, vc, o


def _mla_fwd(q, k, v, *, t_lat, seq, ctx_len, tq):
    nex = t_lat // seq

    def body(q_ref, kl_ref, kc_ref, vl_ref, vc_ref, o_ref, lse_ref):
        qb = q_ref[...]
        s = _dot_nt(qb, kl_ref[...]) * MLA_SCALE
        sc = _dot_nt(qb, kc_ref[...]) * MLA_SCALE
        m = jnp.maximum(jnp.max(s, axis=-1, keepdims=True), jnp.max(sc, axis=-1, keepdims=True))
        p = jnp.exp(s - m)
        pc = jnp.exp(sc - m)
        total = jnp.sum(p, axis=-1, keepdims=True) + jnp.sum(pc, axis=-1, keepdims=True)
        o = _dot(p.astype(BF16), vl_ref[...]) + _dot(pc.astype(BF16), vc_ref[...])
        o_ref[...] = (o * (1.0 / total)).astype(BF16)
        lse_ref[...] = jnp.broadcast_to(m + jnp.log(total), lse_ref.shape)

    qs, kl, kc, vl, vc, os_ = _mla_specs(t_lat, seq, ctx_len, tq)
    return pl.pallas_call(
        body, name="mla_fwd", grid=(nex, HEADS, seq // tq),
        in_specs=[qs, kl, kc, vl, vc], out_specs=[os_, os_],
        out_shape=[jax.ShapeDtypeStruct((t_lat, HEADS * 128), BF16), jax.ShapeDtypeStruct((t_lat, HEADS * 128), F32)],
        compiler_params=_params(("parallel", "parallel", "arbitrary"), VMEM_LIMIT),
    )(q, k, k, v, v)


def _mla_bwd(q, k, v, ymla, lse, dmix, *, t_lat, seq, ctx_len, tq, rider=None):
    nex = t_lat // seq
    nqt = seq // tq
    t_ctx = nex * ctx_len
    kb = min(KEY_BLOCK, seq)

    def body(q_ref, kl_ref, kc_ref, vl_ref, vc_ref, o_ref, lse_ref, do_ref, dq_ref, dkl_ref, dkc_ref, dvl_ref, dvc_ref):
        j = pl.program_id(2)

        @pl.when(j == 0)
        def _():
            dkl_ref[...] = jnp.zeros(dkl_ref.shape, F32)
            dkc_ref[...] = jnp.zeros(dkc_ref.shape, F32)
            dvl_ref[...] = jnp.zeros(dvl_ref.shape, F32)
            dvc_ref[...] = jnp.zeros(dvc_ref.shape, F32)

        qb = q_ref[...]
        dob = do_ref[...]
        delta = jnp.sum(dob.astype(F32) * o_ref[...].astype(F32), axis=-1, keepdims=True)
        lse_row = lse_ref[:, 0:1]

        def block(k_ref, v_ref, dk_ref, dv_ref, rows):
            kbl = k_ref[rows, :]
            vbl = v_ref[rows, :]
            p = jnp.exp(_dot_nt(qb, kbl) * MLA_SCALE - lse_row)
            ds = (p * (_dot_nt(dob, vbl) - delta) * MLA_SCALE).astype(BF16)
            dk_ref[rows, :] += _dot_tn(ds, qb)
            dv_ref[rows, :] += _dot_tn(p.astype(BF16), dob)
            return _dot(ds, kbl)

        dq = block(kc_ref, vc_ref, dkc_ref, dvc_ref, pl.ds(0, ctx_len))
        for i in range(seq // kb):
            dq = dq + block(kl_ref, vl_ref, dkl_ref, dvl_ref, pl.ds(i * kb, kb))
        dq_ref[...] = dq

    qs, kl, kc, vl, vc, os_ = _mla_specs(t_lat, seq, ctx_len, tq)
    do_spec = pl.BlockSpec((tq, 128), lambda b, h, j: (b * nqt + j, HEADS + h))
    return _hosted_call(
        body, (q, k, k, v, v, ymla, lse, dmix), name="mla_bwd", grid=(nex, HEADS, nqt),
        in_specs=[qs, kl, kc, vl, vc, os_, os_, do_spec],
        out_specs=[
            qs,
            pl.BlockSpec((seq, MLA_HEAD), lambda b, h, j: (b, h)),
            pl.BlockSpec((ctx_len, MLA_HEAD), lambda b, h, j: (b, h)),
            pl.BlockSpec((seq, 128), lambda b, h, j: (b, h)),
            pl.BlockSpec((ctx_len, 128), lambda b, h, j: (b, h)),
        ],
        out_shape=[
            jax.ShapeDtypeStruct((t_lat, HEADS * MLA_HEAD), F32),
            jax.ShapeDtypeStruct((t_lat, HEADS * MLA_HEAD), F32),
            jax.ShapeDtypeStruct((t_ctx, HEADS * MLA_HEAD), F32),
            jax.ShapeDtypeStruct((t_lat, HEADS * 128), F32),
            jax.ShapeDtypeStruct((t_ctx, HEADS * 128), F32),
        ],
        sem=("parallel", "parallel", "arbitrary"), rider=rider)


def _decay_terms(lg, chunk, forward):
    ii = lax.broadcasted_iota(jnp.int32, (chunk, chunk), 0)
    jj = lax.broadcasted_iota(jnp.int32, (chunk, chunk), 1)
    diff = (ii - jj) if forward else (jj - ii)
    dist = jnp.maximum(diff, 0).astype(F32)
    dmat = jnp.where(diff >= 0, jnp.exp(lg * dist), 0.0)
    pos = lax.broadcasted_iota(jnp.int32, (chunk, 1), 0).astype(F32)
    if forward:
        e_q = pos + 1.0
        e_k = (chunk - 1.0) - pos
    else:
        e_q = chunk - pos
        e_k = pos
    wq = jnp.exp(lg * e_q)
    wk = jnp.exp(lg * e_k)
    cd = jnp.exp(jnp.full((1, 1), lg * chunk, F32))
    return dmat, dist, wq, wk, e_q, e_k, cd


def _ctx_weights(lg, ctx_len, forward):
    pos = lax.broadcasted_iota(jnp.int32, (ctx_len, 1), 0).astype(F32)
    e = ((ctx_len - 1.0) - pos) if forward else pos
    return jnp.exp(lg * e), e


def _ret_specs(t_lat, seq, ctx_len):
    cb = t_lat // ctx_len
    qk = pl.BlockSpec((seq, 128), lambda b, h: (b, h // 2))
    v = pl.BlockSpec((seq, 128), lambda b, h: (b, h))
    kc = pl.BlockSpec((ctx_len, 128), lambda b, h: (cb + b, h // 2))
    vc = pl.BlockSpec((ctx_len, 128), lambda b, h: (cb + b, h))
    return qk, v, kc, vc


def _head_mask(h):
    lane = lax.broadcasted_iota(jnp.int32, (1, 128), 1)
    return (lane // RET_DK) == (h % 2)


def _ret_fwd(rq, rk, rv, pg, lg, g_ret, *, t_lat, seq, ctx_len, chunk, rider=None):
    nex = t_lat // seq
    n_chunk = seq // chunk

    def body(q_ref, k_ref, v_ref, kc_ref, vc_ref, rg_ref, lg_ref, g_ref, y_ref, o_ref):
        h = pl.program_id(1)
        hm = _head_mask(h)
        gain = g_ref[...]
        kcm = jnp.where(hm, kc_ref[...].astype(F32), 0.0)
        vcb = vc_ref[...]

        def run(forward):
            lgd = lg_ref[0 if forward else 1, h]
            dmat, _, wq, wk, _, _, cd = _decay_terms(lgd, chunk, forward)
            wc, _ = _ctx_weights(lgd, ctx_len, forward)
            s0 = _dot_tn((kcm * wc).astype(BF16), vcb)

            def step(t, s):
                n = t if forward else n_chunk - 1 - t
                sl = pl.ds(pl.multiple_of(n * chunk, chunk), chunk)
                qm = jnp.where(hm, q_ref[sl, :], jnp.zeros((), BF16))
                kf = jnp.where(hm, k_ref[sl, :].astype(F32), 0.0)
                vb = v_ref[sl, :]
                a = _dot_nt(qm, kf.astype(BF16)) * dmat
                o = _dot(a.astype(BF16), vb) + wq * _dot(qm, s.astype(BF16))
                if forward:
                    o_ref[sl, :] = o
                else:
                    o = o_ref[sl, :] + o
                    o_ref[sl, :] = o
                    mu = jnp.mean(o, axis=-1, keepdims=True)
                    oc = o - mu
                    var = jnp.mean(oc * oc, axis=-1, keepdims=True)
                    on = oc * lax.rsqrt(var + EPS) * gain
                    rg = rg_ref[sl, :]
                    y_ref[sl, :] = (on * (rg / (1.0 + jnp.exp(-rg)))).astype(BF16)
                return cd * s + _dot_tn((kf * wk).astype(BF16), vb)

            lax.fori_loop(0, n_chunk, step, s0)

        run(True)
        run(False)

    qk, v, kc, vc = _ret_specs(t_lat, seq, ctx_len)
    return _hosted_call(
        body, (rq, rk, rv, rk, rv, pg, lg, g_ret), name="ret_fwd", grid=(nex, HEADS),
        in_specs=[qk, qk, v, kc, vc, v, pl.BlockSpec(memory_space=pltpu.SMEM), pl.BlockSpec((1, 128), lambda b, h: (0, h))],
        out_specs=[v, v],
        out_shape=[jax.ShapeDtypeStruct((t_lat, HEADS * RET_DV), BF16), jax.ShapeDtypeStruct((t_lat, HEADS * RET_DV), F32)],
        sem=("parallel", "arbitrary"), rider=rider)


def _ret_bwd(rq, rk, rv, pg, osum, dmix, lg, g_ret, *, t_lat, seq, ctx_len, chunk, rider=None):
    nex = t_lat // seq
    n_chunk = seq // chunk
    t_ctx = nex * ctx_len

    def body(q_ref, k_ref, v_ref, kc_ref, vc_ref, rg_ref, o_ref, dy_ref, lg_ref, g_ref,
             dq_ref, dk_ref, dv_ref, dkc_ref, dvc_ref, drg_ref, st_ref, do_s, s_st):
        h = pl.program_id(1)
        hm = _head_mask(h)
        gain = g_ref[...]
        kcm = jnp.where(hm, kc_ref[...].astype(F32), 0.0)
        vcb = vc_ref[...]

        def norm_step(n, dgain):
            sl = pl.ds(pl.multiple_of(n * chunk, chunk), chunk)
            o = o_ref[sl, :]
            mu = jnp.mean(o, axis=-1, keepdims=True)
            oc = o - mu
            rstd = lax.rsqrt(jnp.mean(oc * oc, axis=-1, keepdims=True) + EPS)
            ohat = oc * rstd
            rg = rg_ref[sl, :]
            sg = 1.0 / (1.0 + jnp.exp(-rg))
            dy = dy_ref[sl, :].astype(F32)
            don = dy * (rg * sg)
            drg_ref[sl, :] = dy * (ohat * gain) * (sg * (1.0 + rg * (1.0 - sg)))
            dohat = don * gain
            do_s[sl, :] = rstd * (dohat - jnp.mean(dohat, axis=-1, keepdims=True)
                                  - ohat * jnp.mean(dohat * ohat, axis=-1, keepdims=True))
            return dgain + jnp.sum(don * ohat, axis=0, keepdims=True)

        dgain = lax.fori_loop(0, n_chunk, norm_step, jnp.zeros((1, 128), F32))

        @pl.when(h % 2 == 0)
        def _():
            dq_ref[...] = jnp.zeros(dq_ref.shape, F32)
            dk_ref[...] = jnp.zeros(dk_ref.shape, F32)
            dkc_ref[...] = jnp.zeros(dkc_ref.shape, F32)

        dv_ref[...] = jnp.zeros(dv_ref.shape, F32)

        def run(forward):
            lgd = lg_ref[0 if forward else 1, h]
            dmat, dist, wq, wk, e_q, e_k, cd = _decay_terms(lgd, chunk, forward)
            wc, e_c = _ctx_weights(lgd, ctx_len, forward)
            s0 = _dot_tn((kcm * wc).astype(BF16), vcb)

            def state_step(t, s):
                n = t if forward else n_chunk - 1 - t
                sl = pl.ds(pl.multiple_of(n * chunk, chunk), chunk)
                s_st[n] = s
                kf = jnp.where(hm, k_ref[sl, :].astype(F32), 0.0)
                return cd * s + _dot_tn((kf * wk).astype(BF16), v_ref[sl, :])

            lax.fori_loop(0, n_chunk, state_step, s0)

            def grad_step(t, carry):
                g_next, dlg = carry
                n = (n_chunk - 1 - t) if forward else t
                sl = pl.ds(pl.multiple_of(n * chunk, chunk), chunk)
                qm = jnp.where(hm, q_ref[sl, :], jnp.zeros((), BF16))
                kf = jnp.where(hm, k_ref[sl, :].astype(F32), 0.0)
                kb = kf.astype(BF16)
                vb = v_ref[sl, :]
                do = do_s[sl, :]
                dob = do.astype(BF16)
                s_n = s_st[n]
                s_nb = s_n.astype(BF16)
                gb = g_next.astype(BF16)
                dk_cross = wk * _dot_nt(vb, gb)
                dv_cross = _dot((kf * wk).astype(BF16), gb)
                a = _dot_nt(qm, kb) * dmat
                da_raw = _dot_nt(dob, vb)
                dab = (da_raw * dmat).astype(BF16)
                ab = a.astype(BF16)
                o_cross = wq * _dot(qm, s_nb)
                dq_ref[sl, :] += _dot(dab, kb) + wq * _dot_nt(dob, s_nb)
                dk_ref[sl, :] += _dot_tn(dab, qm) + dk_cross
                dv_ref[sl, :] += _dot_tn(ab, dob) + dv_cross
                dlg = (dlg + chunk * cd * jnp.sum(g_next * s_n, keepdims=True)
                       + jnp.sum(e_k * jnp.sum(kf * dk_cross, axis=-1, keepdims=True), keepdims=True)
                       + jnp.sum(dist * a * da_raw, keepdims=True)
                       + jnp.sum(e_q * jnp.sum(o_cross * do, axis=-1, keepdims=True), keepdims=True))
                g_new = cd * g_next + _dot_tn((qm.astype(F32) * wq).astype(BF16), dob)
                return g_new, dlg

            ds0, dlg = lax.fori_loop(0, n_chunk, grad_step, (jnp.zeros((128, 128), F32), jnp.zeros((1, 1), F32)))
            ds0b = ds0.astype(BF16)
            dkc_part = wc * _dot_nt(vcb, ds0b)
            dkc_ref[...] += dkc_part
            dvc_part = _dot((kcm * wc).astype(BF16), ds0b)
            dlg = dlg + jnp.sum(e_c * jnp.sum(kcm * dkc_part, axis=-1, keepdims=True), keepdims=True)
            return dvc_part, dlg

        dvc_f, dlg_f = run(True)
        dvc_b, dlg_b = run(False)
        dvc_ref[...] = dvc_f + dvc_b
        st_ref[0] = jnp.concatenate([
            dgain, jnp.broadcast_to(dlg_f, (1, 128)), jnp.broadcast_to(dlg_b, (1, 128)), jnp.zeros((5, 128), F32)], axis=0)

    qk, v, kc, vc = _ret_specs(t_lat, seq, ctx_len)
    dy_spec = v
    return _hosted_call(
        body, (rq, rk, rv, rk, rv, pg, osum, dmix, lg, g_ret), name="ret_bwd", grid=(nex, HEADS),
        in_specs=[qk, qk, v, kc, vc, v, v, dy_spec, pl.BlockSpec(memory_space=pltpu.SMEM),
                  pl.BlockSpec((1, 128), lambda b, h: (0, h))],
        out_specs=[
            qk, qk, v,
            pl.BlockSpec((ctx_len, 128), lambda b, h: (b, h // 2)),
            pl.BlockSpec((ctx_len, 128), lambda b, h: (b, h)),
            v,
            pl.BlockSpec((1, 8, 128), lambda b, h: (b, 0, h)),
        ],
        out_shape=[
            jax.ShapeDtypeStruct((t_lat, 256), F32), jax.ShapeDtypeStruct((t_lat, 256), F32),
            jax.ShapeDtypeStruct((t_lat, 512), F32), jax.ShapeDtypeStruct((t_ctx, 256), F32),
            jax.ShapeDtypeStruct((t_ctx, 512), F32), jax.ShapeDtypeStruct((t_lat, 512), F32),
            jax.ShapeDtypeStruct((nex, 8, 512), F32),
        ],
        scratch_shapes=[pltpu.VMEM((seq, 128), F32), pltpu.VMEM((n_chunk, 128, 128), F32)],
        sem=("parallel", "arbitrary"), rider=rider)


def _matmul_tn(a, b, *, bm, bn, bk, chip_major, name):
    tk, m = a.shape
    n = b.shape[1]

    def body(a_ref, b_ref, o_ref):
        k = pl.program_id(2)
        part = _dot_tn(a_ref[...], b_ref[...])

        @pl.when(k == 0)
        def _():
            o_ref[...] = part

        @pl.when(k > 0)
        def _():
            o_ref[...] += part

    if chip_major:
        out_spec = pl.BlockSpec((None, bm, bn), lambda i, j, k: (j, i, 0))
        out_shape = jax.ShapeDtypeStruct((n // bn, m, bn), F32)
    else:
        out_spec = pl.BlockSpec((bm, bn), lambda i, j, k: (i, j))
        out_shape = jax.ShapeDtypeStruct((m, n), F32)
    return pl.pallas_call(
        body, name=name, grid=(m // bm, n // bn, tk // bk),
        in_specs=[pl.BlockSpec((bk, bm), lambda i, j, k: (k, i)), pl.BlockSpec((bk, bn), lambda i, j, k: (k, j))],
        out_specs=out_spec, out_shape=out_shape,
        compiler_params=_params(("parallel", "parallel", "arbitrary"), VMEM_LIMIT),
    )(a, b)


_LATE = ("w_out", "w_ff1", "w_ff2")
_EARLY = ("w_in", "w_uq", "w_ukv")


def _local_step(x, ctx, tgt, modv, lg, g_attn, g_ffn, g_fin, g_ret, g_q, g_kv, w_in, w_uq, w_ukv, late, place=None,
                *, tm=256, tq=256, chunk=256):
    nex, seq, d = x.shape
    ctx_len = ctx.shape[1]
    t_lat = nex * seq
    x2 = x.reshape(t_lat, d)
    ctx2 = ctx.reshape(nex * ctx_len, d)
    tgt2 = tgt.reshape(t_lat, d)
    cos_t, sin_t = _rope_tables(seq, tm)
    dims = dict(t_lat=t_lat, seq=seq, ctx_len=ctx_len)
    alone = place is None

    hb, pg, rq, rk, rv, nq, nkv, q, k, v = _pre_fwd(x2, ctx2, modv, g_attn, w_in, g_q, g_kv, w_uq, w_ukv, cos_t, sin_t,
                                                    seq=seq, tm=tm)
    (yret, osum), gathered = _ret_fwd(rq, rk, rv, pg, lg, g_ret, chunk=chunk, **dims,
                                      rider=None if alone else _gather_rider(list(late)))
    w_out, w_ff1, w_ff2 = late if alone else gathered
    ymla, lse = _mla_fwd(q, k, v, tq=tq, **dims)
    mix, act, du, h2, df, dmo, dmix, dxm, st_post = _post(yret, ymla, x2, tgt2, modv, g_ffn, g_fin, w_out.reshape(d, d),
                                                         w_ff1, w_ff2.reshape(D_FF, d), seq=seq, tm=tm)
    bk = 512
    g_late = [
        _matmul_tn(mix, dmo, bm=512, bn=1024, bk=bk, chip_major=False, name="gw_out").reshape(N_CHIPS, d // N_CHIPS, d),
        _matmul_tn(h2, du, bm=512, bn=1024, bk=bk, chip_major=True, name="gw_ff1"),
        _matmul_tn(act, df, bm=512, bn=1024, bk=bk, chip_major=False, name="gw_ff2").reshape(N_CHIPS, D_FF // N_CHIPS, d),
    ]
    (dq_m, dkl, dkc, dvl, dvc), got = _mla_bwd(q, k, v, ymla, lse, dmix, tq=tq, **dims,
                                               rider=None if alone else _exchange_rider(g_late))
    if not alone:
        core, slot = place
        part = [_add_half(g, r, core, "add_half_" + n) for g, r, n in zip(g_late, got, _LATE)]
    (drq, drk, drv, dkc_r, dvc_r, drg, st_ret), landed = _ret_bwd(
        rq, rk, rv, pg, osum, dmix, lg, g_ret, chunk=chunk, **dims, rider=None if alone else _scatter_rider(part))
    if not alone:
        mine = [_sum_chips(p, l, slot, "sum_chips_" + n) for p, l, n in zip(part, landed, _LATE)]
    (dpb, dqf, dkvf, gx, st_pre), theirs = _pre_bwd(
        x2, ctx2, modv, g_attn, pg, drq, drk, dkc_r, drv, dvc_r, drg, dq_m, dkl, dkc, dvl, dvc, dxm, w_in, g_q, g_kv,
        w_uq, w_ukv, cos_t, sin_t, seq=seq, tm=tm, rider=None if alone else _swap_rider(mine))
    g_early = [
        _matmul_tn(hb, dpb, bm=512, bn=768, bk=bk, chip_major=False, name="gw_in"),
        _matmul_tn(nq, dqf, bm=Q_LORA, bn=MLA_HEAD, bk=bk, chip_major=True, name="gw_uq"),
        _matmul_tn(nkv, dkvf, bm=KV_LORA, bn=256, bk=bk, chip_major=True, name="gw_ukv"),
    ]
    late_out = g_late if alone else list(zip(mine, theirs))
    return gx.reshape(nex, seq, d), g_early, late_out, st_post, st_ret, st_pre


_ANY = pl.BlockSpec(memory_space=pl.ANY)
_VMEM = pl.BlockSpec(memory_space=pltpu.VMEM)
_OFFSETS = tuple((dx, dy, dc) for dx in (0, 1) for dy in (0, 1) for dc in (0, 1))[1:]
_CHIP_OFFSETS = ((1, 0), (0, 1), (1, 1))


def _place():
    return lax.axis_index("x"), lax.axis_index("y"), lax.axis_index("c")


def _flip(v, d):
    return 1 - v if d else v


def _allgather8(a, name):
    r, c = a.shape

    def body(a_ref, o_ref, send, recv, lsem):
        x, y, z = _place()
        me = 4 * x + 2 * y + z
        mine = pltpu.make_async_copy(a_ref, o_ref.at[me], lsem)
        mine.start()
        copies = []
        for k, (dx, dy, dc) in enumerate(_OFFSETS):
            cp = pltpu.make_async_remote_copy(
                src_ref=a_ref, dst_ref=o_ref.at[me], send_sem=send.at[k], recv_sem=recv.at[k],
                device_id=(_flip(x, dx), _flip(y, dy), _flip(z, dc)), device_id_type=MESH)
            cp.start()
            copies.append(cp)
        for k, (dx, dy, dc) in enumerate(_OFFSETS):
            peer = 4 * _flip(x, dx) + 2 * _flip(y, dy) + _flip(z, dc)
            pltpu.make_async_remote_copy(
                src_ref=a_ref, dst_ref=o_ref.at[peer], send_sem=send.at[k], recv_sem=recv.at[k],
                device_id=(_flip(x, dx), _flip(y, dy), _flip(z, dc)), device_id_type=MESH).wait_recv()
        for cp in copies:
            cp.wait_send()
        mine.wait()

    return pl.pallas_call(
        body, name=name, in_specs=[_VMEM], out_specs=_VMEM,
        out_shape=jax.ShapeDtypeStruct((N_DEV, r, c), a.dtype),
        scratch_shapes=[pltpu.SemaphoreType.DMA((7,)), pltpu.SemaphoreType.DMA((7,)), pltpu.SemaphoreType.DMA],
    )(a)


def _gather_send(o_refs, send, recv):
    x, y, z = _place()
    chip = 2 * x + y
    for a, o in enumerate(o_refs):
        r2 = o.shape[1] // 2
        mine = o.at[chip, pl.ds(z * r2, r2)]
        for k, (dx, dy) in enumerate(_CHIP_OFFSETS):
            pltpu.make_async_remote_copy(
                src_ref=mine, dst_ref=mine, send_sem=send.at[a, k], recv_sem=recv.at[a, k],
                device_id=(_flip(x, dx), _flip(y, dy), z), device_id_type=MESH).start()


def _gather_finish(o_refs, send, recv, fsend, frecv):
    x, y, z = _place()
    chip = 2 * x + y
    sib = (x, y, 1 - z)
    passed = []
    for a, o in enumerate(o_refs):
        r2 = o.shape[1] // 2
        for k, (dx, dy) in enumerate(_CHIP_OFFSETS):
            other = 2 * _flip(x, dx) + _flip(y, dy)
            landed = o.at[other, pl.ds(z * r2, r2)]
            pltpu.make_async_remote_copy(
                src_ref=landed, dst_ref=landed, send_sem=send.at[a, k], recv_sem=recv.at[a, k],
                device_id=(_flip(x, dx), _flip(y, dy), z), device_id_type=MESH).wait_recv()
            cp = pltpu.make_async_remote_copy(
                src_ref=landed, dst_ref=landed, send_sem=fsend.at[a, k], recv_sem=frecv.at[a, k],
                device_id=sib, device_id_type=MESH)
            cp.start()
            passed.append(cp)
    for a, o in enumerate(o_refs):
        r2 = o.shape[1] // 2
        mine = o.at[chip, pl.ds(z * r2, r2)]
        for k, (dx, dy) in enumerate(_CHIP_OFFSETS):
            other = 2 * _flip(x, dx) + _flip(y, dy)
            got = o.at[other, pl.ds((1 - z) * r2, r2)]
            pltpu.make_async_remote_copy(
                src_ref=got, dst_ref=got, send_sem=fsend.at[a, k], recv_sem=frecv.at[a, k],
                device_id=sib, device_id_type=MESH).wait_recv()
            pltpu.make_async_remote_copy(
                src_ref=mine, dst_ref=mine, send_sem=send.at[a, k], recv_sem=recv.at[a, k],
                device_id=(_flip(x, dx), _flip(y, dy), z), device_id_type=MESH).wait_send()
    for cp in passed:
        cp.wait_send()


class _Rider:
    def __init__(self, ins, out_shapes, sems, start, finish, aliases=None):
        self.ins, self.out_shapes, self.sems = list(ins), list(out_shapes), list(sems)
        self.start, self.finish, self.aliases = start, finish, dict(aliases or {})


def _run_rider(rider, name):
    r_in, r_out = len(rider.ins), len(rider.out_shapes)

    def body(*refs):
        ins, outs, sems = refs[:r_in], refs[r_in:r_in + r_out], refs[r_in + r_out:]
        rider.start(ins, outs, sems)
        rider.finish(ins, outs, sems)

    return pl.pallas_call(
        body, name=name, in_specs=[_ANY] * r_in, out_specs=[_ANY] * r_out, out_shape=rider.out_shapes,
        input_output_aliases=rider.aliases, scratch_shapes=rider.sems,
    )(*rider.ins)


def _hosted_call(body, args, *, name, grid, in_specs, out_specs, out_shape, scratch_shapes=(), sem, rider=None):
    scratch_shapes = list(scratch_shapes)
    if rider is None:
        res = pl.pallas_call(
            body, name=name, grid=grid, in_specs=in_specs, out_specs=out_specs, out_shape=out_shape,
            scratch_shapes=scratch_shapes, compiler_params=_params(sem, VMEM_LIMIT))(*args)
        return list(res), []
    n_in, n_out, n_sc = len(in_specs), len(out_specs), len(scratch_shapes)
    r_in, r_out = len(rider.ins), len(rider.out_shapes)
    last = tuple(g - 1 for g in grid)

    def hosted(*refs):
        p = 0
        parts = []
        for cnt in (n_in, r_in, n_out, r_out, n_sc):
            parts.append(refs[p:p + cnt])
            p += cnt
        ins, r_ins, outs, r_outs, scratch = parts
        sems = refs[p:]
        ids = [pl.program_id(a) for a in range(len(grid))]
        is_first = functools.reduce(jnp.logical_and, [i == 0 for i in ids])
        is_last = functools.reduce(jnp.logical_and, [i == e for i, e in zip(ids, last)])

        @pl.when(is_first)
        def _():
            rider.start(r_ins, r_outs, sems)

        body(*ins, *outs, *scratch)

        @pl.when(is_last)
        def _():
            rider.finish(r_ins, r_outs, sems)

    res = pl.pallas_call(
        hosted, name=name, grid=grid, in_specs=list(in_specs) + [_ANY] * r_in, out_specs=list(out_specs) + [_ANY] * r_out,
        out_shape=list(out_shape) + rider.out_shapes, scratch_shapes=scratch_shapes + rider.sems,
        input_output_aliases={n_in + i: n_out + j for i, j in rider.aliases.items()},
        compiler_params=_params(("arbitrary",) * len(grid), VMEM_LIMIT))(*args, *rider.ins)
    return list(res[:n_out]), list(res[n_out:])


def _gather_rider(ws):
    n = len(ws)
    return _Rider(
        ws, [jax.ShapeDtypeStruct(w.shape, w.dtype) for w in ws], [pltpu.SemaphoreType.DMA((n, 3))] * 4,
        lambda ins, outs, sems: _gather_send(outs, sems[0], sems[1]),
        lambda ins, outs, sems: _gather_finish(outs, *sems),
        aliases={a: a for a in range(n)})


def _copies_rider(ins, out_shapes, sem_shape, make):
    def start(r_ins, r_outs, sems):
        for cp in make(r_ins, r_outs, sems[0], sems[1]):
            cp.start()

    def finish(r_ins, r_outs, sems):
        for cp in make(r_ins, r_outs, sems[0], sems[1]):
            cp.wait()

    return _Rider(ins, out_shapes, [pltpu.SemaphoreType.DMA(sem_shape)] * 2, start, finish)


def _exchange_rider(gs):
    def make(g_refs, r_refs, send, recv):
        x, y, z = _place()
        return [pltpu.make_async_remote_copy(
            src_ref=g.at[:, pl.ds((1 - z) * (g.shape[1] // 2), g.shape[1] // 2)], dst_ref=r, send_sem=send.at[a],
            recv_sem=recv.at[a], device_id=(x, y, 1 - z), device_id_type=MESH)
            for a, (g, r) in enumerate(zip(g_refs, r_refs))]

    shapes = [jax.ShapeDtypeStruct((g.shape[0], g.shape[1] // 2, g.shape[2]), g.dtype) for g in gs]
    return _copies_rider(gs, shapes, (len(gs),), make)


def _add_half(g, recv, core, name):
    s, r, c = g.shape
    r2 = r // 2
    rb = r2
    for cand in (256, 128, 64):
        if r2 % cand == 0:
            rb = cand
            break
    g4 = g.reshape(s, 2, r2, c)

    def body(core_ref, g_ref, r_ref, o_ref):
        o_ref[...] = (g_ref[...] + r_ref[...]).astype(BF16)

    return pl.pallas_call(
        body, name=name,
        grid_spec=pltpu.PrefetchScalarGridSpec(
            num_scalar_prefetch=1, grid=(s, r2 // rb),
            in_specs=[pl.BlockSpec((None, None, rb, c), lambda i, j, cr: (i, cr[0], j, 0)),
                      pl.BlockSpec((None, rb, c), lambda i, j, cr: (i, j, 0))],
            out_specs=pl.BlockSpec((None, rb, c), lambda i, j, cr: (i, j, 0))),
        out_shape=jax.ShapeDtypeStruct((s, r2, c), BF16),
        compiler_params=_params(("parallel", "parallel")),
    )(core, g4, recv)


def _scatter_rider(ps):
    def make(p_refs, o_refs, send, recv):
        x, y, z = _place()
        copies = []
        for a, (p, o) in enumerate(zip(p_refs, o_refs)):
            for k, (dx, dy) in enumerate(_CHIP_OFFSETS):
                other = 2 * _flip(x, dx) + _flip(y, dy)
                copies.append(pltpu.make_async_remote_copy(
                    src_ref=p.at[other], dst_ref=o.at[k], send_sem=send.at[a, k], recv_sem=recv.at[a, k],
                    device_id=(_flip(x, dx), _flip(y, dy), z), device_id_type=MESH))
        return copies

    shapes = [jax.ShapeDtypeStruct((3,) + p.shape[1:], p.dtype) for p in ps]
    return _copies_rider(ps, shapes, (len(ps), 3), make)


def _sum_chips(p, landed, chip, name):
    _, r2, c = p.shape
    rb = r2
    for cand in (256, 128, 64):
        if r2 % cand == 0:
            rb = cand
            break

    def body(s_ref, p_ref, l_ref, o_ref):
        acc = p_ref[...].astype(F32)
        for k in range(3):
            acc = acc + l_ref[k].astype(F32)
        o_ref[...] = acc

    return pl.pallas_call(
        body, name=name,
        grid_spec=pltpu.PrefetchScalarGridSpec(
            num_scalar_prefetch=1, grid=(r2 // rb,),
            in_specs=[pl.BlockSpec((None, rb, c), lambda i, s: (s[0], i, 0)),
                      pl.BlockSpec((3, rb, c), lambda i, s: (0, i, 0))],
            out_specs=pl.BlockSpec((rb, c), lambda i, s: (i, 0))),
        out_shape=jax.ShapeDtypeStruct((r2, c), F32),
        compiler_params=_params(("parallel",)),
    )(chip, p, landed)


def _swap_rider(hs):
    def make(h_refs, o_refs, send, recv):
        x, y, z = _place()
        return [pltpu.make_async_remote_copy(
            src_ref=h, dst_ref=o, send_sem=send.at[a], recv_sem=recv.at[a], device_id=(x, y, 1 - z),
            device_id_type=MESH) for a, (h, o) in enumerate(zip(h_refs, o_refs))]

    return _copies_rider(hs, [jax.ShapeDtypeStruct(h.shape, h.dtype) for h in hs], (len(hs),), make)


SMALL_ROWS = 32
PACK_ROWS = 16


def _pack_small(st_post, st_ret, st_pre):
    d = st_post.shape[2]

    def body(po_ref, re_ref, pr_ref, o_ref):
        o_ref[...] = jnp.zeros(o_ref.shape, F32)
        o_ref[0:1, :] = pr_ref[0, 2:3, :] + pr_ref[1, 2:3, :] + pr_ref[2, 2:3, :]
        o_ref[1:2, :] = po_ref[0, 4:5, :] + po_ref[1, 4:5, :]
        o_ref[2:3, :] = po_ref[0, 5:6, :] + po_ref[1, 5:6, :]
        o_ref[3:4, 0:512] = re_ref[0, 0:1, :] + re_ref[1, 0:1, :]
        o_ref[4:5, :] = pr_ref[0, 3:4, :] + pr_ref[1, 3:4, :] + pr_ref[2, 3:4, :]
        o_ref[5:6, :] = pr_ref[0, 4:5, :] + pr_ref[1, 4:5, :] + pr_ref[2, 4:5, :]
        lane = lax.broadcasted_iota(jnp.int32, (1, LANES), 1)
        for row, src in ((6, 1), (10, 2)):
            acc = jnp.zeros((1, LANES), F32)
            for hd in range(HEADS):
                grp = re_ref[0, src:src + 1, hd * LANES:(hd + 1) * LANES] + re_ref[1, src:src + 1, hd * LANES:(hd + 1) * LANES]
                acc = acc + jnp.where(lane == hd, grp, 0.0)
            o_ref[row:row + 1, 0:LANES] = acc
        o_ref[7:8, :] = po_ref[0, 6:7, :] + po_ref[1, 6:7, :]
        o_ref[8:9, :] = pr_ref[2, 0:1, :]
        o_ref[9:10, :] = pr_ref[2, 1:2, :]
        for e in range(2):
            b = 12 + 6 * e
            o_ref[b:b + 1, :] = pr_ref[e, 0:1, :]
            o_ref[b + 1:b + 2, :] = pr_ref[e, 1:2, :]
            o_ref[b + 2:b + 3, :] = po_ref[e, 3:4, :]
            o_ref[b + 3:b + 4, :] = po_ref[e, 0:1, :]
            o_ref[b + 4:b + 5, :] = po_ref[e, 1:2, :]
            o_ref[b + 5:b + 6, :] = po_ref[e, 2:3, :]

    return pl.pallas_call(body, name="pack_small", out_shape=jax.ShapeDtypeStruct((SMALL_ROWS, d), F32))(st_post, st_ret, st_pre)


def _small_reduce(gathered):
    d = gathered.shape[2]

    def body(g_ref, o_ref):
        tot = g_ref[0, 0:PACK_ROWS, :]
        for dev in range(1, N_DEV):
            tot = tot + g_ref[dev, 0:PACK_ROWS, :]
        o_ref[0:PACK_ROWS, :] = tot
        for j in range(6):
            acc = g_ref[0, 12 + j:13 + j, :] + g_ref[0, 18 + j:19 + j, :]
            for dev in range(1, N_DEV):
                acc = acc + g_ref[dev, 12 + j:13 + j, :] + g_ref[dev, 18 + j:19 + j, :]
            if j < 2:
                acc = acc + o_ref[8 + j:9 + j, :]
            o_ref[PACK_ROWS + j:PACK_ROWS + j + 1, :] = acc
        o_ref[PACK_ROWS + 6:PACK_ROWS + 8, :] = jnp.zeros((2, d), F32)

    return pl.pallas_call(body, name="small_reduce", out_shape=jax.ShapeDtypeStruct((PACK_ROWS + 8, d), F32))(gathered)


_SMALL = (("g_attn", 0, 1024), ("g_ffn", 1, 1024), ("g_final", 2, 1024), ("g_ret", 3, 512), ("g_q_lora", 4, 384),
          ("g_kv_lora", 5, 256), ("ret_decay_fwd", 6, HEADS), ("ret_decay_bwd", 10, HEADS))
_SMALL_NAMES = tuple(s[0] for s in _SMALL) + ("c_ctx", "b_ada")


def _small_final(tot, dcc, sg8, ws, ms, vs):
    d = tot.shape[1]
    n = len(_SMALL_NAMES)

    def body(*refs):
        t_ref, dcc_ref, sg_ref = refs[0:3]
        w_refs, m_refs, v_refs = refs[3:3 + n], refs[3 + n:3 + 2 * n], refs[3 + 2 * n:3 + 3 * n]
        outs = refs[3 + 3 * n:]
        g_refs, d_refs, mo_refs, vo_refs = outs[0:n], outs[n:2 * n], outs[2 * n:3 * n], outs[3 * n:4 * n]
        l_ref = outs[4 * n]

        def update(i, g, sl=None):
            pick = (lambda r: r[...]) if sl is None else (lambda r: r[:, sl])
            dl, mn, vn = _adam_math(pick(w_refs[i]), g, pick(m_refs[i]), pick(v_refs[i]))
            if sl is None:
                g_refs[i][...], d_refs[i][...], mo_refs[i][...], vo_refs[i][...] = g, dl, mn, vn
            else:
                g_refs[i][:, sl], d_refs[i][:, sl], mo_refs[i][:, sl], vo_refs[i][:, sl] = g, dl, mn, vn

        for i, (name, row, width) in enumerate(_SMALL):
            g = t_ref[row:row + 1, 0:width]
            if name == "ret_decay_fwd":
                g = g * sg_ref[0:1, 0:width]
            elif name == "ret_decay_bwd":
                g = g * sg_ref[1:2, 0:width]
            update(i, g)
        i_cc, i_b = n - 2, n - 1
        cc = w_refs[i_cc][...]
        s = 1.0 / (1.0 + jnp.exp(-cc))
        dsilu = dcc_ref[0, 0:1, :] + dcc_ref[2, 0:1, :] + dcc_ref[4, 0:1, :] + dcc_ref[6, 0:1, :]
        update(i_cc, dsilu * (s * (1.0 + cc * (1.0 - s))))
        for j in range(6):
            update(i_b, t_ref[PACK_ROWS + j:PACK_ROWS + j + 1, :], pl.ds(j * d, d))
        l_ref[...] = jnp.broadcast_to((0.5 / d) * jnp.sum(t_ref[7:8, :], keepdims=True), l_ref.shape)

    shapes = [jax.ShapeDtypeStruct(a.shape, F32) for a in ws]
    outs = pl.pallas_call(
        body, name="small_final", out_shape=shapes * 4 + [jax.ShapeDtypeStruct((8, LANES), F32)],
    )(tot, dcc, sg8, *ws, *ms, *vs)
    return outs[0:n], outs[n:2 * n], outs[2 * n:3 * n], outs[3 * n:4 * n], outs[4 * n]


_WEIGHTS = ("c_ctx", "w_ada", "b_ada", "g_attn", "g_ffn", "w_in", "ret_decay_fwd", "ret_decay_bwd", "g_ret", "g_q_lora",
            "w_uq", "g_kv_lora", "w_ukv", "w_out", "w_ff1", "w_ff2", "g_final")
_BIG = ("w_in", "w_uq", "w_ukv", "w_out", "w_ff1", "w_ff2")


def kernel(x, c, ctx, c_ctx, w_ada, b_ada, g_attn, g_ffn, w_in, ret_decay_fwd, ret_decay_bwd, g_ret, g_q_lora, w_uq, g_kv_lora, w_ukv, w_out, w_ff1, w_ff2, g_final, loss_target, m_c_ctx, m_w_ada, m_b_ada, m_g_attn, m_g_ffn, m_w_in, m_ret_decay_fwd, m_ret_decay_bwd, m_g_ret, m_g_q_lora, m_w_uq, m_g_kv_lora, m_w_ukv, m_w_out, m_w_ff1, m_w_ff2, m_g_final, v_c_ctx, v_w_ada, v_b_ada, v_g_attn, v_g_ffn, v_w_in, v_ret_decay_fwd, v_ret_decay_bwd, v_g_ret, v_g_q_lora, v_w_uq, v_g_kv_lora, v_w_ukv, v_w_out, v_w_ff1, v_w_ff2, v_g_final):
    w = dict(c_ctx=c_ctx, w_ada=w_ada, b_ada=b_ada, g_attn=g_attn, g_ffn=g_ffn, w_in=w_in, ret_decay_fwd=ret_decay_fwd,
             ret_decay_bwd=ret_decay_bwd, g_ret=g_ret, g_q_lora=g_q_lora, w_uq=w_uq, g_kv_lora=g_kv_lora, w_ukv=w_ukv,
             w_out=w_out, w_ff1=w_ff1, w_ff2=w_ff2, g_final=g_final)
    m = dict(c_ctx=m_c_ctx, w_ada=m_w_ada, b_ada=m_b_ada, g_attn=m_g_attn, g_ffn=m_g_ffn, w_in=m_w_in,
             ret_decay_fwd=m_ret_decay_fwd, ret_decay_bwd=m_ret_decay_bwd, g_ret=m_g_ret, g_q_lora=m_g_q_lora, w_uq=m_w_uq,
             g_kv_lora=m_g_kv_lora, w_ukv=m_w_ukv, w_out=m_w_out, w_ff1=m_w_ff1, w_ff2=m_w_ff2, g_final=m_g_final)
    v = dict(c_ctx=v_c_ctx, w_ada=v_w_ada, b_ada=v_b_ada, g_attn=v_g_attn, g_ffn=v_g_ffn, w_in=v_w_in,
             ret_decay_fwd=v_ret_decay_fwd, ret_decay_bwd=v_ret_decay_bwd, g_ret=v_g_ret, g_q_lora=v_g_q_lora, w_uq=v_w_uq,
             g_kv_lora=v_g_kv_lora, w_ukv=v_w_ukv, w_out=v_w_out, w_ff1=v_w_ff1, w_ff2=v_w_ff2, g_final=v_g_final)
    xi, yi, ci = lax.axis_index("x"), lax.axis_index("y"), lax.axis_index("c")
    chip = 2 * xi + yi
    dev = 2 * chip + ci
    nex, seq, d = x.shape
    n_ada = w_ada.shape[2]

    c_all = _allgather8(jnp.pad(c, ((0, 8 - nex), (0, 0))), "ag_c")[:, :nex].reshape(N_DEV * nex, d)
    a_in = jnp.concatenate([c_all, c_ctx.reshape(1, d), jnp.zeros((7, d), F32)], axis=0)
    b_sh = lax.dynamic_slice(b_ada, (0, chip * n_ada), (1, n_ada))
    mod_sh = _mod_fwd(a_in, w_ada[0], b_sh)
    mod_all = _allgather8(mod_sh, "ag_mod")[0::2].transpose(1, 0, 2).reshape(a_in.shape[0], N_CHIPS * n_ada)
    mod_me = lax.dynamic_slice(mod_all, (nex * dev, 0), (nex, N_CHIPS * n_ada)).reshape(nex, 6, d)
    mod_c = mod_all[N_DEV * nex].reshape(1, 6, d)
    modv = jnp.pad(jnp.concatenate([mod_me, mod_c], axis=0), ((0, 0), (0, 2), (0, 0)))

    dec = jnp.zeros((8, LANES), F32).at[0, :HEADS].set(ret_decay_fwd[0]).at[1, :HEADS].set(ret_decay_bwd[0])
    lg8, sg8 = _decay_prep(dec)
    lg = lg8[:2, :HEADS]

    shard = {k: w[k][0] for k in _BIG}
    shard["w_uq"] = jnp.pad(shard["w_uq"], ((0, 0), (0, MLA_HEAD - MLA_NOPE - MLA_ROPE)))
    slot = chip.reshape(1).astype(jnp.int32)
    core = ci.reshape(1).astype(jnp.int32)
    slots = {k: _cast_into_slot(shard[k], slot, "cast_" + k) for k in _BIG}
    w_in_f, w_uq_k, w_ukv_k = _run_rider(_gather_rider([slots[k] for k in _EARLY]), "ag_early")
    w_in_k = jnp.pad(w_in_f.transpose(1, 0, 2).reshape(d, IN_COLS), ((0, 0), (0, IN_PAD - IN_COLS)))

    gx, g_early, late, st_post, st_ret, st_pre = _local_step(
        x, ctx, loss_target, modv, lg, g_attn, g_ffn, g_final.reshape(1, d), g_ret, g_q_lora, g_kv_lora,
        w_in_k, w_uq_k, w_ukv_k, [slots[k] for k in _LATE], (core, slot))

    per_chip = IN_COLS // N_CHIPS
    g4 = [
        g_early[0][:, :IN_COLS].reshape(d, N_CHIPS, per_chip).transpose(1, 0, 2),
        g_early[1][:, :, :MLA_NOPE + MLA_ROPE],
        g_early[2],
    ]
    got = _run_rider(_exchange_rider(g4), "rs_exchange")
    partial = [_add_half(g, r, core, "add_half_" + k) for g, r, k in zip(g4, got, _EARLY)]
    landed = _run_rider(_scatter_rider(partial), "rs_scatter")
    mine = [_sum_chips(p, l, slot, "sum_chips_" + k) for p, l, k in zip(partial, landed, _EARLY)]
    theirs = _run_rider(_swap_rider(mine), "rs_swap")
    halves = dict(zip(_EARLY, zip(mine, theirs)))
    halves.update(zip(_LATE, late))
    grad, delta, new_m, new_v = {}, {}, {}, {}
    for k in _BIG:
        a, b = halves[k]
        shp = w[k].shape
        outs = _adamw_halves(w[k].reshape(shp[1:]), a, b, m[k].reshape(shp[1:]), v[k].reshape(shp[1:]), core, "adamw_" + k)
        grad[k], delta[k], new_m[k], new_v[k] = [o.reshape(shp) for o in outs]

    gathered = _allgather8(_pack_small(st_post, st_ret, st_pre), "ag_small")
    tot = _small_reduce(gathered)
    dm = jnp.concatenate([
        gathered[:, 12:24].reshape(N_DEV * nex, 6 * d),
        jnp.concatenate([tot[8:10].reshape(1, 2 * d), jnp.zeros((1, 4 * d), F32)], axis=1),
        jnp.zeros((7, 6 * d), F32)], axis=0)
    dm_sh = lax.dynamic_slice(dm, (0, chip * n_ada), (dm.shape[0], n_ada))
    g_ada, da = _mod_bwd(a_in, dm_sh, w_ada[0])
    dcc = _allgather8(da[N_DEV * nex:], "ag_dcc")
    shp = w_ada.shape
    outs = _adamw(w_ada[0], g_ada, m["w_ada"][0], v["w_ada"][0], "adamw_w_ada")
    grad["w_ada"] = g_ada.reshape(shp)
    delta["w_ada"], new_m["w_ada"], new_v["w_ada"] = [o.reshape(shp) for o in outs]
    rows = [{k: t[k].reshape(1, -1) for k in _SMALL_NAMES} for t in (w, m, v)]
    small = _small_final(tot, dcc, sg8, *[[t[k] for k in _SMALL_NAMES] for t in rows])
    for res, outs in zip((grad, delta, new_m, new_v), small[:4]):
        for k, o in zip(_SMALL_NAMES, outs):
            res[k] = o.reshape(w[k].shape)
    return (small[4][0, 0], gx, *[grad[k] for k in _WEIGHTS], *[delta[k] for k in _WEIGHTS],
            *[new_m[k] for k in _WEIGHTS], *[new_v[k] for k in _WEIGHTS])
```

```python
import functools
import math

import jax
import jax.numpy as jnp
from jax import lax
from jax.experimental import pallas as pl
from jax.experimental.pallas import tpu as pltpu

F32 = jnp.float32
BF16 = jnp.bfloat16
MESH = pl.DeviceIdType.MESH

EPS = 1e-6
D_MODEL = 1024
D_FF = 4096
HEADS = 4
RET_DK = 64
RET_DV = 128
MLA_NOPE = 128
MLA_ROPE = 64
MLA_HEAD = 256
Q_LORA = 384
KV_LORA = 256
GRID_W = 64
ROPE_BASE = 10000.0
IN_COLS = 2240
IN_PAD = 2304
PG_COLS = 1152
N_CHIPS = 4
N_DEV = 8
LANES = 128
ADAM_LR = 0.001
ADAM_B1 = 0.9
ADAM_B2 = 0.999
ADAM_EPS = 1e-08
ADAM_WD = 0.01
ADAM_STEP = 10
VMEM_LIMIT = 56 * 1024 * 1024


def _dot(a, b):
    return jnp.dot(a, b, preferred_element_type=F32)


def _dot_nt(a, b):
    return lax.dot_general(a, b, (((1,), (1,)), ((), ())), preferred_element_type=F32)


def _dot_tn(a, b):
    return lax.dot_general(a, b, (((0,), (0,)), ((), ())), preferred_element_type=F32)


def _params(sem=None, vmem=None):
    return pltpu.CompilerParams(dimension_semantics=sem, vmem_limit_bytes=vmem)


def _full(shape):
    n = len(shape)
    return pl.BlockSpec(shape, lambda *_: (0,) * n)


def _rope(x, cos, sin):
    w = x.shape[-1]
    lo = (lax.broadcasted_iota(jnp.int32, (1, w), 1) % 64) < 32
    swapped = jnp.where(lo, pltpu.roll(x, w - 32, 1), pltpu.roll(x, 32, 1))
    return x * cos + swapped * sin


def _rope_t(g, cos, sin):
    w = g.shape[-1]
    lo = (lax.broadcasted_iota(jnp.int32, (1, w), 1) % 64) < 32
    t = g * sin
    swapped = jnp.where(lo, pltpu.roll(t, w - 32, 1), pltpu.roll(t, 32, 1))
    return g * cos + swapped


def _rope_tables(seq, tm):
    rows = seq // GRID_W
    row = jnp.repeat(jnp.arange(rows, dtype=F32), GRID_W)
    col = jnp.tile(jnp.arange(GRID_W, dtype=F32), rows)
    n_freq = RET_DK // 4
    freq = ROPE_BASE ** (-jnp.arange(n_freq, dtype=F32) / n_freq)
    ang = jnp.concatenate([row[:, None] * freq, col[:, None] * freq], axis=-1)
    cos, sin = jnp.cos(ang), jnp.sin(ang)
    cos_t = jnp.tile(jnp.concatenate([cos, cos], -1), (1, HEADS))
    sin_t = jnp.tile(jnp.concatenate([-sin, sin], -1), (1, HEADS))
    cos_t = jnp.concatenate([cos_t, jnp.ones((tm, 4 * RET_DK), F32)], 0)
    sin_t = jnp.concatenate([sin_t, jnp.zeros((tm, 4 * RET_DK), F32)], 0)
    return cos_t, sin_t


def _adam_math(w, g, m, v):
    mn = ADAM_B1 * m + (1.0 - ADAM_B1) * g
    vn = ADAM_B2 * v + (1.0 - ADAM_B2) * (g * g)
    m_hat = mn / (1.0 - ADAM_B1 ** ADAM_STEP)
    v_hat = vn / (1.0 - ADAM_B2 ** ADAM_STEP)
    return -ADAM_LR * (m_hat / (jnp.sqrt(v_hat) + ADAM_EPS) + ADAM_WD * w), mn, vn


def _cast_into_slot(w, slot, name):
    r, c = w.shape
    rb = math.gcd(r, 256)

    def body(s_ref, w_ref, o_ref):
        o_ref[...] = w_ref[...].astype(BF16)

    return pl.pallas_call(
        body, name=name,
        grid_spec=pltpu.PrefetchScalarGridSpec(
            num_scalar_prefetch=1, grid=(r // rb,),
            in_specs=[pl.BlockSpec((rb, c), lambda i, s: (i, 0))],
            out_specs=pl.BlockSpec((None, rb, c), lambda i, s: (s[0], i, 0))),
        out_shape=jax.ShapeDtypeStruct((N_CHIPS, r, c), BF16),
        compiler_params=_params(("parallel",)),
    )(slot, w)


def _adamw_halves(w, mine, theirs, m, v, core, name):
    r, c = w.shape
    r2 = r // 2
    rb = 8
    for cand in (256, 128, 64, 32, 16):
        if r2 % cand == 0 and cand * c * 4 <= (1 << 20):
            rb = cand
            break
    nbh = r2 // rb

    def body(z_ref, w_ref, a_ref, b_ref, m_ref, v_ref, g_ref, d_ref, mo_ref, vo_ref):
        here = (pl.program_id(0) // nbh) == z_ref[0]
        gg = jnp.where(here, a_ref[...], b_ref[...])
        g_ref[...] = gg
        d_ref[...], mo_ref[...], vo_ref[...] = _adam_math(w_ref[...], gg, m_ref[...], v_ref[...])

    spec = pl.BlockSpec((rb, c), lambda i, z: (i, 0))
    a_spec = pl.BlockSpec((rb, c), lambda i, z: (jnp.clip(i - z[0] * nbh, 0, nbh - 1), 0))
    b_spec = pl.BlockSpec((rb, c), lambda i, z: (jnp.clip(i - (1 - z[0]) * nbh, 0, nbh - 1), 0))
    shp = jax.ShapeDtypeStruct((r, c), F32)
    return pl.pallas_call(
        body, name=name,
        grid_spec=pltpu.PrefetchScalarGridSpec(
            num_scalar_prefetch=1, grid=(r // rb,), in_specs=[spec, a_spec, b_spec, spec, spec], out_specs=[spec] * 4),
        out_shape=[shp] * 4,
        compiler_params=_params(("parallel",)),
    )(core, w, mine, theirs, m, v)


def _adamw(w, g, m, v, name):
    r, c = w.shape
    rb = r
    for cand in (256, 128, 64, 32, 16, 8):
        if r % cand == 0 and cand * c * 4 <= (1 << 20):
            rb = cand
            break
    if r * c * 4 <= (1 << 20):
        rb = r

    def body(w_ref, g_ref, m_ref, v_ref, d_ref, mo_ref, vo_ref):
        d_ref[...], mo_ref[...], vo_ref[...] = _adam_math(w_ref[...], g_ref[...], m_ref[...], v_ref[...])

    spec = pl.BlockSpec((rb, c), lambda i: (i, 0))
    shp = jax.ShapeDtypeStruct((r, c), F32)
    return pl.pallas_call(
        body, name=name, grid=(r // rb,), in_specs=[spec] * 4, out_specs=[spec] * 3, out_shape=[shp] * 3,
        compiler_params=_params(("parallel",)),
    )(w, g, m, v)


def _decay_prep(dec):
    def body(d_ref, lg_ref, sg_ref):
        d = d_ref[...]
        lg_ref[...] = jnp.minimum(d, 0.0) - jnp.log(1.0 + jnp.exp(-jnp.abs(d)))
        sg_ref[...] = 1.0 / (1.0 + jnp.exp(d))

    shp = jax.ShapeDtypeStruct(dec.shape, F32)
    return pl.pallas_call(body, name="decay_prep", out_shape=[shp, shp])(dec)


def _mod_fwd(a_in, w_ada, b_sh):
    rows, d = a_in.shape
    n = w_ada.shape[1]
    bn = 512

    def body(a_ref, w_ref, b_ref, o_ref):
        a = a_ref[...]
        s = (a / (1.0 + jnp.exp(-a))).astype(BF16)
        o_ref[...] = _dot(s, w_ref[...].astype(BF16)) + b_ref[...]

    return pl.pallas_call(
        body, name="mod_fwd", grid=(n // bn,),
        in_specs=[_full((rows, d)), pl.BlockSpec((d, bn), lambda j: (0, j)), pl.BlockSpec((1, bn), lambda j: (0, j))],
        out_specs=pl.BlockSpec((rows, bn), lambda j: (0, j)),
        out_shape=jax.ShapeDtypeStruct((rows, n), F32),
        compiler_params=_params(("parallel",)),
    )(a_in, w_ada, b_sh)


def _mod_bwd(a_in, dm, w_ada):
    rows, d = a_in.shape
    n = w_ada.shape[1]
    bn = 512
    nb = n // bn

    def body(a_ref, dm_ref, w_ref, gw_ref, da_ref):
        j = pl.program_id(0)
        a = a_ref[...]
        s = (a / (1.0 + jnp.exp(-a))).astype(BF16)
        dmb = dm_ref[...].astype(BF16)
        gw_ref[...] = _dot_tn(s, dmb)
        part = _dot_nt(dmb, w_ref[...].astype(BF16))

        @pl.when(j == 0)
        def _():
            da_ref[...] = part

        @pl.when(j > 0)
        def _():
            da_ref[...] += part

    return pl.pallas_call(
        body, name="mod_bwd", grid=(nb,),
        in_specs=[_full((rows, d)), pl.BlockSpec((rows, bn), lambda j: (0, j)), pl.BlockSpec((d, bn), lambda j: (0, j))],
        out_specs=[pl.BlockSpec((d, bn), lambda j: (0, j)), _full((rows, d))],
        out_shape=[jax.ShapeDtypeStruct((d, n), F32), jax.ShapeDtypeStruct((rows, d), F32)],
        compiler_params=_params(("arbitrary",)),
    )(a_in, dm, w_ada)


def _pre_fwd(x2, ctx2, modv, g_attn, w_in, g_q, g_kv, w_uq, w_ukv, cos_t, sin_t, *, seq, tm):
    t_lat, d = x2.shape
    t_ctx = ctx2.shape[0]
    nl, nc = t_lat // tm, t_ctx // tm
    n_all = t_lat + t_ctx
    tpe = seq // tm
    nex = t_lat // seq

    def body(x_ref, c_ref, mod_ref, g_ref, win_ref, gq_ref, gkv_ref, wuq_ref, wukv_ref, cos_ref, sin_ref,
             h_ref, pg_ref, rq_ref, rk_ref, rv_ref, nq_ref, nkv_ref, q_ref, k_ref, v_ref):
        i = pl.program_id(0)
        xt = jnp.where(i < nl, x_ref[...], c_ref[...])
        sh = mod_ref[0, 0:1, :]
        sc = mod_ref[0, 1:2, :]
        r = lax.rsqrt(jnp.mean(xt * xt, axis=-1, keepdims=True) + EPS)
        hb = ((xt * r) * g_ref[...] * (1.0 + sc) + sh).astype(BF16)
        h_ref[...] = hb
        p = _dot(hb, win_ref[...])
        cos = cos_ref[...]
        sin = sin_ref[...]
        rq_ref[...] = _rope(p[:, 0:256], cos, sin).astype(BF16)
        rk_ref[...] = _rope(p[:, 256:512] * (RET_DK ** -0.5), cos, sin).astype(BF16)
        rv_ref[...] = p[:, 512:1024].astype(BF16)
        pg_ref[...] = p[:, 1024:2176]
        cq = p[:, 1536:1920]
        ckv = p[:, 1920:2176]
        nqb = (cq * lax.rsqrt(jnp.mean(cq * cq, axis=-1, keepdims=True) + EPS) * gq_ref[...]).astype(BF16)
        nkvb = (ckv * lax.rsqrt(jnp.mean(ckv * ckv, axis=-1, keepdims=True) + EPS) * gkv_ref[...]).astype(BF16)
        nq_ref[...] = nqb
        nkv_ref[...] = nkvb
        cos1 = cos[:, 0:LANES]
        sin1 = sin[:, 0:LANES]
        kpe = _rope(p[:, 2176:2304], cos1, sin1).astype(BF16)
        for hd in range(HEADS):
            o = hd * MLA_HEAD
            qh = _dot(nqb, wuq_ref[hd])
            q_ref[:, o:o + 128] = qh[:, 0:128].astype(BF16)
            q_ref[:, o + 128:o + 256] = _rope(qh[:, 128:256], cos1, sin1).astype(BF16)
            kvh = _dot(nkvb, wukv_ref[hd])
            k_ref[:, o:o + 128] = kvh[:, 0:128].astype(BF16)
            k_ref[:, o + 128:o + 256] = kpe
            v_ref[:, hd * 128:(hd + 1) * 128] = kvh[:, 128:256].astype(BF16)

    def tile(width):
        return pl.BlockSpec((tm, width), lambda i: (i, 0))

    widths = (d, PG_COLS, 256, 256, 512, Q_LORA, KV_LORA, HEADS * MLA_HEAD, HEADS * MLA_HEAD, HEADS * 128)
    dtypes = (BF16, F32, BF16, BF16, BF16, BF16, BF16, BF16, BF16, BF16)
    tab = pl.BlockSpec((tm, 256), lambda i: (jnp.where(i < nl, i % tpe, tpe), 0))
    return pl.pallas_call(
        body, name="pre_fwd", grid=(nl + nc,),
        in_specs=[
            pl.BlockSpec((tm, d), lambda i: (jnp.minimum(i, nl - 1), 0)),
            pl.BlockSpec((tm, d), lambda i: (jnp.maximum(i - nl, 0), 0)),
            pl.BlockSpec((1, 8, d), lambda i: (jnp.minimum(i // tpe, nex), 0, 0)),
            _full((1, d)), _full(w_in.shape), _full((1, Q_LORA)), _full((1, KV_LORA)),
            _full(w_uq.shape), _full(w_ukv.shape), tab, tab,
        ],
        out_specs=[tile(w) for w in widths],
        out_shape=[jax.ShapeDtypeStruct((n_all, w), dt) for w, dt in zip(widths, dtypes)],
        compiler_params=_params(("parallel",), VMEM_LIMIT),
    )(x2, ctx2, modv, g_attn, w_in, g_q, g_kv, w_uq, w_ukv, cos_t, sin_t)


def _post(yret, ymla, x2, tgt2, modv, g_ffn, g_fin, w_out, w_ff1, w_ff2, *, seq, tm):
    t_lat, d = x2.shape
    nl = t_lat // tm
    tpe = seq // tm
    nex = t_lat // seq
    n_slab = w_ff1.shape[0]
    fs = w_ff1.shape[2]

    def body(yr_ref, ym_ref, x_ref, t_ref, mod_ref, gf_ref, gl_ref, wo_ref, w1_ref, w2_ref,
             mix_ref, a_ref, du_ref, h2_ref, df_ref, dmo_ref, dmix_ref, dxm_ref, st_ref, ru_ref):
        i = pl.program_id(0)
        gt_a = mod_ref[0, 2:3, :]
        sh_f = mod_ref[0, 3:4, :]
        sc_f = mod_ref[0, 4:5, :]
        gt_f = mod_ref[0, 5:6, :]
        g_ffn_v = gf_ref[...]
        g_fin_v = gl_ref[...]
        yr = yr_ref[...]
        ym = ym_ref[...]
        mix_ref[:, 0:512] = yr
        mix_ref[:, 512:1024] = ym
        op = _dot(yr, wo_ref[0:512, :]) + _dot(ym, wo_ref[512:1024, :])
        x_mid = x_ref[...] + gt_a * op
        r2 = lax.rsqrt(jnp.mean(x_mid * x_mid, axis=-1, keepdims=True) + EPS)
        xh2 = x_mid * r2
        h2b = (xh2 * g_ffn_v * (1.0 + sc_f) + sh_f).astype(BF16)
        h2_ref[...] = h2b
        f = jnp.zeros((tm, d), F32)
        for s in range(n_slab):
            ru = jnp.maximum(_dot(h2b, w1_ref[s]), 0.0)
            ru_ref[:, s * fs:(s + 1) * fs] = ru
            ab = (ru * ru).astype(BF16)
            a_ref[:, s * fs:(s + 1) * fs] = ab
            f = f + _dot(ab, w2_ref[s * fs:(s + 1) * fs, :])
        x_out = x_mid + gt_f * f
        r3 = lax.rsqrt(jnp.mean(x_out * x_out, axis=-1, keepdims=True) + EPS)
        xh3 = x_out * r3
        err = xh3 * g_fin_v - t_ref[...]
        dy = err * (1.0 / d)
        dxh3 = dy * g_fin_v
        dx_out = r3 * (dxh3 - xh3 * jnp.mean(dxh3 * xh3, axis=-1, keepdims=True))
        dfb = (dx_out * gt_f).astype(BF16)
        df_ref[...] = dfb
        dh2 = jnp.zeros((tm, d), F32)
        for s in range(n_slab):
            da = _dot_nt(dfb, w2_ref[s * fs:(s + 1) * fs, :])
            dub = (da * (2.0 * ru_ref[:, s * fs:(s + 1) * fs])).astype(BF16)
            du_ref[:, s * fs:(s + 1) * fs] = dub
            dh2 = dh2 + _dot_nt(dub, w1_ref[s])
        dxh2 = dh2 * (1.0 + sc_f) * g_ffn_v
        dx_mid = dx_out + r2 * (dxh2 - xh2 * jnp.mean(dxh2 * xh2, axis=-1, keepdims=True))
        dxm_ref[...] = dx_mid
        dmob = (dx_mid * gt_a).astype(BF16)
        dmo_ref[...] = dmob
        dmix_ref[...] = _dot_nt(dmob, wo_ref[...]).astype(BF16)

        def rsum(v):
            return jnp.sum(v, axis=0, keepdims=True)

        stats = jnp.concatenate([
            rsum(dh2), rsum(dh2 * xh2 * g_ffn_v), rsum(dx_out * f), rsum(dx_mid * op),
            rsum(dh2 * (1.0 + sc_f) * xh2), rsum(dy * xh3), rsum(err * err), jnp.zeros((1, d), F32)], axis=0)

        @pl.when(i % tpe == 0)
        def _():
            st_ref[0] = stats

        @pl.when(i % tpe != 0)
        def _():
            st_ref[0] += stats

    def tile(width):
        return pl.BlockSpec((tm, width), lambda i: (i, 0))

    widths = (d, D_FF, D_FF, d, d, d, d, d)
    dtypes = (BF16, BF16, BF16, BF16, BF16, BF16, BF16, F32)
    const = pl.Buffered(1)
    return pl.pallas_call(
        body, name="post", grid=(nl,),
        in_specs=[
            tile(512), tile(512), tile(d), tile(d),
            pl.BlockSpec((1, 8, d), lambda i: (i // tpe, 0, 0)),
            _full((1, d)), _full((1, d)),
            pl.BlockSpec(w_out.shape, lambda i: (0, 0), pipeline_mode=const),
            pl.BlockSpec(w_ff1.shape, lambda i: (0, 0, 0), pipeline_mode=const),
            pl.BlockSpec(w_ff2.shape, lambda i: (0, 0), pipeline_mode=const),
        ],
        out_specs=[tile(w) for w in widths] + [pl.BlockSpec((1, 8, d), lambda i: (i // tpe, 0, 0))],
        out_shape=[jax.ShapeDtypeStruct((t_lat, w), dt) for w, dt in zip(widths, dtypes)]
        + [jax.ShapeDtypeStruct((nex, 8, d), F32)],
        scratch_shapes=[pltpu.VMEM((tm, D_FF), F32)],
        compiler_params=_params(("arbitrary",), VMEM_LIMIT),
    )(yret, ymla, x2, tgt2, modv, g_ffn, g_fin, w_out, w_ff1, w_ff2)


def _pre_bwd(x2, ctx2, modv, g_attn, pg, drq, drk, dkc_r, drv, dvc_r, drg, dq_m, dkl, dkc, dvl, dvc, dxm,
             w_in, g_q, g_kv, w_uq, w_ukv, cos_t, sin_t, *, seq, tm, rider=None):
    t_lat, d = x2.shape
    t_ctx = ctx2.shape[0]
    nl, nc = t_lat // tm, t_ctx // tm
    n_all = t_lat + t_ctx
    tpe = seq // tm
    nex = t_lat // seq

    def body(x_ref, c_ref, mod_ref, g_ref, pg_ref, drq_ref, drk_ref, dkcr_ref, drv_ref, dvcr_ref, drg_ref,
             dq_ref, dkl_ref, dkc_ref, dvl_ref, dvc_ref, dxm_ref, win_ref, gq_ref, gkv_ref, wuq_ref, wukv_ref,
             cos_ref, sin_ref, dpb_ref, dqf_ref, dkvf_ref, gx_ref, st_ref):
        i = pl.program_id(0)
        lat = i < nl
        latf = lat.astype(F32)
        cos = cos_ref[...]
        sin = sin_ref[...]
        cos1 = cos[:, 0:LANES]
        sin1 = sin[:, 0:LANES]
        d_rq = _rope_t(drq_ref[...] * latf, cos, sin)
        d_rk = _rope_t(jnp.where(lat, drk_ref[...], dkcr_ref[...]), cos, sin) * (RET_DK ** -0.5)
        d_rv = jnp.where(lat, drv_ref[...], dvcr_ref[...])
        d_rg = drg_ref[...] * latf
        dq_all = dq_ref[...] * latf
        dk_all = jnp.where(lat, dkl_ref[...], dkc_ref[...])
        dv_all = jnp.where(lat, dvl_ref[...], dvc_ref[...])
        dnq = jnp.zeros((tm, Q_LORA), F32)
        dnkv = jnp.zeros((tm, KV_LORA), F32)
        dkpe = jnp.zeros((tm, LANES), F32)
        for hd in range(HEADS):
            o = hd * MLA_HEAD
            dqh = jnp.concatenate([dq_all[:, o:o + 128], _rope_t(dq_all[:, o + 128:o + 256], cos1, sin1)],
                                  axis=1).astype(BF16)
            dqf_ref[:, o:o + 256] = dqh
            dnq = dnq + _dot_nt(dqh, wuq_ref[hd])
            dkpe = dkpe + dk_all[:, o + 128:o + 256]
            dkvh = jnp.concatenate([dk_all[:, o:o + 128], dv_all[:, hd * 128:(hd + 1) * 128]], axis=1).astype(BF16)
            dkvf_ref[:, o:o + 256] = dkvh
            dnkv = dnkv + _dot_nt(dkvh, wukv_ref[hd])
        d_kpe = _rope_t(dkpe, cos1, sin1)
        pgv = pg_ref[...]
        cq = pgv[:, 512:896]
        ckv = pgv[:, 896:1152]
        rq_ = lax.rsqrt(jnp.mean(cq * cq, axis=-1, keepdims=True) + EPS)
        cqh = cq * rq_
        dcqh = dnq * gq_ref[...]
        d_cq = rq_ * (dcqh - cqh * jnp.mean(dcqh * cqh, axis=-1, keepdims=True))
        rkv_ = lax.rsqrt(jnp.mean(ckv * ckv, axis=-1, keepdims=True) + EPS)
        ckvh = ckv * rkv_
        dckvh = dnkv * gkv_ref[...]
        d_ckv = rkv_ * (dckvh - ckvh * jnp.mean(dckvh * ckvh, axis=-1, keepdims=True))
        dpb = jnp.concatenate([d_rq, d_rk, d_rv, d_rg, d_cq, d_ckv, d_kpe], axis=1).astype(BF16)
        dpb_ref[...] = dpb
        dh = _dot_nt(dpb, win_ref[...])
        xt = jnp.where(lat, x_ref[...], c_ref[...])
        sc = mod_ref[0, 1:2, :]
        g = g_ref[...]
        r = lax.rsqrt(jnp.mean(xt * xt, axis=-1, keepdims=True) + EPS)
        xh = xt * r
        dxh = dh * (1.0 + sc) * g
        dx = r * (dxh - xh * jnp.mean(dxh * xh, axis=-1, keepdims=True))

        @pl.when(lat)
        def _():
            gx_ref[...] = dxm_ref[...] + dx

        def rsum(v):
            return jnp.sum(v, axis=0, keepdims=True)

        def widen(v):
            return jnp.concatenate([v, jnp.zeros((1, d - v.shape[1]), F32)], axis=1)

        stats = jnp.concatenate([
            rsum(dh), rsum(dh * xh * g), rsum(dh * (1.0 + sc) * xh), widen(rsum(dnq * cqh)), widen(rsum(dnkv * ckvh)),
            jnp.zeros((3, d), F32)], axis=0)
        first = jnp.logical_or(jnp.logical_and(lat, i % tpe == 0), i == nl)

        @pl.when(first)
        def _():
            st_ref[0] = stats

        @pl.when(jnp.logical_not(first))
        def _():
            st_ref[0] += stats

    def lat_tile(width):
        return pl.BlockSpec((tm, width), lambda i: (jnp.minimum(i, nl - 1), 0))

    def ctx_tile(width):
        return pl.BlockSpec((tm, width), lambda i: (jnp.maximum(i - nl, 0), 0))

    def tile(width):
        return pl.BlockSpec((tm, width), lambda i: (i, 0))

    tab = pl.BlockSpec((tm, 256), lambda i: (jnp.where(i < nl, i % tpe, tpe), 0))
    ex = pl.BlockSpec((1, 8, d), lambda i: (jnp.minimum(i // tpe, nex), 0, 0))
    return _hosted_call(
        body, (x2, ctx2, modv, g_attn, pg, drq, drk, dkc_r, drv, dvc_r, drg, dq_m, dkl, dkc, dvl, dvc, dxm,
               w_in, g_q, g_kv, w_uq, w_ukv, cos_t, sin_t), name="pre_bwd", grid=(nl + nc,),
        in_specs=[
            lat_tile(d), ctx_tile(d), ex, _full((1, d)), tile(PG_COLS),
            lat_tile(256), lat_tile(256), ctx_tile(256), lat_tile(512), ctx_tile(512), lat_tile(512),
            lat_tile(1024), lat_tile(1024), ctx_tile(1024), lat_tile(512), ctx_tile(512), lat_tile(d),
            _full(w_in.shape), _full((1, Q_LORA)), _full((1, KV_LORA)), _full(w_uq.shape), _full(w_ukv.shape),
            tab, tab,
        ],
        out_specs=[tile(IN_PAD), tile(1024), tile(1024), lat_tile(d), ex],
        out_shape=[
            jax.ShapeDtypeStruct((n_all, IN_PAD), BF16), jax.ShapeDtypeStruct((n_all, 1024), BF16),
            jax.ShapeDtypeStruct((n_all, 1024), BF16), jax.ShapeDtypeStruct((t_lat, d), F32),
            jax.ShapeDtypeStruct((nex + 1, 8, d), F32),
        ],
        sem=("arbitrary",), rider=rider)


MLA_SCALE = 1.0 / math.sqrt(MLA_NOPE + MLA_ROPE)
KEY_BLOCK = 2048


def _mla_specs(t_lat, seq, ctx_len, tq):
    nqt = seq // tq
    cb = t_lat // ctx_len
    q = pl.BlockSpec((tq, MLA_HEAD), lambda b, h, j: (b * nqt + j, h))
    kl = pl.BlockSpec((seq, MLA_HEAD), lambda b, h, j: (b, h))
    kc = pl.BlockSpec((ctx_len, MLA_HEAD), lambda b, h, j: (cb + b, h))
    vl = pl.BlockSpec((seq, 128), lambda b, h, j: (b, h))
    vc = pl.BlockSpec((ctx_len, 128), lambda b, h, j: (cb + b, h))
    o = pl.BlockSpec((tq, 128), lambda b, h, j: (b * nqt + j, h))
    return q, kl, kc, vl, vc, o


def _mla_fwd(q, k, v, *, t_lat, seq, ctx_len, tq, rider=None):
    nex = t_lat // seq

    def body(q_ref, kl_ref, kc_ref, vl_ref, vc_ref, o_ref, lse_ref):
        qb = q_ref[...]
        s = _dot_nt(qb, kl_ref[...]) * MLA_SCALE
        sc = _dot_nt(qb, kc_ref[...]) * MLA_SCALE
        m = jnp.maximum(jnp.max(s, axis=-1, keepdims=True), jnp.max(sc, axis=-1, keepdims=True))
        p = jnp.exp(s - m)
        pc = jnp.exp(sc - m)
        total = jnp.sum(p, axis=-1, keepdims=True) + jnp.sum(pc, axis=-1, keepdims=True)
        o = _dot(p.astype(BF16), vl_ref[...]) + _dot(pc.astype(BF16), vc_ref[...])
        o_ref[...] = (o * (1.0 / total)).astype(BF16)
        lse_ref[...] = jnp.broadcast_to(m + jnp.log(total), lse_ref.shape)

    qs, kl, kc, vl, vc, os_ = _mla_specs(t_lat, seq, ctx_len, tq)
    return _hosted_call(
        body, (q, k, k, v, v), name="mla_fwd", grid=(nex, HEADS, seq // tq),
        in_specs=[qs, kl, kc, vl, vc], out_specs=[os_, os_],
        out_shape=[jax.ShapeDtypeStruct((t_lat, HEADS * 128), BF16), jax.ShapeDtypeStruct((t_lat, HEADS * 128), F32)],
        sem=("parallel", "parallel", "arbitrary"), rider=rider)


def _mla_bwd(q, k, v, ymla, lse, dmix, *, t_lat, seq, ctx_len, tq, rider=None):
    nex = t_lat // seq
    nqt = seq // tq
    t_ctx = nex * ctx_len
    kb = min(KEY_BLOCK, seq)

    def body(q_ref, kl_ref, kc_ref, vl_ref, vc_ref, o_ref, lse_ref, do_ref, dq_ref, dkl_ref, dkc_ref, dvl_ref, dvc_ref):
        j = pl.program_id(2)

        @pl.when(j == 0)
        def _():
            dkl_ref[...] = jnp.zeros(dkl_ref.shape, F32)
            dkc_ref[...] = jnp.zeros(dkc_ref.shape, F32)
            dvl_ref[...] = jnp.zeros(dvl_ref.shape, F32)
            dvc_ref[...] = jnp.zeros(dvc_ref.shape, F32)

        qb = q_ref[...]
        dob = do_ref[...]
        delta = jnp.sum(dob.astype(F32) * o_ref[...].astype(F32), axis=-1, keepdims=True)
        lse_row = lse_ref[:, 0:1]

        def block(k_ref, v_ref, dk_ref, dv_ref, rows):
            kbl = k_ref[rows, :]
            vbl = v_ref[rows, :]
            p = jnp.exp(_dot_nt(qb, kbl) * MLA_SCALE - lse_row)
            ds = (p * (_dot_nt(dob, vbl) - delta) * MLA_SCALE).astype(BF16)
            dk_ref[rows, :] += _dot_tn(ds, qb)
            dv_ref[rows, :] += _dot_tn(p.astype(BF16), dob)
            return _dot(ds, kbl)

        dq = block(kc_ref, vc_ref, dkc_ref, dvc_ref, pl.ds(0, ctx_len))
        for i in range(seq // kb):
            dq = dq + block(kl_ref, vl_ref, dkl_ref, dvl_ref, pl.ds(i * kb, kb))
        dq_ref[...] = dq

    qs, kl, kc, vl, vc, os_ = _mla_specs(t_lat, seq, ctx_len, tq)
    do_spec = pl.BlockSpec((tq, 128), lambda b, h, j: (b * nqt + j, HEADS + h))
    return _hosted_call(
        body, (q, k, k, v, v, ymla, lse, dmix), name="mla_bwd", grid=(nex, HEADS, nqt),
        in_specs=[qs, kl, kc, vl, vc, os_, os_, do_spec],
        out_specs=[
            qs,
            pl.BlockSpec((seq, MLA_HEAD), lambda b, h, j: (b, h)),
            pl.BlockSpec((ctx_len, MLA_HEAD), lambda b, h, j: (b, h)),
            pl.BlockSpec((seq, 128), lambda b, h, j: (b, h)),
            pl.BlockSpec((ctx_len, 128), lambda b, h, j: (b, h)),
        ],
        out_shape=[
            jax.ShapeDtypeStruct((t_lat, HEADS * MLA_HEAD), F32),
            jax.ShapeDtypeStruct((t_lat, HEADS * MLA_HEAD), F32),
            jax.ShapeDtypeStruct((t_ctx, HEADS * MLA_HEAD), F32),
            jax.ShapeDtypeStruct((t_lat, HEADS * 128), F32),
            jax.ShapeDtypeStruct((t_ctx, HEADS * 128), F32),
        ],
        sem=("parallel", "parallel", "arbitrary"), rider=rider)


def _decay_terms(lg, chunk, forward):
    ii = lax.broadcasted_iota(jnp.int32, (chunk, chunk), 0)
    jj = lax.broadcasted_iota(jnp.int32, (chunk, chunk), 1)
    diff = (ii - jj) if forward else (jj - ii)
    dist = jnp.maximum(diff, 0).astype(F32)
    dmat = jnp.where(diff >= 0, jnp.exp(lg * dist), 0.0)
    pos = lax.broadcasted_iota(jnp.int32, (chunk, 1), 0).astype(F32)
    if forward:
        e_q = pos + 1.0
        e_k = (chunk - 1.0) - pos
    else:
        e_q = chunk - pos
        e_k = pos
    wq = jnp.exp(lg * e_q)
    wk = jnp.exp(lg * e_k)
    cd = jnp.exp(jnp.full((1, 1), lg * chunk, F32))
    return dmat, dist, wq, wk, e_q, e_k, cd


def _ctx_weights(lg, ctx_len, forward):
    pos = lax.broadcasted_iota(jnp.int32, (ctx_len, 1), 0).astype(F32)
    e = ((ctx_len - 1.0) - pos) if forward else pos
    return jnp.exp(lg * e), e


def _ret_specs(t_lat, seq, ctx_len):
    cb = t_lat // ctx_len
    qk = pl.BlockSpec((seq, 128), lambda b, h: (b, h // 2))
    v = pl.BlockSpec((seq, 128), lambda b, h: (b, h))
    kc = pl.BlockSpec((ctx_len, 128), lambda b, h: (cb + b, h // 2))
    vc = pl.BlockSpec((ctx_len, 128), lambda b, h: (cb + b, h))
    return qk, v, kc, vc


def _head_mask(h):
    lane = lax.broadcasted_iota(jnp.int32, (1, 128), 1)
    return (lane // RET_DK) == (h % 2)


def _ret_fwd(rq, rk, rv, pg, lg, g_ret, *, t_lat, seq, ctx_len, chunk, rider=None):
    nex = t_lat // seq
    n_chunk = seq // chunk

    def body(q_ref, k_ref, v_ref, kc_ref, vc_ref, rg_ref, lg_ref, g_ref, y_ref, o_ref):
        h = pl.program_id(1)
        hm = _head_mask(h)
        gain = g_ref[...]
        kcm = jnp.where(hm, kc_ref[...].astype(F32), 0.0)
        vcb = vc_ref[...]

        def run(forward):
            lgd = lg_ref[0 if forward else 1, h]
            dmat, _, wq, wk, _, _, cd = _decay_terms(lgd, chunk, forward)
            wc, _ = _ctx_weights(lgd, ctx_len, forward)
            s0 = _dot_tn((kcm * wc).astype(BF16), vcb)

            def step(t, s):
                n = t if forward else n_chunk - 1 - t
                sl = pl.ds(pl.multiple_of(n * chunk, chunk), chunk)
                qm = jnp.where(hm, q_ref[sl, :], jnp.zeros((), BF16))
                kf = jnp.where(hm, k_ref[sl, :].astype(F32), 0.0)
                vb = v_ref[sl, :]
                a = _dot_nt(qm, kf.astype(BF16)) * dmat
                o = _dot(a.astype(BF16), vb) + wq * _dot(qm, s.astype(BF16))
                if forward:
                    o_ref[sl, :] = o
                else:
                    o = o_ref[sl, :] + o
                    o_ref[sl, :] = o
                    mu = jnp.mean(o, axis=-1, keepdims=True)
                    oc = o - mu
                    var = jnp.mean(oc * oc, axis=-1, keepdims=True)
                    on = oc * lax.rsqrt(var + EPS) * gain
                    rg = rg_ref[sl, :]
                    y_ref[sl, :] = (on * (rg / (1.0 + jnp.exp(-rg)))).astype(BF16)
                return cd * s + _dot_tn((kf * wk).astype(BF16), vb)

            lax.fori_loop(0, n_chunk, step, s0)

        run(True)
        run(False)

    qk, v, kc, vc = _ret_specs(t_lat, seq, ctx_len)
    return _hosted_call(
        body, (rq, rk, rv, rk, rv, pg, lg, g_ret), name="ret_fwd", grid=(nex, HEADS),
        in_specs=[qk, qk, v, kc, vc, v, pl.BlockSpec(memory_space=pltpu.SMEM), pl.BlockSpec((1, 128), lambda b, h: (0, h))],
        out_specs=[v, v],
        out_shape=[jax.ShapeDtypeStruct((t_lat, HEADS * RET_DV), BF16), jax.ShapeDtypeStruct((t_lat, HEADS * RET_DV), F32)],
        sem=("parallel", "arbitrary"), rider=rider)


def _ret_bwd(rq, rk, rv, pg, osum, dmix, lg, g_ret, *, t_lat, seq, ctx_len, chunk, rider=None):
    nex = t_lat // seq
    n_chunk = seq // chunk
    t_ctx = nex * ctx_len

    def body(q_ref, k_ref, v_ref, kc_ref, vc_ref, rg_ref, o_ref, dy_ref, lg_ref, g_ref,
             dq_ref, dk_ref, dv_ref, dkc_ref, dvc_ref, drg_ref, st_ref, do_s, s_st):
        h = pl.program_id(1)
        hm = _head_mask(h)
        gain = g_ref[...]
        kcm = jnp.where(hm, kc_ref[...].astype(F32), 0.0)
        vcb = vc_ref[...]

        def norm_step(n, dgain):
            sl = pl.ds(pl.multiple_of(n * chunk, chunk), chunk)
            o = o_ref[sl, :]
            mu = jnp.mean(o, axis=-1, keepdims=True)
            oc = o - mu
            rstd = lax.rsqrt(jnp.mean(oc * oc, axis=-1, keepdims=True) + EPS)
            ohat = oc * rstd
            rg = rg_ref[sl, :]
            sg = 1.0 / (1.0 + jnp.exp(-rg))
            dy = dy_ref[sl, :].astype(F32)
            don = dy * (rg * sg)
            drg_ref[sl, :] = dy * (ohat * gain) * (sg * (1.0 + rg * (1.0 - sg)))
            dohat = don * gain
            do_s[sl, :] = rstd * (dohat - jnp.mean(dohat, axis=-1, keepdims=True)
                                  - ohat * jnp.mean(dohat * ohat, axis=-1, keepdims=True))
            return dgain + jnp.sum(don * ohat, axis=0, keepdims=True)

        dgain = lax.fori_loop(0, n_chunk, norm_step, jnp.zeros((1, 128), F32))

        @pl.when(h % 2 == 0)
        def _():
            dq_ref[...] = jnp.zeros(dq_ref.shape, F32)
            dk_ref[...] = jnp.zeros(dk_ref.shape, F32)
            dkc_ref[...] = jnp.zeros(dkc_ref.shape, F32)

        dv_ref[...] = jnp.zeros(dv_ref.shape, F32)

        def run(forward):
            lgd = lg_ref[0 if forward else 1, h]
            dmat, dist, wq, wk, e_q, e_k, cd = _decay_terms(lgd, chunk, forward)
            wc, e_c = _ctx_weights(lgd, ctx_len, forward)
            s0 = _dot_tn((kcm * wc).astype(BF16), vcb)

            def state_step(t, s):
                n = t if forward else n_chunk - 1 - t
                sl = pl.ds(pl.multiple_of(n * chunk, chunk), chunk)
                s_st[n] = s
                kf = jnp.where(hm, k_ref[sl, :].astype(F32), 0.0)
                return cd * s + _dot_tn((kf * wk).astype(BF16), v_ref[sl, :])

            lax.fori_loop(0, n_chunk, state_step, s0)

            def grad_step(t, carry):
                g_next, dlg = carry
                n = (n_chunk - 1 - t) if forward else t
                sl = pl.ds(pl.multiple_of(n * chunk, chunk), chunk)
                qm = jnp.where(hm, q_ref[sl, :], jnp.zeros((), BF16))
                kf = jnp.where(hm, k_ref[sl, :].astype(F32), 0.0)
                kb = kf.astype(BF16)
                vb = v_ref[sl, :]
                do = do_s[sl, :]
                dob = do.astype(BF16)
                s_n = s_st[n]
                s_nb = s_n.astype(BF16)
                gb = g_next.astype(BF16)
                dk_cross = wk * _dot_nt(vb, gb)
                dv_cross = _dot((kf * wk).astype(BF16), gb)
                a = _dot_nt(qm, kb) * dmat
                da_raw = _dot_nt(dob, vb)
                dab = (da_raw * dmat).astype(BF16)
                ab = a.astype(BF16)
                o_cross = wq * _dot(qm, s_nb)
                dq_ref[sl, :] += _dot(dab, kb) + wq * _dot_nt(dob, s_nb)
                dk_ref[sl, :] += _dot_tn(dab, qm) + dk_cross
                dv_ref[sl, :] += _dot_tn(ab, dob) + dv_cross
                dlg = (dlg + chunk * cd * jnp.sum(g_next * s_n, keepdims=True)
                       + jnp.sum(e_k * jnp.sum(kf * dk_cross, axis=-1, keepdims=True), keepdims=True)
                       + jnp.sum(dist * a * da_raw, keepdims=True)
                       + jnp.sum(e_q * jnp.sum(o_cross * do, axis=-1, keepdims=True), keepdims=True))
                g_new = cd * g_next + _dot_tn((qm.astype(F32) * wq).astype(BF16), dob)
                return g_new, dlg

            ds0, dlg = lax.fori_loop(0, n_chunk, grad_step, (jnp.zeros((128, 128), F32), jnp.zeros((1, 1), F32)))
            ds0b = ds0.astype(BF16)
            dkc_part = wc * _dot_nt(vcb, ds0b)
            dkc_ref[...] += dkc_part
            dvc_part = _dot((kcm * wc).astype(BF16), ds0b)
            dlg = dlg + jnp.sum(e_c * jnp.sum(kcm * dkc_part, axis=-1, keepdims=True), keepdims=True)
            return dvc_part, dlg

        dvc_f, dlg_f = run(True)
        dvc_b, dlg_b = run(False)
        dvc_ref[...] = dvc_f + dvc_b
        st_ref[0] = jnp.concatenate([
            dgain, jnp.broadcast_to(dlg_f, (1, 128)), jnp.broadcast_to(dlg_b, (1, 128)), jnp.zeros((5, 128), F32)], axis=0)

    qk, v, kc, vc = _ret_specs(t_lat, seq, ctx_len)
    dy_spec = v
    return _hosted_call(
        body, (rq, rk, rv, rk, rv, pg, osum, dmix, lg, g_ret), name="ret_bwd", grid=(nex, HEADS),
        in_specs=[qk, qk, v, kc, vc, v, v, dy_spec, pl.BlockSpec(memory_space=pltpu.SMEM),
                  pl.BlockSpec((1, 128), lambda b, h: (0, h))],
        out_specs=[
            qk, qk, v,
            pl.BlockSpec((ctx_len, 128), lambda b, h: (b, h // 2)),
            pl.BlockSpec((ctx_len, 128), lambda b, h: (b, h)),
            v,
            pl.BlockSpec((1, 8, 128), lambda b, h: (b, 0, h)),
        ],
        out_shape=[
            jax.ShapeDtypeStruct((t_lat, 256), F32), jax.ShapeDtypeStruct((t_lat, 256), F32),
            jax.ShapeDtypeStruct((t_lat, 512), F32), jax.ShapeDtypeStruct((t_ctx, 256), F32),
            jax.ShapeDtypeStruct((t_ctx, 512), F32), jax.ShapeDtypeStruct((t_lat, 512), F32),
            jax.ShapeDtypeStruct((nex, 8, 512), F32),
        ],
        scratch_shapes=[pltpu.VMEM((seq, 128), F32), pltpu.VMEM((n_chunk, 128, 128), F32)],
        sem=("parallel", "arbitrary"), rider=rider)


def _matmul_tn(a, b, *, bm, bn, bk, chip_major, name):
    tk, m = a.shape
    n = b.shape[1]
    slab = n // N_CHIPS
    per_block = bn // slab if chip_major else 1
    bk = max(c for c in range(LANES, min(bk, tk) + 1, LANES) if tk % c == 0)

    def body(a_ref, b_ref, o_ref):
        k = pl.program_id(2)
        if chip_major:
            parts = [_dot_tn(a_ref[...], b_ref[:, s * slab:(s + 1) * slab]) for s in range(per_block)]
        else:
            parts = [_dot_tn(a_ref[...], b_ref[...])]

        @pl.when(k == 0)
        def _():
            for s, part in enumerate(parts):
                if chip_major:
                    o_ref[s] = part
                else:
                    o_ref[...] = part

        @pl.when(k > 0)
        def _():
            for s, part in enumerate(parts):
                if chip_major:
                    o_ref[s] += part
                else:
                    o_ref[...] += part

    if chip_major:
        out_spec = pl.BlockSpec((per_block, bm, slab), lambda i, j, k: (j, i, 0))
        out_shape = jax.ShapeDtypeStruct((N_CHIPS, m, slab), F32)
    else:
        out_spec = pl.BlockSpec((bm, bn), lambda i, j, k: (i, j))
        out_shape = jax.ShapeDtypeStruct((m, n), F32)
    return pl.pallas_call(
        body, name=name, grid=(m // bm, n // bn, tk // bk),
        in_specs=[pl.BlockSpec((bk, bm), lambda i, j, k: (k, i)), pl.BlockSpec((bk, bn), lambda i, j, k: (k, j))],
        out_specs=out_spec, out_shape=out_shape,
        compiler_params=_params(("parallel", "parallel", "arbitrary"), VMEM_LIMIT),
    )(a, b)


_LATE = ("w_out", "w_ff1", "w_ff2")
_EARLY = ("w_in", "w_uq", "w_ukv")


def _local_step(x, ctx, tgt, modv, lg, g_attn, g_ffn, g_fin, g_ret, g_q, g_kv, w_in, w_uq, w_ukv, late, place=None,
                *, tm=256, tq=256, chunk=256):
    nex, seq, d = x.shape
    ctx_len = ctx.shape[1]
    t_lat = nex * seq
    x2 = x.reshape(t_lat, d)
    ctx2 = ctx.reshape(nex * ctx_len, d)
    tgt2 = tgt.reshape(t_lat, d)
    cos_t, sin_t = _rope_tables(seq, tm)
    dims = dict(t_lat=t_lat, seq=seq, ctx_len=ctx_len)
    alone = place is None

    hb, pg, rq, rk, rv, nq, nkv, q, k, v = _pre_fwd(x2, ctx2, modv, g_attn, w_in, g_q, g_kv, w_uq, w_ukv, cos_t, sin_t,
                                                    seq=seq, tm=tm)
    (yret, osum), gathered = _ret_fwd(rq, rk, rv, pg, lg, g_ret, chunk=chunk, **dims,
                                      rider=None if alone else _gather_rider(list(late[:2])))
    (ymla, lse), gathered2 = _mla_fwd(q, k, v, tq=tq, **dims, rider=None if alone else _gather_rider(list(late[2:])))
    w_out, w_ff1, w_ff2 = late if alone else gathered + gathered2
    mix, act, du, h2, df, dmo, dmix, dxm, st_post = _post(yret, ymla, x2, tgt2, modv, g_ffn, g_fin, w_out.reshape(d, d),
                                                         w_ff1, w_ff2.reshape(D_FF, d), seq=seq, tm=tm)
    bk = 512
    g_late = [
        _matmul_tn(mix, dmo, bm=1024, bn=1024, bk=1024, chip_major=False, name="gw_out").reshape(N_CHIPS, d // N_CHIPS, d),
        _matmul_tn(h2, du, bm=1024, bn=1024, bk=1024, chip_major=True, name="gw_ff1"),
        _matmul_tn(act, df, bm=1024, bn=1024, bk=1024, chip_major=False, name="gw_ff2").reshape(N_CHIPS, D_FF // N_CHIPS, d),
    ]
    (dq_m, dkl, dkc, dvl, dvc), got = _mla_bwd(q, k, v, ymla, lse, dmix, tq=tq, **dims,
                                               rider=None if alone else _exchange_rider(g_late))
    if not alone:
        core, slot = place
        part = [_add_half(g, r, core, "add_half_" + n) for g, r, n in zip(g_late, got, _LATE)]
    (drq, drk, drv, dkc_r, dvc_r, drg, st_ret), landed = _ret_bwd(
        rq, rk, rv, pg, osum, dmix, lg, g_ret, chunk=chunk, **dims, rider=None if alone else _scatter_rider(part))
    if not alone:
        mine = [_sum_chips(p, l, slot, "sum_chips_" + n) for p, l, n in zip(part, landed, _LATE)]
    (dpb, dqf, dkvf, gx, st_pre), theirs = _pre_bwd(
        x2, ctx2, modv, g_attn, pg, drq, drk, dkc_r, drv, dvc_r, drg, dq_m, dkl, dkc, dvl, dvc, dxm, w_in, g_q, g_kv,
        w_uq, w_ukv, cos_t, sin_t, seq=seq, tm=tm, rider=None if alone else _swap_rider(mine))
    g_early = [
        _matmul_tn(hb, dpb, bm=1024, bn=IN_PAD // 2, bk=512, chip_major=False, name="gw_in"),
        _matmul_tn(nq, dqf, bm=Q_LORA, bn=HEADS * MLA_HEAD, bk=1536, chip_major=True, name="gw_uq"),
        _matmul_tn(nkv, dkvf, bm=KV_LORA, bn=HEADS * 256, bk=1536, chip_major=True, name="gw_ukv"),
    ]
    late_out = g_late if alone else list(zip(mine, theirs))
    return gx.reshape(nex, seq, d), g_early, late_out, st_post, st_ret, st_pre


_ANY = pl.BlockSpec(memory_space=pl.ANY)
_VMEM = pl.BlockSpec(memory_space=pltpu.VMEM)
_OFFSETS = tuple((dx, dy, dc) for dx in (0, 1) for dy in (0, 1) for dc in (0, 1))[1:]
_CHIP_OFFSETS = ((1, 0), (0, 1), (1, 1))


def _place():
    return lax.axis_index("x"), lax.axis_index("y"), lax.axis_index("c")


def _flip(v, d):
    return 1 - v if d else v


def _gather8_rider(a):
    def copies(a_ref, o_ref, send, recv):
        x, y, z = _place()
        me = 4 * x + 2 * y + z
        out = []
        for k, (dx, dy, dc) in enumerate(_OFFSETS):
            peer = (_flip(x, dx), _flip(y, dy), _flip(z, dc))
            landing = o_ref.at[4 * peer[0] + 2 * peer[1] + peer[2]]
            out.append((
                pltpu.make_async_remote_copy(src_ref=a_ref, dst_ref=o_ref.at[me], send_sem=send.at[k],
                                             recv_sem=recv.at[k], device_id=peer, device_id_type=MESH),
                pltpu.make_async_remote_copy(src_ref=a_ref, dst_ref=landing, send_sem=send.at[k],
                                             recv_sem=recv.at[k], device_id=peer, device_id_type=MESH)))
        return me, out

    def start(ins, outs, sems):
        me, cps = copies(ins[0], outs[0], sems[0], sems[1])
        pltpu.make_async_copy(ins[0], outs[0].at[me], sems[2]).start()
        for out_cp, _ in cps:
            out_cp.start()

    def finish(ins, outs, sems):
        me, cps = copies(ins[0], outs[0], sems[0], sems[1])
        for out_cp, in_cp in cps:
            in_cp.wait_recv()
            out_cp.wait_send()
        pltpu.make_async_copy(ins[0], outs[0].at[me], sems[2]).wait()

    return _Rider([a], [jax.ShapeDtypeStruct((N_DEV,) + a.shape, a.dtype)],
                  [pltpu.SemaphoreType.DMA((7,)), pltpu.SemaphoreType.DMA((7,)), pltpu.SemaphoreType.DMA],
                  start, finish, in_specs=[_VMEM], out_specs=[_VMEM])


def _merge_riders(*riders):
    ins, outs, sems, in_specs, out_specs, aliases, cuts = [], [], [], [], [], {}, []
    for r in riders:
        cuts.append((len(ins), len(outs), len(sems)))
        aliases.update({len(ins) + i: len(outs) + j for i, j in r.aliases.items()})
        ins += r.ins
        outs += r.out_shapes
        sems += r.sems
        in_specs += r.in_specs
        out_specs += r.out_specs

    def part(r, cut, r_ins, r_outs, r_sems):
        return (r_ins[cut[0]:cut[0] + len(r.ins)], r_outs[cut[1]:cut[1] + len(r.out_shapes)],
                r_sems[cut[2]:cut[2] + len(r.sems)])

    def start(r_ins, r_outs, r_sems):
        for r, cut in zip(riders, cuts):
            r.start(*part(r, cut, r_ins, r_outs, r_sems))

    def finish(r_ins, r_outs, r_sems):
        for r, cut in zip(riders, cuts):
            r.finish(*part(r, cut, r_ins, r_outs, r_sems))

    return _Rider(ins, outs, sems, start, finish, aliases=aliases, in_specs=in_specs, out_specs=out_specs)


def _allgather8(a, name):
    return _run_rider(_gather8_rider(a), name)[0]


def _gather_send(o_refs, send, recv):
    x, y, z = _place()
    chip = 2 * x + y
    for a, o in enumerate(o_refs):
        r2 = o.shape[1] // 2
        mine = o.at[chip, pl.ds(z * r2, r2)]
        for k, (dx, dy) in enumerate(_CHIP_OFFSETS):
            pltpu.make_async_remote_copy(
                src_ref=mine, dst_ref=mine, send_sem=send.at[a, k], recv_sem=recv.at[a, k],
                device_id=(_flip(x, dx), _flip(y, dy), z), device_id_type=MESH).start()


def _gather_finish(o_refs, send, recv, fsend, frecv):
    x, y, z = _place()
    chip = 2 * x + y
    sib = (x, y, 1 - z)
    passed = []
    for a, o in enumerate(o_refs):
        r2 = o.shape[1] // 2
        for k, (dx, dy) in enumerate(_CHIP_OFFSETS):
            other = 2 * _flip(x, dx) + _flip(y, dy)
            landed = o.at[other, pl.ds(z * r2, r2)]
            pltpu.make_async_remote_copy(
                src_ref=landed, dst_ref=landed, send_sem=send.at[a, k], recv_sem=recv.at[a, k],
                device_id=(_flip(x, dx), _flip(y, dy), z), device_id_type=MESH).wait_recv()
            cp = pltpu.make_async_remote_copy(
                src_ref=landed, dst_ref=landed, send_sem=fsend.at[a, k], recv_sem=frecv.at[a, k],
                device_id=sib, device_id_type=MESH)
            cp.start()
            passed.append(cp)
    for a, o in enumerate(o_refs):
        r2 = o.shape[1] // 2
        mine = o.at[chip, pl.ds(z * r2, r2)]
        for k, (dx, dy) in enumerate(_CHIP_OFFSETS):
            other = 2 * _flip(x, dx) + _flip(y, dy)
            got = o.at[other, pl.ds((1 - z) * r2, r2)]
            pltpu.make_async_remote_copy(
                src_ref=got, dst_ref=got, send_sem=fsend.at[a, k], recv_sem=frecv.at[a, k],
                device_id=sib, device_id_type=MESH).wait_recv()
            pltpu.make_async_remote_copy(
                src_ref=mine, dst_ref=mine, send_sem=send.at[a, k], recv_sem=recv.at[a, k],
                device_id=(_flip(x, dx), _flip(y, dy), z), device_id_type=MESH).wait_send()
    for cp in passed:
        cp.wait_send()


class _Rider:
    def __init__(self, ins, out_shapes, sems, start, finish, aliases=None, in_specs=None, out_specs=None):
        self.ins, self.out_shapes, self.sems = list(ins), list(out_shapes), list(sems)
        self.start, self.finish, self.aliases = start, finish, dict(aliases or {})
        self.in_specs = list(in_specs) if in_specs else [_ANY] * len(self.ins)
        self.out_specs = list(out_specs) if out_specs else [_ANY] * len(self.out_shapes)


def _run_rider(rider, name):
    r_in, r_out = len(rider.ins), len(rider.out_shapes)

    def body(*refs):
        ins, outs, sems = refs[:r_in], refs[r_in:r_in + r_out], refs[r_in + r_out:]
        rider.start(ins, outs, sems)
        rider.finish(ins, outs, sems)

    return pl.pallas_call(
        body, name=name, in_specs=rider.in_specs, out_specs=rider.out_specs, out_shape=rider.out_shapes,
        input_output_aliases=rider.aliases, scratch_shapes=rider.sems,
    )(*rider.ins)


def _hosted_call(body, args, *, name, grid, in_specs, out_specs, out_shape, scratch_shapes=(), sem, rider=None):
    scratch_shapes = list(scratch_shapes)
    if rider is None:
        res = pl.pallas_call(
            body, name=name, grid=grid, in_specs=in_specs, out_specs=out_specs, out_shape=out_shape,
            scratch_shapes=scratch_shapes, compiler_params=_params(sem, VMEM_LIMIT))(*args)
        return list(res), []
    n_in, n_out, n_sc = len(in_specs), len(out_specs), len(scratch_shapes)
    r_in, r_out = len(rider.ins), len(rider.out_shapes)
    last = tuple(g - 1 for g in grid)

    def hosted(*refs):
        p = 0
        parts = []
        for cnt in (n_in, r_in, n_out, r_out, n_sc):
            parts.append(refs[p:p + cnt])
            p += cnt
        ins, r_ins, outs, r_outs, scratch = parts
        sems = refs[p:]
        ids = [pl.program_id(a) for a in range(len(grid))]
        is_first = functools.reduce(jnp.logical_and, [i == 0 for i in ids])
        is_last = functools.reduce(jnp.logical_and, [i == e for i, e in zip(ids, last)])

        @pl.when(is_first)
        def _():
            rider.start(r_ins, r_outs, sems)

        body(*ins, *outs, *scratch)

        @pl.when(is_last)
        def _():
            rider.finish(r_ins, r_outs, sems)

    res = pl.pallas_call(
        hosted, name=name, grid=grid, in_specs=list(in_specs) + rider.in_specs, out_specs=list(out_specs) + rider.out_specs,
        out_shape=list(out_shape) + rider.out_shapes, scratch_shapes=scratch_shapes + rider.sems,
        input_output_aliases={n_in + i: n_out + j for i, j in rider.aliases.items()},
        compiler_params=_params(("arbitrary",) * len(grid), VMEM_LIMIT))(*args, *rider.ins)
    return list(res[:n_out]), list(res[n_out:])


def _gather_rider(ws):
    n = len(ws)
    return _Rider(
        ws, [jax.ShapeDtypeStruct(w.shape, w.dtype) for w in ws], [pltpu.SemaphoreType.DMA((n, 3))] * 4,
        lambda ins, outs, sems: _gather_send(outs, sems[0], sems[1]),
        lambda ins, outs, sems: _gather_finish(outs, *sems),
        aliases={a: a for a in range(n)})


def _copies_rider(ins, out_shapes, sem_shape, make):
    def start(r_ins, r_outs, sems):
        for cp in make(r_ins, r_outs, sems[0], sems[1]):
            cp.start()

    def finish(r_ins, r_outs, sems):
        for cp in make(r_ins, r_outs, sems[0], sems[1]):
            cp.wait()

    return _Rider(ins, out_shapes, [pltpu.SemaphoreType.DMA(sem_shape)] * 2, start, finish)


def _exchange_rider(gs):
    def make(g_refs, r_refs, send, recv):
        x, y, z = _place()
        return [pltpu.make_async_remote_copy(
            src_ref=g.at[:, pl.ds((1 - z) * (g.shape[1] // 2), g.shape[1] // 2)], dst_ref=r, send_sem=send.at[a],
            recv_sem=recv.at[a], device_id=(x, y, 1 - z), device_id_type=MESH)
            for a, (g, r) in enumerate(zip(g_refs, r_refs))]

    shapes = [jax.ShapeDtypeStruct((g.shape[0], g.shape[1] // 2, g.shape[2]), g.dtype) for g in gs]
    return _copies_rider(gs, shapes, (len(gs),), make)


def _add_half(g, recv, core, name):
    s, r, c = g.shape
    r2 = r // 2
    rb = r2
    for cand in (256, 128, 64):
        if r2 % cand == 0:
            rb = cand
            break
    g4 = g.reshape(s, 2, r2, c)

    def body(core_ref, g_ref, r_ref, o_ref):
        o_ref[...] = (g_ref[...] + r_ref[...]).astype(BF16)

    return pl.pallas_call(
        body, name=name,
        grid_spec=pltpu.PrefetchScalarGridSpec(
            num_scalar_prefetch=1, grid=(s, r2 // rb),
            in_specs=[pl.BlockSpec((None, None, rb, c), lambda i, j, cr: (i, cr[0], j, 0)),
                      pl.BlockSpec((None, rb, c), lambda i, j, cr: (i, j, 0))],
            out_specs=pl.BlockSpec((None, rb, c), lambda i, j, cr: (i, j, 0))),
        out_shape=jax.ShapeDtypeStruct((s, r2, c), BF16),
        compiler_params=_params(("parallel", "parallel")),
    )(core, g4, recv)


def _scatter_rider(ps):
    def make(p_refs, o_refs, send, recv):
        x, y, z = _place()
        copies = []
        for a, (p, o) in enumerate(zip(p_refs, o_refs)):
            for k, (dx, dy) in enumerate(_CHIP_OFFSETS):
                other = 2 * _flip(x, dx) + _flip(y, dy)
                copies.append(pltpu.make_async_remote_copy(
                    src_ref=p.at[other], dst_ref=o.at[k], send_sem=send.at[a, k], recv_sem=recv.at[a, k],
                    device_id=(_flip(x, dx), _flip(y, dy), z), device_id_type=MESH))
        return copies

    shapes = [jax.ShapeDtypeStruct((3,) + p.shape[1:], p.dtype) for p in ps]
    return _copies_rider(ps, shapes, (len(ps), 3), make)


def _sum_chips(p, landed, chip, name):
    _, r2, c = p.shape
    rb = r2
    for cand in (256, 128, 64):
        if r2 % cand == 0:
            rb = cand
            break

    def body(s_ref, p_ref, l_ref, o_ref):
        acc = p_ref[...].astype(F32)
        for k in range(3):
            acc = acc + l_ref[k].astype(F32)
        o_ref[...] = acc

    return pl.pallas_call(
        body, name=name,
        grid_spec=pltpu.PrefetchScalarGridSpec(
            num_scalar_prefetch=1, grid=(r2 // rb,),
            in_specs=[pl.BlockSpec((None, rb, c), lambda i, s: (s[0], i, 0)),
                      pl.BlockSpec((3, rb, c), lambda i, s: (0, i, 0))],
            out_specs=pl.BlockSpec((rb, c), lambda i, s: (i, 0))),
        out_shape=jax.ShapeDtypeStruct((r2, c), F32),
        compiler_params=_params(("parallel",)),
    )(chip, p, landed)


def _swap_rider(hs):
    def make(h_refs, o_refs, send, recv):
        x, y, z = _place()
        return [pltpu.make_async_remote_copy(
            src_ref=h, dst_ref=o, send_sem=send.at[a], recv_sem=recv.at[a], device_id=(x, y, 1 - z),
            device_id_type=MESH) for a, (h, o) in enumerate(zip(h_refs, o_refs))]

    return _copies_rider(hs, [jax.ShapeDtypeStruct(h.shape, h.dtype) for h in hs], (len(hs),), make)


SMALL_ROWS = 32
PACK_ROWS = 16


def _pack_small(st_post, st_ret, st_pre):
    d = st_post.shape[2]

    def body(po_ref, re_ref, pr_ref, o_ref):
        o_ref[...] = jnp.zeros(o_ref.shape, F32)
        o_ref[0:1, :] = pr_ref[0, 2:3, :] + pr_ref[1, 2:3, :] + pr_ref[2, 2:3, :]
        o_ref[1:2, :] = po_ref[0, 4:5, :] + po_ref[1, 4:5, :]
        o_ref[2:3, :] = po_ref[0, 5:6, :] + po_ref[1, 5:6, :]
        o_ref[3:4, 0:512] = re_ref[0, 0:1, :] + re_ref[1, 0:1, :]
        o_ref[4:5, :] = pr_ref[0, 3:4, :] + pr_ref[1, 3:4, :] + pr_ref[2, 3:4, :]
        o_ref[5:6, :] = pr_ref[0, 4:5, :] + pr_ref[1, 4:5, :] + pr_ref[2, 4:5, :]
        lane = lax.broadcasted_iota(jnp.int32, (1, LANES), 1)
        for row, src in ((6, 1), (10, 2)):
            acc = jnp.zeros((1, LANES), F32)
            for hd in range(HEADS):
                grp = re_ref[0, src:src + 1, hd * LANES:(hd + 1) * LANES] + re_ref[1, src:src + 1, hd * LANES:(hd + 1) * LANES]
                acc = acc + jnp.where(lane == hd, grp, 0.0)
            o_ref[row:row + 1, 0:LANES] = acc
        o_ref[7:8, :] = po_ref[0, 6:7, :] + po_ref[1, 6:7, :]
        o_ref[8:9, :] = pr_ref[2, 0:1, :]
        o_ref[9:10, :] = pr_ref[2, 1:2, :]
        for e in range(2):
            b = 12 + 6 * e
            o_ref[b:b + 1, :] = pr_ref[e, 0:1, :]
            o_ref[b + 1:b + 2, :] = pr_ref[e, 1:2, :]
            o_ref[b + 2:b + 3, :] = po_ref[e, 3:4, :]
            o_ref[b + 3:b + 4, :] = po_ref[e, 0:1, :]
            o_ref[b + 4:b + 5, :] = po_ref[e, 1:2, :]
            o_ref[b + 5:b + 6, :] = po_ref[e, 2:3, :]

    return pl.pallas_call(body, name="pack_small", out_shape=jax.ShapeDtypeStruct((SMALL_ROWS, d), F32))(st_post, st_ret, st_pre)


def _small_reduce(gathered):
    d = gathered.shape[2]

    def body(g_ref, o_ref):
        tot = g_ref[0, 0:PACK_ROWS, :]
        for dev in range(1, N_DEV):
            tot = tot + g_ref[dev, 0:PACK_ROWS, :]
        o_ref[0:PACK_ROWS, :] = tot
        for j in range(6):
            acc = g_ref[0, 12 + j:13 + j, :] + g_ref[0, 18 + j:19 + j, :]
            for dev in range(1, N_DEV):
                acc = acc + g_ref[dev, 12 + j:13 + j, :] + g_ref[dev, 18 + j:19 + j, :]
            if j < 2:
                acc = acc + o_ref[8 + j:9 + j, :]
            o_ref[PACK_ROWS + j:PACK_ROWS + j + 1, :] = acc
        o_ref[PACK_ROWS + 6:PACK_ROWS + 8, :] = jnp.zeros((2, d), F32)

    return pl.pallas_call(body, name="small_reduce", out_shape=jax.ShapeDtypeStruct((PACK_ROWS + 8, d), F32))(gathered)


_SMALL = (("g_attn", 0, 1024), ("g_ffn", 1, 1024), ("g_final", 2, 1024), ("g_ret", 3, 512), ("g_q_lora", 4, 384),
          ("g_kv_lora", 5, 256), ("ret_decay_fwd", 6, HEADS), ("ret_decay_bwd", 10, HEADS))
_SMALL_NAMES = tuple(s[0] for s in _SMALL) + ("c_ctx", "b_ada")


def _small_final(tot, dcc, sg8, ws, ms, vs):
    d = tot.shape[1]
    n = len(_SMALL_NAMES)

    def body(*refs):
        t_ref, dcc_ref, sg_ref = refs[0:3]
        w_refs, m_refs, v_refs = refs[3:3 + n], refs[3 + n:3 + 2 * n], refs[3 + 2 * n:3 + 3 * n]
        outs = refs[3 + 3 * n:]
        g_refs, d_refs, mo_refs, vo_refs = outs[0:n], outs[n:2 * n], outs[2 * n:3 * n], outs[3 * n:4 * n]
        l_ref = outs[4 * n]

        def update(i, g, sl=None):
            pick = (lambda r: r[...]) if sl is None else (lambda r: r[:, sl])
            dl, mn, vn = _adam_math(pick(w_refs[i]), g, pick(m_refs[i]), pick(v_refs[i]))
            if sl is None:
                g_refs[i][...], d_refs[i][...], mo_refs[i][...], vo_refs[i][...] = g, dl, mn, vn
            else:
                g_refs[i][:, sl], d_refs[i][:, sl], mo_refs[i][:, sl], vo_refs[i][:, sl] = g, dl, mn, vn

        for i, (name, row, width) in enumerate(_SMALL):
            g = t_ref[row:row + 1, 0:width]
            if name == "ret_decay_fwd":
                g = g * sg_ref[0:1, 0:width]
            elif name == "ret_decay_bwd":
                g = g * sg_ref[1:2, 0:width]
            update(i, g)
        i_cc, i_b = n - 2, n - 1
        cc = w_refs[i_cc][...]
        s = 1.0 / (1.0 + jnp.exp(-cc))
        dsilu = dcc_ref[0, 0:1, :] + dcc_ref[2, 0:1, :] + dcc_ref[4, 0:1, :] + dcc_ref[6, 0:1, :]
        update(i_cc, dsilu * (s * (1.0 + cc * (1.0 - s))))
        for j in range(6):
            update(i_b, t_ref[PACK_ROWS + j:PACK_ROWS + j + 1, :], pl.ds(j * d, d))
        l_ref[...] = jnp.broadcast_to((0.5 / d) * jnp.sum(t_ref[7:8, :], keepdims=True), l_ref.shape)

    shapes = [jax.ShapeDtypeStruct(a.shape, F32) for a in ws]
    outs = pl.pallas_call(
        body, name="small_final", out_shape=shapes * 4 + [jax.ShapeDtypeStruct((8, LANES), F32)],
    )(tot, dcc, sg8, *ws, *ms, *vs)
    return outs[0:n], outs[n:2 * n], outs[2 * n:3 * n], outs[3 * n:4 * n], outs[4 * n]


_WEIGHTS = ("c_ctx", "w_ada", "b_ada", "g_attn", "g_ffn", "w_in", "ret_decay_fwd", "ret_decay_bwd", "g_ret", "g_q_lora",
            "w_uq", "g_kv_lora", "w_ukv", "w_out", "w_ff1", "w_ff2", "g_final")
_BIG = ("w_in", "w_uq", "w_ukv", "w_out", "w_ff1", "w_ff2")


def kernel(x, c, ctx, c_ctx, w_ada, b_ada, g_attn, g_ffn, w_in, ret_decay_fwd, ret_decay_bwd, g_ret, g_q_lora, w_uq, g_kv_lora, w_ukv, w_out, w_ff1, w_ff2, g_final, loss_target, m_c_ctx, m_w_ada, m_b_ada, m_g_attn, m_g_ffn, m_w_in, m_ret_decay_fwd, m_ret_decay_bwd, m_g_ret, m_g_q_lora, m_w_uq, m_g_kv_lora, m_w_ukv, m_w_out, m_w_ff1, m_w_ff2, m_g_final, v_c_ctx, v_w_ada, v_b_ada, v_g_attn, v_g_ffn, v_w_in, v_ret_decay_fwd, v_ret_decay_bwd, v_g_ret, v_g_q_lora, v_w_uq, v_g_kv_lora, v_w_ukv, v_w_out, v_w_ff1, v_w_ff2, v_g_final):
    w = dict(c_ctx=c_ctx, w_ada=w_ada, b_ada=b_ada, g_attn=g_attn, g_ffn=g_ffn, w_in=w_in, ret_decay_fwd=ret_decay_fwd,
             ret_decay_bwd=ret_decay_bwd, g_ret=g_ret, g_q_lora=g_q_lora, w_uq=w_uq, g_kv_lora=g_kv_lora, w_ukv=w_ukv,
             w_out=w_out, w_ff1=w_ff1, w_ff2=w_ff2, g_final=g_final)
    m = dict(c_ctx=m_c_ctx, w_ada=m_w_ada, b_ada=m_b_ada, g_attn=m_g_attn, g_ffn=m_g_ffn, w_in=m_w_in,
             ret_decay_fwd=m_ret_decay_fwd, ret_decay_bwd=m_ret_decay_bwd, g_ret=m_g_ret, g_q_lora=m_g_q_lora, w_uq=m_w_uq,
             g_kv_lora=m_g_kv_lora, w_ukv=m_w_ukv, w_out=m_w_out, w_ff1=m_w_ff1, w_ff2=m_w_ff2, g_final=m_g_final)
    v = dict(c_ctx=v_c_ctx, w_ada=v_w_ada, b_ada=v_b_ada, g_attn=v_g_attn, g_ffn=v_g_ffn, w_in=v_w_in,
             ret_decay_fwd=v_ret_decay_fwd, ret_decay_bwd=v_ret_decay_bwd, g_ret=v_g_ret, g_q_lora=v_g_q_lora, w_uq=v_w_uq,
             g_kv_lora=v_g_kv_lora, w_ukv=v_w_ukv, w_out=v_w_out, w_ff1=v_w_ff1, w_ff2=v_w_ff2, g_final=v_g_final)
    xi, yi, ci = lax.axis_index("x"), lax.axis_index("y"), lax.axis_index("c")
    chip = 2 * xi + yi
    dev = 2 * chip + ci
    nex, seq, d = x.shape
    n_ada = w_ada.shape[2]

    c_all = _allgather8(jnp.pad(c, ((0, 8 - nex), (0, 0))), "ag_c")[:, :nex].reshape(N_DEV * nex, d)
    a_in = jnp.concatenate([c_all, c_ctx.reshape(1, d), jnp.zeros((7, d), F32)], axis=0)
    b_sh = lax.dynamic_slice(b_ada, (0, chip * n_ada), (1, n_ada))
    mod_sh = _mod_fwd(a_in, w_ada[0], b_sh)

    dec = jnp.zeros((8, LANES), F32).at[0, :HEADS].set(ret_decay_fwd[0]).at[1, :HEADS].set(ret_decay_bwd[0])
    lg8, sg8 = _decay_prep(dec)
    lg = lg8[:2, :HEADS]

    shard = {k: w[k][0] for k in _BIG}
    shard["w_uq"] = jnp.pad(shard["w_uq"], ((0, 0), (0, MLA_HEAD - MLA_NOPE - MLA_ROPE)))
    slot = chip.reshape(1).astype(jnp.int32)
    core = ci.reshape(1).astype(jnp.int32)
    slots = {k: _cast_into_slot(shard[k], slot, "cast_" + k) for k in _BIG}
    mod8, w_in_f, w_uq_k, w_ukv_k = _run_rider(
        _merge_riders(_gather8_rider(mod_sh), _gather_rider([slots[k] for k in _EARLY])), "ag_early")
    w_in_k = jnp.pad(w_in_f.transpose(1, 0, 2).reshape(d, IN_COLS), ((0, 0), (0, IN_PAD - IN_COLS)))
    mod_all = mod8[0::2].transpose(1, 0, 2).reshape(a_in.shape[0], N_CHIPS * n_ada)
    mod_me = lax.dynamic_slice(mod_all, (nex * dev, 0), (nex, N_CHIPS * n_ada)).reshape(nex, 6, d)
    mod_c = mod_all[N_DEV * nex].reshape(1, 6, d)
    modv = jnp.pad(jnp.concatenate([mod_me, mod_c], axis=0), ((0, 0), (0, 2), (0, 0)))

    gx, g_early, late, st_post, st_ret, st_pre = _local_step(
        x, ctx, loss_target, modv, lg, g_attn, g_ffn, g_final.reshape(1, d), g_ret, g_q_lora, g_kv_lora,
        w_in_k, w_uq_k, w_ukv_k, [slots[k] for k in _LATE], (core, slot))

    per_chip = IN_COLS // N_CHIPS
    g4 = [
        g_early[0][:, :IN_COLS].reshape(d, N_CHIPS, per_chip).transpose(1, 0, 2),
        g_early[1][:, :, :MLA_NOPE + MLA_ROPE],
        g_early[2],
    ]
    *got, gathered = _run_rider(_merge_riders(_exchange_rider(g4), _gather8_rider(_pack_small(st_post, st_ret, st_pre))),
                                "rs_exchange")
    partial = [_add_half(g, r, core, "add_half_" + k) for g, r, k in zip(g4, got, _EARLY)]
    tot = _small_reduce(gathered)
    dm = jnp.concatenate([
        gathered[:, 12:24].reshape(N_DEV * nex, 6 * d),
        jnp.concatenate([tot[8:10].reshape(1, 2 * d), jnp.zeros((1, 4 * d), F32)], axis=1),
        jnp.zeros((7, 6 * d), F32)], axis=0)
    dm_sh = lax.dynamic_slice(dm, (0, chip * n_ada), (dm.shape[0], n_ada))
    g_ada, da = _mod_bwd(a_in, dm_sh, w_ada[0])
    *landed, dcc = _run_rider(_merge_riders(_scatter_rider(partial), _gather8_rider(da[N_DEV * nex:])), "rs_scatter")
    mine = [_sum_chips(p, l, slot, "sum_chips_" + k) for p, l, k in zip(partial, landed, _EARLY)]
    theirs = _run_rider(_swap_rider(mine), "rs_swap")
    halves = dict(zip(_EARLY, zip(mine, theirs)))
    halves.update(zip(_LATE, late))
    grad, delta, new_m, new_v = {}, {}, {}, {}
    for k in _BIG:
        a, b = halves[k]
        shp = w[k].shape
        outs = _adamw_halves(w[k].reshape(shp[1:]), a, b, m[k].reshape(shp[1:]), v[k].reshape(shp[1:]), core, "adamw_" + k)
        grad[k], delta[k], new_m[k], new_v[k] = [o.reshape(shp) for o in outs]

    shp = w_ada.shape
    outs = _adamw(w_ada[0], g_ada, m["w_ada"][0], v["w_ada"][0], "adamw_w_ada")
    grad["w_ada"] = g_ada.reshape(shp)
    delta["w_ada"], new_m["w_ada"], new_v["w_ada"] = [o.reshape(shp) for o in outs]
    rows = [{k: t[k].reshape(1, -1) for k in _SMALL_NAMES} for t in (w, m, v)]
    small = _small_final(tot, dcc, sg8, *[[t[k] for k in _SMALL_NAMES] for t in rows])
    for res, outs in zip((grad, delta, new_m, new_v), small[:4]):
        for k, o in zip(_SMALL_NAMES, outs):
            res[k] = o.reshape(w[k].shape)
    return (small[4][0, 0], gx, *[grad[k] for k in _WEIGHTS], *[delta[k] for k in _WEIGHTS],
            *[new_m[k] for k in _WEIGHTS], *[new_v[k] for k in _WEIGHTS])
```

```python
import functools
import math

import jax
import jax.numpy as jnp
from jax import lax
from jax.experimental import pallas as pl
from jax.experimental.pallas import tpu as pltpu

F32 = jnp.float32
BF16 = jnp.bfloat16
MESH = pl.DeviceIdType.MESH

EPS = 1e-6
D_MODEL = 1024
D_FF = 4096
HEADS = 4
RET_DK = 64
RET_DV = 128
MLA_NOPE = 128
MLA_ROPE = 64
MLA_HEAD = 256
Q_LORA = 384
KV_LORA = 256
GRID_W = 64
ROPE_BASE = 10000.0
IN_COLS = 2240
IN_PAD = 2304
PG_COLS = 1152
N_CHIPS = 4
N_DEV = 8
LANES = 128
ADAM_LR = 0.001
ADAM_B1 = 0.9
ADAM_B2 = 0.999
ADAM_EPS = 1e-08
ADAM_WD = 0.01
ADAM_STEP = 10
VMEM_LIMIT = 56 * 1024 * 1024


def _dot(a, b):
    return jnp.dot(a, b, preferred_element_type=F32)


def _dot_nt(a, b):
    return lax.dot_general(a, b, (((1,), (1,)), ((), ())), preferred_element_type=F32)


def _dot_tn(a, b):
    return lax.dot_general(a, b, (((0,), (0,)), ((), ())), preferred_element_type=F32)


def _params(sem=None, vmem=None):
    return pltpu.CompilerParams(dimension_semantics=sem, vmem_limit_bytes=vmem)


def _full(shape):
    n = len(shape)
    return pl.BlockSpec(shape, lambda *_: (0,) * n)


def _rope(x, cos, sin):
    w = x.shape[-1]
    lo = (lax.broadcasted_iota(jnp.int32, (1, w), 1) % 64) < 32
    swapped = jnp.where(lo, pltpu.roll(x, w - 32, 1), pltpu.roll(x, 32, 1))
    return x * cos + swapped * sin


def _rope_t(g, cos, sin):
    w = g.shape[-1]
    lo = (lax.broadcasted_iota(jnp.int32, (1, w), 1) % 64) < 32
    t = g * sin
    swapped = jnp.where(lo, pltpu.roll(t, w - 32, 1), pltpu.roll(t, 32, 1))
    return g * cos + swapped


def _rope_tables(seq, tm):
    rows = seq // GRID_W
    row = jnp.repeat(jnp.arange(rows, dtype=F32), GRID_W)
    col = jnp.tile(jnp.arange(GRID_W, dtype=F32), rows)
    n_freq = RET_DK // 4
    freq = ROPE_BASE ** (-jnp.arange(n_freq, dtype=F32) / n_freq)
    ang = jnp.concatenate([row[:, None] * freq, col[:, None] * freq], axis=-1)
    cos, sin = jnp.cos(ang), jnp.sin(ang)
    cos_t = jnp.tile(jnp.concatenate([cos, cos], -1), (1, HEADS))
    sin_t = jnp.tile(jnp.concatenate([-sin, sin], -1), (1, HEADS))
    cos_t = jnp.concatenate([cos_t, jnp.ones((tm, 4 * RET_DK), F32)], 0)
    sin_t = jnp.concatenate([sin_t, jnp.zeros((tm, 4 * RET_DK), F32)], 0)
    return cos_t, sin_t


def _adam_math(w, g, m, v):
    mn = ADAM_B1 * m + (1.0 - ADAM_B1) * g
    vn = ADAM_B2 * v + (1.0 - ADAM_B2) * (g * g)
    m_hat = mn / (1.0 - ADAM_B1 ** ADAM_STEP)
    v_hat = vn / (1.0 - ADAM_B2 ** ADAM_STEP)
    return -ADAM_LR * (m_hat / (jnp.sqrt(v_hat) + ADAM_EPS) + ADAM_WD * w), mn, vn


def _cast_into_slot(w, slot, name):
    r, c = w.shape
    rb = max(b for b in range(16, 257, 16) if r % b == 0)

    def body(s_ref, w_ref, o_ref):
        o_ref[...] = w_ref[...].astype(BF16)

    return pl.pallas_call(
        body, name=name,
        grid_spec=pltpu.PrefetchScalarGridSpec(
            num_scalar_prefetch=1, grid=(r // rb,),
            in_specs=[pl.BlockSpec((rb, c), lambda i, s: (i, 0))],
            out_specs=pl.BlockSpec((None, rb, c), lambda i, s: (s[0], i, 0))),
        out_shape=jax.ShapeDtypeStruct((N_CHIPS, r, c), BF16),
        compiler_params=_params(("parallel",)),
    )(slot, w)


def _adamw_halves(w, mine, theirs, m, v, core, name):
    r, c = w.shape
    r2 = r // 2
    rb = max(b for b in range(8, r2 + 1, 8) if r2 % b == 0 and b * c * 4 <= (1 << 21))
    nbh = r2 // rb

    def body(z_ref, w_ref, a_ref, b_ref, m_ref, v_ref, g_ref, d_ref, mo_ref, vo_ref):
        here = (pl.program_id(0) // nbh) == z_ref[0]
        gg = jnp.where(here, a_ref[...], b_ref[...])
        g_ref[...] = gg
        d_ref[...], mo_ref[...], vo_ref[...] = _adam_math(w_ref[...], gg, m_ref[...], v_ref[...])

    spec = pl.BlockSpec((rb, c), lambda i, z: (i, 0))
    a_spec = pl.BlockSpec((rb, c), lambda i, z: (jnp.clip(i - z[0] * nbh, 0, nbh - 1), 0))
    b_spec = pl.BlockSpec((rb, c), lambda i, z: (jnp.clip(i - (1 - z[0]) * nbh, 0, nbh - 1), 0))
    shp = jax.ShapeDtypeStruct((r, c), F32)
    return pl.pallas_call(
        body, name=name,
        grid_spec=pltpu.PrefetchScalarGridSpec(
            num_scalar_prefetch=1, grid=(r // rb,), in_specs=[spec, a_spec, b_spec, spec, spec], out_specs=[spec] * 4),
        out_shape=[shp] * 4,
        compiler_params=_params(("parallel",)),
    )(core, w, mine, theirs, m, v)


def _adamw(w, g, m, v, name):
    r, c = w.shape
    rb = r
    for cand in (256, 128, 64, 32, 16, 8):
        if r % cand == 0 and cand * c * 4 <= (1 << 20):
            rb = cand
            break
    if r * c * 4 <= (1 << 20):
        rb = r

    def body(w_ref, g_ref, m_ref, v_ref, d_ref, mo_ref, vo_ref):
        d_ref[...], mo_ref[...], vo_ref[...] = _adam_math(w_ref[...], g_ref[...], m_ref[...], v_ref[...])

    spec = pl.BlockSpec((rb, c), lambda i: (i, 0))
    shp = jax.ShapeDtypeStruct((r, c), F32)
    return pl.pallas_call(
        body, name=name, grid=(r // rb,), in_specs=[spec] * 4, out_specs=[spec] * 3, out_shape=[shp] * 3,
        compiler_params=_params(("parallel",)),
    )(w, g, m, v)


def _decay_prep(dec):
    def body(d_ref, lg_ref, sg_ref):
        d = d_ref[...]
        lg_ref[...] = jnp.minimum(d, 0.0) - jnp.log(1.0 + jnp.exp(-jnp.abs(d)))
        sg_ref[...] = 1.0 / (1.0 + jnp.exp(d))

    shp = jax.ShapeDtypeStruct(dec.shape, F32)
    return pl.pallas_call(body, name="decay_prep", out_shape=[shp, shp])(dec)


def _mod_fwd(a_in, w_ada, b_sh):
    rows, d = a_in.shape
    n = w_ada.shape[1]
    bn = 512

    def body(a_ref, w_ref, b_ref, o_ref):
        a = a_ref[...]
        s = (a / (1.0 + jnp.exp(-a))).astype(BF16)
        o_ref[...] = _dot(s, w_ref[...].astype(BF16)) + b_ref[...]

    return pl.pallas_call(
        body, name="mod_fwd", grid=(n // bn,),
        in_specs=[_full((rows, d)), pl.BlockSpec((d, bn), lambda j: (0, j)), pl.BlockSpec((1, bn), lambda j: (0, j))],
        out_specs=pl.BlockSpec((rows, bn), lambda j: (0, j)),
        out_shape=jax.ShapeDtypeStruct((rows, n), F32),
        compiler_params=_params(("parallel",)),
    )(a_in, w_ada, b_sh)


def _mod_bwd(a_in, dm, w_ada):
    rows, d = a_in.shape
    n = w_ada.shape[1]
    bn = 512
    nb = n // bn

    def body(a_ref, dm_ref, w_ref, gw_ref, da_ref):
        j = pl.program_id(0)
        a = a_ref[...]
        s = (a / (1.0 + jnp.exp(-a))).astype(BF16)
        dmb = dm_ref[...].astype(BF16)
        gw_ref[...] = _dot_tn(s, dmb)
        part = _dot_nt(dmb, w_ref[...].astype(BF16))

        @pl.when(j == 0)
        def _():
            da_ref[...] = part

        @pl.when(j > 0)
        def _():
            da_ref[...] += part

    return pl.pallas_call(
        body, name="mod_bwd", grid=(nb,),
        in_specs=[_full((rows, d)), pl.BlockSpec((rows, bn), lambda j: (0, j)), pl.BlockSpec((d, bn), lambda j: (0, j))],
        out_specs=[pl.BlockSpec((d, bn), lambda j: (0, j)), _full((rows, d))],
        out_shape=[jax.ShapeDtypeStruct((d, n), F32), jax.ShapeDtypeStruct((rows, d), F32)],
        compiler_params=_params(("arbitrary",)),
    )(a_in, dm, w_ada)


def _pre_fwd(x2, ctx2, modv, g_attn, w_in, g_q, g_kv, w_uq, w_ukv, cos_t, sin_t, *, seq, tm):
    t_lat, d = x2.shape
    t_ctx = ctx2.shape[0]
    nl, nc = t_lat // tm, t_ctx // tm
    n_all = t_lat + t_ctx
    tpe = seq // tm
    nex = t_lat // seq

    def body(x_ref, c_ref, mod_ref, g_ref, win_ref, gq_ref, gkv_ref, wuq_ref, wukv_ref, cos_ref, sin_ref,
             h_ref, pg_ref, rq_ref, rk_ref, rv_ref, nq_ref, nkv_ref, q_ref, k_ref, v_ref):
        i = pl.program_id(0)
        xt = jnp.where(i < nl, x_ref[...], c_ref[...])
        sh = mod_ref[0, 0:1, :]
        sc = mod_ref[0, 1:2, :]
        r = lax.rsqrt(jnp.mean(xt * xt, axis=-1, keepdims=True) + EPS)
        hb = ((xt * r) * g_ref[...] * (1.0 + sc) + sh).astype(BF16)
        h_ref[...] = hb
        p = _dot_nt(hb, win_ref[...])
        cos = cos_ref[...]
        sin = sin_ref[...]
        rq_ref[...] = _rope(p[:, 0:256], cos, sin).astype(BF16)
        rk_ref[...] = _rope(p[:, 256:512] * (RET_DK ** -0.5), cos, sin).astype(BF16)
        rv_ref[...] = p[:, 512:1024].astype(BF16)
        pg_ref[...] = p[:, 1024:2176]
        cq = p[:, 1536:1920]
        ckv = p[:, 1920:2176]
        nqb = (cq * lax.rsqrt(jnp.mean(cq * cq, axis=-1, keepdims=True) + EPS) * gq_ref[...]).astype(BF16)
        nkvb = (ckv * lax.rsqrt(jnp.mean(ckv * ckv, axis=-1, keepdims=True) + EPS) * gkv_ref[...]).astype(BF16)
        nq_ref[...] = nqb
        nkv_ref[...] = nkvb
        cos1 = cos[:, 0:LANES]
        sin1 = sin[:, 0:LANES]
        kpe = _rope(p[:, 2176:2304], cos1, sin1).astype(BF16)
        for hd in range(HEADS):
            o = hd * MLA_HEAD
            qh = _dot_nt(nqb, wuq_ref[hd])
            q_ref[:, o:o + 128] = qh[:, 0:128].astype(BF16)
            q_ref[:, o + 128:o + 256] = _rope(qh[:, 128:256], cos1, sin1).astype(BF16)
            kvh = _dot(nkvb, wukv_ref[hd])
            k_ref[:, o:o + 128] = kvh[:, 0:128].astype(BF16)
            k_ref[:, o + 128:o + 256] = kpe
            v_ref[:, hd * 128:(hd + 1) * 128] = kvh[:, 128:256].astype(BF16)

    def tile(width):
        return pl.BlockSpec((tm, width), lambda i: (i, 0))

    widths = (d, PG_COLS, 256, 256, 512, Q_LORA, KV_LORA, HEADS * MLA_HEAD, HEADS * MLA_HEAD, HEADS * 128)
    dtypes = (BF16, F32, BF16, BF16, BF16, BF16, BF16, BF16, BF16, BF16)
    tab = pl.BlockSpec((tm, 256), lambda i: (jnp.where(i < nl, i % tpe, tpe), 0))
    return pl.pallas_call(
        body, name="pre_fwd", grid=(nl + nc,),
        in_specs=[
            pl.BlockSpec((tm, d), lambda i: (jnp.minimum(i, nl - 1), 0)),
            pl.BlockSpec((tm, d), lambda i: (jnp.maximum(i - nl, 0), 0)),
            pl.BlockSpec((1, 8, d), lambda i: (jnp.minimum(i // tpe, nex), 0, 0)),
            _full((1, d)), _full(w_in.shape), _full((1, Q_LORA)), _full((1, KV_LORA)),
            _full(w_uq.shape), _full(w_ukv.shape), tab, tab,
        ],
        out_specs=[tile(w) for w in widths],
        out_shape=[jax.ShapeDtypeStruct((n_all, w), dt) for w, dt in zip(widths, dtypes)],
        compiler_params=_params(("parallel",), VMEM_LIMIT),
    )(x2, ctx2, modv, g_attn, w_in, g_q, g_kv, w_uq, w_ukv, cos_t, sin_t)


def _post(yret, ymla, x2, tgt2, modv, g_ffn, g_fin, w_out, w_ff1, w_ff2, *, seq, tm):
    t_lat, d = x2.shape
    nl = t_lat // tm
    tpe = seq // tm
    nex = t_lat // seq
    n_slab = w_ff1.shape[0]
    fs = w_ff1.shape[2]

    def body(yr_ref, ym_ref, x_ref, t_ref, mod_ref, gf_ref, gl_ref, wo_ref, w1_ref, w2_ref,
             mix_ref, a_ref, du_ref, h2_ref, df_ref, dmo_ref, dmix_ref, dxm_ref, st_ref, ru_ref):
        i = pl.program_id(0)
        gt_a = mod_ref[0, 2:3, :]
        sh_f = mod_ref[0, 3:4, :]
        sc_f = mod_ref[0, 4:5, :]
        gt_f = mod_ref[0, 5:6, :]
        g_ffn_v = gf_ref[...]
        g_fin_v = gl_ref[...]
        yr = yr_ref[...]
        ym = ym_ref[...]
        mix_ref[:, 0:512] = yr
        mix_ref[:, 512:1024] = ym
        op = _dot(yr, wo_ref[0:512, :]) + _dot(ym, wo_ref[512:1024, :])
        x_mid = x_ref[...] + gt_a * op
        r2 = lax.rsqrt(jnp.mean(x_mid * x_mid, axis=-1, keepdims=True) + EPS)
        xh2 = x_mid * r2
        h2b = (xh2 * g_ffn_v * (1.0 + sc_f) + sh_f).astype(BF16)
        h2_ref[...] = h2b
        f = jnp.zeros((tm, d), F32)
        for s in range(n_slab):
            ru = jnp.maximum(_dot(h2b, w1_ref[s]), 0.0)
            ru_ref[:, s * fs:(s + 1) * fs] = ru
            ab = (ru * ru).astype(BF16)
            a_ref[:, s * fs:(s + 1) * fs] = ab
            f = f + _dot(ab, w2_ref[s * fs:(s + 1) * fs, :])
        x_out = x_mid + gt_f * f
        r3 = lax.rsqrt(jnp.mean(x_out * x_out, axis=-1, keepdims=True) + EPS)
        xh3 = x_out * r3
        err = xh3 * g_fin_v - t_ref[...]
        dy = err * (1.0 / d)
        dxh3 = dy * g_fin_v
        dx_out = r3 * (dxh3 - xh3 * jnp.mean(dxh3 * xh3, axis=-1, keepdims=True))
        dfb = (dx_out * gt_f).astype(BF16)
        df_ref[...] = dfb
        dh2 = jnp.zeros((tm, d), F32)
        for s in range(n_slab):
            da = _dot_nt(dfb, w2_ref[s * fs:(s + 1) * fs, :])
            dub = (da * (2.0 * ru_ref[:, s * fs:(s + 1) * fs])).astype(BF16)
            du_ref[:, s * fs:(s + 1) * fs] = dub
            dh2 = dh2 + _dot_nt(dub, w1_ref[s])
        dxh2 = dh2 * (1.0 + sc_f) * g_ffn_v
        dx_mid = dx_out + r2 * (dxh2 - xh2 * jnp.mean(dxh2 * xh2, axis=-1, keepdims=True))
        dxm_ref[...] = dx_mid
        dmob = (dx_mid * gt_a).astype(BF16)
        dmo_ref[...] = dmob
        dmix_ref[...] = _dot_nt(dmob, wo_ref[...]).astype(BF16)

        def rsum(v):
            return jnp.sum(v, axis=0, keepdims=True)

        stats = jnp.concatenate([
            rsum(dh2), rsum(dh2 * xh2 * g_ffn_v), rsum(dx_out * f), rsum(dx_mid * op),
            rsum(dh2 * (1.0 + sc_f) * xh2), rsum(dy * xh3), rsum(err * err), jnp.zeros((1, d), F32)], axis=0)

        @pl.when(i % tpe == 0)
        def _():
            st_ref[0] = stats

        @pl.when(i % tpe != 0)
        def _():
            st_ref[0] += stats

    def tile(width):
        return pl.BlockSpec((tm, width), lambda i: (i, 0))

    widths = (d, D_FF, D_FF, d, d, d, d, d)
    dtypes = (BF16, BF16, BF16, BF16, BF16, BF16, BF16, F32)
    const = pl.Buffered(1)
    return pl.pallas_call(
        body, name="post", grid=(nl,),
        in_specs=[
            tile(512), tile(512), tile(d), tile(d),
            pl.BlockSpec((1, 8, d), lambda i: (i // tpe, 0, 0)),
            _full((1, d)), _full((1, d)),
            pl.BlockSpec(w_out.shape, lambda i: (0, 0), pipeline_mode=const),
            pl.BlockSpec(w_ff1.shape, lambda i: (0, 0, 0), pipeline_mode=const),
            pl.BlockSpec(w_ff2.shape, lambda i: (0, 0), pipeline_mode=const),
        ],
        out_specs=[tile(w) for w in widths] + [pl.BlockSpec((1, 8, d), lambda i: (i // tpe, 0, 0))],
        out_shape=[jax.ShapeDtypeStruct((t_lat, w), dt) for w, dt in zip(widths, dtypes)]
        + [jax.ShapeDtypeStruct((nex, 8, d), F32)],
        scratch_shapes=[pltpu.VMEM((tm, D_FF), F32)],
        compiler_params=_params(("arbitrary",), VMEM_LIMIT),
    )(yret, ymla, x2, tgt2, modv, g_ffn, g_fin, w_out, w_ff1, w_ff2)


def _pre_bwd(x2, ctx2, modv, g_attn, pg, drq, drk, dkc_r, drv, dvc_r, drg, dq_m, dkl, dkc, dvl, dvc, dxm,
             w_in, g_q, g_kv, w_uq, w_ukv, cos_t, sin_t, *, seq, tm, rider=None):
    t_lat, d = x2.shape
    t_ctx = ctx2.shape[0]
    nl, nc = t_lat // tm, t_ctx // tm
    n_all = t_lat + t_ctx
    tpe = seq // tm
    nex = t_lat // seq

    def body(x_ref, c_ref, mod_ref, g_ref, pg_ref, drq_ref, drk_ref, dkcr_ref, drv_ref, dvcr_ref, drg_ref,
             dq_ref, dkl_ref, dkc_ref, dvl_ref, dvc_ref, dxm_ref, win_ref, gq_ref, gkv_ref, wuq_ref, wukv_ref,
             cos_ref, sin_ref, dpb_ref, dqf_ref, dkvf_ref, gx_ref, st_ref):
        i = pl.program_id(0)
        lat = i < nl
        latf = lat.astype(F32)
        cos = cos_ref[...]
        sin = sin_ref[...]
        cos1 = cos[:, 0:LANES]
        sin1 = sin[:, 0:LANES]
        d_rq = _rope_t(drq_ref[...] * latf, cos, sin)
        d_rk = _rope_t(jnp.where(lat, drk_ref[...], dkcr_ref[...]), cos, sin) * (RET_DK ** -0.5)
        d_rv = jnp.where(lat, drv_ref[...], dvcr_ref[...])
        d_rg = drg_ref[...] * latf
        dq_all = dq_ref[...] * latf
        dk_all = jnp.where(lat, dkl_ref[...], dkc_ref[...])
        dv_all = jnp.where(lat, dvl_ref[...], dvc_ref[...])
        dnq = jnp.zeros((tm, Q_LORA), F32)
        dnkv = jnp.zeros((tm, KV_LORA), F32)
        dkpe = jnp.zeros((tm, LANES), F32)
        for hd in range(HEADS):
            o = hd * MLA_HEAD
            dqh = jnp.concatenate([dq_all[:, o:o + 128], _rope_t(dq_all[:, o + 128:o + 256], cos1, sin1)],
                                  axis=1).astype(BF16)
            dqf_ref[:, o:o + 256] = dqh
            dnq = dnq + _dot(dqh, wuq_ref[hd])
            dkpe = dkpe + dk_all[:, o + 128:o + 256]
            dkvh = jnp.concatenate([dk_all[:, o:o + 128], dv_all[:, hd * 128:(hd + 1) * 128]], axis=1).astype(BF16)
            dkvf_ref[:, o:o + 256] = dkvh
            dnkv = dnkv + _dot_nt(dkvh, wukv_ref[hd])
        d_kpe = _rope_t(dkpe, cos1, sin1)
        pgv = pg_ref[...]
        cq = pgv[:, 512:896]
        ckv = pgv[:, 896:1152]
        rq_ = lax.rsqrt(jnp.mean(cq * cq, axis=-1, keepdims=True) + EPS)
        cqh = cq * rq_
        dcqh = dnq * gq_ref[...]
        d_cq = rq_ * (dcqh - cqh * jnp.mean(dcqh * cqh, axis=-1, keepdims=True))
        rkv_ = lax.rsqrt(jnp.mean(ckv * ckv, axis=-1, keepdims=True) + EPS)
        ckvh = ckv * rkv_
        dckvh = dnkv * gkv_ref[...]
        d_ckv = rkv_ * (dckvh - ckvh * jnp.mean(dckvh * ckvh, axis=-1, keepdims=True))
        dpb = jnp.concatenate([d_rq, d_rk, d_rv, d_rg, d_cq, d_ckv, d_kpe], axis=1).astype(BF16)
        dpb_ref[...] = dpb
        dh = _dot(dpb, win_ref[...])
        xt = jnp.where(lat, x_ref[...], c_ref[...])
        sc = mod_ref[0, 1:2, :]
        g = g_ref[...]
        r = lax.rsqrt(jnp.mean(xt * xt, axis=-1, keepdims=True) + EPS)
        xh = xt * r
        dxh = dh * (1.0 + sc) * g
        dx = r * (dxh - xh * jnp.mean(dxh * xh, axis=-1, keepdims=True))

        @pl.when(lat)
        def _():
            gx_ref[...] = dxm_ref[...] + dx

        def rsum(v):
            return jnp.sum(v, axis=0, keepdims=True)

        def widen(v):
            return jnp.concatenate([v, jnp.zeros((1, d - v.shape[1]), F32)], axis=1)

        stats = jnp.concatenate([
            rsum(dh), rsum(dh * xh * g), rsum(dh * (1.0 + sc) * xh), widen(rsum(dnq * cqh)), widen(rsum(dnkv * ckvh)),
            jnp.zeros((3, d), F32)], axis=0)
        first = jnp.logical_or(jnp.logical_and(lat, i % tpe == 0), i == nl)

        @pl.when(first)
        def _():
            st_ref[0] = stats

        @pl.when(jnp.logical_not(first))
        def _():
            st_ref[0] += stats

    def lat_tile(width):
        return pl.BlockSpec((tm, width), lambda i: (jnp.minimum(i, nl - 1), 0))

    def ctx_tile(width):
        return pl.BlockSpec((tm, width), lambda i: (jnp.maximum(i - nl, 0), 0))

    def tile(width):
        return pl.BlockSpec((tm, width), lambda i: (i, 0))

    tab = pl.BlockSpec((tm, 256), lambda i: (jnp.where(i < nl, i % tpe, tpe), 0))
    ex = pl.BlockSpec((1, 8, d), lambda i: (jnp.minimum(i // tpe, nex), 0, 0))
    return _hosted_call(
        body, (x2, ctx2, modv, g_attn, pg, drq, drk, dkc_r, drv, dvc_r, drg, dq_m, dkl, dkc, dvl, dvc, dxm,
               w_in, g_q, g_kv, w_uq, w_ukv, cos_t, sin_t), name="pre_bwd", grid=(nl + nc,),
        in_specs=[
            lat_tile(d), ctx_tile(d), ex, _full((1, d)), tile(PG_COLS),
            lat_tile(256), lat_tile(256), ctx_tile(256), lat_tile(512), ctx_tile(512), lat_tile(512),
            lat_tile(1024), lat_tile(1024), ctx_tile(1024), lat_tile(512), ctx_tile(512), lat_tile(d),
            _full(w_in.shape), _full((1, Q_LORA)), _full((1, KV_LORA)), _full(w_uq.shape), _full(w_ukv.shape),
            tab, tab,
        ],
        out_specs=[tile(IN_PAD), tile(1024), tile(1024), lat_tile(d), ex],
        out_shape=[
            jax.ShapeDtypeStruct((n_all, IN_PAD), BF16), jax.ShapeDtypeStruct((n_all, 1024), BF16),
            jax.ShapeDtypeStruct((n_all, 1024), BF16), jax.ShapeDtypeStruct((t_lat, d), F32),
            jax.ShapeDtypeStruct((nex + 1, 8, d), F32),
        ],
        sem=("arbitrary",), rider=rider)


MLA_SCALE = 1.0 / math.sqrt(MLA_NOPE + MLA_ROPE)
KEY_BLOCK = 2048


def _mla_specs(t_lat, seq, ctx_len, tq):
    nqt = seq // tq
    cb = t_lat // ctx_len
    q = pl.BlockSpec((tq, MLA_HEAD), lambda b, h, j: (b * nqt + j, h))
    kl = pl.BlockSpec((seq, MLA_HEAD), lambda b, h, j: (b, h))
    kc = pl.BlockSpec((ctx_len, MLA_HEAD), lambda b, h, j: (cb + b, h))
    vl = pl.BlockSpec((seq, 128), lambda b, h, j: (b, h))
    vc = pl.BlockSpec((ctx_len, 128), lambda b, h, j: (cb + b, h))
    o = pl.BlockSpec((tq, 128), lambda b, h, j: (b * nqt + j, h))
    return q, kl, kc, vl, vc, o


def _mla_fwd(q, k, v, *, t_lat, seq, ctx_len, tq, rider=None):
    nex = t_lat // seq

    def body(q_ref, kl_ref, kc_ref, vl_ref, vc_ref, o_ref, lse_ref):
        qb = q_ref[...]
        s = _dot_nt(qb, kl_ref[...]) * MLA_SCALE
        sc = _dot_nt(qb, kc_ref[...]) * MLA_SCALE
        m = jnp.maximum(jnp.max(s, axis=-1, keepdims=True), jnp.max(sc, axis=-1, keepdims=True))
        p = jnp.exp(s - m)
        pc = jnp.exp(sc - m)
        total = jnp.sum(p, axis=-1, keepdims=True) + jnp.sum(pc, axis=-1, keepdims=True)
        o = _dot(p.astype(BF16), vl_ref[...]) + _dot(pc.astype(BF16), vc_ref[...])
        o_ref[...] = (o * (1.0 / total)).astype(BF16)
        lse_ref[...] = jnp.broadcast_to(m + jnp.log(total), lse_ref.shape)

    qs, kl, kc, vl, vc, os_ = _mla_specs(t_lat, seq, ctx_len, tq)
    return _hosted_call(
        body, (q, k, k, v, v), name="mla_fwd", grid=(nex, HEADS, seq // tq),
        in_specs=[qs, kl, kc, vl, vc], out_specs=[os_, os_],
        out_shape=[jax.ShapeDtypeStruct((t_lat, HEADS * 128), BF16), jax.ShapeDtypeStruct((t_lat, HEADS * 128), F32)],
        sem=("parallel", "parallel", "arbitrary"), rider=rider)


def _mla_bwd(q, k, v, ymla, lse, dmix, *, t_lat, seq, ctx_len, tq, rider=None):
    nex = t_lat // seq
    nqt = seq // tq
    t_ctx = nex * ctx_len
    kb = min(KEY_BLOCK, seq)

    def body(q_ref, kl_ref, kc_ref, vl_ref, vc_ref, o_ref, lse_ref, do_ref, dq_ref, dkl_ref, dkc_ref, dvl_ref, dvc_ref):
        j = pl.program_id(2)

        @pl.when(j == 0)
        def _():
            dkl_ref[...] = jnp.zeros(dkl_ref.shape, F32)
            dkc_ref[...] = jnp.zeros(dkc_ref.shape, F32)
            dvl_ref[...] = jnp.zeros(dvl_ref.shape, F32)
            dvc_ref[...] = jnp.zeros(dvc_ref.shape, F32)

        qb = q_ref[...]
        dob = do_ref[...]
        delta = jnp.sum(dob.astype(F32) * o_ref[...].astype(F32), axis=-1, keepdims=True)
        lse_row = lse_ref[:, 0:1]

        def block(k_ref, v_ref, dk_ref, dv_ref, rows):
            kbl = k_ref[rows, :]
            vbl = v_ref[rows, :]
            p = jnp.exp(_dot_nt(qb, kbl) * MLA_SCALE - lse_row)
            ds = (p * (_dot_nt(dob, vbl) - delta) * MLA_SCALE).astype(BF16)
            dk_ref[rows, :] += _dot_tn(ds, qb)
            dv_ref[rows, :] += _dot_tn(p.astype(BF16), dob)
            return _dot(ds, kbl)

        dq = block(kc_ref, vc_ref, dkc_ref, dvc_ref, pl.ds(0, ctx_len))
        for i in range(seq // kb):
            dq = dq + block(kl_ref, vl_ref, dkl_ref, dvl_ref, pl.ds(i * kb, kb))
        dq_ref[...] = dq

    qs, kl, kc, vl, vc, os_ = _mla_specs(t_lat, seq, ctx_len, tq)
    do_spec = pl.BlockSpec((tq, 128), lambda b, h, j: (b * nqt + j, HEADS + h))
    return _hosted_call(
        body, (q, k, k, v, v, ymla, lse, dmix), name="mla_bwd", grid=(nex, HEADS, nqt),
        in_specs=[qs, kl, kc, vl, vc, os_, os_, do_spec],
        out_specs=[
            qs,
            pl.BlockSpec((seq, MLA_HEAD), lambda b, h, j: (b, h)),
            pl.BlockSpec((ctx_len, MLA_HEAD), lambda b, h, j: (b, h)),
            pl.BlockSpec((seq, 128), lambda b, h, j: (b, h)),
            pl.BlockSpec((ctx_len, 128), lambda b, h, j: (b, h)),
        ],
        out_shape=[
            jax.ShapeDtypeStruct((t_lat, HEADS * MLA_HEAD), F32),
            jax.ShapeDtypeStruct((t_lat, HEADS * MLA_HEAD), F32),
            jax.ShapeDtypeStruct((t_ctx, HEADS * MLA_HEAD), F32),
            jax.ShapeDtypeStruct((t_lat, HEADS * 128), F32),
            jax.ShapeDtypeStruct((t_ctx, HEADS * 128), F32),
        ],
        sem=("parallel", "parallel", "arbitrary"), rider=rider)


def _decay_terms(lg, chunk, forward):
    ii = lax.broadcasted_iota(jnp.int32, (chunk, chunk), 0)
    jj = lax.broadcasted_iota(jnp.int32, (chunk, chunk), 1)
    diff = (ii - jj) if forward else (jj - ii)
    dist = jnp.maximum(diff, 0).astype(F32)
    dmat = jnp.where(diff >= 0, jnp.exp(lg * dist), 0.0)
    pos = lax.broadcasted_iota(jnp.int32, (chunk, 1), 0).astype(F32)
    if forward:
        e_q = pos + 1.0
        e_k = (chunk - 1.0) - pos
    else:
        e_q = chunk - pos
        e_k = pos
    wq = jnp.exp(lg * e_q)
    wk = jnp.exp(lg * e_k)
    cd = jnp.exp(jnp.full((1, 1), lg * chunk, F32))
    return dmat, dist, wq, wk, e_q, e_k, cd


def _ctx_weights(lg, ctx_len, forward):
    pos = lax.broadcasted_iota(jnp.int32, (ctx_len, 1), 0).astype(F32)
    e = ((ctx_len - 1.0) - pos) if forward else pos
    return jnp.exp(lg * e), e


def _ret_specs(t_lat, seq, ctx_len):
    cb = t_lat // ctx_len
    qk = pl.BlockSpec((seq, 128), lambda b, h: (b, h // 2))
    v = pl.BlockSpec((seq, 128), lambda b, h: (b, h))
    kc = pl.BlockSpec((ctx_len, 128), lambda b, h: (cb + b, h // 2))
    vc = pl.BlockSpec((ctx_len, 128), lambda b, h: (cb + b, h))
    return qk, v, kc, vc


def _head_mask(h):
    lane = lax.broadcasted_iota(jnp.int32, (1, 128), 1)
    return (lane // RET_DK) == (h % 2)


def _ret_fwd(rq, rk, rv, pg, lg, g_ret, *, t_lat, seq, ctx_len, chunk, rider=None):
    nex = t_lat // seq
    n_chunk = seq // chunk

    def body(q_ref, k_ref, v_ref, kc_ref, vc_ref, rg_ref, lg_ref, g_ref, y_ref, o_ref):
        h = pl.program_id(1)
        hm = _head_mask(h)
        gain = g_ref[...]
        kcm = jnp.where(hm, kc_ref[...].astype(F32), 0.0)
        vcb = vc_ref[...]

        def run(forward):
            lgd = lg_ref[0 if forward else 1, h]
            dmat, _, wq, wk, _, _, cd = _decay_terms(lgd, chunk, forward)
            wc, _ = _ctx_weights(lgd, ctx_len, forward)
            s0 = _dot_tn((kcm * wc).astype(BF16), vcb)

            def step(t, s):
                n = t if forward else n_chunk - 1 - t
                sl = pl.ds(pl.multiple_of(n * chunk, chunk), chunk)
                qm = jnp.where(hm, q_ref[sl, :], jnp.zeros((), BF16))
                kf = jnp.where(hm, k_ref[sl, :].astype(F32), 0.0)
                vb = v_ref[sl, :]
                a = _dot_nt(qm, kf.astype(BF16)) * dmat
                o = _dot(a.astype(BF16), vb) + wq * _dot(qm, s.astype(BF16))
                if forward:
                    o_ref[sl, :] = o
                else:
                    o = o_ref[sl, :] + o
                    o_ref[sl, :] = o
                    mu = jnp.mean(o, axis=-1, keepdims=True)
                    oc = o - mu
                    var = jnp.mean(oc * oc, axis=-1, keepdims=True)
                    on = oc * lax.rsqrt(var + EPS) * gain
                    rg = rg_ref[sl, :]
                    y_ref[sl, :] = (on * (rg / (1.0 + jnp.exp(-rg)))).astype(BF16)
                return cd * s + _dot_tn((kf * wk).astype(BF16), vb)

            lax.fori_loop(0, n_chunk, step, s0)

        run(True)
        run(False)

    qk, v, kc, vc = _ret_specs(t_lat, seq, ctx_len)
    return _hosted_call(
        body, (rq, rk, rv, rk, rv, pg, lg, g_ret), name="ret_fwd", grid=(nex, HEADS),
        in_specs=[qk, qk, v, kc, vc, v, pl.BlockSpec(memory_space=pltpu.SMEM), pl.BlockSpec((1, 128), lambda b, h: (0, h))],
        out_specs=[v, v],
        out_shape=[jax.ShapeDtypeStruct((t_lat, HEADS * RET_DV), BF16), jax.ShapeDtypeStruct((t_lat, HEADS * RET_DV), F32)],
        sem=("parallel", "arbitrary"), rider=rider)


def _ret_bwd(rq, rk, rv, pg, osum, dmix, lg, g_ret, *, t_lat, seq, ctx_len, chunk, rider=None):
    nex = t_lat // seq
    n_chunk = seq // chunk
    t_ctx = nex * ctx_len

    def body(q_ref, k_ref, v_ref, kc_ref, vc_ref, rg_ref, o_ref, dy_ref, lg_ref, g_ref,
             dq_ref, dk_ref, dv_ref, dkc_ref, dvc_ref, drg_ref, st_ref, do_s, s_st):
        h = pl.program_id(1)
        hm = _head_mask(h)
        gain = g_ref[...]
        kcm = jnp.where(hm, kc_ref[...].astype(F32), 0.0)
        vcb = vc_ref[...]

        def norm_step(n, dgain):
            sl = pl.ds(pl.multiple_of(n * chunk, chunk), chunk)
            o = o_ref[sl, :]
            mu = jnp.mean(o, axis=-1, keepdims=True)
            oc = o - mu
            rstd = lax.rsqrt(jnp.mean(oc * oc, axis=-1, keepdims=True) + EPS)
            ohat = oc * rstd
            rg = rg_ref[sl, :]
            sg = 1.0 / (1.0 + jnp.exp(-rg))
            dy = dy_ref[sl, :].astype(F32)
            don = dy * (rg * sg)
            drg_ref[sl, :] = dy * (ohat * gain) * (sg * (1.0 + rg * (1.0 - sg)))
            dohat = don * gain
            do_s[sl, :] = rstd * (dohat - jnp.mean(dohat, axis=-1, keepdims=True)
                                  - ohat * jnp.mean(dohat * ohat, axis=-1, keepdims=True))
            return dgain + jnp.sum(don * ohat, axis=0, keepdims=True)

        dgain = lax.fori_loop(0, n_chunk, norm_step, jnp.zeros((1, 128), F32))

        @pl.when(h % 2 == 0)
        def _():
            dq_ref[...] = jnp.zeros(dq_ref.shape, F32)
            dk_ref[...] = jnp.zeros(dk_ref.shape, F32)
            dkc_ref[...] = jnp.zeros(dkc_ref.shape, F32)

        dv_ref[...] = jnp.zeros(dv_ref.shape, F32)

        def run(forward):
            lgd = lg_ref[0 if forward else 1, h]
            dmat, dist, wq, wk, e_q, e_k, cd = _decay_terms(lgd, chunk, forward)
            wc, e_c = _ctx_weights(lgd, ctx_len, forward)
            s0 = _dot_tn((kcm * wc).astype(BF16), vcb)

            def state_step(t, s):
                n = t if forward else n_chunk - 1 - t
                sl = pl.ds(pl.multiple_of(n * chunk, chunk), chunk)
                s_st[n] = s
                kf = jnp.where(hm, k_ref[sl, :].astype(F32), 0.0)
                return cd * s + _dot_tn((kf * wk).astype(BF16), v_ref[sl, :])

            lax.fori_loop(0, n_chunk, state_step, s0)

            def grad_step(t, carry):
                g_next, dlg = carry
                n = (n_chunk - 1 - t) if forward else t
                sl = pl.ds(pl.multiple_of(n * chunk, chunk), chunk)
                qm = jnp.where(hm, q_ref[sl, :], jnp.zeros((), BF16))
                kf = jnp.where(hm, k_ref[sl, :].astype(F32), 0.0)
                kb = kf.astype(BF16)
                vb = v_ref[sl, :]
                do = do_s[sl, :]
                dob = do.astype(BF16)
                s_n = s_st[n]
                s_nb = s_n.astype(BF16)
                gb = g_next.astype(BF16)
                dk_cross = wk * _dot_nt(vb, gb)
                dv_cross = _dot((kf * wk).astype(BF16), gb)
                a = _dot_nt(qm, kb) * dmat
                da_raw = _dot_nt(dob, vb)
                dab = (da_raw * dmat).astype(BF16)
                ab = a.astype(BF16)
                o_cross = wq * _dot(qm, s_nb)
                dq_ref[sl, :] += _dot(dab, kb) + wq * _dot_nt(dob, s_nb)
                dk_ref[sl, :] += _dot_tn(dab, qm) + dk_cross
                dv_ref[sl, :] += _dot_tn(ab, dob) + dv_cross
                dlg = (dlg + chunk * cd * jnp.sum(g_next * s_n, keepdims=True)
                       + jnp.sum(e_k * jnp.sum(kf * dk_cross, axis=-1, keepdims=True), keepdims=True)
                       + jnp.sum(dist * a * da_raw, keepdims=True)
                       + jnp.sum(e_q * jnp.sum(o_cross * do, axis=-1, keepdims=True), keepdims=True))
                g_new = cd * g_next + _dot_tn((qm.astype(F32) * wq).astype(BF16), dob)
                return g_new, dlg

            ds0, dlg = lax.fori_loop(0, n_chunk, grad_step, (jnp.zeros((128, 128), F32), jnp.zeros((1, 1), F32)))
            ds0b = ds0.astype(BF16)
            dkc_part = wc * _dot_nt(vcb, ds0b)
            dkc_ref[...] += dkc_part
            dvc_part = _dot((kcm * wc).astype(BF16), ds0b)
            dlg = dlg + jnp.sum(e_c * jnp.sum(kcm * dkc_part, axis=-1, keepdims=True), keepdims=True)
            return dvc_part, dlg

        dvc_f, dlg_f = run(True)
        dvc_b, dlg_b = run(False)
        dvc_ref[...] = dvc_f + dvc_b
        st_ref[0] = jnp.concatenate([
            dgain, jnp.broadcast_to(dlg_f, (1, 128)), jnp.broadcast_to(dlg_b, (1, 128)), jnp.zeros((5, 128), F32)], axis=0)

    qk, v, kc, vc = _ret_specs(t_lat, seq, ctx_len)
    dy_spec = v
    return _hosted_call(
        body, (rq, rk, rv, rk, rv, pg, osum, dmix, lg, g_ret), name="ret_bwd", grid=(nex, HEADS),
        in_specs=[qk, qk, v, kc, vc, v, v, dy_spec, pl.BlockSpec(memory_space=pltpu.SMEM),
                  pl.BlockSpec((1, 128), lambda b, h: (0, h))],
        out_specs=[
            qk, qk, v,
            pl.BlockSpec((ctx_len, 128), lambda b, h: (b, h // 2)),
            pl.BlockSpec((ctx_len, 128), lambda b, h: (b, h)),
            v,
            pl.BlockSpec((1, 8, 128), lambda b, h: (b, 0, h)),
        ],
        out_shape=[
            jax.ShapeDtypeStruct((t_lat, 256), F32), jax.ShapeDtypeStruct((t_lat, 256), F32),
            jax.ShapeDtypeStruct((t_lat, 512), F32), jax.ShapeDtypeStruct((t_ctx, 256), F32),
            jax.ShapeDtypeStruct((t_ctx, 512), F32), jax.ShapeDtypeStruct((t_lat, 512), F32),
            jax.ShapeDtypeStruct((nex, 8, 512), F32),
        ],
        scratch_shapes=[pltpu.VMEM((seq, 128), F32), pltpu.VMEM((n_chunk, 128, 128), F32)],
        sem=("parallel", "arbitrary"), rider=rider)


def _matmul_tn(a, b, *, bm, bn, bk, chip_major, name):
    tk, m = a.shape
    n = b.shape[1]
    slab = n // N_CHIPS
    per_block = bn // slab if chip_major else 1
    bk = max(c for c in range(LANES, min(bk, tk) + 1, LANES) if tk % c == 0)

    def body(a_ref, b_ref, o_ref):
        k = pl.program_id(2)
        if chip_major:
            parts = [_dot_tn(a_ref[...], b_ref[:, s * slab:(s + 1) * slab]) for s in range(per_block)]
        else:
            parts = [_dot_tn(a_ref[...], b_ref[...])]

        @pl.when(k == 0)
        def _():
            for s, part in enumerate(parts):
                if chip_major:
                    o_ref[s] = part
                else:
                    o_ref[...] = part

        @pl.when(k > 0)
        def _():
            for s, part in enumerate(parts):
                if chip_major:
                    o_ref[s] += part
                else:
                    o_ref[...] += part

    if chip_major:
        out_spec = pl.BlockSpec((per_block, bm, slab), lambda i, j, k: (j, i, 0))
        out_shape = jax.ShapeDtypeStruct((N_CHIPS, m, slab), F32)
    else:
        out_spec = pl.BlockSpec((bm, bn), lambda i, j, k: (i, j))
        out_shape = jax.ShapeDtypeStruct((m, n), F32)
    return pl.pallas_call(
        body, name=name, grid=(m // bm, n // bn, tk // bk),
        in_specs=[pl.BlockSpec((bk, bm), lambda i, j, k: (k, i)), pl.BlockSpec((bk, bn), lambda i, j, k: (k, j))],
        out_specs=out_spec, out_shape=out_shape,
        compiler_params=_params(("parallel", "parallel", "arbitrary"), VMEM_LIMIT),
    )(a, b)


_LATE = ("w_out", "w_ff1", "w_ff2")
_EARLY = ("w_in", "w_uq", "w_ukv")


def _local_step(x, ctx, tgt, modv, lg, g_attn, g_ffn, g_fin, g_ret, g_q, g_kv, w_in, w_uq, w_ukv, late, place=None,
                *, tm=256, tq=256, chunk=256):
    nex, seq, d = x.shape
    ctx_len = ctx.shape[1]
    t_lat = nex * seq
    x2 = x.reshape(t_lat, d)
    ctx2 = ctx.reshape(nex * ctx_len, d)
    tgt2 = tgt.reshape(t_lat, d)
    cos_t, sin_t = _rope_tables(seq, tm)
    dims = dict(t_lat=t_lat, seq=seq, ctx_len=ctx_len)
    alone = place is None

    hb, pg, rq, rk, rv, nq, nkv, q, k, v = _pre_fwd(x2, ctx2, modv, g_attn, w_in, g_q, g_kv, w_uq, w_ukv, cos_t, sin_t,
                                                    seq=seq, tm=tm)
    (yret, osum), gathered = _ret_fwd(rq, rk, rv, pg, lg, g_ret, chunk=chunk, **dims,
                                      rider=None if alone else _gather_rider(list(late[:2])))
    (ymla, lse), gathered2 = _mla_fwd(q, k, v, tq=tq, **dims, rider=None if alone else _gather_rider(list(late[2:])))
    w_out, w_ff1, w_ff2 = late if alone else gathered + gathered2
    mix, act, du, h2, df, dmo, dmix, dxm, st_post = _post(yret, ymla, x2, tgt2, modv, g_ffn, g_fin, w_out.reshape(d, d),
                                                         w_ff1, w_ff2.reshape(D_FF, d), seq=seq, tm=tm)
    bk = 512
    g_late = [
        _matmul_tn(mix, dmo, bm=1024, bn=1024, bk=1024, chip_major=False, name="gw_out").reshape(N_CHIPS, d // N_CHIPS, d),
        _matmul_tn(h2, du, bm=1024, bn=1024, bk=1024, chip_major=True, name="gw_ff1"),
        _matmul_tn(act, df, bm=1024, bn=1024, bk=1024, chip_major=False, name="gw_ff2").reshape(N_CHIPS, D_FF // N_CHIPS, d),
    ]
    (dq_m, dkl, dkc, dvl, dvc), got = _mla_bwd(q, k, v, ymla, lse, dmix, tq=tq, **dims,
                                               rider=None if alone else _exchange_rider(g_late))
    if not alone:
        core, slot = place
        part = [_add_half(g, r, core, "add_half_" + n) for g, r, n in zip(g_late, got, _LATE)]
    (drq, drk, drv, dkc_r, dvc_r, drg, st_ret), landed = _ret_bwd(
        rq, rk, rv, pg, osum, dmix, lg, g_ret, chunk=chunk, **dims, rider=None if alone else _scatter_rider(part))
    if not alone:
        mine = [_sum_chips(p, l, slot, "sum_chips_" + n) for p, l, n in zip(part, landed, _LATE)]
    (dpb, dqf, dkvf, gx, st_pre), _ = _pre_bwd(
        x2, ctx2, modv, g_attn, pg, drq, drk, dkc_r, drv, dvc_r, drg, dq_m, dkl, dkc, dvl, dvc, dxm, w_in, g_q, g_kv,
        w_uq, w_ukv, cos_t, sin_t, seq=seq, tm=tm)
    g_early = [
        _matmul_tn(dpb, hb, bm=IN_PAD // 2, bn=d, bk=512, chip_major=False, name="gw_in"),
        _matmul_tn(dqf, nq, bm=HEADS * MLA_HEAD, bn=Q_LORA, bk=1536, chip_major=False, name="gw_uq"),
        _matmul_tn(nkv, dkvf, bm=KV_LORA, bn=HEADS * 256, bk=1536, chip_major=True, name="gw_ukv"),
    ]
    late_out = g_late if alone else mine
    return gx.reshape(nex, seq, d), g_early, late_out, st_post, st_ret, st_pre


_ANY = pl.BlockSpec(memory_space=pl.ANY)
_VMEM = pl.BlockSpec(memory_space=pltpu.VMEM)
_OFFSETS = tuple((dx, dy, dc) for dx in (0, 1) for dy in (0, 1) for dc in (0, 1))[1:]
_CHIP_OFFSETS = ((1, 0), (0, 1), (1, 1))


def _place():
    return lax.axis_index("x"), lax.axis_index("y"), lax.axis_index("c")


def _flip(v, d):
    return 1 - v if d else v


def _gather8_rider(a):
    def copies(a_ref, o_ref, send, recv):
        x, y, z = _place()
        me = 4 * x + 2 * y + z
        out = []
        for k, (dx, dy, dc) in enumerate(_OFFSETS):
            peer = (_flip(x, dx), _flip(y, dy), _flip(z, dc))
            landing = o_ref.at[4 * peer[0] + 2 * peer[1] + peer[2]]
            out.append((
                pltpu.make_async_remote_copy(src_ref=a_ref, dst_ref=o_ref.at[me], send_sem=send.at[k],
                                             recv_sem=recv.at[k], device_id=peer, device_id_type=MESH),
                pltpu.make_async_remote_copy(src_ref=a_ref, dst_ref=landing, send_sem=send.at[k],
                                             recv_sem=recv.at[k], device_id=peer, device_id_type=MESH)))
        return me, out

    def start(ins, outs, sems):
        me, cps = copies(ins[0], outs[0], sems[0], sems[1])
        pltpu.make_async_copy(ins[0], outs[0].at[me], sems[2]).start()
        for out_cp, _ in cps:
            out_cp.start()

    def finish(ins, outs, sems):
        me, cps = copies(ins[0], outs[0], sems[0], sems[1])
        for out_cp, in_cp in cps:
            in_cp.wait_recv()
            out_cp.wait_send()
        pltpu.make_async_copy(ins[0], outs[0].at[me], sems[2]).wait()

    return _Rider([a], [jax.ShapeDtypeStruct((N_DEV,) + a.shape, a.dtype)],
                  [pltpu.SemaphoreType.DMA((7,)), pltpu.SemaphoreType.DMA((7,)), pltpu.SemaphoreType.DMA],
                  start, finish, in_specs=[_VMEM], out_specs=[_VMEM])


def _merge_riders(*riders):
    ins, outs, sems, in_specs, out_specs, aliases, cuts = [], [], [], [], [], {}, []
    for r in riders:
        cuts.append((len(ins), len(outs), len(sems)))
        aliases.update({len(ins) + i: len(outs) + j for i, j in r.aliases.items()})
        ins += r.ins
        outs += r.out_shapes
        sems += r.sems
        in_specs += r.in_specs
        out_specs += r.out_specs

    def part(r, cut, r_ins, r_outs, r_sems):
        return (r_ins[cut[0]:cut[0] + len(r.ins)], r_outs[cut[1]:cut[1] + len(r.out_shapes)],
                r_sems[cut[2]:cut[2] + len(r.sems)])

    def start(r_ins, r_outs, r_sems):
        for r, cut in zip(riders, cuts):
            r.start(*part(r, cut, r_ins, r_outs, r_sems))

    def finish(r_ins, r_outs, r_sems):
        for r, cut in zip(riders, cuts):
            r.finish(*part(r, cut, r_ins, r_outs, r_sems))

    return _Rider(ins, outs, sems, start, finish, aliases=aliases, in_specs=in_specs, out_specs=out_specs)


def _allgather8(a, name):
    return _run_rider(_gather8_rider(a), name)[0]


BF16_TILE_ROWS = 16


def _half(o, slot, which):
    r2 = o.shape[1] // 2
    if r2 % BF16_TILE_ROWS == 0:
        return o.at[slot, pl.ds(which * r2, r2)]
    c2 = o.shape[2] // 2
    assert c2 % LANES == 0
    return o.at[slot, :, pl.ds(which * c2, c2)]


def _gather_send(o_refs, send, recv):
    x, y, z = _place()
    chip = 2 * x + y
    for a, o in enumerate(o_refs):
        r2 = o.shape[1] // 2
        mine = _half(o, chip, z)
        for k, (dx, dy) in enumerate(_CHIP_OFFSETS):
            pltpu.make_async_remote_copy(
                src_ref=mine, dst_ref=mine, send_sem=send.at[a, k], recv_sem=recv.at[a, k],
                device_id=(_flip(x, dx), _flip(y, dy), z), device_id_type=MESH).start()


def _gather_finish(o_refs, send, recv, fsend, frecv):
    x, y, z = _place()
    chip = 2 * x + y
    sib = (x, y, 1 - z)
    passed = []
    for a, o in enumerate(o_refs):
        r2 = o.shape[1] // 2
        for k, (dx, dy) in enumerate(_CHIP_OFFSETS):
            other = 2 * _flip(x, dx) + _flip(y, dy)
            landed = _half(o, other, z)
            pltpu.make_async_remote_copy(
                src_ref=landed, dst_ref=landed, send_sem=send.at[a, k], recv_sem=recv.at[a, k],
                device_id=(_flip(x, dx), _flip(y, dy), z), device_id_type=MESH).wait_recv()
            cp = pltpu.make_async_remote_copy(
                src_ref=landed, dst_ref=landed, send_sem=fsend.at[a, k], recv_sem=frecv.at[a, k],
                device_id=sib, device_id_type=MESH)
            cp.start()
            passed.append(cp)
    for a, o in enumerate(o_refs):
        r2 = o.shape[1] // 2
        mine = _half(o, chip, z)
        for k, (dx, dy) in enumerate(_CHIP_OFFSETS):
            other = 2 * _flip(x, dx) + _flip(y, dy)
            got = _half(o, other, 1 - z)
            pltpu.make_async_remote_copy(
                src_ref=got, dst_ref=got, send_sem=fsend.at[a, k], recv_sem=frecv.at[a, k],
                device_id=sib, device_id_type=MESH).wait_recv()
            pltpu.make_async_remote_copy(
                src_ref=mine, dst_ref=mine, send_sem=send.at[a, k], recv_sem=recv.at[a, k],
                device_id=(_flip(x, dx), _flip(y, dy), z), device_id_type=MESH).wait_send()
    for cp in passed:
        cp.wait_send()


class _Rider:
    def __init__(self, ins, out_shapes, sems, start, finish, aliases=None, in_specs=None, out_specs=None):
        self.ins, self.out_shapes, self.sems = list(ins), list(out_shapes), list(sems)
        self.start, self.finish, self.aliases = start, finish, dict(aliases or {})
        self.in_specs = list(in_specs) if in_specs else [_ANY] * len(self.ins)
        self.out_specs = list(out_specs) if out_specs else [_ANY] * len(self.out_shapes)


def _run_rider(rider, name):
    r_in, r_out = len(rider.ins), len(rider.out_shapes)

    def body(*refs):
        ins, outs, sems = refs[:r_in], refs[r_in:r_in + r_out], refs[r_in + r_out:]
        rider.start(ins, outs, sems)
        rider.finish(ins, outs, sems)

    return pl.pallas_call(
        body, name=name, in_specs=rider.in_specs, out_specs=rider.out_specs, out_shape=rider.out_shapes,
        input_output_aliases=rider.aliases, scratch_shapes=rider.sems,
    )(*rider.ins)


def _hosted_call(body, args, *, name, grid, in_specs, out_specs, out_shape, scratch_shapes=(), sem, rider=None):
    scratch_shapes = list(scratch_shapes)
    if rider is None:
        res = pl.pallas_call(
            body, name=name, grid=grid, in_specs=in_specs, out_specs=out_specs, out_shape=out_shape,
            scratch_shapes=scratch_shapes, compiler_params=_params(sem, VMEM_LIMIT))(*args)
        return list(res), []
    n_in, n_out, n_sc = len(in_specs), len(out_specs), len(scratch_shapes)
    r_in, r_out = len(rider.ins), len(rider.out_shapes)
    last = tuple(g - 1 for g in grid)

    def hosted(*refs):
        p = 0
        parts = []
        for cnt in (n_in, r_in, n_out, r_out, n_sc):
            parts.append(refs[p:p + cnt])
            p += cnt
        ins, r_ins, outs, r_outs, scratch = parts
        sems = refs[p:]
        ids = [pl.program_id(a) for a in range(len(grid))]
        is_first = functools.reduce(jnp.logical_and, [i == 0 for i in ids])
        is_last = functools.reduce(jnp.logical_and, [i == e for i, e in zip(ids, last)])

        @pl.when(is_first)
        def _():
            rider.start(r_ins, r_outs, sems)

        body(*ins, *outs, *scratch)

        @pl.when(is_last)
        def _():
            rider.finish(r_ins, r_outs, sems)

    res = pl.pallas_call(
        hosted, name=name, grid=grid, in_specs=list(in_specs) + rider.in_specs, out_specs=list(out_specs) + rider.out_specs,
        out_shape=list(out_shape) + rider.out_shapes, scratch_shapes=scratch_shapes + rider.sems,
        input_output_aliases={n_in + i: n_out + j for i, j in rider.aliases.items()},
        compiler_params=_params(("arbitrary",) * len(grid), VMEM_LIMIT))(*args, *rider.ins)
    return list(res[:n_out]), list(res[n_out:])


def _gather_rider(ws):
    n = len(ws)
    return _Rider(
        ws, [jax.ShapeDtypeStruct(w.shape, w.dtype) for w in ws], [pltpu.SemaphoreType.DMA((n, 3))] * 4,
        lambda ins, outs, sems: _gather_send(outs, sems[0], sems[1]),
        lambda ins, outs, sems: _gather_finish(outs, *sems),
        aliases={a: a for a in range(n)})


def _copies_rider(ins, out_shapes, sem_shape, make):
    def start(r_ins, r_outs, sems):
        for cp in make(r_ins, r_outs, sems[0], sems[1]):
            cp.start()

    def finish(r_ins, r_outs, sems):
        for cp in make(r_ins, r_outs, sems[0], sems[1]):
            cp.wait()

    return _Rider(ins, out_shapes, [pltpu.SemaphoreType.DMA(sem_shape)] * 2, start, finish)


def _exchange_rider(gs):
    def make(g_refs, r_refs, send, recv):
        x, y, z = _place()
        return [pltpu.make_async_remote_copy(
            src_ref=g.at[:, pl.ds((1 - z) * (g.shape[1] // 2), g.shape[1] // 2)], dst_ref=r, send_sem=send.at[a],
            recv_sem=recv.at[a], device_id=(x, y, 1 - z), device_id_type=MESH)
            for a, (g, r) in enumerate(zip(g_refs, r_refs))]

    shapes = [jax.ShapeDtypeStruct((g.shape[0], g.shape[1] // 2, g.shape[2]), g.dtype) for g in gs]
    return _copies_rider(gs, shapes, (len(gs),), make)


def _add_half(g, recv, core, name):
    s, r, c = g.shape
    r2 = r // 2
    rb = r2
    for cand in (256, 128, 64):
        if r2 % cand == 0:
            rb = cand
            break
    g4 = g.reshape(s, 2, r2, c)

    def body(core_ref, g_ref, r_ref, o_ref):
        o_ref[...] = (g_ref[...] + r_ref[...]).astype(BF16)

    return pl.pallas_call(
        body, name=name,
        grid_spec=pltpu.PrefetchScalarGridSpec(
            num_scalar_prefetch=1, grid=(s, r2 // rb),
            in_specs=[pl.BlockSpec((None, None, rb, c), lambda i, j, cr: (i, cr[0], j, 0)),
                      pl.BlockSpec((None, rb, c), lambda i, j, cr: (i, j, 0))],
            out_specs=pl.BlockSpec((None, rb, c), lambda i, j, cr: (i, j, 0))),
        out_shape=jax.ShapeDtypeStruct((s, r2, c), BF16),
        compiler_params=_params(("parallel", "parallel")),
    )(core, g4, recv)


def _scatter_rider(ps):
    def make(p_refs, o_refs, send, recv):
        x, y, z = _place()
        copies = []
        for a, (p, o) in enumerate(zip(p_refs, o_refs)):
            for k, (dx, dy) in enumerate(_CHIP_OFFSETS):
                other = 2 * _flip(x, dx) + _flip(y, dy)
                copies.append(pltpu.make_async_remote_copy(
                    src_ref=p.at[other], dst_ref=o.at[k], send_sem=send.at[a, k], recv_sem=recv.at[a, k],
                    device_id=(_flip(x, dx), _flip(y, dy), z), device_id_type=MESH))
        return copies

    shapes = [jax.ShapeDtypeStruct((3,) + p.shape[1:], p.dtype) for p in ps]
    return _copies_rider(ps, shapes, (len(ps), 3), make)


def _sum_chips(p, landed, chip, name):
    _, r2, c = p.shape
    rb = r2
    for cand in (256, 128, 64):
        if r2 % cand == 0:
            rb = cand
            break

    def body(s_ref, p_ref, l_ref, o_ref):
        acc = p_ref[...].astype(F32)
        for k in range(3):
            acc = acc + l_ref[k].astype(F32)
        o_ref[...] = acc

    return pl.pallas_call(
        body, name=name,
        grid_spec=pltpu.PrefetchScalarGridSpec(
            num_scalar_prefetch=1, grid=(r2 // rb,),
            in_specs=[pl.BlockSpec((None, rb, c), lambda i, s: (s[0], i, 0)),
                      pl.BlockSpec((3, rb, c), lambda i, s: (0, i, 0))],
            out_specs=pl.BlockSpec((rb, c), lambda i, s: (i, 0))),
        out_shape=jax.ShapeDtypeStruct((r2, c), F32),
        compiler_params=_params(("parallel",)),
    )(chip, p, landed)


def _swap_rider(hs):
    def make(h_refs, o_refs, send, recv):
        x, y, z = _place()
        return [pltpu.make_async_remote_copy(
            src_ref=h, dst_ref=o, send_sem=send.at[a], recv_sem=recv.at[a], device_id=(x, y, 1 - z),
            device_id_type=MESH) for a, (h, o) in enumerate(zip(h_refs, o_refs))]

    return _copies_rider(hs, [jax.ShapeDtypeStruct(h.shape, h.dtype) for h in hs], (len(hs),), make)


SMALL_ROWS = 32
PACK_ROWS = 16


def _pack_small(st_post, st_ret, st_pre):
    d = st_post.shape[2]

    def body(po_ref, re_ref, pr_ref, o_ref):
        o_ref[...] = jnp.zeros(o_ref.shape, F32)
        o_ref[0:1, :] = pr_ref[0, 2:3, :] + pr_ref[1, 2:3, :] + pr_ref[2, 2:3, :]
        o_ref[1:2, :] = po_ref[0, 4:5, :] + po_ref[1, 4:5, :]
        o_ref[2:3, :] = po_ref[0, 5:6, :] + po_ref[1, 5:6, :]
        o_ref[3:4, 0:512] = re_ref[0, 0:1, :] + re_ref[1, 0:1, :]
        o_ref[4:5, :] = pr_ref[0, 3:4, :] + pr_ref[1, 3:4, :] + pr_ref[2, 3:4, :]
        o_ref[5:6, :] = pr_ref[0, 4:5, :] + pr_ref[1, 4:5, :] + pr_ref[2, 4:5, :]
        lane = lax.broadcasted_iota(jnp.int32, (1, LANES), 1)
        for row, src in ((6, 1), (10, 2)):
            acc = jnp.zeros((1, LANES), F32)
            for hd in range(HEADS):
                grp = re_ref[0, src:src + 1, hd * LANES:(hd + 1) * LANES] + re_ref[1, src:src + 1, hd * LANES:(hd + 1) * LANES]
                acc = acc + jnp.where(lane == hd, grp, 0.0)
            o_ref[row:row + 1, 0:LANES] = acc
        o_ref[7:8, :] = po_ref[0, 6:7, :] + po_ref[1, 6:7, :]
        o_ref[8:9, :] = pr_ref[2, 0:1, :]
        o_ref[9:10, :] = pr_ref[2, 1:2, :]
        for e in range(2):
            b = 12 + 6 * e
            o_ref[b:b + 1, :] = pr_ref[e, 0:1, :]
            o_ref[b + 1:b + 2, :] = pr_ref[e, 1:2, :]
            o_ref[b + 2:b + 3, :] = po_ref[e, 3:4, :]
            o_ref[b + 3:b + 4, :] = po_ref[e, 0:1, :]
            o_ref[b + 4:b + 5, :] = po_ref[e, 1:2, :]
            o_ref[b + 5:b + 6, :] = po_ref[e, 2:3, :]

    return pl.pallas_call(body, name="pack_small", out_shape=jax.ShapeDtypeStruct((SMALL_ROWS, d), F32))(st_post, st_ret, st_pre)


def _small_reduce(gathered):
    d = gathered.shape[2]

    def body(g_ref, o_ref):
        tot = g_ref[0, 0:PACK_ROWS, :]
        for dev in range(1, N_DEV):
            tot = tot + g_ref[dev, 0:PACK_ROWS, :]
        o_ref[0:PACK_ROWS, :] = tot
        for j in range(6):
            acc = g_ref[0, 12 + j:13 + j, :] + g_ref[0, 18 + j:19 + j, :]
            for dev in range(1, N_DEV):
                acc = acc + g_ref[dev, 12 + j:13 + j, :] + g_ref[dev, 18 + j:19 + j, :]
            if j < 2:
                acc = acc + o_ref[8 + j:9 + j, :]
            o_ref[PACK_ROWS + j:PACK_ROWS + j + 1, :] = acc
        o_ref[PACK_ROWS + 6:PACK_ROWS + 8, :] = jnp.zeros((2, d), F32)

    return pl.pallas_call(body, name="small_reduce", out_shape=jax.ShapeDtypeStruct((PACK_ROWS + 8, d), F32))(gathered)


_SMALL = (("g_attn", 0, 1024), ("g_ffn", 1, 1024), ("g_final", 2, 1024), ("g_ret", 3, 512), ("g_q_lora", 4, 384),
          ("g_kv_lora", 5, 256), ("ret_decay_fwd", 6, HEADS), ("ret_decay_bwd", 10, HEADS))
_SMALL_NAMES = tuple(s[0] for s in _SMALL) + ("c_ctx", "b_ada")


def _small_final(tot, dcc, sg8, ws, ms, vs):
    d = tot.shape[1]
    n = len(_SMALL_NAMES)

    def body(*refs):
        t_ref, dcc_ref, sg_ref = refs[0:3]
        w_refs, m_refs, v_refs = refs[3:3 + n], refs[3 + n:3 + 2 * n], refs[3 + 2 * n:3 + 3 * n]
        outs = refs[3 + 3 * n:]
        g_refs, d_refs, mo_refs, vo_refs = outs[0:n], outs[n:2 * n], outs[2 * n:3 * n], outs[3 * n:4 * n]
        l_ref = outs[4 * n]

        def update(i, g, sl=None):
            pick = (lambda r: r[...]) if sl is None else (lambda r: r[:, sl])
            dl, mn, vn = _adam_math(pick(w_refs[i]), g, pick(m_refs[i]), pick(v_refs[i]))
            if sl is None:
                g_refs[i][...], d_refs[i][...], mo_refs[i][...], vo_refs[i][...] = g, dl, mn, vn
            else:
                g_refs[i][:, sl], d_refs[i][:, sl], mo_refs[i][:, sl], vo_refs[i][:, sl] = g, dl, mn, vn

        for i, (name, row, width) in enumerate(_SMALL):
            g = t_ref[row:row + 1, 0:width]
            if name == "ret_decay_fwd":
                g = g * sg_ref[0:1, 0:width]
            elif name == "ret_decay_bwd":
                g = g * sg_ref[1:2, 0:width]
            update(i, g)
        i_cc, i_b = n - 2, n - 1
        cc = w_refs[i_cc][...]
        s = 1.0 / (1.0 + jnp.exp(-cc))
        dsilu = dcc_ref[0, 0:1, :] + dcc_ref[2, 0:1, :] + dcc_ref[4, 0:1, :] + dcc_ref[6, 0:1, :]
        update(i_cc, dsilu * (s * (1.0 + cc * (1.0 - s))))
        for j in range(6):
            update(i_b, t_ref[PACK_ROWS + j:PACK_ROWS + j + 1, :], pl.ds(j * d, d))
        l_ref[...] = jnp.broadcast_to((0.5 / d) * jnp.sum(t_ref[7:8, :], keepdims=True), l_ref.shape)

    shapes = [jax.ShapeDtypeStruct(a.shape, F32) for a in ws]
    outs = pl.pallas_call(
        body, name="small_final", out_shape=shapes * 4 + [jax.ShapeDtypeStruct((8, LANES), F32)],
    )(tot, dcc, sg8, *ws, *ms, *vs)
    return outs[0:n], outs[n:2 * n], outs[2 * n:3 * n], outs[3 * n:4 * n], outs[4 * n]


_WEIGHTS = ("c_ctx", "w_ada", "b_ada", "g_attn", "g_ffn", "w_in", "ret_decay_fwd", "ret_decay_bwd", "g_ret", "g_q_lora",
            "w_uq", "g_kv_lora", "w_ukv", "w_out", "w_ff1", "w_ff2", "g_final")
_BIG = ("w_in", "w_uq", "w_ukv", "w_out", "w_ff1", "w_ff2")
_TRANSPOSED = ("w_in", "w_uq")


def kernel(x, c, ctx, c_ctx, w_ada, b_ada, g_attn, g_ffn, w_in, ret_decay_fwd, ret_decay_bwd, g_ret, g_q_lora, w_uq, g_kv_lora, w_ukv, w_out, w_ff1, w_ff2, g_final, loss_target, m_c_ctx, m_w_ada, m_b_ada, m_g_attn, m_g_ffn, m_w_in, m_ret_decay_fwd, m_ret_decay_bwd, m_g_ret, m_g_q_lora, m_w_uq, m_g_kv_lora, m_w_ukv, m_w_out, m_w_ff1, m_w_ff2, m_g_final, v_c_ctx, v_w_ada, v_b_ada, v_g_attn, v_g_ffn, v_w_in, v_ret_decay_fwd, v_ret_decay_bwd, v_g_ret, v_g_q_lora, v_w_uq, v_g_kv_lora, v_w_ukv, v_w_out, v_w_ff1, v_w_ff2, v_g_final):
    w = dict(c_ctx=c_ctx, w_ada=w_ada, b_ada=b_ada, g_attn=g_attn, g_ffn=g_ffn, w_in=w_in, ret_decay_fwd=ret_decay_fwd,
             ret_decay_bwd=ret_decay_bwd, g_ret=g_ret, g_q_lora=g_q_lora, w_uq=w_uq, g_kv_lora=g_kv_lora, w_ukv=w_ukv,
             w_out=w_out, w_ff1=w_ff1, w_ff2=w_ff2, g_final=g_final)
    m = dict(c_ctx=m_c_ctx, w_ada=m_w_ada, b_ada=m_b_ada, g_attn=m_g_attn, g_ffn=m_g_ffn, w_in=m_w_in,
             ret_decay_fwd=m_ret_decay_fwd, ret_decay_bwd=m_ret_decay_bwd, g_ret=m_g_ret, g_q_lora=m_g_q_lora, w_uq=m_w_uq,
             g_kv_lora=m_g_kv_lora, w_ukv=m_w_ukv, w_out=m_w_out, w_ff1=m_w_ff1, w_ff2=m_w_ff2, g_final=m_g_final)
    v = dict(c_ctx=v_c_ctx, w_ada=v_w_ada, b_ada=v_b_ada, g_attn=v_g_attn, g_ffn=v_g_ffn, w_in=v_w_in,
             ret_decay_fwd=v_ret_decay_fwd, ret_decay_bwd=v_ret_decay_bwd, g_ret=v_g_ret, g_q_lora=v_g_q_lora, w_uq=v_w_uq,
             g_kv_lora=v_g_kv_lora, w_ukv=v_w_ukv, w_out=v_w_out, w_ff1=v_w_ff1, w_ff2=v_w_ff2, g_final=v_g_final)
    xi, yi, ci = lax.axis_index("x"), lax.axis_index("y"), lax.axis_index("c")
    chip = 2 * xi + yi
    dev = 2 * chip + ci
    nex, seq, d = x.shape
    n_ada = w_ada.shape[2]

    c_all = _allgather8(jnp.pad(c, ((0, 8 - nex), (0, 0))), "ag_c")[:, :nex].reshape(N_DEV * nex, d)
    a_in = jnp.concatenate([c_all, c_ctx.reshape(1, d), jnp.zeros((7, d), F32)], axis=0)
    b_sh = lax.dynamic_slice(b_ada, (0, chip * n_ada), (1, n_ada))
    mod_sh = _mod_fwd(a_in, w_ada[0], b_sh)

    dec = jnp.zeros((8, LANES), F32).at[0, :HEADS].set(ret_decay_fwd[0]).at[1, :HEADS].set(ret_decay_bwd[0])
    lg8, sg8 = _decay_prep(dec)
    lg = lg8[:2, :HEADS]

    def shard_of(t, k):
        return t[k][0].T if k in _TRANSPOSED else t[k][0]

    shard = {k: shard_of(w, k) for k in _BIG}
    head_rows = MLA_NOPE + MLA_ROPE
    shard["w_uq"] = jnp.pad(shard["w_uq"], ((0, MLA_HEAD - head_rows), (0, 0)))
    slot = chip.reshape(1).astype(jnp.int32)
    core = ci.reshape(1).astype(jnp.int32)
    slots = {k: _cast_into_slot(shard[k], slot, "cast_" + k) for k in _BIG}
    mod8, w_in_f, w_uq_k, w_ukv_k = _run_rider(
        _merge_riders(_gather8_rider(mod_sh), _gather_rider([slots[k] for k in _EARLY])), "ag_early")
    w_in_k = jnp.pad(w_in_f.reshape(IN_COLS, d), ((0, IN_PAD - IN_COLS), (0, 0)))
    mod_all = mod8[0::2].transpose(1, 0, 2).reshape(a_in.shape[0], N_CHIPS * n_ada)
    mod_me = lax.dynamic_slice(mod_all, (nex * dev, 0), (nex, N_CHIPS * n_ada)).reshape(nex, 6, d)
    mod_c = mod_all[N_DEV * nex].reshape(1, 6, d)
    modv = jnp.pad(jnp.concatenate([mod_me, mod_c], axis=0), ((0, 0), (0, 2), (0, 0)))

    gx, g_early, late, st_post, st_ret, st_pre = _local_step(
        x, ctx, loss_target, modv, lg, g_attn, g_ffn, g_final.reshape(1, d), g_ret, g_q_lora, g_kv_lora,
        w_in_k, w_uq_k, w_ukv_k, [slots[k] for k in _LATE], (core, slot))

    g4 = [
        g_early[0][:IN_COLS].reshape(N_CHIPS, IN_COLS // N_CHIPS, d),
        g_early[1].reshape(N_CHIPS, MLA_HEAD, Q_LORA)[:, :head_rows],
        g_early[2],
    ]
    *got, gathered = _run_rider(
        _merge_riders(_exchange_rider(g4), _swap_rider(late), _gather8_rider(_pack_small(st_post, st_ret, st_pre))),
        "rs_exchange")
    got, late_theirs = got[:len(g4)], got[len(g4):]
    partial = [_add_half(g, r, core, "add_half_" + k) for g, r, k in zip(g4, got, _EARLY)]
    tot = _small_reduce(gathered)
    dm = jnp.concatenate([
        gathered[:, 12:24].reshape(N_DEV * nex, 6 * d),
        jnp.concatenate([tot[8:10].reshape(1, 2 * d), jnp.zeros((1, 4 * d), F32)], axis=1),
        jnp.zeros((7, 6 * d), F32)], axis=0)
    dm_sh = lax.dynamic_slice(dm, (0, chip * n_ada), (dm.shape[0], n_ada))
    g_ada, da = _mod_bwd(a_in, dm_sh, w_ada[0])
    *landed, dcc = _run_rider(_merge_riders(_scatter_rider(partial), _gather8_rider(da[N_DEV * nex:])), "rs_scatter")
    mine = [_sum_chips(p, l, slot, "sum_chips_" + k) for p, l, k in zip(partial, landed, _EARLY)]
    theirs = _run_rider(_swap_rider(mine), "rs_swap")
    halves = dict(zip(_EARLY, zip(mine, theirs)))
    halves.update(zip(_LATE, zip(late, late_theirs)))
    grad, delta, new_m, new_v = {}, {}, {}, {}
    for k in _BIG:
        a, b = halves[k]
        shp = w[k].shape
        outs = _adamw_halves(shard_of(w, k), a, b, shard_of(m, k), shard_of(v, k), core, "adamw_" + k)
        grad[k], delta[k], new_m[k], new_v[k] = [(o.T if k in _TRANSPOSED else o).reshape(shp) for o in outs]

    shp = w_ada.shape
    outs = _adamw(w_ada[0], g_ada, m["w_ada"][0], v["w_ada"][0], "adamw_w_ada")
    grad["w_ada"] = g_ada.reshape(shp)
    delta["w_ada"], new_m["w_ada"], new_v["w_ada"] = [o.reshape(shp) for o in outs]
    rows = [{k: t[k].reshape(1, -1) for k in _SMALL_NAMES} for t in (w, m, v)]
    small = _small_final(tot, dcc, sg8, *[[t[k] for k in _SMALL_NAMES] for t in rows])
    for res, outs in zip((grad, delta, new_m, new_v), small[:4]):
        for k, o in zip(_SMALL_NAMES, outs):
            res[k] = o.reshape(w[k].shape)
    return (small[4][0, 0], gx, *[grad[k] for k in _WEIGHTS], *[delta[k] for k in _WEIGHTS],
            *[new_m[k] for k in _WEIGHTS], *[new_v[k] for k in _WEIGHTS])
```

```python
import functools
import math

import jax
import jax.numpy as jnp
from jax import lax
from jax.experimental import pallas as pl
from jax.experimental.pallas import tpu as pltpu

F32 = jnp.float32
BF16 = jnp.bfloat16
MESH = pl.DeviceIdType.MESH

EPS = 1e-6
D_MODEL = 1024
D_FF = 4096
HEADS = 4
RET_DK = 64
RET_DV = 128
MLA_NOPE = 128
MLA_ROPE = 64
MLA_HEAD = 256
Q_LORA = 384
KV_LORA = 256
GRID_W = 64
ROPE_BASE = 10000.0
IN_COLS = 2240
IN_PAD = 2304
PG_COLS = 1152
N_CHIPS = 4
N_DEV = 8
LANES = 128
ADAM_LR = 0.001
ADAM_B1 = 0.9
ADAM_B2 = 0.999
ADAM_EPS = 1e-08
ADAM_WD = 0.01
ADAM_STEP = 10
VMEM_LIMIT = 56 * 1024 * 1024


def _dot(a, b):
    return jnp.dot(a, b, preferred_element_type=F32)


def _dot_nt(a, b):
    return lax.dot_general(a, b, (((1,), (1,)), ((), ())), preferred_element_type=F32)


def _dot_tn(a, b):
    return lax.dot_general(a, b, (((0,), (0,)), ((), ())), preferred_element_type=F32)


def _params(sem=None, vmem=None):
    return pltpu.CompilerParams(dimension_semantics=sem, vmem_limit_bytes=vmem)


def _full(shape):
    n = len(shape)
    return pl.BlockSpec(shape, lambda *_: (0,) * n)


def _rope(x, cos, sin):
    w = x.shape[-1]
    lo = (lax.broadcasted_iota(jnp.int32, (1, w), 1) % 64) < 32
    swapped = jnp.where(lo, pltpu.roll(x, w - 32, 1), pltpu.roll(x, 32, 1))
    return x * cos + swapped * sin


def _rope_t(g, cos, sin):
    w = g.shape[-1]
    lo = (lax.broadcasted_iota(jnp.int32, (1, w), 1) % 64) < 32
    t = g * sin
    swapped = jnp.where(lo, pltpu.roll(t, w - 32, 1), pltpu.roll(t, 32, 1))
    return g * cos + swapped


def _rope_tables(seq, tm):
    rows = seq // GRID_W
    row = jnp.repeat(jnp.arange(rows, dtype=F32), GRID_W)
    col = jnp.tile(jnp.arange(GRID_W, dtype=F32), rows)
    n_freq = RET_DK // 4
    freq = ROPE_BASE ** (-jnp.arange(n_freq, dtype=F32) / n_freq)
    ang = jnp.concatenate([row[:, None] * freq, col[:, None] * freq], axis=-1)
    cos, sin = jnp.cos(ang), jnp.sin(ang)
    cos_t = jnp.tile(jnp.concatenate([cos, cos], -1), (1, HEADS))
    sin_t = jnp.tile(jnp.concatenate([-sin, sin], -1), (1, HEADS))
    cos_t = jnp.concatenate([cos_t, jnp.ones((tm, 4 * RET_DK), F32)], 0)
    sin_t = jnp.concatenate([sin_t, jnp.zeros((tm, 4 * RET_DK), F32)], 0)
    return cos_t, sin_t


def _adam_math(w, g, m, v):
    mn = ADAM_B1 * m + (1.0 - ADAM_B1) * g
    vn = ADAM_B2 * v + (1.0 - ADAM_B2) * (g * g)
    m_hat = mn / (1.0 - ADAM_B1 ** ADAM_STEP)
    v_hat = vn / (1.0 - ADAM_B2 ** ADAM_STEP)
    return -ADAM_LR * (m_hat / (jnp.sqrt(v_hat) + ADAM_EPS) + ADAM_WD * w), mn, vn


def _cast_into_slot(w, slot, name):
    r, c = w.shape
    rb = max(b for b in range(16, 257, 16) if r % b == 0)

    def body(s_ref, w_ref, o_ref):
        o_ref[...] = w_ref[...].astype(BF16)

    return pl.pallas_call(
        body, name=name,
        grid_spec=pltpu.PrefetchScalarGridSpec(
            num_scalar_prefetch=1, grid=(r // rb,),
            in_specs=[pl.BlockSpec((rb, c), lambda i, s: (i, 0))],
            out_specs=pl.BlockSpec((None, rb, c), lambda i, s: (s[0], i, 0))),
        out_shape=jax.ShapeDtypeStruct((N_CHIPS, r, c), BF16),
        compiler_params=_params(("parallel",)),
    )(slot, w)


def _adamw_halves(w, mine, theirs, m, v, core, name):
    r, c = w.shape
    r2 = r // 2
    rb = max(b for b in range(8, r2 + 1, 8) if r2 % b == 0 and b * c * 4 <= (1 << 21))
    nbh = r2 // rb

    def body(z_ref, w_ref, a_ref, b_ref, m_ref, v_ref, g_ref, d_ref, mo_ref, vo_ref):
        here = (pl.program_id(0) // nbh) == z_ref[0]
        gg = jnp.where(here, a_ref[...], b_ref[...])
        g_ref[...] = gg
        d_ref[...], mo_ref[...], vo_ref[...] = _adam_math(w_ref[...], gg, m_ref[...], v_ref[...])

    spec = pl.BlockSpec((rb, c), lambda i, z: (i, 0))
    a_spec = pl.BlockSpec((rb, c), lambda i, z: (jnp.clip(i - z[0] * nbh, 0, nbh - 1), 0))
    b_spec = pl.BlockSpec((rb, c), lambda i, z: (jnp.clip(i - (1 - z[0]) * nbh, 0, nbh - 1), 0))
    shp = jax.ShapeDtypeStruct((r, c), F32)
    return pl.pallas_call(
        body, name=name,
        grid_spec=pltpu.PrefetchScalarGridSpec(
            num_scalar_prefetch=1, grid=(r // rb,), in_specs=[spec, a_spec, b_spec, spec, spec], out_specs=[spec] * 4),
        out_shape=[shp] * 4,
        compiler_params=_params(("parallel",)),
    )(core, w, mine, theirs, m, v)


def _adamw(w, g, m, v, name):
    r, c = w.shape
    rb = r
    for cand in (256, 128, 64, 32, 16, 8):
        if r % cand == 0 and cand * c * 4 <= (1 << 20):
            rb = cand
            break
    if r * c * 4 <= (1 << 20):
        rb = r

    def body(w_ref, g_ref, m_ref, v_ref, d_ref, mo_ref, vo_ref):
        d_ref[...], mo_ref[...], vo_ref[...] = _adam_math(w_ref[...], g_ref[...], m_ref[...], v_ref[...])

    spec = pl.BlockSpec((rb, c), lambda i: (i, 0))
    shp = jax.ShapeDtypeStruct((r, c), F32)
    return pl.pallas_call(
        body, name=name, grid=(r // rb,), in_specs=[spec] * 4, out_specs=[spec] * 3, out_shape=[shp] * 3,
        compiler_params=_params(("parallel",)),
    )(w, g, m, v)


def _decay_prep(dec):
    def body(d_ref, lg_ref, sg_ref):
        d = d_ref[...]
        lg_ref[...] = jnp.minimum(d, 0.0) - jnp.log(1.0 + jnp.exp(-jnp.abs(d)))
        sg_ref[...] = 1.0 / (1.0 + jnp.exp(d))

    shp = jax.ShapeDtypeStruct(dec.shape, F32)
    return pl.pallas_call(body, name="decay_prep", out_shape=[shp, shp])(dec)


def _mod_fwd(a_in, w_ada, b_sh):
    rows, d = a_in.shape
    n = w_ada.shape[1]
    bn = 512

    def body(a_ref, w_ref, b_ref, o_ref):
        a = a_ref[...]
        s = (a / (1.0 + jnp.exp(-a))).astype(BF16)
        o_ref[...] = _dot(s, w_ref[...].astype(BF16)) + b_ref[...]

    return pl.pallas_call(
        body, name="mod_fwd", grid=(n // bn,),
        in_specs=[_full((rows, d)), pl.BlockSpec((d, bn), lambda j: (0, j)), pl.BlockSpec((1, bn), lambda j: (0, j))],
        out_specs=pl.BlockSpec((rows, bn), lambda j: (0, j)),
        out_shape=jax.ShapeDtypeStruct((rows, n), F32),
        compiler_params=_params(("parallel",)),
    )(a_in, w_ada, b_sh)


def _mod_bwd(a_in, dm, w_ada):
    rows, d = a_in.shape
    n = w_ada.shape[1]
    bn = 512
    nb = n // bn

    def body(a_ref, dm_ref, w_ref, gw_ref, da_ref):
        j = pl.program_id(0)
        a = a_ref[...]
        s = (a / (1.0 + jnp.exp(-a))).astype(BF16)
        dmb = dm_ref[...].astype(BF16)
        gw_ref[...] = _dot_tn(s, dmb)
        part = _dot_nt(dmb, w_ref[...].astype(BF16))

        @pl.when(j == 0)
        def _():
            da_ref[...] = part

        @pl.when(j > 0)
        def _():
            da_ref[...] += part

    return pl.pallas_call(
        body, name="mod_bwd", grid=(nb,),
        in_specs=[_full((rows, d)), pl.BlockSpec((rows, bn), lambda j: (0, j)), pl.BlockSpec((d, bn), lambda j: (0, j))],
        out_specs=[pl.BlockSpec((d, bn), lambda j: (0, j)), _full((rows, d))],
        out_shape=[jax.ShapeDtypeStruct((d, n), F32), jax.ShapeDtypeStruct((rows, d), F32)],
        compiler_params=_params(("arbitrary",)),
    )(a_in, dm, w_ada)


def _pre_fwd(x2, ctx2, modv, g_attn, w_in, g_q, g_kv, w_uq, w_ukv, cos_t, sin_t, *, seq, tm, rider=None):
    t_lat, d = x2.shape
    t_ctx = ctx2.shape[0]
    nl, nc = t_lat // tm, t_ctx // tm
    n_all = t_lat + t_ctx
    tpe = seq // tm
    nex = t_lat // seq

    def body(x_ref, c_ref, mod_ref, g_ref, win_ref, gq_ref, gkv_ref, wuq_ref, wukv_ref, cos_ref, sin_ref,
             h_ref, pg_ref, rq_ref, rk_ref, rv_ref, nq_ref, nkv_ref, q_ref, k_ref, v_ref):
        i = pl.program_id(0)
        xt = jnp.where(i < nl, x_ref[...], c_ref[...])
        sh = mod_ref[0, 0:1, :]
        sc = mod_ref[0, 1:2, :]
        r = lax.rsqrt(jnp.mean(xt * xt, axis=-1, keepdims=True) + EPS)
        hb = ((xt * r) * g_ref[...] * (1.0 + sc) + sh).astype(BF16)
        h_ref[...] = hb
        p = _dot_nt(hb, win_ref[...])
        cos = cos_ref[...]
        sin = sin_ref[...]
        rq_ref[...] = _rope(p[:, 0:256], cos, sin).astype(BF16)
        rk_ref[...] = _rope(p[:, 256:512] * (RET_DK ** -0.5), cos, sin).astype(BF16)
        rv_ref[...] = p[:, 512:1024].astype(BF16)
        pg_ref[...] = p[:, 1024:2176]
        cq = p[:, 1536:1920]
        ckv = p[:, 1920:2176]
        nqb = (cq * lax.rsqrt(jnp.mean(cq * cq, axis=-1, keepdims=True) + EPS) * gq_ref[...]).astype(BF16)
        nkvb = (ckv * lax.rsqrt(jnp.mean(ckv * ckv, axis=-1, keepdims=True) + EPS) * gkv_ref[...]).astype(BF16)
        nq_ref[...] = nqb
        nkv_ref[...] = nkvb
        cos1 = cos[:, 0:LANES]
        sin1 = sin[:, 0:LANES]
        kpe = _rope(p[:, 2176:2304], cos1, sin1).astype(BF16)
        for hd in range(HEADS):
            o = hd * MLA_HEAD
            qh = _dot_nt(nqb, wuq_ref[hd])
            q_ref[:, o:o + 128] = qh[:, 0:128].astype(BF16)
            q_ref[:, o + 128:o + 256] = _rope(qh[:, 128:256], cos1, sin1).astype(BF16)
            kvh = _dot(nkvb, wukv_ref[hd])
            k_ref[:, o:o + 128] = kvh[:, 0:128].astype(BF16)
            k_ref[:, o + 128:o + 256] = kpe
            v_ref[:, hd * 128:(hd + 1) * 128] = kvh[:, 128:256].astype(BF16)

    def tile(width):
        return pl.BlockSpec((tm, width), lambda i: (i, 0))

    widths = (d, PG_COLS, 256, 256, 512, Q_LORA, KV_LORA, HEADS * MLA_HEAD, HEADS * MLA_HEAD, HEADS * 128)
    dtypes = (BF16, F32, BF16, BF16, BF16, BF16, BF16, BF16, BF16, BF16)
    tab = pl.BlockSpec((tm, 256), lambda i: (jnp.where(i < nl, i % tpe, tpe), 0))
    return _hosted_call(
        body, (x2, ctx2, modv, g_attn, w_in, g_q, g_kv, w_uq, w_ukv, cos_t, sin_t), name="pre_fwd", grid=(nl + nc,),
        in_specs=[
            pl.BlockSpec((tm, d), lambda i: (jnp.minimum(i, nl - 1), 0)),
            pl.BlockSpec((tm, d), lambda i: (jnp.maximum(i - nl, 0), 0)),
            pl.BlockSpec((1, 8, d), lambda i: (jnp.minimum(i // tpe, nex), 0, 0)),
            _full((1, d)), _full(w_in.shape), _full((1, Q_LORA)), _full((1, KV_LORA)),
            _full(w_uq.shape), _full(w_ukv.shape), tab, tab,
        ],
        out_specs=[tile(w) for w in widths],
        out_shape=[jax.ShapeDtypeStruct((n_all, w), dt) for w, dt in zip(widths, dtypes)],
        sem=("parallel",), rider=rider)


def _post(yret, ymla, x2, tgt2, modv, g_ffn, g_fin, w_out, w_ff1, w_ff2, *, seq, tm):
    t_lat, d = x2.shape
    nl = t_lat // tm
    tpe = seq // tm
    nex = t_lat // seq
    n_slab = w_ff1.shape[0]
    fs = w_ff1.shape[2]

    def body(yr_ref, ym_ref, x_ref, t_ref, mod_ref, gf_ref, gl_ref, wo_ref, w1_ref, w2_ref,
             mix_ref, a_ref, du_ref, h2_ref, df_ref, dmo_ref, dmix_ref, dxm_ref, st_ref, ru_ref):
        i = pl.program_id(0)
        gt_a = mod_ref[0, 2:3, :]
        sh_f = mod_ref[0, 3:4, :]
        sc_f = mod_ref[0, 4:5, :]
        gt_f = mod_ref[0, 5:6, :]
        g_ffn_v = gf_ref[...]
        g_fin_v = gl_ref[...]
        yr = yr_ref[...]
        ym = ym_ref[...]
        mix_ref[:, 0:512] = yr
        mix_ref[:, 512:1024] = ym
        op = _dot(yr, wo_ref[0:512, :]) + _dot(ym, wo_ref[512:1024, :])
        x_mid = x_ref[...] + gt_a * op
        r2 = lax.rsqrt(jnp.mean(x_mid * x_mid, axis=-1, keepdims=True) + EPS)
        xh2 = x_mid * r2
        h2b = (xh2 * g_ffn_v * (1.0 + sc_f) + sh_f).astype(BF16)
        h2_ref[...] = h2b
        f = jnp.zeros((tm, d), F32)
        for s in range(n_slab):
            ru = jnp.maximum(_dot(h2b, w1_ref[s]), 0.0)
            ru_ref[:, s * fs:(s + 1) * fs] = ru
            ab = (ru * ru).astype(BF16)
            a_ref[:, s * fs:(s + 1) * fs] = ab
            f = f + _dot(ab, w2_ref[s * fs:(s + 1) * fs, :])
        x_out = x_mid + gt_f * f
        r3 = lax.rsqrt(jnp.mean(x_out * x_out, axis=-1, keepdims=True) + EPS)
        xh3 = x_out * r3
        err = xh3 * g_fin_v - t_ref[...]
        dy = err * (1.0 / d)
        dxh3 = dy * g_fin_v
        dx_out = r3 * (dxh3 - xh3 * jnp.mean(dxh3 * xh3, axis=-1, keepdims=True))
        dfb = (dx_out * gt_f).astype(BF16)
        df_ref[...] = dfb
        dh2 = jnp.zeros((tm, d), F32)
        for s in range(n_slab):
            da = _dot_nt(dfb, w2_ref[s * fs:(s + 1) * fs, :])
            dub = (da * (2.0 * ru_ref[:, s * fs:(s + 1) * fs])).astype(BF16)
            du_ref[:, s * fs:(s + 1) * fs] = dub
            dh2 = dh2 + _dot_nt(dub, w1_ref[s])
        dxh2 = dh2 * (1.0 + sc_f) * g_ffn_v
        dx_mid = dx_out + r2 * (dxh2 - xh2 * jnp.mean(dxh2 * xh2, axis=-1, keepdims=True))
        dxm_ref[...] = dx_mid
        dmob = (dx_mid * gt_a).astype(BF16)
        dmo_ref[...] = dmob
        dmix_ref[...] = _dot_nt(dmob, wo_ref[...]).astype(BF16)

        def rsum(v):
            return jnp.sum(v, axis=0, keepdims=True)

        stats = jnp.concatenate([
            rsum(dh2), rsum(dh2 * xh2 * g_ffn_v), rsum(dx_out * f), rsum(dx_mid * op),
            rsum(dh2 * (1.0 + sc_f) * xh2), rsum(dy * xh3), rsum(err * err), jnp.zeros((1, d), F32)], axis=0)

        @pl.when(i % tpe == 0)
        def _():
            st_ref[0] = stats

        @pl.when(i % tpe != 0)
        def _():
            st_ref[0] += stats

    def tile(width):
        return pl.BlockSpec((tm, width), lambda i: (i, 0))

    widths = (d, D_FF, D_FF, d, d, d, d, d)
    dtypes = (BF16, BF16, BF16, BF16, BF16, BF16, BF16, F32)
    const = pl.Buffered(1)
    return pl.pallas_call(
        body, name="post", grid=(nl,),
        in_specs=[
            tile(512), tile(512), tile(d), tile(d),
            pl.BlockSpec((1, 8, d), lambda i: (i // tpe, 0, 0)),
            _full((1, d)), _full((1, d)),
            pl.BlockSpec(w_out.shape, lambda i: (0, 0), pipeline_mode=const),
            pl.BlockSpec(w_ff1.shape, lambda i: (0, 0, 0), pipeline_mode=const),
            pl.BlockSpec(w_ff2.shape, lambda i: (0, 0), pipeline_mode=const),
        ],
        out_specs=[tile(w) for w in widths] + [pl.BlockSpec((1, 8, d), lambda i: (i // tpe, 0, 0))],
        out_shape=[jax.ShapeDtypeStruct((t_lat, w), dt) for w, dt in zip(widths, dtypes)]
        + [jax.ShapeDtypeStruct((nex, 8, d), F32)],
        scratch_shapes=[pltpu.VMEM((tm, D_FF), F32)],
        compiler_params=_params(("arbitrary",), VMEM_LIMIT),
    )(yret, ymla, x2, tgt2, modv, g_ffn, g_fin, w_out, w_ff1, w_ff2)


def _pre_bwd(x2, ctx2, modv, g_attn, pg, drq, drk, dkc_r, drv, dvc_r, drg, dq_m, dkl, dkc, dvl, dvc, dxm,
             w_in, g_q, g_kv, w_uq, w_ukv, cos_t, sin_t, *, seq, tm, rider=None):
    t_lat, d = x2.shape
    t_ctx = ctx2.shape[0]
    nl, nc = t_lat // tm, t_ctx // tm
    n_all = t_lat + t_ctx
    tpe = seq // tm
    nex = t_lat // seq

    def body(x_ref, c_ref, mod_ref, g_ref, pg_ref, drq_ref, drk_ref, dkcr_ref, drv_ref, dvcr_ref, drg_ref,
             dq_ref, dkl_ref, dkc_ref, dvl_ref, dvc_ref, dxm_ref, win_ref, gq_ref, gkv_ref, wuq_ref, wukv_ref,
             cos_ref, sin_ref, dpb_ref, dqf_ref, dkvf_ref, gx_ref, st_ref):
        i = pl.program_id(0)
        lat = i < nl
        latf = lat.astype(F32)
        cos = cos_ref[...]
        sin = sin_ref[...]
        cos1 = cos[:, 0:LANES]
        sin1 = sin[:, 0:LANES]
        d_rq = _rope_t(drq_ref[...] * latf, cos, sin)
        d_rk = _rope_t(jnp.where(lat, drk_ref[...], dkcr_ref[...]), cos, sin) * (RET_DK ** -0.5)
        d_rv = jnp.where(lat, drv_ref[...], dvcr_ref[...])
        d_rg = drg_ref[...] * latf
        dq_all = dq_ref[...] * latf
        dk_all = jnp.where(lat, dkl_ref[...], dkc_ref[...])
        dv_all = jnp.where(lat, dvl_ref[...], dvc_ref[...])
        dnq = jnp.zeros((tm, Q_LORA), F32)
        dnkv = jnp.zeros((tm, KV_LORA), F32)
        dkpe = jnp.zeros((tm, LANES), F32)
        for hd in range(HEADS):
            o = hd * MLA_HEAD
            dqh = jnp.concatenate([dq_all[:, o:o + 128], _rope_t(dq_all[:, o + 128:o + 256], cos1, sin1)],
                                  axis=1).astype(BF16)
            dqf_ref[:, o:o + 256] = dqh
            dnq = dnq + _dot(dqh, wuq_ref[hd])
            dkpe = dkpe + dk_all[:, o + 128:o + 256]
            dkvh = jnp.concatenate([dk_all[:, o:o + 128], dv_all[:, hd * 128:(hd + 1) * 128]], axis=1).astype(BF16)
            dkvf_ref[:, o:o + 256] = dkvh
            dnkv = dnkv + _dot_nt(dkvh, wukv_ref[hd])
        d_kpe = _rope_t(dkpe, cos1, sin1)
        pgv = pg_ref[...]
        cq = pgv[:, 512:896]
        ckv = pgv[:, 896:1152]
        rq_ = lax.rsqrt(jnp.mean(cq * cq, axis=-1, keepdims=True) + EPS)
        cqh = cq * rq_
        dcqh = dnq * gq_ref[...]
        d_cq = rq_ * (dcqh - cqh * jnp.mean(dcqh * cqh, axis=-1, keepdims=True))
        rkv_ = lax.rsqrt(jnp.mean(ckv * ckv, axis=-1, keepdims=True) + EPS)
        ckvh = ckv * rkv_
        dckvh = dnkv * gkv_ref[...]
        d_ckv = rkv_ * (dckvh - ckvh * jnp.mean(dckvh * ckvh, axis=-1, keepdims=True))
        dpb = jnp.concatenate([d_rq, d_rk, d_rv, d_rg, d_cq, d_ckv, d_kpe], axis=1).astype(BF16)
        dpb_ref[...] = dpb
        dh = _dot(dpb, win_ref[...])
        xt = jnp.where(lat, x_ref[...], c_ref[...])
        sc = mod_ref[0, 1:2, :]
        g = g_ref[...]
        r = lax.rsqrt(jnp.mean(xt * xt, axis=-1, keepdims=True) + EPS)
        xh = xt * r
        dxh = dh * (1.0 + sc) * g
        dx = r * (dxh - xh * jnp.mean(dxh * xh, axis=-1, keepdims=True))

        @pl.when(lat)
        def _():
            gx_ref[...] = dxm_ref[...] + dx

        def rsum(v):
            return jnp.sum(v, axis=0, keepdims=True)

        def widen(v):
            return jnp.concatenate([v, jnp.zeros((1, d - v.shape[1]), F32)], axis=1)

        stats = jnp.concatenate([
            rsum(dh), rsum(dh * xh * g), rsum(dh * (1.0 + sc) * xh), widen(rsum(dnq * cqh)), widen(rsum(dnkv * ckvh)),
            jnp.zeros((3, d), F32)], axis=0)
        first = jnp.logical_or(jnp.logical_and(lat, i % tpe == 0), i == nl)

        @pl.when(first)
        def _():
            st_ref[0] = stats

        @pl.when(jnp.logical_not(first))
        def _():
            st_ref[0] += stats

    def lat_tile(width):
        return pl.BlockSpec((tm, width), lambda i: (jnp.minimum(i, nl - 1), 0))

    def ctx_tile(width):
        return pl.BlockSpec((tm, width), lambda i: (jnp.maximum(i - nl, 0), 0))

    def tile(width):
        return pl.BlockSpec((tm, width), lambda i: (i, 0))

    tab = pl.BlockSpec((tm, 256), lambda i: (jnp.where(i < nl, i % tpe, tpe), 0))
    ex = pl.BlockSpec((1, 8, d), lambda i: (jnp.minimum(i // tpe, nex), 0, 0))
    return _hosted_call(
        body, (x2, ctx2, modv, g_attn, pg, drq, drk, dkc_r, drv, dvc_r, drg, dq_m, dkl, dkc, dvl, dvc, dxm,
               w_in, g_q, g_kv, w_uq, w_ukv, cos_t, sin_t), name="pre_bwd", grid=(nl + nc,),
        in_specs=[
            lat_tile(d), ctx_tile(d), ex, _full((1, d)), tile(PG_COLS),
            lat_tile(256), lat_tile(256), ctx_tile(256), lat_tile(512), ctx_tile(512), lat_tile(512),
            lat_tile(1024), lat_tile(1024), ctx_tile(1024), lat_tile(512), ctx_tile(512), lat_tile(d),
            _full(w_in.shape), _full((1, Q_LORA)), _full((1, KV_LORA)), _full(w_uq.shape), _full(w_ukv.shape),
            tab, tab,
        ],
        out_specs=[tile(IN_PAD), tile(1024), tile(1024), lat_tile(d), ex],
        out_shape=[
            jax.ShapeDtypeStruct((n_all, IN_PAD), BF16), jax.ShapeDtypeStruct((n_all, 1024), BF16),
            jax.ShapeDtypeStruct((n_all, 1024), BF16), jax.ShapeDtypeStruct((t_lat, d), F32),
            jax.ShapeDtypeStruct((nex + 1, 8, d), F32),
        ],
        sem=("arbitrary",), rider=rider)


MLA_SCALE = 1.0 / math.sqrt(MLA_NOPE + MLA_ROPE)
KEY_BLOCK = 2048


def _mla_specs(t_lat, seq, ctx_len, tq):
    nqt = seq // tq
    cb = t_lat // ctx_len
    q = pl.BlockSpec((tq, MLA_HEAD), lambda b, h, j: (b * nqt + j, h))
    kl = pl.BlockSpec((seq, MLA_HEAD), lambda b, h, j: (b, h))
    kc = pl.BlockSpec((ctx_len, MLA_HEAD), lambda b, h, j: (cb + b, h))
    vl = pl.BlockSpec((seq, 128), lambda b, h, j: (b, h))
    vc = pl.BlockSpec((ctx_len, 128), lambda b, h, j: (cb + b, h))
    o = pl.BlockSpec((tq, 128), lambda b, h, j: (b * nqt + j, h))
    return q, kl, kc, vl, vc, o


def _mla_fwd(q, k, v, *, t_lat, seq, ctx_len, tq, rider=None):
    nex = t_lat // seq

    def body(q_ref, kl_ref, kc_ref, vl_ref, vc_ref, o_ref, lse_ref):
        qb = q_ref[...]
        s = _dot_nt(qb, kl_ref[...]) * MLA_SCALE
        sc = _dot_nt(qb, kc_ref[...]) * MLA_SCALE
        m = jnp.maximum(jnp.max(s, axis=-1, keepdims=True), jnp.max(sc, axis=-1, keepdims=True))
        p = jnp.exp(s - m)
        pc = jnp.exp(sc - m)
        total = jnp.sum(p, axis=-1, keepdims=True) + jnp.sum(pc, axis=-1, keepdims=True)
        o = _dot(p.astype(BF16), vl_ref[...]) + _dot(pc.astype(BF16), vc_ref[...])
        o_ref[...] = (o * (1.0 / total)).astype(BF16)
        lse_ref[...] = jnp.broadcast_to(m + jnp.log(total), lse_ref.shape)

    qs, kl, kc, vl, vc, os_ = _mla_specs(t_lat, seq, ctx_len, tq)
    return _hosted_call(
        body, (q, k, k, v, v), name="mla_fwd", grid=(nex, HEADS, seq // tq),
        in_specs=[qs, kl, kc, vl, vc], out_specs=[os_, os_],
        out_shape=[jax.ShapeDtypeStruct((t_lat, HEADS * 128), BF16), jax.ShapeDtypeStruct((t_lat, HEADS * 128), F32)],
        sem=("parallel", "parallel", "arbitrary"), rider=rider)


def _mla_bwd(q, k, v, ymla, lse, dmix, *, t_lat, seq, ctx_len, tq, rider=None):
    nex = t_lat // seq
    nqt = seq // tq
    t_ctx = nex * ctx_len
    kb = min(KEY_BLOCK, seq)

    def body(q_ref, kl_ref, kc_ref, vl_ref, vc_ref, o_ref, lse_ref, do_ref, dq_ref, dkl_ref, dkc_ref, dvl_ref, dvc_ref):
        j = pl.program_id(2)

        @pl.when(j == 0)
        def _():
            dkl_ref[...] = jnp.zeros(dkl_ref.shape, F32)
            dkc_ref[...] = jnp.zeros(dkc_ref.shape, F32)
            dvl_ref[...] = jnp.zeros(dvl_ref.shape, F32)
            dvc_ref[...] = jnp.zeros(dvc_ref.shape, F32)

        qb = q_ref[...]
        dob = do_ref[...]
        delta = jnp.sum(dob.astype(F32) * o_ref[...].astype(F32), axis=-1, keepdims=True)
        lse_row = lse_ref[:, 0:1]

        def block(k_ref, v_ref, dk_ref, dv_ref, rows):
            kbl = k_ref[rows, :]
            vbl = v_ref[rows, :]
            p = jnp.exp(_dot_nt(qb, kbl) * MLA_SCALE - lse_row)
            ds = (p * (_dot_nt(dob, vbl) - delta) * MLA_SCALE).astype(BF16)
            dk_ref[rows, :] += _dot_tn(ds, qb)
            dv_ref[rows, :] += _dot_tn(p.astype(BF16), dob)
            return _dot(ds, kbl)

        dq = block(kc_ref, vc_ref, dkc_ref, dvc_ref, pl.ds(0, ctx_len))
        for i in range(seq // kb):
            dq = dq + block(kl_ref, vl_ref, dkl_ref, dvl_ref, pl.ds(i * kb, kb))
        dq_ref[...] = dq

    qs, kl, kc, vl, vc, os_ = _mla_specs(t_lat, seq, ctx_len, tq)
    do_spec = pl.BlockSpec((tq, 128), lambda b, h, j: (b * nqt + j, HEADS + h))
    return _hosted_call(
        body, (q, k, k, v, v, ymla, lse, dmix), name="mla_bwd", grid=(nex, HEADS, nqt),
        in_specs=[qs, kl, kc, vl, vc, os_, os_, do_spec],
        out_specs=[
            qs,
            pl.BlockSpec((seq, MLA_HEAD), lambda b, h, j: (b, h)),
            pl.BlockSpec((ctx_len, MLA_HEAD), lambda b, h, j: (b, h)),
            pl.BlockSpec((seq, 128), lambda b, h, j: (b, h)),
            pl.BlockSpec((ctx_len, 128), lambda b, h, j: (b, h)),
        ],
        out_shape=[
            jax.ShapeDtypeStruct((t_lat, HEADS * MLA_HEAD), F32),
            jax.ShapeDtypeStruct((t_lat, HEADS * MLA_HEAD), F32),
            jax.ShapeDtypeStruct((t_ctx, HEADS * MLA_HEAD), F32),
            jax.ShapeDtypeStruct((t_lat, HEADS * 128), F32),
            jax.ShapeDtypeStruct((t_ctx, HEADS * 128), F32),
        ],
        sem=("parallel", "parallel", "arbitrary"), rider=rider)


def _decay_terms(lg, chunk, forward):
    ii = lax.broadcasted_iota(jnp.int32, (chunk, chunk), 0)
    jj = lax.broadcasted_iota(jnp.int32, (chunk, chunk), 1)
    diff = (ii - jj) if forward else (jj - ii)
    dist = jnp.maximum(diff, 0).astype(F32)
    dmat = jnp.where(diff >= 0, jnp.exp(lg * dist), 0.0)
    pos = lax.broadcasted_iota(jnp.int32, (chunk, 1), 0).astype(F32)
    if forward:
        e_q = pos + 1.0
        e_k = (chunk - 1.0) - pos
    else:
        e_q = chunk - pos
        e_k = pos
    wq = jnp.exp(lg * e_q)
    wk = jnp.exp(lg * e_k)
    cd = jnp.exp(jnp.full((1, 1), lg * chunk, F32))
    return dmat, dist, wq, wk, e_q, e_k, cd


def _ctx_weights(lg, ctx_len, forward):
    pos = lax.broadcasted_iota(jnp.int32, (ctx_len, 1), 0).astype(F32)
    e = ((ctx_len - 1.0) - pos) if forward else pos
    return jnp.exp(lg * e), e


def _pair_specs(t_lat, seq, ctx_len):
    cb = t_lat // ctx_len
    qk = pl.BlockSpec((seq, 128), lambda b, p: (b, p))
    v = pl.BlockSpec((seq, 256), lambda b, p: (b, p))
    kc = pl.BlockSpec((ctx_len, 128), lambda b, p: (cb + b, p))
    vc = pl.BlockSpec((ctx_len, 256), lambda b, p: (cb + b, p))
    return qk, v, kc, vc


def _lane_masks():
    lane = lax.broadcasted_iota(jnp.int32, (1, 128), 1)
    return [(lane // RET_DK) == hh for hh in (0, 1)]


def _ret_fwd_pair(rq, rk, rv, pg, lg, g_ret, *, t_lat, seq, ctx_len, chunk, rider=None):
    nex = t_lat // seq
    n_chunk = seq // chunk

    def body(q_ref, k_ref, v_ref, kc_ref, vc_ref, rg_ref, lg_ref, g_ref, y_ref, o_ref):
        pair = pl.program_id(1)
        masks = _lane_masks()
        kcf = kc_ref[...].astype(F32)

        def run(forward):
            terms, s0 = [], []
            for hh in (0, 1):
                lgd = lg_ref[0 if forward else 1, 2 * pair + hh]
                terms.append(_decay_terms(lgd, chunk, forward))
                wc, _ = _ctx_weights(lgd, ctx_len, forward)
                s0.append(_dot_tn((jnp.where(masks[hh], kcf, 0.0) * wc).astype(BF16), vc_ref[:, hh * 128:(hh + 1) * 128]))

            def step(t, states):
                n = t if forward else n_chunk - 1 - t
                sl = pl.ds(pl.multiple_of(n * chunk, chunk), chunk)
                qb = q_ref[sl, :]
                kf_all = k_ref[sl, :].astype(F32)
                new = []
                for hh in (0, 1):
                    dmat, _, wq, wk, _, _, cd = terms[hh]
                    cols = slice(hh * 128, (hh + 1) * 128)
                    qm = jnp.where(masks[hh], qb, jnp.zeros((), BF16))
                    kf = jnp.where(masks[hh], kf_all, 0.0)
                    vb = v_ref[sl, cols]
                    a = _dot_nt(qm, kf.astype(BF16)) * dmat
                    o = _dot(a.astype(BF16), vb) + wq * _dot(qm, states[hh].astype(BF16))
                    if forward:
                        o_ref[sl, cols] = o
                    else:
                        o = o_ref[sl, cols] + o
                        o_ref[sl, cols] = o
                        mu = jnp.mean(o, axis=-1, keepdims=True)
                        oc = o - mu
                        var = jnp.mean(oc * oc, axis=-1, keepdims=True)
                        rg = rg_ref[sl, cols]
                        y_ref[sl, cols] = (oc * lax.rsqrt(var + EPS) * g_ref[:, cols] * (rg / (1.0 + jnp.exp(-rg)))).astype(BF16)
                    new.append(cd * states[hh] + _dot_tn((kf * wk).astype(BF16), vb))
                return tuple(new)

            lax.fori_loop(0, n_chunk, step, tuple(s0))

        run(True)
        run(False)

    qk, v, kc, vc = _pair_specs(t_lat, seq, ctx_len)
    return _hosted_call(
        body, (rq, rk, rv, rk, rv, pg, lg, g_ret), name="ret_fwd", grid=(nex, HEADS // 2),
        in_specs=[qk, qk, v, kc, vc, v, pl.BlockSpec(memory_space=pltpu.SMEM), pl.BlockSpec((1, 256), lambda b, p: (0, p))],
        out_specs=[v, v],
        out_shape=[jax.ShapeDtypeStruct((t_lat, HEADS * RET_DV), BF16), jax.ShapeDtypeStruct((t_lat, HEADS * RET_DV), F32)],
        sem=("parallel", "arbitrary"), rider=rider)


def _ret_bwd_pair(rq, rk, rv, pg, osum, dmix, lg, g_ret, *, t_lat, seq, ctx_len, chunk, rider=None):
    nex = t_lat // seq
    n_chunk = seq // chunk
    t_ctx = nex * ctx_len

    def body(q_ref, k_ref, v_ref, kc_ref, vc_ref, rg_ref, o_ref, dy_ref, lg_ref, g_ref,
             dq_ref, dk_ref, dv_ref, dkc_ref, dvc_ref, drg_ref, st_ref, do_s, s_st):
        pair = pl.program_id(1)
        masks = _lane_masks()
        kcf = kc_ref[...].astype(F32)

        def norm_step(n, dgains):
            sl = pl.ds(pl.multiple_of(n * chunk, chunk), chunk)
            out = []
            for hh in (0, 1):
                cols = slice(hh * 128, (hh + 1) * 128)
                gain = g_ref[:, cols]
                o = o_ref[sl, cols]
                mu = jnp.mean(o, axis=-1, keepdims=True)
                oc = o - mu
                rstd = lax.rsqrt(jnp.mean(oc * oc, axis=-1, keepdims=True) + EPS)
                ohat = oc * rstd
                rg = rg_ref[sl, cols]
                sg = 1.0 / (1.0 + jnp.exp(-rg))
                dy = dy_ref[sl, cols].astype(F32)
                don = dy * (rg * sg)
                drg_ref[sl, cols] = dy * (ohat * gain) * (sg * (1.0 + rg * (1.0 - sg)))
                dohat = don * gain
                do_s[sl, cols] = rstd * (dohat - jnp.mean(dohat, axis=-1, keepdims=True)
                                         - ohat * jnp.mean(dohat * ohat, axis=-1, keepdims=True))
                out.append(dgains[hh] + jnp.sum(don * ohat, axis=0, keepdims=True))
            return tuple(out)

        zero_row = jnp.zeros((1, 128), F32)
        dgains = lax.fori_loop(0, n_chunk, norm_step, (zero_row, zero_row))
        dq_ref[...] = jnp.zeros(dq_ref.shape, F32)
        dk_ref[...] = jnp.zeros(dk_ref.shape, F32)
        dv_ref[...] = jnp.zeros(dv_ref.shape, F32)

        def run(forward):
            terms, ctxw, s0 = [], [], []
            for hh in (0, 1):
                lgd = lg_ref[0 if forward else 1, 2 * pair + hh]
                terms.append(_decay_terms(lgd, chunk, forward))
                ctxw.append(_ctx_weights(lgd, ctx_len, forward))
                s0.append(_dot_tn((jnp.where(masks[hh], kcf, 0.0) * ctxw[hh][0]).astype(BF16),
                                  vc_ref[:, hh * 128:(hh + 1) * 128]))

            def state_step(t, states):
                n = t if forward else n_chunk - 1 - t
                sl = pl.ds(pl.multiple_of(n * chunk, chunk), chunk)
                kf_all = k_ref[sl, :].astype(F32)
                new = []
                for hh in (0, 1):
                    wk, cd = terms[hh][3], terms[hh][6]
                    s_st[hh, n] = states[hh]
                    kf = jnp.where(masks[hh], kf_all, 0.0)
                    new.append(cd * states[hh] + _dot_tn((kf * wk).astype(BF16), v_ref[sl, hh * 128:(hh + 1) * 128]))
                return tuple(new)

            lax.fori_loop(0, n_chunk, state_step, tuple(s0))

            def grad_step(t, carry):
                n = (n_chunk - 1 - t) if forward else t
                sl = pl.ds(pl.multiple_of(n * chunk, chunk), chunk)
                qb = q_ref[sl, :]
                kf_all = k_ref[sl, :].astype(F32)
                dq_sum = jnp.zeros((chunk, 128), F32)
                dk_sum = jnp.zeros((chunk, 128), F32)
                out = []
                for hh in (0, 1):
                    g_next, dlg = carry[hh]
                    dmat, dist, wq, wk, e_q, e_k, cd = terms[hh]
                    cols = slice(hh * 128, (hh + 1) * 128)
                    qm = jnp.where(masks[hh], qb, jnp.zeros((), BF16))
                    kf = jnp.where(masks[hh], kf_all, 0.0)
                    kb = kf.astype(BF16)
                    vb = v_ref[sl, cols]
                    do = do_s[sl, cols]
                    dob = do.astype(BF16)
                    s_n = s_st[hh, n]
                    s_nb = s_n.astype(BF16)
                    gb = g_next.astype(BF16)
                    dk_cross = wk * _dot_nt(vb, gb)
                    dv_cross = _dot((kf * wk).astype(BF16), gb)
                    a = _dot_nt(qm, kb) * dmat
                    da_raw = _dot_nt(dob, vb)
                    dab = (da_raw * dmat).astype(BF16)
                    o_cross = wq * _dot(qm, s_nb)
                    dq_sum = dq_sum + _dot(dab, kb) + wq * _dot_nt(dob, s_nb)
                    dk_sum = dk_sum + _dot_tn(dab, qm) + dk_cross
                    dv_ref[sl, cols] += _dot_tn(a.astype(BF16), dob) + dv_cross
                    dlg = (dlg + chunk * cd * jnp.sum(g_next * s_n, keepdims=True)
                           + jnp.sum(e_k * jnp.sum(kf * dk_cross, axis=-1, keepdims=True), keepdims=True)
                           + jnp.sum(dist * a * da_raw, keepdims=True)
                           + jnp.sum(e_q * jnp.sum(o_cross * do, axis=-1, keepdims=True), keepdims=True))
                    out.append((cd * g_next + _dot_tn((qm.astype(F32) * wq).astype(BF16), dob), dlg))
                dq_ref[sl, :] += dq_sum
                dk_ref[sl, :] += dk_sum
                return tuple(out)

            zero = (jnp.zeros((128, 128), F32), jnp.zeros((1, 1), F32))
            res = lax.fori_loop(0, n_chunk, grad_step, (zero, zero))
            dkc_sum = jnp.zeros((ctx_len, 128), F32)
            dvc, dlgs = [], []
            for hh in (0, 1):
                ds0, dlg = res[hh]
                wc, e_c = ctxw[hh]
                kcm = jnp.where(masks[hh], kcf, 0.0)
                ds0b = ds0.astype(BF16)
                dkc_part = wc * _dot_nt(vc_ref[:, hh * 128:(hh + 1) * 128], ds0b)
                dkc_sum = dkc_sum + dkc_part
                dvc.append(_dot((kcm * wc).astype(BF16), ds0b))
                dlgs.append(dlg + jnp.sum(e_c * jnp.sum(kcm * dkc_part, axis=-1, keepdims=True), keepdims=True))
            return dkc_sum, dvc, dlgs

        dkc_f, dvc_f, dlg_f = run(True)
        dkc_b, dvc_b, dlg_b = run(False)
        dkc_ref[...] = dkc_f + dkc_b
        for hh in (0, 1):
            cols = slice(hh * 128, (hh + 1) * 128)
            dvc_ref[:, cols] = dvc_f[hh] + dvc_b[hh]
            st_ref[0, :, cols] = jnp.concatenate([
                dgains[hh], jnp.broadcast_to(dlg_f[hh], (1, 128)), jnp.broadcast_to(dlg_b[hh], (1, 128)),
                jnp.zeros((5, 128), F32)], axis=0)

    qk, v, kc, vc = _pair_specs(t_lat, seq, ctx_len)
    return _hosted_call(
        body, (rq, rk, rv, rk, rv, pg, osum, dmix, lg, g_ret), name="ret_bwd", grid=(nex, HEADS // 2),
        in_specs=[qk, qk, v, kc, vc, v, v, v, pl.BlockSpec(memory_space=pltpu.SMEM),
                  pl.BlockSpec((1, 256), lambda b, p: (0, p))],
        out_specs=[
            qk, qk, v,
            pl.BlockSpec((ctx_len, 128), lambda b, p: (b, p)),
            pl.BlockSpec((ctx_len, 256), lambda b, p: (b, p)),
            v,
            pl.BlockSpec((1, 8, 256), lambda b, p: (b, 0, p)),
        ],
        out_shape=[
            jax.ShapeDtypeStruct((t_lat, 256), F32), jax.ShapeDtypeStruct((t_lat, 256), F32),
            jax.ShapeDtypeStruct((t_lat, 512), F32), jax.ShapeDtypeStruct((t_ctx, 256), F32),
            jax.ShapeDtypeStruct((t_ctx, 512), F32), jax.ShapeDtypeStruct((t_lat, 512), F32),
            jax.ShapeDtypeStruct((nex, 8, 512), F32),
        ],
        scratch_shapes=[pltpu.VMEM((seq, 256), F32), pltpu.VMEM((2, n_chunk, 128, 128), F32)],
        sem=("parallel", "arbitrary"), rider=rider)


def _ret_specs(t_lat, seq, ctx_len):
    cb = t_lat // ctx_len
    qk = pl.BlockSpec((seq, 128), lambda b, h: (b, h // 2))
    v = pl.BlockSpec((seq, 128), lambda b, h: (b, h))
    kc = pl.BlockSpec((ctx_len, 128), lambda b, h: (cb + b, h // 2))
    vc = pl.BlockSpec((ctx_len, 128), lambda b, h: (cb + b, h))
    return qk, v, kc, vc


def _head_mask(h):
    lane = lax.broadcasted_iota(jnp.int32, (1, 128), 1)
    return (lane // RET_DK) == (h % 2)


def _ret_fwd(rq, rk, rv, pg, lg, g_ret, *, t_lat, seq, ctx_len, chunk, rider=None):
    nex = t_lat // seq
    n_chunk = seq // chunk

    def body(q_ref, k_ref, v_ref, kc_ref, vc_ref, rg_ref, lg_ref, g_ref, y_ref, o_ref):
        h = pl.program_id(1)
        hm = _head_mask(h)
        gain = g_ref[...]
        kcm = jnp.where(hm, kc_ref[...].astype(F32), 0.0)
        vcb = vc_ref[...]

        def run(forward):
            lgd = lg_ref[0 if forward else 1, h]
            dmat, _, wq, wk, _, _, cd = _decay_terms(lgd, chunk, forward)
            wc, _ = _ctx_weights(lgd, ctx_len, forward)
            s0 = _dot_tn((kcm * wc).astype(BF16), vcb)

            def step(t, s):
                n = t if forward else n_chunk - 1 - t
                sl = pl.ds(pl.multiple_of(n * chunk, chunk), chunk)
                qm = jnp.where(hm, q_ref[sl, :], jnp.zeros((), BF16))
                kf = jnp.where(hm, k_ref[sl, :].astype(F32), 0.0)
                vb = v_ref[sl, :]
                a = _dot_nt(qm, kf.astype(BF16)) * dmat
                o = _dot(a.astype(BF16), vb) + wq * _dot(qm, s.astype(BF16))
                if forward:
                    o_ref[sl, :] = o
                else:
                    o = o_ref[sl, :] + o
                    o_ref[sl, :] = o
                    mu = jnp.mean(o, axis=-1, keepdims=True)
                    oc = o - mu
                    var = jnp.mean(oc * oc, axis=-1, keepdims=True)
                    on = oc * lax.rsqrt(var + EPS) * gain
                    rg = rg_ref[sl, :]
                    y_ref[sl, :] = (on * (rg / (1.0 + jnp.exp(-rg)))).astype(BF16)
                return cd * s + _dot_tn((kf * wk).astype(BF16), vb)

            lax.fori_loop(0, n_chunk, step, s0)

        run(True)
        run(False)

    qk, v, kc, vc = _ret_specs(t_lat, seq, ctx_len)
    return _hosted_call(
        body, (rq, rk, rv, rk, rv, pg, lg, g_ret), name="ret_fwd", grid=(nex, HEADS),
        in_specs=[qk, qk, v, kc, vc, v, pl.BlockSpec(memory_space=pltpu.SMEM), pl.BlockSpec((1, 128), lambda b, h: (0, h))],
        out_specs=[v, v],
        out_shape=[jax.ShapeDtypeStruct((t_lat, HEADS * RET_DV), BF16), jax.ShapeDtypeStruct((t_lat, HEADS * RET_DV), F32)],
        sem=("parallel", "arbitrary"), rider=rider)


def _ret_bwd(rq, rk, rv, pg, osum, dmix, lg, g_ret, *, t_lat, seq, ctx_len, chunk, rider=None):
    nex = t_lat // seq
    n_chunk = seq // chunk
    t_ctx = nex * ctx_len

    def body(q_ref, k_ref, v_ref, kc_ref, vc_ref, rg_ref, o_ref, dy_ref, lg_ref, g_ref,
             dq_ref, dk_ref, dv_ref, dkc_ref, dvc_ref, drg_ref, st_ref, do_s, s_st):
        h = pl.program_id(1)
        hm = _head_mask(h)
        gain = g_ref[...]
        kcm = jnp.where(hm, kc_ref[...].astype(F32), 0.0)
        vcb = vc_ref[...]

        def norm_step(n, dgain):
            sl = pl.ds(pl.multiple_of(n * chunk, chunk), chunk)
            o = o_ref[sl, :]
            mu = jnp.mean(o, axis=-1, keepdims=True)
            oc = o - mu
            rstd = lax.rsqrt(jnp.mean(oc * oc, axis=-1, keepdims=True) + EPS)
            ohat = oc * rstd
            rg = rg_ref[sl, :]
            sg = 1.0 / (1.0 + jnp.exp(-rg))
            dy = dy_ref[sl, :].astype(F32)
            don = dy * (rg * sg)
            drg_ref[sl, :] = dy * (ohat * gain) * (sg * (1.0 + rg * (1.0 - sg)))
            dohat = don * gain
            do_s[sl, :] = rstd * (dohat - jnp.mean(dohat, axis=-1, keepdims=True)
                                  - ohat * jnp.mean(dohat * ohat, axis=-1, keepdims=True))
            return dgain + jnp.sum(don * ohat, axis=0, keepdims=True)

        dgain = lax.fori_loop(0, n_chunk, norm_step, jnp.zeros((1, 128), F32))

        @pl.when(h % 2 == 0)
        def _():
            dq_ref[...] = jnp.zeros(dq_ref.shape, F32)
            dk_ref[...] = jnp.zeros(dk_ref.shape, F32)
            dkc_ref[...] = jnp.zeros(dkc_ref.shape, F32)

        dv_ref[...] = jnp.zeros(dv_ref.shape, F32)

        def run(forward):
            lgd = lg_ref[0 if forward else 1, h]
            dmat, dist, wq, wk, e_q, e_k, cd = _decay_terms(lgd, chunk, forward)
            wc, e_c = _ctx_weights(lgd, ctx_len, forward)
            s0 = _dot_tn((kcm * wc).astype(BF16), vcb)

            def state_step(t, s):
                n = t if forward else n_chunk - 1 - t
                sl = pl.ds(pl.multiple_of(n * chunk, chunk), chunk)
                s_st[n] = s
                kf = jnp.where(hm, k_ref[sl, :].astype(F32), 0.0)
                return cd * s + _dot_tn((kf * wk).astype(BF16), v_ref[sl, :])

            lax.fori_loop(0, n_chunk, state_step, s0)

            def grad_step(t, carry):
                g_next, dlg = carry
                n = (n_chunk - 1 - t) if forward else t
                sl = pl.ds(pl.multiple_of(n * chunk, chunk), chunk)
                qm = jnp.where(hm, q_ref[sl, :], jnp.zeros((), BF16))
                kf = jnp.where(hm, k_ref[sl, :].astype(F32), 0.0)
                kb = kf.astype(BF16)
                vb = v_ref[sl, :]
                do = do_s[sl, :]
                dob = do.astype(BF16)
                s_n = s_st[n]
                s_nb = s_n.astype(BF16)
                gb = g_next.astype(BF16)
                dk_cross = wk * _dot_nt(vb, gb)
                dv_cross = _dot((kf * wk).astype(BF16), gb)
                a = _dot_nt(qm, kb) * dmat
                da_raw = _dot_nt(dob, vb)
                dab = (da_raw * dmat).astype(BF16)
                ab = a.astype(BF16)
                o_cross = wq * _dot(qm, s_nb)
                dq_ref[sl, :] += _dot(dab, kb) + wq * _dot_nt(dob, s_nb)
                dk_ref[sl, :] += _dot_tn(dab, qm) + dk_cross
                dv_ref[sl, :] += _dot_tn(ab, dob) + dv_cross
                dlg = (dlg + chunk * cd * jnp.sum(g_next * s_n, keepdims=True)
                       + jnp.sum(e_k * jnp.sum(kf * dk_cross, axis=-1, keepdims=True), keepdims=True)
                       + jnp.sum(dist * a * da_raw, keepdims=True)
                       + jnp.sum(e_q * jnp.sum(o_cross * do, axis=-1, keepdims=True), keepdims=True))
                g_new = cd * g_next + _dot_tn((qm.astype(F32) * wq).astype(BF16), dob)
                return g_new, dlg

            ds0, dlg = lax.fori_loop(0, n_chunk, grad_step, (jnp.zeros((128, 128), F32), jnp.zeros((1, 1), F32)))
            ds0b = ds0.astype(BF16)
            dkc_part = wc * _dot_nt(vcb, ds0b)
            dkc_ref[...] += dkc_part
            dvc_part = _dot((kcm * wc).astype(BF16), ds0b)
            dlg = dlg + jnp.sum(e_c * jnp.sum(kcm * dkc_part, axis=-1, keepdims=True), keepdims=True)
            return dvc_part, dlg

        dvc_f, dlg_f = run(True)
        dvc_b, dlg_b = run(False)
        dvc_ref[...] = dvc_f + dvc_b
        st_ref[0] = jnp.concatenate([
            dgain, jnp.broadcast_to(dlg_f, (1, 128)), jnp.broadcast_to(dlg_b, (1, 128)), jnp.zeros((5, 128), F32)], axis=0)

    qk, v, kc, vc = _ret_specs(t_lat, seq, ctx_len)
    dy_spec = v
    return _hosted_call(
        body, (rq, rk, rv, rk, rv, pg, osum, dmix, lg, g_ret), name="ret_bwd", grid=(nex, HEADS),
        in_specs=[qk, qk, v, kc, vc, v, v, dy_spec, pl.BlockSpec(memory_space=pltpu.SMEM),
                  pl.BlockSpec((1, 128), lambda b, h: (0, h))],
        out_specs=[
            qk, qk, v,
            pl.BlockSpec((ctx_len, 128), lambda b, h: (b, h // 2)),
            pl.BlockSpec((ctx_len, 128), lambda b, h: (b, h)),
            v,
            pl.BlockSpec((1, 8, 128), lambda b, h: (b, 0, h)),
        ],
        out_shape=[
            jax.ShapeDtypeStruct((t_lat, 256), F32), jax.ShapeDtypeStruct((t_lat, 256), F32),
            jax.ShapeDtypeStruct((t_lat, 512), F32), jax.ShapeDtypeStruct((t_ctx, 256), F32),
            jax.ShapeDtypeStruct((t_ctx, 512), F32), jax.ShapeDtypeStruct((t_lat, 512), F32),
            jax.ShapeDtypeStruct((nex, 8, 512), F32),
        ],
        scratch_shapes=[pltpu.VMEM((seq, 128), F32), pltpu.VMEM((n_chunk, 128, 128), F32)],
        sem=("parallel", "arbitrary"), rider=rider)


def _matmul_tn(a, b, *, bm, bn, bk, chip_major, name):
    tk, m = a.shape
    n = b.shape[1]
    slab = n // N_CHIPS
    per_block = bn // slab if chip_major else 1
    bk = max(c for c in range(LANES, min(bk, tk) + 1, LANES) if tk % c == 0)

    def body(a_ref, b_ref, o_ref):
        k = pl.program_id(2)
        if chip_major:
            parts = [_dot_tn(a_ref[...], b_ref[:, s * slab:(s + 1) * slab]) for s in range(per_block)]
        else:
            parts = [_dot_tn(a_ref[...], b_ref[...])]

        @pl.when(k == 0)
        def _():
            for s, part in enumerate(parts):
                if chip_major:
                    o_ref[s] = part
                else:
                    o_ref[...] = part

        @pl.when(k > 0)
        def _():
            for s, part in enumerate(parts):
                if chip_major:
                    o_ref[s] += part
                else:
                    o_ref[...] += part

    if chip_major:
        out_spec = pl.BlockSpec((per_block, bm, slab), lambda i, j, k: (j, i, 0))
        out_shape = jax.ShapeDtypeStruct((N_CHIPS, m, slab), F32)
    else:
        out_spec = pl.BlockSpec((bm, bn), lambda i, j, k: (i, j))
        out_shape = jax.ShapeDtypeStruct((m, n), F32)
    return pl.pallas_call(
        body, name=name, grid=(m // bm, n // bn, tk // bk),
        in_specs=[pl.BlockSpec((bk, bm), lambda i, j, k: (k, i)), pl.BlockSpec((bk, bn), lambda i, j, k: (k, j))],
        out_specs=out_spec, out_shape=out_shape,
        compiler_params=_params(("parallel", "parallel", "arbitrary"), VMEM_LIMIT),
    )(a, b)


_LATE = ("w_out", "w_ff1", "w_ff2")
_EARLY = ("w_in", "w_uq", "w_ukv")


def _local_step(x, ctx, tgt, modv, lg, g_attn, g_ffn, g_fin, g_ret, g_q, g_kv, w_in, w_uq, w_ukv, late, place=None,
                *, tm=256, tq=256, chunk=256):
    nex, seq, d = x.shape
    ctx_len = ctx.shape[1]
    t_lat = nex * seq
    x2 = x.reshape(t_lat, d)
    ctx2 = ctx.reshape(nex * ctx_len, d)
    tgt2 = tgt.reshape(t_lat, d)
    cos_t, sin_t = _rope_tables(seq, tm)
    dims = dict(t_lat=t_lat, seq=seq, ctx_len=ctx_len)
    alone = place is None

    (hb, pg, rq, rk, rv, nq, nkv, q, k, v), crossed2 = _pre_fwd(
        x2, ctx2, modv, g_attn, w_in, g_q, g_kv, w_uq, w_ukv, cos_t, sin_t, seq=seq, tm=tm,
        rider=None if alone else _gather_ici_rider(list(late[2:])))
    (yret, osum), crossed = _ret_fwd_pair(rq, rk, rv, pg, lg, g_ret, chunk=chunk, **dims,
                                          rider=None if alone else _gather_ici_rider(list(late[:2])))
    (ymla, lse), gathered = _mla_fwd(q, k, v, tq=tq, **dims,
                                     rider=None if alone else _gather_d2d_rider(crossed + crossed2))
    w_out, w_ff1, w_ff2 = late if alone else gathered
    mix, act, du, h2, df, dmo, dmix, dxm, st_post = _post(yret, ymla, x2, tgt2, modv, g_ffn, g_fin, w_out.reshape(d, d),
                                                         w_ff1, w_ff2.reshape(D_FF, d), seq=seq, tm=tm)
    bk = 512
    g_late = [
        _matmul_tn(mix, dmo, bm=1024, bn=1024, bk=1024, chip_major=False, name="gw_out").reshape(N_CHIPS, d // N_CHIPS, d),
        _matmul_tn(h2, du, bm=1024, bn=1024, bk=1024, chip_major=True, name="gw_ff1"),
        _matmul_tn(act, df, bm=1024, bn=1024, bk=1024, chip_major=False, name="gw_ff2").reshape(N_CHIPS, D_FF // N_CHIPS, d),
    ]
    (dq_m, dkl, dkc, dvl, dvc), got = _mla_bwd(q, k, v, ymla, lse, dmix, tq=tq, **dims,
                                               rider=None if alone else _exchange_rider(g_late))
    if not alone:
        core, slot = place
        part = [_add_half(g, r, core, "add_half_" + n) for g, r, n in zip(g_late, got, _LATE)]
    (drq, drk, drv, dkc_r, dvc_r, drg, st_ret), landed = _ret_bwd_pair(
        rq, rk, rv, pg, osum, dmix, lg, g_ret, chunk=chunk, **dims, rider=None if alone else _scatter_rider(part))
    if not alone:
        mine = [_sum_chips(p, l, slot, "sum_chips_" + n) for p, l, n in zip(part, landed, _LATE)]
    (dpb, dqf, dkvf, gx, st_pre), _ = _pre_bwd(
        x2, ctx2, modv, g_attn, pg, drq, drk, dkc_r, drv, dvc_r, drg, dq_m, dkl, dkc, dvl, dvc, dxm, w_in, g_q, g_kv,
        w_uq, w_ukv, cos_t, sin_t, seq=seq, tm=tm)
    g_early = [
        _matmul_tn(dpb, hb, bm=IN_PAD // 2, bn=d, bk=512, chip_major=False, name="gw_in"),
        _matmul_tn(dqf, nq, bm=HEADS * MLA_HEAD, bn=Q_LORA, bk=1536, chip_major=False, name="gw_uq"),
        _matmul_tn(nkv, dkvf, bm=KV_LORA, bn=HEADS * 256, bk=1536, chip_major=True, name="gw_ukv"),
    ]
    late_out = g_late if alone else mine
    return gx.reshape(nex, seq, d), g_early, late_out, st_post, st_ret, st_pre


_ANY = pl.BlockSpec(memory_space=pl.ANY)
_VMEM = pl.BlockSpec(memory_space=pltpu.VMEM)
_OFFSETS = tuple((dx, dy, dc) for dx in (0, 1) for dy in (0, 1) for dc in (0, 1))[1:]
_CHIP_OFFSETS = ((1, 0), (0, 1), (1, 1))


def _place():
    return lax.axis_index("x"), lax.axis_index("y"), lax.axis_index("c")


def _flip(v, d):
    return 1 - v if d else v


def _gather8_rider(a):
    def copies(a_ref, o_ref, send, recv):
        x, y, z = _place()
        me = 4 * x + 2 * y + z
        out = []
        for k, (dx, dy, dc) in enumerate(_OFFSETS):
            peer = (_flip(x, dx), _flip(y, dy), _flip(z, dc))
            landing = o_ref.at[4 * peer[0] + 2 * peer[1] + peer[2]]
            out.append((
                pltpu.make_async_remote_copy(src_ref=a_ref, dst_ref=o_ref.at[me], send_sem=send.at[k],
                                             recv_sem=recv.at[k], device_id=peer, device_id_type=MESH),
                pltpu.make_async_remote_copy(src_ref=a_ref, dst_ref=landing, send_sem=send.at[k],
                                             recv_sem=recv.at[k], device_id=peer, device_id_type=MESH)))
        return me, out

    def start(ins, outs, sems):
        me, cps = copies(ins[0], outs[0], sems[0], sems[1])
        pltpu.make_async_copy(ins[0], outs[0].at[me], sems[2]).start()
        for out_cp, _ in cps:
            out_cp.start()

    def finish(ins, outs, sems):
        me, cps = copies(ins[0], outs[0], sems[0], sems[1])
        for out_cp, in_cp in cps:
            in_cp.wait_recv()
            out_cp.wait_send()
        pltpu.make_async_copy(ins[0], outs[0].at[me], sems[2]).wait()

    return _Rider([a], [jax.ShapeDtypeStruct((N_DEV,) + a.shape, a.dtype)],
                  [pltpu.SemaphoreType.DMA((7,)), pltpu.SemaphoreType.DMA((7,)), pltpu.SemaphoreType.DMA],
                  start, finish, in_specs=[_VMEM], out_specs=[_VMEM])


def _merge_riders(*riders):
    ins, outs, sems, in_specs, out_specs, aliases, cuts = [], [], [], [], [], {}, []
    for r in riders:
        cuts.append((len(ins), len(outs), len(sems)))
        aliases.update({len(ins) + i: len(outs) + j for i, j in r.aliases.items()})
        ins += r.ins
        outs += r.out_shapes
        sems += r.sems
        in_specs += r.in_specs
        out_specs += r.out_specs

    def part(r, cut, r_ins, r_outs, r_sems):
        return (r_ins[cut[0]:cut[0] + len(r.ins)], r_outs[cut[1]:cut[1] + len(r.out_shapes)],
                r_sems[cut[2]:cut[2] + len(r.sems)])

    def start(r_ins, r_outs, r_sems):
        for r, cut in zip(riders, cuts):
            r.start(*part(r, cut, r_ins, r_outs, r_sems))

    def finish(r_ins, r_outs, r_sems):
        for r, cut in zip(riders, cuts):
            r.finish(*part(r, cut, r_ins, r_outs, r_sems))

    return _Rider(ins, outs, sems, start, finish, aliases=aliases, in_specs=in_specs, out_specs=out_specs)


def _allgather8(a, name):
    return _run_rider(_gather8_rider(a), name)[0]


BF16_TILE_ROWS = 16


def _half(o, slot, which):
    r2 = o.shape[1] // 2
    if r2 % BF16_TILE_ROWS == 0:
        return o.at[slot, pl.ds(which * r2, r2)]
    c2 = o.shape[2] // 2
    assert c2 % LANES == 0
    return o.at[slot, :, pl.ds(which * c2, c2)]


def _gather_send(o_refs, send, recv):
    x, y, z = _place()
    chip = 2 * x + y
    for a, o in enumerate(o_refs):
        r2 = o.shape[1] // 2
        mine = _half(o, chip, z)
        for k, (dx, dy) in enumerate(_CHIP_OFFSETS):
            pltpu.make_async_remote_copy(
                src_ref=mine, dst_ref=mine, send_sem=send.at[a, k], recv_sem=recv.at[a, k],
                device_id=(_flip(x, dx), _flip(y, dy), z), device_id_type=MESH).start()


def _gather_landed(o_refs, send, recv, then=None):
    x, y, z = _place()
    chip = 2 * x + y
    for a, o in enumerate(o_refs):
        for k, (dx, dy) in enumerate(_CHIP_OFFSETS):
            landed = _half(o, 2 * _flip(x, dx) + _flip(y, dy), z)
            pltpu.make_async_remote_copy(
                src_ref=landed, dst_ref=landed, send_sem=send.at[a, k], recv_sem=recv.at[a, k],
                device_id=(_flip(x, dx), _flip(y, dy), z), device_id_type=MESH).wait_recv()
            if then is not None:
                then(a, k, landed)
    for a, o in enumerate(o_refs):
        mine = _half(o, chip, z)
        for k, (dx, dy) in enumerate(_CHIP_OFFSETS):
            pltpu.make_async_remote_copy(
                src_ref=mine, dst_ref=mine, send_sem=send.at[a, k], recv_sem=recv.at[a, k],
                device_id=(_flip(x, dx), _flip(y, dy), z), device_id_type=MESH).wait_send()


def _pass_on(o_refs, fsend, frecv, a, k, landed):
    x, y, z = _place()
    pltpu.make_async_remote_copy(
        src_ref=landed, dst_ref=landed, send_sem=fsend.at[a, k], recv_sem=frecv.at[a, k],
        device_id=(x, y, 1 - z), device_id_type=MESH).start()


def _passed_on(o_refs, fsend, frecv):
    x, y, z = _place()
    for a, o in enumerate(o_refs):
        for k, (dx, dy) in enumerate(_CHIP_OFFSETS):
            other = 2 * _flip(x, dx) + _flip(y, dy)
            got = _half(o, other, 1 - z)
            gave = _half(o, other, z)
            pltpu.make_async_remote_copy(
                src_ref=got, dst_ref=got, send_sem=fsend.at[a, k], recv_sem=frecv.at[a, k],
                device_id=(x, y, 1 - z), device_id_type=MESH).wait_recv()
            pltpu.make_async_remote_copy(
                src_ref=gave, dst_ref=gave, send_sem=fsend.at[a, k], recv_sem=frecv.at[a, k],
                device_id=(x, y, 1 - z), device_id_type=MESH).wait_send()


def _gather_finish(o_refs, send, recv, fsend, frecv):
    _gather_landed(o_refs, send, recv, functools.partial(_pass_on, o_refs, fsend, frecv))
    _passed_on(o_refs, fsend, frecv)


class _Rider:
    def __init__(self, ins, out_shapes, sems, start, finish, aliases=None, in_specs=None, out_specs=None):
        self.ins, self.out_shapes, self.sems = list(ins), list(out_shapes), list(sems)
        self.start, self.finish, self.aliases = start, finish, dict(aliases or {})
        self.in_specs = list(in_specs) if in_specs else [_ANY] * len(self.ins)
        self.out_specs = list(out_specs) if out_specs else [_ANY] * len(self.out_shapes)


def _run_rider(rider, name):
    r_in, r_out = len(rider.ins), len(rider.out_shapes)

    def body(*refs):
        ins, outs, sems = refs[:r_in], refs[r_in:r_in + r_out], refs[r_in + r_out:]
        rider.start(ins, outs, sems)
        rider.finish(ins, outs, sems)

    return pl.pallas_call(
        body, name=name, in_specs=rider.in_specs, out_specs=rider.out_specs, out_shape=rider.out_shapes,
        input_output_aliases=rider.aliases, scratch_shapes=rider.sems,
    )(*rider.ins)


def _hosted_call(body, args, *, name, grid, in_specs, out_specs, out_shape, scratch_shapes=(), sem, rider=None):
    scratch_shapes = list(scratch_shapes)
    if rider is None:
        res = pl.pallas_call(
            body, name=name, grid=grid, in_specs=in_specs, out_specs=out_specs, out_shape=out_shape,
            scratch_shapes=scratch_shapes, compiler_params=_params(sem, VMEM_LIMIT))(*args)
        return list(res), []
    n_in, n_out, n_sc = len(in_specs), len(out_specs), len(scratch_shapes)
    r_in, r_out = len(rider.ins), len(rider.out_shapes)
    last = tuple(g - 1 for g in grid)

    def hosted(*refs):
        p = 0
        parts = []
        for cnt in (n_in, r_in, n_out, r_out, n_sc):
            parts.append(refs[p:p + cnt])
            p += cnt
        ins, r_ins, outs, r_outs, scratch = parts
        sems = refs[p:]
        ids = [pl.program_id(a) for a in range(len(grid))]
        is_first = functools.reduce(jnp.logical_and, [i == 0 for i in ids])
        is_last = functools.reduce(jnp.logical_and, [i == e for i, e in zip(ids, last)])

        @pl.when(is_first)
        def _():
            rider.start(r_ins, r_outs, sems)

        body(*ins, *outs, *scratch)

        @pl.when(is_last)
        def _():
            rider.finish(r_ins, r_outs, sems)

    res = pl.pallas_call(
        hosted, name=name, grid=grid, in_specs=list(in_specs) + rider.in_specs, out_specs=list(out_specs) + rider.out_specs,
        out_shape=list(out_shape) + rider.out_shapes, scratch_shapes=scratch_shapes + rider.sems,
        input_output_aliases={n_in + i: n_out + j for i, j in rider.aliases.items()},
        compiler_params=_params(("arbitrary",) * len(grid), VMEM_LIMIT))(*args, *rider.ins)
    return list(res[:n_out]), list(res[n_out:])


def _gather_rider(ws):
    n = len(ws)
    return _Rider(
        ws, [jax.ShapeDtypeStruct(w.shape, w.dtype) for w in ws], [pltpu.SemaphoreType.DMA((n, 3))] * 4,
        lambda ins, outs, sems: _gather_send(outs, sems[0], sems[1]),
        lambda ins, outs, sems: _gather_finish(outs, *sems),
        aliases={a: a for a in range(n)})


def _gather_ici_rider(ws):
    n = len(ws)
    return _Rider(
        ws, [jax.ShapeDtypeStruct(w.shape, w.dtype) for w in ws], [pltpu.SemaphoreType.DMA((n, 3))] * 2,
        lambda ins, outs, sems: _gather_send(outs, sems[0], sems[1]),
        lambda ins, outs, sems: _gather_landed(outs, sems[0], sems[1]),
        aliases={a: a for a in range(n)})


def _gather_d2d_rider(ws):
    n = len(ws)

    def start(ins, outs, sems):
        x, y, z = _place()
        for a, o in enumerate(outs):
            for k, (dx, dy) in enumerate(_CHIP_OFFSETS):
                _pass_on(outs, sems[0], sems[1], a, k, _half(o, 2 * _flip(x, dx) + _flip(y, dy), z))

    return _Rider(
        ws, [jax.ShapeDtypeStruct(w.shape, w.dtype) for w in ws], [pltpu.SemaphoreType.DMA((n, 3))] * 2,
        start, lambda ins, outs, sems: _passed_on(outs, sems[0], sems[1]), aliases={a: a for a in range(n)})


def _copies_rider(ins, out_shapes, sem_shape, make):
    def start(r_ins, r_outs, sems):
        for cp in make(r_ins, r_outs, sems[0], sems[1]):
            cp.start()

    def finish(r_ins, r_outs, sems):
        for cp in make(r_ins, r_outs, sems[0], sems[1]):
            cp.wait()

    return _Rider(ins, out_shapes, [pltpu.SemaphoreType.DMA(sem_shape)] * 2, start, finish)


def _exchange_rider(gs):
    def make(g_refs, r_refs, send, recv):
        x, y, z = _place()
        return [pltpu.make_async_remote_copy(
            src_ref=g.at[:, pl.ds((1 - z) * (g.shape[1] // 2), g.shape[1] // 2)], dst_ref=r, send_sem=send.at[a],
            recv_sem=recv.at[a], device_id=(x, y, 1 - z), device_id_type=MESH)
            for a, (g, r) in enumerate(zip(g_refs, r_refs))]

    shapes = [jax.ShapeDtypeStruct((g.shape[0], g.shape[1] // 2, g.shape[2]), g.dtype) for g in gs]
    return _copies_rider(gs, shapes, (len(gs),), make)


def _add_half(g, recv, core, name):
    s, r, c = g.shape
    r2 = r // 2
    rb = r2
    for cand in (256, 128, 64):
        if r2 % cand == 0:
            rb = cand
            break
    g4 = g.reshape(s, 2, r2, c)

    def body(core_ref, g_ref, r_ref, o_ref):
        o_ref[...] = (g_ref[...] + r_ref[...]).astype(BF16)

    return pl.pallas_call(
        body, name=name,
        grid_spec=pltpu.PrefetchScalarGridSpec(
            num_scalar_prefetch=1, grid=(s, r2 // rb),
            in_specs=[pl.BlockSpec((None, None, rb, c), lambda i, j, cr: (i, cr[0], j, 0)),
                      pl.BlockSpec((None, rb, c), lambda i, j, cr: (i, j, 0))],
            out_specs=pl.BlockSpec((None, rb, c), lambda i, j, cr: (i, j, 0))),
        out_shape=jax.ShapeDtypeStruct((s, r2, c), BF16),
        compiler_params=_params(("parallel", "parallel")),
    )(core, g4, recv)


def _scatter_rider(ps):
    def make(p_refs, o_refs, send, recv):
        x, y, z = _place()
        copies = []
        for a, (p, o) in enumerate(zip(p_refs, o_refs)):
            for k, (dx, dy) in enumerate(_CHIP_OFFSETS):
                other = 2 * _flip(x, dx) + _flip(y, dy)
                copies.append(pltpu.make_async_remote_copy(
                    src_ref=p.at[other], dst_ref=o.at[k], send_sem=send.at[a, k], recv_sem=recv.at[a, k],
                    device_id=(_flip(x, dx), _flip(y, dy), z), device_id_type=MESH))
        return copies

    shapes = [jax.ShapeDtypeStruct((3,) + p.shape[1:], p.dtype) for p in ps]
    return _copies_rider(ps, shapes, (len(ps), 3), make)


def _sum_chips(p, landed, chip, name):
    _, r2, c = p.shape
    rb = r2
    for cand in (256, 128, 64):
        if r2 % cand == 0:
            rb = cand
            break

    def body(s_ref, p_ref, l_ref, o_ref):
        acc = p_ref[...].astype(F32)
        for k in range(3):
            acc = acc + l_ref[k].astype(F32)
        o_ref[...] = acc

    return pl.pallas_call(
        body, name=name,
        grid_spec=pltpu.PrefetchScalarGridSpec(
            num_scalar_prefetch=1, grid=(r2 // rb,),
            in_specs=[pl.BlockSpec((None, rb, c), lambda i, s: (s[0], i, 0)),
                      pl.BlockSpec((3, rb, c), lambda i, s: (0, i, 0))],
            out_specs=pl.BlockSpec((rb, c), lambda i, s: (i, 0))),
        out_shape=jax.ShapeDtypeStruct((r2, c), F32),
        compiler_params=_params(("parallel",)),
    )(chip, p, landed)


def _swap_rider(hs):
    def make(h_refs, o_refs, send, recv):
        x, y, z = _place()
        return [pltpu.make_async_remote_copy(
            src_ref=h, dst_ref=o, send_sem=send.at[a], recv_sem=recv.at[a], device_id=(x, y, 1 - z),
            device_id_type=MESH) for a, (h, o) in enumerate(zip(h_refs, o_refs))]

    return _copies_rider(hs, [jax.ShapeDtypeStruct(h.shape, h.dtype) for h in hs], (len(hs),), make)


SMALL_ROWS = 32
PACK_ROWS = 16


def _pack_small(st_post, st_ret, st_pre):
    d = st_post.shape[2]

    def body(po_ref, re_ref, pr_ref, o_ref):
        o_ref[...] = jnp.zeros(o_ref.shape, F32)
        o_ref[0:1, :] = pr_ref[0, 2:3, :] + pr_ref[1, 2:3, :] + pr_ref[2, 2:3, :]
        o_ref[1:2, :] = po_ref[0, 4:5, :] + po_ref[1, 4:5, :]
        o_ref[2:3, :] = po_ref[0, 5:6, :] + po_ref[1, 5:6, :]
        o_ref[3:4, 0:512] = re_ref[0, 0:1, :] + re_ref[1, 0:1, :]
        o_ref[4:5, :] = pr_ref[0, 3:4, :] + pr_ref[1, 3:4, :] + pr_ref[2, 3:4, :]
        o_ref[5:6, :] = pr_ref[0, 4:5, :] + pr_ref[1, 4:5, :] + pr_ref[2, 4:5, :]
        lane = lax.broadcasted_iota(jnp.int32, (1, LANES), 1)
        for row, src in ((6, 1), (10, 2)):
            acc = jnp.zeros((1, LANES), F32)
            for hd in range(HEADS):
                grp = re_ref[0, src:src + 1, hd * LANES:(hd + 1) * LANES] + re_ref[1, src:src + 1, hd * LANES:(hd + 1) * LANES]
                acc = acc + jnp.where(lane == hd, grp, 0.0)
            o_ref[row:row + 1, 0:LANES] = acc
        o_ref[7:8, :] = po_ref[0, 6:7, :] + po_ref[1, 6:7, :]
        o_ref[8:9, :] = pr_ref[2, 0:1, :]
        o_ref[9:10, :] = pr_ref[2, 1:2, :]
        for e in range(2):
            b = 12 + 6 * e
            o_ref[b:b + 1, :] = pr_ref[e, 0:1, :]
            o_ref[b + 1:b + 2, :] = pr_ref[e, 1:2, :]
            o_ref[b + 2:b + 3, :] = po_ref[e, 3:4, :]
            o_ref[b + 3:b + 4, :] = po_ref[e, 0:1, :]
            o_ref[b + 4:b + 5, :] = po_ref[e, 1:2, :]
            o_ref[b + 5:b + 6, :] = po_ref[e, 2:3, :]

    return pl.pallas_call(body, name="pack_small", out_shape=jax.ShapeDtypeStruct((SMALL_ROWS, d), F32))(st_post, st_ret, st_pre)


def _small_reduce(gathered):
    d = gathered.shape[2]

    def body(g_ref, o_ref):
        tot = g_ref[0, 0:PACK_ROWS, :]
        for dev in range(1, N_DEV):
            tot = tot + g_ref[dev, 0:PACK_ROWS, :]
        o_ref[0:PACK_ROWS, :] = tot
        for j in range(6):
            acc = g_ref[0, 12 + j:13 + j, :] + g_ref[0, 18 + j:19 + j, :]
            for dev in range(1, N_DEV):
                acc = acc + g_ref[dev, 12 + j:13 + j, :] + g_ref[dev, 18 + j:19 + j, :]
            if j < 2:
                acc = acc + o_ref[8 + j:9 + j, :]
            o_ref[PACK_ROWS + j:PACK_ROWS + j + 1, :] = acc
        o_ref[PACK_ROWS + 6:PACK_ROWS + 8, :] = jnp.zeros((2, d), F32)

    return pl.pallas_call(body, name="small_reduce", out_shape=jax.ShapeDtypeStruct((PACK_ROWS + 8, d), F32))(gathered)


_SMALL = (("g_attn", 0, 1024), ("g_ffn", 1, 1024), ("g_final", 2, 1024), ("g_ret", 3, 512), ("g_q_lora", 4, 384),
          ("g_kv_lora", 5, 256), ("ret_decay_fwd", 6, HEADS), ("ret_decay_bwd", 10, HEADS))
_SMALL_NAMES = tuple(s[0] for s in _SMALL) + ("c_ctx", "b_ada")


def _small_final(tot, dcc, sg8, ws, ms, vs):
    d = tot.shape[1]
    n = len(_SMALL_NAMES)

    def body(*refs):
        t_ref, dcc_ref, sg_ref = refs[0:3]
        w_refs, m_refs, v_refs = refs[3:3 + n], refs[3 + n:3 + 2 * n], refs[3 + 2 * n:3 + 3 * n]
        outs = refs[3 + 3 * n:]
        g_refs, d_refs, mo_refs, vo_refs = outs[0:n], outs[n:2 * n], outs[2 * n:3 * n], outs[3 * n:4 * n]
        l_ref = outs[4 * n]

        def update(i, g, sl=None):
            pick = (lambda r: r[...]) if sl is None else (lambda r: r[:, sl])
            dl, mn, vn = _adam_math(pick(w_refs[i]), g, pick(m_refs[i]), pick(v_refs[i]))
            if sl is None:
                g_refs[i][...], d_refs[i][...], mo_refs[i][...], vo_refs[i][...] = g, dl, mn, vn
            else:
                g_refs[i][:, sl], d_refs[i][:, sl], mo_refs[i][:, sl], vo_refs[i][:, sl] = g, dl, mn, vn

        for i, (name, row, width) in enumerate(_SMALL):
            g = t_ref[row:row + 1, 0:width]
            if name == "ret_decay_fwd":
                g = g * sg_ref[0:1, 0:width]
            elif name == "ret_decay_bwd":
                g = g * sg_ref[1:2, 0:width]
            update(i, g)
        i_cc, i_b = n - 2, n - 1
        cc = w_refs[i_cc][...]
        s = 1.0 / (1.0 + jnp.exp(-cc))
        dsilu = dcc_ref[0, 0:1, :] + dcc_ref[2, 0:1, :] + dcc_ref[4, 0:1, :] + dcc_ref[6, 0:1, :]
        update(i_cc, dsilu * (s * (1.0 + cc * (1.0 - s))))
        for j in range(6):
            update(i_b, t_ref[PACK_ROWS + j:PACK_ROWS + j + 1, :], pl.ds(j * d, d))
        l_ref[...] = jnp.broadcast_to((0.5 / d) * jnp.sum(t_ref[7:8, :], keepdims=True), l_ref.shape)

    shapes = [jax.ShapeDtypeStruct(a.shape, F32) for a in ws]
    outs = pl.pallas_call(
        body, name="small_final", out_shape=shapes * 4 + [jax.ShapeDtypeStruct((8, LANES), F32)],
    )(tot, dcc, sg8, *ws, *ms, *vs)
    return outs[0:n], outs[n:2 * n], outs[2 * n:3 * n], outs[3 * n:4 * n], outs[4 * n]


_WEIGHTS = ("c_ctx", "w_ada", "b_ada", "g_attn", "g_ffn", "w_in", "ret_decay_fwd", "ret_decay_bwd", "g_ret", "g_q_lora",
            "w_uq", "g_kv_lora", "w_ukv", "w_out", "w_ff1", "w_ff2", "g_final")
_BIG = ("w_in", "w_uq", "w_ukv", "w_out", "w_ff1", "w_ff2")
_TRANSPOSED = ("w_in", "w_uq")


def kernel(x, c, ctx, c_ctx, w_ada, b_ada, g_attn, g_ffn, w_in, ret_decay_fwd, ret_decay_bwd, g_ret, g_q_lora, w_uq, g_kv_lora, w_ukv, w_out, w_ff1, w_ff2, g_final, loss_target, m_c_ctx, m_w_ada, m_b_ada, m_g_attn, m_g_ffn, m_w_in, m_ret_decay_fwd, m_ret_decay_bwd, m_g_ret, m_g_q_lora, m_w_uq, m_g_kv_lora, m_w_ukv, m_w_out, m_w_ff1, m_w_ff2, m_g_final, v_c_ctx, v_w_ada, v_b_ada, v_g_attn, v_g_ffn, v_w_in, v_ret_decay_fwd, v_ret_decay_bwd, v_g_ret, v_g_q_lora, v_w_uq, v_g_kv_lora, v_w_ukv, v_w_out, v_w_ff1, v_w_ff2, v_g_final):
    w = dict(c_ctx=c_ctx, w_ada=w_ada, b_ada=b_ada, g_attn=g_attn, g_ffn=g_ffn, w_in=w_in, ret_decay_fwd=ret_decay_fwd,
             ret_decay_bwd=ret_decay_bwd, g_ret=g_ret, g_q_lora=g_q_lora, w_uq=w_uq, g_kv_lora=g_kv_lora, w_ukv=w_ukv,
             w_out=w_out, w_ff1=w_ff1, w_ff2=w_ff2, g_final=g_final)
    m = dict(c_ctx=m_c_ctx, w_ada=m_w_ada, b_ada=m_b_ada, g_attn=m_g_attn, g_ffn=m_g_ffn, w_in=m_w_in,
             ret_decay_fwd=m_ret_decay_fwd, ret_decay_bwd=m_ret_decay_bwd, g_ret=m_g_ret, g_q_lora=m_g_q_lora, w_uq=m_w_uq,
             g_kv_lora=m_g_kv_lora, w_ukv=m_w_ukv, w_out=m_w_out, w_ff1=m_w_ff1, w_ff2=m_w_ff2, g_final=m_g_final)
    v = dict(c_ctx=v_c_ctx, w_ada=v_w_ada, b_ada=v_b_ada, g_attn=v_g_attn, g_ffn=v_g_ffn, w_in=v_w_in,
             ret_decay_fwd=v_ret_decay_fwd, ret_decay_bwd=v_ret_decay_bwd, g_ret=v_g_ret, g_q_lora=v_g_q_lora, w_uq=v_w_uq,
             g_kv_lora=v_g_kv_lora, w_ukv=v_w_ukv, w_out=v_w_out, w_ff1=v_w_ff1, w_ff2=v_w_ff2, g_final=v_g_final)
    xi, yi, ci = lax.axis_index("x"), lax.axis_index("y"), lax.axis_index("c")
    chip = 2 * xi + yi
    dev = 2 * chip + ci
    nex, seq, d = x.shape
    n_ada = w_ada.shape[2]

    c_all = _allgather8(jnp.pad(c, ((0, 8 - nex), (0, 0))), "ag_c")[:, :nex].reshape(N_DEV * nex, d)
    a_in = jnp.concatenate([c_all, c_ctx.reshape(1, d), jnp.zeros((7, d), F32)], axis=0)
    b_sh = lax.dynamic_slice(b_ada, (0, chip * n_ada), (1, n_ada))
    mod_sh = _mod_fwd(a_in, w_ada[0], b_sh)

    dec = jnp.zeros((8, LANES), F32).at[0, :HEADS].set(ret_decay_fwd[0]).at[1, :HEADS].set(ret_decay_bwd[0])
    lg8, sg8 = _decay_prep(dec)
    lg = lg8[:2, :HEADS]

    def shard_of(t, k):
        return t[k][0].T if k in _TRANSPOSED else t[k][0]

    shard = {k: shard_of(w, k) for k in _BIG}
    head_rows = MLA_NOPE + MLA_ROPE
    shard["w_uq"] = jnp.pad(shard["w_uq"], ((0, MLA_HEAD - head_rows), (0, 0)))
    slot = chip.reshape(1).astype(jnp.int32)
    core = ci.reshape(1).astype(jnp.int32)
    slots = {k: _cast_into_slot(shard[k], slot, "cast_" + k) for k in _BIG}
    mod8, w_in_f, w_uq_k, w_ukv_k = _run_rider(
        _merge_riders(_gather8_rider(mod_sh), _gather_rider([slots[k] for k in _EARLY])), "ag_early")
    w_in_k = jnp.pad(w_in_f.reshape(IN_COLS, d), ((0, IN_PAD - IN_COLS), (0, 0)))
    mod_all = mod8[0::2].transpose(1, 0, 2).reshape(a_in.shape[0], N_CHIPS * n_ada)
    mod_me = lax.dynamic_slice(mod_all, (nex * dev, 0), (nex, N_CHIPS * n_ada)).reshape(nex, 6, d)
    mod_c = mod_all[N_DEV * nex].reshape(1, 6, d)
    modv = jnp.pad(jnp.concatenate([mod_me, mod_c], axis=0), ((0, 0), (0, 2), (0, 0)))

    gx, g_early, late, st_post, st_ret, st_pre = _local_step(
        x, ctx, loss_target, modv, lg, g_attn, g_ffn, g_final.reshape(1, d), g_ret, g_q_lora, g_kv_lora,
        w_in_k, w_uq_k, w_ukv_k, [slots[k] for k in _LATE], (core, slot))

    g4 = [
        g_early[0][:IN_COLS].reshape(N_CHIPS, IN_COLS // N_CHIPS, d),
        g_early[1].reshape(N_CHIPS, MLA_HEAD, Q_LORA)[:, :head_rows],
        g_early[2],
    ]
    *got, gathered = _run_rider(
        _merge_riders(_exchange_rider(g4), _swap_rider(late), _gather8_rider(_pack_small(st_post, st_ret, st_pre))),
        "rs_exchange")
    got, late_theirs = got[:len(g4)], got[len(g4):]
    partial = [_add_half(g, r, core, "add_half_" + k) for g, r, k in zip(g4, got, _EARLY)]
    tot = _small_reduce(gathered)
    dm = jnp.concatenate([
        gathered[:, 12:24].reshape(N_DEV * nex, 6 * d),
        jnp.concatenate([tot[8:10].reshape(1, 2 * d), jnp.zeros((1, 4 * d), F32)], axis=1),
        jnp.zeros((7, 6 * d), F32)], axis=0)
    dm_sh = lax.dynamic_slice(dm, (0, chip * n_ada), (dm.shape[0], n_ada))
    g_ada, da = _mod_bwd(a_in, dm_sh, w_ada[0])
    *landed, dcc = _run_rider(_merge_riders(_scatter_rider(partial), _gather8_rider(da[N_DEV * nex:])), "rs_scatter")
    mine = [_sum_chips(p, l, slot, "sum_chips_" + k) for p, l, k in zip(partial, landed, _EARLY)]
    theirs = _run_rider(_swap_rider(mine), "rs_swap")
    halves = dict(zip(_EARLY, zip(mine, theirs)))
    halves.update(zip(_LATE, zip(late, late_theirs)))
    grad, delta, new_m, new_v = {}, {}, {}, {}
    for k in _BIG:
        a, b = halves[k]
        shp = w[k].shape
        outs = _adamw_halves(shard_of(w, k), a, b, shard_of(m, k), shard_of(v, k), core, "adamw_" + k)
        grad[k], delta[k], new_m[k], new_v[k] = [(o.T if k in _TRANSPOSED else o).reshape(shp) for o in outs]

    shp = w_ada.shape
    outs = _adamw(w_ada[0], g_ada, m["w_ada"][0], v["w_ada"][0], "adamw_w_ada")
    grad["w_ada"] = g_ada.reshape(shp)
    delta["w_ada"], new_m["w_ada"], new_v["w_ada"] = [o.reshape(shp) for o in outs]
    rows = [{k: t[k].reshape(1, -1) for k in _SMALL_NAMES} for t in (w, m, v)]
    small = _small_final(tot, dcc, sg8, *[[t[k] for k in _SMALL_NAMES] for t in rows])
    for res, outs in zip((grad, delta, new_m, new_v), small[:4]):
        for k, o in zip(_SMALL_NAMES, outs):
            res[k] = o.reshape(w[k].shape)
    return (small[4][0, 0], gx, *[grad[k] for k in _WEIGHTS], *[delta[k] for k in _WEIGHTS],
            *[new_m[k] for k in _WEIGHTS], *[new_v[k] for k in _WEIGHTS])
```

```python
import functools
import math

import jax
import jax.numpy as jnp
from jax import lax
from jax.experimental import pallas as pl
from jax.experimental.pallas import tpu as pltpu

F32 = jnp.float32
BF16 = jnp.bfloat16
MESH = pl.DeviceIdType.MESH

EPS = 1e-6
D_MODEL = 1024
D_FF = 4096
HEADS = 4
RET_DK = 64
RET_DV = 128
MLA_NOPE = 128
MLA_ROPE = 64
MLA_HEAD = 256
Q_LORA = 384
KV_LORA = 256
GRID_W = 64
ROPE_BASE = 10000.0
IN_COLS = 2240
IN_PAD = 2304
PG_COLS = 1152
N_CHIPS = 4
N_DEV = 8
LANES = 128
ADAM_LR = 0.001
ADAM_B1 = 0.9
ADAM_B2 = 0.999
ADAM_EPS = 1e-08
ADAM_WD = 0.01
ADAM_STEP = 10
VMEM_LIMIT = 56 * 1024 * 1024


def _dot(a, b):
    return jnp.dot(a, b, preferred_element_type=F32)


def _dot_nt(a, b):
    return lax.dot_general(a, b, (((1,), (1,)), ((), ())), preferred_element_type=F32)


def _dot_tn(a, b):
    return lax.dot_general(a, b, (((0,), (0,)), ((), ())), preferred_element_type=F32)


def _params(sem=None, vmem=None):
    return pltpu.CompilerParams(dimension_semantics=sem, vmem_limit_bytes=vmem)


def _full(shape):
    n = len(shape)
    return pl.BlockSpec(shape, lambda *_: (0,) * n)


def _rope(x, cos, sin):
    w = x.shape[-1]
    lo = (lax.broadcasted_iota(jnp.int32, (1, w), 1) % 64) < 32
    swapped = jnp.where(lo, pltpu.roll(x, w - 32, 1), pltpu.roll(x, 32, 1))
    return x * cos + swapped * sin


def _rope_t(g, cos, sin):
    w = g.shape[-1]
    lo = (lax.broadcasted_iota(jnp.int32, (1, w), 1) % 64) < 32
    t = g * sin
    swapped = jnp.where(lo, pltpu.roll(t, w - 32, 1), pltpu.roll(t, 32, 1))
    return g * cos + swapped


def _rope_tables(seq, tm):
    rows = seq // GRID_W
    row = jnp.repeat(jnp.arange(rows, dtype=F32), GRID_W)
    col = jnp.tile(jnp.arange(GRID_W, dtype=F32), rows)
    n_freq = RET_DK // 4
    freq = ROPE_BASE ** (-jnp.arange(n_freq, dtype=F32) / n_freq)
    ang = jnp.concatenate([row[:, None] * freq, col[:, None] * freq], axis=-1)
    cos, sin = jnp.cos(ang), jnp.sin(ang)
    cos_t = jnp.tile(jnp.concatenate([cos, cos], -1), (1, HEADS))
    sin_t = jnp.tile(jnp.concatenate([-sin, sin], -1), (1, HEADS))
    cos_t = jnp.concatenate([cos_t, jnp.ones((tm, 4 * RET_DK), F32)], 0)
    sin_t = jnp.concatenate([sin_t, jnp.zeros((tm, 4 * RET_DK), F32)], 0)
    return cos_t, sin_t


def _adam_math(w, g, m, v):
    mn = ADAM_B1 * m + (1.0 - ADAM_B1) * g
    vn = ADAM_B2 * v + (1.0 - ADAM_B2) * (g * g)
    m_hat = mn / (1.0 - ADAM_B1 ** ADAM_STEP)
    v_hat = vn / (1.0 - ADAM_B2 ** ADAM_STEP)
    return -ADAM_LR * (m_hat / (jnp.sqrt(v_hat) + ADAM_EPS) + ADAM_WD * w), mn, vn


def _cast_into_slot(w, slot, name):
    r, c = w.shape
    rb = max(b for b in range(16, 257, 16) if r % b == 0)

    def body(s_ref, w_ref, o_ref):
        o_ref[...] = w_ref[...].astype(BF16)

    return pl.pallas_call(
        body, name=name,
        grid_spec=pltpu.PrefetchScalarGridSpec(
            num_scalar_prefetch=1, grid=(r // rb,),
            in_specs=[pl.BlockSpec((rb, c), lambda i, s: (i, 0))],
            out_specs=pl.BlockSpec((None, rb, c), lambda i, s: (s[0], i, 0))),
        out_shape=jax.ShapeDtypeStruct((N_CHIPS, r, c), BF16),
        compiler_params=_params(("parallel",)),
    )(slot, w)


def _adamw_halves(w, mine, theirs, m, v, core, name):
    r, c = w.shape
    r2 = r // 2
    rb = max(b for b in range(8, r2 + 1, 8) if r2 % b == 0 and b * c * 4 <= (1 << 21))
    nbh = r2 // rb

    def body(z_ref, w_ref, a_ref, b_ref, m_ref, v_ref, g_ref, d_ref, mo_ref, vo_ref):
        here = (pl.program_id(0) // nbh) == z_ref[0]
        gg = jnp.where(here, a_ref[...], b_ref[...])
        g_ref[...] = gg
        d_ref[...], mo_ref[...], vo_ref[...] = _adam_math(w_ref[...], gg, m_ref[...], v_ref[...])

    spec = pl.BlockSpec((rb, c), lambda i, z: (i, 0))
    a_spec = pl.BlockSpec((rb, c), lambda i, z: (jnp.clip(i - z[0] * nbh, 0, nbh - 1), 0))
    b_spec = pl.BlockSpec((rb, c), lambda i, z: (jnp.clip(i - (1 - z[0]) * nbh, 0, nbh - 1), 0))
    shp = jax.ShapeDtypeStruct((r, c), F32)
    return pl.pallas_call(
        body, name=name,
        grid_spec=pltpu.PrefetchScalarGridSpec(
            num_scalar_prefetch=1, grid=(r // rb,), in_specs=[spec, a_spec, b_spec, spec, spec], out_specs=[spec] * 4),
        out_shape=[shp] * 4,
        compiler_params=_params(("parallel",)),
    )(core, w, mine, theirs, m, v)


def _adamw(w, g, m, v, name, rider=None):
    r, c = w.shape
    rb = r
    for cand in (256, 128, 64, 32, 16, 8):
        if r % cand == 0 and cand * c * 4 <= (1 << 20):
            rb = cand
            break
    if r * c * 4 <= (1 << 20):
        rb = r

    def body(w_ref, g_ref, m_ref, v_ref, d_ref, mo_ref, vo_ref):
        d_ref[...], mo_ref[...], vo_ref[...] = _adam_math(w_ref[...], g_ref[...], m_ref[...], v_ref[...])

    spec = pl.BlockSpec((rb, c), lambda i: (i, 0))
    shp = jax.ShapeDtypeStruct((r, c), F32)
    return _hosted_call(
        body, (w, g, m, v), name=name, grid=(r // rb,), in_specs=[spec] * 4, out_specs=[spec] * 3, out_shape=[shp] * 3,
        sem=("parallel",), rider=rider)


def _adamw_halves_group(items, core, name, rider=None):
    c = items[0][0].shape[1]
    rb = 128
    n = len(items)
    nbs = [it[0].shape[0] // rb for it in items]
    starts = [sum(nbs[:s]) for s in range(n)]

    def body(z_ref, *refs):
        ins, outs = refs[:5 * n], refs[5 * n:]
        i = pl.program_id(0)
        for s in range(n):
            w_ref, a_ref, b_ref, m_ref, v_ref = ins[5 * s:5 * s + 5]
            g_ref, d_ref, mo_ref, vo_ref = outs[4 * s:4 * s + 4]

            @pl.when(jnp.logical_and(i >= starts[s], i < starts[s] + nbs[s]))
            def _():
                here = ((i - starts[s]) // (nbs[s] // 2)) == z_ref[0]
                gg = jnp.where(here, a_ref[...], b_ref[...])
                g_ref[...] = gg
                d_ref[...], mo_ref[...], vo_ref[...] = _adam_math(w_ref[...], gg, m_ref[...], v_ref[...])

    in_specs, out_specs, out_shape, args = [pl.BlockSpec(memory_space=pltpu.SMEM)], [], [], [core]
    for (w, a, b, m, v), nb, st in zip(items, nbs, starts):
        full = pl.BlockSpec((rb, c), lambda i, nb=nb, st=st: (jnp.clip(i - st, 0, nb - 1), 0))
        half = pl.BlockSpec((rb, c), lambda i, nb=nb, st=st: (jnp.clip(i - st, 0, nb - 1) % (nb // 2), 0))
        in_specs += [full, half, half, full, full]
        out_specs += [full] * 4
        out_shape += [jax.ShapeDtypeStruct(w.shape, F32)] * 4
        args += [w, a, b, m, v]
    res, carried = _hosted_call(body, args, name=name, grid=(sum(nbs),), in_specs=in_specs, out_specs=out_specs,
                                out_shape=out_shape, sem=("arbitrary",), rider=rider)
    return [tuple(res[4 * s:4 * s + 4]) for s in range(n)], carried


def _decay_prep(dec):
    def body(d_ref, lg_ref, sg_ref):
        d = d_ref[...]
        lg_ref[...] = jnp.minimum(d, 0.0) - jnp.log(1.0 + jnp.exp(-jnp.abs(d)))
        sg_ref[...] = 1.0 / (1.0 + jnp.exp(d))

    shp = jax.ShapeDtypeStruct(dec.shape, F32)
    return pl.pallas_call(body, name="decay_prep", out_shape=[shp, shp])(dec)


def _mod_fwd(a_in, w_ada, b_sh):
    rows, d = a_in.shape
    n = w_ada.shape[1]
    bn = 512

    def body(a_ref, w_ref, b_ref, o_ref):
        a = a_ref[...]
        s = (a / (1.0 + jnp.exp(-a))).astype(BF16)
        o_ref[...] = _dot(s, w_ref[...].astype(BF16)) + b_ref[...]

    return pl.pallas_call(
        body, name="mod_fwd", grid=(n // bn,),
        in_specs=[_full((rows, d)), pl.BlockSpec((d, bn), lambda j: (0, j)), pl.BlockSpec((1, bn), lambda j: (0, j))],
        out_specs=pl.BlockSpec((rows, bn), lambda j: (0, j)),
        out_shape=jax.ShapeDtypeStruct((rows, n), F32),
        compiler_params=_params(("parallel",)),
    )(a_in, w_ada, b_sh)


def _mod_bwd(a_in, dm, w_ada):
    rows, d = a_in.shape
    n = w_ada.shape[1]
    bn = 512
    nb = n // bn

    def body(a_ref, dm_ref, w_ref, gw_ref, da_ref):
        j = pl.program_id(0)
        a = a_ref[...]
        s = (a / (1.0 + jnp.exp(-a))).astype(BF16)
        dmb = dm_ref[...].astype(BF16)
        gw_ref[...] = _dot_tn(s, dmb)
        part = _dot_nt(dmb, w_ref[...].astype(BF16))

        @pl.when(j == 0)
        def _():
            da_ref[...] = part

        @pl.when(j > 0)
        def _():
            da_ref[...] += part

    return pl.pallas_call(
        body, name="mod_bwd", grid=(nb,),
        in_specs=[_full((rows, d)), pl.BlockSpec((rows, bn), lambda j: (0, j)), pl.BlockSpec((d, bn), lambda j: (0, j))],
        out_specs=[pl.BlockSpec((d, bn), lambda j: (0, j)), _full((rows, d))],
        out_shape=[jax.ShapeDtypeStruct((d, n), F32), jax.ShapeDtypeStruct((rows, d), F32)],
        compiler_params=_params(("arbitrary",)),
    )(a_in, dm, w_ada)


def _pre_fwd(x2, ctx2, modv, g_attn, w_in, g_q, g_kv, w_uq, w_ukv, cos_t, sin_t, *, seq, tm, rider=None):
    t_lat, d = x2.shape
    t_ctx = ctx2.shape[0]
    nl, nc = t_lat // tm, t_ctx // tm
    n_all = t_lat + t_ctx
    tpe = seq // tm
    nex = t_lat // seq

    def body(x_ref, c_ref, mod_ref, g_ref, win_ref, gq_ref, gkv_ref, wuq_ref, wukv_ref, cos_ref, sin_ref,
             h_ref, pg_ref, rq_ref, rk_ref, rv_ref, nq_ref, nkv_ref, q_ref, k_ref, v_ref):
        i = pl.program_id(0)
        xt = jnp.where(i < nl, x_ref[...], c_ref[...])
        sh = mod_ref[0, 0:1, :]
        sc = mod_ref[0, 1:2, :]
        r = lax.rsqrt(jnp.mean(xt * xt, axis=-1, keepdims=True) + EPS)
        hb = ((xt * r) * g_ref[...] * (1.0 + sc) + sh).astype(BF16)
        h_ref[...] = hb
        p = _dot_nt(hb, win_ref[...])
        cos = cos_ref[...]
        sin = sin_ref[...]
        rq_ref[...] = _rope(p[:, 0:256], cos, sin).astype(BF16)
        rk_ref[...] = _rope(p[:, 256:512] * (RET_DK ** -0.5), cos, sin).astype(BF16)
        rv_ref[...] = p[:, 512:1024].astype(BF16)
        pg_ref[...] = p[:, 1024:2176]
        cq = p[:, 1536:1920]
        ckv = p[:, 1920:2176]
        nqb = (cq * lax.rsqrt(jnp.mean(cq * cq, axis=-1, keepdims=True) + EPS) * gq_ref[...]).astype(BF16)
        nkvb = (ckv * lax.rsqrt(jnp.mean(ckv * ckv, axis=-1, keepdims=True) + EPS) * gkv_ref[...]).astype(BF16)
        nq_ref[...] = nqb
        nkv_ref[...] = nkvb
        cos1 = cos[:, 0:LANES]
        sin1 = sin[:, 0:LANES]
        kpe = _rope(p[:, 2176:2304], cos1, sin1).astype(BF16)
        for hd in range(HEADS):
            o = hd * MLA_HEAD
            qh = _dot_nt(nqb, wuq_ref[hd]) * MLA_SCALE
            q_ref[:, o:o + 128] = qh[:, 0:128].astype(BF16)
            q_ref[:, o + 128:o + 256] = _rope(qh[:, 128:256], cos1, sin1).astype(BF16)
            kvh = _dot(nkvb, wukv_ref[hd])
            k_ref[:, o:o + 128] = kvh[:, 0:128].astype(BF16)
            k_ref[:, o + 128:o + 256] = kpe
            v_ref[:, hd * 128:(hd + 1) * 128] = kvh[:, 128:256].astype(BF16)

    def tile(width):
        return pl.BlockSpec((tm, width), lambda i: (i, 0))

    widths = (d, PG_COLS, 256, 256, 512, Q_LORA, KV_LORA, HEADS * MLA_HEAD, HEADS * MLA_HEAD, HEADS * 128)
    dtypes = (BF16, F32, BF16, BF16, BF16, BF16, BF16, BF16, BF16, BF16)
    tab = pl.BlockSpec((tm, 256), lambda i: (jnp.where(i < nl, i % tpe, tpe), 0))
    return _hosted_call(
        body, (x2, ctx2, modv, g_attn, w_in, g_q, g_kv, w_uq, w_ukv, cos_t, sin_t), name="pre_fwd", grid=(nl + nc,),
        in_specs=[
            pl.BlockSpec((tm, d), lambda i: (jnp.minimum(i, nl - 1), 0)),
            pl.BlockSpec((tm, d), lambda i: (jnp.maximum(i - nl, 0), 0)),
            pl.BlockSpec((1, 8, d), lambda i: (jnp.minimum(i // tpe, nex), 0, 0)),
            _full((1, d)), _full(w_in.shape), _full((1, Q_LORA)), _full((1, KV_LORA)),
            _full(w_uq.shape), _full(w_ukv.shape), tab, tab,
        ],
        out_specs=[tile(w) for w in widths],
        out_shape=[jax.ShapeDtypeStruct((n_all, w), dt) for w, dt in zip(widths, dtypes)],
        sem=("parallel",), rider=rider)


def _post(yret, ymla, x2, tgt2, modv, g_ffn, g_fin, w_out, w_ff1, w_ff2, *, seq, tm):
    t_lat, d = x2.shape
    nl = t_lat // tm
    tpe = seq // tm
    nex = t_lat // seq
    n_slab = w_ff1.shape[0]
    fs = w_ff1.shape[2]

    def body(yr_ref, ym_ref, x_ref, t_ref, mod_ref, gf_ref, gl_ref, wo_ref, w1_ref, w2_ref,
             mix_ref, a_ref, du_ref, h2_ref, df_ref, dmo_ref, dmix_ref, dxm_ref, st_ref, ru_ref):
        i = pl.program_id(0)
        gt_a = mod_ref[0, 2:3, :]
        sh_f = mod_ref[0, 3:4, :]
        sc_f = mod_ref[0, 4:5, :]
        gt_f = mod_ref[0, 5:6, :]
        g_ffn_v = gf_ref[...]
        g_fin_v = gl_ref[...]
        yr = yr_ref[...]
        ym = ym_ref[...]
        mix_ref[:, 0:512] = yr
        mix_ref[:, 512:1024] = ym
        op = _dot(yr, wo_ref[0:512, :]) + _dot(ym, wo_ref[512:1024, :])
        x_mid = x_ref[...] + gt_a * op
        r2 = lax.rsqrt(jnp.mean(x_mid * x_mid, axis=-1, keepdims=True) + EPS)
        xh2 = x_mid * r2
        h2b = (xh2 * g_ffn_v * (1.0 + sc_f) + sh_f).astype(BF16)
        h2_ref[...] = h2b
        f = jnp.zeros((tm, d), F32)
        for s in range(n_slab):
            ru = jnp.maximum(_dot(h2b, w1_ref[s]), 0.0)
            ru_ref[:, s * fs:(s + 1) * fs] = ru
            ab = (ru * ru).astype(BF16)
            a_ref[:, s * fs:(s + 1) * fs] = ab
            f = f + _dot(ab, w2_ref[s * fs:(s + 1) * fs, :])
        x_out = x_mid + gt_f * f
        r3 = lax.rsqrt(jnp.mean(x_out * x_out, axis=-1, keepdims=True) + EPS)
        xh3 = x_out * r3
        err = xh3 * g_fin_v - t_ref[...]
        dy = err * (1.0 / d)
        dxh3 = dy * g_fin_v
        dx_out = r3 * (dxh3 - xh3 * jnp.mean(dxh3 * xh3, axis=-1, keepdims=True))
        dfb = (dx_out * gt_f).astype(BF16)
        df_ref[...] = dfb
        dh2 = jnp.zeros((tm, d), F32)
        for s in range(n_slab):
            da = _dot_nt(dfb, w2_ref[s * fs:(s + 1) * fs, :])
            dub = (da * (2.0 * ru_ref[:, s * fs:(s + 1) * fs])).astype(BF16)
            du_ref[:, s * fs:(s + 1) * fs] = dub
            dh2 = dh2 + _dot_nt(dub, w1_ref[s])
        dxh2 = dh2 * (1.0 + sc_f) * g_ffn_v
        dx_mid = dx_out + r2 * (dxh2 - xh2 * jnp.mean(dxh2 * xh2, axis=-1, keepdims=True))
        dxm_ref[...] = dx_mid
        dmob = (dx_mid * gt_a).astype(BF16)
        dmo_ref[...] = dmob
        dmix_ref[...] = _dot_nt(dmob, wo_ref[...]).astype(BF16)

        def rsum(v):
            return jnp.sum(v, axis=0, keepdims=True)

        stats = jnp.concatenate([
            rsum(dh2), rsum(dh2 * xh2 * g_ffn_v), rsum(dx_out * f), rsum(dx_mid * op),
            rsum(dh2 * (1.0 + sc_f) * xh2), rsum(dy * xh3), rsum(err * err), jnp.zeros((1, d), F32)], axis=0)

        @pl.when(i % tpe == 0)
        def _():
            st_ref[0] = stats

        @pl.when(i % tpe != 0)
        def _():
            st_ref[0] += stats

    def tile(width):
        return pl.BlockSpec((tm, width), lambda i: (i, 0))

    widths = (d, D_FF, D_FF, d, d, d, d, d)
    dtypes = (BF16, BF16, BF16, BF16, BF16, BF16, BF16, F32)
    const = pl.Buffered(1)
    return pl.pallas_call(
        body, name="post", grid=(nl,),
        in_specs=[
            tile(512), tile(512), tile(d), tile(d),
            pl.BlockSpec((1, 8, d), lambda i: (i // tpe, 0, 0)),
            _full((1, d)), _full((1, d)),
            pl.BlockSpec(w_out.shape, lambda i: (0, 0), pipeline_mode=const),
            pl.BlockSpec(w_ff1.shape, lambda i: (0, 0, 0), pipeline_mode=const),
            pl.BlockSpec(w_ff2.shape, lambda i: (0, 0), pipeline_mode=const),
        ],
        out_specs=[tile(w) for w in widths] + [pl.BlockSpec((1, 8, d), lambda i: (i // tpe, 0, 0))],
        out_shape=[jax.ShapeDtypeStruct((t_lat, w), dt) for w, dt in zip(widths, dtypes)]
        + [jax.ShapeDtypeStruct((nex, 8, d), F32)],
        scratch_shapes=[pltpu.VMEM((tm, D_FF), F32)],
        compiler_params=_params(("arbitrary",), VMEM_LIMIT),
    )(yret, ymla, x2, tgt2, modv, g_ffn, g_fin, w_out, w_ff1, w_ff2)


def _pre_bwd(x2, ctx2, modv, g_attn, pg, drq, drk, dkc_r, drv, dvc_r, drg, dq_m, dkl, dkc, dvl, dvc, dxm,
             w_in, g_q, g_kv, w_uq, w_ukv, cos_t, sin_t, *, seq, tm, rider=None):
    t_lat, d = x2.shape
    t_ctx = ctx2.shape[0]
    nl, nc = t_lat // tm, t_ctx // tm
    n_all = t_lat + t_ctx
    tpe = seq // tm
    nex = t_lat // seq

    def body(x_ref, c_ref, mod_ref, g_ref, pg_ref, drq_ref, drk_ref, dkcr_ref, drv_ref, dvcr_ref, drg_ref,
             dq_ref, dkl_ref, dkc_ref, dvl_ref, dvc_ref, dxm_ref, win_ref, gq_ref, gkv_ref, wuq_ref, wukv_ref,
             cos_ref, sin_ref, dpb_ref, dqf_ref, dkvf_ref, gx_ref, st_ref):
        i = pl.program_id(0)
        lat = i < nl
        latf = lat.astype(F32)
        cos = cos_ref[...]
        sin = sin_ref[...]
        cos1 = cos[:, 0:LANES]
        sin1 = sin[:, 0:LANES]
        d_rq = _rope_t(drq_ref[...] * latf, cos, sin)
        d_rk = _rope_t(jnp.where(lat, drk_ref[...], dkcr_ref[...]), cos, sin) * (RET_DK ** -0.5)
        d_rv = jnp.where(lat, drv_ref[...], dvcr_ref[...])
        d_rg = drg_ref[...] * latf
        dq_all = dq_ref[...] * (latf * MLA_SCALE)
        dk_all = jnp.where(lat, dkl_ref[...], dkc_ref[...])
        dv_all = jnp.where(lat, dvl_ref[...], dvc_ref[...])
        dnq = jnp.zeros((tm, Q_LORA), F32)
        dnkv = jnp.zeros((tm, KV_LORA), F32)
        dkpe = jnp.zeros((tm, LANES), F32)
        for hd in range(HEADS):
            o = hd * MLA_HEAD
            dqh = jnp.concatenate([dq_all[:, o:o + 128], _rope_t(dq_all[:, o + 128:o + 256], cos1, sin1)],
                                  axis=1).astype(BF16)
            dqf_ref[:, o:o + 256] = dqh
            dnq = dnq + _dot(dqh, wuq_ref[hd])
            dkpe = dkpe + dk_all[:, o + 128:o + 256]
            dkvh = jnp.concatenate([dk_all[:, o:o + 128], dv_all[:, hd * 128:(hd + 1) * 128]], axis=1).astype(BF16)
            dkvf_ref[:, o:o + 256] = dkvh
            dnkv = dnkv + _dot_nt(dkvh, wukv_ref[hd])
        d_kpe = _rope_t(dkpe, cos1, sin1)
        pgv = pg_ref[...]
        cq = pgv[:, 512:896]
        ckv = pgv[:, 896:1152]
        rq_ = lax.rsqrt(jnp.mean(cq * cq, axis=-1, keepdims=True) + EPS)
        cqh = cq * rq_
        dcqh = dnq * gq_ref[...]
        d_cq = rq_ * (dcqh - cqh * jnp.mean(dcqh * cqh, axis=-1, keepdims=True))
        rkv_ = lax.rsqrt(jnp.mean(ckv * ckv, axis=-1, keepdims=True) + EPS)
        ckvh = ckv * rkv_
        dckvh = dnkv * gkv_ref[...]
        d_ckv = rkv_ * (dckvh - ckvh * jnp.mean(dckvh * ckvh, axis=-1, keepdims=True))
        dpb = jnp.concatenate([d_rq, d_rk, d_rv, d_rg, d_cq, d_ckv, d_kpe], axis=1).astype(BF16)
        dpb_ref[...] = dpb
        dh = _dot(dpb, win_ref[...])
        xt = jnp.where(lat, x_ref[...], c_ref[...])
        sc = mod_ref[0, 1:2, :]
        g = g_ref[...]
        r = lax.rsqrt(jnp.mean(xt * xt, axis=-1, keepdims=True) + EPS)
        xh = xt * r
        dxh = dh * (1.0 + sc) * g
        dx = r * (dxh - xh * jnp.mean(dxh * xh, axis=-1, keepdims=True))

        @pl.when(lat)
        def _():
            gx_ref[...] = dxm_ref[...] + dx

        def rsum(v):
            return jnp.sum(v, axis=0, keepdims=True)

        def widen(v):
            return jnp.concatenate([v, jnp.zeros((1, d - v.shape[1]), F32)], axis=1)

        stats = jnp.concatenate([
            rsum(dh), rsum(dh * xh * g), rsum(dh * (1.0 + sc) * xh), widen(rsum(dnq * cqh)), widen(rsum(dnkv * ckvh)),
            jnp.zeros((3, d), F32)], axis=0)
        first = jnp.logical_or(jnp.logical_and(lat, i % tpe == 0), i == nl)

        @pl.when(first)
        def _():
            st_ref[0] = stats

        @pl.when(jnp.logical_not(first))
        def _():
            st_ref[0] += stats

    def lat_tile(width):
        return pl.BlockSpec((tm, width), lambda i: (jnp.minimum(i, nl - 1), 0))

    def ctx_tile(width):
        return pl.BlockSpec((tm, width), lambda i: (jnp.maximum(i - nl, 0), 0))

    def tile(width):
        return pl.BlockSpec((tm, width), lambda i: (i, 0))

    tab = pl.BlockSpec((tm, 256), lambda i: (jnp.where(i < nl, i % tpe, tpe), 0))
    ex = pl.BlockSpec((1, 8, d), lambda i: (jnp.minimum(i // tpe, nex), 0, 0))
    return _hosted_call(
        body, (x2, ctx2, modv, g_attn, pg, drq, drk, dkc_r, drv, dvc_r, drg, dq_m, dkl, dkc, dvl, dvc, dxm,
               w_in, g_q, g_kv, w_uq, w_ukv, cos_t, sin_t), name="pre_bwd", grid=(nl + nc,),
        in_specs=[
            lat_tile(d), ctx_tile(d), ex, _full((1, d)), tile(PG_COLS),
            lat_tile(256), lat_tile(256), ctx_tile(256), lat_tile(512), ctx_tile(512), lat_tile(512),
            lat_tile(1024), lat_tile(1024), ctx_tile(1024), lat_tile(512), ctx_tile(512), lat_tile(d),
            _full(w_in.shape), _full((1, Q_LORA)), _full((1, KV_LORA)), _full(w_uq.shape), _full(w_ukv.shape),
            tab, tab,
        ],
        out_specs=[tile(IN_PAD), tile(1024), tile(1024), lat_tile(d), ex],
        out_shape=[
            jax.ShapeDtypeStruct((n_all, IN_PAD), BF16), jax.ShapeDtypeStruct((n_all, 1024), BF16),
            jax.ShapeDtypeStruct((n_all, 1024), BF16), jax.ShapeDtypeStruct((t_lat, d), F32),
            jax.ShapeDtypeStruct((nex + 1, 8, d), F32),
        ],
        sem=("arbitrary",), rider=rider)


MLA_SCALE = 1.0 / math.sqrt(MLA_NOPE + MLA_ROPE)
KEY_BLOCK = 2048


def _mla_specs(t_lat, seq, ctx_len, tq):
    nqt = seq // tq
    cb = t_lat // ctx_len
    q = pl.BlockSpec((tq, MLA_HEAD), lambda b, h, j: (b * nqt + j, h))
    kl = pl.BlockSpec((seq, MLA_HEAD), lambda b, h, j: (b, h))
    kc = pl.BlockSpec((ctx_len, MLA_HEAD), lambda b, h, j: (cb + b, h))
    vl = pl.BlockSpec((seq, 128), lambda b, h, j: (b, h))
    vc = pl.BlockSpec((ctx_len, 128), lambda b, h, j: (cb + b, h))
    o = pl.BlockSpec((tq, 128), lambda b, h, j: (b * nqt + j, h))
    return q, kl, kc, vl, vc, o


def _mla_fwd(q, k, v, *, t_lat, seq, ctx_len, tq, rider=None):
    nex = t_lat // seq

    def body(q_ref, kl_ref, kc_ref, vl_ref, vc_ref, o_ref, lse_ref):
        qb = q_ref[...]
        s = _dot_nt(qb, kl_ref[...])
        sc = _dot_nt(qb, kc_ref[...])
        m = jnp.maximum(jnp.max(s, axis=-1, keepdims=True), jnp.max(sc, axis=-1, keepdims=True))
        p = jnp.exp(s - m)
        pc = jnp.exp(sc - m)
        total = jnp.sum(p, axis=-1, keepdims=True) + jnp.sum(pc, axis=-1, keepdims=True)
        o = _dot(p.astype(BF16), vl_ref[...]) + _dot(pc.astype(BF16), vc_ref[...])
        o_ref[...] = (o * (1.0 / total)).astype(BF16)
        lse_ref[...] = jnp.broadcast_to(m + jnp.log(total), lse_ref.shape)

    qs, kl, kc, vl, vc, os_ = _mla_specs(t_lat, seq, ctx_len, tq)
    return _hosted_call(
        body, (q, k, k, v, v), name="mla_fwd", grid=(nex, HEADS, seq // tq),
        in_specs=[qs, kl, kc, vl, vc], out_specs=[os_, os_],
        out_shape=[jax.ShapeDtypeStruct((t_lat, HEADS * 128), BF16), jax.ShapeDtypeStruct((t_lat, HEADS * 128), F32)],
        sem=("parallel", "parallel", "arbitrary"), rider=rider)


def _mla_bwd(q, k, v, ymla, lse, dmix, *, t_lat, seq, ctx_len, tq, rider=None):
    nex = t_lat // seq
    nqt = seq // tq
    t_ctx = nex * ctx_len
    kb = min(KEY_BLOCK, seq)

    def body(q_ref, kl_ref, kc_ref, vl_ref, vc_ref, o_ref, lse_ref, do_ref, dq_ref, dkl_ref, dkc_ref, dvl_ref, dvc_ref):
        j = pl.program_id(2)

        @pl.when(j == 0)
        def _():
            dkl_ref[...] = jnp.zeros(dkl_ref.shape, F32)
            dkc_ref[...] = jnp.zeros(dkc_ref.shape, F32)
            dvl_ref[...] = jnp.zeros(dvl_ref.shape, F32)
            dvc_ref[...] = jnp.zeros(dvc_ref.shape, F32)

        qb = q_ref[...]
        dob = do_ref[...]
        delta = jnp.sum(dob.astype(F32) * o_ref[...].astype(F32), axis=-1, keepdims=True)
        lse_row = lse_ref[:, 0:1]

        def block(k_ref, v_ref, dk_ref, dv_ref, rows):
            kbl = k_ref[rows, :]
            vbl = v_ref[rows, :]
            p = jnp.exp(_dot_nt(qb, kbl) - lse_row)
            ds = (p * (_dot_nt(dob, vbl) - delta)).astype(BF16)
            dk_ref[rows, :] += _dot_tn(ds, qb)
            dv_ref[rows, :] += _dot_tn(p.astype(BF16), dob)
            return _dot(ds, kbl)

        dq = block(kc_ref, vc_ref, dkc_ref, dvc_ref, pl.ds(0, ctx_len))
        for i in range(seq // kb):
            dq = dq + block(kl_ref, vl_ref, dkl_ref, dvl_ref, pl.ds(i * kb, kb))
        dq_ref[...] = dq

    qs, kl, kc, vl, vc, os_ = _mla_specs(t_lat, seq, ctx_len, tq)
    do_spec = pl.BlockSpec((tq, 128), lambda b, h, j: (b * nqt + j, HEADS + h))
    return _hosted_call(
        body, (q, k, k, v, v, ymla, lse, dmix), name="mla_bwd", grid=(nex, HEADS, nqt),
        in_specs=[qs, kl, kc, vl, vc, os_, os_, do_spec],
        out_specs=[
            qs,
            pl.BlockSpec((seq, MLA_HEAD), lambda b, h, j: (b, h)),
            pl.BlockSpec((ctx_len, MLA_HEAD), lambda b, h, j: (b, h)),
            pl.BlockSpec((seq, 128), lambda b, h, j: (b, h)),
            pl.BlockSpec((ctx_len, 128), lambda b, h, j: (b, h)),
        ],
        out_shape=[
            jax.ShapeDtypeStruct((t_lat, HEADS * MLA_HEAD), F32),
            jax.ShapeDtypeStruct((t_lat, HEADS * MLA_HEAD), F32),
            jax.ShapeDtypeStruct((t_ctx, HEADS * MLA_HEAD), F32),
            jax.ShapeDtypeStruct((t_lat, HEADS * 128), F32),
            jax.ShapeDtypeStruct((t_ctx, HEADS * 128), F32),
        ],
        sem=("parallel", "parallel", "arbitrary"), rider=rider)


def _decay_terms(lg, chunk, forward):
    ii = lax.broadcasted_iota(jnp.int32, (chunk, chunk), 0)
    jj = lax.broadcasted_iota(jnp.int32, (chunk, chunk), 1)
    diff = (ii - jj) if forward else (jj - ii)
    dist = jnp.maximum(diff, 0).astype(F32)
    dmat = jnp.where(diff >= 0, jnp.exp(lg * dist), 0.0)
    pos = lax.broadcasted_iota(jnp.int32, (chunk, 1), 0).astype(F32)
    if forward:
        e_q = pos + 1.0
        e_k = (chunk - 1.0) - pos
    else:
        e_q = chunk - pos
        e_k = pos
    wq = jnp.exp(lg * e_q)
    wk = jnp.exp(lg * e_k)
    cd = jnp.exp(jnp.full((1, 1), lg * chunk, F32))
    return dmat, dist, wq, wk, e_q, e_k, cd


def _ctx_weights(lg, ctx_len, forward):
    pos = lax.broadcasted_iota(jnp.int32, (ctx_len, 1), 0).astype(F32)
    e = ((ctx_len - 1.0) - pos) if forward else pos
    return jnp.exp(lg * e), e


def _pair_specs(t_lat, seq, ctx_len):
    cb = t_lat // ctx_len
    qk = pl.BlockSpec((seq, 128), lambda b, p: (b, p))
    v = pl.BlockSpec((seq, 256), lambda b, p: (b, p))
    kc = pl.BlockSpec((ctx_len, 128), lambda b, p: (cb + b, p))
    vc = pl.BlockSpec((ctx_len, 256), lambda b, p: (cb + b, p))
    return qk, v, kc, vc


def _lane_masks():
    lane = lax.broadcasted_iota(jnp.int32, (1, 128), 1)
    return [(lane // RET_DK) == hh for hh in (0, 1)]


def _ret_fwd_pair(rq, rk, rv, pg, lg, g_ret, *, t_lat, seq, ctx_len, chunk, rider=None):
    nex = t_lat // seq
    n_chunk = seq // chunk

    def body(q_ref, k_ref, v_ref, kc_ref, vc_ref, rg_ref, lg_ref, g_ref, y_ref, o_ref):
        pair = pl.program_id(1)
        masks = _lane_masks()
        kcf = kc_ref[...].astype(F32)

        def run(forward):
            terms, s0 = [], []
            for hh in (0, 1):
                lgd = lg_ref[0 if forward else 1, 2 * pair + hh]
                terms.append(_decay_terms(lgd, chunk, forward))
                wc, _ = _ctx_weights(lgd, ctx_len, forward)
                s0.append(_dot_tn((jnp.where(masks[hh], kcf, 0.0) * wc).astype(BF16), vc_ref[:, hh * 128:(hh + 1) * 128]))

            def step(t, states):
                n = t if forward else n_chunk - 1 - t
                sl = pl.ds(pl.multiple_of(n * chunk, chunk), chunk)
                qb = q_ref[sl, :]
                kf_all = k_ref[sl, :].astype(F32)
                new = []
                for hh in (0, 1):
                    dmat, _, wq, wk, _, _, cd = terms[hh]
                    cols = slice(hh * 128, (hh + 1) * 128)
                    qm = jnp.where(masks[hh], qb, jnp.zeros((), BF16))
                    kf = jnp.where(masks[hh], kf_all, 0.0)
                    vb = v_ref[sl, cols]
                    a = _dot_nt(qm, kf.astype(BF16)) * dmat
                    o = _dot(a.astype(BF16), vb) + wq * _dot(qm, states[hh].astype(BF16))
                    if forward:
                        o_ref[sl, cols] = o
                    else:
                        o = o_ref[sl, cols] + o
                        o_ref[sl, cols] = o
                        mu = jnp.mean(o, axis=-1, keepdims=True)
                        oc = o - mu
                        var = jnp.mean(oc * oc, axis=-1, keepdims=True)
                        rg = rg_ref[sl, cols]
                        y_ref[sl, cols] = (oc * lax.rsqrt(var + EPS) * g_ref[:, cols] * (rg / (1.0 + jnp.exp(-rg)))).astype(BF16)
                    new.append(cd * states[hh] + _dot_tn((kf * wk).astype(BF16), vb))
                return tuple(new)

            lax.fori_loop(0, n_chunk, step, tuple(s0))

        run(True)
        run(False)

    qk, v, kc, vc = _pair_specs(t_lat, seq, ctx_len)
    return _hosted_call(
        body, (rq, rk, rv, rk, rv, pg, lg, g_ret), name="ret_fwd", grid=(nex, HEADS // 2),
        in_specs=[qk, qk, v, kc, vc, v, pl.BlockSpec(memory_space=pltpu.SMEM), pl.BlockSpec((1, 256), lambda b, p: (0, p))],
        out_specs=[v, v],
        out_shape=[jax.ShapeDtypeStruct((t_lat, HEADS * RET_DV), BF16), jax.ShapeDtypeStruct((t_lat, HEADS * RET_DV), F32)],
        sem=("parallel", "arbitrary"), rider=rider)


def _ret_bwd_pair(rq, rk, rv, pg, osum, dmix, lg, g_ret, *, t_lat, seq, ctx_len, chunk, rider=None):
    nex = t_lat // seq
    n_chunk = seq // chunk
    t_ctx = nex * ctx_len

    def body(q_ref, k_ref, v_ref, kc_ref, vc_ref, rg_ref, o_ref, dy_ref, lg_ref, g_ref,
             dq_ref, dk_ref, dv_ref, dkc_ref, dvc_ref, drg_ref, st_ref, do_s, s_st):
        pair = pl.program_id(1)
        masks = _lane_masks()
        kcf = kc_ref[...].astype(F32)

        def norm_step(n, dgains):
            sl = pl.ds(pl.multiple_of(n * chunk, chunk), chunk)
            out = []
            for hh in (0, 1):
                cols = slice(hh * 128, (hh + 1) * 128)
                gain = g_ref[:, cols]
                o = o_ref[sl, cols]
                mu = jnp.mean(o, axis=-1, keepdims=True)
                oc = o - mu
                rstd = lax.rsqrt(jnp.mean(oc * oc, axis=-1, keepdims=True) + EPS)
                ohat = oc * rstd
                rg = rg_ref[sl, cols]
                sg = 1.0 / (1.0 + jnp.exp(-rg))
                dy = dy_ref[sl, cols].astype(F32)
                don = dy * (rg * sg)
                drg_ref[sl, cols] = dy * (ohat * gain) * (sg * (1.0 + rg * (1.0 - sg)))
                dohat = don * gain
                do_s[sl, cols] = rstd * (dohat - jnp.mean(dohat, axis=-1, keepdims=True)
                                         - ohat * jnp.mean(dohat * ohat, axis=-1, keepdims=True))
                out.append(dgains[hh] + jnp.sum(don * ohat, axis=0, keepdims=True))
            return tuple(out)

        zero_row = jnp.zeros((1, 128), F32)
        dgains = lax.fori_loop(0, n_chunk, norm_step, (zero_row, zero_row))
        dq_ref[...] = jnp.zeros(dq_ref.shape, F32)
        dk_ref[...] = jnp.zeros(dk_ref.shape, F32)
        dv_ref[...] = jnp.zeros(dv_ref.shape, F32)

        def run(forward):
            terms, ctxw, s0 = [], [], []
            for hh in (0, 1):
                lgd = lg_ref[0 if forward else 1, 2 * pair + hh]
                terms.append(_decay_terms(lgd, chunk, forward))
                ctxw.append(_ctx_weights(lgd, ctx_len, forward))
                s0.append(_dot_tn((jnp.where(masks[hh], kcf, 0.0) * ctxw[hh][0]).astype(BF16),
                                  vc_ref[:, hh * 128:(hh + 1) * 128]))

            def state_step(t, states):
                n = t if forward else n_chunk - 1 - t
                sl = pl.ds(pl.multiple_of(n * chunk, chunk), chunk)
                kf_all = k_ref[sl, :].astype(F32)
                new = []
                for hh in (0, 1):
                    wk, cd = terms[hh][3], terms[hh][6]
                    s_st[hh, n] = states[hh]
                    kf = jnp.where(masks[hh], kf_all, 0.0)
                    new.append(cd * states[hh] + _dot_tn((kf * wk).astype(BF16), v_ref[sl, hh * 128:(hh + 1) * 128]))
                return tuple(new)

            lax.fori_loop(0, n_chunk, state_step, tuple(s0))

            def grad_step(t, carry):
                n = (n_chunk - 1 - t) if forward else t
                sl = pl.ds(pl.multiple_of(n * chunk, chunk), chunk)
                qb = q_ref[sl, :]
                kf_all = k_ref[sl, :].astype(F32)
                dq_sum = jnp.zeros((chunk, 128), F32)
                dk_sum = jnp.zeros((chunk, 128), F32)
                out = []
                for hh in (0, 1):
                    g_next, dlg = carry[hh]
                    dmat, dist, wq, wk, e_q, e_k, cd = terms[hh]
                    cols = slice(hh * 128, (hh + 1) * 128)
                    qm = jnp.where(masks[hh], qb, jnp.zeros((), BF16))
                    kf = jnp.where(masks[hh], kf_all, 0.0)
                    kb = kf.astype(BF16)
                    vb = v_ref[sl, cols]
                    do = do_s[sl, cols]
                    dob = do.astype(BF16)
                    s_n = s_st[hh, n]
                    s_nb = s_n.astype(BF16)
                    gb = g_next.astype(BF16)
                    dk_cross = wk * _dot_nt(vb, gb)
                    dv_cross = _dot((kf * wk).astype(BF16), gb)
                    a = _dot_nt(qm, kb) * dmat
                    da_raw = _dot_nt(dob, vb)
                    dab = (da_raw * dmat).astype(BF16)
                    o_cross = wq * _dot(qm, s_nb)
                    dq_sum = dq_sum + _dot(dab, kb) + wq * _dot_nt(dob, s_nb)
                    dk_sum = dk_sum + _dot_tn(dab, qm) + dk_cross
                    dv_ref[sl, cols] += _dot_tn(a.astype(BF16), dob) + dv_cross
                    dlg = (dlg + chunk * cd * jnp.sum(g_next * s_n, keepdims=True)
                           + jnp.sum(e_k * jnp.sum(kf * dk_cross, axis=-1, keepdims=True), keepdims=True)
                           + jnp.sum(dist * a * da_raw, keepdims=True)
                           + jnp.sum(e_q * jnp.sum(o_cross * do, axis=-1, keepdims=True), keepdims=True))
                    out.append((cd * g_next + _dot_tn((qm.astype(F32) * wq).astype(BF16), dob), dlg))
                dq_ref[sl, :] += dq_sum
                dk_ref[sl, :] += dk_sum
                return tuple(out)

            zero = (jnp.zeros((128, 128), F32), jnp.zeros((1, 1), F32))
            res = lax.fori_loop(0, n_chunk, grad_step, (zero, zero))
            dkc_sum = jnp.zeros((ctx_len, 128), F32)
            dvc, dlgs = [], []
            for hh in (0, 1):
                ds0, dlg = res[hh]
                wc, e_c = ctxw[hh]
                kcm = jnp.where(masks[hh], kcf, 0.0)
                ds0b = ds0.astype(BF16)
                dkc_part = wc * _dot_nt(vc_ref[:, hh * 128:(hh + 1) * 128], ds0b)
                dkc_sum = dkc_sum + dkc_part
                dvc.append(_dot((kcm * wc).astype(BF16), ds0b))
                dlgs.append(dlg + jnp.sum(e_c * jnp.sum(kcm * dkc_part, axis=-1, keepdims=True), keepdims=True))
            return dkc_sum, dvc, dlgs

        dkc_f, dvc_f, dlg_f = run(True)
        dkc_b, dvc_b, dlg_b = run(False)
        dkc_ref[...] = dkc_f + dkc_b
        for hh in (0, 1):
            cols = slice(hh * 128, (hh + 1) * 128)
            dvc_ref[:, cols] = dvc_f[hh] + dvc_b[hh]
            st_ref[0, :, cols] = jnp.concatenate([
                dgains[hh], jnp.broadcast_to(dlg_f[hh], (1, 128)), jnp.broadcast_to(dlg_b[hh], (1, 128)),
                jnp.zeros((5, 128), F32)], axis=0)

    qk, v, kc, vc = _pair_specs(t_lat, seq, ctx_len)
    return _hosted_call(
        body, (rq, rk, rv, rk, rv, pg, osum, dmix, lg, g_ret), name="ret_bwd", grid=(nex, HEADS // 2),
        in_specs=[qk, qk, v, kc, vc, v, v, v, pl.BlockSpec(memory_space=pltpu.SMEM),
                  pl.BlockSpec((1, 256), lambda b, p: (0, p))],
        out_specs=[
            qk, qk, v,
            pl.BlockSpec((ctx_len, 128), lambda b, p: (b, p)),
            pl.BlockSpec((ctx_len, 256), lambda b, p: (b, p)),
            v,
            pl.BlockSpec((1, 8, 256), lambda b, p: (b, 0, p)),
        ],
        out_shape=[
            jax.ShapeDtypeStruct((t_lat, 256), F32), jax.ShapeDtypeStruct((t_lat, 256), F32),
            jax.ShapeDtypeStruct((t_lat, 512), F32), jax.ShapeDtypeStruct((t_ctx, 256), F32),
            jax.ShapeDtypeStruct((t_ctx, 512), F32), jax.ShapeDtypeStruct((t_lat, 512), F32),
            jax.ShapeDtypeStruct((nex, 8, 512), F32),
        ],
        scratch_shapes=[pltpu.VMEM((seq, 256), F32), pltpu.VMEM((2, n_chunk, 128, 128), F32)],
        sem=("parallel", "arbitrary"), rider=rider)


def _ret_specs(t_lat, seq, ctx_len):
    cb = t_lat // ctx_len
    qk = pl.BlockSpec((seq, 128), lambda b, h: (b, h // 2))
    v = pl.BlockSpec((seq, 128), lambda b, h: (b, h))
    kc = pl.BlockSpec((ctx_len, 128), lambda b, h: (cb + b, h // 2))
    vc = pl.BlockSpec((ctx_len, 128), lambda b, h: (cb + b, h))
    return qk, v, kc, vc


def _head_mask(h):
    lane = lax.broadcasted_iota(jnp.int32, (1, 128), 1)
    return (lane // RET_DK) == (h % 2)


def _ret_fwd(rq, rk, rv, pg, lg, g_ret, *, t_lat, seq, ctx_len, chunk, rider=None):
    nex = t_lat // seq
    n_chunk = seq // chunk

    def body(q_ref, k_ref, v_ref, kc_ref, vc_ref, rg_ref, lg_ref, g_ref, y_ref, o_ref):
        h = pl.program_id(1)
        hm = _head_mask(h)
        gain = g_ref[...]
        kcm = jnp.where(hm, kc_ref[...].astype(F32), 0.0)
        vcb = vc_ref[...]

        def run(forward):
            lgd = lg_ref[0 if forward else 1, h]
            dmat, _, wq, wk, _, _, cd = _decay_terms(lgd, chunk, forward)
            wc, _ = _ctx_weights(lgd, ctx_len, forward)
            s0 = _dot_tn((kcm * wc).astype(BF16), vcb)

            def step(t, s):
                n = t if forward else n_chunk - 1 - t
                sl = pl.ds(pl.multiple_of(n * chunk, chunk), chunk)
                qm = jnp.where(hm, q_ref[sl, :], jnp.zeros((), BF16))
                kf = jnp.where(hm, k_ref[sl, :].astype(F32), 0.0)
                vb = v_ref[sl, :]
                a = _dot_nt(qm, kf.astype(BF16)) * dmat
                o = _dot(a.astype(BF16), vb) + wq * _dot(qm, s.astype(BF16))
                if forward:
                    o_ref[sl, :] = o
                else:
                    o = o_ref[sl, :] + o
                    o_ref[sl, :] = o
                    mu = jnp.mean(o, axis=-1, keepdims=True)
                    oc = o - mu
                    var = jnp.mean(oc * oc, axis=-1, keepdims=True)
                    on = oc * lax.rsqrt(var + EPS) * gain
                    rg = rg_ref[sl, :]
                    y_ref[sl, :] = (on * (rg / (1.0 + jnp.exp(-rg)))).astype(BF16)
                return cd * s + _dot_tn((kf * wk).astype(BF16), vb)

            lax.fori_loop(0, n_chunk, step, s0)

        run(True)
        run(False)

    qk, v, kc, vc = _ret_specs(t_lat, seq, ctx_len)
    return _hosted_call(
        body, (rq, rk, rv, rk, rv, pg, lg, g_ret), name="ret_fwd", grid=(nex, HEADS),
        in_specs=[qk, qk, v, kc, vc, v, pl.BlockSpec(memory_space=pltpu.SMEM), pl.BlockSpec((1, 128), lambda b, h: (0, h))],
        out_specs=[v, v],
        out_shape=[jax.ShapeDtypeStruct((t_lat, HEADS * RET_DV), BF16), jax.ShapeDtypeStruct((t_lat, HEADS * RET_DV), F32)],
        sem=("parallel", "arbitrary"), rider=rider)


def _ret_bwd(rq, rk, rv, pg, osum, dmix, lg, g_ret, *, t_lat, seq, ctx_len, chunk, rider=None):
    nex = t_lat // seq
    n_chunk = seq // chunk
    t_ctx = nex * ctx_len

    def body(q_ref, k_ref, v_ref, kc_ref, vc_ref, rg_ref, o_ref, dy_ref, lg_ref, g_ref,
             dq_ref, dk_ref, dv_ref, dkc_ref, dvc_ref, drg_ref, st_ref, do_s, s_st):
        h = pl.program_id(1)
        hm = _head_mask(h)
        gain = g_ref[...]
        kcm = jnp.where(hm, kc_ref[...].astype(F32), 0.0)
        vcb = vc_ref[...]

        def norm_step(n, dgain):
            sl = pl.ds(pl.multiple_of(n * chunk, chunk), chunk)
            o = o_ref[sl, :]
            mu = jnp.mean(o, axis=-1, keepdims=True)
            oc = o - mu
            rstd = lax.rsqrt(jnp.mean(oc * oc, axis=-1, keepdims=True) + EPS)
            ohat = oc * rstd
            rg = rg_ref[sl, :]
            sg = 1.0 / (1.0 + jnp.exp(-rg))
            dy = dy_ref[sl, :].astype(F32)
            don = dy * (rg * sg)
            drg_ref[sl, :] = dy * (ohat * gain) * (sg * (1.0 + rg * (1.0 - sg)))
            dohat = don * gain
            do_s[sl, :] = rstd * (dohat - jnp.mean(dohat, axis=-1, keepdims=True)
                                  - ohat * jnp.mean(dohat * ohat, axis=-1, keepdims=True))
            return dgain + jnp.sum(don * ohat, axis=0, keepdims=True)

        dgain = lax.fori_loop(0, n_chunk, norm_step, jnp.zeros((1, 128), F32))

        @pl.when(h % 2 == 0)
        def _():
            dq_ref[...] = jnp.zeros(dq_ref.shape, F32)
            dk_ref[...] = jnp.zeros(dk_ref.shape, F32)
            dkc_ref[...] = jnp.zeros(dkc_ref.shape, F32)

        dv_ref[...] = jnp.zeros(dv_ref.shape, F32)

        def run(forward):
            lgd = lg_ref[0 if forward else 1, h]
            dmat, dist, wq, wk, e_q, e_k, cd = _decay_terms(lgd, chunk, forward)
            wc, e_c = _ctx_weights(lgd, ctx_len, forward)
            s0 = _dot_tn((kcm * wc).astype(BF16), vcb)

            def state_step(t, s):
                n = t if forward else n_chunk - 1 - t
                sl = pl.ds(pl.multiple_of(n * chunk, chunk), chunk)
                s_st[n] = s
                kf = jnp.where(hm, k_ref[sl, :].astype(F32), 0.0)
                return cd * s + _dot_tn((kf * wk).astype(BF16), v_ref[sl, :])

            lax.fori_loop(0, n_chunk, state_step, s0)

            def grad_step(t, carry):
                g_next, dlg = carry
                n = (n_chunk - 1 - t) if forward else t
                sl = pl.ds(pl.multiple_of(n * chunk, chunk), chunk)
                qm = jnp.where(hm, q_ref[sl, :], jnp.zeros((), BF16))
                kf = jnp.where(hm, k_ref[sl, :].astype(F32), 0.0)
                kb = kf.astype(BF16)
                vb = v_ref[sl, :]
                do = do_s[sl, :]
                dob = do.astype(BF16)
                s_n = s_st[n]
                s_nb = s_n.astype(BF16)
                gb = g_next.astype(BF16)
                dk_cross = wk * _dot_nt(vb, gb)
                dv_cross = _dot((kf * wk).astype(BF16), gb)
                a = _dot_nt(qm, kb) * dmat
                da_raw = _dot_nt(dob, vb)
                dab = (da_raw * dmat).astype(BF16)
                ab = a.astype(BF16)
                o_cross = wq * _dot(qm, s_nb)
                dq_ref[sl, :] += _dot(dab, kb) + wq * _dot_nt(dob, s_nb)
                dk_ref[sl, :] += _dot_tn(dab, qm) + dk_cross
                dv_ref[sl, :] += _dot_tn(ab, dob) + dv_cross
                dlg = (dlg + chunk * cd * jnp.sum(g_next * s_n, keepdims=True)
                       + jnp.sum(e_k * jnp.sum(kf * dk_cross, axis=-1, keepdims=True), keepdims=True)
                       + jnp.sum(dist * a * da_raw, keepdims=True)
                       + jnp.sum(e_q * jnp.sum(o_cross * do, axis=-1, keepdims=True), keepdims=True))
                g_new = cd * g_next + _dot_tn((qm.astype(F32) * wq).astype(BF16), dob)
                return g_new, dlg

            ds0, dlg = lax.fori_loop(0, n_chunk, grad_step, (jnp.zeros((128, 128), F32), jnp.zeros((1, 1), F32)))
            ds0b = ds0.astype(BF16)
            dkc_part = wc * _dot_nt(vcb, ds0b)
            dkc_ref[...] += dkc_part
            dvc_part = _dot((kcm * wc).astype(BF16), ds0b)
            dlg = dlg + jnp.sum(e_c * jnp.sum(kcm * dkc_part, axis=-1, keepdims=True), keepdims=True)
            return dvc_part, dlg

        dvc_f, dlg_f = run(True)
        dvc_b, dlg_b = run(False)
        dvc_ref[...] = dvc_f + dvc_b
        st_ref[0] = jnp.concatenate([
            dgain, jnp.broadcast_to(dlg_f, (1, 128)), jnp.broadcast_to(dlg_b, (1, 128)), jnp.zeros((5, 128), F32)], axis=0)

    qk, v, kc, vc = _ret_specs(t_lat, seq, ctx_len)
    dy_spec = v
    return _hosted_call(
        body, (rq, rk, rv, rk, rv, pg, osum, dmix, lg, g_ret), name="ret_bwd", grid=(nex, HEADS),
        in_specs=[qk, qk, v, kc, vc, v, v, dy_spec, pl.BlockSpec(memory_space=pltpu.SMEM),
                  pl.BlockSpec((1, 128), lambda b, h: (0, h))],
        out_specs=[
            qk, qk, v,
            pl.BlockSpec((ctx_len, 128), lambda b, h: (b, h // 2)),
            pl.BlockSpec((ctx_len, 128), lambda b, h: (b, h)),
            v,
            pl.BlockSpec((1, 8, 128), lambda b, h: (b, 0, h)),
        ],
        out_shape=[
            jax.ShapeDtypeStruct((t_lat, 256), F32), jax.ShapeDtypeStruct((t_lat, 256), F32),
            jax.ShapeDtypeStruct((t_lat, 512), F32), jax.ShapeDtypeStruct((t_ctx, 256), F32),
            jax.ShapeDtypeStruct((t_ctx, 512), F32), jax.ShapeDtypeStruct((t_lat, 512), F32),
            jax.ShapeDtypeStruct((nex, 8, 512), F32),
        ],
        scratch_shapes=[pltpu.VMEM((seq, 128), F32), pltpu.VMEM((n_chunk, 128, 128), F32)],
        sem=("parallel", "arbitrary"), rider=rider)


def _matmul_tn(a, b, *, bm, bn, bk, chip_major, name):
    tk, m = a.shape
    n = b.shape[1]
    slab = n // N_CHIPS
    per_block = bn // slab if chip_major else 1
    bk = max(c for c in range(LANES, min(bk, tk) + 1, LANES) if tk % c == 0)

    def body(a_ref, b_ref, o_ref):
        k = pl.program_id(2)
        if chip_major:
            parts = [_dot_tn(a_ref[...], b_ref[:, s * slab:(s + 1) * slab]) for s in range(per_block)]
        else:
            parts = [_dot_tn(a_ref[...], b_ref[...])]

        @pl.when(k == 0)
        def _():
            for s, part in enumerate(parts):
                if chip_major:
                    o_ref[s] = part
                else:
                    o_ref[...] = part

        @pl.when(k > 0)
        def _():
            for s, part in enumerate(parts):
                if chip_major:
                    o_ref[s] += part
                else:
                    o_ref[...] += part

    if chip_major:
        out_spec = pl.BlockSpec((per_block, bm, slab), lambda i, j, k: (j, i, 0))
        out_shape = jax.ShapeDtypeStruct((N_CHIPS, m, slab), F32)
    else:
        out_spec = pl.BlockSpec((bm, bn), lambda i, j, k: (i, j))
        out_shape = jax.ShapeDtypeStruct((m, n), F32)
    return pl.pallas_call(
        body, name=name, grid=(m // bm, n // bn, tk // bk),
        in_specs=[pl.BlockSpec((bk, bm), lambda i, j, k: (k, i)), pl.BlockSpec((bk, bn), lambda i, j, k: (k, j))],
        out_specs=out_spec, out_shape=out_shape,
        compiler_params=_params(("parallel", "parallel", "arbitrary"), VMEM_LIMIT),
    )(a, b)


_LATE = ("w_out", "w_ff1", "w_ff2")
_EARLY = ("w_in", "w_uq", "w_ukv")


def _local_step(x, ctx, tgt, modv, lg, g_attn, g_ffn, g_fin, g_ret, g_q, g_kv, w_in, w_uq, w_ukv, late, place=None,
                *, tm=256, tq=256, chunk=256):
    nex, seq, d = x.shape
    ctx_len = ctx.shape[1]
    t_lat = nex * seq
    x2 = x.reshape(t_lat, d)
    ctx2 = ctx.reshape(nex * ctx_len, d)
    tgt2 = tgt.reshape(t_lat, d)
    cos_t, sin_t = _rope_tables(seq, tm)
    dims = dict(t_lat=t_lat, seq=seq, ctx_len=ctx_len)
    alone = place is None

    (hb, pg, rq, rk, rv, nq, nkv, q, k, v), crossed2 = _pre_fwd(
        x2, ctx2, modv, g_attn, w_in, g_q, g_kv, w_uq, w_ukv, cos_t, sin_t, seq=seq, tm=tm,
        rider=None if alone else _gather_ici_rider([late[2]]))
    (yret, osum), crossed1 = _ret_fwd_pair(rq, rk, rv, pg, lg, g_ret, chunk=chunk, **dims,
                                           rider=None if alone else _gather_ici_rider([late[1]]))
    (ymla, lse), gathered = _mla_fwd(
        q, k, v, tq=tq, **dims,
        rider=None if alone else _merge_riders(_gather_rider([late[0]]), _gather_d2d_rider(crossed1 + crossed2)))
    w_out, w_ff1, w_ff2 = late if alone else gathered
    mix, act, du, h2, df, dmo, dmix, dxm, st_post = _post(yret, ymla, x2, tgt2, modv, g_ffn, g_fin, w_out.reshape(d, d),
                                                         w_ff1, w_ff2.reshape(D_FF, d), seq=seq, tm=tm)
    bk = 512
    g_late = [
        _matmul_tn(mix, dmo, bm=1024, bn=1024, bk=1024, chip_major=False, name="gw_out").reshape(N_CHIPS, d // N_CHIPS, d),
        _matmul_tn(h2, du, bm=1024, bn=1024, bk=1024, chip_major=True, name="gw_ff1"),
        _matmul_tn(act, df, bm=1024, bn=1024, bk=1024, chip_major=False, name="gw_ff2").reshape(N_CHIPS, D_FF // N_CHIPS, d),
    ]
    (dq_m, dkl, dkc, dvl, dvc), got = _mla_bwd(q, k, v, ymla, lse, dmix, tq=tq, **dims,
                                               rider=None if alone else _exchange_rider(g_late))
    if not alone:
        core, slot = place
        part = [_add_half(g, r, core, "add_half_" + n) for g, r, n in zip(g_late, got, _LATE)]
    (drq, drk, drv, dkc_r, dvc_r, drg, st_ret), landed = _ret_bwd_pair(
        rq, rk, rv, pg, osum, dmix, lg, g_ret, chunk=chunk, **dims, rider=None if alone else _scatter_rider(part))
    if not alone:
        mine = [_sum_chips(p, l, slot, "sum_chips_" + n) for p, l, n in zip(part, landed, _LATE)]
    (dpb, dqf, dkvf, gx, st_pre), _ = _pre_bwd(
        x2, ctx2, modv, g_attn, pg, drq, drk, dkc_r, drv, dvc_r, drg, dq_m, dkl, dkc, dvl, dvc, dxm, w_in, g_q, g_kv,
        w_uq, w_ukv, cos_t, sin_t, seq=seq, tm=tm)
    g_early = [
        _matmul_tn(dpb, hb, bm=IN_PAD // 2, bn=d, bk=512, chip_major=False, name="gw_in"),
        _matmul_tn(dqf, nq, bm=HEADS * MLA_HEAD, bn=Q_LORA, bk=1536, chip_major=False, name="gw_uq"),
        _matmul_tn(nkv, dkvf, bm=KV_LORA, bn=HEADS * 256, bk=1536, chip_major=True, name="gw_ukv"),
    ]
    late_out = g_late if alone else mine
    return gx.reshape(nex, seq, d), g_early, late_out, st_post, st_ret, st_pre


_ANY = pl.BlockSpec(memory_space=pl.ANY)
_VMEM = pl.BlockSpec(memory_space=pltpu.VMEM)
_OFFSETS = tuple((dx, dy, dc) for dx in (0, 1) for dy in (0, 1) for dc in (0, 1))[1:]
_CHIP_OFFSETS = ((1, 0), (0, 1), (1, 1))


def _place():
    return lax.axis_index("x"), lax.axis_index("y"), lax.axis_index("c")


def _flip(v, d):
    return 1 - v if d else v


def _gather8_rider(a):
    def copies(a_ref, o_ref, send, recv):
        x, y, z = _place()
        me = 4 * x + 2 * y + z
        out = []
        for k, (dx, dy, dc) in enumerate(_OFFSETS):
            peer = (_flip(x, dx), _flip(y, dy), _flip(z, dc))
            landing = o_ref.at[4 * peer[0] + 2 * peer[1] + peer[2]]
            out.append((
                pltpu.make_async_remote_copy(src_ref=a_ref, dst_ref=o_ref.at[me], send_sem=send.at[k],
                                             recv_sem=recv.at[k], device_id=peer, device_id_type=MESH),
                pltpu.make_async_remote_copy(src_ref=a_ref, dst_ref=landing, send_sem=send.at[k],
                                             recv_sem=recv.at[k], device_id=peer, device_id_type=MESH)))
        return me, out

    def start(ins, outs, sems):
        me, cps = copies(ins[0], outs[0], sems[0], sems[1])
        pltpu.make_async_copy(ins[0], outs[0].at[me], sems[2]).start()
        for out_cp, _ in cps:
            out_cp.start()

    def finish(ins, outs, sems):
        me, cps = copies(ins[0], outs[0], sems[0], sems[1])
        for out_cp, in_cp in cps:
            in_cp.wait_recv()
            out_cp.wait_send()
        pltpu.make_async_copy(ins[0], outs[0].at[me], sems[2]).wait()

    return _Rider([a], [jax.ShapeDtypeStruct((N_DEV,) + a.shape, a.dtype)],
                  [pltpu.SemaphoreType.DMA((7,)), pltpu.SemaphoreType.DMA((7,)), pltpu.SemaphoreType.DMA],
                  start, finish, in_specs=[_VMEM], out_specs=[_VMEM])


def _merge_riders(*riders):
    ins, outs, sems, in_specs, out_specs, aliases, cuts = [], [], [], [], [], {}, []
    for r in riders:
        cuts.append((len(ins), len(outs), len(sems)))
        aliases.update({len(ins) + i: len(outs) + j for i, j in r.aliases.items()})
        ins += r.ins
        outs += r.out_shapes
        sems += r.sems
        in_specs += r.in_specs
        out_specs += r.out_specs

    def part(r, cut, r_ins, r_outs, r_sems):
        return (r_ins[cut[0]:cut[0] + len(r.ins)], r_outs[cut[1]:cut[1] + len(r.out_shapes)],
                r_sems[cut[2]:cut[2] + len(r.sems)])

    def start(r_ins, r_outs, r_sems):
        for r, cut in zip(riders, cuts):
            r.start(*part(r, cut, r_ins, r_outs, r_sems))

    def finish(r_ins, r_outs, r_sems):
        for r, cut in zip(riders, cuts):
            r.finish(*part(r, cut, r_ins, r_outs, r_sems))

    return _Rider(ins, outs, sems, start, finish, aliases=aliases, in_specs=in_specs, out_specs=out_specs)


def _allgather8(a, name):
    return _run_rider(_gather8_rider(a), name)[0]


BF16_TILE_ROWS = 16


def _half(o, slot, which):
    r2 = o.shape[1] // 2
    if r2 % BF16_TILE_ROWS == 0:
        return o.at[slot, pl.ds(which * r2, r2)]
    c2 = o.shape[2] // 2
    assert c2 % LANES == 0
    return o.at[slot, :, pl.ds(which * c2, c2)]


def _gather_send(o_refs, send, recv):
    x, y, z = _place()
    chip = 2 * x + y
    for a, o in enumerate(o_refs):
        r2 = o.shape[1] // 2
        mine = _half(o, chip, z)
        for k, (dx, dy) in enumerate(_CHIP_OFFSETS):
            pltpu.make_async_remote_copy(
                src_ref=mine, dst_ref=mine, send_sem=send.at[a, k], recv_sem=recv.at[a, k],
                device_id=(_flip(x, dx), _flip(y, dy), z), device_id_type=MESH).start()


def _gather_landed(o_refs, send, recv, then=None):
    x, y, z = _place()
    chip = 2 * x + y
    for a, o in enumerate(o_refs):
        for k, (dx, dy) in enumerate(_CHIP_OFFSETS):
            landed = _half(o, 2 * _flip(x, dx) + _flip(y, dy), z)
            pltpu.make_async_remote_copy(
                src_ref=landed, dst_ref=landed, send_sem=send.at[a, k], recv_sem=recv.at[a, k],
                device_id=(_flip(x, dx), _flip(y, dy), z), device_id_type=MESH).wait_recv()
            if then is not None:
                then(a, k, landed)
    for a, o in enumerate(o_refs):
        mine = _half(o, chip, z)
        for k, (dx, dy) in enumerate(_CHIP_OFFSETS):
            pltpu.make_async_remote_copy(
                src_ref=mine, dst_ref=mine, send_sem=send.at[a, k], recv_sem=recv.at[a, k],
                device_id=(_flip(x, dx), _flip(y, dy), z), device_id_type=MESH).wait_send()


def _pass_on(o_refs, fsend, frecv, a, k, landed):
    x, y, z = _place()
    pltpu.make_async_remote_copy(
        src_ref=landed, dst_ref=landed, send_sem=fsend.at[a, k], recv_sem=frecv.at[a, k],
        device_id=(x, y, 1 - z), device_id_type=MESH).start()


def _passed_on(o_refs, fsend, frecv):
    x, y, z = _place()
    for a, o in enumerate(o_refs):
        for k, (dx, dy) in enumerate(_CHIP_OFFSETS):
            other = 2 * _flip(x, dx) + _flip(y, dy)
            got = _half(o, other, 1 - z)
            gave = _half(o, other, z)
            pltpu.make_async_remote_copy(
                src_ref=got, dst_ref=got, send_sem=fsend.at[a, k], recv_sem=frecv.at[a, k],
                device_id=(x, y, 1 - z), device_id_type=MESH).wait_recv()
            pltpu.make_async_remote_copy(
                src_ref=gave, dst_ref=gave, send_sem=fsend.at[a, k], recv_sem=frecv.at[a, k],
                device_id=(x, y, 1 - z), device_id_type=MESH).wait_send()


def _gather_finish(o_refs, send, recv, fsend, frecv):
    _gather_landed(o_refs, send, recv, functools.partial(_pass_on, o_refs, fsend, frecv))
    _passed_on(o_refs, fsend, frecv)


class _Rider:
    def __init__(self, ins, out_shapes, sems, start, finish, aliases=None, in_specs=None, out_specs=None):
        self.ins, self.out_shapes, self.sems = list(ins), list(out_shapes), list(sems)
        self.start, self.finish, self.aliases = start, finish, dict(aliases or {})
        self.in_specs = list(in_specs) if in_specs else [_ANY] * len(self.ins)
        self.out_specs = list(out_specs) if out_specs else [_ANY] * len(self.out_shapes)


def _run_rider(rider, name):
    r_in, r_out = len(rider.ins), len(rider.out_shapes)

    def body(*refs):
        ins, outs, sems = refs[:r_in], refs[r_in:r_in + r_out], refs[r_in + r_out:]
        rider.start(ins, outs, sems)
        rider.finish(ins, outs, sems)

    return pl.pallas_call(
        body, name=name, in_specs=rider.in_specs, out_specs=rider.out_specs, out_shape=rider.out_shapes,
        input_output_aliases=rider.aliases, scratch_shapes=rider.sems,
    )(*rider.ins)


def _hosted_call(body, args, *, name, grid, in_specs, out_specs, out_shape, scratch_shapes=(), sem, rider=None):
    scratch_shapes = list(scratch_shapes)
    if rider is None:
        res = pl.pallas_call(
            body, name=name, grid=grid, in_specs=in_specs, out_specs=out_specs, out_shape=out_shape,
            scratch_shapes=scratch_shapes, compiler_params=_params(sem, VMEM_LIMIT))(*args)
        return list(res), []
    n_in, n_out, n_sc = len(in_specs), len(out_specs), len(scratch_shapes)
    r_in, r_out = len(rider.ins), len(rider.out_shapes)
    last = tuple(g - 1 for g in grid)

    def hosted(*refs):
        p = 0
        parts = []
        for cnt in (n_in, r_in, n_out, r_out, n_sc):
            parts.append(refs[p:p + cnt])
            p += cnt
        ins, r_ins, outs, r_outs, scratch = parts
        sems = refs[p:]
        ids = [pl.program_id(a) for a in range(len(grid))]
        is_first = functools.reduce(jnp.logical_and, [i == 0 for i in ids])
        is_last = functools.reduce(jnp.logical_and, [i == e for i, e in zip(ids, last)])

        @pl.when(is_first)
        def _():
            rider.start(r_ins, r_outs, sems)

        body(*ins, *outs, *scratch)

        @pl.when(is_last)
        def _():
            rider.finish(r_ins, r_outs, sems)

    res = pl.pallas_call(
        hosted, name=name, grid=grid, in_specs=list(in_specs) + rider.in_specs, out_specs=list(out_specs) + rider.out_specs,
        out_shape=list(out_shape) + rider.out_shapes, scratch_shapes=scratch_shapes + rider.sems,
        input_output_aliases={n_in + i: n_out + j for i, j in rider.aliases.items()},
        compiler_params=_params(("arbitrary",) * len(grid), VMEM_LIMIT))(*args, *rider.ins)
    return list(res[:n_out]), list(res[n_out:])


def _gather_rider(ws):
    n = len(ws)
    return _Rider(
        ws, [jax.ShapeDtypeStruct(w.shape, w.dtype) for w in ws], [pltpu.SemaphoreType.DMA((n, 3))] * 4,
        lambda ins, outs, sems: _gather_send(outs, sems[0], sems[1]),
        lambda ins, outs, sems: _gather_finish(outs, *sems),
        aliases={a: a for a in range(n)})


def _gather_ici_rider(ws):
    n = len(ws)
    return _Rider(
        ws, [jax.ShapeDtypeStruct(w.shape, w.dtype) for w in ws], [pltpu.SemaphoreType.DMA((n, 3))] * 2,
        lambda ins, outs, sems: _gather_send(outs, sems[0], sems[1]),
        lambda ins, outs, sems: _gather_landed(outs, sems[0], sems[1]),
        aliases={a: a for a in range(n)})


def _gather_d2d_rider(ws):
    n = len(ws)

    def start(ins, outs, sems):
        x, y, z = _place()
        for a, o in enumerate(outs):
            for k, (dx, dy) in enumerate(_CHIP_OFFSETS):
                _pass_on(outs, sems[0], sems[1], a, k, _half(o, 2 * _flip(x, dx) + _flip(y, dy), z))

    return _Rider(
        ws, [jax.ShapeDtypeStruct(w.shape, w.dtype) for w in ws], [pltpu.SemaphoreType.DMA((n, 3))] * 2,
        start, lambda ins, outs, sems: _passed_on(outs, sems[0], sems[1]), aliases={a: a for a in range(n)})


def _copies_rider(ins, out_shapes, sem_shape, make):
    def start(r_ins, r_outs, sems):
        for cp in make(r_ins, r_outs, sems[0], sems[1]):
            cp.start()

    def finish(r_ins, r_outs, sems):
        for cp in make(r_ins, r_outs, sems[0], sems[1]):
            cp.wait()

    return _Rider(ins, out_shapes, [pltpu.SemaphoreType.DMA(sem_shape)] * 2, start, finish)


def _exchange_rider(gs):
    def make(g_refs, r_refs, send, recv):
        x, y, z = _place()
        return [pltpu.make_async_remote_copy(
            src_ref=g.at[:, pl.ds((1 - z) * (g.shape[1] // 2), g.shape[1] // 2)], dst_ref=r, send_sem=send.at[a],
            recv_sem=recv.at[a], device_id=(x, y, 1 - z), device_id_type=MESH)
            for a, (g, r) in enumerate(zip(g_refs, r_refs))]

    shapes = [jax.ShapeDtypeStruct((g.shape[0], g.shape[1] // 2, g.shape[2]), g.dtype) for g in gs]
    return _copies_rider(gs, shapes, (len(gs),), make)


def _add_half(g, recv, core, name):
    s, r, c = g.shape
    r2 = r // 2
    rb = r2
    for cand in (256, 128, 64):
        if r2 % cand == 0:
            rb = cand
            break
    g4 = g.reshape(s, 2, r2, c)

    def body(core_ref, g_ref, r_ref, o_ref):
        o_ref[...] = (g_ref[...] + r_ref[...]).astype(BF16)

    return pl.pallas_call(
        body, name=name,
        grid_spec=pltpu.PrefetchScalarGridSpec(
            num_scalar_prefetch=1, grid=(s, r2 // rb),
            in_specs=[pl.BlockSpec((None, None, rb, c), lambda i, j, cr: (i, cr[0], j, 0)),
                      pl.BlockSpec((None, rb, c), lambda i, j, cr: (i, j, 0))],
            out_specs=pl.BlockSpec((None, rb, c), lambda i, j, cr: (i, j, 0))),
        out_shape=jax.ShapeDtypeStruct((s, r2, c), BF16),
        compiler_params=_params(("parallel", "parallel")),
    )(core, g4, recv)


def _scatter_rider(ps):
    def make(p_refs, o_refs, send, recv):
        x, y, z = _place()
        copies = []
        for a, (p, o) in enumerate(zip(p_refs, o_refs)):
            for k, (dx, dy) in enumerate(_CHIP_OFFSETS):
                other = 2 * _flip(x, dx) + _flip(y, dy)
                copies.append(pltpu.make_async_remote_copy(
                    src_ref=p.at[other], dst_ref=o.at[k], send_sem=send.at[a, k], recv_sem=recv.at[a, k],
                    device_id=(_flip(x, dx), _flip(y, dy), z), device_id_type=MESH))
        return copies

    shapes = [jax.ShapeDtypeStruct((3,) + p.shape[1:], p.dtype) for p in ps]
    return _copies_rider(ps, shapes, (len(ps), 3), make)


def _sum_chips(p, landed, chip, name):
    _, r2, c = p.shape
    rb = r2
    for cand in (256, 128, 64):
        if r2 % cand == 0:
            rb = cand
            break

    def body(s_ref, p_ref, l_ref, o_ref):
        acc = p_ref[...].astype(F32)
        for k in range(3):
            acc = acc + l_ref[k].astype(F32)
        o_ref[...] = acc

    return pl.pallas_call(
        body, name=name,
        grid_spec=pltpu.PrefetchScalarGridSpec(
            num_scalar_prefetch=1, grid=(r2 // rb,),
            in_specs=[pl.BlockSpec((None, rb, c), lambda i, s: (s[0], i, 0)),
                      pl.BlockSpec((3, rb, c), lambda i, s: (0, i, 0))],
            out_specs=pl.BlockSpec((rb, c), lambda i, s: (i, 0))),
        out_shape=jax.ShapeDtypeStruct((r2, c), F32),
        compiler_params=_params(("parallel",)),
    )(chip, p, landed)


def _swap_rider(hs):
    def make(h_refs, o_refs, send, recv):
        x, y, z = _place()
        return [pltpu.make_async_remote_copy(
            src_ref=h, dst_ref=o, send_sem=send.at[a], recv_sem=recv.at[a], device_id=(x, y, 1 - z),
            device_id_type=MESH) for a, (h, o) in enumerate(zip(h_refs, o_refs))]

    return _copies_rider(hs, [jax.ShapeDtypeStruct(h.shape, h.dtype) for h in hs], (len(hs),), make)


SMALL_ROWS = 32
PACK_ROWS = 16


def _pack_small(st_post, st_ret, st_pre):
    d = st_post.shape[2]

    def body(po_ref, re_ref, pr_ref, o_ref):
        o_ref[...] = jnp.zeros(o_ref.shape, F32)
        o_ref[0:1, :] = pr_ref[0, 2:3, :] + pr_ref[1, 2:3, :] + pr_ref[2, 2:3, :]
        o_ref[1:2, :] = po_ref[0, 4:5, :] + po_ref[1, 4:5, :]
        o_ref[2:3, :] = po_ref[0, 5:6, :] + po_ref[1, 5:6, :]
        o_ref[3:4, 0:512] = re_ref[0, 0:1, :] + re_ref[1, 0:1, :]
        o_ref[4:5, :] = pr_ref[0, 3:4, :] + pr_ref[1, 3:4, :] + pr_ref[2, 3:4, :]
        o_ref[5:6, :] = pr_ref[0, 4:5, :] + pr_ref[1, 4:5, :] + pr_ref[2, 4:5, :]
        lane = lax.broadcasted_iota(jnp.int32, (1, LANES), 1)
        for row, src in ((6, 1), (10, 2)):
            acc = jnp.zeros((1, LANES), F32)
            for hd in range(HEADS):
                grp = re_ref[0, src:src + 1, hd * LANES:(hd + 1) * LANES] + re_ref[1, src:src + 1, hd * LANES:(hd + 1) * LANES]
                acc = acc + jnp.where(lane == hd, grp, 0.0)
            o_ref[row:row + 1, 0:LANES] = acc
        o_ref[7:8, :] = po_ref[0, 6:7, :] + po_ref[1, 6:7, :]
        o_ref[8:9, :] = pr_ref[2, 0:1, :]
        o_ref[9:10, :] = pr_ref[2, 1:2, :]
        for e in range(2):
            b = 12 + 6 * e
            o_ref[b:b + 1, :] = pr_ref[e, 0:1, :]
            o_ref[b + 1:b + 2, :] = pr_ref[e, 1:2, :]
            o_ref[b + 2:b + 3, :] = po_ref[e, 3:4, :]
            o_ref[b + 3:b + 4, :] = po_ref[e, 0:1, :]
            o_ref[b + 4:b + 5, :] = po_ref[e, 1:2, :]
            o_ref[b + 5:b + 6, :] = po_ref[e, 2:3, :]

    return pl.pallas_call(body, name="pack_small", out_shape=jax.ShapeDtypeStruct((SMALL_ROWS, d), F32))(st_post, st_ret, st_pre)


def _small_reduce(gathered):
    d = gathered.shape[2]

    def body(g_ref, o_ref):
        tot = g_ref[0, 0:PACK_ROWS, :]
        for dev in range(1, N_DEV):
            tot = tot + g_ref[dev, 0:PACK_ROWS, :]
        o_ref[0:PACK_ROWS, :] = tot
        for j in range(6):
            acc = g_ref[0, 12 + j:13 + j, :] + g_ref[0, 18 + j:19 + j, :]
            for dev in range(1, N_DEV):
                acc = acc + g_ref[dev, 12 + j:13 + j, :] + g_ref[dev, 18 + j:19 + j, :]
            if j < 2:
                acc = acc + o_ref[8 + j:9 + j, :]
            o_ref[PACK_ROWS + j:PACK_ROWS + j + 1, :] = acc
        o_ref[PACK_ROWS + 6:PACK_ROWS + 8, :] = jnp.zeros((2, d), F32)

    return pl.pallas_call(body, name="small_reduce", out_shape=jax.ShapeDtypeStruct((PACK_ROWS + 8, d), F32))(gathered)


_SMALL = (("g_attn", 0, 1024), ("g_ffn", 1, 1024), ("g_final", 2, 1024), ("g_ret", 3, 512), ("g_q_lora", 4, 384),
          ("g_kv_lora", 5, 256), ("ret_decay_fwd", 6, HEADS), ("ret_decay_bwd", 10, HEADS))
_SMALL_NAMES = tuple(s[0] for s in _SMALL) + ("c_ctx", "b_ada")


def _small_final(tot, dcc, sg8, ws, ms, vs):
    d = tot.shape[1]
    n = len(_SMALL_NAMES)

    def body(*refs):
        t_ref, dcc_ref, sg_ref = refs[0:3]
        w_refs, m_refs, v_refs = refs[3:3 + n], refs[3 + n:3 + 2 * n], refs[3 + 2 * n:3 + 3 * n]
        outs = refs[3 + 3 * n:]
        g_refs, d_refs, mo_refs, vo_refs = outs[0:n], outs[n:2 * n], outs[2 * n:3 * n], outs[3 * n:4 * n]
        l_ref = outs[4 * n]

        def update(i, g, sl=None):
            pick = (lambda r: r[...]) if sl is None else (lambda r: r[:, sl])
            dl, mn, vn = _adam_math(pick(w_refs[i]), g, pick(m_refs[i]), pick(v_refs[i]))
            if sl is None:
                g_refs[i][...], d_refs[i][...], mo_refs[i][...], vo_refs[i][...] = g, dl, mn, vn
            else:
                g_refs[i][:, sl], d_refs[i][:, sl], mo_refs[i][:, sl], vo_refs[i][:, sl] = g, dl, mn, vn

        for i, (name, row, width) in enumerate(_SMALL):
            g = t_ref[row:row + 1, 0:width]
            if name == "ret_decay_fwd":
                g = g * sg_ref[0:1, 0:width]
            elif name == "ret_decay_bwd":
                g = g * sg_ref[1:2, 0:width]
            update(i, g)
        i_cc, i_b = n - 2, n - 1
        cc = w_refs[i_cc][...]
        s = 1.0 / (1.0 + jnp.exp(-cc))
        dsilu = dcc_ref[0, 0:1, :] + dcc_ref[2, 0:1, :] + dcc_ref[4, 0:1, :] + dcc_ref[6, 0:1, :]
        update(i_cc, dsilu * (s * (1.0 + cc * (1.0 - s))))
        for j in range(6):
            update(i_b, t_ref[PACK_ROWS + j:PACK_ROWS + j + 1, :], pl.ds(j * d, d))
        l_ref[...] = jnp.broadcast_to((0.5 / d) * jnp.sum(t_ref[7:8, :], keepdims=True), l_ref.shape)

    shapes = [jax.ShapeDtypeStruct(a.shape, F32) for a in ws]
    outs = pl.pallas_call(
        body, name="small_final", out_shape=shapes * 4 + [jax.ShapeDtypeStruct((8, LANES), F32)],
    )(tot, dcc, sg8, *ws, *ms, *vs)
    return outs[0:n], outs[n:2 * n], outs[2 * n:3 * n], outs[3 * n:4 * n], outs[4 * n]


_WEIGHTS = ("c_ctx", "w_ada", "b_ada", "g_attn", "g_ffn", "w_in", "ret_decay_fwd", "ret_decay_bwd", "g_ret", "g_q_lora",
            "w_uq", "g_kv_lora", "w_ukv", "w_out", "w_ff1", "w_ff2", "g_final")
_BIG = ("w_in", "w_uq", "w_ukv", "w_out", "w_ff1", "w_ff2")
_TRANSPOSED = ("w_in", "w_uq")


def kernel(x, c, ctx, c_ctx, w_ada, b_ada, g_attn, g_ffn, w_in, ret_decay_fwd, ret_decay_bwd, g_ret, g_q_lora, w_uq, g_kv_lora, w_ukv, w_out, w_ff1, w_ff2, g_final, loss_target, m_c_ctx, m_w_ada, m_b_ada, m_g_attn, m_g_ffn, m_w_in, m_ret_decay_fwd, m_ret_decay_bwd, m_g_ret, m_g_q_lora, m_w_uq, m_g_kv_lora, m_w_ukv, m_w_out, m_w_ff1, m_w_ff2, m_g_final, v_c_ctx, v_w_ada, v_b_ada, v_g_attn, v_g_ffn, v_w_in, v_ret_decay_fwd, v_ret_decay_bwd, v_g_ret, v_g_q_lora, v_w_uq, v_g_kv_lora, v_w_ukv, v_w_out, v_w_ff1, v_w_ff2, v_g_final):
    w = dict(c_ctx=c_ctx, w_ada=w_ada, b_ada=b_ada, g_attn=g_attn, g_ffn=g_ffn, w_in=w_in, ret_decay_fwd=ret_decay_fwd,
             ret_decay_bwd=ret_decay_bwd, g_ret=g_ret, g_q_lora=g_q_lora, w_uq=w_uq, g_kv_lora=g_kv_lora, w_ukv=w_ukv,
             w_out=w_out, w_ff1=w_ff1, w_ff2=w_ff2, g_final=g_final)
    m = dict(c_ctx=m_c_ctx, w_ada=m_w_ada, b_ada=m_b_ada, g_attn=m_g_attn, g_ffn=m_g_ffn, w_in=m_w_in,
             ret_decay_fwd=m_ret_decay_fwd, ret_decay_bwd=m_ret_decay_bwd, g_ret=m_g_ret, g_q_lora=m_g_q_lora, w_uq=m_w_uq,
             g_kv_lora=m_g_kv_lora, w_ukv=m_w_ukv, w_out=m_w_out, w_ff1=m_w_ff1, w_ff2=m_w_ff2, g_final=m_g_final)
    v = dict(c_ctx=v_c_ctx, w_ada=v_w_ada, b_ada=v_b_ada, g_attn=v_g_attn, g_ffn=v_g_ffn, w_in=v_w_in,
             ret_decay_fwd=v_ret_decay_fwd, ret_decay_bwd=v_ret_decay_bwd, g_ret=v_g_ret, g_q_lora=v_g_q_lora, w_uq=v_w_uq,
             g_kv_lora=v_g_kv_lora, w_ukv=v_w_ukv, w_out=v_w_out, w_ff1=v_w_ff1, w_ff2=v_w_ff2, g_final=v_g_final)
    xi, yi, ci = lax.axis_index("x"), lax.axis_index("y"), lax.axis_index("c")
    chip = 2 * xi + yi
    dev = 2 * chip + ci
    nex, seq, d = x.shape
    n_ada = w_ada.shape[2]

    c_all = _allgather8(jnp.pad(c, ((0, 8 - nex), (0, 0))), "ag_c")[:, :nex].reshape(N_DEV * nex, d)
    a_in = jnp.concatenate([c_all, c_ctx.reshape(1, d), jnp.zeros((7, d), F32)], axis=0)
    b_sh = lax.dynamic_slice(b_ada, (0, chip * n_ada), (1, n_ada))
    mod_sh = _mod_fwd(a_in, w_ada[0], b_sh)

    dec = jnp.zeros((8, LANES), F32).at[0, :HEADS].set(ret_decay_fwd[0]).at[1, :HEADS].set(ret_decay_bwd[0])
    lg8, sg8 = _decay_prep(dec)
    lg = lg8[:2, :HEADS]

    def shard_of(t, k):
        return t[k][0].T if k in _TRANSPOSED else t[k][0]

    shard = {k: shard_of(w, k) for k in _BIG}
    head_rows = MLA_NOPE + MLA_ROPE
    shard["w_uq"] = jnp.pad(shard["w_uq"], ((0, MLA_HEAD - head_rows), (0, 0)))
    slot = chip.reshape(1).astype(jnp.int32)
    core = ci.reshape(1).astype(jnp.int32)
    slots = {k: _cast_into_slot(shard[k], slot, "cast_" + k) for k in _BIG}
    mod8, w_in_f, w_uq_k, w_ukv_k = _run_rider(
        _merge_riders(_gather8_rider(mod_sh), _gather_rider([slots[k] for k in _EARLY])), "ag_early")
    w_in_k = jnp.pad(w_in_f.reshape(IN_COLS, d), ((0, IN_PAD - IN_COLS), (0, 0)))
    mod_all = mod8[0::2].transpose(1, 0, 2).reshape(a_in.shape[0], N_CHIPS * n_ada)
    mod_me = lax.dynamic_slice(mod_all, (nex * dev, 0), (nex, N_CHIPS * n_ada)).reshape(nex, 6, d)
    mod_c = mod_all[N_DEV * nex].reshape(1, 6, d)
    modv = jnp.pad(jnp.concatenate([mod_me, mod_c], axis=0), ((0, 0), (0, 2), (0, 0)))

    gx, g_early, late, st_post, st_ret, st_pre = _local_step(
        x, ctx, loss_target, modv, lg, g_attn, g_ffn, g_final.reshape(1, d), g_ret, g_q_lora, g_kv_lora,
        w_in_k, w_uq_k, w_ukv_k, [slots[k] for k in _LATE], (core, slot))

    g4 = [
        g_early[0][:IN_COLS].reshape(N_CHIPS, IN_COLS // N_CHIPS, d),
        g_early[1].reshape(N_CHIPS, MLA_HEAD, Q_LORA)[:, :head_rows],
        g_early[2],
    ]
    *got, gathered = _run_rider(
        _merge_riders(_exchange_rider(g4), _swap_rider(late), _gather8_rider(_pack_small(st_post, st_ret, st_pre))),
        "rs_exchange")
    got, late_theirs = got[:len(g4)], got[len(g4):]
    partial = [_add_half(g, r, core, "add_half_" + k) for g, r, k in zip(g4, got, _EARLY)]
    tot = _small_reduce(gathered)
    dm = jnp.concatenate([
        gathered[:, 12:24].reshape(N_DEV * nex, 6 * d),
        jnp.concatenate([tot[8:10].reshape(1, 2 * d), jnp.zeros((1, 4 * d), F32)], axis=1),
        jnp.zeros((7, 6 * d), F32)], axis=0)
    dm_sh = lax.dynamic_slice(dm, (0, chip * n_ada), (dm.shape[0], n_ada))
    g_ada, da = _mod_bwd(a_in, dm_sh, w_ada[0])
    *landed, dcc = _run_rider(_merge_riders(_scatter_rider(partial), _gather8_rider(da[N_DEV * nex:])), "rs_scatter")
    mine = [_sum_chips(p, l, slot, "sum_chips_" + k) for p, l, k in zip(partial, landed, _EARLY)]
    theirs = _run_rider(_swap_rider(mine), "rs_swap")
    halves = dict(zip(_EARLY, zip(mine, theirs)))
    halves.update(zip(_LATE, zip(late, late_theirs)))
    grad, delta, new_m, new_v = {}, {}, {}, {}
    for k in _BIG:
        a, b = halves[k]
        res = _adamw_halves(shard_of(w, k), a, b, shard_of(m, k), shard_of(v, k), core, "adamw_" + k)
        grad[k], delta[k], new_m[k], new_v[k] = [(o.T if k in _TRANSPOSED else o).reshape(w[k].shape) for o in res]

    shp = w_ada.shape
    outs, _ = _adamw(w_ada[0], g_ada, m["w_ada"][0], v["w_ada"][0], "adamw_w_ada")
    grad["w_ada"] = g_ada.reshape(shp)
    delta["w_ada"], new_m["w_ada"], new_v["w_ada"] = [o.reshape(shp) for o in outs]
    rows = [{k: t[k].reshape(1, -1) for k in _SMALL_NAMES} for t in (w, m, v)]
    small = _small_final(tot, dcc, sg8, *[[t[k] for k in _SMALL_NAMES] for t in rows])
    for res, outs in zip((grad, delta, new_m, new_v), small[:4]):
        for k, o in zip(_SMALL_NAMES, outs):
            res[k] = o.reshape(w[k].shape)
    return (small[4][0, 0], gx, *[grad[k] for k in _WEIGHTS], *[delta[k] for k in _WEIGHTS],
            *[new_m[k] for k in _WEIGHTS], *[new_v[k] for k in _WEIGHTS])
```

```python
import functools
import math

import jax
import jax.numpy as jnp
from jax import lax
from jax.experimental import pallas as pl
from jax.experimental.pallas import tpu as pltpu

F32 = jnp.float32
BF16 = jnp.bfloat16
MESH = pl.DeviceIdType.MESH

EPS = 1e-6
D_MODEL = 1024
D_FF = 4096
HEADS = 4
RET_DK = 64
RET_DV = 128
MLA_NOPE = 128
MLA_ROPE = 64
MLA_HEAD = 256
Q_LORA = 384
KV_LORA = 256
GRID_W = 64
ROPE_BASE = 10000.0
IN_COLS = 2240
IN_PAD = 2304
PG_COLS = 1152
N_CHIPS = 4
N_DEV = 8
LANES = 128
ADAM_LR = 0.001
ADAM_B1 = 0.9
ADAM_B2 = 0.999
ADAM_EPS = 1e-08
ADAM_WD = 0.01
ADAM_STEP = 10
VMEM_LIMIT = 56 * 1024 * 1024


def _dot(a, b):
    return jnp.dot(a, b, preferred_element_type=F32)


def _dot_nt(a, b):
    return lax.dot_general(a, b, (((1,), (1,)), ((), ())), preferred_element_type=F32)


def _dot_tn(a, b):
    return lax.dot_general(a, b, (((0,), (0,)), ((), ())), preferred_element_type=F32)


def _params(sem=None, vmem=None):
    return pltpu.CompilerParams(dimension_semantics=sem, vmem_limit_bytes=vmem)


def _full(shape):
    n = len(shape)
    return pl.BlockSpec(shape, lambda *_: (0,) * n)


def _rope(x, cos, sin):
    w = x.shape[-1]
    lo = (lax.broadcasted_iota(jnp.int32, (1, w), 1) % 64) < 32
    swapped = jnp.where(lo, pltpu.roll(x, w - 32, 1), pltpu.roll(x, 32, 1))
    return x * cos + swapped * sin


def _rope_t(g, cos, sin):
    w = g.shape[-1]
    lo = (lax.broadcasted_iota(jnp.int32, (1, w), 1) % 64) < 32
    t = g * sin
    swapped = jnp.where(lo, pltpu.roll(t, w - 32, 1), pltpu.roll(t, 32, 1))
    return g * cos + swapped


def _rope_tables(seq, tm):
    rows = seq // GRID_W
    row = jnp.repeat(jnp.arange(rows, dtype=F32), GRID_W)
    col = jnp.tile(jnp.arange(GRID_W, dtype=F32), rows)
    n_freq = RET_DK // 4
    freq = ROPE_BASE ** (-jnp.arange(n_freq, dtype=F32) / n_freq)
    ang = jnp.concatenate([row[:, None] * freq, col[:, None] * freq], axis=-1)
    cos, sin = jnp.cos(ang), jnp.sin(ang)
    cos_t = jnp.tile(jnp.concatenate([cos, cos], -1), (1, HEADS))
    sin_t = jnp.tile(jnp.concatenate([-sin, sin], -1), (1, HEADS))
    cos_t = jnp.concatenate([cos_t, jnp.ones((tm, 4 * RET_DK), F32)], 0)
    sin_t = jnp.concatenate([sin_t, jnp.zeros((tm, 4 * RET_DK), F32)], 0)
    return cos_t, sin_t


def _adam_math(w, g, m, v):
    mn = ADAM_B1 * m + (1.0 - ADAM_B1) * g
    vn = ADAM_B2 * v + (1.0 - ADAM_B2) * (g * g)
    m_hat = mn / (1.0 - ADAM_B1 ** ADAM_STEP)
    v_hat = vn / (1.0 - ADAM_B2 ** ADAM_STEP)
    return -ADAM_LR * (m_hat / (jnp.sqrt(v_hat) + ADAM_EPS) + ADAM_WD * w), mn, vn


def _cast_into_slot(w, slot, name, rider=None):
    r, c = w.shape
    rb = max(b for b in range(16, 257, 16) if r % b == 0)

    def body(s_ref, w_ref, o_ref):
        o_ref[...] = w_ref[...].astype(BF16)

    (out,), carried = _hosted_call(
        body, (w,), name=name, grid=(r // rb,), prefetch=(slot,),
        in_specs=[pl.BlockSpec((rb, c), lambda i, s: (i, 0))],
        out_specs=[pl.BlockSpec((None, rb, c), lambda i, s: (s[0], i, 0))],
        out_shape=[jax.ShapeDtypeStruct((N_CHIPS, r, c), BF16)], sem=("parallel",), rider=rider)
    return out, carried


def _adamw_halves(w, mine, theirs, m, v, core, name):
    r, c = w.shape
    r2 = r // 2
    rb = max(b for b in range(8, r2 + 1, 8) if r2 % b == 0 and b * c * 4 <= (1 << 21))
    nbh = r2 // rb

    def body(z_ref, w_ref, a_ref, b_ref, m_ref, v_ref, g_ref, d_ref, mo_ref, vo_ref):
        here = (pl.program_id(0) // nbh) == z_ref[0]
        gg = jnp.where(here, a_ref[...], b_ref[...])
        g_ref[...] = gg
        d_ref[...], mo_ref[...], vo_ref[...] = _adam_math(w_ref[...], gg, m_ref[...], v_ref[...])

    spec = pl.BlockSpec((rb, c), lambda i, z: (i, 0))
    a_spec = pl.BlockSpec((rb, c), lambda i, z: (jnp.clip(i - z[0] * nbh, 0, nbh - 1), 0))
    b_spec = pl.BlockSpec((rb, c), lambda i, z: (jnp.clip(i - (1 - z[0]) * nbh, 0, nbh - 1), 0))
    shp = jax.ShapeDtypeStruct((r, c), F32)
    return pl.pallas_call(
        body, name=name,
        grid_spec=pltpu.PrefetchScalarGridSpec(
            num_scalar_prefetch=1, grid=(r // rb,), in_specs=[spec, a_spec, b_spec, spec, spec], out_specs=[spec] * 4),
        out_shape=[shp] * 4,
        compiler_params=_params(("parallel",)),
    )(core, w, mine, theirs, m, v)


def _adamw(w, g, m, v, name, rider=None):
    r, c = w.shape
    rb = r
    for cand in (256, 128, 64, 32, 16, 8):
        if r % cand == 0 and cand * c * 4 <= (1 << 20):
            rb = cand
            break
    if r * c * 4 <= (1 << 20):
        rb = r

    def body(w_ref, g_ref, m_ref, v_ref, d_ref, mo_ref, vo_ref):
        d_ref[...], mo_ref[...], vo_ref[...] = _adam_math(w_ref[...], g_ref[...], m_ref[...], v_ref[...])

    spec = pl.BlockSpec((rb, c), lambda i: (i, 0))
    shp = jax.ShapeDtypeStruct((r, c), F32)
    return _hosted_call(
        body, (w, g, m, v), name=name, grid=(r // rb,), in_specs=[spec] * 4, out_specs=[spec] * 3, out_shape=[shp] * 3,
        sem=("parallel",), rider=rider)


def _adamw_halves_group(items, core, name, rider=None):
    c = items[0][0].shape[1]
    rb = 128
    n = len(items)
    nbs = [it[0].shape[0] // rb for it in items]
    starts = [sum(nbs[:s]) for s in range(n)]

    def body(z_ref, *refs):
        ins, outs = refs[:5 * n], refs[5 * n:]
        i = pl.program_id(0)
        for s in range(n):
            w_ref, a_ref, b_ref, m_ref, v_ref = ins[5 * s:5 * s + 5]
            g_ref, d_ref, mo_ref, vo_ref = outs[4 * s:4 * s + 4]

            @pl.when(jnp.logical_and(i >= starts[s], i < starts[s] + nbs[s]))
            def _():
                here = ((i - starts[s]) // (nbs[s] // 2)) == z_ref[0]
                gg = jnp.where(here, a_ref[...], b_ref[...])
                g_ref[...] = gg
                d_ref[...], mo_ref[...], vo_ref[...] = _adam_math(w_ref[...], gg, m_ref[...], v_ref[...])

    in_specs, out_specs, out_shape, args = [pl.BlockSpec(memory_space=pltpu.SMEM)], [], [], [core]
    for (w, a, b, m, v), nb, st in zip(items, nbs, starts):
        full = pl.BlockSpec((rb, c), lambda i, nb=nb, st=st: (jnp.clip(i - st, 0, nb - 1), 0))
        half = pl.BlockSpec((rb, c), lambda i, nb=nb, st=st: (jnp.clip(i - st, 0, nb - 1) % (nb // 2), 0))
        in_specs += [full, half, half, full, full]
        out_specs += [full] * 4
        out_shape += [jax.ShapeDtypeStruct(w.shape, F32)] * 4
        args += [w, a, b, m, v]
    res, carried = _hosted_call(body, args, name=name, grid=(sum(nbs),), in_specs=in_specs, out_specs=out_specs,
                                out_shape=out_shape, sem=("arbitrary",), rider=rider)
    return [tuple(res[4 * s:4 * s + 4]) for s in range(n)], carried


def _decay_prep(dec):
    def body(d_ref, lg_ref, sg_ref):
        d = d_ref[...]
        lg_ref[...] = jnp.minimum(d, 0.0) - jnp.log(1.0 + jnp.exp(-jnp.abs(d)))
        sg_ref[...] = 1.0 / (1.0 + jnp.exp(d))

    shp = jax.ShapeDtypeStruct(dec.shape, F32)
    return pl.pallas_call(body, name="decay_prep", out_shape=[shp, shp])(dec)


def _mod_fwd(a_in, w_ada, b_sh):
    rows, d = a_in.shape
    n = w_ada.shape[1]
    bn = 512

    def body(a_ref, w_ref, b_ref, o_ref):
        a = a_ref[...]
        s = (a / (1.0 + jnp.exp(-a))).astype(BF16)
        o_ref[...] = _dot(s, w_ref[...].astype(BF16)) + b_ref[...]

    return pl.pallas_call(
        body, name="mod_fwd", grid=(n // bn,),
        in_specs=[_full((rows, d)), pl.BlockSpec((d, bn), lambda j: (0, j)), pl.BlockSpec((1, bn), lambda j: (0, j))],
        out_specs=pl.BlockSpec((rows, bn), lambda j: (0, j)),
        out_shape=jax.ShapeDtypeStruct((rows, n), F32),
        compiler_params=_params(("parallel",)),
    )(a_in, w_ada, b_sh)


def _mod_bwd(a_in, dm, w_ada):
    rows, d = a_in.shape
    n = w_ada.shape[1]
    bn = 512
    nb = n // bn

    def body(a_ref, dm_ref, w_ref, gw_ref, da_ref):
        j = pl.program_id(0)
        a = a_ref[...]
        s = (a / (1.0 + jnp.exp(-a))).astype(BF16)
        dmb = dm_ref[...].astype(BF16)
        gw_ref[...] = _dot_tn(s, dmb)
        part = _dot_nt(dmb, w_ref[...].astype(BF16))

        @pl.when(j == 0)
        def _():
            da_ref[...] = part

        @pl.when(j > 0)
        def _():
            da_ref[...] += part

    return pl.pallas_call(
        body, name="mod_bwd", grid=(nb,),
        in_specs=[_full((rows, d)), pl.BlockSpec((rows, bn), lambda j: (0, j)), pl.BlockSpec((d, bn), lambda j: (0, j))],
        out_specs=[pl.BlockSpec((d, bn), lambda j: (0, j)), _full((rows, d))],
        out_shape=[jax.ShapeDtypeStruct((d, n), F32), jax.ShapeDtypeStruct((rows, d), F32)],
        compiler_params=_params(("arbitrary",)),
    )(a_in, dm, w_ada)


def _pre_fwd(x2, ctx2, modv, g_attn, w_in, g_q, g_kv, w_uq, w_ukv, cos_t, sin_t, *, seq, tm, rider=None):
    t_lat, d = x2.shape
    t_ctx = ctx2.shape[0]
    nl, nc = t_lat // tm, t_ctx // tm
    n_all = t_lat + t_ctx
    tpe = seq // tm
    nex = t_lat // seq

    def body(x_ref, c_ref, mod_ref, g_ref, win_ref, gq_ref, gkv_ref, wuq_ref, wukv_ref, cos_ref, sin_ref,
             h_ref, pg_ref, rq_ref, rk_ref, rv_ref, nq_ref, nkv_ref, q_ref, k_ref, v_ref):
        i = pl.program_id(0)
        xt = jnp.where(i < nl, x_ref[...], c_ref[...])
        sh = mod_ref[0, 0:1, :]
        sc = mod_ref[0, 1:2, :]
        r = lax.rsqrt(jnp.mean(xt * xt, axis=-1, keepdims=True) + EPS)
        hb = ((xt * r) * g_ref[...] * (1.0 + sc) + sh).astype(BF16)
        h_ref[...] = hb
        p = _dot_nt(hb, win_ref[...])
        cos = cos_ref[...]
        sin = sin_ref[...]
        rq_ref[...] = _rope(p[:, 0:256], cos, sin).astype(BF16)
        rk_ref[...] = _rope(p[:, 256:512] * (RET_DK ** -0.5), cos, sin).astype(BF16)
        rv_ref[...] = p[:, 512:1024].astype(BF16)
        pg_ref[...] = p[:, 1024:2176]
        cq = p[:, 1536:1920]
        ckv = p[:, 1920:2176]
        nqb = (cq * lax.rsqrt(jnp.mean(cq * cq, axis=-1, keepdims=True) + EPS) * gq_ref[...]).astype(BF16)
        nkvb = (ckv * lax.rsqrt(jnp.mean(ckv * ckv, axis=-1, keepdims=True) + EPS) * gkv_ref[...]).astype(BF16)
        nq_ref[...] = nqb
        nkv_ref[...] = nkvb
        cos1 = cos[:, 0:LANES]
        sin1 = sin[:, 0:LANES]
        kpe = _rope(p[:, 2176:2304], cos1, sin1).astype(BF16)
        for hd in range(HEADS):
            o = hd * MLA_HEAD
            qh = _dot_nt(nqb, wuq_ref[hd]) * MLA_SCALE
            q_ref[:, o:o + 128] = qh[:, 0:128].astype(BF16)
            q_ref[:, o + 128:o + 256] = _rope(qh[:, 128:256], cos1, sin1).astype(BF16)
            kvh = _dot(nkvb, wukv_ref[hd])
            k_ref[:, o:o + 128] = kvh[:, 0:128].astype(BF16)
            k_ref[:, o + 128:o + 256] = kpe
            v_ref[:, hd * 128:(hd + 1) * 128] = kvh[:, 128:256].astype(BF16)

    def tile(width):
        return pl.BlockSpec((tm, width), lambda i: (i, 0))

    widths = (d, PG_COLS, 256, 256, 512, Q_LORA, KV_LORA, HEADS * MLA_HEAD, HEADS * MLA_HEAD, HEADS * 128)
    dtypes = (BF16, F32, BF16, BF16, BF16, BF16, BF16, BF16, BF16, BF16)
    tab = pl.BlockSpec((tm, 256), lambda i: (jnp.where(i < nl, i % tpe, tpe), 0))
    return _hosted_call(
        body, (x2, ctx2, modv, g_attn, w_in, g_q, g_kv, w_uq, w_ukv, cos_t, sin_t), name="pre_fwd", grid=(nl + nc,),
        in_specs=[
            pl.BlockSpec((tm, d), lambda i: (jnp.minimum(i, nl - 1), 0)),
            pl.BlockSpec((tm, d), lambda i: (jnp.maximum(i - nl, 0), 0)),
            pl.BlockSpec((1, 8, d), lambda i: (jnp.minimum(i // tpe, nex), 0, 0)),
            _full((1, d)), _full(w_in.shape), _full((1, Q_LORA)), _full((1, KV_LORA)),
            _full(w_uq.shape), _full(w_ukv.shape), tab, tab,
        ],
        out_specs=[tile(w) for w in widths],
        out_shape=[jax.ShapeDtypeStruct((n_all, w), dt) for w, dt in zip(widths, dtypes)],
        sem=("parallel",), rider=rider)


def _post(yret, ymla, x2, tgt2, modv, g_ffn, g_fin, w_out, w_ff1, w_ff2, *, seq, tm):
    t_lat, d = x2.shape
    nl = t_lat // tm
    tpe = seq // tm
    nex = t_lat // seq
    n_slab = w_ff1.shape[0]
    fs = w_ff1.shape[2]

    def body(yr_ref, ym_ref, x_ref, t_ref, mod_ref, gf_ref, gl_ref, wo_ref, w1_ref, w2_ref,
             mix_ref, a_ref, du_ref, h2_ref, df_ref, dmo_ref, dmix_ref, dxm_ref, st_ref, ru_ref):
        i = pl.program_id(0)
        gt_a = mod_ref[0, 2:3, :]
        sh_f = mod_ref[0, 3:4, :]
        sc_f = mod_ref[0, 4:5, :]
        gt_f = mod_ref[0, 5:6, :]
        g_ffn_v = gf_ref[...]
        g_fin_v = gl_ref[...]
        yr = yr_ref[...]
        ym = ym_ref[...]
        mix_ref[:, 0:512] = yr
        mix_ref[:, 512:1024] = ym
        op = _dot(yr, wo_ref[0:512, :]) + _dot(ym, wo_ref[512:1024, :])
        x_mid = x_ref[...] + gt_a * op
        r2 = lax.rsqrt(jnp.mean(x_mid * x_mid, axis=-1, keepdims=True) + EPS)
        xh2 = x_mid * r2
        h2b = (xh2 * g_ffn_v * (1.0 + sc_f) + sh_f).astype(BF16)
        h2_ref[...] = h2b
        f = jnp.zeros((tm, d), F32)
        for s in range(n_slab):
            ru = jnp.maximum(_dot(h2b, w1_ref[s]), 0.0)
            ru_ref[:, s * fs:(s + 1) * fs] = ru
            ab = (ru * ru).astype(BF16)
            a_ref[:, s * fs:(s + 1) * fs] = ab
            f = f + _dot(ab, w2_ref[s * fs:(s + 1) * fs, :])
        x_out = x_mid + gt_f * f
        r3 = lax.rsqrt(jnp.mean(x_out * x_out, axis=-1, keepdims=True) + EPS)
        xh3 = x_out * r3
        err = xh3 * g_fin_v - t_ref[...]
        dy = err * (1.0 / d)
        dxh3 = dy * g_fin_v
        dx_out = r3 * (dxh3 - xh3 * jnp.mean(dxh3 * xh3, axis=-1, keepdims=True))
        dfb = (dx_out * gt_f).astype(BF16)
        df_ref[...] = dfb
        dh2 = jnp.zeros((tm, d), F32)
        for s in range(n_slab):
            da = _dot_nt(dfb, w2_ref[s * fs:(s + 1) * fs, :])
            dub = (da * (2.0 * ru_ref[:, s * fs:(s + 1) * fs])).astype(BF16)
            du_ref[:, s * fs:(s + 1) * fs] = dub
            dh2 = dh2 + _dot_nt(dub, w1_ref[s])
        dxh2 = dh2 * (1.0 + sc_f) * g_ffn_v
        dx_mid = dx_out + r2 * (dxh2 - xh2 * jnp.mean(dxh2 * xh2, axis=-1, keepdims=True))
        dxm_ref[...] = dx_mid
        dmob = (dx_mid * gt_a).astype(BF16)
        dmo_ref[...] = dmob
        dmix_ref[...] = _dot_nt(dmob, wo_ref[...]).astype(BF16)

        def rsum(v):
            return jnp.sum(v, axis=0, keepdims=True)

        stats = jnp.concatenate([
            rsum(dh2), rsum(dh2 * xh2 * g_ffn_v), rsum(dx_out * f), rsum(dx_mid * op),
            rsum(dh2 * (1.0 + sc_f) * xh2), rsum(dy * xh3), rsum(err * err), jnp.zeros((1, d), F32)], axis=0)

        @pl.when(i % tpe == 0)
        def _():
            st_ref[0] = stats

        @pl.when(i % tpe != 0)
        def _():
            st_ref[0] += stats

    def tile(width):
        return pl.BlockSpec((tm, width), lambda i: (i, 0))

    widths = (d, D_FF, D_FF, d, d, d, d, d)
    dtypes = (BF16, BF16, BF16, BF16, BF16, BF16, BF16, F32)
    const = pl.Buffered(1)
    return pl.pallas_call(
        body, name="post", grid=(nl,),
        in_specs=[
            tile(512), tile(512), tile(d), tile(d),
            pl.BlockSpec((1, 8, d), lambda i: (i // tpe, 0, 0)),
            _full((1, d)), _full((1, d)),
            pl.BlockSpec(w_out.shape, lambda i: (0, 0), pipeline_mode=const),
            pl.BlockSpec(w_ff1.shape, lambda i: (0, 0, 0), pipeline_mode=const),
            pl.BlockSpec(w_ff2.shape, lambda i: (0, 0), pipeline_mode=const),
        ],
        out_specs=[tile(w) for w in widths] + [pl.BlockSpec((1, 8, d), lambda i: (i // tpe, 0, 0))],
        out_shape=[jax.ShapeDtypeStruct((t_lat, w), dt) for w, dt in zip(widths, dtypes)]
        + [jax.ShapeDtypeStruct((nex, 8, d), F32)],
        scratch_shapes=[pltpu.VMEM((tm, D_FF), F32)],
        compiler_params=_params(("arbitrary",), VMEM_LIMIT),
    )(yret, ymla, x2, tgt2, modv, g_ffn, g_fin, w_out, w_ff1, w_ff2)


def _pre_bwd(x2, ctx2, modv, g_attn, pg, drq, drk, dkc_r, drv, dvc_r, drg, dq_m, dkl, dkc, dvl, dvc, dxm,
             w_in, g_q, g_kv, w_uq, w_ukv, cos_t, sin_t, *, seq, tm, rider=None):
    t_lat, d = x2.shape
    t_ctx = ctx2.shape[0]
    nl, nc = t_lat // tm, t_ctx // tm
    n_all = t_lat + t_ctx
    tpe = seq // tm
    nex = t_lat // seq

    def body(x_ref, c_ref, mod_ref, g_ref, pg_ref, drq_ref, drk_ref, dkcr_ref, drv_ref, dvcr_ref, drg_ref,
             dq_ref, dkl_ref, dkc_ref, dvl_ref, dvc_ref, dxm_ref, win_ref, gq_ref, gkv_ref, wuq_ref, wukv_ref,
             cos_ref, sin_ref, dpb_ref, dqf_ref, dkvf_ref, gx_ref, st_ref):
        i = pl.program_id(0)
        lat = i < nl
        latf = lat.astype(F32)
        cos = cos_ref[...]
        sin = sin_ref[...]
        cos1 = cos[:, 0:LANES]
        sin1 = sin[:, 0:LANES]
        d_rq = _rope_t(drq_ref[...] * latf, cos, sin)
        d_rk = _rope_t(jnp.where(lat, drk_ref[...], dkcr_ref[...]), cos, sin) * (RET_DK ** -0.5)
        d_rv = jnp.where(lat, drv_ref[...], dvcr_ref[...])
        d_rg = drg_ref[...] * latf
        dq_all = dq_ref[...] * (latf * MLA_SCALE)
        dk_all = jnp.where(lat, dkl_ref[...], dkc_ref[...])
        dv_all = jnp.where(lat, dvl_ref[...], dvc_ref[...])
        dnq = jnp.zeros((tm, Q_LORA), F32)
        dnkv = jnp.zeros((tm, KV_LORA), F32)
        dkpe = jnp.zeros((tm, LANES), F32)
        for hd in range(HEADS):
            o = hd * MLA_HEAD
            dqh = jnp.concatenate([dq_all[:, o:o + 128], _rope_t(dq_all[:, o + 128:o + 256], cos1, sin1)],
                                  axis=1).astype(BF16)
            dqf_ref[:, o:o + 256] = dqh
            dnq = dnq + _dot(dqh, wuq_ref[hd])
            dkpe = dkpe + dk_all[:, o + 128:o + 256]
            dkvh = jnp.concatenate([dk_all[:, o:o + 128], dv_all[:, hd * 128:(hd + 1) * 128]], axis=1).astype(BF16)
            dkvf_ref[:, o:o + 256] = dkvh
            dnkv = dnkv + _dot_nt(dkvh, wukv_ref[hd])
        d_kpe = _rope_t(dkpe, cos1, sin1)
        pgv = pg_ref[...]
        cq = pgv[:, 512:896]
        ckv = pgv[:, 896:1152]
        rq_ = lax.rsqrt(jnp.mean(cq * cq, axis=-1, keepdims=True) + EPS)
        cqh = cq * rq_
        dcqh = dnq * gq_ref[...]
        d_cq = rq_ * (dcqh - cqh * jnp.mean(dcqh * cqh, axis=-1, keepdims=True))
        rkv_ = lax.rsqrt(jnp.mean(ckv * ckv, axis=-1, keepdims=True) + EPS)
        ckvh = ckv * rkv_
        dckvh = dnkv * gkv_ref[...]
        d_ckv = rkv_ * (dckvh - ckvh * jnp.mean(dckvh * ckvh, axis=-1, keepdims=True))
        dpb = jnp.concatenate([d_rq, d_rk, d_rv, d_rg, d_cq, d_ckv, d_kpe], axis=1).astype(BF16)
        dpb_ref[...] = dpb
        dh = _dot(dpb, win_ref[...])
        xt = jnp.where(lat, x_ref[...], c_ref[...])
        sc = mod_ref[0, 1:2, :]
        g = g_ref[...]
        r = lax.rsqrt(jnp.mean(xt * xt, axis=-1, keepdims=True) + EPS)
        xh = xt * r
        dxh = dh * (1.0 + sc) * g
        dx = r * (dxh - xh * jnp.mean(dxh * xh, axis=-1, keepdims=True))

        @pl.when(lat)
        def _():
            gx_ref[...] = dxm_ref[...] + dx

        def rsum(v):
            return jnp.sum(v, axis=0, keepdims=True)

        def widen(v):
            return jnp.concatenate([v, jnp.zeros((1, d - v.shape[1]), F32)], axis=1)

        stats = jnp.concatenate([
            rsum(dh), rsum(dh * xh * g), rsum(dh * (1.0 + sc) * xh), widen(rsum(dnq * cqh)), widen(rsum(dnkv * ckvh)),
            jnp.zeros((3, d), F32)], axis=0)
        first = jnp.logical_or(jnp.logical_and(lat, i % tpe == 0), i == nl)

        @pl.when(first)
        def _():
            st_ref[0] = stats

        @pl.when(jnp.logical_not(first))
        def _():
            st_ref[0] += stats

    def lat_tile(width):
        return pl.BlockSpec((tm, width), lambda i: (jnp.minimum(i, nl - 1), 0))

    def ctx_tile(width):
        return pl.BlockSpec((tm, width), lambda i: (jnp.maximum(i - nl, 0), 0))

    def tile(width):
        return pl.BlockSpec((tm, width), lambda i: (i, 0))

    tab = pl.BlockSpec((tm, 256), lambda i: (jnp.where(i < nl, i % tpe, tpe), 0))
    ex = pl.BlockSpec((1, 8, d), lambda i: (jnp.minimum(i // tpe, nex), 0, 0))
    return _hosted_call(
        body, (x2, ctx2, modv, g_attn, pg, drq, drk, dkc_r, drv, dvc_r, drg, dq_m, dkl, dkc, dvl, dvc, dxm,
               w_in, g_q, g_kv, w_uq, w_ukv, cos_t, sin_t), name="pre_bwd", grid=(nl + nc,),
        in_specs=[
            lat_tile(d), ctx_tile(d), ex, _full((1, d)), tile(PG_COLS),
            lat_tile(256), lat_tile(256), ctx_tile(256), lat_tile(512), ctx_tile(512), lat_tile(512),
            lat_tile(1024), lat_tile(1024), ctx_tile(1024), lat_tile(512), ctx_tile(512), lat_tile(d),
            _full(w_in.shape), _full((1, Q_LORA)), _full((1, KV_LORA)), _full(w_uq.shape), _full(w_ukv.shape),
            tab, tab,
        ],
        out_specs=[tile(IN_PAD), tile(1024), tile(1024), lat_tile(d), ex],
        out_shape=[
            jax.ShapeDtypeStruct((n_all, IN_PAD), BF16), jax.ShapeDtypeStruct((n_all, 1024), BF16),
            jax.ShapeDtypeStruct((n_all, 1024), BF16), jax.ShapeDtypeStruct((t_lat, d), F32),
            jax.ShapeDtypeStruct((nex + 1, 8, d), F32),
        ],
        sem=("arbitrary",), rider=rider)


MLA_SCALE = 1.0 / math.sqrt(MLA_NOPE + MLA_ROPE)
KEY_BLOCK = 2048


def _mla_specs(t_lat, seq, ctx_len, tq):
    nqt = seq // tq
    cb = t_lat // ctx_len
    q = pl.BlockSpec((tq, MLA_HEAD), lambda b, h, j: (b * nqt + j, h))
    kl = pl.BlockSpec((seq, MLA_HEAD), lambda b, h, j: (b, h))
    kc = pl.BlockSpec((ctx_len, MLA_HEAD), lambda b, h, j: (cb + b, h))
    vl = pl.BlockSpec((seq, 128), lambda b, h, j: (b, h))
    vc = pl.BlockSpec((ctx_len, 128), lambda b, h, j: (cb + b, h))
    o = pl.BlockSpec((tq, 128), lambda b, h, j: (b * nqt + j, h))
    return q, kl, kc, vl, vc, o


def _mla_fwd(q, k, v, *, t_lat, seq, ctx_len, tq, rider=None):
    nex = t_lat // seq

    def body(q_ref, kl_ref, kc_ref, vl_ref, vc_ref, o_ref, lse_ref):
        qb = q_ref[...]
        s = _dot_nt(qb, kl_ref[...])
        sc = _dot_nt(qb, kc_ref[...])
        m = jnp.maximum(jnp.max(s, axis=-1, keepdims=True), jnp.max(sc, axis=-1, keepdims=True))
        p = jnp.exp(s - m)
        pc = jnp.exp(sc - m)
        total = jnp.sum(p, axis=-1, keepdims=True) + jnp.sum(pc, axis=-1, keepdims=True)
        o = _dot(p.astype(BF16), vl_ref[...]) + _dot(pc.astype(BF16), vc_ref[...])
        o_ref[...] = (o * (1.0 / total)).astype(BF16)
        lse_ref[...] = jnp.broadcast_to(m + jnp.log(total), lse_ref.shape)

    qs, kl, kc, vl, vc, os_ = _mla_specs(t_lat, seq, ctx_len, tq)
    return _hosted_call(
        body, (q, k, k, v, v), name="mla_fwd", grid=(nex, HEADS, seq // tq),
        in_specs=[qs, kl, kc, vl, vc], out_specs=[os_, os_],
        out_shape=[jax.ShapeDtypeStruct((t_lat, HEADS * 128), BF16), jax.ShapeDtypeStruct((t_lat, HEADS * 128), F32)],
        sem=("parallel", "parallel", "arbitrary"), rider=rider)


def _mla_bwd(q, k, v, ymla, lse, dmix, *, t_lat, seq, ctx_len, tq, rider=None):
    nex = t_lat // seq
    nqt = seq // tq
    t_ctx = nex * ctx_len
    kb = min(KEY_BLOCK, seq)

    def body(q_ref, kl_ref, kc_ref, vl_ref, vc_ref, o_ref, lse_ref, do_ref, dq_ref, dkl_ref, dkc_ref, dvl_ref, dvc_ref):
        j = pl.program_id(2)

        @pl.when(j == 0)
        def _():
            dkl_ref[...] = jnp.zeros(dkl_ref.shape, F32)
            dkc_ref[...] = jnp.zeros(dkc_ref.shape, F32)
            dvl_ref[...] = jnp.zeros(dvl_ref.shape, F32)
            dvc_ref[...] = jnp.zeros(dvc_ref.shape, F32)

        qb = q_ref[...]
        dob = do_ref[...]
        delta = jnp.sum(dob.astype(F32) * o_ref[...].astype(F32), axis=-1, keepdims=True)
        lse_row = lse_ref[:, 0:1]

        def block(k_ref, v_ref, dk_ref, dv_ref, rows):
            kbl = k_ref[rows, :]
            vbl = v_ref[rows, :]
            p = jnp.exp(_dot_nt(qb, kbl) - lse_row)
            ds = (p * (_dot_nt(dob, vbl) - delta)).astype(BF16)
            dk_ref[rows, :] += _dot_tn(ds, qb)
            dv_ref[rows, :] += _dot_tn(p.astype(BF16), dob)
            return _dot(ds, kbl)

        dq = block(kc_ref, vc_ref, dkc_ref, dvc_ref, pl.ds(0, ctx_len))
        for i in range(seq // kb):
            dq = dq + block(kl_ref, vl_ref, dkl_ref, dvl_ref, pl.ds(i * kb, kb))
        dq_ref[...] = dq

    qs, kl, kc, vl, vc, os_ = _mla_specs(t_lat, seq, ctx_len, tq)
    do_spec = pl.BlockSpec((tq, 128), lambda b, h, j: (b * nqt + j, HEADS + h))
    return _hosted_call(
        body, (q, k, k, v, v, ymla, lse, dmix), name="mla_bwd", grid=(nex, HEADS, nqt),
        in_specs=[qs, kl, kc, vl, vc, os_, os_, do_spec],
        out_specs=[
            qs,
            pl.BlockSpec((seq, MLA_HEAD), lambda b, h, j: (b, h)),
            pl.BlockSpec((ctx_len, MLA_HEAD), lambda b, h, j: (b, h)),
            pl.BlockSpec((seq, 128), lambda b, h, j: (b, h)),
            pl.BlockSpec((ctx_len, 128), lambda b, h, j: (b, h)),
        ],
        out_shape=[
            jax.ShapeDtypeStruct((t_lat, HEADS * MLA_HEAD), F32),
            jax.ShapeDtypeStruct((t_lat, HEADS * MLA_HEAD), F32),
            jax.ShapeDtypeStruct((t_ctx, HEADS * MLA_HEAD), F32),
            jax.ShapeDtypeStruct((t_lat, HEADS * 128), F32),
            jax.ShapeDtypeStruct((t_ctx, HEADS * 128), F32),
        ],
        sem=("parallel", "parallel", "arbitrary"), rider=rider)


def _decay_terms(lg, chunk, forward):
    ii = lax.broadcasted_iota(jnp.int32, (chunk, chunk), 0)
    jj = lax.broadcasted_iota(jnp.int32, (chunk, chunk), 1)
    diff = (ii - jj) if forward else (jj - ii)
    dist = jnp.maximum(diff, 0).astype(F32)
    dmat = jnp.where(diff >= 0, jnp.exp(lg * dist), 0.0)
    pos = lax.broadcasted_iota(jnp.int32, (chunk, 1), 0).astype(F32)
    if forward:
        e_q = pos + 1.0
        e_k = (chunk - 1.0) - pos
    else:
        e_q = chunk - pos
        e_k = pos
    wq = jnp.exp(lg * e_q)
    wk = jnp.exp(lg * e_k)
    cd = jnp.exp(jnp.full((1, 1), lg * chunk, F32))
    return dmat, dist, wq, wk, e_q, e_k, cd


def _ctx_weights(lg, ctx_len, forward):
    pos = lax.broadcasted_iota(jnp.int32, (ctx_len, 1), 0).astype(F32)
    e = ((ctx_len - 1.0) - pos) if forward else pos
    return jnp.exp(lg * e), e


def _pair_specs(t_lat, seq, ctx_len):
    cb = t_lat // ctx_len
    qk = pl.BlockSpec((seq, 128), lambda b, p: (b, p))
    v = pl.BlockSpec((seq, 256), lambda b, p: (b, p))
    kc = pl.BlockSpec((ctx_len, 128), lambda b, p: (cb + b, p))
    vc = pl.BlockSpec((ctx_len, 256), lambda b, p: (cb + b, p))
    return qk, v, kc, vc


def _lane_masks():
    lane = lax.broadcasted_iota(jnp.int32, (1, 128), 1)
    return [(lane // RET_DK) == hh for hh in (0, 1)]


def _ret_fwd_pair(rq, rk, rv, pg, lg, g_ret, *, t_lat, seq, ctx_len, chunk, rider=None):
    nex = t_lat // seq
    n_chunk = seq // chunk

    def body(q_ref, k_ref, v_ref, kc_ref, vc_ref, rg_ref, lg_ref, g_ref, y_ref, o_ref):
        pair = pl.program_id(1)
        masks = _lane_masks()
        kcf = kc_ref[...].astype(F32)

        def run(forward):
            terms, s0 = [], []
            for hh in (0, 1):
                lgd = lg_ref[0 if forward else 1, 2 * pair + hh]
                terms.append(_decay_terms(lgd, chunk, forward))
                wc, _ = _ctx_weights(lgd, ctx_len, forward)
                s0.append(_dot_tn((jnp.where(masks[hh], kcf, 0.0) * wc).astype(BF16), vc_ref[:, hh * 128:(hh + 1) * 128]))

            def step(t, states):
                n = t if forward else n_chunk - 1 - t
                sl = pl.ds(pl.multiple_of(n * chunk, chunk), chunk)
                qb = q_ref[sl, :]
                kf_all = k_ref[sl, :].astype(F32)
                new = []
                for hh in (0, 1):
                    dmat, _, wq, wk, _, _, cd = terms[hh]
                    cols = slice(hh * 128, (hh + 1) * 128)
                    qm = jnp.where(masks[hh], qb, jnp.zeros((), BF16))
                    kf = jnp.where(masks[hh], kf_all, 0.0)
                    vb = v_ref[sl, cols]
                    a = _dot_nt(qm, kf.astype(BF16)) * dmat
                    o = _dot(a.astype(BF16), vb) + wq * _dot(qm, states[hh].astype(BF16))
                    if forward:
                        o_ref[sl, cols] = o
                    else:
                        o = o_ref[sl, cols] + o
                        o_ref[sl, cols] = o
                        mu = jnp.mean(o, axis=-1, keepdims=True)
                        oc = o - mu
                        var = jnp.mean(oc * oc, axis=-1, keepdims=True)
                        rg = rg_ref[sl, cols]
                        y_ref[sl, cols] = (oc * lax.rsqrt(var + EPS) * g_ref[:, cols] * (rg / (1.0 + jnp.exp(-rg)))).astype(BF16)
                    new.append(cd * states[hh] + _dot_tn((kf * wk).astype(BF16), vb))
                return tuple(new)

            lax.fori_loop(0, n_chunk, step, tuple(s0))

        run(True)
        run(False)

    qk, v, kc, vc = _pair_specs(t_lat, seq, ctx_len)
    return _hosted_call(
        body, (rq, rk, rv, rk, rv, pg, lg, g_ret), name="ret_fwd", grid=(nex, HEADS // 2),
        in_specs=[qk, qk, v, kc, vc, v, pl.BlockSpec(memory_space=pltpu.SMEM), pl.BlockSpec((1, 256), lambda b, p: (0, p))],
        out_specs=[v, v],
        out_shape=[jax.ShapeDtypeStruct((t_lat, HEADS * RET_DV), BF16), jax.ShapeDtypeStruct((t_lat, HEADS * RET_DV), F32)],
        sem=("parallel", "arbitrary"), rider=rider)


def _ret_bwd_pair(rq, rk, rv, pg, osum, dmix, lg, g_ret, *, t_lat, seq, ctx_len, chunk, rider=None):
    nex = t_lat // seq
    n_chunk = seq // chunk
    t_ctx = nex * ctx_len

    def body(q_ref, k_ref, v_ref, kc_ref, vc_ref, rg_ref, o_ref, dy_ref, lg_ref, g_ref,
             dq_ref, dk_ref, dv_ref, dkc_ref, dvc_ref, drg_ref, st_ref, do_s, s_st):
        pair = pl.program_id(1)
        masks = _lane_masks()
        kcf = kc_ref[...].astype(F32)

        def norm_step(n, dgains):
            sl = pl.ds(pl.multiple_of(n * chunk, chunk), chunk)
            out = []
            for hh in (0, 1):
                cols = slice(hh * 128, (hh + 1) * 128)
                gain = g_ref[:, cols]
                o = o_ref[sl, cols]
                mu = jnp.mean(o, axis=-1, keepdims=True)
                oc = o - mu
                rstd = lax.rsqrt(jnp.mean(oc * oc, axis=-1, keepdims=True) + EPS)
                ohat = oc * rstd
                rg = rg_ref[sl, cols]
                sg = 1.0 / (1.0 + jnp.exp(-rg))
                dy = dy_ref[sl, cols].astype(F32)
                don = dy * (rg * sg)
                drg_ref[sl, cols] = dy * (ohat * gain) * (sg * (1.0 + rg * (1.0 - sg)))
                dohat = don * gain
                do_s[sl, cols] = rstd * (dohat - jnp.mean(dohat, axis=-1, keepdims=True)
                                         - ohat * jnp.mean(dohat * ohat, axis=-1, keepdims=True))
                out.append(dgains[hh] + jnp.sum(don * ohat, axis=0, keepdims=True))
            return tuple(out)

        zero_row = jnp.zeros((1, 128), F32)
        dgains = lax.fori_loop(0, n_chunk, norm_step, (zero_row, zero_row))
        dq_ref[...] = jnp.zeros(dq_ref.shape, F32)
        dk_ref[...] = jnp.zeros(dk_ref.shape, F32)
        dv_ref[...] = jnp.zeros(dv_ref.shape, F32)

        def run(forward):
            terms, ctxw, s0 = [], [], []
            for hh in (0, 1):
                lgd = lg_ref[0 if forward else 1, 2 * pair + hh]
                terms.append(_decay_terms(lgd, chunk, forward))
                ctxw.append(_ctx_weights(lgd, ctx_len, forward))
                s0.append(_dot_tn((jnp.where(masks[hh], kcf, 0.0) * ctxw[hh][0]).astype(BF16),
                                  vc_ref[:, hh * 128:(hh + 1) * 128]))

            def state_step(t, states):
                n = t if forward else n_chunk - 1 - t
                sl = pl.ds(pl.multiple_of(n * chunk, chunk), chunk)
                kf_all = k_ref[sl, :].astype(F32)
                new = []
                for hh in (0, 1):
                    wk, cd = terms[hh][3], terms[hh][6]
                    s_st[hh, n] = states[hh]
                    kf = jnp.where(masks[hh], kf_all, 0.0)
                    new.append(cd * states[hh] + _dot_tn((kf * wk).astype(BF16), v_ref[sl, hh * 128:(hh + 1) * 128]))
                return tuple(new)

            lax.fori_loop(0, n_chunk, state_step, tuple(s0))

            def grad_step(t, carry):
                n = (n_chunk - 1 - t) if forward else t
                sl = pl.ds(pl.multiple_of(n * chunk, chunk), chunk)
                qb = q_ref[sl, :]
                kf_all = k_ref[sl, :].astype(F32)
                dq_sum = jnp.zeros((chunk, 128), F32)
                dk_sum = jnp.zeros((chunk, 128), F32)
                out = []
                for hh in (0, 1):
                    g_next, dlg = carry[hh]
                    dmat, dist, wq, wk, e_q, e_k, cd = terms[hh]
                    cols = slice(hh * 128, (hh + 1) * 128)
                    qm = jnp.where(masks[hh], qb, jnp.zeros((), BF16))
                    kf = jnp.where(masks[hh], kf_all, 0.0)
                    kb = kf.astype(BF16)
                    vb = v_ref[sl, cols]
                    do = do_s[sl, cols]
                    dob = do.astype(BF16)
                    s_n = s_st[hh, n]
                    s_nb = s_n.astype(BF16)
                    gb = g_next.astype(BF16)
                    dk_cross = wk * _dot_nt(vb, gb)
                    dv_cross = _dot((kf * wk).astype(BF16), gb)
                    a = _dot_nt(qm, kb) * dmat
                    da_raw = _dot_nt(dob, vb)
                    dab = (da_raw * dmat).astype(BF16)
                    o_cross = wq * _dot(qm, s_nb)
                    dq_sum = dq_sum + _dot(dab, kb) + wq * _dot_nt(dob, s_nb)
                    dk_sum = dk_sum + _dot_tn(dab, qm) + dk_cross
                    dv_ref[sl, cols] += _dot_tn(a.astype(BF16), dob) + dv_cross
                    dlg = (dlg + chunk * cd * jnp.sum(g_next * s_n, keepdims=True)
                           + jnp.sum(e_k * jnp.sum(kf * dk_cross, axis=-1, keepdims=True), keepdims=True)
                           + jnp.sum(dist * a * da_raw, keepdims=True)
                           + jnp.sum(e_q * jnp.sum(o_cross * do, axis=-1, keepdims=True), keepdims=True))
                    out.append((cd * g_next + _dot_tn((qm.astype(F32) * wq).astype(BF16), dob), dlg))
                dq_ref[sl, :] += dq_sum
                dk_ref[sl, :] += dk_sum
                return tuple(out)

            zero = (jnp.zeros((128, 128), F32), jnp.zeros((1, 1), F32))
            res = lax.fori_loop(0, n_chunk, grad_step, (zero, zero))
            dkc_sum = jnp.zeros((ctx_len, 128), F32)
            dvc, dlgs = [], []
            for hh in (0, 1):
                ds0, dlg = res[hh]
                wc, e_c = ctxw[hh]
                kcm = jnp.where(masks[hh], kcf, 0.0)
                ds0b = ds0.astype(BF16)
                dkc_part = wc * _dot_nt(vc_ref[:, hh * 128:(hh + 1) * 128], ds0b)
                dkc_sum = dkc_sum + dkc_part
                dvc.append(_dot((kcm * wc).astype(BF16), ds0b))
                dlgs.append(dlg + jnp.sum(e_c * jnp.sum(kcm * dkc_part, axis=-1, keepdims=True), keepdims=True))
            return dkc_sum, dvc, dlgs

        dkc_f, dvc_f, dlg_f = run(True)
        dkc_b, dvc_b, dlg_b = run(False)
        dkc_ref[...] = dkc_f + dkc_b
        for hh in (0, 1):
            cols = slice(hh * 128, (hh + 1) * 128)
            dvc_ref[:, cols] = dvc_f[hh] + dvc_b[hh]
            st_ref[0, :, cols] = jnp.concatenate([
                dgains[hh], jnp.broadcast_to(dlg_f[hh], (1, 128)), jnp.broadcast_to(dlg_b[hh], (1, 128)),
                jnp.zeros((5, 128), F32)], axis=0)

    qk, v, kc, vc = _pair_specs(t_lat, seq, ctx_len)
    return _hosted_call(
        body, (rq, rk, rv, rk, rv, pg, osum, dmix, lg, g_ret), name="ret_bwd", grid=(nex, HEADS // 2),
        in_specs=[qk, qk, v, kc, vc, v, v, v, pl.BlockSpec(memory_space=pltpu.SMEM),
                  pl.BlockSpec((1, 256), lambda b, p: (0, p))],
        out_specs=[
            qk, qk, v,
            pl.BlockSpec((ctx_len, 128), lambda b, p: (b, p)),
            pl.BlockSpec((ctx_len, 256), lambda b, p: (b, p)),
            v,
            pl.BlockSpec((1, 8, 256), lambda b, p: (b, 0, p)),
        ],
        out_shape=[
            jax.ShapeDtypeStruct((t_lat, 256), F32), jax.ShapeDtypeStruct((t_lat, 256), F32),
            jax.ShapeDtypeStruct((t_lat, 512), F32), jax.ShapeDtypeStruct((t_ctx, 256), F32),
            jax.ShapeDtypeStruct((t_ctx, 512), F32), jax.ShapeDtypeStruct((t_lat, 512), F32),
            jax.ShapeDtypeStruct((nex, 8, 512), F32),
        ],
        scratch_shapes=[pltpu.VMEM((seq, 256), F32), pltpu.VMEM((2, n_chunk, 128, 128), F32)],
        sem=("parallel", "arbitrary"), rider=rider)


def _ret_specs(t_lat, seq, ctx_len):
    cb = t_lat // ctx_len
    qk = pl.BlockSpec((seq, 128), lambda b, h: (b, h // 2))
    v = pl.BlockSpec((seq, 128), lambda b, h: (b, h))
    kc = pl.BlockSpec((ctx_len, 128), lambda b, h: (cb + b, h // 2))
    vc = pl.BlockSpec((ctx_len, 128), lambda b, h: (cb + b, h))
    return qk, v, kc, vc


def _head_mask(h):
    lane = lax.broadcasted_iota(jnp.int32, (1, 128), 1)
    return (lane // RET_DK) == (h % 2)


def _ret_fwd(rq, rk, rv, pg, lg, g_ret, *, t_lat, seq, ctx_len, chunk, rider=None):
    nex = t_lat // seq
    n_chunk = seq // chunk

    def body(q_ref, k_ref, v_ref, kc_ref, vc_ref, rg_ref, lg_ref, g_ref, y_ref, o_ref):
        h = pl.program_id(1)
        hm = _head_mask(h)
        gain = g_ref[...]
        kcm = jnp.where(hm, kc_ref[...].astype(F32), 0.0)
        vcb = vc_ref[...]

        def run(forward):
            lgd = lg_ref[0 if forward else 1, h]
            dmat, _, wq, wk, _, _, cd = _decay_terms(lgd, chunk, forward)
            wc, _ = _ctx_weights(lgd, ctx_len, forward)
            s0 = _dot_tn((kcm * wc).astype(BF16), vcb)

            def step(t, s):
                n = t if forward else n_chunk - 1 - t
                sl = pl.ds(pl.multiple_of(n * chunk, chunk), chunk)
                qm = jnp.where(hm, q_ref[sl, :], jnp.zeros((), BF16))
                kf = jnp.where(hm, k_ref[sl, :].astype(F32), 0.0)
                vb = v_ref[sl, :]
                a = _dot_nt(qm, kf.astype(BF16)) * dmat
                o = _dot(a.astype(BF16), vb) + wq * _dot(qm, s.astype(BF16))
                if forward:
                    o_ref[sl, :] = o
                else:
                    o = o_ref[sl, :] + o
                    o_ref[sl, :] = o
                    mu = jnp.mean(o, axis=-1, keepdims=True)
                    oc = o - mu
                    var = jnp.mean(oc * oc, axis=-1, keepdims=True)
                    on = oc * lax.rsqrt(var + EPS) * gain
                    rg = rg_ref[sl, :]
                    y_ref[sl, :] = (on * (rg / (1.0 + jnp.exp(-rg)))).astype(BF16)
                return cd * s + _dot_tn((kf * wk).astype(BF16), vb)

            lax.fori_loop(0, n_chunk, step, s0)

        run(True)
        run(False)

    qk, v, kc, vc = _ret_specs(t_lat, seq, ctx_len)
    return _hosted_call(
        body, (rq, rk, rv, rk, rv, pg, lg, g_ret), name="ret_fwd", grid=(nex, HEADS),
        in_specs=[qk, qk, v, kc, vc, v, pl.BlockSpec(memory_space=pltpu.SMEM), pl.BlockSpec((1, 128), lambda b, h: (0, h))],
        out_specs=[v, v],
        out_shape=[jax.ShapeDtypeStruct((t_lat, HEADS * RET_DV), BF16), jax.ShapeDtypeStruct((t_lat, HEADS * RET_DV), F32)],
        sem=("parallel", "arbitrary"), rider=rider)


def _ret_bwd(rq, rk, rv, pg, osum, dmix, lg, g_ret, *, t_lat, seq, ctx_len, chunk, rider=None):
    nex = t_lat // seq
    n_chunk = seq // chunk
    t_ctx = nex * ctx_len

    def body(q_ref, k_ref, v_ref, kc_ref, vc_ref, rg_ref, o_ref, dy_ref, lg_ref, g_ref,
             dq_ref, dk_ref, dv_ref, dkc_ref, dvc_ref, drg_ref, st_ref, do_s, s_st):
        h = pl.program_id(1)
        hm = _head_mask(h)
        gain = g_ref[...]
        kcm = jnp.where(hm, kc_ref[...].astype(F32), 0.0)
        vcb = vc_ref[...]

        def norm_step(n, dgain):
            sl = pl.ds(pl.multiple_of(n * chunk, chunk), chunk)
            o = o_ref[sl, :]
            mu = jnp.mean(o, axis=-1, keepdims=True)
            oc = o - mu
            rstd = lax.rsqrt(jnp.mean(oc * oc, axis=-1, keepdims=True) + EPS)
            ohat = oc * rstd
            rg = rg_ref[sl, :]
            sg = 1.0 / (1.0 + jnp.exp(-rg))
            dy = dy_ref[sl, :].astype(F32)
            don = dy * (rg * sg)
            drg_ref[sl, :] = dy * (ohat * gain) * (sg * (1.0 + rg * (1.0 - sg)))
            dohat = don * gain
            do_s[sl, :] = rstd * (dohat - jnp.mean(dohat, axis=-1, keepdims=True)
                                  - ohat * jnp.mean(dohat * ohat, axis=-1, keepdims=True))
            return dgain + jnp.sum(don * ohat, axis=0, keepdims=True)

        dgain = lax.fori_loop(0, n_chunk, norm_step, jnp.zeros((1, 128), F32))

        @pl.when(h % 2 == 0)
        def _():
            dq_ref[...] = jnp.zeros(dq_ref.shape, F32)
            dk_ref[...] = jnp.zeros(dk_ref.shape, F32)
            dkc_ref[...] = jnp.zeros(dkc_ref.shape, F32)

        dv_ref[...] = jnp.zeros(dv_ref.shape, F32)

        def run(forward):
            lgd = lg_ref[0 if forward else 1, h]
            dmat, dist, wq, wk, e_q, e_k, cd = _decay_terms(lgd, chunk, forward)
            wc, e_c = _ctx_weights(lgd, ctx_len, forward)
            s0 = _dot_tn((kcm * wc).astype(BF16), vcb)

            def state_step(t, s):
                n = t if forward else n_chunk - 1 - t
                sl = pl.ds(pl.multiple_of(n * chunk, chunk), chunk)
                s_st[n] = s
                kf = jnp.where(hm, k_ref[sl, :].astype(F32), 0.0)
                return cd * s + _dot_tn((kf * wk).astype(BF16), v_ref[sl, :])

            lax.fori_loop(0, n_chunk, state_step, s0)

            def grad_step(t, carry):
                g_next, dlg = carry
                n = (n_chunk - 1 - t) if forward else t
                sl = pl.ds(pl.multiple_of(n * chunk, chunk), chunk)
                qm = jnp.where(hm, q_ref[sl, :], jnp.zeros((), BF16))
                kf = jnp.where(hm, k_ref[sl, :].astype(F32), 0.0)
                kb = kf.astype(BF16)
                vb = v_ref[sl, :]
                do = do_s[sl, :]
                dob = do.astype(BF16)
                s_n = s_st[n]
                s_nb = s_n.astype(BF16)
                gb = g_next.astype(BF16)
                dk_cross = wk * _dot_nt(vb, gb)
                dv_cross = _dot((kf * wk).astype(BF16), gb)
                a = _dot_nt(qm, kb) * dmat
                da_raw = _dot_nt(dob, vb)
                dab = (da_raw * dmat).astype(BF16)
                ab = a.astype(BF16)
                o_cross = wq * _dot(qm, s_nb)
                dq_ref[sl, :] += _dot(dab, kb) + wq * _dot_nt(dob, s_nb)
                dk_ref[sl, :] += _dot_tn(dab, qm) + dk_cross
                dv_ref[sl, :] += _dot_tn(ab, dob) + dv_cross
                dlg = (dlg + chunk * cd * jnp.sum(g_next * s_n, keepdims=True)
                       + jnp.sum(e_k * jnp.sum(kf * dk_cross, axis=-1, keepdims=True), keepdims=True)
                       + jnp.sum(dist * a * da_raw, keepdims=True)
                       + jnp.sum(e_q * jnp.sum(o_cross * do, axis=-1, keepdims=True), keepdims=True))
                g_new = cd * g_next + _dot_tn((qm.astype(F32) * wq).astype(BF16), dob)
                return g_new, dlg

            ds0, dlg = lax.fori_loop(0, n_chunk, grad_step, (jnp.zeros((128, 128), F32), jnp.zeros((1, 1), F32)))
            ds0b = ds0.astype(BF16)
            dkc_part = wc * _dot_nt(vcb, ds0b)
            dkc_ref[...] += dkc_part
            dvc_part = _dot((kcm * wc).astype(BF16), ds0b)
            dlg = dlg + jnp.sum(e_c * jnp.sum(kcm * dkc_part, axis=-1, keepdims=True), keepdims=True)
            return dvc_part, dlg

        dvc_f, dlg_f = run(True)
        dvc_b, dlg_b = run(False)
        dvc_ref[...] = dvc_f + dvc_b
        st_ref[0] = jnp.concatenate([
            dgain, jnp.broadcast_to(dlg_f, (1, 128)), jnp.broadcast_to(dlg_b, (1, 128)), jnp.zeros((5, 128), F32)], axis=0)

    qk, v, kc, vc = _ret_specs(t_lat, seq, ctx_len)
    dy_spec = v
    return _hosted_call(
        body, (rq, rk, rv, rk, rv, pg, osum, dmix, lg, g_ret), name="ret_bwd", grid=(nex, HEADS),
        in_specs=[qk, qk, v, kc, vc, v, v, dy_spec, pl.BlockSpec(memory_space=pltpu.SMEM),
                  pl.BlockSpec((1, 128), lambda b, h: (0, h))],
        out_specs=[
            qk, qk, v,
            pl.BlockSpec((ctx_len, 128), lambda b, h: (b, h // 2)),
            pl.BlockSpec((ctx_len, 128), lambda b, h: (b, h)),
            v,
            pl.BlockSpec((1, 8, 128), lambda b, h: (b, 0, h)),
        ],
        out_shape=[
            jax.ShapeDtypeStruct((t_lat, 256), F32), jax.ShapeDtypeStruct((t_lat, 256), F32),
            jax.ShapeDtypeStruct((t_lat, 512), F32), jax.ShapeDtypeStruct((t_ctx, 256), F32),
            jax.ShapeDtypeStruct((t_ctx, 512), F32), jax.ShapeDtypeStruct((t_lat, 512), F32),
            jax.ShapeDtypeStruct((nex, 8, 512), F32),
        ],
        scratch_shapes=[pltpu.VMEM((seq, 128), F32), pltpu.VMEM((n_chunk, 128, 128), F32)],
        sem=("parallel", "arbitrary"), rider=rider)


def _matmul_tn(a, b, *, bm, bn, bk, chip_major, name, out_dtype=F32):
    tk, m = a.shape
    n = b.shape[1]
    slab = n // N_CHIPS
    per_block = bn // slab if chip_major else 1
    bk = max(c for c in range(LANES, min(bk, tk) + 1, LANES) if tk % c == 0)
    nk = tk // bk
    blk = (per_block, bm, slab) if chip_major else (bm, bn)

    def body(a_ref, b_ref, o_ref, acc_ref):
        k = pl.program_id(2)
        if chip_major:
            parts = [_dot_tn(a_ref[...], b_ref[:, s * slab:(s + 1) * slab]) for s in range(per_block)]
        else:
            parts = [_dot_tn(a_ref[...], b_ref[...])]

        @pl.when(k == 0)
        def _():
            for s, part in enumerate(parts):
                if chip_major:
                    acc_ref[s] = part
                else:
                    acc_ref[...] = part

        @pl.when(k > 0)
        def _():
            for s, part in enumerate(parts):
                if chip_major:
                    acc_ref[s] += part
                else:
                    acc_ref[...] += part

        @pl.when(k == nk - 1)
        def _():
            o_ref[...] = acc_ref[...].astype(out_dtype)

    if chip_major:
        out_spec = pl.BlockSpec(blk, lambda i, j, k: (j, i, 0))
        out_shape = jax.ShapeDtypeStruct((N_CHIPS, m, slab), out_dtype)
    else:
        out_spec = pl.BlockSpec(blk, lambda i, j, k: (i, j))
        out_shape = jax.ShapeDtypeStruct((m, n), out_dtype)
    return pl.pallas_call(
        body, name=name, grid=(m // bm, n // bn, nk),
        in_specs=[pl.BlockSpec((bk, bm), lambda i, j, k: (k, i)), pl.BlockSpec((bk, bn), lambda i, j, k: (k, j))],
        out_specs=out_spec, out_shape=out_shape, scratch_shapes=[pltpu.VMEM(blk, F32)],
        compiler_params=_params(("parallel", "parallel", "arbitrary"), VMEM_LIMIT),
    )(a, b)


_LATE = ("w_out", "w_ff1", "w_ff2")
_EARLY = ("w_in", "w_uq", "w_ukv")


def _local_step(x, ctx, tgt, modv, lg, g_attn, g_ffn, g_fin, g_ret, g_q, g_kv, w_in, w_uq, w_ukv, late, place=None,
                *, tm=256, tq=256, chunk=256):
    nex, seq, d = x.shape
    ctx_len = ctx.shape[1]
    t_lat = nex * seq
    x2 = x.reshape(t_lat, d)
    ctx2 = ctx.reshape(nex * ctx_len, d)
    tgt2 = tgt.reshape(t_lat, d)
    cos_t, sin_t = _rope_tables(seq, tm)
    dims = dict(t_lat=t_lat, seq=seq, ctx_len=ctx_len)
    alone = place is None

    (hb, pg, rq, rk, rv, nq, nkv, q, k, v), crossed2 = _pre_fwd(
        x2, ctx2, modv, g_attn, w_in, g_q, g_kv, w_uq, w_ukv, cos_t, sin_t, seq=seq, tm=tm,
        rider=None if alone else _gather_ici_rider([late[2]]))
    (yret, osum), crossed1 = _ret_fwd_pair(rq, rk, rv, pg, lg, g_ret, chunk=chunk, **dims,
                                           rider=None if alone else _gather_ici_rider([late[1]]))
    (ymla, lse), gathered = _mla_fwd(
        q, k, v, tq=tq, **dims,
        rider=None if alone else _merge_riders(_gather_rider([late[0]]), _gather_d2d_rider(crossed1 + crossed2)))
    w_out, w_ff1, w_ff2 = late if alone else gathered
    mix, act, du, h2, df, dmo, dmix, dxm, st_post = _post(yret, ymla, x2, tgt2, modv, g_ffn, g_fin, w_out.reshape(d, d),
                                                         w_ff1, w_ff2.reshape(D_FF, d), seq=seq, tm=tm)
    bk = 512
    g_late = [
        _matmul_tn(mix, dmo, bm=1024, bn=1024, bk=1024, chip_major=False, name="gw_out",
                   out_dtype=BF16).reshape(N_CHIPS, d // N_CHIPS, d),
        _matmul_tn(h2, du, bm=1024, bn=1024, bk=1024, chip_major=True, name="gw_ff1", out_dtype=BF16),
        _matmul_tn(act, df, bm=1024, bn=1024, bk=1024, chip_major=False, name="gw_ff2",
                   out_dtype=BF16).reshape(N_CHIPS, D_FF // N_CHIPS, d),
    ]
    (dq_m, dkl, dkc, dvl, dvc), got = _mla_bwd(q, k, v, ymla, lse, dmix, tq=tq, **dims,
                                               rider=None if alone else _exchange_rider(g_late))
    if not alone:
        core, slot = place
        part = [_add_half(g, r, core, "add_half_" + n) for g, r, n in zip(g_late, got, _LATE)]
    (drq, drk, drv, dkc_r, dvc_r, drg, st_ret), landed = _ret_bwd_pair(
        rq, rk, rv, pg, osum, dmix, lg, g_ret, chunk=chunk, **dims, rider=None if alone else _scatter_rider(part))
    if not alone:
        mine = [_sum_chips(p, l, slot, "sum_chips_" + n) for p, l, n in zip(part, landed, _LATE)]
    (dpb, dqf, dkvf, gx, st_pre), _ = _pre_bwd(
        x2, ctx2, modv, g_attn, pg, drq, drk, dkc_r, drv, dvc_r, drg, dq_m, dkl, dkc, dvl, dvc, dxm, w_in, g_q, g_kv,
        w_uq, w_ukv, cos_t, sin_t, seq=seq, tm=tm)
    g_early = [
        _matmul_tn(dpb, hb, bm=IN_PAD // 2, bn=d, bk=512, chip_major=False, name="gw_in"),
        _matmul_tn(dqf, nq, bm=HEADS * MLA_HEAD, bn=Q_LORA, bk=1536, chip_major=False, name="gw_uq"),
        _matmul_tn(nkv, dkvf, bm=KV_LORA, bn=HEADS * 256, bk=1536, chip_major=True, name="gw_ukv"),
    ]
    late_out = g_late if alone else mine
    return gx.reshape(nex, seq, d), g_early, late_out, st_post, st_ret, st_pre


_ANY = pl.BlockSpec(memory_space=pl.ANY)
_VMEM = pl.BlockSpec(memory_space=pltpu.VMEM)
_OFFSETS = tuple((dx, dy, dc) for dx in (0, 1) for dy in (0, 1) for dc in (0, 1))[1:]
_CHIP_OFFSETS = ((1, 0), (0, 1), (1, 1))


def _place():
    return lax.axis_index("x"), lax.axis_index("y"), lax.axis_index("c")


def _flip(v, d):
    return 1 - v if d else v


def _gather8_rider(a, in_vmem=True):
    def copies(a_ref, o_ref, send, recv):
        x, y, z = _place()
        me = 4 * x + 2 * y + z
        out = []
        for k, (dx, dy, dc) in enumerate(_OFFSETS):
            peer = (_flip(x, dx), _flip(y, dy), _flip(z, dc))
            landing = o_ref.at[4 * peer[0] + 2 * peer[1] + peer[2]]
            out.append((
                pltpu.make_async_remote_copy(src_ref=a_ref, dst_ref=o_ref.at[me], send_sem=send.at[k],
                                             recv_sem=recv.at[k], device_id=peer, device_id_type=MESH),
                pltpu.make_async_remote_copy(src_ref=a_ref, dst_ref=landing, send_sem=send.at[k],
                                             recv_sem=recv.at[k], device_id=peer, device_id_type=MESH)))
        return me, out

    def start(ins, outs, sems):
        me, cps = copies(ins[0], outs[0], sems[0], sems[1])
        pltpu.make_async_copy(ins[0], outs[0].at[me], sems[2]).start()
        for out_cp, _ in cps:
            out_cp.start()

    def finish(ins, outs, sems):
        me, cps = copies(ins[0], outs[0], sems[0], sems[1])
        for out_cp, in_cp in cps:
            in_cp.wait_recv()
            out_cp.wait_send()
        pltpu.make_async_copy(ins[0], outs[0].at[me], sems[2]).wait()

    spec = [_VMEM] if in_vmem else [_ANY]
    return _Rider([a], [jax.ShapeDtypeStruct((N_DEV,) + a.shape, a.dtype)],
                  [pltpu.SemaphoreType.DMA((7,)), pltpu.SemaphoreType.DMA((7,)), pltpu.SemaphoreType.DMA],
                  start, finish, in_specs=spec, out_specs=spec)


def _merge_riders(*riders):
    ins, outs, sems, in_specs, out_specs, aliases, cuts = [], [], [], [], [], {}, []
    for r in riders:
        cuts.append((len(ins), len(outs), len(sems)))
        aliases.update({len(ins) + i: len(outs) + j for i, j in r.aliases.items()})
        ins += r.ins
        outs += r.out_shapes
        sems += r.sems
        in_specs += r.in_specs
        out_specs += r.out_specs

    def part(r, cut, r_ins, r_outs, r_sems):
        return (r_ins[cut[0]:cut[0] + len(r.ins)], r_outs[cut[1]:cut[1] + len(r.out_shapes)],
                r_sems[cut[2]:cut[2] + len(r.sems)])

    def start(r_ins, r_outs, r_sems):
        for r, cut in zip(riders, cuts):
            r.start(*part(r, cut, r_ins, r_outs, r_sems))

    def finish(r_ins, r_outs, r_sems):
        for r, cut in zip(riders, cuts):
            r.finish(*part(r, cut, r_ins, r_outs, r_sems))

    return _Rider(ins, outs, sems, start, finish, aliases=aliases, in_specs=in_specs, out_specs=out_specs)


def _allgather8(a, name):
    return _run_rider(_gather8_rider(a), name)[0]


BF16_TILE_ROWS = 16


def _half(o, slot, which):
    r2 = o.shape[1] // 2
    if r2 % BF16_TILE_ROWS == 0:
        return o.at[slot, pl.ds(which * r2, r2)]
    c2 = o.shape[2] // 2
    assert c2 % LANES == 0
    return o.at[slot, :, pl.ds(which * c2, c2)]


def _gather_send(o_refs, send, recv):
    x, y, z = _place()
    chip = 2 * x + y
    for a, o in enumerate(o_refs):
        r2 = o.shape[1] // 2
        mine = _half(o, chip, z)
        for k, (dx, dy) in enumerate(_CHIP_OFFSETS):
            pltpu.make_async_remote_copy(
                src_ref=mine, dst_ref=mine, send_sem=send.at[a, k], recv_sem=recv.at[a, k],
                device_id=(_flip(x, dx), _flip(y, dy), z), device_id_type=MESH).start()


def _gather_landed(o_refs, send, recv, then=None):
    x, y, z = _place()
    chip = 2 * x + y
    for a, o in enumerate(o_refs):
        for k, (dx, dy) in enumerate(_CHIP_OFFSETS):
            landed = _half(o, 2 * _flip(x, dx) + _flip(y, dy), z)
            pltpu.make_async_remote_copy(
                src_ref=landed, dst_ref=landed, send_sem=send.at[a, k], recv_sem=recv.at[a, k],
                device_id=(_flip(x, dx), _flip(y, dy), z), device_id_type=MESH).wait_recv()
            if then is not None:
                then(a, k, landed)
    for a, o in enumerate(o_refs):
        mine = _half(o, chip, z)
        for k, (dx, dy) in enumerate(_CHIP_OFFSETS):
            pltpu.make_async_remote_copy(
                src_ref=mine, dst_ref=mine, send_sem=send.at[a, k], recv_sem=recv.at[a, k],
                device_id=(_flip(x, dx), _flip(y, dy), z), device_id_type=MESH).wait_send()


def _pass_on(o_refs, fsend, frecv, a, k, landed):
    x, y, z = _place()
    pltpu.make_async_remote_copy(
        src_ref=landed, dst_ref=landed, send_sem=fsend.at[a, k], recv_sem=frecv.at[a, k],
        device_id=(x, y, 1 - z), device_id_type=MESH).start()


def _passed_on(o_refs, fsend, frecv):
    x, y, z = _place()
    for a, o in enumerate(o_refs):
        for k, (dx, dy) in enumerate(_CHIP_OFFSETS):
            other = 2 * _flip(x, dx) + _flip(y, dy)
            got = _half(o, other, 1 - z)
            gave = _half(o, other, z)
            pltpu.make_async_remote_copy(
                src_ref=got, dst_ref=got, send_sem=fsend.at[a, k], recv_sem=frecv.at[a, k],
                device_id=(x, y, 1 - z), device_id_type=MESH).wait_recv()
            pltpu.make_async_remote_copy(
                src_ref=gave, dst_ref=gave, send_sem=fsend.at[a, k], recv_sem=frecv.at[a, k],
                device_id=(x, y, 1 - z), device_id_type=MESH).wait_send()


def _gather_finish(o_refs, send, recv, fsend, frecv):
    _gather_landed(o_refs, send, recv, functools.partial(_pass_on, o_refs, fsend, frecv))
    _passed_on(o_refs, fsend, frecv)


class _Rider:
    def __init__(self, ins, out_shapes, sems, start, finish, aliases=None, in_specs=None, out_specs=None):
        self.ins, self.out_shapes, self.sems = list(ins), list(out_shapes), list(sems)
        self.start, self.finish, self.aliases = start, finish, dict(aliases or {})
        self.in_specs = list(in_specs) if in_specs else [_ANY] * len(self.ins)
        self.out_specs = list(out_specs) if out_specs else [_ANY] * len(self.out_shapes)


def _run_rider(rider, name):
    r_in, r_out = len(rider.ins), len(rider.out_shapes)

    def body(*refs):
        ins, outs, sems = refs[:r_in], refs[r_in:r_in + r_out], refs[r_in + r_out:]
        rider.start(ins, outs, sems)
        rider.finish(ins, outs, sems)

    return pl.pallas_call(
        body, name=name, in_specs=rider.in_specs, out_specs=rider.out_specs, out_shape=rider.out_shapes,
        input_output_aliases=rider.aliases, scratch_shapes=rider.sems,
    )(*rider.ins)


def _hosted_call(body, args, *, name, grid, in_specs, out_specs, out_shape, scratch_shapes=(), sem, rider=None,
                 prefetch=()):
    scratch_shapes = list(scratch_shapes)
    n_pf, n_in, n_out, n_sc = len(prefetch), len(in_specs), len(out_specs), len(scratch_shapes)
    r_in, r_out = (len(rider.ins), len(rider.out_shapes)) if rider else (0, 0)
    last = tuple(g - 1 for g in grid)

    def hosted(*refs):
        p = 0
        parts = []
        for cnt in (n_pf, n_in, r_in, n_out, r_out, n_sc):
            parts.append(refs[p:p + cnt])
            p += cnt
        pf, ins, r_ins, outs, r_outs, scratch = parts
        sems = refs[p:]
        ids = [pl.program_id(a) for a in range(len(grid))]
        is_first = functools.reduce(jnp.logical_and, [i == 0 for i in ids])
        is_last = functools.reduce(jnp.logical_and, [i == e for i, e in zip(ids, last)])

        @pl.when(is_first)
        def _():
            rider.start(r_ins, r_outs, sems)

        body(*pf, *ins, *outs, *scratch)

        @pl.when(is_last)
        def _():
            rider.finish(r_ins, r_outs, sems)

    if rider is None:
        kern, all_in, all_out, shapes, scratch, aliases, extra = body, list(in_specs), list(out_specs), list(out_shape), \
            scratch_shapes, {}, []
    else:
        kern, all_in, all_out = hosted, list(in_specs) + rider.in_specs, list(out_specs) + rider.out_specs
        shapes, scratch, extra = list(out_shape) + rider.out_shapes, scratch_shapes + rider.sems, rider.ins
        aliases = {n_pf + n_in + i: n_out + j for i, j in rider.aliases.items()}
        sem = ("arbitrary",) * len(grid)
    if prefetch:
        spec = dict(grid_spec=pltpu.PrefetchScalarGridSpec(
            num_scalar_prefetch=n_pf, grid=grid, in_specs=all_in, out_specs=all_out, scratch_shapes=scratch))
    else:
        spec = dict(grid=grid, in_specs=all_in, out_specs=all_out, scratch_shapes=scratch)
    res = pl.pallas_call(kern, name=name, out_shape=shapes, input_output_aliases=aliases,
                         compiler_params=_params(sem, VMEM_LIMIT), **spec)(*prefetch, *args, *extra)
    return list(res[:n_out]), list(res[n_out:])


def _gather_rider(ws):
    n = len(ws)
    return _Rider(
        ws, [jax.ShapeDtypeStruct(w.shape, w.dtype) for w in ws], [pltpu.SemaphoreType.DMA((n, 3))] * 4,
        lambda ins, outs, sems: _gather_send(outs, sems[0], sems[1]),
        lambda ins, outs, sems: _gather_finish(outs, *sems),
        aliases={a: a for a in range(n)})


def _gather_ici_rider(ws):
    n = len(ws)
    return _Rider(
        ws, [jax.ShapeDtypeStruct(w.shape, w.dtype) for w in ws], [pltpu.SemaphoreType.DMA((n, 3))] * 2,
        lambda ins, outs, sems: _gather_send(outs, sems[0], sems[1]),
        lambda ins, outs, sems: _gather_landed(outs, sems[0], sems[1]),
        aliases={a: a for a in range(n)})


def _gather_d2d_rider(ws):
    n = len(ws)

    def start(ins, outs, sems):
        x, y, z = _place()
        for a, o in enumerate(outs):
            for k, (dx, dy) in enumerate(_CHIP_OFFSETS):
                _pass_on(outs, sems[0], sems[1], a, k, _half(o, 2 * _flip(x, dx) + _flip(y, dy), z))

    return _Rider(
        ws, [jax.ShapeDtypeStruct(w.shape, w.dtype) for w in ws], [pltpu.SemaphoreType.DMA((n, 3))] * 2,
        start, lambda ins, outs, sems: _passed_on(outs, sems[0], sems[1]), aliases={a: a for a in range(n)})


def _copies_rider(ins, out_shapes, sem_shape, make):
    def start(r_ins, r_outs, sems):
        for cp in make(r_ins, r_outs, sems[0], sems[1]):
            cp.start()

    def finish(r_ins, r_outs, sems):
        for cp in make(r_ins, r_outs, sems[0], sems[1]):
            cp.wait()

    return _Rider(ins, out_shapes, [pltpu.SemaphoreType.DMA(sem_shape)] * 2, start, finish)


def _exchange_rider(gs):
    def make(g_refs, r_refs, send, recv):
        x, y, z = _place()
        return [pltpu.make_async_remote_copy(
            src_ref=g.at[:, pl.ds((1 - z) * (g.shape[1] // 2), g.shape[1] // 2)], dst_ref=r, send_sem=send.at[a],
            recv_sem=recv.at[a], device_id=(x, y, 1 - z), device_id_type=MESH)
            for a, (g, r) in enumerate(zip(g_refs, r_refs))]

    shapes = [jax.ShapeDtypeStruct((g.shape[0], g.shape[1] // 2, g.shape[2]), g.dtype) for g in gs]
    return _copies_rider(gs, shapes, (len(gs),), make)


def _add_half(g, recv, core, name):
    s, r, c = g.shape
    r2 = r // 2
    rb = r2
    for cand in (256, 128, 64):
        if r2 % cand == 0:
            rb = cand
            break
    g4 = g.reshape(s, 2, r2, c)

    def body(core_ref, g_ref, r_ref, o_ref):
        o_ref[...] = (g_ref[...].astype(F32) + r_ref[...].astype(F32)).astype(BF16)

    return pl.pallas_call(
        body, name=name,
        grid_spec=pltpu.PrefetchScalarGridSpec(
            num_scalar_prefetch=1, grid=(s, r2 // rb),
            in_specs=[pl.BlockSpec((None, None, rb, c), lambda i, j, cr: (i, cr[0], j, 0)),
                      pl.BlockSpec((None, rb, c), lambda i, j, cr: (i, j, 0))],
            out_specs=pl.BlockSpec((None, rb, c), lambda i, j, cr: (i, j, 0))),
        out_shape=jax.ShapeDtypeStruct((s, r2, c), BF16),
        compiler_params=_params(("parallel", "parallel")),
    )(core, g4, recv)


def _scatter_rider(ps):
    def make(p_refs, o_refs, send, recv):
        x, y, z = _place()
        copies = []
        for a, (p, o) in enumerate(zip(p_refs, o_refs)):
            for k, (dx, dy) in enumerate(_CHIP_OFFSETS):
                other = 2 * _flip(x, dx) + _flip(y, dy)
                copies.append(pltpu.make_async_remote_copy(
                    src_ref=p.at[other], dst_ref=o.at[k], send_sem=send.at[a, k], recv_sem=recv.at[a, k],
                    device_id=(_flip(x, dx), _flip(y, dy), z), device_id_type=MESH))
        return copies

    shapes = [jax.ShapeDtypeStruct((3,) + p.shape[1:], p.dtype) for p in ps]
    return _copies_rider(ps, shapes, (len(ps), 3), make)


def _sum_chips(p, landed, chip, name):
    _, r2, c = p.shape
    rb = r2
    for cand in (256, 128, 64):
        if r2 % cand == 0:
            rb = cand
            break

    def body(s_ref, p_ref, l_ref, o_ref):
        acc = p_ref[...].astype(F32)
        for k in range(3):
            acc = acc + l_ref[k].astype(F32)
        o_ref[...] = acc

    return pl.pallas_call(
        body, name=name,
        grid_spec=pltpu.PrefetchScalarGridSpec(
            num_scalar_prefetch=1, grid=(r2 // rb,),
            in_specs=[pl.BlockSpec((None, rb, c), lambda i, s: (s[0], i, 0)),
                      pl.BlockSpec((3, rb, c), lambda i, s: (0, i, 0))],
            out_specs=pl.BlockSpec((rb, c), lambda i, s: (i, 0))),
        out_shape=jax.ShapeDtypeStruct((r2, c), F32),
        compiler_params=_params(("parallel",)),
    )(chip, p, landed)


def _swap_rider(hs):
    def make(h_refs, o_refs, send, recv):
        x, y, z = _place()
        return [pltpu.make_async_remote_copy(
            src_ref=h, dst_ref=o, send_sem=send.at[a], recv_sem=recv.at[a], device_id=(x, y, 1 - z),
            device_id_type=MESH) for a, (h, o) in enumerate(zip(h_refs, o_refs))]

    return _copies_rider(hs, [jax.ShapeDtypeStruct(h.shape, h.dtype) for h in hs], (len(hs),), make)


SMALL_ROWS = 32
PACK_ROWS = 16


def _pack_small(st_post, st_ret, st_pre):
    d = st_post.shape[2]

    def body(po_ref, re_ref, pr_ref, o_ref):
        o_ref[...] = jnp.zeros(o_ref.shape, F32)
        o_ref[0:1, :] = pr_ref[0, 2:3, :] + pr_ref[1, 2:3, :] + pr_ref[2, 2:3, :]
        o_ref[1:2, :] = po_ref[0, 4:5, :] + po_ref[1, 4:5, :]
        o_ref[2:3, :] = po_ref[0, 5:6, :] + po_ref[1, 5:6, :]
        o_ref[3:4, 0:512] = re_ref[0, 0:1, :] + re_ref[1, 0:1, :]
        o_ref[4:5, :] = pr_ref[0, 3:4, :] + pr_ref[1, 3:4, :] + pr_ref[2, 3:4, :]
        o_ref[5:6, :] = pr_ref[0, 4:5, :] + pr_ref[1, 4:5, :] + pr_ref[2, 4:5, :]
        lane = lax.broadcasted_iota(jnp.int32, (1, LANES), 1)
        for row, src in ((6, 1), (10, 2)):
            acc = jnp.zeros((1, LANES), F32)
            for hd in range(HEADS):
                grp = re_ref[0, src:src + 1, hd * LANES:(hd + 1) * LANES] + re_ref[1, src:src + 1, hd * LANES:(hd + 1) * LANES]
                acc = acc + jnp.where(lane == hd, grp, 0.0)
            o_ref[row:row + 1, 0:LANES] = acc
        o_ref[7:8, :] = po_ref[0, 6:7, :] + po_ref[1, 6:7, :]
        o_ref[8:9, :] = pr_ref[2, 0:1, :]
        o_ref[9:10, :] = pr_ref[2, 1:2, :]
        for e in range(2):
            b = 12 + 6 * e
            o_ref[b:b + 1, :] = pr_ref[e, 0:1, :]
            o_ref[b + 1:b + 2, :] = pr_ref[e, 1:2, :]
            o_ref[b + 2:b + 3, :] = po_ref[e, 3:4, :]
            o_ref[b + 3:b + 4, :] = po_ref[e, 0:1, :]
            o_ref[b + 4:b + 5, :] = po_ref[e, 1:2, :]
            o_ref[b + 5:b + 6, :] = po_ref[e, 2:3, :]

    return pl.pallas_call(body, name="pack_small", out_shape=jax.ShapeDtypeStruct((SMALL_ROWS, d), F32))(st_post, st_ret, st_pre)


def _small_reduce(gathered):
    d = gathered.shape[2]

    def body(g_ref, o_ref):
        tot = g_ref[0, 0:PACK_ROWS, :]
        for dev in range(1, N_DEV):
            tot = tot + g_ref[dev, 0:PACK_ROWS, :]
        o_ref[0:PACK_ROWS, :] = tot
        for j in range(6):
            acc = g_ref[0, 12 + j:13 + j, :] + g_ref[0, 18 + j:19 + j, :]
            for dev in range(1, N_DEV):
                acc = acc + g_ref[dev, 12 + j:13 + j, :] + g_ref[dev, 18 + j:19 + j, :]
            if j < 2:
                acc = acc + o_ref[8 + j:9 + j, :]
            o_ref[PACK_ROWS + j:PACK_ROWS + j + 1, :] = acc
        o_ref[PACK_ROWS + 6:PACK_ROWS + 8, :] = jnp.zeros((2, d), F32)

    return pl.pallas_call(body, name="small_reduce", out_shape=jax.ShapeDtypeStruct((PACK_ROWS + 8, d), F32))(gathered)


_SMALL = (("g_attn", 0, 1024), ("g_ffn", 1, 1024), ("g_final", 2, 1024), ("g_ret", 3, 512), ("g_q_lora", 4, 384),
          ("g_kv_lora", 5, 256), ("ret_decay_fwd", 6, HEADS), ("ret_decay_bwd", 10, HEADS))
_SMALL_NAMES = tuple(s[0] for s in _SMALL) + ("c_ctx", "b_ada")


def _small_final(tot, dcc, sg8, ws, ms, vs):
    d = tot.shape[1]
    n = len(_SMALL_NAMES)

    def body(*refs):
        t_ref, dcc_ref, sg_ref = refs[0:3]
        w_refs, m_refs, v_refs = refs[3:3 + n], refs[3 + n:3 + 2 * n], refs[3 + 2 * n:3 + 3 * n]
        outs = refs[3 + 3 * n:]
        g_refs, d_refs, mo_refs, vo_refs = outs[0:n], outs[n:2 * n], outs[2 * n:3 * n], outs[3 * n:4 * n]
        l_ref = outs[4 * n]

        def update(i, g, sl=None):
            pick = (lambda r: r[...]) if sl is None else (lambda r: r[:, sl])
            dl, mn, vn = _adam_math(pick(w_refs[i]), g, pick(m_refs[i]), pick(v_refs[i]))
            if sl is None:
                g_refs[i][...], d_refs[i][...], mo_refs[i][...], vo_refs[i][...] = g, dl, mn, vn
            else:
                g_refs[i][:, sl], d_refs[i][:, sl], mo_refs[i][:, sl], vo_refs[i][:, sl] = g, dl, mn, vn

        for i, (name, row, width) in enumerate(_SMALL):
            g = t_ref[row:row + 1, 0:width]
            if name == "ret_decay_fwd":
                g = g * sg_ref[0:1, 0:width]
            elif name == "ret_decay_bwd":
                g = g * sg_ref[1:2, 0:width]
            update(i, g)
        i_cc, i_b = n - 2, n - 1
        cc = w_refs[i_cc][...]
        s = 1.0 / (1.0 + jnp.exp(-cc))
        dsilu = dcc_ref[0, 0:1, :] + dcc_ref[2, 0:1, :] + dcc_ref[4, 0:1, :] + dcc_ref[6, 0:1, :]
        update(i_cc, dsilu * (s * (1.0 + cc * (1.0 - s))))
        for j in range(6):
            update(i_b, t_ref[PACK_ROWS + j:PACK_ROWS + j + 1, :], pl.ds(j * d, d))
        l_ref[...] = jnp.broadcast_to((0.5 / d) * jnp.sum(t_ref[7:8, :], keepdims=True), l_ref.shape)

    shapes = [jax.ShapeDtypeStruct(a.shape, F32) for a in ws]
    outs = pl.pallas_call(
        body, name="small_final", out_shape=shapes * 4 + [jax.ShapeDtypeStruct((8, LANES), F32)],
    )(tot, dcc, sg8, *ws, *ms, *vs)
    return outs[0:n], outs[n:2 * n], outs[2 * n:3 * n], outs[3 * n:4 * n], outs[4 * n]


_WEIGHTS = ("c_ctx", "w_ada", "b_ada", "g_attn", "g_ffn", "w_in", "ret_decay_fwd", "ret_decay_bwd", "g_ret", "g_q_lora",
            "w_uq", "g_kv_lora", "w_ukv", "w_out", "w_ff1", "w_ff2", "g_final")
_BIG = ("w_in", "w_uq", "w_ukv", "w_out", "w_ff1", "w_ff2")
_TRANSPOSED = ("w_in", "w_uq")


def kernel(x, c, ctx, c_ctx, w_ada, b_ada, g_attn, g_ffn, w_in, ret_decay_fwd, ret_decay_bwd, g_ret, g_q_lora, w_uq, g_kv_lora, w_ukv, w_out, w_ff1, w_ff2, g_final, loss_target, m_c_ctx, m_w_ada, m_b_ada, m_g_attn, m_g_ffn, m_w_in, m_ret_decay_fwd, m_ret_decay_bwd, m_g_ret, m_g_q_lora, m_w_uq, m_g_kv_lora, m_w_ukv, m_w_out, m_w_ff1, m_w_ff2, m_g_final, v_c_ctx, v_w_ada, v_b_ada, v_g_attn, v_g_ffn, v_w_in, v_ret_decay_fwd, v_ret_decay_bwd, v_g_ret, v_g_q_lora, v_w_uq, v_g_kv_lora, v_w_ukv, v_w_out, v_w_ff1, v_w_ff2, v_g_final):
    w = dict(c_ctx=c_ctx, w_ada=w_ada, b_ada=b_ada, g_attn=g_attn, g_ffn=g_ffn, w_in=w_in, ret_decay_fwd=ret_decay_fwd,
             ret_decay_bwd=ret_decay_bwd, g_ret=g_ret, g_q_lora=g_q_lora, w_uq=w_uq, g_kv_lora=g_kv_lora, w_ukv=w_ukv,
             w_out=w_out, w_ff1=w_ff1, w_ff2=w_ff2, g_final=g_final)
    m = dict(c_ctx=m_c_ctx, w_ada=m_w_ada, b_ada=m_b_ada, g_attn=m_g_attn, g_ffn=m_g_ffn, w_in=m_w_in,
             ret_decay_fwd=m_ret_decay_fwd, ret_decay_bwd=m_ret_decay_bwd, g_ret=m_g_ret, g_q_lora=m_g_q_lora, w_uq=m_w_uq,
             g_kv_lora=m_g_kv_lora, w_ukv=m_w_ukv, w_out=m_w_out, w_ff1=m_w_ff1, w_ff2=m_w_ff2, g_final=m_g_final)
    v = dict(c_ctx=v_c_ctx, w_ada=v_w_ada, b_ada=v_b_ada, g_attn=v_g_attn, g_ffn=v_g_ffn, w_in=v_w_in,
             ret_decay_fwd=v_ret_decay_fwd, ret_decay_bwd=v_ret_decay_bwd, g_ret=v_g_ret, g_q_lora=v_g_q_lora, w_uq=v_w_uq,
             g_kv_lora=v_g_kv_lora, w_ukv=v_w_ukv, w_out=v_w_out, w_ff1=v_w_ff1, w_ff2=v_w_ff2, g_final=v_g_final)
    xi, yi, ci = lax.axis_index("x"), lax.axis_index("y"), lax.axis_index("c")
    chip = 2 * xi + yi
    dev = 2 * chip + ci
    nex, seq, d = x.shape
    n_ada = w_ada.shape[2]

    dec = jnp.zeros((8, LANES), F32).at[0, :HEADS].set(ret_decay_fwd[0]).at[1, :HEADS].set(ret_decay_bwd[0])
    lg8, sg8 = _decay_prep(dec)
    lg = lg8[:2, :HEADS]

    def shard_of(t, k):
        return t[k][0].T if k in _TRANSPOSED else t[k][0]

    shard = {k: shard_of(w, k) for k in _BIG}
    head_rows = MLA_NOPE + MLA_ROPE
    shard["w_uq"] = jnp.pad(shard["w_uq"], ((0, MLA_HEAD - head_rows), (0, 0)))
    slot = chip.reshape(1).astype(jnp.int32)
    core = ci.reshape(1).astype(jnp.int32)
    slots = {k: _cast_into_slot(shard[k], slot, "cast_" + k)[0] for k in _BIG if k != "w_ff1"}
    slots["w_ff1"], (w_in_x, w_uq_x, w_ukv_x, c8) = _cast_into_slot(
        shard["w_ff1"], slot, "cast_w_ff1",
        rider=_merge_riders(_gather_ici_rider([slots[k] for k in _EARLY]),
                            _gather8_rider(jnp.pad(c, ((0, 8 - nex), (0, 0))), in_vmem=False)))

    a_in = jnp.concatenate([c8[:, :nex].reshape(N_DEV * nex, d), c_ctx.reshape(1, d), jnp.zeros((7, d), F32)], axis=0)
    b_sh = lax.dynamic_slice(b_ada, (0, chip * n_ada), (1, n_ada))
    mod_sh = _mod_fwd(a_in, w_ada[0], b_sh)
    mod8, w_in_f, w_uq_k, w_ukv_k = _run_rider(
        _merge_riders(_gather8_rider(mod_sh), _gather_d2d_rider([w_in_x, w_uq_x, w_ukv_x])), "ag_early")
    w_in_k = jnp.pad(w_in_f.reshape(IN_COLS, d), ((0, IN_PAD - IN_COLS), (0, 0)))
    mod_all = mod8[0::2].transpose(1, 0, 2).reshape(a_in.shape[0], N_CHIPS * n_ada)
    mod_me = lax.dynamic_slice(mod_all, (nex * dev, 0), (nex, N_CHIPS * n_ada)).reshape(nex, 6, d)
    mod_c = mod_all[N_DEV * nex].reshape(1, 6, d)
    modv = jnp.pad(jnp.concatenate([mod_me, mod_c], axis=0), ((0, 0), (0, 2), (0, 0)))

    gx, g_early, late, st_post, st_ret, st_pre = _local_step(
        x, ctx, loss_target, modv, lg, g_attn, g_ffn, g_final.reshape(1, d), g_ret, g_q_lora, g_kv_lora,
        w_in_k, w_uq_k, w_ukv_k, [slots[k] for k in _LATE], (core, slot))

    g4 = [
        g_early[0][:IN_COLS].reshape(N_CHIPS, IN_COLS // N_CHIPS, d),
        g_early[1].reshape(N_CHIPS, MLA_HEAD, Q_LORA)[:, :head_rows],
        g_early[2],
    ]
    *got, gathered = _run_rider(
        _merge_riders(_exchange_rider(g4), _swap_rider(late), _gather8_rider(_pack_small(st_post, st_ret, st_pre))),
        "rs_exchange")
    got, late_theirs = got[:len(g4)], got[len(g4):]
    partial = [_add_half(g, r, core, "add_half_" + k) for g, r, k in zip(g4, got, _EARLY)]
    tot = _small_reduce(gathered)
    dm = jnp.concatenate([
        gathered[:, 12:24].reshape(N_DEV * nex, 6 * d),
        jnp.concatenate([tot[8:10].reshape(1, 2 * d), jnp.zeros((1, 4 * d), F32)], axis=1),
        jnp.zeros((7, 6 * d), F32)], axis=0)
    dm_sh = lax.dynamic_slice(dm, (0, chip * n_ada), (dm.shape[0], n_ada))
    g_ada, da = _mod_bwd(a_in, dm_sh, w_ada[0])
    *landed, dcc = _run_rider(_merge_riders(_scatter_rider(partial), _gather8_rider(da[N_DEV * nex:])), "rs_scatter")
    mine = [_sum_chips(p, l, slot, "sum_chips_" + k) for p, l, k in zip(partial, landed, _EARLY)]
    theirs = _run_rider(_swap_rider(mine), "rs_swap")
    halves = dict(zip(_EARLY, zip(mine, theirs)))
    halves.update(zip(_LATE, zip(late, late_theirs)))
    grad, delta, new_m, new_v = {}, {}, {}, {}
    for k in _BIG:
        a, b = halves[k]
        res = _adamw_halves(shard_of(w, k), a, b, shard_of(m, k), shard_of(v, k), core, "adamw_" + k)
        grad[k], delta[k], new_m[k], new_v[k] = [(o.T if k in _TRANSPOSED else o).reshape(w[k].shape) for o in res]

    shp = w_ada.shape
    outs, _ = _adamw(w_ada[0], g_ada, m["w_ada"][0], v["w_ada"][0], "adamw_w_ada")
    grad["w_ada"] = g_ada.reshape(shp)
    delta["w_ada"], new_m["w_ada"], new_v["w_ada"] = [o.reshape(shp) for o in outs]
    rows = [{k: t[k].reshape(1, -1) for k in _SMALL_NAMES} for t in (w, m, v)]
    small = _small_final(tot, dcc, sg8, *[[t[k] for k in _SMALL_NAMES] for t in rows])
    for res, outs in zip((grad, delta, new_m, new_v), small[:4]):
        for k, o in zip(_SMALL_NAMES, outs):
            res[k] = o.reshape(w[k].shape)
    return (small[4][0, 0], gx, *[grad[k] for k in _WEIGHTS], *[delta[k] for k in _WEIGHTS],
            *[new_m[k] for k in _WEIGHTS], *[new_v[k] for k in _WEIGHTS])
```

```python
import functools
import math

import jax
import jax.numpy as jnp
from jax import lax
from jax.experimental import pallas as pl
from jax.experimental.pallas import tpu as pltpu

F32 = jnp.float32
BF16 = jnp.bfloat16
MESH = pl.DeviceIdType.MESH

EPS = 1e-6
D_MODEL = 1024
D_FF = 4096
HEADS = 4
RET_DK = 64
RET_DV = 128
MLA_NOPE = 128
MLA_ROPE = 64
MLA_HEAD = 256
Q_LORA = 384
KV_LORA = 256
GRID_W = 64
ROPE_BASE = 10000.0
IN_COLS = 2240
IN_PAD = 2304
PG_COLS = 1152
N_CHIPS = 4
N_DEV = 8
LANES = 128
ADAM_LR = 0.001
ADAM_B1 = 0.9
ADAM_B2 = 0.999
ADAM_EPS = 1e-08
ADAM_WD = 0.01
ADAM_STEP = 10
VMEM_LIMIT = 56 * 1024 * 1024


def _dot(a, b):
    return jnp.dot(a, b, preferred_element_type=F32)


def _dot_nt(a, b):
    return lax.dot_general(a, b, (((1,), (1,)), ((), ())), preferred_element_type=F32)


def _dot_tn(a, b):
    return lax.dot_general(a, b, (((0,), (0,)), ((), ())), preferred_element_type=F32)


def _params(sem=None, vmem=None):
    return pltpu.CompilerParams(dimension_semantics=sem, vmem_limit_bytes=vmem)


def _full(shape):
    n = len(shape)
    return pl.BlockSpec(shape, lambda *_: (0,) * n)


def _rope(x, cos, sin):
    w = x.shape[-1]
    lo = (lax.broadcasted_iota(jnp.int32, (1, w), 1) % 64) < 32
    swapped = jnp.where(lo, pltpu.roll(x, w - 32, 1), pltpu.roll(x, 32, 1))
    return x * cos + swapped * sin


def _rope_t(g, cos, sin):
    w = g.shape[-1]
    lo = (lax.broadcasted_iota(jnp.int32, (1, w), 1) % 64) < 32
    t = g * sin
    swapped = jnp.where(lo, pltpu.roll(t, w - 32, 1), pltpu.roll(t, 32, 1))
    return g * cos + swapped


def _rope_tables(seq, tm):
    rows = seq // GRID_W
    row = jnp.repeat(jnp.arange(rows, dtype=F32), GRID_W)
    col = jnp.tile(jnp.arange(GRID_W, dtype=F32), rows)
    n_freq = RET_DK // 4
    freq = ROPE_BASE ** (-jnp.arange(n_freq, dtype=F32) / n_freq)
    ang = jnp.concatenate([row[:, None] * freq, col[:, None] * freq], axis=-1)
    cos, sin = jnp.cos(ang), jnp.sin(ang)
    cos_t = jnp.tile(jnp.concatenate([cos, cos], -1), (1, HEADS))
    sin_t = jnp.tile(jnp.concatenate([-sin, sin], -1), (1, HEADS))
    cos_t = jnp.concatenate([cos_t, jnp.ones((tm, 4 * RET_DK), F32)], 0)
    sin_t = jnp.concatenate([sin_t, jnp.zeros((tm, 4 * RET_DK), F32)], 0)
    return cos_t, sin_t


def _adam_math(w, g, m, v):
    mn = ADAM_B1 * m + (1.0 - ADAM_B1) * g
    vn = ADAM_B2 * v + (1.0 - ADAM_B2) * (g * g)
    m_hat = mn / (1.0 - ADAM_B1 ** ADAM_STEP)
    v_hat = vn / (1.0 - ADAM_B2 ** ADAM_STEP)
    return -ADAM_LR * (m_hat / (jnp.sqrt(v_hat) + ADAM_EPS) + ADAM_WD * w), mn, vn


def _cast_into_slot(w, slot, name, rider=None):
    r, c = w.shape
    rb = max(b for b in range(16, 257, 16) if r % b == 0)

    def body(s_ref, w_ref, o_ref):
        o_ref[...] = w_ref[...].astype(BF16)

    (out,), carried = _hosted_call(
        body, (w,), name=name, grid=(r // rb,), prefetch=(slot,),
        in_specs=[pl.BlockSpec((rb, c), lambda i, s: (i, 0))],
        out_specs=[pl.BlockSpec((None, rb, c), lambda i, s: (s[0], i, 0))],
        out_shape=[jax.ShapeDtypeStruct((N_CHIPS, r, c), BF16)], sem=("parallel",), rider=rider)
    return out, carried


def _adamw_halves(w, mine, theirs, m, v, core, name):
    r, c = w.shape
    r2 = r // 2
    rb = max(b for b in range(8, r2 + 1, 8) if r2 % b == 0 and b * c * 4 <= (1 << 21))
    nbh = r2 // rb

    def body(z_ref, w_ref, a_ref, b_ref, m_ref, v_ref, g_ref, d_ref, mo_ref, vo_ref):
        here = (pl.program_id(0) // nbh) == z_ref[0]
        gg = jnp.where(here, a_ref[...], b_ref[...])
        g_ref[...] = gg
        d_ref[...], mo_ref[...], vo_ref[...] = _adam_math(w_ref[...], gg, m_ref[...], v_ref[...])

    spec = pl.BlockSpec((rb, c), lambda i, z: (i, 0))
    a_spec = pl.BlockSpec((rb, c), lambda i, z: (jnp.clip(i - z[0] * nbh, 0, nbh - 1), 0))
    b_spec = pl.BlockSpec((rb, c), lambda i, z: (jnp.clip(i - (1 - z[0]) * nbh, 0, nbh - 1), 0))
    shp = jax.ShapeDtypeStruct((r, c), F32)
    return pl.pallas_call(
        body, name=name,
        grid_spec=pltpu.PrefetchScalarGridSpec(
            num_scalar_prefetch=1, grid=(r // rb,), in_specs=[spec, a_spec, b_spec, spec, spec], out_specs=[spec] * 4),
        out_shape=[shp] * 4,
        compiler_params=_params(("parallel",)),
    )(core, w, mine, theirs, m, v)


def _adamw(w, g, m, v, name, rider=None):
    r, c = w.shape
    rb = r
    for cand in (256, 128, 64, 32, 16, 8):
        if r % cand == 0 and cand * c * 4 <= (1 << 20):
            rb = cand
            break
    if r * c * 4 <= (1 << 20):
        rb = r

    def body(w_ref, g_ref, m_ref, v_ref, d_ref, mo_ref, vo_ref):
        d_ref[...], mo_ref[...], vo_ref[...] = _adam_math(w_ref[...], g_ref[...], m_ref[...], v_ref[...])

    spec = pl.BlockSpec((rb, c), lambda i: (i, 0))
    shp = jax.ShapeDtypeStruct((r, c), F32)
    return _hosted_call(
        body, (w, g, m, v), name=name, grid=(r // rb,), in_specs=[spec] * 4, out_specs=[spec] * 3, out_shape=[shp] * 3,
        sem=("parallel",), rider=rider)


def _adamw_halves_group(items, core, name, rider=None):
    c = items[0][0].shape[1]
    rb = 128
    n = len(items)
    nbs = [it[0].shape[0] // rb for it in items]
    starts = [sum(nbs[:s]) for s in range(n)]

    def body(z_ref, *refs):
        ins, outs = refs[:5 * n], refs[5 * n:]
        i = pl.program_id(0)
        for s in range(n):
            w_ref, a_ref, b_ref, m_ref, v_ref = ins[5 * s:5 * s + 5]
            g_ref, d_ref, mo_ref, vo_ref = outs[4 * s:4 * s + 4]

            @pl.when(jnp.logical_and(i >= starts[s], i < starts[s] + nbs[s]))
            def _():
                here = ((i - starts[s]) // (nbs[s] // 2)) == z_ref[0]
                gg = jnp.where(here, a_ref[...], b_ref[...])
                g_ref[...] = gg
                d_ref[...], mo_ref[...], vo_ref[...] = _adam_math(w_ref[...], gg, m_ref[...], v_ref[...])

    in_specs, out_specs, out_shape, args = [pl.BlockSpec(memory_space=pltpu.SMEM)], [], [], [core]
    for (w, a, b, m, v), nb, st in zip(items, nbs, starts):
        full = pl.BlockSpec((rb, c), lambda i, nb=nb, st=st: (jnp.clip(i - st, 0, nb - 1), 0))
        half = pl.BlockSpec((rb, c), lambda i, nb=nb, st=st: (jnp.clip(i - st, 0, nb - 1) % (nb // 2), 0))
        in_specs += [full, half, half, full, full]
        out_specs += [full] * 4
        out_shape += [jax.ShapeDtypeStruct(w.shape, F32)] * 4
        args += [w, a, b, m, v]
    res, carried = _hosted_call(body, args, name=name, grid=(sum(nbs),), in_specs=in_specs, out_specs=out_specs,
                                out_shape=out_shape, sem=("arbitrary",), rider=rider)
    return [tuple(res[4 * s:4 * s + 4]) for s in range(n)], carried


def _decay_prep(dec):
    def body(d_ref, lg_ref, sg_ref):
        d = d_ref[...]
        lg_ref[...] = jnp.minimum(d, 0.0) - jnp.log(1.0 + jnp.exp(-jnp.abs(d)))
        sg_ref[...] = 1.0 / (1.0 + jnp.exp(d))

    shp = jax.ShapeDtypeStruct(dec.shape, F32)
    return pl.pallas_call(body, name="decay_prep", out_shape=[shp, shp])(dec)


def _mod_fwd(a_in, w_ada, b_sh):
    rows, d = a_in.shape
    n = w_ada.shape[1]
    bn = 512

    def body(a_ref, w_ref, b_ref, o_ref):
        a = a_ref[...]
        s = (a / (1.0 + jnp.exp(-a))).astype(BF16)
        o_ref[...] = _dot(s, w_ref[...].astype(BF16)) + b_ref[...]

    return pl.pallas_call(
        body, name="mod_fwd", grid=(n // bn,),
        in_specs=[_full((rows, d)), pl.BlockSpec((d, bn), lambda j: (0, j)), pl.BlockSpec((1, bn), lambda j: (0, j))],
        out_specs=pl.BlockSpec((rows, bn), lambda j: (0, j)),
        out_shape=jax.ShapeDtypeStruct((rows, n), F32),
        compiler_params=_params(("parallel",)),
    )(a_in, w_ada, b_sh)


def _mod_bwd(a_in, dm, w_ada):
    rows, d = a_in.shape
    n = w_ada.shape[1]
    bn = 512
    nb = n // bn

    def body(a_ref, dm_ref, w_ref, gw_ref, da_ref):
        j = pl.program_id(0)
        a = a_ref[...]
        s = (a / (1.0 + jnp.exp(-a))).astype(BF16)
        dmb = dm_ref[...].astype(BF16)
        gw_ref[...] = _dot_tn(s, dmb)
        part = _dot_nt(dmb, w_ref[...].astype(BF16))

        @pl.when(j == 0)
        def _():
            da_ref[...] = part

        @pl.when(j > 0)
        def _():
            da_ref[...] += part

    return pl.pallas_call(
        body, name="mod_bwd", grid=(nb,),
        in_specs=[_full((rows, d)), pl.BlockSpec((rows, bn), lambda j: (0, j)), pl.BlockSpec((d, bn), lambda j: (0, j))],
        out_specs=[pl.BlockSpec((d, bn), lambda j: (0, j)), _full((rows, d))],
        out_shape=[jax.ShapeDtypeStruct((d, n), F32), jax.ShapeDtypeStruct((rows, d), F32)],
        compiler_params=_params(("arbitrary",)),
    )(a_in, dm, w_ada)


def _pre_fwd(x2, ctx2, modv, g_attn, w_in, g_q, g_kv, w_uq, w_ukv, cos_t, sin_t, *, seq, tm, rider=None):
    t_lat, d = x2.shape
    t_ctx = ctx2.shape[0]
    nl, nc = t_lat // tm, t_ctx // tm
    n_all = t_lat + t_ctx
    tpe = seq // tm
    nex = t_lat // seq

    def body(x_ref, c_ref, mod_ref, g_ref, win_ref, gq_ref, gkv_ref, wuq_ref, wukv_ref, cos_ref, sin_ref,
             h_ref, pg_ref, rq_ref, rk_ref, rv_ref, nq_ref, nkv_ref, q_ref, k_ref, v_ref):
        i = pl.program_id(0)
        xt = jnp.where(i < nl, x_ref[...], c_ref[...])
        sh = mod_ref[0, 0:1, :]
        sc = mod_ref[0, 1:2, :]
        r = lax.rsqrt(jnp.mean(xt * xt, axis=-1, keepdims=True) + EPS)
        hb = ((xt * r) * g_ref[...] * (1.0 + sc) + sh).astype(BF16)
        h_ref[...] = hb
        p = _dot_nt(hb, win_ref[...])
        cos = cos_ref[...]
        sin = sin_ref[...]
        rq_ref[...] = _rope(p[:, 0:256], cos, sin).astype(BF16)
        rk_ref[...] = _rope(p[:, 256:512] * (RET_DK ** -0.5), cos, sin).astype(BF16)
        rv_ref[...] = p[:, 512:1024].astype(BF16)
        pg_ref[...] = p[:, 1024:2176]
        cq = p[:, 1536:1920]
        ckv = p[:, 1920:2176]
        nqb = (cq * lax.rsqrt(jnp.mean(cq * cq, axis=-1, keepdims=True) + EPS) * gq_ref[...]).astype(BF16)
        nkvb = (ckv * lax.rsqrt(jnp.mean(ckv * ckv, axis=-1, keepdims=True) + EPS) * gkv_ref[...]).astype(BF16)
        nq_ref[...] = nqb
        nkv_ref[...] = nkvb
        cos1 = cos[:, 0:LANES]
        sin1 = sin[:, 0:LANES]
        kpe = _rope(p[:, 2176:2304], cos1, sin1).astype(BF16)
        for hd in range(HEADS):
            o = hd * MLA_HEAD
            qh = _dot_nt(nqb, wuq_ref[hd]) * MLA_SCALE
            q_ref[:, o:o + 128] = qh[:, 0:128].astype(BF16)
            q_ref[:, o + 128:o + 256] = _rope(qh[:, 128:256], cos1, sin1).astype(BF16)
            kvh = _dot(nkvb, wukv_ref[hd])
            k_ref[:, o:o + 128] = kvh[:, 0:128].astype(BF16)
            k_ref[:, o + 128:o + 256] = kpe
            v_ref[:, hd * 128:(hd + 1) * 128] = kvh[:, 128:256].astype(BF16)

    def tile(width):
        return pl.BlockSpec((tm, width), lambda i: (i, 0))

    widths = (d, PG_COLS, 256, 256, 512, Q_LORA, KV_LORA, HEADS * MLA_HEAD, HEADS * MLA_HEAD, HEADS * 128)
    dtypes = (BF16, F32, BF16, BF16, BF16, BF16, BF16, BF16, BF16, BF16)
    tab = pl.BlockSpec((tm, 256), lambda i: (jnp.where(i < nl, i % tpe, tpe), 0))
    return _hosted_call(
        body, (x2, ctx2, modv, g_attn, w_in, g_q, g_kv, w_uq, w_ukv, cos_t, sin_t), name="pre_fwd", grid=(nl + nc,),
        in_specs=[
            pl.BlockSpec((tm, d), lambda i: (jnp.minimum(i, nl - 1), 0)),
            pl.BlockSpec((tm, d), lambda i: (jnp.maximum(i - nl, 0), 0)),
            pl.BlockSpec((1, 8, d), lambda i: (jnp.minimum(i // tpe, nex), 0, 0)),
            _full((1, d)), _full(w_in.shape), _full((1, Q_LORA)), _full((1, KV_LORA)),
            _full(w_uq.shape), _full(w_ukv.shape), tab, tab,
        ],
        out_specs=[tile(w) for w in widths],
        out_shape=[jax.ShapeDtypeStruct((n_all, w), dt) for w, dt in zip(widths, dtypes)],
        sem=("parallel",), rider=rider)


def _post(yret, ymla, x2, tgt2, modv, g_ffn, g_fin, w_out, w_ff1, w_ff2, *, seq, tm):
    t_lat, d = x2.shape
    nl = t_lat // tm
    tpe = seq // tm
    nex = t_lat // seq
    n_slab = w_ff1.shape[0]
    fs = w_ff1.shape[2]

    def body(yr_ref, ym_ref, x_ref, t_ref, mod_ref, gf_ref, gl_ref, wo_ref, w1_ref, w2_ref,
             mix_ref, a_ref, du_ref, h2_ref, df_ref, dmo_ref, dmix_ref, dxm_ref, st_ref, ru_ref):
        i = pl.program_id(0)
        gt_a = mod_ref[0, 2:3, :]
        sh_f = mod_ref[0, 3:4, :]
        sc_f = mod_ref[0, 4:5, :]
        gt_f = mod_ref[0, 5:6, :]
        g_ffn_v = gf_ref[...]
        g_fin_v = gl_ref[...]
        yr = yr_ref[...]
        ym = ym_ref[...]
        mix_ref[:, 0:512] = yr
        mix_ref[:, 512:1024] = ym
        op = _dot(yr, wo_ref[0:512, :]) + _dot(ym, wo_ref[512:1024, :])
        x_mid = x_ref[...] + gt_a * op
        r2 = lax.rsqrt(jnp.mean(x_mid * x_mid, axis=-1, keepdims=True) + EPS)
        xh2 = x_mid * r2
        h2b = (xh2 * g_ffn_v * (1.0 + sc_f) + sh_f).astype(BF16)
        h2_ref[...] = h2b
        f = jnp.zeros((tm, d), F32)
        for s in range(n_slab):
            ru = jnp.maximum(_dot(h2b, w1_ref[s]), 0.0)
            ru_ref[:, s * fs:(s + 1) * fs] = ru
            ab = (ru * ru).astype(BF16)
            a_ref[:, s * fs:(s + 1) * fs] = ab
            f = f + _dot(ab, w2_ref[s * fs:(s + 1) * fs, :])
        x_out = x_mid + gt_f * f
        r3 = lax.rsqrt(jnp.mean(x_out * x_out, axis=-1, keepdims=True) + EPS)
        xh3 = x_out * r3
        err = xh3 * g_fin_v - t_ref[...]
        dy = err * (1.0 / d)
        dxh3 = dy * g_fin_v
        dx_out = r3 * (dxh3 - xh3 * jnp.mean(dxh3 * xh3, axis=-1, keepdims=True))
        dfb = (dx_out * gt_f).astype(BF16)
        df_ref[...] = dfb
        dh2 = jnp.zeros((tm, d), F32)
        for s in range(n_slab):
            da = _dot_nt(dfb, w2_ref[s * fs:(s + 1) * fs, :])
            dub = (da * (2.0 * ru_ref[:, s * fs:(s + 1) * fs])).astype(BF16)
            du_ref[:, s * fs:(s + 1) * fs] = dub
            dh2 = dh2 + _dot_nt(dub, w1_ref[s])
        dxh2 = dh2 * (1.0 + sc_f) * g_ffn_v
        dx_mid = dx_out + r2 * (dxh2 - xh2 * jnp.mean(dxh2 * xh2, axis=-1, keepdims=True))
        dxm_ref[...] = dx_mid
        dmob = (dx_mid * gt_a).astype(BF16)
        dmo_ref[...] = dmob
        dmix_ref[...] = _dot_nt(dmob, wo_ref[...]).astype(BF16)

        def rsum(v):
            return jnp.sum(v, axis=0, keepdims=True)

        stats = jnp.concatenate([
            rsum(dh2), rsum(dh2 * xh2 * g_ffn_v), rsum(dx_out * f), rsum(dx_mid * op),
            rsum(dh2 * (1.0 + sc_f) * xh2), rsum(dy * xh3), rsum(err * err), jnp.zeros((1, d), F32)], axis=0)

        @pl.when(i % tpe == 0)
        def _():
            st_ref[0] = stats

        @pl.when(i % tpe != 0)
        def _():
            st_ref[0] += stats

    def tile(width):
        return pl.BlockSpec((tm, width), lambda i: (i, 0))

    widths = (d, D_FF, D_FF, d, d, d, d, d)
    dtypes = (BF16, BF16, BF16, BF16, BF16, BF16, BF16, F32)
    const = pl.Buffered(1)
    return pl.pallas_call(
        body, name="post", grid=(nl,),
        in_specs=[
            tile(512), tile(512), tile(d), tile(d),
            pl.BlockSpec((1, 8, d), lambda i: (i // tpe, 0, 0)),
            _full((1, d)), _full((1, d)),
            pl.BlockSpec(w_out.shape, lambda i: (0, 0), pipeline_mode=const),
            pl.BlockSpec(w_ff1.shape, lambda i: (0, 0, 0), pipeline_mode=const),
            pl.BlockSpec(w_ff2.shape, lambda i: (0, 0), pipeline_mode=const),
        ],
        out_specs=[tile(w) for w in widths] + [pl.BlockSpec((1, 8, d), lambda i: (i // tpe, 0, 0))],
        out_shape=[jax.ShapeDtypeStruct((t_lat, w), dt) for w, dt in zip(widths, dtypes)]
        + [jax.ShapeDtypeStruct((nex, 8, d), F32)],
        scratch_shapes=[pltpu.VMEM((tm, D_FF), F32)],
        compiler_params=_params(("arbitrary",), VMEM_LIMIT),
    )(yret, ymla, x2, tgt2, modv, g_ffn, g_fin, w_out, w_ff1, w_ff2)


def _pre_bwd(x2, ctx2, modv, g_attn, pg, drq, drk, dkc_r, drv, dvc_r, drg, dq_m, dkl, dkc, dvl, dvc, dxm,
             w_in, g_q, g_kv, w_uq, w_ukv, cos_t, sin_t, *, seq, tm, rider=None):
    t_lat, d = x2.shape
    t_ctx = ctx2.shape[0]
    nl, nc = t_lat // tm, t_ctx // tm
    n_all = t_lat + t_ctx
    tpe = seq // tm
    nex = t_lat // seq

    def body(x_ref, c_ref, mod_ref, g_ref, pg_ref, drq_ref, drk_ref, dkcr_ref, drv_ref, dvcr_ref, drg_ref,
             dq_ref, dkl_ref, dkc_ref, dvl_ref, dvc_ref, dxm_ref, win_ref, gq_ref, gkv_ref, wuq_ref, wukv_ref,
             cos_ref, sin_ref, dpb_ref, dqf_ref, dkvf_ref, gx_ref, st_ref):
        i = pl.program_id(0)
        lat = i < nl
        latf = lat.astype(F32)
        cos = cos_ref[...]
        sin = sin_ref[...]
        cos1 = cos[:, 0:LANES]
        sin1 = sin[:, 0:LANES]
        d_rq = _rope_t(drq_ref[...] * latf, cos, sin)
        d_rk = _rope_t(jnp.where(lat, drk_ref[...], dkcr_ref[...]), cos, sin) * (RET_DK ** -0.5)
        d_rv = jnp.where(lat, drv_ref[...], dvcr_ref[...])
        d_rg = drg_ref[...] * latf
        dq_all = dq_ref[...] * (latf * MLA_SCALE)
        dk_all = jnp.where(lat, dkl_ref[...], dkc_ref[...])
        dv_all = jnp.where(lat, dvl_ref[...], dvc_ref[...])
        dnq = jnp.zeros((tm, Q_LORA), F32)
        dnkv = jnp.zeros((tm, KV_LORA), F32)
        dkpe = jnp.zeros((tm, LANES), F32)
        for hd in range(HEADS):
            o = hd * MLA_HEAD
            dqh = jnp.concatenate([dq_all[:, o:o + 128], _rope_t(dq_all[:, o + 128:o + 256], cos1, sin1)],
                                  axis=1).astype(BF16)
            dqf_ref[:, o:o + 256] = dqh
            dnq = dnq + _dot(dqh, wuq_ref[hd])
            dkpe = dkpe + dk_all[:, o + 128:o + 256]
            dkvh = jnp.concatenate([dk_all[:, o:o + 128], dv_all[:, hd * 128:(hd + 1) * 128]], axis=1).astype(BF16)
            dkvf_ref[:, o:o + 256] = dkvh
            dnkv = dnkv + _dot_nt(dkvh, wukv_ref[hd])
        d_kpe = _rope_t(dkpe, cos1, sin1)
        pgv = pg_ref[...]
        cq = pgv[:, 512:896]
        ckv = pgv[:, 896:1152]
        rq_ = lax.rsqrt(jnp.mean(cq * cq, axis=-1, keepdims=True) + EPS)
        cqh = cq * rq_
        dcqh = dnq * gq_ref[...]
        d_cq = rq_ * (dcqh - cqh * jnp.mean(dcqh * cqh, axis=-1, keepdims=True))
        rkv_ = lax.rsqrt(jnp.mean(ckv * ckv, axis=-1, keepdims=True) + EPS)
        ckvh = ckv * rkv_
        dckvh = dnkv * gkv_ref[...]
        d_ckv = rkv_ * (dckvh - ckvh * jnp.mean(dckvh * ckvh, axis=-1, keepdims=True))
        dpb = jnp.concatenate([d_rq, d_rk, d_rv, d_rg, d_cq, d_ckv, d_kpe], axis=1).astype(BF16)
        dpb_ref[...] = dpb
        dh = _dot(dpb, win_ref[...])
        xt = jnp.where(lat, x_ref[...], c_ref[...])
        sc = mod_ref[0, 1:2, :]
        g = g_ref[...]
        r = lax.rsqrt(jnp.mean(xt * xt, axis=-1, keepdims=True) + EPS)
        xh = xt * r
        dxh = dh * (1.0 + sc) * g
        dx = r * (dxh - xh * jnp.mean(dxh * xh, axis=-1, keepdims=True))

        @pl.when(lat)
        def _():
            gx_ref[...] = dxm_ref[...] + dx

        def rsum(v):
            return jnp.sum(v, axis=0, keepdims=True)

        def widen(v):
            return jnp.concatenate([v, jnp.zeros((1, d - v.shape[1]), F32)], axis=1)

        stats = jnp.concatenate([
            rsum(dh), rsum(dh * xh * g), rsum(dh * (1.0 + sc) * xh), widen(rsum(dnq * cqh)), widen(rsum(dnkv * ckvh)),
            jnp.zeros((3, d), F32)], axis=0)
        first = jnp.logical_or(jnp.logical_and(lat, i % tpe == 0), i == nl)

        @pl.when(first)
        def _():
            st_ref[0] = stats

        @pl.when(jnp.logical_not(first))
        def _():
            st_ref[0] += stats

    def lat_tile(width):
        return pl.BlockSpec((tm, width), lambda i: (jnp.minimum(i, nl - 1), 0))

    def ctx_tile(width):
        return pl.BlockSpec((tm, width), lambda i: (jnp.maximum(i - nl, 0), 0))

    def tile(width):
        return pl.BlockSpec((tm, width), lambda i: (i, 0))

    tab = pl.BlockSpec((tm, 256), lambda i: (jnp.where(i < nl, i % tpe, tpe), 0))
    ex = pl.BlockSpec((1, 8, d), lambda i: (jnp.minimum(i // tpe, nex), 0, 0))
    return _hosted_call(
        body, (x2, ctx2, modv, g_attn, pg, drq, drk, dkc_r, drv, dvc_r, drg, dq_m, dkl, dkc, dvl, dvc, dxm,
               w_in, g_q, g_kv, w_uq, w_ukv, cos_t, sin_t), name="pre_bwd", grid=(nl + nc,),
        in_specs=[
            lat_tile(d), ctx_tile(d), ex, _full((1, d)), tile(PG_COLS),
            lat_tile(256), lat_tile(256), ctx_tile(256), lat_tile(512), ctx_tile(512), lat_tile(512),
            lat_tile(1024), lat_tile(1024), ctx_tile(1024), lat_tile(512), ctx_tile(512), lat_tile(d),
            _full(w_in.shape), _full((1, Q_LORA)), _full((1, KV_LORA)), _full(w_uq.shape), _full(w_ukv.shape),
            tab, tab,
        ],
        out_specs=[tile(IN_PAD), tile(1024), tile(1024), lat_tile(d), ex],
        out_shape=[
            jax.ShapeDtypeStruct((n_all, IN_PAD), BF16), jax.ShapeDtypeStruct((n_all, 1024), BF16),
            jax.ShapeDtypeStruct((n_all, 1024), BF16), jax.ShapeDtypeStruct((t_lat, d), F32),
            jax.ShapeDtypeStruct((nex + 1, 8, d), F32),
        ],
        sem=("arbitrary",), rider=rider)


MLA_SCALE = 1.0 / math.sqrt(MLA_NOPE + MLA_ROPE)
KEY_BLOCK = 2048


def _mla_specs(t_lat, seq, ctx_len, tq):
    nqt = seq // tq
    cb = t_lat // ctx_len
    q = pl.BlockSpec((tq, MLA_HEAD), lambda b, h, j: (b * nqt + j, h))
    kl = pl.BlockSpec((seq, MLA_HEAD), lambda b, h, j: (b, h))
    kc = pl.BlockSpec((ctx_len, MLA_HEAD), lambda b, h, j: (cb + b, h))
    vl = pl.BlockSpec((seq, 128), lambda b, h, j: (b, h))
    vc = pl.BlockSpec((ctx_len, 128), lambda b, h, j: (cb + b, h))
    o = pl.BlockSpec((tq, 128), lambda b, h, j: (b * nqt + j, h))
    return q, kl, kc, vl, vc, o


def _mla_fwd(q, k, v, *, t_lat, seq, ctx_len, tq, rider=None):
    nex = t_lat // seq

    def body(q_ref, kl_ref, kc_ref, vl_ref, vc_ref, o_ref, lse_ref):
        qb = q_ref[...]
        s = _dot_nt(qb, kl_ref[...])
        sc = _dot_nt(qb, kc_ref[...])
        m = jnp.maximum(jnp.max(s, axis=-1, keepdims=True), jnp.max(sc, axis=-1, keepdims=True))
        p = jnp.exp(s - m)
        pc = jnp.exp(sc - m)
        total = jnp.sum(p, axis=-1, keepdims=True) + jnp.sum(pc, axis=-1, keepdims=True)
        o = _dot(p.astype(BF16), vl_ref[...]) + _dot(pc.astype(BF16), vc_ref[...])
        o_ref[...] = (o * (1.0 / total)).astype(BF16)
        lse_ref[...] = jnp.broadcast_to(m + jnp.log(total), lse_ref.shape)

    qs, kl, kc, vl, vc, os_ = _mla_specs(t_lat, seq, ctx_len, tq)
    return _hosted_call(
        body, (q, k, k, v, v), name="mla_fwd", grid=(nex, HEADS, seq // tq),
        in_specs=[qs, kl, kc, vl, vc], out_specs=[os_, os_],
        out_shape=[jax.ShapeDtypeStruct((t_lat, HEADS * 128), BF16), jax.ShapeDtypeStruct((t_lat, HEADS * 128), F32)],
        sem=("parallel", "parallel", "arbitrary"), rider=rider)


def _mla_bwd(q, k, v, ymla, lse, dmix, *, t_lat, seq, ctx_len, tq, rider=None):
    nex = t_lat // seq
    nqt = seq // tq
    t_ctx = nex * ctx_len
    kb = min(KEY_BLOCK, seq)

    def body(q_ref, kl_ref, kc_ref, vl_ref, vc_ref, o_ref, lse_ref, do_ref, dq_ref, dkl_ref, dkc_ref, dvl_ref, dvc_ref):
        j = pl.program_id(2)

        @pl.when(j == 0)
        def _():
            dkl_ref[...] = jnp.zeros(dkl_ref.shape, F32)
            dkc_ref[...] = jnp.zeros(dkc_ref.shape, F32)
            dvl_ref[...] = jnp.zeros(dvl_ref.shape, F32)
            dvc_ref[...] = jnp.zeros(dvc_ref.shape, F32)

        qb = q_ref[...]
        dob = do_ref[...]
        delta = jnp.sum(dob.astype(F32) * o_ref[...].astype(F32), axis=-1, keepdims=True)
        lse_row = lse_ref[:, 0:1]

        def block(k_ref, v_ref, dk_ref, dv_ref, rows):
            kbl = k_ref[rows, :]
            vbl = v_ref[rows, :]
            p = jnp.exp(_dot_nt(qb, kbl) - lse_row)
            ds = (p * (_dot_nt(dob, vbl) - delta)).astype(BF16)
            dk_ref[rows, :] += _dot_tn(ds, qb)
            dv_ref[rows, :] += _dot_tn(p.astype(BF16), dob)
            return _dot(ds, kbl)

        dq = block(kc_ref, vc_ref, dkc_ref, dvc_ref, pl.ds(0, ctx_len))
        for i in range(seq // kb):
            dq = dq + block(kl_ref, vl_ref, dkl_ref, dvl_ref, pl.ds(i * kb, kb))
        dq_ref[...] = dq

    qs, kl, kc, vl, vc, os_ = _mla_specs(t_lat, seq, ctx_len, tq)
    do_spec = pl.BlockSpec((tq, 128), lambda b, h, j: (b * nqt + j, HEADS + h))
    return _hosted_call(
        body, (q, k, k, v, v, ymla, lse, dmix), name="mla_bwd", grid=(nex, HEADS, nqt),
        in_specs=[qs, kl, kc, vl, vc, os_, os_, do_spec],
        out_specs=[
            qs,
            pl.BlockSpec((seq, MLA_HEAD), lambda b, h, j: (b, h)),
            pl.BlockSpec((ctx_len, MLA_HEAD), lambda b, h, j: (b, h)),
            pl.BlockSpec((seq, 128), lambda b, h, j: (b, h)),
            pl.BlockSpec((ctx_len, 128), lambda b, h, j: (b, h)),
        ],
        out_shape=[
            jax.ShapeDtypeStruct((t_lat, HEADS * MLA_HEAD), F32),
            jax.ShapeDtypeStruct((t_lat, HEADS * MLA_HEAD), F32),
            jax.ShapeDtypeStruct((t_ctx, HEADS * MLA_HEAD), F32),
            jax.ShapeDtypeStruct((t_lat, HEADS * 128), F32),
            jax.ShapeDtypeStruct((t_ctx, HEADS * 128), F32),
        ],
        sem=("parallel", "parallel", "arbitrary"), rider=rider)


def _decay_terms(lg, chunk, forward):
    ii = lax.broadcasted_iota(jnp.int32, (chunk, chunk), 0)
    jj = lax.broadcasted_iota(jnp.int32, (chunk, chunk), 1)
    diff = (ii - jj) if forward else (jj - ii)
    dist = jnp.maximum(diff, 0).astype(F32)
    dmat = jnp.where(diff >= 0, jnp.exp(lg * dist), 0.0)
    pos = lax.broadcasted_iota(jnp.int32, (chunk, 1), 0).astype(F32)
    if forward:
        e_q = pos + 1.0
        e_k = (chunk - 1.0) - pos
    else:
        e_q = chunk - pos
        e_k = pos
    wq = jnp.exp(lg * e_q)
    wk = jnp.exp(lg * e_k)
    cd = jnp.exp(jnp.full((1, 1), lg * chunk, F32))
    return dmat, dist, wq, wk, e_q, e_k, cd


def _ctx_weights(lg, ctx_len, forward):
    pos = lax.broadcasted_iota(jnp.int32, (ctx_len, 1), 0).astype(F32)
    e = ((ctx_len - 1.0) - pos) if forward else pos
    return jnp.exp(lg * e), e


def _pair_specs(t_lat, seq, ctx_len):
    cb = t_lat // ctx_len
    qk = pl.BlockSpec((seq, 128), lambda b, p: (b, p))
    v = pl.BlockSpec((seq, 256), lambda b, p: (b, p))
    kc = pl.BlockSpec((ctx_len, 128), lambda b, p: (cb + b, p))
    vc = pl.BlockSpec((ctx_len, 256), lambda b, p: (cb + b, p))
    return qk, v, kc, vc


def _lane_masks():
    lane = lax.broadcasted_iota(jnp.int32, (1, 128), 1)
    return [(lane // RET_DK) == hh for hh in (0, 1)]


def _ret_fwd_pair(rq, rk, rv, pg, lg, g_ret, *, t_lat, seq, ctx_len, chunk, rider=None):
    nex = t_lat // seq
    n_chunk = seq // chunk

    def body(q_ref, k_ref, v_ref, kc_ref, vc_ref, rg_ref, lg_ref, g_ref, y_ref, o_ref):
        pair = pl.program_id(1)
        masks = _lane_masks()
        kcf = kc_ref[...].astype(F32)

        def run(forward):
            terms, s0 = [], []
            for hh in (0, 1):
                lgd = lg_ref[0 if forward else 1, 2 * pair + hh]
                terms.append(_decay_terms(lgd, chunk, forward))
                wc, _ = _ctx_weights(lgd, ctx_len, forward)
                s0.append(_dot_tn((jnp.where(masks[hh], kcf, 0.0) * wc).astype(BF16), vc_ref[:, hh * 128:(hh + 1) * 128]))

            def step(t, states):
                n = t if forward else n_chunk - 1 - t
                sl = pl.ds(pl.multiple_of(n * chunk, chunk), chunk)
                qb = q_ref[sl, :]
                kf_all = k_ref[sl, :].astype(F32)
                new = []
                for hh in (0, 1):
                    dmat, _, wq, wk, _, _, cd = terms[hh]
                    cols = slice(hh * 128, (hh + 1) * 128)
                    qm = jnp.where(masks[hh], qb, jnp.zeros((), BF16))
                    kf = jnp.where(masks[hh], kf_all, 0.0)
                    vb = v_ref[sl, cols]
                    a = _dot_nt(qm, kf.astype(BF16)) * dmat
                    o = _dot(a.astype(BF16), vb) + wq * _dot(qm, states[hh].astype(BF16))
                    if forward:
                        o_ref[sl, cols] = o
                    else:
                        o = o_ref[sl, cols] + o
                        o_ref[sl, cols] = o
                        mu = jnp.mean(o, axis=-1, keepdims=True)
                        oc = o - mu
                        var = jnp.mean(oc * oc, axis=-1, keepdims=True)
                        rg = rg_ref[sl, cols]
                        y_ref[sl, cols] = (oc * lax.rsqrt(var + EPS) * g_ref[:, cols] * (rg / (1.0 + jnp.exp(-rg)))).astype(BF16)
                    new.append(cd * states[hh] + _dot_tn((kf * wk).astype(BF16), vb))
                return tuple(new)

            lax.fori_loop(0, n_chunk, step, tuple(s0))

        run(True)
        run(False)

    qk, v, kc, vc = _pair_specs(t_lat, seq, ctx_len)
    return _hosted_call(
        body, (rq, rk, rv, rk, rv, pg, lg, g_ret), name="ret_fwd", grid=(nex, HEADS // 2),
        in_specs=[qk, qk, v, kc, vc, v, pl.BlockSpec(memory_space=pltpu.SMEM), pl.BlockSpec((1, 256), lambda b, p: (0, p))],
        out_specs=[v, v],
        out_shape=[jax.ShapeDtypeStruct((t_lat, HEADS * RET_DV), BF16), jax.ShapeDtypeStruct((t_lat, HEADS * RET_DV), F32)],
        sem=("parallel", "arbitrary"), rider=rider)


def _ret_bwd_pair(rq, rk, rv, pg, osum, dmix, lg, g_ret, *, t_lat, seq, ctx_len, chunk, rider=None):
    nex = t_lat // seq
    n_chunk = seq // chunk
    t_ctx = nex * ctx_len

    def body(q_ref, k_ref, v_ref, kc_ref, vc_ref, rg_ref, o_ref, dy_ref, lg_ref, g_ref,
             dq_ref, dk_ref, dv_ref, dkc_ref, dvc_ref, drg_ref, st_ref, do_s, s_st):
        pair = pl.program_id(1)
        masks = _lane_masks()
        kcf = kc_ref[...].astype(F32)

        def norm_step(n, dgains):
            sl = pl.ds(pl.multiple_of(n * chunk, chunk), chunk)
            out = []
            for hh in (0, 1):
                cols = slice(hh * 128, (hh + 1) * 128)
                gain = g_ref[:, cols]
                o = o_ref[sl, cols]
                mu = jnp.mean(o, axis=-1, keepdims=True)
                oc = o - mu
                rstd = lax.rsqrt(jnp.mean(oc * oc, axis=-1, keepdims=True) + EPS)
                ohat = oc * rstd
                rg = rg_ref[sl, cols]
                sg = 1.0 / (1.0 + jnp.exp(-rg))
                dy = dy_ref[sl, cols].astype(F32)
                don = dy * (rg * sg)
                drg_ref[sl, cols] = dy * (ohat * gain) * (sg * (1.0 + rg * (1.0 - sg)))
                dohat = don * gain
                do_s[sl, cols] = rstd * (dohat - jnp.mean(dohat, axis=-1, keepdims=True)
                                         - ohat * jnp.mean(dohat * ohat, axis=-1, keepdims=True))
                out.append(dgains[hh] + jnp.sum(don * ohat, axis=0, keepdims=True))
            return tuple(out)

        zero_row = jnp.zeros((1, 128), F32)
        dgains = lax.fori_loop(0, n_chunk, norm_step, (zero_row, zero_row))
        dq_ref[...] = jnp.zeros(dq_ref.shape, F32)
        dk_ref[...] = jnp.zeros(dk_ref.shape, F32)
        dv_ref[...] = jnp.zeros(dv_ref.shape, F32)

        def run(forward):
            terms, ctxw, s0 = [], [], []
            for hh in (0, 1):
                lgd = lg_ref[0 if forward else 1, 2 * pair + hh]
                terms.append(_decay_terms(lgd, chunk, forward))
                ctxw.append(_ctx_weights(lgd, ctx_len, forward))
                s0.append(_dot_tn((jnp.where(masks[hh], kcf, 0.0) * ctxw[hh][0]).astype(BF16),
                                  vc_ref[:, hh * 128:(hh + 1) * 128]))

            def state_step(t, states):
                n = t if forward else n_chunk - 1 - t
                sl = pl.ds(pl.multiple_of(n * chunk, chunk), chunk)
                kf_all = k_ref[sl, :].astype(F32)
                new = []
                for hh in (0, 1):
                    wk, cd = terms[hh][3], terms[hh][6]
                    s_st[hh, n] = states[hh]
                    kf = jnp.where(masks[hh], kf_all, 0.0)
                    new.append(cd * states[hh] + _dot_tn((kf * wk).astype(BF16), v_ref[sl, hh * 128:(hh + 1) * 128]))
                return tuple(new)

            lax.fori_loop(0, n_chunk, state_step, tuple(s0))

            def grad_step(t, carry):
                n = (n_chunk - 1 - t) if forward else t
                sl = pl.ds(pl.multiple_of(n * chunk, chunk), chunk)
                qb = q_ref[sl, :]
                kf_all = k_ref[sl, :].astype(F32)
                dq_sum = jnp.zeros((chunk, 128), F32)
                dk_sum = jnp.zeros((chunk, 128), F32)
                out = []
                for hh in (0, 1):
                    g_next, dlg = carry[hh]
                    dmat, dist, wq, wk, e_q, e_k, cd = terms[hh]
                    cols = slice(hh * 128, (hh + 1) * 128)
                    qm = jnp.where(masks[hh], qb, jnp.zeros((), BF16))
                    kf = jnp.where(masks[hh], kf_all, 0.0)
                    kb = kf.astype(BF16)
                    vb = v_ref[sl, cols]
                    do = do_s[sl, cols]
                    dob = do.astype(BF16)
                    s_n = s_st[hh, n]
                    s_nb = s_n.astype(BF16)
                    gb = g_next.astype(BF16)
                    dk_cross = wk * _dot_nt(vb, gb)
                    dv_cross = _dot((kf * wk).astype(BF16), gb)
                    a = _dot_nt(qm, kb) * dmat
                    da_raw = _dot_nt(dob, vb)
                    dab = (da_raw * dmat).astype(BF16)
                    o_cross = wq * _dot(qm, s_nb)
                    dq_sum = dq_sum + _dot(dab, kb) + wq * _dot_nt(dob, s_nb)
                    dk_sum = dk_sum + _dot_tn(dab, qm) + dk_cross
                    dv_ref[sl, cols] += _dot_tn(a.astype(BF16), dob) + dv_cross
                    dlg = (dlg + chunk * cd * jnp.sum(g_next * s_n, keepdims=True)
                           + jnp.sum(e_k * jnp.sum(kf * dk_cross, axis=-1, keepdims=True), keepdims=True)
                           + jnp.sum(dist * a * da_raw, keepdims=True)
                           + jnp.sum(e_q * jnp.sum(o_cross * do, axis=-1, keepdims=True), keepdims=True))
                    out.append((cd * g_next + _dot_tn((qm.astype(F32) * wq).astype(BF16), dob), dlg))
                dq_ref[sl, :] += dq_sum
                dk_ref[sl, :] += dk_sum
                return tuple(out)

            zero = (jnp.zeros((128, 128), F32), jnp.zeros((1, 1), F32))
            res = lax.fori_loop(0, n_chunk, grad_step, (zero, zero))
            dkc_sum = jnp.zeros((ctx_len, 128), F32)
            dvc, dlgs = [], []
            for hh in (0, 1):
                ds0, dlg = res[hh]
                wc, e_c = ctxw[hh]
                kcm = jnp.where(masks[hh], kcf, 0.0)
                ds0b = ds0.astype(BF16)
                dkc_part = wc * _dot_nt(vc_ref[:, hh * 128:(hh + 1) * 128], ds0b)
                dkc_sum = dkc_sum + dkc_part
                dvc.append(_dot((kcm * wc).astype(BF16), ds0b))
                dlgs.append(dlg + jnp.sum(e_c * jnp.sum(kcm * dkc_part, axis=-1, keepdims=True), keepdims=True))
            return dkc_sum, dvc, dlgs

        dkc_f, dvc_f, dlg_f = run(True)
        dkc_b, dvc_b, dlg_b = run(False)
        dkc_ref[...] = dkc_f + dkc_b
        for hh in (0, 1):
            cols = slice(hh * 128, (hh + 1) * 128)
            dvc_ref[:, cols] = dvc_f[hh] + dvc_b[hh]
            st_ref[0, :, cols] = jnp.concatenate([
                dgains[hh], jnp.broadcast_to(dlg_f[hh], (1, 128)), jnp.broadcast_to(dlg_b[hh], (1, 128)),
                jnp.zeros((5, 128), F32)], axis=0)

    qk, v, kc, vc = _pair_specs(t_lat, seq, ctx_len)
    return _hosted_call(
        body, (rq, rk, rv, rk, rv, pg, osum, dmix, lg, g_ret), name="ret_bwd", grid=(nex, HEADS // 2),
        in_specs=[qk, qk, v, kc, vc, v, v, v, pl.BlockSpec(memory_space=pltpu.SMEM),
                  pl.BlockSpec((1, 256), lambda b, p: (0, p))],
        out_specs=[
            qk, qk, v,
            pl.BlockSpec((ctx_len, 128), lambda b, p: (b, p)),
            pl.BlockSpec((ctx_len, 256), lambda b, p: (b, p)),
            v,
            pl.BlockSpec((1, 8, 256), lambda b, p: (b, 0, p)),
        ],
        out_shape=[
            jax.ShapeDtypeStruct((t_lat, 256), F32), jax.ShapeDtypeStruct((t_lat, 256), F32),
            jax.ShapeDtypeStruct((t_lat, 512), F32), jax.ShapeDtypeStruct((t_ctx, 256), F32),
            jax.ShapeDtypeStruct((t_ctx, 512), F32), jax.ShapeDtypeStruct((t_lat, 512), F32),
            jax.ShapeDtypeStruct((nex, 8, 512), F32),
        ],
        scratch_shapes=[pltpu.VMEM((seq, 256), F32), pltpu.VMEM((2, n_chunk, 128, 128), F32)],
        sem=("parallel", "arbitrary"), rider=rider)


def _ret_specs(t_lat, seq, ctx_len):
    cb = t_lat // ctx_len
    qk = pl.BlockSpec((seq, 128), lambda b, h: (b, h // 2))
    v = pl.BlockSpec((seq, 128), lambda b, h: (b, h))
    kc = pl.BlockSpec((ctx_len, 128), lambda b, h: (cb + b, h // 2))
    vc = pl.BlockSpec((ctx_len, 128), lambda b, h: (cb + b, h))
    return qk, v, kc, vc


def _head_mask(h):
    lane = lax.broadcasted_iota(jnp.int32, (1, 128), 1)
    return (lane // RET_DK) == (h % 2)


def _ret_fwd(rq, rk, rv, pg, lg, g_ret, *, t_lat, seq, ctx_len, chunk, rider=None):
    nex = t_lat // seq
    n_chunk = seq // chunk

    def body(q_ref, k_ref, v_ref, kc_ref, vc_ref, rg_ref, lg_ref, g_ref, y_ref, o_ref):
        h = pl.program_id(1)
        hm = _head_mask(h)
        gain = g_ref[...]
        kcm = jnp.where(hm, kc_ref[...].astype(F32), 0.0)
        vcb = vc_ref[...]

        def run(forward):
            lgd = lg_ref[0 if forward else 1, h]
            dmat, _, wq, wk, _, _, cd = _decay_terms(lgd, chunk, forward)
            wc, _ = _ctx_weights(lgd, ctx_len, forward)
            s0 = _dot_tn((kcm * wc).astype(BF16), vcb)

            def step(t, s):
                n = t if forward else n_chunk - 1 - t
                sl = pl.ds(pl.multiple_of(n * chunk, chunk), chunk)
                qm = jnp.where(hm, q_ref[sl, :], jnp.zeros((), BF16))
                kf = jnp.where(hm, k_ref[sl, :].astype(F32), 0.0)
                vb = v_ref[sl, :]
                a = _dot_nt(qm, kf.astype(BF16)) * dmat
                o = _dot(a.astype(BF16), vb) + wq * _dot(qm, s.astype(BF16))
                if forward:
                    o_ref[sl, :] = o
                else:
                    o = o_ref[sl, :] + o
                    o_ref[sl, :] = o
                    mu = jnp.mean(o, axis=-1, keepdims=True)
                    oc = o - mu
                    var = jnp.mean(oc * oc, axis=-1, keepdims=True)
                    on = oc * lax.rsqrt(var + EPS) * gain
                    rg = rg_ref[sl, :]
                    y_ref[sl, :] = (on * (rg / (1.0 + jnp.exp(-rg)))).astype(BF16)
                return cd * s + _dot_tn((kf * wk).astype(BF16), vb)

            lax.fori_loop(0, n_chunk, step, s0)

        run(True)
        run(False)

    qk, v, kc, vc = _ret_specs(t_lat, seq, ctx_len)
    return _hosted_call(
        body, (rq, rk, rv, rk, rv, pg, lg, g_ret), name="ret_fwd", grid=(nex, HEADS),
        in_specs=[qk, qk, v, kc, vc, v, pl.BlockSpec(memory_space=pltpu.SMEM), pl.BlockSpec((1, 128), lambda b, h: (0, h))],
        out_specs=[v, v],
        out_shape=[jax.ShapeDtypeStruct((t_lat, HEADS * RET_DV), BF16), jax.ShapeDtypeStruct((t_lat, HEADS * RET_DV), F32)],
        sem=("parallel", "arbitrary"), rider=rider)


def _ret_bwd(rq, rk, rv, pg, osum, dmix, lg, g_ret, *, t_lat, seq, ctx_len, chunk, rider=None):
    nex = t_lat // seq
    n_chunk = seq // chunk
    t_ctx = nex * ctx_len

    def body(q_ref, k_ref, v_ref, kc_ref, vc_ref, rg_ref, o_ref, dy_ref, lg_ref, g_ref,
             dq_ref, dk_ref, dv_ref, dkc_ref, dvc_ref, drg_ref, st_ref, do_s, s_st):
        h = pl.program_id(1)
        hm = _head_mask(h)
        gain = g_ref[...]
        kcm = jnp.where(hm, kc_ref[...].astype(F32), 0.0)
        vcb = vc_ref[...]

        def norm_step(n, dgain):
            sl = pl.ds(pl.multiple_of(n * chunk, chunk), chunk)
            o = o_ref[sl, :]
            mu = jnp.mean(o, axis=-1, keepdims=True)
            oc = o - mu
            rstd = lax.rsqrt(jnp.mean(oc * oc, axis=-1, keepdims=True) + EPS)
            ohat = oc * rstd
            rg = rg_ref[sl, :]
            sg = 1.0 / (1.0 + jnp.exp(-rg))
            dy = dy_ref[sl, :].astype(F32)
            don = dy * (rg * sg)
            drg_ref[sl, :] = dy * (ohat * gain) * (sg * (1.0 + rg * (1.0 - sg)))
            dohat = don * gain
            do_s[sl, :] = rstd * (dohat - jnp.mean(dohat, axis=-1, keepdims=True)
                                  - ohat * jnp.mean(dohat * ohat, axis=-1, keepdims=True))
            return dgain + jnp.sum(don * ohat, axis=0, keepdims=True)

        dgain = lax.fori_loop(0, n_chunk, norm_step, jnp.zeros((1, 128), F32))

        @pl.when(h % 2 == 0)
        def _():
            dq_ref[...] = jnp.zeros(dq_ref.shape, F32)
            dk_ref[...] = jnp.zeros(dk_ref.shape, F32)
            dkc_ref[...] = jnp.zeros(dkc_ref.shape, F32)

        dv_ref[...] = jnp.zeros(dv_ref.shape, F32)

        def run(forward):
            lgd = lg_ref[0 if forward else 1, h]
            dmat, dist, wq, wk, e_q, e_k, cd = _decay_terms(lgd, chunk, forward)
            wc, e_c = _ctx_weights(lgd, ctx_len, forward)
            s0 = _dot_tn((kcm * wc).astype(BF16), vcb)

            def state_step(t, s):
                n = t if forward else n_chunk - 1 - t
                sl = pl.ds(pl.multiple_of(n * chunk, chunk), chunk)
                s_st[n] = s
                kf = jnp.where(hm, k_ref[sl, :].astype(F32), 0.0)
                return cd * s + _dot_tn((kf * wk).astype(BF16), v_ref[sl, :])

            lax.fori_loop(0, n_chunk, state_step, s0)

            def grad_step(t, carry):
                g_next, dlg = carry
                n = (n_chunk - 1 - t) if forward else t
                sl = pl.ds(pl.multiple_of(n * chunk, chunk), chunk)
                qm = jnp.where(hm, q_ref[sl, :], jnp.zeros((), BF16))
                kf = jnp.where(hm, k_ref[sl, :].astype(F32), 0.0)
                kb = kf.astype(BF16)
                vb = v_ref[sl, :]
                do = do_s[sl, :]
                dob = do.astype(BF16)
                s_n = s_st[n]
                s_nb = s_n.astype(BF16)
                gb = g_next.astype(BF16)
                dk_cross = wk * _dot_nt(vb, gb)
                dv_cross = _dot((kf * wk).astype(BF16), gb)
                a = _dot_nt(qm, kb) * dmat
                da_raw = _dot_nt(dob, vb)
                dab = (da_raw * dmat).astype(BF16)
                ab = a.astype(BF16)
                o_cross = wq * _dot(qm, s_nb)
                dq_ref[sl, :] += _dot(dab, kb) + wq * _dot_nt(dob, s_nb)
                dk_ref[sl, :] += _dot_tn(dab, qm) + dk_cross
                dv_ref[sl, :] += _dot_tn(ab, dob) + dv_cross
                dlg = (dlg + chunk * cd * jnp.sum(g_next * s_n, keepdims=True)
                       + jnp.sum(e_k * jnp.sum(kf * dk_cross, axis=-1, keepdims=True), keepdims=True)
                       + jnp.sum(dist * a * da_raw, keepdims=True)
                       + jnp.sum(e_q * jnp.sum(o_cross * do, axis=-1, keepdims=True), keepdims=True))
                g_new = cd * g_next + _dot_tn((qm.astype(F32) * wq).astype(BF16), dob)
                return g_new, dlg

            ds0, dlg = lax.fori_loop(0, n_chunk, grad_step, (jnp.zeros((128, 128), F32), jnp.zeros((1, 1), F32)))
            ds0b = ds0.astype(BF16)
            dkc_part = wc * _dot_nt(vcb, ds0b)
            dkc_ref[...] += dkc_part
            dvc_part = _dot((kcm * wc).astype(BF16), ds0b)
            dlg = dlg + jnp.sum(e_c * jnp.sum(kcm * dkc_part, axis=-1, keepdims=True), keepdims=True)
            return dvc_part, dlg

        dvc_f, dlg_f = run(True)
        dvc_b, dlg_b = run(False)
        dvc_ref[...] = dvc_f + dvc_b
        st_ref[0] = jnp.concatenate([
            dgain, jnp.broadcast_to(dlg_f, (1, 128)), jnp.broadcast_to(dlg_b, (1, 128)), jnp.zeros((5, 128), F32)], axis=0)

    qk, v, kc, vc = _ret_specs(t_lat, seq, ctx_len)
    dy_spec = v
    return _hosted_call(
        body, (rq, rk, rv, rk, rv, pg, osum, dmix, lg, g_ret), name="ret_bwd", grid=(nex, HEADS),
        in_specs=[qk, qk, v, kc, vc, v, v, dy_spec, pl.BlockSpec(memory_space=pltpu.SMEM),
                  pl.BlockSpec((1, 128), lambda b, h: (0, h))],
        out_specs=[
            qk, qk, v,
            pl.BlockSpec((ctx_len, 128), lambda b, h: (b, h // 2)),
            pl.BlockSpec((ctx_len, 128), lambda b, h: (b, h)),
            v,
            pl.BlockSpec((1, 8, 128), lambda b, h: (b, 0, h)),
        ],
        out_shape=[
            jax.ShapeDtypeStruct((t_lat, 256), F32), jax.ShapeDtypeStruct((t_lat, 256), F32),
            jax.ShapeDtypeStruct((t_lat, 512), F32), jax.ShapeDtypeStruct((t_ctx, 256), F32),
            jax.ShapeDtypeStruct((t_ctx, 512), F32), jax.ShapeDtypeStruct((t_lat, 512), F32),
            jax.ShapeDtypeStruct((nex, 8, 512), F32),
        ],
        scratch_shapes=[pltpu.VMEM((seq, 128), F32), pltpu.VMEM((n_chunk, 128, 128), F32)],
        sem=("parallel", "arbitrary"), rider=rider)


def _matmul_tn(a, b, *, bm, bn, bk, chip_major, name, out_dtype=F32):
    tk, m = a.shape
    n = b.shape[1]
    slab = n // N_CHIPS
    per_block = bn // slab if chip_major else 1
    bk = max(c for c in range(LANES, min(bk, tk) + 1, LANES) if tk % c == 0)
    nk = tk // bk
    blk = (per_block, bm, slab) if chip_major else (bm, bn)

    def body(a_ref, b_ref, o_ref, acc_ref):
        k = pl.program_id(2)
        if chip_major:
            parts = [_dot_tn(a_ref[...], b_ref[:, s * slab:(s + 1) * slab]) for s in range(per_block)]
        else:
            parts = [_dot_tn(a_ref[...], b_ref[...])]

        @pl.when(k == 0)
        def _():
            for s, part in enumerate(parts):
                if chip_major:
                    acc_ref[s] = part
                else:
                    acc_ref[...] = part

        @pl.when(k > 0)
        def _():
            for s, part in enumerate(parts):
                if chip_major:
                    acc_ref[s] += part
                else:
                    acc_ref[...] += part

        @pl.when(k == nk - 1)
        def _():
            o_ref[...] = acc_ref[...].astype(out_dtype)

    if chip_major:
        out_spec = pl.BlockSpec(blk, lambda i, j, k: (j, i, 0))
        out_shape = jax.ShapeDtypeStruct((N_CHIPS, m, slab), out_dtype)
    else:
        out_spec = pl.BlockSpec(blk, lambda i, j, k: (i, j))
        out_shape = jax.ShapeDtypeStruct((m, n), out_dtype)
    return pl.pallas_call(
        body, name=name, grid=(m // bm, n // bn, nk),
        in_specs=[pl.BlockSpec((bk, bm), lambda i, j, k: (k, i)), pl.BlockSpec((bk, bn), lambda i, j, k: (k, j))],
        out_specs=out_spec, out_shape=out_shape, scratch_shapes=[pltpu.VMEM(blk, F32)],
        compiler_params=_params(("parallel", "parallel", "arbitrary"), VMEM_LIMIT),
    )(a, b)


_LATE = ("w_out", "w_ff1", "w_ff2")
_EARLY = ("w_in", "w_uq", "w_ukv")


def _local_step(x, ctx, tgt, modv, lg, g_attn, g_ffn, g_fin, g_ret, g_q, g_kv, w_in, w_uq, w_ukv, late, place=None,
                *, tm=256, tq=256, chunk=256):
    nex, seq, d = x.shape
    ctx_len = ctx.shape[1]
    t_lat = nex * seq
    x2 = x.reshape(t_lat, d)
    ctx2 = ctx.reshape(nex * ctx_len, d)
    tgt2 = tgt.reshape(t_lat, d)
    cos_t, sin_t = _rope_tables(seq, tm)
    dims = dict(t_lat=t_lat, seq=seq, ctx_len=ctx_len)
    alone = place is None

    (hb, pg, rq, rk, rv, nq, nkv, q, k, v), crossed2 = _pre_fwd(
        x2, ctx2, modv, g_attn, w_in, g_q, g_kv, w_uq, w_ukv, cos_t, sin_t, seq=seq, tm=tm,
        rider=None if alone else _gather_ici_rider([late[2]]))
    (yret, osum), crossed1 = _ret_fwd_pair(rq, rk, rv, pg, lg, g_ret, chunk=chunk, **dims,
                                           rider=None if alone else _gather_ici_rider([late[1]]))
    (ymla, lse), gathered = _mla_fwd(
        q, k, v, tq=tq, **dims,
        rider=None if alone else _merge_riders(_gather_rider([late[0]]), _gather_d2d_rider(crossed1 + crossed2)))
    w_out, w_ff1, w_ff2 = late if alone else gathered
    mix, act, du, h2, df, dmo, dmix, dxm, st_post = _post(yret, ymla, x2, tgt2, modv, g_ffn, g_fin, w_out.reshape(d, d),
                                                         w_ff1, w_ff2.reshape(D_FF, d), seq=seq, tm=tm)
    bk = 512
    g_late = [
        _matmul_tn(mix, dmo, bm=1024, bn=1024, bk=1024, chip_major=False, name="gw_out",
                   out_dtype=BF16).reshape(N_CHIPS, d // N_CHIPS, d),
        _matmul_tn(h2, du, bm=1024, bn=1024, bk=1024, chip_major=True, name="gw_ff1", out_dtype=BF16),
        _matmul_tn(act, df, bm=1024, bn=1024, bk=1024, chip_major=False, name="gw_ff2",
                   out_dtype=BF16).reshape(N_CHIPS, D_FF // N_CHIPS, d),
    ]
    (dq_m, dkl, dkc, dvl, dvc), got = _mla_bwd(q, k, v, ymla, lse, dmix, tq=tq, **dims,
                                               rider=None if alone else _exchange_rider(g_late))
    if not alone:
        core, slot = place
        part = [_add_half(g, r, core, "add_half_" + n) for g, r, n in zip(g_late, got, _LATE)]
    (drq, drk, drv, dkc_r, dvc_r, drg, st_ret), landed = _ret_bwd_pair(
        rq, rk, rv, pg, osum, dmix, lg, g_ret, chunk=chunk, **dims, rider=None if alone else _scatter_rider(part))
    if not alone:
        mine = [_sum_chips(p, l, slot, "sum_chips_" + n) for p, l, n in zip(part, landed, _LATE)]
    (dpb, dqf, dkvf, gx, st_pre), _ = _pre_bwd(
        x2, ctx2, modv, g_attn, pg, drq, drk, dkc_r, drv, dvc_r, drg, dq_m, dkl, dkc, dvl, dvc, dxm, w_in, g_q, g_kv,
        w_uq, w_ukv, cos_t, sin_t, seq=seq, tm=tm)
    g_early = [
        _matmul_tn(dpb, hb, bm=IN_PAD // 2, bn=d, bk=512, chip_major=False, name="gw_in"),
        _matmul_tn(dqf, nq, bm=HEADS * MLA_HEAD, bn=Q_LORA, bk=1536, chip_major=False, name="gw_uq"),
        _matmul_tn(nkv, dkvf, bm=KV_LORA, bn=HEADS * 256, bk=1536, chip_major=True, name="gw_ukv"),
    ]
    late_out = g_late if alone else mine
    return gx.reshape(nex, seq, d), g_early, late_out, st_post, st_ret, st_pre


_ANY = pl.BlockSpec(memory_space=pl.ANY)
_VMEM = pl.BlockSpec(memory_space=pltpu.VMEM)
_OFFSETS = tuple((dx, dy, dc) for dx in (0, 1) for dy in (0, 1) for dc in (0, 1))[1:]
_CHIP_OFFSETS = ((1, 0), (0, 1), (1, 1))


def _place():
    return lax.axis_index("x"), lax.axis_index("y"), lax.axis_index("c")


def _flip(v, d):
    return 1 - v if d else v


def _gather8_rider(a, in_vmem=True):
    def copies(a_ref, o_ref, send, recv):
        x, y, z = _place()
        me = 4 * x + 2 * y + z
        out = []
        for k, (dx, dy, dc) in enumerate(_OFFSETS):
            peer = (_flip(x, dx), _flip(y, dy), _flip(z, dc))
            landing = o_ref.at[4 * peer[0] + 2 * peer[1] + peer[2]]
            out.append((
                pltpu.make_async_remote_copy(src_ref=a_ref, dst_ref=o_ref.at[me], send_sem=send.at[k],
                                             recv_sem=recv.at[k], device_id=peer, device_id_type=MESH),
                pltpu.make_async_remote_copy(src_ref=a_ref, dst_ref=landing, send_sem=send.at[k],
                                             recv_sem=recv.at[k], device_id=peer, device_id_type=MESH)))
        return me, out

    def start(ins, outs, sems):
        me, cps = copies(ins[0], outs[0], sems[0], sems[1])
        pltpu.make_async_copy(ins[0], outs[0].at[me], sems[2]).start()
        for out_cp, _ in cps:
            out_cp.start()

    def finish(ins, outs, sems):
        me, cps = copies(ins[0], outs[0], sems[0], sems[1])
        for out_cp, in_cp in cps:
            in_cp.wait_recv()
            out_cp.wait_send()
        pltpu.make_async_copy(ins[0], outs[0].at[me], sems[2]).wait()

    spec = [_VMEM] if in_vmem else [_ANY]
    return _Rider([a], [jax.ShapeDtypeStruct((N_DEV,) + a.shape, a.dtype)],
                  [pltpu.SemaphoreType.DMA((7,)), pltpu.SemaphoreType.DMA((7,)), pltpu.SemaphoreType.DMA],
                  start, finish, in_specs=spec, out_specs=spec)


def _merge_riders(*riders):
    ins, outs, sems, in_specs, out_specs, aliases, cuts = [], [], [], [], [], {}, []
    for r in riders:
        cuts.append((len(ins), len(outs), len(sems)))
        aliases.update({len(ins) + i: len(outs) + j for i, j in r.aliases.items()})
        ins += r.ins
        outs += r.out_shapes
        sems += r.sems
        in_specs += r.in_specs
        out_specs += r.out_specs

    def part(r, cut, r_ins, r_outs, r_sems):
        return (r_ins[cut[0]:cut[0] + len(r.ins)], r_outs[cut[1]:cut[1] + len(r.out_shapes)],
                r_sems[cut[2]:cut[2] + len(r.sems)])

    def start(r_ins, r_outs, r_sems):
        for r, cut in zip(riders, cuts):
            r.start(*part(r, cut, r_ins, r_outs, r_sems))

    def finish(r_ins, r_outs, r_sems):
        for r, cut in zip(riders, cuts):
            r.finish(*part(r, cut, r_ins, r_outs, r_sems))

    return _Rider(ins, outs, sems, start, finish, aliases=aliases, in_specs=in_specs, out_specs=out_specs)


def _allgather8(a, name):
    return _run_rider(_gather8_rider(a), name)[0]


BF16_TILE_ROWS = 16


def _half(o, slot, which):
    r2 = o.shape[1] // 2
    if r2 % BF16_TILE_ROWS == 0:
        return o.at[slot, pl.ds(which * r2, r2)]
    c2 = o.shape[2] // 2
    assert c2 % LANES == 0
    return o.at[slot, :, pl.ds(which * c2, c2)]


def _gather_send(o_refs, send, recv):
    x, y, z = _place()
    chip = 2 * x + y
    for a, o in enumerate(o_refs):
        r2 = o.shape[1] // 2
        mine = _half(o, chip, z)
        for k, (dx, dy) in enumerate(_CHIP_OFFSETS):
            pltpu.make_async_remote_copy(
                src_ref=mine, dst_ref=mine, send_sem=send.at[a, k], recv_sem=recv.at[a, k],
                device_id=(_flip(x, dx), _flip(y, dy), z), device_id_type=MESH).start()


def _gather_landed(o_refs, send, recv, then=None):
    x, y, z = _place()
    chip = 2 * x + y
    for a, o in enumerate(o_refs):
        for k, (dx, dy) in enumerate(_CHIP_OFFSETS):
            landed = _half(o, 2 * _flip(x, dx) + _flip(y, dy), z)
            pltpu.make_async_remote_copy(
                src_ref=landed, dst_ref=landed, send_sem=send.at[a, k], recv_sem=recv.at[a, k],
                device_id=(_flip(x, dx), _flip(y, dy), z), device_id_type=MESH).wait_recv()
            if then is not None:
                then(a, k, landed)
    for a, o in enumerate(o_refs):
        mine = _half(o, chip, z)
        for k, (dx, dy) in enumerate(_CHIP_OFFSETS):
            pltpu.make_async_remote_copy(
                src_ref=mine, dst_ref=mine, send_sem=send.at[a, k], recv_sem=recv.at[a, k],
                device_id=(_flip(x, dx), _flip(y, dy), z), device_id_type=MESH).wait_send()


def _pass_on(o_refs, fsend, frecv, a, k, landed):
    x, y, z = _place()
    pltpu.make_async_remote_copy(
        src_ref=landed, dst_ref=landed, send_sem=fsend.at[a, k], recv_sem=frecv.at[a, k],
        device_id=(x, y, 1 - z), device_id_type=MESH).start()


def _passed_on(o_refs, fsend, frecv):
    x, y, z = _place()
    for a, o in enumerate(o_refs):
        for k, (dx, dy) in enumerate(_CHIP_OFFSETS):
            other = 2 * _flip(x, dx) + _flip(y, dy)
            got = _half(o, other, 1 - z)
            gave = _half(o, other, z)
            pltpu.make_async_remote_copy(
                src_ref=got, dst_ref=got, send_sem=fsend.at[a, k], recv_sem=frecv.at[a, k],
                device_id=(x, y, 1 - z), device_id_type=MESH).wait_recv()
            pltpu.make_async_remote_copy(
                src_ref=gave, dst_ref=gave, send_sem=fsend.at[a, k], recv_sem=frecv.at[a, k],
                device_id=(x, y, 1 - z), device_id_type=MESH).wait_send()


def _gather_finish(o_refs, send, recv, fsend, frecv):
    _gather_landed(o_refs, send, recv, functools.partial(_pass_on, o_refs, fsend, frecv))
    _passed_on(o_refs, fsend, frecv)


class _Rider:
    def __init__(self, ins, out_shapes, sems, start, finish, aliases=None, in_specs=None, out_specs=None):
        self.ins, self.out_shapes, self.sems = list(ins), list(out_shapes), list(sems)
        self.start, self.finish, self.aliases = start, finish, dict(aliases or {})
        self.in_specs = list(in_specs) if in_specs else [_ANY] * len(self.ins)
        self.out_specs = list(out_specs) if out_specs else [_ANY] * len(self.out_shapes)


def _run_rider(rider, name):
    r_in, r_out = len(rider.ins), len(rider.out_shapes)

    def body(*refs):
        ins, outs, sems = refs[:r_in], refs[r_in:r_in + r_out], refs[r_in + r_out:]
        rider.start(ins, outs, sems)
        rider.finish(ins, outs, sems)

    return pl.pallas_call(
        body, name=name, in_specs=rider.in_specs, out_specs=rider.out_specs, out_shape=rider.out_shapes,
        input_output_aliases=rider.aliases, scratch_shapes=rider.sems,
    )(*rider.ins)


def _hosted_call(body, args, *, name, grid, in_specs, out_specs, out_shape, scratch_shapes=(), sem, rider=None,
                 prefetch=()):
    scratch_shapes = list(scratch_shapes)
    n_pf, n_in, n_out, n_sc = len(prefetch), len(in_specs), len(out_specs), len(scratch_shapes)
    r_in, r_out = (len(rider.ins), len(rider.out_shapes)) if rider else (0, 0)
    last = tuple(g - 1 for g in grid)

    def hosted(*refs):
        p = 0
        parts = []
        for cnt in (n_pf, n_in, r_in, n_out, r_out, n_sc):
            parts.append(refs[p:p + cnt])
            p += cnt
        pf, ins, r_ins, outs, r_outs, scratch = parts
        sems = refs[p:]
        ids = [pl.program_id(a) for a in range(len(grid))]
        is_first = functools.reduce(jnp.logical_and, [i == 0 for i in ids])
        is_last = functools.reduce(jnp.logical_and, [i == e for i, e in zip(ids, last)])

        @pl.when(is_first)
        def _():
            rider.start(r_ins, r_outs, sems)

        body(*pf, *ins, *outs, *scratch)

        @pl.when(is_last)
        def _():
            rider.finish(r_ins, r_outs, sems)

    if rider is None:
        kern, all_in, all_out, shapes, scratch, aliases, extra = body, list(in_specs), list(out_specs), list(out_shape), \
            scratch_shapes, {}, []
    else:
        kern, all_in, all_out = hosted, list(in_specs) + rider.in_specs, list(out_specs) + rider.out_specs
        shapes, scratch, extra = list(out_shape) + rider.out_shapes, scratch_shapes + rider.sems, rider.ins
        aliases = {n_pf + n_in + i: n_out + j for i, j in rider.aliases.items()}
        sem = ("arbitrary",) * len(grid)
    if prefetch:
        spec = dict(grid_spec=pltpu.PrefetchScalarGridSpec(
            num_scalar_prefetch=n_pf, grid=grid, in_specs=all_in, out_specs=all_out, scratch_shapes=scratch))
    else:
        spec = dict(grid=grid, in_specs=all_in, out_specs=all_out, scratch_shapes=scratch)
    res = pl.pallas_call(kern, name=name, out_shape=shapes, input_output_aliases=aliases,
                         compiler_params=_params(sem, VMEM_LIMIT), **spec)(*prefetch, *args, *extra)
    return list(res[:n_out]), list(res[n_out:])


def _gather_rider(ws):
    n = len(ws)
    return _Rider(
        ws, [jax.ShapeDtypeStruct(w.shape, w.dtype) for w in ws], [pltpu.SemaphoreType.DMA((n, 3))] * 4,
        lambda ins, outs, sems: _gather_send(outs, sems[0], sems[1]),
        lambda ins, outs, sems: _gather_finish(outs, *sems),
        aliases={a: a for a in range(n)})


def _gather_ici_rider(ws):
    n = len(ws)
    return _Rider(
        ws, [jax.ShapeDtypeStruct(w.shape, w.dtype) for w in ws], [pltpu.SemaphoreType.DMA((n, 3))] * 2,
        lambda ins, outs, sems: _gather_send(outs, sems[0], sems[1]),
        lambda ins, outs, sems: _gather_landed(outs, sems[0], sems[1]),
        aliases={a: a for a in range(n)})


def _gather_d2d_rider(ws):
    n = len(ws)

    def start(ins, outs, sems):
        x, y, z = _place()
        for a, o in enumerate(outs):
            for k, (dx, dy) in enumerate(_CHIP_OFFSETS):
                _pass_on(outs, sems[0], sems[1], a, k, _half(o, 2 * _flip(x, dx) + _flip(y, dy), z))

    return _Rider(
        ws, [jax.ShapeDtypeStruct(w.shape, w.dtype) for w in ws], [pltpu.SemaphoreType.DMA((n, 3))] * 2,
        start, lambda ins, outs, sems: _passed_on(outs, sems[0], sems[1]), aliases={a: a for a in range(n)})


def _copies_rider(ins, out_shapes, sem_shape, make):
    def start(r_ins, r_outs, sems):
        for cp in make(r_ins, r_outs, sems[0], sems[1]):
            cp.start()

    def finish(r_ins, r_outs, sems):
        for cp in make(r_ins, r_outs, sems[0], sems[1]):
            cp.wait()

    return _Rider(ins, out_shapes, [pltpu.SemaphoreType.DMA(sem_shape)] * 2, start, finish)


def _exchange_rider(gs):
    def make(g_refs, r_refs, send, recv):
        x, y, z = _place()
        return [pltpu.make_async_remote_copy(
            src_ref=g.at[:, pl.ds((1 - z) * (g.shape[1] // 2), g.shape[1] // 2)], dst_ref=r, send_sem=send.at[a],
            recv_sem=recv.at[a], device_id=(x, y, 1 - z), device_id_type=MESH)
            for a, (g, r) in enumerate(zip(g_refs, r_refs))]

    shapes = [jax.ShapeDtypeStruct((g.shape[0], g.shape[1] // 2, g.shape[2]), g.dtype) for g in gs]
    return _copies_rider(gs, shapes, (len(gs),), make)


def _add_half(g, recv, core, name):
    s, r, c = g.shape
    r2 = r // 2
    rb = r2
    for cand in (256, 128, 64):
        if r2 % cand == 0:
            rb = cand
            break
    g4 = g.reshape(s, 2, r2, c)

    def body(core_ref, g_ref, r_ref, o_ref):
        o_ref[...] = (g_ref[...].astype(F32) + r_ref[...].astype(F32)).astype(BF16)

    return pl.pallas_call(
        body, name=name,
        grid_spec=pltpu.PrefetchScalarGridSpec(
            num_scalar_prefetch=1, grid=(s, r2 // rb),
            in_specs=[pl.BlockSpec((None, None, rb, c), lambda i, j, cr: (i, cr[0], j, 0)),
                      pl.BlockSpec((None, rb, c), lambda i, j, cr: (i, j, 0))],
            out_specs=pl.BlockSpec((None, rb, c), lambda i, j, cr: (i, j, 0))),
        out_shape=jax.ShapeDtypeStruct((s, r2, c), BF16),
        compiler_params=_params(("parallel", "parallel")),
    )(core, g4, recv)


def _scatter_rider(ps):
    def make(p_refs, o_refs, send, recv):
        x, y, z = _place()
        copies = []
        for a, (p, o) in enumerate(zip(p_refs, o_refs)):
            for k, (dx, dy) in enumerate(_CHIP_OFFSETS):
                other = 2 * _flip(x, dx) + _flip(y, dy)
                copies.append(pltpu.make_async_remote_copy(
                    src_ref=p.at[other], dst_ref=o.at[k], send_sem=send.at[a, k], recv_sem=recv.at[a, k],
                    device_id=(_flip(x, dx), _flip(y, dy), z), device_id_type=MESH))
        return copies

    shapes = [jax.ShapeDtypeStruct((3,) + p.shape[1:], p.dtype) for p in ps]
    return _copies_rider(ps, shapes, (len(ps), 3), make)


def _sum_chips(p, landed, chip, name):
    _, r2, c = p.shape
    rb = r2
    for cand in (256, 128, 64):
        if r2 % cand == 0:
            rb = cand
            break

    def body(s_ref, p_ref, l_ref, o_ref):
        acc = p_ref[...].astype(F32)
        for k in range(3):
            acc = acc + l_ref[k].astype(F32)
        o_ref[...] = acc

    return pl.pallas_call(
        body, name=name,
        grid_spec=pltpu.PrefetchScalarGridSpec(
            num_scalar_prefetch=1, grid=(r2 // rb,),
            in_specs=[pl.BlockSpec((None, rb, c), lambda i, s: (s[0], i, 0)),
                      pl.BlockSpec((3, rb, c), lambda i, s: (0, i, 0))],
            out_specs=pl.BlockSpec((rb, c), lambda i, s: (i, 0))),
        out_shape=jax.ShapeDtypeStruct((r2, c), F32),
        compiler_params=_params(("parallel",)),
    )(chip, p, landed)


def _swap_rider(hs):
    def make(h_refs, o_refs, send, recv):
        x, y, z = _place()
        return [pltpu.make_async_remote_copy(
            src_ref=h, dst_ref=o, send_sem=send.at[a], recv_sem=recv.at[a], device_id=(x, y, 1 - z),
            device_id_type=MESH) for a, (h, o) in enumerate(zip(h_refs, o_refs))]

    return _copies_rider(hs, [jax.ShapeDtypeStruct(h.shape, h.dtype) for h in hs], (len(hs),), make)


def _reduce_scatter_vmem(gs, rows, rider, name):
    n = len(gs)
    r_in, r_out = len(rider.ins), len(rider.out_shapes)
    halves = [(r // 2, g.shape[-1]) for g, (r, _) in zip(gs, rows)]

    def body(*refs):
        p = 0
        parts = []
        for cnt in (n, r_in, n, n, r_out, n, n, n, 6):
            parts.append(refs[p:p + cnt])
            p += cnt
        g_refs, r_ins, mine, theirs, r_outs, recv, part, land, sems = parts
        r_sems = refs[p:]
        xs, xr, ss, sr, ws, wr = sems
        x, y, z = _place()
        chip = 2 * x + y
        sib = (x, y, 1 - z)
        rider.start(r_ins, r_outs, r_sems)

        def half_of(a, s, which):
            r2 = halves[a][0]
            if len(g_refs[a].shape) == 3:
                return g_refs[a].at[s, pl.ds(pl.multiple_of(which * r2, 8), r2)]
            return g_refs[a].at[pl.ds(pl.multiple_of(s * rows[a][1] + which * r2, 8), r2)]

        exchange = [pltpu.make_async_remote_copy(
            src_ref=half_of(a, s, 1 - z), dst_ref=recv[a].at[s], send_sem=xs.at[a, s], recv_sem=xr.at[a, s],
            device_id=sib, device_id_type=MESH) for a in range(n) for s in range(N_CHIPS)]
        for cp in exchange:
            cp.start()
        for cp in exchange:
            cp.wait()
        for a in range(n):
            for s in range(N_CHIPS):
                part[a][s] = (half_of(a, s, z)[...] + recv[a][s]).astype(BF16)
        scatter = []
        for a in range(n):
            for k, (dx, dy) in enumerate(_CHIP_OFFSETS):
                other = 2 * _flip(x, dx) + _flip(y, dy)
                scatter.append(pltpu.make_async_remote_copy(
                    src_ref=part[a].at[other], dst_ref=land[a].at[k], send_sem=ss.at[a, k], recv_sem=sr.at[a, k],
                    device_id=(_flip(x, dx), _flip(y, dy), z), device_id_type=MESH))
        for cp in scatter:
            cp.start()
        for cp in scatter:
            cp.wait()
        for a in range(n):
            acc = part[a][chip].astype(F32)
            for k in range(3):
                acc = acc + land[a][k].astype(F32)
            mine[a][...] = acc
        swap = [pltpu.make_async_remote_copy(
            src_ref=mine[a], dst_ref=theirs[a], send_sem=ws.at[a], recv_sem=wr.at[a], device_id=sib,
            device_id_type=MESH) for a in range(n)]
        for cp in swap:
            cp.start()
        for cp in swap:
            cp.wait()
        rider.finish(r_ins, r_outs, r_sems)

    half_shapes = [jax.ShapeDtypeStruct(h, F32) for h in halves]
    res = pl.pallas_call(
        body, name=name, in_specs=[_VMEM] * n + rider.in_specs, out_specs=[_VMEM] * (2 * n) + rider.out_specs,
        out_shape=half_shapes + half_shapes + rider.out_shapes,
        scratch_shapes=[pltpu.VMEM((N_CHIPS,) + h, F32) for h in halves] + [pltpu.VMEM((N_CHIPS,) + h, BF16) for h in halves]
        + [pltpu.VMEM((3,) + h, BF16) for h in halves]
        + [pltpu.SemaphoreType.DMA((n, N_CHIPS))] * 2 + [pltpu.SemaphoreType.DMA((n, 3))] * 2
        + [pltpu.SemaphoreType.DMA((n,))] * 2 + rider.sems,
        input_output_aliases={n + i: 2 * n + j for i, j in rider.aliases.items()},
        compiler_params=_params(None, VMEM_LIMIT),
    )(*gs, *rider.ins)
    return list(res[:n]), list(res[n:2 * n]), list(res[2 * n:])


SMALL_ROWS = 32
PACK_ROWS = 16


def _pack_small(st_post, st_ret, st_pre):
    d = st_post.shape[2]

    def body(po_ref, re_ref, pr_ref, o_ref):
        o_ref[...] = jnp.zeros(o_ref.shape, F32)
        o_ref[0:1, :] = pr_ref[0, 2:3, :] + pr_ref[1, 2:3, :] + pr_ref[2, 2:3, :]
        o_ref[1:2, :] = po_ref[0, 4:5, :] + po_ref[1, 4:5, :]
        o_ref[2:3, :] = po_ref[0, 5:6, :] + po_ref[1, 5:6, :]
        o_ref[3:4, 0:512] = re_ref[0, 0:1, :] + re_ref[1, 0:1, :]
        o_ref[4:5, :] = pr_ref[0, 3:4, :] + pr_ref[1, 3:4, :] + pr_ref[2, 3:4, :]
        o_ref[5:6, :] = pr_ref[0, 4:5, :] + pr_ref[1, 4:5, :] + pr_ref[2, 4:5, :]
        lane = lax.broadcasted_iota(jnp.int32, (1, LANES), 1)
        for row, src in ((6, 1), (10, 2)):
            acc = jnp.zeros((1, LANES), F32)
            for hd in range(HEADS):
                grp = re_ref[0, src:src + 1, hd * LANES:(hd + 1) * LANES] + re_ref[1, src:src + 1, hd * LANES:(hd + 1) * LANES]
                acc = acc + jnp.where(lane == hd, grp, 0.0)
            o_ref[row:row + 1, 0:LANES] = acc
        o_ref[7:8, :] = po_ref[0, 6:7, :] + po_ref[1, 6:7, :]
        o_ref[8:9, :] = pr_ref[2, 0:1, :]
        o_ref[9:10, :] = pr_ref[2, 1:2, :]
        for e in range(2):
            b = 12 + 6 * e
            o_ref[b:b + 1, :] = pr_ref[e, 0:1, :]
            o_ref[b + 1:b + 2, :] = pr_ref[e, 1:2, :]
            o_ref[b + 2:b + 3, :] = po_ref[e, 3:4, :]
            o_ref[b + 3:b + 4, :] = po_ref[e, 0:1, :]
            o_ref[b + 4:b + 5, :] = po_ref[e, 1:2, :]
            o_ref[b + 5:b + 6, :] = po_ref[e, 2:3, :]

    return pl.pallas_call(body, name="pack_small", out_shape=jax.ShapeDtypeStruct((SMALL_ROWS, d), F32))(st_post, st_ret, st_pre)


def _small_reduce(gathered):
    d = gathered.shape[2]

    def body(g_ref, o_ref):
        tot = g_ref[0, 0:PACK_ROWS, :]
        for dev in range(1, N_DEV):
            tot = tot + g_ref[dev, 0:PACK_ROWS, :]
        o_ref[0:PACK_ROWS, :] = tot
        for j in range(6):
            acc = g_ref[0, 12 + j:13 + j, :] + g_ref[0, 18 + j:19 + j, :]
            for dev in range(1, N_DEV):
                acc = acc + g_ref[dev, 12 + j:13 + j, :] + g_ref[dev, 18 + j:19 + j, :]
            if j < 2:
                acc = acc + o_ref[8 + j:9 + j, :]
            o_ref[PACK_ROWS + j:PACK_ROWS + j + 1, :] = acc
        o_ref[PACK_ROWS + 6:PACK_ROWS + 8, :] = jnp.zeros((2, d), F32)

    return pl.pallas_call(body, name="small_reduce", out_shape=jax.ShapeDtypeStruct((PACK_ROWS + 8, d), F32))(gathered)


_SMALL = (("g_attn", 0, 1024), ("g_ffn", 1, 1024), ("g_final", 2, 1024), ("g_ret", 3, 512), ("g_q_lora", 4, 384),
          ("g_kv_lora", 5, 256), ("ret_decay_fwd", 6, HEADS), ("ret_decay_bwd", 10, HEADS))
_SMALL_NAMES = tuple(s[0] for s in _SMALL) + ("c_ctx", "b_ada")


def _small_final(tot, dcc, sg8, ws, ms, vs):
    d = tot.shape[1]
    n = len(_SMALL_NAMES)

    def body(*refs):
        t_ref, dcc_ref, sg_ref = refs[0:3]
        w_refs, m_refs, v_refs = refs[3:3 + n], refs[3 + n:3 + 2 * n], refs[3 + 2 * n:3 + 3 * n]
        outs = refs[3 + 3 * n:]
        g_refs, d_refs, mo_refs, vo_refs = outs[0:n], outs[n:2 * n], outs[2 * n:3 * n], outs[3 * n:4 * n]
        l_ref = outs[4 * n]

        def update(i, g, sl=None):
            pick = (lambda r: r[...]) if sl is None else (lambda r: r[:, sl])
            dl, mn, vn = _adam_math(pick(w_refs[i]), g, pick(m_refs[i]), pick(v_refs[i]))
            if sl is None:
                g_refs[i][...], d_refs[i][...], mo_refs[i][...], vo_refs[i][...] = g, dl, mn, vn
            else:
                g_refs[i][:, sl], d_refs[i][:, sl], mo_refs[i][:, sl], vo_refs[i][:, sl] = g, dl, mn, vn

        for i, (name, row, width) in enumerate(_SMALL):
            g = t_ref[row:row + 1, 0:width]
            if name == "ret_decay_fwd":
                g = g * sg_ref[0:1, 0:width]
            elif name == "ret_decay_bwd":
                g = g * sg_ref[1:2, 0:width]
            update(i, g)
        i_cc, i_b = n - 2, n - 1
        cc = w_refs[i_cc][...]
        s = 1.0 / (1.0 + jnp.exp(-cc))
        dsilu = dcc_ref[0, 0:1, :] + dcc_ref[2, 0:1, :] + dcc_ref[4, 0:1, :] + dcc_ref[6, 0:1, :]
        update(i_cc, dsilu * (s * (1.0 + cc * (1.0 - s))))
        for j in range(6):
            update(i_b, t_ref[PACK_ROWS + j:PACK_ROWS + j + 1, :], pl.ds(j * d, d))
        l_ref[...] = jnp.broadcast_to((0.5 / d) * jnp.sum(t_ref[7:8, :], keepdims=True), l_ref.shape)

    shapes = [jax.ShapeDtypeStruct(a.shape, F32) for a in ws]
    outs = pl.pallas_call(
        body, name="small_final", out_shape=shapes * 4 + [jax.ShapeDtypeStruct((8, LANES), F32)],
    )(tot, dcc, sg8, *ws, *ms, *vs)
    return outs[0:n], outs[n:2 * n], outs[2 * n:3 * n], outs[3 * n:4 * n], outs[4 * n]


_WEIGHTS = ("c_ctx", "w_ada", "b_ada", "g_attn", "g_ffn", "w_in", "ret_decay_fwd", "ret_decay_bwd", "g_ret", "g_q_lora",
            "w_uq", "g_kv_lora", "w_ukv", "w_out", "w_ff1", "w_ff2", "g_final")
_BIG = ("w_in", "w_uq", "w_ukv", "w_out", "w_ff1", "w_ff2")
_TRANSPOSED = ("w_in", "w_uq")


def kernel(x, c, ctx, c_ctx, w_ada, b_ada, g_attn, g_ffn, w_in, ret_decay_fwd, ret_decay_bwd, g_ret, g_q_lora, w_uq, g_kv_lora, w_ukv, w_out, w_ff1, w_ff2, g_final, loss_target, m_c_ctx, m_w_ada, m_b_ada, m_g_attn, m_g_ffn, m_w_in, m_ret_decay_fwd, m_ret_decay_bwd, m_g_ret, m_g_q_lora, m_w_uq, m_g_kv_lora, m_w_ukv, m_w_out, m_w_ff1, m_w_ff2, m_g_final, v_c_ctx, v_w_ada, v_b_ada, v_g_attn, v_g_ffn, v_w_in, v_ret_decay_fwd, v_ret_decay_bwd, v_g_ret, v_g_q_lora, v_w_uq, v_g_kv_lora, v_w_ukv, v_w_out, v_w_ff1, v_w_ff2, v_g_final):
    w = dict(c_ctx=c_ctx, w_ada=w_ada, b_ada=b_ada, g_attn=g_attn, g_ffn=g_ffn, w_in=w_in, ret_decay_fwd=ret_decay_fwd,
             ret_decay_bwd=ret_decay_bwd, g_ret=g_ret, g_q_lora=g_q_lora, w_uq=w_uq, g_kv_lora=g_kv_lora, w_ukv=w_ukv,
             w_out=w_out, w_ff1=w_ff1, w_ff2=w_ff2, g_final=g_final)
    m = dict(c_ctx=m_c_ctx, w_ada=m_w_ada, b_ada=m_b_ada, g_attn=m_g_attn, g_ffn=m_g_ffn, w_in=m_w_in,
             ret_decay_fwd=m_ret_decay_fwd, ret_decay_bwd=m_ret_decay_bwd, g_ret=m_g_ret, g_q_lora=m_g_q_lora, w_uq=m_w_uq,
             g_kv_lora=m_g_kv_lora, w_ukv=m_w_ukv, w_out=m_w_out, w_ff1=m_w_ff1, w_ff2=m_w_ff2, g_final=m_g_final)
    v = dict(c_ctx=v_c_ctx, w_ada=v_w_ada, b_ada=v_b_ada, g_attn=v_g_attn, g_ffn=v_g_ffn, w_in=v_w_in,
             ret_decay_fwd=v_ret_decay_fwd, ret_decay_bwd=v_ret_decay_bwd, g_ret=v_g_ret, g_q_lora=v_g_q_lora, w_uq=v_w_uq,
             g_kv_lora=v_g_kv_lora, w_ukv=v_w_ukv, w_out=v_w_out, w_ff1=v_w_ff1, w_ff2=v_w_ff2, g_final=v_g_final)
    xi, yi, ci = lax.axis_index("x"), lax.axis_index("y"), lax.axis_index("c")
    chip = 2 * xi + yi
    dev = 2 * chip + ci
    nex, seq, d = x.shape
    n_ada = w_ada.shape[2]

    dec = jnp.zeros((8, LANES), F32).at[0, :HEADS].set(ret_decay_fwd[0]).at[1, :HEADS].set(ret_decay_bwd[0])
    lg8, sg8 = _decay_prep(dec)
    lg = lg8[:2, :HEADS]

    def shard_of(t, k):
        return t[k][0].T if k in _TRANSPOSED else t[k][0]

    shard = {k: shard_of(w, k) for k in _BIG}
    head_rows = MLA_NOPE + MLA_ROPE
    shard["w_uq"] = jnp.pad(shard["w_uq"], ((0, MLA_HEAD - head_rows), (0, 0)))
    slot = chip.reshape(1).astype(jnp.int32)
    core = ci.reshape(1).astype(jnp.int32)
    slots = {k: _cast_into_slot(shard[k], slot, "cast_" + k)[0] for k in _BIG if k != "w_ff1"}
    slots["w_ff1"], (w_in_x, w_uq_x, w_ukv_x, c8) = _cast_into_slot(
        shard["w_ff1"], slot, "cast_w_ff1",
        rider=_merge_riders(_gather_ici_rider([slots[k] for k in _EARLY]),
                            _gather8_rider(jnp.pad(c, ((0, 8 - nex), (0, 0))), in_vmem=False)))

    a_in = jnp.concatenate([c8[:, :nex].reshape(N_DEV * nex, d), c_ctx.reshape(1, d), jnp.zeros((7, d), F32)], axis=0)
    b_sh = lax.dynamic_slice(b_ada, (0, chip * n_ada), (1, n_ada))
    mod_sh = _mod_fwd(a_in, w_ada[0], b_sh)
    mod8, w_in_f, w_uq_k, w_ukv_k = _run_rider(
        _merge_riders(_gather8_rider(mod_sh), _gather_d2d_rider([w_in_x, w_uq_x, w_ukv_x])), "ag_early")
    w_in_k = jnp.pad(w_in_f.reshape(IN_COLS, d), ((0, IN_PAD - IN_COLS), (0, 0)))
    mod_all = mod8[0::2].transpose(1, 0, 2).reshape(a_in.shape[0], N_CHIPS * n_ada)
    mod_me = lax.dynamic_slice(mod_all, (nex * dev, 0), (nex, N_CHIPS * n_ada)).reshape(nex, 6, d)
    mod_c = mod_all[N_DEV * nex].reshape(1, 6, d)
    modv = jnp.pad(jnp.concatenate([mod_me, mod_c], axis=0), ((0, 0), (0, 2), (0, 0)))

    gx, g_early, late, st_post, st_ret, st_pre = _local_step(
        x, ctx, loss_target, modv, lg, g_attn, g_ffn, g_final.reshape(1, d), g_ret, g_q_lora, g_kv_lora,
        w_in_k, w_uq_k, w_ukv_k, [slots[k] for k in _LATE], (core, slot))

    mine, theirs, (*late_theirs, gathered) = _reduce_scatter_vmem(
        g_early, [(IN_COLS // N_CHIPS, IN_COLS // N_CHIPS), (head_rows, MLA_HEAD), (KV_LORA, KV_LORA)],
        _merge_riders(_swap_rider(late), _gather8_rider(_pack_small(st_post, st_ret, st_pre))), "rs_early")
    tot = _small_reduce(gathered)
    dm = jnp.concatenate([
        gathered[:, 12:24].reshape(N_DEV * nex, 6 * d),
        jnp.concatenate([tot[8:10].reshape(1, 2 * d), jnp.zeros((1, 4 * d), F32)], axis=1),
        jnp.zeros((7, 6 * d), F32)], axis=0)
    dm_sh = lax.dynamic_slice(dm, (0, chip * n_ada), (dm.shape[0], n_ada))
    g_ada, da = _mod_bwd(a_in, dm_sh, w_ada[0])
    dcc = _allgather8(da[N_DEV * nex:], "ag_dcc")
    halves = dict(zip(_EARLY, zip(mine, theirs)))
    halves.update(zip(_LATE, zip(late, late_theirs)))
    grad, delta, new_m, new_v = {}, {}, {}, {}
    for k in _BIG:
        a, b = halves[k]
        res = _adamw_halves(shard_of(w, k), a, b, shard_of(m, k), shard_of(v, k), core, "adamw_" + k)
        grad[k], delta[k], new_m[k], new_v[k] = [(o.T if k in _TRANSPOSED else o).reshape(w[k].shape) for o in res]

    shp = w_ada.shape
    outs, _ = _adamw(w_ada[0], g_ada, m["w_ada"][0], v["w_ada"][0], "adamw_w_ada")
    grad["w_ada"] = g_ada.reshape(shp)
    delta["w_ada"], new_m["w_ada"], new_v["w_ada"] = [o.reshape(shp) for o in outs]
    rows = [{k: t[k].reshape(1, -1) for k in _SMALL_NAMES} for t in (w, m, v)]
    small = _small_final(tot, dcc, sg8, *[[t[k] for k in _SMALL_NAMES] for t in rows])
    for res, outs in zip((grad, delta, new_m, new_v), small[:4]):
        for k, o in zip(_SMALL_NAMES, outs):
            res[k] = o.reshape(w[k].shape)
    return (small[4][0, 0], gx, *[grad[k] for k in _WEIGHTS], *[delta[k] for k in _WEIGHTS],
            *[new_m[k] for k in _WEIGHTS], *[new_v[k] for k in _WEIGHTS])
```

```python
import functools
import math

import jax
import jax.numpy as jnp
from jax import lax
from jax.experimental import pallas as pl
from jax.experimental.pallas import tpu as pltpu

F32 = jnp.float32
BF16 = jnp.bfloat16
MESH = pl.DeviceIdType.MESH

EPS = 1e-6
D_MODEL = 1024
D_FF = 4096
HEADS = 4
RET_DK = 64
RET_DV = 128
MLA_NOPE = 128
MLA_ROPE = 64
MLA_HEAD = 256
Q_LORA = 384
KV_LORA = 256
GRID_W = 64
ROPE_BASE = 10000.0
IN_COLS = 2240
IN_PAD = 2304
PG_COLS = 1152
N_CHIPS = 4
N_DEV = 8
LANES = 128
ADAM_LR = 0.001
ADAM_B1 = 0.9
ADAM_B2 = 0.999
ADAM_EPS = 1e-08
ADAM_WD = 0.01
ADAM_STEP = 10
VMEM_LIMIT = 56 * 1024 * 1024


def _dot(a, b):
    return jnp.dot(a, b, preferred_element_type=F32)


def _dot_nt(a, b):
    return lax.dot_general(a, b, (((1,), (1,)), ((), ())), preferred_element_type=F32)


def _dot_tn(a, b):
    return lax.dot_general(a, b, (((0,), (0,)), ((), ())), preferred_element_type=F32)


def _params(sem=None, vmem=None):
    return pltpu.CompilerParams(dimension_semantics=sem, vmem_limit_bytes=vmem)


def _full(shape):
    n = len(shape)
    return pl.BlockSpec(shape, lambda *_: (0,) * n)


def _rope(x, cos, sin):
    w = x.shape[-1]
    lo = (lax.broadcasted_iota(jnp.int32, (1, w), 1) % 64) < 32
    swapped = jnp.where(lo, pltpu.roll(x, w - 32, 1), pltpu.roll(x, 32, 1))
    return x * cos + swapped * sin


def _rope_t(g, cos, sin):
    w = g.shape[-1]
    lo = (lax.broadcasted_iota(jnp.int32, (1, w), 1) % 64) < 32
    t = g * sin
    swapped = jnp.where(lo, pltpu.roll(t, w - 32, 1), pltpu.roll(t, 32, 1))
    return g * cos + swapped


def _rope_tables(seq, tm):
    rows = seq // GRID_W
    row = jnp.repeat(jnp.arange(rows, dtype=F32), GRID_W)
    col = jnp.tile(jnp.arange(GRID_W, dtype=F32), rows)
    n_freq = RET_DK // 4
    freq = ROPE_BASE ** (-jnp.arange(n_freq, dtype=F32) / n_freq)
    ang = jnp.concatenate([row[:, None] * freq, col[:, None] * freq], axis=-1)
    cos, sin = jnp.cos(ang), jnp.sin(ang)
    cos_t = jnp.tile(jnp.concatenate([cos, cos], -1), (1, HEADS))
    sin_t = jnp.tile(jnp.concatenate([-sin, sin], -1), (1, HEADS))
    cos_t = jnp.concatenate([cos_t, jnp.ones((tm, 4 * RET_DK), F32)], 0)
    sin_t = jnp.concatenate([sin_t, jnp.zeros((tm, 4 * RET_DK), F32)], 0)
    return cos_t, sin_t


def _adam_math(w, g, m, v):
    mn = ADAM_B1 * m + (1.0 - ADAM_B1) * g
    vn = ADAM_B2 * v + (1.0 - ADAM_B2) * (g * g)
    m_hat = mn / (1.0 - ADAM_B1 ** ADAM_STEP)
    v_hat = vn / (1.0 - ADAM_B2 ** ADAM_STEP)
    return -ADAM_LR * (m_hat / (jnp.sqrt(v_hat) + ADAM_EPS) + ADAM_WD * w), mn, vn


def _cast_into_slot(w, slot, name, rider=None):
    r, c = w.shape
    rb = max(b for b in range(16, 257, 16) if r % b == 0)

    def body(s_ref, w_ref, o_ref):
        o_ref[...] = w_ref[...].astype(BF16)

    (out,), carried = _hosted_call(
        body, (w,), name=name, grid=(r // rb,), prefetch=(slot,),
        in_specs=[pl.BlockSpec((rb, c), lambda i, s: (i, 0))],
        out_specs=[pl.BlockSpec((None, rb, c), lambda i, s: (s[0], i, 0))],
        out_shape=[jax.ShapeDtypeStruct((N_CHIPS, r, c), BF16)], sem=("parallel",), rider=rider)
    return out, carried


def _adamw_halves(w, mine, theirs, m, v, core, name):
    r, c = w.shape
    r2 = r // 2
    rb = max(b for b in range(8, r2 + 1, 8) if r2 % b == 0 and b * c * 4 <= (1 << 21))
    nbh = r2 // rb

    def body(z_ref, w_ref, a_ref, b_ref, m_ref, v_ref, g_ref, d_ref, mo_ref, vo_ref):
        here = (pl.program_id(0) // nbh) == z_ref[0]
        gg = jnp.where(here, a_ref[...], b_ref[...])
        g_ref[...] = gg
        d_ref[...], mo_ref[...], vo_ref[...] = _adam_math(w_ref[...], gg, m_ref[...], v_ref[...])

    spec = pl.BlockSpec((rb, c), lambda i, z: (i, 0))
    a_spec = pl.BlockSpec((rb, c), lambda i, z: (jnp.clip(i - z[0] * nbh, 0, nbh - 1), 0))
    b_spec = pl.BlockSpec((rb, c), lambda i, z: (jnp.clip(i - (1 - z[0]) * nbh, 0, nbh - 1), 0))
    shp = jax.ShapeDtypeStruct((r, c), F32)
    return pl.pallas_call(
        body, name=name,
        grid_spec=pltpu.PrefetchScalarGridSpec(
            num_scalar_prefetch=1, grid=(r // rb,), in_specs=[spec, a_spec, b_spec, spec, spec], out_specs=[spec] * 4),
        out_shape=[shp] * 4,
        compiler_params=_params(("parallel",)),
    )(core, w, mine, theirs, m, v)


def _adamw(w, g, m, v, name, rider=None):
    r, c = w.shape
    rb = r
    for cand in (256, 128, 64, 32, 16, 8):
        if r % cand == 0 and cand * c * 4 <= (1 << 20):
            rb = cand
            break
    if r * c * 4 <= (1 << 20):
        rb = r

    def body(w_ref, g_ref, m_ref, v_ref, d_ref, mo_ref, vo_ref):
        d_ref[...], mo_ref[...], vo_ref[...] = _adam_math(w_ref[...], g_ref[...], m_ref[...], v_ref[...])

    spec = pl.BlockSpec((rb, c), lambda i: (i, 0))
    shp = jax.ShapeDtypeStruct((r, c), F32)
    return _hosted_call(
        body, (w, g, m, v), name=name, grid=(r // rb,), in_specs=[spec] * 4, out_specs=[spec] * 3, out_shape=[shp] * 3,
        sem=("parallel",), rider=rider)


def _adamw_halves_group(items, core, name, rider=None):
    c = items[0][0].shape[1]
    rb = 128
    n = len(items)
    nbs = [it[0].shape[0] // rb for it in items]
    starts = [sum(nbs[:s]) for s in range(n)]

    def body(z_ref, *refs):
        ins, outs = refs[:5 * n], refs[5 * n:]
        i = pl.program_id(0)
        for s in range(n):
            w_ref, a_ref, b_ref, m_ref, v_ref = ins[5 * s:5 * s + 5]
            g_ref, d_ref, mo_ref, vo_ref = outs[4 * s:4 * s + 4]

            @pl.when(jnp.logical_and(i >= starts[s], i < starts[s] + nbs[s]))
            def _():
                here = ((i - starts[s]) // (nbs[s] // 2)) == z_ref[0]
                gg = jnp.where(here, a_ref[...], b_ref[...])
                g_ref[...] = gg
                d_ref[...], mo_ref[...], vo_ref[...] = _adam_math(w_ref[...], gg, m_ref[...], v_ref[...])

    in_specs, out_specs, out_shape, args = [pl.BlockSpec(memory_space=pltpu.SMEM)], [], [], [core]
    for (w, a, b, m, v), nb, st in zip(items, nbs, starts):
        full = pl.BlockSpec((rb, c), lambda i, nb=nb, st=st: (jnp.clip(i - st, 0, nb - 1), 0))
        half = pl.BlockSpec((rb, c), lambda i, nb=nb, st=st: (jnp.clip(i - st, 0, nb - 1) % (nb // 2), 0))
        in_specs += [full, half, half, full, full]
        out_specs += [full] * 4
        out_shape += [jax.ShapeDtypeStruct(w.shape, F32)] * 4
        args += [w, a, b, m, v]
    res, carried = _hosted_call(body, args, name=name, grid=(sum(nbs),), in_specs=in_specs, out_specs=out_specs,
                                out_shape=out_shape, sem=("arbitrary",), rider=rider)
    return [tuple(res[4 * s:4 * s + 4]) for s in range(n)], carried


def _decay_prep(dec):
    def body(d_ref, lg_ref, sg_ref):
        d = d_ref[...]
        lg_ref[...] = jnp.minimum(d, 0.0) - jnp.log(1.0 + jnp.exp(-jnp.abs(d)))
        sg_ref[...] = 1.0 / (1.0 + jnp.exp(d))

    shp = jax.ShapeDtypeStruct(dec.shape, F32)
    return pl.pallas_call(body, name="decay_prep", out_shape=[shp, shp])(dec)


def _mod_fwd(a_in, w_ada, b_sh):
    rows, d = a_in.shape
    n = w_ada.shape[1]
    bn = 512

    def body(a_ref, w_ref, b_ref, o_ref):
        a = a_ref[...]
        s = (a / (1.0 + jnp.exp(-a))).astype(BF16)
        o_ref[...] = _dot(s, w_ref[...].astype(BF16)) + b_ref[...]

    return pl.pallas_call(
        body, name="mod_fwd", grid=(n // bn,),
        in_specs=[_full((rows, d)), pl.BlockSpec((d, bn), lambda j: (0, j)), pl.BlockSpec((1, bn), lambda j: (0, j))],
        out_specs=pl.BlockSpec((rows, bn), lambda j: (0, j)),
        out_shape=jax.ShapeDtypeStruct((rows, n), F32),
        compiler_params=_params(("parallel",)),
    )(a_in, w_ada, b_sh)


def _mod_bwd(a_in, dm, w_ada):
    rows, d = a_in.shape
    n = w_ada.shape[1]
    bn = 512
    nb = n // bn

    def body(a_ref, dm_ref, w_ref, gw_ref, da_ref):
        j = pl.program_id(0)
        a = a_ref[...]
        s = (a / (1.0 + jnp.exp(-a))).astype(BF16)
        dmb = dm_ref[...].astype(BF16)
        gw_ref[...] = _dot_tn(s, dmb)
        part = _dot_nt(dmb, w_ref[...].astype(BF16))

        @pl.when(j == 0)
        def _():
            da_ref[...] = part

        @pl.when(j > 0)
        def _():
            da_ref[...] += part

    return pl.pallas_call(
        body, name="mod_bwd", grid=(nb,),
        in_specs=[_full((rows, d)), pl.BlockSpec((rows, bn), lambda j: (0, j)), pl.BlockSpec((d, bn), lambda j: (0, j))],
        out_specs=[pl.BlockSpec((d, bn), lambda j: (0, j)), _full((rows, d))],
        out_shape=[jax.ShapeDtypeStruct((d, n), F32), jax.ShapeDtypeStruct((rows, d), F32)],
        compiler_params=_params(("arbitrary",)),
    )(a_in, dm, w_ada)


def _pre_fwd(x2, ctx2, modv, g_attn, w_in, g_q, g_kv, w_uq, w_ukv, cos_t, sin_t, *, seq, tm, rider=None):
    t_lat, d = x2.shape
    t_ctx = ctx2.shape[0]
    nl, nc = t_lat // tm, t_ctx // tm
    n_all = t_lat + t_ctx
    tpe = seq // tm
    nex = t_lat // seq

    def body(x_ref, c_ref, mod_ref, g_ref, win_ref, gq_ref, gkv_ref, wuq_ref, wukv_ref, cos_ref, sin_ref,
             h_ref, pg_ref, rq_ref, rk_ref, rv_ref, nq_ref, nkv_ref, q_ref, k_ref, v_ref):
        i = pl.program_id(0)
        xt = jnp.where(i < nl, x_ref[...], c_ref[...])
        sh = mod_ref[0, 0:1, :]
        sc = mod_ref[0, 1:2, :]
        r = lax.rsqrt(jnp.mean(xt * xt, axis=-1, keepdims=True) + EPS)
        hb = ((xt * r) * g_ref[...] * (1.0 + sc) + sh).astype(BF16)
        h_ref[...] = hb
        p = _dot_nt(hb, win_ref[...])
        cos = cos_ref[...]
        sin = sin_ref[...]
        rq_ref[...] = _rope(p[:, 0:256], cos, sin).astype(BF16)
        rk_ref[...] = _rope(p[:, 256:512] * (RET_DK ** -0.5), cos, sin).astype(BF16)
        rv_ref[...] = p[:, 512:1024].astype(BF16)
        pg_ref[...] = p[:, 1024:2176]
        cq = p[:, 1536:1920]
        ckv = p[:, 1920:2176]
        nqb = (cq * lax.rsqrt(jnp.mean(cq * cq, axis=-1, keepdims=True) + EPS) * gq_ref[...]).astype(BF16)
        nkvb = (ckv * lax.rsqrt(jnp.mean(ckv * ckv, axis=-1, keepdims=True) + EPS) * gkv_ref[...]).astype(BF16)
        nq_ref[...] = nqb
        nkv_ref[...] = nkvb
        cos1 = cos[:, 0:LANES]
        sin1 = sin[:, 0:LANES]
        kpe = _rope(p[:, 2176:2304], cos1, sin1).astype(BF16)
        for hd in range(HEADS):
            o = hd * MLA_HEAD
            qh = _dot_nt(nqb, wuq_ref[hd]) * MLA_SCALE
            q_ref[:, o:o + 128] = qh[:, 0:128].astype(BF16)
            q_ref[:, o + 128:o + 256] = _rope(qh[:, 128:256], cos1, sin1).astype(BF16)
            kvh = _dot(nkvb, wukv_ref[hd])
            k_ref[:, o:o + 128] = kvh[:, 0:128].astype(BF16)
            k_ref[:, o + 128:o + 256] = kpe
            v_ref[:, hd * 128:(hd + 1) * 128] = kvh[:, 128:256].astype(BF16)

    def tile(width):
        return pl.BlockSpec((tm, width), lambda i: (i, 0))

    widths = (d, PG_COLS, 256, 256, 512, Q_LORA, KV_LORA, HEADS * MLA_HEAD, HEADS * MLA_HEAD, HEADS * 128)
    dtypes = (BF16, F32, BF16, BF16, BF16, BF16, BF16, BF16, BF16, BF16)
    tab = pl.BlockSpec((tm, 256), lambda i: (jnp.where(i < nl, i % tpe, tpe), 0))
    return _hosted_call(
        body, (x2, ctx2, modv, g_attn, w_in, g_q, g_kv, w_uq, w_ukv, cos_t, sin_t), name="pre_fwd", grid=(nl + nc,),
        in_specs=[
            pl.BlockSpec((tm, d), lambda i: (jnp.minimum(i, nl - 1), 0)),
            pl.BlockSpec((tm, d), lambda i: (jnp.maximum(i - nl, 0), 0)),
            pl.BlockSpec((1, 8, d), lambda i: (jnp.minimum(i // tpe, nex), 0, 0)),
            _full((1, d)), _full(w_in.shape), _full((1, Q_LORA)), _full((1, KV_LORA)),
            _full(w_uq.shape), _full(w_ukv.shape), tab, tab,
        ],
        out_specs=[tile(w) for w in widths],
        out_shape=[jax.ShapeDtypeStruct((n_all, w), dt) for w, dt in zip(widths, dtypes)],
        sem=("parallel",), rider=rider)


def _post(yret, ymla, x2, tgt2, modv, g_ffn, g_fin, w_out, w_ff1, w_ff2, *, seq, tm):
    t_lat, d = x2.shape
    nl = t_lat // tm
    tpe = seq // tm
    nex = t_lat // seq
    n_slab = w_ff1.shape[0]
    fs = w_ff1.shape[2]

    def body(yr_ref, ym_ref, x_ref, t_ref, mod_ref, gf_ref, gl_ref, wo_ref, w1_ref, w2_ref,
             mix_ref, a_ref, du_ref, h2_ref, df_ref, dmo_ref, dmix_ref, dxm_ref, st_ref, ru_ref):
        i = pl.program_id(0)
        gt_a = mod_ref[0, 2:3, :]
        sh_f = mod_ref[0, 3:4, :]
        sc_f = mod_ref[0, 4:5, :]
        gt_f = mod_ref[0, 5:6, :]
        g_ffn_v = gf_ref[...]
        g_fin_v = gl_ref[...]
        yr = yr_ref[...]
        ym = ym_ref[...]
        mix_ref[:, 0:512] = yr
        mix_ref[:, 512:1024] = ym
        op = _dot(yr, wo_ref[0:512, :]) + _dot(ym, wo_ref[512:1024, :])
        x_mid = x_ref[...] + gt_a * op
        r2 = lax.rsqrt(jnp.mean(x_mid * x_mid, axis=-1, keepdims=True) + EPS)
        xh2 = x_mid * r2
        h2b = (xh2 * g_ffn_v * (1.0 + sc_f) + sh_f).astype(BF16)
        h2_ref[...] = h2b
        f = jnp.zeros((tm, d), F32)
        for s in range(n_slab):
            ru = jnp.maximum(_dot(h2b, w1_ref[s]), 0.0)
            ru_ref[:, s * fs:(s + 1) * fs] = ru
            ab = (ru * ru).astype(BF16)
            a_ref[:, s * fs:(s + 1) * fs] = ab
            f = f + _dot(ab, w2_ref[s * fs:(s + 1) * fs, :])
        x_out = x_mid + gt_f * f
        r3 = lax.rsqrt(jnp.mean(x_out * x_out, axis=-1, keepdims=True) + EPS)
        xh3 = x_out * r3
        err = xh3 * g_fin_v - t_ref[...]
        dy = err * (1.0 / d)
        dxh3 = dy * g_fin_v
        dx_out = r3 * (dxh3 - xh3 * jnp.mean(dxh3 * xh3, axis=-1, keepdims=True))
        dfb = (dx_out * gt_f).astype(BF16)
        df_ref[...] = dfb
        dh2 = jnp.zeros((tm, d), F32)
        for s in range(n_slab):
            da = _dot_nt(dfb, w2_ref[s * fs:(s + 1) * fs, :])
            dub = (da * (2.0 * ru_ref[:, s * fs:(s + 1) * fs])).astype(BF16)
            du_ref[:, s * fs:(s + 1) * fs] = dub
            dh2 = dh2 + _dot_nt(dub, w1_ref[s])
        dxh2 = dh2 * (1.0 + sc_f) * g_ffn_v
        dx_mid = dx_out + r2 * (dxh2 - xh2 * jnp.mean(dxh2 * xh2, axis=-1, keepdims=True))
        dxm_ref[...] = dx_mid
        dmob = (dx_mid * gt_a).astype(BF16)
        dmo_ref[...] = dmob
        dmix_ref[...] = _dot_nt(dmob, wo_ref[...]).astype(BF16)

        def rsum(v):
            return jnp.sum(v, axis=0, keepdims=True)

        stats = jnp.concatenate([
            rsum(dh2), rsum(dh2 * xh2 * g_ffn_v), rsum(dx_out * f), rsum(dx_mid * op),
            rsum(dh2 * (1.0 + sc_f) * xh2), rsum(dy * xh3), rsum(err * err), jnp.zeros((1, d), F32)], axis=0)

        @pl.when(i % tpe == 0)
        def _():
            st_ref[0] = stats

        @pl.when(i % tpe != 0)
        def _():
            st_ref[0] += stats

    def tile(width):
        return pl.BlockSpec((tm, width), lambda i: (i, 0))

    widths = (d, D_FF, D_FF, d, d, d, d, d)
    dtypes = (BF16, BF16, BF16, BF16, BF16, BF16, BF16, F32)
    const = pl.Buffered(1)
    return pl.pallas_call(
        body, name="post", grid=(nl,),
        in_specs=[
            tile(512), tile(512), tile(d), tile(d),
            pl.BlockSpec((1, 8, d), lambda i: (i // tpe, 0, 0)),
            _full((1, d)), _full((1, d)),
            pl.BlockSpec(w_out.shape, lambda i: (0, 0), pipeline_mode=const),
            pl.BlockSpec(w_ff1.shape, lambda i: (0, 0, 0), pipeline_mode=const),
            pl.BlockSpec(w_ff2.shape, lambda i: (0, 0), pipeline_mode=const),
        ],
        out_specs=[tile(w) for w in widths] + [pl.BlockSpec((1, 8, d), lambda i: (i // tpe, 0, 0))],
        out_shape=[jax.ShapeDtypeStruct((t_lat, w), dt) for w, dt in zip(widths, dtypes)]
        + [jax.ShapeDtypeStruct((nex, 8, d), F32)],
        scratch_shapes=[pltpu.VMEM((tm, D_FF), F32)],
        compiler_params=_params(("arbitrary",), VMEM_LIMIT),
    )(yret, ymla, x2, tgt2, modv, g_ffn, g_fin, w_out, w_ff1, w_ff2)


def _pre_bwd(x2, ctx2, modv, g_attn, pg, drq, drk, dkc_r, drv, dvc_r, drg, dq_m, dkl, dkc, dvl, dvc, dxm,
             w_in, g_q, g_kv, w_uq, w_ukv, cos_t, sin_t, *, seq, tm, rider=None):
    t_lat, d = x2.shape
    t_ctx = ctx2.shape[0]
    nl, nc = t_lat // tm, t_ctx // tm
    n_all = t_lat + t_ctx
    tpe = seq // tm
    nex = t_lat // seq

    def body(x_ref, c_ref, mod_ref, g_ref, pg_ref, drq_ref, drk_ref, dkcr_ref, drv_ref, dvcr_ref, drg_ref,
             dq_ref, dkl_ref, dkc_ref, dvl_ref, dvc_ref, dxm_ref, win_ref, gq_ref, gkv_ref, wuq_ref, wukv_ref,
             cos_ref, sin_ref, dpb_ref, dqf_ref, dkvf_ref, gx_ref, st_ref):
        i = pl.program_id(0)
        lat = i < nl
        latf = lat.astype(F32)
        cos = cos_ref[...]
        sin = sin_ref[...]
        cos1 = cos[:, 0:LANES]
        sin1 = sin[:, 0:LANES]
        d_rq = _rope_t(drq_ref[...] * latf, cos, sin)
        d_rk = _rope_t(jnp.where(lat, drk_ref[...], dkcr_ref[...]), cos, sin) * (RET_DK ** -0.5)
        d_rv = jnp.where(lat, drv_ref[...], dvcr_ref[...])
        d_rg = drg_ref[...] * latf
        dq_all = dq_ref[...] * (latf * MLA_SCALE)
        dk_all = jnp.where(lat, dkl_ref[...], dkc_ref[...])
        dv_all = jnp.where(lat, dvl_ref[...], dvc_ref[...])
        dnq = jnp.zeros((tm, Q_LORA), F32)
        dnkv = jnp.zeros((tm, KV_LORA), F32)
        dkpe = jnp.zeros((tm, LANES), F32)
        for hd in range(HEADS):
            o = hd * MLA_HEAD
            dqh = jnp.concatenate([dq_all[:, o:o + 128], _rope_t(dq_all[:, o + 128:o + 256], cos1, sin1)],
                                  axis=1).astype(BF16)
            dqf_ref[:, o:o + 256] = dqh
            dnq = dnq + _dot(dqh, wuq_ref[hd])
            dkpe = dkpe + dk_all[:, o + 128:o + 256]
            dkvh = jnp.concatenate([dk_all[:, o:o + 128], dv_all[:, hd * 128:(hd + 1) * 128]], axis=1).astype(BF16)
            dkvf_ref[:, o:o + 256] = dkvh
            dnkv = dnkv + _dot_nt(dkvh, wukv_ref[hd])
        d_kpe = _rope_t(dkpe, cos1, sin1)
        pgv = pg_ref[...]
        cq = pgv[:, 512:896]
        ckv = pgv[:, 896:1152]
        rq_ = lax.rsqrt(jnp.mean(cq * cq, axis=-1, keepdims=True) + EPS)
        cqh = cq * rq_
        dcqh = dnq * gq_ref[...]
        d_cq = rq_ * (dcqh - cqh * jnp.mean(dcqh * cqh, axis=-1, keepdims=True))
        rkv_ = lax.rsqrt(jnp.mean(ckv * ckv, axis=-1, keepdims=True) + EPS)
        ckvh = ckv * rkv_
        dckvh = dnkv * gkv_ref[...]
        d_ckv = rkv_ * (dckvh - ckvh * jnp.mean(dckvh * ckvh, axis=-1, keepdims=True))
        dpb = jnp.concatenate([d_rq, d_rk, d_rv, d_rg, d_cq, d_ckv, d_kpe], axis=1).astype(BF16)
        dpb_ref[...] = dpb
        dh = _dot(dpb, win_ref[...])
        xt = jnp.where(lat, x_ref[...], c_ref[...])
        sc = mod_ref[0, 1:2, :]
        g = g_ref[...]
        r = lax.rsqrt(jnp.mean(xt * xt, axis=-1, keepdims=True) + EPS)
        xh = xt * r
        dxh = dh * (1.0 + sc) * g
        dx = r * (dxh - xh * jnp.mean(dxh * xh, axis=-1, keepdims=True))

        @pl.when(lat)
        def _():
            gx_ref[...] = dxm_ref[...] + dx

        def rsum(v):
            return jnp.sum(v, axis=0, keepdims=True)

        def widen(v):
            return jnp.concatenate([v, jnp.zeros((1, d - v.shape[1]), F32)], axis=1)

        stats = jnp.concatenate([
            rsum(dh), rsum(dh * xh * g), rsum(dh * (1.0 + sc) * xh), widen(rsum(dnq * cqh)), widen(rsum(dnkv * ckvh)),
            jnp.zeros((3, d), F32)], axis=0)
        first = jnp.logical_or(jnp.logical_and(lat, i % tpe == 0), i == nl)

        @pl.when(first)
        def _():
            st_ref[0] = stats

        @pl.when(jnp.logical_not(first))
        def _():
            st_ref[0] += stats

    def lat_tile(width):
        return pl.BlockSpec((tm, width), lambda i: (jnp.minimum(i, nl - 1), 0))

    def ctx_tile(width):
        return pl.BlockSpec((tm, width), lambda i: (jnp.maximum(i - nl, 0), 0))

    def tile(width):
        return pl.BlockSpec((tm, width), lambda i: (i, 0))

    tab = pl.BlockSpec((tm, 256), lambda i: (jnp.where(i < nl, i % tpe, tpe), 0))
    ex = pl.BlockSpec((1, 8, d), lambda i: (jnp.minimum(i // tpe, nex), 0, 0))
    return _hosted_call(
        body, (x2, ctx2, modv, g_attn, pg, drq, drk, dkc_r, drv, dvc_r, drg, dq_m, dkl, dkc, dvl, dvc, dxm,
               w_in, g_q, g_kv, w_uq, w_ukv, cos_t, sin_t), name="pre_bwd", grid=(nl + nc,),
        in_specs=[
            lat_tile(d), ctx_tile(d), ex, _full((1, d)), tile(PG_COLS),
            lat_tile(256), lat_tile(256), ctx_tile(256), lat_tile(512), ctx_tile(512), lat_tile(512),
            lat_tile(1024), lat_tile(1024), ctx_tile(1024), lat_tile(512), ctx_tile(512), lat_tile(d),
            _full(w_in.shape), _full((1, Q_LORA)), _full((1, KV_LORA)), _full(w_uq.shape), _full(w_ukv.shape),
            tab, tab,
        ],
        out_specs=[tile(IN_PAD), tile(1024), tile(1024), lat_tile(d), ex],
        out_shape=[
            jax.ShapeDtypeStruct((n_all, IN_PAD), BF16), jax.ShapeDtypeStruct((n_all, 1024), BF16),
            jax.ShapeDtypeStruct((n_all, 1024), BF16), jax.ShapeDtypeStruct((t_lat, d), F32),
            jax.ShapeDtypeStruct((nex + 1, 8, d), F32),
        ],
        sem=("arbitrary",), rider=rider)


MLA_SCALE = 1.0 / math.sqrt(MLA_NOPE + MLA_ROPE)
KEY_BLOCK = 2048


def _mla_specs(t_lat, seq, ctx_len, tq):
    nqt = seq // tq
    cb = t_lat // ctx_len
    q = pl.BlockSpec((tq, MLA_HEAD), lambda b, h, j: (b * nqt + j, h))
    kl = pl.BlockSpec((seq, MLA_HEAD), lambda b, h, j: (b, h))
    kc = pl.BlockSpec((ctx_len, MLA_HEAD), lambda b, h, j: (cb + b, h))
    vl = pl.BlockSpec((seq, 128), lambda b, h, j: (b, h))
    vc = pl.BlockSpec((ctx_len, 128), lambda b, h, j: (cb + b, h))
    o = pl.BlockSpec((tq, 128), lambda b, h, j: (b * nqt + j, h))
    return q, kl, kc, vl, vc, o


def _mla_fwd(q, k, v, *, t_lat, seq, ctx_len, tq, rider=None):
    nex = t_lat // seq

    def body(q_ref, kl_ref, kc_ref, vl_ref, vc_ref, o_ref, lse_ref):
        qb = q_ref[...]
        s = _dot_nt(qb, kl_ref[...])
        sc = _dot_nt(qb, kc_ref[...])
        m = jnp.maximum(jnp.max(s, axis=-1, keepdims=True), jnp.max(sc, axis=-1, keepdims=True))
        p = jnp.exp(s - m)
        pc = jnp.exp(sc - m)
        total = jnp.sum(p, axis=-1, keepdims=True) + jnp.sum(pc, axis=-1, keepdims=True)
        o = _dot(p.astype(BF16), vl_ref[...]) + _dot(pc.astype(BF16), vc_ref[...])
        o_ref[...] = (o * (1.0 / total)).astype(BF16)
        lse_ref[...] = jnp.broadcast_to(m + jnp.log(total), lse_ref.shape)

    qs, kl, kc, vl, vc, os_ = _mla_specs(t_lat, seq, ctx_len, tq)
    return _hosted_call(
        body, (q, k, k, v, v), name="mla_fwd", grid=(nex, HEADS, seq // tq),
        in_specs=[qs, kl, kc, vl, vc], out_specs=[os_, os_],
        out_shape=[jax.ShapeDtypeStruct((t_lat, HEADS * 128), BF16), jax.ShapeDtypeStruct((t_lat, HEADS * 128), F32)],
        sem=("parallel", "parallel", "arbitrary"), rider=rider)


def _mla_bwd(q, k, v, ymla, lse, dmix, *, t_lat, seq, ctx_len, tq, rider=None):
    nex = t_lat // seq
    nqt = seq // tq
    t_ctx = nex * ctx_len
    kb = min(KEY_BLOCK, seq)

    def body(q_ref, kl_ref, kc_ref, vl_ref, vc_ref, o_ref, lse_ref, do_ref, dq_ref, dkl_ref, dkc_ref, dvl_ref, dvc_ref):
        j = pl.program_id(2)

        @pl.when(j == 0)
        def _():
            dkl_ref[...] = jnp.zeros(dkl_ref.shape, F32)
            dkc_ref[...] = jnp.zeros(dkc_ref.shape, F32)
            dvl_ref[...] = jnp.zeros(dvl_ref.shape, F32)
            dvc_ref[...] = jnp.zeros(dvc_ref.shape, F32)

        qb = q_ref[...]
        dob = do_ref[...]
        delta = jnp.sum(dob.astype(F32) * o_ref[...].astype(F32), axis=-1, keepdims=True)
        lse_row = lse_ref[:, 0:1]

        def block(k_ref, v_ref, dk_ref, dv_ref, rows):
            kbl = k_ref[rows, :]
            vbl = v_ref[rows, :]
            p = jnp.exp(_dot_nt(qb, kbl) - lse_row)
            ds = (p * (_dot_nt(dob, vbl) - delta)).astype(BF16)
            dk_ref[rows, :] += _dot_tn(ds, qb)
            dv_ref[rows, :] += _dot_tn(p.astype(BF16), dob)
            return _dot(ds, kbl)

        dq = block(kc_ref, vc_ref, dkc_ref, dvc_ref, pl.ds(0, ctx_len))
        for i in range(seq // kb):
            dq = dq + block(kl_ref, vl_ref, dkl_ref, dvl_ref, pl.ds(i * kb, kb))
        dq_ref[...] = dq

    qs, kl, kc, vl, vc, os_ = _mla_specs(t_lat, seq, ctx_len, tq)
    do_spec = pl.BlockSpec((tq, 128), lambda b, h, j: (b * nqt + j, HEADS + h))
    return _hosted_call(
        body, (q, k, k, v, v, ymla, lse, dmix), name="mla_bwd", grid=(nex, HEADS, nqt),
        in_specs=[qs, kl, kc, vl, vc, os_, os_, do_spec],
        out_specs=[
            qs,
            pl.BlockSpec((seq, MLA_HEAD), lambda b, h, j: (b, h)),
            pl.BlockSpec((ctx_len, MLA_HEAD), lambda b, h, j: (b, h)),
            pl.BlockSpec((seq, 128), lambda b, h, j: (b, h)),
            pl.BlockSpec((ctx_len, 128), lambda b, h, j: (b, h)),
        ],
        out_shape=[
            jax.ShapeDtypeStruct((t_lat, HEADS * MLA_HEAD), F32),
            jax.ShapeDtypeStruct((t_lat, HEADS * MLA_HEAD), F32),
            jax.ShapeDtypeStruct((t_ctx, HEADS * MLA_HEAD), F32),
            jax.ShapeDtypeStruct((t_lat, HEADS * 128), F32),
            jax.ShapeDtypeStruct((t_ctx, HEADS * 128), F32),
        ],
        sem=("parallel", "parallel", "arbitrary"), rider=rider)


def _decay_terms(lg, chunk, forward):
    ii = lax.broadcasted_iota(jnp.int32, (chunk, chunk), 0)
    jj = lax.broadcasted_iota(jnp.int32, (chunk, chunk), 1)
    diff = (ii - jj) if forward else (jj - ii)
    dist = jnp.maximum(diff, 0).astype(F32)
    dmat = jnp.where(diff >= 0, jnp.exp(lg * dist), 0.0)
    pos = lax.broadcasted_iota(jnp.int32, (chunk, 1), 0).astype(F32)
    if forward:
        e_q = pos + 1.0
        e_k = (chunk - 1.0) - pos
    else:
        e_q = chunk - pos
        e_k = pos
    wq = jnp.exp(lg * e_q)
    wk = jnp.exp(lg * e_k)
    cd = jnp.exp(jnp.full((1, 1), lg * chunk, F32))
    return dmat, dist, wq, wk, e_q, e_k, cd


def _ctx_weights(lg, ctx_len, forward):
    pos = lax.broadcasted_iota(jnp.int32, (ctx_len, 1), 0).astype(F32)
    e = ((ctx_len - 1.0) - pos) if forward else pos
    return jnp.exp(lg * e), e


def _pair_specs(t_lat, seq, ctx_len):
    cb = t_lat // ctx_len
    qk = pl.BlockSpec((seq, 128), lambda b, p: (b, p))
    v = pl.BlockSpec((seq, 256), lambda b, p: (b, p))
    kc = pl.BlockSpec((ctx_len, 128), lambda b, p: (cb + b, p))
    vc = pl.BlockSpec((ctx_len, 256), lambda b, p: (cb + b, p))
    return qk, v, kc, vc


def _lane_masks():
    lane = lax.broadcasted_iota(jnp.int32, (1, 128), 1)
    return [(lane // RET_DK) == hh for hh in (0, 1)]


def _ret_fwd_pair(rq, rk, rv, pg, lg, g_ret, *, t_lat, seq, ctx_len, chunk, rider=None):
    nex = t_lat // seq
    n_chunk = seq // chunk

    def body(q_ref, k_ref, v_ref, kc_ref, vc_ref, rg_ref, lg_ref, g_ref, y_ref, o_ref):
        pair = pl.program_id(1)
        masks = _lane_masks()
        kcf = kc_ref[...].astype(F32)

        def run(forward):
            terms, s0 = [], []
            for hh in (0, 1):
                lgd = lg_ref[0 if forward else 1, 2 * pair + hh]
                terms.append(_decay_terms(lgd, chunk, forward))
                wc, _ = _ctx_weights(lgd, ctx_len, forward)
                s0.append(_dot_tn((jnp.where(masks[hh], kcf, 0.0) * wc).astype(BF16), vc_ref[:, hh * 128:(hh + 1) * 128]))

            def step(t, states):
                n = t if forward else n_chunk - 1 - t
                sl = pl.ds(pl.multiple_of(n * chunk, chunk), chunk)
                qb = q_ref[sl, :]
                kf_all = k_ref[sl, :].astype(F32)
                new = []
                for hh in (0, 1):
                    dmat, _, wq, wk, _, _, cd = terms[hh]
                    cols = slice(hh * 128, (hh + 1) * 128)
                    qm = jnp.where(masks[hh], qb, jnp.zeros((), BF16))
                    kf = jnp.where(masks[hh], kf_all, 0.0)
                    vb = v_ref[sl, cols]
                    a = _dot_nt(qm, kf.astype(BF16)) * dmat
                    o = _dot(a.astype(BF16), vb) + wq * _dot(qm, states[hh].astype(BF16))
                    if forward:
                        o_ref[sl, cols] = o
                    else:
                        o = o_ref[sl, cols] + o
                        o_ref[sl, cols] = o
                        mu = jnp.mean(o, axis=-1, keepdims=True)
                        oc = o - mu
                        var = jnp.mean(oc * oc, axis=-1, keepdims=True)
                        rg = rg_ref[sl, cols]
                        y_ref[sl, cols] = (oc * lax.rsqrt(var + EPS) * g_ref[:, cols] * (rg / (1.0 + jnp.exp(-rg)))).astype(BF16)
                    new.append(cd * states[hh] + _dot_tn((kf * wk).astype(BF16), vb))
                return tuple(new)

            lax.fori_loop(0, n_chunk, step, tuple(s0))

        run(True)
        run(False)

    qk, v, kc, vc = _pair_specs(t_lat, seq, ctx_len)
    return _hosted_call(
        body, (rq, rk, rv, rk, rv, pg, lg, g_ret), name="ret_fwd", grid=(nex, HEADS // 2),
        in_specs=[qk, qk, v, kc, vc, v, pl.BlockSpec(memory_space=pltpu.SMEM), pl.BlockSpec((1, 256), lambda b, p: (0, p))],
        out_specs=[v, v],
        out_shape=[jax.ShapeDtypeStruct((t_lat, HEADS * RET_DV), BF16), jax.ShapeDtypeStruct((t_lat, HEADS * RET_DV), F32)],
        sem=("parallel", "arbitrary"), rider=rider)


def _ret_bwd_pair(rq, rk, rv, pg, osum, dmix, lg, g_ret, *, t_lat, seq, ctx_len, chunk, rider=None):
    nex = t_lat // seq
    n_chunk = seq // chunk
    t_ctx = nex * ctx_len

    def body(q_ref, k_ref, v_ref, kc_ref, vc_ref, rg_ref, o_ref, dy_ref, lg_ref, g_ref,
             dq_ref, dk_ref, dv_ref, dkc_ref, dvc_ref, drg_ref, st_ref, do_s, s_st):
        pair = pl.program_id(1)
        masks = _lane_masks()
        kcf = kc_ref[...].astype(F32)

        def norm_step(n, dgains):
            sl = pl.ds(pl.multiple_of(n * chunk, chunk), chunk)
            out = []
            for hh in (0, 1):
                cols = slice(hh * 128, (hh + 1) * 128)
                gain = g_ref[:, cols]
                o = o_ref[sl, cols]
                mu = jnp.mean(o, axis=-1, keepdims=True)
                oc = o - mu
                rstd = lax.rsqrt(jnp.mean(oc * oc, axis=-1, keepdims=True) + EPS)
                ohat = oc * rstd
                rg = rg_ref[sl, cols]
                sg = 1.0 / (1.0 + jnp.exp(-rg))
                dy = dy_ref[sl, cols].astype(F32)
                don = dy * (rg * sg)
                drg_ref[sl, cols] = dy * (ohat * gain) * (sg * (1.0 + rg * (1.0 - sg)))
                dohat = don * gain
                do_s[sl, cols] = rstd * (dohat - jnp.mean(dohat, axis=-1, keepdims=True)
                                         - ohat * jnp.mean(dohat * ohat, axis=-1, keepdims=True))
                out.append(dgains[hh] + jnp.sum(don * ohat, axis=0, keepdims=True))
            return tuple(out)

        zero_row = jnp.zeros((1, 128), F32)
        dgains = lax.fori_loop(0, n_chunk, norm_step, (zero_row, zero_row))
        dq_ref[...] = jnp.zeros(dq_ref.shape, F32)
        dk_ref[...] = jnp.zeros(dk_ref.shape, F32)
        dv_ref[...] = jnp.zeros(dv_ref.shape, F32)

        def run(forward):
            terms, ctxw, s0 = [], [], []
            for hh in (0, 1):
                lgd = lg_ref[0 if forward else 1, 2 * pair + hh]
                terms.append(_decay_terms(lgd, chunk, forward))
                ctxw.append(_ctx_weights(lgd, ctx_len, forward))
                s0.append(_dot_tn((jnp.where(masks[hh], kcf, 0.0) * ctxw[hh][0]).astype(BF16),
                                  vc_ref[:, hh * 128:(hh + 1) * 128]))

            def state_step(t, states):
                n = t if forward else n_chunk - 1 - t
                sl = pl.ds(pl.multiple_of(n * chunk, chunk), chunk)
                kf_all = k_ref[sl, :].astype(F32)
                new = []
                for hh in (0, 1):
                    wk, cd = terms[hh][3], terms[hh][6]
                    s_st[hh, n] = states[hh]
                    kf = jnp.where(masks[hh], kf_all, 0.0)
                    new.append(cd * states[hh] + _dot_tn((kf * wk).astype(BF16), v_ref[sl, hh * 128:(hh + 1) * 128]))
                return tuple(new)

            lax.fori_loop(0, n_chunk, state_step, tuple(s0))

            def grad_step(t, carry):
                n = (n_chunk - 1 - t) if forward else t
                sl = pl.ds(pl.multiple_of(n * chunk, chunk), chunk)
                qb = q_ref[sl, :]
                kf_all = k_ref[sl, :].astype(F32)
                dq_sum = jnp.zeros((chunk, 128), F32)
                dk_sum = jnp.zeros((chunk, 128), F32)
                out = []
                for hh in (0, 1):
                    g_next, dlg = carry[hh]
                    dmat, dist, wq, wk, e_q, e_k, cd = terms[hh]
                    cols = slice(hh * 128, (hh + 1) * 128)
                    qm = jnp.where(masks[hh], qb, jnp.zeros((), BF16))
                    kf = jnp.where(masks[hh], kf_all, 0.0)
                    kb = kf.astype(BF16)
                    vb = v_ref[sl, cols]
                    do = do_s[sl, cols]
                    dob = do.astype(BF16)
                    s_n = s_st[hh, n]
                    s_nb = s_n.astype(BF16)
                    gb = g_next.astype(BF16)
                    dk_cross = wk * _dot_nt(vb, gb)
                    dv_cross = _dot((kf * wk).astype(BF16), gb)
                    a = _dot_nt(qm, kb) * dmat
                    da_raw = _dot_nt(dob, vb)
                    dab = (da_raw * dmat).astype(BF16)
                    o_cross = wq * _dot(qm, s_nb)
                    dq_sum = dq_sum + _dot(dab, kb) + wq * _dot_nt(dob, s_nb)
                    dk_sum = dk_sum + _dot_tn(dab, qm) + dk_cross
                    dv_ref[sl, cols] += _dot_tn(a.astype(BF16), dob) + dv_cross
                    dlg = (dlg + chunk * cd * jnp.sum(g_next * s_n, keepdims=True)
                           + jnp.sum(e_k * jnp.sum(kf * dk_cross, axis=-1, keepdims=True), keepdims=True)
                           + jnp.sum(dist * a * da_raw, keepdims=True)
                           + jnp.sum(e_q * jnp.sum(o_cross * do, axis=-1, keepdims=True), keepdims=True))
                    out.append((cd * g_next + _dot_tn((qm.astype(F32) * wq).astype(BF16), dob), dlg))
                dq_ref[sl, :] += dq_sum
                dk_ref[sl, :] += dk_sum
                return tuple(out)

            zero = (jnp.zeros((128, 128), F32), jnp.zeros((1, 1), F32))
            res = lax.fori_loop(0, n_chunk, grad_step, (zero, zero))
            dkc_sum = jnp.zeros((ctx_len, 128), F32)
            dvc, dlgs = [], []
            for hh in (0, 1):
                ds0, dlg = res[hh]
                wc, e_c = ctxw[hh]
                kcm = jnp.where(masks[hh], kcf, 0.0)
                ds0b = ds0.astype(BF16)
                dkc_part = wc * _dot_nt(vc_ref[:, hh * 128:(hh + 1) * 128], ds0b)
                dkc_sum = dkc_sum + dkc_part
                dvc.append(_dot((kcm * wc).astype(BF16), ds0b))
                dlgs.append(dlg + jnp.sum(e_c * jnp.sum(kcm * dkc_part, axis=-1, keepdims=True), keepdims=True))
            return dkc_sum, dvc, dlgs

        dkc_f, dvc_f, dlg_f = run(True)
        dkc_b, dvc_b, dlg_b = run(False)
        dkc_ref[...] = dkc_f + dkc_b
        for hh in (0, 1):
            cols = slice(hh * 128, (hh + 1) * 128)
            dvc_ref[:, cols] = dvc_f[hh] + dvc_b[hh]
            st_ref[0, :, cols] = jnp.concatenate([
                dgains[hh], jnp.broadcast_to(dlg_f[hh], (1, 128)), jnp.broadcast_to(dlg_b[hh], (1, 128)),
                jnp.zeros((5, 128), F32)], axis=0)

    qk, v, kc, vc = _pair_specs(t_lat, seq, ctx_len)
    return _hosted_call(
        body, (rq, rk, rv, rk, rv, pg, osum, dmix, lg, g_ret), name="ret_bwd", grid=(nex, HEADS // 2),
        in_specs=[qk, qk, v, kc, vc, v, v, v, pl.BlockSpec(memory_space=pltpu.SMEM),
                  pl.BlockSpec((1, 256), lambda b, p: (0, p))],
        out_specs=[
            qk, qk, v,
            pl.BlockSpec((ctx_len, 128), lambda b, p: (b, p)),
            pl.BlockSpec((ctx_len, 256), lambda b, p: (b, p)),
            v,
            pl.BlockSpec((1, 8, 256), lambda b, p: (b, 0, p)),
        ],
        out_shape=[
            jax.ShapeDtypeStruct((t_lat, 256), F32), jax.ShapeDtypeStruct((t_lat, 256), F32),
            jax.ShapeDtypeStruct((t_lat, 512), F32), jax.ShapeDtypeStruct((t_ctx, 256), F32),
            jax.ShapeDtypeStruct((t_ctx, 512), F32), jax.ShapeDtypeStruct((t_lat, 512), F32),
            jax.ShapeDtypeStruct((nex, 8, 512), F32),
        ],
        scratch_shapes=[pltpu.VMEM((seq, 256), F32), pltpu.VMEM((2, n_chunk, 128, 128), F32)],
        sem=("parallel", "arbitrary"), rider=rider)


def _ret_specs(t_lat, seq, ctx_len):
    cb = t_lat // ctx_len
    qk = pl.BlockSpec((seq, 128), lambda b, h: (b, h // 2))
    v = pl.BlockSpec((seq, 128), lambda b, h: (b, h))
    kc = pl.BlockSpec((ctx_len, 128), lambda b, h: (cb + b, h // 2))
    vc = pl.BlockSpec((ctx_len, 128), lambda b, h: (cb + b, h))
    return qk, v, kc, vc


def _head_mask(h):
    lane = lax.broadcasted_iota(jnp.int32, (1, 128), 1)
    return (lane // RET_DK) == (h % 2)


def _ret_fwd(rq, rk, rv, pg, lg, g_ret, *, t_lat, seq, ctx_len, chunk, rider=None):
    nex = t_lat // seq
    n_chunk = seq // chunk

    def body(q_ref, k_ref, v_ref, kc_ref, vc_ref, rg_ref, lg_ref, g_ref, y_ref, o_ref):
        h = pl.program_id(1)
        hm = _head_mask(h)
        gain = g_ref[...]
        kcm = jnp.where(hm, kc_ref[...].astype(F32), 0.0)
        vcb = vc_ref[...]

        def run(forward):
            lgd = lg_ref[0 if forward else 1, h]
            dmat, _, wq, wk, _, _, cd = _decay_terms(lgd, chunk, forward)
            wc, _ = _ctx_weights(lgd, ctx_len, forward)
            s0 = _dot_tn((kcm * wc).astype(BF16), vcb)

            def step(t, s):
                n = t if forward else n_chunk - 1 - t
                sl = pl.ds(pl.multiple_of(n * chunk, chunk), chunk)
                qm = jnp.where(hm, q_ref[sl, :], jnp.zeros((), BF16))
                kf = jnp.where(hm, k_ref[sl, :].astype(F32), 0.0)
                vb = v_ref[sl, :]
                a = _dot_nt(qm, kf.astype(BF16)) * dmat
                o = _dot(a.astype(BF16), vb) + wq * _dot(qm, s.astype(BF16))
                if forward:
                    o_ref[sl, :] = o
                else:
                    o = o_ref[sl, :] + o
                    o_ref[sl, :] = o
                    mu = jnp.mean(o, axis=-1, keepdims=True)
                    oc = o - mu
                    var = jnp.mean(oc * oc, axis=-1, keepdims=True)
                    on = oc * lax.rsqrt(var + EPS) * gain
                    rg = rg_ref[sl, :]
                    y_ref[sl, :] = (on * (rg / (1.0 + jnp.exp(-rg)))).astype(BF16)
                return cd * s + _dot_tn((kf * wk).astype(BF16), vb)

            lax.fori_loop(0, n_chunk, step, s0)

        run(True)
        run(False)

    qk, v, kc, vc = _ret_specs(t_lat, seq, ctx_len)
    return _hosted_call(
        body, (rq, rk, rv, rk, rv, pg, lg, g_ret), name="ret_fwd", grid=(nex, HEADS),
        in_specs=[qk, qk, v, kc, vc, v, pl.BlockSpec(memory_space=pltpu.SMEM), pl.BlockSpec((1, 128), lambda b, h: (0, h))],
        out_specs=[v, v],
        out_shape=[jax.ShapeDtypeStruct((t_lat, HEADS * RET_DV), BF16), jax.ShapeDtypeStruct((t_lat, HEADS * RET_DV), F32)],
        sem=("parallel", "arbitrary"), rider=rider)


def _ret_bwd(rq, rk, rv, pg, osum, dmix, lg, g_ret, *, t_lat, seq, ctx_len, chunk, rider=None):
    nex = t_lat // seq
    n_chunk = seq // chunk
    t_ctx = nex * ctx_len

    def body(q_ref, k_ref, v_ref, kc_ref, vc_ref, rg_ref, o_ref, dy_ref, lg_ref, g_ref,
             dq_ref, dk_ref, dv_ref, dkc_ref, dvc_ref, drg_ref, st_ref, do_s, s_st):
        h = pl.program_id(1)
        hm = _head_mask(h)
        gain = g_ref[...]
        kcm = jnp.where(hm, kc_ref[...].astype(F32), 0.0)
        vcb = vc_ref[...]

        def norm_step(n, dgain):
            sl = pl.ds(pl.multiple_of(n * chunk, chunk), chunk)
            o = o_ref[sl, :]
            mu = jnp.mean(o, axis=-1, keepdims=True)
            oc = o - mu
            rstd = lax.rsqrt(jnp.mean(oc * oc, axis=-1, keepdims=True) + EPS)
            ohat = oc * rstd
            rg = rg_ref[sl, :]
            sg = 1.0 / (1.0 + jnp.exp(-rg))
            dy = dy_ref[sl, :].astype(F32)
            don = dy * (rg * sg)
            drg_ref[sl, :] = dy * (ohat * gain) * (sg * (1.0 + rg * (1.0 - sg)))
            dohat = don * gain
            do_s[sl, :] = rstd * (dohat - jnp.mean(dohat, axis=-1, keepdims=True)
                                  - ohat * jnp.mean(dohat * ohat, axis=-1, keepdims=True))
            return dgain + jnp.sum(don * ohat, axis=0, keepdims=True)

        dgain = lax.fori_loop(0, n_chunk, norm_step, jnp.zeros((1, 128), F32))

        @pl.when(h % 2 == 0)
        def _():
            dq_ref[...] = jnp.zeros(dq_ref.shape, F32)
            dk_ref[...] = jnp.zeros(dk_ref.shape, F32)
            dkc_ref[...] = jnp.zeros(dkc_ref.shape, F32)

        dv_ref[...] = jnp.zeros(dv_ref.shape, F32)

        def run(forward):
            lgd = lg_ref[0 if forward else 1, h]
            dmat, dist, wq, wk, e_q, e_k, cd = _decay_terms(lgd, chunk, forward)
            wc, e_c = _ctx_weights(lgd, ctx_len, forward)
            s0 = _dot_tn((kcm * wc).astype(BF16), vcb)

            def state_step(t, s):
                n = t if forward else n_chunk - 1 - t
                sl = pl.ds(pl.multiple_of(n * chunk, chunk), chunk)
                s_st[n] = s
                kf = jnp.where(hm, k_ref[sl, :].astype(F32), 0.0)
                return cd * s + _dot_tn((kf * wk).astype(BF16), v_ref[sl, :])

            lax.fori_loop(0, n_chunk, state_step, s0)

            def grad_step(t, carry):
                g_next, dlg = carry
                n = (n_chunk - 1 - t) if forward else t
                sl = pl.ds(pl.multiple_of(n * chunk, chunk), chunk)
                qm = jnp.where(hm, q_ref[sl, :], jnp.zeros((), BF16))
                kf = jnp.where(hm, k_ref[sl, :].astype(F32), 0.0)
                kb = kf.astype(BF16)
                vb = v_ref[sl, :]
                do = do_s[sl, :]
                dob = do.astype(BF16)
                s_n = s_st[n]
                s_nb = s_n.astype(BF16)
                gb = g_next.astype(BF16)
                dk_cross = wk * _dot_nt(vb, gb)
                dv_cross = _dot((kf * wk).astype(BF16), gb)
                a = _dot_nt(qm, kb) * dmat
                da_raw = _dot_nt(dob, vb)
                dab = (da_raw * dmat).astype(BF16)
                ab = a.astype(BF16)
                o_cross = wq * _dot(qm, s_nb)
                dq_ref[sl, :] += _dot(dab, kb) + wq * _dot_nt(dob, s_nb)
                dk_ref[sl, :] += _dot_tn(dab, qm) + dk_cross
                dv_ref[sl, :] += _dot_tn(ab, dob) + dv_cross
                dlg = (dlg + chunk * cd * jnp.sum(g_next * s_n, keepdims=True)
                       + jnp.sum(e_k * jnp.sum(kf * dk_cross, axis=-1, keepdims=True), keepdims=True)
                       + jnp.sum(dist * a * da_raw, keepdims=True)
                       + jnp.sum(e_q * jnp.sum(o_cross * do, axis=-1, keepdims=True), keepdims=True))
                g_new = cd * g_next + _dot_tn((qm.astype(F32) * wq).astype(BF16), dob)
                return g_new, dlg

            ds0, dlg = lax.fori_loop(0, n_chunk, grad_step, (jnp.zeros((128, 128), F32), jnp.zeros((1, 1), F32)))
            ds0b = ds0.astype(BF16)
            dkc_part = wc * _dot_nt(vcb, ds0b)
            dkc_ref[...] += dkc_part
            dvc_part = _dot((kcm * wc).astype(BF16), ds0b)
            dlg = dlg + jnp.sum(e_c * jnp.sum(kcm * dkc_part, axis=-1, keepdims=True), keepdims=True)
            return dvc_part, dlg

        dvc_f, dlg_f = run(True)
        dvc_b, dlg_b = run(False)
        dvc_ref[...] = dvc_f + dvc_b
        st_ref[0] = jnp.concatenate([
            dgain, jnp.broadcast_to(dlg_f, (1, 128)), jnp.broadcast_to(dlg_b, (1, 128)), jnp.zeros((5, 128), F32)], axis=0)

    qk, v, kc, vc = _ret_specs(t_lat, seq, ctx_len)
    dy_spec = v
    return _hosted_call(
        body, (rq, rk, rv, rk, rv, pg, osum, dmix, lg, g_ret), name="ret_bwd", grid=(nex, HEADS),
        in_specs=[qk, qk, v, kc, vc, v, v, dy_spec, pl.BlockSpec(memory_space=pltpu.SMEM),
                  pl.BlockSpec((1, 128), lambda b, h: (0, h))],
        out_specs=[
            qk, qk, v,
            pl.BlockSpec((ctx_len, 128), lambda b, h: (b, h // 2)),
            pl.BlockSpec((ctx_len, 128), lambda b, h: (b, h)),
            v,
            pl.BlockSpec((1, 8, 128), lambda b, h: (b, 0, h)),
        ],
        out_shape=[
            jax.ShapeDtypeStruct((t_lat, 256), F32), jax.ShapeDtypeStruct((t_lat, 256), F32),
            jax.ShapeDtypeStruct((t_lat, 512), F32), jax.ShapeDtypeStruct((t_ctx, 256), F32),
            jax.ShapeDtypeStruct((t_ctx, 512), F32), jax.ShapeDtypeStruct((t_lat, 512), F32),
            jax.ShapeDtypeStruct((nex, 8, 512), F32),
        ],
        scratch_shapes=[pltpu.VMEM((seq, 128), F32), pltpu.VMEM((n_chunk, 128, 128), F32)],
        sem=("parallel", "arbitrary"), rider=rider)


def _matmul_tn(a, b, *, bm, bn, bk, chip_major, name, out_dtype=F32, rider=None):
    tk, m = a.shape
    n = b.shape[1]
    slab = n // N_CHIPS
    per_block = bn // slab if chip_major else 1
    bk = max(c for c in range(LANES, min(bk, tk) + 1, LANES) if tk % c == 0)
    nk = tk // bk
    blk = (per_block, bm, slab) if chip_major else (bm, bn)

    def body(a_ref, b_ref, o_ref, acc_ref):
        k = pl.program_id(2)
        if chip_major:
            parts = [_dot_tn(a_ref[...], b_ref[:, s * slab:(s + 1) * slab]) for s in range(per_block)]
        else:
            parts = [_dot_tn(a_ref[...], b_ref[...])]

        @pl.when(k == 0)
        def _():
            for s, part in enumerate(parts):
                if chip_major:
                    acc_ref[s] = part
                else:
                    acc_ref[...] = part

        @pl.when(k > 0)
        def _():
            for s, part in enumerate(parts):
                if chip_major:
                    acc_ref[s] += part
                else:
                    acc_ref[...] += part

        @pl.when(k == nk - 1)
        def _():
            o_ref[...] = acc_ref[...].astype(out_dtype)

    if chip_major:
        out_spec = pl.BlockSpec(blk, lambda i, j, k: (j, i, 0))
        out_shape = jax.ShapeDtypeStruct((N_CHIPS, m, slab), out_dtype)
    else:
        out_spec = pl.BlockSpec(blk, lambda i, j, k: (i, j))
        out_shape = jax.ShapeDtypeStruct((m, n), out_dtype)
    (out,), carried = _hosted_call(
        body, (a, b), name=name, grid=(m // bm, n // bn, nk),
        in_specs=[pl.BlockSpec((bk, bm), lambda i, j, k: (k, i)), pl.BlockSpec((bk, bn), lambda i, j, k: (k, j))],
        out_specs=[out_spec], out_shape=[out_shape], scratch_shapes=[pltpu.VMEM(blk, F32)],
        sem=("parallel", "parallel", "arbitrary"), rider=rider)
    return out if rider is None else (out, carried)


_LATE = ("w_out", "w_ff1", "w_ff2")
_EARLY = ("w_in", "w_uq", "w_ukv")


def _local_step(x, ctx, tgt, modv, lg, g_attn, g_ffn, g_fin, g_ret, g_q, g_kv, w_in, w_uq, w_ukv, late, place=None,
                *, tm=256, tq=256, chunk=256):
    nex, seq, d = x.shape
    ctx_len = ctx.shape[1]
    t_lat = nex * seq
    x2 = x.reshape(t_lat, d)
    ctx2 = ctx.reshape(nex * ctx_len, d)
    tgt2 = tgt.reshape(t_lat, d)
    cos_t, sin_t = _rope_tables(seq, tm)
    dims = dict(t_lat=t_lat, seq=seq, ctx_len=ctx_len)
    alone = place is None

    (hb, pg, rq, rk, rv, nq, nkv, q, k, v), crossed = _pre_fwd(
        x2, ctx2, modv, g_attn, w_in, g_q, g_kv, w_uq, w_ukv, cos_t, sin_t, seq=seq, tm=tm,
        rider=None if alone else _gather_ici_rider([late[2]]))
    (yret, osum), got_ff2 = _ret_fwd_pair(rq, rk, rv, pg, lg, g_ret, chunk=chunk, **dims,
                                          rider=None if alone else _gather_d2d_rider(crossed))
    (ymla, lse), got_rest = _mla_fwd(q, k, v, tq=tq, **dims,
                                     rider=None if alone else _gather_rider([late[0], late[1]], staged=True))
    w_out, w_ff1, w_ff2 = late if alone else got_rest + got_ff2
    mix, act, du, h2, df, dmo, dmix, dxm, st_post = _post(yret, ymla, x2, tgt2, modv, g_ffn, g_fin, w_out.reshape(d, d),
                                                         w_ff1, w_ff2.reshape(D_FF, d), seq=seq, tm=tm)
    kw = dict(bm=1024, bn=1024, bk=1024, out_dtype=BF16)
    g_ff2 = _matmul_tn(act, df, chip_major=False, name="gw_ff2", **kw).reshape(N_CHIPS, D_FF // N_CHIPS, d)
    if alone:
        g_ff1 = _matmul_tn(h2, du, chip_major=True, name="gw_ff1", **kw)
        g_out = _matmul_tn(mix, dmo, chip_major=False, name="gw_out", **kw).reshape(N_CHIPS, d // N_CHIPS, d)
        (dq_m, dkl, dkc, dvl, dvc), _ = _mla_bwd(q, k, v, ymla, lse, dmix, tq=tq, **dims)
        (drq, drk, drv, dkc_r, dvc_r, drg, st_ret), _ = _ret_bwd_pair(rq, rk, rv, pg, osum, dmix, lg, g_ret, chunk=chunk,
                                                                      **dims)
        late_out = [g_out, g_ff1, g_ff2]
    else:
        core, slot = place
        g_ff1, x_ff2 = _matmul_tn(h2, du, chip_major=True, name="gw_ff1", rider=_exchange_rider([g_ff2]), **kw)
        g_out, x_ff1 = _matmul_tn(mix, dmo, chip_major=False, name="gw_out", rider=_exchange_rider([g_ff1]), **kw)
        g_out = g_out.reshape(N_CHIPS, d // N_CHIPS, d)
        p_ff2 = _add_half(g_ff2, x_ff2[0], core, "add_half_w_ff2")
        p_ff1 = _add_half(g_ff1, x_ff1[0], core, "add_half_w_ff1")
        (dq_m, dkl, dkc, dvl, dvc), (l_ff2, l_ff1, x_out) = _mla_bwd(
            q, k, v, ymla, lse, dmix, tq=tq, **dims,
            rider=_merge_riders(_scatter_rider([p_ff2, p_ff1]), _exchange_rider([g_out])))
        p_out = _add_half(g_out, x_out, core, "add_half_w_out")
        m_ff2 = _sum_chips(p_ff2, l_ff2, slot, "sum_chips_w_ff2")
        m_ff1 = _sum_chips(p_ff1, l_ff1, slot, "sum_chips_w_ff1")
        (drq, drk, drv, dkc_r, dvc_r, drg, st_ret), (l_out,) = _ret_bwd_pair(
            rq, rk, rv, pg, osum, dmix, lg, g_ret, chunk=chunk, **dims, rider=_scatter_rider([p_out]))
        late_out = [_sum_chips(p_out, l_out, slot, "sum_chips_w_out"), m_ff1, m_ff2]
    (dpb, dqf, dkvf, gx, st_pre), _ = _pre_bwd(
        x2, ctx2, modv, g_attn, pg, drq, drk, dkc_r, drv, dvc_r, drg, dq_m, dkl, dkc, dvl, dvc, dxm, w_in, g_q, g_kv,
        w_uq, w_ukv, cos_t, sin_t, seq=seq, tm=tm)
    g_early = [
        _matmul_tn(dpb, hb, bm=IN_PAD // 2, bn=d, bk=512, chip_major=False, name="gw_in"),
        _matmul_tn(dqf, nq, bm=HEADS * MLA_HEAD, bn=Q_LORA, bk=1536, chip_major=False, name="gw_uq"),
        _matmul_tn(nkv, dkvf, bm=KV_LORA, bn=HEADS * 256, bk=1536, chip_major=True, name="gw_ukv"),
    ]
    return gx.reshape(nex, seq, d), g_early, late_out, st_post, st_ret, st_pre


_ANY = pl.BlockSpec(memory_space=pl.ANY)
_VMEM = pl.BlockSpec(memory_space=pltpu.VMEM)
_OFFSETS = tuple((dx, dy, dc) for dx in (0, 1) for dy in (0, 1) for dc in (0, 1))[1:]
_CHIP_OFFSETS = ((1, 0), (0, 1), (1, 1))


def _place():
    return lax.axis_index("x"), lax.axis_index("y"), lax.axis_index("c")


def _flip(v, d):
    return 1 - v if d else v


def _gather8_rider(a, in_vmem=True):
    def copies(a_ref, o_ref, send, recv):
        x, y, z = _place()
        me = 4 * x + 2 * y + z
        out = []
        for k, (dx, dy, dc) in enumerate(_OFFSETS):
            peer = (_flip(x, dx), _flip(y, dy), _flip(z, dc))
            landing = o_ref.at[4 * peer[0] + 2 * peer[1] + peer[2]]
            out.append((
                pltpu.make_async_remote_copy(src_ref=a_ref, dst_ref=o_ref.at[me], send_sem=send.at[k],
                                             recv_sem=recv.at[k], device_id=peer, device_id_type=MESH),
                pltpu.make_async_remote_copy(src_ref=a_ref, dst_ref=landing, send_sem=send.at[k],
                                             recv_sem=recv.at[k], device_id=peer, device_id_type=MESH)))
        return me, out

    def start(ins, outs, sems):
        me, cps = copies(ins[0], outs[0], sems[0], sems[1])
        pltpu.make_async_copy(ins[0], outs[0].at[me], sems[2]).start()
        for out_cp, _ in cps:
            out_cp.start()

    def finish(ins, outs, sems):
        me, cps = copies(ins[0], outs[0], sems[0], sems[1])
        for out_cp, in_cp in cps:
            in_cp.wait_recv()
            out_cp.wait_send()
        pltpu.make_async_copy(ins[0], outs[0].at[me], sems[2]).wait()

    spec = [_VMEM] if in_vmem else [_ANY]
    return _Rider([a], [jax.ShapeDtypeStruct((N_DEV,) + a.shape, a.dtype)],
                  [pltpu.SemaphoreType.DMA((7,)), pltpu.SemaphoreType.DMA((7,)), pltpu.SemaphoreType.DMA],
                  start, finish, in_specs=spec, out_specs=spec)


def _merge_riders(*riders):
    ins, outs, sems, in_specs, out_specs, aliases, cuts = [], [], [], [], [], {}, []
    for r in riders:
        cuts.append((len(ins), len(outs), len(sems)))
        aliases.update({len(ins) + i: len(outs) + j for i, j in r.aliases.items()})
        ins += r.ins
        outs += r.out_shapes
        sems += r.sems
        in_specs += r.in_specs
        out_specs += r.out_specs

    def part(r, cut, r_ins, r_outs, r_sems):
        return (r_ins[cut[0]:cut[0] + len(r.ins)], r_outs[cut[1]:cut[1] + len(r.out_shapes)],
                r_sems[cut[2]:cut[2] + len(r.sems)])

    def start(r_ins, r_outs, r_sems):
        for r, cut in zip(riders, cuts):
            r.start(*part(r, cut, r_ins, r_outs, r_sems))

    def finish(r_ins, r_outs, r_sems):
        for r, cut in zip(riders, cuts):
            r.finish(*part(r, cut, r_ins, r_outs, r_sems))

    def middle(r_ins, r_outs, r_sems):
        for r, cut in zip(riders, cuts):
            if r.middle is not None:
                r.middle(*part(r, cut, r_ins, r_outs, r_sems))

    return _Rider(ins, outs, sems, start, finish, aliases=aliases, in_specs=in_specs, out_specs=out_specs,
                  middle=middle if any(r.middle is not None for r in riders) else None)


def _allgather8(a, name):
    return _run_rider(_gather8_rider(a), name)[0]


BF16_TILE_ROWS = 16


def _half(o, slot, which):
    r2 = o.shape[1] // 2
    if r2 % BF16_TILE_ROWS == 0:
        return o.at[slot, pl.ds(which * r2, r2)]
    c2 = o.shape[2] // 2
    assert c2 % LANES == 0
    return o.at[slot, :, pl.ds(which * c2, c2)]


def _gather_send(o_refs, send, recv):
    x, y, z = _place()
    chip = 2 * x + y
    for a, o in enumerate(o_refs):
        r2 = o.shape[1] // 2
        mine = _half(o, chip, z)
        for k, (dx, dy) in enumerate(_CHIP_OFFSETS):
            pltpu.make_async_remote_copy(
                src_ref=mine, dst_ref=mine, send_sem=send.at[a, k], recv_sem=recv.at[a, k],
                device_id=(_flip(x, dx), _flip(y, dy), z), device_id_type=MESH).start()


def _gather_landed(o_refs, send, recv, then=None):
    x, y, z = _place()
    chip = 2 * x + y
    for a, o in enumerate(o_refs):
        for k, (dx, dy) in enumerate(_CHIP_OFFSETS):
            landed = _half(o, 2 * _flip(x, dx) + _flip(y, dy), z)
            pltpu.make_async_remote_copy(
                src_ref=landed, dst_ref=landed, send_sem=send.at[a, k], recv_sem=recv.at[a, k],
                device_id=(_flip(x, dx), _flip(y, dy), z), device_id_type=MESH).wait_recv()
            if then is not None:
                then(a, k, landed)
    for a, o in enumerate(o_refs):
        mine = _half(o, chip, z)
        for k, (dx, dy) in enumerate(_CHIP_OFFSETS):
            pltpu.make_async_remote_copy(
                src_ref=mine, dst_ref=mine, send_sem=send.at[a, k], recv_sem=recv.at[a, k],
                device_id=(_flip(x, dx), _flip(y, dy), z), device_id_type=MESH).wait_send()


def _pass_on(o_refs, fsend, frecv, a, k, landed):
    x, y, z = _place()
    pltpu.make_async_remote_copy(
        src_ref=landed, dst_ref=landed, send_sem=fsend.at[a, k], recv_sem=frecv.at[a, k],
        device_id=(x, y, 1 - z), device_id_type=MESH).start()


def _passed_on(o_refs, fsend, frecv):
    x, y, z = _place()
    for a, o in enumerate(o_refs):
        for k, (dx, dy) in enumerate(_CHIP_OFFSETS):
            other = 2 * _flip(x, dx) + _flip(y, dy)
            got = _half(o, other, 1 - z)
            gave = _half(o, other, z)
            pltpu.make_async_remote_copy(
                src_ref=got, dst_ref=got, send_sem=fsend.at[a, k], recv_sem=frecv.at[a, k],
                device_id=(x, y, 1 - z), device_id_type=MESH).wait_recv()
            pltpu.make_async_remote_copy(
                src_ref=gave, dst_ref=gave, send_sem=fsend.at[a, k], recv_sem=frecv.at[a, k],
                device_id=(x, y, 1 - z), device_id_type=MESH).wait_send()


def _gather_finish(o_refs, send, recv, fsend, frecv):
    _gather_landed(o_refs, send, recv, functools.partial(_pass_on, o_refs, fsend, frecv))
    _passed_on(o_refs, fsend, frecv)


class _Rider:
    def __init__(self, ins, out_shapes, sems, start, finish, aliases=None, in_specs=None, out_specs=None, middle=None):
        self.ins, self.out_shapes, self.sems = list(ins), list(out_shapes), list(sems)
        self.start, self.finish, self.aliases = start, finish, dict(aliases or {})
        self.middle = middle
        self.in_specs = list(in_specs) if in_specs else [_ANY] * len(self.ins)
        self.out_specs = list(out_specs) if out_specs else [_ANY] * len(self.out_shapes)


def _run_rider(rider, name):
    r_in, r_out = len(rider.ins), len(rider.out_shapes)

    def body(*refs):
        ins, outs, sems = refs[:r_in], refs[r_in:r_in + r_out], refs[r_in + r_out:]
        rider.start(ins, outs, sems)
        if rider.middle is not None:
            rider.middle(ins, outs, sems)
        rider.finish(ins, outs, sems)

    return pl.pallas_call(
        body, name=name, in_specs=rider.in_specs, out_specs=rider.out_specs, out_shape=rider.out_shapes,
        input_output_aliases=rider.aliases, scratch_shapes=rider.sems,
    )(*rider.ins)


def _hosted_call(body, args, *, name, grid, in_specs, out_specs, out_shape, scratch_shapes=(), sem, rider=None,
                 prefetch=()):
    scratch_shapes = list(scratch_shapes)
    n_pf, n_in, n_out, n_sc = len(prefetch), len(in_specs), len(out_specs), len(scratch_shapes)
    r_in, r_out = (len(rider.ins), len(rider.out_shapes)) if rider else (0, 0)
    last = tuple(g - 1 for g in grid)

    def hosted(*refs):
        p = 0
        parts = []
        for cnt in (n_pf, n_in, r_in, n_out, r_out, n_sc):
            parts.append(refs[p:p + cnt])
            p += cnt
        pf, ins, r_ins, outs, r_outs, scratch = parts
        sems = refs[p:]
        ids = [pl.program_id(a) for a in range(len(grid))]
        is_first = functools.reduce(jnp.logical_and, [i == 0 for i in ids])
        is_last = functools.reduce(jnp.logical_and, [i == e for i, e in zip(ids, last)])

        @pl.when(is_first)
        def _():
            rider.start(r_ins, r_outs, sems)

        if rider.middle is not None:
            linear = functools.reduce(lambda acc, ig: acc * ig[1] + ig[0], zip(ids, grid), 0)

            @pl.when(linear == math.prod(grid) * 3 // 4)
            def _():
                rider.middle(r_ins, r_outs, sems)

        body(*pf, *ins, *outs, *scratch)

        @pl.when(is_last)
        def _():
            rider.finish(r_ins, r_outs, sems)

    if rider is None:
        kern, all_in, all_out, shapes, scratch, aliases, extra = body, list(in_specs), list(out_specs), list(out_shape), \
            scratch_shapes, {}, []
    else:
        kern, all_in, all_out = hosted, list(in_specs) + rider.in_specs, list(out_specs) + rider.out_specs
        shapes, scratch, extra = list(out_shape) + rider.out_shapes, scratch_shapes + rider.sems, rider.ins
        aliases = {n_pf + n_in + i: n_out + j for i, j in rider.aliases.items()}
        sem = ("arbitrary",) * len(grid)
    if prefetch:
        spec = dict(grid_spec=pltpu.PrefetchScalarGridSpec(
            num_scalar_prefetch=n_pf, grid=grid, in_specs=all_in, out_specs=all_out, scratch_shapes=scratch))
    else:
        spec = dict(grid=grid, in_specs=all_in, out_specs=all_out, scratch_shapes=scratch)
    res = pl.pallas_call(kern, name=name, out_shape=shapes, input_output_aliases=aliases,
                         compiler_params=_params(sem, VMEM_LIMIT), **spec)(*prefetch, *args, *extra)
    return list(res[:n_out]), list(res[n_out:])


def _gather_rider(ws, staged=False):
    n = len(ws)
    shapes = [jax.ShapeDtypeStruct(w.shape, w.dtype) for w in ws]
    sems = [pltpu.SemaphoreType.DMA((n, 3))] * 4
    aliases = {a: a for a in range(n)}

    def start(ins, outs, s):
        _gather_send(outs, s[0], s[1])

    if not staged:
        return _Rider(ws, shapes, sems, start, lambda ins, outs, s: _gather_finish(outs, *s), aliases=aliases)
    return _Rider(
        ws, shapes, sems, start, lambda ins, outs, s: _passed_on(outs, s[2], s[3]), aliases=aliases,
        middle=lambda ins, outs, s: _gather_landed(outs, s[0], s[1], functools.partial(_pass_on, outs, s[2], s[3])))


def _gather_ici_rider(ws):
    n = len(ws)
    return _Rider(
        ws, [jax.ShapeDtypeStruct(w.shape, w.dtype) for w in ws], [pltpu.SemaphoreType.DMA((n, 3))] * 2,
        lambda ins, outs, sems: _gather_send(outs, sems[0], sems[1]),
        lambda ins, outs, sems: _gather_landed(outs, sems[0], sems[1]),
        aliases={a: a for a in range(n)})


def _gather_d2d_rider(ws):
    n = len(ws)

    def start(ins, outs, sems):
        x, y, z = _place()
        for a, o in enumerate(outs):
            for k, (dx, dy) in enumerate(_CHIP_OFFSETS):
                _pass_on(outs, sems[0], sems[1], a, k, _half(o, 2 * _flip(x, dx) + _flip(y, dy), z))

    return _Rider(
        ws, [jax.ShapeDtypeStruct(w.shape, w.dtype) for w in ws], [pltpu.SemaphoreType.DMA((n, 3))] * 2,
        start, lambda ins, outs, sems: _passed_on(outs, sems[0], sems[1]), aliases={a: a for a in range(n)})


def _copies_rider(ins, out_shapes, sem_shape, make):
    def start(r_ins, r_outs, sems):
        for cp in make(r_ins, r_outs, sems[0], sems[1]):
            cp.start()

    def finish(r_ins, r_outs, sems):
        for cp in make(r_ins, r_outs, sems[0], sems[1]):
            cp.wait()

    return _Rider(ins, out_shapes, [pltpu.SemaphoreType.DMA(sem_shape)] * 2, start, finish)


def _exchange_rider(gs):
    def make(g_refs, r_refs, send, recv):
        x, y, z = _place()
        return [pltpu.make_async_remote_copy(
            src_ref=g.at[:, pl.ds((1 - z) * (g.shape[1] // 2), g.shape[1] // 2)], dst_ref=r, send_sem=send.at[a],
            recv_sem=recv.at[a], device_id=(x, y, 1 - z), device_id_type=MESH)
            for a, (g, r) in enumerate(zip(g_refs, r_refs))]

    shapes = [jax.ShapeDtypeStruct((g.shape[0], g.shape[1] // 2, g.shape[2]), g.dtype) for g in gs]
    return _copies_rider(gs, shapes, (len(gs),), make)


def _add_half(g, recv, core, name):
    s, r, c = g.shape
    r2 = r // 2
    rb = r2
    for cand in (256, 128, 64):
        if r2 % cand == 0:
            rb = cand
            break
    g4 = g.reshape(s, 2, r2, c)

    def body(core_ref, g_ref, r_ref, o_ref):
        o_ref[...] = (g_ref[...].astype(F32) + r_ref[...].astype(F32)).astype(BF16)

    return pl.pallas_call(
        body, name=name,
        grid_spec=pltpu.PrefetchScalarGridSpec(
            num_scalar_prefetch=1, grid=(s, r2 // rb),
            in_specs=[pl.BlockSpec((None, None, rb, c), lambda i, j, cr: (i, cr[0], j, 0)),
                      pl.BlockSpec((None, rb, c), lambda i, j, cr: (i, j, 0))],
            out_specs=pl.BlockSpec((None, rb, c), lambda i, j, cr: (i, j, 0))),
        out_shape=jax.ShapeDtypeStruct((s, r2, c), BF16),
        compiler_params=_params(("parallel", "parallel")),
    )(core, g4, recv)


def _scatter_rider(ps):
    def make(p_refs, o_refs, send, recv):
        x, y, z = _place()
        copies = []
        for a, (p, o) in enumerate(zip(p_refs, o_refs)):
            for k, (dx, dy) in enumerate(_CHIP_OFFSETS):
                other = 2 * _flip(x, dx) + _flip(y, dy)
                copies.append(pltpu.make_async_remote_copy(
                    src_ref=p.at[other], dst_ref=o.at[k], send_sem=send.at[a, k], recv_sem=recv.at[a, k],
                    device_id=(_flip(x, dx), _flip(y, dy), z), device_id_type=MESH))
        return copies

    shapes = [jax.ShapeDtypeStruct((3,) + p.shape[1:], p.dtype) for p in ps]
    return _copies_rider(ps, shapes, (len(ps), 3), make)


def _sum_chips(p, landed, chip, name):
    _, r2, c = p.shape
    rb = r2
    for cand in (256, 128, 64):
        if r2 % cand == 0:
            rb = cand
            break

    def body(s_ref, p_ref, l_ref, o_ref):
        acc = p_ref[...].astype(F32)
        for k in range(3):
            acc = acc + l_ref[k].astype(F32)
        o_ref[...] = acc

    return pl.pallas_call(
        body, name=name,
        grid_spec=pltpu.PrefetchScalarGridSpec(
            num_scalar_prefetch=1, grid=(r2 // rb,),
            in_specs=[pl.BlockSpec((None, rb, c), lambda i, s: (s[0], i, 0)),
                      pl.BlockSpec((3, rb, c), lambda i, s: (0, i, 0))],
            out_specs=pl.BlockSpec((rb, c), lambda i, s: (i, 0))),
        out_shape=jax.ShapeDtypeStruct((r2, c), F32),
        compiler_params=_params(("parallel",)),
    )(chip, p, landed)


def _swap_rider(hs):
    def make(h_refs, o_refs, send, recv):
        x, y, z = _place()
        return [pltpu.make_async_remote_copy(
            src_ref=h, dst_ref=o, send_sem=send.at[a], recv_sem=recv.at[a], device_id=(x, y, 1 - z),
            device_id_type=MESH) for a, (h, o) in enumerate(zip(h_refs, o_refs))]

    return _copies_rider(hs, [jax.ShapeDtypeStruct(h.shape, h.dtype) for h in hs], (len(hs),), make)


def _reduce_scatter_vmem(gs, rows, rider, name):
    n = len(gs)
    r_in, r_out = len(rider.ins), len(rider.out_shapes)
    halves = [(r // 2, g.shape[-1]) for g, (r, _) in zip(gs, rows)]

    def body(*refs):
        p = 0
        parts = []
        for cnt in (n, r_in, n, n, r_out, n, n, n, 6):
            parts.append(refs[p:p + cnt])
            p += cnt
        g_refs, r_ins, mine, theirs, r_outs, recv, part, land, sems = parts
        r_sems = refs[p:]
        xs, xr, ss, sr, ws, wr = sems
        x, y, z = _place()
        chip = 2 * x + y
        sib = (x, y, 1 - z)
        rider.start(r_ins, r_outs, r_sems)

        def half_of(a, s, which):
            r2 = halves[a][0]
            if len(g_refs[a].shape) == 3:
                return g_refs[a].at[s, pl.ds(pl.multiple_of(which * r2, 8), r2)]
            return g_refs[a].at[pl.ds(pl.multiple_of(s * rows[a][1] + which * r2, 8), r2)]

        exchange = [pltpu.make_async_remote_copy(
            src_ref=half_of(a, s, 1 - z), dst_ref=recv[a].at[s], send_sem=xs.at[a, s], recv_sem=xr.at[a, s],
            device_id=sib, device_id_type=MESH) for a in range(n) for s in range(N_CHIPS)]
        for cp in exchange:
            cp.start()
        for cp in exchange:
            cp.wait()
        for a in range(n):
            for s in range(N_CHIPS):
                part[a][s] = (half_of(a, s, z)[...] + recv[a][s]).astype(BF16)
        scatter = []
        for a in range(n):
            for k, (dx, dy) in enumerate(_CHIP_OFFSETS):
                other = 2 * _flip(x, dx) + _flip(y, dy)
                scatter.append(pltpu.make_async_remote_copy(
                    src_ref=part[a].at[other], dst_ref=land[a].at[k], send_sem=ss.at[a, k], recv_sem=sr.at[a, k],
                    device_id=(_flip(x, dx), _flip(y, dy), z), device_id_type=MESH))
        for cp in scatter:
            cp.start()
        for cp in scatter:
            cp.wait()
        for a in range(n):
            acc = part[a][chip].astype(F32)
            for k in range(3):
                acc = acc + land[a][k].astype(F32)
            mine[a][...] = acc
        swap = [pltpu.make_async_remote_copy(
            src_ref=mine[a], dst_ref=theirs[a], send_sem=ws.at[a], recv_sem=wr.at[a], device_id=sib,
            device_id_type=MESH) for a in range(n)]
        for cp in swap:
            cp.start()
        for cp in swap:
            cp.wait()
        rider.finish(r_ins, r_outs, r_sems)

    half_shapes = [jax.ShapeDtypeStruct(h, F32) for h in halves]
    res = pl.pallas_call(
        body, name=name, in_specs=[_VMEM] * n + rider.in_specs, out_specs=[_VMEM] * (2 * n) + rider.out_specs,
        out_shape=half_shapes + half_shapes + rider.out_shapes,
        scratch_shapes=[pltpu.VMEM((N_CHIPS,) + h, F32) for h in halves] + [pltpu.VMEM((N_CHIPS,) + h, BF16) for h in halves]
        + [pltpu.VMEM((3,) + h, BF16) for h in halves]
        + [pltpu.SemaphoreType.DMA((n, N_CHIPS))] * 2 + [pltpu.SemaphoreType.DMA((n, 3))] * 2
        + [pltpu.SemaphoreType.DMA((n,))] * 2 + rider.sems,
        input_output_aliases={n + i: 2 * n + j for i, j in rider.aliases.items()},
        compiler_params=_params(None, VMEM_LIMIT),
    )(*gs, *rider.ins)
    return list(res[:n]), list(res[n:2 * n]), list(res[2 * n:])


SMALL_ROWS = 32
PACK_ROWS = 16


def _pack_small(st_post, st_ret, st_pre):
    d = st_post.shape[2]

    def body(po_ref, re_ref, pr_ref, o_ref):
        o_ref[...] = jnp.zeros(o_ref.shape, F32)
        o_ref[0:1, :] = pr_ref[0, 2:3, :] + pr_ref[1, 2:3, :] + pr_ref[2, 2:3, :]
        o_ref[1:2, :] = po_ref[0, 4:5, :] + po_ref[1, 4:5, :]
        o_ref[2:3, :] = po_ref[0, 5:6, :] + po_ref[1, 5:6, :]
        o_ref[3:4, 0:512] = re_ref[0, 0:1, :] + re_ref[1, 0:1, :]
        o_ref[4:5, :] = pr_ref[0, 3:4, :] + pr_ref[1, 3:4, :] + pr_ref[2, 3:4, :]
        o_ref[5:6, :] = pr_ref[0, 4:5, :] + pr_ref[1, 4:5, :] + pr_ref[2, 4:5, :]
        lane = lax.broadcasted_iota(jnp.int32, (1, LANES), 1)
        for row, src in ((6, 1), (10, 2)):
            acc = jnp.zeros((1, LANES), F32)
            for hd in range(HEADS):
                grp = re_ref[0, src:src + 1, hd * LANES:(hd + 1) * LANES] + re_ref[1, src:src + 1, hd * LANES:(hd + 1) * LANES]
                acc = acc + jnp.where(lane == hd, grp, 0.0)
            o_ref[row:row + 1, 0:LANES] = acc
        o_ref[7:8, :] = po_ref[0, 6:7, :] + po_ref[1, 6:7, :]
        o_ref[8:9, :] = pr_ref[2, 0:1, :]
        o_ref[9:10, :] = pr_ref[2, 1:2, :]
        for e in range(2):
            b = 12 + 6 * e
            o_ref[b:b + 1, :] = pr_ref[e, 0:1, :]
            o_ref[b + 1:b + 2, :] = pr_ref[e, 1:2, :]
            o_ref[b + 2:b + 3, :] = po_ref[e, 3:4, :]
            o_ref[b + 3:b + 4, :] = po_ref[e, 0:1, :]
            o_ref[b + 4:b + 5, :] = po_ref[e, 1:2, :]
            o_ref[b + 5:b + 6, :] = po_ref[e, 2:3, :]

    return pl.pallas_call(body, name="pack_small", out_shape=jax.ShapeDtypeStruct((SMALL_ROWS, d), F32))(st_post, st_ret, st_pre)


def _small_reduce(gathered):
    d = gathered.shape[2]

    def body(g_ref, o_ref):
        tot = g_ref[0, 0:PACK_ROWS, :]
        for dev in range(1, N_DEV):
            tot = tot + g_ref[dev, 0:PACK_ROWS, :]
        o_ref[0:PACK_ROWS, :] = tot
        for j in range(6):
            acc = g_ref[0, 12 + j:13 + j, :] + g_ref[0, 18 + j:19 + j, :]
            for dev in range(1, N_DEV):
                acc = acc + g_ref[dev, 12 + j:13 + j, :] + g_ref[dev, 18 + j:19 + j, :]
            if j < 2:
                acc = acc + o_ref[8 + j:9 + j, :]
            o_ref[PACK_ROWS + j:PACK_ROWS + j + 1, :] = acc
        o_ref[PACK_ROWS + 6:PACK_ROWS + 8, :] = jnp.zeros((2, d), F32)

    return pl.pallas_call(body, name="small_reduce", out_shape=jax.ShapeDtypeStruct((PACK_ROWS + 8, d), F32))(gathered)


_SMALL = (("g_attn", 0, 1024), ("g_ffn", 1, 1024), ("g_final", 2, 1024), ("g_ret", 3, 512), ("g_q_lora", 4, 384),
          ("g_kv_lora", 5, 256), ("ret_decay_fwd", 6, HEADS), ("ret_decay_bwd", 10, HEADS))
_SMALL_NAMES = tuple(s[0] for s in _SMALL) + ("c_ctx", "b_ada")


def _small_final(tot, dcc, sg8, ws, ms, vs):
    d = tot.shape[1]
    n = len(_SMALL_NAMES)

    def body(*refs):
        t_ref, dcc_ref, sg_ref = refs[0:3]
        w_refs, m_refs, v_refs = refs[3:3 + n], refs[3 + n:3 + 2 * n], refs[3 + 2 * n:3 + 3 * n]
        outs = refs[3 + 3 * n:]
        g_refs, d_refs, mo_refs, vo_refs = outs[0:n], outs[n:2 * n], outs[2 * n:3 * n], outs[3 * n:4 * n]
        l_ref = outs[4 * n]

        def update(i, g, sl=None):
            pick = (lambda r: r[...]) if sl is None else (lambda r: r[:, sl])
            dl, mn, vn = _adam_math(pick(w_refs[i]), g, pick(m_refs[i]), pick(v_refs[i]))
            if sl is None:
                g_refs[i][...], d_refs[i][...], mo_refs[i][...], vo_refs[i][...] = g, dl, mn, vn
            else:
                g_refs[i][:, sl], d_refs[i][:, sl], mo_refs[i][:, sl], vo_refs[i][:, sl] = g, dl, mn, vn

        for i, (name, row, width) in enumerate(_SMALL):
            g = t_ref[row:row + 1, 0:width]
            if name == "ret_decay_fwd":
                g = g * sg_ref[0:1, 0:width]
            elif name == "ret_decay_bwd":
                g = g * sg_ref[1:2, 0:width]
            update(i, g)
        i_cc, i_b = n - 2, n - 1
        cc = w_refs[i_cc][...]
        s = 1.0 / (1.0 + jnp.exp(-cc))
        dsilu = dcc_ref[0, 0:1, :] + dcc_ref[2, 0:1, :] + dcc_ref[4, 0:1, :] + dcc_ref[6, 0:1, :]
        update(i_cc, dsilu * (s * (1.0 + cc * (1.0 - s))))
        for j in range(6):
            update(i_b, t_ref[PACK_ROWS + j:PACK_ROWS + j + 1, :], pl.ds(j * d, d))
        l_ref[...] = jnp.broadcast_to((0.5 / d) * jnp.sum(t_ref[7:8, :], keepdims=True), l_ref.shape)

    shapes = [jax.ShapeDtypeStruct(a.shape, F32) for a in ws]
    outs = pl.pallas_call(
        body, name="small_final", out_shape=shapes * 4 + [jax.ShapeDtypeStruct((8, LANES), F32)],
    )(tot, dcc, sg8, *ws, *ms, *vs)
    return outs[0:n], outs[n:2 * n], outs[2 * n:3 * n], outs[3 * n:4 * n], outs[4 * n]


_WEIGHTS = ("c_ctx", "w_ada", "b_ada", "g_attn", "g_ffn", "w_in", "ret_decay_fwd", "ret_decay_bwd", "g_ret", "g_q_lora",
            "w_uq", "g_kv_lora", "w_ukv", "w_out", "w_ff1", "w_ff2", "g_final")
_BIG = ("w_in", "w_uq", "w_ukv", "w_out", "w_ff1", "w_ff2")
_TRANSPOSED = ("w_in", "w_uq")


def kernel(x, c, ctx, c_ctx, w_ada, b_ada, g_attn, g_ffn, w_in, ret_decay_fwd, ret_decay_bwd, g_ret, g_q_lora, w_uq, g_kv_lora, w_ukv, w_out, w_ff1, w_ff2, g_final, loss_target, m_c_ctx, m_w_ada, m_b_ada, m_g_attn, m_g_ffn, m_w_in, m_ret_decay_fwd, m_ret_decay_bwd, m_g_ret, m_g_q_lora, m_w_uq, m_g_kv_lora, m_w_ukv, m_w_out, m_w_ff1, m_w_ff2, m_g_final, v_c_ctx, v_w_ada, v_b_ada, v_g_attn, v_g_ffn, v_w_in, v_ret_decay_fwd, v_ret_decay_bwd, v_g_ret, v_g_q_lora, v_w_uq, v_g_kv_lora, v_w_ukv, v_w_out, v_w_ff1, v_w_ff2, v_g_final):
    w = dict(c_ctx=c_ctx, w_ada=w_ada, b_ada=b_ada, g_attn=g_attn, g_ffn=g_ffn, w_in=w_in, ret_decay_fwd=ret_decay_fwd,
             ret_decay_bwd=ret_decay_bwd, g_ret=g_ret, g_q_lora=g_q_lora, w_uq=w_uq, g_kv_lora=g_kv_lora, w_ukv=w_ukv,
             w_out=w_out, w_ff1=w_ff1, w_ff2=w_ff2, g_final=g_final)
    m = dict(c_ctx=m_c_ctx, w_ada=m_w_ada, b_ada=m_b_ada, g_attn=m_g_attn, g_ffn=m_g_ffn, w_in=m_w_in,
             ret_decay_fwd=m_ret_decay_fwd, ret_decay_bwd=m_ret_decay_bwd, g_ret=m_g_ret, g_q_lora=m_g_q_lora, w_uq=m_w_uq,
             g_kv_lora=m_g_kv_lora, w_ukv=m_w_ukv, w_out=m_w_out, w_ff1=m_w_ff1, w_ff2=m_w_ff2, g_final=m_g_final)
    v = dict(c_ctx=v_c_ctx, w_ada=v_w_ada, b_ada=v_b_ada, g_attn=v_g_attn, g_ffn=v_g_ffn, w_in=v_w_in,
             ret_decay_fwd=v_ret_decay_fwd, ret_decay_bwd=v_ret_decay_bwd, g_ret=v_g_ret, g_q_lora=v_g_q_lora, w_uq=v_w_uq,
             g_kv_lora=v_g_kv_lora, w_ukv=v_w_ukv, w_out=v_w_out, w_ff1=v_w_ff1, w_ff2=v_w_ff2, g_final=v_g_final)
    xi, yi, ci = lax.axis_index("x"), lax.axis_index("y"), lax.axis_index("c")
    chip = 2 * xi + yi
    dev = 2 * chip + ci
    nex, seq, d = x.shape
    n_ada = w_ada.shape[2]

    dec = jnp.zeros((8, LANES), F32).at[0, :HEADS].set(ret_decay_fwd[0]).at[1, :HEADS].set(ret_decay_bwd[0])
    lg8, sg8 = _decay_prep(dec)
    lg = lg8[:2, :HEADS]

    def shard_of(t, k):
        return t[k][0].T if k in _TRANSPOSED else t[k][0]

    shard = {k: shard_of(w, k) for k in _BIG}
    head_rows = MLA_NOPE + MLA_ROPE
    shard["w_uq"] = jnp.pad(shard["w_uq"], ((0, MLA_HEAD - head_rows), (0, 0)))
    slot = chip.reshape(1).astype(jnp.int32)
    core = ci.reshape(1).astype(jnp.int32)
    slots = {k: _cast_into_slot(shard[k], slot, "cast_" + k)[0] for k in _BIG if k != "w_ff1"}
    slots["w_ff1"], (w_in_x, w_uq_x, w_ukv_x, c8) = _cast_into_slot(
        shard["w_ff1"], slot, "cast_w_ff1",
        rider=_merge_riders(_gather_ici_rider([slots[k] for k in _EARLY]),
                            _gather8_rider(jnp.pad(c, ((0, 8 - nex), (0, 0))), in_vmem=False)))

    a_in = jnp.concatenate([c8[:, :nex].reshape(N_DEV * nex, d), c_ctx.reshape(1, d), jnp.zeros((7, d), F32)], axis=0)
    b_sh = lax.dynamic_slice(b_ada, (0, chip * n_ada), (1, n_ada))
    mod_sh = _mod_fwd(a_in, w_ada[0], b_sh)
    mod8, w_in_f, w_uq_k, w_ukv_k = _run_rider(
        _merge_riders(_gather8_rider(mod_sh), _gather_d2d_rider([w_in_x, w_uq_x, w_ukv_x])), "ag_early")
    w_in_k = jnp.pad(w_in_f.reshape(IN_COLS, d), ((0, IN_PAD - IN_COLS), (0, 0)))
    mod_all = mod8[0::2].transpose(1, 0, 2).reshape(a_in.shape[0], N_CHIPS * n_ada)
    mod_me = lax.dynamic_slice(mod_all, (nex * dev, 0), (nex, N_CHIPS * n_ada)).reshape(nex, 6, d)
    mod_c = mod_all[N_DEV * nex].reshape(1, 6, d)
    modv = jnp.pad(jnp.concatenate([mod_me, mod_c], axis=0), ((0, 0), (0, 2), (0, 0)))

    gx, g_early, late, st_post, st_ret, st_pre = _local_step(
        x, ctx, loss_target, modv, lg, g_attn, g_ffn, g_final.reshape(1, d), g_ret, g_q_lora, g_kv_lora,
        w_in_k, w_uq_k, w_ukv_k, [slots[k] for k in _LATE], (core, slot))

    mine, theirs, (*late_theirs, gathered) = _reduce_scatter_vmem(
        g_early, [(IN_COLS // N_CHIPS, IN_COLS // N_CHIPS), (head_rows, MLA_HEAD), (KV_LORA, KV_LORA)],
        _merge_riders(_swap_rider(late), _gather8_rider(_pack_small(st_post, st_ret, st_pre))), "rs_early")
    tot = _small_reduce(gathered)
    dm = jnp.concatenate([
        gathered[:, 12:24].reshape(N_DEV * nex, 6 * d),
        jnp.concatenate([tot[8:10].reshape(1, 2 * d), jnp.zeros((1, 4 * d), F32)], axis=1),
        jnp.zeros((7, 6 * d), F32)], axis=0)
    dm_sh = lax.dynamic_slice(dm, (0, chip * n_ada), (dm.shape[0], n_ada))
    g_ada, da = _mod_bwd(a_in, dm_sh, w_ada[0])
    dcc = _allgather8(da[N_DEV * nex:], "ag_dcc")
    halves = dict(zip(_EARLY, zip(mine, theirs)))
    halves.update(zip(_LATE, zip(late, late_theirs)))
    grad, delta, new_m, new_v = {}, {}, {}, {}
    for k in _BIG:
        a, b = halves[k]
        res = _adamw_halves(shard_of(w, k), a, b, shard_of(m, k), shard_of(v, k), core, "adamw_" + k)
        grad[k], delta[k], new_m[k], new_v[k] = [(o.T if k in _TRANSPOSED else o).reshape(w[k].shape) for o in res]

    shp = w_ada.shape
    outs, _ = _adamw(w_ada[0], g_ada, m["w_ada"][0], v["w_ada"][0], "adamw_w_ada")
    grad["w_ada"] = g_ada.reshape(shp)
    delta["w_ada"], new_m["w_ada"], new_v["w_ada"] = [o.reshape(shp) for o in outs]
    rows = [{k: t[k].reshape(1, -1) for k in _SMALL_NAMES} for t in (w, m, v)]
    small = _small_final(tot, dcc, sg8, *[[t[k] for k in _SMALL_NAMES] for t in rows])
    for res, outs in zip((grad, delta, new_m, new_v), small[:4]):
        for k, o in zip(_SMALL_NAMES, outs):
            res[k] = o.reshape(w[k].shape)
    return (small[4][0, 0], gx, *[grad[k] for k in _WEIGHTS], *[delta[k] for k in _WEIGHTS],
            *[new_m[k] for k in _WEIGHTS], *[new_v[k] for k in _WEIGHTS])
```

```python
import functools
import math

import jax
import jax.numpy as jnp
from jax import lax
from jax.experimental import pallas as pl
from jax.experimental.pallas import tpu as pltpu

F32 = jnp.float32
BF16 = jnp.bfloat16
MESH = pl.DeviceIdType.MESH

EPS = 1e-6
D_MODEL = 1024
D_FF = 4096
HEADS = 4
RET_DK = 64
RET_DV = 128
MLA_NOPE = 128
MLA_ROPE = 64
MLA_HEAD = 256
Q_LORA = 384
KV_LORA = 256
GRID_W = 64
ROPE_BASE = 10000.0
IN_COLS = 2240
IN_PAD = 2304
PG_COLS = 1152
N_CHIPS = 4
N_DEV = 8
LANES = 128
ADAM_LR = 0.001
ADAM_B1 = 0.9
ADAM_B2 = 0.999
ADAM_EPS = 1e-08
ADAM_WD = 0.01
ADAM_STEP = 10
VMEM_LIMIT = 56 * 1024 * 1024


def _dot(a, b):
    return jnp.dot(a, b, preferred_element_type=F32)


def _dot_nt(a, b):
    return lax.dot_general(a, b, (((1,), (1,)), ((), ())), preferred_element_type=F32)


def _dot_tn(a, b):
    return lax.dot_general(a, b, (((0,), (0,)), ((), ())), preferred_element_type=F32)


def _params(sem=None, vmem=None):
    return pltpu.CompilerParams(dimension_semantics=sem, vmem_limit_bytes=vmem)


def _full(shape):
    n = len(shape)
    return pl.BlockSpec(shape, lambda *_: (0,) * n)


def _rope(x, cos, sin):
    w = x.shape[-1]
    lo = (lax.broadcasted_iota(jnp.int32, (1, w), 1) % 64) < 32
    swapped = jnp.where(lo, pltpu.roll(x, w - 32, 1), pltpu.roll(x, 32, 1))
    return x * cos + swapped * sin


def _rope_t(g, cos, sin):
    w = g.shape[-1]
    lo = (lax.broadcasted_iota(jnp.int32, (1, w), 1) % 64) < 32
    t = g * sin
    swapped = jnp.where(lo, pltpu.roll(t, w - 32, 1), pltpu.roll(t, 32, 1))
    return g * cos + swapped


def _rope_tables(seq, tm):
    rows = seq // GRID_W
    row = jnp.repeat(jnp.arange(rows, dtype=F32), GRID_W)
    col = jnp.tile(jnp.arange(GRID_W, dtype=F32), rows)
    n_freq = RET_DK // 4
    freq = ROPE_BASE ** (-jnp.arange(n_freq, dtype=F32) / n_freq)
    ang = jnp.concatenate([row[:, None] * freq, col[:, None] * freq], axis=-1)
    cos, sin = jnp.cos(ang), jnp.sin(ang)
    cos_t = jnp.tile(jnp.concatenate([cos, cos], -1), (1, HEADS))
    sin_t = jnp.tile(jnp.concatenate([-sin, sin], -1), (1, HEADS))
    cos_t = jnp.concatenate([cos_t, jnp.ones((tm, 4 * RET_DK), F32)], 0)
    sin_t = jnp.concatenate([sin_t, jnp.zeros((tm, 4 * RET_DK), F32)], 0)
    return cos_t, sin_t


def _adam_math(w, g, m, v):
    mn = ADAM_B1 * m + (1.0 - ADAM_B1) * g
    vn = ADAM_B2 * v + (1.0 - ADAM_B2) * (g * g)
    m_hat = mn / (1.0 - ADAM_B1 ** ADAM_STEP)
    v_hat = vn / (1.0 - ADAM_B2 ** ADAM_STEP)
    return -ADAM_LR * (m_hat / (jnp.sqrt(v_hat) + ADAM_EPS) + ADAM_WD * w), mn, vn


def _cast_into_slot(w, slot, name, rider=None):
    r, c = w.shape
    rb = max(b for b in range(16, 257, 16) if r % b == 0)

    def body(s_ref, w_ref, o_ref):
        o_ref[...] = w_ref[...].astype(BF16)

    (out,), carried = _hosted_call(
        body, (w,), name=name, grid=(r // rb,), prefetch=(slot,),
        in_specs=[pl.BlockSpec((rb, c), lambda i, s: (i, 0))],
        out_specs=[pl.BlockSpec((None, rb, c), lambda i, s: (s[0], i, 0))],
        out_shape=[jax.ShapeDtypeStruct((N_CHIPS, r, c), BF16)], sem=("parallel",), rider=rider)
    return out, carried


def _adamw_halves(w, mine, theirs, m, v, core, name):
    r, c = w.shape
    r2 = r // 2
    rb = max(b for b in range(8, r2 + 1, 8) if r2 % b == 0 and b * c * 4 <= (1 << 21))
    nbh = r2 // rb

    def body(z_ref, w_ref, a_ref, b_ref, m_ref, v_ref, g_ref, d_ref, mo_ref, vo_ref):
        here = (pl.program_id(0) // nbh) == z_ref[0]
        gg = jnp.where(here, a_ref[...], b_ref[...])
        g_ref[...] = gg
        d_ref[...], mo_ref[...], vo_ref[...] = _adam_math(w_ref[...], gg, m_ref[...], v_ref[...])

    spec = pl.BlockSpec((rb, c), lambda i, z: (i, 0))
    a_spec = pl.BlockSpec((rb, c), lambda i, z: (jnp.clip(i - z[0] * nbh, 0, nbh - 1), 0))
    b_spec = pl.BlockSpec((rb, c), lambda i, z: (jnp.clip(i - (1 - z[0]) * nbh, 0, nbh - 1), 0))
    shp = jax.ShapeDtypeStruct((r, c), F32)
    return pl.pallas_call(
        body, name=name,
        grid_spec=pltpu.PrefetchScalarGridSpec(
            num_scalar_prefetch=1, grid=(r // rb,), in_specs=[spec, a_spec, b_spec, spec, spec], out_specs=[spec] * 4),
        out_shape=[shp] * 4,
        compiler_params=_params(("parallel",)),
    )(core, w, mine, theirs, m, v)


def _adamw(w, g, m, v, name, rider=None):
    r, c = w.shape
    rb = r
    for cand in (256, 128, 64, 32, 16, 8):
        if r % cand == 0 and cand * c * 4 <= (1 << 20):
            rb = cand
            break
    if r * c * 4 <= (1 << 20):
        rb = r

    def body(w_ref, g_ref, m_ref, v_ref, d_ref, mo_ref, vo_ref):
        d_ref[...], mo_ref[...], vo_ref[...] = _adam_math(w_ref[...], g_ref[...], m_ref[...], v_ref[...])

    spec = pl.BlockSpec((rb, c), lambda i: (i, 0))
    shp = jax.ShapeDtypeStruct((r, c), F32)
    return _hosted_call(
        body, (w, g, m, v), name=name, grid=(r // rb,), in_specs=[spec] * 4, out_specs=[spec] * 3, out_shape=[shp] * 3,
        sem=("parallel",), rider=rider)


def _decay_prep(dec):
    def body(d_ref, lg_ref, sg_ref):
        d = d_ref[...]
        lg_ref[...] = jnp.minimum(d, 0.0) - jnp.log(1.0 + jnp.exp(-jnp.abs(d)))
        sg_ref[...] = 1.0 / (1.0 + jnp.exp(d))

    shp = jax.ShapeDtypeStruct(dec.shape, F32)
    return pl.pallas_call(body, name="decay_prep", out_shape=[shp, shp])(dec)


def _mod_fwd(a_in, w_ada, b_sh):
    rows, d = a_in.shape
    n = w_ada.shape[1]
    bn = 512

    def body(a_ref, w_ref, b_ref, o_ref):
        a = a_ref[...]
        s = (a / (1.0 + jnp.exp(-a))).astype(BF16)
        o_ref[...] = _dot(s, w_ref[...].astype(BF16)) + b_ref[...]

    return pl.pallas_call(
        body, name="mod_fwd", grid=(n // bn,),
        in_specs=[_full((rows, d)), pl.BlockSpec((d, bn), lambda j: (0, j)), pl.BlockSpec((1, bn), lambda j: (0, j))],
        out_specs=pl.BlockSpec((rows, bn), lambda j: (0, j)),
        out_shape=jax.ShapeDtypeStruct((rows, n), F32),
        compiler_params=_params(("parallel",)),
    )(a_in, w_ada, b_sh)


def _mod_bwd(a_in, dm, w_ada):
    rows, d = a_in.shape
    n = w_ada.shape[1]
    bn = 512
    nb = n // bn

    def body(a_ref, dm_ref, w_ref, gw_ref, da_ref):
        j = pl.program_id(0)
        a = a_ref[...]
        s = (a / (1.0 + jnp.exp(-a))).astype(BF16)
        dmb = dm_ref[...].astype(BF16)
        gw_ref[...] = _dot_tn(s, dmb)
        part = _dot_nt(dmb, w_ref[...].astype(BF16))

        @pl.when(j == 0)
        def _():
            da_ref[...] = part

        @pl.when(j > 0)
        def _():
            da_ref[...] += part

    return pl.pallas_call(
        body, name="mod_bwd", grid=(nb,),
        in_specs=[_full((rows, d)), pl.BlockSpec((rows, bn), lambda j: (0, j)), pl.BlockSpec((d, bn), lambda j: (0, j))],
        out_specs=[pl.BlockSpec((d, bn), lambda j: (0, j)), _full((rows, d))],
        out_shape=[jax.ShapeDtypeStruct((d, n), F32), jax.ShapeDtypeStruct((rows, d), F32)],
        compiler_params=_params(("arbitrary",)),
    )(a_in, dm, w_ada)


def _pre_fwd(x2, ctx2, modv, g_attn, w_in, g_q, g_kv, w_uq, w_ukv, cos_t, sin_t, *, seq, tm, rider=None):
    t_lat, d = x2.shape
    t_ctx = ctx2.shape[0]
    nl, nc = t_lat // tm, t_ctx // tm
    n_all = t_lat + t_ctx
    tpe = seq // tm
    nex = t_lat // seq

    def body(x_ref, c_ref, mod_ref, g_ref, win_ref, gq_ref, gkv_ref, wuq_ref, wukv_ref, cos_ref, sin_ref,
             h_ref, pg_ref, rq_ref, rk_ref, rv_ref, nq_ref, nkv_ref, q_ref, k_ref, v_ref):
        i = pl.program_id(0)
        xt = jnp.where(i < nl, x_ref[...], c_ref[...])
        sh = mod_ref[0, 0:1, :]
        sc = mod_ref[0, 1:2, :]
        r = lax.rsqrt(jnp.mean(xt * xt, axis=-1, keepdims=True) + EPS)
        hb = ((xt * r) * g_ref[...] * (1.0 + sc) + sh).astype(BF16)
        h_ref[...] = hb
        p = _dot_nt(hb, win_ref[...])
        cos = cos_ref[...]
        sin = sin_ref[...]
        rq_ref[...] = _rope(p[:, 0:256], cos, sin).astype(BF16)
        rk_ref[...] = _rope(p[:, 256:512] * (RET_DK ** -0.5), cos, sin).astype(BF16)
        rv_ref[...] = p[:, 512:1024].astype(BF16)
        pg_ref[...] = p[:, 1024:2176]
        cq = p[:, 1536:1920]
        ckv = p[:, 1920:2176]
        nqb = (cq * lax.rsqrt(jnp.mean(cq * cq, axis=-1, keepdims=True) + EPS) * gq_ref[...]).astype(BF16)
        nkvb = (ckv * lax.rsqrt(jnp.mean(ckv * ckv, axis=-1, keepdims=True) + EPS) * gkv_ref[...]).astype(BF16)
        nq_ref[...] = nqb
        nkv_ref[...] = nkvb
        cos1 = cos[:, 0:LANES]
        sin1 = sin[:, 0:LANES]
        kpe = _rope(p[:, 2176:2304], cos1, sin1).astype(BF16)
        for hd in range(HEADS):
            o = hd * MLA_HEAD
            qh = _dot_nt(nqb, wuq_ref[hd]) * MLA_SCALE
            q_ref[:, o:o + 128] = qh[:, 0:128].astype(BF16)
            q_ref[:, o + 128:o + 256] = _rope(qh[:, 128:256], cos1, sin1).astype(BF16)
            kvh = _dot(nkvb, wukv_ref[hd])
            k_ref[:, o:o + 128] = kvh[:, 0:128].astype(BF16)
            k_ref[:, o + 128:o + 256] = kpe
            v_ref[:, hd * 128:(hd + 1) * 128] = kvh[:, 128:256].astype(BF16)

    def tile(width):
        return pl.BlockSpec((tm, width), lambda i: (i, 0))

    widths = (d, PG_COLS, 256, 256, 512, Q_LORA, KV_LORA, HEADS * MLA_HEAD, HEADS * MLA_HEAD, HEADS * 128)
    dtypes = (BF16, F32, BF16, BF16, BF16, BF16, BF16, BF16, BF16, BF16)
    tab = pl.BlockSpec((tm, 256), lambda i: (jnp.where(i < nl, i % tpe, tpe), 0))
    return _hosted_call(
        body, (x2, ctx2, modv, g_attn, w_in, g_q, g_kv, w_uq, w_ukv, cos_t, sin_t), name="pre_fwd", grid=(nl + nc,),
        in_specs=[
            pl.BlockSpec((tm, d), lambda i: (jnp.minimum(i, nl - 1), 0)),
            pl.BlockSpec((tm, d), lambda i: (jnp.maximum(i - nl, 0), 0)),
            pl.BlockSpec((1, 8, d), lambda i: (jnp.minimum(i // tpe, nex), 0, 0)),
            _full((1, d)), _full(w_in.shape), _full((1, Q_LORA)), _full((1, KV_LORA)),
            _full(w_uq.shape), _full(w_ukv.shape), tab, tab,
        ],
        out_specs=[tile(w) for w in widths],
        out_shape=[jax.ShapeDtypeStruct((n_all, w), dt) for w, dt in zip(widths, dtypes)],
        sem=("parallel",), rider=rider)


def _post(yret, ymla, x2, tgt2, modv, g_ffn, g_fin, w_out, w_ff1, w_ff2, *, seq, tm):
    t_lat, d = x2.shape
    nl = t_lat // tm
    tpe = seq // tm
    nex = t_lat // seq
    n_slab = w_ff1.shape[0]
    fs = w_ff1.shape[2]

    def body(yr_ref, ym_ref, x_ref, t_ref, mod_ref, gf_ref, gl_ref, wo_ref, w1_ref, w2_ref,
             mix_ref, a_ref, du_ref, h2_ref, df_ref, dmo_ref, dmix_ref, dxm_ref, st_ref, ru_ref):
        i = pl.program_id(0)
        gt_a = mod_ref[0, 2:3, :]
        sh_f = mod_ref[0, 3:4, :]
        sc_f = mod_ref[0, 4:5, :]
        gt_f = mod_ref[0, 5:6, :]
        g_ffn_v = gf_ref[...]
        g_fin_v = gl_ref[...]
        yr = yr_ref[...]
        ym = ym_ref[...]
        mix_ref[:, 0:512] = yr
        mix_ref[:, 512:1024] = ym
        op = _dot(yr, wo_ref[0:512, :]) + _dot(ym, wo_ref[512:1024, :])
        x_mid = x_ref[...] + gt_a * op
        r2 = lax.rsqrt(jnp.mean(x_mid * x_mid, axis=-1, keepdims=True) + EPS)
        xh2 = x_mid * r2
        h2b = (xh2 * g_ffn_v * (1.0 + sc_f) + sh_f).astype(BF16)
        h2_ref[...] = h2b
        f = jnp.zeros((tm, d), F32)
        for s in range(n_slab):
            ru = jnp.maximum(_dot(h2b, w1_ref[s]), 0.0)
            ru_ref[:, s * fs:(s + 1) * fs] = ru
            ab = (ru * ru).astype(BF16)
            a_ref[:, s * fs:(s + 1) * fs] = ab
            f = f + _dot(ab, w2_ref[s * fs:(s + 1) * fs, :])
        x_out = x_mid + gt_f * f
        r3 = lax.rsqrt(jnp.mean(x_out * x_out, axis=-1, keepdims=True) + EPS)
        xh3 = x_out * r3
        err = xh3 * g_fin_v - t_ref[...]
        dy = err * (1.0 / d)
        dxh3 = dy * g_fin_v
        dx_out = r3 * (dxh3 - xh3 * jnp.mean(dxh3 * xh3, axis=-1, keepdims=True))
        dfb = (dx_out * gt_f).astype(BF16)
        df_ref[...] = dfb
        dh2 = jnp.zeros((tm, d), F32)
        for s in range(n_slab):
            da = _dot_nt(dfb, w2_ref[s * fs:(s + 1) * fs, :])
            dub = (da * (2.0 * ru_ref[:, s * fs:(s + 1) * fs])).astype(BF16)
            du_ref[:, s * fs:(s + 1) * fs] = dub
            dh2 = dh2 + _dot_nt(dub, w1_ref[s])
        dxh2 = dh2 * (1.0 + sc_f) * g_ffn_v
        dx_mid = dx_out + r2 * (dxh2 - xh2 * jnp.mean(dxh2 * xh2, axis=-1, keepdims=True))
        dxm_ref[...] = dx_mid
        dmob = (dx_mid * gt_a).astype(BF16)
        dmo_ref[...] = dmob
        dmix_ref[...] = _dot_nt(dmob, wo_ref[...]).astype(BF16)

        def rsum(v):
            return jnp.sum(v, axis=0, keepdims=True)

        stats = jnp.concatenate([
            rsum(dh2), rsum(dh2 * xh2 * g_ffn_v), rsum(dx_out * f), rsum(dx_mid * op),
            rsum(dh2 * (1.0 + sc_f) * xh2), rsum(dy * xh3), rsum(err * err), jnp.zeros((1, d), F32)], axis=0)

        @pl.when(i % tpe == 0)
        def _():
            st_ref[0] = stats

        @pl.when(i % tpe != 0)
        def _():
            st_ref[0] += stats

    def tile(width):
        return pl.BlockSpec((tm, width), lambda i: (i, 0))

    widths = (d, D_FF, D_FF, d, d, d, d, d)
    dtypes = (BF16, BF16, BF16, BF16, BF16, BF16, BF16, F32)
    const = pl.Buffered(1)
    return pl.pallas_call(
        body, name="post", grid=(nl,),
        in_specs=[
            tile(512), tile(512), tile(d), tile(d),
            pl.BlockSpec((1, 8, d), lambda i: (i // tpe, 0, 0)),
            _full((1, d)), _full((1, d)),
            pl.BlockSpec(w_out.shape, lambda i: (0, 0), pipeline_mode=const),
            pl.BlockSpec(w_ff1.shape, lambda i: (0, 0, 0), pipeline_mode=const),
            pl.BlockSpec(w_ff2.shape, lambda i: (0, 0), pipeline_mode=const),
        ],
        out_specs=[tile(w) for w in widths] + [pl.BlockSpec((1, 8, d), lambda i: (i // tpe, 0, 0))],
        out_shape=[jax.ShapeDtypeStruct((t_lat, w), dt) for w, dt in zip(widths, dtypes)]
        + [jax.ShapeDtypeStruct((nex, 8, d), F32)],
        scratch_shapes=[pltpu.VMEM((tm, D_FF), F32)],
        compiler_params=_params(("arbitrary",), VMEM_LIMIT),
    )(yret, ymla, x2, tgt2, modv, g_ffn, g_fin, w_out, w_ff1, w_ff2)


def _pre_bwd(x2, ctx2, modv, g_attn, pg, drq, drk, dkc_r, drv, dvc_r, drg, dq_m, dkl, dkc, dvl, dvc, dxm,
             w_in, g_q, g_kv, w_uq, w_ukv, cos_t, sin_t, *, seq, tm, rider=None):
    t_lat, d = x2.shape
    t_ctx = ctx2.shape[0]
    nl, nc = t_lat // tm, t_ctx // tm
    n_all = t_lat + t_ctx
    tpe = seq // tm
    nex = t_lat // seq

    def body(x_ref, c_ref, mod_ref, g_ref, pg_ref, drq_ref, drk_ref, dkcr_ref, drv_ref, dvcr_ref, drg_ref,
             dq_ref, dkl_ref, dkc_ref, dvl_ref, dvc_ref, dxm_ref, win_ref, gq_ref, gkv_ref, wuq_ref, wukv_ref,
             cos_ref, sin_ref, dpb_ref, dqf_ref, dkvf_ref, gx_ref, st_ref):
        i = pl.program_id(0)
        lat = i < nl
        latf = lat.astype(F32)
        cos = cos_ref[...]
        sin = sin_ref[...]
        cos1 = cos[:, 0:LANES]
        sin1 = sin[:, 0:LANES]
        d_rq = _rope_t(drq_ref[...] * latf, cos, sin)
        d_rk = _rope_t(jnp.where(lat, drk_ref[...], dkcr_ref[...]), cos, sin) * (RET_DK ** -0.5)
        d_rv = jnp.where(lat, drv_ref[...], dvcr_ref[...])
        d_rg = drg_ref[...] * latf
        dq_all = dq_ref[...] * (latf * MLA_SCALE)
        dk_all = jnp.where(lat, dkl_ref[...], dkc_ref[...])
        dv_all = jnp.where(lat, dvl_ref[...], dvc_ref[...])
        dnq = jnp.zeros((tm, Q_LORA), F32)
        dnkv = jnp.zeros((tm, KV_LORA), F32)
        dkpe = jnp.zeros((tm, LANES), F32)
        for hd in range(HEADS):
            o = hd * MLA_HEAD
            dqh = jnp.concatenate([dq_all[:, o:o + 128], _rope_t(dq_all[:, o + 128:o + 256], cos1, sin1)],
                                  axis=1).astype(BF16)
            dqf_ref[:, o:o + 256] = dqh
            dnq = dnq + _dot(dqh, wuq_ref[hd])
            dkpe = dkpe + dk_all[:, o + 128:o + 256]
            dkvh = jnp.concatenate([dk_all[:, o:o + 128], dv_all[:, hd * 128:(hd + 1) * 128]], axis=1).astype(BF16)
            dkvf_ref[:, o:o + 256] = dkvh
            dnkv = dnkv + _dot_nt(dkvh, wukv_ref[hd])
        d_kpe = _rope_t(dkpe, cos1, sin1)
        pgv = pg_ref[...]
        cq = pgv[:, 512:896]
        ckv = pgv[:, 896:1152]
        rq_ = lax.rsqrt(jnp.mean(cq * cq, axis=-1, keepdims=True) + EPS)
        cqh = cq * rq_
        dcqh = dnq * gq_ref[...]
        d_cq = rq_ * (dcqh - cqh * jnp.mean(dcqh * cqh, axis=-1, keepdims=True))
        rkv_ = lax.rsqrt(jnp.mean(ckv * ckv, axis=-1, keepdims=True) + EPS)
        ckvh = ckv * rkv_
        dckvh = dnkv * gkv_ref[...]
        d_ckv = rkv_ * (dckvh - ckvh * jnp.mean(dckvh * ckvh, axis=-1, keepdims=True))
        dpb = jnp.concatenate([d_rq, d_rk, d_rv, d_rg, d_cq, d_ckv, d_kpe], axis=1).astype(BF16)
        dpb_ref[...] = dpb
        dh = _dot(dpb, win_ref[...])
        xt = jnp.where(lat, x_ref[...], c_ref[...])
        sc = mod_ref[0, 1:2, :]
        g = g_ref[...]
        r = lax.rsqrt(jnp.mean(xt * xt, axis=-1, keepdims=True) + EPS)
        xh = xt * r
        dxh = dh * (1.0 + sc) * g
        dx = r * (dxh - xh * jnp.mean(dxh * xh, axis=-1, keepdims=True))

        @pl.when(lat)
        def _():
            gx_ref[...] = dxm_ref[...] + dx

        def rsum(v):
            return jnp.sum(v, axis=0, keepdims=True)

        def widen(v):
            return jnp.concatenate([v, jnp.zeros((1, d - v.shape[1]), F32)], axis=1)

        stats = jnp.concatenate([
            rsum(dh), rsum(dh * xh * g), rsum(dh * (1.0 + sc) * xh), widen(rsum(dnq * cqh)), widen(rsum(dnkv * ckvh)),
            jnp.zeros((3, d), F32)], axis=0)
        first = jnp.logical_or(jnp.logical_and(lat, i % tpe == 0), i == nl)

        @pl.when(first)
        def _():
            st_ref[0] = stats

        @pl.when(jnp.logical_not(first))
        def _():
            st_ref[0] += stats

    def lat_tile(width):
        return pl.BlockSpec((tm, width), lambda i: (jnp.minimum(i, nl - 1), 0))

    def ctx_tile(width):
        return pl.BlockSpec((tm, width), lambda i: (jnp.maximum(i - nl, 0), 0))

    def tile(width):
        return pl.BlockSpec((tm, width), lambda i: (i, 0))

    tab = pl.BlockSpec((tm, 256), lambda i: (jnp.where(i < nl, i % tpe, tpe), 0))
    ex = pl.BlockSpec((1, 8, d), lambda i: (jnp.minimum(i // tpe, nex), 0, 0))
    return _hosted_call(
        body, (x2, ctx2, modv, g_attn, pg, drq, drk, dkc_r, drv, dvc_r, drg, dq_m, dkl, dkc, dvl, dvc, dxm,
               w_in, g_q, g_kv, w_uq, w_ukv, cos_t, sin_t), name="pre_bwd", grid=(nl + nc,),
        in_specs=[
            lat_tile(d), ctx_tile(d), ex, _full((1, d)), tile(PG_COLS),
            lat_tile(256), lat_tile(256), ctx_tile(256), lat_tile(512), ctx_tile(512), lat_tile(512),
            lat_tile(1024), lat_tile(1024), ctx_tile(1024), lat_tile(512), ctx_tile(512), lat_tile(d),
            _full(w_in.shape), _full((1, Q_LORA)), _full((1, KV_LORA)), _full(w_uq.shape), _full(w_ukv.shape),
            tab, tab,
        ],
        out_specs=[tile(IN_PAD), tile(1024), tile(1024), lat_tile(d), ex],
        out_shape=[
            jax.ShapeDtypeStruct((n_all, IN_PAD), BF16), jax.ShapeDtypeStruct((n_all, 1024), BF16),
            jax.ShapeDtypeStruct((n_all, 1024), BF16), jax.ShapeDtypeStruct((t_lat, d), F32),
            jax.ShapeDtypeStruct((nex + 1, 8, d), F32),
        ],
        sem=("arbitrary",), rider=rider)


MLA_SCALE = 1.0 / math.sqrt(MLA_NOPE + MLA_ROPE)
KEY_BLOCK = 2048


def _mla_specs(t_lat, seq, ctx_len, tq):
    nqt = seq // tq
    cb = t_lat // ctx_len
    q = pl.BlockSpec((tq, MLA_HEAD), lambda b, h, j: (b * nqt + j, h))
    kl = pl.BlockSpec((seq, MLA_HEAD), lambda b, h, j: (b, h))
    kc = pl.BlockSpec((ctx_len, MLA_HEAD), lambda b, h, j: (cb + b, h))
    vl = pl.BlockSpec((seq, 128), lambda b, h, j: (b, h))
    vc = pl.BlockSpec((ctx_len, 128), lambda b, h, j: (cb + b, h))
    o = pl.BlockSpec((tq, 128), lambda b, h, j: (b * nqt + j, h))
    return q, kl, kc, vl, vc, o


def _mla_fwd(q, k, v, *, t_lat, seq, ctx_len, tq, rider=None):
    nex = t_lat // seq

    def body(q_ref, kl_ref, kc_ref, vl_ref, vc_ref, o_ref, lse_ref):
        qb = q_ref[...]
        s = _dot_nt(qb, kl_ref[...])
        sc = _dot_nt(qb, kc_ref[...])
        m = jnp.maximum(jnp.max(s, axis=-1, keepdims=True), jnp.max(sc, axis=-1, keepdims=True))
        p = jnp.exp(s - m)
        pc = jnp.exp(sc - m)
        total = jnp.sum(p, axis=-1, keepdims=True) + jnp.sum(pc, axis=-1, keepdims=True)
        o = _dot(p.astype(BF16), vl_ref[...]) + _dot(pc.astype(BF16), vc_ref[...])
        o_ref[...] = (o * (1.0 / total)).astype(BF16)
        lse_ref[...] = jnp.broadcast_to(m + jnp.log(total), lse_ref.shape)

    qs, kl, kc, vl, vc, os_ = _mla_specs(t_lat, seq, ctx_len, tq)
    return _hosted_call(
        body, (q, k, k, v, v), name="mla_fwd", grid=(nex, HEADS, seq // tq),
        in_specs=[qs, kl, kc, vl, vc], out_specs=[os_, os_],
        out_shape=[jax.ShapeDtypeStruct((t_lat, HEADS * 128), BF16), jax.ShapeDtypeStruct((t_lat, HEADS * 128), F32)],
        sem=("parallel", "parallel", "arbitrary"), rider=rider)


def _mla_bwd(q, k, v, ymla, lse, dmix, *, t_lat, seq, ctx_len, tq, rider=None):
    nex = t_lat // seq
    nqt = seq // tq
    t_ctx = nex * ctx_len
    kb = min(KEY_BLOCK, seq)

    def body(q_ref, kl_ref, kc_ref, vl_ref, vc_ref, o_ref, lse_ref, do_ref, dq_ref, dkl_ref, dkc_ref, dvl_ref, dvc_ref):
        j = pl.program_id(2)

        @pl.when(j == 0)
        def _():
            dkl_ref[...] = jnp.zeros(dkl_ref.shape, F32)
            dkc_ref[...] = jnp.zeros(dkc_ref.shape, F32)
            dvl_ref[...] = jnp.zeros(dvl_ref.shape, F32)
            dvc_ref[...] = jnp.zeros(dvc_ref.shape, F32)

        qb = q_ref[...]
        dob = do_ref[...]
        delta = jnp.sum(dob.astype(F32) * o_ref[...].astype(F32), axis=-1, keepdims=True)
        lse_row = lse_ref[:, 0:1]

        def block(k_ref, v_ref, dk_ref, dv_ref, rows):
            kbl = k_ref[rows, :]
            vbl = v_ref[rows, :]
            p = jnp.exp(_dot_nt(qb, kbl) - lse_row)
            ds = (p * (_dot_nt(dob, vbl) - delta)).astype(BF16)
            dk_ref[rows, :] += _dot_tn(ds, qb)
            dv_ref[rows, :] += _dot_tn(p.astype(BF16), dob)
            return _dot(ds, kbl)

        dq = block(kc_ref, vc_ref, dkc_ref, dvc_ref, pl.ds(0, ctx_len))
        for i in range(seq // kb):
            dq = dq + block(kl_ref, vl_ref, dkl_ref, dvl_ref, pl.ds(i * kb, kb))
        dq_ref[...] = dq

    qs, kl, kc, vl, vc, os_ = _mla_specs(t_lat, seq, ctx_len, tq)
    do_spec = pl.BlockSpec((tq, 128), lambda b, h, j: (b * nqt + j, HEADS + h))
    return _hosted_call(
        body, (q, k, k, v, v, ymla, lse, dmix), name="mla_bwd", grid=(nex, HEADS, nqt),
        in_specs=[qs, kl, kc, vl, vc, os_, os_, do_spec],
        out_specs=[
            qs,
            pl.BlockSpec((seq, MLA_HEAD), lambda b, h, j: (b, h)),
            pl.BlockSpec((ctx_len, MLA_HEAD), lambda b, h, j: (b, h)),
            pl.BlockSpec((seq, 128), lambda b, h, j: (b, h)),
            pl.BlockSpec((ctx_len, 128), lambda b, h, j: (b, h)),
        ],
        out_shape=[
            jax.ShapeDtypeStruct((t_lat, HEADS * MLA_HEAD), F32),
            jax.ShapeDtypeStruct((t_lat, HEADS * MLA_HEAD), F32),
            jax.ShapeDtypeStruct((t_ctx, HEADS * MLA_HEAD), F32),
            jax.ShapeDtypeStruct((t_lat, HEADS * 128), F32),
            jax.ShapeDtypeStruct((t_ctx, HEADS * 128), F32),
        ],
        sem=("parallel", "parallel", "arbitrary"), rider=rider)


def _decay_terms(lg, chunk, forward):
    ii = lax.broadcasted_iota(jnp.int32, (chunk, chunk), 0)
    jj = lax.broadcasted_iota(jnp.int32, (chunk, chunk), 1)
    diff = (ii - jj) if forward else (jj - ii)
    dist = jnp.maximum(diff, 0).astype(F32)
    dmat = jnp.where(diff >= 0, jnp.exp(lg * dist), 0.0)
    pos = lax.broadcasted_iota(jnp.int32, (chunk, 1), 0).astype(F32)
    if forward:
        e_q = pos + 1.0
        e_k = (chunk - 1.0) - pos
    else:
        e_q = chunk - pos
        e_k = pos
    wq = jnp.exp(lg * e_q)
    wk = jnp.exp(lg * e_k)
    cd = jnp.exp(jnp.full((1, 1), lg * chunk, F32))
    return dmat, dist, wq, wk, e_q, e_k, cd


def _ctx_weights(lg, ctx_len, forward):
    pos = lax.broadcasted_iota(jnp.int32, (ctx_len, 1), 0).astype(F32)
    e = ((ctx_len - 1.0) - pos) if forward else pos
    return jnp.exp(lg * e), e


def _pair_specs(t_lat, seq, ctx_len):
    cb = t_lat // ctx_len
    qk = pl.BlockSpec((seq, 128), lambda b, p: (b, p))
    v = pl.BlockSpec((seq, 256), lambda b, p: (b, p))
    kc = pl.BlockSpec((ctx_len, 128), lambda b, p: (cb + b, p))
    vc = pl.BlockSpec((ctx_len, 256), lambda b, p: (cb + b, p))
    return qk, v, kc, vc


def _lane_masks():
    lane = lax.broadcasted_iota(jnp.int32, (1, 128), 1)
    return [(lane // RET_DK) == hh for hh in (0, 1)]


def _ret_fwd_pair(rq, rk, rv, pg, lg, g_ret, *, t_lat, seq, ctx_len, chunk, rider=None):
    nex = t_lat // seq
    n_chunk = seq // chunk

    def body(q_ref, k_ref, v_ref, kc_ref, vc_ref, rg_ref, lg_ref, g_ref, y_ref, o_ref):
        pair = pl.program_id(1)
        masks = _lane_masks()
        kcf = kc_ref[...].astype(F32)

        def run(forward):
            terms, s0 = [], []
            for hh in (0, 1):
                lgd = lg_ref[0 if forward else 1, 2 * pair + hh]
                terms.append(_decay_terms(lgd, chunk, forward))
                wc, _ = _ctx_weights(lgd, ctx_len, forward)
                s0.append(_dot_tn((jnp.where(masks[hh], kcf, 0.0) * wc).astype(BF16), vc_ref[:, hh * 128:(hh + 1) * 128]))

            def step(t, states):
                n = t if forward else n_chunk - 1 - t
                sl = pl.ds(pl.multiple_of(n * chunk, chunk), chunk)
                qb = q_ref[sl, :]
                kf_all = k_ref[sl, :].astype(F32)
                new = []
                for hh in (0, 1):
                    dmat, _, wq, wk, _, _, cd = terms[hh]
                    cols = slice(hh * 128, (hh + 1) * 128)
                    qm = jnp.where(masks[hh], qb, jnp.zeros((), BF16))
                    kf = jnp.where(masks[hh], kf_all, 0.0)
                    vb = v_ref[sl, cols]
                    a = _dot_nt(qm, kf.astype(BF16)) * dmat
                    o = _dot(a.astype(BF16), vb) + wq * _dot(qm, states[hh].astype(BF16))
                    if forward:
                        o_ref[sl, cols] = o
                    else:
                        o = o_ref[sl, cols] + o
                        o_ref[sl, cols] = o
                        mu = jnp.mean(o, axis=-1, keepdims=True)
                        oc = o - mu
                        var = jnp.mean(oc * oc, axis=-1, keepdims=True)
                        rg = rg_ref[sl, cols]
                        y_ref[sl, cols] = (oc * lax.rsqrt(var + EPS) * g_ref[:, cols] * (rg / (1.0 + jnp.exp(-rg)))).astype(BF16)
                    new.append(cd * states[hh] + _dot_tn((kf * wk).astype(BF16), vb))
                return tuple(new)

            lax.fori_loop(0, n_chunk, step, tuple(s0))

        run(True)
        run(False)

    qk, v, kc, vc = _pair_specs(t_lat, seq, ctx_len)
    return _hosted_call(
        body, (rq, rk, rv, rk, rv, pg, lg, g_ret), name="ret_fwd", grid=(nex, HEADS // 2),
        in_specs=[qk, qk, v, kc, vc, v, pl.BlockSpec(memory_space=pltpu.SMEM), pl.BlockSpec((1, 256), lambda b, p: (0, p))],
        out_specs=[v, v],
        out_shape=[jax.ShapeDtypeStruct((t_lat, HEADS * RET_DV), BF16), jax.ShapeDtypeStruct((t_lat, HEADS * RET_DV), F32)],
        sem=("parallel", "arbitrary"), rider=rider)


def _ret_bwd_pair(rq, rk, rv, pg, osum, dmix, lg, g_ret, *, t_lat, seq, ctx_len, chunk, rider=None):
    nex = t_lat // seq
    n_chunk = seq // chunk
    t_ctx = nex * ctx_len

    def body(q_ref, k_ref, v_ref, kc_ref, vc_ref, rg_ref, o_ref, dy_ref, lg_ref, g_ref,
             dq_ref, dk_ref, dv_ref, dkc_ref, dvc_ref, drg_ref, st_ref, do_s, s_st):
        pair = pl.program_id(1)
        masks = _lane_masks()
        kcf = kc_ref[...].astype(F32)

        def norm_step(n, dgains):
            sl = pl.ds(pl.multiple_of(n * chunk, chunk), chunk)
            out = []
            for hh in (0, 1):
                cols = slice(hh * 128, (hh + 1) * 128)
                gain = g_ref[:, cols]
                o = o_ref[sl, cols]
                mu = jnp.mean(o, axis=-1, keepdims=True)
                oc = o - mu
                rstd = lax.rsqrt(jnp.mean(oc * oc, axis=-1, keepdims=True) + EPS)
                ohat = oc * rstd
                rg = rg_ref[sl, cols]
                sg = 1.0 / (1.0 + jnp.exp(-rg))
                dy = dy_ref[sl, cols].astype(F32)
                don = dy * (rg * sg)
                drg_ref[sl, cols] = dy * (ohat * gain) * (sg * (1.0 + rg * (1.0 - sg)))
                dohat = don * gain
                do_s[sl, cols] = rstd * (dohat - jnp.mean(dohat, axis=-1, keepdims=True)
                                         - ohat * jnp.mean(dohat * ohat, axis=-1, keepdims=True))
                out.append(dgains[hh] + jnp.sum(don * ohat, axis=0, keepdims=True))
            return tuple(out)

        zero_row = jnp.zeros((1, 128), F32)
        dgains = lax.fori_loop(0, n_chunk, norm_step, (zero_row, zero_row))
        dq_ref[...] = jnp.zeros(dq_ref.shape, F32)
        dk_ref[...] = jnp.zeros(dk_ref.shape, F32)
        dv_ref[...] = jnp.zeros(dv_ref.shape, F32)

        chains = [(forward, hh) for forward in (True, False) for hh in (0, 1)]
        terms, ctxw, s0 = [], [], []
        for forward, hh in chains:
            lgd = lg_ref[0 if forward else 1, 2 * pair + hh]
            terms.append(_decay_terms(lgd, chunk, forward))
            ctxw.append(_ctx_weights(lgd, ctx_len, forward))
            s0.append(_dot_tn((jnp.where(masks[hh], kcf, 0.0) * ctxw[-1][0]).astype(BF16), vc_ref[:, hh * 128:(hh + 1) * 128]))

        def chunk_at(t, ascending):
            n = t if ascending else n_chunk - 1 - t
            return n, pl.ds(pl.multiple_of(n * chunk, chunk), chunk)

        def state_step(t, states):
            new = []
            for c, (forward, hh) in enumerate(chains):
                n, sl = chunk_at(t, forward)
                wk, cd = terms[c][3], terms[c][6]
                s_st[c, n] = states[c]
                kf = jnp.where(masks[hh], k_ref[sl, :].astype(F32), 0.0)
                new.append(cd * states[c] + _dot_tn((kf * wk).astype(BF16), v_ref[sl, hh * 128:(hh + 1) * 128]))
            return tuple(new)

        lax.fori_loop(0, n_chunk, state_step, tuple(s0))

        def grad_step(t, carry):
            out = []
            for forward in (True, False):
                n, sl = chunk_at(t, not forward)
                qb = q_ref[sl, :]
                kf_all = k_ref[sl, :].astype(F32)
                dq_sum = jnp.zeros((chunk, 128), F32)
                dk_sum = jnp.zeros((chunk, 128), F32)
                for hh in (0, 1):
                    c = chains.index((forward, hh))
                    g_next, dlg = carry[c]
                    dmat, dist, wq, wk, e_q, e_k, cd = terms[c]
                    cols = slice(hh * 128, (hh + 1) * 128)
                    qm = jnp.where(masks[hh], qb, jnp.zeros((), BF16))
                    kf = jnp.where(masks[hh], kf_all, 0.0)
                    kb = kf.astype(BF16)
                    vb = v_ref[sl, cols]
                    do = do_s[sl, cols]
                    dob = do.astype(BF16)
                    s_n = s_st[c, n]
                    s_nb = s_n.astype(BF16)
                    gb = g_next.astype(BF16)
                    dk_cross = wk * _dot_nt(vb, gb)
                    dv_cross = _dot((kf * wk).astype(BF16), gb)
                    a = _dot_nt(qm, kb) * dmat
                    da_raw = _dot_nt(dob, vb)
                    dab = (da_raw * dmat).astype(BF16)
                    o_cross = wq * _dot(qm, s_nb)
                    dq_sum = dq_sum + _dot(dab, kb) + wq * _dot_nt(dob, s_nb)
                    dk_sum = dk_sum + _dot_tn(dab, qm) + dk_cross
                    dv_ref[sl, cols] += _dot_tn(a.astype(BF16), dob) + dv_cross
                    dlg = (dlg + chunk * cd * jnp.sum(g_next * s_n, keepdims=True)
                           + jnp.sum(e_k * jnp.sum(kf * dk_cross, axis=-1, keepdims=True), keepdims=True)
                           + jnp.sum(dist * a * da_raw, keepdims=True)
                           + jnp.sum(e_q * jnp.sum(o_cross * do, axis=-1, keepdims=True), keepdims=True))
                    out.append((cd * g_next + _dot_tn((qm.astype(F32) * wq).astype(BF16), dob), dlg))
                dq_ref[sl, :] += dq_sum
                dk_ref[sl, :] += dk_sum
            return tuple(out)

        zero = (jnp.zeros((128, 128), F32), jnp.zeros((1, 1), F32))
        res = lax.fori_loop(0, n_chunk, grad_step, (zero,) * len(chains))
        dkc_sum = jnp.zeros((ctx_len, 128), F32)
        dvc = [jnp.zeros((ctx_len, 128), F32)] * 2
        dlgs = []
        for c, (forward, hh) in enumerate(chains):
            ds0, dlg = res[c]
            wc, e_c = ctxw[c]
            kcm = jnp.where(masks[hh], kcf, 0.0)
            ds0b = ds0.astype(BF16)
            dkc_part = wc * _dot_nt(vc_ref[:, hh * 128:(hh + 1) * 128], ds0b)
            dkc_sum = dkc_sum + dkc_part
            dvc[hh] = dvc[hh] + _dot((kcm * wc).astype(BF16), ds0b)
            dlgs.append(dlg + jnp.sum(e_c * jnp.sum(kcm * dkc_part, axis=-1, keepdims=True), keepdims=True))
        dkc_ref[...] = dkc_sum
        for hh in (0, 1):
            cols = slice(hh * 128, (hh + 1) * 128)
            dvc_ref[:, cols] = dvc[hh]
            st_ref[0, :, cols] = jnp.concatenate([
                dgains[hh], jnp.broadcast_to(dlgs[hh], (1, 128)), jnp.broadcast_to(dlgs[2 + hh], (1, 128)),
                jnp.zeros((5, 128), F32)], axis=0)

    qk, v, kc, vc = _pair_specs(t_lat, seq, ctx_len)
    return _hosted_call(
        body, (rq, rk, rv, rk, rv, pg, osum, dmix, lg, g_ret), name="ret_bwd", grid=(nex, HEADS // 2),
        in_specs=[qk, qk, v, kc, vc, v, v, v, pl.BlockSpec(memory_space=pltpu.SMEM),
                  pl.BlockSpec((1, 256), lambda b, p: (0, p))],
        out_specs=[
            qk, qk, v,
            pl.BlockSpec((ctx_len, 128), lambda b, p: (b, p)),
            pl.BlockSpec((ctx_len, 256), lambda b, p: (b, p)),
            v,
            pl.BlockSpec((1, 8, 256), lambda b, p: (b, 0, p)),
        ],
        out_shape=[
            jax.ShapeDtypeStruct((t_lat, 256), F32), jax.ShapeDtypeStruct((t_lat, 256), F32),
            jax.ShapeDtypeStruct((t_lat, 512), F32), jax.ShapeDtypeStruct((t_ctx, 256), F32),
            jax.ShapeDtypeStruct((t_ctx, 512), F32), jax.ShapeDtypeStruct((t_lat, 512), F32),
            jax.ShapeDtypeStruct((nex, 8, 512), F32),
        ],
        scratch_shapes=[pltpu.VMEM((seq, 256), F32), pltpu.VMEM((4, n_chunk, 128, 128), F32)],
        sem=("parallel", "arbitrary"), rider=rider)


def _matmul_tn(a, b, *, bm, bn, bk, chip_major, name, out_dtype=F32, rider=None):
    tk, m = a.shape
    n = b.shape[1]
    slab = n // N_CHIPS
    per_block = bn // slab if chip_major else 1
    bk = max(c for c in range(LANES, min(bk, tk) + 1, LANES) if tk % c == 0)
    nk = tk // bk
    blk = (per_block, bm, slab) if chip_major else (bm, bn)

    def body(a_ref, b_ref, o_ref, acc_ref):
        k = pl.program_id(2)
        if chip_major:
            parts = [_dot_tn(a_ref[...], b_ref[:, s * slab:(s + 1) * slab]) for s in range(per_block)]
        else:
            parts = [_dot_tn(a_ref[...], b_ref[...])]

        @pl.when(k == 0)
        def _():
            for s, part in enumerate(parts):
                if chip_major:
                    acc_ref[s] = part
                else:
                    acc_ref[...] = part

        @pl.when(k > 0)
        def _():
            for s, part in enumerate(parts):
                if chip_major:
                    acc_ref[s] += part
                else:
                    acc_ref[...] += part

        @pl.when(k == nk - 1)
        def _():
            o_ref[...] = acc_ref[...].astype(out_dtype)

    if chip_major:
        out_spec = pl.BlockSpec(blk, lambda i, j, k: (j, i, 0))
        out_shape = jax.ShapeDtypeStruct((N_CHIPS, m, slab), out_dtype)
    else:
        out_spec = pl.BlockSpec(blk, lambda i, j, k: (i, j))
        out_shape = jax.ShapeDtypeStruct((m, n), out_dtype)
    (out,), carried = _hosted_call(
        body, (a, b), name=name, grid=(m // bm, n // bn, nk),
        in_specs=[pl.BlockSpec((bk, bm), lambda i, j, k: (k, i)), pl.BlockSpec((bk, bn), lambda i, j, k: (k, j))],
        out_specs=[out_spec], out_shape=[out_shape], scratch_shapes=[pltpu.VMEM(blk, F32)],
        sem=("parallel", "parallel", "arbitrary"), rider=rider)
    return out if rider is None else (out, carried)


_LATE = ("w_out", "w_ff1", "w_ff2")
_EARLY = ("w_in", "w_uq", "w_ukv")


def _local_step(x, ctx, tgt, modv, lg, g_attn, g_ffn, g_fin, g_ret, g_q, g_kv, w_in, w_uq, w_ukv, late, place=None,
                *, tm=256, tq=256, chunk=256):
    nex, seq, d = x.shape
    ctx_len = ctx.shape[1]
    t_lat = nex * seq
    x2 = x.reshape(t_lat, d)
    ctx2 = ctx.reshape(nex * ctx_len, d)
    tgt2 = tgt.reshape(t_lat, d)
    cos_t, sin_t = _rope_tables(seq, tm)
    dims = dict(t_lat=t_lat, seq=seq, ctx_len=ctx_len)
    alone = place is None

    (hb, pg, rq, rk, rv, nq, nkv, q, k, v), crossed = _pre_fwd(
        x2, ctx2, modv, g_attn, w_in, g_q, g_kv, w_uq, w_ukv, cos_t, sin_t, seq=seq, tm=tm,
        rider=None if alone else _gather_ici_rider([late[2]]))
    (yret, osum), got_ff2 = _ret_fwd_pair(rq, rk, rv, pg, lg, g_ret, chunk=chunk, **dims,
                                          rider=None if alone else _gather_d2d_rider(crossed))
    (ymla, lse), got_rest = _mla_fwd(q, k, v, tq=tq, **dims,
                                     rider=None if alone else _gather_rider([late[0], late[1]], staged=True))
    w_out, w_ff1, w_ff2 = late if alone else got_rest + got_ff2
    mix, act, du, h2, df, dmo, dmix, dxm, st_post = _post(yret, ymla, x2, tgt2, modv, g_ffn, g_fin, w_out.reshape(d, d),
                                                         w_ff1, w_ff2.reshape(D_FF, d), seq=seq, tm=tm)
    kw = dict(bm=1024, bn=1024, bk=1024, out_dtype=BF16)
    g_ff2 = _matmul_tn(act, df, chip_major=False, name="gw_ff2", **kw).reshape(N_CHIPS, D_FF // N_CHIPS, d)
    if alone:
        g_ff1 = _matmul_tn(h2, du, chip_major=True, name="gw_ff1", **kw)
        g_out = _matmul_tn(mix, dmo, chip_major=False, name="gw_out", **kw).reshape(N_CHIPS, d // N_CHIPS, d)
        (dq_m, dkl, dkc, dvl, dvc), _ = _mla_bwd(q, k, v, ymla, lse, dmix, tq=tq, **dims)
        (drq, drk, drv, dkc_r, dvc_r, drg, st_ret), _ = _ret_bwd_pair(rq, rk, rv, pg, osum, dmix, lg, g_ret, chunk=chunk,
                                                                      **dims)
        late_out = [g_out, g_ff1, g_ff2]
    else:
        core, slot = place
        g_ff1, x_ff2 = _matmul_tn(h2, du, chip_major=True, name="gw_ff1", rider=_exchange_rider([g_ff2]), **kw)
        g_out, x_ff1 = _matmul_tn(mix, dmo, chip_major=False, name="gw_out", rider=_exchange_rider([g_ff1]), **kw)
        g_out = g_out.reshape(N_CHIPS, d // N_CHIPS, d)
        p_ff2 = _add_half(g_ff2, x_ff2[0], core, "add_half_w_ff2")
        p_ff1 = _add_half(g_ff1, x_ff1[0], core, "add_half_w_ff1")
        (dq_m, dkl, dkc, dvl, dvc), (l_ff2, l_ff1, x_out) = _mla_bwd(
            q, k, v, ymla, lse, dmix, tq=tq, **dims,
            rider=_merge_riders(_scatter_rider([p_ff2, p_ff1]), _exchange_rider([g_out])))
        p_out = _add_half(g_out, x_out, core, "add_half_w_out")
        m_ff2 = _sum_chips(p_ff2, l_ff2, slot, "sum_chips_w_ff2")
        m_ff1 = _sum_chips(p_ff1, l_ff1, slot, "sum_chips_w_ff1")
        (drq, drk, drv, dkc_r, dvc_r, drg, st_ret), (l_out,) = _ret_bwd_pair(
            rq, rk, rv, pg, osum, dmix, lg, g_ret, chunk=chunk, **dims, rider=_scatter_rider([p_out]))
        late_out = [_sum_chips(p_out, l_out, slot, "sum_chips_w_out"), m_ff1, m_ff2]
    (dpb, dqf, dkvf, gx, st_pre), _ = _pre_bwd(
        x2, ctx2, modv, g_attn, pg, drq, drk, dkc_r, drv, dvc_r, drg, dq_m, dkl, dkc, dvl, dvc, dxm, w_in, g_q, g_kv,
        w_uq, w_ukv, cos_t, sin_t, seq=seq, tm=tm)
    g_early = [
        _matmul_tn(dpb, hb, bm=IN_PAD // 2, bn=d, bk=512, chip_major=False, name="gw_in"),
        _matmul_tn(dqf, nq, bm=HEADS * MLA_HEAD, bn=Q_LORA, bk=1536, chip_major=False, name="gw_uq"),
        _matmul_tn(nkv, dkvf, bm=KV_LORA, bn=HEADS * 256, bk=1536, chip_major=True, name="gw_ukv"),
    ]
    return gx.reshape(nex, seq, d), g_early, late_out, st_post, st_ret, st_pre


_ANY = pl.BlockSpec(memory_space=pl.ANY)
_VMEM = pl.BlockSpec(memory_space=pltpu.VMEM)
_OFFSETS = tuple((dx, dy, dc) for dx in (0, 1) for dy in (0, 1) for dc in (0, 1))[1:]
_CHIP_OFFSETS = ((1, 0), (0, 1), (1, 1))


def _place():
    return lax.axis_index("x"), lax.axis_index("y"), lax.axis_index("c")


def _flip(v, d):
    return 1 - v if d else v


def _gather8_rider(a, in_vmem=True):
    def copies(a_ref, o_ref, send, recv):
        x, y, z = _place()
        me = 4 * x + 2 * y + z
        out = []
        for k, (dx, dy, dc) in enumerate(_OFFSETS):
            peer = (_flip(x, dx), _flip(y, dy), _flip(z, dc))
            landing = o_ref.at[4 * peer[0] + 2 * peer[1] + peer[2]]
            out.append((
                pltpu.make_async_remote_copy(src_ref=a_ref, dst_ref=o_ref.at[me], send_sem=send.at[k],
                                             recv_sem=recv.at[k], device_id=peer, device_id_type=MESH),
                pltpu.make_async_remote_copy(src_ref=a_ref, dst_ref=landing, send_sem=send.at[k],
                                             recv_sem=recv.at[k], device_id=peer, device_id_type=MESH)))
        return me, out

    def start(ins, outs, sems):
        me, cps = copies(ins[0], outs[0], sems[0], sems[1])
        pltpu.make_async_copy(ins[0], outs[0].at[me], sems[2]).start()
        for out_cp, _ in cps:
            out_cp.start()

    def finish(ins, outs, sems):
        me, cps = copies(ins[0], outs[0], sems[0], sems[1])
        for out_cp, in_cp in cps:
            in_cp.wait_recv()
            out_cp.wait_send()
        pltpu.make_async_copy(ins[0], outs[0].at[me], sems[2]).wait()

    spec = [_VMEM] if in_vmem else [_ANY]
    return _Rider([a], [jax.ShapeDtypeStruct((N_DEV,) + a.shape, a.dtype)],
                  [pltpu.SemaphoreType.DMA((7,)), pltpu.SemaphoreType.DMA((7,)), pltpu.SemaphoreType.DMA],
                  start, finish, in_specs=spec, out_specs=spec)


def _merge_riders(*riders):
    ins, outs, sems, in_specs, out_specs, aliases, cuts = [], [], [], [], [], {}, []
    for r in riders:
        cuts.append((len(ins), len(outs), len(sems)))
        aliases.update({len(ins) + i: len(outs) + j for i, j in r.aliases.items()})
        ins += r.ins
        outs += r.out_shapes
        sems += r.sems
        in_specs += r.in_specs
        out_specs += r.out_specs

    def part(r, cut, r_ins, r_outs, r_sems):
        return (r_ins[cut[0]:cut[0] + len(r.ins)], r_outs[cut[1]:cut[1] + len(r.out_shapes)],
                r_sems[cut[2]:cut[2] + len(r.sems)])

    def start(r_ins, r_outs, r_sems):
        for r, cut in zip(riders, cuts):
            r.start(*part(r, cut, r_ins, r_outs, r_sems))

    def finish(r_ins, r_outs, r_sems):
        for r, cut in zip(riders, cuts):
            r.finish(*part(r, cut, r_ins, r_outs, r_sems))

    def middle(r_ins, r_outs, r_sems):
        for r, cut in zip(riders, cuts):
            if r.middle is not None:
                r.middle(*part(r, cut, r_ins, r_outs, r_sems))

    return _Rider(ins, outs, sems, start, finish, aliases=aliases, in_specs=in_specs, out_specs=out_specs,
                  middle=middle if any(r.middle is not None for r in riders) else None)


def _allgather8(a, name):
    return _run_rider(_gather8_rider(a), name)[0]


BF16_TILE_ROWS = 16


def _half(o, slot, which):
    r2 = o.shape[1] // 2
    if r2 % BF16_TILE_ROWS == 0:
        return o.at[slot, pl.ds(which * r2, r2)]
    c2 = o.shape[2] // 2
    assert c2 % LANES == 0
    return o.at[slot, :, pl.ds(which * c2, c2)]


def _gather_send(o_refs, send, recv):
    x, y, z = _place()
    chip = 2 * x + y
    for a, o in enumerate(o_refs):
        r2 = o.shape[1] // 2
        mine = _half(o, chip, z)
        for k, (dx, dy) in enumerate(_CHIP_OFFSETS):
            pltpu.make_async_remote_copy(
                src_ref=mine, dst_ref=mine, send_sem=send.at[a, k], recv_sem=recv.at[a, k],
                device_id=(_flip(x, dx), _flip(y, dy), z), device_id_type=MESH).start()


def _gather_landed(o_refs, send, recv, then=None):
    x, y, z = _place()
    chip = 2 * x + y
    for a, o in enumerate(o_refs):
        for k, (dx, dy) in enumerate(_CHIP_OFFSETS):
            landed = _half(o, 2 * _flip(x, dx) + _flip(y, dy), z)
            pltpu.make_async_remote_copy(
                src_ref=landed, dst_ref=landed, send_sem=send.at[a, k], recv_sem=recv.at[a, k],
                device_id=(_flip(x, dx), _flip(y, dy), z), device_id_type=MESH).wait_recv()
            if then is not None:
                then(a, k, landed)
    for a, o in enumerate(o_refs):
        mine = _half(o, chip, z)
        for k, (dx, dy) in enumerate(_CHIP_OFFSETS):
            pltpu.make_async_remote_copy(
                src_ref=mine, dst_ref=mine, send_sem=send.at[a, k], recv_sem=recv.at[a, k],
                device_id=(_flip(x, dx), _flip(y, dy), z), device_id_type=MESH).wait_send()


def _pass_on(o_refs, fsend, frecv, a, k, landed):
    x, y, z = _place()
    pltpu.make_async_remote_copy(
        src_ref=landed, dst_ref=landed, send_sem=fsend.at[a, k], recv_sem=frecv.at[a, k],
        device_id=(x, y, 1 - z), device_id_type=MESH).start()


def _passed_on(o_refs, fsend, frecv):
    x, y, z = _place()
    for a, o in enumerate(o_refs):
        for k, (dx, dy) in enumerate(_CHIP_OFFSETS):
            other = 2 * _flip(x, dx) + _flip(y, dy)
            got = _half(o, other, 1 - z)
            gave = _half(o, other, z)
            pltpu.make_async_remote_copy(
                src_ref=got, dst_ref=got, send_sem=fsend.at[a, k], recv_sem=frecv.at[a, k],
                device_id=(x, y, 1 - z), device_id_type=MESH).wait_recv()
            pltpu.make_async_remote_copy(
                src_ref=gave, dst_ref=gave, send_sem=fsend.at[a, k], recv_sem=frecv.at[a, k],
                device_id=(x, y, 1 - z), device_id_type=MESH).wait_send()


def _gather_finish(o_refs, send, recv, fsend, frecv):
    _gather_landed(o_refs, send, recv, functools.partial(_pass_on, o_refs, fsend, frecv))
    _passed_on(o_refs, fsend, frecv)


class _Rider:
    def __init__(self, ins, out_shapes, sems, start, finish, aliases=None, in_specs=None, out_specs=None, middle=None):
        self.ins, self.out_shapes, self.sems = list(ins), list(out_shapes), list(sems)
        self.start, self.finish, self.aliases = start, finish, dict(aliases or {})
        self.middle = middle
        self.in_specs = list(in_specs) if in_specs else [_ANY] * len(self.ins)
        self.out_specs = list(out_specs) if out_specs else [_ANY] * len(self.out_shapes)


def _run_rider(rider, name):
    r_in, r_out = len(rider.ins), len(rider.out_shapes)

    def body(*refs):
        ins, outs, sems = refs[:r_in], refs[r_in:r_in + r_out], refs[r_in + r_out:]
        rider.start(ins, outs, sems)
        if rider.middle is not None:
            rider.middle(ins, outs, sems)
        rider.finish(ins, outs, sems)

    return pl.pallas_call(
        body, name=name, in_specs=rider.in_specs, out_specs=rider.out_specs, out_shape=rider.out_shapes,
        input_output_aliases=rider.aliases, scratch_shapes=rider.sems,
    )(*rider.ins)


def _hosted_call(body, args, *, name, grid, in_specs, out_specs, out_shape, scratch_shapes=(), sem, rider=None,
                 prefetch=()):
    scratch_shapes = list(scratch_shapes)
    n_pf, n_in, n_out, n_sc = len(prefetch), len(in_specs), len(out_specs), len(scratch_shapes)
    r_in, r_out = (len(rider.ins), len(rider.out_shapes)) if rider else (0, 0)
    last = tuple(g - 1 for g in grid)

    def hosted(*refs):
        p = 0
        parts = []
        for cnt in (n_pf, n_in, r_in, n_out, r_out, n_sc):
            parts.append(refs[p:p + cnt])
            p += cnt
        pf, ins, r_ins, outs, r_outs, scratch = parts
        sems = refs[p:]
        ids = [pl.program_id(a) for a in range(len(grid))]
        is_first = functools.reduce(jnp.logical_and, [i == 0 for i in ids])
        is_last = functools.reduce(jnp.logical_and, [i == e for i, e in zip(ids, last)])

        @pl.when(is_first)
        def _():
            rider.start(r_ins, r_outs, sems)

        if rider.middle is not None:
            linear = functools.reduce(lambda acc, ig: acc * ig[1] + ig[0], zip(ids, grid), 0)

            @pl.when(linear == math.prod(grid) * 3 // 4)
            def _():
                rider.middle(r_ins, r_outs, sems)

        body(*pf, *ins, *outs, *scratch)

        @pl.when(is_last)
        def _():
            rider.finish(r_ins, r_outs, sems)

    if rider is None:
        kern, all_in, all_out, shapes, scratch, aliases, extra = body, list(in_specs), list(out_specs), list(out_shape), \
            scratch_shapes, {}, []
    else:
        kern, all_in, all_out = hosted, list(in_specs) + rider.in_specs, list(out_specs) + rider.out_specs
        shapes, scratch, extra = list(out_shape) + rider.out_shapes, scratch_shapes + rider.sems, rider.ins
        aliases = {n_pf + n_in + i: n_out + j for i, j in rider.aliases.items()}
        sem = ("arbitrary",) * len(grid)
    if prefetch:
        spec = dict(grid_spec=pltpu.PrefetchScalarGridSpec(
            num_scalar_prefetch=n_pf, grid=grid, in_specs=all_in, out_specs=all_out, scratch_shapes=scratch))
    else:
        spec = dict(grid=grid, in_specs=all_in, out_specs=all_out, scratch_shapes=scratch)
    res = pl.pallas_call(kern, name=name, out_shape=shapes, input_output_aliases=aliases,
                         compiler_params=_params(sem, VMEM_LIMIT), **spec)(*prefetch, *args, *extra)
    return list(res[:n_out]), list(res[n_out:])


def _gather_rider(ws, staged=False):
    n = len(ws)
    shapes = [jax.ShapeDtypeStruct(w.shape, w.dtype) for w in ws]
    sems = [pltpu.SemaphoreType.DMA((n, 3))] * 4
    aliases = {a: a for a in range(n)}

    def start(ins, outs, s):
        _gather_send(outs, s[0], s[1])

    if not staged:
        return _Rider(ws, shapes, sems, start, lambda ins, outs, s: _gather_finish(outs, *s), aliases=aliases)
    return _Rider(
        ws, shapes, sems, start, lambda ins, outs, s: _passed_on(outs, s[2], s[3]), aliases=aliases,
        middle=lambda ins, outs, s: _gather_landed(outs, s[0], s[1], functools.partial(_pass_on, outs, s[2], s[3])))


def _gather_ici_rider(ws):
    n = len(ws)
    return _Rider(
        ws, [jax.ShapeDtypeStruct(w.shape, w.dtype) for w in ws], [pltpu.SemaphoreType.DMA((n, 3))] * 2,
        lambda ins, outs, sems: _gather_send(outs, sems[0], sems[1]),
        lambda ins, outs, sems: _gather_landed(outs, sems[0], sems[1]),
        aliases={a: a for a in range(n)})


def _gather_d2d_rider(ws):
    n = len(ws)

    def start(ins, outs, sems):
        x, y, z = _place()
        for a, o in enumerate(outs):
            for k, (dx, dy) in enumerate(_CHIP_OFFSETS):
                _pass_on(outs, sems[0], sems[1], a, k, _half(o, 2 * _flip(x, dx) + _flip(y, dy), z))

    return _Rider(
        ws, [jax.ShapeDtypeStruct(w.shape, w.dtype) for w in ws], [pltpu.SemaphoreType.DMA((n, 3))] * 2,
        start, lambda ins, outs, sems: _passed_on(outs, sems[0], sems[1]), aliases={a: a for a in range(n)})


def _copies_rider(ins, out_shapes, sem_shape, make):
    def start(r_ins, r_outs, sems):
        for cp in make(r_ins, r_outs, sems[0], sems[1]):
            cp.start()

    def finish(r_ins, r_outs, sems):
        for cp in make(r_ins, r_outs, sems[0], sems[1]):
            cp.wait()

    return _Rider(ins, out_shapes, [pltpu.SemaphoreType.DMA(sem_shape)] * 2, start, finish)


def _exchange_rider(gs):
    def make(g_refs, r_refs, send, recv):
        x, y, z = _place()
        return [pltpu.make_async_remote_copy(
            src_ref=g.at[:, pl.ds((1 - z) * (g.shape[1] // 2), g.shape[1] // 2)], dst_ref=r, send_sem=send.at[a],
            recv_sem=recv.at[a], device_id=(x, y, 1 - z), device_id_type=MESH)
            for a, (g, r) in enumerate(zip(g_refs, r_refs))]

    shapes = [jax.ShapeDtypeStruct((g.shape[0], g.shape[1] // 2, g.shape[2]), g.dtype) for g in gs]
    return _copies_rider(gs, shapes, (len(gs),), make)


def _add_half(g, recv, core, name):
    s, r, c = g.shape
    r2 = r // 2
    rb = r2
    for cand in (256, 128, 64):
        if r2 % cand == 0:
            rb = cand
            break
    g4 = g.reshape(s, 2, r2, c)

    def body(core_ref, g_ref, r_ref, o_ref):
        o_ref[...] = (g_ref[...].astype(F32) + r_ref[...].astype(F32)).astype(BF16)

    return pl.pallas_call(
        body, name=name,
        grid_spec=pltpu.PrefetchScalarGridSpec(
            num_scalar_prefetch=1, grid=(s, r2 // rb),
            in_specs=[pl.BlockSpec((None, None, rb, c), lambda i, j, cr: (i, cr[0], j, 0)),
                      pl.BlockSpec((None, rb, c), lambda i, j, cr: (i, j, 0))],
            out_specs=pl.BlockSpec((None, rb, c), lambda i, j, cr: (i, j, 0))),
        out_shape=jax.ShapeDtypeStruct((s, r2, c), BF16),
        compiler_params=_params(("parallel", "parallel")),
    )(core, g4, recv)


def _scatter_rider(ps):
    def make(p_refs, o_refs, send, recv):
        x, y, z = _place()
        copies = []
        for a, (p, o) in enumerate(zip(p_refs, o_refs)):
            for k, (dx, dy) in enumerate(_CHIP_OFFSETS):
                other = 2 * _flip(x, dx) + _flip(y, dy)
                copies.append(pltpu.make_async_remote_copy(
                    src_ref=p.at[other], dst_ref=o.at[k], send_sem=send.at[a, k], recv_sem=recv.at[a, k],
                    device_id=(_flip(x, dx), _flip(y, dy), z), device_id_type=MESH))
        return copies

    shapes = [jax.ShapeDtypeStruct((3,) + p.shape[1:], p.dtype) for p in ps]
    return _copies_rider(ps, shapes, (len(ps), 3), make)


def _sum_chips(p, landed, chip, name):
    _, r2, c = p.shape
    rb = r2
    for cand in (256, 128, 64):
        if r2 % cand == 0:
            rb = cand
            break

    def body(s_ref, p_ref, l_ref, o_ref):
        acc = p_ref[...].astype(F32)
        for k in range(3):
            acc = acc + l_ref[k].astype(F32)
        o_ref[...] = acc

    return pl.pallas_call(
        body, name=name,
        grid_spec=pltpu.PrefetchScalarGridSpec(
            num_scalar_prefetch=1, grid=(r2 // rb,),
            in_specs=[pl.BlockSpec((None, rb, c), lambda i, s: (s[0], i, 0)),
                      pl.BlockSpec((3, rb, c), lambda i, s: (0, i, 0))],
            out_specs=pl.BlockSpec((rb, c), lambda i, s: (i, 0))),
        out_shape=jax.ShapeDtypeStruct((r2, c), F32),
        compiler_params=_params(("parallel",)),
    )(chip, p, landed)


def _swap_rider(hs):
    def make(h_refs, o_refs, send, recv):
        x, y, z = _place()
        return [pltpu.make_async_remote_copy(
            src_ref=h, dst_ref=o, send_sem=send.at[a], recv_sem=recv.at[a], device_id=(x, y, 1 - z),
            device_id_type=MESH) for a, (h, o) in enumerate(zip(h_refs, o_refs))]

    return _copies_rider(hs, [jax.ShapeDtypeStruct(h.shape, h.dtype) for h in hs], (len(hs),), make)


def _reduce_scatter_vmem(gs, rows, rider, name):
    n = len(gs)
    r_in, r_out = len(rider.ins), len(rider.out_shapes)
    halves = [(r // 2, g.shape[-1]) for g, (r, _) in zip(gs, rows)]

    def body(*refs):
        p = 0
        parts = []
        for cnt in (n, r_in, n, n, r_out, n, n, n, 6):
            parts.append(refs[p:p + cnt])
            p += cnt
        g_refs, r_ins, mine, theirs, r_outs, recv, part, land, sems = parts
        r_sems = refs[p:]
        xs, xr, ss, sr, ws, wr = sems
        x, y, z = _place()
        chip = 2 * x + y
        sib = (x, y, 1 - z)
        rider.start(r_ins, r_outs, r_sems)

        def half_of(a, s, which):
            r2 = halves[a][0]
            if len(g_refs[a].shape) == 3:
                return g_refs[a].at[s, pl.ds(pl.multiple_of(which * r2, 8), r2)]
            return g_refs[a].at[pl.ds(pl.multiple_of(s * rows[a][1] + which * r2, 8), r2)]

        exchange = [pltpu.make_async_remote_copy(
            src_ref=half_of(a, s, 1 - z), dst_ref=recv[a].at[s], send_sem=xs.at[a, s], recv_sem=xr.at[a, s],
            device_id=sib, device_id_type=MESH) for a in range(n) for s in range(N_CHIPS)]
        for cp in exchange:
            cp.start()
        for cp in exchange:
            cp.wait()
        for a in range(n):
            for s in range(N_CHIPS):
                part[a][s] = (half_of(a, s, z)[...] + recv[a][s]).astype(BF16)
        scatter = []
        for a in range(n):
            for k, (dx, dy) in enumerate(_CHIP_OFFSETS):
                other = 2 * _flip(x, dx) + _flip(y, dy)
                scatter.append(pltpu.make_async_remote_copy(
                    src_ref=part[a].at[other], dst_ref=land[a].at[k], send_sem=ss.at[a, k], recv_sem=sr.at[a, k],
                    device_id=(_flip(x, dx), _flip(y, dy), z), device_id_type=MESH))
        for cp in scatter:
            cp.start()
        for cp in scatter:
            cp.wait()
        for a in range(n):
            acc = part[a][chip].astype(F32)
            for k in range(3):
                acc = acc + land[a][k].astype(F32)
            mine[a][...] = acc
        swap = [pltpu.make_async_remote_copy(
            src_ref=mine[a], dst_ref=theirs[a], send_sem=ws.at[a], recv_sem=wr.at[a], device_id=sib,
            device_id_type=MESH) for a in range(n)]
        for cp in swap:
            cp.start()
        for cp in swap:
            cp.wait()
        rider.finish(r_ins, r_outs, r_sems)

    half_shapes = [jax.ShapeDtypeStruct(h, F32) for h in halves]
    res = pl.pallas_call(
        body, name=name, in_specs=[_VMEM] * n + rider.in_specs, out_specs=[_VMEM] * (2 * n) + rider.out_specs,
        out_shape=half_shapes + half_shapes + rider.out_shapes,
        scratch_shapes=[pltpu.VMEM((N_CHIPS,) + h, F32) for h in halves] + [pltpu.VMEM((N_CHIPS,) + h, BF16) for h in halves]
        + [pltpu.VMEM((3,) + h, BF16) for h in halves]
        + [pltpu.SemaphoreType.DMA((n, N_CHIPS))] * 2 + [pltpu.SemaphoreType.DMA((n, 3))] * 2
        + [pltpu.SemaphoreType.DMA((n,))] * 2 + rider.sems,
        input_output_aliases={n + i: 2 * n + j for i, j in rider.aliases.items()},
        compiler_params=_params(None, VMEM_LIMIT),
    )(*gs, *rider.ins)
    return list(res[:n]), list(res[n:2 * n]), list(res[2 * n:])


SMALL_ROWS = 32
PACK_ROWS = 16


def _pack_small(st_post, st_ret, st_pre):
    d = st_post.shape[2]

    def body(po_ref, re_ref, pr_ref, o_ref):
        o_ref[...] = jnp.zeros(o_ref.shape, F32)
        o_ref[0:1, :] = pr_ref[0, 2:3, :] + pr_ref[1, 2:3, :] + pr_ref[2, 2:3, :]
        o_ref[1:2, :] = po_ref[0, 4:5, :] + po_ref[1, 4:5, :]
        o_ref[2:3, :] = po_ref[0, 5:6, :] + po_ref[1, 5:6, :]
        o_ref[3:4, 0:512] = re_ref[0, 0:1, :] + re_ref[1, 0:1, :]
        o_ref[4:5, :] = pr_ref[0, 3:4, :] + pr_ref[1, 3:4, :] + pr_ref[2, 3:4, :]
        o_ref[5:6, :] = pr_ref[0, 4:5, :] + pr_ref[1, 4:5, :] + pr_ref[2, 4:5, :]
        lane = lax.broadcasted_iota(jnp.int32, (1, LANES), 1)
        for row, src in ((6, 1), (10, 2)):
            acc = jnp.zeros((1, LANES), F32)
            for hd in range(HEADS):
                grp = re_ref[0, src:src + 1, hd * LANES:(hd + 1) * LANES] + re_ref[1, src:src + 1, hd * LANES:(hd + 1) * LANES]
                acc = acc + jnp.where(lane == hd, grp, 0.0)
            o_ref[row:row + 1, 0:LANES] = acc
        o_ref[7:8, :] = po_ref[0, 6:7, :] + po_ref[1, 6:7, :]
        o_ref[8:9, :] = pr_ref[2, 0:1, :]
        o_ref[9:10, :] = pr_ref[2, 1:2, :]
        for e in range(2):
            b = 12 + 6 * e
            o_ref[b:b + 1, :] = pr_ref[e, 0:1, :]
            o_ref[b + 1:b + 2, :] = pr_ref[e, 1:2, :]
            o_ref[b + 2:b + 3, :] = po_ref[e, 3:4, :]
            o_ref[b + 3:b + 4, :] = po_ref[e, 0:1, :]
            o_ref[b + 4:b + 5, :] = po_ref[e, 1:2, :]
            o_ref[b + 5:b + 6, :] = po_ref[e, 2:3, :]

    return pl.pallas_call(body, name="pack_small", out_shape=jax.ShapeDtypeStruct((SMALL_ROWS, d), F32))(st_post, st_ret, st_pre)


def _small_reduce(gathered):
    d = gathered.shape[2]

    def body(g_ref, o_ref):
        tot = g_ref[0, 0:PACK_ROWS, :]
        for dev in range(1, N_DEV):
            tot = tot + g_ref[dev, 0:PACK_ROWS, :]
        o_ref[0:PACK_ROWS, :] = tot
        for j in range(6):
            acc = g_ref[0, 12 + j:13 + j, :] + g_ref[0, 18 + j:19 + j, :]
            for dev in range(1, N_DEV):
                acc = acc + g_ref[dev, 12 + j:13 + j, :] + g_ref[dev, 18 + j:19 + j, :]
            if j < 2:
                acc = acc + o_ref[8 + j:9 + j, :]
            o_ref[PACK_ROWS + j:PACK_ROWS + j + 1, :] = acc
        o_ref[PACK_ROWS + 6:PACK_ROWS + 8, :] = jnp.zeros((2, d), F32)

    return pl.pallas_call(body, name="small_reduce", out_shape=jax.ShapeDtypeStruct((PACK_ROWS + 8, d), F32))(gathered)


_SMALL = (("g_attn", 0, 1024), ("g_ffn", 1, 1024), ("g_final", 2, 1024), ("g_ret", 3, 512), ("g_q_lora", 4, 384),
          ("g_kv_lora", 5, 256), ("ret_decay_fwd", 6, HEADS), ("ret_decay_bwd", 10, HEADS))
_SMALL_NAMES = tuple(s[0] for s in _SMALL) + ("c_ctx", "b_ada")


def _small_final(tot, dcc, sg8, ws, ms, vs):
    d = tot.shape[1]
    n = len(_SMALL_NAMES)

    def body(*refs):
        t_ref, dcc_ref, sg_ref = refs[0:3]
        w_refs, m_refs, v_refs = refs[3:3 + n], refs[3 + n:3 + 2 * n], refs[3 + 2 * n:3 + 3 * n]
        outs = refs[3 + 3 * n:]
        g_refs, d_refs, mo_refs, vo_refs = outs[0:n], outs[n:2 * n], outs[2 * n:3 * n], outs[3 * n:4 * n]
        l_ref = outs[4 * n]

        def update(i, g, sl=None):
            pick = (lambda r: r[...]) if sl is None else (lambda r: r[:, sl])
            dl, mn, vn = _adam_math(pick(w_refs[i]), g, pick(m_refs[i]), pick(v_refs[i]))
            if sl is None:
                g_refs[i][...], d_refs[i][...], mo_refs[i][...], vo_refs[i][...] = g, dl, mn, vn
            else:
                g_refs[i][:, sl], d_refs[i][:, sl], mo_refs[i][:, sl], vo_refs[i][:, sl] = g, dl, mn, vn

        for i, (name, row, width) in enumerate(_SMALL):
            g = t_ref[row:row + 1, 0:width]
            if name == "ret_decay_fwd":
                g = g * sg_ref[0:1, 0:width]
            elif name == "ret_decay_bwd":
                g = g * sg_ref[1:2, 0:width]
            update(i, g)
        i_cc, i_b = n - 2, n - 1
        cc = w_refs[i_cc][...]
        s = 1.0 / (1.0 + jnp.exp(-cc))
        dsilu = dcc_ref[0, 0:1, :] + dcc_ref[2, 0:1, :] + dcc_ref[4, 0:1, :] + dcc_ref[6, 0:1, :]
        update(i_cc, dsilu * (s * (1.0 + cc * (1.0 - s))))
        for j in range(6):
            update(i_b, t_ref[PACK_ROWS + j:PACK_ROWS + j + 1, :], pl.ds(j * d, d))
        l_ref[...] = jnp.broadcast_to((0.5 / d) * jnp.sum(t_ref[7:8, :], keepdims=True), l_ref.shape)

    shapes = [jax.ShapeDtypeStruct(a.shape, F32) for a in ws]
    outs = pl.pallas_call(
        body, name="small_final", out_shape=shapes * 4 + [jax.ShapeDtypeStruct((8, LANES), F32)],
    )(tot, dcc, sg8, *ws, *ms, *vs)
    return outs[0:n], outs[n:2 * n], outs[2 * n:3 * n], outs[3 * n:4 * n], outs[4 * n]


_WEIGHTS = ("c_ctx", "w_ada", "b_ada", "g_attn", "g_ffn", "w_in", "ret_decay_fwd", "ret_decay_bwd", "g_ret", "g_q_lora",
            "w_uq", "g_kv_lora", "w_ukv", "w_out", "w_ff1", "w_ff2", "g_final")
_BIG = ("w_in", "w_uq", "w_ukv", "w_out", "w_ff1", "w_ff2")
_TRANSPOSED = ("w_in", "w_uq")


def kernel(x, c, ctx, c_ctx, w_ada, b_ada, g_attn, g_ffn, w_in, ret_decay_fwd, ret_decay_bwd, g_ret, g_q_lora, w_uq, g_kv_lora, w_ukv, w_out, w_ff1, w_ff2, g_final, loss_target, m_c_ctx, m_w_ada, m_b_ada, m_g_attn, m_g_ffn, m_w_in, m_ret_decay_fwd, m_ret_decay_bwd, m_g_ret, m_g_q_lora, m_w_uq, m_g_kv_lora, m_w_ukv, m_w_out, m_w_ff1, m_w_ff2, m_g_final, v_c_ctx, v_w_ada, v_b_ada, v_g_attn, v_g_ffn, v_w_in, v_ret_decay_fwd, v_ret_decay_bwd, v_g_ret, v_g_q_lora, v_w_uq, v_g_kv_lora, v_w_ukv, v_w_out, v_w_ff1, v_w_ff2, v_g_final):
    w = dict(c_ctx=c_ctx, w_ada=w_ada, b_ada=b_ada, g_attn=g_attn, g_ffn=g_ffn, w_in=w_in, ret_decay_fwd=ret_decay_fwd,
             ret_decay_bwd=ret_decay_bwd, g_ret=g_ret, g_q_lora=g_q_lora, w_uq=w_uq, g_kv_lora=g_kv_lora, w_ukv=w_ukv,
             w_out=w_out, w_ff1=w_ff1, w_ff2=w_ff2, g_final=g_final)
    m = dict(c_ctx=m_c_ctx, w_ada=m_w_ada, b_ada=m_b_ada, g_attn=m_g_attn, g_ffn=m_g_ffn, w_in=m_w_in,
             ret_decay_fwd=m_ret_decay_fwd, ret_decay_bwd=m_ret_decay_bwd, g_ret=m_g_ret, g_q_lora=m_g_q_lora, w_uq=m_w_uq,
             g_kv_lora=m_g_kv_lora, w_ukv=m_w_ukv, w_out=m_w_out, w_ff1=m_w_ff1, w_ff2=m_w_ff2, g_final=m_g_final)
    v = dict(c_ctx=v_c_ctx, w_ada=v_w_ada, b_ada=v_b_ada, g_attn=v_g_attn, g_ffn=v_g_ffn, w_in=v_w_in,
             ret_decay_fwd=v_ret_decay_fwd, ret_decay_bwd=v_ret_decay_bwd, g_ret=v_g_ret, g_q_lora=v_g_q_lora, w_uq=v_w_uq,
             g_kv_lora=v_g_kv_lora, w_ukv=v_w_ukv, w_out=v_w_out, w_ff1=v_w_ff1, w_ff2=v_w_ff2, g_final=v_g_final)
    xi, yi, ci = lax.axis_index("x"), lax.axis_index("y"), lax.axis_index("c")
    chip = 2 * xi + yi
    dev = 2 * chip + ci
    nex, seq, d = x.shape
    n_ada = w_ada.shape[2]

    dec = jnp.zeros((8, LANES), F32).at[0, :HEADS].set(ret_decay_fwd[0]).at[1, :HEADS].set(ret_decay_bwd[0])
    lg8, sg8 = _decay_prep(dec)
    lg = lg8[:2, :HEADS]

    def shard_of(t, k):
        return t[k][0].T if k in _TRANSPOSED else t[k][0]

    shard = {k: shard_of(w, k) for k in _BIG}
    head_rows = MLA_NOPE + MLA_ROPE
    shard["w_uq"] = jnp.pad(shard["w_uq"], ((0, MLA_HEAD - head_rows), (0, 0)))
    slot = chip.reshape(1).astype(jnp.int32)
    core = ci.reshape(1).astype(jnp.int32)
    slots = {k: _cast_into_slot(shard[k], slot, "cast_" + k)[0] for k in _BIG if k != "w_ff1"}
    slots["w_ff1"], (w_in_x, w_uq_x, w_ukv_x, c8) = _cast_into_slot(
        shard["w_ff1"], slot, "cast_w_ff1",
        rider=_merge_riders(_gather_ici_rider([slots[k] for k in _EARLY]),
                            _gather8_rider(jnp.pad(c, ((0, 8 - nex), (0, 0))), in_vmem=False)))

    a_in = jnp.concatenate([c8[:, :nex].reshape(N_DEV * nex, d), c_ctx.reshape(1, d), jnp.zeros((7, d), F32)], axis=0)
    b_sh = lax.dynamic_slice(b_ada, (0, chip * n_ada), (1, n_ada))
    mod_sh = _mod_fwd(a_in, w_ada[0], b_sh)
    mod8, w_in_f, w_uq_k, w_ukv_k = _run_rider(
        _merge_riders(_gather8_rider(mod_sh), _gather_d2d_rider([w_in_x, w_uq_x, w_ukv_x])), "ag_early")
    w_in_k = jnp.pad(w_in_f.reshape(IN_COLS, d), ((0, IN_PAD - IN_COLS), (0, 0)))
    mod_all = mod8[0::2].transpose(1, 0, 2).reshape(a_in.shape[0], N_CHIPS * n_ada)
    mod_me = lax.dynamic_slice(mod_all, (nex * dev, 0), (nex, N_CHIPS * n_ada)).reshape(nex, 6, d)
    mod_c = mod_all[N_DEV * nex].reshape(1, 6, d)
    modv = jnp.pad(jnp.concatenate([mod_me, mod_c], axis=0), ((0, 0), (0, 2), (0, 0)))

    gx, g_early, late, st_post, st_ret, st_pre = _local_step(
        x, ctx, loss_target, modv, lg, g_attn, g_ffn, g_final.reshape(1, d), g_ret, g_q_lora, g_kv_lora,
        w_in_k, w_uq_k, w_ukv_k, [slots[k] for k in _LATE], (core, slot))

    mine, theirs, (*late_theirs, gathered) = _reduce_scatter_vmem(
        g_early, [(IN_COLS // N_CHIPS, IN_COLS // N_CHIPS), (head_rows, MLA_HEAD), (KV_LORA, KV_LORA)],
        _merge_riders(_swap_rider(late), _gather8_rider(_pack_small(st_post, st_ret, st_pre))), "rs_early")
    tot = _small_reduce(gathered)
    dm = jnp.concatenate([
        gathered[:, 12:24].reshape(N_DEV * nex, 6 * d),
        jnp.concatenate([tot[8:10].reshape(1, 2 * d), jnp.zeros((1, 4 * d), F32)], axis=1),
        jnp.zeros((7, 6 * d), F32)], axis=0)
    dm_sh = lax.dynamic_slice(dm, (0, chip * n_ada), (dm.shape[0], n_ada))
    g_ada, da = _mod_bwd(a_in, dm_sh, w_ada[0])
    dcc = _allgather8(da[N_DEV * nex:], "ag_dcc")
    halves = dict(zip(_EARLY, zip(mine, theirs)))
    halves.update(zip(_LATE, zip(late, late_theirs)))
    grad, delta, new_m, new_v = {}, {}, {}, {}
    for k in _BIG:
        a, b = halves[k]
        res = _adamw_halves(shard_of(w, k), a, b, shard_of(m, k), shard_of(v, k), core, "adamw_" + k)
        grad[k], delta[k], new_m[k], new_v[k] = [(o.T if k in _TRANSPOSED else o).reshape(w[k].shape) for o in res]

    shp = w_ada.shape
    outs, _ = _adamw(w_ada[0], g_ada, m["w_ada"][0], v["w_ada"][0], "adamw_w_ada")
    grad["w_ada"] = g_ada.reshape(shp)
    delta["w_ada"], new_m["w_ada"], new_v["w_ada"] = [o.reshape(shp) for o in outs]
    rows = [{k: t[k].reshape(1, -1) for k in _SMALL_NAMES} for t in (w, m, v)]
    small = _small_final(tot, dcc, sg8, *[[t[k] for k in _SMALL_NAMES] for t in rows])
    for res, outs in zip((grad, delta, new_m, new_v), small[:4]):
        for k, o in zip(_SMALL_NAMES, outs):
            res[k] = o.reshape(w[k].shape)
    return (small[4][0, 0], gx, *[grad[k] for k in _WEIGHTS], *[delta[k] for k in _WEIGHTS],
            *[new_m[k] for k in _WEIGHTS], *[new_v[k] for k in _WEIGHTS])
```

```python
import functools
import math

import jax
import jax.numpy as jnp
from jax import lax
from jax.experimental import pallas as pl
from jax.experimental.pallas import tpu as pltpu

F32 = jnp.float32
BF16 = jnp.bfloat16
MESH = pl.DeviceIdType.MESH

EPS = 1e-6
D_MODEL = 1024
D_FF = 4096
HEADS = 4
RET_DK = 64
RET_DV = 128
MLA_NOPE = 128
MLA_ROPE = 64
MLA_HEAD = 256
Q_LORA = 384
KV_LORA = 256
GRID_W = 64
ROPE_BASE = 10000.0
IN_COLS = 2240
IN_PAD = 2304
PG_COLS = 1152
N_CHIPS = 4
N_DEV = 8
LANES = 128
ADAM_LR = 0.001
ADAM_B1 = 0.9
ADAM_B2 = 0.999
ADAM_EPS = 1e-08
ADAM_WD = 0.01
ADAM_STEP = 10
VMEM_LIMIT = 56 * 1024 * 1024


def _dot(a, b):
    return jnp.dot(a, b, preferred_element_type=F32)


def _dot_nt(a, b):
    return lax.dot_general(a, b, (((1,), (1,)), ((), ())), preferred_element_type=F32)


def _dot_tn(a, b):
    return lax.dot_general(a, b, (((0,), (0,)), ((), ())), preferred_element_type=F32)


def _params(sem=None, vmem=None):
    return pltpu.CompilerParams(dimension_semantics=sem, vmem_limit_bytes=vmem)


def _full(shape):
    n = len(shape)
    return pl.BlockSpec(shape, lambda *_: (0,) * n)


def _rope(x, cos, sin):
    w = x.shape[-1]
    lo = (lax.broadcasted_iota(jnp.int32, (1, w), 1) % 64) < 32
    swapped = jnp.where(lo, pltpu.roll(x, w - 32, 1), pltpu.roll(x, 32, 1))
    return x * cos + swapped * sin


def _rope_t(g, cos, sin):
    w = g.shape[-1]
    lo = (lax.broadcasted_iota(jnp.int32, (1, w), 1) % 64) < 32
    t = g * sin
    swapped = jnp.where(lo, pltpu.roll(t, w - 32, 1), pltpu.roll(t, 32, 1))
    return g * cos + swapped


def _rope_tables(seq, tm):
    rows = seq // GRID_W
    row = jnp.repeat(jnp.arange(rows, dtype=F32), GRID_W)
    col = jnp.tile(jnp.arange(GRID_W, dtype=F32), rows)
    n_freq = RET_DK // 4
    freq = ROPE_BASE ** (-jnp.arange(n_freq, dtype=F32) / n_freq)
    ang = jnp.concatenate([row[:, None] * freq, col[:, None] * freq], axis=-1)
    cos, sin = jnp.cos(ang), jnp.sin(ang)
    cos_t = jnp.tile(jnp.concatenate([cos, cos], -1), (1, HEADS))
    sin_t = jnp.tile(jnp.concatenate([-sin, sin], -1), (1, HEADS))
    cos_t = jnp.concatenate([cos_t, jnp.ones((tm, 4 * RET_DK), F32)], 0)
    sin_t = jnp.concatenate([sin_t, jnp.zeros((tm, 4 * RET_DK), F32)], 0)
    return cos_t, sin_t


def _adam_math(w, g, m, v):
    mn = ADAM_B1 * m + (1.0 - ADAM_B1) * g
    vn = ADAM_B2 * v + (1.0 - ADAM_B2) * (g * g)
    m_hat = mn / (1.0 - ADAM_B1 ** ADAM_STEP)
    v_hat = vn / (1.0 - ADAM_B2 ** ADAM_STEP)
    return -ADAM_LR * (m_hat / (jnp.sqrt(v_hat) + ADAM_EPS) + ADAM_WD * w), mn, vn


def _cast_into_slot(w, slot, name, rider=None):
    r, c = w.shape
    rb = max(b for b in range(16, 257, 16) if r % b == 0)

    def body(s_ref, w_ref, o_ref):
        o_ref[...] = w_ref[...].astype(BF16)

    (out,), carried = _hosted_call(
        body, (w,), name=name, grid=(r // rb,), prefetch=(slot,),
        in_specs=[pl.BlockSpec((rb, c), lambda i, s: (i, 0))],
        out_specs=[pl.BlockSpec((None, rb, c), lambda i, s: (s[0], i, 0))],
        out_shape=[jax.ShapeDtypeStruct((N_CHIPS, r, c), BF16)], sem=("parallel",), rider=rider)
    return out, carried


def _adamw_halves(w, mine, theirs, m, v, core, name):
    r, c = w.shape
    r2 = r // 2
    rb = max(b for b in range(8, r2 + 1, 8) if r2 % b == 0 and b * c * 4 <= (1 << 21))
    nbh = r2 // rb

    def body(z_ref, w_ref, a_ref, b_ref, m_ref, v_ref, g_ref, d_ref, mo_ref, vo_ref):
        here = (pl.program_id(0) // nbh) == z_ref[0]
        gg = jnp.where(here, a_ref[...], b_ref[...])
        g_ref[...] = gg
        d_ref[...], mo_ref[...], vo_ref[...] = _adam_math(w_ref[...], gg, m_ref[...], v_ref[...])

    spec = pl.BlockSpec((rb, c), lambda i, z: (i, 0))
    a_spec = pl.BlockSpec((rb, c), lambda i, z: (jnp.clip(i - z[0] * nbh, 0, nbh - 1), 0))
    b_spec = pl.BlockSpec((rb, c), lambda i, z: (jnp.clip(i - (1 - z[0]) * nbh, 0, nbh - 1), 0))
    shp = jax.ShapeDtypeStruct((r, c), F32)
    return pl.pallas_call(
        body, name=name,
        grid_spec=pltpu.PrefetchScalarGridSpec(
            num_scalar_prefetch=1, grid=(r // rb,), in_specs=[spec, a_spec, b_spec, spec, spec], out_specs=[spec] * 4),
        out_shape=[shp] * 4,
        compiler_params=_params(("parallel",)),
    )(core, w, mine, theirs, m, v)


def _adamw(w, g, m, v, name, rider=None):
    r, c = w.shape
    rb = r
    for cand in (256, 128, 64, 32, 16, 8):
        if r % cand == 0 and cand * c * 4 <= (1 << 20):
            rb = cand
            break
    if r * c * 4 <= (1 << 20):
        rb = r

    def body(w_ref, g_ref, m_ref, v_ref, d_ref, mo_ref, vo_ref):
        d_ref[...], mo_ref[...], vo_ref[...] = _adam_math(w_ref[...], g_ref[...], m_ref[...], v_ref[...])

    spec = pl.BlockSpec((rb, c), lambda i: (i, 0))
    shp = jax.ShapeDtypeStruct((r, c), F32)
    return _hosted_call(
        body, (w, g, m, v), name=name, grid=(r // rb,), in_specs=[spec] * 4, out_specs=[spec] * 3, out_shape=[shp] * 3,
        sem=("parallel",), rider=rider)


def _decay_prep(dec):
    def body(d_ref, lg_ref, sg_ref):
        d = d_ref[...]
        lg_ref[...] = jnp.minimum(d, 0.0) - jnp.log(1.0 + jnp.exp(-jnp.abs(d)))
        sg_ref[...] = 1.0 / (1.0 + jnp.exp(d))

    shp = jax.ShapeDtypeStruct(dec.shape, F32)
    return pl.pallas_call(body, name="decay_prep", out_shape=[shp, shp])(dec)


def _mod_fwd(a_in, w_ada, b_sh):
    rows, d = a_in.shape
    n = w_ada.shape[1]
    bn = 512

    def body(a_ref, w_ref, b_ref, o_ref):
        a = a_ref[...]
        s = (a / (1.0 + jnp.exp(-a))).astype(BF16)
        o_ref[...] = _dot(s, w_ref[...].astype(BF16)) + b_ref[...]

    return pl.pallas_call(
        body, name="mod_fwd", grid=(n // bn,),
        in_specs=[_full((rows, d)), pl.BlockSpec((d, bn), lambda j: (0, j)), pl.BlockSpec((1, bn), lambda j: (0, j))],
        out_specs=pl.BlockSpec((rows, bn), lambda j: (0, j)),
        out_shape=jax.ShapeDtypeStruct((rows, n), F32),
        compiler_params=_params(("parallel",)),
    )(a_in, w_ada, b_sh)


def _mod_bwd(a_in, dm, w_ada):
    rows, d = a_in.shape
    n = w_ada.shape[1]
    bn = 512
    nb = n // bn

    def body(a_ref, dm_ref, w_ref, gw_ref, da_ref):
        j = pl.program_id(0)
        a = a_ref[...]
        s = (a / (1.0 + jnp.exp(-a))).astype(BF16)
        dmb = dm_ref[...].astype(BF16)
        gw_ref[...] = _dot_tn(s, dmb)
        part = _dot_nt(dmb, w_ref[...].astype(BF16))

        @pl.when(j == 0)
        def _():
            da_ref[...] = part

        @pl.when(j > 0)
        def _():
            da_ref[...] += part

    return pl.pallas_call(
        body, name="mod_bwd", grid=(nb,),
        in_specs=[_full((rows, d)), pl.BlockSpec((rows, bn), lambda j: (0, j)), pl.BlockSpec((d, bn), lambda j: (0, j))],
        out_specs=[pl.BlockSpec((d, bn), lambda j: (0, j)), _full((rows, d))],
        out_shape=[jax.ShapeDtypeStruct((d, n), F32), jax.ShapeDtypeStruct((rows, d), F32)],
        compiler_params=_params(("arbitrary",)),
    )(a_in, dm, w_ada)


def _pre_fwd(x2, ctx2, modv, g_attn, w_in, g_q, g_kv, w_uq, w_ukv, cos_t, sin_t, *, seq, tm, rider=None):
    t_lat, d = x2.shape
    t_ctx = ctx2.shape[0]
    nl, nc = t_lat // tm, t_ctx // tm
    n_all = t_lat + t_ctx
    tpe = seq // tm
    nex = t_lat // seq

    def body(x_ref, c_ref, mod_ref, g_ref, win_ref, gq_ref, gkv_ref, wuq_ref, wukv_ref, cos_ref, sin_ref,
             h_ref, pg_ref, rq_ref, rk_ref, rv_ref, nq_ref, nkv_ref, q_ref, k_ref, v_ref):
        i = pl.program_id(0)
        xt = jnp.where(i < nl, x_ref[...], c_ref[...])
        sh = mod_ref[0, 0:1, :]
        sc = mod_ref[0, 1:2, :]
        r = lax.rsqrt(jnp.mean(xt * xt, axis=-1, keepdims=True) + EPS)
        hb = ((xt * r) * g_ref[...] * (1.0 + sc) + sh).astype(BF16)
        h_ref[...] = hb
        p = _dot_nt(hb, win_ref[...])
        cos = cos_ref[...]
        sin = sin_ref[...]
        rq_ref[...] = _rope(p[:, 0:256], cos, sin).astype(BF16)
        rk_ref[...] = _rope(p[:, 256:512] * (RET_DK ** -0.5), cos, sin).astype(BF16)
        rv_ref[...] = p[:, 512:1024].astype(BF16)
        pg_ref[...] = p[:, 1024:2176]
        cq = p[:, 1536:1920]
        ckv = p[:, 1920:2176]
        nqb = (cq * lax.rsqrt(jnp.mean(cq * cq, axis=-1, keepdims=True) + EPS) * gq_ref[...]).astype(BF16)
        nkvb = (ckv * lax.rsqrt(jnp.mean(ckv * ckv, axis=-1, keepdims=True) + EPS) * gkv_ref[...]).astype(BF16)
        nq_ref[...] = nqb
        nkv_ref[...] = nkvb
        cos1 = cos[:, 0:LANES]
        sin1 = sin[:, 0:LANES]
        kpe = _rope(p[:, 2176:2304], cos1, sin1).astype(BF16)
        for hd in range(HEADS):
            o = hd * MLA_HEAD
            qh = _dot_nt(nqb, wuq_ref[hd]) * MLA_SCALE
            q_ref[:, o:o + 128] = qh[:, 0:128].astype(BF16)
            q_ref[:, o + 128:o + 256] = _rope(qh[:, 128:256], cos1, sin1).astype(BF16)
            kvh = _dot(nkvb, wukv_ref[hd])
            k_ref[:, o:o + 128] = kvh[:, 0:128].astype(BF16)
            k_ref[:, o + 128:o + 256] = kpe
            v_ref[:, hd * 128:(hd + 1) * 128] = kvh[:, 128:256].astype(BF16)

    def tile(width):
        return pl.BlockSpec((tm, width), lambda i: (i, 0))

    widths = (d, PG_COLS, 256, 256, 512, Q_LORA, KV_LORA, HEADS * MLA_HEAD, HEADS * MLA_HEAD, HEADS * 128)
    dtypes = (BF16, F32, BF16, BF16, BF16, BF16, BF16, BF16, BF16, BF16)
    tab = pl.BlockSpec((tm, 256), lambda i: (jnp.where(i < nl, i % tpe, tpe), 0))
    return _hosted_call(
        body, (x2, ctx2, modv, g_attn, w_in, g_q, g_kv, w_uq, w_ukv, cos_t, sin_t), name="pre_fwd", grid=(nl + nc,),
        in_specs=[
            pl.BlockSpec((tm, d), lambda i: (jnp.minimum(i, nl - 1), 0)),
            pl.BlockSpec((tm, d), lambda i: (jnp.maximum(i - nl, 0), 0)),
            pl.BlockSpec((1, 8, d), lambda i: (jnp.minimum(i // tpe, nex), 0, 0)),
            _full((1, d)), _full(w_in.shape), _full((1, Q_LORA)), _full((1, KV_LORA)),
            _full(w_uq.shape), _full(w_ukv.shape), tab, tab,
        ],
        out_specs=[tile(w) for w in widths],
        out_shape=[jax.ShapeDtypeStruct((n_all, w), dt) for w, dt in zip(widths, dtypes)],
        sem=("parallel",), rider=rider)


def _post(yret, ymla, x2, tgt2, modv, g_ffn, g_fin, w_out, w_ff1, w_ff2, *, seq, tm):
    t_lat, d = x2.shape
    nl = t_lat // tm
    tpe = seq // tm
    nex = t_lat // seq
    n_slab = w_ff1.shape[0]
    fs = w_ff1.shape[2]

    def body(yr_ref, ym_ref, x_ref, t_ref, mod_ref, gf_ref, gl_ref, wo_ref, w1_ref, w2_ref,
             mix_ref, a_ref, du_ref, h2_ref, df_ref, dmo_ref, dmix_ref, dxm_ref, st_ref, ru_ref):
        i = pl.program_id(0)
        gt_a = mod_ref[0, 2:3, :]
        sh_f = mod_ref[0, 3:4, :]
        sc_f = mod_ref[0, 4:5, :]
        gt_f = mod_ref[0, 5:6, :]
        g_ffn_v = gf_ref[...]
        g_fin_v = gl_ref[...]
        yr = yr_ref[...]
        ym = ym_ref[...]
        mix_ref[:, 0:512] = yr
        mix_ref[:, 512:1024] = ym
        op = _dot(yr, wo_ref[0:512, :]) + _dot(ym, wo_ref[512:1024, :])
        x_mid = x_ref[...] + gt_a * op
        r2 = lax.rsqrt(jnp.mean(x_mid * x_mid, axis=-1, keepdims=True) + EPS)
        xh2 = x_mid * r2
        h2b = (xh2 * g_ffn_v * (1.0 + sc_f) + sh_f).astype(BF16)
        h2_ref[...] = h2b
        f = jnp.zeros((tm, d), F32)
        for s in range(n_slab):
            ru = jnp.maximum(_dot(h2b, w1_ref[s]), 0.0)
            ru_ref[:, s * fs:(s + 1) * fs] = ru
            ab = (ru * ru).astype(BF16)
            a_ref[:, s * fs:(s + 1) * fs] = ab
            f = f + _dot(ab, w2_ref[s * fs:(s + 1) * fs, :])
        x_out = x_mid + gt_f * f
        r3 = lax.rsqrt(jnp.mean(x_out * x_out, axis=-1, keepdims=True) + EPS)
        xh3 = x_out * r3
        err = xh3 * g_fin_v - t_ref[...]
        dy = err * (1.0 / d)
        dxh3 = dy * g_fin_v
        dx_out = r3 * (dxh3 - xh3 * jnp.mean(dxh3 * xh3, axis=-1, keepdims=True))
        dfb = (dx_out * gt_f).astype(BF16)
        df_ref[...] = dfb
        dh2 = jnp.zeros((tm, d), F32)
        for s in range(n_slab):
            da = _dot_nt(dfb, w2_ref[s * fs:(s + 1) * fs, :])
            dub = (da * (2.0 * ru_ref[:, s * fs:(s + 1) * fs])).astype(BF16)
            du_ref[:, s * fs:(s + 1) * fs] = dub
            dh2 = dh2 + _dot_nt(dub, w1_ref[s])
        dxh2 = dh2 * (1.0 + sc_f) * g_ffn_v
        dx_mid = dx_out + r2 * (dxh2 - xh2 * jnp.mean(dxh2 * xh2, axis=-1, keepdims=True))
        dxm_ref[...] = dx_mid
        dmob = (dx_mid * gt_a).astype(BF16)
        dmo_ref[...] = dmob
        dmix_ref[...] = _dot_nt(dmob, wo_ref[...]).astype(BF16)

        def rsum(v):
            return jnp.sum(v, axis=0, keepdims=True)

        stats = jnp.concatenate([
            rsum(dh2), rsum(dh2 * xh2 * g_ffn_v), rsum(dx_out * f), rsum(dx_mid * op),
            rsum(dh2 * (1.0 + sc_f) * xh2), rsum(dy * xh3), rsum(err * err), jnp.zeros((1, d), F32)], axis=0)

        @pl.when(i % tpe == 0)
        def _():
            st_ref[0] = stats

        @pl.when(i % tpe != 0)
        def _():
            st_ref[0] += stats

    def tile(width):
        return pl.BlockSpec((tm, width), lambda i: (i, 0))

    widths = (d, D_FF, D_FF, d, d, d, d, d)
    dtypes = (BF16, BF16, BF16, BF16, BF16, BF16, BF16, F32)
    const = pl.Buffered(1)
    return pl.pallas_call(
        body, name="post", grid=(nl,),
        in_specs=[
            tile(512), tile(512), tile(d), tile(d),
            pl.BlockSpec((1, 8, d), lambda i: (i // tpe, 0, 0)),
            _full((1, d)), _full((1, d)),
            pl.BlockSpec(w_out.shape, lambda i: (0, 0), pipeline_mode=const),
            pl.BlockSpec(w_ff1.shape, lambda i: (0, 0, 0), pipeline_mode=const),
            pl.BlockSpec(w_ff2.shape, lambda i: (0, 0), pipeline_mode=const),
        ],
        out_specs=[tile(w) for w in widths] + [pl.BlockSpec((1, 8, d), lambda i: (i // tpe, 0, 0))],
        out_shape=[jax.ShapeDtypeStruct((t_lat, w), dt) for w, dt in zip(widths, dtypes)]
        + [jax.ShapeDtypeStruct((nex, 8, d), F32)],
        scratch_shapes=[pltpu.VMEM((tm, D_FF), F32)],
        compiler_params=_params(("arbitrary",), VMEM_LIMIT),
    )(yret, ymla, x2, tgt2, modv, g_ffn, g_fin, w_out, w_ff1, w_ff2)


def _pre_bwd(x2, ctx2, modv, g_attn, pg, drq, drk, dkc_r, drv, dvc_r, drg, dq_m, dkl, dkc, dvl, dvc, dxm,
             w_in, g_q, g_kv, w_uq, w_ukv, cos_t, sin_t, *, seq, tm, rider=None):
    t_lat, d = x2.shape
    t_ctx = ctx2.shape[0]
    nl, nc = t_lat // tm, t_ctx // tm
    n_all = t_lat + t_ctx
    tpe = seq // tm
    nex = t_lat // seq

    def body(x_ref, c_ref, mod_ref, g_ref, pg_ref, drq_ref, drk_ref, dkcr_ref, drv_ref, dvcr_ref, drg_ref,
             dq_ref, dkl_ref, dkc_ref, dvl_ref, dvc_ref, dxm_ref, win_ref, gq_ref, gkv_ref, wuq_ref, wukv_ref,
             cos_ref, sin_ref, dpb_ref, dqf_ref, dkvf_ref, gx_ref, st_ref):
        i = pl.program_id(0)
        lat = i < nl
        latf = lat.astype(F32)
        cos = cos_ref[...]
        sin = sin_ref[...]
        cos1 = cos[:, 0:LANES]
        sin1 = sin[:, 0:LANES]
        d_rq = _rope_t(drq_ref[...] * latf, cos, sin)
        d_rk = _rope_t(jnp.where(lat, drk_ref[...], dkcr_ref[...]), cos, sin) * (RET_DK ** -0.5)
        d_rv = jnp.where(lat, drv_ref[...], dvcr_ref[...])
        d_rg = drg_ref[...] * latf
        dq_all = dq_ref[...] * (latf * MLA_SCALE)
        dk_all = jnp.where(lat, dkl_ref[...], dkc_ref[...])
        dv_all = jnp.where(lat, dvl_ref[...], dvc_ref[...])
        dnq = jnp.zeros((tm, Q_LORA), F32)
        dnkv = jnp.zeros((tm, KV_LORA), F32)
        dkpe = jnp.zeros((tm, LANES), F32)
        for hd in range(HEADS):
            o = hd * MLA_HEAD
            dqh = jnp.concatenate([dq_all[:, o:o + 128], _rope_t(dq_all[:, o + 128:o + 256], cos1, sin1)],
                                  axis=1).astype(BF16)
            dqf_ref[:, o:o + 256] = dqh
            dnq = dnq + _dot(dqh, wuq_ref[hd])
            dkpe = dkpe + dk_all[:, o + 128:o + 256]
            dkvh = jnp.concatenate([dk_all[:, o:o + 128], dv_all[:, hd * 128:(hd + 1) * 128]], axis=1).astype(BF16)
            dkvf_ref[:, o:o + 256] = dkvh
            dnkv = dnkv + _dot_nt(dkvh, wukv_ref[hd])
        d_kpe = _rope_t(dkpe, cos1, sin1)
        pgv = pg_ref[...]
        cq = pgv[:, 512:896]
        ckv = pgv[:, 896:1152]
        rq_ = lax.rsqrt(jnp.mean(cq * cq, axis=-1, keepdims=True) + EPS)
        cqh = cq * rq_
        dcqh = dnq * gq_ref[...]
        d_cq = rq_ * (dcqh - cqh * jnp.mean(dcqh * cqh, axis=-1, keepdims=True))
        rkv_ = lax.rsqrt(jnp.mean(ckv * ckv, axis=-1, keepdims=True) + EPS)
        ckvh = ckv * rkv_
        dckvh = dnkv * gkv_ref[...]
        d_ckv = rkv_ * (dckvh - ckvh * jnp.mean(dckvh * ckvh, axis=-1, keepdims=True))
        dpb = jnp.concatenate([d_rq, d_rk, d_rv, d_rg, d_cq, d_ckv, d_kpe], axis=1).astype(BF16)
        dpb_ref[...] = dpb
        dh = _dot(dpb, win_ref[...])
        xt = jnp.where(lat, x_ref[...], c_ref[...])
        sc = mod_ref[0, 1:2, :]
        g = g_ref[...]
        r = lax.rsqrt(jnp.mean(xt * xt, axis=-1, keepdims=True) + EPS)
        xh = xt * r
        dxh = dh * (1.0 + sc) * g
        dx = r * (dxh - xh * jnp.mean(dxh * xh, axis=-1, keepdims=True))

        @pl.when(lat)
        def _():
            gx_ref[...] = dxm_ref[...] + dx

        def rsum(v):
            return jnp.sum(v, axis=0, keepdims=True)

        def widen(v):
            return jnp.concatenate([v, jnp.zeros((1, d - v.shape[1]), F32)], axis=1)

        stats = jnp.concatenate([
            rsum(dh), rsum(dh * xh * g), rsum(dh * (1.0 + sc) * xh), widen(rsum(dnq * cqh)), widen(rsum(dnkv * ckvh)),
            jnp.zeros((3, d), F32)], axis=0)
        first = jnp.logical_or(jnp.logical_and(lat, i % tpe == 0), i == nl)

        @pl.when(first)
        def _():
            st_ref[0] = stats

        @pl.when(jnp.logical_not(first))
        def _():
            st_ref[0] += stats

    def lat_tile(width):
        return pl.BlockSpec((tm, width), lambda i: (jnp.minimum(i, nl - 1), 0))

    def ctx_tile(width):
        return pl.BlockSpec((tm, width), lambda i: (jnp.maximum(i - nl, 0), 0))

    def tile(width):
        return pl.BlockSpec((tm, width), lambda i: (i, 0))

    tab = pl.BlockSpec((tm, 256), lambda i: (jnp.where(i < nl, i % tpe, tpe), 0))
    ex = pl.BlockSpec((1, 8, d), lambda i: (jnp.minimum(i // tpe, nex), 0, 0))
    return _hosted_call(
        body, (x2, ctx2, modv, g_attn, pg, drq, drk, dkc_r, drv, dvc_r, drg, dq_m, dkl, dkc, dvl, dvc, dxm,
               w_in, g_q, g_kv, w_uq, w_ukv, cos_t, sin_t), name="pre_bwd", grid=(nl + nc,),
        in_specs=[
            lat_tile(d), ctx_tile(d), ex, _full((1, d)), tile(PG_COLS),
            lat_tile(256), lat_tile(256), ctx_tile(256), lat_tile(512), ctx_tile(512), lat_tile(512),
            lat_tile(1024), lat_tile(1024), ctx_tile(1024), lat_tile(512), ctx_tile(512), lat_tile(d),
            _full(w_in.shape), _full((1, Q_LORA)), _full((1, KV_LORA)), _full(w_uq.shape), _full(w_ukv.shape),
            tab, tab,
        ],
        out_specs=[tile(IN_PAD), tile(1024), tile(1024), lat_tile(d), ex],
        out_shape=[
            jax.ShapeDtypeStruct((n_all, IN_PAD), BF16), jax.ShapeDtypeStruct((n_all, 1024), BF16),
            jax.ShapeDtypeStruct((n_all, 1024), BF16), jax.ShapeDtypeStruct((t_lat, d), F32),
            jax.ShapeDtypeStruct((nex + 1, 8, d), F32),
        ],
        sem=("arbitrary",), rider=rider)


MLA_SCALE = 1.0 / math.sqrt(MLA_NOPE + MLA_ROPE)
KEY_BLOCK = 1024


def _mla_specs(t_lat, seq, ctx_len, tq):
    nqt = seq // tq
    cb = t_lat // ctx_len
    q = pl.BlockSpec((tq, MLA_HEAD), lambda b, h, j: (b * nqt + j, h))
    kl = pl.BlockSpec((seq, MLA_HEAD), lambda b, h, j: (b, h))
    kc = pl.BlockSpec((ctx_len, MLA_HEAD), lambda b, h, j: (cb + b, h))
    vl = pl.BlockSpec((seq, 128), lambda b, h, j: (b, h))
    vc = pl.BlockSpec((ctx_len, 128), lambda b, h, j: (cb + b, h))
    o = pl.BlockSpec((tq, 128), lambda b, h, j: (b * nqt + j, h))
    return q, kl, kc, vl, vc, o


def _mla_fwd(q, k, v, *, t_lat, seq, ctx_len, tq, rider=None):
    nex = t_lat // seq

    def body(q_ref, kl_ref, kc_ref, vl_ref, vc_ref, o_ref, lse_ref):
        qb = q_ref[...]
        s = _dot_nt(qb, kl_ref[...])
        sc = _dot_nt(qb, kc_ref[...])
        m = jnp.maximum(jnp.max(s, axis=-1, keepdims=True), jnp.max(sc, axis=-1, keepdims=True))
        p = jnp.exp(s - m)
        pc = jnp.exp(sc - m)
        total = jnp.sum(p, axis=-1, keepdims=True) + jnp.sum(pc, axis=-1, keepdims=True)
        o = _dot(p.astype(BF16), vl_ref[...]) + _dot(pc.astype(BF16), vc_ref[...])
        o_ref[...] = (o * (1.0 / total)).astype(BF16)
        lse_ref[...] = jnp.broadcast_to(m + jnp.log(total), lse_ref.shape)

    qs, kl, kc, vl, vc, os_ = _mla_specs(t_lat, seq, ctx_len, tq)
    return _hosted_call(
        body, (q, k, k, v, v), name="mla_fwd", grid=(nex, HEADS, seq // tq),
        in_specs=[qs, kl, kc, vl, vc], out_specs=[os_, os_],
        out_shape=[jax.ShapeDtypeStruct((t_lat, HEADS * 128), BF16), jax.ShapeDtypeStruct((t_lat, HEADS * 128), F32)],
        sem=("parallel", "parallel", "arbitrary"), rider=rider)


def _mla_bwd(q, k, v, ymla, lse, dmix, *, t_lat, seq, ctx_len, tq, rider=None):
    nex = t_lat // seq
    nqt = seq // tq
    t_ctx = nex * ctx_len
    kb = min(KEY_BLOCK, seq)

    def body(q_ref, kl_ref, kc_ref, vl_ref, vc_ref, o_ref, lse_ref, do_ref, dq_ref, dkl_ref, dkc_ref, dvl_ref, dvc_ref):
        j = pl.program_id(2)

        @pl.when(j == 0)
        def _():
            dkl_ref[...] = jnp.zeros(dkl_ref.shape, F32)
            dkc_ref[...] = jnp.zeros(dkc_ref.shape, F32)
            dvl_ref[...] = jnp.zeros(dvl_ref.shape, F32)
            dvc_ref[...] = jnp.zeros(dvc_ref.shape, F32)

        qb = q_ref[...]
        dob = do_ref[...]
        delta = jnp.sum(dob.astype(F32) * o_ref[...].astype(F32), axis=-1, keepdims=True)
        lse_row = lse_ref[:, 0:1]

        def block(k_ref, v_ref, dk_ref, dv_ref, rows):
            kbl = k_ref[rows, :]
            vbl = v_ref[rows, :]
            p = jnp.exp(_dot_nt(qb, kbl) - lse_row)
            ds = (p * (_dot_nt(dob, vbl) - delta)).astype(BF16)
            dk_ref[rows, :] += _dot_tn(ds, qb)
            dv_ref[rows, :] += _dot_tn(p.astype(BF16), dob)
            return _dot(ds, kbl)

        dq = block(kc_ref, vc_ref, dkc_ref, dvc_ref, pl.ds(0, ctx_len))
        for i in range(seq // kb):
            dq = dq + block(kl_ref, vl_ref, dkl_ref, dvl_ref, pl.ds(i * kb, kb))
        dq_ref[...] = dq

    qs, kl, kc, vl, vc, os_ = _mla_specs(t_lat, seq, ctx_len, tq)
    do_spec = pl.BlockSpec((tq, 128), lambda b, h, j: (b * nqt + j, HEADS + h))
    return _hosted_call(
        body, (q, k, k, v, v, ymla, lse, dmix), name="mla_bwd", grid=(nex, HEADS, nqt),
        in_specs=[qs, kl, kc, vl, vc, os_, os_, do_spec],
        out_specs=[
            qs,
            pl.BlockSpec((seq, MLA_HEAD), lambda b, h, j: (b, h)),
            pl.BlockSpec((ctx_len, MLA_HEAD), lambda b, h, j: (b, h)),
            pl.BlockSpec((seq, 128), lambda b, h, j: (b, h)),
            pl.BlockSpec((ctx_len, 128), lambda b, h, j: (b, h)),
        ],
        out_shape=[
            jax.ShapeDtypeStruct((t_lat, HEADS * MLA_HEAD), F32),
            jax.ShapeDtypeStruct((t_lat, HEADS * MLA_HEAD), F32),
            jax.ShapeDtypeStruct((t_ctx, HEADS * MLA_HEAD), F32),
            jax.ShapeDtypeStruct((t_lat, HEADS * 128), F32),
            jax.ShapeDtypeStruct((t_ctx, HEADS * 128), F32),
        ],
        sem=("parallel", "parallel", "arbitrary"), rider=rider)


def _decay_terms(lg, chunk, forward):
    ii = lax.broadcasted_iota(jnp.int32, (chunk, chunk), 0)
    jj = lax.broadcasted_iota(jnp.int32, (chunk, chunk), 1)
    diff = (ii - jj) if forward else (jj - ii)
    dist = jnp.maximum(diff, 0).astype(F32)
    dmat = jnp.where(diff >= 0, jnp.exp(lg * dist), 0.0)
    pos = lax.broadcasted_iota(jnp.int32, (chunk, 1), 0).astype(F32)
    if forward:
        e_q = pos + 1.0
        e_k = (chunk - 1.0) - pos
    else:
        e_q = chunk - pos
        e_k = pos
    wq = jnp.exp(lg * e_q)
    wk = jnp.exp(lg * e_k)
    cd = jnp.exp(jnp.full((1, 1), lg * chunk, F32))
    return dmat, dist, wq, wk, e_q, e_k, cd


def _ctx_weights(lg, ctx_len, forward):
    pos = lax.broadcasted_iota(jnp.int32, (ctx_len, 1), 0).astype(F32)
    e = ((ctx_len - 1.0) - pos) if forward else pos
    return jnp.exp(lg * e), e


def _pair_specs(t_lat, seq, ctx_len):
    cb = t_lat // ctx_len
    qk = pl.BlockSpec((seq, 128), lambda b, p: (b, p))
    v = pl.BlockSpec((seq, 256), lambda b, p: (b, p))
    kc = pl.BlockSpec((ctx_len, 128), lambda b, p: (cb + b, p))
    vc = pl.BlockSpec((ctx_len, 256), lambda b, p: (cb + b, p))
    return qk, v, kc, vc


def _lane_masks():
    lane = lax.broadcasted_iota(jnp.int32, (1, 128), 1)
    return [(lane // RET_DK) == hh for hh in (0, 1)]


def _ret_fwd_pair(rq, rk, rv, pg, lg, g_ret, *, t_lat, seq, ctx_len, chunk, rider=None):
    nex = t_lat // seq
    n_chunk = seq // chunk

    def body(q_ref, k_ref, v_ref, kc_ref, vc_ref, rg_ref, lg_ref, g_ref, y_ref, o_ref):
        pair = pl.program_id(1)
        masks = _lane_masks()
        kcf = kc_ref[...].astype(F32)

        def run(forward):
            terms, s0 = [], []
            for hh in (0, 1):
                lgd = lg_ref[0 if forward else 1, 2 * pair + hh]
                terms.append(_decay_terms(lgd, chunk, forward))
                wc, _ = _ctx_weights(lgd, ctx_len, forward)
                s0.append(_dot_tn((jnp.where(masks[hh], kcf, 0.0) * wc).astype(BF16), vc_ref[:, hh * 128:(hh + 1) * 128]))

            def step(t, states):
                n = t if forward else n_chunk - 1 - t
                sl = pl.ds(pl.multiple_of(n * chunk, chunk), chunk)
                qb = q_ref[sl, :]
                kf_all = k_ref[sl, :].astype(F32)
                new = []
                for hh in (0, 1):
                    dmat, _, wq, wk, _, _, cd = terms[hh]
                    cols = slice(hh * 128, (hh + 1) * 128)
                    qm = jnp.where(masks[hh], qb, jnp.zeros((), BF16))
                    kf = jnp.where(masks[hh], kf_all, 0.0)
                    vb = v_ref[sl, cols]
                    a = _dot_nt(qm, kf.astype(BF16)) * dmat
                    o = _dot(a.astype(BF16), vb) + wq * _dot(qm, states[hh].astype(BF16))
                    if forward:
                        o_ref[sl, cols] = o
                    else:
                        o = o_ref[sl, cols] + o
                        o_ref[sl, cols] = o
                        mu = jnp.mean(o, axis=-1, keepdims=True)
                        oc = o - mu
                        var = jnp.mean(oc * oc, axis=-1, keepdims=True)
                        rg = rg_ref[sl, cols]
                        y_ref[sl, cols] = (oc * lax.rsqrt(var + EPS) * g_ref[:, cols] * (rg / (1.0 + jnp.exp(-rg)))).astype(BF16)
                    new.append(cd * states[hh] + _dot_tn((kf * wk).astype(BF16), vb))
                return tuple(new)

            lax.fori_loop(0, n_chunk, step, tuple(s0))

        run(True)
        run(False)

    qk, v, kc, vc = _pair_specs(t_lat, seq, ctx_len)
    return _hosted_call(
        body, (rq, rk, rv, rk, rv, pg, lg, g_ret), name="ret_fwd", grid=(nex, HEADS // 2),
        in_specs=[qk, qk, v, kc, vc, v, pl.BlockSpec(memory_space=pltpu.SMEM), pl.BlockSpec((1, 256), lambda b, p: (0, p))],
        out_specs=[v, v],
        out_shape=[jax.ShapeDtypeStruct((t_lat, HEADS * RET_DV), BF16), jax.ShapeDtypeStruct((t_lat, HEADS * RET_DV), F32)],
        sem=("parallel", "arbitrary"), rider=rider)


def _ret_bwd_pair(rq, rk, rv, pg, osum, dmix, lg, g_ret, *, t_lat, seq, ctx_len, chunk, rider=None):
    nex = t_lat // seq
    n_chunk = seq // chunk
    t_ctx = nex * ctx_len

    def body(q_ref, k_ref, v_ref, kc_ref, vc_ref, rg_ref, o_ref, dy_ref, lg_ref, g_ref,
             dq_ref, dk_ref, dv_ref, dkc_ref, dvc_ref, drg_ref, st_ref, do_s, s_st):
        pair = pl.program_id(1)
        masks = _lane_masks()
        kcf = kc_ref[...].astype(F32)

        def norm_step(n, dgains):
            sl = pl.ds(pl.multiple_of(n * chunk, chunk), chunk)
            out = []
            for hh in (0, 1):
                cols = slice(hh * 128, (hh + 1) * 128)
                gain = g_ref[:, cols]
                o = o_ref[sl, cols]
                mu = jnp.mean(o, axis=-1, keepdims=True)
                oc = o - mu
                rstd = lax.rsqrt(jnp.mean(oc * oc, axis=-1, keepdims=True) + EPS)
                ohat = oc * rstd
                rg = rg_ref[sl, cols]
                sg = 1.0 / (1.0 + jnp.exp(-rg))
                dy = dy_ref[sl, cols].astype(F32)
                don = dy * (rg * sg)
                drg_ref[sl, cols] = dy * (ohat * gain) * (sg * (1.0 + rg * (1.0 - sg)))
                dohat = don * gain
                do_s[sl, cols] = rstd * (dohat - jnp.mean(dohat, axis=-1, keepdims=True)
                                         - ohat * jnp.mean(dohat * ohat, axis=-1, keepdims=True))
                out.append(dgains[hh] + jnp.sum(don * ohat, axis=0, keepdims=True))
            return tuple(out)

        zero_row = jnp.zeros((1, 128), F32)
        dgains = lax.fori_loop(0, n_chunk, norm_step, (zero_row, zero_row))
        dq_ref[...] = jnp.zeros(dq_ref.shape, F32)
        dk_ref[...] = jnp.zeros(dk_ref.shape, F32)
        dv_ref[...] = jnp.zeros(dv_ref.shape, F32)

        chains = [(forward, hh) for forward in (True, False) for hh in (0, 1)]
        terms, ctxw, s0 = [], [], []
        for forward, hh in chains:
            lgd = lg_ref[0 if forward else 1, 2 * pair + hh]
            terms.append(_decay_terms(lgd, chunk, forward))
            ctxw.append(_ctx_weights(lgd, ctx_len, forward))
            s0.append(_dot_tn((jnp.where(masks[hh], kcf, 0.0) * ctxw[-1][0]).astype(BF16), vc_ref[:, hh * 128:(hh + 1) * 128]))

        def chunk_at(t, ascending):
            n = t if ascending else n_chunk - 1 - t
            return n, pl.ds(pl.multiple_of(n * chunk, chunk), chunk)

        def state_step(t, states):
            new = []
            for c, (forward, hh) in enumerate(chains):
                n, sl = chunk_at(t, forward)
                wk, cd = terms[c][3], terms[c][6]
                s_st[c, n] = states[c]
                kf = jnp.where(masks[hh], k_ref[sl, :].astype(F32), 0.0)
                new.append(cd * states[c] + _dot_tn((kf * wk).astype(BF16), v_ref[sl, hh * 128:(hh + 1) * 128]))
            return tuple(new)

        lax.fori_loop(0, n_chunk, state_step, tuple(s0))

        def grad_step(t, carry):
            out = []
            for forward in (True, False):
                n, sl = chunk_at(t, not forward)
                qb = q_ref[sl, :]
                kf_all = k_ref[sl, :].astype(F32)
                dq_sum = jnp.zeros((chunk, 128), F32)
                dk_sum = jnp.zeros((chunk, 128), F32)
                for hh in (0, 1):
                    c = chains.index((forward, hh))
                    g_next, dlg = carry[c]
                    dmat, dist, wq, wk, e_q, e_k, cd = terms[c]
                    cols = slice(hh * 128, (hh + 1) * 128)
                    qm = jnp.where(masks[hh], qb, jnp.zeros((), BF16))
                    kf = jnp.where(masks[hh], kf_all, 0.0)
                    kb = kf.astype(BF16)
                    vb = v_ref[sl, cols]
                    do = do_s[sl, cols]
                    dob = do.astype(BF16)
                    s_n = s_st[c, n]
                    s_nb = s_n.astype(BF16)
                    gb = g_next.astype(BF16)
                    dk_cross = wk * _dot_nt(vb, gb)
                    dv_cross = _dot((kf * wk).astype(BF16), gb)
                    a = _dot_nt(qm, kb) * dmat
                    da_raw = _dot_nt(dob, vb)
                    dab = (da_raw * dmat).astype(BF16)
                    o_cross = wq * _dot(qm, s_nb)
                    dq_sum = dq_sum + _dot(dab, kb) + wq * _dot_nt(dob, s_nb)
                    dk_sum = dk_sum + _dot_tn(dab, qm) + dk_cross
                    dv_ref[sl, cols] += _dot_tn(a.astype(BF16), dob) + dv_cross
                    dlg = (dlg + chunk * cd * jnp.sum(g_next * s_n, keepdims=True)
                           + jnp.sum(e_k * jnp.sum(kf * dk_cross, axis=-1, keepdims=True), keepdims=True)
                           + jnp.sum(dist * a * da_raw, keepdims=True)
                           + jnp.sum(e_q * jnp.sum(o_cross * do, axis=-1, keepdims=True), keepdims=True))
                    out.append((cd * g_next + _dot_tn((qm.astype(F32) * wq).astype(BF16), dob), dlg))
                dq_ref[sl, :] += dq_sum
                dk_ref[sl, :] += dk_sum
            return tuple(out)

        zero = (jnp.zeros((128, 128), F32), jnp.zeros((1, 1), F32))
        res = lax.fori_loop(0, n_chunk, grad_step, (zero,) * len(chains))
        dkc_sum = jnp.zeros((ctx_len, 128), F32)
        dvc = [jnp.zeros((ctx_len, 128), F32)] * 2
        dlgs = []
        for c, (forward, hh) in enumerate(chains):
            ds0, dlg = res[c]
            wc, e_c = ctxw[c]
            kcm = jnp.where(masks[hh], kcf, 0.0)
            ds0b = ds0.astype(BF16)
            dkc_part = wc * _dot_nt(vc_ref[:, hh * 128:(hh + 1) * 128], ds0b)
            dkc_sum = dkc_sum + dkc_part
            dvc[hh] = dvc[hh] + _dot((kcm * wc).astype(BF16), ds0b)
            dlgs.append(dlg + jnp.sum(e_c * jnp.sum(kcm * dkc_part, axis=-1, keepdims=True), keepdims=True))
        dkc_ref[...] = dkc_sum
        for hh in (0, 1):
            cols = slice(hh * 128, (hh + 1) * 128)
            dvc_ref[:, cols] = dvc[hh]
            st_ref[0, :, cols] = jnp.concatenate([
                dgains[hh], jnp.broadcast_to(dlgs[hh], (1, 128)), jnp.broadcast_to(dlgs[2 + hh], (1, 128)),
                jnp.zeros((5, 128), F32)], axis=0)

    qk, v, kc, vc = _pair_specs(t_lat, seq, ctx_len)
    return _hosted_call(
        body, (rq, rk, rv, rk, rv, pg, osum, dmix, lg, g_ret), name="ret_bwd", grid=(nex, HEADS // 2),
        in_specs=[qk, qk, v, kc, vc, v, v, v, pl.BlockSpec(memory_space=pltpu.SMEM),
                  pl.BlockSpec((1, 256), lambda b, p: (0, p))],
        out_specs=[
            qk, qk, v,
            pl.BlockSpec((ctx_len, 128), lambda b, p: (b, p)),
            pl.BlockSpec((ctx_len, 256), lambda b, p: (b, p)),
            v,
            pl.BlockSpec((1, 8, 256), lambda b, p: (b, 0, p)),
        ],
        out_shape=[
            jax.ShapeDtypeStruct((t_lat, 256), F32), jax.ShapeDtypeStruct((t_lat, 256), F32),
            jax.ShapeDtypeStruct((t_lat, 512), F32), jax.ShapeDtypeStruct((t_ctx, 256), F32),
            jax.ShapeDtypeStruct((t_ctx, 512), F32), jax.ShapeDtypeStruct((t_lat, 512), F32),
            jax.ShapeDtypeStruct((nex, 8, 512), F32),
        ],
        scratch_shapes=[pltpu.VMEM((seq, 256), F32), pltpu.VMEM((4, n_chunk, 128, 128), F32)],
        sem=("parallel", "arbitrary"), rider=rider)


def _matmul_tn(a, b, *, bm, bn, bk, chip_major, name, out_dtype=F32, rider=None):
    tk, m = a.shape
    n = b.shape[1]
    slab = n // N_CHIPS
    per_block = bn // slab if chip_major else 1
    bk = max(c for c in range(LANES, min(bk, tk) + 1, LANES) if tk % c == 0)
    nk = tk // bk
    blk = (per_block, bm, slab) if chip_major else (bm, bn)

    def body(a_ref, b_ref, o_ref, acc_ref):
        k = pl.program_id(2)
        if chip_major:
            parts = [_dot_tn(a_ref[...], b_ref[:, s * slab:(s + 1) * slab]) for s in range(per_block)]
        else:
            parts = [_dot_tn(a_ref[...], b_ref[...])]

        @pl.when(k == 0)
        def _():
            for s, part in enumerate(parts):
                if chip_major:
                    acc_ref[s] = part
                else:
                    acc_ref[...] = part

        @pl.when(k > 0)
        def _():
            for s, part in enumerate(parts):
                if chip_major:
                    acc_ref[s] += part
                else:
                    acc_ref[...] += part

        @pl.when(k == nk - 1)
        def _():
            o_ref[...] = acc_ref[...].astype(out_dtype)

    if chip_major:
        out_spec = pl.BlockSpec(blk, lambda i, j, k: (j, i, 0))
        out_shape = jax.ShapeDtypeStruct((N_CHIPS, m, slab), out_dtype)
    else:
        out_spec = pl.BlockSpec(blk, lambda i, j, k: (i, j))
        out_shape = jax.ShapeDtypeStruct((m, n), out_dtype)
    (out,), carried = _hosted_call(
        body, (a, b), name=name, grid=(m // bm, n // bn, nk),
        in_specs=[pl.BlockSpec((bk, bm), lambda i, j, k: (k, i)), pl.BlockSpec((bk, bn), lambda i, j, k: (k, j))],
        out_specs=[out_spec], out_shape=[out_shape], scratch_shapes=[pltpu.VMEM(blk, F32)],
        sem=("parallel", "parallel", "arbitrary"), rider=rider)
    return out if rider is None else (out, carried)


_LATE = ("w_out", "w_ff1", "w_ff2")
_EARLY = ("w_in", "w_uq", "w_ukv")


def _local_step(x, ctx, tgt, modv, lg, g_attn, g_ffn, g_fin, g_ret, g_q, g_kv, w_in, w_uq, w_ukv, late, place=None,
                *, tm=256, tq=256, chunk=256):
    nex, seq, d = x.shape
    ctx_len = ctx.shape[1]
    t_lat = nex * seq
    tm = min(tm, seq)
    x2 = x.reshape(t_lat, d)
    ctx2 = ctx.reshape(nex * ctx_len, d)
    tgt2 = tgt.reshape(t_lat, d)
    tm_fwd = min(2 * tm, seq)
    cos_t, sin_t = _rope_tables(seq, tm)
    dims = dict(t_lat=t_lat, seq=seq, ctx_len=ctx_len)
    alone = place is None

    (hb, pg, rq, rk, rv, nq, nkv, q, k, v), crossed = _pre_fwd(
        x2, ctx2, modv, g_attn, w_in, g_q, g_kv, w_uq, w_ukv, *_rope_tables(seq, tm_fwd), seq=seq, tm=tm_fwd,
        rider=None if alone else _gather_ici_rider([late[2]]))
    (yret, osum), got_ff2 = _ret_fwd_pair(rq, rk, rv, pg, lg, g_ret, chunk=chunk, **dims,
                                          rider=None if alone else _gather_d2d_rider(crossed))
    (ymla, lse), got_rest = _mla_fwd(q, k, v, tq=tq, **dims,
                                     rider=None if alone else _gather_rider([late[0], late[1]], staged=True))
    w_out, w_ff1, w_ff2 = late if alone else got_rest + got_ff2
    mix, act, du, h2, df, dmo, dmix, dxm, st_post = _post(yret, ymla, x2, tgt2, modv, g_ffn, g_fin, w_out.reshape(d, d),
                                                         w_ff1, w_ff2.reshape(D_FF, d), seq=seq, tm=min(tm, 256))
    kw = dict(bm=1024, bn=1024, bk=1024, out_dtype=BF16)
    g_ff2 = _matmul_tn(act, df, chip_major=False, name="gw_ff2", **kw).reshape(N_CHIPS, D_FF // N_CHIPS, d)
    if alone:
        g_ff1 = _matmul_tn(h2, du, chip_major=True, name="gw_ff1", **kw)
        g_out = _matmul_tn(mix, dmo, chip_major=False, name="gw_out", **kw).reshape(N_CHIPS, d // N_CHIPS, d)
        (dq_m, dkl, dkc, dvl, dvc), _ = _mla_bwd(q, k, v, ymla, lse, dmix, tq=tq, **dims)
        (drq, drk, drv, dkc_r, dvc_r, drg, st_ret), _ = _ret_bwd_pair(rq, rk, rv, pg, osum, dmix, lg, g_ret, chunk=chunk,
                                                                      **dims)
        late_out = [g_out, g_ff1, g_ff2]
    else:
        core, slot = place
        g_ff1, x_ff2 = _matmul_tn(h2, du, chip_major=True, name="gw_ff1", rider=_exchange_rider([g_ff2]), **kw)
        g_out, x_ff1 = _matmul_tn(mix, dmo, chip_major=False, name="gw_out", rider=_exchange_rider([g_ff1]), **kw)
        g_out = g_out.reshape(N_CHIPS, d // N_CHIPS, d)
        p_ff2 = _add_half(g_ff2, x_ff2[0], core, "add_half_w_ff2")
        p_ff1 = _add_half(g_ff1, x_ff1[0], core, "add_half_w_ff1")
        (dq_m, dkl, dkc, dvl, dvc), (l_ff2, l_ff1, x_out) = _mla_bwd(
            q, k, v, ymla, lse, dmix, tq=min(seq, 512), **dims,
            rider=_merge_riders(_scatter_rider([p_ff2, p_ff1]), _exchange_rider([g_out])))
        p_out = _add_half(g_out, x_out, core, "add_half_w_out")
        m_ff2 = _sum_chips(p_ff2, l_ff2, slot, "sum_chips_w_ff2")
        m_ff1 = _sum_chips(p_ff1, l_ff1, slot, "sum_chips_w_ff1")
        (drq, drk, drv, dkc_r, dvc_r, drg, st_ret), (l_out,) = _ret_bwd_pair(
            rq, rk, rv, pg, osum, dmix, lg, g_ret, chunk=chunk, **dims, rider=_scatter_rider([p_out]))
        late_out = [_sum_chips(p_out, l_out, slot, "sum_chips_w_out"), m_ff1, m_ff2]
    (dpb, dqf, dkvf, gx, st_pre), _ = _pre_bwd(
        x2, ctx2, modv, g_attn, pg, drq, drk, dkc_r, drv, dvc_r, drg, dq_m, dkl, dkc, dvl, dvc, dxm, w_in, g_q, g_kv,
        w_uq, w_ukv, cos_t, sin_t, seq=seq, tm=tm)
    g_early = [
        _matmul_tn(dpb, hb, bm=IN_PAD // 2, bn=d, bk=512, chip_major=False, name="gw_in"),
        _matmul_tn(dqf, nq, bm=HEADS * MLA_HEAD, bn=Q_LORA, bk=1536, chip_major=False, name="gw_uq"),
        _matmul_tn(nkv, dkvf, bm=KV_LORA, bn=HEADS * 256, bk=1536, chip_major=True, name="gw_ukv"),
    ]
    return gx.reshape(nex, seq, d), g_early, late_out, st_post, st_ret, st_pre


_ANY = pl.BlockSpec(memory_space=pl.ANY)
_VMEM = pl.BlockSpec(memory_space=pltpu.VMEM)
_OFFSETS = tuple((dx, dy, dc) for dx in (0, 1) for dy in (0, 1) for dc in (0, 1))[1:]
_CHIP_OFFSETS = ((1, 0), (0, 1), (1, 1))


def _place():
    return lax.axis_index("x"), lax.axis_index("y"), lax.axis_index("c")


def _flip(v, d):
    return 1 - v if d else v


def _gather8_rider(a, in_vmem=True):
    def copies(a_ref, o_ref, send, recv):
        x, y, z = _place()
        me = 4 * x + 2 * y + z
        out = []
        for k, (dx, dy, dc) in enumerate(_OFFSETS):
            peer = (_flip(x, dx), _flip(y, dy), _flip(z, dc))
            landing = o_ref.at[4 * peer[0] + 2 * peer[1] + peer[2]]
            out.append((
                pltpu.make_async_remote_copy(src_ref=a_ref, dst_ref=o_ref.at[me], send_sem=send.at[k],
                                             recv_sem=recv.at[k], device_id=peer, device_id_type=MESH),
                pltpu.make_async_remote_copy(src_ref=a_ref, dst_ref=landing, send_sem=send.at[k],
                                             recv_sem=recv.at[k], device_id=peer, device_id_type=MESH)))
        return me, out

    def start(ins, outs, sems):
        me, cps = copies(ins[0], outs[0], sems[0], sems[1])
        pltpu.make_async_copy(ins[0], outs[0].at[me], sems[2]).start()
        for out_cp, _ in cps:
            out_cp.start()

    def finish(ins, outs, sems):
        me, cps = copies(ins[0], outs[0], sems[0], sems[1])
        for out_cp, in_cp in cps:
            in_cp.wait_recv()
            out_cp.wait_send()
        pltpu.make_async_copy(ins[0], outs[0].at[me], sems[2]).wait()

    spec = [_VMEM] if in_vmem else [_ANY]
    return _Rider([a], [jax.ShapeDtypeStruct((N_DEV,) + a.shape, a.dtype)],
                  [pltpu.SemaphoreType.DMA((7,)), pltpu.SemaphoreType.DMA((7,)), pltpu.SemaphoreType.DMA],
                  start, finish, in_specs=spec, out_specs=spec)


def _merge_riders(*riders):
    ins, outs, sems, in_specs, out_specs, aliases, cuts = [], [], [], [], [], {}, []
    for r in riders:
        cuts.append((len(ins), len(outs), len(sems)))
        aliases.update({len(ins) + i: len(outs) + j for i, j in r.aliases.items()})
        ins += r.ins
        outs += r.out_shapes
        sems += r.sems
        in_specs += r.in_specs
        out_specs += r.out_specs

    def part(r, cut, r_ins, r_outs, r_sems):
        return (r_ins[cut[0]:cut[0] + len(r.ins)], r_outs[cut[1]:cut[1] + len(r.out_shapes)],
                r_sems[cut[2]:cut[2] + len(r.sems)])

    def start(r_ins, r_outs, r_sems):
        for r, cut in zip(riders, cuts):
            r.start(*part(r, cut, r_ins, r_outs, r_sems))

    def finish(r_ins, r_outs, r_sems):
        for r, cut in zip(riders, cuts):
            r.finish(*part(r, cut, r_ins, r_outs, r_sems))

    def middle(r_ins, r_outs, r_sems):
        for r, cut in zip(riders, cuts):
            if r.middle is not None:
                r.middle(*part(r, cut, r_ins, r_outs, r_sems))

    return _Rider(ins, outs, sems, start, finish, aliases=aliases, in_specs=in_specs, out_specs=out_specs,
                  middle=middle if any(r.middle is not None for r in riders) else None)


def _allgather8(a, name):
    return _run_rider(_gather8_rider(a), name)[0]


BF16_TILE_ROWS = 16


def _half(o, slot, which):
    r2 = o.shape[1] // 2
    if r2 % BF16_TILE_ROWS == 0:
        return o.at[slot, pl.ds(which * r2, r2)]
    c2 = o.shape[2] // 2
    assert c2 % LANES == 0
    return o.at[slot, :, pl.ds(which * c2, c2)]


def _gather_send(o_refs, send, recv):
    x, y, z = _place()
    chip = 2 * x + y
    for a, o in enumerate(o_refs):
        r2 = o.shape[1] // 2
        mine = _half(o, chip, z)
        for k, (dx, dy) in enumerate(_CHIP_OFFSETS):
            pltpu.make_async_remote_copy(
                src_ref=mine, dst_ref=mine, send_sem=send.at[a, k], recv_sem=recv.at[a, k],
                device_id=(_flip(x, dx), _flip(y, dy), z), device_id_type=MESH).start()


def _gather_landed(o_refs, send, recv, then=None):
    x, y, z = _place()
    chip = 2 * x + y
    for a, o in enumerate(o_refs):
        for k, (dx, dy) in enumerate(_CHIP_OFFSETS):
            landed = _half(o, 2 * _flip(x, dx) + _flip(y, dy), z)
            pltpu.make_async_remote_copy(
                src_ref=landed, dst_ref=landed, send_sem=send.at[a, k], recv_sem=recv.at[a, k],
                device_id=(_flip(x, dx), _flip(y, dy), z), device_id_type=MESH).wait_recv()
            if then is not None:
                then(a, k, landed)
    for a, o in enumerate(o_refs):
        mine = _half(o, chip, z)
        for k, (dx, dy) in enumerate(_CHIP_OFFSETS):
            pltpu.make_async_remote_copy(
                src_ref=mine, dst_ref=mine, send_sem=send.at[a, k], recv_sem=recv.at[a, k],
                device_id=(_flip(x, dx), _flip(y, dy), z), device_id_type=MESH).wait_send()


def _pass_on(o_refs, fsend, frecv, a, k, landed):
    x, y, z = _place()
    pltpu.make_async_remote_copy(
        src_ref=landed, dst_ref=landed, send_sem=fsend.at[a, k], recv_sem=frecv.at[a, k],
        device_id=(x, y, 1 - z), device_id_type=MESH).start()


def _passed_on(o_refs, fsend, frecv):
    x, y, z = _place()
    for a, o in enumerate(o_refs):
        for k, (dx, dy) in enumerate(_CHIP_OFFSETS):
            other = 2 * _flip(x, dx) + _flip(y, dy)
            got = _half(o, other, 1 - z)
            gave = _half(o, other, z)
            pltpu.make_async_remote_copy(
                src_ref=got, dst_ref=got, send_sem=fsend.at[a, k], recv_sem=frecv.at[a, k],
                device_id=(x, y, 1 - z), device_id_type=MESH).wait_recv()
            pltpu.make_async_remote_copy(
                src_ref=gave, dst_ref=gave, send_sem=fsend.at[a, k], recv_sem=frecv.at[a, k],
                device_id=(x, y, 1 - z), device_id_type=MESH).wait_send()


def _gather_finish(o_refs, send, recv, fsend, frecv):
    _gather_landed(o_refs, send, recv, functools.partial(_pass_on, o_refs, fsend, frecv))
    _passed_on(o_refs, fsend, frecv)


class _Rider:
    def __init__(self, ins, out_shapes, sems, start, finish, aliases=None, in_specs=None, out_specs=None, middle=None):
        self.ins, self.out_shapes, self.sems = list(ins), list(out_shapes), list(sems)
        self.start, self.finish, self.aliases = start, finish, dict(aliases or {})
        self.middle = middle
        self.in_specs = list(in_specs) if in_specs else [_ANY] * len(self.ins)
        self.out_specs = list(out_specs) if out_specs else [_ANY] * len(self.out_shapes)


def _run_rider(rider, name):
    r_in, r_out = len(rider.ins), len(rider.out_shapes)

    def body(*refs):
        ins, outs, sems = refs[:r_in], refs[r_in:r_in + r_out], refs[r_in + r_out:]
        rider.start(ins, outs, sems)
        if rider.middle is not None:
            rider.middle(ins, outs, sems)
        rider.finish(ins, outs, sems)

    return pl.pallas_call(
        body, name=name, in_specs=rider.in_specs, out_specs=rider.out_specs, out_shape=rider.out_shapes,
        input_output_aliases=rider.aliases, scratch_shapes=rider.sems,
    )(*rider.ins)


def _hosted_call(body, args, *, name, grid, in_specs, out_specs, out_shape, scratch_shapes=(), sem, rider=None,
                 prefetch=()):
    scratch_shapes = list(scratch_shapes)
    n_pf, n_in, n_out, n_sc = len(prefetch), len(in_specs), len(out_specs), len(scratch_shapes)
    r_in, r_out = (len(rider.ins), len(rider.out_shapes)) if rider else (0, 0)
    last = tuple(g - 1 for g in grid)

    def hosted(*refs):
        p = 0
        parts = []
        for cnt in (n_pf, n_in, r_in, n_out, r_out, n_sc):
            parts.append(refs[p:p + cnt])
            p += cnt
        pf, ins, r_ins, outs, r_outs, scratch = parts
        sems = refs[p:]
        ids = [pl.program_id(a) for a in range(len(grid))]
        is_first = functools.reduce(jnp.logical_and, [i == 0 for i in ids])
        is_last = functools.reduce(jnp.logical_and, [i == e for i, e in zip(ids, last)])

        @pl.when(is_first)
        def _():
            rider.start(r_ins, r_outs, sems)

        if rider.middle is not None:
            linear = functools.reduce(lambda acc, ig: acc * ig[1] + ig[0], zip(ids, grid), 0)

            @pl.when(linear == math.prod(grid) * 3 // 4)
            def _():
                rider.middle(r_ins, r_outs, sems)

        body(*pf, *ins, *outs, *scratch)

        @pl.when(is_last)
        def _():
            rider.finish(r_ins, r_outs, sems)

    if rider is None:
        kern, all_in, all_out, shapes, scratch, aliases, extra = body, list(in_specs), list(out_specs), list(out_shape), \
            scratch_shapes, {}, []
    else:
        kern, all_in, all_out = hosted, list(in_specs) + rider.in_specs, list(out_specs) + rider.out_specs
        shapes, scratch, extra = list(out_shape) + rider.out_shapes, scratch_shapes + rider.sems, rider.ins
        aliases = {n_pf + n_in + i: n_out + j for i, j in rider.aliases.items()}
        sem = ("arbitrary",) * len(grid)
    if prefetch:
        spec = dict(grid_spec=pltpu.PrefetchScalarGridSpec(
            num_scalar_prefetch=n_pf, grid=grid, in_specs=all_in, out_specs=all_out, scratch_shapes=scratch))
    else:
        spec = dict(grid=grid, in_specs=all_in, out_specs=all_out, scratch_shapes=scratch)
    res = pl.pallas_call(kern, name=name, out_shape=shapes, input_output_aliases=aliases,
                         compiler_params=_params(sem, VMEM_LIMIT), **spec)(*prefetch, *args, *extra)
    return list(res[:n_out]), list(res[n_out:])


def _gather_rider(ws, staged=False):
    n = len(ws)
    shapes = [jax.ShapeDtypeStruct(w.shape, w.dtype) for w in ws]
    sems = [pltpu.SemaphoreType.DMA((n, 3))] * 4
    aliases = {a: a for a in range(n)}

    def start(ins, outs, s):
        _gather_send(outs, s[0], s[1])

    if not staged:
        return _Rider(ws, shapes, sems, start, lambda ins, outs, s: _gather_finish(outs, *s), aliases=aliases)
    return _Rider(
        ws, shapes, sems, start, lambda ins, outs, s: _passed_on(outs, s[2], s[3]), aliases=aliases,
        middle=lambda ins, outs, s: _gather_landed(outs, s[0], s[1], functools.partial(_pass_on, outs, s[2], s[3])))


def _gather_ici_rider(ws):
    n = len(ws)
    return _Rider(
        ws, [jax.ShapeDtypeStruct(w.shape, w.dtype) for w in ws], [pltpu.SemaphoreType.DMA((n, 3))] * 2,
        lambda ins, outs, sems: _gather_send(outs, sems[0], sems[1]),
        lambda ins, outs, sems: _gather_landed(outs, sems[0], sems[1]),
        aliases={a: a for a in range(n)})


def _gather_d2d_rider(ws):
    n = len(ws)

    def start(ins, outs, sems):
        x, y, z = _place()
        for a, o in enumerate(outs):
            for k, (dx, dy) in enumerate(_CHIP_OFFSETS):
                _pass_on(outs, sems[0], sems[1], a, k, _half(o, 2 * _flip(x, dx) + _flip(y, dy), z))

    return _Rider(
        ws, [jax.ShapeDtypeStruct(w.shape, w.dtype) for w in ws], [pltpu.SemaphoreType.DMA((n, 3))] * 2,
        start, lambda ins, outs, sems: _passed_on(outs, sems[0], sems[1]), aliases={a: a for a in range(n)})


def _copies_rider(ins, out_shapes, sem_shape, make):
    def start(r_ins, r_outs, sems):
        for cp in make(r_ins, r_outs, sems[0], sems[1]):
            cp.start()

    def finish(r_ins, r_outs, sems):
        for cp in make(r_ins, r_outs, sems[0], sems[1]):
            cp.wait()

    return _Rider(ins, out_shapes, [pltpu.SemaphoreType.DMA(sem_shape)] * 2, start, finish)


def _exchange_rider(gs):
    def make(g_refs, r_refs, send, recv):
        x, y, z = _place()
        return [pltpu.make_async_remote_copy(
            src_ref=g.at[:, pl.ds((1 - z) * (g.shape[1] // 2), g.shape[1] // 2)], dst_ref=r, send_sem=send.at[a],
            recv_sem=recv.at[a], device_id=(x, y, 1 - z), device_id_type=MESH)
            for a, (g, r) in enumerate(zip(g_refs, r_refs))]

    shapes = [jax.ShapeDtypeStruct((g.shape[0], g.shape[1] // 2, g.shape[2]), g.dtype) for g in gs]
    return _copies_rider(gs, shapes, (len(gs),), make)


def _add_half(g, recv, core, name):
    s, r, c = g.shape
    r2 = r // 2
    rb = r2
    for cand in (256, 128, 64):
        if r2 % cand == 0:
            rb = cand
            break
    g4 = g.reshape(s, 2, r2, c)

    def body(core_ref, g_ref, r_ref, o_ref):
        o_ref[...] = (g_ref[...].astype(F32) + r_ref[...].astype(F32)).astype(BF16)

    return pl.pallas_call(
        body, name=name,
        grid_spec=pltpu.PrefetchScalarGridSpec(
            num_scalar_prefetch=1, grid=(s, r2 // rb),
            in_specs=[pl.BlockSpec((None, None, rb, c), lambda i, j, cr: (i, cr[0], j, 0)),
                      pl.BlockSpec((None, rb, c), lambda i, j, cr: (i, j, 0))],
            out_specs=pl.BlockSpec((None, rb, c), lambda i, j, cr: (i, j, 0))),
        out_shape=jax.ShapeDtypeStruct((s, r2, c), BF16),
        compiler_params=_params(("parallel", "parallel")),
    )(core, g4, recv)


def _scatter_rider(ps):
    def make(p_refs, o_refs, send, recv):
        x, y, z = _place()
        copies = []
        for a, (p, o) in enumerate(zip(p_refs, o_refs)):
            for k, (dx, dy) in enumerate(_CHIP_OFFSETS):
                other = 2 * _flip(x, dx) + _flip(y, dy)
                copies.append(pltpu.make_async_remote_copy(
                    src_ref=p.at[other], dst_ref=o.at[k], send_sem=send.at[a, k], recv_sem=recv.at[a, k],
                    device_id=(_flip(x, dx), _flip(y, dy), z), device_id_type=MESH))
        return copies

    shapes = [jax.ShapeDtypeStruct((3,) + p.shape[1:], p.dtype) for p in ps]
    return _copies_rider(ps, shapes, (len(ps), 3), make)


def _sum_chips(p, landed, chip, name):
    _, r2, c = p.shape
    rb = r2
    for cand in (256, 128, 64):
        if r2 % cand == 0:
            rb = cand
            break

    def body(s_ref, p_ref, l_ref, o_ref):
        acc = p_ref[...].astype(F32)
        for k in range(3):
            acc = acc + l_ref[k].astype(F32)
        o_ref[...] = acc

    return pl.pallas_call(
        body, name=name,
        grid_spec=pltpu.PrefetchScalarGridSpec(
            num_scalar_prefetch=1, grid=(r2 // rb,),
            in_specs=[pl.BlockSpec((None, rb, c), lambda i, s: (s[0], i, 0)),
                      pl.BlockSpec((3, rb, c), lambda i, s: (0, i, 0))],
            out_specs=pl.BlockSpec((rb, c), lambda i, s: (i, 0))),
        out_shape=jax.ShapeDtypeStruct((r2, c), F32),
        compiler_params=_params(("parallel",)),
    )(chip, p, landed)


def _swap_rider(hs):
    def make(h_refs, o_refs, send, recv):
        x, y, z = _place()
        return [pltpu.make_async_remote_copy(
            src_ref=h, dst_ref=o, send_sem=send.at[a], recv_sem=recv.at[a], device_id=(x, y, 1 - z),
            device_id_type=MESH) for a, (h, o) in enumerate(zip(h_refs, o_refs))]

    return _copies_rider(hs, [jax.ShapeDtypeStruct(h.shape, h.dtype) for h in hs], (len(hs),), make)


def _reduce_scatter_vmem(gs, rows, rider, name):
    n = len(gs)
    r_in, r_out = len(rider.ins), len(rider.out_shapes)
    halves = [(r // 2, g.shape[-1]) for g, (r, _) in zip(gs, rows)]

    def body(*refs):
        p = 0
        parts = []
        for cnt in (n, r_in, n, n, r_out, n, n, n, 6):
            parts.append(refs[p:p + cnt])
            p += cnt
        g_refs, r_ins, mine, theirs, r_outs, recv, part, land, sems = parts
        r_sems = refs[p:]
        xs, xr, ss, sr, ws, wr = sems
        x, y, z = _place()
        chip = 2 * x + y
        sib = (x, y, 1 - z)
        rider.start(r_ins, r_outs, r_sems)

        def half_of(a, s, which):
            r2 = halves[a][0]
            if len(g_refs[a].shape) == 3:
                return g_refs[a].at[s, pl.ds(pl.multiple_of(which * r2, 8), r2)]
            return g_refs[a].at[pl.ds(pl.multiple_of(s * rows[a][1] + which * r2, 8), r2)]

        exchange = [pltpu.make_async_remote_copy(
            src_ref=half_of(a, s, 1 - z), dst_ref=recv[a].at[s], send_sem=xs.at[a, s], recv_sem=xr.at[a, s],
            device_id=sib, device_id_type=MESH) for a in range(n) for s in range(N_CHIPS)]
        for cp in exchange:
            cp.start()
        for cp in exchange:
            cp.wait()
        for a in range(n):
            for s in range(N_CHIPS):
                part[a][s] = (half_of(a, s, z)[...] + recv[a][s]).astype(BF16)
        scatter = []
        for a in range(n):
            for k, (dx, dy) in enumerate(_CHIP_OFFSETS):
                other = 2 * _flip(x, dx) + _flip(y, dy)
                scatter.append(pltpu.make_async_remote_copy(
                    src_ref=part[a].at[other], dst_ref=land[a].at[k], send_sem=ss.at[a, k], recv_sem=sr.at[a, k],
                    device_id=(_flip(x, dx), _flip(y, dy), z), device_id_type=MESH))
        for cp in scatter:
            cp.start()
        for cp in scatter:
            cp.wait()
        for a in range(n):
            acc = part[a][chip].astype(F32)
            for k in range(3):
                acc = acc + land[a][k].astype(F32)
            mine[a][...] = acc
        swap = [pltpu.make_async_remote_copy(
            src_ref=mine[a], dst_ref=theirs[a], send_sem=ws.at[a], recv_sem=wr.at[a], device_id=sib,
            device_id_type=MESH) for a in range(n)]
        for cp in swap:
            cp.start()
        for cp in swap:
            cp.wait()
        rider.finish(r_ins, r_outs, r_sems)

    half_shapes = [jax.ShapeDtypeStruct(h, F32) for h in halves]
    res = pl.pallas_call(
        body, name=name, in_specs=[_VMEM] * n + rider.in_specs, out_specs=[_VMEM] * (2 * n) + rider.out_specs,
        out_shape=half_shapes + half_shapes + rider.out_shapes,
        scratch_shapes=[pltpu.VMEM((N_CHIPS,) + h, F32) for h in halves] + [pltpu.VMEM((N_CHIPS,) + h, BF16) for h in halves]
        + [pltpu.VMEM((3,) + h, BF16) for h in halves]
        + [pltpu.SemaphoreType.DMA((n, N_CHIPS))] * 2 + [pltpu.SemaphoreType.DMA((n, 3))] * 2
        + [pltpu.SemaphoreType.DMA((n,))] * 2 + rider.sems,
        input_output_aliases={n + i: 2 * n + j for i, j in rider.aliases.items()},
        compiler_params=_params(None, VMEM_LIMIT),
    )(*gs, *rider.ins)
    return list(res[:n]), list(res[n:2 * n]), list(res[2 * n:])


SMALL_ROWS = 32
PACK_ROWS = 16


def _pack_small(st_post, st_ret, st_pre):
    d = st_post.shape[2]

    def body(po_ref, re_ref, pr_ref, o_ref):
        o_ref[...] = jnp.zeros(o_ref.shape, F32)
        o_ref[0:1, :] = pr_ref[0, 2:3, :] + pr_ref[1, 2:3, :] + pr_ref[2, 2:3, :]
        o_ref[1:2, :] = po_ref[0, 4:5, :] + po_ref[1, 4:5, :]
        o_ref[2:3, :] = po_ref[0, 5:6, :] + po_ref[1, 5:6, :]
        o_ref[3:4, 0:512] = re_ref[0, 0:1, :] + re_ref[1, 0:1, :]
        o_ref[4:5, :] = pr_ref[0, 3:4, :] + pr_ref[1, 3:4, :] + pr_ref[2, 3:4, :]
        o_ref[5:6, :] = pr_ref[0, 4:5, :] + pr_ref[1, 4:5, :] + pr_ref[2, 4:5, :]
        lane = lax.broadcasted_iota(jnp.int32, (1, LANES), 1)
        for row, src in ((6, 1), (10, 2)):
            acc = jnp.zeros((1, LANES), F32)
            for hd in range(HEADS):
                grp = re_ref[0, src:src + 1, hd * LANES:(hd + 1) * LANES] + re_ref[1, src:src + 1, hd * LANES:(hd + 1) * LANES]
                acc = acc + jnp.where(lane == hd, grp, 0.0)
            o_ref[row:row + 1, 0:LANES] = acc
        o_ref[7:8, :] = po_ref[0, 6:7, :] + po_ref[1, 6:7, :]
        o_ref[8:9, :] = pr_ref[2, 0:1, :]
        o_ref[9:10, :] = pr_ref[2, 1:2, :]
        for e in range(2):
            b = 12 + 6 * e
            o_ref[b:b + 1, :] = pr_ref[e, 0:1, :]
            o_ref[b + 1:b + 2, :] = pr_ref[e, 1:2, :]
            o_ref[b + 2:b + 3, :] = po_ref[e, 3:4, :]
            o_ref[b + 3:b + 4, :] = po_ref[e, 0:1, :]
            o_ref[b + 4:b + 5, :] = po_ref[e, 1:2, :]
            o_ref[b + 5:b + 6, :] = po_ref[e, 2:3, :]

    return pl.pallas_call(body, name="pack_small", out_shape=jax.ShapeDtypeStruct((SMALL_ROWS, d), F32))(st_post, st_ret, st_pre)


def _small_reduce(gathered):
    d = gathered.shape[2]

    def body(g_ref, o_ref):
        tot = g_ref[0, 0:PACK_ROWS, :]
        for dev in range(1, N_DEV):
            tot = tot + g_ref[dev, 0:PACK_ROWS, :]
        o_ref[0:PACK_ROWS, :] = tot
        for j in range(6):
            acc = g_ref[0, 12 + j:13 + j, :] + g_ref[0, 18 + j:19 + j, :]
            for dev in range(1, N_DEV):
                acc = acc + g_ref[dev, 12 + j:13 + j, :] + g_ref[dev, 18 + j:19 + j, :]
            if j < 2:
                acc = acc + o_ref[8 + j:9 + j, :]
            o_ref[PACK_ROWS + j:PACK_ROWS + j + 1, :] = acc
        o_ref[PACK_ROWS + 6:PACK_ROWS + 8, :] = jnp.zeros((2, d), F32)

    return pl.pallas_call(body, name="small_reduce", out_shape=jax.ShapeDtypeStruct((PACK_ROWS + 8, d), F32))(gathered)


_SMALL = (("g_attn", 0, 1024), ("g_ffn", 1, 1024), ("g_final", 2, 1024), ("g_ret", 3, 512), ("g_q_lora", 4, 384),
          ("g_kv_lora", 5, 256), ("ret_decay_fwd", 6, HEADS), ("ret_decay_bwd", 10, HEADS))
_SMALL_NAMES = tuple(s[0] for s in _SMALL) + ("c_ctx", "b_ada")


def _small_final(tot, dcc, sg8, ws, ms, vs):
    d = tot.shape[1]
    n = len(_SMALL_NAMES)

    def body(*refs):
        t_ref, dcc_ref, sg_ref = refs[0:3]
        w_refs, m_refs, v_refs = refs[3:3 + n], refs[3 + n:3 + 2 * n], refs[3 + 2 * n:3 + 3 * n]
        outs = refs[3 + 3 * n:]
        g_refs, d_refs, mo_refs, vo_refs = outs[0:n], outs[n:2 * n], outs[2 * n:3 * n], outs[3 * n:4 * n]
        l_ref = outs[4 * n]

        def update(i, g, sl=None):
            pick = (lambda r: r[...]) if sl is None else (lambda r: r[:, sl])
            dl, mn, vn = _adam_math(pick(w_refs[i]), g, pick(m_refs[i]), pick(v_refs[i]))
            if sl is None:
                g_refs[i][...], d_refs[i][...], mo_refs[i][...], vo_refs[i][...] = g, dl, mn, vn
            else:
                g_refs[i][:, sl], d_refs[i][:, sl], mo_refs[i][:, sl], vo_refs[i][:, sl] = g, dl, mn, vn

        for i, (name, row, width) in enumerate(_SMALL):
            g = t_ref[row:row + 1, 0:width]
            if name == "ret_decay_fwd":
                g = g * sg_ref[0:1, 0:width]
            elif name == "ret_decay_bwd":
                g = g * sg_ref[1:2, 0:width]
            update(i, g)
        i_cc, i_b = n - 2, n - 1
        cc = w_refs[i_cc][...]
        s = 1.0 / (1.0 + jnp.exp(-cc))
        dsilu = dcc_ref[0, 0:1, :] + dcc_ref[2, 0:1, :] + dcc_ref[4, 0:1, :] + dcc_ref[6, 0:1, :]
        update(i_cc, dsilu * (s * (1.0 + cc * (1.0 - s))))
        for j in range(6):
            update(i_b, t_ref[PACK_ROWS + j:PACK_ROWS + j + 1, :], pl.ds(j * d, d))
        l_ref[...] = jnp.broadcast_to((0.5 / d) * jnp.sum(t_ref[7:8, :], keepdims=True), l_ref.shape)

    shapes = [jax.ShapeDtypeStruct(a.shape, F32) for a in ws]
    outs = pl.pallas_call(
        body, name="small_final", out_shape=shapes * 4 + [jax.ShapeDtypeStruct((8, LANES), F32)],
    )(tot, dcc, sg8, *ws, *ms, *vs)
    return outs[0:n], outs[n:2 * n], outs[2 * n:3 * n], outs[3 * n:4 * n], outs[4 * n]


_WEIGHTS = ("c_ctx", "w_ada", "b_ada", "g_attn", "g_ffn", "w_in", "ret_decay_fwd", "ret_decay_bwd", "g_ret", "g_q_lora",
            "w_uq", "g_kv_lora", "w_ukv", "w_out", "w_ff1", "w_ff2", "g_final")
_BIG = ("w_in", "w_uq", "w_ukv", "w_out", "w_ff1", "w_ff2")
_TRANSPOSED = ("w_in", "w_uq")


def kernel(x, c, ctx, c_ctx, w_ada, b_ada, g_attn, g_ffn, w_in, ret_decay_fwd, ret_decay_bwd, g_ret, g_q_lora, w_uq, g_kv_lora, w_ukv, w_out, w_ff1, w_ff2, g_final, loss_target, m_c_ctx, m_w_ada, m_b_ada, m_g_attn, m_g_ffn, m_w_in, m_ret_decay_fwd, m_ret_decay_bwd, m_g_ret, m_g_q_lora, m_w_uq, m_g_kv_lora, m_w_ukv, m_w_out, m_w_ff1, m_w_ff2, m_g_final, v_c_ctx, v_w_ada, v_b_ada, v_g_attn, v_g_ffn, v_w_in, v_ret_decay_fwd, v_ret_decay_bwd, v_g_ret, v_g_q_lora, v_w_uq, v_g_kv_lora, v_w_ukv, v_w_out, v_w_ff1, v_w_ff2, v_g_final):
    w = dict(c_ctx=c_ctx, w_ada=w_ada, b_ada=b_ada, g_attn=g_attn, g_ffn=g_ffn, w_in=w_in, ret_decay_fwd=ret_decay_fwd,
             ret_decay_bwd=ret_decay_bwd, g_ret=g_ret, g_q_lora=g_q_lora, w_uq=w_uq, g_kv_lora=g_kv_lora, w_ukv=w_ukv,
             w_out=w_out, w_ff1=w_ff1, w_ff2=w_ff2, g_final=g_final)
    m = dict(c_ctx=m_c_ctx, w_ada=m_w_ada, b_ada=m_b_ada, g_attn=m_g_attn, g_ffn=m_g_ffn, w_in=m_w_in,
             ret_decay_fwd=m_ret_decay_fwd, ret_decay_bwd=m_ret_decay_bwd, g_ret=m_g_ret, g_q_lora=m_g_q_lora, w_uq=m_w_uq,
             g_kv_lora=m_g_kv_lora, w_ukv=m_w_ukv, w_out=m_w_out, w_ff1=m_w_ff1, w_ff2=m_w_ff2, g_final=m_g_final)
    v = dict(c_ctx=v_c_ctx, w_ada=v_w_ada, b_ada=v_b_ada, g_attn=v_g_attn, g_ffn=v_g_ffn, w_in=v_w_in,
             ret_decay_fwd=v_ret_decay_fwd, ret_decay_bwd=v_ret_decay_bwd, g_ret=v_g_ret, g_q_lora=v_g_q_lora, w_uq=v_w_uq,
             g_kv_lora=v_g_kv_lora, w_ukv=v_w_ukv, w_out=v_w_out, w_ff1=v_w_ff1, w_ff2=v_w_ff2, g_final=v_g_final)
    xi, yi, ci = lax.axis_index("x"), lax.axis_index("y"), lax.axis_index("c")
    chip = 2 * xi + yi
    dev = 2 * chip + ci
    nex, seq, d = x.shape
    n_ada = w_ada.shape[2]

    dec = jnp.zeros((8, LANES), F32).at[0, :HEADS].set(ret_decay_fwd[0]).at[1, :HEADS].set(ret_decay_bwd[0])
    lg8, sg8 = _decay_prep(dec)
    lg = lg8[:2, :HEADS]

    def shard_of(t, k):
        return t[k][0].T if k in _TRANSPOSED else t[k][0]

    shard = {k: shard_of(w, k) for k in _BIG}
    head_rows = MLA_NOPE + MLA_ROPE
    shard["w_uq"] = jnp.pad(shard["w_uq"], ((0, MLA_HEAD - head_rows), (0, 0)))
    slot = chip.reshape(1).astype(jnp.int32)
    core = ci.reshape(1).astype(jnp.int32)
    slots = {k: _cast_into_slot(shard[k], slot, "cast_" + k)[0] for k in _BIG if k != "w_ff1"}
    slots["w_ff1"], (w_in_x, w_uq_x, w_ukv_x, c8) = _cast_into_slot(
        shard["w_ff1"], slot, "cast_w_ff1",
        rider=_merge_riders(_gather_ici_rider([slots[k] for k in _EARLY]),
                            _gather8_rider(jnp.pad(c, ((0, 8 - nex), (0, 0))), in_vmem=False)))

    a_in = jnp.concatenate([c8[:, :nex].reshape(N_DEV * nex, d), c_ctx.reshape(1, d), jnp.zeros((7, d), F32)], axis=0)
    b_sh = lax.dynamic_slice(b_ada, (0, chip * n_ada), (1, n_ada))
    mod_sh = _mod_fwd(a_in, w_ada[0], b_sh)
    mod8, w_in_f, w_uq_k, w_ukv_k = _run_rider(
        _merge_riders(_gather8_rider(mod_sh), _gather_d2d_rider([w_in_x, w_uq_x, w_ukv_x])), "ag_early")
    w_in_k = jnp.pad(w_in_f.reshape(IN_COLS, d), ((0, IN_PAD - IN_COLS), (0, 0)))
    mod_all = mod8[0::2].transpose(1, 0, 2).reshape(a_in.shape[0], N_CHIPS * n_ada)
    mod_me = lax.dynamic_slice(mod_all, (nex * dev, 0), (nex, N_CHIPS * n_ada)).reshape(nex, 6, d)
    mod_c = mod_all[N_DEV * nex].reshape(1, 6, d)
    modv = jnp.pad(jnp.concatenate([mod_me, mod_c], axis=0), ((0, 0), (0, 2), (0, 0)))

    gx, g_early, late, st_post, st_ret, st_pre = _local_step(
        x, ctx, loss_target, modv, lg, g_attn, g_ffn, g_final.reshape(1, d), g_ret, g_q_lora, g_kv_lora,
        w_in_k, w_uq_k, w_ukv_k, [slots[k] for k in _LATE], (core, slot))

    mine, theirs, (*late_theirs, gathered) = _reduce_scatter_vmem(
        g_early, [(IN_COLS // N_CHIPS, IN_COLS // N_CHIPS), (head_rows, MLA_HEAD), (KV_LORA, KV_LORA)],
        _merge_riders(_swap_rider(late), _gather8_rider(_pack_small(st_post, st_ret, st_pre))), "rs_early")
    tot = _small_reduce(gathered)
    dm = jnp.concatenate([
        gathered[:, 12:24].reshape(N_DEV * nex, 6 * d),
        jnp.concatenate([tot[8:10].reshape(1, 2 * d), jnp.zeros((1, 4 * d), F32)], axis=1),
        jnp.zeros((7, 6 * d), F32)], axis=0)
    dm_sh = lax.dynamic_slice(dm, (0, chip * n_ada), (dm.shape[0], n_ada))
    g_ada, da = _mod_bwd(a_in, dm_sh, w_ada[0])
    dcc = _allgather8(da[N_DEV * nex:], "ag_dcc")
    halves = dict(zip(_EARLY, zip(mine, theirs)))
    halves.update(zip(_LATE, zip(late, late_theirs)))
    grad, delta, new_m, new_v = {}, {}, {}, {}
    for k in _BIG:
        a, b = halves[k]
        res = _adamw_halves(shard_of(w, k), a, b, shard_of(m, k), shard_of(v, k), core, "adamw_" + k)
        grad[k], delta[k], new_m[k], new_v[k] = [(o.T if k in _TRANSPOSED else o).reshape(w[k].shape) for o in res]

    shp = w_ada.shape
    outs, _ = _adamw(w_ada[0], g_ada, m["w_ada"][0], v["w_ada"][0], "adamw_w_ada")
    grad["w_ada"] = g_ada.reshape(shp)
    delta["w_ada"], new_m["w_ada"], new_v["w_ada"] = [o.reshape(shp) for o in outs]
    rows = [{k: t[k].reshape(1, -1) for k in _SMALL_NAMES} for t in (w, m, v)]
    small = _small_final(tot, dcc, sg8, *[[t[k] for k in _SMALL_NAMES] for t in rows])
    for res, outs in zip((grad, delta, new_m, new_v), small[:4]):
        for k, o in zip(_SMALL_NAMES, outs):
            res[k] = o.reshape(w[k].shape)
    return (small[4][0, 0], gx, *[grad[k] for k in _WEIGHTS], *[delta[k] for k in _WEIGHTS],
            *[new_m[k] for k in _WEIGHTS], *[new_v[k] for k in _WEIGHTS])
```

```python
import functools
import math

import jax
import jax.numpy as jnp
from jax import lax
from jax.experimental import pallas as pl
from jax.experimental.pallas import tpu as pltpu

F32 = jnp.float32
BF16 = jnp.bfloat16
MESH = pl.DeviceIdType.MESH

EPS = 1e-6
D_MODEL = 1024
D_FF = 4096
HEADS = 4
RET_DK = 64
RET_DV = 128
MLA_NOPE = 128
MLA_ROPE = 64
MLA_HEAD = 256
Q_LORA = 384
KV_LORA = 256
GRID_W = 64
ROPE_BASE = 10000.0
IN_COLS = 2240
IN_PAD = 2304
PG_COLS = 1152
N_CHIPS = 4
N_DEV = 8
LANES = 128
ADAM_LR = 0.001
ADAM_B1 = 0.9
ADAM_B2 = 0.999
ADAM_EPS = 1e-08
ADAM_WD = 0.01
ADAM_STEP = 10
VMEM_LIMIT = 56 * 1024 * 1024


def _dot(a, b):
    return jnp.dot(a, b, preferred_element_type=F32)


def _dot_nt(a, b):
    return lax.dot_general(a, b, (((1,), (1,)), ((), ())), preferred_element_type=F32)


def _dot_tn(a, b):
    return lax.dot_general(a, b, (((0,), (0,)), ((), ())), preferred_element_type=F32)


def _params(sem=None, vmem=None):
    return pltpu.CompilerParams(dimension_semantics=sem, vmem_limit_bytes=vmem)


def _full(shape):
    n = len(shape)
    return pl.BlockSpec(shape, lambda *_: (0,) * n)


def _rope(x, cos, sin):
    w = x.shape[-1]
    lo = (lax.broadcasted_iota(jnp.int32, (1, w), 1) % 64) < 32
    swapped = jnp.where(lo, pltpu.roll(x, w - 32, 1), pltpu.roll(x, 32, 1))
    return x * cos + swapped * sin


def _rope_t(g, cos, sin):
    w = g.shape[-1]
    lo = (lax.broadcasted_iota(jnp.int32, (1, w), 1) % 64) < 32
    t = g * sin
    swapped = jnp.where(lo, pltpu.roll(t, w - 32, 1), pltpu.roll(t, 32, 1))
    return g * cos + swapped


def _rope_tables(seq, tm):
    rows = seq // GRID_W
    row = jnp.repeat(jnp.arange(rows, dtype=F32), GRID_W)
    col = jnp.tile(jnp.arange(GRID_W, dtype=F32), rows)
    n_freq = RET_DK // 4
    freq = ROPE_BASE ** (-jnp.arange(n_freq, dtype=F32) / n_freq)
    ang = jnp.concatenate([row[:, None] * freq, col[:, None] * freq], axis=-1)
    cos, sin = jnp.cos(ang), jnp.sin(ang)
    cos_t = jnp.tile(jnp.concatenate([cos, cos], -1), (1, HEADS))
    sin_t = jnp.tile(jnp.concatenate([-sin, sin], -1), (1, HEADS))
    cos_t = jnp.concatenate([cos_t, jnp.ones((tm, 4 * RET_DK), F32)], 0)
    sin_t = jnp.concatenate([sin_t, jnp.zeros((tm, 4 * RET_DK), F32)], 0)
    return cos_t, sin_t


def _adam_math(w, g, m, v):
    mn = ADAM_B1 * m + (1.0 - ADAM_B1) * g
    vn = ADAM_B2 * v + (1.0 - ADAM_B2) * (g * g)
    m_hat = mn / (1.0 - ADAM_B1 ** ADAM_STEP)
    v_hat = vn / (1.0 - ADAM_B2 ** ADAM_STEP)
    return -ADAM_LR * (m_hat / (jnp.sqrt(v_hat) + ADAM_EPS) + ADAM_WD * w), mn, vn


def _cast_into_slot(w, slot, name, rider=None):
    r, c = w.shape
    rb = max(b for b in range(16, 257, 16) if r % b == 0)

    def body(s_ref, w_ref, o_ref):
        o_ref[...] = w_ref[...].astype(BF16)

    (out,), carried = _hosted_call(
        body, (w,), name=name, grid=(r // rb,), prefetch=(slot,),
        in_specs=[pl.BlockSpec((rb, c), lambda i, s: (i, 0))],
        out_specs=[pl.BlockSpec((None, rb, c), lambda i, s: (s[0], i, 0))],
        out_shape=[jax.ShapeDtypeStruct((N_CHIPS, r, c), BF16)], sem=("parallel",), rider=rider)
    return out, carried


def _adamw_halves(w, mine, theirs, m, v, core, name):
    r, c = w.shape
    r2 = r // 2
    rb = max(b for b in range(8, r2 + 1, 8) if r2 % b == 0 and b * c * 4 <= (1 << 21))
    nbh = r2 // rb

    def body(z_ref, w_ref, a_ref, b_ref, m_ref, v_ref, g_ref, d_ref, mo_ref, vo_ref):
        here = (pl.program_id(0) // nbh) == z_ref[0]
        gg = jnp.where(here, a_ref[...], b_ref[...])
        g_ref[...] = gg
        d_ref[...], mo_ref[...], vo_ref[...] = _adam_math(w_ref[...], gg, m_ref[...], v_ref[...])

    spec = pl.BlockSpec((rb, c), lambda i, z: (i, 0))
    a_spec = pl.BlockSpec((rb, c), lambda i, z: (jnp.clip(i - z[0] * nbh, 0, nbh - 1), 0))
    b_spec = pl.BlockSpec((rb, c), lambda i, z: (jnp.clip(i - (1 - z[0]) * nbh, 0, nbh - 1), 0))
    shp = jax.ShapeDtypeStruct((r, c), F32)
    return pl.pallas_call(
        body, name=name,
        grid_spec=pltpu.PrefetchScalarGridSpec(
            num_scalar_prefetch=1, grid=(r // rb,), in_specs=[spec, a_spec, b_spec, spec, spec], out_specs=[spec] * 4),
        out_shape=[shp] * 4,
        compiler_params=_params(("parallel",)),
    )(core, w, mine, theirs, m, v)


def _adamw(w, g, m, v, name, rider=None):
    r, c = w.shape
    rb = r
    for cand in (256, 128, 64, 32, 16, 8):
        if r % cand == 0 and cand * c * 4 <= (1 << 20):
            rb = cand
            break
    if r * c * 4 <= (1 << 20):
        rb = r

    def body(w_ref, g_ref, m_ref, v_ref, d_ref, mo_ref, vo_ref):
        d_ref[...], mo_ref[...], vo_ref[...] = _adam_math(w_ref[...], g_ref[...], m_ref[...], v_ref[...])

    spec = pl.BlockSpec((rb, c), lambda i: (i, 0))
    shp = jax.ShapeDtypeStruct((r, c), F32)
    return _hosted_call(
        body, (w, g, m, v), name=name, grid=(r // rb,), in_specs=[spec] * 4, out_specs=[spec] * 3, out_shape=[shp] * 3,
        sem=("parallel",), rider=rider)


def _decay_prep(dec):
    def body(d_ref, lg_ref, sg_ref):
        d = d_ref[...]
        lg_ref[...] = jnp.minimum(d, 0.0) - jnp.log(1.0 + jnp.exp(-jnp.abs(d)))
        sg_ref[...] = 1.0 / (1.0 + jnp.exp(d))

    shp = jax.ShapeDtypeStruct(dec.shape, F32)
    return pl.pallas_call(body, name="decay_prep", out_shape=[shp, shp])(dec)


def _mod_fwd(a_in, w_ada, b_sh):
    rows, d = a_in.shape
    n = w_ada.shape[1]
    bn = 512

    def body(a_ref, w_ref, b_ref, o_ref):
        a = a_ref[...]
        s = (a / (1.0 + jnp.exp(-a))).astype(BF16)
        o_ref[...] = _dot(s, w_ref[...].astype(BF16)) + b_ref[...]

    return pl.pallas_call(
        body, name="mod_fwd", grid=(n // bn,),
        in_specs=[_full((rows, d)), pl.BlockSpec((d, bn), lambda j: (0, j)), pl.BlockSpec((1, bn), lambda j: (0, j))],
        out_specs=pl.BlockSpec((rows, bn), lambda j: (0, j)),
        out_shape=jax.ShapeDtypeStruct((rows, n), F32),
        compiler_params=_params(("parallel",)),
    )(a_in, w_ada, b_sh)


def _mod_bwd(a_in, dm, w_ada):
    rows, d = a_in.shape
    n = w_ada.shape[1]
    bn = 512
    nb = n // bn

    def body(a_ref, dm_ref, w_ref, gw_ref, da_ref):
        j = pl.program_id(0)
        a = a_ref[...]
        s = (a / (1.0 + jnp.exp(-a))).astype(BF16)
        dmb = dm_ref[...].astype(BF16)
        gw_ref[...] = _dot_tn(s, dmb)
        part = _dot_nt(dmb, w_ref[...].astype(BF16))

        @pl.when(j == 0)
        def _():
            da_ref[...] = part

        @pl.when(j > 0)
        def _():
            da_ref[...] += part

    return pl.pallas_call(
        body, name="mod_bwd", grid=(nb,),
        in_specs=[_full((rows, d)), pl.BlockSpec((rows, bn), lambda j: (0, j)), pl.BlockSpec((d, bn), lambda j: (0, j))],
        out_specs=[pl.BlockSpec((d, bn), lambda j: (0, j)), _full((rows, d))],
        out_shape=[jax.ShapeDtypeStruct((d, n), F32), jax.ShapeDtypeStruct((rows, d), F32)],
        compiler_params=_params(("arbitrary",)),
    )(a_in, dm, w_ada)


def _pre_fwd(x2, ctx2, modv, g_attn, w_in, g_q, g_kv, w_uq, w_ukv, cos_t, sin_t, *, seq, tm, rider=None):
    t_lat, d = x2.shape
    t_ctx = ctx2.shape[0]
    nl, nc = t_lat // tm, t_ctx // tm
    n_all = t_lat + t_ctx
    tpe = seq // tm
    nex = t_lat // seq

    def body(x_ref, c_ref, mod_ref, g_ref, win_ref, gq_ref, gkv_ref, wuq_ref, wukv_ref, cos_ref, sin_ref,
             h_ref, pg_ref, rq_ref, rk_ref, rv_ref, nq_ref, nkv_ref, q_ref, k_ref, v_ref):
        i = pl.program_id(0)
        xt = jnp.where(i < nl, x_ref[...], c_ref[...])
        sh = mod_ref[0, 0:1, :]
        sc = mod_ref[0, 1:2, :]
        r = lax.rsqrt(jnp.mean(xt * xt, axis=-1, keepdims=True) + EPS)
        hb = ((xt * r) * g_ref[...] * (1.0 + sc) + sh).astype(BF16)
        h_ref[...] = hb
        p = _dot_nt(hb, win_ref[...])
        cos = cos_ref[...]
        sin = sin_ref[...]
        rq_ref[...] = _rope(p[:, 0:256], cos, sin).astype(BF16)
        rk_ref[...] = _rope(p[:, 256:512] * (RET_DK ** -0.5), cos, sin).astype(BF16)
        rv_ref[...] = p[:, 512:1024].astype(BF16)
        pg_ref[...] = p[:, 1024:2176]
        cq = p[:, 1536:1920]
        ckv = p[:, 1920:2176]
        nqb = (cq * lax.rsqrt(jnp.mean(cq * cq, axis=-1, keepdims=True) + EPS) * gq_ref[...]).astype(BF16)
        nkvb = (ckv * lax.rsqrt(jnp.mean(ckv * ckv, axis=-1, keepdims=True) + EPS) * gkv_ref[...]).astype(BF16)
        nq_ref[...] = nqb
        nkv_ref[...] = nkvb
        cos1 = cos[:, 0:LANES]
        sin1 = sin[:, 0:LANES]
        kpe = _rope(p[:, 2176:2304], cos1, sin1).astype(BF16)
        for hd in range(HEADS):
            o = hd * MLA_HEAD
            qh = _dot_nt(nqb, wuq_ref[hd]) * MLA_SCALE
            q_ref[:, o:o + 128] = qh[:, 0:128].astype(BF16)
            q_ref[:, o + 128:o + 256] = _rope(qh[:, 128:256], cos1, sin1).astype(BF16)
            kvh = _dot(nkvb, wukv_ref[hd])
            k_ref[:, o:o + 128] = kvh[:, 0:128].astype(BF16)
            k_ref[:, o + 128:o + 256] = kpe
            v_ref[:, hd * 128:(hd + 1) * 128] = kvh[:, 128:256].astype(BF16)

    def tile(width):
        return pl.BlockSpec((tm, width), lambda i: (i, 0))

    widths = (d, PG_COLS, 256, 256, 512, Q_LORA, KV_LORA, HEADS * MLA_HEAD, HEADS * MLA_HEAD, HEADS * 128)
    dtypes = (BF16, F32, BF16, BF16, BF16, BF16, BF16, BF16, BF16, BF16)
    tab = pl.BlockSpec((tm, 256), lambda i: (jnp.where(i < nl, i % tpe, tpe), 0))
    return _hosted_call(
        body, (x2, ctx2, modv, g_attn, w_in, g_q, g_kv, w_uq, w_ukv, cos_t, sin_t), name="pre_fwd", grid=(nl + nc,),
        in_specs=[
            pl.BlockSpec((tm, d), lambda i: (jnp.minimum(i, nl - 1), 0)),
            pl.BlockSpec((tm, d), lambda i: (jnp.maximum(i - nl, 0), 0)),
            pl.BlockSpec((1, 8, d), lambda i: (jnp.minimum(i // tpe, nex), 0, 0)),
            _full((1, d)), _full(w_in.shape), _full((1, Q_LORA)), _full((1, KV_LORA)),
            _full(w_uq.shape), _full(w_ukv.shape), tab, tab,
        ],
        out_specs=[tile(w) for w in widths],
        out_shape=[jax.ShapeDtypeStruct((n_all, w), dt) for w, dt in zip(widths, dtypes)],
        sem=("parallel",), rider=rider)


def _post(yret, ymla, x2, tgt2, modv, g_ffn, g_fin, w_out, w_ff1, w_ff2, *, seq, tm):
    t_lat, d = x2.shape
    nl = t_lat // tm
    tpe = seq // tm
    nex = t_lat // seq
    n_slab = w_ff1.shape[0]
    fs = w_ff1.shape[2]

    def body(yr_ref, ym_ref, x_ref, t_ref, mod_ref, gf_ref, gl_ref, wo_ref, w1_ref, w2_ref,
             mix_ref, a_ref, du_ref, h2_ref, df_ref, dmo_ref, dmix_ref, dxm_ref, st_ref, ru_ref):
        i = pl.program_id(0)
        gt_a = mod_ref[0, 2:3, :]
        sh_f = mod_ref[0, 3:4, :]
        sc_f = mod_ref[0, 4:5, :]
        gt_f = mod_ref[0, 5:6, :]
        g_ffn_v = gf_ref[...]
        g_fin_v = gl_ref[...]
        yr = yr_ref[...]
        ym = ym_ref[...]
        mix_ref[:, 0:512] = yr
        mix_ref[:, 512:1024] = ym
        op = _dot(yr, wo_ref[0:512, :]) + _dot(ym, wo_ref[512:1024, :])
        x_mid = x_ref[...] + gt_a * op
        r2 = lax.rsqrt(jnp.mean(x_mid * x_mid, axis=-1, keepdims=True) + EPS)
        xh2 = x_mid * r2
        h2b = (xh2 * g_ffn_v * (1.0 + sc_f) + sh_f).astype(BF16)
        h2_ref[...] = h2b
        f = jnp.zeros((tm, d), F32)
        for s in range(n_slab):
            ru = jnp.maximum(_dot(h2b, w1_ref[s]), 0.0)
            ru_ref[:, s * fs:(s + 1) * fs] = ru
            ab = (ru * ru).astype(BF16)
            a_ref[:, s * fs:(s + 1) * fs] = ab
            f = f + _dot(ab, w2_ref[s * fs:(s + 1) * fs, :])
        x_out = x_mid + gt_f * f
        r3 = lax.rsqrt(jnp.mean(x_out * x_out, axis=-1, keepdims=True) + EPS)
        xh3 = x_out * r3
        err = xh3 * g_fin_v - t_ref[...]
        dy = err * (1.0 / d)
        dxh3 = dy * g_fin_v
        dx_out = r3 * (dxh3 - xh3 * jnp.mean(dxh3 * xh3, axis=-1, keepdims=True))
        dfb = (dx_out * gt_f).astype(BF16)
        df_ref[...] = dfb
        dh2 = jnp.zeros((tm, d), F32)
        for s in range(n_slab):
            da = _dot_nt(dfb, w2_ref[s * fs:(s + 1) * fs, :])
            dub = (da * (2.0 * ru_ref[:, s * fs:(s + 1) * fs])).astype(BF16)
            du_ref[:, s * fs:(s + 1) * fs] = dub
            dh2 = dh2 + _dot_nt(dub, w1_ref[s])
        dxh2 = dh2 * (1.0 + sc_f) * g_ffn_v
        dx_mid = dx_out + r2 * (dxh2 - xh2 * jnp.mean(dxh2 * xh2, axis=-1, keepdims=True))
        dxm_ref[...] = dx_mid
        dmob = (dx_mid * gt_a).astype(BF16)
        dmo_ref[...] = dmob
        dmix_ref[...] = _dot_nt(dmob, wo_ref[...]).astype(BF16)

        def rsum(v):
            return jnp.sum(v, axis=0, keepdims=True)

        stats = jnp.concatenate([
            rsum(dh2), rsum(dh2 * xh2 * g_ffn_v), rsum(dx_out * f), rsum(dx_mid * op),
            rsum(dh2 * (1.0 + sc_f) * xh2), rsum(dy * xh3), rsum(err * err), jnp.zeros((1, d), F32)], axis=0)

        @pl.when(i % tpe == 0)
        def _():
            st_ref[0] = stats

        @pl.when(i % tpe != 0)
        def _():
            st_ref[0] += stats

    def tile(width):
        return pl.BlockSpec((tm, width), lambda i: (i, 0))

    widths = (d, D_FF, D_FF, d, d, d, d, d)
    dtypes = (BF16, BF16, BF16, BF16, BF16, BF16, BF16, F32)
    const = pl.Buffered(1)
    return pl.pallas_call(
        body, name="post", grid=(nl,),
        in_specs=[
            tile(512), tile(512), tile(d), tile(d),
            pl.BlockSpec((1, 8, d), lambda i: (i // tpe, 0, 0)),
            _full((1, d)), _full((1, d)),
            pl.BlockSpec(w_out.shape, lambda i: (0, 0), pipeline_mode=const),
            pl.BlockSpec(w_ff1.shape, lambda i: (0, 0, 0), pipeline_mode=const),
            pl.BlockSpec(w_ff2.shape, lambda i: (0, 0), pipeline_mode=const),
        ],
        out_specs=[tile(w) for w in widths] + [pl.BlockSpec((1, 8, d), lambda i: (i // tpe, 0, 0))],
        out_shape=[jax.ShapeDtypeStruct((t_lat, w), dt) for w, dt in zip(widths, dtypes)]
        + [jax.ShapeDtypeStruct((nex, 8, d), F32)],
        scratch_shapes=[pltpu.VMEM((tm, D_FF), F32)],
        compiler_params=_params(("arbitrary",), VMEM_LIMIT),
    )(yret, ymla, x2, tgt2, modv, g_ffn, g_fin, w_out, w_ff1, w_ff2)


def _pre_bwd(x2, ctx2, modv, g_attn, pg, drq, drk, dkc_r, drv, dvc_r, drg, dq_m, dkl, dkc, dvl, dvc, dxm,
             w_in, g_q, g_kv, w_uq, w_ukv, cos_t, sin_t, *, seq, tm, rider=None):
    t_lat, d = x2.shape
    t_ctx = ctx2.shape[0]
    nl, nc = t_lat // tm, t_ctx // tm
    n_all = t_lat + t_ctx
    tpe = seq // tm
    nex = t_lat // seq

    def body(x_ref, c_ref, mod_ref, g_ref, pg_ref, drq_ref, drk_ref, dkcr_ref, drv_ref, dvcr_ref, drg_ref,
             dq_ref, dkl_ref, dkc_ref, dvl_ref, dvc_ref, dxm_ref, win_ref, gq_ref, gkv_ref, wuq_ref, wukv_ref,
             cos_ref, sin_ref, dpb_ref, dqf_ref, dkvf_ref, gx_ref, st_ref):
        i = pl.program_id(0)
        lat = i < nl
        latf = lat.astype(F32)
        cos = cos_ref[...]
        sin = sin_ref[...]
        cos1 = cos[:, 0:LANES]
        sin1 = sin[:, 0:LANES]
        d_rq = _rope_t(drq_ref[...] * latf, cos, sin)
        d_rk = _rope_t(jnp.where(lat, drk_ref[...], dkcr_ref[...]), cos, sin) * (RET_DK ** -0.5)
        d_rv = jnp.where(lat, drv_ref[...], dvcr_ref[...])
        d_rg = drg_ref[...] * latf
        dq_all = dq_ref[...] * (latf * MLA_SCALE)
        dk_all = jnp.where(lat, dkl_ref[...], dkc_ref[...])
        dv_all = jnp.where(lat, dvl_ref[...], dvc_ref[...])
        dnq = jnp.zeros((tm, Q_LORA), F32)
        dnkv = jnp.zeros((tm, KV_LORA), F32)
        dkpe = jnp.zeros((tm, LANES), F32)
        for hd in range(HEADS):
            o = hd * MLA_HEAD
            dqh = jnp.concatenate([dq_all[:, o:o + 128], _rope_t(dq_all[:, o + 128:o + 256], cos1, sin1)],
                                  axis=1).astype(BF16)
            dqf_ref[:, o:o + 256] = dqh
            dnq = dnq + _dot(dqh, wuq_ref[hd])
            dkpe = dkpe + dk_all[:, o + 128:o + 256]
            dkvh = jnp.concatenate([dk_all[:, o:o + 128], dv_all[:, hd * 128:(hd + 1) * 128]], axis=1).astype(BF16)
            dkvf_ref[:, o:o + 256] = dkvh
            dnkv = dnkv + _dot_nt(dkvh, wukv_ref[hd])
        d_kpe = _rope_t(dkpe, cos1, sin1)
        pgv = pg_ref[...]
        cq = pgv[:, 512:896]
        ckv = pgv[:, 896:1152]
        rq_ = lax.rsqrt(jnp.mean(cq * cq, axis=-1, keepdims=True) + EPS)
        cqh = cq * rq_
        dcqh = dnq * gq_ref[...]
        d_cq = rq_ * (dcqh - cqh * jnp.mean(dcqh * cqh, axis=-1, keepdims=True))
        rkv_ = lax.rsqrt(jnp.mean(ckv * ckv, axis=-1, keepdims=True) + EPS)
        ckvh = ckv * rkv_
        dckvh = dnkv * gkv_ref[...]
        d_ckv = rkv_ * (dckvh - ckvh * jnp.mean(dckvh * ckvh, axis=-1, keepdims=True))
        dpb = jnp.concatenate([d_rq, d_rk, d_rv, d_rg, d_cq, d_ckv, d_kpe], axis=1).astype(BF16)
        dpb_ref[...] = dpb
        dh = _dot(dpb, win_ref[...])
        xt = jnp.where(lat, x_ref[...], c_ref[...])
        sc = mod_ref[0, 1:2, :]
        g = g_ref[...]
        r = lax.rsqrt(jnp.mean(xt * xt, axis=-1, keepdims=True) + EPS)
        xh = xt * r
        dxh = dh * (1.0 + sc) * g
        dx = r * (dxh - xh * jnp.mean(dxh * xh, axis=-1, keepdims=True))

        @pl.when(lat)
        def _():
            gx_ref[...] = dxm_ref[...] + dx

        def rsum(v):
            return jnp.sum(v, axis=0, keepdims=True)

        def widen(v):
            return jnp.concatenate([v, jnp.zeros((1, d - v.shape[1]), F32)], axis=1)

        stats = jnp.concatenate([
            rsum(dh), rsum(dh * xh * g), rsum(dh * (1.0 + sc) * xh), widen(rsum(dnq * cqh)), widen(rsum(dnkv * ckvh)),
            jnp.zeros((3, d), F32)], axis=0)
        first = jnp.logical_or(jnp.logical_and(lat, i % tpe == 0), i == nl)

        @pl.when(first)
        def _():
            st_ref[0] = stats

        @pl.when(jnp.logical_not(first))
        def _():
            st_ref[0] += stats

    def lat_tile(width):
        return pl.BlockSpec((tm, width), lambda i: (jnp.minimum(i, nl - 1), 0))

    def ctx_tile(width):
        return pl.BlockSpec((tm, width), lambda i: (jnp.maximum(i - nl, 0), 0))

    def tile(width):
        return pl.BlockSpec((tm, width), lambda i: (i, 0))

    tab = pl.BlockSpec((tm, 256), lambda i: (jnp.where(i < nl, i % tpe, tpe), 0))
    ex = pl.BlockSpec((1, 8, d), lambda i: (jnp.minimum(i // tpe, nex), 0, 0))
    return _hosted_call(
        body, (x2, ctx2, modv, g_attn, pg, drq, drk, dkc_r, drv, dvc_r, drg, dq_m, dkl, dkc, dvl, dvc, dxm,
               w_in, g_q, g_kv, w_uq, w_ukv, cos_t, sin_t), name="pre_bwd", grid=(nl + nc,),
        in_specs=[
            lat_tile(d), ctx_tile(d), ex, _full((1, d)), tile(PG_COLS),
            lat_tile(256), lat_tile(256), ctx_tile(256), lat_tile(512), ctx_tile(512), lat_tile(512),
            lat_tile(1024), lat_tile(1024), ctx_tile(1024), lat_tile(512), ctx_tile(512), lat_tile(d),
            _full(w_in.shape), _full((1, Q_LORA)), _full((1, KV_LORA)), _full(w_uq.shape), _full(w_ukv.shape),
            tab, tab,
        ],
        out_specs=[tile(IN_PAD), tile(1024), tile(1024), lat_tile(d), ex],
        out_shape=[
            jax.ShapeDtypeStruct((n_all, IN_PAD), BF16), jax.ShapeDtypeStruct((n_all, 1024), BF16),
            jax.ShapeDtypeStruct((n_all, 1024), BF16), jax.ShapeDtypeStruct((t_lat, d), F32),
            jax.ShapeDtypeStruct((nex + 1, 8, d), F32),
        ],
        sem=("arbitrary",), rider=rider)


MLA_SCALE = 1.0 / math.sqrt(MLA_NOPE + MLA_ROPE)
KEY_BLOCK = 1024


def _mla_specs(t_lat, seq, ctx_len, tq, heads=1):
    nqt = seq // tq
    cb = t_lat // ctx_len
    q = pl.BlockSpec((tq, heads * MLA_HEAD), lambda b, h, j: (b * nqt + j, h))
    kl = pl.BlockSpec((seq, heads * MLA_HEAD), lambda b, h, j: (b, h))
    kc = pl.BlockSpec((ctx_len, heads * MLA_HEAD), lambda b, h, j: (cb + b, h))
    vl = pl.BlockSpec((seq, heads * 128), lambda b, h, j: (b, h))
    vc = pl.BlockSpec((ctx_len, heads * 128), lambda b, h, j: (cb + b, h))
    o = pl.BlockSpec((tq, heads * 128), lambda b, h, j: (b * nqt + j, h))
    return q, kl, kc, vl, vc, o


FWD_HEADS = 2
BWD_HEADS = 1


def _mla_fwd(q, k, v, *, t_lat, seq, ctx_len, tq, rider=None):
    nex = t_lat // seq

    def body(q_ref, kl_ref, kc_ref, vl_ref, vc_ref, o_ref, lse_ref):
        for hh in range(FWD_HEADS):
            wide = slice(hh * MLA_HEAD, (hh + 1) * MLA_HEAD)
            cols = slice(hh * 128, (hh + 1) * 128)
            qb = q_ref[:, wide]
            s = _dot_nt(qb, kl_ref[:, wide])
            sc = _dot_nt(qb, kc_ref[:, wide])
            m = jnp.maximum(jnp.max(s, axis=-1, keepdims=True), jnp.max(sc, axis=-1, keepdims=True))
            p = jnp.exp(s - m)
            pc = jnp.exp(sc - m)
            total = jnp.sum(p, axis=-1, keepdims=True) + jnp.sum(pc, axis=-1, keepdims=True)
            o = _dot(p.astype(BF16), vl_ref[:, cols]) + _dot(pc.astype(BF16), vc_ref[:, cols])
            o_ref[:, cols] = (o * (1.0 / total)).astype(BF16)
            lse_ref[:, cols] = jnp.broadcast_to(m + jnp.log(total), (tq, 128))

    qs, kl, kc, vl, vc, os_ = _mla_specs(t_lat, seq, ctx_len, tq, FWD_HEADS)
    return _hosted_call(
        body, (q, k, k, v, v), name="mla_fwd", grid=(nex, HEADS // FWD_HEADS, seq // tq),
        in_specs=[qs, kl, kc, vl, vc], out_specs=[os_, os_],
        out_shape=[jax.ShapeDtypeStruct((t_lat, HEADS * 128), BF16), jax.ShapeDtypeStruct((t_lat, HEADS * 128), F32)],
        sem=("parallel", "parallel", "arbitrary"), rider=rider)


def _mla_bwd(q, k, v, ymla, lse, dmix, *, t_lat, seq, ctx_len, tq, rider=None):
    nex = t_lat // seq
    nqt = seq // tq
    t_ctx = nex * ctx_len
    kb = min(KEY_BLOCK, seq)

    def body(q_ref, kl_ref, kc_ref, vl_ref, vc_ref, o_ref, lse_ref, do_ref, dq_ref, dkl_ref, dkc_ref, dvl_ref, dvc_ref):
        j = pl.program_id(2)

        @pl.when(j == 0)
        def _():
            dkl_ref[...] = jnp.zeros(dkl_ref.shape, F32)
            dkc_ref[...] = jnp.zeros(dkc_ref.shape, F32)
            dvl_ref[...] = jnp.zeros(dvl_ref.shape, F32)
            dvc_ref[...] = jnp.zeros(dvc_ref.shape, F32)

        for hh in range(BWD_HEADS):
            wide = slice(hh * MLA_HEAD, (hh + 1) * MLA_HEAD)
            cols = slice(hh * 128, (hh + 1) * 128)
            qb = q_ref[:, wide]
            dob = do_ref[:, cols]
            delta = jnp.sum(dob.astype(F32) * o_ref[:, cols].astype(F32), axis=-1, keepdims=True)
            lse_row = lse_ref[:, hh * 128:hh * 128 + 1]

            def block(k_ref, v_ref, dk_ref, dv_ref, rows):
                kbl = k_ref[rows, wide]
                vbl = v_ref[rows, cols]
                p = jnp.exp(_dot_nt(qb, kbl) - lse_row)
                ds = (p * (_dot_nt(dob, vbl) - delta)).astype(BF16)
                dk_ref[rows, wide] += _dot_tn(ds, qb)
                dv_ref[rows, cols] += _dot_tn(p.astype(BF16), dob)
                return _dot(ds, kbl)

            dq = block(kc_ref, vc_ref, dkc_ref, dvc_ref, pl.ds(0, ctx_len))
            for i in range(seq // kb):
                dq = dq + block(kl_ref, vl_ref, dkl_ref, dvl_ref, pl.ds(i * kb, kb))
            dq_ref[:, wide] = dq

    g = BWD_HEADS
    qs, kl, kc, vl, vc, os_ = _mla_specs(t_lat, seq, ctx_len, tq, g)
    do_spec = pl.BlockSpec((tq, g * 128), lambda b, h, j: (b * nqt + j, HEADS // g + h))
    return _hosted_call(
        body, (q, k, k, v, v, ymla, lse, dmix), name="mla_bwd", grid=(nex, HEADS // g, nqt),
        in_specs=[qs, kl, kc, vl, vc, os_, os_, do_spec],
        out_specs=[
            qs,
            pl.BlockSpec((seq, g * MLA_HEAD), lambda b, h, j: (b, h)),
            pl.BlockSpec((ctx_len, g * MLA_HEAD), lambda b, h, j: (b, h)),
            pl.BlockSpec((seq, g * 128), lambda b, h, j: (b, h)),
            pl.BlockSpec((ctx_len, g * 128), lambda b, h, j: (b, h)),
        ],
        out_shape=[
            jax.ShapeDtypeStruct((t_lat, HEADS * MLA_HEAD), F32),
            jax.ShapeDtypeStruct((t_lat, HEADS * MLA_HEAD), F32),
            jax.ShapeDtypeStruct((t_ctx, HEADS * MLA_HEAD), F32),
            jax.ShapeDtypeStruct((t_lat, HEADS * 128), F32),
            jax.ShapeDtypeStruct((t_ctx, HEADS * 128), F32),
        ],
        sem=("parallel", "parallel", "arbitrary"), rider=rider)


def _decay_terms(lg, chunk, forward):
    ii = lax.broadcasted_iota(jnp.int32, (chunk, chunk), 0)
    jj = lax.broadcasted_iota(jnp.int32, (chunk, chunk), 1)
    diff = (ii - jj) if forward else (jj - ii)
    dist = jnp.maximum(diff, 0).astype(F32)
    dmat = jnp.where(diff >= 0, jnp.exp(lg * dist), 0.0)
    pos = lax.broadcasted_iota(jnp.int32, (chunk, 1), 0).astype(F32)
    if forward:
        e_q = pos + 1.0
        e_k = (chunk - 1.0) - pos
    else:
        e_q = chunk - pos
        e_k = pos
    wq = jnp.exp(lg * e_q)
    wk = jnp.exp(lg * e_k)
    cd = jnp.exp(jnp.full((1, 1), lg * chunk, F32))
    return dmat, dist, wq, wk, e_q, e_k, cd


def _ctx_weights(lg, ctx_len, forward):
    pos = lax.broadcasted_iota(jnp.int32, (ctx_len, 1), 0).astype(F32)
    e = ((ctx_len - 1.0) - pos) if forward else pos
    return jnp.exp(lg * e), e


def _pair_specs(t_lat, seq, ctx_len):
    cb = t_lat // ctx_len
    qk = pl.BlockSpec((seq, 128), lambda b, p: (b, p))
    v = pl.BlockSpec((seq, 256), lambda b, p: (b, p))
    kc = pl.BlockSpec((ctx_len, 128), lambda b, p: (cb + b, p))
    vc = pl.BlockSpec((ctx_len, 256), lambda b, p: (cb + b, p))
    return qk, v, kc, vc


def _lane_masks():
    lane = lax.broadcasted_iota(jnp.int32, (1, 128), 1)
    return [(lane // RET_DK) == hh for hh in (0, 1)]


def _ret_fwd_pair(rq, rk, rv, pg, lg, g_ret, *, t_lat, seq, ctx_len, chunk, rider=None):
    nex = t_lat // seq
    n_chunk = seq // chunk

    def body(q_ref, k_ref, v_ref, kc_ref, vc_ref, rg_ref, lg_ref, g_ref, y_ref, o_ref):
        pair = pl.program_id(1)
        masks = _lane_masks()
        kcf = kc_ref[...].astype(F32)

        def run(forward):
            terms, s0 = [], []
            for hh in (0, 1):
                lgd = lg_ref[0 if forward else 1, 2 * pair + hh]
                terms.append(_decay_terms(lgd, chunk, forward))
                wc, _ = _ctx_weights(lgd, ctx_len, forward)
                s0.append(_dot_tn((jnp.where(masks[hh], kcf, 0.0) * wc).astype(BF16), vc_ref[:, hh * 128:(hh + 1) * 128]))

            def step(t, states):
                n = t if forward else n_chunk - 1 - t
                sl = pl.ds(pl.multiple_of(n * chunk, chunk), chunk)
                qb = q_ref[sl, :]
                kf_all = k_ref[sl, :].astype(F32)
                new = []
                for hh in (0, 1):
                    dmat, _, wq, wk, _, _, cd = terms[hh]
                    cols = slice(hh * 128, (hh + 1) * 128)
                    qm = jnp.where(masks[hh], qb, jnp.zeros((), BF16))
                    kf = jnp.where(masks[hh], kf_all, 0.0)
                    vb = v_ref[sl, cols]
                    a = _dot_nt(qm, kf.astype(BF16)) * dmat
                    o = _dot(a.astype(BF16), vb) + wq * _dot(qm, states[hh].astype(BF16))
                    if forward:
                        o_ref[sl, cols] = o
                    else:
                        o = o_ref[sl, cols] + o
                        o_ref[sl, cols] = o
                        mu = jnp.mean(o, axis=-1, keepdims=True)
                        oc = o - mu
                        var = jnp.mean(oc * oc, axis=-1, keepdims=True)
                        rg = rg_ref[sl, cols]
                        y_ref[sl, cols] = (oc * lax.rsqrt(var + EPS) * g_ref[:, cols] * (rg / (1.0 + jnp.exp(-rg)))).astype(BF16)
                    new.append(cd * states[hh] + _dot_tn((kf * wk).astype(BF16), vb))
                return tuple(new)

            lax.fori_loop(0, n_chunk, step, tuple(s0))

        run(True)
        run(False)

    qk, v, kc, vc = _pair_specs(t_lat, seq, ctx_len)
    return _hosted_call(
        body, (rq, rk, rv, rk, rv, pg, lg, g_ret), name="ret_fwd", grid=(nex, HEADS // 2),
        in_specs=[qk, qk, v, kc, vc, v, pl.BlockSpec(memory_space=pltpu.SMEM), pl.BlockSpec((1, 256), lambda b, p: (0, p))],
        out_specs=[v, v],
        out_shape=[jax.ShapeDtypeStruct((t_lat, HEADS * RET_DV), BF16), jax.ShapeDtypeStruct((t_lat, HEADS * RET_DV), F32)],
        sem=("parallel", "arbitrary"), rider=rider)


def _ret_bwd_pair(rq, rk, rv, pg, osum, dmix, lg, g_ret, *, t_lat, seq, ctx_len, chunk, rider=None):
    nex = t_lat // seq
    n_chunk = seq // chunk
    t_ctx = nex * ctx_len

    def body(q_ref, k_ref, v_ref, kc_ref, vc_ref, rg_ref, o_ref, dy_ref, lg_ref, g_ref,
             dq_ref, dk_ref, dv_ref, dkc_ref, dvc_ref, drg_ref, st_ref, do_s, s_st):
        pair = pl.program_id(1)
        masks = _lane_masks()
        kcf = kc_ref[...].astype(F32)

        def norm_step(n, dgains):
            sl = pl.ds(pl.multiple_of(n * chunk, chunk), chunk)
            out = []
            for hh in (0, 1):
                cols = slice(hh * 128, (hh + 1) * 128)
                gain = g_ref[:, cols]
                o = o_ref[sl, cols]
                mu = jnp.mean(o, axis=-1, keepdims=True)
                oc = o - mu
                rstd = lax.rsqrt(jnp.mean(oc * oc, axis=-1, keepdims=True) + EPS)
                ohat = oc * rstd
                rg = rg_ref[sl, cols]
                sg = 1.0 / (1.0 + jnp.exp(-rg))
                dy = dy_ref[sl, cols].astype(F32)
                don = dy * (rg * sg)
                drg_ref[sl, cols] = dy * (ohat * gain) * (sg * (1.0 + rg * (1.0 - sg)))
                dohat = don * gain
                do_s[sl, cols] = rstd * (dohat - jnp.mean(dohat, axis=-1, keepdims=True)
                                         - ohat * jnp.mean(dohat * ohat, axis=-1, keepdims=True))
                out.append(dgains[hh] + jnp.sum(don * ohat, axis=0, keepdims=True))
            return tuple(out)

        zero_row = jnp.zeros((1, 128), F32)
        dgains = lax.fori_loop(0, n_chunk, norm_step, (zero_row, zero_row))
        dq_ref[...] = jnp.zeros(dq_ref.shape, F32)
        dk_ref[...] = jnp.zeros(dk_ref.shape, F32)
        dv_ref[...] = jnp.zeros(dv_ref.shape, F32)

        chains = [(forward, hh) for forward in (True, False) for hh in (0, 1)]
        terms, ctxw, s0 = [], [], []
        for forward, hh in chains:
            lgd = lg_ref[0 if forward else 1, 2 * pair + hh]
            terms.append(_decay_terms(lgd, chunk, forward))
            ctxw.append(_ctx_weights(lgd, ctx_len, forward))
            s0.append(_dot_tn((jnp.where(masks[hh], kcf, 0.0) * ctxw[-1][0]).astype(BF16), vc_ref[:, hh * 128:(hh + 1) * 128]))

        def chunk_at(t, ascending):
            n = t if ascending else n_chunk - 1 - t
            return n, pl.ds(pl.multiple_of(n * chunk, chunk), chunk)

        def state_step(t, states):
            new = []
            for c, (forward, hh) in enumerate(chains):
                n, sl = chunk_at(t, forward)
                wk, cd = terms[c][3], terms[c][6]
                s_st[c, n] = states[c]
                kf = jnp.where(masks[hh], k_ref[sl, :].astype(F32), 0.0)
                new.append(cd * states[c] + _dot_tn((kf * wk).astype(BF16), v_ref[sl, hh * 128:(hh + 1) * 128]))
            return tuple(new)

        lax.fori_loop(0, n_chunk, state_step, tuple(s0))

        def grad_step(t, carry):
            out = []
            for forward in (True, False):
                n, sl = chunk_at(t, not forward)
                qb = q_ref[sl, :]
                kf_all = k_ref[sl, :].astype(F32)
                dq_sum = jnp.zeros((chunk, 128), F32)
                dk_sum = jnp.zeros((chunk, 128), F32)
                for hh in (0, 1):
                    c = chains.index((forward, hh))
                    g_next, dlg = carry[c]
                    dmat, dist, wq, wk, e_q, e_k, cd = terms[c]
                    cols = slice(hh * 128, (hh + 1) * 128)
                    qm = jnp.where(masks[hh], qb, jnp.zeros((), BF16))
                    kf = jnp.where(masks[hh], kf_all, 0.0)
                    kb = kf.astype(BF16)
                    vb = v_ref[sl, cols]
                    do = do_s[sl, cols]
                    dob = do.astype(BF16)
                    s_n = s_st[c, n]
                    s_nb = s_n.astype(BF16)
                    gb = g_next.astype(BF16)
                    dk_cross = wk * _dot_nt(vb, gb)
                    dv_cross = _dot((kf * wk).astype(BF16), gb)
                    a = _dot_nt(qm, kb) * dmat
                    da_raw = _dot_nt(dob, vb)
                    dab = (da_raw * dmat).astype(BF16)
                    o_cross = wq * _dot(qm, s_nb)
                    dq_sum = dq_sum + _dot(dab, kb) + wq * _dot_nt(dob, s_nb)
                    dk_sum = dk_sum + _dot_tn(dab, qm) + dk_cross
                    dv_ref[sl, cols] += _dot_tn(a.astype(BF16), dob) + dv_cross
                    dlg = (dlg + chunk * cd * jnp.sum(g_next * s_n, keepdims=True)
                           + jnp.sum(e_k * jnp.sum(kf * dk_cross, axis=-1, keepdims=True), keepdims=True)
                           + jnp.sum(dist * a * da_raw, keepdims=True)
                           + jnp.sum(e_q * jnp.sum(o_cross * do, axis=-1, keepdims=True), keepdims=True))
                    out.append((cd * g_next + _dot_tn((qm.astype(F32) * wq).astype(BF16), dob), dlg))
                dq_ref[sl, :] += dq_sum
                dk_ref[sl, :] += dk_sum
            return tuple(out)

        zero = (jnp.zeros((128, 128), F32), jnp.zeros((1, 1), F32))
        res = lax.fori_loop(0, n_chunk, grad_step, (zero,) * len(chains))
        dkc_sum = jnp.zeros((ctx_len, 128), F32)
        dvc = [jnp.zeros((ctx_len, 128), F32)] * 2
        dlgs = []
        for c, (forward, hh) in enumerate(chains):
            ds0, dlg = res[c]
            wc, e_c = ctxw[c]
            kcm = jnp.where(masks[hh], kcf, 0.0)
            ds0b = ds0.astype(BF16)
            dkc_part = wc * _dot_nt(vc_ref[:, hh * 128:(hh + 1) * 128], ds0b)
            dkc_sum = dkc_sum + dkc_part
            dvc[hh] = dvc[hh] + _dot((kcm * wc).astype(BF16), ds0b)
            dlgs.append(dlg + jnp.sum(e_c * jnp.sum(kcm * dkc_part, axis=-1, keepdims=True), keepdims=True))
        dkc_ref[...] = dkc_sum
        for hh in (0, 1):
            cols = slice(hh * 128, (hh + 1) * 128)
            dvc_ref[:, cols] = dvc[hh]
            st_ref[0, :, cols] = jnp.concatenate([
                dgains[hh], jnp.broadcast_to(dlgs[hh], (1, 128)), jnp.broadcast_to(dlgs[2 + hh], (1, 128)),
                jnp.zeros((5, 128), F32)], axis=0)

    qk, v, kc, vc = _pair_specs(t_lat, seq, ctx_len)
    return _hosted_call(
        body, (rq, rk, rv, rk, rv, pg, osum, dmix, lg, g_ret), name="ret_bwd", grid=(nex, HEADS // 2),
        in_specs=[qk, qk, v, kc, vc, v, v, v, pl.BlockSpec(memory_space=pltpu.SMEM),
                  pl.BlockSpec((1, 256), lambda b, p: (0, p))],
        out_specs=[
            qk, qk, v,
            pl.BlockSpec((ctx_len, 128), lambda b, p: (b, p)),
            pl.BlockSpec((ctx_len, 256), lambda b, p: (b, p)),
            v,
            pl.BlockSpec((1, 8, 256), lambda b, p: (b, 0, p)),
        ],
        out_shape=[
            jax.ShapeDtypeStruct((t_lat, 256), F32), jax.ShapeDtypeStruct((t_lat, 256), F32),
            jax.ShapeDtypeStruct((t_lat, 512), F32), jax.ShapeDtypeStruct((t_ctx, 256), F32),
            jax.ShapeDtypeStruct((t_ctx, 512), F32), jax.ShapeDtypeStruct((t_lat, 512), F32),
            jax.ShapeDtypeStruct((nex, 8, 512), F32),
        ],
        scratch_shapes=[pltpu.VMEM((seq, 256), F32), pltpu.VMEM((4, n_chunk, 128, 128), F32)],
        sem=("parallel", "arbitrary"), rider=rider)


def _matmul_tn(a, b, *, bm, bn, bk, chip_major, name, out_dtype=F32, rider=None):
    tk, m = a.shape
    n = b.shape[1]
    slab = n // N_CHIPS
    per_block = bn // slab if chip_major else 1
    bk = max(c for c in range(LANES, min(bk, tk) + 1, LANES) if tk % c == 0)
    nk = tk // bk
    blk = (per_block, bm, slab) if chip_major else (bm, bn)

    def body(a_ref, b_ref, o_ref, acc_ref):
        k = pl.program_id(2)
        if chip_major:
            parts = [_dot_tn(a_ref[...], b_ref[:, s * slab:(s + 1) * slab]) for s in range(per_block)]
        else:
            parts = [_dot_tn(a_ref[...], b_ref[...])]

        @pl.when(k == 0)
        def _():
            for s, part in enumerate(parts):
                if chip_major:
                    acc_ref[s] = part
                else:
                    acc_ref[...] = part

        @pl.when(k > 0)
        def _():
            for s, part in enumerate(parts):
                if chip_major:
                    acc_ref[s] += part
                else:
                    acc_ref[...] += part

        @pl.when(k == nk - 1)
        def _():
            o_ref[...] = acc_ref[...].astype(out_dtype)

    if chip_major:
        out_spec = pl.BlockSpec(blk, lambda i, j, k: (j, i, 0))
        out_shape = jax.ShapeDtypeStruct((N_CHIPS, m, slab), out_dtype)
    else:
        out_spec = pl.BlockSpec(blk, lambda i, j, k: (i, j))
        out_shape = jax.ShapeDtypeStruct((m, n), out_dtype)
    (out,), carried = _hosted_call(
        body, (a, b), name=name, grid=(m // bm, n // bn, nk),
        in_specs=[pl.BlockSpec((bk, bm), lambda i, j, k: (k, i)), pl.BlockSpec((bk, bn), lambda i, j, k: (k, j))],
        out_specs=[out_spec], out_shape=[out_shape], scratch_shapes=[pltpu.VMEM(blk, F32)],
        sem=("parallel", "parallel", "arbitrary"), rider=rider)
    return out if rider is None else (out, carried)


_LATE = ("w_out", "w_ff1", "w_ff2")
_EARLY = ("w_in", "w_uq", "w_ukv")


def _local_step(x, ctx, tgt, modv, lg, g_attn, g_ffn, g_fin, g_ret, g_q, g_kv, w_in, w_uq, w_ukv, late, place=None,
                *, tm=256, tq=256, chunk=256):
    nex, seq, d = x.shape
    ctx_len = ctx.shape[1]
    t_lat = nex * seq
    tm = min(tm, seq)
    x2 = x.reshape(t_lat, d)
    ctx2 = ctx.reshape(nex * ctx_len, d)
    tgt2 = tgt.reshape(t_lat, d)
    tm_fwd = min(2 * tm, seq)
    cos_t, sin_t = _rope_tables(seq, tm)
    dims = dict(t_lat=t_lat, seq=seq, ctx_len=ctx_len)
    alone = place is None

    (hb, pg, rq, rk, rv, nq, nkv, q, k, v), crossed = _pre_fwd(
        x2, ctx2, modv, g_attn, w_in, g_q, g_kv, w_uq, w_ukv, *_rope_tables(seq, tm_fwd), seq=seq, tm=tm_fwd,
        rider=None if alone else _gather_ici_rider([late[2]]))
    (yret, osum), got_ff2 = _ret_fwd_pair(rq, rk, rv, pg, lg, g_ret, chunk=chunk, **dims,
                                          rider=None if alone else _gather_d2d_rider(crossed))
    (ymla, lse), got_rest = _mla_fwd(q, k, v, tq=tq, **dims,
                                     rider=None if alone else _gather_rider([late[0], late[1]], staged=True))
    w_out, w_ff1, w_ff2 = late if alone else got_rest + got_ff2
    mix, act, du, h2, df, dmo, dmix, dxm, st_post = _post(yret, ymla, x2, tgt2, modv, g_ffn, g_fin, w_out.reshape(d, d),
                                                         w_ff1, w_ff2.reshape(D_FF, d), seq=seq, tm=min(tm, 256))
    kw = dict(bm=1024, bn=1024, bk=1024, out_dtype=BF16)
    g_ff2 = _matmul_tn(act, df, chip_major=False, name="gw_ff2", **kw).reshape(N_CHIPS, D_FF // N_CHIPS, d)
    if alone:
        g_ff1 = _matmul_tn(h2, du, chip_major=True, name="gw_ff1", **kw)
        g_out = _matmul_tn(mix, dmo, chip_major=False, name="gw_out", **kw).reshape(N_CHIPS, d // N_CHIPS, d)
        (dq_m, dkl, dkc, dvl, dvc), _ = _mla_bwd(q, k, v, ymla, lse, dmix, tq=tq, **dims)
        (drq, drk, drv, dkc_r, dvc_r, drg, st_ret), _ = _ret_bwd_pair(rq, rk, rv, pg, osum, dmix, lg, g_ret, chunk=chunk,
                                                                      **dims)
        late_out = [g_out, g_ff1, g_ff2]
    else:
        core, slot = place
        g_ff1, x_ff2 = _matmul_tn(h2, du, chip_major=True, name="gw_ff1", rider=_exchange_rider([g_ff2]), **kw)
        g_out, x_ff1 = _matmul_tn(mix, dmo, chip_major=False, name="gw_out", rider=_exchange_rider([g_ff1]), **kw)
        g_out = g_out.reshape(N_CHIPS, d // N_CHIPS, d)
        p_ff2 = _add_half(g_ff2, x_ff2[0], core, "add_half_w_ff2")
        p_ff1 = _add_half(g_ff1, x_ff1[0], core, "add_half_w_ff1")
        (dq_m, dkl, dkc, dvl, dvc), (l_ff2, l_ff1, x_out) = _mla_bwd(
            q, k, v, ymla, lse, dmix, tq=min(seq, 512), **dims,
            rider=_merge_riders(_scatter_rider([p_ff2, p_ff1]), _exchange_rider([g_out])))
        p_out = _add_half(g_out, x_out, core, "add_half_w_out")
        m_ff2 = _sum_chips(p_ff2, l_ff2, slot, "sum_chips_w_ff2")
        m_ff1 = _sum_chips(p_ff1, l_ff1, slot, "sum_chips_w_ff1")
        (drq, drk, drv, dkc_r, dvc_r, drg, st_ret), (l_out,) = _ret_bwd_pair(
            rq, rk, rv, pg, osum, dmix, lg, g_ret, chunk=chunk, **dims, rider=_scatter_rider([p_out]))
        late_out = [_sum_chips(p_out, l_out, slot, "sum_chips_w_out"), m_ff1, m_ff2]
    (dpb, dqf, dkvf, gx, st_pre), _ = _pre_bwd(
        x2, ctx2, modv, g_attn, pg, drq, drk, dkc_r, drv, dvc_r, drg, dq_m, dkl, dkc, dvl, dvc, dxm, w_in, g_q, g_kv,
        w_uq, w_ukv, cos_t, sin_t, seq=seq, tm=tm)
    g_early = [
        _matmul_tn(dpb, hb, bm=IN_PAD // 2, bn=d, bk=512, chip_major=False, name="gw_in"),
        _matmul_tn(dqf, nq, bm=HEADS * MLA_HEAD, bn=Q_LORA, bk=1536, chip_major=False, name="gw_uq"),
        _matmul_tn(nkv, dkvf, bm=KV_LORA, bn=HEADS * 256, bk=1536, chip_major=True, name="gw_ukv"),
    ]
    return gx.reshape(nex, seq, d), g_early, late_out, st_post, st_ret, st_pre


_ANY = pl.BlockSpec(memory_space=pl.ANY)
_VMEM = pl.BlockSpec(memory_space=pltpu.VMEM)
_OFFSETS = tuple((dx, dy, dc) for dx in (0, 1) for dy in (0, 1) for dc in (0, 1))[1:]
_CHIP_OFFSETS = ((1, 0), (0, 1), (1, 1))


def _place():
    return lax.axis_index("x"), lax.axis_index("y"), lax.axis_index("c")


def _flip(v, d):
    return 1 - v if d else v


def _gather8_rider(a, in_vmem=True):
    def copies(a_ref, o_ref, send, recv):
        x, y, z = _place()
        me = 4 * x + 2 * y + z
        out = []
        for k, (dx, dy, dc) in enumerate(_OFFSETS):
            peer = (_flip(x, dx), _flip(y, dy), _flip(z, dc))
            landing = o_ref.at[4 * peer[0] + 2 * peer[1] + peer[2]]
            out.append((
                pltpu.make_async_remote_copy(src_ref=a_ref, dst_ref=o_ref.at[me], send_sem=send.at[k],
                                             recv_sem=recv.at[k], device_id=peer, device_id_type=MESH),
                pltpu.make_async_remote_copy(src_ref=a_ref, dst_ref=landing, send_sem=send.at[k],
                                             recv_sem=recv.at[k], device_id=peer, device_id_type=MESH)))
        return me, out

    def start(ins, outs, sems):
        me, cps = copies(ins[0], outs[0], sems[0], sems[1])
        pltpu.make_async_copy(ins[0], outs[0].at[me], sems[2]).start()
        for out_cp, _ in cps:
            out_cp.start()

    def finish(ins, outs, sems):
        me, cps = copies(ins[0], outs[0], sems[0], sems[1])
        for out_cp, in_cp in cps:
            in_cp.wait_recv()
            out_cp.wait_send()
        pltpu.make_async_copy(ins[0], outs[0].at[me], sems[2]).wait()

    spec = [_VMEM] if in_vmem else [_ANY]
    return _Rider([a], [jax.ShapeDtypeStruct((N_DEV,) + a.shape, a.dtype)],
                  [pltpu.SemaphoreType.DMA((7,)), pltpu.SemaphoreType.DMA((7,)), pltpu.SemaphoreType.DMA],
                  start, finish, in_specs=spec, out_specs=spec)


def _merge_riders(*riders):
    ins, outs, sems, in_specs, out_specs, aliases, cuts = [], [], [], [], [], {}, []
    for r in riders:
        cuts.append((len(ins), len(outs), len(sems)))
        aliases.update({len(ins) + i: len(outs) + j for i, j in r.aliases.items()})
        ins += r.ins
        outs += r.out_shapes
        sems += r.sems
        in_specs += r.in_specs
        out_specs += r.out_specs

    def part(r, cut, r_ins, r_outs, r_sems):
        return (r_ins[cut[0]:cut[0] + len(r.ins)], r_outs[cut[1]:cut[1] + len(r.out_shapes)],
                r_sems[cut[2]:cut[2] + len(r.sems)])

    def start(r_ins, r_outs, r_sems):
        for r, cut in zip(riders, cuts):
            r.start(*part(r, cut, r_ins, r_outs, r_sems))

    def finish(r_ins, r_outs, r_sems):
        for r, cut in zip(riders, cuts):
            r.finish(*part(r, cut, r_ins, r_outs, r_sems))

    def middle(r_ins, r_outs, r_sems):
        for r, cut in zip(riders, cuts):
            if r.middle is not None:
                r.middle(*part(r, cut, r_ins, r_outs, r_sems))

    return _Rider(ins, outs, sems, start, finish, aliases=aliases, in_specs=in_specs, out_specs=out_specs,
                  middle=middle if any(r.middle is not None for r in riders) else None)


def _allgather8(a, name):
    return _run_rider(_gather8_rider(a), name)[0]


BF16_TILE_ROWS = 16


def _half(o, slot, which):
    r2 = o.shape[1] // 2
    if r2 % BF16_TILE_ROWS == 0:
        return o.at[slot, pl.ds(which * r2, r2)]
    c2 = o.shape[2] // 2
    assert c2 % LANES == 0
    return o.at[slot, :, pl.ds(which * c2, c2)]


def _gather_send(o_refs, send, recv):
    x, y, z = _place()
    chip = 2 * x + y
    for a, o in enumerate(o_refs):
        r2 = o.shape[1] // 2
        mine = _half(o, chip, z)
        for k, (dx, dy) in enumerate(_CHIP_OFFSETS):
            pltpu.make_async_remote_copy(
                src_ref=mine, dst_ref=mine, send_sem=send.at[a, k], recv_sem=recv.at[a, k],
                device_id=(_flip(x, dx), _flip(y, dy), z), device_id_type=MESH).start()


def _gather_landed(o_refs, send, recv, then=None):
    x, y, z = _place()
    chip = 2 * x + y
    for a, o in enumerate(o_refs):
        for k, (dx, dy) in enumerate(_CHIP_OFFSETS):
            landed = _half(o, 2 * _flip(x, dx) + _flip(y, dy), z)
            pltpu.make_async_remote_copy(
                src_ref=landed, dst_ref=landed, send_sem=send.at[a, k], recv_sem=recv.at[a, k],
                device_id=(_flip(x, dx), _flip(y, dy), z), device_id_type=MESH).wait_recv()
            if then is not None:
                then(a, k, landed)
    for a, o in enumerate(o_refs):
        mine = _half(o, chip, z)
        for k, (dx, dy) in enumerate(_CHIP_OFFSETS):
            pltpu.make_async_remote_copy(
                src_ref=mine, dst_ref=mine, send_sem=send.at[a, k], recv_sem=recv.at[a, k],
                device_id=(_flip(x, dx), _flip(y, dy), z), device_id_type=MESH).wait_send()


def _pass_on(o_refs, fsend, frecv, a, k, landed):
    x, y, z = _place()
    pltpu.make_async_remote_copy(
        src_ref=landed, dst_ref=landed, send_sem=fsend.at[a, k], recv_sem=frecv.at[a, k],
        device_id=(x, y, 1 - z), device_id_type=MESH).start()


def _passed_on(o_refs, fsend, frecv):
    x, y, z = _place()
    for a, o in enumerate(o_refs):
        for k, (dx, dy) in enumerate(_CHIP_OFFSETS):
            other = 2 * _flip(x, dx) + _flip(y, dy)
            got = _half(o, other, 1 - z)
            gave = _half(o, other, z)
            pltpu.make_async_remote_copy(
                src_ref=got, dst_ref=got, send_sem=fsend.at[a, k], recv_sem=frecv.at[a, k],
                device_id=(x, y, 1 - z), device_id_type=MESH).wait_recv()
            pltpu.make_async_remote_copy(
                src_ref=gave, dst_ref=gave, send_sem=fsend.at[a, k], recv_sem=frecv.at[a, k],
                device_id=(x, y, 1 - z), device_id_type=MESH).wait_send()


def _gather_finish(o_refs, send, recv, fsend, frecv):
    _gather_landed(o_refs, send, recv, functools.partial(_pass_on, o_refs, fsend, frecv))
    _passed_on(o_refs, fsend, frecv)


class _Rider:
    def __init__(self, ins, out_shapes, sems, start, finish, aliases=None, in_specs=None, out_specs=None, middle=None):
        self.ins, self.out_shapes, self.sems = list(ins), list(out_shapes), list(sems)
        self.start, self.finish, self.aliases = start, finish, dict(aliases or {})
        self.middle = middle
        self.in_specs = list(in_specs) if in_specs else [_ANY] * len(self.ins)
        self.out_specs = list(out_specs) if out_specs else [_ANY] * len(self.out_shapes)


def _run_rider(rider, name):
    r_in, r_out = len(rider.ins), len(rider.out_shapes)

    def body(*refs):
        ins, outs, sems = refs[:r_in], refs[r_in:r_in + r_out], refs[r_in + r_out:]
        rider.start(ins, outs, sems)
        if rider.middle is not None:
            rider.middle(ins, outs, sems)
        rider.finish(ins, outs, sems)

    return pl.pallas_call(
        body, name=name, in_specs=rider.in_specs, out_specs=rider.out_specs, out_shape=rider.out_shapes,
        input_output_aliases=rider.aliases, scratch_shapes=rider.sems,
    )(*rider.ins)


def _hosted_call(body, args, *, name, grid, in_specs, out_specs, out_shape, scratch_shapes=(), sem, rider=None,
                 prefetch=()):
    scratch_shapes = list(scratch_shapes)
    n_pf, n_in, n_out, n_sc = len(prefetch), len(in_specs), len(out_specs), len(scratch_shapes)
    r_in, r_out = (len(rider.ins), len(rider.out_shapes)) if rider else (0, 0)
    last = tuple(g - 1 for g in grid)

    def hosted(*refs):
        p = 0
        parts = []
        for cnt in (n_pf, n_in, r_in, n_out, r_out, n_sc):
            parts.append(refs[p:p + cnt])
            p += cnt
        pf, ins, r_ins, outs, r_outs, scratch = parts
        sems = refs[p:]
        ids = [pl.program_id(a) for a in range(len(grid))]
        is_first = functools.reduce(jnp.logical_and, [i == 0 for i in ids])
        is_last = functools.reduce(jnp.logical_and, [i == e for i, e in zip(ids, last)])

        @pl.when(is_first)
        def _():
            rider.start(r_ins, r_outs, sems)

        if rider.middle is not None:
            linear = functools.reduce(lambda acc, ig: acc * ig[1] + ig[0], zip(ids, grid), 0)

            @pl.when(linear == math.prod(grid) * 3 // 4)
            def _():
                rider.middle(r_ins, r_outs, sems)

        body(*pf, *ins, *outs, *scratch)

        @pl.when(is_last)
        def _():
            rider.finish(r_ins, r_outs, sems)

    if rider is None:
        kern, all_in, all_out, shapes, scratch, aliases, extra = body, list(in_specs), list(out_specs), list(out_shape), \
            scratch_shapes, {}, []
    else:
        kern, all_in, all_out = hosted, list(in_specs) + rider.in_specs, list(out_specs) + rider.out_specs
        shapes, scratch, extra = list(out_shape) + rider.out_shapes, scratch_shapes + rider.sems, rider.ins
        aliases = {n_pf + n_in + i: n_out + j for i, j in rider.aliases.items()}
        sem = ("arbitrary",) * len(grid)
    if prefetch:
        spec = dict(grid_spec=pltpu.PrefetchScalarGridSpec(
            num_scalar_prefetch=n_pf, grid=grid, in_specs=all_in, out_specs=all_out, scratch_shapes=scratch))
    else:
        spec = dict(grid=grid, in_specs=all_in, out_specs=all_out, scratch_shapes=scratch)
    res = pl.pallas_call(kern, name=name, out_shape=shapes, input_output_aliases=aliases,
                         compiler_params=_params(sem, VMEM_LIMIT), **spec)(*prefetch, *args, *extra)
    return list(res[:n_out]), list(res[n_out:])


def _gather_rider(ws, staged=False):
    n = len(ws)
    shapes = [jax.ShapeDtypeStruct(w.shape, w.dtype) for w in ws]
    sems = [pltpu.SemaphoreType.DMA((n, 3))] * 4
    aliases = {a: a for a in range(n)}

    def start(ins, outs, s):
        _gather_send(outs, s[0], s[1])

    if not staged:
        return _Rider(ws, shapes, sems, start, lambda ins, outs, s: _gather_finish(outs, *s), aliases=aliases)
    return _Rider(
        ws, shapes, sems, start, lambda ins, outs, s: _passed_on(outs, s[2], s[3]), aliases=aliases,
        middle=lambda ins, outs, s: _gather_landed(outs, s[0], s[1], functools.partial(_pass_on, outs, s[2], s[3])))


def _gather_ici_rider(ws):
    n = len(ws)
    return _Rider(
        ws, [jax.ShapeDtypeStruct(w.shape, w.dtype) for w in ws], [pltpu.SemaphoreType.DMA((n, 3))] * 2,
        lambda ins, outs, sems: _gather_send(outs, sems[0], sems[1]),
        lambda ins, outs, sems: _gather_landed(outs, sems[0], sems[1]),
        aliases={a: a for a in range(n)})


def _gather_d2d_rider(ws):
    n = len(ws)

    def start(ins, outs, sems):
        x, y, z = _place()
        for a, o in enumerate(outs):
            for k, (dx, dy) in enumerate(_CHIP_OFFSETS):
                _pass_on(outs, sems[0], sems[1], a, k, _half(o, 2 * _flip(x, dx) + _flip(y, dy), z))

    return _Rider(
        ws, [jax.ShapeDtypeStruct(w.shape, w.dtype) for w in ws], [pltpu.SemaphoreType.DMA((n, 3))] * 2,
        start, lambda ins, outs, sems: _passed_on(outs, sems[0], sems[1]), aliases={a: a for a in range(n)})


def _copies_rider(ins, out_shapes, sem_shape, make):
    def start(r_ins, r_outs, sems):
        for cp in make(r_ins, r_outs, sems[0], sems[1]):
            cp.start()

    def finish(r_ins, r_outs, sems):
        for cp in make(r_ins, r_outs, sems[0], sems[1]):
            cp.wait()

    return _Rider(ins, out_shapes, [pltpu.SemaphoreType.DMA(sem_shape)] * 2, start, finish)


def _exchange_rider(gs):
    def make(g_refs, r_refs, send, recv):
        x, y, z = _place()
        return [pltpu.make_async_remote_copy(
            src_ref=g.at[:, pl.ds((1 - z) * (g.shape[1] // 2), g.shape[1] // 2)], dst_ref=r, send_sem=send.at[a],
            recv_sem=recv.at[a], device_id=(x, y, 1 - z), device_id_type=MESH)
            for a, (g, r) in enumerate(zip(g_refs, r_refs))]

    shapes = [jax.ShapeDtypeStruct((g.shape[0], g.shape[1] // 2, g.shape[2]), g.dtype) for g in gs]
    return _copies_rider(gs, shapes, (len(gs),), make)


def _add_half(g, recv, core, name):
    s, r, c = g.shape
    r2 = r // 2
    rb = r2
    for cand in (256, 128, 64):
        if r2 % cand == 0:
            rb = cand
            break
    g4 = g.reshape(s, 2, r2, c)

    def body(core_ref, g_ref, r_ref, o_ref):
        o_ref[...] = (g_ref[...].astype(F32) + r_ref[...].astype(F32)).astype(BF16)

    return pl.pallas_call(
        body, name=name,
        grid_spec=pltpu.PrefetchScalarGridSpec(
            num_scalar_prefetch=1, grid=(s, r2 // rb),
            in_specs=[pl.BlockSpec((None, None, rb, c), lambda i, j, cr: (i, cr[0], j, 0)),
                      pl.BlockSpec((None, rb, c), lambda i, j, cr: (i, j, 0))],
            out_specs=pl.BlockSpec((None, rb, c), lambda i, j, cr: (i, j, 0))),
        out_shape=jax.ShapeDtypeStruct((s, r2, c), BF16),
        compiler_params=_params(("parallel", "parallel")),
    )(core, g4, recv)


def _scatter_rider(ps):
    def make(p_refs, o_refs, send, recv):
        x, y, z = _place()
        copies = []
        for a, (p, o) in enumerate(zip(p_refs, o_refs)):
            for k, (dx, dy) in enumerate(_CHIP_OFFSETS):
                other = 2 * _flip(x, dx) + _flip(y, dy)
                copies.append(pltpu.make_async_remote_copy(
                    src_ref=p.at[other], dst_ref=o.at[k], send_sem=send.at[a, k], recv_sem=recv.at[a, k],
                    device_id=(_flip(x, dx), _flip(y, dy), z), device_id_type=MESH))
        return copies

    shapes = [jax.ShapeDtypeStruct((3,) + p.shape[1:], p.dtype) for p in ps]
    return _copies_rider(ps, shapes, (len(ps), 3), make)


def _sum_chips(p, landed, chip, name):
    _, r2, c = p.shape
    rb = r2
    for cand in (256, 128, 64):
        if r2 % cand == 0:
            rb = cand
            break

    def body(s_ref, p_ref, l_ref, o_ref):
        acc = p_ref[...].astype(F32)
        for k in range(3):
            acc = acc + l_ref[k].astype(F32)
        o_ref[...] = acc

    return pl.pallas_call(
        body, name=name,
        grid_spec=pltpu.PrefetchScalarGridSpec(
            num_scalar_prefetch=1, grid=(r2 // rb,),
            in_specs=[pl.BlockSpec((None, rb, c), lambda i, s: (s[0], i, 0)),
                      pl.BlockSpec((3, rb, c), lambda i, s: (0, i, 0))],
            out_specs=pl.BlockSpec((rb, c), lambda i, s: (i, 0))),
        out_shape=jax.ShapeDtypeStruct((r2, c), F32),
        compiler_params=_params(("parallel",)),
    )(chip, p, landed)


def _swap_rider(hs):
    def make(h_refs, o_refs, send, recv):
        x, y, z = _place()
        return [pltpu.make_async_remote_copy(
            src_ref=h, dst_ref=o, send_sem=send.at[a], recv_sem=recv.at[a], device_id=(x, y, 1 - z),
            device_id_type=MESH) for a, (h, o) in enumerate(zip(h_refs, o_refs))]

    return _copies_rider(hs, [jax.ShapeDtypeStruct(h.shape, h.dtype) for h in hs], (len(hs),), make)


def _reduce_scatter_vmem(gs, rows, rider, name):
    n = len(gs)
    r_in, r_out = len(rider.ins), len(rider.out_shapes)
    halves = [(r // 2, g.shape[-1]) for g, (r, _) in zip(gs, rows)]

    def body(*refs):
        p = 0
        parts = []
        for cnt in (n, r_in, n, n, r_out, n, n, n, 6):
            parts.append(refs[p:p + cnt])
            p += cnt
        g_refs, r_ins, mine, theirs, r_outs, recv, part, land, sems = parts
        r_sems = refs[p:]
        xs, xr, ss, sr, ws, wr = sems
        x, y, z = _place()
        chip = 2 * x + y
        sib = (x, y, 1 - z)
        rider.start(r_ins, r_outs, r_sems)

        def half_of(a, s, which):
            r2 = halves[a][0]
            if len(g_refs[a].shape) == 3:
                return g_refs[a].at[s, pl.ds(pl.multiple_of(which * r2, 8), r2)]
            return g_refs[a].at[pl.ds(pl.multiple_of(s * rows[a][1] + which * r2, 8), r2)]

        exchange = [pltpu.make_async_remote_copy(
            src_ref=half_of(a, s, 1 - z), dst_ref=recv[a].at[s], send_sem=xs.at[a, s], recv_sem=xr.at[a, s],
            device_id=sib, device_id_type=MESH) for a in range(n) for s in range(N_CHIPS)]
        for cp in exchange:
            cp.start()
        for cp in exchange:
            cp.wait()
        for a in range(n):
            for s in range(N_CHIPS):
                part[a][s] = (half_of(a, s, z)[...] + recv[a][s]).astype(BF16)
        scatter = []
        for a in range(n):
            for k, (dx, dy) in enumerate(_CHIP_OFFSETS):
                other = 2 * _flip(x, dx) + _flip(y, dy)
                scatter.append(pltpu.make_async_remote_copy(
                    src_ref=part[a].at[other], dst_ref=land[a].at[k], send_sem=ss.at[a, k], recv_sem=sr.at[a, k],
                    device_id=(_flip(x, dx), _flip(y, dy), z), device_id_type=MESH))
        for cp in scatter:
            cp.start()
        for cp in scatter:
            cp.wait()
        for a in range(n):
            acc = part[a][chip].astype(F32)
            for k in range(3):
                acc = acc + land[a][k].astype(F32)
            mine[a][...] = acc
        swap = [pltpu.make_async_remote_copy(
            src_ref=mine[a], dst_ref=theirs[a], send_sem=ws.at[a], recv_sem=wr.at[a], device_id=sib,
            device_id_type=MESH) for a in range(n)]
        for cp in swap:
            cp.start()
        for cp in swap:
            cp.wait()
        rider.finish(r_ins, r_outs, r_sems)

    half_shapes = [jax.ShapeDtypeStruct(h, F32) for h in halves]
    res = pl.pallas_call(
        body, name=name, in_specs=[_VMEM] * n + rider.in_specs, out_specs=[_VMEM] * (2 * n) + rider.out_specs,
        out_shape=half_shapes + half_shapes + rider.out_shapes,
        scratch_shapes=[pltpu.VMEM((N_CHIPS,) + h, F32) for h in halves] + [pltpu.VMEM((N_CHIPS,) + h, BF16) for h in halves]
        + [pltpu.VMEM((3,) + h, BF16) for h in halves]
        + [pltpu.SemaphoreType.DMA((n, N_CHIPS))] * 2 + [pltpu.SemaphoreType.DMA((n, 3))] * 2
        + [pltpu.SemaphoreType.DMA((n,))] * 2 + rider.sems,
        input_output_aliases={n + i: 2 * n + j for i, j in rider.aliases.items()},
        compiler_params=_params(None, VMEM_LIMIT),
    )(*gs, *rider.ins)
    return list(res[:n]), list(res[n:2 * n]), list(res[2 * n:])


SMALL_ROWS = 32
PACK_ROWS = 16


def _pack_small(st_post, st_ret, st_pre):
    d = st_post.shape[2]

    def body(po_ref, re_ref, pr_ref, o_ref):
        o_ref[...] = jnp.zeros(o_ref.shape, F32)
        o_ref[0:1, :] = pr_ref[0, 2:3, :] + pr_ref[1, 2:3, :] + pr_ref[2, 2:3, :]
        o_ref[1:2, :] = po_ref[0, 4:5, :] + po_ref[1, 4:5, :]
        o_ref[2:3, :] = po_ref[0, 5:6, :] + po_ref[1, 5:6, :]
        o_ref[3:4, 0:512] = re_ref[0, 0:1, :] + re_ref[1, 0:1, :]
        o_ref[4:5, :] = pr_ref[0, 3:4, :] + pr_ref[1, 3:4, :] + pr_ref[2, 3:4, :]
        o_ref[5:6, :] = pr_ref[0, 4:5, :] + pr_ref[1, 4:5, :] + pr_ref[2, 4:5, :]
        lane = lax.broadcasted_iota(jnp.int32, (1, LANES), 1)
        for row, src in ((6, 1), (10, 2)):
            acc = jnp.zeros((1, LANES), F32)
            for hd in range(HEADS):
                grp = re_ref[0, src:src + 1, hd * LANES:(hd + 1) * LANES] + re_ref[1, src:src + 1, hd * LANES:(hd + 1) * LANES]
                acc = acc + jnp.where(lane == hd, grp, 0.0)
            o_ref[row:row + 1, 0:LANES] = acc
        o_ref[7:8, :] = po_ref[0, 6:7, :] + po_ref[1, 6:7, :]
        o_ref[8:9, :] = pr_ref[2, 0:1, :]
        o_ref[9:10, :] = pr_ref[2, 1:2, :]
        for e in range(2):
            b = 12 + 6 * e
            o_ref[b:b + 1, :] = pr_ref[e, 0:1, :]
            o_ref[b + 1:b + 2, :] = pr_ref[e, 1:2, :]
            o_ref[b + 2:b + 3, :] = po_ref[e, 3:4, :]
            o_ref[b + 3:b + 4, :] = po_ref[e, 0:1, :]
            o_ref[b + 4:b + 5, :] = po_ref[e, 1:2, :]
            o_ref[b + 5:b + 6, :] = po_ref[e, 2:3, :]

    return pl.pallas_call(body, name="pack_small", out_shape=jax.ShapeDtypeStruct((SMALL_ROWS, d), F32))(st_post, st_ret, st_pre)


def _small_reduce(gathered):
    d = gathered.shape[2]

    def body(g_ref, o_ref):
        tot = g_ref[0, 0:PACK_ROWS, :]
        for dev in range(1, N_DEV):
            tot = tot + g_ref[dev, 0:PACK_ROWS, :]
        o_ref[0:PACK_ROWS, :] = tot
        for j in range(6):
            acc = g_ref[0, 12 + j:13 + j, :] + g_ref[0, 18 + j:19 + j, :]
            for dev in range(1, N_DEV):
                acc = acc + g_ref[dev, 12 + j:13 + j, :] + g_ref[dev, 18 + j:19 + j, :]
            if j < 2:
                acc = acc + o_ref[8 + j:9 + j, :]
            o_ref[PACK_ROWS + j:PACK_ROWS + j + 1, :] = acc
        o_ref[PACK_ROWS + 6:PACK_ROWS + 8, :] = jnp.zeros((2, d), F32)

    return pl.pallas_call(body, name="small_reduce", out_shape=jax.ShapeDtypeStruct((PACK_ROWS + 8, d), F32))(gathered)


_SMALL = (("g_attn", 0, 1024), ("g_ffn", 1, 1024), ("g_final", 2, 1024), ("g_ret", 3, 512), ("g_q_lora", 4, 384),
          ("g_kv_lora", 5, 256), ("ret_decay_fwd", 6, HEADS), ("ret_decay_bwd", 10, HEADS))
_SMALL_NAMES = tuple(s[0] for s in _SMALL) + ("c_ctx", "b_ada")


def _small_final(tot, dcc, sg8, ws, ms, vs):
    d = tot.shape[1]
    n = len(_SMALL_NAMES)

    def body(*refs):
        t_ref, dcc_ref, sg_ref = refs[0:3]
        w_refs, m_refs, v_refs = refs[3:3 + n], refs[3 + n:3 + 2 * n], refs[3 + 2 * n:3 + 3 * n]
        outs = refs[3 + 3 * n:]
        g_refs, d_refs, mo_refs, vo_refs = outs[0:n], outs[n:2 * n], outs[2 * n:3 * n], outs[3 * n:4 * n]
        l_ref = outs[4 * n]

        def update(i, g, sl=None):
            pick = (lambda r: r[...]) if sl is None else (lambda r: r[:, sl])
            dl, mn, vn = _adam_math(pick(w_refs[i]), g, pick(m_refs[i]), pick(v_refs[i]))
            if sl is None:
                g_refs[i][...], d_refs[i][...], mo_refs[i][...], vo_refs[i][...] = g, dl, mn, vn
            else:
                g_refs[i][:, sl], d_refs[i][:, sl], mo_refs[i][:, sl], vo_refs[i][:, sl] = g, dl, mn, vn

        for i, (name, row, width) in enumerate(_SMALL):
            g = t_ref[row:row + 1, 0:width]
            if name == "ret_decay_fwd":
                g = g * sg_ref[0:1, 0:width]
            elif name == "ret_decay_bwd":
                g = g * sg_ref[1:2, 0:width]
            update(i, g)
        i_cc, i_b = n - 2, n - 1
        cc = w_refs[i_cc][...]
        s = 1.0 / (1.0 + jnp.exp(-cc))
        dsilu = dcc_ref[0, 0:1, :] + dcc_ref[2, 0:1, :] + dcc_ref[4, 0:1, :] + dcc_ref[6, 0:1, :]
        update(i_cc, dsilu * (s * (1.0 + cc * (1.0 - s))))
        for j in range(6):
            update(i_b, t_ref[PACK_ROWS + j:PACK_ROWS + j + 1, :], pl.ds(j * d, d))
        l_ref[...] = jnp.broadcast_to((0.5 / d) * jnp.sum(t_ref[7:8, :], keepdims=True), l_ref.shape)

    shapes = [jax.ShapeDtypeStruct(a.shape, F32) for a in ws]
    outs = pl.pallas_call(
        body, name="small_final", out_shape=shapes * 4 + [jax.ShapeDtypeStruct((8, LANES), F32)],
    )(tot, dcc, sg8, *ws, *ms, *vs)
    return outs[0:n], outs[n:2 * n], outs[2 * n:3 * n], outs[3 * n:4 * n], outs[4 * n]


_WEIGHTS = ("c_ctx", "w_ada", "b_ada", "g_attn", "g_ffn", "w_in", "ret_decay_fwd", "ret_decay_bwd", "g_ret", "g_q_lora",
            "w_uq", "g_kv_lora", "w_ukv", "w_out", "w_ff1", "w_ff2", "g_final")
_BIG = ("w_in", "w_uq", "w_ukv", "w_out", "w_ff1", "w_ff2")
_TRANSPOSED = ("w_in", "w_uq")


def kernel(x, c, ctx, c_ctx, w_ada, b_ada, g_attn, g_ffn, w_in, ret_decay_fwd, ret_decay_bwd, g_ret, g_q_lora, w_uq, g_kv_lora, w_ukv, w_out, w_ff1, w_ff2, g_final, loss_target, m_c_ctx, m_w_ada, m_b_ada, m_g_attn, m_g_ffn, m_w_in, m_ret_decay_fwd, m_ret_decay_bwd, m_g_ret, m_g_q_lora, m_w_uq, m_g_kv_lora, m_w_ukv, m_w_out, m_w_ff1, m_w_ff2, m_g_final, v_c_ctx, v_w_ada, v_b_ada, v_g_attn, v_g_ffn, v_w_in, v_ret_decay_fwd, v_ret_decay_bwd, v_g_ret, v_g_q_lora, v_w_uq, v_g_kv_lora, v_w_ukv, v_w_out, v_w_ff1, v_w_ff2, v_g_final):
    w = dict(c_ctx=c_ctx, w_ada=w_ada, b_ada=b_ada, g_attn=g_attn, g_ffn=g_ffn, w_in=w_in, ret_decay_fwd=ret_decay_fwd,
             ret_decay_bwd=ret_decay_bwd, g_ret=g_ret, g_q_lora=g_q_lora, w_uq=w_uq, g_kv_lora=g_kv_lora, w_ukv=w_ukv,
             w_out=w_out, w_ff1=w_ff1, w_ff2=w_ff2, g_final=g_final)
    m = dict(c_ctx=m_c_ctx, w_ada=m_w_ada, b_ada=m_b_ada, g_attn=m_g_attn, g_ffn=m_g_ffn, w_in=m_w_in,
             ret_decay_fwd=m_ret_decay_fwd, ret_decay_bwd=m_ret_decay_bwd, g_ret=m_g_ret, g_q_lora=m_g_q_lora, w_uq=m_w_uq,
             g_kv_lora=m_g_kv_lora, w_ukv=m_w_ukv, w_out=m_w_out, w_ff1=m_w_ff1, w_ff2=m_w_ff2, g_final=m_g_final)
    v = dict(c_ctx=v_c_ctx, w_ada=v_w_ada, b_ada=v_b_ada, g_attn=v_g_attn, g_ffn=v_g_ffn, w_in=v_w_in,
             ret_decay_fwd=v_ret_decay_fwd, ret_decay_bwd=v_ret_decay_bwd, g_ret=v_g_ret, g_q_lora=v_g_q_lora, w_uq=v_w_uq,
             g_kv_lora=v_g_kv_lora, w_ukv=v_w_ukv, w_out=v_w_out, w_ff1=v_w_ff1, w_ff2=v_w_ff2, g_final=v_g_final)
    xi, yi, ci = lax.axis_index("x"), lax.axis_index("y"), lax.axis_index("c")
    chip = 2 * xi + yi
    dev = 2 * chip + ci
    nex, seq, d = x.shape
    n_ada = w_ada.shape[2]

    dec = jnp.zeros((8, LANES), F32).at[0, :HEADS].set(ret_decay_fwd[0]).at[1, :HEADS].set(ret_decay_bwd[0])
    lg8, sg8 = _decay_prep(dec)
    lg = lg8[:2, :HEADS]

    def shard_of(t, k):
        return t[k][0].T if k in _TRANSPOSED else t[k][0]

    shard = {k: shard_of(w, k) for k in _BIG}
    head_rows = MLA_NOPE + MLA_ROPE
    shard["w_uq"] = jnp.pad(shard["w_uq"], ((0, MLA_HEAD - head_rows), (0, 0)))
    slot = chip.reshape(1).astype(jnp.int32)
    core = ci.reshape(1).astype(jnp.int32)
    slots = {k: _cast_into_slot(shard[k], slot, "cast_" + k)[0] for k in _BIG if k != "w_ff1"}
    slots["w_ff1"], (w_in_x, w_uq_x, w_ukv_x, c8) = _cast_into_slot(
        shard["w_ff1"], slot, "cast_w_ff1",
        rider=_merge_riders(_gather_ici_rider([slots[k] for k in _EARLY]),
                            _gather8_rider(jnp.pad(c, ((0, 8 - nex), (0, 0))), in_vmem=False)))

    a_in = jnp.concatenate([c8[:, :nex].reshape(N_DEV * nex, d), c_ctx.reshape(1, d), jnp.zeros((7, d), F32)], axis=0)
    b_sh = lax.dynamic_slice(b_ada, (0, chip * n_ada), (1, n_ada))
    mod_sh = _mod_fwd(a_in, w_ada[0], b_sh)
    mod8, w_in_f, w_uq_k, w_ukv_k = _run_rider(
        _merge_riders(_gather8_rider(mod_sh), _gather_d2d_rider([w_in_x, w_uq_x, w_ukv_x])), "ag_early")
    w_in_k = jnp.pad(w_in_f.reshape(IN_COLS, d), ((0, IN_PAD - IN_COLS), (0, 0)))
    mod_all = mod8[0::2].transpose(1, 0, 2).reshape(a_in.shape[0], N_CHIPS * n_ada)
    mod_me = lax.dynamic_slice(mod_all, (nex * dev, 0), (nex, N_CHIPS * n_ada)).reshape(nex, 6, d)
    mod_c = mod_all[N_DEV * nex].reshape(1, 6, d)
    modv = jnp.pad(jnp.concatenate([mod_me, mod_c], axis=0), ((0, 0), (0, 2), (0, 0)))

    gx, g_early, late, st_post, st_ret, st_pre = _local_step(
        x, ctx, loss_target, modv, lg, g_attn, g_ffn, g_final.reshape(1, d), g_ret, g_q_lora, g_kv_lora,
        w_in_k, w_uq_k, w_ukv_k, [slots[k] for k in _LATE], (core, slot))

    mine, theirs, (*late_theirs, gathered) = _reduce_scatter_vmem(
        g_early, [(IN_COLS // N_CHIPS, IN_COLS // N_CHIPS), (head_rows, MLA_HEAD), (KV_LORA, KV_LORA)],
        _merge_riders(_swap_rider(late), _gather8_rider(_pack_small(st_post, st_ret, st_pre))), "rs_early")
    tot = _small_reduce(gathered)
    dm = jnp.concatenate([
        gathered[:, 12:24].reshape(N_DEV * nex, 6 * d),
        jnp.concatenate([tot[8:10].reshape(1, 2 * d), jnp.zeros((1, 4 * d), F32)], axis=1),
        jnp.zeros((7, 6 * d), F32)], axis=0)
    dm_sh = lax.dynamic_slice(dm, (0, chip * n_ada), (dm.shape[0], n_ada))
    g_ada, da = _mod_bwd(a_in, dm_sh, w_ada[0])
    dcc = _allgather8(da[N_DEV * nex:], "ag_dcc")
    halves = dict(zip(_EARLY, zip(mine, theirs)))
    halves.update(zip(_LATE, zip(late, late_theirs)))
    grad, delta, new_m, new_v = {}, {}, {}, {}
    for k in _BIG:
        a, b = halves[k]
        res = _adamw_halves(shard_of(w, k), a, b, shard_of(m, k), shard_of(v, k), core, "adamw_" + k)
        grad[k], delta[k], new_m[k], new_v[k] = [(o.T if k in _TRANSPOSED else o).reshape(w[k].shape) for o in res]

    shp = w_ada.shape
    outs, _ = _adamw(w_ada[0], g_ada, m["w_ada"][0], v["w_ada"][0], "adamw_w_ada")
    grad["w_ada"] = g_ada.reshape(shp)
    delta["w_ada"], new_m["w_ada"], new_v["w_ada"] = [o.reshape(shp) for o in outs]
    rows = [{k: t[k].reshape(1, -1) for k in _SMALL_NAMES} for t in (w, m, v)]
    small = _small_final(tot, dcc, sg8, *[[t[k] for k in _SMALL_NAMES] for t in rows])
    for res, outs in zip((grad, delta, new_m, new_v), small[:4]):
        for k, o in zip(_SMALL_NAMES, outs):
            res[k] = o.reshape(w[k].shape)
    return (small[4][0, 0], gx, *[grad[k] for k in _WEIGHTS], *[delta[k] for k in _WEIGHTS],
            *[new_m[k] for k in _WEIGHTS], *[new_v[k] for k in _WEIGHTS])
```

```python
import functools
import math

import jax
import jax.numpy as jnp
from jax import lax
from jax.experimental import pallas as pl
from jax.experimental.pallas import tpu as pltpu

F32 = jnp.float32
BF16 = jnp.bfloat16
MESH = pl.DeviceIdType.MESH

EPS = 1e-6
D_MODEL = 1024
D_FF = 4096
HEADS = 4
RET_DK = 64
RET_DV = 128
MLA_NOPE = 128
MLA_ROPE = 64
MLA_HEAD = 256
Q_LORA = 384
KV_LORA = 256
GRID_W = 64
ROPE_BASE = 10000.0
IN_COLS = 2240
IN_PAD = 2304
PG_COLS = 1152
N_CHIPS = 4
N_DEV = 8
LANES = 128
ADAM_LR = 0.001
ADAM_B1 = 0.9
ADAM_B2 = 0.999
ADAM_EPS = 1e-08
ADAM_WD = 0.01
ADAM_STEP = 10
VMEM_LIMIT = 56 * 1024 * 1024


def _dot(a, b):
    return jnp.dot(a, b, preferred_element_type=F32)


def _dot_nt(a, b):
    return lax.dot_general(a, b, (((1,), (1,)), ((), ())), preferred_element_type=F32)


def _dot_tn(a, b):
    return lax.dot_general(a, b, (((0,), (0,)), ((), ())), preferred_element_type=F32)


def _params(sem=None, vmem=None):
    return pltpu.CompilerParams(dimension_semantics=sem, vmem_limit_bytes=vmem)


def _full(shape):
    n = len(shape)
    return pl.BlockSpec(shape, lambda *_: (0,) * n)


def _rope(x, cos, sin):
    w = x.shape[-1]
    lo = (lax.broadcasted_iota(jnp.int32, (1, w), 1) % 64) < 32
    swapped = jnp.where(lo, pltpu.roll(x, w - 32, 1), pltpu.roll(x, 32, 1))
    return x * cos + swapped * sin


def _rope_t(g, cos, sin):
    w = g.shape[-1]
    lo = (lax.broadcasted_iota(jnp.int32, (1, w), 1) % 64) < 32
    t = g * sin
    swapped = jnp.where(lo, pltpu.roll(t, w - 32, 1), pltpu.roll(t, 32, 1))
    return g * cos + swapped


def _rope_tables(seq, tm):
    rows = seq // GRID_W
    row = jnp.repeat(jnp.arange(rows, dtype=F32), GRID_W)
    col = jnp.tile(jnp.arange(GRID_W, dtype=F32), rows)
    n_freq = RET_DK // 4
    freq = ROPE_BASE ** (-jnp.arange(n_freq, dtype=F32) / n_freq)
    ang = jnp.concatenate([row[:, None] * freq, col[:, None] * freq], axis=-1)
    cos, sin = jnp.cos(ang), jnp.sin(ang)
    cos_t = jnp.tile(jnp.concatenate([cos, cos], -1), (1, HEADS))
    sin_t = jnp.tile(jnp.concatenate([-sin, sin], -1), (1, HEADS))
    cos_t = jnp.concatenate([cos_t, jnp.ones((tm, 4 * RET_DK), F32)], 0)
    sin_t = jnp.concatenate([sin_t, jnp.zeros((tm, 4 * RET_DK), F32)], 0)
    return cos_t, sin_t


def _adam_math(w, g, m, v):
    mn = ADAM_B1 * m + (1.0 - ADAM_B1) * g
    vn = ADAM_B2 * v + (1.0 - ADAM_B2) * (g * g)
    m_hat = mn / (1.0 - ADAM_B1 ** ADAM_STEP)
    v_hat = vn / (1.0 - ADAM_B2 ** ADAM_STEP)
    return -ADAM_LR * (m_hat / (jnp.sqrt(v_hat) + ADAM_EPS) + ADAM_WD * w), mn, vn


def _cast_into_slot(w, slot, name, rider=None):
    r, c = w.shape
    rb = max(b for b in range(16, 257, 16) if r % b == 0)

    def body(s_ref, w_ref, o_ref):
        o_ref[...] = w_ref[...].astype(BF16)

    (out,), carried = _hosted_call(
        body, (w,), name=name, grid=(r // rb,), prefetch=(slot,),
        in_specs=[pl.BlockSpec((rb, c), lambda i, s: (i, 0))],
        out_specs=[pl.BlockSpec((None, rb, c), lambda i, s: (s[0], i, 0))],
        out_shape=[jax.ShapeDtypeStruct((N_CHIPS, r, c), BF16)], sem=("parallel",), rider=rider)
    return out, carried


def _adamw_halves(w, mine, theirs, m, v, core, name):
    r, c = w.shape
    r2 = r // 2
    rb = max(b for b in range(8, r2 + 1, 8) if r2 % b == 0 and b * c * 4 <= (1 << 21))
    nbh = r2 // rb

    def body(z_ref, w_ref, a_ref, b_ref, m_ref, v_ref, g_ref, d_ref, mo_ref, vo_ref):
        here = (pl.program_id(0) // nbh) == z_ref[0]
        gg = jnp.where(here, a_ref[...], b_ref[...])
        g_ref[...] = gg
        d_ref[...], mo_ref[...], vo_ref[...] = _adam_math(w_ref[...], gg, m_ref[...], v_ref[...])

    spec = pl.BlockSpec((rb, c), lambda i, z: (i, 0))
    a_spec = pl.BlockSpec((rb, c), lambda i, z: (jnp.clip(i - z[0] * nbh, 0, nbh - 1), 0))
    b_spec = pl.BlockSpec((rb, c), lambda i, z: (jnp.clip(i - (1 - z[0]) * nbh, 0, nbh - 1), 0))
    shp = jax.ShapeDtypeStruct((r, c), F32)
    return pl.pallas_call(
        body, name=name,
        grid_spec=pltpu.PrefetchScalarGridSpec(
            num_scalar_prefetch=1, grid=(r // rb,), in_specs=[spec, a_spec, b_spec, spec, spec], out_specs=[spec] * 4),
        out_shape=[shp] * 4,
        compiler_params=_params(("parallel",)),
    )(core, w, mine, theirs, m, v)


def _adamw(w, g, m, v, name, rider=None):
    r, c = w.shape
    rb = r
    for cand in (256, 128, 64, 32, 16, 8):
        if r % cand == 0 and cand * c * 4 <= (1 << 20):
            rb = cand
            break
    if r * c * 4 <= (1 << 20):
        rb = r

    def body(w_ref, g_ref, m_ref, v_ref, d_ref, mo_ref, vo_ref):
        d_ref[...], mo_ref[...], vo_ref[...] = _adam_math(w_ref[...], g_ref[...], m_ref[...], v_ref[...])

    spec = pl.BlockSpec((rb, c), lambda i: (i, 0))
    shp = jax.ShapeDtypeStruct((r, c), F32)
    return _hosted_call(
        body, (w, g, m, v), name=name, grid=(r // rb,), in_specs=[spec] * 4, out_specs=[spec] * 3, out_shape=[shp] * 3,
        sem=("parallel",), rider=rider)


def _decay_prep(dec):
    def body(d_ref, lg_ref, sg_ref):
        d = d_ref[...]
        lg_ref[...] = jnp.minimum(d, 0.0) - jnp.log(1.0 + jnp.exp(-jnp.abs(d)))
        sg_ref[...] = 1.0 / (1.0 + jnp.exp(d))

    shp = jax.ShapeDtypeStruct(dec.shape, F32)
    return pl.pallas_call(body, name="decay_prep", out_shape=[shp, shp])(dec)


def _mod_fwd(a_in, w_ada, b_sh):
    rows, d = a_in.shape
    n = w_ada.shape[1]
    bn = 512

    def body(a_ref, w_ref, b_ref, o_ref):
        a = a_ref[...]
        s = (a / (1.0 + jnp.exp(-a))).astype(BF16)
        o_ref[...] = _dot(s, w_ref[...].astype(BF16)) + b_ref[...]

    return pl.pallas_call(
        body, name="mod_fwd", grid=(n // bn,),
        in_specs=[_full((rows, d)), pl.BlockSpec((d, bn), lambda j: (0, j)), pl.BlockSpec((1, bn), lambda j: (0, j))],
        out_specs=pl.BlockSpec((rows, bn), lambda j: (0, j)),
        out_shape=jax.ShapeDtypeStruct((rows, n), F32),
        compiler_params=_params(("parallel",)),
    )(a_in, w_ada, b_sh)


def _mod_bwd(a_in, dm, w_ada):
    rows, d = a_in.shape
    n = w_ada.shape[1]
    bn = 512
    nb = n // bn

    def body(a_ref, dm_ref, w_ref, gw_ref, da_ref):
        j = pl.program_id(0)
        a = a_ref[...]
        s = (a / (1.0 + jnp.exp(-a))).astype(BF16)
        dmb = dm_ref[...].astype(BF16)
        gw_ref[...] = _dot_tn(s, dmb)
        part = _dot_nt(dmb, w_ref[...].astype(BF16))

        @pl.when(j == 0)
        def _():
            da_ref[...] = part

        @pl.when(j > 0)
        def _():
            da_ref[...] += part

    return pl.pallas_call(
        body, name="mod_bwd", grid=(nb,),
        in_specs=[_full((rows, d)), pl.BlockSpec((rows, bn), lambda j: (0, j)), pl.BlockSpec((d, bn), lambda j: (0, j))],
        out_specs=[pl.BlockSpec((d, bn), lambda j: (0, j)), _full((rows, d))],
        out_shape=[jax.ShapeDtypeStruct((d, n), F32), jax.ShapeDtypeStruct((rows, d), F32)],
        compiler_params=_params(("arbitrary",)),
    )(a_in, dm, w_ada)


def _pre_fwd(x2, ctx2, modv, g_attn, w_in, g_q, g_kv, w_uq, w_ukv, cos_t, sin_t, *, seq, tm, rider=None):
    t_lat, d = x2.shape
    t_ctx = ctx2.shape[0]
    nl, nc = t_lat // tm, t_ctx // tm
    n_all = t_lat + t_ctx
    tpe = seq // tm
    nex = t_lat // seq

    def body(x_ref, c_ref, mod_ref, g_ref, win_ref, gq_ref, gkv_ref, wuq_ref, wukv_ref, cos_ref, sin_ref,
             h_ref, pg_ref, rq_ref, rk_ref, rv_ref, nq_ref, nkv_ref, q_ref, k_ref, v_ref):
        i = pl.program_id(0)
        xt = jnp.where(i < nl, x_ref[...], c_ref[...])
        sh = mod_ref[0, 0:1, :]
        sc = mod_ref[0, 1:2, :]
        r = lax.rsqrt(jnp.mean(xt * xt, axis=-1, keepdims=True) + EPS)
        hb = ((xt * r) * g_ref[...] * (1.0 + sc) + sh).astype(BF16)
        h_ref[...] = hb
        p = _dot_nt(hb, win_ref[...])
        cos = cos_ref[...]
        sin = sin_ref[...]
        rq_ref[...] = _rope(p[:, 0:256], cos, sin).astype(BF16)
        rk_ref[...] = _rope(p[:, 256:512] * (RET_DK ** -0.5), cos, sin).astype(BF16)
        rv_ref[...] = p[:, 512:1024].astype(BF16)
        pg_ref[...] = p[:, 1024:2176]
        cq = p[:, 1536:1920]
        ckv = p[:, 1920:2176]
        nqb = (cq * lax.rsqrt(jnp.mean(cq * cq, axis=-1, keepdims=True) + EPS) * gq_ref[...]).astype(BF16)
        nkvb = (ckv * lax.rsqrt(jnp.mean(ckv * ckv, axis=-1, keepdims=True) + EPS) * gkv_ref[...]).astype(BF16)
        nq_ref[...] = nqb
        nkv_ref[...] = nkvb
        cos1 = cos[:, 0:LANES]
        sin1 = sin[:, 0:LANES]
        kpe = _rope(p[:, 2176:2304], cos1, sin1).astype(BF16)
        for hd in range(HEADS):
            o = hd * MLA_HEAD
            qh = _dot_nt(nqb, wuq_ref[hd]) * MLA_SCALE
            q_ref[:, o:o + 128] = qh[:, 0:128].astype(BF16)
            q_ref[:, o + 128:o + 256] = _rope(qh[:, 128:256], cos1, sin1).astype(BF16)
            kvh = _dot(nkvb, wukv_ref[hd])
            k_ref[:, o:o + 128] = kvh[:, 0:128].astype(BF16)
            k_ref[:, o + 128:o + 256] = kpe
            v_ref[:, hd * 128:(hd + 1) * 128] = kvh[:, 128:256].astype(BF16)

    def tile(width):
        return pl.BlockSpec((tm, width), lambda i: (i, 0))

    widths = (d, PG_COLS, 256, 256, 512, Q_LORA, KV_LORA, HEADS * MLA_HEAD, HEADS * MLA_HEAD, HEADS * 128)
    dtypes = (BF16, F32, BF16, BF16, BF16, BF16, BF16, BF16, BF16, BF16)
    tab = pl.BlockSpec((tm, 256), lambda i: (jnp.where(i < nl, i % tpe, tpe), 0))
    return _hosted_call(
        body, (x2, ctx2, modv, g_attn, w_in, g_q, g_kv, w_uq, w_ukv, cos_t, sin_t), name="pre_fwd", grid=(nl + nc,),
        in_specs=[
            pl.BlockSpec((tm, d), lambda i: (jnp.minimum(i, nl - 1), 0)),
            pl.BlockSpec((tm, d), lambda i: (jnp.maximum(i - nl, 0), 0)),
            pl.BlockSpec((1, 8, d), lambda i: (jnp.minimum(i // tpe, nex), 0, 0)),
            _full((1, d)), _full(w_in.shape), _full((1, Q_LORA)), _full((1, KV_LORA)),
            _full(w_uq.shape), _full(w_ukv.shape), tab, tab,
        ],
        out_specs=[tile(w) for w in widths],
        out_shape=[jax.ShapeDtypeStruct((n_all, w), dt) for w, dt in zip(widths, dtypes)],
        sem=("parallel",), rider=rider)


def _post(yret, ymla, x2, tgt2, modv, g_ffn, g_fin, w_out, w_ff1, w_ff2, *, seq, tm):
    t_lat, d = x2.shape
    nl = t_lat // tm
    tpe = seq // tm
    nex = t_lat // seq
    n_slab = w_ff1.shape[0]
    fs = w_ff1.shape[2]

    def body(yr_ref, ym_ref, x_ref, t_ref, mod_ref, gf_ref, gl_ref, wo_ref, w1_ref, w2_ref,
             mix_ref, a_ref, du_ref, h2_ref, df_ref, dmo_ref, dmix_ref, dxm_ref, st_ref, ru_ref):
        i = pl.program_id(0)
        gt_a = mod_ref[0, 2:3, :]
        sh_f = mod_ref[0, 3:4, :]
        sc_f = mod_ref[0, 4:5, :]
        gt_f = mod_ref[0, 5:6, :]
        g_ffn_v = gf_ref[...]
        g_fin_v = gl_ref[...]
        yr = yr_ref[...]
        ym = ym_ref[...]
        mix_ref[:, 0:512] = yr
        mix_ref[:, 512:1024] = ym
        op = _dot(yr, wo_ref[0:512, :]) + _dot(ym, wo_ref[512:1024, :])
        x_mid = x_ref[...] + gt_a * op
        r2 = lax.rsqrt(jnp.mean(x_mid * x_mid, axis=-1, keepdims=True) + EPS)
        xh2 = x_mid * r2
        h2b = (xh2 * g_ffn_v * (1.0 + sc_f) + sh_f).astype(BF16)
        h2_ref[...] = h2b
        f = jnp.zeros((tm, d), F32)
        for s in range(n_slab):
            ru = jnp.maximum(_dot(h2b, w1_ref[s]), 0.0)
            ru_ref[:, s * fs:(s + 1) * fs] = ru
            ab = (ru * ru).astype(BF16)
            a_ref[:, s * fs:(s + 1) * fs] = ab
            f = f + _dot(ab, w2_ref[s * fs:(s + 1) * fs, :])
        x_out = x_mid + gt_f * f
        r3 = lax.rsqrt(jnp.mean(x_out * x_out, axis=-1, keepdims=True) + EPS)
        xh3 = x_out * r3
        err = xh3 * g_fin_v - t_ref[...]
        dy = err * (1.0 / d)
        dxh3 = dy * g_fin_v
        dx_out = r3 * (dxh3 - xh3 * jnp.mean(dxh3 * xh3, axis=-1, keepdims=True))
        dfb = (dx_out * gt_f).astype(BF16)
        df_ref[...] = dfb
        dh2 = jnp.zeros((tm, d), F32)
        for s in range(n_slab):
            da = _dot_nt(dfb, w2_ref[s * fs:(s + 1) * fs, :])
            dub = (da * (2.0 * ru_ref[:, s * fs:(s + 1) * fs])).astype(BF16)
            du_ref[:, s * fs:(s + 1) * fs] = dub
            dh2 = dh2 + _dot_nt(dub, w1_ref[s])
        dxh2 = dh2 * (1.0 + sc_f) * g_ffn_v
        dx_mid = dx_out + r2 * (dxh2 - xh2 * jnp.mean(dxh2 * xh2, axis=-1, keepdims=True))
        dxm_ref[...] = dx_mid
        dmob = (dx_mid * gt_a).astype(BF16)
        dmo_ref[...] = dmob
        dmix_ref[...] = _dot_nt(dmob, wo_ref[...]).astype(BF16)

        def rsum(v):
            return jnp.sum(v, axis=0, keepdims=True)

        stats = jnp.concatenate([
            rsum(dh2), rsum(dh2 * xh2 * g_ffn_v), rsum(dx_out * f), rsum(dx_mid * op),
            rsum(dh2 * (1.0 + sc_f) * xh2), rsum(dy * xh3), rsum(err * err), jnp.zeros((1, d), F32)], axis=0)

        @pl.when(i % tpe == 0)
        def _():
            st_ref[0] = stats

        @pl.when(i % tpe != 0)
        def _():
            st_ref[0] += stats

    def tile(width):
        return pl.BlockSpec((tm, width), lambda i: (i, 0))

    widths = (d, D_FF, D_FF, d, d, d, d, d)
    dtypes = (BF16, BF16, BF16, BF16, BF16, BF16, BF16, F32)
    const = pl.Buffered(1)
    return pl.pallas_call(
        body, name="post", grid=(nl,),
        in_specs=[
            tile(512), tile(512), tile(d), tile(d),
            pl.BlockSpec((1, 8, d), lambda i: (i // tpe, 0, 0)),
            _full((1, d)), _full((1, d)),
            pl.BlockSpec(w_out.shape, lambda i: (0, 0), pipeline_mode=const),
            pl.BlockSpec(w_ff1.shape, lambda i: (0, 0, 0), pipeline_mode=const),
            pl.BlockSpec(w_ff2.shape, lambda i: (0, 0), pipeline_mode=const),
        ],
        out_specs=[tile(w) for w in widths] + [pl.BlockSpec((1, 8, d), lambda i: (i // tpe, 0, 0))],
        out_shape=[jax.ShapeDtypeStruct((t_lat, w), dt) for w, dt in zip(widths, dtypes)]
        + [jax.ShapeDtypeStruct((nex, 8, d), F32)],
        scratch_shapes=[pltpu.VMEM((tm, D_FF), F32)],
        compiler_params=_params(("arbitrary",), VMEM_LIMIT),
    )(yret, ymla, x2, tgt2, modv, g_ffn, g_fin, w_out, w_ff1, w_ff2)


def _pre_bwd(x2, ctx2, modv, g_attn, pg, drq, drk, dkc_r, drv, dvc_r, drg, dq_m, dkl, dkc, dvl, dvc, dxm,
             w_in, g_q, g_kv, w_uq, w_ukv, cos_t, sin_t, *, seq, tm, rider=None):
    t_lat, d = x2.shape
    t_ctx = ctx2.shape[0]
    nl, nc = t_lat // tm, t_ctx // tm
    n_all = t_lat + t_ctx
    tpe = seq // tm
    nex = t_lat // seq

    def body(x_ref, c_ref, mod_ref, g_ref, pg_ref, drq_ref, drk_ref, dkcr_ref, drv_ref, dvcr_ref, drg_ref,
             dq_ref, dkl_ref, dkc_ref, dvl_ref, dvc_ref, dxm_ref, win_ref, gq_ref, gkv_ref, wuq_ref, wukv_ref,
             cos_ref, sin_ref, dpb_ref, dqf_ref, dkvf_ref, gx_ref, st_ref):
        i = pl.program_id(0)
        lat = i < nl
        latf = lat.astype(F32)
        cos = cos_ref[...]
        sin = sin_ref[...]
        cos1 = cos[:, 0:LANES]
        sin1 = sin[:, 0:LANES]
        d_rq = _rope_t(drq_ref[...] * latf, cos, sin)
        d_rk = _rope_t(jnp.where(lat, drk_ref[...], dkcr_ref[...]), cos, sin) * (RET_DK ** -0.5)
        d_rv = jnp.where(lat, drv_ref[...], dvcr_ref[...])
        d_rg = drg_ref[...] * latf
        dq_all = dq_ref[...] * (latf * MLA_SCALE)
        dk_all = jnp.where(lat, dkl_ref[...], dkc_ref[...])
        dv_all = jnp.where(lat, dvl_ref[...], dvc_ref[...])
        dnq = jnp.zeros((tm, Q_LORA), F32)
        dnkv = jnp.zeros((tm, KV_LORA), F32)
        dkpe = jnp.zeros((tm, LANES), F32)
        for hd in range(HEADS):
            o = hd * MLA_HEAD
            dqh = jnp.concatenate([dq_all[:, o:o + 128], _rope_t(dq_all[:, o + 128:o + 256], cos1, sin1)],
                                  axis=1).astype(BF16)
            dqf_ref[:, o:o + 256] = dqh
            dnq = dnq + _dot(dqh, wuq_ref[hd])
            dkpe = dkpe + dk_all[:, o + 128:o + 256]
            dkvh = jnp.concatenate([dk_all[:, o:o + 128], dv_all[:, hd * 128:(hd + 1) * 128]], axis=1).astype(BF16)
            dkvf_ref[:, o:o + 256] = dkvh
            dnkv = dnkv + _dot_nt(dkvh, wukv_ref[hd])
        d_kpe = _rope_t(dkpe, cos1, sin1)
        pgv = pg_ref[...]
        cq = pgv[:, 512:896]
        ckv = pgv[:, 896:1152]
        rq_ = lax.rsqrt(jnp.mean(cq * cq, axis=-1, keepdims=True) + EPS)
        cqh = cq * rq_
        dcqh = dnq * gq_ref[...]
        d_cq = rq_ * (dcqh - cqh * jnp.mean(dcqh * cqh, axis=-1, keepdims=True))
        rkv_ = lax.rsqrt(jnp.mean(ckv * ckv, axis=-1, keepdims=True) + EPS)
        ckvh = ckv * rkv_
        dckvh = dnkv * gkv_ref[...]
        d_ckv = rkv_ * (dckvh - ckvh * jnp.mean(dckvh * ckvh, axis=-1, keepdims=True))
        dpb = jnp.concatenate([d_rq, d_rk, d_rv, d_rg, d_cq, d_ckv, d_kpe], axis=1).astype(BF16)
        dpb_ref[...] = dpb
        dh = _dot(dpb, win_ref[...])
        xt = jnp.where(lat, x_ref[...], c_ref[...])
        sc = mod_ref[0, 1:2, :]
        g = g_ref[...]
        r = lax.rsqrt(jnp.mean(xt * xt, axis=-1, keepdims=True) + EPS)
        xh = xt * r
        dxh = dh * (1.0 + sc) * g
        dx = r * (dxh - xh * jnp.mean(dxh * xh, axis=-1, keepdims=True))

        @pl.when(lat)
        def _():
            gx_ref[...] = dxm_ref[...] + dx

        def rsum(v):
            return jnp.sum(v, axis=0, keepdims=True)

        def widen(v):
            return jnp.concatenate([v, jnp.zeros((1, d - v.shape[1]), F32)], axis=1)

        stats = jnp.concatenate([
            rsum(dh), rsum(dh * xh * g), rsum(dh * (1.0 + sc) * xh), widen(rsum(dnq * cqh)), widen(rsum(dnkv * ckvh)),
            jnp.zeros((3, d), F32)], axis=0)
        first = jnp.logical_or(jnp.logical_and(lat, i % tpe == 0), i == nl)

        @pl.when(first)
        def _():
            st_ref[0] = stats

        @pl.when(jnp.logical_not(first))
        def _():
            st_ref[0] += stats

    def lat_tile(width):
        return pl.BlockSpec((tm, width), lambda i: (jnp.minimum(i, nl - 1), 0))

    def ctx_tile(width):
        return pl.BlockSpec((tm, width), lambda i: (jnp.maximum(i - nl, 0), 0))

    def tile(width):
        return pl.BlockSpec((tm, width), lambda i: (i, 0))

    tab = pl.BlockSpec((tm, 256), lambda i: (jnp.where(i < nl, i % tpe, tpe), 0))
    ex = pl.BlockSpec((1, 8, d), lambda i: (jnp.minimum(i // tpe, nex), 0, 0))
    return _hosted_call(
        body, (x2, ctx2, modv, g_attn, pg, drq, drk, dkc_r, drv, dvc_r, drg, dq_m, dkl, dkc, dvl, dvc, dxm,
               w_in, g_q, g_kv, w_uq, w_ukv, cos_t, sin_t), name="pre_bwd", grid=(nl + nc,),
        in_specs=[
            lat_tile(d), ctx_tile(d), ex, _full((1, d)), tile(PG_COLS),
            lat_tile(256), lat_tile(256), ctx_tile(256), lat_tile(512), ctx_tile(512), lat_tile(512),
            lat_tile(1024), lat_tile(1024), ctx_tile(1024), lat_tile(512), ctx_tile(512), lat_tile(d),
            _full(w_in.shape), _full((1, Q_LORA)), _full((1, KV_LORA)), _full(w_uq.shape), _full(w_ukv.shape),
            tab, tab,
        ],
        out_specs=[tile(IN_PAD), tile(1024), tile(1024), lat_tile(d), ex],
        out_shape=[
            jax.ShapeDtypeStruct((n_all, IN_PAD), BF16), jax.ShapeDtypeStruct((n_all, 1024), BF16),
            jax.ShapeDtypeStruct((n_all, 1024), BF16), jax.ShapeDtypeStruct((t_lat, d), F32),
            jax.ShapeDtypeStruct((nex + 1, 8, d), F32),
        ],
        sem=("arbitrary",), rider=rider)


MLA_SCALE = 1.0 / math.sqrt(MLA_NOPE + MLA_ROPE)
KEY_BLOCK = 1024


def _mla_specs(t_lat, seq, ctx_len, tq, heads=1):
    nqt = seq // tq
    cb = t_lat // ctx_len
    q = pl.BlockSpec((tq, heads * MLA_HEAD), lambda b, h, j: (b * nqt + j, h))
    kl = pl.BlockSpec((seq, heads * MLA_HEAD), lambda b, h, j: (b, h))
    kc = pl.BlockSpec((ctx_len, heads * MLA_HEAD), lambda b, h, j: (cb + b, h))
    vl = pl.BlockSpec((seq, heads * 128), lambda b, h, j: (b, h))
    vc = pl.BlockSpec((ctx_len, heads * 128), lambda b, h, j: (cb + b, h))
    o = pl.BlockSpec((tq, heads * 128), lambda b, h, j: (b * nqt + j, h))
    return q, kl, kc, vl, vc, o


FWD_HEADS = 2
BWD_HEADS = 1


def _mla_fwd(q, k, v, *, t_lat, seq, ctx_len, tq, rider=None):
    nex = t_lat // seq

    def body(q_ref, kl_ref, kc_ref, vl_ref, vc_ref, o_ref, lse_ref):
        for hh in range(FWD_HEADS):
            wide = slice(hh * MLA_HEAD, (hh + 1) * MLA_HEAD)
            cols = slice(hh * 128, (hh + 1) * 128)
            qb = q_ref[:, wide]
            s = _dot_nt(qb, kl_ref[:, wide])
            sc = _dot_nt(qb, kc_ref[:, wide])
            m = jnp.maximum(jnp.max(s, axis=-1, keepdims=True), jnp.max(sc, axis=-1, keepdims=True))
            p = jnp.exp(s - m)
            pc = jnp.exp(sc - m)
            total = jnp.sum(p, axis=-1, keepdims=True) + jnp.sum(pc, axis=-1, keepdims=True)
            o = _dot(p.astype(BF16), vl_ref[:, cols]) + _dot(pc.astype(BF16), vc_ref[:, cols])
            o_ref[:, cols] = (o * (1.0 / total)).astype(BF16)
            lse_ref[:, cols] = jnp.broadcast_to(m + jnp.log(total), (tq, 128))

    qs, kl, kc, vl, vc, os_ = _mla_specs(t_lat, seq, ctx_len, tq, FWD_HEADS)
    return _hosted_call(
        body, (q, k, k, v, v), name="mla_fwd", grid=(nex, HEADS // FWD_HEADS, seq // tq),
        in_specs=[qs, kl, kc, vl, vc], out_specs=[os_, os_],
        out_shape=[jax.ShapeDtypeStruct((t_lat, HEADS * 128), BF16), jax.ShapeDtypeStruct((t_lat, HEADS * 128), F32)],
        sem=("parallel", "parallel", "arbitrary"), rider=rider)


def _mla_bwd(q, k, v, ymla, lse, dmix, *, t_lat, seq, ctx_len, tq, rider=None):
    nex = t_lat // seq
    nqt = seq // tq
    t_ctx = nex * ctx_len
    kb = min(KEY_BLOCK, seq)

    def body(q_ref, kl_ref, kc_ref, vl_ref, vc_ref, o_ref, lse_ref, do_ref, dq_ref, dkl_ref, dkc_ref, dvl_ref, dvc_ref):
        j = pl.program_id(2)

        @pl.when(j == 0)
        def _():
            dkl_ref[...] = jnp.zeros(dkl_ref.shape, F32)
            dkc_ref[...] = jnp.zeros(dkc_ref.shape, F32)
            dvl_ref[...] = jnp.zeros(dvl_ref.shape, F32)
            dvc_ref[...] = jnp.zeros(dvc_ref.shape, F32)

        for hh in range(BWD_HEADS):
            wide = slice(hh * MLA_HEAD, (hh + 1) * MLA_HEAD)
            cols = slice(hh * 128, (hh + 1) * 128)
            qb = q_ref[:, wide]
            dob = do_ref[:, cols]
            delta = jnp.sum(dob.astype(F32) * o_ref[:, cols].astype(F32), axis=-1, keepdims=True)
            lse_row = lse_ref[:, hh * 128:hh * 128 + 1]

            def block(k_ref, v_ref, dk_ref, dv_ref, rows):
                kbl = k_ref[rows, wide]
                vbl = v_ref[rows, cols]
                p = jnp.exp(_dot_nt(qb, kbl) - lse_row)
                ds = (p * (_dot_nt(dob, vbl) - delta)).astype(BF16)
                dk_ref[rows, wide] += _dot_tn(ds, qb)
                dv_ref[rows, cols] += _dot_tn(p.astype(BF16), dob)
                return _dot(ds, kbl)

            dq = block(kc_ref, vc_ref, dkc_ref, dvc_ref, pl.ds(0, ctx_len))
            for i in range(seq // kb):
                dq = dq + block(kl_ref, vl_ref, dkl_ref, dvl_ref, pl.ds(i * kb, kb))
            dq_ref[:, wide] = dq

    g = BWD_HEADS
    qs, kl, kc, vl, vc, os_ = _mla_specs(t_lat, seq, ctx_len, tq, g)
    do_spec = pl.BlockSpec((tq, g * 128), lambda b, h, j: (b * nqt + j, HEADS // g + h))
    return _hosted_call(
        body, (q, k, k, v, v, ymla, lse, dmix), name="mla_bwd", grid=(nex, HEADS // g, nqt),
        in_specs=[qs, kl, kc, vl, vc, os_, os_, do_spec],
        out_specs=[
            qs,
            pl.BlockSpec((seq, g * MLA_HEAD), lambda b, h, j: (b, h)),
            pl.BlockSpec((ctx_len, g * MLA_HEAD), lambda b, h, j: (b, h)),
            pl.BlockSpec((seq, g * 128), lambda b, h, j: (b, h)),
            pl.BlockSpec((ctx_len, g * 128), lambda b, h, j: (b, h)),
        ],
        out_shape=[
            jax.ShapeDtypeStruct((t_lat, HEADS * MLA_HEAD), F32),
            jax.ShapeDtypeStruct((t_lat, HEADS * MLA_HEAD), F32),
            jax.ShapeDtypeStruct((t_ctx, HEADS * MLA_HEAD), F32),
            jax.ShapeDtypeStruct((t_lat, HEADS * 128), F32),
            jax.ShapeDtypeStruct((t_ctx, HEADS * 128), F32),
        ],
        sem=("parallel", "parallel", "arbitrary"), rider=rider)


def _decay_terms(lg, chunk, forward):
    ii = lax.broadcasted_iota(jnp.int32, (chunk, chunk), 0)
    jj = lax.broadcasted_iota(jnp.int32, (chunk, chunk), 1)
    diff = (ii - jj) if forward else (jj - ii)
    dist = jnp.maximum(diff, 0).astype(F32)
    dmat = jnp.where(diff >= 0, jnp.exp(lg * dist), 0.0)
    pos = lax.broadcasted_iota(jnp.int32, (chunk, 1), 0).astype(F32)
    if forward:
        e_q = pos + 1.0
        e_k = (chunk - 1.0) - pos
    else:
        e_q = chunk - pos
        e_k = pos
    wq = jnp.exp(lg * e_q)
    wk = jnp.exp(lg * e_k)
    cd = jnp.exp(jnp.full((1, 1), lg * chunk, F32))
    return dmat, dist, wq, wk, e_q, e_k, cd


def _ctx_weights(lg, ctx_len, forward):
    pos = lax.broadcasted_iota(jnp.int32, (ctx_len, 1), 0).astype(F32)
    e = ((ctx_len - 1.0) - pos) if forward else pos
    return jnp.exp(lg * e), e


def _pair_specs(t_lat, seq, ctx_len):
    cb = t_lat // ctx_len
    qk = pl.BlockSpec((seq, 128), lambda b, p: (b, p))
    v = pl.BlockSpec((seq, 256), lambda b, p: (b, p))
    kc = pl.BlockSpec((ctx_len, 128), lambda b, p: (cb + b, p))
    vc = pl.BlockSpec((ctx_len, 256), lambda b, p: (cb + b, p))
    return qk, v, kc, vc


def _lane_masks():
    lane = lax.broadcasted_iota(jnp.int32, (1, 128), 1)
    return [(lane // RET_DK) == hh for hh in (0, 1)]


def _ret_fwd_pair(rq, rk, rv, pg, lg, g_ret, *, t_lat, seq, ctx_len, chunk, rider=None):
    nex = t_lat // seq
    n_chunk = seq // chunk

    def body(q_ref, k_ref, v_ref, kc_ref, vc_ref, rg_ref, lg_ref, g_ref, y_ref, o_ref):
        pair = pl.program_id(1)
        masks = _lane_masks()
        kcf = kc_ref[...].astype(F32)

        def run(forward):
            terms, s0 = [], []
            for hh in (0, 1):
                lgd = lg_ref[0 if forward else 1, 2 * pair + hh]
                terms.append(_decay_terms(lgd, chunk, forward))
                wc, _ = _ctx_weights(lgd, ctx_len, forward)
                s0.append(_dot_tn((jnp.where(masks[hh], kcf, 0.0) * wc).astype(BF16), vc_ref[:, hh * 128:(hh + 1) * 128]))

            def step(t, states):
                n = t if forward else n_chunk - 1 - t
                sl = pl.ds(pl.multiple_of(n * chunk, chunk), chunk)
                qb = q_ref[sl, :]
                kf_all = k_ref[sl, :].astype(F32)
                new = []
                for hh in (0, 1):
                    dmat, _, wq, wk, _, _, cd = terms[hh]
                    cols = slice(hh * 128, (hh + 1) * 128)
                    qm = jnp.where(masks[hh], qb, jnp.zeros((), BF16))
                    kf = jnp.where(masks[hh], kf_all, 0.0)
                    vb = v_ref[sl, cols]
                    a = _dot_nt(qm, kf.astype(BF16)) * dmat
                    o = _dot(a.astype(BF16), vb) + wq * _dot(qm, states[hh].astype(BF16))
                    if forward:
                        o_ref[sl, cols] = o
                    else:
                        o = o_ref[sl, cols] + o
                        o_ref[sl, cols] = o
                        mu = jnp.mean(o, axis=-1, keepdims=True)
                        oc = o - mu
                        var = jnp.mean(oc * oc, axis=-1, keepdims=True)
                        rg = rg_ref[sl, cols]
                        y_ref[sl, cols] = (oc * lax.rsqrt(var + EPS) * g_ref[:, cols] * (rg / (1.0 + jnp.exp(-rg)))).astype(BF16)
                    new.append(cd * states[hh] + _dot_tn((kf * wk).astype(BF16), vb))
                return tuple(new)

            lax.fori_loop(0, n_chunk, step, tuple(s0))

        run(True)
        run(False)

    qk, v, kc, vc = _pair_specs(t_lat, seq, ctx_len)
    return _hosted_call(
        body, (rq, rk, rv, rk, rv, pg, lg, g_ret), name="ret_fwd", grid=(nex, HEADS // 2),
        in_specs=[qk, qk, v, kc, vc, v, pl.BlockSpec(memory_space=pltpu.SMEM), pl.BlockSpec((1, 256), lambda b, p: (0, p))],
        out_specs=[v, v],
        out_shape=[jax.ShapeDtypeStruct((t_lat, HEADS * RET_DV), BF16), jax.ShapeDtypeStruct((t_lat, HEADS * RET_DV), F32)],
        sem=("parallel", "arbitrary"), rider=rider)


def _ret_bwd_pair(rq, rk, rv, pg, osum, dmix, lg, g_ret, *, t_lat, seq, ctx_len, chunk, rider=None):
    nex = t_lat // seq
    n_chunk = seq // chunk
    t_ctx = nex * ctx_len

    def body(q_ref, k_ref, v_ref, kc_ref, vc_ref, rg_ref, o_ref, dy_ref, lg_ref, g_ref,
             dq_ref, dk_ref, dv_ref, dkc_ref, dvc_ref, drg_ref, st_ref, do_s, s_st):
        pair = pl.program_id(1)
        masks = _lane_masks()
        kcf = kc_ref[...].astype(F32)

        def norm_step(n, dgains):
            sl = pl.ds(pl.multiple_of(n * chunk, chunk), chunk)
            out = []
            for hh in (0, 1):
                cols = slice(hh * 128, (hh + 1) * 128)
                gain = g_ref[:, cols]
                o = o_ref[sl, cols]
                mu = jnp.mean(o, axis=-1, keepdims=True)
                oc = o - mu
                rstd = lax.rsqrt(jnp.mean(oc * oc, axis=-1, keepdims=True) + EPS)
                ohat = oc * rstd
                rg = rg_ref[sl, cols]
                sg = 1.0 / (1.0 + jnp.exp(-rg))
                dy = dy_ref[sl, cols].astype(F32)
                don = dy * (rg * sg)
                drg_ref[sl, cols] = dy * (ohat * gain) * (sg * (1.0 + rg * (1.0 - sg)))
                dohat = don * gain
                do_s[sl, cols] = rstd * (dohat - jnp.mean(dohat, axis=-1, keepdims=True)
                                         - ohat * jnp.mean(dohat * ohat, axis=-1, keepdims=True))
                out.append(dgains[hh] + jnp.sum(don * ohat, axis=0, keepdims=True))
            return tuple(out)

        zero_row = jnp.zeros((1, 128), F32)
        dgains = lax.fori_loop(0, n_chunk, norm_step, (zero_row, zero_row))
        dq_ref[...] = jnp.zeros(dq_ref.shape, F32)
        dk_ref[...] = jnp.zeros(dk_ref.shape, F32)
        dv_ref[...] = jnp.zeros(dv_ref.shape, F32)

        chains = [(forward, hh) for forward in (True, False) for hh in (0, 1)]
        terms, ctxw, s0 = [], [], []
        for forward, hh in chains:
            lgd = lg_ref[0 if forward else 1, 2 * pair + hh]
            terms.append(_decay_terms(lgd, chunk, forward))
            ctxw.append(_ctx_weights(lgd, ctx_len, forward))
            s0.append(_dot_tn((jnp.where(masks[hh], kcf, 0.0) * ctxw[-1][0]).astype(BF16), vc_ref[:, hh * 128:(hh + 1) * 128]))

        def chunk_at(t, ascending):
            n = t if ascending else n_chunk - 1 - t
            return n, pl.ds(pl.multiple_of(n * chunk, chunk), chunk)

        def state_step(t, states):
            new = []
            for c, (forward, hh) in enumerate(chains):
                n, sl = chunk_at(t, forward)
                wk, cd = terms[c][3], terms[c][6]
                s_st[c, n] = states[c]
                kf = jnp.where(masks[hh], k_ref[sl, :].astype(F32), 0.0)
                new.append(cd * states[c] + _dot_tn((kf * wk).astype(BF16), v_ref[sl, hh * 128:(hh + 1) * 128]))
            return tuple(new)

        lax.fori_loop(0, n_chunk, state_step, tuple(s0))

        def grad_step(t, carry):
            out = []
            for forward in (True, False):
                n, sl = chunk_at(t, not forward)
                qb = q_ref[sl, :]
                kf_all = k_ref[sl, :].astype(F32)
                dq_sum = jnp.zeros((chunk, 128), F32)
                dk_sum = jnp.zeros((chunk, 128), F32)
                for hh in (0, 1):
                    c = chains.index((forward, hh))
                    g_next, dlg = carry[c]
                    dmat, dist, wq, wk, e_q, e_k, cd = terms[c]
                    cols = slice(hh * 128, (hh + 1) * 128)
                    qm = jnp.where(masks[hh], qb, jnp.zeros((), BF16))
                    kf = jnp.where(masks[hh], kf_all, 0.0)
                    kb = kf.astype(BF16)
                    vb = v_ref[sl, cols]
                    do = do_s[sl, cols]
                    dob = do.astype(BF16)
                    s_n = s_st[c, n]
                    s_nb = s_n.astype(BF16)
                    gb = g_next.astype(BF16)
                    dk_cross = wk * _dot_nt(vb, gb)
                    dv_cross = _dot((kf * wk).astype(BF16), gb)
                    a = _dot_nt(qm, kb) * dmat
                    da_raw = _dot_nt(dob, vb)
                    dab = (da_raw * dmat).astype(BF16)
                    o_cross = wq * _dot(qm, s_nb)
                    dq_sum = dq_sum + _dot(dab, kb) + wq * _dot_nt(dob, s_nb)
                    dk_sum = dk_sum + _dot_tn(dab, qm) + dk_cross
                    dv_ref[sl, cols] += _dot_tn(a.astype(BF16), dob) + dv_cross
                    dlg = (dlg + chunk * cd * jnp.sum(g_next * s_n, keepdims=True)
                           + jnp.sum(e_k * jnp.sum(kf * dk_cross, axis=-1, keepdims=True), keepdims=True)
                           + jnp.sum(dist * a * da_raw, keepdims=True)
                           + jnp.sum(e_q * jnp.sum(o_cross * do, axis=-1, keepdims=True), keepdims=True))
                    out.append((cd * g_next + _dot_tn((qm.astype(F32) * wq).astype(BF16), dob), dlg))
                dq_ref[sl, :] += dq_sum
                dk_ref[sl, :] += dk_sum
            return tuple(out)

        zero = (jnp.zeros((128, 128), F32), jnp.zeros((1, 1), F32))
        res = lax.fori_loop(0, n_chunk, grad_step, (zero,) * len(chains))
        dkc_sum = jnp.zeros((ctx_len, 128), F32)
        dvc = [jnp.zeros((ctx_len, 128), F32)] * 2
        dlgs = []
        for c, (forward, hh) in enumerate(chains):
            ds0, dlg = res[c]
            wc, e_c = ctxw[c]
            kcm = jnp.where(masks[hh], kcf, 0.0)
            ds0b = ds0.astype(BF16)
            dkc_part = wc * _dot_nt(vc_ref[:, hh * 128:(hh + 1) * 128], ds0b)
            dkc_sum = dkc_sum + dkc_part
            dvc[hh] = dvc[hh] + _dot((kcm * wc).astype(BF16), ds0b)
            dlgs.append(dlg + jnp.sum(e_c * jnp.sum(kcm * dkc_part, axis=-1, keepdims=True), keepdims=True))
        dkc_ref[...] = dkc_sum
        for hh in (0, 1):
            cols = slice(hh * 128, (hh + 1) * 128)
            dvc_ref[:, cols] = dvc[hh]
            st_ref[0, :, cols] = jnp.concatenate([
                dgains[hh], jnp.broadcast_to(dlgs[hh], (1, 128)), jnp.broadcast_to(dlgs[2 + hh], (1, 128)),
                jnp.zeros((5, 128), F32)], axis=0)

    qk, v, kc, vc = _pair_specs(t_lat, seq, ctx_len)
    return _hosted_call(
        body, (rq, rk, rv, rk, rv, pg, osum, dmix, lg, g_ret), name="ret_bwd", grid=(nex, HEADS // 2),
        in_specs=[qk, qk, v, kc, vc, v, v, v, pl.BlockSpec(memory_space=pltpu.SMEM),
                  pl.BlockSpec((1, 256), lambda b, p: (0, p))],
        out_specs=[
            qk, qk, v,
            pl.BlockSpec((ctx_len, 128), lambda b, p: (b, p)),
            pl.BlockSpec((ctx_len, 256), lambda b, p: (b, p)),
            v,
            pl.BlockSpec((1, 8, 256), lambda b, p: (b, 0, p)),
        ],
        out_shape=[
            jax.ShapeDtypeStruct((t_lat, 256), F32), jax.ShapeDtypeStruct((t_lat, 256), F32),
            jax.ShapeDtypeStruct((t_lat, 512), F32), jax.ShapeDtypeStruct((t_ctx, 256), F32),
            jax.ShapeDtypeStruct((t_ctx, 512), F32), jax.ShapeDtypeStruct((t_lat, 512), F32),
            jax.ShapeDtypeStruct((nex, 8, 512), F32),
        ],
        scratch_shapes=[pltpu.VMEM((seq, 256), F32), pltpu.VMEM((4, n_chunk, 128, 128), F32)],
        sem=("parallel", "arbitrary"), rider=rider)


def _matmul_tn(a, b, *, bm, bn, bk, chip_major, name, out_dtype=F32, rider=None):
    tk, m = a.shape
    n = b.shape[1]
    slab = n // N_CHIPS
    per_block = bn // slab if chip_major else 1
    bk = max(c for c in range(LANES, min(bk, tk) + 1, LANES) if tk % c == 0)
    nk = tk // bk
    blk = (per_block, bm, slab) if chip_major else (bm, bn)

    def body(a_ref, b_ref, o_ref, acc_ref):
        k = pl.program_id(2)
        if chip_major:
            parts = [_dot_tn(a_ref[...], b_ref[:, s * slab:(s + 1) * slab]) for s in range(per_block)]
        else:
            parts = [_dot_tn(a_ref[...], b_ref[...])]

        @pl.when(k == 0)
        def _():
            for s, part in enumerate(parts):
                if chip_major:
                    acc_ref[s] = part
                else:
                    acc_ref[...] = part

        @pl.when(k > 0)
        def _():
            for s, part in enumerate(parts):
                if chip_major:
                    acc_ref[s] += part
                else:
                    acc_ref[...] += part

        @pl.when(k == nk - 1)
        def _():
            o_ref[...] = acc_ref[...].astype(out_dtype)

    if chip_major:
        out_spec = pl.BlockSpec(blk, lambda i, j, k: (j, i, 0))
        out_shape = jax.ShapeDtypeStruct((N_CHIPS, m, slab), out_dtype)
    else:
        out_spec = pl.BlockSpec(blk, lambda i, j, k: (i, j))
        out_shape = jax.ShapeDtypeStruct((m, n), out_dtype)
    (out,), carried = _hosted_call(
        body, (a, b), name=name, grid=(m // bm, n // bn, nk),
        in_specs=[pl.BlockSpec((bk, bm), lambda i, j, k: (k, i)), pl.BlockSpec((bk, bn), lambda i, j, k: (k, j))],
        out_specs=[out_spec], out_shape=[out_shape], scratch_shapes=[pltpu.VMEM(blk, F32)],
        sem=("parallel", "parallel", "arbitrary"), rider=rider)
    return out if rider is None else (out, carried)


_LATE = ("w_out", "w_ff1", "w_ff2")
_EARLY = ("w_in", "w_uq", "w_ukv")


def _local_step(x, ctx, tgt, modv, lg, g_attn, g_ffn, g_fin, g_ret, g_q, g_kv, w_in, w_uq, w_ukv, late, place=None,
                *, tm=256, tq=256, chunk=256):
    nex, seq, d = x.shape
    ctx_len = ctx.shape[1]
    t_lat = nex * seq
    tm = min(tm, seq)
    x2 = x.reshape(t_lat, d)
    ctx2 = ctx.reshape(nex * ctx_len, d)
    tgt2 = tgt.reshape(t_lat, d)
    tm_fwd = min(2 * tm, seq)
    cos_t, sin_t = _rope_tables(seq, tm)
    dims = dict(t_lat=t_lat, seq=seq, ctx_len=ctx_len)
    alone = place is None

    (hb, pg, rq, rk, rv, nq, nkv, q, k, v), crossed = _pre_fwd(
        x2, ctx2, modv, g_attn, w_in, g_q, g_kv, w_uq, w_ukv, *_rope_tables(seq, tm_fwd), seq=seq, tm=tm_fwd,
        rider=None if alone else _gather_ici_rider([late[2]]))
    (yret, osum), got_ff2 = _ret_fwd_pair(rq, rk, rv, pg, lg, g_ret, chunk=chunk, **dims,
                                          rider=None if alone else _gather_d2d_rider(crossed))
    (ymla, lse), got_rest = _mla_fwd(q, k, v, tq=tq, **dims,
                                     rider=None if alone else _gather_rider([late[0], late[1]], staged=True))
    w_out, w_ff1, w_ff2 = late if alone else got_rest + got_ff2
    mix, act, du, h2, df, dmo, dmix, dxm, st_post = _post(yret, ymla, x2, tgt2, modv, g_ffn, g_fin, w_out.reshape(d, d),
                                                         w_ff1, w_ff2.reshape(D_FF, d), seq=seq, tm=min(tm, 256))
    kw = dict(bm=1024, bn=1024, bk=2048, out_dtype=BF16)
    g_ff2 = _matmul_tn(act, df, chip_major=False, name="gw_ff2", **kw).reshape(N_CHIPS, D_FF // N_CHIPS, d)
    if alone:
        g_ff1 = _matmul_tn(h2, du, chip_major=True, name="gw_ff1", **kw)
        g_out = _matmul_tn(mix, dmo, chip_major=False, name="gw_out", **kw).reshape(N_CHIPS, d // N_CHIPS, d)
        (dq_m, dkl, dkc, dvl, dvc), _ = _mla_bwd(q, k, v, ymla, lse, dmix, tq=tq, **dims)
        (drq, drk, drv, dkc_r, dvc_r, drg, st_ret), _ = _ret_bwd_pair(rq, rk, rv, pg, osum, dmix, lg, g_ret, chunk=chunk,
                                                                      **dims)
        late_out = [g_out, g_ff1, g_ff2]
    else:
        core, slot = place
        g_ff1, x_ff2 = _matmul_tn(h2, du, chip_major=True, name="gw_ff1", rider=_exchange_rider([g_ff2]), **kw)
        g_out, x_ff1 = _matmul_tn(mix, dmo, chip_major=False, name="gw_out", rider=_exchange_rider([g_ff1]), **kw)
        g_out = g_out.reshape(N_CHIPS, d // N_CHIPS, d)
        p_ff2 = _add_half(g_ff2, x_ff2[0], core, "add_half_w_ff2")
        p_ff1 = _add_half(g_ff1, x_ff1[0], core, "add_half_w_ff1")
        (dq_m, dkl, dkc, dvl, dvc), (l_ff2, l_ff1, x_out) = _mla_bwd(
            q, k, v, ymla, lse, dmix, tq=min(seq, 512), **dims,
            rider=_merge_riders(_scatter_rider([p_ff2, p_ff1]), _exchange_rider([g_out])))
        p_out = _add_half(g_out, x_out, core, "add_half_w_out")
        m_ff2 = _sum_chips(p_ff2, l_ff2, slot, "sum_chips_w_ff2")
        m_ff1 = _sum_chips(p_ff1, l_ff1, slot, "sum_chips_w_ff1")
        (drq, drk, drv, dkc_r, dvc_r, drg, st_ret), (l_out,) = _ret_bwd_pair(
            rq, rk, rv, pg, osum, dmix, lg, g_ret, chunk=chunk, **dims, rider=_scatter_rider([p_out]))
        late_out = [_sum_chips(p_out, l_out, slot, "sum_chips_w_out"), m_ff1, m_ff2]
    (dpb, dqf, dkvf, gx, st_pre), _ = _pre_bwd(
        x2, ctx2, modv, g_attn, pg, drq, drk, dkc_r, drv, dvc_r, drg, dq_m, dkl, dkc, dvl, dvc, dxm, w_in, g_q, g_kv,
        w_uq, w_ukv, cos_t, sin_t, seq=seq, tm=tm)
    g_early = [
        _matmul_tn(dpb, hb, bm=IN_PAD // 2, bn=d, bk=1536, chip_major=False, name="gw_in"),
        _matmul_tn(dqf, nq, bm=HEADS * MLA_HEAD, bn=Q_LORA, bk=1536, chip_major=False, name="gw_uq"),
        _matmul_tn(nkv, dkvf, bm=KV_LORA, bn=HEADS * 256, bk=1536, chip_major=True, name="gw_ukv"),
    ]
    return gx.reshape(nex, seq, d), g_early, late_out, st_post, st_ret, st_pre


_ANY = pl.BlockSpec(memory_space=pl.ANY)
_VMEM = pl.BlockSpec(memory_space=pltpu.VMEM)
_OFFSETS = tuple((dx, dy, dc) for dx in (0, 1) for dy in (0, 1) for dc in (0, 1))[1:]
_CHIP_OFFSETS = ((1, 0), (0, 1), (1, 1))


def _place():
    return lax.axis_index("x"), lax.axis_index("y"), lax.axis_index("c")


def _flip(v, d):
    return 1 - v if d else v


def _gather8_rider(a, in_vmem=True):
    def copies(a_ref, o_ref, send, recv):
        x, y, z = _place()
        me = 4 * x + 2 * y + z
        out = []
        for k, (dx, dy, dc) in enumerate(_OFFSETS):
            peer = (_flip(x, dx), _flip(y, dy), _flip(z, dc))
            landing = o_ref.at[4 * peer[0] + 2 * peer[1] + peer[2]]
            out.append((
                pltpu.make_async_remote_copy(src_ref=a_ref, dst_ref=o_ref.at[me], send_sem=send.at[k],
                                             recv_sem=recv.at[k], device_id=peer, device_id_type=MESH),
                pltpu.make_async_remote_copy(src_ref=a_ref, dst_ref=landing, send_sem=send.at[k],
                                             recv_sem=recv.at[k], device_id=peer, device_id_type=MESH)))
        return me, out

    def start(ins, outs, sems):
        me, cps = copies(ins[0], outs[0], sems[0], sems[1])
        pltpu.make_async_copy(ins[0], outs[0].at[me], sems[2]).start()
        for out_cp, _ in cps:
            out_cp.start()

    def finish(ins, outs, sems):
        me, cps = copies(ins[0], outs[0], sems[0], sems[1])
        for out_cp, in_cp in cps:
            in_cp.wait_recv()
            out_cp.wait_send()
        pltpu.make_async_copy(ins[0], outs[0].at[me], sems[2]).wait()

    spec = [_VMEM] if in_vmem else [_ANY]
    return _Rider([a], [jax.ShapeDtypeStruct((N_DEV,) + a.shape, a.dtype)],
                  [pltpu.SemaphoreType.DMA((7,)), pltpu.SemaphoreType.DMA((7,)), pltpu.SemaphoreType.DMA],
                  start, finish, in_specs=spec, out_specs=spec)


def _merge_riders(*riders):
    ins, outs, sems, in_specs, out_specs, aliases, cuts = [], [], [], [], [], {}, []
    for r in riders:
        cuts.append((len(ins), len(outs), len(sems)))
        aliases.update({len(ins) + i: len(outs) + j for i, j in r.aliases.items()})
        ins += r.ins
        outs += r.out_shapes
        sems += r.sems
        in_specs += r.in_specs
        out_specs += r.out_specs

    def part(r, cut, r_ins, r_outs, r_sems):
        return (r_ins[cut[0]:cut[0] + len(r.ins)], r_outs[cut[1]:cut[1] + len(r.out_shapes)],
                r_sems[cut[2]:cut[2] + len(r.sems)])

    def start(r_ins, r_outs, r_sems):
        for r, cut in zip(riders, cuts):
            r.start(*part(r, cut, r_ins, r_outs, r_sems))

    def finish(r_ins, r_outs, r_sems):
        for r, cut in zip(riders, cuts):
            r.finish(*part(r, cut, r_ins, r_outs, r_sems))

    def middle(r_ins, r_outs, r_sems):
        for r, cut in zip(riders, cuts):
            if r.middle is not None:
                r.middle(*part(r, cut, r_ins, r_outs, r_sems))

    return _Rider(ins, outs, sems, start, finish, aliases=aliases, in_specs=in_specs, out_specs=out_specs,
                  middle=middle if any(r.middle is not None for r in riders) else None)


def _allgather8(a, name):
    return _run_rider(_gather8_rider(a), name)[0]


BF16_TILE_ROWS = 16


def _half(o, slot, which):
    r2 = o.shape[1] // 2
    if r2 % BF16_TILE_ROWS == 0:
        return o.at[slot, pl.ds(which * r2, r2)]
    c2 = o.shape[2] // 2
    assert c2 % LANES == 0
    return o.at[slot, :, pl.ds(which * c2, c2)]


def _gather_send(o_refs, send, recv):
    x, y, z = _place()
    chip = 2 * x + y
    for a, o in enumerate(o_refs):
        r2 = o.shape[1] // 2
        mine = _half(o, chip, z)
        for k, (dx, dy) in enumerate(_CHIP_OFFSETS):
            pltpu.make_async_remote_copy(
                src_ref=mine, dst_ref=mine, send_sem=send.at[a, k], recv_sem=recv.at[a, k],
                device_id=(_flip(x, dx), _flip(y, dy), z), device_id_type=MESH).start()


def _gather_landed(o_refs, send, recv, then=None):
    x, y, z = _place()
    chip = 2 * x + y
    for a, o in enumerate(o_refs):
        for k, (dx, dy) in enumerate(_CHIP_OFFSETS):
            landed = _half(o, 2 * _flip(x, dx) + _flip(y, dy), z)
            pltpu.make_async_remote_copy(
                src_ref=landed, dst_ref=landed, send_sem=send.at[a, k], recv_sem=recv.at[a, k],
                device_id=(_flip(x, dx), _flip(y, dy), z), device_id_type=MESH).wait_recv()
            if then is not None:
                then(a, k, landed)
    for a, o in enumerate(o_refs):
        mine = _half(o, chip, z)
        for k, (dx, dy) in enumerate(_CHIP_OFFSETS):
            pltpu.make_async_remote_copy(
                src_ref=mine, dst_ref=mine, send_sem=send.at[a, k], recv_sem=recv.at[a, k],
                device_id=(_flip(x, dx), _flip(y, dy), z), device_id_type=MESH).wait_send()


def _pass_on(o_refs, fsend, frecv, a, k, landed):
    x, y, z = _place()
    pltpu.make_async_remote_copy(
        src_ref=landed, dst_ref=landed, send_sem=fsend.at[a, k], recv_sem=frecv.at[a, k],
        device_id=(x, y, 1 - z), device_id_type=MESH).start()


def _passed_on(o_refs, fsend, frecv):
    x, y, z = _place()
    for a, o in enumerate(o_refs):
        for k, (dx, dy) in enumerate(_CHIP_OFFSETS):
            other = 2 * _flip(x, dx) + _flip(y, dy)
            got = _half(o, other, 1 - z)
            gave = _half(o, other, z)
            pltpu.make_async_remote_copy(
                src_ref=got, dst_ref=got, send_sem=fsend.at[a, k], recv_sem=frecv.at[a, k],
                device_id=(x, y, 1 - z), device_id_type=MESH).wait_recv()
            pltpu.make_async_remote_copy(
                src_ref=gave, dst_ref=gave, send_sem=fsend.at[a, k], recv_sem=frecv.at[a, k],
                device_id=(x, y, 1 - z), device_id_type=MESH).wait_send()


def _gather_finish(o_refs, send, recv, fsend, frecv):
    _gather_landed(o_refs, send, recv, functools.partial(_pass_on, o_refs, fsend, frecv))
    _passed_on(o_refs, fsend, frecv)


class _Rider:
    def __init__(self, ins, out_shapes, sems, start, finish, aliases=None, in_specs=None, out_specs=None, middle=None):
        self.ins, self.out_shapes, self.sems = list(ins), list(out_shapes), list(sems)
        self.start, self.finish, self.aliases = start, finish, dict(aliases or {})
        self.middle = middle
        self.in_specs = list(in_specs) if in_specs else [_ANY] * len(self.ins)
        self.out_specs = list(out_specs) if out_specs else [_ANY] * len(self.out_shapes)


def _run_rider(rider, name):
    r_in, r_out = len(rider.ins), len(rider.out_shapes)

    def body(*refs):
        ins, outs, sems = refs[:r_in], refs[r_in:r_in + r_out], refs[r_in + r_out:]
        rider.start(ins, outs, sems)
        if rider.middle is not None:
            rider.middle(ins, outs, sems)
        rider.finish(ins, outs, sems)

    return pl.pallas_call(
        body, name=name, in_specs=rider.in_specs, out_specs=rider.out_specs, out_shape=rider.out_shapes,
        input_output_aliases=rider.aliases, scratch_shapes=rider.sems,
    )(*rider.ins)


def _hosted_call(body, args, *, name, grid, in_specs, out_specs, out_shape, scratch_shapes=(), sem, rider=None,
                 prefetch=()):
    scratch_shapes = list(scratch_shapes)
    n_pf, n_in, n_out, n_sc = len(prefetch), len(in_specs), len(out_specs), len(scratch_shapes)
    r_in, r_out = (len(rider.ins), len(rider.out_shapes)) if rider else (0, 0)
    last = tuple(g - 1 for g in grid)

    def hosted(*refs):
        p = 0
        parts = []
        for cnt in (n_pf, n_in, r_in, n_out, r_out, n_sc):
            parts.append(refs[p:p + cnt])
            p += cnt
        pf, ins, r_ins, outs, r_outs, scratch = parts
        sems = refs[p:]
        ids = [pl.program_id(a) for a in range(len(grid))]
        is_first = functools.reduce(jnp.logical_and, [i == 0 for i in ids])
        is_last = functools.reduce(jnp.logical_and, [i == e for i, e in zip(ids, last)])

        @pl.when(is_first)
        def _():
            rider.start(r_ins, r_outs, sems)

        if rider.middle is not None:
            linear = functools.reduce(lambda acc, ig: acc * ig[1] + ig[0], zip(ids, grid), 0)

            @pl.when(linear == math.prod(grid) * 3 // 4)
            def _():
                rider.middle(r_ins, r_outs, sems)

        body(*pf, *ins, *outs, *scratch)

        @pl.when(is_last)
        def _():
            rider.finish(r_ins, r_outs, sems)

    if rider is None:
        kern, all_in, all_out, shapes, scratch, aliases, extra = body, list(in_specs), list(out_specs), list(out_shape), \
            scratch_shapes, {}, []
    else:
        kern, all_in, all_out = hosted, list(in_specs) + rider.in_specs, list(out_specs) + rider.out_specs
        shapes, scratch, extra = list(out_shape) + rider.out_shapes, scratch_shapes + rider.sems, rider.ins
        aliases = {n_pf + n_in + i: n_out + j for i, j in rider.aliases.items()}
        sem = ("arbitrary",) * len(grid)
    if prefetch:
        spec = dict(grid_spec=pltpu.PrefetchScalarGridSpec(
            num_scalar_prefetch=n_pf, grid=grid, in_specs=all_in, out_specs=all_out, scratch_shapes=scratch))
    else:
        spec = dict(grid=grid, in_specs=all_in, out_specs=all_out, scratch_shapes=scratch)
    res = pl.pallas_call(kern, name=name, out_shape=shapes, input_output_aliases=aliases,
                         compiler_params=_params(sem, VMEM_LIMIT), **spec)(*prefetch, *args, *extra)
    return list(res[:n_out]), list(res[n_out:])


def _gather_rider(ws, staged=False):
    n = len(ws)
    shapes = [jax.ShapeDtypeStruct(w.shape, w.dtype) for w in ws]
    sems = [pltpu.SemaphoreType.DMA((n, 3))] * 4
    aliases = {a: a for a in range(n)}

    def start(ins, outs, s):
        _gather_send(outs, s[0], s[1])

    if not staged:
        return _Rider(ws, shapes, sems, start, lambda ins, outs, s: _gather_finish(outs, *s), aliases=aliases)
    return _Rider(
        ws, shapes, sems, start, lambda ins, outs, s: _passed_on(outs, s[2], s[3]), aliases=aliases,
        middle=lambda ins, outs, s: _gather_landed(outs, s[0], s[1], functools.partial(_pass_on, outs, s[2], s[3])))


def _gather_ici_rider(ws):
    n = len(ws)
    return _Rider(
        ws, [jax.ShapeDtypeStruct(w.shape, w.dtype) for w in ws], [pltpu.SemaphoreType.DMA((n, 3))] * 2,
        lambda ins, outs, sems: _gather_send(outs, sems[0], sems[1]),
        lambda ins, outs, sems: _gather_landed(outs, sems[0], sems[1]),
        aliases={a: a for a in range(n)})


def _gather_d2d_rider(ws):
    n = len(ws)

    def start(ins, outs, sems):
        x, y, z = _place()
        for a, o in enumerate(outs):
            for k, (dx, dy) in enumerate(_CHIP_OFFSETS):
                _pass_on(outs, sems[0], sems[1], a, k, _half(o, 2 * _flip(x, dx) + _flip(y, dy), z))

    return _Rider(
        ws, [jax.ShapeDtypeStruct(w.shape, w.dtype) for w in ws], [pltpu.SemaphoreType.DMA((n, 3))] * 2,
        start, lambda ins, outs, sems: _passed_on(outs, sems[0], sems[1]), aliases={a: a for a in range(n)})


def _copies_rider(ins, out_shapes, sem_shape, make):
    def start(r_ins, r_outs, sems):
        for cp in make(r_ins, r_outs, sems[0], sems[1]):
            cp.start()

    def finish(r_ins, r_outs, sems):
        for cp in make(r_ins, r_outs, sems[0], sems[1]):
            cp.wait()

    return _Rider(ins, out_shapes, [pltpu.SemaphoreType.DMA(sem_shape)] * 2, start, finish)


def _exchange_rider(gs):
    def make(g_refs, r_refs, send, recv):
        x, y, z = _place()
        return [pltpu.make_async_remote_copy(
            src_ref=g.at[:, pl.ds((1 - z) * (g.shape[1] // 2), g.shape[1] // 2)], dst_ref=r, send_sem=send.at[a],
            recv_sem=recv.at[a], device_id=(x, y, 1 - z), device_id_type=MESH)
            for a, (g, r) in enumerate(zip(g_refs, r_refs))]

    shapes = [jax.ShapeDtypeStruct((g.shape[0], g.shape[1] // 2, g.shape[2]), g.dtype) for g in gs]
    return _copies_rider(gs, shapes, (len(gs),), make)


def _add_half(g, recv, core, name):
    s, r, c = g.shape
    r2 = r // 2
    rb = r2
    for cand in (256, 128, 64):
        if r2 % cand == 0:
            rb = cand
            break
    g4 = g.reshape(s, 2, r2, c)

    def body(core_ref, g_ref, r_ref, o_ref):
        o_ref[...] = (g_ref[...].astype(F32) + r_ref[...].astype(F32)).astype(BF16)

    return pl.pallas_call(
        body, name=name,
        grid_spec=pltpu.PrefetchScalarGridSpec(
            num_scalar_prefetch=1, grid=(s, r2 // rb),
            in_specs=[pl.BlockSpec((None, None, rb, c), lambda i, j, cr: (i, cr[0], j, 0)),
                      pl.BlockSpec((None, rb, c), lambda i, j, cr: (i, j, 0))],
            out_specs=pl.BlockSpec((None, rb, c), lambda i, j, cr: (i, j, 0))),
        out_shape=jax.ShapeDtypeStruct((s, r2, c), BF16),
        compiler_params=_params(("parallel", "parallel")),
    )(core, g4, recv)


def _scatter_rider(ps):
    def make(p_refs, o_refs, send, recv):
        x, y, z = _place()
        copies = []
        for a, (p, o) in enumerate(zip(p_refs, o_refs)):
            for k, (dx, dy) in enumerate(_CHIP_OFFSETS):
                other = 2 * _flip(x, dx) + _flip(y, dy)
                copies.append(pltpu.make_async_remote_copy(
                    src_ref=p.at[other], dst_ref=o.at[k], send_sem=send.at[a, k], recv_sem=recv.at[a, k],
                    device_id=(_flip(x, dx), _flip(y, dy), z), device_id_type=MESH))
        return copies

    shapes = [jax.ShapeDtypeStruct((3,) + p.shape[1:], p.dtype) for p in ps]
    return _copies_rider(ps, shapes, (len(ps), 3), make)


def _sum_chips(p, landed, chip, name):
    _, r2, c = p.shape
    rb = r2
    for cand in (256, 128, 64):
        if r2 % cand == 0:
            rb = cand
            break

    def body(s_ref, p_ref, l_ref, o_ref):
        acc = p_ref[...].astype(F32)
        for k in range(3):
            acc = acc + l_ref[k].astype(F32)
        o_ref[...] = acc

    return pl.pallas_call(
        body, name=name,
        grid_spec=pltpu.PrefetchScalarGridSpec(
            num_scalar_prefetch=1, grid=(r2 // rb,),
            in_specs=[pl.BlockSpec((None, rb, c), lambda i, s: (s[0], i, 0)),
                      pl.BlockSpec((3, rb, c), lambda i, s: (0, i, 0))],
            out_specs=pl.BlockSpec((rb, c), lambda i, s: (i, 0))),
        out_shape=jax.ShapeDtypeStruct((r2, c), F32),
        compiler_params=_params(("parallel",)),
    )(chip, p, landed)


def _swap_rider(hs):
    def make(h_refs, o_refs, send, recv):
        x, y, z = _place()
        return [pltpu.make_async_remote_copy(
            src_ref=h, dst_ref=o, send_sem=send.at[a], recv_sem=recv.at[a], device_id=(x, y, 1 - z),
            device_id_type=MESH) for a, (h, o) in enumerate(zip(h_refs, o_refs))]

    return _copies_rider(hs, [jax.ShapeDtypeStruct(h.shape, h.dtype) for h in hs], (len(hs),), make)


def _reduce_scatter_vmem(gs, rows, rider, name):
    n = len(gs)
    r_in, r_out = len(rider.ins), len(rider.out_shapes)
    halves = [(r // 2, g.shape[-1]) for g, (r, _) in zip(gs, rows)]

    def body(*refs):
        p = 0
        parts = []
        for cnt in (n, r_in, n, n, r_out, n, n, n, 6):
            parts.append(refs[p:p + cnt])
            p += cnt
        g_refs, r_ins, mine, theirs, r_outs, recv, part, land, sems = parts
        r_sems = refs[p:]
        xs, xr, ss, sr, ws, wr = sems
        x, y, z = _place()
        chip = 2 * x + y
        sib = (x, y, 1 - z)
        rider.start(r_ins, r_outs, r_sems)

        def half_of(a, s, which):
            r2 = halves[a][0]
            if len(g_refs[a].shape) == 3:
                return g_refs[a].at[s, pl.ds(pl.multiple_of(which * r2, 8), r2)]
            return g_refs[a].at[pl.ds(pl.multiple_of(s * rows[a][1] + which * r2, 8), r2)]

        exchange = [pltpu.make_async_remote_copy(
            src_ref=half_of(a, s, 1 - z), dst_ref=recv[a].at[s], send_sem=xs.at[a, s], recv_sem=xr.at[a, s],
            device_id=sib, device_id_type=MESH) for a in range(n) for s in range(N_CHIPS)]
        for cp in exchange:
            cp.start()
        for cp in exchange:
            cp.wait()
        for a in range(n):
            for s in range(N_CHIPS):
                part[a][s] = (half_of(a, s, z)[...] + recv[a][s]).astype(BF16)
        scatter = []
        for a in range(n):
            for k, (dx, dy) in enumerate(_CHIP_OFFSETS):
                other = 2 * _flip(x, dx) + _flip(y, dy)
                scatter.append(pltpu.make_async_remote_copy(
                    src_ref=part[a].at[other], dst_ref=land[a].at[k], send_sem=ss.at[a, k], recv_sem=sr.at[a, k],
                    device_id=(_flip(x, dx), _flip(y, dy), z), device_id_type=MESH))
        for cp in scatter:
            cp.start()
        for cp in scatter:
            cp.wait()
        for a in range(n):
            acc = part[a][chip].astype(F32)
            for k in range(3):
                acc = acc + land[a][k].astype(F32)
            mine[a][...] = acc
        swap = [pltpu.make_async_remote_copy(
            src_ref=mine[a], dst_ref=theirs[a], send_sem=ws.at[a], recv_sem=wr.at[a], device_id=sib,
            device_id_type=MESH) for a in range(n)]
        for cp in swap:
            cp.start()
        for cp in swap:
            cp.wait()
        rider.finish(r_ins, r_outs, r_sems)

    half_shapes = [jax.ShapeDtypeStruct(h, F32) for h in halves]
    res = pl.pallas_call(
        body, name=name, in_specs=[_VMEM] * n + rider.in_specs, out_specs=[_VMEM] * (2 * n) + rider.out_specs,
        out_shape=half_shapes + half_shapes + rider.out_shapes,
        scratch_shapes=[pltpu.VMEM((N_CHIPS,) + h, F32) for h in halves] + [pltpu.VMEM((N_CHIPS,) + h, BF16) for h in halves]
        + [pltpu.VMEM((3,) + h, BF16) for h in halves]
        + [pltpu.SemaphoreType.DMA((n, N_CHIPS))] * 2 + [pltpu.SemaphoreType.DMA((n, 3))] * 2
        + [pltpu.SemaphoreType.DMA((n,))] * 2 + rider.sems,
        input_output_aliases={n + i: 2 * n + j for i, j in rider.aliases.items()},
        compiler_params=_params(None, VMEM_LIMIT),
    )(*gs, *rider.ins)
    return list(res[:n]), list(res[n:2 * n]), list(res[2 * n:])


SMALL_ROWS = 32
PACK_ROWS = 16


def _pack_small(st_post, st_ret, st_pre):
    d = st_post.shape[2]

    def body(po_ref, re_ref, pr_ref, o_ref):
        o_ref[...] = jnp.zeros(o_ref.shape, F32)
        o_ref[0:1, :] = pr_ref[0, 2:3, :] + pr_ref[1, 2:3, :] + pr_ref[2, 2:3, :]
        o_ref[1:2, :] = po_ref[0, 4:5, :] + po_ref[1, 4:5, :]
        o_ref[2:3, :] = po_ref[0, 5:6, :] + po_ref[1, 5:6, :]
        o_ref[3:4, 0:512] = re_ref[0, 0:1, :] + re_ref[1, 0:1, :]
        o_ref[4:5, :] = pr_ref[0, 3:4, :] + pr_ref[1, 3:4, :] + pr_ref[2, 3:4, :]
        o_ref[5:6, :] = pr_ref[0, 4:5, :] + pr_ref[1, 4:5, :] + pr_ref[2, 4:5, :]
        lane = lax.broadcasted_iota(jnp.int32, (1, LANES), 1)
        for row, src in ((6, 1), (10, 2)):
            acc = jnp.zeros((1, LANES), F32)
            for hd in range(HEADS):
                grp = re_ref[0, src:src + 1, hd * LANES:(hd + 1) * LANES] + re_ref[1, src:src + 1, hd * LANES:(hd + 1) * LANES]
                acc = acc + jnp.where(lane == hd, grp, 0.0)
            o_ref[row:row + 1, 0:LANES] = acc
        o_ref[7:8, :] = po_ref[0, 6:7, :] + po_ref[1, 6:7, :]
        o_ref[8:9, :] = pr_ref[2, 0:1, :]
        o_ref[9:10, :] = pr_ref[2, 1:2, :]
        for e in range(2):
            b = 12 + 6 * e
            o_ref[b:b + 1, :] = pr_ref[e, 0:1, :]
            o_ref[b + 1:b + 2, :] = pr_ref[e, 1:2, :]
            o_ref[b + 2:b + 3, :] = po_ref[e, 3:4, :]
            o_ref[b + 3:b + 4, :] = po_ref[e, 0:1, :]
            o_ref[b + 4:b + 5, :] = po_ref[e, 1:2, :]
            o_ref[b + 5:b + 6, :] = po_ref[e, 2:3, :]

    return pl.pallas_call(body, name="pack_small", out_shape=jax.ShapeDtypeStruct((SMALL_ROWS, d), F32))(st_post, st_ret, st_pre)


def _small_reduce(gathered):
    d = gathered.shape[2]

    def body(g_ref, o_ref):
        tot = g_ref[0, 0:PACK_ROWS, :]
        for dev in range(1, N_DEV):
            tot = tot + g_ref[dev, 0:PACK_ROWS, :]
        o_ref[0:PACK_ROWS, :] = tot
        for j in range(6):
            acc = g_ref[0, 12 + j:13 + j, :] + g_ref[0, 18 + j:19 + j, :]
            for dev in range(1, N_DEV):
                acc = acc + g_ref[dev, 12 + j:13 + j, :] + g_ref[dev, 18 + j:19 + j, :]
            if j < 2:
                acc = acc + o_ref[8 + j:9 + j, :]
            o_ref[PACK_ROWS + j:PACK_ROWS + j + 1, :] = acc
        o_ref[PACK_ROWS + 6:PACK_ROWS + 8, :] = jnp.zeros((2, d), F32)

    return pl.pallas_call(body, name="small_reduce", out_shape=jax.ShapeDtypeStruct((PACK_ROWS + 8, d), F32))(gathered)


_SMALL = (("g_attn", 0, 1024), ("g_ffn", 1, 1024), ("g_final", 2, 1024), ("g_ret", 3, 512), ("g_q_lora", 4, 384),
          ("g_kv_lora", 5, 256), ("ret_decay_fwd", 6, HEADS), ("ret_decay_bwd", 10, HEADS))
_SMALL_NAMES = tuple(s[0] for s in _SMALL) + ("c_ctx", "b_ada")


def _small_final(tot, dcc, sg8, ws, ms, vs):
    d = tot.shape[1]
    n = len(_SMALL_NAMES)

    def body(*refs):
        t_ref, dcc_ref, sg_ref = refs[0:3]
        w_refs, m_refs, v_refs = refs[3:3 + n], refs[3 + n:3 + 2 * n], refs[3 + 2 * n:3 + 3 * n]
        outs = refs[3 + 3 * n:]
        g_refs, d_refs, mo_refs, vo_refs = outs[0:n], outs[n:2 * n], outs[2 * n:3 * n], outs[3 * n:4 * n]
        l_ref = outs[4 * n]

        def update(i, g, sl=None):
            pick = (lambda r: r[...]) if sl is None else (lambda r: r[:, sl])
            dl, mn, vn = _adam_math(pick(w_refs[i]), g, pick(m_refs[i]), pick(v_refs[i]))
            if sl is None:
                g_refs[i][...], d_refs[i][...], mo_refs[i][...], vo_refs[i][...] = g, dl, mn, vn
            else:
                g_refs[i][:, sl], d_refs[i][:, sl], mo_refs[i][:, sl], vo_refs[i][:, sl] = g, dl, mn, vn

        for i, (name, row, width) in enumerate(_SMALL):
            g = t_ref[row:row + 1, 0:width]
            if name == "ret_decay_fwd":
                g = g * sg_ref[0:1, 0:width]
            elif name == "ret_decay_bwd":
                g = g * sg_ref[1:2, 0:width]
            update(i, g)
        i_cc, i_b = n - 2, n - 1
        cc = w_refs[i_cc][...]
        s = 1.0 / (1.0 + jnp.exp(-cc))
        dsilu = dcc_ref[0, 0:1, :] + dcc_ref[2, 0:1, :] + dcc_ref[4, 0:1, :] + dcc_ref[6, 0:1, :]
        update(i_cc, dsilu * (s * (1.0 + cc * (1.0 - s))))
        for j in range(6):
            update(i_b, t_ref[PACK_ROWS + j:PACK_ROWS + j + 1, :], pl.ds(j * d, d))
        l_ref[...] = jnp.broadcast_to((0.5 / d) * jnp.sum(t_ref[7:8, :], keepdims=True), l_ref.shape)

    shapes = [jax.ShapeDtypeStruct(a.shape, F32) for a in ws]
    outs = pl.pallas_call(
        body, name="small_final", out_shape=shapes * 4 + [jax.ShapeDtypeStruct((8, LANES), F32)],
    )(tot, dcc, sg8, *ws, *ms, *vs)
    return outs[0:n], outs[n:2 * n], outs[2 * n:3 * n], outs[3 * n:4 * n], outs[4 * n]


_WEIGHTS = ("c_ctx", "w_ada", "b_ada", "g_attn", "g_ffn", "w_in", "ret_decay_fwd", "ret_decay_bwd", "g_ret", "g_q_lora",
            "w_uq", "g_kv_lora", "w_ukv", "w_out", "w_ff1", "w_ff2", "g_final")
_BIG = ("w_in", "w_uq", "w_ukv", "w_out", "w_ff1", "w_ff2")
_TRANSPOSED = ("w_in", "w_uq")


def kernel(x, c, ctx, c_ctx, w_ada, b_ada, g_attn, g_ffn, w_in, ret_decay_fwd, ret_decay_bwd, g_ret, g_q_lora, w_uq, g_kv_lora, w_ukv, w_out, w_ff1, w_ff2, g_final, loss_target, m_c_ctx, m_w_ada, m_b_ada, m_g_attn, m_g_ffn, m_w_in, m_ret_decay_fwd, m_ret_decay_bwd, m_g_ret, m_g_q_lora, m_w_uq, m_g_kv_lora, m_w_ukv, m_w_out, m_w_ff1, m_w_ff2, m_g_final, v_c_ctx, v_w_ada, v_b_ada, v_g_attn, v_g_ffn, v_w_in, v_ret_decay_fwd, v_ret_decay_bwd, v_g_ret, v_g_q_lora, v_w_uq, v_g_kv_lora, v_w_ukv, v_w_out, v_w_ff1, v_w_ff2, v_g_final):
    w = dict(c_ctx=c_ctx, w_ada=w_ada, b_ada=b_ada, g_attn=g_attn, g_ffn=g_ffn, w_in=w_in, ret_decay_fwd=ret_decay_fwd,
             ret_decay_bwd=ret_decay_bwd, g_ret=g_ret, g_q_lora=g_q_lora, w_uq=w_uq, g_kv_lora=g_kv_lora, w_ukv=w_ukv,
             w_out=w_out, w_ff1=w_ff1, w_ff2=w_ff2, g_final=g_final)
    m = dict(c_ctx=m_c_ctx, w_ada=m_w_ada, b_ada=m_b_ada, g_attn=m_g_attn, g_ffn=m_g_ffn, w_in=m_w_in,
             ret_decay_fwd=m_ret_decay_fwd, ret_decay_bwd=m_ret_decay_bwd, g_ret=m_g_ret, g_q_lora=m_g_q_lora, w_uq=m_w_uq,
             g_kv_lora=m_g_kv_lora, w_ukv=m_w_ukv, w_out=m_w_out, w_ff1=m_w_ff1, w_ff2=m_w_ff2, g_final=m_g_final)
    v = dict(c_ctx=v_c_ctx, w_ada=v_w_ada, b_ada=v_b_ada, g_attn=v_g_attn, g_ffn=v_g_ffn, w_in=v_w_in,
             ret_decay_fwd=v_ret_decay_fwd, ret_decay_bwd=v_ret_decay_bwd, g_ret=v_g_ret, g_q_lora=v_g_q_lora, w_uq=v_w_uq,
             g_kv_lora=v_g_kv_lora, w_ukv=v_w_ukv, w_out=v_w_out, w_ff1=v_w_ff1, w_ff2=v_w_ff2, g_final=v_g_final)
    xi, yi, ci = lax.axis_index("x"), lax.axis_index("y"), lax.axis_index("c")
    chip = 2 * xi + yi
    dev = 2 * chip + ci
    nex, seq, d = x.shape
    n_ada = w_ada.shape[2]

    dec = jnp.zeros((8, LANES), F32).at[0, :HEADS].set(ret_decay_fwd[0]).at[1, :HEADS].set(ret_decay_bwd[0])
    lg8, sg8 = _decay_prep(dec)
    lg = lg8[:2, :HEADS]

    def shard_of(t, k):
        return t[k][0].T if k in _TRANSPOSED else t[k][0]

    shard = {k: shard_of(w, k) for k in _BIG}
    head_rows = MLA_NOPE + MLA_ROPE
    shard["w_uq"] = jnp.pad(shard["w_uq"], ((0, MLA_HEAD - head_rows), (0, 0)))
    slot = chip.reshape(1).astype(jnp.int32)
    core = ci.reshape(1).astype(jnp.int32)
    slots = {k: _cast_into_slot(shard[k], slot, "cast_" + k)[0] for k in _BIG if k != "w_ff1"}
    slots["w_ff1"], (w_in_x, w_uq_x, w_ukv_x, c8) = _cast_into_slot(
        shard["w_ff1"], slot, "cast_w_ff1",
        rider=_merge_riders(_gather_ici_rider([slots[k] for k in _EARLY]),
                            _gather8_rider(jnp.pad(c, ((0, 8 - nex), (0, 0))), in_vmem=False)))

    a_in = jnp.concatenate([c8[:, :nex].reshape(N_DEV * nex, d), c_ctx.reshape(1, d), jnp.zeros((7, d), F32)], axis=0)
    b_sh = lax.dynamic_slice(b_ada, (0, chip * n_ada), (1, n_ada))
    mod_sh = _mod_fwd(a_in, w_ada[0], b_sh)
    mod8, w_in_f, w_uq_k, w_ukv_k = _run_rider(
        _merge_riders(_gather8_rider(mod_sh), _gather_d2d_rider([w_in_x, w_uq_x, w_ukv_x])), "ag_early")
    w_in_k = jnp.pad(w_in_f.reshape(IN_COLS, d), ((0, IN_PAD - IN_COLS), (0, 0)))
    mod_all = mod8[0::2].transpose(1, 0, 2).reshape(a_in.shape[0], N_CHIPS * n_ada)
    mod_me = lax.dynamic_slice(mod_all, (nex * dev, 0), (nex, N_CHIPS * n_ada)).reshape(nex, 6, d)
    mod_c = mod_all[N_DEV * nex].reshape(1, 6, d)
    modv = jnp.pad(jnp.concatenate([mod_me, mod_c], axis=0), ((0, 0), (0, 2), (0, 0)))

    gx, g_early, late, st_post, st_ret, st_pre = _local_step(
        x, ctx, loss_target, modv, lg, g_attn, g_ffn, g_final.reshape(1, d), g_ret, g_q_lora, g_kv_lora,
        w_in_k, w_uq_k, w_ukv_k, [slots[k] for k in _LATE], (core, slot))

    mine, theirs, (*late_theirs, gathered) = _reduce_scatter_vmem(
        g_early, [(IN_COLS // N_CHIPS, IN_COLS // N_CHIPS), (head_rows, MLA_HEAD), (KV_LORA, KV_LORA)],
        _merge_riders(_swap_rider(late), _gather8_rider(_pack_small(st_post, st_ret, st_pre))), "rs_early")
    tot = _small_reduce(gathered)
    dm = jnp.concatenate([
        gathered[:, 12:24].reshape(N_DEV * nex, 6 * d),
        jnp.concatenate([tot[8:10].reshape(1, 2 * d), jnp.zeros((1, 4 * d), F32)], axis=1),
        jnp.zeros((7, 6 * d), F32)], axis=0)
    dm_sh = lax.dynamic_slice(dm, (0, chip * n_ada), (dm.shape[0], n_ada))
    g_ada, da = _mod_bwd(a_in, dm_sh, w_ada[0])
    dcc = _allgather8(da[N_DEV * nex:], "ag_dcc")
    halves = dict(zip(_EARLY, zip(mine, theirs)))
    halves.update(zip(_LATE, zip(late, late_theirs)))
    grad, delta, new_m, new_v = {}, {}, {}, {}
    for k in _BIG:
        a, b = halves[k]
        res = _adamw_halves(shard_of(w, k), a, b, shard_of(m, k), shard_of(v, k), core, "adamw_" + k)
        grad[k], delta[k], new_m[k], new_v[k] = [(o.T if k in _TRANSPOSED else o).reshape(w[k].shape) for o in res]

    shp = w_ada.shape
    outs, _ = _adamw(w_ada[0], g_ada, m["w_ada"][0], v["w_ada"][0], "adamw_w_ada")
    grad["w_ada"] = g_ada.reshape(shp)
    delta["w_ada"], new_m["w_ada"], new_v["w_ada"] = [o.reshape(shp) for o in outs]
    rows = [{k: t[k].reshape(1, -1) for k in _SMALL_NAMES} for t in (w, m, v)]
    small = _small_final(tot, dcc, sg8, *[[t[k] for k in _SMALL_NAMES] for t in rows])
    for res, outs in zip((grad, delta, new_m, new_v), small[:4]):
        for k, o in zip(_SMALL_NAMES, outs):
            res[k] = o.reshape(w[k].shape)
    return (small[4][0, 0], gx, *[grad[k] for k in _WEIGHTS], *[delta[k] for k in _WEIGHTS],
            *[new_m[k] for k in _WEIGHTS], *[new_v[k] for k in _WEIGHTS])
```

```python
import functools
import math

import jax
import jax.numpy as jnp
from jax import lax
from jax.experimental import pallas as pl
from jax.experimental.pallas import tpu as pltpu

F32 = jnp.float32
BF16 = jnp.bfloat16
MESH = pl.DeviceIdType.MESH

EPS = 1e-6
D_MODEL = 1024
D_FF = 4096
HEADS = 4
RET_DK = 64
RET_DV = 128
MLA_NOPE = 128
MLA_ROPE = 64
MLA_HEAD = 256
Q_LORA = 384
KV_LORA = 256
GRID_W = 64
ROPE_BASE = 10000.0
IN_COLS = 2240
IN_PAD = 2304
PG_COLS = 1152
N_CHIPS = 4
N_DEV = 8
LANES = 128
ADAM_LR = 0.001
ADAM_B1 = 0.9
ADAM_B2 = 0.999
ADAM_EPS = 1e-08
ADAM_WD = 0.01
ADAM_STEP = 10
VMEM_LIMIT = 56 * 1024 * 1024


def _dot(a, b):
    return jnp.dot(a, b, preferred_element_type=F32)


def _dot_nt(a, b):
    return lax.dot_general(a, b, (((1,), (1,)), ((), ())), preferred_element_type=F32)


def _dot_tn(a, b):
    return lax.dot_general(a, b, (((0,), (0,)), ((), ())), preferred_element_type=F32)


def _params(sem=None, vmem=None):
    return pltpu.CompilerParams(dimension_semantics=sem, vmem_limit_bytes=vmem)


def _full(shape):
    n = len(shape)
    return pl.BlockSpec(shape, lambda *_: (0,) * n)


def _once(shape):
    n = len(shape)
    return pl.BlockSpec(shape, lambda *_: (0,) * n, pipeline_mode=pl.Buffered(1))


def _rope(x, cos, sin):
    w = x.shape[-1]
    lo = (lax.broadcasted_iota(jnp.int32, (1, w), 1) % 64) < 32
    swapped = jnp.where(lo, pltpu.roll(x, w - 32, 1), pltpu.roll(x, 32, 1))
    return x * cos + swapped * sin


def _rope_t(g, cos, sin):
    w = g.shape[-1]
    lo = (lax.broadcasted_iota(jnp.int32, (1, w), 1) % 64) < 32
    t = g * sin
    swapped = jnp.where(lo, pltpu.roll(t, w - 32, 1), pltpu.roll(t, 32, 1))
    return g * cos + swapped


def _rope_tables(seq, tm):
    rows = seq // GRID_W
    row = jnp.repeat(jnp.arange(rows, dtype=F32), GRID_W)
    col = jnp.tile(jnp.arange(GRID_W, dtype=F32), rows)
    n_freq = RET_DK // 4
    freq = ROPE_BASE ** (-jnp.arange(n_freq, dtype=F32) / n_freq)
    ang = jnp.concatenate([row[:, None] * freq, col[:, None] * freq], axis=-1)
    cos, sin = jnp.cos(ang), jnp.sin(ang)
    cos_t = jnp.tile(jnp.concatenate([cos, cos], -1), (1, HEADS))
    sin_t = jnp.tile(jnp.concatenate([-sin, sin], -1), (1, HEADS))
    cos_t = jnp.concatenate([cos_t, jnp.ones((tm, 4 * RET_DK), F32)], 0)
    sin_t = jnp.concatenate([sin_t, jnp.zeros((tm, 4 * RET_DK), F32)], 0)
    return cos_t, sin_t


def _adam_math(w, g, m, v):
    mn = ADAM_B1 * m + (1.0 - ADAM_B1) * g
    vn = ADAM_B2 * v + (1.0 - ADAM_B2) * (g * g)
    m_hat = mn / (1.0 - ADAM_B1 ** ADAM_STEP)
    v_hat = vn / (1.0 - ADAM_B2 ** ADAM_STEP)
    return -ADAM_LR * (m_hat / (jnp.sqrt(v_hat) + ADAM_EPS) + ADAM_WD * w), mn, vn


def _cast_into_slot(w, slot, name, rider=None, rows=None):
    start, r = rows if rows else (0, w.shape[0])
    c = w.shape[1]
    rb = max(b for b in range(16, 257, 16) if r % b == 0 and start % b == 0)
    first = start // rb

    def body(s_ref, w_ref, o_ref):
        o_ref[...] = w_ref[...].astype(BF16)

    (out,), carried = _hosted_call(
        body, (w,), name=name, grid=(r // rb,), prefetch=(slot,),
        in_specs=[pl.BlockSpec((rb, c), lambda i, s: (first + i, 0))],
        out_specs=[pl.BlockSpec((None, rb, c), lambda i, s: (s[0], i, 0))],
        out_shape=[jax.ShapeDtypeStruct((N_CHIPS, r, c), BF16)], sem=("parallel",), rider=rider)
    return out, carried


def _adamw_halves(w, mine, theirs, m, v, core, name):
    r, c = w.shape
    r2 = r // 2
    rb = max(b for b in range(8, r2 + 1, 8) if r2 % b == 0 and b * c * 4 <= (1 << 21))
    nbh = r2 // rb

    def body(z_ref, w_ref, a_ref, b_ref, m_ref, v_ref, g_ref, d_ref, mo_ref, vo_ref):
        here = (pl.program_id(0) // nbh) == z_ref[0]
        gg = jnp.where(here, a_ref[...], b_ref[...])
        g_ref[...] = gg
        d_ref[...], mo_ref[...], vo_ref[...] = _adam_math(w_ref[...], gg, m_ref[...], v_ref[...])

    spec = pl.BlockSpec((rb, c), lambda i, z: (i, 0))
    a_spec = pl.BlockSpec((rb, c), lambda i, z: (jnp.clip(i - z[0] * nbh, 0, nbh - 1), 0))
    b_spec = pl.BlockSpec((rb, c), lambda i, z: (jnp.clip(i - (1 - z[0]) * nbh, 0, nbh - 1), 0))
    shp = jax.ShapeDtypeStruct((r, c), F32)
    return pl.pallas_call(
        body, name=name,
        grid_spec=pltpu.PrefetchScalarGridSpec(
            num_scalar_prefetch=1, grid=(r // rb,), in_specs=[spec, a_spec, b_spec, spec, spec], out_specs=[spec] * 4),
        out_shape=[shp] * 4,
        compiler_params=_params(("parallel",)),
    )(core, w, mine, theirs, m, v)


def _adamw(w, g, m, v, name, rider=None):
    r, c = w.shape
    rb = r
    for cand in (256, 128, 64, 32, 16, 8):
        if r % cand == 0 and cand * c * 4 <= (1 << 20):
            rb = cand
            break
    if r * c * 4 <= (1 << 20):
        rb = r

    def body(w_ref, g_ref, m_ref, v_ref, d_ref, mo_ref, vo_ref):
        d_ref[...], mo_ref[...], vo_ref[...] = _adam_math(w_ref[...], g_ref[...], m_ref[...], v_ref[...])

    spec = pl.BlockSpec((rb, c), lambda i: (i, 0))
    shp = jax.ShapeDtypeStruct((r, c), F32)
    return _hosted_call(
        body, (w, g, m, v), name=name, grid=(r // rb,), in_specs=[spec] * 4, out_specs=[spec] * 3, out_shape=[shp] * 3,
        sem=("parallel",), rider=rider)


def _decay_prep(dec):
    def body(d_ref, lg_ref, sg_ref):
        d = d_ref[...]
        lg_ref[...] = jnp.minimum(d, 0.0) - jnp.log(1.0 + jnp.exp(-jnp.abs(d)))
        sg_ref[...] = 1.0 / (1.0 + jnp.exp(d))

    shp = jax.ShapeDtypeStruct(dec.shape, F32)
    return pl.pallas_call(body, name="decay_prep", out_shape=[shp, shp])(dec)


def _mod_fwd(a_in, w_ada, b_sh):
    rows, d = a_in.shape
    n = w_ada.shape[1]
    bn = 512

    def body(a_ref, w_ref, b_ref, o_ref):
        a = a_ref[...]
        s = (a / (1.0 + jnp.exp(-a))).astype(BF16)
        o_ref[...] = _dot(s, w_ref[...].astype(BF16)) + b_ref[...]

    return pl.pallas_call(
        body, name="mod_fwd", grid=(n // bn,),
        in_specs=[_full((rows, d)), pl.BlockSpec((d, bn), lambda j: (0, j)), pl.BlockSpec((1, bn), lambda j: (0, j))],
        out_specs=pl.BlockSpec((rows, bn), lambda j: (0, j)),
        out_shape=jax.ShapeDtypeStruct((rows, n), F32),
        compiler_params=_params(("parallel",)),
    )(a_in, w_ada, b_sh)


def _mod_bwd(a_in, dm, w_ada):
    rows, d = a_in.shape
    n = w_ada.shape[1]
    bn = 512
    nb = n // bn

    def body(a_ref, dm_ref, w_ref, gw_ref, da_ref):
        j = pl.program_id(0)
        a = a_ref[...]
        s = (a / (1.0 + jnp.exp(-a))).astype(BF16)
        dmb = dm_ref[...].astype(BF16)
        gw_ref[...] = _dot_tn(s, dmb)
        part = _dot_nt(dmb, w_ref[...].astype(BF16))

        @pl.when(j == 0)
        def _():
            da_ref[...] = part

        @pl.when(j > 0)
        def _():
            da_ref[...] += part

    return pl.pallas_call(
        body, name="mod_bwd", grid=(nb,),
        in_specs=[_full((rows, d)), pl.BlockSpec((rows, bn), lambda j: (0, j)), pl.BlockSpec((d, bn), lambda j: (0, j))],
        out_specs=[pl.BlockSpec((d, bn), lambda j: (0, j)), _full((rows, d))],
        out_shape=[jax.ShapeDtypeStruct((d, n), F32), jax.ShapeDtypeStruct((rows, d), F32)],
        compiler_params=_params(("arbitrary",)),
    )(a_in, dm, w_ada)


def _pre_fwd(x2, ctx2, modv, g_attn, w_in, g_q, g_kv, w_uq, w_ukv, cos_t, sin_t, *, seq, tm, rider=None):
    t_lat, d = x2.shape
    t_ctx = ctx2.shape[0]
    nl, nc = t_lat // tm, t_ctx // tm
    n_all = t_lat + t_ctx
    tpe = seq // tm
    nex = t_lat // seq

    def body(x_ref, c_ref, mod_ref, g_ref, win_ref, gq_ref, gkv_ref, wuq_ref, wukv_ref, cos_ref, sin_ref,
             h_ref, pg_ref, rq_ref, rk_ref, rv_ref, nq_ref, nkv_ref, q_ref, k_ref, v_ref):
        i = pl.program_id(0)
        xt = jnp.where(i < nl, x_ref[...], c_ref[...])
        sh = mod_ref[0, 0:1, :]
        sc = mod_ref[0, 1:2, :]
        r = lax.rsqrt(jnp.mean(xt * xt, axis=-1, keepdims=True) + EPS)
        hb = ((xt * r) * g_ref[...] * (1.0 + sc) + sh).astype(BF16)
        h_ref[...] = hb
        p = _dot_nt(hb, win_ref[...])
        cos = cos_ref[...]
        sin = sin_ref[...]
        rq_ref[...] = _rope(p[:, 0:256], cos, sin).astype(BF16)
        rk_ref[...] = _rope(p[:, 256:512] * (RET_DK ** -0.5), cos, sin).astype(BF16)
        rv_ref[...] = p[:, 512:1024].astype(BF16)
        pg_ref[...] = p[:, 1024:2176]
        cq = p[:, 1536:1920]
        ckv = p[:, 1920:2176]
        nqb = (cq * lax.rsqrt(jnp.mean(cq * cq, axis=-1, keepdims=True) + EPS) * gq_ref[...]).astype(BF16)
        nkvb = (ckv * lax.rsqrt(jnp.mean(ckv * ckv, axis=-1, keepdims=True) + EPS) * gkv_ref[...]).astype(BF16)
        nq_ref[...] = nqb
        nkv_ref[...] = nkvb
        cos1 = cos[:, 0:LANES]
        sin1 = sin[:, 0:LANES]
        kpe = _rope(p[:, 2176:2304], cos1, sin1).astype(BF16)
        for hd in range(HEADS):
            o = hd * MLA_HEAD
            qh = _dot_nt(nqb, wuq_ref[hd]) * MLA_SCALE
            q_ref[:, o:o + 128] = qh[:, 0:128].astype(BF16)
            q_ref[:, o + 128:o + 256] = _rope(qh[:, 128:256], cos1, sin1).astype(BF16)
            kvh = _dot(nkvb, wukv_ref[hd])
            k_ref[:, o:o + 128] = kvh[:, 0:128].astype(BF16)
            k_ref[:, o + 128:o + 256] = kpe
            v_ref[:, hd * 128:(hd + 1) * 128] = kvh[:, 128:256].astype(BF16)

    def tile(width):
        return pl.BlockSpec((tm, width), lambda i: (i, 0))

    widths = (d, PG_COLS, 256, 256, 512, Q_LORA, KV_LORA, HEADS * MLA_HEAD, HEADS * MLA_HEAD, HEADS * 128)
    dtypes = (BF16, F32, BF16, BF16, BF16, BF16, BF16, BF16, BF16, BF16)
    tab = pl.BlockSpec((tm, 256), lambda i: (jnp.where(i < nl, i % tpe, tpe), 0))
    return _hosted_call(
        body, (x2, ctx2, modv, g_attn, w_in, g_q, g_kv, w_uq, w_ukv, cos_t, sin_t), name="pre_fwd", grid=(nl + nc,),
        in_specs=[
            pl.BlockSpec((tm, d), lambda i: (jnp.minimum(i, nl - 1), 0)),
            pl.BlockSpec((tm, d), lambda i: (jnp.maximum(i - nl, 0), 0)),
            pl.BlockSpec((1, 8, d), lambda i: (jnp.minimum(i // tpe, nex), 0, 0)),
            _full((1, d)), _full(w_in.shape), _full((1, Q_LORA)), _full((1, KV_LORA)),
            _full(w_uq.shape), _full(w_ukv.shape), tab, tab,
        ],
        out_specs=[tile(w) for w in widths],
        out_shape=[jax.ShapeDtypeStruct((n_all, w), dt) for w, dt in zip(widths, dtypes)],
        sem=("parallel",), rider=rider)


def _post(yret, ymla, x2, tgt2, modv, g_ffn, g_fin, w_out, w_ff1, w_ff2a, w_ff2b, *, seq, tm):
    t_lat, d = x2.shape
    nl = t_lat // tm
    tpe = seq // tm
    nex = t_lat // seq
    n_slab = w_ff1.shape[0]
    fs = w_ff1.shape[2]
    fh = w_ff2a.shape[1]

    def body(yr_ref, ym_ref, x_ref, t_ref, mod_ref, gf_ref, gl_ref, wo_ref, w1_ref, w2a_ref, w2b_ref,
             mix_ref, a_ref, du_ref, h2_ref, df_ref, dmo_ref, dmix_ref, dxm_ref, st_ref, ru_ref):
        i = pl.program_id(0)
        gt_a = mod_ref[0, 2:3, :]
        sh_f = mod_ref[0, 3:4, :]
        sc_f = mod_ref[0, 4:5, :]
        gt_f = mod_ref[0, 5:6, :]
        g_ffn_v = gf_ref[...]
        g_fin_v = gl_ref[...]
        yr = yr_ref[...]
        ym = ym_ref[...]
        mix_ref[:, 0:512] = yr
        mix_ref[:, 512:1024] = ym
        op = _dot(yr, wo_ref[0:512, :]) + _dot(ym, wo_ref[512:1024, :])
        x_mid = x_ref[...] + gt_a * op
        r2 = lax.rsqrt(jnp.mean(x_mid * x_mid, axis=-1, keepdims=True) + EPS)
        xh2 = x_mid * r2
        h2b = (xh2 * g_ffn_v * (1.0 + sc_f) + sh_f).astype(BF16)
        h2_ref[...] = h2b
        f = jnp.zeros((tm, d), F32)
        for s in range(n_slab):
            ru = jnp.maximum(_dot(h2b, w1_ref[s]), 0.0)
            ru_ref[:, s * fs:(s + 1) * fs] = ru
            ab = (ru * ru).astype(BF16)
            a_ref[:, s * fs:(s + 1) * fs] = ab
            f = f + _dot(ab[:, 0:fh], w2a_ref[s]) + _dot(ab[:, fh:fs], w2b_ref[s])
        x_out = x_mid + gt_f * f
        r3 = lax.rsqrt(jnp.mean(x_out * x_out, axis=-1, keepdims=True) + EPS)
        xh3 = x_out * r3
        err = xh3 * g_fin_v - t_ref[...]
        dy = err * (1.0 / d)
        dxh3 = dy * g_fin_v
        dx_out = r3 * (dxh3 - xh3 * jnp.mean(dxh3 * xh3, axis=-1, keepdims=True))
        dfb = (dx_out * gt_f).astype(BF16)
        df_ref[...] = dfb
        dh2 = jnp.zeros((tm, d), F32)
        for s in range(n_slab):
            da = jnp.concatenate([_dot_nt(dfb, w2a_ref[s]), _dot_nt(dfb, w2b_ref[s])], axis=1)
            dub = (da * (2.0 * ru_ref[:, s * fs:(s + 1) * fs])).astype(BF16)
            du_ref[:, s * fs:(s + 1) * fs] = dub
            dh2 = dh2 + _dot_nt(dub, w1_ref[s])
        dxh2 = dh2 * (1.0 + sc_f) * g_ffn_v
        dx_mid = dx_out + r2 * (dxh2 - xh2 * jnp.mean(dxh2 * xh2, axis=-1, keepdims=True))
        dxm_ref[...] = dx_mid
        dmob = (dx_mid * gt_a).astype(BF16)
        dmo_ref[...] = dmob
        dmix_ref[...] = _dot_nt(dmob, wo_ref[...]).astype(BF16)

        def rsum(v):
            return jnp.sum(v, axis=0, keepdims=True)

        stats = jnp.concatenate([
            rsum(dh2), rsum(dh2 * xh2 * g_ffn_v), rsum(dx_out * f), rsum(dx_mid * op),
            rsum(dh2 * (1.0 + sc_f) * xh2), rsum(dy * xh3), rsum(err * err), jnp.zeros((1, d), F32)], axis=0)

        @pl.when(i % tpe == 0)
        def _():
            st_ref[0] = stats

        @pl.when(i % tpe != 0)
        def _():
            st_ref[0] += stats

    def tile(width):
        return pl.BlockSpec((tm, width), lambda i: (i, 0))

    widths = (d, D_FF, D_FF, d, d, d, d, d)
    dtypes = (BF16, BF16, BF16, BF16, BF16, BF16, BF16, F32)
    const = pl.Buffered(1)
    return pl.pallas_call(
        body, name="post", grid=(nl,),
        in_specs=[
            tile(512), tile(512), tile(d), tile(d),
            pl.BlockSpec((1, 8, d), lambda i: (i // tpe, 0, 0)),
            _full((1, d)), _full((1, d)),
            pl.BlockSpec(w_out.shape, lambda i: (0, 0), pipeline_mode=const),
            pl.BlockSpec(w_ff1.shape, lambda i: (0, 0, 0), pipeline_mode=const),
            pl.BlockSpec(w_ff2a.shape, lambda i: (0, 0, 0), pipeline_mode=const),
            pl.BlockSpec(w_ff2b.shape, lambda i: (0, 0, 0), pipeline_mode=const),
        ],
        out_specs=[tile(w) for w in widths] + [pl.BlockSpec((1, 8, d), lambda i: (i // tpe, 0, 0))],
        out_shape=[jax.ShapeDtypeStruct((t_lat, w), dt) for w, dt in zip(widths, dtypes)]
        + [jax.ShapeDtypeStruct((nex, 8, d), F32)],
        scratch_shapes=[pltpu.VMEM((tm, D_FF), F32)],
        compiler_params=_params(("arbitrary",), VMEM_LIMIT),
    )(yret, ymla, x2, tgt2, modv, g_ffn, g_fin, w_out, w_ff1, w_ff2a, w_ff2b)


def _pre_bwd(x2, ctx2, modv, g_attn, pg, drq, drk, dkc_r, drv, dvc_r, drg, dq_m, dkl, dkc, dvl, dvc, dxm,
             w_in, g_q, g_kv, w_uq, w_ukv, cos_t, sin_t, *, seq, tm, rider=None):
    t_lat, d = x2.shape
    t_ctx = ctx2.shape[0]
    nl, nc = t_lat // tm, t_ctx // tm
    n_all = t_lat + t_ctx
    tpe = seq // tm
    nex = t_lat // seq

    def body(x_ref, c_ref, mod_ref, g_ref, pg_ref, drq_ref, drk_ref, dkcr_ref, drv_ref, dvcr_ref, drg_ref,
             dq_ref, dkl_ref, dkc_ref, dvl_ref, dvc_ref, dxm_ref, win_ref, gq_ref, gkv_ref, wuq_ref, wukv_ref,
             cos_ref, sin_ref, dpb_ref, dqf_ref, dkvf_ref, gx_ref, st_ref):
        i = pl.program_id(0)
        lat = i < nl
        latf = lat.astype(F32)
        cos = cos_ref[...]
        sin = sin_ref[...]
        cos1 = cos[:, 0:LANES]
        sin1 = sin[:, 0:LANES]
        d_rq = _rope_t(drq_ref[...] * latf, cos, sin)
        d_rk = _rope_t(jnp.where(lat, drk_ref[...], dkcr_ref[...]), cos, sin) * (RET_DK ** -0.5)
        d_rv = jnp.where(lat, drv_ref[...], dvcr_ref[...])
        d_rg = drg_ref[...] * latf
        dq_all = dq_ref[...] * (latf * MLA_SCALE)
        dk_all = jnp.where(lat, dkl_ref[...], dkc_ref[...])
        dv_all = jnp.where(lat, dvl_ref[...], dvc_ref[...])
        dnq = jnp.zeros((tm, Q_LORA), F32)
        dnkv = jnp.zeros((tm, KV_LORA), F32)
        dkpe = jnp.zeros((tm, LANES), F32)
        for hd in range(HEADS):
            o = hd * MLA_HEAD
            dqh = jnp.concatenate([dq_all[:, o:o + 128], _rope_t(dq_all[:, o + 128:o + 256], cos1, sin1)],
                                  axis=1).astype(BF16)
            dqf_ref[:, o:o + 256] = dqh
            dnq = dnq + _dot(dqh, wuq_ref[hd])
            dkpe = dkpe + dk_all[:, o + 128:o + 256]
            dkvh = jnp.concatenate([dk_all[:, o:o + 128], dv_all[:, hd * 128:(hd + 1) * 128]], axis=1).astype(BF16)
            dkvf_ref[:, o:o + 256] = dkvh
            dnkv = dnkv + _dot_nt(dkvh, wukv_ref[hd])
        d_kpe = _rope_t(dkpe, cos1, sin1)
        pgv = pg_ref[...]
        cq = pgv[:, 512:896]
        ckv = pgv[:, 896:1152]
        rq_ = lax.rsqrt(jnp.mean(cq * cq, axis=-1, keepdims=True) + EPS)
        cqh = cq * rq_
        dcqh = dnq * gq_ref[...]
        d_cq = rq_ * (dcqh - cqh * jnp.mean(dcqh * cqh, axis=-1, keepdims=True))
        rkv_ = lax.rsqrt(jnp.mean(ckv * ckv, axis=-1, keepdims=True) + EPS)
        ckvh = ckv * rkv_
        dckvh = dnkv * gkv_ref[...]
        d_ckv = rkv_ * (dckvh - ckvh * jnp.mean(dckvh * ckvh, axis=-1, keepdims=True))
        dpb = jnp.concatenate([d_rq, d_rk, d_rv, d_rg, d_cq, d_ckv, d_kpe], axis=1).astype(BF16)
        dpb_ref[...] = dpb
        dh = _dot(dpb, win_ref[...])
        xt = jnp.where(lat, x_ref[...], c_ref[...])
        sc = mod_ref[0, 1:2, :]
        g = g_ref[...]
        r = lax.rsqrt(jnp.mean(xt * xt, axis=-1, keepdims=True) + EPS)
        xh = xt * r
        dxh = dh * (1.0 + sc) * g
        dx = r * (dxh - xh * jnp.mean(dxh * xh, axis=-1, keepdims=True))

        @pl.when(lat)
        def _():
            gx_ref[...] = dxm_ref[...] + dx

        def rsum(v):
            return jnp.sum(v, axis=0, keepdims=True)

        def widen(v):
            return jnp.concatenate([v, jnp.zeros((1, d - v.shape[1]), F32)], axis=1)

        stats = jnp.concatenate([
            rsum(dh), rsum(dh * xh * g), rsum(dh * (1.0 + sc) * xh), widen(rsum(dnq * cqh)), widen(rsum(dnkv * ckvh)),
            jnp.zeros((3, d), F32)], axis=0)
        first = jnp.logical_or(jnp.logical_and(lat, i % tpe == 0), i == nl)

        @pl.when(first)
        def _():
            st_ref[0] = stats

        @pl.when(jnp.logical_not(first))
        def _():
            st_ref[0] += stats

    def lat_tile(width):
        return pl.BlockSpec((tm, width), lambda i: (jnp.minimum(i, nl - 1), 0))

    def ctx_tile(width):
        return pl.BlockSpec((tm, width), lambda i: (jnp.maximum(i - nl, 0), 0))

    def tile(width):
        return pl.BlockSpec((tm, width), lambda i: (i, 0))

    tab = pl.BlockSpec((tm, 256), lambda i: (jnp.where(i < nl, i % tpe, tpe), 0))
    ex = pl.BlockSpec((1, 8, d), lambda i: (jnp.minimum(i // tpe, nex), 0, 0))
    return _hosted_call(
        body, (x2, ctx2, modv, g_attn, pg, drq, drk, dkc_r, drv, dvc_r, drg, dq_m, dkl, dkc, dvl, dvc, dxm,
               w_in, g_q, g_kv, w_uq, w_ukv, cos_t, sin_t), name="pre_bwd", grid=(nl + nc,),
        in_specs=[
            lat_tile(d), ctx_tile(d), ex, _full((1, d)), tile(PG_COLS),
            lat_tile(256), lat_tile(256), ctx_tile(256), lat_tile(512), ctx_tile(512), lat_tile(512),
            lat_tile(1024), lat_tile(1024), ctx_tile(1024), lat_tile(512), ctx_tile(512), lat_tile(d),
            _once(w_in.shape), _full((1, Q_LORA)), _full((1, KV_LORA)), _once(w_uq.shape), _once(w_ukv.shape),
            tab, tab,
        ],
        out_specs=[tile(IN_PAD), tile(1024), tile(1024), lat_tile(d), ex],
        out_shape=[
            jax.ShapeDtypeStruct((n_all, IN_PAD), BF16), jax.ShapeDtypeStruct((n_all, 1024), BF16),
            jax.ShapeDtypeStruct((n_all, 1024), BF16), jax.ShapeDtypeStruct((t_lat, d), F32),
            jax.ShapeDtypeStruct((nex + 1, 8, d), F32),
        ],
        sem=("arbitrary",), rider=rider)


MLA_SCALE = 1.0 / math.sqrt(MLA_NOPE + MLA_ROPE)
KEY_BLOCK = 1024


def _mla_specs(t_lat, seq, ctx_len, tq, heads=1):
    nqt = seq // tq
    cb = t_lat // ctx_len
    q = pl.BlockSpec((tq, heads * MLA_HEAD), lambda b, h, j: (b * nqt + j, h))
    kl = pl.BlockSpec((seq, heads * MLA_HEAD), lambda b, h, j: (b, h))
    kc = pl.BlockSpec((ctx_len, heads * MLA_HEAD), lambda b, h, j: (cb + b, h))
    vl = pl.BlockSpec((seq, heads * 128), lambda b, h, j: (b, h))
    vc = pl.BlockSpec((ctx_len, heads * 128), lambda b, h, j: (cb + b, h))
    o = pl.BlockSpec((tq, heads * 128), lambda b, h, j: (b * nqt + j, h))
    return q, kl, kc, vl, vc, o


FWD_HEADS = 2
BWD_HEADS = 1


def _mla_fwd(q, k, v, *, t_lat, seq, ctx_len, tq, rider=None):
    nex = t_lat // seq

    def body(q_ref, kl_ref, kc_ref, vl_ref, vc_ref, o_ref, lse_ref):
        for hh in range(FWD_HEADS):
            wide = slice(hh * MLA_HEAD, (hh + 1) * MLA_HEAD)
            cols = slice(hh * 128, (hh + 1) * 128)
            qb = q_ref[:, wide]
            s = _dot_nt(qb, kl_ref[:, wide])
            sc = _dot_nt(qb, kc_ref[:, wide])
            m = jnp.maximum(jnp.max(s, axis=-1, keepdims=True), jnp.max(sc, axis=-1, keepdims=True))
            p = jnp.exp(s - m)
            pc = jnp.exp(sc - m)
            total = jnp.sum(p, axis=-1, keepdims=True) + jnp.sum(pc, axis=-1, keepdims=True)
            o = _dot(p.astype(BF16), vl_ref[:, cols]) + _dot(pc.astype(BF16), vc_ref[:, cols])
            o_ref[:, cols] = (o * (1.0 / total)).astype(BF16)
            lse_ref[:, cols] = jnp.broadcast_to(m + jnp.log(total), (tq, 128))

    qs, kl, kc, vl, vc, os_ = _mla_specs(t_lat, seq, ctx_len, tq, FWD_HEADS)
    return _hosted_call(
        body, (q, k, k, v, v), name="mla_fwd", grid=(nex, HEADS // FWD_HEADS, seq // tq),
        in_specs=[qs, kl, kc, vl, vc], out_specs=[os_, os_],
        out_shape=[jax.ShapeDtypeStruct((t_lat, HEADS * 128), BF16), jax.ShapeDtypeStruct((t_lat, HEADS * 128), F32)],
        sem=("parallel", "parallel", "arbitrary"), rider=rider)


def _mla_bwd(q, k, v, ymla, lse, dmix, *, t_lat, seq, ctx_len, tq, rider=None):
    nex = t_lat // seq
    nqt = seq // tq
    t_ctx = nex * ctx_len
    kb = min(KEY_BLOCK, seq)

    def body(q_ref, kl_ref, kc_ref, vl_ref, vc_ref, o_ref, lse_ref, do_ref, dq_ref, dkl_ref, dkc_ref, dvl_ref, dvc_ref):
        j = pl.program_id(2)

        @pl.when(j == 0)
        def _():
            dkl_ref[...] = jnp.zeros(dkl_ref.shape, F32)
            dkc_ref[...] = jnp.zeros(dkc_ref.shape, F32)
            dvl_ref[...] = jnp.zeros(dvl_ref.shape, F32)
            dvc_ref[...] = jnp.zeros(dvc_ref.shape, F32)

        for hh in range(BWD_HEADS):
            wide = slice(hh * MLA_HEAD, (hh + 1) * MLA_HEAD)
            cols = slice(hh * 128, (hh + 1) * 128)
            qb = q_ref[:, wide]
            dob = do_ref[:, cols]
            delta = jnp.sum(dob.astype(F32) * o_ref[:, cols].astype(F32), axis=-1, keepdims=True)
            lse_row = lse_ref[:, hh * 128:hh * 128 + 1]

            def block(k_ref, v_ref, dk_ref, dv_ref, rows):
                kbl = k_ref[rows, wide]
                vbl = v_ref[rows, cols]
                p = jnp.exp(_dot_nt(qb, kbl) - lse_row)
                ds = (p * (_dot_nt(dob, vbl) - delta)).astype(BF16)
                dk_ref[rows, wide] += _dot_tn(ds, qb)
                dv_ref[rows, cols] += _dot_tn(p.astype(BF16), dob)
                return _dot(ds, kbl)

            dq = block(kc_ref, vc_ref, dkc_ref, dvc_ref, pl.ds(0, ctx_len))
            for i in range(seq // kb):
                dq = dq + block(kl_ref, vl_ref, dkl_ref, dvl_ref, pl.ds(i * kb, kb))
            dq_ref[:, wide] = dq

    g = BWD_HEADS
    qs, kl, kc, vl, vc, os_ = _mla_specs(t_lat, seq, ctx_len, tq, g)
    do_spec = pl.BlockSpec((tq, g * 128), lambda b, h, j: (b * nqt + j, HEADS // g + h))
    return _hosted_call(
        body, (q, k, k, v, v, ymla, lse, dmix), name="mla_bwd", grid=(nex, HEADS // g, nqt),
        in_specs=[qs, kl, kc, vl, vc, os_, os_, do_spec],
        out_specs=[
            qs,
            pl.BlockSpec((seq, g * MLA_HEAD), lambda b, h, j: (b, h)),
            pl.BlockSpec((ctx_len, g * MLA_HEAD), lambda b, h, j: (b, h)),
            pl.BlockSpec((seq, g * 128), lambda b, h, j: (b, h)),
            pl.BlockSpec((ctx_len, g * 128), lambda b, h, j: (b, h)),
        ],
        out_shape=[
            jax.ShapeDtypeStruct((t_lat, HEADS * MLA_HEAD), F32),
            jax.ShapeDtypeStruct((t_lat, HEADS * MLA_HEAD), F32),
            jax.ShapeDtypeStruct((t_ctx, HEADS * MLA_HEAD), F32),
            jax.ShapeDtypeStruct((t_lat, HEADS * 128), F32),
            jax.ShapeDtypeStruct((t_ctx, HEADS * 128), F32),
        ],
        sem=("parallel", "parallel", "arbitrary"), rider=rider)


def _decay_terms(lg, chunk, forward):
    ii = lax.broadcasted_iota(jnp.int32, (chunk, chunk), 0)
    jj = lax.broadcasted_iota(jnp.int32, (chunk, chunk), 1)
    diff = (ii - jj) if forward else (jj - ii)
    dist = jnp.maximum(diff, 0).astype(F32)
    dmat = jnp.where(diff >= 0, jnp.exp(lg * dist), 0.0)
    pos = lax.broadcasted_iota(jnp.int32, (chunk, 1), 0).astype(F32)
    if forward:
        e_q = pos + 1.0
        e_k = (chunk - 1.0) - pos
    else:
        e_q = chunk - pos
        e_k = pos
    wq = jnp.exp(lg * e_q)
    wk = jnp.exp(lg * e_k)
    cd = jnp.exp(jnp.full((1, 1), lg * chunk, F32))
    return dmat, dist, wq, wk, e_q, e_k, cd


def _ctx_weights(lg, ctx_len, forward):
    pos = lax.broadcasted_iota(jnp.int32, (ctx_len, 1), 0).astype(F32)
    e = ((ctx_len - 1.0) - pos) if forward else pos
    return jnp.exp(lg * e), e


def _pair_specs(t_lat, seq, ctx_len):
    cb = t_lat // ctx_len
    qk = pl.BlockSpec((seq, 128), lambda b, p: (b, p))
    v = pl.BlockSpec((seq, 256), lambda b, p: (b, p))
    kc = pl.BlockSpec((ctx_len, 128), lambda b, p: (cb + b, p))
    vc = pl.BlockSpec((ctx_len, 256), lambda b, p: (cb + b, p))
    return qk, v, kc, vc


def _lane_masks():
    lane = lax.broadcasted_iota(jnp.int32, (1, 128), 1)
    return [(lane // RET_DK) == hh for hh in (0, 1)]


def _ret_fwd_pair(rq, rk, rv, pg, lg, g_ret, *, t_lat, seq, ctx_len, chunk, rider=None):
    nex = t_lat // seq
    n_chunk = seq // chunk

    def body(q_ref, k_ref, v_ref, kc_ref, vc_ref, rg_ref, lg_ref, g_ref, y_ref, o_ref):
        pair = pl.program_id(1)
        masks = _lane_masks()
        kcf = kc_ref[...].astype(F32)

        def run(forward):
            terms, s0 = [], []
            for hh in (0, 1):
                lgd = lg_ref[0 if forward else 1, 2 * pair + hh]
                terms.append(_decay_terms(lgd, chunk, forward))
                wc, _ = _ctx_weights(lgd, ctx_len, forward)
                s0.append(_dot_tn((jnp.where(masks[hh], kcf, 0.0) * wc).astype(BF16), vc_ref[:, hh * 128:(hh + 1) * 128]))

            def step(t, states):
                n = t if forward else n_chunk - 1 - t
                sl = pl.ds(pl.multiple_of(n * chunk, chunk), chunk)
                qb = q_ref[sl, :]
                kf_all = k_ref[sl, :].astype(F32)
                new = []
                for hh in (0, 1):
                    dmat, _, wq, wk, _, _, cd = terms[hh]
                    cols = slice(hh * 128, (hh + 1) * 128)
                    qm = jnp.where(masks[hh], qb, jnp.zeros((), BF16))
                    kf = jnp.where(masks[hh], kf_all, 0.0)
                    vb = v_ref[sl, cols]
                    a = _dot_nt(qm, kf.astype(BF16)) * dmat
                    o = _dot(a.astype(BF16), vb) + wq * _dot(qm, states[hh].astype(BF16))
                    if forward:
                        o_ref[sl, cols] = o
                    else:
                        o = o_ref[sl, cols] + o
                        o_ref[sl, cols] = o
                        mu = jnp.mean(o, axis=-1, keepdims=True)
                        oc = o - mu
                        var = jnp.mean(oc * oc, axis=-1, keepdims=True)
                        rg = rg_ref[sl, cols]
                        y_ref[sl, cols] = (oc * lax.rsqrt(var + EPS) * g_ref[:, cols] * (rg / (1.0 + jnp.exp(-rg)))).astype(BF16)
                    new.append(cd * states[hh] + _dot_tn((kf * wk).astype(BF16), vb))
                return tuple(new)

            lax.fori_loop(0, n_chunk, step, tuple(s0))

        run(True)
        run(False)

    qk, v, kc, vc = _pair_specs(t_lat, seq, ctx_len)
    return _hosted_call(
        body, (rq, rk, rv, rk, rv, pg, lg, g_ret), name="ret_fwd", grid=(nex, HEADS // 2),
        in_specs=[qk, qk, v, kc, vc, v, pl.BlockSpec(memory_space=pltpu.SMEM), pl.BlockSpec((1, 256), lambda b, p: (0, p))],
        out_specs=[v, v],
        out_shape=[jax.ShapeDtypeStruct((t_lat, HEADS * RET_DV), BF16), jax.ShapeDtypeStruct((t_lat, HEADS * RET_DV), F32)],
        sem=("parallel", "arbitrary"), rider=rider)


def _ret_bwd_pair(rq, rk, rv, pg, osum, dmix, lg, g_ret, *, t_lat, seq, ctx_len, chunk, rider=None):
    nex = t_lat // seq
    n_chunk = seq // chunk
    t_ctx = nex * ctx_len

    def body(q_ref, k_ref, v_ref, kc_ref, vc_ref, rg_ref, o_ref, dy_ref, lg_ref, g_ref,
             dq_ref, dk_ref, dv_ref, dkc_ref, dvc_ref, drg_ref, st_ref, do_s, s_st):
        pair = pl.program_id(1)
        masks = _lane_masks()
        kcf = kc_ref[...].astype(F32)

        def norm_step(n, dgains):
            sl = pl.ds(pl.multiple_of(n * chunk, chunk), chunk)
            out = []
            for hh in (0, 1):
                cols = slice(hh * 128, (hh + 1) * 128)
                gain = g_ref[:, cols]
                o = o_ref[sl, cols]
                mu = jnp.mean(o, axis=-1, keepdims=True)
                oc = o - mu
                rstd = lax.rsqrt(jnp.mean(oc * oc, axis=-1, keepdims=True) + EPS)
                ohat = oc * rstd
                rg = rg_ref[sl, cols]
                sg = 1.0 / (1.0 + jnp.exp(-rg))
                dy = dy_ref[sl, cols].astype(F32)
                don = dy * (rg * sg)
                drg_ref[sl, cols] = dy * (ohat * gain) * (sg * (1.0 + rg * (1.0 - sg)))
                dohat = don * gain
                do_s[sl, cols] = rstd * (dohat - jnp.mean(dohat, axis=-1, keepdims=True)
                                         - ohat * jnp.mean(dohat * ohat, axis=-1, keepdims=True))
                out.append(dgains[hh] + jnp.sum(don * ohat, axis=0, keepdims=True))
            return tuple(out)

        zero_row = jnp.zeros((1, 128), F32)
        dgains = lax.fori_loop(0, n_chunk, norm_step, (zero_row, zero_row))
        dq_ref[...] = jnp.zeros(dq_ref.shape, F32)
        dk_ref[...] = jnp.zeros(dk_ref.shape, F32)
        dv_ref[...] = jnp.zeros(dv_ref.shape, F32)

        chains = [(forward, hh) for forward in (True, False) for hh in (0, 1)]
        terms, ctxw, s0 = [], [], []
        for forward, hh in chains:
            lgd = lg_ref[0 if forward else 1, 2 * pair + hh]
            terms.append(_decay_terms(lgd, chunk, forward))
            ctxw.append(_ctx_weights(lgd, ctx_len, forward))
            s0.append(_dot_tn((jnp.where(masks[hh], kcf, 0.0) * ctxw[-1][0]).astype(BF16), vc_ref[:, hh * 128:(hh + 1) * 128]))

        def chunk_at(t, ascending):
            n = t if ascending else n_chunk - 1 - t
            return n, pl.ds(pl.multiple_of(n * chunk, chunk), chunk)

        def state_step(t, states):
            new = []
            for c, (forward, hh) in enumerate(chains):
                n, sl = chunk_at(t, forward)
                wk, cd = terms[c][3], terms[c][6]
                s_st[c, n] = states[c]
                kf = jnp.where(masks[hh], k_ref[sl, :].astype(F32), 0.0)
                new.append(cd * states[c] + _dot_tn((kf * wk).astype(BF16), v_ref[sl, hh * 128:(hh + 1) * 128]))
            return tuple(new)

        lax.fori_loop(0, n_chunk, state_step, tuple(s0))

        def grad_step(t, carry):
            out = []
            for forward in (True, False):
                n, sl = chunk_at(t, not forward)
                qb = q_ref[sl, :]
                kf_all = k_ref[sl, :].astype(F32)
                dq_sum = jnp.zeros((chunk, 128), F32)
                dk_sum = jnp.zeros((chunk, 128), F32)
                for hh in (0, 1):
                    c = chains.index((forward, hh))
                    g_next, dlg = carry[c]
                    dmat, dist, wq, wk, e_q, e_k, cd = terms[c]
                    cols = slice(hh * 128, (hh + 1) * 128)
                    qm = jnp.where(masks[hh], qb, jnp.zeros((), BF16))
                    kf = jnp.where(masks[hh], kf_all, 0.0)
                    kb = kf.astype(BF16)
                    vb = v_ref[sl, cols]
                    do = do_s[sl, cols]
                    dob = do.astype(BF16)
                    s_n = s_st[c, n]
                    s_nb = s_n.astype(BF16)
                    gb = g_next.astype(BF16)
                    dk_cross = wk * _dot_nt(vb, gb)
                    dv_cross = _dot((kf * wk).astype(BF16), gb)
                    a = _dot_nt(qm, kb) * dmat
                    da_raw = _dot_nt(dob, vb)
                    dab = (da_raw * dmat).astype(BF16)
                    o_cross = wq * _dot(qm, s_nb)
                    dq_sum = dq_sum + _dot(dab, kb) + wq * _dot_nt(dob, s_nb)
                    dk_sum = dk_sum + _dot_tn(dab, qm) + dk_cross
                    dv_ref[sl, cols] += _dot_tn(a.astype(BF16), dob) + dv_cross
                    dlg = (dlg + chunk * cd * jnp.sum(g_next * s_n, keepdims=True)
                           + jnp.sum(e_k * jnp.sum(kf * dk_cross, axis=-1, keepdims=True), keepdims=True)
                           + jnp.sum(dist * a * da_raw, keepdims=True)
                           + jnp.sum(e_q * jnp.sum(o_cross * do, axis=-1, keepdims=True), keepdims=True))
                    out.append((cd * g_next + _dot_tn((qm.astype(F32) * wq).astype(BF16), dob), dlg))
                dq_ref[sl, :] += dq_sum
                dk_ref[sl, :] += dk_sum
            return tuple(out)

        zero = (jnp.zeros((128, 128), F32), jnp.zeros((1, 1), F32))
        res = lax.fori_loop(0, n_chunk, grad_step, (zero,) * len(chains))
        dkc_sum = jnp.zeros((ctx_len, 128), F32)
        dvc = [jnp.zeros((ctx_len, 128), F32)] * 2
        dlgs = []
        for c, (forward, hh) in enumerate(chains):
            ds0, dlg = res[c]
            wc, e_c = ctxw[c]
            kcm = jnp.where(masks[hh], kcf, 0.0)
            ds0b = ds0.astype(BF16)
            dkc_part = wc * _dot_nt(vc_ref[:, hh * 128:(hh + 1) * 128], ds0b)
            dkc_sum = dkc_sum + dkc_part
            dvc[hh] = dvc[hh] + _dot((kcm * wc).astype(BF16), ds0b)
            dlgs.append(dlg + jnp.sum(e_c * jnp.sum(kcm * dkc_part, axis=-1, keepdims=True), keepdims=True))
        dkc_ref[...] = dkc_sum
        for hh in (0, 1):
            cols = slice(hh * 128, (hh + 1) * 128)
            dvc_ref[:, cols] = dvc[hh]
            st_ref[0, :, cols] = jnp.concatenate([
                dgains[hh], jnp.broadcast_to(dlgs[hh], (1, 128)), jnp.broadcast_to(dlgs[2 + hh], (1, 128)),
                jnp.zeros((5, 128), F32)], axis=0)

    qk, v, kc, vc = _pair_specs(t_lat, seq, ctx_len)
    return _hosted_call(
        body, (rq, rk, rv, rk, rv, pg, osum, dmix, lg, g_ret), name="ret_bwd", grid=(nex, HEADS // 2),
        in_specs=[qk, qk, v, kc, vc, v, v, v, pl.BlockSpec(memory_space=pltpu.SMEM),
                  pl.BlockSpec((1, 256), lambda b, p: (0, p))],
        out_specs=[
            qk, qk, v,
            pl.BlockSpec((ctx_len, 128), lambda b, p: (b, p)),
            pl.BlockSpec((ctx_len, 256), lambda b, p: (b, p)),
            v,
            pl.BlockSpec((1, 8, 256), lambda b, p: (b, 0, p)),
        ],
        out_shape=[
            jax.ShapeDtypeStruct((t_lat, 256), F32), jax.ShapeDtypeStruct((t_lat, 256), F32),
            jax.ShapeDtypeStruct((t_lat, 512), F32), jax.ShapeDtypeStruct((t_ctx, 256), F32),
            jax.ShapeDtypeStruct((t_ctx, 512), F32), jax.ShapeDtypeStruct((t_lat, 512), F32),
            jax.ShapeDtypeStruct((nex, 8, 512), F32),
        ],
        scratch_shapes=[pltpu.VMEM((seq, 256), F32), pltpu.VMEM((4, n_chunk, 128, 128), F32)],
        sem=("parallel", "arbitrary"), rider=rider)


def _matmul_tn(a, b, *, bm, bn, bk, chip_major, name, out_dtype=F32, rider=None):
    tk, m = a.shape
    n = b.shape[1]
    slab = n // N_CHIPS
    per_block = bn // slab if chip_major else 1
    bk = max(c for c in range(LANES, min(bk, tk) + 1, LANES) if tk % c == 0)
    nk = tk // bk
    blk = (per_block, bm, slab) if chip_major else (bm, bn)

    def body(a_ref, b_ref, o_ref, acc_ref):
        k = pl.program_id(2)
        if chip_major:
            parts = [_dot_tn(a_ref[...], b_ref[:, s * slab:(s + 1) * slab]) for s in range(per_block)]
        else:
            parts = [_dot_tn(a_ref[...], b_ref[...])]

        @pl.when(k == 0)
        def _():
            for s, part in enumerate(parts):
                if chip_major:
                    acc_ref[s] = part
                else:
                    acc_ref[...] = part

        @pl.when(k > 0)
        def _():
            for s, part in enumerate(parts):
                if chip_major:
                    acc_ref[s] += part
                else:
                    acc_ref[...] += part

        @pl.when(k == nk - 1)
        def _():
            o_ref[...] = acc_ref[...].astype(out_dtype)

    if chip_major:
        out_spec = pl.BlockSpec(blk, lambda i, j, k: (j, i, 0))
        out_shape = jax.ShapeDtypeStruct((N_CHIPS, m, slab), out_dtype)
    else:
        out_spec = pl.BlockSpec(blk, lambda i, j, k: (i, j))
        out_shape = jax.ShapeDtypeStruct((m, n), out_dtype)
    (out,), carried = _hosted_call(
        body, (a, b), name=name, grid=(m // bm, n // bn, nk),
        in_specs=[pl.BlockSpec((bk, bm), lambda i, j, k: (k, i)), pl.BlockSpec((bk, bn), lambda i, j, k: (k, j))],
        out_specs=[out_spec], out_shape=[out_shape], scratch_shapes=[pltpu.VMEM(blk, F32)],
        sem=("parallel", "parallel", "arbitrary"), rider=rider)
    return out if rider is None else (out, carried)


_LATE = ("w_out", "w_ff1", "w_ff2")
_EARLY = ("w_in", "w_uq", "w_ukv")


def _local_step(x, ctx, tgt, modv, lg, g_attn, g_ffn, g_fin, g_ret, g_q, g_kv, w_in, w_uq, w_ukv, late, place=None,
                *, tm=256, tq=256, chunk=256):
    nex, seq, d = x.shape
    ctx_len = ctx.shape[1]
    t_lat = nex * seq
    tm = min(tm, seq)
    x2 = x.reshape(t_lat, d)
    ctx2 = ctx.reshape(nex * ctx_len, d)
    tgt2 = tgt.reshape(t_lat, d)
    tm_fwd = min(2 * tm, seq)
    cos_t, sin_t = _rope_tables(seq, tm)
    dims = dict(t_lat=t_lat, seq=seq, ctx_len=ctx_len)
    alone = place is None

    (hb, pg, rq, rk, rv, nq, nkv, q, k, v), crossed_a = _pre_fwd(
        x2, ctx2, modv, g_attn, w_in, g_q, g_kv, w_uq, w_ukv, *_rope_tables(seq, tm_fwd), seq=seq, tm=tm_fwd,
        rider=None if alone else _gather_ici_rider([late[2]]))
    (yret, osum), got = _ret_fwd_pair(
        rq, rk, rv, pg, lg, g_ret, chunk=min(2 * chunk, seq), **dims,
        rider=None if alone else _merge_riders(_gather_d2d_rider(crossed_a), _gather_ici_rider([late[3]])))
    (ymla, lse), got_rest = _mla_fwd(
        q, k, v, tq=tq, **dims,
        rider=None if alone else _merge_riders(_gather_rider([late[0], late[1]], staged=True), _gather_d2d_rider(got[1:])))
    w_out, w_ff1, w_ff2a, w_ff2b = late if alone else (got_rest[0], got_rest[1], got[0], got_rest[2])
    mix, act, du, h2, df, dmo, dmix, dxm, st_post = _post(yret, ymla, x2, tgt2, modv, g_ffn, g_fin, w_out.reshape(d, d),
                                                         w_ff1, w_ff2a, w_ff2b, seq=seq, tm=min(tm, 256))
    kw = dict(bm=1024, bn=1024, bk=2048, out_dtype=BF16)
    g_ff2 = _matmul_tn(act, df, chip_major=False, name="gw_ff2", **kw).reshape(N_CHIPS, D_FF // N_CHIPS, d)
    if alone:
        g_ff1 = _matmul_tn(h2, du, chip_major=True, name="gw_ff1", **kw)
        g_out = _matmul_tn(mix, dmo, chip_major=False, name="gw_out", **kw).reshape(N_CHIPS, d // N_CHIPS, d)
        (dq_m, dkl, dkc, dvl, dvc), _ = _mla_bwd(q, k, v, ymla, lse, dmix, tq=tq, **dims)
        (drq, drk, drv, dkc_r, dvc_r, drg, st_ret), _ = _ret_bwd_pair(rq, rk, rv, pg, osum, dmix, lg, g_ret, chunk=chunk,
                                                                      **dims)
        late_out = [g_out, g_ff1, g_ff2]
    else:
        core, slot = place
        g_ff1, x_ff2 = _matmul_tn(h2, du, chip_major=True, name="gw_ff1", rider=_exchange_rider([g_ff2]), **kw)
        g_out, x_ff1 = _matmul_tn(mix, dmo, chip_major=False, name="gw_out", rider=_exchange_rider([g_ff1]), **kw)
        g_out = g_out.reshape(N_CHIPS, d // N_CHIPS, d)
        p_ff2 = _add_half(g_ff2, x_ff2[0], core, "add_half_w_ff2")
        p_ff1 = _add_half(g_ff1, x_ff1[0], core, "add_half_w_ff1")
        (dq_m, dkl, dkc, dvl, dvc), (l_ff2, l_ff1, x_out) = _mla_bwd(
            q, k, v, ymla, lse, dmix, tq=min(seq, 512), **dims,
            rider=_merge_riders(_scatter_rider([p_ff2, p_ff1]), _exchange_rider([g_out])))
        p_out = _add_half(g_out, x_out, core, "add_half_w_out")
        m_ff2 = _sum_chips(p_ff2, l_ff2, slot, "sum_chips_w_ff2")
        m_ff1 = _sum_chips(p_ff1, l_ff1, slot, "sum_chips_w_ff1")
        (drq, drk, drv, dkc_r, dvc_r, drg, st_ret), (l_out,) = _ret_bwd_pair(
            rq, rk, rv, pg, osum, dmix, lg, g_ret, chunk=chunk, **dims, rider=_scatter_rider([p_out]))
        late_out = [_sum_chips(p_out, l_out, slot, "sum_chips_w_out"), m_ff1, m_ff2]
    (dpb, dqf, dkvf, gx, st_pre), _ = _pre_bwd(
        x2, ctx2, modv, g_attn, pg, drq, drk, dkc_r, drv, dvc_r, drg, dq_m, dkl, dkc, dvl, dvc, dxm, w_in, g_q, g_kv,
        w_uq, w_ukv, cos_t, sin_t, seq=seq, tm=tm)
    g_early = [
        _matmul_tn(dpb, hb, bm=IN_PAD // 2, bn=d, bk=1536, chip_major=False, name="gw_in"),
        _matmul_tn(dqf, nq, bm=HEADS * MLA_HEAD, bn=Q_LORA, bk=1536, chip_major=False, name="gw_uq"),
        _matmul_tn(nkv, dkvf, bm=KV_LORA, bn=HEADS * 256, bk=1536, chip_major=True, name="gw_ukv"),
    ]
    return gx.reshape(nex, seq, d), g_early, late_out, st_post, st_ret, st_pre


_ANY = pl.BlockSpec(memory_space=pl.ANY)
_VMEM = pl.BlockSpec(memory_space=pltpu.VMEM)
_OFFSETS = tuple((dx, dy, dc) for dx in (0, 1) for dy in (0, 1) for dc in (0, 1))[1:]
_CHIP_OFFSETS = ((1, 0), (0, 1), (1, 1))


def _place():
    return lax.axis_index("x"), lax.axis_index("y"), lax.axis_index("c")


def _flip(v, d):
    return 1 - v if d else v


def _gather8_rider(a, in_vmem=True):
    def copies(a_ref, o_ref, send, recv):
        x, y, z = _place()
        me = 4 * x + 2 * y + z
        out = []
        for k, (dx, dy, dc) in enumerate(_OFFSETS):
            peer = (_flip(x, dx), _flip(y, dy), _flip(z, dc))
            landing = o_ref.at[4 * peer[0] + 2 * peer[1] + peer[2]]
            out.append((
                pltpu.make_async_remote_copy(src_ref=a_ref, dst_ref=o_ref.at[me], send_sem=send.at[k],
                                             recv_sem=recv.at[k], device_id=peer, device_id_type=MESH),
                pltpu.make_async_remote_copy(src_ref=a_ref, dst_ref=landing, send_sem=send.at[k],
                                             recv_sem=recv.at[k], device_id=peer, device_id_type=MESH)))
        return me, out

    def start(ins, outs, sems):
        me, cps = copies(ins[0], outs[0], sems[0], sems[1])
        pltpu.make_async_copy(ins[0], outs[0].at[me], sems[2]).start()
        for out_cp, _ in cps:
            out_cp.start()

    def finish(ins, outs, sems):
        me, cps = copies(ins[0], outs[0], sems[0], sems[1])
        for out_cp, in_cp in cps:
            in_cp.wait_recv()
            out_cp.wait_send()
        pltpu.make_async_copy(ins[0], outs[0].at[me], sems[2]).wait()

    spec = [_VMEM] if in_vmem else [_ANY]
    return _Rider([a], [jax.ShapeDtypeStruct((N_DEV,) + a.shape, a.dtype)],
                  [pltpu.SemaphoreType.DMA((7,)), pltpu.SemaphoreType.DMA((7,)), pltpu.SemaphoreType.DMA],
                  start, finish, in_specs=spec, out_specs=spec)


def _merge_riders(*riders):
    ins, outs, sems, in_specs, out_specs, aliases, cuts = [], [], [], [], [], {}, []
    for r in riders:
        cuts.append((len(ins), len(outs), len(sems)))
        aliases.update({len(ins) + i: len(outs) + j for i, j in r.aliases.items()})
        ins += r.ins
        outs += r.out_shapes
        sems += r.sems
        in_specs += r.in_specs
        out_specs += r.out_specs

    def part(r, cut, r_ins, r_outs, r_sems):
        return (r_ins[cut[0]:cut[0] + len(r.ins)], r_outs[cut[1]:cut[1] + len(r.out_shapes)],
                r_sems[cut[2]:cut[2] + len(r.sems)])

    def start(r_ins, r_outs, r_sems):
        for r, cut in zip(riders, cuts):
            r.start(*part(r, cut, r_ins, r_outs, r_sems))

    def finish(r_ins, r_outs, r_sems):
        for r, cut in zip(riders, cuts):
            r.finish(*part(r, cut, r_ins, r_outs, r_sems))

    def middle(r_ins, r_outs, r_sems):
        for r, cut in zip(riders, cuts):
            if r.middle is not None:
                r.middle(*part(r, cut, r_ins, r_outs, r_sems))

    return _Rider(ins, outs, sems, start, finish, aliases=aliases, in_specs=in_specs, out_specs=out_specs,
                  middle=middle if any(r.middle is not None for r in riders) else None)


def _allgather8(a, name):
    return _run_rider(_gather8_rider(a), name)[0]


BF16_TILE_ROWS = 16


def _half(o, slot, which):
    r2 = o.shape[1] // 2
    if r2 % BF16_TILE_ROWS == 0:
        return o.at[slot, pl.ds(which * r2, r2)]
    c2 = o.shape[2] // 2
    assert c2 % LANES == 0
    return o.at[slot, :, pl.ds(which * c2, c2)]


def _gather_send(o_refs, send, recv):
    x, y, z = _place()
    chip = 2 * x + y
    for a, o in enumerate(o_refs):
        r2 = o.shape[1] // 2
        mine = _half(o, chip, z)
        for k, (dx, dy) in enumerate(_CHIP_OFFSETS):
            pltpu.make_async_remote_copy(
                src_ref=mine, dst_ref=mine, send_sem=send.at[a, k], recv_sem=recv.at[a, k],
                device_id=(_flip(x, dx), _flip(y, dy), z), device_id_type=MESH).start()


def _gather_landed(o_refs, send, recv, then=None):
    x, y, z = _place()
    chip = 2 * x + y
    for a, o in enumerate(o_refs):
        for k, (dx, dy) in enumerate(_CHIP_OFFSETS):
            landed = _half(o, 2 * _flip(x, dx) + _flip(y, dy), z)
            pltpu.make_async_remote_copy(
                src_ref=landed, dst_ref=landed, send_sem=send.at[a, k], recv_sem=recv.at[a, k],
                device_id=(_flip(x, dx), _flip(y, dy), z), device_id_type=MESH).wait_recv()
            if then is not None:
                then(a, k, landed)
    for a, o in enumerate(o_refs):
        mine = _half(o, chip, z)
        for k, (dx, dy) in enumerate(_CHIP_OFFSETS):
            pltpu.make_async_remote_copy(
                src_ref=mine, dst_ref=mine, send_sem=send.at[a, k], recv_sem=recv.at[a, k],
                device_id=(_flip(x, dx), _flip(y, dy), z), device_id_type=MESH).wait_send()


def _pass_on(o_refs, fsend, frecv, a, k, landed):
    x, y, z = _place()
    pltpu.make_async_remote_copy(
        src_ref=landed, dst_ref=landed, send_sem=fsend.at[a, k], recv_sem=frecv.at[a, k],
        device_id=(x, y, 1 - z), device_id_type=MESH).start()


def _passed_on(o_refs, fsend, frecv):
    x, y, z = _place()
    for a, o in enumerate(o_refs):
        for k, (dx, dy) in enumerate(_CHIP_OFFSETS):
            other = 2 * _flip(x, dx) + _flip(y, dy)
            got = _half(o, other, 1 - z)
            gave = _half(o, other, z)
            pltpu.make_async_remote_copy(
                src_ref=got, dst_ref=got, send_sem=fsend.at[a, k], recv_sem=frecv.at[a, k],
                device_id=(x, y, 1 - z), device_id_type=MESH).wait_recv()
            pltpu.make_async_remote_copy(
                src_ref=gave, dst_ref=gave, send_sem=fsend.at[a, k], recv_sem=frecv.at[a, k],
                device_id=(x, y, 1 - z), device_id_type=MESH).wait_send()


def _gather_finish(o_refs, send, recv, fsend, frecv):
    _gather_landed(o_refs, send, recv, functools.partial(_pass_on, o_refs, fsend, frecv))
    _passed_on(o_refs, fsend, frecv)


class _Rider:
    def __init__(self, ins, out_shapes, sems, start, finish, aliases=None, in_specs=None, out_specs=None, middle=None):
        self.ins, self.out_shapes, self.sems = list(ins), list(out_shapes), list(sems)
        self.start, self.finish, self.aliases = start, finish, dict(aliases or {})
        self.middle = middle
        self.in_specs = list(in_specs) if in_specs else [_ANY] * len(self.ins)
        self.out_specs = list(out_specs) if out_specs else [_ANY] * len(self.out_shapes)


def _run_rider(rider, name):
    r_in, r_out = len(rider.ins), len(rider.out_shapes)

    def body(*refs):
        ins, outs, sems = refs[:r_in], refs[r_in:r_in + r_out], refs[r_in + r_out:]
        rider.start(ins, outs, sems)
        if rider.middle is not None:
            rider.middle(ins, outs, sems)
        rider.finish(ins, outs, sems)

    return pl.pallas_call(
        body, name=name, in_specs=rider.in_specs, out_specs=rider.out_specs, out_shape=rider.out_shapes,
        input_output_aliases=rider.aliases, scratch_shapes=rider.sems,
    )(*rider.ins)


def _hosted_call(body, args, *, name, grid, in_specs, out_specs, out_shape, scratch_shapes=(), sem, rider=None,
                 prefetch=()):
    scratch_shapes = list(scratch_shapes)
    n_pf, n_in, n_out, n_sc = len(prefetch), len(in_specs), len(out_specs), len(scratch_shapes)
    r_in, r_out = (len(rider.ins), len(rider.out_shapes)) if rider else (0, 0)
    last = tuple(g - 1 for g in grid)

    def hosted(*refs):
        p = 0
        parts = []
        for cnt in (n_pf, n_in, r_in, n_out, r_out, n_sc):
            parts.append(refs[p:p + cnt])
            p += cnt
        pf, ins, r_ins, outs, r_outs, scratch = parts
        sems = refs[p:]
        ids = [pl.program_id(a) for a in range(len(grid))]
        is_first = functools.reduce(jnp.logical_and, [i == 0 for i in ids])
        is_last = functools.reduce(jnp.logical_and, [i == e for i, e in zip(ids, last)])

        @pl.when(is_first)
        def _():
            rider.start(r_ins, r_outs, sems)

        if rider.middle is not None:
            linear = functools.reduce(lambda acc, ig: acc * ig[1] + ig[0], zip(ids, grid), 0)

            @pl.when(linear == math.prod(grid) * 3 // 4)
            def _():
                rider.middle(r_ins, r_outs, sems)

        body(*pf, *ins, *outs, *scratch)

        @pl.when(is_last)
        def _():
            rider.finish(r_ins, r_outs, sems)

    if rider is None:
        kern, all_in, all_out, shapes, scratch, aliases, extra = body, list(in_specs), list(out_specs), list(out_shape), \
            scratch_shapes, {}, []
    else:
        kern, all_in, all_out = hosted, list(in_specs) + rider.in_specs, list(out_specs) + rider.out_specs
        shapes, scratch, extra = list(out_shape) + rider.out_shapes, scratch_shapes + rider.sems, rider.ins
        aliases = {n_pf + n_in + i: n_out + j for i, j in rider.aliases.items()}
        sem = ("arbitrary",) * len(grid)
    if prefetch:
        spec = dict(grid_spec=pltpu.PrefetchScalarGridSpec(
            num_scalar_prefetch=n_pf, grid=grid, in_specs=all_in, out_specs=all_out, scratch_shapes=scratch))
    else:
        spec = dict(grid=grid, in_specs=all_in, out_specs=all_out, scratch_shapes=scratch)
    res = pl.pallas_call(kern, name=name, out_shape=shapes, input_output_aliases=aliases,
                         compiler_params=_params(sem, VMEM_LIMIT), **spec)(*prefetch, *args, *extra)
    return list(res[:n_out]), list(res[n_out:])


def _gather_rider(ws, staged=False):
    n = len(ws)
    shapes = [jax.ShapeDtypeStruct(w.shape, w.dtype) for w in ws]
    sems = [pltpu.SemaphoreType.DMA((n, 3))] * 4
    aliases = {a: a for a in range(n)}

    def start(ins, outs, s):
        _gather_send(outs, s[0], s[1])

    if not staged:
        return _Rider(ws, shapes, sems, start, lambda ins, outs, s: _gather_finish(outs, *s), aliases=aliases)
    return _Rider(
        ws, shapes, sems, start, lambda ins, outs, s: _passed_on(outs, s[2], s[3]), aliases=aliases,
        middle=lambda ins, outs, s: _gather_landed(outs, s[0], s[1], functools.partial(_pass_on, outs, s[2], s[3])))


def _gather_ici_rider(ws):
    n = len(ws)
    return _Rider(
        ws, [jax.ShapeDtypeStruct(w.shape, w.dtype) for w in ws], [pltpu.SemaphoreType.DMA((n, 3))] * 2,
        lambda ins, outs, sems: _gather_send(outs, sems[0], sems[1]),
        lambda ins, outs, sems: _gather_landed(outs, sems[0], sems[1]),
        aliases={a: a for a in range(n)})


def _gather_d2d_rider(ws):
    n = len(ws)

    def start(ins, outs, sems):
        x, y, z = _place()
        for a, o in enumerate(outs):
            for k, (dx, dy) in enumerate(_CHIP_OFFSETS):
                _pass_on(outs, sems[0], sems[1], a, k, _half(o, 2 * _flip(x, dx) + _flip(y, dy), z))

    return _Rider(
        ws, [jax.ShapeDtypeStruct(w.shape, w.dtype) for w in ws], [pltpu.SemaphoreType.DMA((n, 3))] * 2,
        start, lambda ins, outs, sems: _passed_on(outs, sems[0], sems[1]), aliases={a: a for a in range(n)})


def _copies_rider(ins, out_shapes, sem_shape, make):
    def start(r_ins, r_outs, sems):
        for cp in make(r_ins, r_outs, sems[0], sems[1]):
            cp.start()

    def finish(r_ins, r_outs, sems):
        for cp in make(r_ins, r_outs, sems[0], sems[1]):
            cp.wait()

    return _Rider(ins, out_shapes, [pltpu.SemaphoreType.DMA(sem_shape)] * 2, start, finish)


def _exchange_rider(gs):
    def make(g_refs, r_refs, send, recv):
        x, y, z = _place()
        return [pltpu.make_async_remote_copy(
            src_ref=g.at[:, pl.ds((1 - z) * (g.shape[1] // 2), g.shape[1] // 2)], dst_ref=r, send_sem=send.at[a],
            recv_sem=recv.at[a], device_id=(x, y, 1 - z), device_id_type=MESH)
            for a, (g, r) in enumerate(zip(g_refs, r_refs))]

    shapes = [jax.ShapeDtypeStruct((g.shape[0], g.shape[1] // 2, g.shape[2]), g.dtype) for g in gs]
    return _copies_rider(gs, shapes, (len(gs),), make)


def _add_half(g, recv, core, name):
    s, r, c = g.shape
    r2 = r // 2
    rb = r2
    for cand in (256, 128, 64):
        if r2 % cand == 0:
            rb = cand
            break
    g4 = g.reshape(s, 2, r2, c)

    def body(core_ref, g_ref, r_ref, o_ref):
        o_ref[...] = (g_ref[...].astype(F32) + r_ref[...].astype(F32)).astype(BF16)

    return pl.pallas_call(
        body, name=name,
        grid_spec=pltpu.PrefetchScalarGridSpec(
            num_scalar_prefetch=1, grid=(s, r2 // rb),
            in_specs=[pl.BlockSpec((None, None, rb, c), lambda i, j, cr: (i, cr[0], j, 0)),
                      pl.BlockSpec((None, rb, c), lambda i, j, cr: (i, j, 0))],
            out_specs=pl.BlockSpec((None, rb, c), lambda i, j, cr: (i, j, 0))),
        out_shape=jax.ShapeDtypeStruct((s, r2, c), BF16),
        compiler_params=_params(("parallel", "parallel")),
    )(core, g4, recv)


def _scatter_rider(ps):
    def make(p_refs, o_refs, send, recv):
        x, y, z = _place()
        copies = []
        for a, (p, o) in enumerate(zip(p_refs, o_refs)):
            for k, (dx, dy) in enumerate(_CHIP_OFFSETS):
                other = 2 * _flip(x, dx) + _flip(y, dy)
                copies.append(pltpu.make_async_remote_copy(
                    src_ref=p.at[other], dst_ref=o.at[k], send_sem=send.at[a, k], recv_sem=recv.at[a, k],
                    device_id=(_flip(x, dx), _flip(y, dy), z), device_id_type=MESH))
        return copies

    shapes = [jax.ShapeDtypeStruct((3,) + p.shape[1:], p.dtype) for p in ps]
    return _copies_rider(ps, shapes, (len(ps), 3), make)


def _sum_chips(p, landed, chip, name):
    _, r2, c = p.shape
    rb = r2
    for cand in (256, 128, 64):
        if r2 % cand == 0:
            rb = cand
            break

    def body(s_ref, p_ref, l_ref, o_ref):
        acc = p_ref[...].astype(F32)
        for k in range(3):
            acc = acc + l_ref[k].astype(F32)
        o_ref[...] = acc

    return pl.pallas_call(
        body, name=name,
        grid_spec=pltpu.PrefetchScalarGridSpec(
            num_scalar_prefetch=1, grid=(r2 // rb,),
            in_specs=[pl.BlockSpec((None, rb, c), lambda i, s: (s[0], i, 0)),
                      pl.BlockSpec((3, rb, c), lambda i, s: (0, i, 0))],
            out_specs=pl.BlockSpec((rb, c), lambda i, s: (i, 0))),
        out_shape=jax.ShapeDtypeStruct((r2, c), F32),
        compiler_params=_params(("parallel",)),
    )(chip, p, landed)


def _swap_rider(hs):
    def make(h_refs, o_refs, send, recv):
        x, y, z = _place()
        return [pltpu.make_async_remote_copy(
            src_ref=h, dst_ref=o, send_sem=send.at[a], recv_sem=recv.at[a], device_id=(x, y, 1 - z),
            device_id_type=MESH) for a, (h, o) in enumerate(zip(h_refs, o_refs))]

    return _copies_rider(hs, [jax.ShapeDtypeStruct(h.shape, h.dtype) for h in hs], (len(hs),), make)


def _reduce_scatter_vmem(gs, rows, rider, name):
    n = len(gs)
    r_in, r_out = len(rider.ins), len(rider.out_shapes)
    halves = [(r // 2, g.shape[-1]) for g, (r, _) in zip(gs, rows)]

    def body(*refs):
        p = 0
        parts = []
        for cnt in (n, r_in, n, n, r_out, n, n, n, 6):
            parts.append(refs[p:p + cnt])
            p += cnt
        g_refs, r_ins, mine, theirs, r_outs, recv, part, land, sems = parts
        r_sems = refs[p:]
        xs, xr, ss, sr, ws, wr = sems
        x, y, z = _place()
        chip = 2 * x + y
        sib = (x, y, 1 - z)
        rider.start(r_ins, r_outs, r_sems)

        def half_of(a, s, which):
            r2 = halves[a][0]
            if len(g_refs[a].shape) == 3:
                return g_refs[a].at[s, pl.ds(pl.multiple_of(which * r2, 8), r2)]
            return g_refs[a].at[pl.ds(pl.multiple_of(s * rows[a][1] + which * r2, 8), r2)]

        exchange = [pltpu.make_async_remote_copy(
            src_ref=half_of(a, s, 1 - z), dst_ref=recv[a].at[s], send_sem=xs.at[a, s], recv_sem=xr.at[a, s],
            device_id=sib, device_id_type=MESH) for a in range(n) for s in range(N_CHIPS)]
        for cp in exchange:
            cp.start()
        for cp in exchange:
            cp.wait()
        for a in range(n):
            for s in range(N_CHIPS):
                part[a][s] = (half_of(a, s, z)[...] + recv[a][s]).astype(BF16)
        scatter = []
        for a in range(n):
            for k, (dx, dy) in enumerate(_CHIP_OFFSETS):
                other = 2 * _flip(x, dx) + _flip(y, dy)
                scatter.append(pltpu.make_async_remote_copy(
                    src_ref=part[a].at[other], dst_ref=land[a].at[k], send_sem=ss.at[a, k], recv_sem=sr.at[a, k],
                    device_id=(_flip(x, dx), _flip(y, dy), z), device_id_type=MESH))
        for cp in scatter:
            cp.start()
        for cp in scatter:
            cp.wait()
        for a in range(n):
            acc = part[a][chip].astype(F32)
            for k in range(3):
                acc = acc + land[a][k].astype(F32)
            mine[a][...] = acc
        swap = [pltpu.make_async_remote_copy(
            src_ref=mine[a], dst_ref=theirs[a], send_sem=ws.at[a], recv_sem=wr.at[a], device_id=sib,
            device_id_type=MESH) for a in range(n)]
        for cp in swap:
            cp.start()
        for cp in swap:
            cp.wait()
        rider.finish(r_ins, r_outs, r_sems)

    half_shapes = [jax.ShapeDtypeStruct(h, F32) for h in halves]
    res = pl.pallas_call(
        body, name=name, in_specs=[_VMEM] * n + rider.in_specs, out_specs=[_VMEM] * (2 * n) + rider.out_specs,
        out_shape=half_shapes + half_shapes + rider.out_shapes,
        scratch_shapes=[pltpu.VMEM((N_CHIPS,) + h, F32) for h in halves] + [pltpu.VMEM((N_CHIPS,) + h, BF16) for h in halves]
        + [pltpu.VMEM((3,) + h, BF16) for h in halves]
        + [pltpu.SemaphoreType.DMA((n, N_CHIPS))] * 2 + [pltpu.SemaphoreType.DMA((n, 3))] * 2
        + [pltpu.SemaphoreType.DMA((n,))] * 2 + rider.sems,
        input_output_aliases={n + i: 2 * n + j for i, j in rider.aliases.items()},
        compiler_params=_params(None, VMEM_LIMIT),
    )(*gs, *rider.ins)
    return list(res[:n]), list(res[n:2 * n]), list(res[2 * n:])


SMALL_ROWS = 32
PACK_ROWS = 16


def _pack_small(st_post, st_ret, st_pre):
    d = st_post.shape[2]

    def body(po_ref, re_ref, pr_ref, o_ref):
        o_ref[...] = jnp.zeros(o_ref.shape, F32)
        o_ref[0:1, :] = pr_ref[0, 2:3, :] + pr_ref[1, 2:3, :] + pr_ref[2, 2:3, :]
        o_ref[1:2, :] = po_ref[0, 4:5, :] + po_ref[1, 4:5, :]
        o_ref[2:3, :] = po_ref[0, 5:6, :] + po_ref[1, 5:6, :]
        o_ref[3:4, 0:512] = re_ref[0, 0:1, :] + re_ref[1, 0:1, :]
        o_ref[4:5, :] = pr_ref[0, 3:4, :] + pr_ref[1, 3:4, :] + pr_ref[2, 3:4, :]
        o_ref[5:6, :] = pr_ref[0, 4:5, :] + pr_ref[1, 4:5, :] + pr_ref[2, 4:5, :]
        lane = lax.broadcasted_iota(jnp.int32, (1, LANES), 1)
        for row, src in ((6, 1), (10, 2)):
            acc = jnp.zeros((1, LANES), F32)
            for hd in range(HEADS):
                grp = re_ref[0, src:src + 1, hd * LANES:(hd + 1) * LANES] + re_ref[1, src:src + 1, hd * LANES:(hd + 1) * LANES]
                acc = acc + jnp.where(lane == hd, grp, 0.0)
            o_ref[row:row + 1, 0:LANES] = acc
        o_ref[7:8, :] = po_ref[0, 6:7, :] + po_ref[1, 6:7, :]
        o_ref[8:9, :] = pr_ref[2, 0:1, :]
        o_ref[9:10, :] = pr_ref[2, 1:2, :]
        for e in range(2):
            b = 12 + 6 * e
            o_ref[b:b + 1, :] = pr_ref[e, 0:1, :]
            o_ref[b + 1:b + 2, :] = pr_ref[e, 1:2, :]
            o_ref[b + 2:b + 3, :] = po_ref[e, 3:4, :]
            o_ref[b + 3:b + 4, :] = po_ref[e, 0:1, :]
            o_ref[b + 4:b + 5, :] = po_ref[e, 1:2, :]
            o_ref[b + 5:b + 6, :] = po_ref[e, 2:3, :]

    return pl.pallas_call(body, name="pack_small", out_shape=jax.ShapeDtypeStruct((SMALL_ROWS, d), F32))(st_post, st_ret, st_pre)


def _small_reduce(gathered):
    d = gathered.shape[2]

    def body(g_ref, o_ref):
        tot = g_ref[0, 0:PACK_ROWS, :]
        for dev in range(1, N_DEV):
            tot = tot + g_ref[dev, 0:PACK_ROWS, :]
        o_ref[0:PACK_ROWS, :] = tot
        for j in range(6):
            acc = g_ref[0, 12 + j:13 + j, :] + g_ref[0, 18 + j:19 + j, :]
            for dev in range(1, N_DEV):
                acc = acc + g_ref[dev, 12 + j:13 + j, :] + g_ref[dev, 18 + j:19 + j, :]
            if j < 2:
                acc = acc + o_ref[8 + j:9 + j, :]
            o_ref[PACK_ROWS + j:PACK_ROWS + j + 1, :] = acc
        o_ref[PACK_ROWS + 6:PACK_ROWS + 8, :] = jnp.zeros((2, d), F32)

    return pl.pallas_call(body, name="small_reduce", out_shape=jax.ShapeDtypeStruct((PACK_ROWS + 8, d), F32))(gathered)


_SMALL = (("g_attn", 0, 1024), ("g_ffn", 1, 1024), ("g_final", 2, 1024), ("g_ret", 3, 512), ("g_q_lora", 4, 384),
          ("g_kv_lora", 5, 256), ("ret_decay_fwd", 6, HEADS), ("ret_decay_bwd", 10, HEADS))
_SMALL_NAMES = tuple(s[0] for s in _SMALL) + ("c_ctx", "b_ada")


def _small_final(tot, dcc, sg8, ws, ms, vs):
    d = tot.shape[1]
    n = len(_SMALL_NAMES)

    def body(*refs):
        t_ref, dcc_ref, sg_ref = refs[0:3]
        w_refs, m_refs, v_refs = refs[3:3 + n], refs[3 + n:3 + 2 * n], refs[3 + 2 * n:3 + 3 * n]
        outs = refs[3 + 3 * n:]
        g_refs, d_refs, mo_refs, vo_refs = outs[0:n], outs[n:2 * n], outs[2 * n:3 * n], outs[3 * n:4 * n]
        l_ref = outs[4 * n]

        def update(i, g, sl=None):
            pick = (lambda r: r[...]) if sl is None else (lambda r: r[:, sl])
            dl, mn, vn = _adam_math(pick(w_refs[i]), g, pick(m_refs[i]), pick(v_refs[i]))
            if sl is None:
                g_refs[i][...], d_refs[i][...], mo_refs[i][...], vo_refs[i][...] = g, dl, mn, vn
            else:
                g_refs[i][:, sl], d_refs[i][:, sl], mo_refs[i][:, sl], vo_refs[i][:, sl] = g, dl, mn, vn

        for i, (name, row, width) in enumerate(_SMALL):
            g = t_ref[row:row + 1, 0:width]
            if name == "ret_decay_fwd":
                g = g * sg_ref[0:1, 0:width]
            elif name == "ret_decay_bwd":
                g = g * sg_ref[1:2, 0:width]
            update(i, g)
        i_cc, i_b = n - 2, n - 1
        cc = w_refs[i_cc][...]
        s = 1.0 / (1.0 + jnp.exp(-cc))
        dsilu = dcc_ref[0, 0:1, :] + dcc_ref[2, 0:1, :] + dcc_ref[4, 0:1, :] + dcc_ref[6, 0:1, :]
        update(i_cc, dsilu * (s * (1.0 + cc * (1.0 - s))))
        for j in range(6):
            update(i_b, t_ref[PACK_ROWS + j:PACK_ROWS + j + 1, :], pl.ds(j * d, d))
        l_ref[...] = jnp.broadcast_to((0.5 / d) * jnp.sum(t_ref[7:8, :], keepdims=True), l_ref.shape)

    shapes = [jax.ShapeDtypeStruct(a.shape, F32) for a in ws]
    outs = pl.pallas_call(
        body, name="small_final", out_shape=shapes * 4 + [jax.ShapeDtypeStruct((8, LANES), F32)],
    )(tot, dcc, sg8, *ws, *ms, *vs)
    return outs[0:n], outs[n:2 * n], outs[2 * n:3 * n], outs[3 * n:4 * n], outs[4 * n]


_WEIGHTS = ("c_ctx", "w_ada", "b_ada", "g_attn", "g_ffn", "w_in", "ret_decay_fwd", "ret_decay_bwd", "g_ret", "g_q_lora",
            "w_uq", "g_kv_lora", "w_ukv", "w_out", "w_ff1", "w_ff2", "g_final")
_BIG = ("w_in", "w_uq", "w_ukv", "w_out", "w_ff1", "w_ff2")
_TRANSPOSED = ("w_in", "w_uq")


def kernel(x, c, ctx, c_ctx, w_ada, b_ada, g_attn, g_ffn, w_in, ret_decay_fwd, ret_decay_bwd, g_ret, g_q_lora, w_uq, g_kv_lora, w_ukv, w_out, w_ff1, w_ff2, g_final, loss_target, m_c_ctx, m_w_ada, m_b_ada, m_g_attn, m_g_ffn, m_w_in, m_ret_decay_fwd, m_ret_decay_bwd, m_g_ret, m_g_q_lora, m_w_uq, m_g_kv_lora, m_w_ukv, m_w_out, m_w_ff1, m_w_ff2, m_g_final, v_c_ctx, v_w_ada, v_b_ada, v_g_attn, v_g_ffn, v_w_in, v_ret_decay_fwd, v_ret_decay_bwd, v_g_ret, v_g_q_lora, v_w_uq, v_g_kv_lora, v_w_ukv, v_w_out, v_w_ff1, v_w_ff2, v_g_final):
    w = dict(c_ctx=c_ctx, w_ada=w_ada, b_ada=b_ada, g_attn=g_attn, g_ffn=g_ffn, w_in=w_in, ret_decay_fwd=ret_decay_fwd,
             ret_decay_bwd=ret_decay_bwd, g_ret=g_ret, g_q_lora=g_q_lora, w_uq=w_uq, g_kv_lora=g_kv_lora, w_ukv=w_ukv,
             w_out=w_out, w_ff1=w_ff1, w_ff2=w_ff2, g_final=g_final)
    m = dict(c_ctx=m_c_ctx, w_ada=m_w_ada, b_ada=m_b_ada, g_attn=m_g_attn, g_ffn=m_g_ffn, w_in=m_w_in,
             ret_decay_fwd=m_ret_decay_fwd, ret_decay_bwd=m_ret_decay_bwd, g_ret=m_g_ret, g_q_lora=m_g_q_lora, w_uq=m_w_uq,
             g_kv_lora=m_g_kv_lora, w_ukv=m_w_ukv, w_out=m_w_out, w_ff1=m_w_ff1, w_ff2=m_w_ff2, g_final=m_g_final)
    v = dict(c_ctx=v_c_ctx, w_ada=v_w_ada, b_ada=v_b_ada, g_attn=v_g_attn, g_ffn=v_g_ffn, w_in=v_w_in,
             ret_decay_fwd=v_ret_decay_fwd, ret_decay_bwd=v_ret_decay_bwd, g_ret=v_g_ret, g_q_lora=v_g_q_lora, w_uq=v_w_uq,
             g_kv_lora=v_g_kv_lora, w_ukv=v_w_ukv, w_out=v_w_out, w_ff1=v_w_ff1, w_ff2=v_w_ff2, g_final=v_g_final)
    xi, yi, ci = lax.axis_index("x"), lax.axis_index("y"), lax.axis_index("c")
    chip = 2 * xi + yi
    dev = 2 * chip + ci
    nex, seq, d = x.shape
    n_ada = w_ada.shape[2]

    dec = jnp.zeros((8, LANES), F32).at[0, :HEADS].set(ret_decay_fwd[0]).at[1, :HEADS].set(ret_decay_bwd[0])
    lg8, sg8 = _decay_prep(dec)
    lg = lg8[:2, :HEADS]

    def shard_of(t, k):
        return t[k][0].T if k in _TRANSPOSED else t[k][0]

    shard = {k: shard_of(w, k) for k in _BIG}
    head_rows = MLA_NOPE + MLA_ROPE
    shard["w_uq"] = jnp.pad(shard["w_uq"], ((0, MLA_HEAD - head_rows), (0, 0)))
    slot = chip.reshape(1).astype(jnp.int32)
    core = ci.reshape(1).astype(jnp.int32)
    slots = {k: _cast_into_slot(shard[k], slot, "cast_" + k)[0] for k in _BIG if k not in ("w_ff1", "w_ff2")}
    half_ff = shard["w_ff2"].shape[0] // 2
    slots["w_ff2a"] = _cast_into_slot(shard["w_ff2"], slot, "cast_w_ff2a", rows=(0, half_ff))[0]
    slots["w_ff2b"] = _cast_into_slot(shard["w_ff2"], slot, "cast_w_ff2b", rows=(half_ff, half_ff))[0]
    slots["w_ff1"], (w_in_x, w_uq_x, w_ukv_x, c8) = _cast_into_slot(
        shard["w_ff1"], slot, "cast_w_ff1",
        rider=_merge_riders(_gather_ici_rider([slots[k] for k in _EARLY]),
                            _gather8_rider(jnp.pad(c, ((0, 8 - nex), (0, 0))), in_vmem=False)))

    a_in = jnp.concatenate([c8[:, :nex].reshape(N_DEV * nex, d), c_ctx.reshape(1, d), jnp.zeros((7, d), F32)], axis=0)
    b_sh = lax.dynamic_slice(b_ada, (0, chip * n_ada), (1, n_ada))
    mod_sh = _mod_fwd(a_in, w_ada[0], b_sh)
    mod8, w_in_f, w_uq_k, w_ukv_k = _run_rider(
        _merge_riders(_gather8_rider(mod_sh), _gather_d2d_rider([w_in_x, w_uq_x, w_ukv_x])), "ag_early")
    w_in_k = jnp.pad(w_in_f.reshape(IN_COLS, d), ((0, IN_PAD - IN_COLS), (0, 0)))
    mod_all = mod8[0::2].transpose(1, 0, 2).reshape(a_in.shape[0], N_CHIPS * n_ada)
    mod_me = lax.dynamic_slice(mod_all, (nex * dev, 0), (nex, N_CHIPS * n_ada)).reshape(nex, 6, d)
    mod_c = mod_all[N_DEV * nex].reshape(1, 6, d)
    modv = jnp.pad(jnp.concatenate([mod_me, mod_c], axis=0), ((0, 0), (0, 2), (0, 0)))

    gx, g_early, late, st_post, st_ret, st_pre = _local_step(
        x, ctx, loss_target, modv, lg, g_attn, g_ffn, g_final.reshape(1, d), g_ret, g_q_lora, g_kv_lora,
        w_in_k, w_uq_k, w_ukv_k, [slots[k] for k in ("w_out", "w_ff1", "w_ff2a", "w_ff2b")], (core, slot))

    mine, theirs, (*late_theirs, gathered) = _reduce_scatter_vmem(
        g_early, [(IN_COLS // N_CHIPS, IN_COLS // N_CHIPS), (head_rows, MLA_HEAD), (KV_LORA, KV_LORA)],
        _merge_riders(_swap_rider(late), _gather8_rider(_pack_small(st_post, st_ret, st_pre))), "rs_early")
    tot = _small_reduce(gathered)
    dm = jnp.concatenate([
        gathered[:, 12:24].reshape(N_DEV * nex, 6 * d),
        jnp.concatenate([tot[8:10].reshape(1, 2 * d), jnp.zeros((1, 4 * d), F32)], axis=1),
        jnp.zeros((7, 6 * d), F32)], axis=0)
    dm_sh = lax.dynamic_slice(dm, (0, chip * n_ada), (dm.shape[0], n_ada))
    g_ada, da = _mod_bwd(a_in, dm_sh, w_ada[0])
    dcc = _allgather8(da[N_DEV * nex:], "ag_dcc")
    halves = dict(zip(_EARLY, zip(mine, theirs)))
    halves.update(zip(_LATE, zip(late, late_theirs)))
    grad, delta, new_m, new_v = {}, {}, {}, {}
    for k in _BIG:
        a, b = halves[k]
        res = _adamw_halves(shard_of(w, k), a, b, shard_of(m, k), shard_of(v, k), core, "adamw_" + k)
        grad[k], delta[k], new_m[k], new_v[k] = [(o.T if k in _TRANSPOSED else o).reshape(w[k].shape) for o in res]

    shp = w_ada.shape
    outs, _ = _adamw(w_ada[0], g_ada, m["w_ada"][0], v["w_ada"][0], "adamw_w_ada")
    grad["w_ada"] = g_ada.reshape(shp)
    delta["w_ada"], new_m["w_ada"], new_v["w_ada"] = [o.reshape(shp) for o in outs]
    rows = [{k: t[k].reshape(1, -1) for k in _SMALL_NAMES} for t in (w, m, v)]
    small = _small_final(tot, dcc, sg8, *[[t[k] for k in _SMALL_NAMES] for t in rows])
    for res, outs in zip((grad, delta, new_m, new_v), small[:4]):
        for k, o in zip(_SMALL_NAMES, outs):
            res[k] = o.reshape(w[k].shape)
    return (small[4][0, 0], gx, *[grad[k] for k in _WEIGHTS], *[delta[k] for k in _WEIGHTS],
            *[new_m[k] for k in _WEIGHTS], *[new_v[k] for k in _WEIGHTS])
```

```python
import functools
import math

import jax
import jax.numpy as jnp
from jax import lax
from jax.experimental import pallas as pl
from jax.experimental.pallas import tpu as pltpu

F32 = jnp.float32
BF16 = jnp.bfloat16
MESH = pl.DeviceIdType.MESH

EPS = 1e-6
D_MODEL = 1024
D_FF = 4096
HEADS = 4
RET_DK = 64
RET_DV = 128
MLA_NOPE = 128
MLA_ROPE = 64
MLA_HEAD = 256
Q_LORA = 384
KV_LORA = 256
GRID_W = 64
ROPE_BASE = 10000.0
IN_COLS = 2240
IN_PAD = 2304
PG_COLS = 1152
N_CHIPS = 4
N_DEV = 8
LANES = 128
ADAM_LR = 0.001
ADAM_B1 = 0.9
ADAM_B2 = 0.999
ADAM_EPS = 1e-08
ADAM_WD = 0.01
ADAM_STEP = 10
VMEM_LIMIT = 56 * 1024 * 1024


def _dot(a, b):
    return jnp.dot(a, b, preferred_element_type=F32)


def _dot_nt(a, b):
    return lax.dot_general(a, b, (((1,), (1,)), ((), ())), preferred_element_type=F32)


def _dot_tn(a, b):
    return lax.dot_general(a, b, (((0,), (0,)), ((), ())), preferred_element_type=F32)


def _params(sem=None, vmem=None):
    return pltpu.CompilerParams(dimension_semantics=sem, vmem_limit_bytes=vmem)


def _full(shape):
    n = len(shape)
    return pl.BlockSpec(shape, lambda *_: (0,) * n)


def _once(shape):
    n = len(shape)
    return pl.BlockSpec(shape, lambda *_: (0,) * n, pipeline_mode=pl.Buffered(1))


def _rope(x, cos, sin):
    w = x.shape[-1]
    lo = (lax.broadcasted_iota(jnp.int32, (1, w), 1) % 64) < 32
    swapped = jnp.where(lo, pltpu.roll(x, w - 32, 1), pltpu.roll(x, 32, 1))
    return x * cos + swapped * sin


def _rope_t(g, cos, sin):
    w = g.shape[-1]
    lo = (lax.broadcasted_iota(jnp.int32, (1, w), 1) % 64) < 32
    t = g * sin
    swapped = jnp.where(lo, pltpu.roll(t, w - 32, 1), pltpu.roll(t, 32, 1))
    return g * cos + swapped


def _rope_tables(seq, tm):
    rows = seq // GRID_W
    row = jnp.repeat(jnp.arange(rows, dtype=F32), GRID_W)
    col = jnp.tile(jnp.arange(GRID_W, dtype=F32), rows)
    n_freq = RET_DK // 4
    freq = ROPE_BASE ** (-jnp.arange(n_freq, dtype=F32) / n_freq)
    ang = jnp.concatenate([row[:, None] * freq, col[:, None] * freq], axis=-1)
    cos, sin = jnp.cos(ang), jnp.sin(ang)
    cos_t = jnp.tile(jnp.concatenate([cos, cos], -1), (1, HEADS))
    sin_t = jnp.tile(jnp.concatenate([-sin, sin], -1), (1, HEADS))
    cos_t = jnp.concatenate([cos_t, jnp.ones((tm, 4 * RET_DK), F32)], 0)
    sin_t = jnp.concatenate([sin_t, jnp.zeros((tm, 4 * RET_DK), F32)], 0)
    return cos_t, sin_t


def _adam_math(w, g, m, v):
    mn = ADAM_B1 * m + (1.0 - ADAM_B1) * g
    vn = ADAM_B2 * v + (1.0 - ADAM_B2) * (g * g)
    m_hat = mn / (1.0 - ADAM_B1 ** ADAM_STEP)
    v_hat = vn / (1.0 - ADAM_B2 ** ADAM_STEP)
    return -ADAM_LR * (m_hat / (jnp.sqrt(v_hat) + ADAM_EPS) + ADAM_WD * w), mn, vn


def _cast_into_slot(w, slot, name, rider=None, rows=None):
    start, r = rows if rows else (0, w.shape[0])
    c = w.shape[1]
    rb = max(b for b in range(16, 257, 16) if r % b == 0 and start % b == 0)
    first = start // rb

    def body(s_ref, w_ref, o_ref):
        o_ref[...] = w_ref[...].astype(BF16)

    (out,), carried = _hosted_call(
        body, (w,), name=name, grid=(r // rb,), prefetch=(slot,),
        in_specs=[pl.BlockSpec((rb, c), lambda i, s: (first + i, 0))],
        out_specs=[pl.BlockSpec((None, rb, c), lambda i, s: (s[0], i, 0))],
        out_shape=[jax.ShapeDtypeStruct((N_CHIPS, r, c), BF16)], sem=("parallel",), rider=rider)
    return out, carried


def _adamw_halves(w, mine, theirs, m, v, core, name):
    r, c = w.shape
    r2 = r // 2
    rb = max(b for b in range(8, r2 + 1, 8) if r2 % b == 0 and b * c * 4 <= (1 << 21))
    nbh = r2 // rb

    def body(z_ref, w_ref, a_ref, b_ref, m_ref, v_ref, g_ref, d_ref, mo_ref, vo_ref):
        here = (pl.program_id(0) // nbh) == z_ref[0]
        gg = jnp.where(here, a_ref[...], b_ref[...])
        g_ref[...] = gg
        d_ref[...], mo_ref[...], vo_ref[...] = _adam_math(w_ref[...], gg, m_ref[...], v_ref[...])

    spec = pl.BlockSpec((rb, c), lambda i, z: (i, 0))
    a_spec = pl.BlockSpec((rb, c), lambda i, z: (jnp.clip(i - z[0] * nbh, 0, nbh - 1), 0))
    b_spec = pl.BlockSpec((rb, c), lambda i, z: (jnp.clip(i - (1 - z[0]) * nbh, 0, nbh - 1), 0))
    shp = jax.ShapeDtypeStruct((r, c), F32)
    return pl.pallas_call(
        body, name=name,
        grid_spec=pltpu.PrefetchScalarGridSpec(
            num_scalar_prefetch=1, grid=(r // rb,), in_specs=[spec, a_spec, b_spec, spec, spec], out_specs=[spec] * 4),
        out_shape=[shp] * 4,
        compiler_params=_params(("parallel",)),
    )(core, w, mine, theirs, m, v)


def _adamw(w, g, m, v, name, rider=None):
    r, c = w.shape
    rb = r
    for cand in (256, 128, 64, 32, 16, 8):
        if r % cand == 0 and cand * c * 4 <= (1 << 20):
            rb = cand
            break
    if r * c * 4 <= (1 << 20):
        rb = r

    def body(w_ref, g_ref, m_ref, v_ref, d_ref, mo_ref, vo_ref):
        d_ref[...], mo_ref[...], vo_ref[...] = _adam_math(w_ref[...], g_ref[...], m_ref[...], v_ref[...])

    spec = pl.BlockSpec((rb, c), lambda i: (i, 0))
    shp = jax.ShapeDtypeStruct((r, c), F32)
    return _hosted_call(
        body, (w, g, m, v), name=name, grid=(r // rb,), in_specs=[spec] * 4, out_specs=[spec] * 3, out_shape=[shp] * 3,
        sem=("parallel",), rider=rider)


def _decay_prep(dec):
    def body(d_ref, lg_ref, sg_ref):
        d = d_ref[...]
        lg_ref[...] = jnp.minimum(d, 0.0) - jnp.log(1.0 + jnp.exp(-jnp.abs(d)))
        sg_ref[...] = 1.0 / (1.0 + jnp.exp(d))

    shp = jax.ShapeDtypeStruct(dec.shape, F32)
    return pl.pallas_call(body, name="decay_prep", out_shape=[shp, shp])(dec)


def _mod_fwd(a_in, w_ada, b_sh):
    rows, d = a_in.shape
    n = w_ada.shape[1]
    bn = 512

    def body(a_ref, w_ref, b_ref, o_ref):
        a = a_ref[...]
        s = (a / (1.0 + jnp.exp(-a))).astype(BF16)
        o_ref[...] = _dot(s, w_ref[...].astype(BF16)) + b_ref[...]

    return pl.pallas_call(
        body, name="mod_fwd", grid=(n // bn,),
        in_specs=[_full((rows, d)), pl.BlockSpec((d, bn), lambda j: (0, j)), pl.BlockSpec((1, bn), lambda j: (0, j))],
        out_specs=pl.BlockSpec((rows, bn), lambda j: (0, j)),
        out_shape=jax.ShapeDtypeStruct((rows, n), F32),
        compiler_params=_params(("parallel",)),
    )(a_in, w_ada, b_sh)


def _mod_bwd(a_in, dm, w_ada):
    rows, d = a_in.shape
    n = w_ada.shape[1]
    bn = 512
    nb = n // bn

    def body(a_ref, dm_ref, w_ref, gw_ref, da_ref):
        j = pl.program_id(0)
        a = a_ref[...]
        s = (a / (1.0 + jnp.exp(-a))).astype(BF16)
        dmb = dm_ref[...].astype(BF16)
        gw_ref[...] = _dot_tn(s, dmb)
        part = _dot_nt(dmb, w_ref[...].astype(BF16))

        @pl.when(j == 0)
        def _():
            da_ref[...] = part

        @pl.when(j > 0)
        def _():
            da_ref[...] += part

    return pl.pallas_call(
        body, name="mod_bwd", grid=(nb,),
        in_specs=[_full((rows, d)), pl.BlockSpec((rows, bn), lambda j: (0, j)), pl.BlockSpec((d, bn), lambda j: (0, j))],
        out_specs=[pl.BlockSpec((d, bn), lambda j: (0, j)), _full((rows, d))],
        out_shape=[jax.ShapeDtypeStruct((d, n), F32), jax.ShapeDtypeStruct((rows, d), F32)],
        compiler_params=_params(("arbitrary",)),
    )(a_in, dm, w_ada)


def _pre_fwd(x2, ctx2, modv, g_attn, w_in, g_q, g_kv, w_uq, w_ukv, cos_t, sin_t, *, seq, tm, rider=None):
    t_lat, d = x2.shape
    t_ctx = ctx2.shape[0]
    nl, nc = t_lat // tm, t_ctx // tm
    n_all = t_lat + t_ctx
    tpe = seq // tm
    nex = t_lat // seq

    def body(x_ref, c_ref, mod_ref, g_ref, win_ref, gq_ref, gkv_ref, wuq_ref, wukv_ref, cos_ref, sin_ref,
             h_ref, pg_ref, rq_ref, rk_ref, rv_ref, nq_ref, nkv_ref, q_ref, k_ref, v_ref):
        i = pl.program_id(0)
        xt = jnp.where(i < nl, x_ref[...], c_ref[...])
        sh = mod_ref[0, 0:1, :]
        sc = mod_ref[0, 1:2, :]
        r = lax.rsqrt(jnp.mean(xt * xt, axis=-1, keepdims=True) + EPS)
        hb = ((xt * r) * g_ref[...] * (1.0 + sc) + sh).astype(BF16)
        h_ref[...] = hb
        p = _dot_nt(hb, win_ref[...])
        cos = cos_ref[...]
        sin = sin_ref[...]
        rq_ref[...] = _rope(p[:, 0:256], cos, sin).astype(BF16)
        rk_ref[...] = _rope(p[:, 256:512] * (RET_DK ** -0.5), cos, sin).astype(BF16)
        rv_ref[...] = p[:, 512:1024].astype(BF16)
        pg_ref[...] = p[:, 1024:2176]
        cq = p[:, 1536:1920]
        ckv = p[:, 1920:2176]
        nqb = (cq * lax.rsqrt(jnp.mean(cq * cq, axis=-1, keepdims=True) + EPS) * gq_ref[...]).astype(BF16)
        nkvb = (ckv * lax.rsqrt(jnp.mean(ckv * ckv, axis=-1, keepdims=True) + EPS) * gkv_ref[...]).astype(BF16)
        nq_ref[...] = nqb
        nkv_ref[...] = nkvb
        cos1 = cos[:, 0:LANES]
        sin1 = sin[:, 0:LANES]
        kpe = _rope(p[:, 2176:2304], cos1, sin1).astype(BF16)
        for hd in range(HEADS):
            o = hd * MLA_HEAD
            qh = _dot_nt(nqb, wuq_ref[hd]) * MLA_SCALE
            q_ref[:, o:o + 128] = qh[:, 0:128].astype(BF16)
            q_ref[:, o + 128:o + 256] = _rope(qh[:, 128:256], cos1, sin1).astype(BF16)
            kvh = _dot(nkvb, wukv_ref[hd])
            k_ref[:, o:o + 128] = kvh[:, 0:128].astype(BF16)
            k_ref[:, o + 128:o + 256] = kpe
            v_ref[:, hd * 128:(hd + 1) * 128] = kvh[:, 128:256].astype(BF16)

    def tile(width):
        return pl.BlockSpec((tm, width), lambda i: (i, 0))

    widths = (d, PG_COLS, 256, 256, 512, Q_LORA, KV_LORA, HEADS * MLA_HEAD, HEADS * MLA_HEAD, HEADS * 128)
    dtypes = (BF16, F32, BF16, BF16, BF16, BF16, BF16, BF16, BF16, BF16)
    tab = pl.BlockSpec((tm, 256), lambda i: (jnp.where(i < nl, i % tpe, tpe), 0))
    return _hosted_call(
        body, (x2, ctx2, modv, g_attn, w_in, g_q, g_kv, w_uq, w_ukv, cos_t, sin_t), name="pre_fwd", grid=(nl + nc,),
        in_specs=[
            pl.BlockSpec((tm, d), lambda i: (jnp.minimum(i, nl - 1), 0)),
            pl.BlockSpec((tm, d), lambda i: (jnp.maximum(i - nl, 0), 0)),
            pl.BlockSpec((1, 8, d), lambda i: (jnp.minimum(i // tpe, nex), 0, 0)),
            _full((1, d)), _full(w_in.shape), _full((1, Q_LORA)), _full((1, KV_LORA)),
            _full(w_uq.shape), _full(w_ukv.shape), tab, tab,
        ],
        out_specs=[tile(w) for w in widths],
        out_shape=[jax.ShapeDtypeStruct((n_all, w), dt) for w, dt in zip(widths, dtypes)],
        sem=("parallel",), rider=rider)


def _post(yret, ymla, x2, tgt2, modv, g_ffn, g_fin, w_out, w_ff1, w_ff2a, w_ff2b, *, seq, tm):
    t_lat, d = x2.shape
    nl = t_lat // tm
    tpe = seq // tm
    nex = t_lat // seq
    n_slab = w_ff1.shape[0]
    fs = w_ff1.shape[2]
    fh = w_ff2a.shape[1]

    def body(yr_ref, ym_ref, x_ref, t_ref, mod_ref, gf_ref, gl_ref, wo_ref, w1_ref, w2a_ref, w2b_ref,
             mix_ref, a_ref, du_ref, h2_ref, df_ref, dmo_ref, dmix_ref, dxm_ref, st_ref, ru_ref):
        i = pl.program_id(0)
        gt_a = mod_ref[0, 2:3, :]
        sh_f = mod_ref[0, 3:4, :]
        sc_f = mod_ref[0, 4:5, :]
        gt_f = mod_ref[0, 5:6, :]
        g_ffn_v = gf_ref[...]
        g_fin_v = gl_ref[...]
        yr = yr_ref[...]
        ym = ym_ref[...]
        mix_ref[:, 0:512] = yr
        mix_ref[:, 512:1024] = ym
        op = _dot(yr, wo_ref[0:512, :]) + _dot(ym, wo_ref[512:1024, :])
        x_mid = x_ref[...] + gt_a * op
        r2 = lax.rsqrt(jnp.mean(x_mid * x_mid, axis=-1, keepdims=True) + EPS)
        xh2 = x_mid * r2
        h2b = (xh2 * g_ffn_v * (1.0 + sc_f) + sh_f).astype(BF16)
        h2_ref[...] = h2b
        f = jnp.zeros((tm, d), F32)
        for s in range(n_slab):
            ru = jnp.maximum(_dot(h2b, w1_ref[s]), 0.0)
            ru_ref[:, s * fs:(s + 1) * fs] = ru
            ab = (ru * ru).astype(BF16)
            a_ref[:, s * fs:(s + 1) * fs] = ab
            f = f + _dot(ab[:, 0:fh], w2a_ref[s]) + _dot(ab[:, fh:fs], w2b_ref[s])
        x_out = x_mid + gt_f * f
        r3 = lax.rsqrt(jnp.mean(x_out * x_out, axis=-1, keepdims=True) + EPS)
        xh3 = x_out * r3
        err = xh3 * g_fin_v - t_ref[...]
        dy = err * (1.0 / d)
        dxh3 = dy * g_fin_v
        dx_out = r3 * (dxh3 - xh3 * jnp.mean(dxh3 * xh3, axis=-1, keepdims=True))
        dfb = (dx_out * gt_f).astype(BF16)
        df_ref[...] = dfb
        dh2 = jnp.zeros((tm, d), F32)
        for s in range(n_slab):
            da = jnp.concatenate([_dot_nt(dfb, w2a_ref[s]), _dot_nt(dfb, w2b_ref[s])], axis=1)
            dub = (da * (2.0 * ru_ref[:, s * fs:(s + 1) * fs])).astype(BF16)
            du_ref[:, s * fs:(s + 1) * fs] = dub
            dh2 = dh2 + _dot_nt(dub, w1_ref[s])
        dxh2 = dh2 * (1.0 + sc_f) * g_ffn_v
        dx_mid = dx_out + r2 * (dxh2 - xh2 * jnp.mean(dxh2 * xh2, axis=-1, keepdims=True))
        dxm_ref[...] = dx_mid
        dmob = (dx_mid * gt_a).astype(BF16)
        dmo_ref[...] = dmob
        dmix_ref[...] = _dot_nt(dmob, wo_ref[...]).astype(BF16)

        def rsum(v):
            return jnp.sum(v, axis=0, keepdims=True)

        stats = jnp.concatenate([
            rsum(dh2), rsum(dh2 * xh2 * g_ffn_v), rsum(dx_out * f), rsum(dx_mid * op),
            rsum(dh2 * (1.0 + sc_f) * xh2), rsum(dy * xh3), rsum(err * err), jnp.zeros((1, d), F32)], axis=0)

        @pl.when(i % tpe == 0)
        def _():
            st_ref[0] = stats

        @pl.when(i % tpe != 0)
        def _():
            st_ref[0] += stats

    def tile(width):
        return pl.BlockSpec((tm, width), lambda i: (i, 0))

    widths = (d, D_FF, D_FF, d, d, d, d, d)
    dtypes = (BF16, BF16, BF16, BF16, BF16, BF16, BF16, F32)
    const = pl.Buffered(1)
    return pl.pallas_call(
        body, name="post", grid=(nl,),
        in_specs=[
            tile(512), tile(512), tile(d), tile(d),
            pl.BlockSpec((1, 8, d), lambda i: (i // tpe, 0, 0)),
            _full((1, d)), _full((1, d)),
            pl.BlockSpec(w_out.shape, lambda i: (0, 0), pipeline_mode=const),
            pl.BlockSpec(w_ff1.shape, lambda i: (0, 0, 0), pipeline_mode=const),
            pl.BlockSpec(w_ff2a.shape, lambda i: (0, 0, 0), pipeline_mode=const),
            pl.BlockSpec(w_ff2b.shape, lambda i: (0, 0, 0), pipeline_mode=const),
        ],
        out_specs=[tile(w) for w in widths] + [pl.BlockSpec((1, 8, d), lambda i: (i // tpe, 0, 0))],
        out_shape=[jax.ShapeDtypeStruct((t_lat, w), dt) for w, dt in zip(widths, dtypes)]
        + [jax.ShapeDtypeStruct((nex, 8, d), F32)],
        scratch_shapes=[pltpu.VMEM((tm, D_FF), F32)],
        compiler_params=_params(("arbitrary",), VMEM_LIMIT),
    )(yret, ymla, x2, tgt2, modv, g_ffn, g_fin, w_out, w_ff1, w_ff2a, w_ff2b)


def _pre_bwd(x2, ctx2, modv, g_attn, pg, drq, drk, dkc_r, drv, dvc_r, drg, dq_m, dkl, dkc, dvl, dvc, dxm,
             w_in, g_q, g_kv, w_uq, w_ukv, cos_t, sin_t, *, seq, tm, rider=None):
    t_lat, d = x2.shape
    t_ctx = ctx2.shape[0]
    nl, nc = t_lat // tm, t_ctx // tm
    n_all = t_lat + t_ctx
    tpe = seq // tm
    nex = t_lat // seq

    def body(x_ref, c_ref, mod_ref, g_ref, pg_ref, drq_ref, drk_ref, dkcr_ref, drv_ref, dvcr_ref, drg_ref,
             dq_ref, dkl_ref, dkc_ref, dvl_ref, dvc_ref, dxm_ref, win_ref, gq_ref, gkv_ref, wuq_ref, wukv_ref,
             cos_ref, sin_ref, dpb_ref, dqf_ref, dkvf_ref, gx_ref, st_ref):
        i = pl.program_id(0)
        lat = i < nl
        latf = lat.astype(F32)
        cos = cos_ref[...]
        sin = sin_ref[...]
        cos1 = cos[:, 0:LANES]
        sin1 = sin[:, 0:LANES]
        d_rq = _rope_t(drq_ref[...] * latf, cos, sin)
        d_rk = _rope_t(jnp.where(lat, drk_ref[...], dkcr_ref[...]), cos, sin) * (RET_DK ** -0.5)
        d_rv = jnp.where(lat, drv_ref[...], dvcr_ref[...])
        d_rg = drg_ref[...] * latf
        dq_all = dq_ref[...] * (latf * MLA_SCALE)
        dk_all = jnp.where(lat, dkl_ref[...], dkc_ref[...])
        dv_all = jnp.where(lat, dvl_ref[...], dvc_ref[...])
        dnq = jnp.zeros((tm, Q_LORA), F32)
        dnkv = jnp.zeros((tm, KV_LORA), F32)
        dkpe = jnp.zeros((tm, LANES), F32)
        for hd in range(HEADS):
            o = hd * MLA_HEAD
            dqh = jnp.concatenate([dq_all[:, o:o + 128], _rope_t(dq_all[:, o + 128:o + 256], cos1, sin1)],
                                  axis=1).astype(BF16)
            dqf_ref[:, o:o + 256] = dqh
            dnq = dnq + _dot(dqh, wuq_ref[hd])
            dkpe = dkpe + dk_all[:, o + 128:o + 256]
            dkvh = jnp.concatenate([dk_all[:, o:o + 128], dv_all[:, hd * 128:(hd + 1) * 128]], axis=1).astype(BF16)
            dkvf_ref[:, o:o + 256] = dkvh
            dnkv = dnkv + _dot_nt(dkvh, wukv_ref[hd])
        d_kpe = _rope_t(dkpe, cos1, sin1)
        pgv = pg_ref[...]
        cq = pgv[:, 512:896]
        ckv = pgv[:, 896:1152]
        rq_ = lax.rsqrt(jnp.mean(cq * cq, axis=-1, keepdims=True) + EPS)
        cqh = cq * rq_
        dcqh = dnq * gq_ref[...]
        d_cq = rq_ * (dcqh - cqh * jnp.mean(dcqh * cqh, axis=-1, keepdims=True))
        rkv_ = lax.rsqrt(jnp.mean(ckv * ckv, axis=-1, keepdims=True) + EPS)
        ckvh = ckv * rkv_
        dckvh = dnkv * gkv_ref[...]
        d_ckv = rkv_ * (dckvh - ckvh * jnp.mean(dckvh * ckvh, axis=-1, keepdims=True))
        dpb = jnp.concatenate([d_rq, d_rk, d_rv, d_rg, d_cq, d_ckv, d_kpe], axis=1).astype(BF16)
        dpb_ref[...] = dpb
        dh = _dot(dpb, win_ref[...])
        xt = jnp.where(lat, x_ref[...], c_ref[...])
        sc = mod_ref[0, 1:2, :]
        g = g_ref[...]
        r = lax.rsqrt(jnp.mean(xt * xt, axis=-1, keepdims=True) + EPS)
        xh = xt * r
        dxh = dh * (1.0 + sc) * g
        dx = r * (dxh - xh * jnp.mean(dxh * xh, axis=-1, keepdims=True))

        @pl.when(lat)
        def _():
            gx_ref[...] = dxm_ref[...] + dx

        def rsum(v):
            return jnp.sum(v, axis=0, keepdims=True)

        def widen(v):
            return jnp.concatenate([v, jnp.zeros((1, d - v.shape[1]), F32)], axis=1)

        stats = jnp.concatenate([
            rsum(dh), rsum(dh * xh * g), rsum(dh * (1.0 + sc) * xh), widen(rsum(dnq * cqh)), widen(rsum(dnkv * ckvh)),
            jnp.zeros((3, d), F32)], axis=0)
        first = jnp.logical_or(jnp.logical_and(lat, i % tpe == 0), i == nl)

        @pl.when(first)
        def _():
            st_ref[0] = stats

        @pl.when(jnp.logical_not(first))
        def _():
            st_ref[0] += stats

    def lat_tile(width):
        return pl.BlockSpec((tm, width), lambda i: (jnp.minimum(i, nl - 1), 0))

    def ctx_tile(width):
        return pl.BlockSpec((tm, width), lambda i: (jnp.maximum(i - nl, 0), 0))

    def tile(width):
        return pl.BlockSpec((tm, width), lambda i: (i, 0))

    tab = pl.BlockSpec((tm, 256), lambda i: (jnp.where(i < nl, i % tpe, tpe), 0))
    ex = pl.BlockSpec((1, 8, d), lambda i: (jnp.minimum(i // tpe, nex), 0, 0))
    return _hosted_call(
        body, (x2, ctx2, modv, g_attn, pg, drq, drk, dkc_r, drv, dvc_r, drg, dq_m, dkl, dkc, dvl, dvc, dxm,
               w_in, g_q, g_kv, w_uq, w_ukv, cos_t, sin_t), name="pre_bwd", grid=(nl + nc,),
        in_specs=[
            lat_tile(d), ctx_tile(d), ex, _full((1, d)), tile(PG_COLS),
            lat_tile(256), lat_tile(256), ctx_tile(256), lat_tile(512), ctx_tile(512), lat_tile(512),
            lat_tile(1024), lat_tile(1024), ctx_tile(1024), lat_tile(512), ctx_tile(512), lat_tile(d),
            _once(w_in.shape), _full((1, Q_LORA)), _full((1, KV_LORA)), _once(w_uq.shape), _once(w_ukv.shape),
            tab, tab,
        ],
        out_specs=[tile(IN_PAD), tile(1024), tile(1024), lat_tile(d), ex],
        out_shape=[
            jax.ShapeDtypeStruct((n_all, IN_PAD), BF16), jax.ShapeDtypeStruct((n_all, 1024), BF16),
            jax.ShapeDtypeStruct((n_all, 1024), BF16), jax.ShapeDtypeStruct((t_lat, d), F32),
            jax.ShapeDtypeStruct((nex + 1, 8, d), F32),
        ],
        sem=("arbitrary",), rider=rider)


MLA_SCALE = 1.0 / math.sqrt(MLA_NOPE + MLA_ROPE)
KEY_BLOCK = 1024


def _mla_specs(t_lat, seq, ctx_len, tq, heads=1):
    nqt = seq // tq
    cb = t_lat // ctx_len
    q = pl.BlockSpec((tq, heads * MLA_HEAD), lambda b, h, j: (b * nqt + j, h))
    kl = pl.BlockSpec((seq, heads * MLA_HEAD), lambda b, h, j: (b, h))
    kc = pl.BlockSpec((ctx_len, heads * MLA_HEAD), lambda b, h, j: (cb + b, h))
    vl = pl.BlockSpec((seq, heads * 128), lambda b, h, j: (b, h))
    vc = pl.BlockSpec((ctx_len, heads * 128), lambda b, h, j: (cb + b, h))
    o = pl.BlockSpec((tq, heads * 128), lambda b, h, j: (b * nqt + j, h))
    return q, kl, kc, vl, vc, o


FWD_HEADS = 2
BWD_HEADS = 1


def _mla_fwd(q, k, v, *, t_lat, seq, ctx_len, tq, rider=None):
    nex = t_lat // seq

    def body(q_ref, kl_ref, kc_ref, vl_ref, vc_ref, o_ref, lse_ref):
        for hh in range(FWD_HEADS):
            wide = slice(hh * MLA_HEAD, (hh + 1) * MLA_HEAD)
            cols = slice(hh * 128, (hh + 1) * 128)
            qb = q_ref[:, wide]
            s = _dot_nt(qb, kl_ref[:, wide])
            sc = _dot_nt(qb, kc_ref[:, wide])
            m = jnp.maximum(jnp.max(s, axis=-1, keepdims=True), jnp.max(sc, axis=-1, keepdims=True))
            p = jnp.exp(s - m)
            pc = jnp.exp(sc - m)
            total = jnp.sum(p, axis=-1, keepdims=True) + jnp.sum(pc, axis=-1, keepdims=True)
            o = _dot(p.astype(BF16), vl_ref[:, cols]) + _dot(pc.astype(BF16), vc_ref[:, cols])
            o_ref[:, cols] = (o * (1.0 / total)).astype(BF16)
            lse_ref[:, cols] = jnp.broadcast_to(m + jnp.log(total), (tq, 128))

    qs, kl, kc, vl, vc, os_ = _mla_specs(t_lat, seq, ctx_len, tq, FWD_HEADS)
    return _hosted_call(
        body, (q, k, k, v, v), name="mla_fwd", grid=(nex, HEADS // FWD_HEADS, seq // tq),
        in_specs=[qs, kl, kc, vl, vc], out_specs=[os_, os_],
        out_shape=[jax.ShapeDtypeStruct((t_lat, HEADS * 128), BF16), jax.ShapeDtypeStruct((t_lat, HEADS * 128), F32)],
        sem=("parallel", "parallel", "arbitrary"), rider=rider)


def _mla_bwd(q, k, v, ymla, lse, dmix, *, t_lat, seq, ctx_len, tq, rider=None):
    nex = t_lat // seq
    nqt = seq // tq
    t_ctx = nex * ctx_len
    kb = min(KEY_BLOCK, seq)

    def body(q_ref, kl_ref, kc_ref, vl_ref, vc_ref, o_ref, lse_ref, do_ref, dq_ref, dkl_ref, dkc_ref, dvl_ref, dvc_ref):
        j = pl.program_id(2)

        @pl.when(j == 0)
        def _():
            dkl_ref[...] = jnp.zeros(dkl_ref.shape, F32)
            dkc_ref[...] = jnp.zeros(dkc_ref.shape, F32)
            dvl_ref[...] = jnp.zeros(dvl_ref.shape, F32)
            dvc_ref[...] = jnp.zeros(dvc_ref.shape, F32)

        for hh in range(BWD_HEADS):
            wide = slice(hh * MLA_HEAD, (hh + 1) * MLA_HEAD)
            cols = slice(hh * 128, (hh + 1) * 128)
            qb = q_ref[:, wide]
            dob = do_ref[:, cols]
            delta = jnp.sum(dob.astype(F32) * o_ref[:, cols].astype(F32), axis=-1, keepdims=True)
            lse_row = lse_ref[:, hh * 128:hh * 128 + 1]

            def block(k_ref, v_ref, dk_ref, dv_ref, rows):
                kbl = k_ref[rows, wide]
                vbl = v_ref[rows, cols]
                p = jnp.exp(_dot_nt(qb, kbl) - lse_row)
                ds = (p * (_dot_nt(dob, vbl) - delta)).astype(BF16)
                dk_ref[rows, wide] += _dot_tn(ds, qb)
                dv_ref[rows, cols] += _dot_tn(p.astype(BF16), dob)
                return _dot(ds, kbl)

            dq = block(kc_ref, vc_ref, dkc_ref, dvc_ref, pl.ds(0, ctx_len))
            for i in range(seq // kb):
                dq = dq + block(kl_ref, vl_ref, dkl_ref, dvl_ref, pl.ds(i * kb, kb))
            dq_ref[:, wide] = dq

    g = BWD_HEADS
    qs, kl, kc, vl, vc, os_ = _mla_specs(t_lat, seq, ctx_len, tq, g)
    do_spec = pl.BlockSpec((tq, g * 128), lambda b, h, j: (b * nqt + j, HEADS // g + h))
    return _hosted_call(
        body, (q, k, k, v, v, ymla, lse, dmix), name="mla_bwd", grid=(nex, HEADS // g, nqt),
        in_specs=[qs, kl, kc, vl, vc, os_, os_, do_spec],
        out_specs=[
            qs,
            pl.BlockSpec((seq, g * MLA_HEAD), lambda b, h, j: (b, h)),
            pl.BlockSpec((ctx_len, g * MLA_HEAD), lambda b, h, j: (b, h)),
            pl.BlockSpec((seq, g * 128), lambda b, h, j: (b, h)),
            pl.BlockSpec((ctx_len, g * 128), lambda b, h, j: (b, h)),
        ],
        out_shape=[
            jax.ShapeDtypeStruct((t_lat, HEADS * MLA_HEAD), F32),
            jax.ShapeDtypeStruct((t_lat, HEADS * MLA_HEAD), F32),
            jax.ShapeDtypeStruct((t_ctx, HEADS * MLA_HEAD), F32),
            jax.ShapeDtypeStruct((t_lat, HEADS * 128), F32),
            jax.ShapeDtypeStruct((t_ctx, HEADS * 128), F32),
        ],
        sem=("parallel", "parallel", "arbitrary"), rider=rider)


def _decay_terms(lg, chunk, forward):
    ii = lax.broadcasted_iota(jnp.int32, (chunk, chunk), 0)
    jj = lax.broadcasted_iota(jnp.int32, (chunk, chunk), 1)
    diff = (ii - jj) if forward else (jj - ii)
    dist = jnp.maximum(diff, 0).astype(F32)
    dmat = jnp.where(diff >= 0, jnp.exp(lg * dist), 0.0)
    pos = lax.broadcasted_iota(jnp.int32, (chunk, 1), 0).astype(F32)
    if forward:
        e_q = pos + 1.0
        e_k = (chunk - 1.0) - pos
    else:
        e_q = chunk - pos
        e_k = pos
    wq = jnp.exp(lg * e_q)
    wk = jnp.exp(lg * e_k)
    cd = jnp.exp(jnp.full((1, 1), lg * chunk, F32))
    return dmat, dist, wq, wk, e_q, e_k, cd


def _ctx_weights(lg, ctx_len, forward):
    pos = lax.broadcasted_iota(jnp.int32, (ctx_len, 1), 0).astype(F32)
    e = ((ctx_len - 1.0) - pos) if forward else pos
    return jnp.exp(lg * e), e


def _pair_specs(t_lat, seq, ctx_len):
    cb = t_lat // ctx_len
    qk = pl.BlockSpec((seq, 128), lambda b, p: (b, p))
    v = pl.BlockSpec((seq, 256), lambda b, p: (b, p))
    kc = pl.BlockSpec((ctx_len, 128), lambda b, p: (cb + b, p))
    vc = pl.BlockSpec((ctx_len, 256), lambda b, p: (cb + b, p))
    return qk, v, kc, vc


def _lane_masks():
    lane = lax.broadcasted_iota(jnp.int32, (1, 128), 1)
    return [(lane // RET_DK) == hh for hh in (0, 1)]


def _ret_fwd_pair(rq, rk, rv, pg, lg, g_ret, *, t_lat, seq, ctx_len, chunk, rider=None):
    nex = t_lat // seq
    n_chunk = seq // chunk

    def body(q_ref, k_ref, v_ref, kc_ref, vc_ref, rg_ref, lg_ref, g_ref, y_ref, o_ref):
        pair = pl.program_id(1)
        masks = _lane_masks()
        kcf = kc_ref[...].astype(F32)
        chains = [(forward, hh) for forward in (True, False) for hh in (0, 1)]
        terms, s0 = [], []
        for forward, hh in chains:
            lgd = lg_ref[0 if forward else 1, 2 * pair + hh]
            terms.append(_decay_terms(lgd, chunk, forward))
            wc, _ = _ctx_weights(lgd, ctx_len, forward)
            s0.append(_dot_tn((jnp.where(masks[hh], kcf, 0.0) * wc).astype(BF16), vc_ref[:, hh * 128:(hh + 1) * 128]))
        both = [terms[hh][0] + terms[2 + hh][0] for hh in (0, 1)]
        o_ref[...] = jnp.zeros(o_ref.shape, F32)

        def step(t, states):
            new = [None] * 4
            for forward in (True, False):
                n = t if forward else n_chunk - 1 - t
                sl = pl.ds(pl.multiple_of(n * chunk, chunk), chunk)
                qb = q_ref[sl, :]
                kf_all = k_ref[sl, :].astype(F32)
                for hh in (0, 1):
                    c = (0 if forward else 2) + hh
                    _, _, wq, wk, _, _, cd = terms[c]
                    cols = slice(hh * 128, (hh + 1) * 128)
                    qm = jnp.where(masks[hh], qb, jnp.zeros((), BF16))
                    kf = jnp.where(masks[hh], kf_all, 0.0)
                    vb = v_ref[sl, cols]
                    o = wq * _dot(qm, states[c].astype(BF16))
                    if forward:
                        o = o + _dot((_dot_nt(qm, kf.astype(BF16)) * both[hh]).astype(BF16), vb)
                    o_ref[sl, cols] += o
                    new[c] = cd * states[c] + _dot_tn((kf * wk).astype(BF16), vb)
            return tuple(new)

        lax.fori_loop(0, n_chunk, step, tuple(s0))

        def norm_step(n, carry):
            sl = pl.ds(pl.multiple_of(n * chunk, chunk), chunk)
            for hh in (0, 1):
                cols = slice(hh * 128, (hh + 1) * 128)
                o = o_ref[sl, cols]
                mu = jnp.mean(o, axis=-1, keepdims=True)
                oc = o - mu
                var = jnp.mean(oc * oc, axis=-1, keepdims=True)
                rg = rg_ref[sl, cols]
                y_ref[sl, cols] = (oc * lax.rsqrt(var + EPS) * g_ref[:, cols] * (rg / (1.0 + jnp.exp(-rg)))).astype(BF16)
            return carry

        lax.fori_loop(0, n_chunk, norm_step, 0)

    qk, v, kc, vc = _pair_specs(t_lat, seq, ctx_len)
    return _hosted_call(
        body, (rq, rk, rv, rk, rv, pg, lg, g_ret), name="ret_fwd", grid=(nex, HEADS // 2),
        in_specs=[qk, qk, v, kc, vc, v, pl.BlockSpec(memory_space=pltpu.SMEM), pl.BlockSpec((1, 256), lambda b, p: (0, p))],
        out_specs=[v, v],
        out_shape=[jax.ShapeDtypeStruct((t_lat, HEADS * RET_DV), BF16), jax.ShapeDtypeStruct((t_lat, HEADS * RET_DV), F32)],
        sem=("parallel", "arbitrary"), rider=rider)


def _ret_bwd_pair(rq, rk, rv, pg, osum, dmix, lg, g_ret, *, t_lat, seq, ctx_len, chunk, rider=None):
    nex = t_lat // seq
    n_chunk = seq // chunk
    t_ctx = nex * ctx_len

    def body(q_ref, k_ref, v_ref, kc_ref, vc_ref, rg_ref, o_ref, dy_ref, lg_ref, g_ref,
             dq_ref, dk_ref, dv_ref, dkc_ref, dvc_ref, drg_ref, st_ref, do_s, s_st):
        pair = pl.program_id(1)
        masks = _lane_masks()
        kcf = kc_ref[...].astype(F32)

        def norm_step(n, dgains):
            sl = pl.ds(pl.multiple_of(n * chunk, chunk), chunk)
            out = []
            for hh in (0, 1):
                cols = slice(hh * 128, (hh + 1) * 128)
                gain = g_ref[:, cols]
                o = o_ref[sl, cols]
                mu = jnp.mean(o, axis=-1, keepdims=True)
                oc = o - mu
                rstd = lax.rsqrt(jnp.mean(oc * oc, axis=-1, keepdims=True) + EPS)
                ohat = oc * rstd
                rg = rg_ref[sl, cols]
                sg = 1.0 / (1.0 + jnp.exp(-rg))
                dy = dy_ref[sl, cols].astype(F32)
                don = dy * (rg * sg)
                drg_ref[sl, cols] = dy * (ohat * gain) * (sg * (1.0 + rg * (1.0 - sg)))
                dohat = don * gain
                do_s[sl, cols] = rstd * (dohat - jnp.mean(dohat, axis=-1, keepdims=True)
                                         - ohat * jnp.mean(dohat * ohat, axis=-1, keepdims=True))
                out.append(dgains[hh] + jnp.sum(don * ohat, axis=0, keepdims=True))
            return tuple(out)

        zero_row = jnp.zeros((1, 128), F32)
        dgains = lax.fori_loop(0, n_chunk, norm_step, (zero_row, zero_row))
        dq_ref[...] = jnp.zeros(dq_ref.shape, F32)
        dk_ref[...] = jnp.zeros(dk_ref.shape, F32)
        dv_ref[...] = jnp.zeros(dv_ref.shape, F32)

        chains = [(forward, hh) for forward in (True, False) for hh in (0, 1)]
        terms, ctxw, s0 = [], [], []
        for forward, hh in chains:
            lgd = lg_ref[0 if forward else 1, 2 * pair + hh]
            terms.append(_decay_terms(lgd, chunk, forward))
            ctxw.append(_ctx_weights(lgd, ctx_len, forward))
            s0.append(_dot_tn((jnp.where(masks[hh], kcf, 0.0) * ctxw[-1][0]).astype(BF16), vc_ref[:, hh * 128:(hh + 1) * 128]))

        def chunk_at(t, ascending):
            n = t if ascending else n_chunk - 1 - t
            return n, pl.ds(pl.multiple_of(n * chunk, chunk), chunk)

        def state_step(t, states):
            new = []
            for c, (forward, hh) in enumerate(chains):
                n, sl = chunk_at(t, forward)
                wk, cd = terms[c][3], terms[c][6]
                s_st[c, n] = states[c]
                kf = jnp.where(masks[hh], k_ref[sl, :].astype(F32), 0.0)
                new.append(cd * states[c] + _dot_tn((kf * wk).astype(BF16), v_ref[sl, hh * 128:(hh + 1) * 128]))
            return tuple(new)

        lax.fori_loop(0, n_chunk, state_step, tuple(s0))

        both = [terms[hh][0] + terms[2 + hh][0] for hh in (0, 1)]

        def grad_step(t, carry):
            out = [None] * len(chains)
            in_chunk_b = [None, None]
            for forward in (True, False):
                n, sl = chunk_at(t, not forward)
                qb = q_ref[sl, :]
                kf_all = k_ref[sl, :].astype(F32)
                dq_sum = jnp.zeros((chunk, 128), F32)
                dk_sum = jnp.zeros((chunk, 128), F32)
                for hh in (0, 1):
                    c = (0 if forward else 2) + hh
                    g_next, dlg = carry[c]
                    dmat, dist, wq, wk, e_q, e_k, cd = terms[c]
                    cols = slice(hh * 128, (hh + 1) * 128)
                    qm = jnp.where(masks[hh], qb, jnp.zeros((), BF16))
                    kf = jnp.where(masks[hh], kf_all, 0.0)
                    kb = kf.astype(BF16)
                    vb = v_ref[sl, cols]
                    do = do_s[sl, cols]
                    dob = do.astype(BF16)
                    s_n = s_st[c, n]
                    s_nb = s_n.astype(BF16)
                    gb = g_next.astype(BF16)
                    dk_cross = wk * _dot_nt(vb, gb)
                    dv = _dot((kf * wk).astype(BF16), gb)
                    o_cross = wq * _dot(qm, s_nb)
                    dq_sum = dq_sum + wq * _dot_nt(dob, s_nb)
                    dk_sum = dk_sum + dk_cross
                    dlg = (dlg + chunk * cd * jnp.sum(g_next * s_n, keepdims=True)
                           + jnp.sum(e_k * jnp.sum(kf * dk_cross, axis=-1, keepdims=True), keepdims=True)
                           + jnp.sum(e_q * jnp.sum(o_cross * do, axis=-1, keepdims=True), keepdims=True))
                    if forward:
                        a_raw = _dot_nt(qm, kb)
                        da_raw = _dot_nt(dob, vb)
                        prod = a_raw * da_raw
                        dlg = dlg + jnp.sum(dist * dmat * prod, keepdims=True)
                        in_chunk_b[hh] = jnp.sum(terms[2 + hh][1] * terms[2 + hh][0] * prod, keepdims=True)
                        dab = (da_raw * both[hh]).astype(BF16)
                        dq_sum = dq_sum + _dot(dab, kb)
                        dk_sum = dk_sum + _dot_tn(dab, qm)
                        dv = dv + _dot_tn((a_raw * both[hh]).astype(BF16), dob)
                    else:
                        dlg = dlg + in_chunk_b[hh]
                    dv_ref[sl, cols] += dv
                    out[c] = (cd * g_next + _dot_tn((qm.astype(F32) * wq).astype(BF16), dob), dlg)
                dq_ref[sl, :] += dq_sum
                dk_ref[sl, :] += dk_sum
            return tuple(out)

        zero = (jnp.zeros((128, 128), F32), jnp.zeros((1, 1), F32))
        res = lax.fori_loop(0, n_chunk, grad_step, (zero,) * len(chains))
        dkc_sum = jnp.zeros((ctx_len, 128), F32)
        dvc = [jnp.zeros((ctx_len, 128), F32)] * 2
        dlgs = []
        for c, (forward, hh) in enumerate(chains):
            ds0, dlg = res[c]
            wc, e_c = ctxw[c]
            kcm = jnp.where(masks[hh], kcf, 0.0)
            ds0b = ds0.astype(BF16)
            dkc_part = wc * _dot_nt(vc_ref[:, hh * 128:(hh + 1) * 128], ds0b)
            dkc_sum = dkc_sum + dkc_part
            dvc[hh] = dvc[hh] + _dot((kcm * wc).astype(BF16), ds0b)
            dlgs.append(dlg + jnp.sum(e_c * jnp.sum(kcm * dkc_part, axis=-1, keepdims=True), keepdims=True))
        dkc_ref[...] = dkc_sum
        for hh in (0, 1):
            cols = slice(hh * 128, (hh + 1) * 128)
            dvc_ref[:, cols] = dvc[hh]
            st_ref[0, :, cols] = jnp.concatenate([
                dgains[hh], jnp.broadcast_to(dlgs[hh], (1, 128)), jnp.broadcast_to(dlgs[2 + hh], (1, 128)),
                jnp.zeros((5, 128), F32)], axis=0)

    qk, v, kc, vc = _pair_specs(t_lat, seq, ctx_len)
    return _hosted_call(
        body, (rq, rk, rv, rk, rv, pg, osum, dmix, lg, g_ret), name="ret_bwd", grid=(nex, HEADS // 2),
        in_specs=[qk, qk, v, kc, vc, v, v, v, pl.BlockSpec(memory_space=pltpu.SMEM),
                  pl.BlockSpec((1, 256), lambda b, p: (0, p))],
        out_specs=[
            qk, qk, v,
            pl.BlockSpec((ctx_len, 128), lambda b, p: (b, p)),
            pl.BlockSpec((ctx_len, 256), lambda b, p: (b, p)),
            v,
            pl.BlockSpec((1, 8, 256), lambda b, p: (b, 0, p)),
        ],
        out_shape=[
            jax.ShapeDtypeStruct((t_lat, 256), F32), jax.ShapeDtypeStruct((t_lat, 256), F32),
            jax.ShapeDtypeStruct((t_lat, 512), F32), jax.ShapeDtypeStruct((t_ctx, 256), F32),
            jax.ShapeDtypeStruct((t_ctx, 512), F32), jax.ShapeDtypeStruct((t_lat, 512), F32),
            jax.ShapeDtypeStruct((nex, 8, 512), F32),
        ],
        scratch_shapes=[pltpu.VMEM((seq, 256), F32), pltpu.VMEM((4, n_chunk, 128, 128), F32)],
        sem=("parallel", "arbitrary"), rider=rider)


def _matmul_tn(a, b, *, bm, bn, bk, chip_major, name, out_dtype=F32, rider=None):
    tk, m = a.shape
    n = b.shape[1]
    slab = n // N_CHIPS
    per_block = bn // slab if chip_major else 1
    bk = max(c for c in range(LANES, min(bk, tk) + 1, LANES) if tk % c == 0)
    nk = tk // bk
    blk = (per_block, bm, slab) if chip_major else (bm, bn)

    def body(a_ref, b_ref, o_ref, acc_ref):
        k = pl.program_id(2)
        if chip_major:
            parts = [_dot_tn(a_ref[...], b_ref[:, s * slab:(s + 1) * slab]) for s in range(per_block)]
        else:
            parts = [_dot_tn(a_ref[...], b_ref[...])]

        @pl.when(k == 0)
        def _():
            for s, part in enumerate(parts):
                if chip_major:
                    acc_ref[s] = part
                else:
                    acc_ref[...] = part

        @pl.when(k > 0)
        def _():
            for s, part in enumerate(parts):
                if chip_major:
                    acc_ref[s] += part
                else:
                    acc_ref[...] += part

        @pl.when(k == nk - 1)
        def _():
            o_ref[...] = acc_ref[...].astype(out_dtype)

    if chip_major:
        out_spec = pl.BlockSpec(blk, lambda i, j, k: (j, i, 0))
        out_shape = jax.ShapeDtypeStruct((N_CHIPS, m, slab), out_dtype)
    else:
        out_spec = pl.BlockSpec(blk, lambda i, j, k: (i, j))
        out_shape = jax.ShapeDtypeStruct((m, n), out_dtype)
    (out,), carried = _hosted_call(
        body, (a, b), name=name, grid=(m // bm, n // bn, nk),
        in_specs=[pl.BlockSpec((bk, bm), lambda i, j, k: (k, i)), pl.BlockSpec((bk, bn), lambda i, j, k: (k, j))],
        out_specs=[out_spec], out_shape=[out_shape], scratch_shapes=[pltpu.VMEM(blk, F32)],
        sem=("parallel", "parallel", "arbitrary"), rider=rider)
    return out if rider is None else (out, carried)


_LATE = ("w_out", "w_ff1", "w_ff2")
_EARLY = ("w_in", "w_uq", "w_ukv")


def _local_step(x, ctx, tgt, modv, lg, g_attn, g_ffn, g_fin, g_ret, g_q, g_kv, w_in, w_uq, w_ukv, late, place=None,
                *, tm=256, tq=256, chunk=256):
    nex, seq, d = x.shape
    ctx_len = ctx.shape[1]
    t_lat = nex * seq
    tm = min(tm, seq)
    x2 = x.reshape(t_lat, d)
    ctx2 = ctx.reshape(nex * ctx_len, d)
    tgt2 = tgt.reshape(t_lat, d)
    tm_fwd = min(2 * tm, seq)
    cos_t, sin_t = _rope_tables(seq, tm)
    dims = dict(t_lat=t_lat, seq=seq, ctx_len=ctx_len)
    alone = place is None

    (hb, pg, rq, rk, rv, nq, nkv, q, k, v), crossed_a = _pre_fwd(
        x2, ctx2, modv, g_attn, w_in, g_q, g_kv, w_uq, w_ukv, *_rope_tables(seq, tm_fwd), seq=seq, tm=tm_fwd,
        rider=None if alone else _gather_ici_rider([late[2]]))
    (yret, osum), got = _ret_fwd_pair(
        rq, rk, rv, pg, lg, g_ret, chunk=min(2 * chunk, seq), **dims,
        rider=None if alone else _merge_riders(_gather_d2d_rider(crossed_a), _gather_ici_rider([late[3]])))
    (ymla, lse), got_rest = _mla_fwd(
        q, k, v, tq=tq, **dims,
        rider=None if alone else _merge_riders(_gather_rider([late[0], late[1]], staged=True), _gather_d2d_rider(got[1:])))
    w_out, w_ff1, w_ff2a, w_ff2b = late if alone else (got_rest[0], got_rest[1], got[0], got_rest[2])
    mix, act, du, h2, df, dmo, dmix, dxm, st_post = _post(yret, ymla, x2, tgt2, modv, g_ffn, g_fin, w_out.reshape(d, d),
                                                         w_ff1, w_ff2a, w_ff2b, seq=seq, tm=min(tm, 256))
    kw = dict(bm=1024, bn=1024, bk=2048, out_dtype=BF16)
    g_ff2 = _matmul_tn(act, df, chip_major=False, name="gw_ff2", **kw).reshape(N_CHIPS, D_FF // N_CHIPS, d)
    if alone:
        g_ff1 = _matmul_tn(h2, du, chip_major=True, name="gw_ff1", **kw)
        g_out = _matmul_tn(mix, dmo, chip_major=False, name="gw_out", **kw).reshape(N_CHIPS, d // N_CHIPS, d)
        (dq_m, dkl, dkc, dvl, dvc), _ = _mla_bwd(q, k, v, ymla, lse, dmix, tq=tq, **dims)
        (drq, drk, drv, dkc_r, dvc_r, drg, st_ret), _ = _ret_bwd_pair(rq, rk, rv, pg, osum, dmix, lg, g_ret, chunk=chunk,
                                                                      **dims)
        late_out = [g_out, g_ff1, g_ff2]
    else:
        core, slot = place
        g_ff1, x_ff2 = _matmul_tn(h2, du, chip_major=True, name="gw_ff1", rider=_exchange_rider([g_ff2]), **kw)
        g_out, x_ff1 = _matmul_tn(mix, dmo, chip_major=False, name="gw_out", rider=_exchange_rider([g_ff1]), **kw)
        g_out = g_out.reshape(N_CHIPS, d // N_CHIPS, d)
        p_ff2 = _add_half(g_ff2, x_ff2[0], core, "add_half_w_ff2")
        p_ff1 = _add_half(g_ff1, x_ff1[0], core, "add_half_w_ff1")
        (dq_m, dkl, dkc, dvl, dvc), (l_ff2, l_ff1, x_out) = _mla_bwd(
            q, k, v, ymla, lse, dmix, tq=min(seq, 512), **dims,
            rider=_merge_riders(_scatter_rider([p_ff2, p_ff1]), _exchange_rider([g_out])))
        p_out = _add_half(g_out, x_out, core, "add_half_w_out")
        m_ff2 = _sum_chips(p_ff2, l_ff2, slot, "sum_chips_w_ff2")
        m_ff1 = _sum_chips(p_ff1, l_ff1, slot, "sum_chips_w_ff1")
        (drq, drk, drv, dkc_r, dvc_r, drg, st_ret), (l_out,) = _ret_bwd_pair(
            rq, rk, rv, pg, osum, dmix, lg, g_ret, chunk=chunk, **dims, rider=_scatter_rider([p_out]))
        late_out = [_sum_chips(p_out, l_out, slot, "sum_chips_w_out"), m_ff1, m_ff2]
    (dpb, dqf, dkvf, gx, st_pre), _ = _pre_bwd(
        x2, ctx2, modv, g_attn, pg, drq, drk, dkc_r, drv, dvc_r, drg, dq_m, dkl, dkc, dvl, dvc, dxm, w_in, g_q, g_kv,
        w_uq, w_ukv, cos_t, sin_t, seq=seq, tm=tm)
    g_early = [
        _matmul_tn(dpb, hb, bm=IN_PAD // 2, bn=d, bk=1536, chip_major=False, name="gw_in"),
        _matmul_tn(dqf, nq, bm=HEADS * MLA_HEAD, bn=Q_LORA, bk=1536, chip_major=False, name="gw_uq"),
        _matmul_tn(nkv, dkvf, bm=KV_LORA, bn=HEADS * 256, bk=1536, chip_major=True, name="gw_ukv"),
    ]
    return gx.reshape(nex, seq, d), g_early, late_out, st_post, st_ret, st_pre


_ANY = pl.BlockSpec(memory_space=pl.ANY)
_VMEM = pl.BlockSpec(memory_space=pltpu.VMEM)
_OFFSETS = tuple((dx, dy, dc) for dx in (0, 1) for dy in (0, 1) for dc in (0, 1))[1:]
_CHIP_OFFSETS = ((1, 0), (0, 1), (1, 1))


def _place():
    return lax.axis_index("x"), lax.axis_index("y"), lax.axis_index("c")


def _flip(v, d):
    return 1 - v if d else v


def _gather8_rider(a, in_vmem=True):
    def copies(a_ref, o_ref, send, recv):
        x, y, z = _place()
        me = 4 * x + 2 * y + z
        out = []
        for k, (dx, dy, dc) in enumerate(_OFFSETS):
            peer = (_flip(x, dx), _flip(y, dy), _flip(z, dc))
            landing = o_ref.at[4 * peer[0] + 2 * peer[1] + peer[2]]
            out.append((
                pltpu.make_async_remote_copy(src_ref=a_ref, dst_ref=o_ref.at[me], send_sem=send.at[k],
                                             recv_sem=recv.at[k], device_id=peer, device_id_type=MESH),
                pltpu.make_async_remote_copy(src_ref=a_ref, dst_ref=landing, send_sem=send.at[k],
                                             recv_sem=recv.at[k], device_id=peer, device_id_type=MESH)))
        return me, out

    def start(ins, outs, sems):
        me, cps = copies(ins[0], outs[0], sems[0], sems[1])
        pltpu.make_async_copy(ins[0], outs[0].at[me], sems[2]).start()
        for out_cp, _ in cps:
            out_cp.start()

    def finish(ins, outs, sems):
        me, cps = copies(ins[0], outs[0], sems[0], sems[1])
        for out_cp, in_cp in cps:
            in_cp.wait_recv()
            out_cp.wait_send()
        pltpu.make_async_copy(ins[0], outs[0].at[me], sems[2]).wait()

    spec = [_VMEM] if in_vmem else [_ANY]
    return _Rider([a], [jax.ShapeDtypeStruct((N_DEV,) + a.shape, a.dtype)],
                  [pltpu.SemaphoreType.DMA((7,)), pltpu.SemaphoreType.DMA((7,)), pltpu.SemaphoreType.DMA],
                  start, finish, in_specs=spec, out_specs=spec)


def _merge_riders(*riders):
    ins, outs, sems, in_specs, out_specs, aliases, cuts = [], [], [], [], [], {}, []
    for r in riders:
        cuts.append((len(ins), len(outs), len(sems)))
        aliases.update({len(ins) + i: len(outs) + j for i, j in r.aliases.items()})
        ins += r.ins
        outs += r.out_shapes
        sems += r.sems
        in_specs += r.in_specs
        out_specs += r.out_specs

    def part(r, cut, r_ins, r_outs, r_sems):
        return (r_ins[cut[0]:cut[0] + len(r.ins)], r_outs[cut[1]:cut[1] + len(r.out_shapes)],
                r_sems[cut[2]:cut[2] + len(r.sems)])

    def start(r_ins, r_outs, r_sems):
        for r, cut in zip(riders, cuts):
            r.start(*part(r, cut, r_ins, r_outs, r_sems))

    def finish(r_ins, r_outs, r_sems):
        for r, cut in zip(riders, cuts):
            r.finish(*part(r, cut, r_ins, r_outs, r_sems))

    def middle(r_ins, r_outs, r_sems):
        for r, cut in zip(riders, cuts):
            if r.middle is not None:
                r.middle(*part(r, cut, r_ins, r_outs, r_sems))

    return _Rider(ins, outs, sems, start, finish, aliases=aliases, in_specs=in_specs, out_specs=out_specs,
                  middle=middle if any(r.middle is not None for r in riders) else None)


def _allgather8(a, name):
    return _run_rider(_gather8_rider(a), name)[0]


BF16_TILE_ROWS = 16


def _half(o, slot, which):
    r2 = o.shape[1] // 2
    if r2 % BF16_TILE_ROWS == 0:
        return o.at[slot, pl.ds(which * r2, r2)]
    c2 = o.shape[2] // 2
    assert c2 % LANES == 0
    return o.at[slot, :, pl.ds(which * c2, c2)]


def _gather_send(o_refs, send, recv):
    x, y, z = _place()
    chip = 2 * x + y
    for a, o in enumerate(o_refs):
        r2 = o.shape[1] // 2
        mine = _half(o, chip, z)
        for k, (dx, dy) in enumerate(_CHIP_OFFSETS):
            pltpu.make_async_remote_copy(
                src_ref=mine, dst_ref=mine, send_sem=send.at[a, k], recv_sem=recv.at[a, k],
                device_id=(_flip(x, dx), _flip(y, dy), z), device_id_type=MESH).start()


def _gather_landed(o_refs, send, recv, then=None):
    x, y, z = _place()
    chip = 2 * x + y
    for a, o in enumerate(o_refs):
        for k, (dx, dy) in enumerate(_CHIP_OFFSETS):
            landed = _half(o, 2 * _flip(x, dx) + _flip(y, dy), z)
            pltpu.make_async_remote_copy(
                src_ref=landed, dst_ref=landed, send_sem=send.at[a, k], recv_sem=recv.at[a, k],
                device_id=(_flip(x, dx), _flip(y, dy), z), device_id_type=MESH).wait_recv()
            if then is not None:
                then(a, k, landed)
    for a, o in enumerate(o_refs):
        mine = _half(o, chip, z)
        for k, (dx, dy) in enumerate(_CHIP_OFFSETS):
            pltpu.make_async_remote_copy(
                src_ref=mine, dst_ref=mine, send_sem=send.at[a, k], recv_sem=recv.at[a, k],
                device_id=(_flip(x, dx), _flip(y, dy), z), device_id_type=MESH).wait_send()


def _pass_on(o_refs, fsend, frecv, a, k, landed):
    x, y, z = _place()
    pltpu.make_async_remote_copy(
        src_ref=landed, dst_ref=landed, send_sem=fsend.at[a, k], recv_sem=frecv.at[a, k],
        device_id=(x, y, 1 - z), device_id_type=MESH).start()


def _passed_on(o_refs, fsend, frecv):
    x, y, z = _place()
    for a, o in enumerate(o_refs):
        for k, (dx, dy) in enumerate(_CHIP_OFFSETS):
            other = 2 * _flip(x, dx) + _flip(y, dy)
            got = _half(o, other, 1 - z)
            gave = _half(o, other, z)
            pltpu.make_async_remote_copy(
                src_ref=got, dst_ref=got, send_sem=fsend.at[a, k], recv_sem=frecv.at[a, k],
                device_id=(x, y, 1 - z), device_id_type=MESH).wait_recv()
            pltpu.make_async_remote_copy(
                src_ref=gave, dst_ref=gave, send_sem=fsend.at[a, k], recv_sem=frecv.at[a, k],
                device_id=(x, y, 1 - z), device_id_type=MESH).wait_send()


def _gather_finish(o_refs, send, recv, fsend, frecv):
    _gather_landed(o_refs, send, recv, functools.partial(_pass_on, o_refs, fsend, frecv))
    _passed_on(o_refs, fsend, frecv)


class _Rider:
    def __init__(self, ins, out_shapes, sems, start, finish, aliases=None, in_specs=None, out_specs=None, middle=None):
        self.ins, self.out_shapes, self.sems = list(ins), list(out_shapes), list(sems)
        self.start, self.finish, self.aliases = start, finish, dict(aliases or {})
        self.middle = middle
        self.in_specs = list(in_specs) if in_specs else [_ANY] * len(self.ins)
        self.out_specs = list(out_specs) if out_specs else [_ANY] * len(self.out_shapes)


def _run_rider(rider, name):
    r_in, r_out = len(rider.ins), len(rider.out_shapes)

    def body(*refs):
        ins, outs, sems = refs[:r_in], refs[r_in:r_in + r_out], refs[r_in + r_out:]
        rider.start(ins, outs, sems)
        if rider.middle is not None:
            rider.middle(ins, outs, sems)
        rider.finish(ins, outs, sems)

    return pl.pallas_call(
        body, name=name, in_specs=rider.in_specs, out_specs=rider.out_specs, out_shape=rider.out_shapes,
        input_output_aliases=rider.aliases, scratch_shapes=rider.sems,
    )(*rider.ins)


def _hosted_call(body, args, *, name, grid, in_specs, out_specs, out_shape, scratch_shapes=(), sem, rider=None,
                 prefetch=()):
    scratch_shapes = list(scratch_shapes)
    n_pf, n_in, n_out, n_sc = len(prefetch), len(in_specs), len(out_specs), len(scratch_shapes)
    r_in, r_out = (len(rider.ins), len(rider.out_shapes)) if rider else (0, 0)
    last = tuple(g - 1 for g in grid)

    def hosted(*refs):
        p = 0
        parts = []
        for cnt in (n_pf, n_in, r_in, n_out, r_out, n_sc):
            parts.append(refs[p:p + cnt])
            p += cnt
        pf, ins, r_ins, outs, r_outs, scratch = parts
        sems = refs[p:]
        ids = [pl.program_id(a) for a in range(len(grid))]
        is_first = functools.reduce(jnp.logical_and, [i == 0 for i in ids])
        is_last = functools.reduce(jnp.logical_and, [i == e for i, e in zip(ids, last)])

        @pl.when(is_first)
        def _():
            rider.start(r_ins, r_outs, sems)

        if rider.middle is not None:
            linear = functools.reduce(lambda acc, ig: acc * ig[1] + ig[0], zip(ids, grid), 0)

            @pl.when(linear == math.prod(grid) * 3 // 4)
            def _():
                rider.middle(r_ins, r_outs, sems)

        body(*pf, *ins, *outs, *scratch)

        @pl.when(is_last)
        def _():
            rider.finish(r_ins, r_outs, sems)

    if rider is None:
        kern, all_in, all_out, shapes, scratch, aliases, extra = body, list(in_specs), list(out_specs), list(out_shape), \
            scratch_shapes, {}, []
    else:
        kern, all_in, all_out = hosted, list(in_specs) + rider.in_specs, list(out_specs) + rider.out_specs
        shapes, scratch, extra = list(out_shape) + rider.out_shapes, scratch_shapes + rider.sems, rider.ins
        aliases = {n_pf + n_in + i: n_out + j for i, j in rider.aliases.items()}
        sem = ("arbitrary",) * len(grid)
    if prefetch:
        spec = dict(grid_spec=pltpu.PrefetchScalarGridSpec(
            num_scalar_prefetch=n_pf, grid=grid, in_specs=all_in, out_specs=all_out, scratch_shapes=scratch))
    else:
        spec = dict(grid=grid, in_specs=all_in, out_specs=all_out, scratch_shapes=scratch)
    res = pl.pallas_call(kern, name=name, out_shape=shapes, input_output_aliases=aliases,
                         compiler_params=_params(sem, VMEM_LIMIT), **spec)(*prefetch, *args, *extra)
    return list(res[:n_out]), list(res[n_out:])


def _gather_rider(ws, staged=False):
    n = len(ws)
    shapes = [jax.ShapeDtypeStruct(w.shape, w.dtype) for w in ws]
    sems = [pltpu.SemaphoreType.DMA((n, 3))] * 4
    aliases = {a: a for a in range(n)}

    def start(ins, outs, s):
        _gather_send(outs, s[0], s[1])

    if not staged:
        return _Rider(ws, shapes, sems, start, lambda ins, outs, s: _gather_finish(outs, *s), aliases=aliases)
    return _Rider(
        ws, shapes, sems, start, lambda ins, outs, s: _passed_on(outs, s[2], s[3]), aliases=aliases,
        middle=lambda ins, outs, s: _gather_landed(outs, s[0], s[1], functools.partial(_pass_on, outs, s[2], s[3])))


def _gather_ici_rider(ws):
    n = len(ws)
    return _Rider(
        ws, [jax.ShapeDtypeStruct(w.shape, w.dtype) for w in ws], [pltpu.SemaphoreType.DMA((n, 3))] * 2,
        lambda ins, outs, sems: _gather_send(outs, sems[0], sems[1]),
        lambda ins, outs, sems: _gather_landed(outs, sems[0], sems[1]),
        aliases={a: a for a in range(n)})


def _gather_d2d_rider(ws):
    n = len(ws)

    def start(ins, outs, sems):
        x, y, z = _place()
        for a, o in enumerate(outs):
            for k, (dx, dy) in enumerate(_CHIP_OFFSETS):
                _pass_on(outs, sems[0], sems[1], a, k, _half(o, 2 * _flip(x, dx) + _flip(y, dy), z))

    return _Rider(
        ws, [jax.ShapeDtypeStruct(w.shape, w.dtype) for w in ws], [pltpu.SemaphoreType.DMA((n, 3))] * 2,
        start, lambda ins, outs, sems: _passed_on(outs, sems[0], sems[1]), aliases={a: a for a in range(n)})


def _copies_rider(ins, out_shapes, sem_shape, make):
    def start(r_ins, r_outs, sems):
        for cp in make(r_ins, r_outs, sems[0], sems[1]):
            cp.start()

    def finish(r_ins, r_outs, sems):
        for cp in make(r_ins, r_outs, sems[0], sems[1]):
            cp.wait()

    return _Rider(ins, out_shapes, [pltpu.SemaphoreType.DMA(sem_shape)] * 2, start, finish)


def _exchange_rider(gs):
    def make(g_refs, r_refs, send, recv):
        x, y, z = _place()
        return [pltpu.make_async_remote_copy(
            src_ref=g.at[:, pl.ds((1 - z) * (g.shape[1] // 2), g.shape[1] // 2)], dst_ref=r, send_sem=send.at[a],
            recv_sem=recv.at[a], device_id=(x, y, 1 - z), device_id_type=MESH)
            for a, (g, r) in enumerate(zip(g_refs, r_refs))]

    shapes = [jax.ShapeDtypeStruct((g.shape[0], g.shape[1] // 2, g.shape[2]), g.dtype) for g in gs]
    return _copies_rider(gs, shapes, (len(gs),), make)


def _add_half(g, recv, core, name):
    s, r, c = g.shape
    r2 = r // 2
    rb = r2
    for cand in (256, 128, 64):
        if r2 % cand == 0:
            rb = cand
            break
    g4 = g.reshape(s, 2, r2, c)

    def body(core_ref, g_ref, r_ref, o_ref):
        o_ref[...] = (g_ref[...].astype(F32) + r_ref[...].astype(F32)).astype(BF16)

    return pl.pallas_call(
        body, name=name,
        grid_spec=pltpu.PrefetchScalarGridSpec(
            num_scalar_prefetch=1, grid=(s, r2 // rb),
            in_specs=[pl.BlockSpec((None, None, rb, c), lambda i, j, cr: (i, cr[0], j, 0)),
                      pl.BlockSpec((None, rb, c), lambda i, j, cr: (i, j, 0))],
            out_specs=pl.BlockSpec((None, rb, c), lambda i, j, cr: (i, j, 0))),
        out_shape=jax.ShapeDtypeStruct((s, r2, c), BF16),
        compiler_params=_params(("parallel", "parallel")),
    )(core, g4, recv)


def _scatter_rider(ps):
    def make(p_refs, o_refs, send, recv):
        x, y, z = _place()
        copies = []
        for a, (p, o) in enumerate(zip(p_refs, o_refs)):
            for k, (dx, dy) in enumerate(_CHIP_OFFSETS):
                other = 2 * _flip(x, dx) + _flip(y, dy)
                copies.append(pltpu.make_async_remote_copy(
                    src_ref=p.at[other], dst_ref=o.at[k], send_sem=send.at[a, k], recv_sem=recv.at[a, k],
                    device_id=(_flip(x, dx), _flip(y, dy), z), device_id_type=MESH))
        return copies

    shapes = [jax.ShapeDtypeStruct((3,) + p.shape[1:], p.dtype) for p in ps]
    return _copies_rider(ps, shapes, (len(ps), 3), make)


def _sum_chips(p, landed, chip, name):
    _, r2, c = p.shape
    rb = r2
    for cand in (256, 128, 64):
        if r2 % cand == 0:
            rb = cand
            break

    def body(s_ref, p_ref, l_ref, o_ref):
        acc = p_ref[...].astype(F32)
        for k in range(3):
            acc = acc + l_ref[k].astype(F32)
        o_ref[...] = acc

    return pl.pallas_call(
        body, name=name,
        grid_spec=pltpu.PrefetchScalarGridSpec(
            num_scalar_prefetch=1, grid=(r2 // rb,),
            in_specs=[pl.BlockSpec((None, rb, c), lambda i, s: (s[0], i, 0)),
                      pl.BlockSpec((3, rb, c), lambda i, s: (0, i, 0))],
            out_specs=pl.BlockSpec((rb, c), lambda i, s: (i, 0))),
        out_shape=jax.ShapeDtypeStruct((r2, c), F32),
        compiler_params=_params(("parallel",)),
    )(chip, p, landed)


def _swap_rider(hs):
    def make(h_refs, o_refs, send, recv):
        x, y, z = _place()
        return [pltpu.make_async_remote_copy(
            src_ref=h, dst_ref=o, send_sem=send.at[a], recv_sem=recv.at[a], device_id=(x, y, 1 - z),
            device_id_type=MESH) for a, (h, o) in enumerate(zip(h_refs, o_refs))]

    return _copies_rider(hs, [jax.ShapeDtypeStruct(h.shape, h.dtype) for h in hs], (len(hs),), make)


def _reduce_scatter_vmem(gs, rows, rider, name):
    n = len(gs)
    r_in, r_out = len(rider.ins), len(rider.out_shapes)
    halves = [(r // 2, g.shape[-1]) for g, (r, _) in zip(gs, rows)]

    def body(*refs):
        p = 0
        parts = []
        for cnt in (n, r_in, n, n, r_out, n, n, n, 6):
            parts.append(refs[p:p + cnt])
            p += cnt
        g_refs, r_ins, mine, theirs, r_outs, recv, part, land, sems = parts
        r_sems = refs[p:]
        xs, xr, ss, sr, ws, wr = sems
        x, y, z = _place()
        chip = 2 * x + y
        sib = (x, y, 1 - z)
        rider.start(r_ins, r_outs, r_sems)

        def half_of(a, s, which):
            r2 = halves[a][0]
            if len(g_refs[a].shape) == 3:
                return g_refs[a].at[s, pl.ds(pl.multiple_of(which * r2, 8), r2)]
            return g_refs[a].at[pl.ds(pl.multiple_of(s * rows[a][1] + which * r2, 8), r2)]

        exchange = [pltpu.make_async_remote_copy(
            src_ref=half_of(a, s, 1 - z), dst_ref=recv[a].at[s], send_sem=xs.at[a, s], recv_sem=xr.at[a, s],
            device_id=sib, device_id_type=MESH) for a in range(n) for s in range(N_CHIPS)]
        for cp in exchange:
            cp.start()
        for cp in exchange:
            cp.wait()
        for a in range(n):
            for s in range(N_CHIPS):
                part[a][s] = (half_of(a, s, z)[...] + recv[a][s]).astype(BF16)
        scatter = []
        for a in range(n):
            for k, (dx, dy) in enumerate(_CHIP_OFFSETS):
                other = 2 * _flip(x, dx) + _flip(y, dy)
                scatter.append(pltpu.make_async_remote_copy(
                    src_ref=part[a].at[other], dst_ref=land[a].at[k], send_sem=ss.at[a, k], recv_sem=sr.at[a, k],
                    device_id=(_flip(x, dx), _flip(y, dy), z), device_id_type=MESH))
        for cp in scatter:
            cp.start()
        for cp in scatter:
            cp.wait()
        for a in range(n):
            acc = part[a][chip].astype(F32)
            for k in range(3):
                acc = acc + land[a][k].astype(F32)
            mine[a][...] = acc
        swap = [pltpu.make_async_remote_copy(
            src_ref=mine[a], dst_ref=theirs[a], send_sem=ws.at[a], recv_sem=wr.at[a], device_id=sib,
            device_id_type=MESH) for a in range(n)]
        for cp in swap:
            cp.start()
        for cp in swap:
            cp.wait()
        rider.finish(r_ins, r_outs, r_sems)

    half_shapes = [jax.ShapeDtypeStruct(h, F32) for h in halves]
    res = pl.pallas_call(
        body, name=name, in_specs=[_VMEM] * n + rider.in_specs, out_specs=[_VMEM] * (2 * n) + rider.out_specs,
        out_shape=half_shapes + half_shapes + rider.out_shapes,
        scratch_shapes=[pltpu.VMEM((N_CHIPS,) + h, F32) for h in halves] + [pltpu.VMEM((N_CHIPS,) + h, BF16) for h in halves]
        + [pltpu.VMEM((3,) + h, BF16) for h in halves]
        + [pltpu.SemaphoreType.DMA((n, N_CHIPS))] * 2 + [pltpu.SemaphoreType.DMA((n, 3))] * 2
        + [pltpu.SemaphoreType.DMA((n,))] * 2 + rider.sems,
        input_output_aliases={n + i: 2 * n + j for i, j in rider.aliases.items()},
        compiler_params=_params(None, VMEM_LIMIT),
    )(*gs, *rider.ins)
    return list(res[:n]), list(res[n:2 * n]), list(res[2 * n:])


SMALL_ROWS = 32
PACK_ROWS = 16


def _pack_small(st_post, st_ret, st_pre):
    d = st_post.shape[2]

    def body(po_ref, re_ref, pr_ref, o_ref):
        o_ref[...] = jnp.zeros(o_ref.shape, F32)
        o_ref[0:1, :] = pr_ref[0, 2:3, :] + pr_ref[1, 2:3, :] + pr_ref[2, 2:3, :]
        o_ref[1:2, :] = po_ref[0, 4:5, :] + po_ref[1, 4:5, :]
        o_ref[2:3, :] = po_ref[0, 5:6, :] + po_ref[1, 5:6, :]
        o_ref[3:4, 0:512] = re_ref[0, 0:1, :] + re_ref[1, 0:1, :]
        o_ref[4:5, :] = pr_ref[0, 3:4, :] + pr_ref[1, 3:4, :] + pr_ref[2, 3:4, :]
        o_ref[5:6, :] = pr_ref[0, 4:5, :] + pr_ref[1, 4:5, :] + pr_ref[2, 4:5, :]
        lane = lax.broadcasted_iota(jnp.int32, (1, LANES), 1)
        for row, src in ((6, 1), (10, 2)):
            acc = jnp.zeros((1, LANES), F32)
            for hd in range(HEADS):
                grp = re_ref[0, src:src + 1, hd * LANES:(hd + 1) * LANES] + re_ref[1, src:src + 1, hd * LANES:(hd + 1) * LANES]
                acc = acc + jnp.where(lane == hd, grp, 0.0)
            o_ref[row:row + 1, 0:LANES] = acc
        o_ref[7:8, :] = po_ref[0, 6:7, :] + po_ref[1, 6:7, :]
        o_ref[8:9, :] = pr_ref[2, 0:1, :]
        o_ref[9:10, :] = pr_ref[2, 1:2, :]
        for e in range(2):
            b = 12 + 6 * e
            o_ref[b:b + 1, :] = pr_ref[e, 0:1, :]
            o_ref[b + 1:b + 2, :] = pr_ref[e, 1:2, :]
            o_ref[b + 2:b + 3, :] = po_ref[e, 3:4, :]
            o_ref[b + 3:b + 4, :] = po_ref[e, 0:1, :]
            o_ref[b + 4:b + 5, :] = po_ref[e, 1:2, :]
            o_ref[b + 5:b + 6, :] = po_ref[e, 2:3, :]

    return pl.pallas_call(body, name="pack_small", out_shape=jax.ShapeDtypeStruct((SMALL_ROWS, d), F32))(st_post, st_ret, st_pre)


def _small_reduce(gathered):
    d = gathered.shape[2]

    def body(g_ref, o_ref):
        tot = g_ref[0, 0:PACK_ROWS, :]
        for dev in range(1, N_DEV):
            tot = tot + g_ref[dev, 0:PACK_ROWS, :]
        o_ref[0:PACK_ROWS, :] = tot
        for j in range(6):
            acc = g_ref[0, 12 + j:13 + j, :] + g_ref[0, 18 + j:19 + j, :]
            for dev in range(1, N_DEV):
                acc = acc + g_ref[dev, 12 + j:13 + j, :] + g_ref[dev, 18 + j:19 + j, :]
            if j < 2:
                acc = acc + o_ref[8 + j:9 + j, :]
            o_ref[PACK_ROWS + j:PACK_ROWS + j + 1, :] = acc
        o_ref[PACK_ROWS + 6:PACK_ROWS + 8, :] = jnp.zeros((2, d), F32)

    return pl.pallas_call(body, name="small_reduce", out_shape=jax.ShapeDtypeStruct((PACK_ROWS + 8, d), F32))(gathered)


_SMALL = (("g_attn", 0, 1024), ("g_ffn", 1, 1024), ("g_final", 2, 1024), ("g_ret", 3, 512), ("g_q_lora", 4, 384),
          ("g_kv_lora", 5, 256), ("ret_decay_fwd", 6, HEADS), ("ret_decay_bwd", 10, HEADS))
_SMALL_NAMES = tuple(s[0] for s in _SMALL) + ("c_ctx", "b_ada")


def _small_final(tot, dcc, sg8, ws, ms, vs):
    d = tot.shape[1]
    n = len(_SMALL_NAMES)

    def body(*refs):
        t_ref, dcc_ref, sg_ref = refs[0:3]
        w_refs, m_refs, v_refs = refs[3:3 + n], refs[3 + n:3 + 2 * n], refs[3 + 2 * n:3 + 3 * n]
        outs = refs[3 + 3 * n:]
        g_refs, d_refs, mo_refs, vo_refs = outs[0:n], outs[n:2 * n], outs[2 * n:3 * n], outs[3 * n:4 * n]
        l_ref = outs[4 * n]

        def update(i, g, sl=None):
            pick = (lambda r: r[...]) if sl is None else (lambda r: r[:, sl])
            dl, mn, vn = _adam_math(pick(w_refs[i]), g, pick(m_refs[i]), pick(v_refs[i]))
            if sl is None:
                g_refs[i][...], d_refs[i][...], mo_refs[i][...], vo_refs[i][...] = g, dl, mn, vn
            else:
                g_refs[i][:, sl], d_refs[i][:, sl], mo_refs[i][:, sl], vo_refs[i][:, sl] = g, dl, mn, vn

        for i, (name, row, width) in enumerate(_SMALL):
            g = t_ref[row:row + 1, 0:width]
            if name == "ret_decay_fwd":
                g = g * sg_ref[0:1, 0:width]
            elif name == "ret_decay_bwd":
                g = g * sg_ref[1:2, 0:width]
            update(i, g)
        i_cc, i_b = n - 2, n - 1
        cc = w_refs[i_cc][...]
        s = 1.0 / (1.0 + jnp.exp(-cc))
        dsilu = dcc_ref[0, 0:1, :] + dcc_ref[2, 0:1, :] + dcc_ref[4, 0:1, :] + dcc_ref[6, 0:1, :]
        update(i_cc, dsilu * (s * (1.0 + cc * (1.0 - s))))
        for j in range(6):
            update(i_b, t_ref[PACK_ROWS + j:PACK_ROWS + j + 1, :], pl.ds(j * d, d))
        l_ref[...] = jnp.broadcast_to((0.5 / d) * jnp.sum(t_ref[7:8, :], keepdims=True), l_ref.shape)

    shapes = [jax.ShapeDtypeStruct(a.shape, F32) for a in ws]
    outs = pl.pallas_call(
        body, name="small_final", out_shape=shapes * 4 + [jax.ShapeDtypeStruct((8, LANES), F32)],
    )(tot, dcc, sg8, *ws, *ms, *vs)
    return outs[0:n], outs[n:2 * n], outs[2 * n:3 * n], outs[3 * n:4 * n], outs[4 * n]


_WEIGHTS = ("c_ctx", "w_ada", "b_ada", "g_attn", "g_ffn", "w_in", "ret_decay_fwd", "ret_decay_bwd", "g_ret", "g_q_lora",
            "w_uq", "g_kv_lora", "w_ukv", "w_out", "w_ff1", "w_ff2", "g_final")
_BIG = ("w_in", "w_uq", "w_ukv", "w_out", "w_ff1", "w_ff2")
_TRANSPOSED = ("w_in", "w_uq")


def kernel(x, c, ctx, c_ctx, w_ada, b_ada, g_attn, g_ffn, w_in, ret_decay_fwd, ret_decay_bwd, g_ret, g_q_lora, w_uq, g_kv_lora, w_ukv, w_out, w_ff1, w_ff2, g_final, loss_target, m_c_ctx, m_w_ada, m_b_ada, m_g_attn, m_g_ffn, m_w_in, m_ret_decay_fwd, m_ret_decay_bwd, m_g_ret, m_g_q_lora, m_w_uq, m_g_kv_lora, m_w_ukv, m_w_out, m_w_ff1, m_w_ff2, m_g_final, v_c_ctx, v_w_ada, v_b_ada, v_g_attn, v_g_ffn, v_w_in, v_ret_decay_fwd, v_ret_decay_bwd, v_g_ret, v_g_q_lora, v_w_uq, v_g_kv_lora, v_w_ukv, v_w_out, v_w_ff1, v_w_ff2, v_g_final):
    w = dict(c_ctx=c_ctx, w_ada=w_ada, b_ada=b_ada, g_attn=g_attn, g_ffn=g_ffn, w_in=w_in, ret_decay_fwd=ret_decay_fwd,
             ret_decay_bwd=ret_decay_bwd, g_ret=g_ret, g_q_lora=g_q_lora, w_uq=w_uq, g_kv_lora=g_kv_lora, w_ukv=w_ukv,
             w_out=w_out, w_ff1=w_ff1, w_ff2=w_ff2, g_final=g_final)
    m = dict(c_ctx=m_c_ctx, w_ada=m_w_ada, b_ada=m_b_ada, g_attn=m_g_attn, g_ffn=m_g_ffn, w_in=m_w_in,
             ret_decay_fwd=m_ret_decay_fwd, ret_decay_bwd=m_ret_decay_bwd, g_ret=m_g_ret, g_q_lora=m_g_q_lora, w_uq=m_w_uq,
             g_kv_lora=m_g_kv_lora, w_ukv=m_w_ukv, w_out=m_w_out, w_ff1=m_w_ff1, w_ff2=m_w_ff2, g_final=m_g_final)
    v = dict(c_ctx=v_c_ctx, w_ada=v_w_ada, b_ada=v_b_ada, g_attn=v_g_attn, g_ffn=v_g_ffn, w_in=v_w_in,
             ret_decay_fwd=v_ret_decay_fwd, ret_decay_bwd=v_ret_decay_bwd, g_ret=v_g_ret, g_q_lora=v_g_q_lora, w_uq=v_w_uq,
             g_kv_lora=v_g_kv_lora, w_ukv=v_w_ukv, w_out=v_w_out, w_ff1=v_w_ff1, w_ff2=v_w_ff2, g_final=v_g_final)
    xi, yi, ci = lax.axis_index("x"), lax.axis_index("y"), lax.axis_index("c")
    chip = 2 * xi + yi
    dev = 2 * chip + ci
    nex, seq, d = x.shape
    n_ada = w_ada.shape[2]

    dec = jnp.zeros((8, LANES), F32).at[0, :HEADS].set(ret_decay_fwd[0]).at[1, :HEADS].set(ret_decay_bwd[0])
    lg8, sg8 = _decay_prep(dec)
    lg = lg8[:2, :HEADS]

    def shard_of(t, k):
        return t[k][0].T if k in _TRANSPOSED else t[k][0]

    shard = {k: shard_of(w, k) for k in _BIG}
    head_rows = MLA_NOPE + MLA_ROPE
    shard["w_uq"] = jnp.pad(shard["w_uq"], ((0, MLA_HEAD - head_rows), (0, 0)))
    slot = chip.reshape(1).astype(jnp.int32)
    core = ci.reshape(1).astype(jnp.int32)
    slots = {k: _cast_into_slot(shard[k], slot, "cast_" + k)[0] for k in _BIG if k not in ("w_ff1", "w_ff2")}
    half_ff = shard["w_ff2"].shape[0] // 2
    slots["w_ff2a"] = _cast_into_slot(shard["w_ff2"], slot, "cast_w_ff2a", rows=(0, half_ff))[0]
    slots["w_ff2b"] = _cast_into_slot(shard["w_ff2"], slot, "cast_w_ff2b", rows=(half_ff, half_ff))[0]
    slots["w_ff1"], (w_in_x, w_uq_x, w_ukv_x, c8) = _cast_into_slot(
        shard["w_ff1"], slot, "cast_w_ff1",
        rider=_merge_riders(_gather_ici_rider([slots[k] for k in _EARLY]),
                            _gather8_rider(jnp.pad(c, ((0, 8 - nex), (0, 0))), in_vmem=False)))

    a_in = jnp.concatenate([c8[:, :nex].reshape(N_DEV * nex, d), c_ctx.reshape(1, d), jnp.zeros((7, d), F32)], axis=0)
    b_sh = lax.dynamic_slice(b_ada, (0, chip * n_ada), (1, n_ada))
    mod_sh = _mod_fwd(a_in, w_ada[0], b_sh)
    mod8, w_in_f, w_uq_k, w_ukv_k = _run_rider(
        _merge_riders(_gather8_rider(mod_sh), _gather_d2d_rider([w_in_x, w_uq_x, w_ukv_x])), "ag_early")
    w_in_k = jnp.pad(w_in_f.reshape(IN_COLS, d), ((0, IN_PAD - IN_COLS), (0, 0)))
    mod_all = mod8[0::2].transpose(1, 0, 2).reshape(a_in.shape[0], N_CHIPS * n_ada)
    mod_me = lax.dynamic_slice(mod_all, (nex * dev, 0), (nex, N_CHIPS * n_ada)).reshape(nex, 6, d)
    mod_c = mod_all[N_DEV * nex].reshape(1, 6, d)
    modv = jnp.pad(jnp.concatenate([mod_me, mod_c], axis=0), ((0, 0), (0, 2), (0, 0)))

    gx, g_early, late, st_post, st_ret, st_pre = _local_step(
        x, ctx, loss_target, modv, lg, g_attn, g_ffn, g_final.reshape(1, d), g_ret, g_q_lora, g_kv_lora,
        w_in_k, w_uq_k, w_ukv_k, [slots[k] for k in ("w_out", "w_ff1", "w_ff2a", "w_ff2b")], (core, slot))

    mine, theirs, (*late_theirs, gathered) = _reduce_scatter_vmem(
        g_early, [(IN_COLS // N_CHIPS, IN_COLS // N_CHIPS), (head_rows, MLA_HEAD), (KV_LORA, KV_LORA)],
        _merge_riders(_swap_rider(late), _gather8_rider(_pack_small(st_post, st_ret, st_pre))), "rs_early")
    tot = _small_reduce(gathered)
    dm = jnp.concatenate([
        gathered[:, 12:24].reshape(N_DEV * nex, 6 * d),
        jnp.concatenate([tot[8:10].reshape(1, 2 * d), jnp.zeros((1, 4 * d), F32)], axis=1),
        jnp.zeros((7, 6 * d), F32)], axis=0)
    dm_sh = lax.dynamic_slice(dm, (0, chip * n_ada), (dm.shape[0], n_ada))
    g_ada, da = _mod_bwd(a_in, dm_sh, w_ada[0])
    dcc = _allgather8(da[N_DEV * nex:], "ag_dcc")
    halves = dict(zip(_EARLY, zip(mine, theirs)))
    halves.update(zip(_LATE, zip(late, late_theirs)))
    grad, delta, new_m, new_v = {}, {}, {}, {}
    for k in _BIG:
        a, b = halves[k]
        res = _adamw_halves(shard_of(w, k), a, b, shard_of(m, k), shard_of(v, k), core, "adamw_" + k)
        grad[k], delta[k], new_m[k], new_v[k] = [(o.T if k in _TRANSPOSED else o).reshape(w[k].shape) for o in res]

    shp = w_ada.shape
    outs, _ = _adamw(w_ada[0], g_ada, m["w_ada"][0], v["w_ada"][0], "adamw_w_ada")
    grad["w_ada"] = g_ada.reshape(shp)
    delta["w_ada"], new_m["w_ada"], new_v["w_ada"] = [o.reshape(shp) for o in outs]
    rows = [{k: t[k].reshape(1, -1) for k in _SMALL_NAMES} for t in (w, m, v)]
    small = _small_final(tot, dcc, sg8, *[[t[k] for k in _SMALL_NAMES] for t in rows])
    for res, outs in zip((grad, delta, new_m, new_v), small[:4]):
        for k, o in zip(_SMALL_NAMES, outs):
            res[k] = o.reshape(w[k].shape)
    return (small[4][0, 0], gx, *[grad[k] for k in _WEIGHTS], *[delta[k] for k in _WEIGHTS],
            *[new_m[k] for k in _WEIGHTS], *[new_v[k] for k in _WEIGHTS])
```

```python
import functools
import math

import jax
import jax.numpy as jnp
from jax import lax
from jax.experimental import pallas as pl
from jax.experimental.pallas import tpu as pltpu

F32 = jnp.float32
BF16 = jnp.bfloat16
MESH = pl.DeviceIdType.MESH

EPS = 1e-6
D_MODEL = 1024
D_FF = 4096
HEADS = 4
RET_DK = 64
RET_DV = 128
MLA_NOPE = 128
MLA_ROPE = 64
MLA_HEAD = 256
Q_LORA = 384
KV_LORA = 256
GRID_W = 64
ROPE_BASE = 10000.0
IN_COLS = 2240
IN_PAD = 2304
PG_COLS = 1152
N_CHIPS = 4
N_DEV = 8
LANES = 128
ADAM_LR = 0.001
ADAM_B1 = 0.9
ADAM_B2 = 0.999
ADAM_EPS = 1e-08
ADAM_WD = 0.01
ADAM_STEP = 10
VMEM_LIMIT = 56 * 1024 * 1024


def _dot(a, b):
    return jnp.dot(a, b, preferred_element_type=F32)


def _dot_nt(a, b):
    return lax.dot_general(a, b, (((1,), (1,)), ((), ())), preferred_element_type=F32)


def _dot_tn(a, b):
    return lax.dot_general(a, b, (((0,), (0,)), ((), ())), preferred_element_type=F32)


def _params(sem=None, vmem=None):
    return pltpu.CompilerParams(dimension_semantics=sem, vmem_limit_bytes=vmem)


def _full(shape):
    n = len(shape)
    return pl.BlockSpec(shape, lambda *_: (0,) * n)


def _once(shape):
    n = len(shape)
    return pl.BlockSpec(shape, lambda *_: (0,) * n, pipeline_mode=pl.Buffered(1))


def _rope(x, cos, sin):
    w = x.shape[-1]
    lo = (lax.broadcasted_iota(jnp.int32, (1, w), 1) % 64) < 32
    swapped = jnp.where(lo, pltpu.roll(x, w - 32, 1), pltpu.roll(x, 32, 1))
    return x * cos + swapped * sin


def _rope_t(g, cos, sin):
    w = g.shape[-1]
    lo = (lax.broadcasted_iota(jnp.int32, (1, w), 1) % 64) < 32
    t = g * sin
    swapped = jnp.where(lo, pltpu.roll(t, w - 32, 1), pltpu.roll(t, 32, 1))
    return g * cos + swapped


def _rope_tables(seq, tm):
    rows = seq // GRID_W
    row = jnp.repeat(jnp.arange(rows, dtype=F32), GRID_W)
    col = jnp.tile(jnp.arange(GRID_W, dtype=F32), rows)
    n_freq = RET_DK // 4
    freq = ROPE_BASE ** (-jnp.arange(n_freq, dtype=F32) / n_freq)
    ang = jnp.concatenate([row[:, None] * freq, col[:, None] * freq], axis=-1)
    cos, sin = jnp.cos(ang), jnp.sin(ang)
    cos_t = jnp.tile(jnp.concatenate([cos, cos], -1), (1, HEADS))
    sin_t = jnp.tile(jnp.concatenate([-sin, sin], -1), (1, HEADS))
    cos_t = jnp.concatenate([cos_t, jnp.ones((tm, 4 * RET_DK), F32)], 0)
    sin_t = jnp.concatenate([sin_t, jnp.zeros((tm, 4 * RET_DK), F32)], 0)
    return cos_t, sin_t


def _adam_math(w, g, m, v):
    mn = ADAM_B1 * m + (1.0 - ADAM_B1) * g
    vn = ADAM_B2 * v + (1.0 - ADAM_B2) * (g * g)
    m_hat = mn / (1.0 - ADAM_B1 ** ADAM_STEP)
    v_hat = vn / (1.0 - ADAM_B2 ** ADAM_STEP)
    return -ADAM_LR * (m_hat / (jnp.sqrt(v_hat) + ADAM_EPS) + ADAM_WD * w), mn, vn


def _cast_into_slots(pieces, slot, name, rider=None):
    c = pieces[0][0].shape[1]
    rb = max(b for b in range(16, 257, 16) if all(cnt % b == 0 and st % b == 0 for _, st, cnt in pieces))
    nbs = [cnt // rb for _, _, cnt in pieces]
    starts = [sum(nbs[:s]) for s in range(len(pieces))]

    def body(s_ref, *refs):
        i = pl.program_id(0)
        for s in range(len(pieces)):
            @pl.when(jnp.logical_and(i >= starts[s], i < starts[s] + nbs[s]))
            def _():
                refs[len(pieces) + s][...] = refs[s][...].astype(BF16)

    in_specs, out_specs = [], []
    for (_, first_row, _), nb, st in zip(pieces, nbs, starts):
        in_specs.append(pl.BlockSpec((rb, c), lambda i, s, nb=nb, st=st, f=first_row // rb: (f + jnp.clip(i - st, 0, nb - 1), 0)))
        out_specs.append(pl.BlockSpec((None, rb, c), lambda i, s, nb=nb, st=st: (s[0], jnp.clip(i - st, 0, nb - 1), 0)))
    return _hosted_call(
        body, [w for w, _, _ in pieces], name=name, grid=(sum(nbs),), prefetch=(slot,), in_specs=in_specs,
        out_specs=out_specs, out_shape=[jax.ShapeDtypeStruct((N_CHIPS, cnt, c), BF16) for _, _, cnt in pieces],
        sem=("arbitrary",), rider=rider)


def _cast_into_slot(w, slot, name):
    return _cast_into_slots([(w, 0, w.shape[0])], slot, name)[0][0]


def _adamw_halves(w, mine, theirs, m, v, core, name):
    r, c = w.shape
    r2 = r // 2
    rb = max(b for b in range(8, r2 + 1, 8) if r2 % b == 0 and b * c * 4 <= (1 << 21))
    nbh = r2 // rb

    def body(z_ref, w_ref, a_ref, b_ref, m_ref, v_ref, g_ref, d_ref, mo_ref, vo_ref):
        here = (pl.program_id(0) // nbh) == z_ref[0]
        gg = jnp.where(here, a_ref[...], b_ref[...])
        g_ref[...] = gg
        d_ref[...], mo_ref[...], vo_ref[...] = _adam_math(w_ref[...], gg, m_ref[...], v_ref[...])

    spec = pl.BlockSpec((rb, c), lambda i, z: (i, 0))
    a_spec = pl.BlockSpec((rb, c), lambda i, z: (jnp.clip(i - z[0] * nbh, 0, nbh - 1), 0))
    b_spec = pl.BlockSpec((rb, c), lambda i, z: (jnp.clip(i - (1 - z[0]) * nbh, 0, nbh - 1), 0))
    shp = jax.ShapeDtypeStruct((r, c), F32)
    return pl.pallas_call(
        body, name=name,
        grid_spec=pltpu.PrefetchScalarGridSpec(
            num_scalar_prefetch=1, grid=(r // rb,), in_specs=[spec, a_spec, b_spec, spec, spec], out_specs=[spec] * 4),
        out_shape=[shp] * 4,
        compiler_params=_params(("parallel",)),
    )(core, w, mine, theirs, m, v)


def _adamw(w, g, m, v, name, rider=None):
    r, c = w.shape
    rb = r
    for cand in (256, 128, 64, 32, 16, 8):
        if r % cand == 0 and cand * c * 4 <= (1 << 20):
            rb = cand
            break
    if r * c * 4 <= (1 << 20):
        rb = r

    def body(w_ref, g_ref, m_ref, v_ref, d_ref, mo_ref, vo_ref):
        d_ref[...], mo_ref[...], vo_ref[...] = _adam_math(w_ref[...], g_ref[...], m_ref[...], v_ref[...])

    spec = pl.BlockSpec((rb, c), lambda i: (i, 0))
    shp = jax.ShapeDtypeStruct((r, c), F32)
    return _hosted_call(
        body, (w, g, m, v), name=name, grid=(r // rb,), in_specs=[spec] * 4, out_specs=[spec] * 3, out_shape=[shp] * 3,
        sem=("parallel",), rider=rider)


def _decay_prep(dec):
    def body(d_ref, lg_ref, sg_ref):
        d = d_ref[...]
        lg_ref[...] = jnp.minimum(d, 0.0) - jnp.log(1.0 + jnp.exp(-jnp.abs(d)))
        sg_ref[...] = 1.0 / (1.0 + jnp.exp(d))

    shp = jax.ShapeDtypeStruct(dec.shape, F32)
    return pl.pallas_call(body, name="decay_prep", out_shape=[shp, shp])(dec)


def _mod_fwd(a_in, w_ada, b_sh):
    rows, d = a_in.shape
    n = w_ada.shape[1]
    bn = 512

    def body(a_ref, w_ref, b_ref, o_ref):
        a = a_ref[...]
        s = (a / (1.0 + jnp.exp(-a))).astype(BF16)
        o_ref[...] = _dot(s, w_ref[...].astype(BF16)) + b_ref[...]

    return pl.pallas_call(
        body, name="mod_fwd", grid=(n // bn,),
        in_specs=[_full((rows, d)), pl.BlockSpec((d, bn), lambda j: (0, j)), pl.BlockSpec((1, bn), lambda j: (0, j))],
        out_specs=pl.BlockSpec((rows, bn), lambda j: (0, j)),
        out_shape=jax.ShapeDtypeStruct((rows, n), F32),
        compiler_params=_params(("parallel",)),
    )(a_in, w_ada, b_sh)


def _mod_bwd(a_in, dm, w_ada):
    rows, d = a_in.shape
    n = w_ada.shape[1]
    bn = 512
    nb = n // bn

    def body(a_ref, dm_ref, w_ref, gw_ref, da_ref):
        j = pl.program_id(0)
        a = a_ref[...]
        s = (a / (1.0 + jnp.exp(-a))).astype(BF16)
        dmb = dm_ref[...].astype(BF16)
        gw_ref[...] = _dot_tn(s, dmb)
        part = _dot_nt(dmb, w_ref[...].astype(BF16))

        @pl.when(j == 0)
        def _():
            da_ref[...] = part

        @pl.when(j > 0)
        def _():
            da_ref[...] += part

    return pl.pallas_call(
        body, name="mod_bwd", grid=(nb,),
        in_specs=[_full((rows, d)), pl.BlockSpec((rows, bn), lambda j: (0, j)), pl.BlockSpec((d, bn), lambda j: (0, j))],
        out_specs=[pl.BlockSpec((d, bn), lambda j: (0, j)), _full((rows, d))],
        out_shape=[jax.ShapeDtypeStruct((d, n), F32), jax.ShapeDtypeStruct((rows, d), F32)],
        compiler_params=_params(("arbitrary",)),
    )(a_in, dm, w_ada)


def _pre_fwd(x2, ctx2, modv, g_attn, w_in, g_q, g_kv, w_uq, w_ukv, cos_t, sin_t, *, seq, tm, rider=None):
    t_lat, d = x2.shape
    t_ctx = ctx2.shape[0]
    nl, nc = t_lat // tm, t_ctx // tm
    n_all = t_lat + t_ctx
    tpe = seq // tm
    nex = t_lat // seq

    def body(x_ref, c_ref, mod_ref, g_ref, win_ref, gq_ref, gkv_ref, wuq_ref, wukv_ref, cos_ref, sin_ref,
             h_ref, pg_ref, rq_ref, rk_ref, rv_ref, nq_ref, nkv_ref, q_ref, k_ref, v_ref):
        i = pl.program_id(0)
        xt = jnp.where(i < nl, x_ref[...], c_ref[...])
        sh = mod_ref[0, 0:1, :]
        sc = mod_ref[0, 1:2, :]
        r = lax.rsqrt(jnp.mean(xt * xt, axis=-1, keepdims=True) + EPS)
        hb = ((xt * r) * g_ref[...] * (1.0 + sc) + sh).astype(BF16)
        h_ref[...] = hb
        p = _dot_nt(hb, win_ref[...])
        cos = cos_ref[...]
        sin = sin_ref[...]
        rq_ref[...] = _rope(p[:, 0:256], cos, sin).astype(BF16)
        rk_ref[...] = _rope(p[:, 256:512] * (RET_DK ** -0.5), cos, sin).astype(BF16)
        rv_ref[...] = p[:, 512:1024].astype(BF16)
        pg_ref[...] = p[:, 1024:2176]
        cq = p[:, 1536:1920]
        ckv = p[:, 1920:2176]
        nqb = (cq * lax.rsqrt(jnp.mean(cq * cq, axis=-1, keepdims=True) + EPS) * gq_ref[...]).astype(BF16)
        nkvb = (ckv * lax.rsqrt(jnp.mean(ckv * ckv, axis=-1, keepdims=True) + EPS) * gkv_ref[...]).astype(BF16)
        nq_ref[...] = nqb
        nkv_ref[...] = nkvb
        cos1 = cos[:, 0:LANES]
        sin1 = sin[:, 0:LANES]
        kpe = _rope(p[:, 2176:2304], cos1, sin1).astype(BF16)
        for hd in range(HEADS):
            o = hd * MLA_HEAD
            qh = _dot_nt(nqb, wuq_ref[hd]) * MLA_SCALE
            q_ref[:, o:o + 128] = qh[:, 0:128].astype(BF16)
            q_ref[:, o + 128:o + 256] = _rope(qh[:, 128:256], cos1, sin1).astype(BF16)
            kvh = _dot(nkvb, wukv_ref[hd])
            k_ref[:, o:o + 128] = kvh[:, 0:128].astype(BF16)
            k_ref[:, o + 128:o + 256] = kpe
            v_ref[:, hd * 128:(hd + 1) * 128] = kvh[:, 128:256].astype(BF16)

    def tile(width):
        return pl.BlockSpec((tm, width), lambda i: (i, 0))

    widths = (d, PG_COLS, 256, 256, 512, Q_LORA, KV_LORA, HEADS * MLA_HEAD, HEADS * MLA_HEAD, HEADS * 128)
    dtypes = (BF16, F32, BF16, BF16, BF16, BF16, BF16, BF16, BF16, BF16)
    tab = pl.BlockSpec((tm, 256), lambda i: (jnp.where(i < nl, i % tpe, tpe), 0))
    return _hosted_call(
        body, (x2, ctx2, modv, g_attn, w_in, g_q, g_kv, w_uq, w_ukv, cos_t, sin_t), name="pre_fwd", grid=(nl + nc,),
        in_specs=[
            pl.BlockSpec((tm, d), lambda i: (jnp.minimum(i, nl - 1), 0)),
            pl.BlockSpec((tm, d), lambda i: (jnp.maximum(i - nl, 0), 0)),
            pl.BlockSpec((1, 8, d), lambda i: (jnp.minimum(i // tpe, nex), 0, 0)),
            _full((1, d)), _full(w_in.shape), _full((1, Q_LORA)), _full((1, KV_LORA)),
            _full(w_uq.shape), _full(w_ukv.shape), tab, tab,
        ],
        out_specs=[tile(w) for w in widths],
        out_shape=[jax.ShapeDtypeStruct((n_all, w), dt) for w, dt in zip(widths, dtypes)],
        sem=("parallel",), rider=rider)


def _post(yret, ymla, x2, tgt2, modv, g_ffn, g_fin, w_out, w_ff1, w_ff2a, w_ff2b, *, seq, tm):
    t_lat, d = x2.shape
    nl = t_lat // tm
    tpe = seq // tm
    nex = t_lat // seq
    n_slab = w_ff1.shape[0]
    fs = w_ff1.shape[2]
    fh = w_ff2a.shape[1]

    def body(yr_ref, ym_ref, x_ref, t_ref, mod_ref, gf_ref, gl_ref, wo_ref, w1_ref, w2a_ref, w2b_ref,
             mix_ref, a_ref, du_ref, h2_ref, df_ref, dmo_ref, dmix_ref, dxm_ref, st_ref, ru_ref):
        i = pl.program_id(0)
        gt_a = mod_ref[0, 2:3, :]
        sh_f = mod_ref[0, 3:4, :]
        sc_f = mod_ref[0, 4:5, :]
        gt_f = mod_ref[0, 5:6, :]
        g_ffn_v = gf_ref[...]
        g_fin_v = gl_ref[...]
        yr = yr_ref[...]
        ym = ym_ref[...]
        mix_ref[:, 0:512] = yr
        mix_ref[:, 512:1024] = ym
        op = _dot(yr, wo_ref[0:512, :]) + _dot(ym, wo_ref[512:1024, :])
        x_mid = x_ref[...] + gt_a * op
        r2 = lax.rsqrt(jnp.mean(x_mid * x_mid, axis=-1, keepdims=True) + EPS)
        xh2 = x_mid * r2
        h2b = (xh2 * g_ffn_v * (1.0 + sc_f) + sh_f).astype(BF16)
        h2_ref[...] = h2b
        f = jnp.zeros((tm, d), F32)
        for s in range(n_slab):
            ru = jnp.maximum(_dot(h2b, w1_ref[s]), 0.0)
            ru_ref[:, s * fs:(s + 1) * fs] = ru
            ab = (ru * ru).astype(BF16)
            a_ref[:, s * fs:(s + 1) * fs] = ab
            f = f + _dot(ab[:, 0:fh], w2a_ref[s]) + _dot(ab[:, fh:fs], w2b_ref[s])
        x_out = x_mid + gt_f * f
        r3 = lax.rsqrt(jnp.mean(x_out * x_out, axis=-1, keepdims=True) + EPS)
        xh3 = x_out * r3
        err = xh3 * g_fin_v - t_ref[...]
        dy = err * (1.0 / d)
        dxh3 = dy * g_fin_v
        dx_out = r3 * (dxh3 - xh3 * jnp.mean(dxh3 * xh3, axis=-1, keepdims=True))
        dfb = (dx_out * gt_f).astype(BF16)
        df_ref[...] = dfb
        dh2 = jnp.zeros((tm, d), F32)
        for s in range(n_slab):
            da = jnp.concatenate([_dot_nt(dfb, w2a_ref[s]), _dot_nt(dfb, w2b_ref[s])], axis=1)
            dub = (da * (2.0 * ru_ref[:, s * fs:(s + 1) * fs])).astype(BF16)
            du_ref[:, s * fs:(s + 1) * fs] = dub
            dh2 = dh2 + _dot_nt(dub, w1_ref[s])
        dxh2 = dh2 * (1.0 + sc_f) * g_ffn_v
        dx_mid = dx_out + r2 * (dxh2 - xh2 * jnp.mean(dxh2 * xh2, axis=-1, keepdims=True))
        dxm_ref[...] = dx_mid
        dmob = (dx_mid * gt_a).astype(BF16)
        dmo_ref[...] = dmob
        dmix_ref[...] = _dot_nt(dmob, wo_ref[...]).astype(BF16)

        def rsum(v):
            return jnp.sum(v, axis=0, keepdims=True)

        stats = jnp.concatenate([
            rsum(dh2), rsum(dh2 * xh2 * g_ffn_v), rsum(dx_out * f), rsum(dx_mid * op),
            rsum(dh2 * (1.0 + sc_f) * xh2), rsum(dy * xh3), rsum(err * err), jnp.zeros((1, d), F32)], axis=0)

        @pl.when(i % tpe == 0)
        def _():
            st_ref[0] = stats

        @pl.when(i % tpe != 0)
        def _():
            st_ref[0] += stats

    def tile(width):
        return pl.BlockSpec((tm, width), lambda i: (i, 0))

    widths = (d, D_FF, D_FF, d, d, d, d, d)
    dtypes = (BF16, BF16, BF16, BF16, BF16, BF16, BF16, F32)
    const = pl.Buffered(1)
    return pl.pallas_call(
        body, name="post", grid=(nl,),
        in_specs=[
            tile(512), tile(512), tile(d), tile(d),
            pl.BlockSpec((1, 8, d), lambda i: (i // tpe, 0, 0)),
            _full((1, d)), _full((1, d)),
            pl.BlockSpec(w_out.shape, lambda i: (0, 0), pipeline_mode=const),
            pl.BlockSpec(w_ff1.shape, lambda i: (0, 0, 0), pipeline_mode=const),
            pl.BlockSpec(w_ff2a.shape, lambda i: (0, 0, 0), pipeline_mode=const),
            pl.BlockSpec(w_ff2b.shape, lambda i: (0, 0, 0), pipeline_mode=const),
        ],
        out_specs=[tile(w) for w in widths] + [pl.BlockSpec((1, 8, d), lambda i: (i // tpe, 0, 0))],
        out_shape=[jax.ShapeDtypeStruct((t_lat, w), dt) for w, dt in zip(widths, dtypes)]
        + [jax.ShapeDtypeStruct((nex, 8, d), F32)],
        scratch_shapes=[pltpu.VMEM((tm, D_FF), F32)],
        compiler_params=_params(("arbitrary",), VMEM_LIMIT),
    )(yret, ymla, x2, tgt2, modv, g_ffn, g_fin, w_out, w_ff1, w_ff2a, w_ff2b)


def _pre_bwd(x2, ctx2, modv, g_attn, pg, drq, drk, dkc_r, drv, dvc_r, drg, dq_m, dkl, dkc, dvl, dvc, dxm,
             w_in, g_q, g_kv, w_uq, w_ukv, cos_t, sin_t, *, seq, tm, rider=None):
    t_lat, d = x2.shape
    t_ctx = ctx2.shape[0]
    nl, nc = t_lat // tm, t_ctx // tm
    n_all = t_lat + t_ctx
    tpe = seq // tm
    nex = t_lat // seq

    def body(x_ref, c_ref, mod_ref, g_ref, pg_ref, drq_ref, drk_ref, dkcr_ref, drv_ref, dvcr_ref, drg_ref,
             dq_ref, dkl_ref, dkc_ref, dvl_ref, dvc_ref, dxm_ref, win_ref, gq_ref, gkv_ref, wuq_ref, wukv_ref,
             cos_ref, sin_ref, dpb_ref, dqf_ref, dkvf_ref, gx_ref, st_ref):
        i = pl.program_id(0)
        lat = i < nl
        latf = lat.astype(F32)
        cos = cos_ref[...]
        sin = sin_ref[...]
        cos1 = cos[:, 0:LANES]
        sin1 = sin[:, 0:LANES]
        d_rq = _rope_t(drq_ref[...] * latf, cos, sin)
        d_rk = _rope_t(jnp.where(lat, drk_ref[...], dkcr_ref[...]), cos, sin) * (RET_DK ** -0.5)
        d_rv = jnp.where(lat, drv_ref[...], dvcr_ref[...])
        d_rg = drg_ref[...] * latf
        dq_all = dq_ref[...] * (latf * MLA_SCALE)
        dk_all = jnp.where(lat, dkl_ref[...], dkc_ref[...])
        dv_all = jnp.where(lat, dvl_ref[...], dvc_ref[...])
        dnq = jnp.zeros((tm, Q_LORA), F32)
        dnkv = jnp.zeros((tm, KV_LORA), F32)
        dkpe = jnp.zeros((tm, LANES), F32)
        for hd in range(HEADS):
            o = hd * MLA_HEAD
            dqh = jnp.concatenate([dq_all[:, o:o + 128], _rope_t(dq_all[:, o + 128:o + 256], cos1, sin1)],
                                  axis=1).astype(BF16)
            dqf_ref[:, o:o + 256] = dqh
            dnq = dnq + _dot(dqh, wuq_ref[hd])
            dkpe = dkpe + dk_all[:, o + 128:o + 256]
            dkvh = jnp.concatenate([dk_all[:, o:o + 128], dv_all[:, hd * 128:(hd + 1) * 128]], axis=1).astype(BF16)
            dkvf_ref[:, o:o + 256] = dkvh
            dnkv = dnkv + _dot_nt(dkvh, wukv_ref[hd])
        d_kpe = _rope_t(dkpe, cos1, sin1)
        pgv = pg_ref[...]
        cq = pgv[:, 512:896]
        ckv = pgv[:, 896:1152]
        rq_ = lax.rsqrt(jnp.mean(cq * cq, axis=-1, keepdims=True) + EPS)
        cqh = cq * rq_
        dcqh = dnq * gq_ref[...]
        d_cq = rq_ * (dcqh - cqh * jnp.mean(dcqh * cqh, axis=-1, keepdims=True))
        rkv_ = lax.rsqrt(jnp.mean(ckv * ckv, axis=-1, keepdims=True) + EPS)
        ckvh = ckv * rkv_
        dckvh = dnkv * gkv_ref[...]
        d_ckv = rkv_ * (dckvh - ckvh * jnp.mean(dckvh * ckvh, axis=-1, keepdims=True))
        dpb = jnp.concatenate([d_rq, d_rk, d_rv, d_rg, d_cq, d_ckv, d_kpe], axis=1).astype(BF16)
        dpb_ref[...] = dpb
        dh = _dot(dpb, win_ref[...])
        xt = jnp.where(lat, x_ref[...], c_ref[...])
        sc = mod_ref[0, 1:2, :]
        g = g_ref[...]
        r = lax.rsqrt(jnp.mean(xt * xt, axis=-1, keepdims=True) + EPS)
        xh = xt * r
        dxh = dh * (1.0 + sc) * g
        dx = r * (dxh - xh * jnp.mean(dxh * xh, axis=-1, keepdims=True))

        @pl.when(lat)
        def _():
            gx_ref[...] = dxm_ref[...] + dx

        def rsum(v):
            return jnp.sum(v, axis=0, keepdims=True)

        def widen(v):
            return jnp.concatenate([v, jnp.zeros((1, d - v.shape[1]), F32)], axis=1)

        stats = jnp.concatenate([
            rsum(dh), rsum(dh * xh * g), rsum(dh * (1.0 + sc) * xh), widen(rsum(dnq * cqh)), widen(rsum(dnkv * ckvh)),
            jnp.zeros((3, d), F32)], axis=0)
        first = jnp.logical_or(jnp.logical_and(lat, i % tpe == 0), i == nl)

        @pl.when(first)
        def _():
            st_ref[0] = stats

        @pl.when(jnp.logical_not(first))
        def _():
            st_ref[0] += stats

    def lat_tile(width):
        return pl.BlockSpec((tm, width), lambda i: (jnp.minimum(i, nl - 1), 0))

    def ctx_tile(width):
        return pl.BlockSpec((tm, width), lambda i: (jnp.maximum(i - nl, 0), 0))

    def tile(width):
        return pl.BlockSpec((tm, width), lambda i: (i, 0))

    tab = pl.BlockSpec((tm, 256), lambda i: (jnp.where(i < nl, i % tpe, tpe), 0))
    ex = pl.BlockSpec((1, 8, d), lambda i: (jnp.minimum(i // tpe, nex), 0, 0))
    return _hosted_call(
        body, (x2, ctx2, modv, g_attn, pg, drq, drk, dkc_r, drv, dvc_r, drg, dq_m, dkl, dkc, dvl, dvc, dxm,
               w_in, g_q, g_kv, w_uq, w_ukv, cos_t, sin_t), name="pre_bwd", grid=(nl + nc,),
        in_specs=[
            lat_tile(d), ctx_tile(d), ex, _full((1, d)), tile(PG_COLS),
            lat_tile(256), lat_tile(256), ctx_tile(256), lat_tile(512), ctx_tile(512), lat_tile(512),
            lat_tile(1024), lat_tile(1024), ctx_tile(1024), lat_tile(512), ctx_tile(512), lat_tile(d),
            _once(w_in.shape), _full((1, Q_LORA)), _full((1, KV_LORA)), _once(w_uq.shape), _once(w_ukv.shape),
            tab, tab,
        ],
        out_specs=[tile(IN_PAD), tile(1024), tile(1024), lat_tile(d), ex],
        out_shape=[
            jax.ShapeDtypeStruct((n_all, IN_PAD), BF16), jax.ShapeDtypeStruct((n_all, 1024), BF16),
            jax.ShapeDtypeStruct((n_all, 1024), BF16), jax.ShapeDtypeStruct((t_lat, d), F32),
            jax.ShapeDtypeStruct((nex + 1, 8, d), F32),
        ],
        sem=("arbitrary",), rider=rider)


MLA_SCALE = 1.0 / math.sqrt(MLA_NOPE + MLA_ROPE)
KEY_BLOCK = 1024


def _mla_specs(t_lat, seq, ctx_len, tq, heads=1):
    nqt = seq // tq
    cb = t_lat // ctx_len
    q = pl.BlockSpec((tq, heads * MLA_HEAD), lambda b, h, j: (b * nqt + j, h))
    kl = pl.BlockSpec((seq, heads * MLA_HEAD), lambda b, h, j: (b, h))
    kc = pl.BlockSpec((ctx_len, heads * MLA_HEAD), lambda b, h, j: (cb + b, h))
    vl = pl.BlockSpec((seq, heads * 128), lambda b, h, j: (b, h))
    vc = pl.BlockSpec((ctx_len, heads * 128), lambda b, h, j: (cb + b, h))
    o = pl.BlockSpec((tq, heads * 128), lambda b, h, j: (b * nqt + j, h))
    return q, kl, kc, vl, vc, o


FWD_HEADS = 2
BWD_HEADS = 1


def _mla_fwd(q, k, v, *, t_lat, seq, ctx_len, tq, rider=None):
    nex = t_lat // seq

    def body(q_ref, kl_ref, kc_ref, vl_ref, vc_ref, o_ref, lse_ref):
        for hh in range(FWD_HEADS):
            wide = slice(hh * MLA_HEAD, (hh + 1) * MLA_HEAD)
            cols = slice(hh * 128, (hh + 1) * 128)
            qb = q_ref[:, wide]
            s = _dot_nt(qb, kl_ref[:, wide])
            sc = _dot_nt(qb, kc_ref[:, wide])
            m = jnp.maximum(jnp.max(s, axis=-1, keepdims=True), jnp.max(sc, axis=-1, keepdims=True))
            p = jnp.exp(s - m)
            pc = jnp.exp(sc - m)
            total = jnp.sum(p, axis=-1, keepdims=True) + jnp.sum(pc, axis=-1, keepdims=True)
            o = _dot(p.astype(BF16), vl_ref[:, cols]) + _dot(pc.astype(BF16), vc_ref[:, cols])
            o_ref[:, cols] = (o * (1.0 / total)).astype(BF16)
            lse_ref[:, cols] = jnp.broadcast_to(m + jnp.log(total), (tq, 128))

    qs, kl, kc, vl, vc, os_ = _mla_specs(t_lat, seq, ctx_len, tq, FWD_HEADS)
    return _hosted_call(
        body, (q, k, k, v, v), name="mla_fwd", grid=(nex, HEADS // FWD_HEADS, seq // tq),
        in_specs=[qs, kl, kc, vl, vc], out_specs=[os_, os_],
        out_shape=[jax.ShapeDtypeStruct((t_lat, HEADS * 128), BF16), jax.ShapeDtypeStruct((t_lat, HEADS * 128), F32)],
        sem=("parallel", "parallel", "arbitrary"), rider=rider)


def _mla_bwd(q, k, v, ymla, lse, dmix, *, t_lat, seq, ctx_len, tq, rider=None):
    nex = t_lat // seq
    nqt = seq // tq
    t_ctx = nex * ctx_len
    kb = min(KEY_BLOCK, seq)

    def body(q_ref, kl_ref, kc_ref, vl_ref, vc_ref, o_ref, lse_ref, do_ref, dq_ref, dkl_ref, dkc_ref, dvl_ref, dvc_ref):
        j = pl.program_id(2)

        @pl.when(j == 0)
        def _():
            dkl_ref[...] = jnp.zeros(dkl_ref.shape, F32)
            dkc_ref[...] = jnp.zeros(dkc_ref.shape, F32)
            dvl_ref[...] = jnp.zeros(dvl_ref.shape, F32)
            dvc_ref[...] = jnp.zeros(dvc_ref.shape, F32)

        for hh in range(BWD_HEADS):
            wide = slice(hh * MLA_HEAD, (hh + 1) * MLA_HEAD)
            cols = slice(hh * 128, (hh + 1) * 128)
            qb = q_ref[:, wide]
            dob = do_ref[:, cols]
            delta = jnp.sum(dob.astype(F32) * o_ref[:, cols].astype(F32), axis=-1, keepdims=True)
            lse_row = lse_ref[:, hh * 128:hh * 128 + 1]

            def block(k_ref, v_ref, dk_ref, dv_ref, rows):
                kbl = k_ref[rows, wide]
                vbl = v_ref[rows, cols]
                p = jnp.exp(_dot_nt(qb, kbl) - lse_row)
                ds = (p * (_dot_nt(dob, vbl) - delta)).astype(BF16)
                dk_ref[rows, wide] += _dot_tn(ds, qb)
                dv_ref[rows, cols] += _dot_tn(p.astype(BF16), dob)
                return _dot(ds, kbl)

            dq = block(kc_ref, vc_ref, dkc_ref, dvc_ref, pl.ds(0, ctx_len))
            for i in range(seq // kb):
                dq = dq + block(kl_ref, vl_ref, dkl_ref, dvl_ref, pl.ds(i * kb, kb))
            dq_ref[:, wide] = dq

    g = BWD_HEADS
    qs, kl, kc, vl, vc, os_ = _mla_specs(t_lat, seq, ctx_len, tq, g)
    do_spec = pl.BlockSpec((tq, g * 128), lambda b, h, j: (b * nqt + j, HEADS // g + h))
    return _hosted_call(
        body, (q, k, k, v, v, ymla, lse, dmix), name="mla_bwd", grid=(nex, HEADS // g, nqt),
        in_specs=[qs, kl, kc, vl, vc, os_, os_, do_spec],
        out_specs=[
            qs,
            pl.BlockSpec((seq, g * MLA_HEAD), lambda b, h, j: (b, h)),
            pl.BlockSpec((ctx_len, g * MLA_HEAD), lambda b, h, j: (b, h)),
            pl.BlockSpec((seq, g * 128), lambda b, h, j: (b, h)),
            pl.BlockSpec((ctx_len, g * 128), lambda b, h, j: (b, h)),
        ],
        out_shape=[
            jax.ShapeDtypeStruct((t_lat, HEADS * MLA_HEAD), F32),
            jax.ShapeDtypeStruct((t_lat, HEADS * MLA_HEAD), F32),
            jax.ShapeDtypeStruct((t_ctx, HEADS * MLA_HEAD), F32),
            jax.ShapeDtypeStruct((t_lat, HEADS * 128), F32),
            jax.ShapeDtypeStruct((t_ctx, HEADS * 128), F32),
        ],
        sem=("parallel", "parallel", "arbitrary"), rider=rider)


def _decay_terms(lg, chunk, forward):
    ii = lax.broadcasted_iota(jnp.int32, (chunk, chunk), 0)
    jj = lax.broadcasted_iota(jnp.int32, (chunk, chunk), 1)
    diff = (ii - jj) if forward else (jj - ii)
    dist = jnp.maximum(diff, 0).astype(F32)
    dmat = jnp.where(diff >= 0, jnp.exp(lg * dist), 0.0)
    pos = lax.broadcasted_iota(jnp.int32, (chunk, 1), 0).astype(F32)
    if forward:
        e_q = pos + 1.0
        e_k = (chunk - 1.0) - pos
    else:
        e_q = chunk - pos
        e_k = pos
    wq = jnp.exp(lg * e_q)
    wk = jnp.exp(lg * e_k)
    cd = jnp.exp(jnp.full((1, 1), lg * chunk, F32))
    return dmat, dist, wq, wk, e_q, e_k, cd


def _ctx_weights(lg, ctx_len, forward):
    pos = lax.broadcasted_iota(jnp.int32, (ctx_len, 1), 0).astype(F32)
    e = ((ctx_len - 1.0) - pos) if forward else pos
    return jnp.exp(lg * e), e


def _pair_specs(t_lat, seq, ctx_len):
    cb = t_lat // ctx_len
    qk = pl.BlockSpec((seq, 128), lambda b, p: (b, p))
    v = pl.BlockSpec((seq, 256), lambda b, p: (b, p))
    kc = pl.BlockSpec((ctx_len, 128), lambda b, p: (cb + b, p))
    vc = pl.BlockSpec((ctx_len, 256), lambda b, p: (cb + b, p))
    return qk, v, kc, vc


def _lane_masks():
    lane = lax.broadcasted_iota(jnp.int32, (1, 128), 1)
    return [(lane // RET_DK) == hh for hh in (0, 1)]


def _ret_fwd_pair(rq, rk, rv, pg, lg, g_ret, *, t_lat, seq, ctx_len, chunk, rider=None):
    nex = t_lat // seq
    n_chunk = seq // chunk

    def body(q_ref, k_ref, v_ref, kc_ref, vc_ref, rg_ref, lg_ref, g_ref, y_ref, o_ref):
        pair = pl.program_id(1)
        masks = _lane_masks()
        kcf = kc_ref[...].astype(F32)
        chains = [(forward, hh) for forward in (True, False) for hh in (0, 1)]
        terms, s0 = [], []
        for forward, hh in chains:
            lgd = lg_ref[0 if forward else 1, 2 * pair + hh]
            terms.append(_decay_terms(lgd, chunk, forward))
            wc, _ = _ctx_weights(lgd, ctx_len, forward)
            s0.append(_dot_tn((jnp.where(masks[hh], kcf, 0.0) * wc).astype(BF16), vc_ref[:, hh * 128:(hh + 1) * 128]))
        both = [terms[hh][0] + terms[2 + hh][0] for hh in (0, 1)]
        o_ref[...] = jnp.zeros(o_ref.shape, F32)

        def step(t, states):
            new = [None] * 4
            for forward in (True, False):
                n = t if forward else n_chunk - 1 - t
                sl = pl.ds(pl.multiple_of(n * chunk, chunk), chunk)
                qb = q_ref[sl, :]
                kf_all = k_ref[sl, :].astype(F32)
                for hh in (0, 1):
                    c = (0 if forward else 2) + hh
                    _, _, wq, wk, _, _, cd = terms[c]
                    cols = slice(hh * 128, (hh + 1) * 128)
                    qm = jnp.where(masks[hh], qb, jnp.zeros((), BF16))
                    kf = jnp.where(masks[hh], kf_all, 0.0)
                    vb = v_ref[sl, cols]
                    o = wq * _dot(qm, states[c].astype(BF16))
                    if forward:
                        o = o + _dot((_dot_nt(qm, kf.astype(BF16)) * both[hh]).astype(BF16), vb)
                    o_ref[sl, cols] += o
                    new[c] = cd * states[c] + _dot_tn((kf * wk).astype(BF16), vb)
            return tuple(new)

        lax.fori_loop(0, n_chunk, step, tuple(s0))

        def norm_step(n, carry):
            sl = pl.ds(pl.multiple_of(n * chunk, chunk), chunk)
            for hh in (0, 1):
                cols = slice(hh * 128, (hh + 1) * 128)
                o = o_ref[sl, cols]
                mu = jnp.mean(o, axis=-1, keepdims=True)
                oc = o - mu
                var = jnp.mean(oc * oc, axis=-1, keepdims=True)
                rg = rg_ref[sl, cols]
                y_ref[sl, cols] = (oc * lax.rsqrt(var + EPS) * g_ref[:, cols] * (rg / (1.0 + jnp.exp(-rg)))).astype(BF16)
            return carry

        lax.fori_loop(0, n_chunk, norm_step, 0)

    qk, v, kc, vc = _pair_specs(t_lat, seq, ctx_len)
    return _hosted_call(
        body, (rq, rk, rv, rk, rv, pg, lg, g_ret), name="ret_fwd", grid=(nex, HEADS // 2),
        in_specs=[qk, qk, v, kc, vc, v, pl.BlockSpec(memory_space=pltpu.SMEM), pl.BlockSpec((1, 256), lambda b, p: (0, p))],
        out_specs=[v, v],
        out_shape=[jax.ShapeDtypeStruct((t_lat, HEADS * RET_DV), BF16), jax.ShapeDtypeStruct((t_lat, HEADS * RET_DV), F32)],
        sem=("parallel", "arbitrary"), rider=rider)


def _ret_bwd_pair(rq, rk, rv, pg, osum, dmix, lg, g_ret, *, t_lat, seq, ctx_len, chunk, rider=None):
    nex = t_lat // seq
    n_chunk = seq // chunk
    t_ctx = nex * ctx_len

    def body(q_ref, k_ref, v_ref, kc_ref, vc_ref, rg_ref, o_ref, dy_ref, lg_ref, g_ref,
             dq_ref, dk_ref, dv_ref, dkc_ref, dvc_ref, drg_ref, st_ref, do_s, s_st):
        pair = pl.program_id(1)
        masks = _lane_masks()
        kcf = kc_ref[...].astype(F32)

        def norm_step(n, dgains):
            sl = pl.ds(pl.multiple_of(n * chunk, chunk), chunk)
            out = []
            for hh in (0, 1):
                cols = slice(hh * 128, (hh + 1) * 128)
                gain = g_ref[:, cols]
                o = o_ref[sl, cols]
                mu = jnp.mean(o, axis=-1, keepdims=True)
                oc = o - mu
                rstd = lax.rsqrt(jnp.mean(oc * oc, axis=-1, keepdims=True) + EPS)
                ohat = oc * rstd
                rg = rg_ref[sl, cols]
                sg = 1.0 / (1.0 + jnp.exp(-rg))
                dy = dy_ref[sl, cols].astype(F32)
                don = dy * (rg * sg)
                drg_ref[sl, cols] = dy * (ohat * gain) * (sg * (1.0 + rg * (1.0 - sg)))
                dohat = don * gain
                do_s[sl, cols] = rstd * (dohat - jnp.mean(dohat, axis=-1, keepdims=True)
                                         - ohat * jnp.mean(dohat * ohat, axis=-1, keepdims=True))
                out.append(dgains[hh] + jnp.sum(don * ohat, axis=0, keepdims=True))
            return tuple(out)

        zero_row = jnp.zeros((1, 128), F32)
        dgains = lax.fori_loop(0, n_chunk, norm_step, (zero_row, zero_row))
        dq_ref[...] = jnp.zeros(dq_ref.shape, F32)
        dk_ref[...] = jnp.zeros(dk_ref.shape, F32)
        dv_ref[...] = jnp.zeros(dv_ref.shape, F32)

        chains = [(forward, hh) for forward in (True, False) for hh in (0, 1)]
        terms, ctxw, s0 = [], [], []
        for forward, hh in chains:
            lgd = lg_ref[0 if forward else 1, 2 * pair + hh]
            terms.append(_decay_terms(lgd, chunk, forward))
            ctxw.append(_ctx_weights(lgd, ctx_len, forward))
            s0.append(_dot_tn((jnp.where(masks[hh], kcf, 0.0) * ctxw[-1][0]).astype(BF16), vc_ref[:, hh * 128:(hh + 1) * 128]))

        def chunk_at(t, ascending):
            n = t if ascending else n_chunk - 1 - t
            return n, pl.ds(pl.multiple_of(n * chunk, chunk), chunk)

        def state_step(t, states):
            new = []
            for c, (forward, hh) in enumerate(chains):
                n, sl = chunk_at(t, forward)
                wk, cd = terms[c][3], terms[c][6]
                s_st[c, n] = states[c]
                kf = jnp.where(masks[hh], k_ref[sl, :].astype(F32), 0.0)
                new.append(cd * states[c] + _dot_tn((kf * wk).astype(BF16), v_ref[sl, hh * 128:(hh + 1) * 128]))
            return tuple(new)

        lax.fori_loop(0, n_chunk, state_step, tuple(s0))

        both = [terms[hh][0] + terms[2 + hh][0] for hh in (0, 1)]

        def grad_step(t, carry):
            out = [None] * len(chains)
            in_chunk_b = [None, None]
            for forward in (True, False):
                n, sl = chunk_at(t, not forward)
                qb = q_ref[sl, :]
                kf_all = k_ref[sl, :].astype(F32)
                dq_sum = jnp.zeros((chunk, 128), F32)
                dk_sum = jnp.zeros((chunk, 128), F32)
                for hh in (0, 1):
                    c = (0 if forward else 2) + hh
                    g_next, dlg = carry[c]
                    dmat, dist, wq, wk, e_q, e_k, cd = terms[c]
                    cols = slice(hh * 128, (hh + 1) * 128)
                    qm = jnp.where(masks[hh], qb, jnp.zeros((), BF16))
                    kf = jnp.where(masks[hh], kf_all, 0.0)
                    kb = kf.astype(BF16)
                    vb = v_ref[sl, cols]
                    do = do_s[sl, cols]
                    dob = do.astype(BF16)
                    s_n = s_st[c, n]
                    s_nb = s_n.astype(BF16)
                    gb = g_next.astype(BF16)
                    dk_cross = wk * _dot_nt(vb, gb)
                    dv = _dot((kf * wk).astype(BF16), gb)
                    o_cross = wq * _dot(qm, s_nb)
                    dq_sum = dq_sum + wq * _dot_nt(dob, s_nb)
                    dk_sum = dk_sum + dk_cross
                    dlg = (dlg + chunk * cd * jnp.sum(g_next * s_n, keepdims=True)
                           + jnp.sum(e_k * jnp.sum(kf * dk_cross, axis=-1, keepdims=True), keepdims=True)
                           + jnp.sum(e_q * jnp.sum(o_cross * do, axis=-1, keepdims=True), keepdims=True))
                    if forward:
                        a_raw = _dot_nt(qm, kb)
                        da_raw = _dot_nt(dob, vb)
                        prod = a_raw * da_raw
                        dlg = dlg + jnp.sum(dist * dmat * prod, keepdims=True)
                        in_chunk_b[hh] = jnp.sum(terms[2 + hh][1] * terms[2 + hh][0] * prod, keepdims=True)
                        dab = (da_raw * both[hh]).astype(BF16)
                        dq_sum = dq_sum + _dot(dab, kb)
                        dk_sum = dk_sum + _dot_tn(dab, qm)
                        dv = dv + _dot_tn((a_raw * both[hh]).astype(BF16), dob)
                    else:
                        dlg = dlg + in_chunk_b[hh]
                    dv_ref[sl, cols] += dv
                    out[c] = (cd * g_next + _dot_tn((qm.astype(F32) * wq).astype(BF16), dob), dlg)
                dq_ref[sl, :] += dq_sum
                dk_ref[sl, :] += dk_sum
            return tuple(out)

        zero = (jnp.zeros((128, 128), F32), jnp.zeros((1, 1), F32))
        res = lax.fori_loop(0, n_chunk, grad_step, (zero,) * len(chains))
        dkc_sum = jnp.zeros((ctx_len, 128), F32)
        dvc = [jnp.zeros((ctx_len, 128), F32)] * 2
        dlgs = []
        for c, (forward, hh) in enumerate(chains):
            ds0, dlg = res[c]
            wc, e_c = ctxw[c]
            kcm = jnp.where(masks[hh], kcf, 0.0)
            ds0b = ds0.astype(BF16)
            dkc_part = wc * _dot_nt(vc_ref[:, hh * 128:(hh + 1) * 128], ds0b)
            dkc_sum = dkc_sum + dkc_part
            dvc[hh] = dvc[hh] + _dot((kcm * wc).astype(BF16), ds0b)
            dlgs.append(dlg + jnp.sum(e_c * jnp.sum(kcm * dkc_part, axis=-1, keepdims=True), keepdims=True))
        dkc_ref[...] = dkc_sum
        for hh in (0, 1):
            cols = slice(hh * 128, (hh + 1) * 128)
            dvc_ref[:, cols] = dvc[hh]
            st_ref[0, :, cols] = jnp.concatenate([
                dgains[hh], jnp.broadcast_to(dlgs[hh], (1, 128)), jnp.broadcast_to(dlgs[2 + hh], (1, 128)),
                jnp.zeros((5, 128), F32)], axis=0)

    qk, v, kc, vc = _pair_specs(t_lat, seq, ctx_len)
    return _hosted_call(
        body, (rq, rk, rv, rk, rv, pg, osum, dmix, lg, g_ret), name="ret_bwd", grid=(nex, HEADS // 2),
        in_specs=[qk, qk, v, kc, vc, v, v, v, pl.BlockSpec(memory_space=pltpu.SMEM),
                  pl.BlockSpec((1, 256), lambda b, p: (0, p))],
        out_specs=[
            qk, qk, v,
            pl.BlockSpec((ctx_len, 128), lambda b, p: (b, p)),
            pl.BlockSpec((ctx_len, 256), lambda b, p: (b, p)),
            v,
            pl.BlockSpec((1, 8, 256), lambda b, p: (b, 0, p)),
        ],
        out_shape=[
            jax.ShapeDtypeStruct((t_lat, 256), F32), jax.ShapeDtypeStruct((t_lat, 256), F32),
            jax.ShapeDtypeStruct((t_lat, 512), F32), jax.ShapeDtypeStruct((t_ctx, 256), F32),
            jax.ShapeDtypeStruct((t_ctx, 512), F32), jax.ShapeDtypeStruct((t_lat, 512), F32),
            jax.ShapeDtypeStruct((nex, 8, 512), F32),
        ],
        scratch_shapes=[pltpu.VMEM((seq, 256), F32), pltpu.VMEM((4, n_chunk, 128, 128), F32)],
        sem=("parallel", "arbitrary"), rider=rider)


def _matmul_tn(a, b, *, bm, bn, bk, chip_major, name, out_dtype=F32, rider=None):
    tk, m = a.shape
    n = b.shape[1]
    slab = n // N_CHIPS
    per_block = bn // slab if chip_major else 1
    bk = max(c for c in range(LANES, min(bk, tk) + 1, LANES) if tk % c == 0)
    nk = tk // bk
    blk = (per_block, bm, slab) if chip_major else (bm, bn)

    def body(a_ref, b_ref, o_ref, acc_ref):
        k = pl.program_id(2)
        if chip_major:
            parts = [_dot_tn(a_ref[...], b_ref[:, s * slab:(s + 1) * slab]) for s in range(per_block)]
        else:
            parts = [_dot_tn(a_ref[...], b_ref[...])]

        @pl.when(k == 0)
        def _():
            for s, part in enumerate(parts):
                if chip_major:
                    acc_ref[s] = part
                else:
                    acc_ref[...] = part

        @pl.when(k > 0)
        def _():
            for s, part in enumerate(parts):
                if chip_major:
                    acc_ref[s] += part
                else:
                    acc_ref[...] += part

        @pl.when(k == nk - 1)
        def _():
            o_ref[...] = acc_ref[...].astype(out_dtype)

    if chip_major:
        out_spec = pl.BlockSpec(blk, lambda i, j, k: (j, i, 0))
        out_shape = jax.ShapeDtypeStruct((N_CHIPS, m, slab), out_dtype)
    else:
        out_spec = pl.BlockSpec(blk, lambda i, j, k: (i, j))
        out_shape = jax.ShapeDtypeStruct((m, n), out_dtype)
    (out,), carried = _hosted_call(
        body, (a, b), name=name, grid=(m // bm, n // bn, nk),
        in_specs=[pl.BlockSpec((bk, bm), lambda i, j, k: (k, i)), pl.BlockSpec((bk, bn), lambda i, j, k: (k, j))],
        out_specs=[out_spec], out_shape=[out_shape], scratch_shapes=[pltpu.VMEM(blk, F32)],
        sem=("parallel", "parallel", "arbitrary"), rider=rider)
    return out if rider is None else (out, carried)


_LATE = ("w_out", "w_ff1", "w_ff2")
_EARLY = ("w_in", "w_uq", "w_ukv")


def _local_step(x, ctx, tgt, modv, lg, g_attn, g_ffn, g_fin, g_ret, g_q, g_kv, w_in, w_uq, w_ukv, late, place=None,
                *, tm=256, tq=256, chunk=256):
    nex, seq, d = x.shape
    ctx_len = ctx.shape[1]
    t_lat = nex * seq
    tm = min(tm, seq)
    x2 = x.reshape(t_lat, d)
    ctx2 = ctx.reshape(nex * ctx_len, d)
    tgt2 = tgt.reshape(t_lat, d)
    tm_fwd = min(2 * tm, seq)
    cos_t, sin_t = _rope_tables(seq, tm)
    dims = dict(t_lat=t_lat, seq=seq, ctx_len=ctx_len)
    alone = place is None

    (hb, pg, rq, rk, rv, nq, nkv, q, k, v), crossed_a = _pre_fwd(
        x2, ctx2, modv, g_attn, w_in, g_q, g_kv, w_uq, w_ukv, *_rope_tables(seq, tm_fwd), seq=seq, tm=tm_fwd,
        rider=None if alone else _gather_ici_rider([late[2]]))
    (yret, osum), got = _ret_fwd_pair(
        rq, rk, rv, pg, lg, g_ret, chunk=min(2 * chunk, seq), **dims,
        rider=None if alone else _merge_riders(_gather_d2d_rider(crossed_a), _gather_ici_rider([late[3]])))
    (ymla, lse), got_rest = _mla_fwd(
        q, k, v, tq=tq, **dims,
        rider=None if alone else _merge_riders(_gather_rider([late[0], late[1]], staged=True), _gather_d2d_rider(got[1:])))
    w_out, w_ff1, w_ff2a, w_ff2b = late if alone else (got_rest[0], got_rest[1], got[0], got_rest[2])
    mix, act, du, h2, df, dmo, dmix, dxm, st_post = _post(yret, ymla, x2, tgt2, modv, g_ffn, g_fin, w_out.reshape(d, d),
                                                         w_ff1, w_ff2a, w_ff2b, seq=seq, tm=min(tm, 256))
    kw = dict(bm=1024, bn=1024, bk=2048, out_dtype=BF16)
    g_ff2 = _matmul_tn(act, df, chip_major=False, name="gw_ff2", **kw).reshape(N_CHIPS, D_FF // N_CHIPS, d)
    if alone:
        g_ff1 = _matmul_tn(h2, du, chip_major=True, name="gw_ff1", **kw)
        g_out = _matmul_tn(mix, dmo, chip_major=False, name="gw_out", **kw).reshape(N_CHIPS, d // N_CHIPS, d)
        (dq_m, dkl, dkc, dvl, dvc), _ = _mla_bwd(q, k, v, ymla, lse, dmix, tq=tq, **dims)
        (drq, drk, drv, dkc_r, dvc_r, drg, st_ret), _ = _ret_bwd_pair(rq, rk, rv, pg, osum, dmix, lg, g_ret, chunk=chunk,
                                                                      **dims)
        late_out = [g_out, g_ff1, g_ff2]
    else:
        core, slot = place
        g_ff1, x_ff2 = _matmul_tn(h2, du, chip_major=True, name="gw_ff1", rider=_exchange_rider([g_ff2]), **kw)
        g_out, x_ff1 = _matmul_tn(mix, dmo, chip_major=False, name="gw_out", rider=_exchange_rider([g_ff1]), **kw)
        g_out = g_out.reshape(N_CHIPS, d // N_CHIPS, d)
        p_ff2 = _add_half(g_ff2, x_ff2[0], core, "add_half_w_ff2")
        p_ff1 = _add_half(g_ff1, x_ff1[0], core, "add_half_w_ff1")
        (dq_m, dkl, dkc, dvl, dvc), (l_ff2, l_ff1, x_out) = _mla_bwd(
            q, k, v, ymla, lse, dmix, tq=min(seq, 512), **dims,
            rider=_merge_riders(_scatter_rider([p_ff2, p_ff1]), _exchange_rider([g_out])))
        p_out = _add_half(g_out, x_out, core, "add_half_w_out")
        m_ff2 = _sum_chips(p_ff2, l_ff2, slot, "sum_chips_w_ff2")
        m_ff1 = _sum_chips(p_ff1, l_ff1, slot, "sum_chips_w_ff1")
        (drq, drk, drv, dkc_r, dvc_r, drg, st_ret), (l_out,) = _ret_bwd_pair(
            rq, rk, rv, pg, osum, dmix, lg, g_ret, chunk=chunk, **dims, rider=_scatter_rider([p_out]))
        late_out = [_sum_chips(p_out, l_out, slot, "sum_chips_w_out"), m_ff1, m_ff2]
    (dpb, dqf, dkvf, gx, st_pre), _ = _pre_bwd(
        x2, ctx2, modv, g_attn, pg, drq, drk, dkc_r, drv, dvc_r, drg, dq_m, dkl, dkc, dvl, dvc, dxm, w_in, g_q, g_kv,
        w_uq, w_ukv, cos_t, sin_t, seq=seq, tm=tm)
    g_early = [
        _matmul_tn(dpb, hb, bm=IN_PAD // 2, bn=d, bk=1536, chip_major=False, name="gw_in"),
        _matmul_tn(dqf, nq, bm=HEADS * MLA_HEAD, bn=Q_LORA, bk=1536, chip_major=False, name="gw_uq"),
        _matmul_tn(nkv, dkvf, bm=KV_LORA, bn=HEADS * 256, bk=1536, chip_major=True, name="gw_ukv"),
    ]
    return gx.reshape(nex, seq, d), g_early, late_out, st_post, st_ret, st_pre


_ANY = pl.BlockSpec(memory_space=pl.ANY)
_VMEM = pl.BlockSpec(memory_space=pltpu.VMEM)
_OFFSETS = tuple((dx, dy, dc) for dx in (0, 1) for dy in (0, 1) for dc in (0, 1))[1:]
_CHIP_OFFSETS = ((1, 0), (0, 1), (1, 1))


def _place():
    return lax.axis_index("x"), lax.axis_index("y"), lax.axis_index("c")


def _flip(v, d):
    return 1 - v if d else v


def _gather8_rider(a, in_vmem=True):
    def copies(a_ref, o_ref, send, recv):
        x, y, z = _place()
        me = 4 * x + 2 * y + z
        out = []
        for k, (dx, dy, dc) in enumerate(_OFFSETS):
            peer = (_flip(x, dx), _flip(y, dy), _flip(z, dc))
            landing = o_ref.at[4 * peer[0] + 2 * peer[1] + peer[2]]
            out.append((
                pltpu.make_async_remote_copy(src_ref=a_ref, dst_ref=o_ref.at[me], send_sem=send.at[k],
                                             recv_sem=recv.at[k], device_id=peer, device_id_type=MESH),
                pltpu.make_async_remote_copy(src_ref=a_ref, dst_ref=landing, send_sem=send.at[k],
                                             recv_sem=recv.at[k], device_id=peer, device_id_type=MESH)))
        return me, out

    def start(ins, outs, sems):
        me, cps = copies(ins[0], outs[0], sems[0], sems[1])
        pltpu.make_async_copy(ins[0], outs[0].at[me], sems[2]).start()
        for out_cp, _ in cps:
            out_cp.start()

    def finish(ins, outs, sems):
        me, cps = copies(ins[0], outs[0], sems[0], sems[1])
        for out_cp, in_cp in cps:
            in_cp.wait_recv()
            out_cp.wait_send()
        pltpu.make_async_copy(ins[0], outs[0].at[me], sems[2]).wait()

    spec = [_VMEM] if in_vmem else [_ANY]
    return _Rider([a], [jax.ShapeDtypeStruct((N_DEV,) + a.shape, a.dtype)],
                  [pltpu.SemaphoreType.DMA((7,)), pltpu.SemaphoreType.DMA((7,)), pltpu.SemaphoreType.DMA],
                  start, finish, in_specs=spec, out_specs=spec)


def _merge_riders(*riders):
    ins, outs, sems, in_specs, out_specs, aliases, cuts = [], [], [], [], [], {}, []
    for r in riders:
        cuts.append((len(ins), len(outs), len(sems)))
        aliases.update({len(ins) + i: len(outs) + j for i, j in r.aliases.items()})
        ins += r.ins
        outs += r.out_shapes
        sems += r.sems
        in_specs += r.in_specs
        out_specs += r.out_specs

    def part(r, cut, r_ins, r_outs, r_sems):
        return (r_ins[cut[0]:cut[0] + len(r.ins)], r_outs[cut[1]:cut[1] + len(r.out_shapes)],
                r_sems[cut[2]:cut[2] + len(r.sems)])

    def start(r_ins, r_outs, r_sems):
        for r, cut in zip(riders, cuts):
            r.start(*part(r, cut, r_ins, r_outs, r_sems))

    def finish(r_ins, r_outs, r_sems):
        for r, cut in zip(riders, cuts):
            r.finish(*part(r, cut, r_ins, r_outs, r_sems))

    def middle(r_ins, r_outs, r_sems):
        for r, cut in zip(riders, cuts):
            if r.middle is not None:
                r.middle(*part(r, cut, r_ins, r_outs, r_sems))

    return _Rider(ins, outs, sems, start, finish, aliases=aliases, in_specs=in_specs, out_specs=out_specs,
                  middle=middle if any(r.middle is not None for r in riders) else None)


def _allgather8(a, name):
    return _run_rider(_gather8_rider(a), name)[0]


BF16_TILE_ROWS = 16


def _half(o, slot, which):
    r2 = o.shape[1] // 2
    if r2 % BF16_TILE_ROWS == 0:
        return o.at[slot, pl.ds(which * r2, r2)]
    c2 = o.shape[2] // 2
    assert c2 % LANES == 0
    return o.at[slot, :, pl.ds(which * c2, c2)]


def _gather_send(o_refs, send, recv):
    x, y, z = _place()
    chip = 2 * x + y
    for a, o in enumerate(o_refs):
        r2 = o.shape[1] // 2
        mine = _half(o, chip, z)
        for k, (dx, dy) in enumerate(_CHIP_OFFSETS):
            pltpu.make_async_remote_copy(
                src_ref=mine, dst_ref=mine, send_sem=send.at[a, k], recv_sem=recv.at[a, k],
                device_id=(_flip(x, dx), _flip(y, dy), z), device_id_type=MESH).start()


def _gather_landed(o_refs, send, recv, then=None):
    x, y, z = _place()
    chip = 2 * x + y
    for a, o in enumerate(o_refs):
        for k, (dx, dy) in enumerate(_CHIP_OFFSETS):
            landed = _half(o, 2 * _flip(x, dx) + _flip(y, dy), z)
            pltpu.make_async_remote_copy(
                src_ref=landed, dst_ref=landed, send_sem=send.at[a, k], recv_sem=recv.at[a, k],
                device_id=(_flip(x, dx), _flip(y, dy), z), device_id_type=MESH).wait_recv()
            if then is not None:
                then(a, k, landed)
    for a, o in enumerate(o_refs):
        mine = _half(o, chip, z)
        for k, (dx, dy) in enumerate(_CHIP_OFFSETS):
            pltpu.make_async_remote_copy(
                src_ref=mine, dst_ref=mine, send_sem=send.at[a, k], recv_sem=recv.at[a, k],
                device_id=(_flip(x, dx), _flip(y, dy), z), device_id_type=MESH).wait_send()


def _pass_on(o_refs, fsend, frecv, a, k, landed):
    x, y, z = _place()
    pltpu.make_async_remote_copy(
        src_ref=landed, dst_ref=landed, send_sem=fsend.at[a, k], recv_sem=frecv.at[a, k],
        device_id=(x, y, 1 - z), device_id_type=MESH).start()


def _passed_on(o_refs, fsend, frecv):
    x, y, z = _place()
    for a, o in enumerate(o_refs):
        for k, (dx, dy) in enumerate(_CHIP_OFFSETS):
            other = 2 * _flip(x, dx) + _flip(y, dy)
            got = _half(o, other, 1 - z)
            gave = _half(o, other, z)
            pltpu.make_async_remote_copy(
                src_ref=got, dst_ref=got, send_sem=fsend.at[a, k], recv_sem=frecv.at[a, k],
                device_id=(x, y, 1 - z), device_id_type=MESH).wait_recv()
            pltpu.make_async_remote_copy(
                src_ref=gave, dst_ref=gave, send_sem=fsend.at[a, k], recv_sem=frecv.at[a, k],
                device_id=(x, y, 1 - z), device_id_type=MESH).wait_send()


def _gather_finish(o_refs, send, recv, fsend, frecv):
    _gather_landed(o_refs, send, recv, functools.partial(_pass_on, o_refs, fsend, frecv))
    _passed_on(o_refs, fsend, frecv)


class _Rider:
    def __init__(self, ins, out_shapes, sems, start, finish, aliases=None, in_specs=None, out_specs=None, middle=None):
        self.ins, self.out_shapes, self.sems = list(ins), list(out_shapes), list(sems)
        self.start, self.finish, self.aliases = start, finish, dict(aliases or {})
        self.middle = middle
        self.in_specs = list(in_specs) if in_specs else [_ANY] * len(self.ins)
        self.out_specs = list(out_specs) if out_specs else [_ANY] * len(self.out_shapes)


def _run_rider(rider, name):
    r_in, r_out = len(rider.ins), len(rider.out_shapes)

    def body(*refs):
        ins, outs, sems = refs[:r_in], refs[r_in:r_in + r_out], refs[r_in + r_out:]
        rider.start(ins, outs, sems)
        if rider.middle is not None:
            rider.middle(ins, outs, sems)
        rider.finish(ins, outs, sems)

    return pl.pallas_call(
        body, name=name, in_specs=rider.in_specs, out_specs=rider.out_specs, out_shape=rider.out_shapes,
        input_output_aliases=rider.aliases, scratch_shapes=rider.sems,
    )(*rider.ins)


def _hosted_call(body, args, *, name, grid, in_specs, out_specs, out_shape, scratch_shapes=(), sem, rider=None,
                 prefetch=()):
    scratch_shapes = list(scratch_shapes)
    n_pf, n_in, n_out, n_sc = len(prefetch), len(in_specs), len(out_specs), len(scratch_shapes)
    r_in, r_out = (len(rider.ins), len(rider.out_shapes)) if rider else (0, 0)
    last = tuple(g - 1 for g in grid)

    def hosted(*refs):
        p = 0
        parts = []
        for cnt in (n_pf, n_in, r_in, n_out, r_out, n_sc):
            parts.append(refs[p:p + cnt])
            p += cnt
        pf, ins, r_ins, outs, r_outs, scratch = parts
        sems = refs[p:]
        ids = [pl.program_id(a) for a in range(len(grid))]
        is_first = functools.reduce(jnp.logical_and, [i == 0 for i in ids])
        is_last = functools.reduce(jnp.logical_and, [i == e for i, e in zip(ids, last)])

        @pl.when(is_first)
        def _():
            rider.start(r_ins, r_outs, sems)

        if rider.middle is not None:
            linear = functools.reduce(lambda acc, ig: acc * ig[1] + ig[0], zip(ids, grid), 0)

            @pl.when(linear == math.prod(grid) * 3 // 4)
            def _():
                rider.middle(r_ins, r_outs, sems)

        body(*pf, *ins, *outs, *scratch)

        @pl.when(is_last)
        def _():
            rider.finish(r_ins, r_outs, sems)

    if rider is None:
        kern, all_in, all_out, shapes, scratch, aliases, extra = body, list(in_specs), list(out_specs), list(out_shape), \
            scratch_shapes, {}, []
    else:
        kern, all_in, all_out = hosted, list(in_specs) + rider.in_specs, list(out_specs) + rider.out_specs
        shapes, scratch, extra = list(out_shape) + rider.out_shapes, scratch_shapes + rider.sems, rider.ins
        aliases = {n_pf + n_in + i: n_out + j for i, j in rider.aliases.items()}
        sem = ("arbitrary",) * len(grid)
    if prefetch:
        spec = dict(grid_spec=pltpu.PrefetchScalarGridSpec(
            num_scalar_prefetch=n_pf, grid=grid, in_specs=all_in, out_specs=all_out, scratch_shapes=scratch))
    else:
        spec = dict(grid=grid, in_specs=all_in, out_specs=all_out, scratch_shapes=scratch)
    res = pl.pallas_call(kern, name=name, out_shape=shapes, input_output_aliases=aliases,
                         compiler_params=_params(sem, VMEM_LIMIT), **spec)(*prefetch, *args, *extra)
    return list(res[:n_out]), list(res[n_out:])


def _gather_rider(ws, staged=False):
    n = len(ws)
    shapes = [jax.ShapeDtypeStruct(w.shape, w.dtype) for w in ws]
    sems = [pltpu.SemaphoreType.DMA((n, 3))] * 4
    aliases = {a: a for a in range(n)}

    def start(ins, outs, s):
        _gather_send(outs, s[0], s[1])

    if not staged:
        return _Rider(ws, shapes, sems, start, lambda ins, outs, s: _gather_finish(outs, *s), aliases=aliases)
    return _Rider(
        ws, shapes, sems, start, lambda ins, outs, s: _passed_on(outs, s[2], s[3]), aliases=aliases,
        middle=lambda ins, outs, s: _gather_landed(outs, s[0], s[1], functools.partial(_pass_on, outs, s[2], s[3])))


def _gather_ici_rider(ws):
    n = len(ws)
    return _Rider(
        ws, [jax.ShapeDtypeStruct(w.shape, w.dtype) for w in ws], [pltpu.SemaphoreType.DMA((n, 3))] * 2,
        lambda ins, outs, sems: _gather_send(outs, sems[0], sems[1]),
        lambda ins, outs, sems: _gather_landed(outs, sems[0], sems[1]),
        aliases={a: a for a in range(n)})


def _gather_d2d_rider(ws):
    n = len(ws)

    def start(ins, outs, sems):
        x, y, z = _place()
        for a, o in enumerate(outs):
            for k, (dx, dy) in enumerate(_CHIP_OFFSETS):
                _pass_on(outs, sems[0], sems[1], a, k, _half(o, 2 * _flip(x, dx) + _flip(y, dy), z))

    return _Rider(
        ws, [jax.ShapeDtypeStruct(w.shape, w.dtype) for w in ws], [pltpu.SemaphoreType.DMA((n, 3))] * 2,
        start, lambda ins, outs, sems: _passed_on(outs, sems[0], sems[1]), aliases={a: a for a in range(n)})


def _copies_rider(ins, out_shapes, sem_shape, make):
    def start(r_ins, r_outs, sems):
        for cp in make(r_ins, r_outs, sems[0], sems[1]):
            cp.start()

    def finish(r_ins, r_outs, sems):
        for cp in make(r_ins, r_outs, sems[0], sems[1]):
            cp.wait()

    return _Rider(ins, out_shapes, [pltpu.SemaphoreType.DMA(sem_shape)] * 2, start, finish)


def _exchange_rider(gs):
    def make(g_refs, r_refs, send, recv):
        x, y, z = _place()
        return [pltpu.make_async_remote_copy(
            src_ref=g.at[:, pl.ds((1 - z) * (g.shape[1] // 2), g.shape[1] // 2)], dst_ref=r, send_sem=send.at[a],
            recv_sem=recv.at[a], device_id=(x, y, 1 - z), device_id_type=MESH)
            for a, (g, r) in enumerate(zip(g_refs, r_refs))]

    shapes = [jax.ShapeDtypeStruct((g.shape[0], g.shape[1] // 2, g.shape[2]), g.dtype) for g in gs]
    return _copies_rider(gs, shapes, (len(gs),), make)


def _add_half(g, recv, core, name):
    s, r, c = g.shape
    r2 = r // 2
    rb = r2
    for cand in (256, 128, 64):
        if r2 % cand == 0:
            rb = cand
            break
    g4 = g.reshape(s, 2, r2, c)

    def body(core_ref, g_ref, r_ref, o_ref):
        o_ref[...] = (g_ref[...].astype(F32) + r_ref[...].astype(F32)).astype(BF16)

    return pl.pallas_call(
        body, name=name,
        grid_spec=pltpu.PrefetchScalarGridSpec(
            num_scalar_prefetch=1, grid=(s, r2 // rb),
            in_specs=[pl.BlockSpec((None, None, rb, c), lambda i, j, cr: (i, cr[0], j, 0)),
                      pl.BlockSpec((None, rb, c), lambda i, j, cr: (i, j, 0))],
            out_specs=pl.BlockSpec((None, rb, c), lambda i, j, cr: (i, j, 0))),
        out_shape=jax.ShapeDtypeStruct((s, r2, c), BF16),
        compiler_params=_params(("parallel", "parallel")),
    )(core, g4, recv)


def _scatter_rider(ps):
    def make(p_refs, o_refs, send, recv):
        x, y, z = _place()
        copies = []
        for a, (p, o) in enumerate(zip(p_refs, o_refs)):
            for k, (dx, dy) in enumerate(_CHIP_OFFSETS):
                other = 2 * _flip(x, dx) + _flip(y, dy)
                copies.append(pltpu.make_async_remote_copy(
                    src_ref=p.at[other], dst_ref=o.at[k], send_sem=send.at[a, k], recv_sem=recv.at[a, k],
                    device_id=(_flip(x, dx), _flip(y, dy), z), device_id_type=MESH))
        return copies

    shapes = [jax.ShapeDtypeStruct((3,) + p.shape[1:], p.dtype) for p in ps]
    return _copies_rider(ps, shapes, (len(ps), 3), make)


def _sum_chips(p, landed, chip, name):
    _, r2, c = p.shape
    rb = r2
    for cand in (256, 128, 64):
        if r2 % cand == 0:
            rb = cand
            break

    def body(s_ref, p_ref, l_ref, o_ref):
        acc = p_ref[...].astype(F32)
        for k in range(3):
            acc = acc + l_ref[k].astype(F32)
        o_ref[...] = acc

    return pl.pallas_call(
        body, name=name,
        grid_spec=pltpu.PrefetchScalarGridSpec(
            num_scalar_prefetch=1, grid=(r2 // rb,),
            in_specs=[pl.BlockSpec((None, rb, c), lambda i, s: (s[0], i, 0)),
                      pl.BlockSpec((3, rb, c), lambda i, s: (0, i, 0))],
            out_specs=pl.BlockSpec((rb, c), lambda i, s: (i, 0))),
        out_shape=jax.ShapeDtypeStruct((r2, c), F32),
        compiler_params=_params(("parallel",)),
    )(chip, p, landed)


def _swap_rider(hs):
    def make(h_refs, o_refs, send, recv):
        x, y, z = _place()
        return [pltpu.make_async_remote_copy(
            src_ref=h, dst_ref=o, send_sem=send.at[a], recv_sem=recv.at[a], device_id=(x, y, 1 - z),
            device_id_type=MESH) for a, (h, o) in enumerate(zip(h_refs, o_refs))]

    return _copies_rider(hs, [jax.ShapeDtypeStruct(h.shape, h.dtype) for h in hs], (len(hs),), make)


def _reduce_scatter_vmem(gs, rows, rider, name):
    n = len(gs)
    r_in, r_out = len(rider.ins), len(rider.out_shapes)
    halves = [(r // 2, g.shape[-1]) for g, (r, _) in zip(gs, rows)]

    def body(*refs):
        p = 0
        parts = []
        for cnt in (n, r_in, n, n, r_out, n, n, n, 6):
            parts.append(refs[p:p + cnt])
            p += cnt
        g_refs, r_ins, mine, theirs, r_outs, recv, part, land, sems = parts
        r_sems = refs[p:]
        xs, xr, ss, sr, ws, wr = sems
        x, y, z = _place()
        chip = 2 * x + y
        sib = (x, y, 1 - z)
        rider.start(r_ins, r_outs, r_sems)

        def half_of(a, s, which):
            r2 = halves[a][0]
            if len(g_refs[a].shape) == 3:
                return g_refs[a].at[s, pl.ds(pl.multiple_of(which * r2, 8), r2)]
            return g_refs[a].at[pl.ds(pl.multiple_of(s * rows[a][1] + which * r2, 8), r2)]

        exchange = [pltpu.make_async_remote_copy(
            src_ref=half_of(a, s, 1 - z), dst_ref=recv[a].at[s], send_sem=xs.at[a, s], recv_sem=xr.at[a, s],
            device_id=sib, device_id_type=MESH) for a in range(n) for s in range(N_CHIPS)]
        for cp in exchange:
            cp.start()
        for cp in exchange:
            cp.wait()
        for a in range(n):
            for s in range(N_CHIPS):
                part[a][s] = (half_of(a, s, z)[...] + recv[a][s]).astype(BF16)
        scatter = []
        for a in range(n):
            for k, (dx, dy) in enumerate(_CHIP_OFFSETS):
                other = 2 * _flip(x, dx) + _flip(y, dy)
                scatter.append(pltpu.make_async_remote_copy(
                    src_ref=part[a].at[other], dst_ref=land[a].at[k], send_sem=ss.at[a, k], recv_sem=sr.at[a, k],
                    device_id=(_flip(x, dx), _flip(y, dy), z), device_id_type=MESH))
        for cp in scatter:
            cp.start()
        for cp in scatter:
            cp.wait()
        for a in range(n):
            acc = part[a][chip].astype(F32)
            for k in range(3):
                acc = acc + land[a][k].astype(F32)
            mine[a][...] = acc
        swap = [pltpu.make_async_remote_copy(
            src_ref=mine[a], dst_ref=theirs[a], send_sem=ws.at[a], recv_sem=wr.at[a], device_id=sib,
            device_id_type=MESH) for a in range(n)]
        for cp in swap:
            cp.start()
        for cp in swap:
            cp.wait()
        rider.finish(r_ins, r_outs, r_sems)

    half_shapes = [jax.ShapeDtypeStruct(h, F32) for h in halves]
    res = pl.pallas_call(
        body, name=name, in_specs=[_VMEM] * n + rider.in_specs, out_specs=[_VMEM] * (2 * n) + rider.out_specs,
        out_shape=half_shapes + half_shapes + rider.out_shapes,
        scratch_shapes=[pltpu.VMEM((N_CHIPS,) + h, F32) for h in halves] + [pltpu.VMEM((N_CHIPS,) + h, BF16) for h in halves]
        + [pltpu.VMEM((3,) + h, BF16) for h in halves]
        + [pltpu.SemaphoreType.DMA((n, N_CHIPS))] * 2 + [pltpu.SemaphoreType.DMA((n, 3))] * 2
        + [pltpu.SemaphoreType.DMA((n,))] * 2 + rider.sems,
        input_output_aliases={n + i: 2 * n + j for i, j in rider.aliases.items()},
        compiler_params=_params(None, VMEM_LIMIT),
    )(*gs, *rider.ins)
    return list(res[:n]), list(res[n:2 * n]), list(res[2 * n:])


SMALL_ROWS = 32
PACK_ROWS = 16


def _pack_small(st_post, st_ret, st_pre):
    d = st_post.shape[2]

    def body(po_ref, re_ref, pr_ref, o_ref):
        o_ref[...] = jnp.zeros(o_ref.shape, F32)
        o_ref[0:1, :] = pr_ref[0, 2:3, :] + pr_ref[1, 2:3, :] + pr_ref[2, 2:3, :]
        o_ref[1:2, :] = po_ref[0, 4:5, :] + po_ref[1, 4:5, :]
        o_ref[2:3, :] = po_ref[0, 5:6, :] + po_ref[1, 5:6, :]
        o_ref[3:4, 0:512] = re_ref[0, 0:1, :] + re_ref[1, 0:1, :]
        o_ref[4:5, :] = pr_ref[0, 3:4, :] + pr_ref[1, 3:4, :] + pr_ref[2, 3:4, :]
        o_ref[5:6, :] = pr_ref[0, 4:5, :] + pr_ref[1, 4:5, :] + pr_ref[2, 4:5, :]
        lane = lax.broadcasted_iota(jnp.int32, (1, LANES), 1)
        for row, src in ((6, 1), (10, 2)):
            acc = jnp.zeros((1, LANES), F32)
            for hd in range(HEADS):
                grp = re_ref[0, src:src + 1, hd * LANES:(hd + 1) * LANES] + re_ref[1, src:src + 1, hd * LANES:(hd + 1) * LANES]
                acc = acc + jnp.where(lane == hd, grp, 0.0)
            o_ref[row:row + 1, 0:LANES] = acc
        o_ref[7:8, :] = po_ref[0, 6:7, :] + po_ref[1, 6:7, :]
        o_ref[8:9, :] = pr_ref[2, 0:1, :]
        o_ref[9:10, :] = pr_ref[2, 1:2, :]
        for e in range(2):
            b = 12 + 6 * e
            o_ref[b:b + 1, :] = pr_ref[e, 0:1, :]
            o_ref[b + 1:b + 2, :] = pr_ref[e, 1:2, :]
            o_ref[b + 2:b + 3, :] = po_ref[e, 3:4, :]
            o_ref[b + 3:b + 4, :] = po_ref[e, 0:1, :]
            o_ref[b + 4:b + 5, :] = po_ref[e, 1:2, :]
            o_ref[b + 5:b + 6, :] = po_ref[e, 2:3, :]

    return pl.pallas_call(body, name="pack_small", out_shape=jax.ShapeDtypeStruct((SMALL_ROWS, d), F32))(st_post, st_ret, st_pre)


def _small_reduce(gathered):
    d = gathered.shape[2]

    def body(g_ref, o_ref):
        tot = g_ref[0, 0:PACK_ROWS, :]
        for dev in range(1, N_DEV):
            tot = tot + g_ref[dev, 0:PACK_ROWS, :]
        o_ref[0:PACK_ROWS, :] = tot
        for j in range(6):
            acc = g_ref[0, 12 + j:13 + j, :] + g_ref[0, 18 + j:19 + j, :]
            for dev in range(1, N_DEV):
                acc = acc + g_ref[dev, 12 + j:13 + j, :] + g_ref[dev, 18 + j:19 + j, :]
            if j < 2:
                acc = acc + o_ref[8 + j:9 + j, :]
            o_ref[PACK_ROWS + j:PACK_ROWS + j + 1, :] = acc
        o_ref[PACK_ROWS + 6:PACK_ROWS + 8, :] = jnp.zeros((2, d), F32)

    return pl.pallas_call(body, name="small_reduce", out_shape=jax.ShapeDtypeStruct((PACK_ROWS + 8, d), F32))(gathered)


_SMALL = (("g_attn", 0, 1024), ("g_ffn", 1, 1024), ("g_final", 2, 1024), ("g_ret", 3, 512), ("g_q_lora", 4, 384),
          ("g_kv_lora", 5, 256), ("ret_decay_fwd", 6, HEADS), ("ret_decay_bwd", 10, HEADS))
_SMALL_NAMES = tuple(s[0] for s in _SMALL) + ("c_ctx", "b_ada")


def _small_final(tot, dcc, sg8, ws, ms, vs):
    d = tot.shape[1]
    n = len(_SMALL_NAMES)

    def body(*refs):
        t_ref, dcc_ref, sg_ref = refs[0:3]
        w_refs, m_refs, v_refs = refs[3:3 + n], refs[3 + n:3 + 2 * n], refs[3 + 2 * n:3 + 3 * n]
        outs = refs[3 + 3 * n:]
        g_refs, d_refs, mo_refs, vo_refs = outs[0:n], outs[n:2 * n], outs[2 * n:3 * n], outs[3 * n:4 * n]
        l_ref = outs[4 * n]

        def update(i, g, sl=None):
            pick = (lambda r: r[...]) if sl is None else (lambda r: r[:, sl])
            dl, mn, vn = _adam_math(pick(w_refs[i]), g, pick(m_refs[i]), pick(v_refs[i]))
            if sl is None:
                g_refs[i][...], d_refs[i][...], mo_refs[i][...], vo_refs[i][...] = g, dl, mn, vn
            else:
                g_refs[i][:, sl], d_refs[i][:, sl], mo_refs[i][:, sl], vo_refs[i][:, sl] = g, dl, mn, vn

        for i, (name, row, width) in enumerate(_SMALL):
            g = t_ref[row:row + 1, 0:width]
            if name == "ret_decay_fwd":
                g = g * sg_ref[0:1, 0:width]
            elif name == "ret_decay_bwd":
                g = g * sg_ref[1:2, 0:width]
            update(i, g)
        i_cc, i_b = n - 2, n - 1
        cc = w_refs[i_cc][...]
        s = 1.0 / (1.0 + jnp.exp(-cc))
        dsilu = dcc_ref[0, 0:1, :] + dcc_ref[2, 0:1, :] + dcc_ref[4, 0:1, :] + dcc_ref[6, 0:1, :]
        update(i_cc, dsilu * (s * (1.0 + cc * (1.0 - s))))
        for j in range(6):
            update(i_b, t_ref[PACK_ROWS + j:PACK_ROWS + j + 1, :], pl.ds(j * d, d))
        l_ref[...] = jnp.broadcast_to((0.5 / d) * jnp.sum(t_ref[7:8, :], keepdims=True), l_ref.shape)

    shapes = [jax.ShapeDtypeStruct(a.shape, F32) for a in ws]
    outs = pl.pallas_call(
        body, name="small_final", out_shape=shapes * 4 + [jax.ShapeDtypeStruct((8, LANES), F32)],
    )(tot, dcc, sg8, *ws, *ms, *vs)
    return outs[0:n], outs[n:2 * n], outs[2 * n:3 * n], outs[3 * n:4 * n], outs[4 * n]


_WEIGHTS = ("c_ctx", "w_ada", "b_ada", "g_attn", "g_ffn", "w_in", "ret_decay_fwd", "ret_decay_bwd", "g_ret", "g_q_lora",
            "w_uq", "g_kv_lora", "w_ukv", "w_out", "w_ff1", "w_ff2", "g_final")
_BIG = ("w_in", "w_uq", "w_ukv", "w_out", "w_ff1", "w_ff2")
_TRANSPOSED = ("w_in", "w_uq")


def kernel(x, c, ctx, c_ctx, w_ada, b_ada, g_attn, g_ffn, w_in, ret_decay_fwd, ret_decay_bwd, g_ret, g_q_lora, w_uq, g_kv_lora, w_ukv, w_out, w_ff1, w_ff2, g_final, loss_target, m_c_ctx, m_w_ada, m_b_ada, m_g_attn, m_g_ffn, m_w_in, m_ret_decay_fwd, m_ret_decay_bwd, m_g_ret, m_g_q_lora, m_w_uq, m_g_kv_lora, m_w_ukv, m_w_out, m_w_ff1, m_w_ff2, m_g_final, v_c_ctx, v_w_ada, v_b_ada, v_g_attn, v_g_ffn, v_w_in, v_ret_decay_fwd, v_ret_decay_bwd, v_g_ret, v_g_q_lora, v_w_uq, v_g_kv_lora, v_w_ukv, v_w_out, v_w_ff1, v_w_ff2, v_g_final):
    w = dict(c_ctx=c_ctx, w_ada=w_ada, b_ada=b_ada, g_attn=g_attn, g_ffn=g_ffn, w_in=w_in, ret_decay_fwd=ret_decay_fwd,
             ret_decay_bwd=ret_decay_bwd, g_ret=g_ret, g_q_lora=g_q_lora, w_uq=w_uq, g_kv_lora=g_kv_lora, w_ukv=w_ukv,
             w_out=w_out, w_ff1=w_ff1, w_ff2=w_ff2, g_final=g_final)
    m = dict(c_ctx=m_c_ctx, w_ada=m_w_ada, b_ada=m_b_ada, g_attn=m_g_attn, g_ffn=m_g_ffn, w_in=m_w_in,
             ret_decay_fwd=m_ret_decay_fwd, ret_decay_bwd=m_ret_decay_bwd, g_ret=m_g_ret, g_q_lora=m_g_q_lora, w_uq=m_w_uq,
             g_kv_lora=m_g_kv_lora, w_ukv=m_w_ukv, w_out=m_w_out, w_ff1=m_w_ff1, w_ff2=m_w_ff2, g_final=m_g_final)
    v = dict(c_ctx=v_c_ctx, w_ada=v_w_ada, b_ada=v_b_ada, g_attn=v_g_attn, g_ffn=v_g_ffn, w_in=v_w_in,
             ret_decay_fwd=v_ret_decay_fwd, ret_decay_bwd=v_ret_decay_bwd, g_ret=v_g_ret, g_q_lora=v_g_q_lora, w_uq=v_w_uq,
             g_kv_lora=v_g_kv_lora, w_ukv=v_w_ukv, w_out=v_w_out, w_ff1=v_w_ff1, w_ff2=v_w_ff2, g_final=v_g_final)
    xi, yi, ci = lax.axis_index("x"), lax.axis_index("y"), lax.axis_index("c")
    chip = 2 * xi + yi
    dev = 2 * chip + ci
    nex, seq, d = x.shape
    n_ada = w_ada.shape[2]

    dec = jnp.zeros((8, LANES), F32).at[0, :HEADS].set(ret_decay_fwd[0]).at[1, :HEADS].set(ret_decay_bwd[0])
    lg8, sg8 = _decay_prep(dec)
    lg = lg8[:2, :HEADS]

    def shard_of(t, k):
        return t[k][0].T if k in _TRANSPOSED else t[k][0]

    shard = {k: shard_of(w, k) for k in _BIG}
    head_rows = MLA_NOPE + MLA_ROPE
    shard["w_uq"] = jnp.pad(shard["w_uq"], ((0, MLA_HEAD - head_rows), (0, 0)))
    slot = chip.reshape(1).astype(jnp.int32)
    core = ci.reshape(1).astype(jnp.int32)
    slots = {k: _cast_into_slot(shard[k], slot, "cast_" + k) for k in _EARLY}
    half_ff = shard["w_ff2"].shape[0] // 2
    late_pieces = [(shard["w_out"], 0, shard["w_out"].shape[0]), (shard["w_ff1"], 0, shard["w_ff1"].shape[0]),
                   (shard["w_ff2"], 0, half_ff), (shard["w_ff2"], half_ff, half_ff)]
    late_slots, (w_in_x, w_uq_x, w_ukv_x, c8) = _cast_into_slots(
        late_pieces, slot, "cast_late",
        rider=_merge_riders(_gather_ici_rider([slots[k] for k in _EARLY]),
                            _gather8_rider(jnp.pad(c, ((0, 8 - nex), (0, 0))), in_vmem=False)))

    a_in = jnp.concatenate([c8[:, :nex].reshape(N_DEV * nex, d), c_ctx.reshape(1, d), jnp.zeros((7, d), F32)], axis=0)
    b_sh = lax.dynamic_slice(b_ada, (0, chip * n_ada), (1, n_ada))
    mod_sh = _mod_fwd(a_in, w_ada[0], b_sh)
    mod8, w_in_f, w_uq_k, w_ukv_k = _run_rider(
        _merge_riders(_gather8_rider(mod_sh), _gather_d2d_rider([w_in_x, w_uq_x, w_ukv_x])), "ag_early")
    w_in_k = jnp.pad(w_in_f.reshape(IN_COLS, d), ((0, IN_PAD - IN_COLS), (0, 0)))
    mod_all = mod8[0::2].transpose(1, 0, 2).reshape(a_in.shape[0], N_CHIPS * n_ada)
    mod_me = lax.dynamic_slice(mod_all, (nex * dev, 0), (nex, N_CHIPS * n_ada)).reshape(nex, 6, d)
    mod_c = mod_all[N_DEV * nex].reshape(1, 6, d)
    modv = jnp.pad(jnp.concatenate([mod_me, mod_c], axis=0), ((0, 0), (0, 2), (0, 0)))

    gx, g_early, late, st_post, st_ret, st_pre = _local_step(
        x, ctx, loss_target, modv, lg, g_attn, g_ffn, g_final.reshape(1, d), g_ret, g_q_lora, g_kv_lora,
        w_in_k, w_uq_k, w_ukv_k, late_slots, (core, slot))

    mine, theirs, (*late_theirs, gathered) = _reduce_scatter_vmem(
        g_early, [(IN_COLS // N_CHIPS, IN_COLS // N_CHIPS), (head_rows, MLA_HEAD), (KV_LORA, KV_LORA)],
        _merge_riders(_swap_rider(late), _gather8_rider(_pack_small(st_post, st_ret, st_pre))), "rs_early")
    tot = _small_reduce(gathered)
    dm = jnp.concatenate([
        gathered[:, 12:24].reshape(N_DEV * nex, 6 * d),
        jnp.concatenate([tot[8:10].reshape(1, 2 * d), jnp.zeros((1, 4 * d), F32)], axis=1),
        jnp.zeros((7, 6 * d), F32)], axis=0)
    dm_sh = lax.dynamic_slice(dm, (0, chip * n_ada), (dm.shape[0], n_ada))
    g_ada, da = _mod_bwd(a_in, dm_sh, w_ada[0])
    dcc = _allgather8(da[N_DEV * nex:], "ag_dcc")
    halves = dict(zip(_EARLY, zip(mine, theirs)))
    halves.update(zip(_LATE, zip(late, late_theirs)))
    grad, delta, new_m, new_v = {}, {}, {}, {}
    for k in _BIG:
        a, b = halves[k]
        res = _adamw_halves(shard_of(w, k), a, b, shard_of(m, k), shard_of(v, k), core, "adamw_" + k)
        grad[k], delta[k], new_m[k], new_v[k] = [(o.T if k in _TRANSPOSED else o).reshape(w[k].shape) for o in res]

    shp = w_ada.shape
    outs, _ = _adamw(w_ada[0], g_ada, m["w_ada"][0], v["w_ada"][0], "adamw_w_ada")
    grad["w_ada"] = g_ada.reshape(shp)
    delta["w_ada"], new_m["w_ada"], new_v["w_ada"] = [o.reshape(shp) for o in outs]
    rows = [{k: t[k].reshape(1, -1) for k in _SMALL_NAMES} for t in (w, m, v)]
    small = _small_final(tot, dcc, sg8, *[[t[k] for k in _SMALL_NAMES] for t in rows])
    for res, outs in zip((grad, delta, new_m, new_v), small[:4]):
        for k, o in zip(_SMALL_NAMES, outs):
            res[k] = o.reshape(w[k].shape)
    return (small[4][0, 0], gx, *[grad[k] for k in _WEIGHTS], *[delta[k] for k in _WEIGHTS],
            *[new_m[k] for k in _WEIGHTS], *[new_v[k] for k in _WEIGHTS])
```

```python
import functools
import math

import jax
import jax.numpy as jnp
from jax import lax
from jax.experimental import pallas as pl
from jax.experimental.pallas import tpu as pltpu

F32 = jnp.float32
BF16 = jnp.bfloat16
MESH = pl.DeviceIdType.MESH

EPS = 1e-6
D_MODEL = 1024
D_FF = 4096
HEADS = 4
RET_DK = 64
RET_DV = 128
MLA_NOPE = 128
MLA_ROPE = 64
MLA_HEAD = 256
Q_LORA = 384
KV_LORA = 256
GRID_W = 64
ROPE_BASE = 10000.0
IN_COLS = 2240
IN_PAD = 2304
PG_COLS = 1152
N_CHIPS = 4
N_DEV = 8
LANES = 128
ADAM_LR = 0.001
ADAM_B1 = 0.9
ADAM_B2 = 0.999
ADAM_EPS = 1e-08
ADAM_WD = 0.01
ADAM_STEP = 10
VMEM_LIMIT = 56 * 1024 * 1024


def _dot(a, b):
    return jnp.dot(a, b, preferred_element_type=F32)


def _dot_nt(a, b):
    return lax.dot_general(a, b, (((1,), (1,)), ((), ())), preferred_element_type=F32)


def _dot_tn(a, b):
    return lax.dot_general(a, b, (((0,), (0,)), ((), ())), preferred_element_type=F32)


def _params(sem=None, vmem=None):
    return pltpu.CompilerParams(dimension_semantics=sem, vmem_limit_bytes=vmem)


def _full(shape):
    n = len(shape)
    return pl.BlockSpec(shape, lambda *_: (0,) * n)


def _once(shape):
    n = len(shape)
    return pl.BlockSpec(shape, lambda *_: (0,) * n, pipeline_mode=pl.Buffered(1))


def _rope(x, cos, sin):
    w = x.shape[-1]
    lo = (lax.broadcasted_iota(jnp.int32, (1, w), 1) % 64) < 32
    swapped = jnp.where(lo, pltpu.roll(x, w - 32, 1), pltpu.roll(x, 32, 1))
    return x * cos + swapped * sin


def _rope_t(g, cos, sin):
    w = g.shape[-1]
    lo = (lax.broadcasted_iota(jnp.int32, (1, w), 1) % 64) < 32
    t = g * sin
    swapped = jnp.where(lo, pltpu.roll(t, w - 32, 1), pltpu.roll(t, 32, 1))
    return g * cos + swapped


def _rope_tables(seq, tm):
    rows = seq // GRID_W
    row = jnp.repeat(jnp.arange(rows, dtype=F32), GRID_W)
    col = jnp.tile(jnp.arange(GRID_W, dtype=F32), rows)
    n_freq = RET_DK // 4
    freq = ROPE_BASE ** (-jnp.arange(n_freq, dtype=F32) / n_freq)
    ang = jnp.concatenate([row[:, None] * freq, col[:, None] * freq], axis=-1)
    cos, sin = jnp.cos(ang), jnp.sin(ang)
    cos_t = jnp.tile(jnp.concatenate([cos, cos], -1), (1, HEADS))
    sin_t = jnp.tile(jnp.concatenate([-sin, sin], -1), (1, HEADS))
    cos_t = jnp.concatenate([cos_t, jnp.ones((tm, 4 * RET_DK), F32)], 0)
    sin_t = jnp.concatenate([sin_t, jnp.zeros((tm, 4 * RET_DK), F32)], 0)
    return cos_t, sin_t


def _adam_math(w, g, m, v):
    mn = ADAM_B1 * m + (1.0 - ADAM_B1) * g
    vn = ADAM_B2 * v + (1.0 - ADAM_B2) * (g * g)
    m_hat = mn / (1.0 - ADAM_B1 ** ADAM_STEP)
    v_hat = vn / (1.0 - ADAM_B2 ** ADAM_STEP)
    return -ADAM_LR * (m_hat / (jnp.sqrt(v_hat) + ADAM_EPS) + ADAM_WD * w), mn, vn


def _cast_into_slots(pieces, slot, name, rider=None):
    c = pieces[0][0].shape[1]
    rb = max(b for b in range(16, 257, 16) if all(cnt % b == 0 and st % b == 0 for _, st, cnt in pieces))
    nbs = [cnt // rb for _, _, cnt in pieces]
    starts = [sum(nbs[:s]) for s in range(len(pieces))]

    def body(s_ref, *refs):
        i = pl.program_id(0)
        for s in range(len(pieces)):
            @pl.when(jnp.logical_and(i >= starts[s], i < starts[s] + nbs[s]))
            def _():
                refs[len(pieces) + s][...] = refs[s][...].astype(BF16)

    in_specs, out_specs = [], []
    for (_, first_row, _), nb, st in zip(pieces, nbs, starts):
        in_specs.append(pl.BlockSpec((rb, c), lambda i, s, nb=nb, st=st, f=first_row // rb: (f + jnp.clip(i - st, 0, nb - 1), 0)))
        out_specs.append(pl.BlockSpec((None, rb, c), lambda i, s, nb=nb, st=st: (s[0], jnp.clip(i - st, 0, nb - 1), 0)))
    return _hosted_call(
        body, [w for w, _, _ in pieces], name=name, grid=(sum(nbs),), prefetch=(slot,), in_specs=in_specs,
        out_specs=out_specs, out_shape=[jax.ShapeDtypeStruct((N_CHIPS, cnt, c), BF16) for _, _, cnt in pieces],
        sem=("arbitrary",), rider=rider)


def _cast_into_slot(w, slot, name):
    return _cast_into_slots([(w, 0, w.shape[0])], slot, name)[0][0]


def _adamw_halves(w, mine, theirs, m, v, core, name):
    r, c = w.shape
    r2 = r // 2
    rb = max(b for b in range(8, r2 + 1, 8) if r2 % b == 0 and b * c * 4 <= (1 << 21))
    nbh = r2 // rb

    def body(z_ref, w_ref, a_ref, b_ref, m_ref, v_ref, g_ref, d_ref, mo_ref, vo_ref):
        here = (pl.program_id(0) // nbh) == z_ref[0]
        gg = jnp.where(here, a_ref[...], b_ref[...])
        g_ref[...] = gg
        d_ref[...], mo_ref[...], vo_ref[...] = _adam_math(w_ref[...], gg, m_ref[...], v_ref[...])

    spec = pl.BlockSpec((rb, c), lambda i, z: (i, 0))
    a_spec = pl.BlockSpec((rb, c), lambda i, z: (jnp.clip(i - z[0] * nbh, 0, nbh - 1), 0))
    b_spec = pl.BlockSpec((rb, c), lambda i, z: (jnp.clip(i - (1 - z[0]) * nbh, 0, nbh - 1), 0))
    shp = jax.ShapeDtypeStruct((r, c), F32)
    return pl.pallas_call(
        body, name=name,
        grid_spec=pltpu.PrefetchScalarGridSpec(
            num_scalar_prefetch=1, grid=(r // rb,), in_specs=[spec, a_spec, b_spec, spec, spec], out_specs=[spec] * 4),
        out_shape=[shp] * 4,
        compiler_params=_params(("parallel",)),
    )(core, w, mine, theirs, m, v)


def _adamw(w, g, m, v, name, rider=None):
    r, c = w.shape
    rb = r
    for cand in (256, 128, 64, 32, 16, 8):
        if r % cand == 0 and cand * c * 4 <= (1 << 20):
            rb = cand
            break
    if r * c * 4 <= (1 << 20):
        rb = r

    def body(w_ref, g_ref, m_ref, v_ref, d_ref, mo_ref, vo_ref):
        d_ref[...], mo_ref[...], vo_ref[...] = _adam_math(w_ref[...], g_ref[...], m_ref[...], v_ref[...])

    spec = pl.BlockSpec((rb, c), lambda i: (i, 0))
    shp = jax.ShapeDtypeStruct((r, c), F32)
    return _hosted_call(
        body, (w, g, m, v), name=name, grid=(r // rb,), in_specs=[spec] * 4, out_specs=[spec] * 3, out_shape=[shp] * 3,
        sem=("parallel",), rider=rider)


def _decay_prep(dec):
    def body(d_ref, lg_ref, sg_ref):
        d = d_ref[...]
        lg_ref[...] = jnp.minimum(d, 0.0) - jnp.log(1.0 + jnp.exp(-jnp.abs(d)))
        sg_ref[...] = 1.0 / (1.0 + jnp.exp(d))

    shp = jax.ShapeDtypeStruct(dec.shape, F32)
    return pl.pallas_call(body, name="decay_prep", out_shape=[shp, shp])(dec)


def _mod_fwd(a_in, w_ada, b_sh):
    rows, d = a_in.shape
    n = w_ada.shape[1]
    bn = 512

    def body(a_ref, w_ref, b_ref, o_ref):
        a = a_ref[...]
        s = (a / (1.0 + jnp.exp(-a))).astype(BF16)
        o_ref[...] = _dot(s, w_ref[...].astype(BF16)) + b_ref[...]

    return pl.pallas_call(
        body, name="mod_fwd", grid=(n // bn,),
        in_specs=[_full((rows, d)), pl.BlockSpec((d, bn), lambda j: (0, j)), pl.BlockSpec((1, bn), lambda j: (0, j))],
        out_specs=pl.BlockSpec((rows, bn), lambda j: (0, j)),
        out_shape=jax.ShapeDtypeStruct((rows, n), F32),
        compiler_params=_params(("parallel",)),
    )(a_in, w_ada, b_sh)


def _mod_bwd(a_in, dm, w_ada):
    rows, d = a_in.shape
    n = w_ada.shape[1]
    bn = 512
    nb = n // bn

    def body(a_ref, dm_ref, w_ref, gw_ref, da_ref):
        j = pl.program_id(0)
        a = a_ref[...]
        s = (a / (1.0 + jnp.exp(-a))).astype(BF16)
        dmb = dm_ref[...].astype(BF16)
        gw_ref[...] = _dot_tn(s, dmb)
        part = _dot_nt(dmb, w_ref[...].astype(BF16))

        @pl.when(j == 0)
        def _():
            da_ref[...] = part

        @pl.when(j > 0)
        def _():
            da_ref[...] += part

    return pl.pallas_call(
        body, name="mod_bwd", grid=(nb,),
        in_specs=[_full((rows, d)), pl.BlockSpec((rows, bn), lambda j: (0, j)), pl.BlockSpec((d, bn), lambda j: (0, j))],
        out_specs=[pl.BlockSpec((d, bn), lambda j: (0, j)), _full((rows, d))],
        out_shape=[jax.ShapeDtypeStruct((d, n), F32), jax.ShapeDtypeStruct((rows, d), F32)],
        compiler_params=_params(("arbitrary",)),
    )(a_in, dm, w_ada)


def _pre_fwd(x2, ctx2, modv, g_attn, w_in, g_q, g_kv, w_uq, w_ukv, cos_t, sin_t, *, seq, tm, rider=None):
    t_lat, d = x2.shape
    t_ctx = ctx2.shape[0]
    nl, nc = t_lat // tm, t_ctx // tm
    n_all = t_lat + t_ctx
    tpe = seq // tm
    nex = t_lat // seq

    def body(x_ref, c_ref, mod_ref, g_ref, win_ref, gq_ref, gkv_ref, wuq_ref, wukv_ref, cos_ref, sin_ref,
             h_ref, pg_ref, rq_ref, rk_ref, rv_ref, nq_ref, nkv_ref, q_ref, k_ref, v_ref):
        i = pl.program_id(0)
        xt = jnp.where(i < nl, x_ref[...], c_ref[...])
        sh = mod_ref[0, 0:1, :]
        sc = mod_ref[0, 1:2, :]
        r = lax.rsqrt(jnp.mean(xt * xt, axis=-1, keepdims=True) + EPS)
        hb = ((xt * r) * g_ref[...] * (1.0 + sc) + sh).astype(BF16)
        h_ref[...] = hb
        p = _dot_nt(hb, win_ref[...])
        cos = cos_ref[...]
        sin = sin_ref[...]
        rq_ref[...] = _rope(p[:, 0:256], cos, sin).astype(BF16)
        rk_ref[...] = _rope(p[:, 256:512] * (RET_DK ** -0.5), cos, sin).astype(BF16)
        rv_ref[...] = p[:, 512:1024].astype(BF16)
        pg_ref[...] = p[:, 1024:2176]
        cq = p[:, 1536:1920]
        ckv = p[:, 1920:2176]
        nqb = (cq * lax.rsqrt(jnp.mean(cq * cq, axis=-1, keepdims=True) + EPS) * gq_ref[...]).astype(BF16)
        nkvb = (ckv * lax.rsqrt(jnp.mean(ckv * ckv, axis=-1, keepdims=True) + EPS) * gkv_ref[...]).astype(BF16)
        nq_ref[...] = nqb
        nkv_ref[...] = nkvb
        cos1 = cos[:, 0:LANES]
        sin1 = sin[:, 0:LANES]
        kpe = _rope(p[:, 2176:2304], cos1, sin1).astype(BF16)
        for hd in range(HEADS):
            o = hd * MLA_HEAD
            qh = _dot_nt(nqb, wuq_ref[hd]) * MLA_SCALE
            q_ref[:, o:o + 128] = qh[:, 0:128].astype(BF16)
            q_ref[:, o + 128:o + 256] = _rope(qh[:, 128:256], cos1, sin1).astype(BF16)
            kvh = _dot(nkvb, wukv_ref[hd])
            k_ref[:, o:o + 128] = kvh[:, 0:128].astype(BF16)
            k_ref[:, o + 128:o + 256] = kpe
            v_ref[:, hd * 128:(hd + 1) * 128] = kvh[:, 128:256].astype(BF16)

    def tile(width):
        return pl.BlockSpec((tm, width), lambda i: (i, 0))

    widths = (d, PG_COLS, 256, 256, 512, Q_LORA, KV_LORA, HEADS * MLA_HEAD, HEADS * MLA_HEAD, HEADS * 128)
    dtypes = (BF16, F32, BF16, BF16, BF16, BF16, BF16, BF16, BF16, BF16)
    tab = pl.BlockSpec((tm, 256), lambda i: (jnp.where(i < nl, i % tpe, tpe), 0))
    return _hosted_call(
        body, (x2, ctx2, modv, g_attn, w_in, g_q, g_kv, w_uq, w_ukv, cos_t, sin_t), name="pre_fwd", grid=(nl + nc,),
        in_specs=[
            pl.BlockSpec((tm, d), lambda i: (jnp.minimum(i, nl - 1), 0)),
            pl.BlockSpec((tm, d), lambda i: (jnp.maximum(i - nl, 0), 0)),
            pl.BlockSpec((1, 8, d), lambda i: (jnp.minimum(i // tpe, nex), 0, 0)),
            _full((1, d)), _full(w_in.shape), _full((1, Q_LORA)), _full((1, KV_LORA)),
            _full(w_uq.shape), _full(w_ukv.shape), tab, tab,
        ],
        out_specs=[tile(w) for w in widths],
        out_shape=[jax.ShapeDtypeStruct((n_all, w), dt) for w, dt in zip(widths, dtypes)],
        sem=("parallel",), rider=rider)


def _post(yret, ymla, x2, tgt2, modv, g_ffn, g_fin, w_out, w_ff1, w_ff2a, w_ff2b, *, seq, tm):
    t_lat, d = x2.shape
    nl = t_lat // tm
    tpe = seq // tm
    nex = t_lat // seq
    n_slab = w_ff1.shape[0]
    fs = w_ff1.shape[2]
    fh = w_ff2a.shape[1]

    def body(yr_ref, ym_ref, x_ref, t_ref, mod_ref, gf_ref, gl_ref, wo_ref, w1_ref, w2a_ref, w2b_ref,
             mix_ref, a_ref, du_ref, h2_ref, df_ref, dmo_ref, dmix_ref, dxm_ref, st_ref, ru_ref):
        i = pl.program_id(0)
        gt_a = mod_ref[0, 2:3, :]
        sh_f = mod_ref[0, 3:4, :]
        sc_f = mod_ref[0, 4:5, :]
        gt_f = mod_ref[0, 5:6, :]
        g_ffn_v = gf_ref[...]
        g_fin_v = gl_ref[...]
        yr = yr_ref[...]
        ym = ym_ref[...]
        mix_ref[:, 0:512] = yr
        mix_ref[:, 512:1024] = ym
        op = _dot(yr, wo_ref[0:512, :]) + _dot(ym, wo_ref[512:1024, :])
        x_mid = x_ref[...] + gt_a * op
        r2 = lax.rsqrt(jnp.mean(x_mid * x_mid, axis=-1, keepdims=True) + EPS)
        xh2 = x_mid * r2
        h2b = (xh2 * g_ffn_v * (1.0 + sc_f) + sh_f).astype(BF16)
        h2_ref[...] = h2b
        f = jnp.zeros((tm, d), F32)
        for s in range(n_slab):
            ru = jnp.maximum(_dot(h2b, w1_ref[s]), 0.0)
            ru_ref[:, s * fs:(s + 1) * fs] = ru
            ab = (ru * ru).astype(BF16)
            a_ref[:, s * fs:(s + 1) * fs] = ab
            f = f + _dot(ab[:, 0:fh], w2a_ref[s]) + _dot(ab[:, fh:fs], w2b_ref[s])
        x_out = x_mid + gt_f * f
        r3 = lax.rsqrt(jnp.mean(x_out * x_out, axis=-1, keepdims=True) + EPS)
        xh3 = x_out * r3
        err = xh3 * g_fin_v - t_ref[...]
        dy = err * (1.0 / d)
        dxh3 = dy * g_fin_v
        dx_out = r3 * (dxh3 - xh3 * jnp.mean(dxh3 * xh3, axis=-1, keepdims=True))
        dfb = (dx_out * gt_f).astype(BF16)
        df_ref[...] = dfb
        dh2 = jnp.zeros((tm, d), F32)
        for s in range(n_slab):
            da = jnp.concatenate([_dot_nt(dfb, w2a_ref[s]), _dot_nt(dfb, w2b_ref[s])], axis=1)
            dub = (da * (2.0 * ru_ref[:, s * fs:(s + 1) * fs])).astype(BF16)
            du_ref[:, s * fs:(s + 1) * fs] = dub
            dh2 = dh2 + _dot_nt(dub, w1_ref[s])
        dxh2 = dh2 * (1.0 + sc_f) * g_ffn_v
        dx_mid = dx_out + r2 * (dxh2 - xh2 * jnp.mean(dxh2 * xh2, axis=-1, keepdims=True))
        dxm_ref[...] = dx_mid
        dmob = (dx_mid * gt_a).astype(BF16)
        dmo_ref[...] = dmob
        dmix_ref[...] = _dot_nt(dmob, wo_ref[...]).astype(BF16)

        def rsum(v):
            return jnp.sum(v, axis=0, keepdims=True)

        stats = jnp.concatenate([
            rsum(dh2), rsum(dh2 * xh2 * g_ffn_v), rsum(dx_out * f), rsum(dx_mid * op),
            rsum(dh2 * (1.0 + sc_f) * xh2), rsum(dy * xh3), rsum(err * err), jnp.zeros((1, d), F32)], axis=0)

        @pl.when(i % tpe == 0)
        def _():
            st_ref[0] = stats

        @pl.when(i % tpe != 0)
        def _():
            st_ref[0] += stats

    def tile(width):
        return pl.BlockSpec((tm, width), lambda i: (i, 0))

    widths = (d, D_FF, D_FF, d, d, d, d, d)
    dtypes = (BF16, BF16, BF16, BF16, BF16, BF16, BF16, F32)
    const = pl.Buffered(1)
    return pl.pallas_call(
        body, name="post", grid=(nl,),
        in_specs=[
            tile(512), tile(512), tile(d), tile(d),
            pl.BlockSpec((1, 8, d), lambda i: (i // tpe, 0, 0)),
            _full((1, d)), _full((1, d)),
            pl.BlockSpec(w_out.shape, lambda i: (0, 0), pipeline_mode=const),
            pl.BlockSpec(w_ff1.shape, lambda i: (0, 0, 0), pipeline_mode=const),
            pl.BlockSpec(w_ff2a.shape, lambda i: (0, 0, 0), pipeline_mode=const),
            pl.BlockSpec(w_ff2b.shape, lambda i: (0, 0, 0), pipeline_mode=const),
        ],
        out_specs=[tile(w) for w in widths] + [pl.BlockSpec((1, 8, d), lambda i: (i // tpe, 0, 0))],
        out_shape=[jax.ShapeDtypeStruct((t_lat, w), dt) for w, dt in zip(widths, dtypes)]
        + [jax.ShapeDtypeStruct((nex, 8, d), F32)],
        scratch_shapes=[pltpu.VMEM((tm, D_FF), F32)],
        compiler_params=_params(("arbitrary",), VMEM_LIMIT),
    )(yret, ymla, x2, tgt2, modv, g_ffn, g_fin, w_out, w_ff1, w_ff2a, w_ff2b)


def _pre_bwd(x2, ctx2, modv, g_attn, pg, drq, drk, dkc_r, drv, dvc_r, drg, dq_m, dkl, dkc, dvl, dvc, dxm,
             w_in, g_q, g_kv, w_uq, w_ukv, cos_t, sin_t, *, seq, tm, rider=None):
    t_lat, d = x2.shape
    t_ctx = ctx2.shape[0]
    nl, nc = t_lat // tm, t_ctx // tm
    n_all = t_lat + t_ctx
    tpe = seq // tm
    nex = t_lat // seq

    def body(x_ref, c_ref, mod_ref, g_ref, pg_ref, drq_ref, drk_ref, dkcr_ref, drv_ref, dvcr_ref, drg_ref,
             dq_ref, dkl_ref, dkc_ref, dvl_ref, dvc_ref, dxm_ref, win_ref, gq_ref, gkv_ref, wuq_ref, wukv_ref,
             cos_ref, sin_ref, dpb_ref, dqf_ref, dkvf_ref, gx_ref, st_ref):
        i = pl.program_id(0)
        lat = i < nl
        latf = lat.astype(F32)
        cos = cos_ref[...]
        sin = sin_ref[...]
        cos1 = cos[:, 0:LANES]
        sin1 = sin[:, 0:LANES]
        d_rq = _rope_t(drq_ref[...] * latf, cos, sin)
        d_rk = _rope_t(jnp.where(lat, drk_ref[...], dkcr_ref[...]), cos, sin) * (RET_DK ** -0.5)
        d_rv = jnp.where(lat, drv_ref[...], dvcr_ref[...])
        d_rg = drg_ref[...] * latf
        dq_all = dq_ref[...] * (latf * MLA_SCALE)
        dk_all = jnp.where(lat, dkl_ref[...], dkc_ref[...])
        dv_all = jnp.where(lat, dvl_ref[...], dvc_ref[...])
        dnq = jnp.zeros((tm, Q_LORA), F32)
        dnkv = jnp.zeros((tm, KV_LORA), F32)
        dkpe = jnp.zeros((tm, LANES), F32)
        for hd in range(HEADS):
            o = hd * MLA_HEAD
            dqh = jnp.concatenate([dq_all[:, o:o + 128], _rope_t(dq_all[:, o + 128:o + 256], cos1, sin1)],
                                  axis=1).astype(BF16)
            dqf_ref[:, o:o + 256] = dqh
            dnq = dnq + _dot(dqh, wuq_ref[hd])
            dkpe = dkpe + dk_all[:, o + 128:o + 256]
            dkvh = jnp.concatenate([dk_all[:, o:o + 128], dv_all[:, hd * 128:(hd + 1) * 128]], axis=1).astype(BF16)
            dkvf_ref[:, o:o + 256] = dkvh
            dnkv = dnkv + _dot_nt(dkvh, wukv_ref[hd])
        d_kpe = _rope_t(dkpe, cos1, sin1)
        pgv = pg_ref[...]
        cq = pgv[:, 512:896]
        ckv = pgv[:, 896:1152]
        rq_ = lax.rsqrt(jnp.mean(cq * cq, axis=-1, keepdims=True) + EPS)
        cqh = cq * rq_
        dcqh = dnq * gq_ref[...]
        d_cq = rq_ * (dcqh - cqh * jnp.mean(dcqh * cqh, axis=-1, keepdims=True))
        rkv_ = lax.rsqrt(jnp.mean(ckv * ckv, axis=-1, keepdims=True) + EPS)
        ckvh = ckv * rkv_
        dckvh = dnkv * gkv_ref[...]
        d_ckv = rkv_ * (dckvh - ckvh * jnp.mean(dckvh * ckvh, axis=-1, keepdims=True))
        dpb = jnp.concatenate([d_rq, d_rk, d_rv, d_rg, d_cq, d_ckv, d_kpe], axis=1).astype(BF16)
        dpb_ref[...] = dpb
        dh = _dot(dpb, win_ref[...])
        xt = jnp.where(lat, x_ref[...], c_ref[...])
        sc = mod_ref[0, 1:2, :]
        g = g_ref[...]
        r = lax.rsqrt(jnp.mean(xt * xt, axis=-1, keepdims=True) + EPS)
        xh = xt * r
        dxh = dh * (1.0 + sc) * g
        dx = r * (dxh - xh * jnp.mean(dxh * xh, axis=-1, keepdims=True))

        @pl.when(lat)
        def _():
            gx_ref[...] = dxm_ref[...] + dx

        def rsum(v):
            return jnp.sum(v, axis=0, keepdims=True)

        def widen(v):
            return jnp.concatenate([v, jnp.zeros((1, d - v.shape[1]), F32)], axis=1)

        stats = jnp.concatenate([
            rsum(dh), rsum(dh * xh * g), rsum(dh * (1.0 + sc) * xh), widen(rsum(dnq * cqh)), widen(rsum(dnkv * ckvh)),
            jnp.zeros((3, d), F32)], axis=0)
        first = jnp.logical_or(jnp.logical_and(lat, i % tpe == 0), i == nl)

        @pl.when(first)
        def _():
            st_ref[0] = stats

        @pl.when(jnp.logical_not(first))
        def _():
            st_ref[0] += stats

    def lat_tile(width):
        return pl.BlockSpec((tm, width), lambda i: (jnp.minimum(i, nl - 1), 0))

    def ctx_tile(width):
        return pl.BlockSpec((tm, width), lambda i: (jnp.maximum(i - nl, 0), 0))

    def tile(width):
        return pl.BlockSpec((tm, width), lambda i: (i, 0))

    tab = pl.BlockSpec((tm, 256), lambda i: (jnp.where(i < nl, i % tpe, tpe), 0))
    ex = pl.BlockSpec((1, 8, d), lambda i: (jnp.minimum(i // tpe, nex), 0, 0))
    return _hosted_call(
        body, (x2, ctx2, modv, g_attn, pg, drq, drk, dkc_r, drv, dvc_r, drg, dq_m, dkl, dkc, dvl, dvc, dxm,
               w_in, g_q, g_kv, w_uq, w_ukv, cos_t, sin_t), name="pre_bwd", grid=(nl + nc,),
        in_specs=[
            lat_tile(d), ctx_tile(d), ex, _full((1, d)), tile(PG_COLS),
            lat_tile(256), lat_tile(256), ctx_tile(256), lat_tile(512), ctx_tile(512), lat_tile(512),
            lat_tile(1024), lat_tile(1024), ctx_tile(1024), lat_tile(512), ctx_tile(512), lat_tile(d),
            _once(w_in.shape), _full((1, Q_LORA)), _full((1, KV_LORA)), _once(w_uq.shape), _once(w_ukv.shape),
            tab, tab,
        ],
        out_specs=[tile(IN_PAD), tile(1024), tile(1024), lat_tile(d), ex],
        out_shape=[
            jax.ShapeDtypeStruct((n_all, IN_PAD), BF16), jax.ShapeDtypeStruct((n_all, 1024), BF16),
            jax.ShapeDtypeStruct((n_all, 1024), BF16), jax.ShapeDtypeStruct((t_lat, d), F32),
            jax.ShapeDtypeStruct((nex + 1, 8, d), F32),
        ],
        sem=("arbitrary",), rider=rider)


MLA_SCALE = 1.0 / math.sqrt(MLA_NOPE + MLA_ROPE)
KEY_BLOCK = 1024


def _mla_specs(t_lat, seq, ctx_len, tq, heads=1):
    nqt = seq // tq
    cb = t_lat // ctx_len
    q = pl.BlockSpec((tq, heads * MLA_HEAD), lambda b, h, j: (b * nqt + j, h))
    kl = pl.BlockSpec((seq, heads * MLA_HEAD), lambda b, h, j: (b, h))
    kc = pl.BlockSpec((ctx_len, heads * MLA_HEAD), lambda b, h, j: (cb + b, h))
    vl = pl.BlockSpec((seq, heads * 128), lambda b, h, j: (b, h))
    vc = pl.BlockSpec((ctx_len, heads * 128), lambda b, h, j: (cb + b, h))
    o = pl.BlockSpec((tq, heads * 128), lambda b, h, j: (b * nqt + j, h))
    return q, kl, kc, vl, vc, o


FWD_HEADS = 2
BWD_HEADS = 1


def _mla_fwd(q, k, v, *, t_lat, seq, ctx_len, tq, rider=None):
    nex = t_lat // seq

    def body(q_ref, kl_ref, kc_ref, vl_ref, vc_ref, o_ref, lse_ref):
        for hh in range(FWD_HEADS):
            wide = slice(hh * MLA_HEAD, (hh + 1) * MLA_HEAD)
            cols = slice(hh * 128, (hh + 1) * 128)
            qb = q_ref[:, wide]
            s = _dot_nt(qb, kl_ref[:, wide])
            sc = _dot_nt(qb, kc_ref[:, wide])
            m = jnp.maximum(jnp.max(s, axis=-1, keepdims=True), jnp.max(sc, axis=-1, keepdims=True))
            p = jnp.exp(s - m)
            pc = jnp.exp(sc - m)
            total = jnp.sum(p, axis=-1, keepdims=True) + jnp.sum(pc, axis=-1, keepdims=True)
            o = _dot(p.astype(BF16), vl_ref[:, cols]) + _dot(pc.astype(BF16), vc_ref[:, cols])
            o_ref[:, cols] = (o * (1.0 / total)).astype(BF16)
            lse_ref[:, cols] = jnp.broadcast_to(m + jnp.log(total), (tq, 128))

    qs, kl, kc, vl, vc, os_ = _mla_specs(t_lat, seq, ctx_len, tq, FWD_HEADS)
    return _hosted_call(
        body, (q, k, k, v, v), name="mla_fwd", grid=(nex, HEADS // FWD_HEADS, seq // tq),
        in_specs=[qs, kl, kc, vl, vc], out_specs=[os_, os_],
        out_shape=[jax.ShapeDtypeStruct((t_lat, HEADS * 128), BF16), jax.ShapeDtypeStruct((t_lat, HEADS * 128), F32)],
        sem=("parallel", "parallel", "arbitrary"), rider=rider)


def _mla_bwd(q, k, v, ymla, lse, dmix, *, t_lat, seq, ctx_len, tq, rider=None):
    nex = t_lat // seq
    nqt = seq // tq
    t_ctx = nex * ctx_len
    kb = min(KEY_BLOCK, seq)

    def body(q_ref, kl_ref, kc_ref, vl_ref, vc_ref, o_ref, lse_ref, do_ref, dq_ref, dkl_out, dkc_out, dvl_out, dvc_out,
             dkl_ref, dkc_ref, dvl_ref, dvc_ref):
        j = pl.program_id(2)

        @pl.when(j == 0)
        def _():
            dkl_ref[...] = jnp.zeros(dkl_ref.shape, F32)
            dkc_ref[...] = jnp.zeros(dkc_ref.shape, F32)
            dvl_ref[...] = jnp.zeros(dvl_ref.shape, F32)
            dvc_ref[...] = jnp.zeros(dvc_ref.shape, F32)

        for hh in range(BWD_HEADS):
            wide = slice(hh * MLA_HEAD, (hh + 1) * MLA_HEAD)
            cols = slice(hh * 128, (hh + 1) * 128)
            qb = q_ref[:, wide]
            dob = do_ref[:, cols]
            delta = jnp.sum(dob.astype(F32) * o_ref[:, cols].astype(F32), axis=-1, keepdims=True)
            lse_row = lse_ref[:, hh * 128:hh * 128 + 1]

            def block(k_ref, v_ref, dk_ref, dv_ref, rows):
                kbl = k_ref[rows, wide]
                vbl = v_ref[rows, cols]
                p = jnp.exp(_dot_nt(qb, kbl) - lse_row)
                ds = (p * (_dot_nt(dob, vbl) - delta)).astype(BF16)
                dk_ref[rows, wide] += _dot_tn(ds, qb)
                dv_ref[rows, cols] += _dot_tn(p.astype(BF16), dob)
                return _dot(ds, kbl)

            dq = block(kc_ref, vc_ref, dkc_ref, dvc_ref, pl.ds(0, ctx_len))
            for i in range(seq // kb):
                dq = dq + block(kl_ref, vl_ref, dkl_ref, dvl_ref, pl.ds(i * kb, kb))
            dq_ref[:, wide] = dq.astype(BF16)

        @pl.when(j == nqt - 1)
        def _():
            dkl_out[...] = dkl_ref[...].astype(BF16)
            dkc_out[...] = dkc_ref[...].astype(BF16)
            dvl_out[...] = dvl_ref[...].astype(BF16)
            dvc_out[...] = dvc_ref[...].astype(BF16)

    g = BWD_HEADS
    qs, kl, kc, vl, vc, os_ = _mla_specs(t_lat, seq, ctx_len, tq, g)
    do_spec = pl.BlockSpec((tq, g * 128), lambda b, h, j: (b * nqt + j, HEADS // g + h))
    key_blocks = [(seq, g * MLA_HEAD), (ctx_len, g * MLA_HEAD), (seq, g * 128), (ctx_len, g * 128)]
    return _hosted_call(
        body, (q, k, k, v, v, ymla, lse, dmix), name="mla_bwd", grid=(nex, HEADS // g, nqt),
        in_specs=[qs, kl, kc, vl, vc, os_, os_, do_spec],
        out_specs=[qs] + [pl.BlockSpec(blk, lambda b, h, j: (b, h)) for blk in key_blocks],
        out_shape=[
            jax.ShapeDtypeStruct((t_lat, HEADS * MLA_HEAD), BF16),
            jax.ShapeDtypeStruct((t_lat, HEADS * MLA_HEAD), BF16),
            jax.ShapeDtypeStruct((t_ctx, HEADS * MLA_HEAD), BF16),
            jax.ShapeDtypeStruct((t_lat, HEADS * 128), BF16),
            jax.ShapeDtypeStruct((t_ctx, HEADS * 128), BF16),
        ],
        scratch_shapes=[pltpu.VMEM(blk, F32) for blk in key_blocks],
        sem=("parallel", "parallel", "arbitrary"), rider=rider)


def _decay_terms(lg, chunk, forward):
    ii = lax.broadcasted_iota(jnp.int32, (chunk, chunk), 0)
    jj = lax.broadcasted_iota(jnp.int32, (chunk, chunk), 1)
    diff = (ii - jj) if forward else (jj - ii)
    dist = jnp.maximum(diff, 0).astype(F32)
    dmat = jnp.where(diff >= 0, jnp.exp(lg * dist), 0.0)
    pos = lax.broadcasted_iota(jnp.int32, (chunk, 1), 0).astype(F32)
    if forward:
        e_q = pos + 1.0
        e_k = (chunk - 1.0) - pos
    else:
        e_q = chunk - pos
        e_k = pos
    wq = jnp.exp(lg * e_q)
    wk = jnp.exp(lg * e_k)
    cd = jnp.exp(jnp.full((1, 1), lg * chunk, F32))
    return dmat, dist, wq, wk, e_q, e_k, cd


def _ctx_weights(lg, ctx_len, forward):
    pos = lax.broadcasted_iota(jnp.int32, (ctx_len, 1), 0).astype(F32)
    e = ((ctx_len - 1.0) - pos) if forward else pos
    return jnp.exp(lg * e), e


def _pair_specs(t_lat, seq, ctx_len):
    cb = t_lat // ctx_len
    qk = pl.BlockSpec((seq, 128), lambda b, p: (b, p))
    v = pl.BlockSpec((seq, 256), lambda b, p: (b, p))
    kc = pl.BlockSpec((ctx_len, 128), lambda b, p: (cb + b, p))
    vc = pl.BlockSpec((ctx_len, 256), lambda b, p: (cb + b, p))
    return qk, v, kc, vc


def _lane_masks():
    lane = lax.broadcasted_iota(jnp.int32, (1, 128), 1)
    return [(lane // RET_DK) == hh for hh in (0, 1)]


def _ret_fwd_pair(rq, rk, rv, pg, lg, g_ret, *, t_lat, seq, ctx_len, chunk, rider=None):
    nex = t_lat // seq
    n_chunk = seq // chunk

    def body(q_ref, k_ref, v_ref, kc_ref, vc_ref, rg_ref, lg_ref, g_ref, y_ref, o_ref):
        pair = pl.program_id(1)
        masks = _lane_masks()
        kcf = kc_ref[...].astype(F32)
        chains = [(forward, hh) for forward in (True, False) for hh in (0, 1)]
        terms, s0 = [], []
        for forward, hh in chains:
            lgd = lg_ref[0 if forward else 1, 2 * pair + hh]
            terms.append(_decay_terms(lgd, chunk, forward))
            wc, _ = _ctx_weights(lgd, ctx_len, forward)
            s0.append(_dot_tn((jnp.where(masks[hh], kcf, 0.0) * wc).astype(BF16), vc_ref[:, hh * 128:(hh + 1) * 128]))
        both = [terms[hh][0] + terms[2 + hh][0] for hh in (0, 1)]
        o_ref[...] = jnp.zeros(o_ref.shape, F32)

        def step(t, states):
            new = [None] * 4
            for forward in (True, False):
                n = t if forward else n_chunk - 1 - t
                sl = pl.ds(pl.multiple_of(n * chunk, chunk), chunk)
                qb = q_ref[sl, :]
                kf_all = k_ref[sl, :].astype(F32)
                for hh in (0, 1):
                    c = (0 if forward else 2) + hh
                    _, _, wq, wk, _, _, cd = terms[c]
                    cols = slice(hh * 128, (hh + 1) * 128)
                    qm = jnp.where(masks[hh], qb, jnp.zeros((), BF16))
                    kf = jnp.where(masks[hh], kf_all, 0.0)
                    vb = v_ref[sl, cols]
                    o = wq * _dot(qm, states[c].astype(BF16))
                    if forward:
                        o = o + _dot((_dot_nt(qm, kf.astype(BF16)) * both[hh]).astype(BF16), vb)
                    o_ref[sl, cols] += o
                    new[c] = cd * states[c] + _dot_tn((kf * wk).astype(BF16), vb)
            return tuple(new)

        lax.fori_loop(0, n_chunk, step, tuple(s0))

        def norm_step(n, carry):
            sl = pl.ds(pl.multiple_of(n * chunk, chunk), chunk)
            for hh in (0, 1):
                cols = slice(hh * 128, (hh + 1) * 128)
                o = o_ref[sl, cols]
                mu = jnp.mean(o, axis=-1, keepdims=True)
                oc = o - mu
                var = jnp.mean(oc * oc, axis=-1, keepdims=True)
                rg = rg_ref[sl, cols]
                y_ref[sl, cols] = (oc * lax.rsqrt(var + EPS) * g_ref[:, cols] * (rg / (1.0 + jnp.exp(-rg)))).astype(BF16)
            return carry

        lax.fori_loop(0, n_chunk, norm_step, 0)

    qk, v, kc, vc = _pair_specs(t_lat, seq, ctx_len)
    return _hosted_call(
        body, (rq, rk, rv, rk, rv, pg, lg, g_ret), name="ret_fwd", grid=(nex, HEADS // 2),
        in_specs=[qk, qk, v, kc, vc, v, pl.BlockSpec(memory_space=pltpu.SMEM), pl.BlockSpec((1, 256), lambda b, p: (0, p))],
        out_specs=[v, v],
        out_shape=[jax.ShapeDtypeStruct((t_lat, HEADS * RET_DV), BF16), jax.ShapeDtypeStruct((t_lat, HEADS * RET_DV), F32)],
        sem=("parallel", "arbitrary"), rider=rider)


def _ret_bwd_pair(rq, rk, rv, pg, osum, dmix, lg, g_ret, *, t_lat, seq, ctx_len, chunk, rider=None):
    nex = t_lat // seq
    n_chunk = seq // chunk
    t_ctx = nex * ctx_len

    def body(q_ref, k_ref, v_ref, kc_ref, vc_ref, rg_ref, o_ref, dy_ref, lg_ref, g_ref,
             dq_ref, dk_ref, dv_ref, dkc_ref, dvc_ref, drg_ref, st_ref, do_s, s_st):
        pair = pl.program_id(1)
        masks = _lane_masks()
        kcf = kc_ref[...].astype(F32)

        def norm_step(n, dgains):
            sl = pl.ds(pl.multiple_of(n * chunk, chunk), chunk)
            out = []
            for hh in (0, 1):
                cols = slice(hh * 128, (hh + 1) * 128)
                gain = g_ref[:, cols]
                o = o_ref[sl, cols]
                mu = jnp.mean(o, axis=-1, keepdims=True)
                oc = o - mu
                rstd = lax.rsqrt(jnp.mean(oc * oc, axis=-1, keepdims=True) + EPS)
                ohat = oc * rstd
                rg = rg_ref[sl, cols]
                sg = 1.0 / (1.0 + jnp.exp(-rg))
                dy = dy_ref[sl, cols].astype(F32)
                don = dy * (rg * sg)
                drg_ref[sl, cols] = (dy * (ohat * gain) * (sg * (1.0 + rg * (1.0 - sg)))).astype(BF16)
                dohat = don * gain
                do_s[sl, cols] = rstd * (dohat - jnp.mean(dohat, axis=-1, keepdims=True)
                                         - ohat * jnp.mean(dohat * ohat, axis=-1, keepdims=True))
                out.append(dgains[hh] + jnp.sum(don * ohat, axis=0, keepdims=True))
            return tuple(out)

        zero_row = jnp.zeros((1, 128), F32)
        dgains = lax.fori_loop(0, n_chunk, norm_step, (zero_row, zero_row))
        dq_ref[...] = jnp.zeros(dq_ref.shape, F32)
        dk_ref[...] = jnp.zeros(dk_ref.shape, F32)
        dv_ref[...] = jnp.zeros(dv_ref.shape, F32)

        chains = [(forward, hh) for forward in (True, False) for hh in (0, 1)]
        terms, ctxw, s0 = [], [], []
        for forward, hh in chains:
            lgd = lg_ref[0 if forward else 1, 2 * pair + hh]
            terms.append(_decay_terms(lgd, chunk, forward))
            ctxw.append(_ctx_weights(lgd, ctx_len, forward))
            s0.append(_dot_tn((jnp.where(masks[hh], kcf, 0.0) * ctxw[-1][0]).astype(BF16), vc_ref[:, hh * 128:(hh + 1) * 128]))

        def chunk_at(t, ascending):
            n = t if ascending else n_chunk - 1 - t
            return n, pl.ds(pl.multiple_of(n * chunk, chunk), chunk)

        def state_step(t, states):
            new = []
            for c, (forward, hh) in enumerate(chains):
                n, sl = chunk_at(t, forward)
                wk, cd = terms[c][3], terms[c][6]
                s_st[c, n] = states[c]
                kf = jnp.where(masks[hh], k_ref[sl, :].astype(F32), 0.0)
                new.append(cd * states[c] + _dot_tn((kf * wk).astype(BF16), v_ref[sl, hh * 128:(hh + 1) * 128]))
            return tuple(new)

        lax.fori_loop(0, n_chunk, state_step, tuple(s0))

        both = [terms[hh][0] + terms[2 + hh][0] for hh in (0, 1)]

        def grad_step(t, carry):
            out = [None] * len(chains)
            in_chunk_b = [None, None]
            for forward in (True, False):
                n, sl = chunk_at(t, not forward)
                qb = q_ref[sl, :]
                kf_all = k_ref[sl, :].astype(F32)
                dq_sum = jnp.zeros((chunk, 128), F32)
                dk_sum = jnp.zeros((chunk, 128), F32)
                for hh in (0, 1):
                    c = (0 if forward else 2) + hh
                    g_next, dlg = carry[c]
                    dmat, dist, wq, wk, e_q, e_k, cd = terms[c]
                    cols = slice(hh * 128, (hh + 1) * 128)
                    qm = jnp.where(masks[hh], qb, jnp.zeros((), BF16))
                    kf = jnp.where(masks[hh], kf_all, 0.0)
                    kb = kf.astype(BF16)
                    vb = v_ref[sl, cols]
                    do = do_s[sl, cols]
                    dob = do.astype(BF16)
                    s_n = s_st[c, n]
                    s_nb = s_n.astype(BF16)
                    gb = g_next.astype(BF16)
                    dk_cross = wk * _dot_nt(vb, gb)
                    dv = _dot((kf * wk).astype(BF16), gb)
                    o_cross = wq * _dot(qm, s_nb)
                    dq_sum = dq_sum + wq * _dot_nt(dob, s_nb)
                    dk_sum = dk_sum + dk_cross
                    dlg = (dlg + chunk * cd * jnp.sum(g_next * s_n, keepdims=True)
                           + jnp.sum(e_k * jnp.sum(kf * dk_cross, axis=-1, keepdims=True), keepdims=True)
                           + jnp.sum(e_q * jnp.sum(o_cross * do, axis=-1, keepdims=True), keepdims=True))
                    if forward:
                        a_raw = _dot_nt(qm, kb)
                        da_raw = _dot_nt(dob, vb)
                        prod = a_raw * da_raw
                        dlg = dlg + jnp.sum(dist * dmat * prod, keepdims=True)
                        in_chunk_b[hh] = jnp.sum(terms[2 + hh][1] * terms[2 + hh][0] * prod, keepdims=True)
                        dab = (da_raw * both[hh]).astype(BF16)
                        dq_sum = dq_sum + _dot(dab, kb)
                        dk_sum = dk_sum + _dot_tn(dab, qm)
                        dv = dv + _dot_tn((a_raw * both[hh]).astype(BF16), dob)
                    else:
                        dlg = dlg + in_chunk_b[hh]
                    dv_ref[sl, cols] += dv
                    out[c] = (cd * g_next + _dot_tn((qm.astype(F32) * wq).astype(BF16), dob), dlg)
                dq_ref[sl, :] += dq_sum
                dk_ref[sl, :] += dk_sum
            return tuple(out)

        zero = (jnp.zeros((128, 128), F32), jnp.zeros((1, 1), F32))
        res = lax.fori_loop(0, n_chunk, grad_step, (zero,) * len(chains))
        dkc_sum = jnp.zeros((ctx_len, 128), F32)
        dvc = [jnp.zeros((ctx_len, 128), F32)] * 2
        dlgs = []
        for c, (forward, hh) in enumerate(chains):
            ds0, dlg = res[c]
            wc, e_c = ctxw[c]
            kcm = jnp.where(masks[hh], kcf, 0.0)
            ds0b = ds0.astype(BF16)
            dkc_part = wc * _dot_nt(vc_ref[:, hh * 128:(hh + 1) * 128], ds0b)
            dkc_sum = dkc_sum + dkc_part
            dvc[hh] = dvc[hh] + _dot((kcm * wc).astype(BF16), ds0b)
            dlgs.append(dlg + jnp.sum(e_c * jnp.sum(kcm * dkc_part, axis=-1, keepdims=True), keepdims=True))
        dkc_ref[...] = dkc_sum
        for hh in (0, 1):
            cols = slice(hh * 128, (hh + 1) * 128)
            dvc_ref[:, cols] = dvc[hh]
            st_ref[0, :, cols] = jnp.concatenate([
                dgains[hh], jnp.broadcast_to(dlgs[hh], (1, 128)), jnp.broadcast_to(dlgs[2 + hh], (1, 128)),
                jnp.zeros((5, 128), F32)], axis=0)

    qk, v, kc, vc = _pair_specs(t_lat, seq, ctx_len)
    return _hosted_call(
        body, (rq, rk, rv, rk, rv, pg, osum, dmix, lg, g_ret), name="ret_bwd", grid=(nex, HEADS // 2),
        in_specs=[qk, qk, v, kc, vc, v, v, v, pl.BlockSpec(memory_space=pltpu.SMEM),
                  pl.BlockSpec((1, 256), lambda b, p: (0, p))],
        out_specs=[
            qk, qk, v,
            pl.BlockSpec((ctx_len, 128), lambda b, p: (b, p)),
            pl.BlockSpec((ctx_len, 256), lambda b, p: (b, p)),
            v,
            pl.BlockSpec((1, 8, 256), lambda b, p: (b, 0, p)),
        ],
        out_shape=[
            jax.ShapeDtypeStruct((t_lat, 256), F32), jax.ShapeDtypeStruct((t_lat, 256), F32),
            jax.ShapeDtypeStruct((t_lat, 512), F32), jax.ShapeDtypeStruct((t_ctx, 256), F32),
            jax.ShapeDtypeStruct((t_ctx, 512), F32), jax.ShapeDtypeStruct((t_lat, 512), BF16),
            jax.ShapeDtypeStruct((nex, 8, 512), F32),
        ],
        scratch_shapes=[pltpu.VMEM((seq, 256), F32), pltpu.VMEM((4, n_chunk, 128, 128), F32)],
        sem=("parallel", "arbitrary"), rider=rider)


def _matmul_tn(a, b, *, bm, bn, bk, chip_major, name, out_dtype=F32, rider=None):
    tk, m = a.shape
    n = b.shape[1]
    slab = n // N_CHIPS
    per_block = bn // slab if chip_major else 1
    bk = max(c for c in range(LANES, min(bk, tk) + 1, LANES) if tk % c == 0)
    nk = tk // bk
    blk = (per_block, bm, slab) if chip_major else (bm, bn)

    def body(a_ref, b_ref, o_ref, acc_ref):
        k = pl.program_id(2)
        if chip_major:
            parts = [_dot_tn(a_ref[...], b_ref[:, s * slab:(s + 1) * slab]) for s in range(per_block)]
        else:
            parts = [_dot_tn(a_ref[...], b_ref[...])]

        @pl.when(k == 0)
        def _():
            for s, part in enumerate(parts):
                if chip_major:
                    acc_ref[s] = part
                else:
                    acc_ref[...] = part

        @pl.when(k > 0)
        def _():
            for s, part in enumerate(parts):
                if chip_major:
                    acc_ref[s] += part
                else:
                    acc_ref[...] += part

        @pl.when(k == nk - 1)
        def _():
            o_ref[...] = acc_ref[...].astype(out_dtype)

    if chip_major:
        out_spec = pl.BlockSpec(blk, lambda i, j, k: (j, i, 0))
        out_shape = jax.ShapeDtypeStruct((N_CHIPS, m, slab), out_dtype)
    else:
        out_spec = pl.BlockSpec(blk, lambda i, j, k: (i, j))
        out_shape = jax.ShapeDtypeStruct((m, n), out_dtype)
    (out,), carried = _hosted_call(
        body, (a, b), name=name, grid=(m // bm, n // bn, nk),
        in_specs=[pl.BlockSpec((bk, bm), lambda i, j, k: (k, i)), pl.BlockSpec((bk, bn), lambda i, j, k: (k, j))],
        out_specs=[out_spec], out_shape=[out_shape], scratch_shapes=[pltpu.VMEM(blk, F32)],
        sem=("parallel", "parallel", "arbitrary"), rider=rider)
    return out if rider is None else (out, carried)


_LATE = ("w_out", "w_ff1", "w_ff2")
_EARLY = ("w_in", "w_uq", "w_ukv")


def _local_step(x, ctx, tgt, modv, lg, g_attn, g_ffn, g_fin, g_ret, g_q, g_kv, w_in, w_uq, w_ukv, late, place=None,
                *, tm=256, tq=256, chunk=256):
    nex, seq, d = x.shape
    ctx_len = ctx.shape[1]
    t_lat = nex * seq
    tm = min(tm, seq)
    x2 = x.reshape(t_lat, d)
    ctx2 = ctx.reshape(nex * ctx_len, d)
    tgt2 = tgt.reshape(t_lat, d)
    tm_fwd = min(2 * tm, seq)
    cos_t, sin_t = _rope_tables(seq, tm)
    dims = dict(t_lat=t_lat, seq=seq, ctx_len=ctx_len)
    alone = place is None

    (hb, pg, rq, rk, rv, nq, nkv, q, k, v), crossed_a = _pre_fwd(
        x2, ctx2, modv, g_attn, w_in, g_q, g_kv, w_uq, w_ukv, *_rope_tables(seq, tm_fwd), seq=seq, tm=tm_fwd,
        rider=None if alone else _gather_ici_rider([late[2]]))
    (yret, osum), got = _ret_fwd_pair(
        rq, rk, rv, pg, lg, g_ret, chunk=min(2 * chunk, seq), **dims,
        rider=None if alone else _merge_riders(_gather_d2d_rider(crossed_a), _gather_ici_rider([late[3]])))
    (ymla, lse), got_rest = _mla_fwd(
        q, k, v, tq=tq, **dims,
        rider=None if alone else _merge_riders(_gather_rider([late[0], late[1]], staged=True), _gather_d2d_rider(got[1:])))
    w_out, w_ff1, w_ff2a, w_ff2b = late if alone else (got_rest[0], got_rest[1], got[0], got_rest[2])
    mix, act, du, h2, df, dmo, dmix, dxm, st_post = _post(yret, ymla, x2, tgt2, modv, g_ffn, g_fin, w_out.reshape(d, d),
                                                         w_ff1, w_ff2a, w_ff2b, seq=seq, tm=min(tm, 256))
    kw = dict(bm=1024, bn=1024, bk=2048, out_dtype=BF16)
    g_ff2 = _matmul_tn(act, df, chip_major=False, name="gw_ff2", **kw).reshape(N_CHIPS, D_FF // N_CHIPS, d)
    if alone:
        g_ff1 = _matmul_tn(h2, du, chip_major=True, name="gw_ff1", **kw)
        g_out = _matmul_tn(mix, dmo, chip_major=False, name="gw_out", **kw).reshape(N_CHIPS, d // N_CHIPS, d)
        (dq_m, dkl, dkc, dvl, dvc), _ = _mla_bwd(q, k, v, ymla, lse, dmix, tq=tq, **dims)
        (drq, drk, drv, dkc_r, dvc_r, drg, st_ret), _ = _ret_bwd_pair(rq, rk, rv, pg, osum, dmix, lg, g_ret, chunk=chunk,
                                                                      **dims)
        late_out = [g_out, g_ff1, g_ff2]
    else:
        core, slot = place
        g_ff1, x_ff2 = _matmul_tn(h2, du, chip_major=True, name="gw_ff1", rider=_exchange_rider([g_ff2]), **kw)
        g_out, x_ff1 = _matmul_tn(mix, dmo, chip_major=False, name="gw_out", rider=_exchange_rider([g_ff1]), **kw)
        g_out = g_out.reshape(N_CHIPS, d // N_CHIPS, d)
        p_ff2 = _add_half(g_ff2, x_ff2[0], core, "add_half_w_ff2")
        p_ff1 = _add_half(g_ff1, x_ff1[0], core, "add_half_w_ff1")
        (dq_m, dkl, dkc, dvl, dvc), (l_ff2, l_ff1, x_out) = _mla_bwd(
            q, k, v, ymla, lse, dmix, tq=min(seq, 512), **dims,
            rider=_merge_riders(_scatter_rider([p_ff2, p_ff1]), _exchange_rider([g_out])))
        p_out = _add_half(g_out, x_out, core, "add_half_w_out")
        m_ff2 = _sum_chips(p_ff2, l_ff2, slot, "sum_chips_w_ff2")
        m_ff1 = _sum_chips(p_ff1, l_ff1, slot, "sum_chips_w_ff1")
        (drq, drk, drv, dkc_r, dvc_r, drg, st_ret), (l_out,) = _ret_bwd_pair(
            rq, rk, rv, pg, osum, dmix, lg, g_ret, chunk=chunk, **dims, rider=_scatter_rider([p_out]))
        late_out = [_sum_chips(p_out, l_out, slot, "sum_chips_w_out"), m_ff1, m_ff2]
    (dpb, dqf, dkvf, gx, st_pre), _ = _pre_bwd(
        x2, ctx2, modv, g_attn, pg, drq, drk, dkc_r, drv, dvc_r, drg, dq_m, dkl, dkc, dvl, dvc, dxm, w_in, g_q, g_kv,
        w_uq, w_ukv, cos_t, sin_t, seq=seq, tm=tm)
    g_early = [
        _matmul_tn(dpb, hb, bm=IN_PAD // 2, bn=d, bk=1536, chip_major=False, name="gw_in"),
        _matmul_tn(dqf, nq, bm=HEADS * MLA_HEAD, bn=Q_LORA, bk=1536, chip_major=False, name="gw_uq"),
        _matmul_tn(nkv, dkvf, bm=KV_LORA, bn=HEADS * 256, bk=1536, chip_major=True, name="gw_ukv"),
    ]
    return gx.reshape(nex, seq, d), g_early, late_out, st_post, st_ret, st_pre


_ANY = pl.BlockSpec(memory_space=pl.ANY)
_VMEM = pl.BlockSpec(memory_space=pltpu.VMEM)
_OFFSETS = tuple((dx, dy, dc) for dx in (0, 1) for dy in (0, 1) for dc in (0, 1))[1:]
_CHIP_OFFSETS = ((1, 0), (0, 1), (1, 1))


def _place():
    return lax.axis_index("x"), lax.axis_index("y"), lax.axis_index("c")


def _flip(v, d):
    return 1 - v if d else v


def _gather8_rider(a, in_vmem=True):
    def copies(a_ref, o_ref, send, recv):
        x, y, z = _place()
        me = 4 * x + 2 * y + z
        out = []
        for k, (dx, dy, dc) in enumerate(_OFFSETS):
            peer = (_flip(x, dx), _flip(y, dy), _flip(z, dc))
            landing = o_ref.at[4 * peer[0] + 2 * peer[1] + peer[2]]
            out.append((
                pltpu.make_async_remote_copy(src_ref=a_ref, dst_ref=o_ref.at[me], send_sem=send.at[k],
                                             recv_sem=recv.at[k], device_id=peer, device_id_type=MESH),
                pltpu.make_async_remote_copy(src_ref=a_ref, dst_ref=landing, send_sem=send.at[k],
                                             recv_sem=recv.at[k], device_id=peer, device_id_type=MESH)))
        return me, out

    def start(ins, outs, sems):
        me, cps = copies(ins[0], outs[0], sems[0], sems[1])
        pltpu.make_async_copy(ins[0], outs[0].at[me], sems[2]).start()
        for out_cp, _ in cps:
            out_cp.start()

    def finish(ins, outs, sems):
        me, cps = copies(ins[0], outs[0], sems[0], sems[1])
        for out_cp, in_cp in cps:
            in_cp.wait_recv()
            out_cp.wait_send()
        pltpu.make_async_copy(ins[0], outs[0].at[me], sems[2]).wait()

    spec = [_VMEM] if in_vmem else [_ANY]
    return _Rider([a], [jax.ShapeDtypeStruct((N_DEV,) + a.shape, a.dtype)],
                  [pltpu.SemaphoreType.DMA((7,)), pltpu.SemaphoreType.DMA((7,)), pltpu.SemaphoreType.DMA],
                  start, finish, in_specs=spec, out_specs=spec)


def _merge_riders(*riders):
    ins, outs, sems, in_specs, out_specs, aliases, cuts = [], [], [], [], [], {}, []
    for r in riders:
        cuts.append((len(ins), len(outs), len(sems)))
        aliases.update({len(ins) + i: len(outs) + j for i, j in r.aliases.items()})
        ins += r.ins
        outs += r.out_shapes
        sems += r.sems
        in_specs += r.in_specs
        out_specs += r.out_specs

    def part(r, cut, r_ins, r_outs, r_sems):
        return (r_ins[cut[0]:cut[0] + len(r.ins)], r_outs[cut[1]:cut[1] + len(r.out_shapes)],
                r_sems[cut[2]:cut[2] + len(r.sems)])

    def start(r_ins, r_outs, r_sems):
        for r, cut in zip(riders, cuts):
            r.start(*part(r, cut, r_ins, r_outs, r_sems))

    def finish(r_ins, r_outs, r_sems):
        for r, cut in zip(riders, cuts):
            r.finish(*part(r, cut, r_ins, r_outs, r_sems))

    def middle(r_ins, r_outs, r_sems):
        for r, cut in zip(riders, cuts):
            if r.middle is not None:
                r.middle(*part(r, cut, r_ins, r_outs, r_sems))

    return _Rider(ins, outs, sems, start, finish, aliases=aliases, in_specs=in_specs, out_specs=out_specs,
                  middle=middle if any(r.middle is not None for r in riders) else None)


def _allgather8(a, name):
    return _run_rider(_gather8_rider(a), name)[0]


BF16_TILE_ROWS = 16


def _half(o, slot, which):
    r2 = o.shape[1] // 2
    if r2 % BF16_TILE_ROWS == 0:
        return o.at[slot, pl.ds(which * r2, r2)]
    c2 = o.shape[2] // 2
    assert c2 % LANES == 0
    return o.at[slot, :, pl.ds(which * c2, c2)]


def _gather_send(o_refs, send, recv):
    x, y, z = _place()
    chip = 2 * x + y
    for a, o in enumerate(o_refs):
        r2 = o.shape[1] // 2
        mine = _half(o, chip, z)
        for k, (dx, dy) in enumerate(_CHIP_OFFSETS):
            pltpu.make_async_remote_copy(
                src_ref=mine, dst_ref=mine, send_sem=send.at[a, k], recv_sem=recv.at[a, k],
                device_id=(_flip(x, dx), _flip(y, dy), z), device_id_type=MESH).start()


def _gather_landed(o_refs, send, recv, then=None):
    x, y, z = _place()
    chip = 2 * x + y
    for a, o in enumerate(o_refs):
        for k, (dx, dy) in enumerate(_CHIP_OFFSETS):
            landed = _half(o, 2 * _flip(x, dx) + _flip(y, dy), z)
            pltpu.make_async_remote_copy(
                src_ref=landed, dst_ref=landed, send_sem=send.at[a, k], recv_sem=recv.at[a, k],
                device_id=(_flip(x, dx), _flip(y, dy), z), device_id_type=MESH).wait_recv()
            if then is not None:
                then(a, k, landed)
    for a, o in enumerate(o_refs):
        mine = _half(o, chip, z)
        for k, (dx, dy) in enumerate(_CHIP_OFFSETS):
            pltpu.make_async_remote_copy(
                src_ref=mine, dst_ref=mine, send_sem=send.at[a, k], recv_sem=recv.at[a, k],
                device_id=(_flip(x, dx), _flip(y, dy), z), device_id_type=MESH).wait_send()


def _pass_on(o_refs, fsend, frecv, a, k, landed):
    x, y, z = _place()
    pltpu.make_async_remote_copy(
        src_ref=landed, dst_ref=landed, send_sem=fsend.at[a, k], recv_sem=frecv.at[a, k],
        device_id=(x, y, 1 - z), device_id_type=MESH).start()


def _passed_on(o_refs, fsend, frecv):
    x, y, z = _place()
    for a, o in enumerate(o_refs):
        for k, (dx, dy) in enumerate(_CHIP_OFFSETS):
            other = 2 * _flip(x, dx) + _flip(y, dy)
            got = _half(o, other, 1 - z)
            gave = _half(o, other, z)
            pltpu.make_async_remote_copy(
                src_ref=got, dst_ref=got, send_sem=fsend.at[a, k], recv_sem=frecv.at[a, k],
                device_id=(x, y, 1 - z), device_id_type=MESH).wait_recv()
            pltpu.make_async_remote_copy(
                src_ref=gave, dst_ref=gave, send_sem=fsend.at[a, k], recv_sem=frecv.at[a, k],
                device_id=(x, y, 1 - z), device_id_type=MESH).wait_send()


def _gather_finish(o_refs, send, recv, fsend, frecv):
    _gather_landed(o_refs, send, recv, functools.partial(_pass_on, o_refs, fsend, frecv))
    _passed_on(o_refs, fsend, frecv)


class _Rider:
    def __init__(self, ins, out_shapes, sems, start, finish, aliases=None, in_specs=None, out_specs=None, middle=None):
        self.ins, self.out_shapes, self.sems = list(ins), list(out_shapes), list(sems)
        self.start, self.finish, self.aliases = start, finish, dict(aliases or {})
        self.middle = middle
        self.in_specs = list(in_specs) if in_specs else [_ANY] * len(self.ins)
        self.out_specs = list(out_specs) if out_specs else [_ANY] * len(self.out_shapes)


def _run_rider(rider, name):
    r_in, r_out = len(rider.ins), len(rider.out_shapes)

    def body(*refs):
        ins, outs, sems = refs[:r_in], refs[r_in:r_in + r_out], refs[r_in + r_out:]
        rider.start(ins, outs, sems)
        if rider.middle is not None:
            rider.middle(ins, outs, sems)
        rider.finish(ins, outs, sems)

    return pl.pallas_call(
        body, name=name, in_specs=rider.in_specs, out_specs=rider.out_specs, out_shape=rider.out_shapes,
        input_output_aliases=rider.aliases, scratch_shapes=rider.sems,
    )(*rider.ins)


def _hosted_call(body, args, *, name, grid, in_specs, out_specs, out_shape, scratch_shapes=(), sem, rider=None,
                 prefetch=()):
    scratch_shapes = list(scratch_shapes)
    n_pf, n_in, n_out, n_sc = len(prefetch), len(in_specs), len(out_specs), len(scratch_shapes)
    r_in, r_out = (len(rider.ins), len(rider.out_shapes)) if rider else (0, 0)
    last = tuple(g - 1 for g in grid)

    def hosted(*refs):
        p = 0
        parts = []
        for cnt in (n_pf, n_in, r_in, n_out, r_out, n_sc):
            parts.append(refs[p:p + cnt])
            p += cnt
        pf, ins, r_ins, outs, r_outs, scratch = parts
        sems = refs[p:]
        ids = [pl.program_id(a) for a in range(len(grid))]
        is_first = functools.reduce(jnp.logical_and, [i == 0 for i in ids])
        is_last = functools.reduce(jnp.logical_and, [i == e for i, e in zip(ids, last)])

        @pl.when(is_first)
        def _():
            rider.start(r_ins, r_outs, sems)

        if rider.middle is not None:
            linear = functools.reduce(lambda acc, ig: acc * ig[1] + ig[0], zip(ids, grid), 0)

            @pl.when(linear == math.prod(grid) * 3 // 4)
            def _():
                rider.middle(r_ins, r_outs, sems)

        body(*pf, *ins, *outs, *scratch)

        @pl.when(is_last)
        def _():
            rider.finish(r_ins, r_outs, sems)

    if rider is None:
        kern, all_in, all_out, shapes, scratch, aliases, extra = body, list(in_specs), list(out_specs), list(out_shape), \
            scratch_shapes, {}, []
    else:
        kern, all_in, all_out = hosted, list(in_specs) + rider.in_specs, list(out_specs) + rider.out_specs
        shapes, scratch, extra = list(out_shape) + rider.out_shapes, scratch_shapes + rider.sems, rider.ins
        aliases = {n_pf + n_in + i: n_out + j for i, j in rider.aliases.items()}
        sem = ("arbitrary",) * len(grid)
    if prefetch:
        spec = dict(grid_spec=pltpu.PrefetchScalarGridSpec(
            num_scalar_prefetch=n_pf, grid=grid, in_specs=all_in, out_specs=all_out, scratch_shapes=scratch))
    else:
        spec = dict(grid=grid, in_specs=all_in, out_specs=all_out, scratch_shapes=scratch)
    res = pl.pallas_call(kern, name=name, out_shape=shapes, input_output_aliases=aliases,
                         compiler_params=_params(sem, VMEM_LIMIT), **spec)(*prefetch, *args, *extra)
    return list(res[:n_out]), list(res[n_out:])


def _gather_rider(ws, staged=False):
    n = len(ws)
    shapes = [jax.ShapeDtypeStruct(w.shape, w.dtype) for w in ws]
    sems = [pltpu.SemaphoreType.DMA((n, 3))] * 4
    aliases = {a: a for a in range(n)}

    def start(ins, outs, s):
        _gather_send(outs, s[0], s[1])

    if not staged:
        return _Rider(ws, shapes, sems, start, lambda ins, outs, s: _gather_finish(outs, *s), aliases=aliases)
    return _Rider(
        ws, shapes, sems, start, lambda ins, outs, s: _passed_on(outs, s[2], s[3]), aliases=aliases,
        middle=lambda ins, outs, s: _gather_landed(outs, s[0], s[1], functools.partial(_pass_on, outs, s[2], s[3])))


def _gather_ici_rider(ws):
    n = len(ws)
    return _Rider(
        ws, [jax.ShapeDtypeStruct(w.shape, w.dtype) for w in ws], [pltpu.SemaphoreType.DMA((n, 3))] * 2,
        lambda ins, outs, sems: _gather_send(outs, sems[0], sems[1]),
        lambda ins, outs, sems: _gather_landed(outs, sems[0], sems[1]),
        aliases={a: a for a in range(n)})


def _gather_d2d_rider(ws):
    n = len(ws)

    def start(ins, outs, sems):
        x, y, z = _place()
        for a, o in enumerate(outs):
            for k, (dx, dy) in enumerate(_CHIP_OFFSETS):
                _pass_on(outs, sems[0], sems[1], a, k, _half(o, 2 * _flip(x, dx) + _flip(y, dy), z))

    return _Rider(
        ws, [jax.ShapeDtypeStruct(w.shape, w.dtype) for w in ws], [pltpu.SemaphoreType.DMA((n, 3))] * 2,
        start, lambda ins, outs, sems: _passed_on(outs, sems[0], sems[1]), aliases={a: a for a in range(n)})


def _copies_rider(ins, out_shapes, sem_shape, make):
    def start(r_ins, r_outs, sems):
        for cp in make(r_ins, r_outs, sems[0], sems[1]):
            cp.start()

    def finish(r_ins, r_outs, sems):
        for cp in make(r_ins, r_outs, sems[0], sems[1]):
            cp.wait()

    return _Rider(ins, out_shapes, [pltpu.SemaphoreType.DMA(sem_shape)] * 2, start, finish)


def _exchange_rider(gs):
    def make(g_refs, r_refs, send, recv):
        x, y, z = _place()
        return [pltpu.make_async_remote_copy(
            src_ref=g.at[:, pl.ds((1 - z) * (g.shape[1] // 2), g.shape[1] // 2)], dst_ref=r, send_sem=send.at[a],
            recv_sem=recv.at[a], device_id=(x, y, 1 - z), device_id_type=MESH)
            for a, (g, r) in enumerate(zip(g_refs, r_refs))]

    shapes = [jax.ShapeDtypeStruct((g.shape[0], g.shape[1] // 2, g.shape[2]), g.dtype) for g in gs]
    return _copies_rider(gs, shapes, (len(gs),), make)


def _add_half(g, recv, core, name):
    s, r, c = g.shape
    r2 = r // 2
    rb = r2
    for cand in (256, 128, 64):
        if r2 % cand == 0:
            rb = cand
            break
    g4 = g.reshape(s, 2, r2, c)

    def body(core_ref, g_ref, r_ref, o_ref):
        o_ref[...] = (g_ref[...].astype(F32) + r_ref[...].astype(F32)).astype(BF16)

    return pl.pallas_call(
        body, name=name,
        grid_spec=pltpu.PrefetchScalarGridSpec(
            num_scalar_prefetch=1, grid=(s, r2 // rb),
            in_specs=[pl.BlockSpec((None, None, rb, c), lambda i, j, cr: (i, cr[0], j, 0)),
                      pl.BlockSpec((None, rb, c), lambda i, j, cr: (i, j, 0))],
            out_specs=pl.BlockSpec((None, rb, c), lambda i, j, cr: (i, j, 0))),
        out_shape=jax.ShapeDtypeStruct((s, r2, c), BF16),
        compiler_params=_params(("parallel", "parallel")),
    )(core, g4, recv)


def _scatter_rider(ps):
    def make(p_refs, o_refs, send, recv):
        x, y, z = _place()
        copies = []
        for a, (p, o) in enumerate(zip(p_refs, o_refs)):
            for k, (dx, dy) in enumerate(_CHIP_OFFSETS):
                other = 2 * _flip(x, dx) + _flip(y, dy)
                copies.append(pltpu.make_async_remote_copy(
                    src_ref=p.at[other], dst_ref=o.at[k], send_sem=send.at[a, k], recv_sem=recv.at[a, k],
                    device_id=(_flip(x, dx), _flip(y, dy), z), device_id_type=MESH))
        return copies

    shapes = [jax.ShapeDtypeStruct((3,) + p.shape[1:], p.dtype) for p in ps]
    return _copies_rider(ps, shapes, (len(ps), 3), make)


def _sum_chips(p, landed, chip, name):
    _, r2, c = p.shape
    rb = r2
    for cand in (256, 128, 64):
        if r2 % cand == 0:
            rb = cand
            break

    def body(s_ref, p_ref, l_ref, o_ref):
        acc = p_ref[...].astype(F32)
        for k in range(3):
            acc = acc + l_ref[k].astype(F32)
        o_ref[...] = acc

    return pl.pallas_call(
        body, name=name,
        grid_spec=pltpu.PrefetchScalarGridSpec(
            num_scalar_prefetch=1, grid=(r2 // rb,),
            in_specs=[pl.BlockSpec((None, rb, c), lambda i, s: (s[0], i, 0)),
                      pl.BlockSpec((3, rb, c), lambda i, s: (0, i, 0))],
            out_specs=pl.BlockSpec((rb, c), lambda i, s: (i, 0))),
        out_shape=jax.ShapeDtypeStruct((r2, c), F32),
        compiler_params=_params(("parallel",)),
    )(chip, p, landed)


def _swap_rider(hs):
    def make(h_refs, o_refs, send, recv):
        x, y, z = _place()
        return [pltpu.make_async_remote_copy(
            src_ref=h, dst_ref=o, send_sem=send.at[a], recv_sem=recv.at[a], device_id=(x, y, 1 - z),
            device_id_type=MESH) for a, (h, o) in enumerate(zip(h_refs, o_refs))]

    return _copies_rider(hs, [jax.ShapeDtypeStruct(h.shape, h.dtype) for h in hs], (len(hs),), make)


def _reduce_scatter_vmem(gs, rows, rider, name):
    n = len(gs)
    r_in, r_out = len(rider.ins), len(rider.out_shapes)
    halves = [(r // 2, g.shape[-1]) for g, (r, _) in zip(gs, rows)]

    def body(*refs):
        p = 0
        parts = []
        for cnt in (n, r_in, n, n, r_out, n, n, n, 6):
            parts.append(refs[p:p + cnt])
            p += cnt
        g_refs, r_ins, mine, theirs, r_outs, recv, part, land, sems = parts
        r_sems = refs[p:]
        xs, xr, ss, sr, ws, wr = sems
        x, y, z = _place()
        chip = 2 * x + y
        sib = (x, y, 1 - z)
        rider.start(r_ins, r_outs, r_sems)

        def half_of(a, s, which):
            r2 = halves[a][0]
            if len(g_refs[a].shape) == 3:
                return g_refs[a].at[s, pl.ds(pl.multiple_of(which * r2, 8), r2)]
            return g_refs[a].at[pl.ds(pl.multiple_of(s * rows[a][1] + which * r2, 8), r2)]

        exchange = [pltpu.make_async_remote_copy(
            src_ref=half_of(a, s, 1 - z), dst_ref=recv[a].at[s], send_sem=xs.at[a, s], recv_sem=xr.at[a, s],
            device_id=sib, device_id_type=MESH) for a in range(n) for s in range(N_CHIPS)]
        for cp in exchange:
            cp.start()
        for cp in exchange:
            cp.wait()
        for a in range(n):
            for s in range(N_CHIPS):
                part[a][s] = (half_of(a, s, z)[...] + recv[a][s]).astype(BF16)
        scatter = []
        for a in range(n):
            for k, (dx, dy) in enumerate(_CHIP_OFFSETS):
                other = 2 * _flip(x, dx) + _flip(y, dy)
                scatter.append(pltpu.make_async_remote_copy(
                    src_ref=part[a].at[other], dst_ref=land[a].at[k], send_sem=ss.at[a, k], recv_sem=sr.at[a, k],
                    device_id=(_flip(x, dx), _flip(y, dy), z), device_id_type=MESH))
        for cp in scatter:
            cp.start()
        for cp in scatter:
            cp.wait()
        for a in range(n):
            acc = part[a][chip].astype(F32)
            for k in range(3):
                acc = acc + land[a][k].astype(F32)
            mine[a][...] = acc
        swap = [pltpu.make_async_remote_copy(
            src_ref=mine[a], dst_ref=theirs[a], send_sem=ws.at[a], recv_sem=wr.at[a], device_id=sib,
            device_id_type=MESH) for a in range(n)]
        for cp in swap:
            cp.start()
        for cp in swap:
            cp.wait()
        rider.finish(r_ins, r_outs, r_sems)

    half_shapes = [jax.ShapeDtypeStruct(h, F32) for h in halves]
    res = pl.pallas_call(
        body, name=name, in_specs=[_VMEM] * n + rider.in_specs, out_specs=[_VMEM] * (2 * n) + rider.out_specs,
        out_shape=half_shapes + half_shapes + rider.out_shapes,
        scratch_shapes=[pltpu.VMEM((N_CHIPS,) + h, F32) for h in halves] + [pltpu.VMEM((N_CHIPS,) + h, BF16) for h in halves]
        + [pltpu.VMEM((3,) + h, BF16) for h in halves]
        + [pltpu.SemaphoreType.DMA((n, N_CHIPS))] * 2 + [pltpu.SemaphoreType.DMA((n, 3))] * 2
        + [pltpu.SemaphoreType.DMA((n,))] * 2 + rider.sems,
        input_output_aliases={n + i: 2 * n + j for i, j in rider.aliases.items()},
        compiler_params=_params(None, VMEM_LIMIT),
    )(*gs, *rider.ins)
    return list(res[:n]), list(res[n:2 * n]), list(res[2 * n:])


SMALL_ROWS = 32
PACK_ROWS = 16


def _pack_small(st_post, st_ret, st_pre):
    d = st_post.shape[2]

    def body(po_ref, re_ref, pr_ref, o_ref):
        o_ref[...] = jnp.zeros(o_ref.shape, F32)
        o_ref[0:1, :] = pr_ref[0, 2:3, :] + pr_ref[1, 2:3, :] + pr_ref[2, 2:3, :]
        o_ref[1:2, :] = po_ref[0, 4:5, :] + po_ref[1, 4:5, :]
        o_ref[2:3, :] = po_ref[0, 5:6, :] + po_ref[1, 5:6, :]
        o_ref[3:4, 0:512] = re_ref[0, 0:1, :] + re_ref[1, 0:1, :]
        o_ref[4:5, :] = pr_ref[0, 3:4, :] + pr_ref[1, 3:4, :] + pr_ref[2, 3:4, :]
        o_ref[5:6, :] = pr_ref[0, 4:5, :] + pr_ref[1, 4:5, :] + pr_ref[2, 4:5, :]
        lane = lax.broadcasted_iota(jnp.int32, (1, LANES), 1)
        for row, src in ((6, 1), (10, 2)):
            acc = jnp.zeros((1, LANES), F32)
            for hd in range(HEADS):
                grp = re_ref[0, src:src + 1, hd * LANES:(hd + 1) * LANES] + re_ref[1, src:src + 1, hd * LANES:(hd + 1) * LANES]
                acc = acc + jnp.where(lane == hd, grp, 0.0)
            o_ref[row:row + 1, 0:LANES] = acc
        o_ref[7:8, :] = po_ref[0, 6:7, :] + po_ref[1, 6:7, :]
        o_ref[8:9, :] = pr_ref[2, 0:1, :]
        o_ref[9:10, :] = pr_ref[2, 1:2, :]
        for e in range(2):
            b = 12 + 6 * e
            o_ref[b:b + 1, :] = pr_ref[e, 0:1, :]
            o_ref[b + 1:b + 2, :] = pr_ref[e, 1:2, :]
            o_ref[b + 2:b + 3, :] = po_ref[e, 3:4, :]
            o_ref[b + 3:b + 4, :] = po_ref[e, 0:1, :]
            o_ref[b + 4:b + 5, :] = po_ref[e, 1:2, :]
            o_ref[b + 5:b + 6, :] = po_ref[e, 2:3, :]

    return pl.pallas_call(body, name="pack_small", out_shape=jax.ShapeDtypeStruct((SMALL_ROWS, d), F32))(st_post, st_ret, st_pre)


def _small_reduce(gathered):
    d = gathered.shape[2]

    def body(g_ref, o_ref):
        tot = g_ref[0, 0:PACK_ROWS, :]
        for dev in range(1, N_DEV):
            tot = tot + g_ref[dev, 0:PACK_ROWS, :]
        o_ref[0:PACK_ROWS, :] = tot
        for j in range(6):
            acc = g_ref[0, 12 + j:13 + j, :] + g_ref[0, 18 + j:19 + j, :]
            for dev in range(1, N_DEV):
                acc = acc + g_ref[dev, 12 + j:13 + j, :] + g_ref[dev, 18 + j:19 + j, :]
            if j < 2:
                acc = acc + o_ref[8 + j:9 + j, :]
            o_ref[PACK_ROWS + j:PACK_ROWS + j + 1, :] = acc
        o_ref[PACK_ROWS + 6:PACK_ROWS + 8, :] = jnp.zeros((2, d), F32)

    return pl.pallas_call(body, name="small_reduce", out_shape=jax.ShapeDtypeStruct((PACK_ROWS + 8, d), F32))(gathered)


_SMALL = (("g_attn", 0, 1024), ("g_ffn", 1, 1024), ("g_final", 2, 1024), ("g_ret", 3, 512), ("g_q_lora", 4, 384),
          ("g_kv_lora", 5, 256), ("ret_decay_fwd", 6, HEADS), ("ret_decay_bwd", 10, HEADS))
_SMALL_NAMES = tuple(s[0] for s in _SMALL) + ("c_ctx", "b_ada")


def _small_final(tot, dcc, sg8, ws, ms, vs):
    d = tot.shape[1]
    n = len(_SMALL_NAMES)

    def body(*refs):
        t_ref, dcc_ref, sg_ref = refs[0:3]
        w_refs, m_refs, v_refs = refs[3:3 + n], refs[3 + n:3 + 2 * n], refs[3 + 2 * n:3 + 3 * n]
        outs = refs[3 + 3 * n:]
        g_refs, d_refs, mo_refs, vo_refs = outs[0:n], outs[n:2 * n], outs[2 * n:3 * n], outs[3 * n:4 * n]
        l_ref = outs[4 * n]

        def update(i, g, sl=None):
            pick = (lambda r: r[...]) if sl is None else (lambda r: r[:, sl])
            dl, mn, vn = _adam_math(pick(w_refs[i]), g, pick(m_refs[i]), pick(v_refs[i]))
            if sl is None:
                g_refs[i][...], d_refs[i][...], mo_refs[i][...], vo_refs[i][...] = g, dl, mn, vn
            else:
                g_refs[i][:, sl], d_refs[i][:, sl], mo_refs[i][:, sl], vo_refs[i][:, sl] = g, dl, mn, vn

        for i, (name, row, width) in enumerate(_SMALL):
            g = t_ref[row:row + 1, 0:width]
            if name == "ret_decay_fwd":
                g = g * sg_ref[0:1, 0:width]
            elif name == "ret_decay_bwd":
                g = g * sg_ref[1:2, 0:width]
            update(i, g)
        i_cc, i_b = n - 2, n - 1
        cc = w_refs[i_cc][...]
        s = 1.0 / (1.0 + jnp.exp(-cc))
        dsilu = dcc_ref[0, 0:1, :] + dcc_ref[2, 0:1, :] + dcc_ref[4, 0:1, :] + dcc_ref[6, 0:1, :]
        update(i_cc, dsilu * (s * (1.0 + cc * (1.0 - s))))
        for j in range(6):
            update(i_b, t_ref[PACK_ROWS + j:PACK_ROWS + j + 1, :], pl.ds(j * d, d))
        l_ref[...] = jnp.broadcast_to((0.5 / d) * jnp.sum(t_ref[7:8, :], keepdims=True), l_ref.shape)

    shapes = [jax.ShapeDtypeStruct(a.shape, F32) for a in ws]
    outs = pl.pallas_call(
        body, name="small_final", out_shape=shapes * 4 + [jax.ShapeDtypeStruct((8, LANES), F32)],
    )(tot, dcc, sg8, *ws, *ms, *vs)
    return outs[0:n], outs[n:2 * n], outs[2 * n:3 * n], outs[3 * n:4 * n], outs[4 * n]


_WEIGHTS = ("c_ctx", "w_ada", "b_ada", "g_attn", "g_ffn", "w_in", "ret_decay_fwd", "ret_decay_bwd", "g_ret", "g_q_lora",
            "w_uq", "g_kv_lora", "w_ukv", "w_out", "w_ff1", "w_ff2", "g_final")
_BIG = ("w_in", "w_uq", "w_ukv", "w_out", "w_ff1", "w_ff2")
_TRANSPOSED = ("w_in", "w_uq")


def kernel(x, c, ctx, c_ctx, w_ada, b_ada, g_attn, g_ffn, w_in, ret_decay_fwd, ret_decay_bwd, g_ret, g_q_lora, w_uq, g_kv_lora, w_ukv, w_out, w_ff1, w_ff2, g_final, loss_target, m_c_ctx, m_w_ada, m_b_ada, m_g_attn, m_g_ffn, m_w_in, m_ret_decay_fwd, m_ret_decay_bwd, m_g_ret, m_g_q_lora, m_w_uq, m_g_kv_lora, m_w_ukv, m_w_out, m_w_ff1, m_w_ff2, m_g_final, v_c_ctx, v_w_ada, v_b_ada, v_g_attn, v_g_ffn, v_w_in, v_ret_decay_fwd, v_ret_decay_bwd, v_g_ret, v_g_q_lora, v_w_uq, v_g_kv_lora, v_w_ukv, v_w_out, v_w_ff1, v_w_ff2, v_g_final):
    w = dict(c_ctx=c_ctx, w_ada=w_ada, b_ada=b_ada, g_attn=g_attn, g_ffn=g_ffn, w_in=w_in, ret_decay_fwd=ret_decay_fwd,
             ret_decay_bwd=ret_decay_bwd, g_ret=g_ret, g_q_lora=g_q_lora, w_uq=w_uq, g_kv_lora=g_kv_lora, w_ukv=w_ukv,
             w_out=w_out, w_ff1=w_ff1, w_ff2=w_ff2, g_final=g_final)
    m = dict(c_ctx=m_c_ctx, w_ada=m_w_ada, b_ada=m_b_ada, g_attn=m_g_attn, g_ffn=m_g_ffn, w_in=m_w_in,
             ret_decay_fwd=m_ret_decay_fwd, ret_decay_bwd=m_ret_decay_bwd, g_ret=m_g_ret, g_q_lora=m_g_q_lora, w_uq=m_w_uq,
             g_kv_lora=m_g_kv_lora, w_ukv=m_w_ukv, w_out=m_w_out, w_ff1=m_w_ff1, w_ff2=m_w_ff2, g_final=m_g_final)
    v = dict(c_ctx=v_c_ctx, w_ada=v_w_ada, b_ada=v_b_ada, g_attn=v_g_attn, g_ffn=v_g_ffn, w_in=v_w_in,
             ret_decay_fwd=v_ret_decay_fwd, ret_decay_bwd=v_ret_decay_bwd, g_ret=v_g_ret, g_q_lora=v_g_q_lora, w_uq=v_w_uq,
             g_kv_lora=v_g_kv_lora, w_ukv=v_w_ukv, w_out=v_w_out, w_ff1=v_w_ff1, w_ff2=v_w_ff2, g_final=v_g_final)
    xi, yi, ci = lax.axis_index("x"), lax.axis_index("y"), lax.axis_index("c")
    chip = 2 * xi + yi
    dev = 2 * chip + ci
    nex, seq, d = x.shape
    n_ada = w_ada.shape[2]

    dec = jnp.zeros((8, LANES), F32).at[0, :HEADS].set(ret_decay_fwd[0]).at[1, :HEADS].set(ret_decay_bwd[0])
    lg8, sg8 = _decay_prep(dec)
    lg = lg8[:2, :HEADS]

    def shard_of(t, k):
        return t[k][0].T if k in _TRANSPOSED else t[k][0]

    shard = {k: shard_of(w, k) for k in _BIG}
    head_rows = MLA_NOPE + MLA_ROPE
    shard["w_uq"] = jnp.pad(shard["w_uq"], ((0, MLA_HEAD - head_rows), (0, 0)))
    slot = chip.reshape(1).astype(jnp.int32)
    core = ci.reshape(1).astype(jnp.int32)
    slots = {k: _cast_into_slot(shard[k], slot, "cast_" + k) for k in _EARLY}
    half_ff = shard["w_ff2"].shape[0] // 2
    late_pieces = [(shard["w_out"], 0, shard["w_out"].shape[0]), (shard["w_ff1"], 0, shard["w_ff1"].shape[0]),
                   (shard["w_ff2"], 0, half_ff), (shard["w_ff2"], half_ff, half_ff)]
    late_slots, (w_in_x, w_uq_x, w_ukv_x, c8) = _cast_into_slots(
        late_pieces, slot, "cast_late",
        rider=_merge_riders(_gather_ici_rider([slots[k] for k in _EARLY]),
                            _gather8_rider(jnp.pad(c, ((0, 8 - nex), (0, 0))), in_vmem=False)))

    a_in = jnp.concatenate([c8[:, :nex].reshape(N_DEV * nex, d), c_ctx.reshape(1, d), jnp.zeros((7, d), F32)], axis=0)
    b_sh = lax.dynamic_slice(b_ada, (0, chip * n_ada), (1, n_ada))
    mod_sh = _mod_fwd(a_in, w_ada[0], b_sh)
    mod8, w_in_f, w_uq_k, w_ukv_k = _run_rider(
        _merge_riders(_gather8_rider(mod_sh), _gather_d2d_rider([w_in_x, w_uq_x, w_ukv_x])), "ag_early")
    w_in_k = jnp.pad(w_in_f.reshape(IN_COLS, d), ((0, IN_PAD - IN_COLS), (0, 0)))
    mod_all = mod8[0::2].transpose(1, 0, 2).reshape(a_in.shape[0], N_CHIPS * n_ada)
    mod_me = lax.dynamic_slice(mod_all, (nex * dev, 0), (nex, N_CHIPS * n_ada)).reshape(nex, 6, d)
    mod_c = mod_all[N_DEV * nex].reshape(1, 6, d)
    modv = jnp.pad(jnp.concatenate([mod_me, mod_c], axis=0), ((0, 0), (0, 2), (0, 0)))

    gx, g_early, late, st_post, st_ret, st_pre = _local_step(
        x, ctx, loss_target, modv, lg, g_attn, g_ffn, g_final.reshape(1, d), g_ret, g_q_lora, g_kv_lora,
        w_in_k, w_uq_k, w_ukv_k, late_slots, (core, slot))

    mine, theirs, (*late_theirs, gathered) = _reduce_scatter_vmem(
        g_early, [(IN_COLS // N_CHIPS, IN_COLS // N_CHIPS), (head_rows, MLA_HEAD), (KV_LORA, KV_LORA)],
        _merge_riders(_swap_rider(late), _gather8_rider(_pack_small(st_post, st_ret, st_pre))), "rs_early")
    tot = _small_reduce(gathered)
    dm = jnp.concatenate([
        gathered[:, 12:24].reshape(N_DEV * nex, 6 * d),
        jnp.concatenate([tot[8:10].reshape(1, 2 * d), jnp.zeros((1, 4 * d), F32)], axis=1),
        jnp.zeros((7, 6 * d), F32)], axis=0)
    dm_sh = lax.dynamic_slice(dm, (0, chip * n_ada), (dm.shape[0], n_ada))
    g_ada, da = _mod_bwd(a_in, dm_sh, w_ada[0])
    dcc = _allgather8(da[N_DEV * nex:], "ag_dcc")
    halves = dict(zip(_EARLY, zip(mine, theirs)))
    halves.update(zip(_LATE, zip(late, late_theirs)))
    grad, delta, new_m, new_v = {}, {}, {}, {}
    for k in _BIG:
        a, b = halves[k]
        res = _adamw_halves(shard_of(w, k), a, b, shard_of(m, k), shard_of(v, k), core, "adamw_" + k)
        grad[k], delta[k], new_m[k], new_v[k] = [(o.T if k in _TRANSPOSED else o).reshape(w[k].shape) for o in res]

    shp = w_ada.shape
    outs, _ = _adamw(w_ada[0], g_ada, m["w_ada"][0], v["w_ada"][0], "adamw_w_ada")
    grad["w_ada"] = g_ada.reshape(shp)
    delta["w_ada"], new_m["w_ada"], new_v["w_ada"] = [o.reshape(shp) for o in outs]
    rows = [{k: t[k].reshape(1, -1) for k in _SMALL_NAMES} for t in (w, m, v)]
    small = _small_final(tot, dcc, sg8, *[[t[k] for k in _SMALL_NAMES] for t in rows])
    for res, outs in zip((grad, delta, new_m, new_v), small[:4]):
        for k, o in zip(_SMALL_NAMES, outs):
            res[k] = o.reshape(w[k].shape)
    return (small[4][0, 0], gx, *[grad[k] for k in _WEIGHTS], *[delta[k] for k in _WEIGHTS],
            *[new_m[k] for k in _WEIGHTS], *[new_v[k] for k in _WEIGHTS])
```

```python
import functools
import math

import jax
import jax.numpy as jnp
from jax import lax
from jax.experimental import pallas as pl
from jax.experimental.pallas import tpu as pltpu

F32 = jnp.float32
BF16 = jnp.bfloat16
MESH = pl.DeviceIdType.MESH

EPS = 1e-6
D_MODEL = 1024
D_FF = 4096
HEADS = 4
RET_DK = 64
RET_DV = 128
MLA_NOPE = 128
MLA_ROPE = 64
MLA_HEAD = 256
Q_LORA = 384
KV_LORA = 256
GRID_W = 64
ROPE_BASE = 10000.0
IN_COLS = 2240
IN_PAD = 2304
PG_COLS = 1152
N_CHIPS = 4
N_DEV = 8
LANES = 128
ADAM_LR = 0.001
ADAM_B1 = 0.9
ADAM_B2 = 0.999
ADAM_EPS = 1e-08
ADAM_WD = 0.01
ADAM_STEP = 10
VMEM_LIMIT = 56 * 1024 * 1024


def _dot(a, b):
    return jnp.dot(a, b, preferred_element_type=F32)


def _dot_nt(a, b):
    return lax.dot_general(a, b, (((1,), (1,)), ((), ())), preferred_element_type=F32)


def _dot_tn(a, b):
    return lax.dot_general(a, b, (((0,), (0,)), ((), ())), preferred_element_type=F32)


def _params(sem=None, vmem=None):
    return pltpu.CompilerParams(dimension_semantics=sem, vmem_limit_bytes=vmem)


def _full(shape):
    n = len(shape)
    return pl.BlockSpec(shape, lambda *_: (0,) * n)


def _once(shape):
    n = len(shape)
    return pl.BlockSpec(shape, lambda *_: (0,) * n, pipeline_mode=pl.Buffered(1))


def _rope(x, cos, sin):
    w = x.shape[-1]
    lo = (lax.broadcasted_iota(jnp.int32, (1, w), 1) % 64) < 32
    swapped = jnp.where(lo, pltpu.roll(x, w - 32, 1), pltpu.roll(x, 32, 1))
    return x * cos + swapped * sin


def _rope_t(g, cos, sin):
    w = g.shape[-1]
    lo = (lax.broadcasted_iota(jnp.int32, (1, w), 1) % 64) < 32
    t = g * sin
    swapped = jnp.where(lo, pltpu.roll(t, w - 32, 1), pltpu.roll(t, 32, 1))
    return g * cos + swapped


def _rope_tables(seq, tm):
    rows = seq // GRID_W
    row = jnp.repeat(jnp.arange(rows, dtype=F32), GRID_W)
    col = jnp.tile(jnp.arange(GRID_W, dtype=F32), rows)
    n_freq = RET_DK // 4
    freq = ROPE_BASE ** (-jnp.arange(n_freq, dtype=F32) / n_freq)
    ang = jnp.concatenate([row[:, None] * freq, col[:, None] * freq], axis=-1)
    cos, sin = jnp.cos(ang), jnp.sin(ang)
    cos_t = jnp.tile(jnp.concatenate([cos, cos], -1), (1, HEADS))
    sin_t = jnp.tile(jnp.concatenate([-sin, sin], -1), (1, HEADS))
    cos_t = jnp.concatenate([cos_t, jnp.ones((tm, 4 * RET_DK), F32)], 0)
    sin_t = jnp.concatenate([sin_t, jnp.zeros((tm, 4 * RET_DK), F32)], 0)
    return cos_t, sin_t


def _adam_math(w, g, m, v):
    mn = ADAM_B1 * m + (1.0 - ADAM_B1) * g
    vn = ADAM_B2 * v + (1.0 - ADAM_B2) * (g * g)
    m_hat = mn / (1.0 - ADAM_B1 ** ADAM_STEP)
    v_hat = vn / (1.0 - ADAM_B2 ** ADAM_STEP)
    return -ADAM_LR * (m_hat / (jnp.sqrt(v_hat) + ADAM_EPS) + ADAM_WD * w), mn, vn


def _cast_into_slots(pieces, slot, name, rider=None):
    c = pieces[0][0].shape[1]
    rb = max(b for b in range(16, 257, 16) if all(cnt % b == 0 and st % b == 0 for _, st, cnt in pieces))
    nbs = [cnt // rb for _, _, cnt in pieces]
    starts = [sum(nbs[:s]) for s in range(len(pieces))]

    def body(s_ref, *refs):
        i = pl.program_id(0)
        for s in range(len(pieces)):
            @pl.when(jnp.logical_and(i >= starts[s], i < starts[s] + nbs[s]))
            def _():
                refs[len(pieces) + s][...] = refs[s][...].astype(BF16)

    in_specs, out_specs = [], []
    for (_, first_row, _), nb, st in zip(pieces, nbs, starts):
        in_specs.append(pl.BlockSpec((rb, c), lambda i, s, nb=nb, st=st, f=first_row // rb: (f + jnp.clip(i - st, 0, nb - 1), 0)))
        out_specs.append(pl.BlockSpec((None, rb, c), lambda i, s, nb=nb, st=st: (s[0], jnp.clip(i - st, 0, nb - 1), 0)))
    return _hosted_call(
        body, [w for w, _, _ in pieces], name=name, grid=(sum(nbs),), prefetch=(slot,), in_specs=in_specs,
        out_specs=out_specs, out_shape=[jax.ShapeDtypeStruct((N_CHIPS, cnt, c), BF16) for _, _, cnt in pieces],
        sem=("arbitrary",), rider=rider)


def _cast_into_slot(w, slot, name):
    return _cast_into_slots([(w, 0, w.shape[0])], slot, name)[0][0]


def _adamw_halves(w, mine, theirs, m, v, core, name):
    r, c = w.shape
    r2 = r // 2
    rb = max(b for b in range(8, r2 + 1, 8) if r2 % b == 0 and b * c * 4 <= (1 << 21))
    nbh = r2 // rb

    def body(z_ref, w_ref, a_ref, b_ref, m_ref, v_ref, g_ref, d_ref, mo_ref, vo_ref):
        here = (pl.program_id(0) // nbh) == z_ref[0]
        gg = jnp.where(here, a_ref[...], b_ref[...])
        g_ref[...] = gg
        d_ref[...], mo_ref[...], vo_ref[...] = _adam_math(w_ref[...], gg, m_ref[...], v_ref[...])

    spec = pl.BlockSpec((rb, c), lambda i, z: (i, 0))
    a_spec = pl.BlockSpec((rb, c), lambda i, z: (jnp.clip(i - z[0] * nbh, 0, nbh - 1), 0))
    b_spec = pl.BlockSpec((rb, c), lambda i, z: (jnp.clip(i - (1 - z[0]) * nbh, 0, nbh - 1), 0))
    shp = jax.ShapeDtypeStruct((r, c), F32)
    return pl.pallas_call(
        body, name=name,
        grid_spec=pltpu.PrefetchScalarGridSpec(
            num_scalar_prefetch=1, grid=(r // rb,), in_specs=[spec, a_spec, b_spec, spec, spec], out_specs=[spec] * 4),
        out_shape=[shp] * 4,
        compiler_params=_params(("parallel",)),
    )(core, w, mine, theirs, m, v)


def _adamw(w, g, m, v, name, rider=None):
    r, c = w.shape
    rb = r
    for cand in (256, 128, 64, 32, 16, 8):
        if r % cand == 0 and cand * c * 4 <= (1 << 20):
            rb = cand
            break
    if r * c * 4 <= (1 << 20):
        rb = r

    def body(w_ref, g_ref, m_ref, v_ref, d_ref, mo_ref, vo_ref):
        d_ref[...], mo_ref[...], vo_ref[...] = _adam_math(w_ref[...], g_ref[...], m_ref[...], v_ref[...])

    spec = pl.BlockSpec((rb, c), lambda i: (i, 0))
    shp = jax.ShapeDtypeStruct((r, c), F32)
    return _hosted_call(
        body, (w, g, m, v), name=name, grid=(r // rb,), in_specs=[spec] * 4, out_specs=[spec] * 3, out_shape=[shp] * 3,
        sem=("parallel",), rider=rider)


def _decay_prep(dec):
    def body(d_ref, lg_ref, sg_ref):
        d = d_ref[...]
        lg_ref[...] = jnp.minimum(d, 0.0) - jnp.log(1.0 + jnp.exp(-jnp.abs(d)))
        sg_ref[...] = 1.0 / (1.0 + jnp.exp(d))

    shp = jax.ShapeDtypeStruct(dec.shape, F32)
    return pl.pallas_call(body, name="decay_prep", out_shape=[shp, shp])(dec)


def _mod_fwd(a_in, w_ada, b_sh):
    rows, d = a_in.shape
    n = w_ada.shape[1]
    bn = 512

    def body(a_ref, w_ref, b_ref, o_ref):
        a = a_ref[...]
        s = (a / (1.0 + jnp.exp(-a))).astype(BF16)
        o_ref[...] = _dot(s, w_ref[...].astype(BF16)) + b_ref[...]

    return pl.pallas_call(
        body, name="mod_fwd", grid=(n // bn,),
        in_specs=[_full((rows, d)), pl.BlockSpec((d, bn), lambda j: (0, j)), pl.BlockSpec((1, bn), lambda j: (0, j))],
        out_specs=pl.BlockSpec((rows, bn), lambda j: (0, j)),
        out_shape=jax.ShapeDtypeStruct((rows, n), F32),
        compiler_params=_params(("parallel",)),
    )(a_in, w_ada, b_sh)


def _mod_bwd(a_in, dm, w_ada):
    rows, d = a_in.shape
    n = w_ada.shape[1]
    bn = 512
    nb = n // bn

    def body(a_ref, dm_ref, w_ref, gw_ref, da_ref):
        j = pl.program_id(0)
        a = a_ref[...]
        s = (a / (1.0 + jnp.exp(-a))).astype(BF16)
        dmb = dm_ref[...].astype(BF16)
        gw_ref[...] = _dot_tn(s, dmb)
        part = _dot_nt(dmb, w_ref[...].astype(BF16))

        @pl.when(j == 0)
        def _():
            da_ref[...] = part

        @pl.when(j > 0)
        def _():
            da_ref[...] += part

    return pl.pallas_call(
        body, name="mod_bwd", grid=(nb,),
        in_specs=[_full((rows, d)), pl.BlockSpec((rows, bn), lambda j: (0, j)), pl.BlockSpec((d, bn), lambda j: (0, j))],
        out_specs=[pl.BlockSpec((d, bn), lambda j: (0, j)), _full((rows, d))],
        out_shape=[jax.ShapeDtypeStruct((d, n), F32), jax.ShapeDtypeStruct((rows, d), F32)],
        compiler_params=_params(("arbitrary",)),
    )(a_in, dm, w_ada)


def _pre_fwd(x2, ctx2, modv, g_attn, w_in, g_q, g_kv, w_uq, w_ukv, cos_t, sin_t, *, seq, tm, rider=None):
    t_lat, d = x2.shape
    t_ctx = ctx2.shape[0]
    nl, nc = t_lat // tm, t_ctx // tm
    n_all = t_lat + t_ctx
    tpe = seq // tm
    nex = t_lat // seq

    def body(x_ref, c_ref, mod_ref, g_ref, win_ref, gq_ref, gkv_ref, wuq_ref, wukv_ref, cos_ref, sin_ref,
             h_ref, pg_ref, rq_ref, rk_ref, rv_ref, nq_ref, nkv_ref, q_ref, k_ref, v_ref):
        i = pl.program_id(0)
        xt = jnp.where(i < nl, x_ref[...], c_ref[...])
        sh = mod_ref[0, 0:1, :]
        sc = mod_ref[0, 1:2, :]
        r = lax.rsqrt(jnp.mean(xt * xt, axis=-1, keepdims=True) + EPS)
        hb = ((xt * r) * g_ref[...] * (1.0 + sc) + sh).astype(BF16)
        h_ref[...] = hb
        p = _dot_nt(hb, win_ref[...])
        cos = cos_ref[...]
        sin = sin_ref[...]
        rq_ref[...] = _rope(p[:, 0:256], cos, sin).astype(BF16)
        rk_ref[...] = _rope(p[:, 256:512] * (RET_DK ** -0.5), cos, sin).astype(BF16)
        rv_ref[...] = p[:, 512:1024].astype(BF16)
        pg_ref[...] = p[:, 1024:2176]
        cq = p[:, 1536:1920]
        ckv = p[:, 1920:2176]
        nqb = (cq * lax.rsqrt(jnp.mean(cq * cq, axis=-1, keepdims=True) + EPS) * gq_ref[...]).astype(BF16)
        nkvb = (ckv * lax.rsqrt(jnp.mean(ckv * ckv, axis=-1, keepdims=True) + EPS) * gkv_ref[...]).astype(BF16)
        nq_ref[...] = nqb
        nkv_ref[...] = nkvb
        cos1 = cos[:, 0:LANES]
        sin1 = sin[:, 0:LANES]
        kpe = _rope(p[:, 2176:2304], cos1, sin1).astype(BF16)
        for hd in range(HEADS):
            o = hd * MLA_HEAD
            qh = _dot_nt(nqb, wuq_ref[hd]) * MLA_SCALE
            q_ref[:, o:o + 128] = qh[:, 0:128].astype(BF16)
            q_ref[:, o + 128:o + 256] = _rope(qh[:, 128:256], cos1, sin1).astype(BF16)
            kvh = _dot(nkvb, wukv_ref[hd])
            k_ref[:, o:o + 128] = kvh[:, 0:128].astype(BF16)
            k_ref[:, o + 128:o + 256] = kpe
            v_ref[:, hd * 128:(hd + 1) * 128] = kvh[:, 128:256].astype(BF16)

    def tile(width):
        return pl.BlockSpec((tm, width), lambda i: (i, 0))

    widths = (d, PG_COLS, 256, 256, 512, Q_LORA, KV_LORA, HEADS * MLA_HEAD, HEADS * MLA_HEAD, HEADS * 128)
    dtypes = (BF16, F32, BF16, BF16, BF16, BF16, BF16, BF16, BF16, BF16)
    tab = pl.BlockSpec((tm, 256), lambda i: (jnp.where(i < nl, i % tpe, tpe), 0))
    return _hosted_call(
        body, (x2, ctx2, modv, g_attn, w_in, g_q, g_kv, w_uq, w_ukv, cos_t, sin_t), name="pre_fwd", grid=(nl + nc,),
        in_specs=[
            pl.BlockSpec((tm, d), lambda i: (jnp.minimum(i, nl - 1), 0)),
            pl.BlockSpec((tm, d), lambda i: (jnp.maximum(i - nl, 0), 0)),
            pl.BlockSpec((1, 8, d), lambda i: (jnp.minimum(i // tpe, nex), 0, 0)),
            _full((1, d)), _full(w_in.shape), _full((1, Q_LORA)), _full((1, KV_LORA)),
            _full(w_uq.shape), _full(w_ukv.shape), tab, tab,
        ],
        out_specs=[tile(w) for w in widths],
        out_shape=[jax.ShapeDtypeStruct((n_all, w), dt) for w, dt in zip(widths, dtypes)],
        sem=("parallel",), rider=rider)


def _post(yret, ymla, x2, tgt2, modv, g_ffn, g_fin, w_out, w_ff1, w_ff2a, w_ff2b, *, seq, tm):
    t_lat, d = x2.shape
    nl = t_lat // tm
    tpe = seq // tm
    nex = t_lat // seq
    n_slab = w_ff1.shape[0]
    fs = w_ff1.shape[2]
    fh = w_ff2a.shape[1]

    def body(yr_ref, ym_ref, x_ref, t_ref, mod_ref, gf_ref, gl_ref, wo_ref, w1_ref, w2a_ref, w2b_ref,
             mix_ref, a_ref, du_ref, h2_ref, df_ref, dmo_ref, dmix_ref, dxm_ref, st_ref, ru_ref):
        i = pl.program_id(0)
        gt_a = mod_ref[0, 2:3, :]
        sh_f = mod_ref[0, 3:4, :]
        sc_f = mod_ref[0, 4:5, :]
        gt_f = mod_ref[0, 5:6, :]
        g_ffn_v = gf_ref[...]
        g_fin_v = gl_ref[...]
        yr = yr_ref[...]
        ym = ym_ref[...]
        mix_ref[:, 0:512] = yr
        mix_ref[:, 512:1024] = ym
        op = _dot(yr, wo_ref[0:512, :]) + _dot(ym, wo_ref[512:1024, :])
        x_mid = x_ref[...] + gt_a * op
        r2 = lax.rsqrt(jnp.mean(x_mid * x_mid, axis=-1, keepdims=True) + EPS)
        xh2 = x_mid * r2
        h2b = (xh2 * g_ffn_v * (1.0 + sc_f) + sh_f).astype(BF16)
        h2_ref[...] = h2b
        f = jnp.zeros((tm, d), F32)
        for s in range(n_slab):
            ru = jnp.maximum(_dot(h2b, w1_ref[s]), 0.0)
            ru_ref[:, s * fs:(s + 1) * fs] = ru
            ab = (ru * ru).astype(BF16)
            a_ref[:, s * fs:(s + 1) * fs] = ab
            f = f + _dot(ab[:, 0:fh], w2a_ref[s]) + _dot(ab[:, fh:fs], w2b_ref[s])
        x_out = x_mid + gt_f * f
        r3 = lax.rsqrt(jnp.mean(x_out * x_out, axis=-1, keepdims=True) + EPS)
        xh3 = x_out * r3
        err = xh3 * g_fin_v - t_ref[...]
        dy = err * (1.0 / d)
        dxh3 = dy * g_fin_v
        dx_out = r3 * (dxh3 - xh3 * jnp.mean(dxh3 * xh3, axis=-1, keepdims=True))
        dfb = (dx_out * gt_f).astype(BF16)
        df_ref[...] = dfb
        dh2 = jnp.zeros((tm, d), F32)
        for s in range(n_slab):
            da = jnp.concatenate([_dot_nt(dfb, w2a_ref[s]), _dot_nt(dfb, w2b_ref[s])], axis=1)
            dub = (da * (2.0 * ru_ref[:, s * fs:(s + 1) * fs])).astype(BF16)
            du_ref[:, s * fs:(s + 1) * fs] = dub
            dh2 = dh2 + _dot_nt(dub, w1_ref[s])
        dxh2 = dh2 * (1.0 + sc_f) * g_ffn_v
        dx_mid = dx_out + r2 * (dxh2 - xh2 * jnp.mean(dxh2 * xh2, axis=-1, keepdims=True))
        dxm_ref[...] = dx_mid
        dmob = (dx_mid * gt_a).astype(BF16)
        dmo_ref[...] = dmob
        dmix_ref[...] = _dot_nt(dmob, wo_ref[...]).astype(BF16)

        def rsum(v):
            return jnp.sum(v, axis=0, keepdims=True)

        stats = jnp.concatenate([
            rsum(dh2), rsum(dh2 * xh2 * g_ffn_v), rsum(dx_out * f), rsum(dx_mid * op),
            rsum(dh2 * (1.0 + sc_f) * xh2), rsum(dy * xh3), rsum(err * err), jnp.zeros((1, d), F32)], axis=0)

        @pl.when(i % tpe == 0)
        def _():
            st_ref[0] = stats

        @pl.when(i % tpe != 0)
        def _():
            st_ref[0] += stats

    def tile(width):
        return pl.BlockSpec((tm, width), lambda i: (i, 0))

    widths = (d, D_FF, D_FF, d, d, d, d, d)
    dtypes = (BF16, BF16, BF16, BF16, BF16, BF16, BF16, F32)
    const = pl.Buffered(1)
    return pl.pallas_call(
        body, name="post", grid=(nl,),
        in_specs=[
            tile(512), tile(512), tile(d), tile(d),
            pl.BlockSpec((1, 8, d), lambda i: (i // tpe, 0, 0)),
            _full((1, d)), _full((1, d)),
            pl.BlockSpec(w_out.shape, lambda i: (0, 0), pipeline_mode=const),
            pl.BlockSpec(w_ff1.shape, lambda i: (0, 0, 0), pipeline_mode=const),
            pl.BlockSpec(w_ff2a.shape, lambda i: (0, 0, 0), pipeline_mode=const),
            pl.BlockSpec(w_ff2b.shape, lambda i: (0, 0, 0), pipeline_mode=const),
        ],
        out_specs=[tile(w) for w in widths] + [pl.BlockSpec((1, 8, d), lambda i: (i // tpe, 0, 0))],
        out_shape=[jax.ShapeDtypeStruct((t_lat, w), dt) for w, dt in zip(widths, dtypes)]
        + [jax.ShapeDtypeStruct((nex, 8, d), F32)],
        scratch_shapes=[pltpu.VMEM((tm, D_FF), F32)],
        compiler_params=_params(("arbitrary",), VMEM_LIMIT),
    )(yret, ymla, x2, tgt2, modv, g_ffn, g_fin, w_out, w_ff1, w_ff2a, w_ff2b)


def _pre_bwd(x2, ctx2, modv, g_attn, pg, drq, drk, dkc_r, drv, dvc_r, drg, dq_m, dkl, dkc, dvl, dvc, dxm,
             w_in, g_q, g_kv, w_uq, w_ukv, cos_t, sin_t, *, seq, tm, rider=None):
    t_lat, d = x2.shape
    t_ctx = ctx2.shape[0]
    nl, nc = t_lat // tm, t_ctx // tm
    n_all = t_lat + t_ctx
    tpe = seq // tm
    nex = t_lat // seq

    def body(x_ref, c_ref, mod_ref, g_ref, pg_ref, drq_ref, drk_ref, dkcr_ref, drv_ref, dvcr_ref, drg_ref,
             dq_ref, dkl_ref, dkc_ref, dvl_ref, dvc_ref, dxm_ref, win_ref, gq_ref, gkv_ref, wuq_ref, wukv_ref,
             cos_ref, sin_ref, dpb_ref, dqf_ref, dkvf_ref, gx_ref, st_ref):
        i = pl.program_id(0)
        lat = i < nl
        latf = lat.astype(F32)
        cos = cos_ref[...]
        sin = sin_ref[...]
        cos1 = cos[:, 0:LANES]
        sin1 = sin[:, 0:LANES]
        d_rq = _rope_t(drq_ref[...] * latf, cos, sin)
        d_rk = _rope_t(jnp.where(lat, drk_ref[...], dkcr_ref[...]), cos, sin) * (RET_DK ** -0.5)
        d_rv = jnp.where(lat, drv_ref[...], dvcr_ref[...])
        d_rg = drg_ref[...] * latf
        dq_all = dq_ref[...] * (latf * MLA_SCALE)
        dk_all = jnp.where(lat, dkl_ref[...], dkc_ref[...])
        dv_all = jnp.where(lat, dvl_ref[...], dvc_ref[...])
        dnq = jnp.zeros((tm, Q_LORA), F32)
        dnkv = jnp.zeros((tm, KV_LORA), F32)
        dkpe = jnp.zeros((tm, LANES), F32)
        for hd in range(HEADS):
            o = hd * MLA_HEAD
            dqh = jnp.concatenate([dq_all[:, o:o + 128], _rope_t(dq_all[:, o + 128:o + 256], cos1, sin1)],
                                  axis=1).astype(BF16)
            dqf_ref[:, o:o + 256] = dqh
            dnq = dnq + _dot(dqh, wuq_ref[hd])
            dkpe = dkpe + dk_all[:, o + 128:o + 256]
            dkvh = jnp.concatenate([dk_all[:, o:o + 128], dv_all[:, hd * 128:(hd + 1) * 128]], axis=1).astype(BF16)
            dkvf_ref[:, o:o + 256] = dkvh
            dnkv = dnkv + _dot_nt(dkvh, wukv_ref[hd])
        d_kpe = _rope_t(dkpe, cos1, sin1)
        pgv = pg_ref[...]
        cq = pgv[:, 512:896]
        ckv = pgv[:, 896:1152]
        rq_ = lax.rsqrt(jnp.mean(cq * cq, axis=-1, keepdims=True) + EPS)
        cqh = cq * rq_
        dcqh = dnq * gq_ref[...]
        d_cq = rq_ * (dcqh - cqh * jnp.mean(dcqh * cqh, axis=-1, keepdims=True))
        rkv_ = lax.rsqrt(jnp.mean(ckv * ckv, axis=-1, keepdims=True) + EPS)
        ckvh = ckv * rkv_
        dckvh = dnkv * gkv_ref[...]
        d_ckv = rkv_ * (dckvh - ckvh * jnp.mean(dckvh * ckvh, axis=-1, keepdims=True))
        dpb = jnp.concatenate([d_rq, d_rk, d_rv, d_rg, d_cq, d_ckv, d_kpe], axis=1).astype(BF16)
        dpb_ref[...] = dpb
        dh = _dot(dpb, win_ref[...])
        xt = jnp.where(lat, x_ref[...], c_ref[...])
        sc = mod_ref[0, 1:2, :]
        g = g_ref[...]
        r = lax.rsqrt(jnp.mean(xt * xt, axis=-1, keepdims=True) + EPS)
        xh = xt * r
        dxh = dh * (1.0 + sc) * g
        dx = r * (dxh - xh * jnp.mean(dxh * xh, axis=-1, keepdims=True))

        @pl.when(lat)
        def _():
            gx_ref[...] = dxm_ref[...] + dx

        def rsum(v):
            return jnp.sum(v, axis=0, keepdims=True)

        def widen(v):
            return jnp.concatenate([v, jnp.zeros((1, d - v.shape[1]), F32)], axis=1)

        stats = jnp.concatenate([
            rsum(dh), rsum(dh * xh * g), rsum(dh * (1.0 + sc) * xh), widen(rsum(dnq * cqh)), widen(rsum(dnkv * ckvh)),
            jnp.zeros((3, d), F32)], axis=0)
        first = jnp.logical_or(jnp.logical_and(lat, i % tpe == 0), i == nl)

        @pl.when(first)
        def _():
            st_ref[0] = stats

        @pl.when(jnp.logical_not(first))
        def _():
            st_ref[0] += stats

    def lat_tile(width):
        return pl.BlockSpec((tm, width), lambda i: (jnp.minimum(i, nl - 1), 0))

    def ctx_tile(width):
        return pl.BlockSpec((tm, width), lambda i: (jnp.maximum(i - nl, 0), 0))

    def tile(width):
        return pl.BlockSpec((tm, width), lambda i: (i, 0))

    tab = pl.BlockSpec((tm, 256), lambda i: (jnp.where(i < nl, i % tpe, tpe), 0))
    ex = pl.BlockSpec((1, 8, d), lambda i: (jnp.minimum(i // tpe, nex), 0, 0))
    return _hosted_call(
        body, (x2, ctx2, modv, g_attn, pg, drq, drk, dkc_r, drv, dvc_r, drg, dq_m, dkl, dkc, dvl, dvc, dxm,
               w_in, g_q, g_kv, w_uq, w_ukv, cos_t, sin_t), name="pre_bwd", grid=(nl + nc,),
        in_specs=[
            lat_tile(d), ctx_tile(d), ex, _full((1, d)), tile(PG_COLS),
            lat_tile(256), lat_tile(256), ctx_tile(256), lat_tile(512), ctx_tile(512), lat_tile(512),
            lat_tile(1024), lat_tile(1024), ctx_tile(1024), lat_tile(512), ctx_tile(512), lat_tile(d),
            _once(w_in.shape), _full((1, Q_LORA)), _full((1, KV_LORA)), _once(w_uq.shape), _once(w_ukv.shape),
            tab, tab,
        ],
        out_specs=[tile(IN_PAD), tile(1024), tile(1024), lat_tile(d), ex],
        out_shape=[
            jax.ShapeDtypeStruct((n_all, IN_PAD), BF16), jax.ShapeDtypeStruct((n_all, 1024), BF16),
            jax.ShapeDtypeStruct((n_all, 1024), BF16), jax.ShapeDtypeStruct((t_lat, d), F32),
            jax.ShapeDtypeStruct((nex + 1, 8, d), F32),
        ],
        sem=("arbitrary",), rider=rider)


MLA_SCALE = 1.0 / math.sqrt(MLA_NOPE + MLA_ROPE)
KEY_BLOCK = 1024


def _mla_specs(t_lat, seq, ctx_len, tq, heads=1):
    nqt = seq // tq
    cb = t_lat // ctx_len
    q = pl.BlockSpec((tq, heads * MLA_HEAD), lambda b, h, j: (b * nqt + j, h))
    kl = pl.BlockSpec((seq, heads * MLA_HEAD), lambda b, h, j: (b, h))
    kc = pl.BlockSpec((ctx_len, heads * MLA_HEAD), lambda b, h, j: (cb + b, h))
    vl = pl.BlockSpec((seq, heads * 128), lambda b, h, j: (b, h))
    vc = pl.BlockSpec((ctx_len, heads * 128), lambda b, h, j: (cb + b, h))
    o = pl.BlockSpec((tq, heads * 128), lambda b, h, j: (b * nqt + j, h))
    return q, kl, kc, vl, vc, o


FWD_HEADS = 2
BWD_HEADS = 1


def _mla_fwd(q, k, v, *, t_lat, seq, ctx_len, tq, rider=None):
    nex = t_lat // seq

    def body(q_ref, kl_ref, kc_ref, vl_ref, vc_ref, o_ref, lse_ref):
        for hh in range(FWD_HEADS):
            wide = slice(hh * MLA_HEAD, (hh + 1) * MLA_HEAD)
            cols = slice(hh * 128, (hh + 1) * 128)
            qb = q_ref[:, wide]
            s = _dot_nt(qb, kl_ref[:, wide])
            sc = _dot_nt(qb, kc_ref[:, wide])
            m = jnp.maximum(jnp.max(s, axis=-1, keepdims=True), jnp.max(sc, axis=-1, keepdims=True))
            p = jnp.exp(s - m)
            pc = jnp.exp(sc - m)
            total = jnp.sum(p, axis=-1, keepdims=True) + jnp.sum(pc, axis=-1, keepdims=True)
            o = _dot(p.astype(BF16), vl_ref[:, cols]) + _dot(pc.astype(BF16), vc_ref[:, cols])
            o_ref[:, cols] = (o * (1.0 / total)).astype(BF16)
            lse_ref[:, cols] = jnp.broadcast_to(m + jnp.log(total), (tq, 128))

    qs, kl, kc, vl, vc, os_ = _mla_specs(t_lat, seq, ctx_len, tq, FWD_HEADS)
    return _hosted_call(
        body, (q, k, k, v, v), name="mla_fwd", grid=(nex, HEADS // FWD_HEADS, seq // tq),
        in_specs=[qs, kl, kc, vl, vc], out_specs=[os_, os_],
        out_shape=[jax.ShapeDtypeStruct((t_lat, HEADS * 128), BF16), jax.ShapeDtypeStruct((t_lat, HEADS * 128), F32)],
        sem=("parallel", "parallel", "arbitrary"), rider=rider)


def _mla_bwd(q, k, v, ymla, lse, dmix, *, t_lat, seq, ctx_len, tq, rider=None):
    nex = t_lat // seq
    nqt = seq // tq
    t_ctx = nex * ctx_len
    kb = min(KEY_BLOCK, seq)

    def body(q_ref, kl_ref, kc_ref, vl_ref, vc_ref, o_ref, lse_ref, do_ref, dq_ref, dkl_out, dkc_out, dvl_out, dvc_out,
             dkl_ref, dkc_ref, dvl_ref, dvc_ref):
        j = pl.program_id(2)

        @pl.when(j == 0)
        def _():
            dkl_ref[...] = jnp.zeros(dkl_ref.shape, F32)
            dkc_ref[...] = jnp.zeros(dkc_ref.shape, F32)
            dvl_ref[...] = jnp.zeros(dvl_ref.shape, F32)
            dvc_ref[...] = jnp.zeros(dvc_ref.shape, F32)

        for hh in range(BWD_HEADS):
            wide = slice(hh * MLA_HEAD, (hh + 1) * MLA_HEAD)
            cols = slice(hh * 128, (hh + 1) * 128)
            qb = q_ref[:, wide]
            dob = do_ref[:, cols]
            delta = jnp.sum(dob.astype(F32) * o_ref[:, cols].astype(F32), axis=-1, keepdims=True)
            lse_row = lse_ref[:, hh * 128:hh * 128 + 1]

            def block(k_ref, v_ref, dk_ref, dv_ref, rows):
                kbl = k_ref[rows, wide]
                vbl = v_ref[rows, cols]
                p = jnp.exp(_dot_nt(qb, kbl) - lse_row)
                ds = (p * (_dot_nt(dob, vbl) - delta)).astype(BF16)
                dk_ref[rows, wide] += _dot_tn(ds, qb)
                dv_ref[rows, cols] += _dot_tn(p.astype(BF16), dob)
                return _dot(ds, kbl)

            dq = block(kc_ref, vc_ref, dkc_ref, dvc_ref, pl.ds(0, ctx_len))
            for i in range(seq // kb):
                dq = dq + block(kl_ref, vl_ref, dkl_ref, dvl_ref, pl.ds(i * kb, kb))
            dq_ref[:, wide] = dq.astype(BF16)

        @pl.when(j == nqt - 1)
        def _():
            dkl_out[...] = dkl_ref[...].astype(BF16)
            dkc_out[...] = dkc_ref[...].astype(BF16)
            dvl_out[...] = dvl_ref[...].astype(BF16)
            dvc_out[...] = dvc_ref[...].astype(BF16)

    g = BWD_HEADS
    qs, kl, kc, vl, vc, os_ = _mla_specs(t_lat, seq, ctx_len, tq, g)
    do_spec = pl.BlockSpec((tq, g * 128), lambda b, h, j: (b * nqt + j, HEADS // g + h))
    key_blocks = [(seq, g * MLA_HEAD), (ctx_len, g * MLA_HEAD), (seq, g * 128), (ctx_len, g * 128)]
    return _hosted_call(
        body, (q, k, k, v, v, ymla, lse, dmix), name="mla_bwd", grid=(nex, HEADS // g, nqt),
        in_specs=[qs, kl, kc, vl, vc, os_, os_, do_spec],
        out_specs=[qs] + [pl.BlockSpec(blk, lambda b, h, j: (b, h)) for blk in key_blocks],
        out_shape=[
            jax.ShapeDtypeStruct((t_lat, HEADS * MLA_HEAD), BF16),
            jax.ShapeDtypeStruct((t_lat, HEADS * MLA_HEAD), BF16),
            jax.ShapeDtypeStruct((t_ctx, HEADS * MLA_HEAD), BF16),
            jax.ShapeDtypeStruct((t_lat, HEADS * 128), BF16),
            jax.ShapeDtypeStruct((t_ctx, HEADS * 128), BF16),
        ],
        scratch_shapes=[pltpu.VMEM(blk, F32) for blk in key_blocks],
        sem=("parallel", "parallel", "arbitrary"), rider=rider)


def _decay_terms(lg, chunk, forward):
    ii = lax.broadcasted_iota(jnp.int32, (chunk, chunk), 0)
    jj = lax.broadcasted_iota(jnp.int32, (chunk, chunk), 1)
    diff = (ii - jj) if forward else (jj - ii)
    dist = jnp.maximum(diff, 0).astype(F32)
    dmat = jnp.where(diff >= 0, jnp.exp(lg * dist), 0.0)
    pos = lax.broadcasted_iota(jnp.int32, (chunk, 1), 0).astype(F32)
    if forward:
        e_q = pos + 1.0
        e_k = (chunk - 1.0) - pos
    else:
        e_q = chunk - pos
        e_k = pos
    wq = jnp.exp(lg * e_q)
    wk = jnp.exp(lg * e_k)
    cd = jnp.exp(jnp.full((1, 1), lg * chunk, F32))
    return dmat, dist, wq, wk, e_q, e_k, cd


def _ctx_weights(lg, ctx_len, forward):
    pos = lax.broadcasted_iota(jnp.int32, (ctx_len, 1), 0).astype(F32)
    e = ((ctx_len - 1.0) - pos) if forward else pos
    return jnp.exp(lg * e), e


def _pair_specs(t_lat, seq, ctx_len):
    cb = t_lat // ctx_len
    qk = pl.BlockSpec((seq, 128), lambda b, p: (b, p))
    v = pl.BlockSpec((seq, 256), lambda b, p: (b, p))
    kc = pl.BlockSpec((ctx_len, 128), lambda b, p: (cb + b, p))
    vc = pl.BlockSpec((ctx_len, 256), lambda b, p: (cb + b, p))
    return qk, v, kc, vc


def _lane_masks():
    lane = lax.broadcasted_iota(jnp.int32, (1, 128), 1)
    return [(lane // RET_DK) == hh for hh in (0, 1)]


def _ret_fwd_pair(rq, rk, rv, pg, lg, g_ret, *, t_lat, seq, ctx_len, chunk, rider=None):
    nex = t_lat // seq
    n_chunk = seq // chunk

    def body(q_ref, k_ref, v_ref, kc_ref, vc_ref, rg_ref, lg_ref, g_ref, y_ref, o_ref):
        pair = pl.program_id(1)
        masks = _lane_masks()
        kcf = kc_ref[...].astype(F32)
        chains = [(forward, hh) for forward in (True, False) for hh in (0, 1)]
        terms, s0 = [], []
        for forward, hh in chains:
            lgd = lg_ref[0 if forward else 1, 2 * pair + hh]
            terms.append(_decay_terms(lgd, chunk, forward))
            wc, _ = _ctx_weights(lgd, ctx_len, forward)
            s0.append(_dot_tn((jnp.where(masks[hh], kcf, 0.0) * wc).astype(BF16), vc_ref[:, hh * 128:(hh + 1) * 128]))
        both = [terms[hh][0] + terms[2 + hh][0] for hh in (0, 1)]
        o_ref[...] = jnp.zeros(o_ref.shape, F32)

        def step(t, states):
            new = [None] * 4
            for forward in (True, False):
                n = t if forward else n_chunk - 1 - t
                sl = pl.ds(pl.multiple_of(n * chunk, chunk), chunk)
                qb = q_ref[sl, :]
                kf_all = k_ref[sl, :].astype(F32)
                for hh in (0, 1):
                    c = (0 if forward else 2) + hh
                    _, _, wq, wk, _, _, cd = terms[c]
                    cols = slice(hh * 128, (hh + 1) * 128)
                    qm = jnp.where(masks[hh], qb, jnp.zeros((), BF16))
                    kf = jnp.where(masks[hh], kf_all, 0.0)
                    vb = v_ref[sl, cols]
                    o = wq * _dot(qm, states[c].astype(BF16))
                    if forward:
                        o = o + _dot((_dot_nt(qm, kf.astype(BF16)) * both[hh]).astype(BF16), vb)
                    o_ref[sl, cols] += o
                    new[c] = cd * states[c] + _dot_tn((kf * wk).astype(BF16), vb)
            return tuple(new)

        lax.fori_loop(0, n_chunk, step, tuple(s0))

        def norm_step(n, carry):
            sl = pl.ds(pl.multiple_of(n * chunk, chunk), chunk)
            for hh in (0, 1):
                cols = slice(hh * 128, (hh + 1) * 128)
                o = o_ref[sl, cols]
                mu = jnp.mean(o, axis=-1, keepdims=True)
                oc = o - mu
                var = jnp.mean(oc * oc, axis=-1, keepdims=True)
                rg = rg_ref[sl, cols]
                y_ref[sl, cols] = (oc * lax.rsqrt(var + EPS) * g_ref[:, cols] * (rg / (1.0 + jnp.exp(-rg)))).astype(BF16)
            return carry

        lax.fori_loop(0, n_chunk, norm_step, 0)

    qk, v, kc, vc = _pair_specs(t_lat, seq, ctx_len)
    return _hosted_call(
        body, (rq, rk, rv, rk, rv, pg, lg, g_ret), name="ret_fwd", grid=(nex, HEADS // 2),
        in_specs=[qk, qk, v, kc, vc, v, pl.BlockSpec(memory_space=pltpu.SMEM), pl.BlockSpec((1, 256), lambda b, p: (0, p))],
        out_specs=[v, v],
        out_shape=[jax.ShapeDtypeStruct((t_lat, HEADS * RET_DV), BF16), jax.ShapeDtypeStruct((t_lat, HEADS * RET_DV), F32)],
        sem=("parallel", "arbitrary"), rider=rider)


def _ret_bwd_pair(rq, rk, rv, pg, osum, dmix, lg, g_ret, *, t_lat, seq, ctx_len, chunk, rider=None):
    nex = t_lat // seq
    n_chunk = seq // chunk
    t_ctx = nex * ctx_len

    def body(q_ref, k_ref, v_ref, kc_ref, vc_ref, rg_ref, o_ref, dy_ref, lg_ref, g_ref,
             dq_out, dk_out, dv_out, dkc_ref, dvc_ref, drg_ref, st_ref, do_s, s_st, dq_ref, dk_ref, dv_ref):
        pair = pl.program_id(1)
        masks = _lane_masks()
        kcf = kc_ref[...].astype(F32)

        def norm_step(n, dgains):
            sl = pl.ds(pl.multiple_of(n * chunk, chunk), chunk)
            out = []
            for hh in (0, 1):
                cols = slice(hh * 128, (hh + 1) * 128)
                gain = g_ref[:, cols]
                o = o_ref[sl, cols]
                mu = jnp.mean(o, axis=-1, keepdims=True)
                oc = o - mu
                rstd = lax.rsqrt(jnp.mean(oc * oc, axis=-1, keepdims=True) + EPS)
                ohat = oc * rstd
                rg = rg_ref[sl, cols]
                sg = 1.0 / (1.0 + jnp.exp(-rg))
                dy = dy_ref[sl, cols].astype(F32)
                don = dy * (rg * sg)
                drg_ref[sl, cols] = (dy * (ohat * gain) * (sg * (1.0 + rg * (1.0 - sg)))).astype(BF16)
                dohat = don * gain
                do_s[sl, cols] = rstd * (dohat - jnp.mean(dohat, axis=-1, keepdims=True)
                                         - ohat * jnp.mean(dohat * ohat, axis=-1, keepdims=True))
                out.append(dgains[hh] + jnp.sum(don * ohat, axis=0, keepdims=True))
            return tuple(out)

        zero_row = jnp.zeros((1, 128), F32)
        dgains = lax.fori_loop(0, n_chunk, norm_step, (zero_row, zero_row))
        dq_ref[...] = jnp.zeros(dq_ref.shape, F32)
        dk_ref[...] = jnp.zeros(dk_ref.shape, F32)
        dv_ref[...] = jnp.zeros(dv_ref.shape, F32)

        chains = [(forward, hh) for forward in (True, False) for hh in (0, 1)]
        terms, ctxw, s0 = [], [], []
        for forward, hh in chains:
            lgd = lg_ref[0 if forward else 1, 2 * pair + hh]
            terms.append(_decay_terms(lgd, chunk, forward))
            ctxw.append(_ctx_weights(lgd, ctx_len, forward))
            s0.append(_dot_tn((jnp.where(masks[hh], kcf, 0.0) * ctxw[-1][0]).astype(BF16), vc_ref[:, hh * 128:(hh + 1) * 128]))

        def chunk_at(t, ascending):
            n = t if ascending else n_chunk - 1 - t
            return n, pl.ds(pl.multiple_of(n * chunk, chunk), chunk)

        def state_step(t, states):
            new = []
            for c, (forward, hh) in enumerate(chains):
                n, sl = chunk_at(t, forward)
                wk, cd = terms[c][3], terms[c][6]
                s_st[c, n] = states[c]
                kf = jnp.where(masks[hh], k_ref[sl, :].astype(F32), 0.0)
                new.append(cd * states[c] + _dot_tn((kf * wk).astype(BF16), v_ref[sl, hh * 128:(hh + 1) * 128]))
            return tuple(new)

        lax.fori_loop(0, n_chunk, state_step, tuple(s0))

        both = [terms[hh][0] + terms[2 + hh][0] for hh in (0, 1)]

        def grad_step(t, carry):
            out = [None] * len(chains)
            in_chunk_b = [None, None]
            for forward in (True, False):
                n, sl = chunk_at(t, not forward)
                qb = q_ref[sl, :]
                kf_all = k_ref[sl, :].astype(F32)
                dq_sum = jnp.zeros((chunk, 128), F32)
                dk_sum = jnp.zeros((chunk, 128), F32)
                for hh in (0, 1):
                    c = (0 if forward else 2) + hh
                    g_next, dlg = carry[c]
                    dmat, dist, wq, wk, e_q, e_k, cd = terms[c]
                    cols = slice(hh * 128, (hh + 1) * 128)
                    qm = jnp.where(masks[hh], qb, jnp.zeros((), BF16))
                    kf = jnp.where(masks[hh], kf_all, 0.0)
                    kb = kf.astype(BF16)
                    vb = v_ref[sl, cols]
                    do = do_s[sl, cols]
                    dob = do.astype(BF16)
                    s_n = s_st[c, n]
                    s_nb = s_n.astype(BF16)
                    gb = g_next.astype(BF16)
                    dk_cross = wk * _dot_nt(vb, gb)
                    dv = _dot((kf * wk).astype(BF16), gb)
                    o_cross = wq * _dot(qm, s_nb)
                    dq_sum = dq_sum + wq * _dot_nt(dob, s_nb)
                    dk_sum = dk_sum + dk_cross
                    dlg = (dlg + chunk * cd * jnp.sum(g_next * s_n, keepdims=True)
                           + jnp.sum(e_k * jnp.sum(kf * dk_cross, axis=-1, keepdims=True), keepdims=True)
                           + jnp.sum(e_q * jnp.sum(o_cross * do, axis=-1, keepdims=True), keepdims=True))
                    if forward:
                        a_raw = _dot_nt(qm, kb)
                        da_raw = _dot_nt(dob, vb)
                        prod = a_raw * da_raw
                        dlg = dlg + jnp.sum(dist * dmat * prod, keepdims=True)
                        in_chunk_b[hh] = jnp.sum(terms[2 + hh][1] * terms[2 + hh][0] * prod, keepdims=True)
                        dab = (da_raw * both[hh]).astype(BF16)
                        dq_sum = dq_sum + _dot(dab, kb)
                        dk_sum = dk_sum + _dot_tn(dab, qm)
                        dv = dv + _dot_tn((a_raw * both[hh]).astype(BF16), dob)
                    else:
                        dlg = dlg + in_chunk_b[hh]
                    dv_ref[sl, cols] += dv
                    out[c] = (cd * g_next + _dot_tn((qm.astype(F32) * wq).astype(BF16), dob), dlg)
                dq_ref[sl, :] += dq_sum
                dk_ref[sl, :] += dk_sum
            return tuple(out)

        zero = (jnp.zeros((128, 128), F32), jnp.zeros((1, 1), F32))
        res = lax.fori_loop(0, n_chunk, grad_step, (zero,) * len(chains))
        dkc_sum = jnp.zeros((ctx_len, 128), F32)
        dvc = [jnp.zeros((ctx_len, 128), F32)] * 2
        dlgs = []
        for c, (forward, hh) in enumerate(chains):
            ds0, dlg = res[c]
            wc, e_c = ctxw[c]
            kcm = jnp.where(masks[hh], kcf, 0.0)
            ds0b = ds0.astype(BF16)
            dkc_part = wc * _dot_nt(vc_ref[:, hh * 128:(hh + 1) * 128], ds0b)
            dkc_sum = dkc_sum + dkc_part
            dvc[hh] = dvc[hh] + _dot((kcm * wc).astype(BF16), ds0b)
            dlgs.append(dlg + jnp.sum(e_c * jnp.sum(kcm * dkc_part, axis=-1, keepdims=True), keepdims=True))
        dq_out[...] = dq_ref[...].astype(BF16)
        dk_out[...] = dk_ref[...].astype(BF16)
        dv_out[...] = dv_ref[...].astype(BF16)
        dkc_ref[...] = dkc_sum
        for hh in (0, 1):
            cols = slice(hh * 128, (hh + 1) * 128)
            dvc_ref[:, cols] = dvc[hh]
            st_ref[0, :, cols] = jnp.concatenate([
                dgains[hh], jnp.broadcast_to(dlgs[hh], (1, 128)), jnp.broadcast_to(dlgs[2 + hh], (1, 128)),
                jnp.zeros((5, 128), F32)], axis=0)

    qk, v, kc, vc = _pair_specs(t_lat, seq, ctx_len)
    return _hosted_call(
        body, (rq, rk, rv, rk, rv, pg, osum, dmix, lg, g_ret), name="ret_bwd", grid=(nex, HEADS // 2),
        in_specs=[qk, qk, v, kc, vc, v, v, v, pl.BlockSpec(memory_space=pltpu.SMEM),
                  pl.BlockSpec((1, 256), lambda b, p: (0, p))],
        out_specs=[
            qk, qk, v,
            pl.BlockSpec((ctx_len, 128), lambda b, p: (b, p)),
            pl.BlockSpec((ctx_len, 256), lambda b, p: (b, p)),
            v,
            pl.BlockSpec((1, 8, 256), lambda b, p: (b, 0, p)),
        ],
        out_shape=[
            jax.ShapeDtypeStruct((t_lat, 256), BF16), jax.ShapeDtypeStruct((t_lat, 256), BF16),
            jax.ShapeDtypeStruct((t_lat, 512), BF16), jax.ShapeDtypeStruct((t_ctx, 256), F32),
            jax.ShapeDtypeStruct((t_ctx, 512), F32), jax.ShapeDtypeStruct((t_lat, 512), BF16),
            jax.ShapeDtypeStruct((nex, 8, 512), F32),
        ],
        scratch_shapes=[pltpu.VMEM((seq, 256), F32), pltpu.VMEM((4, n_chunk, 128, 128), F32),
                        pltpu.VMEM((seq, 128), F32), pltpu.VMEM((seq, 128), F32), pltpu.VMEM((seq, 256), F32)],
        sem=("parallel", "arbitrary"), rider=rider)


def _matmul_tn(a, b, *, bm, bn, bk, chip_major, name, out_dtype=F32, rider=None):
    tk, m = a.shape
    n = b.shape[1]
    slab = n // N_CHIPS
    per_block = bn // slab if chip_major else 1
    bk = max(c for c in range(LANES, min(bk, tk) + 1, LANES) if tk % c == 0)
    nk = tk // bk
    blk = (per_block, bm, slab) if chip_major else (bm, bn)

    def body(a_ref, b_ref, o_ref, acc_ref):
        k = pl.program_id(2)
        if chip_major:
            parts = [_dot_tn(a_ref[...], b_ref[:, s * slab:(s + 1) * slab]) for s in range(per_block)]
        else:
            parts = [_dot_tn(a_ref[...], b_ref[...])]

        @pl.when(k == 0)
        def _():
            for s, part in enumerate(parts):
                if chip_major:
                    acc_ref[s] = part
                else:
                    acc_ref[...] = part

        @pl.when(k > 0)
        def _():
            for s, part in enumerate(parts):
                if chip_major:
                    acc_ref[s] += part
                else:
                    acc_ref[...] += part

        @pl.when(k == nk - 1)
        def _():
            o_ref[...] = acc_ref[...].astype(out_dtype)

    if chip_major:
        out_spec = pl.BlockSpec(blk, lambda i, j, k: (j, i, 0))
        out_shape = jax.ShapeDtypeStruct((N_CHIPS, m, slab), out_dtype)
    else:
        out_spec = pl.BlockSpec(blk, lambda i, j, k: (i, j))
        out_shape = jax.ShapeDtypeStruct((m, n), out_dtype)
    (out,), carried = _hosted_call(
        body, (a, b), name=name, grid=(m // bm, n // bn, nk),
        in_specs=[pl.BlockSpec((bk, bm), lambda i, j, k: (k, i)), pl.BlockSpec((bk, bn), lambda i, j, k: (k, j))],
        out_specs=[out_spec], out_shape=[out_shape], scratch_shapes=[pltpu.VMEM(blk, F32)],
        sem=("parallel", "parallel", "arbitrary"), rider=rider)
    return out if rider is None else (out, carried)


_LATE = ("w_out", "w_ff1", "w_ff2")
_EARLY = ("w_in", "w_uq", "w_ukv")


def _local_step(x, ctx, tgt, modv, lg, g_attn, g_ffn, g_fin, g_ret, g_q, g_kv, w_in, w_uq, w_ukv, late, place=None,
                *, tm=256, tq=256, chunk=256):
    nex, seq, d = x.shape
    ctx_len = ctx.shape[1]
    t_lat = nex * seq
    tm = min(tm, seq)
    x2 = x.reshape(t_lat, d)
    ctx2 = ctx.reshape(nex * ctx_len, d)
    tgt2 = tgt.reshape(t_lat, d)
    tm_fwd = min(2 * tm, seq)
    cos_t, sin_t = _rope_tables(seq, tm)
    dims = dict(t_lat=t_lat, seq=seq, ctx_len=ctx_len)
    alone = place is None

    (hb, pg, rq, rk, rv, nq, nkv, q, k, v), crossed_a = _pre_fwd(
        x2, ctx2, modv, g_attn, w_in, g_q, g_kv, w_uq, w_ukv, *_rope_tables(seq, tm_fwd), seq=seq, tm=tm_fwd,
        rider=None if alone else _gather_ici_rider([late[2]]))
    (yret, osum), got = _ret_fwd_pair(
        rq, rk, rv, pg, lg, g_ret, chunk=min(2 * chunk, seq), **dims,
        rider=None if alone else _merge_riders(_gather_d2d_rider(crossed_a), _gather_ici_rider([late[3]])))
    (ymla, lse), got_rest = _mla_fwd(
        q, k, v, tq=tq, **dims,
        rider=None if alone else _merge_riders(_gather_rider([late[0], late[1]], staged=True), _gather_d2d_rider(got[1:])))
    w_out, w_ff1, w_ff2a, w_ff2b = late if alone else (got_rest[0], got_rest[1], got[0], got_rest[2])
    mix, act, du, h2, df, dmo, dmix, dxm, st_post = _post(yret, ymla, x2, tgt2, modv, g_ffn, g_fin, w_out.reshape(d, d),
                                                         w_ff1, w_ff2a, w_ff2b, seq=seq, tm=min(tm, 256))
    kw = dict(bm=1024, bn=1024, bk=2048, out_dtype=BF16)
    g_ff2 = _matmul_tn(act, df, chip_major=False, name="gw_ff2", **kw).reshape(N_CHIPS, D_FF // N_CHIPS, d)
    if alone:
        g_ff1 = _matmul_tn(h2, du, chip_major=True, name="gw_ff1", **kw)
        g_out = _matmul_tn(mix, dmo, chip_major=False, name="gw_out", **kw).reshape(N_CHIPS, d // N_CHIPS, d)
        (dq_m, dkl, dkc, dvl, dvc), _ = _mla_bwd(q, k, v, ymla, lse, dmix, tq=tq, **dims)
        (drq, drk, drv, dkc_r, dvc_r, drg, st_ret), _ = _ret_bwd_pair(rq, rk, rv, pg, osum, dmix, lg, g_ret, chunk=chunk,
                                                                      **dims)
        late_out = [g_out, g_ff1, g_ff2]
    else:
        core, slot = place
        g_ff1, x_ff2 = _matmul_tn(h2, du, chip_major=True, name="gw_ff1", rider=_exchange_rider([g_ff2]), **kw)
        g_out, x_ff1 = _matmul_tn(mix, dmo, chip_major=False, name="gw_out", rider=_exchange_rider([g_ff1]), **kw)
        g_out = g_out.reshape(N_CHIPS, d // N_CHIPS, d)
        p_ff2 = _add_half(g_ff2, x_ff2[0], core, "add_half_w_ff2")
        p_ff1 = _add_half(g_ff1, x_ff1[0], core, "add_half_w_ff1")
        (dq_m, dkl, dkc, dvl, dvc), (l_ff2, l_ff1, x_out) = _mla_bwd(
            q, k, v, ymla, lse, dmix, tq=min(seq, 512), **dims,
            rider=_merge_riders(_scatter_rider([p_ff2, p_ff1]), _exchange_rider([g_out])))
        p_out = _add_half(g_out, x_out, core, "add_half_w_out")
        m_ff2 = _sum_chips(p_ff2, l_ff2, slot, "sum_chips_w_ff2")
        m_ff1 = _sum_chips(p_ff1, l_ff1, slot, "sum_chips_w_ff1")
        (drq, drk, drv, dkc_r, dvc_r, drg, st_ret), (l_out,) = _ret_bwd_pair(
            rq, rk, rv, pg, osum, dmix, lg, g_ret, chunk=chunk, **dims, rider=_scatter_rider([p_out]))
        late_out = [_sum_chips(p_out, l_out, slot, "sum_chips_w_out"), m_ff1, m_ff2]
    (dpb, dqf, dkvf, gx, st_pre), _ = _pre_bwd(
        x2, ctx2, modv, g_attn, pg, drq, drk, dkc_r, drv, dvc_r, drg, dq_m, dkl, dkc, dvl, dvc, dxm, w_in, g_q, g_kv,
        w_uq, w_ukv, cos_t, sin_t, seq=seq, tm=tm)
    g_early = [
        _matmul_tn(dpb, hb, bm=IN_PAD // 2, bn=d, bk=1536, chip_major=False, name="gw_in"),
        _matmul_tn(dqf, nq, bm=HEADS * MLA_HEAD, bn=Q_LORA, bk=1536, chip_major=False, name="gw_uq"),
        _matmul_tn(nkv, dkvf, bm=KV_LORA, bn=HEADS * 256, bk=1536, chip_major=True, name="gw_ukv"),
    ]
    return gx.reshape(nex, seq, d), g_early, late_out, st_post, st_ret, st_pre


_ANY = pl.BlockSpec(memory_space=pl.ANY)
_VMEM = pl.BlockSpec(memory_space=pltpu.VMEM)
_OFFSETS = tuple((dx, dy, dc) for dx in (0, 1) for dy in (0, 1) for dc in (0, 1))[1:]
_CHIP_OFFSETS = ((1, 0), (0, 1), (1, 1))


def _place():
    return lax.axis_index("x"), lax.axis_index("y"), lax.axis_index("c")


def _flip(v, d):
    return 1 - v if d else v


def _gather8_rider(a, in_vmem=True):
    def copies(a_ref, o_ref, send, recv):
        x, y, z = _place()
        me = 4 * x + 2 * y + z
        out = []
        for k, (dx, dy, dc) in enumerate(_OFFSETS):
            peer = (_flip(x, dx), _flip(y, dy), _flip(z, dc))
            landing = o_ref.at[4 * peer[0] + 2 * peer[1] + peer[2]]
            out.append((
                pltpu.make_async_remote_copy(src_ref=a_ref, dst_ref=o_ref.at[me], send_sem=send.at[k],
                                             recv_sem=recv.at[k], device_id=peer, device_id_type=MESH),
                pltpu.make_async_remote_copy(src_ref=a_ref, dst_ref=landing, send_sem=send.at[k],
                                             recv_sem=recv.at[k], device_id=peer, device_id_type=MESH)))
        return me, out

    def start(ins, outs, sems):
        me, cps = copies(ins[0], outs[0], sems[0], sems[1])
        pltpu.make_async_copy(ins[0], outs[0].at[me], sems[2]).start()
        for out_cp, _ in cps:
            out_cp.start()

    def finish(ins, outs, sems):
        me, cps = copies(ins[0], outs[0], sems[0], sems[1])
        for out_cp, in_cp in cps:
            in_cp.wait_recv()
            out_cp.wait_send()
        pltpu.make_async_copy(ins[0], outs[0].at[me], sems[2]).wait()

    spec = [_VMEM] if in_vmem else [_ANY]
    return _Rider([a], [jax.ShapeDtypeStruct((N_DEV,) + a.shape, a.dtype)],
                  [pltpu.SemaphoreType.DMA((7,)), pltpu.SemaphoreType.DMA((7,)), pltpu.SemaphoreType.DMA],
                  start, finish, in_specs=spec, out_specs=spec)


def _merge_riders(*riders):
    ins, outs, sems, in_specs, out_specs, aliases, cuts = [], [], [], [], [], {}, []
    for r in riders:
        cuts.append((len(ins), len(outs), len(sems)))
        aliases.update({len(ins) + i: len(outs) + j for i, j in r.aliases.items()})
        ins += r.ins
        outs += r.out_shapes
        sems += r.sems
        in_specs += r.in_specs
        out_specs += r.out_specs

    def part(r, cut, r_ins, r_outs, r_sems):
        return (r_ins[cut[0]:cut[0] + len(r.ins)], r_outs[cut[1]:cut[1] + len(r.out_shapes)],
                r_sems[cut[2]:cut[2] + len(r.sems)])

    def start(r_ins, r_outs, r_sems):
        for r, cut in zip(riders, cuts):
            r.start(*part(r, cut, r_ins, r_outs, r_sems))

    def finish(r_ins, r_outs, r_sems):
        for r, cut in zip(riders, cuts):
            r.finish(*part(r, cut, r_ins, r_outs, r_sems))

    def middle(r_ins, r_outs, r_sems):
        for r, cut in zip(riders, cuts):
            if r.middle is not None:
                r.middle(*part(r, cut, r_ins, r_outs, r_sems))

    return _Rider(ins, outs, sems, start, finish, aliases=aliases, in_specs=in_specs, out_specs=out_specs,
                  middle=middle if any(r.middle is not None for r in riders) else None)


def _allgather8(a, name):
    return _run_rider(_gather8_rider(a), name)[0]


BF16_TILE_ROWS = 16


def _half(o, slot, which):
    r2 = o.shape[1] // 2
    if r2 % BF16_TILE_ROWS == 0:
        return o.at[slot, pl.ds(which * r2, r2)]
    c2 = o.shape[2] // 2
    assert c2 % LANES == 0
    return o.at[slot, :, pl.ds(which * c2, c2)]


def _gather_send(o_refs, send, recv):
    x, y, z = _place()
    chip = 2 * x + y
    for a, o in enumerate(o_refs):
        r2 = o.shape[1] // 2
        mine = _half(o, chip, z)
        for k, (dx, dy) in enumerate(_CHIP_OFFSETS):
            pltpu.make_async_remote_copy(
                src_ref=mine, dst_ref=mine, send_sem=send.at[a, k], recv_sem=recv.at[a, k],
                device_id=(_flip(x, dx), _flip(y, dy), z), device_id_type=MESH).start()


def _gather_landed(o_refs, send, recv, then=None):
    x, y, z = _place()
    chip = 2 * x + y
    for a, o in enumerate(o_refs):
        for k, (dx, dy) in enumerate(_CHIP_OFFSETS):
            landed = _half(o, 2 * _flip(x, dx) + _flip(y, dy), z)
            pltpu.make_async_remote_copy(
                src_ref=landed, dst_ref=landed, send_sem=send.at[a, k], recv_sem=recv.at[a, k],
                device_id=(_flip(x, dx), _flip(y, dy), z), device_id_type=MESH).wait_recv()
            if then is not None:
                then(a, k, landed)
    for a, o in enumerate(o_refs):
        mine = _half(o, chip, z)
        for k, (dx, dy) in enumerate(_CHIP_OFFSETS):
            pltpu.make_async_remote_copy(
                src_ref=mine, dst_ref=mine, send_sem=send.at[a, k], recv_sem=recv.at[a, k],
                device_id=(_flip(x, dx), _flip(y, dy), z), device_id_type=MESH).wait_send()


def _pass_on(o_refs, fsend, frecv, a, k, landed):
    x, y, z = _place()
    pltpu.make_async_remote_copy(
        src_ref=landed, dst_ref=landed, send_sem=fsend.at[a, k], recv_sem=frecv.at[a, k],
        device_id=(x, y, 1 - z), device_id_type=MESH).start()


def _passed_on(o_refs, fsend, frecv):
    x, y, z = _place()
    for a, o in enumerate(o_refs):
        for k, (dx, dy) in enumerate(_CHIP_OFFSETS):
            other = 2 * _flip(x, dx) + _flip(y, dy)
            got = _half(o, other, 1 - z)
            gave = _half(o, other, z)
            pltpu.make_async_remote_copy(
                src_ref=got, dst_ref=got, send_sem=fsend.at[a, k], recv_sem=frecv.at[a, k],
                device_id=(x, y, 1 - z), device_id_type=MESH).wait_recv()
            pltpu.make_async_remote_copy(
                src_ref=gave, dst_ref=gave, send_sem=fsend.at[a, k], recv_sem=frecv.at[a, k],
                device_id=(x, y, 1 - z), device_id_type=MESH).wait_send()


def _gather_finish(o_refs, send, recv, fsend, frecv):
    _gather_landed(o_refs, send, recv, functools.partial(_pass_on, o_refs, fsend, frecv))
    _passed_on(o_refs, fsend, frecv)


class _Rider:
    def __init__(self, ins, out_shapes, sems, start, finish, aliases=None, in_specs=None, out_specs=None, middle=None):
        self.ins, self.out_shapes, self.sems = list(ins), list(out_shapes), list(sems)
        self.start, self.finish, self.aliases = start, finish, dict(aliases or {})
        self.middle = middle
        self.in_specs = list(in_specs) if in_specs else [_ANY] * len(self.ins)
        self.out_specs = list(out_specs) if out_specs else [_ANY] * len(self.out_shapes)


def _run_rider(rider, name):
    r_in, r_out = len(rider.ins), len(rider.out_shapes)

    def body(*refs):
        ins, outs, sems = refs[:r_in], refs[r_in:r_in + r_out], refs[r_in + r_out:]
        rider.start(ins, outs, sems)
        if rider.middle is not None:
            rider.middle(ins, outs, sems)
        rider.finish(ins, outs, sems)

    return pl.pallas_call(
        body, name=name, in_specs=rider.in_specs, out_specs=rider.out_specs, out_shape=rider.out_shapes,
        input_output_aliases=rider.aliases, scratch_shapes=rider.sems,
    )(*rider.ins)


def _hosted_call(body, args, *, name, grid, in_specs, out_specs, out_shape, scratch_shapes=(), sem, rider=None,
                 prefetch=()):
    scratch_shapes = list(scratch_shapes)
    n_pf, n_in, n_out, n_sc = len(prefetch), len(in_specs), len(out_specs), len(scratch_shapes)
    r_in, r_out = (len(rider.ins), len(rider.out_shapes)) if rider else (0, 0)
    last = tuple(g - 1 for g in grid)

    def hosted(*refs):
        p = 0
        parts = []
        for cnt in (n_pf, n_in, r_in, n_out, r_out, n_sc):
            parts.append(refs[p:p + cnt])
            p += cnt
        pf, ins, r_ins, outs, r_outs, scratch = parts
        sems = refs[p:]
        ids = [pl.program_id(a) for a in range(len(grid))]
        is_first = functools.reduce(jnp.logical_and, [i == 0 for i in ids])
        is_last = functools.reduce(jnp.logical_and, [i == e for i, e in zip(ids, last)])

        @pl.when(is_first)
        def _():
            rider.start(r_ins, r_outs, sems)

        if rider.middle is not None:
            linear = functools.reduce(lambda acc, ig: acc * ig[1] + ig[0], zip(ids, grid), 0)

            @pl.when(linear == math.prod(grid) * 3 // 4)
            def _():
                rider.middle(r_ins, r_outs, sems)

        body(*pf, *ins, *outs, *scratch)

        @pl.when(is_last)
        def _():
            rider.finish(r_ins, r_outs, sems)

    if rider is None:
        kern, all_in, all_out, shapes, scratch, aliases, extra = body, list(in_specs), list(out_specs), list(out_shape), \
            scratch_shapes, {}, []
    else:
        kern, all_in, all_out = hosted, list(in_specs) + rider.in_specs, list(out_specs) + rider.out_specs
        shapes, scratch, extra = list(out_shape) + rider.out_shapes, scratch_shapes + rider.sems, rider.ins
        aliases = {n_pf + n_in + i: n_out + j for i, j in rider.aliases.items()}
        sem = ("arbitrary",) * len(grid)
    if prefetch:
        spec = dict(grid_spec=pltpu.PrefetchScalarGridSpec(
            num_scalar_prefetch=n_pf, grid=grid, in_specs=all_in, out_specs=all_out, scratch_shapes=scratch))
    else:
        spec = dict(grid=grid, in_specs=all_in, out_specs=all_out, scratch_shapes=scratch)
    res = pl.pallas_call(kern, name=name, out_shape=shapes, input_output_aliases=aliases,
                         compiler_params=_params(sem, VMEM_LIMIT), **spec)(*prefetch, *args, *extra)
    return list(res[:n_out]), list(res[n_out:])


def _gather_rider(ws, staged=False):
    n = len(ws)
    shapes = [jax.ShapeDtypeStruct(w.shape, w.dtype) for w in ws]
    sems = [pltpu.SemaphoreType.DMA((n, 3))] * 4
    aliases = {a: a for a in range(n)}

    def start(ins, outs, s):
        _gather_send(outs, s[0], s[1])

    if not staged:
        return _Rider(ws, shapes, sems, start, lambda ins, outs, s: _gather_finish(outs, *s), aliases=aliases)
    return _Rider(
        ws, shapes, sems, start, lambda ins, outs, s: _passed_on(outs, s[2], s[3]), aliases=aliases,
        middle=lambda ins, outs, s: _gather_landed(outs, s[0], s[1], functools.partial(_pass_on, outs, s[2], s[3])))


def _gather_ici_rider(ws):
    n = len(ws)
    return _Rider(
        ws, [jax.ShapeDtypeStruct(w.shape, w.dtype) for w in ws], [pltpu.SemaphoreType.DMA((n, 3))] * 2,
        lambda ins, outs, sems: _gather_send(outs, sems[0], sems[1]),
        lambda ins, outs, sems: _gather_landed(outs, sems[0], sems[1]),
        aliases={a: a for a in range(n)})


def _gather_d2d_rider(ws):
    n = len(ws)

    def start(ins, outs, sems):
        x, y, z = _place()
        for a, o in enumerate(outs):
            for k, (dx, dy) in enumerate(_CHIP_OFFSETS):
                _pass_on(outs, sems[0], sems[1], a, k, _half(o, 2 * _flip(x, dx) + _flip(y, dy), z))

    return _Rider(
        ws, [jax.ShapeDtypeStruct(w.shape, w.dtype) for w in ws], [pltpu.SemaphoreType.DMA((n, 3))] * 2,
        start, lambda ins, outs, sems: _passed_on(outs, sems[0], sems[1]), aliases={a: a for a in range(n)})


def _copies_rider(ins, out_shapes, sem_shape, make):
    def start(r_ins, r_outs, sems):
        for cp in make(r_ins, r_outs, sems[0], sems[1]):
            cp.start()

    def finish(r_ins, r_outs, sems):
        for cp in make(r_ins, r_outs, sems[0], sems[1]):
            cp.wait()

    return _Rider(ins, out_shapes, [pltpu.SemaphoreType.DMA(sem_shape)] * 2, start, finish)


def _exchange_rider(gs):
    def make(g_refs, r_refs, send, recv):
        x, y, z = _place()
        return [pltpu.make_async_remote_copy(
            src_ref=g.at[:, pl.ds((1 - z) * (g.shape[1] // 2), g.shape[1] // 2)], dst_ref=r, send_sem=send.at[a],
            recv_sem=recv.at[a], device_id=(x, y, 1 - z), device_id_type=MESH)
            for a, (g, r) in enumerate(zip(g_refs, r_refs))]

    shapes = [jax.ShapeDtypeStruct((g.shape[0], g.shape[1] // 2, g.shape[2]), g.dtype) for g in gs]
    return _copies_rider(gs, shapes, (len(gs),), make)


def _add_half(g, recv, core, name):
    s, r, c = g.shape
    r2 = r // 2
    rb = r2
    for cand in (256, 128, 64):
        if r2 % cand == 0:
            rb = cand
            break
    g4 = g.reshape(s, 2, r2, c)

    def body(core_ref, g_ref, r_ref, o_ref):
        o_ref[...] = (g_ref[...].astype(F32) + r_ref[...].astype(F32)).astype(BF16)

    return pl.pallas_call(
        body, name=name,
        grid_spec=pltpu.PrefetchScalarGridSpec(
            num_scalar_prefetch=1, grid=(s, r2 // rb),
            in_specs=[pl.BlockSpec((None, None, rb, c), lambda i, j, cr: (i, cr[0], j, 0)),
                      pl.BlockSpec((None, rb, c), lambda i, j, cr: (i, j, 0))],
            out_specs=pl.BlockSpec((None, rb, c), lambda i, j, cr: (i, j, 0))),
        out_shape=jax.ShapeDtypeStruct((s, r2, c), BF16),
        compiler_params=_params(("parallel", "parallel")),
    )(core, g4, recv)


def _scatter_rider(ps):
    def make(p_refs, o_refs, send, recv):
        x, y, z = _place()
        copies = []
        for a, (p, o) in enumerate(zip(p_refs, o_refs)):
            for k, (dx, dy) in enumerate(_CHIP_OFFSETS):
                other = 2 * _flip(x, dx) + _flip(y, dy)
                copies.append(pltpu.make_async_remote_copy(
                    src_ref=p.at[other], dst_ref=o.at[k], send_sem=send.at[a, k], recv_sem=recv.at[a, k],
                    device_id=(_flip(x, dx), _flip(y, dy), z), device_id_type=MESH))
        return copies

    shapes = [jax.ShapeDtypeStruct((3,) + p.shape[1:], p.dtype) for p in ps]
    return _copies_rider(ps, shapes, (len(ps), 3), make)


def _sum_chips(p, landed, chip, name):
    _, r2, c = p.shape
    rb = r2
    for cand in (256, 128, 64):
        if r2 % cand == 0:
            rb = cand
            break

    def body(s_ref, p_ref, l_ref, o_ref):
        acc = p_ref[...].astype(F32)
        for k in range(3):
            acc = acc + l_ref[k].astype(F32)
        o_ref[...] = acc

    return pl.pallas_call(
        body, name=name,
        grid_spec=pltpu.PrefetchScalarGridSpec(
            num_scalar_prefetch=1, grid=(r2 // rb,),
            in_specs=[pl.BlockSpec((None, rb, c), lambda i, s: (s[0], i, 0)),
                      pl.BlockSpec((3, rb, c), lambda i, s: (0, i, 0))],
            out_specs=pl.BlockSpec((rb, c), lambda i, s: (i, 0))),
        out_shape=jax.ShapeDtypeStruct((r2, c), F32),
        compiler_params=_params(("parallel",)),
    )(chip, p, landed)


def _swap_rider(hs):
    def make(h_refs, o_refs, send, recv):
        x, y, z = _place()
        return [pltpu.make_async_remote_copy(
            src_ref=h, dst_ref=o, send_sem=send.at[a], recv_sem=recv.at[a], device_id=(x, y, 1 - z),
            device_id_type=MESH) for a, (h, o) in enumerate(zip(h_refs, o_refs))]

    return _copies_rider(hs, [jax.ShapeDtypeStruct(h.shape, h.dtype) for h in hs], (len(hs),), make)


def _reduce_scatter_vmem(gs, rows, rider, name):
    n = len(gs)
    r_in, r_out = len(rider.ins), len(rider.out_shapes)
    halves = [(r // 2, g.shape[-1]) for g, (r, _) in zip(gs, rows)]

    def body(*refs):
        p = 0
        parts = []
        for cnt in (n, r_in, n, n, r_out, n, n, n, 6):
            parts.append(refs[p:p + cnt])
            p += cnt
        g_refs, r_ins, mine, theirs, r_outs, recv, part, land, sems = parts
        r_sems = refs[p:]
        xs, xr, ss, sr, ws, wr = sems
        x, y, z = _place()
        chip = 2 * x + y
        sib = (x, y, 1 - z)
        rider.start(r_ins, r_outs, r_sems)

        def half_of(a, s, which):
            r2 = halves[a][0]
            if len(g_refs[a].shape) == 3:
                return g_refs[a].at[s, pl.ds(pl.multiple_of(which * r2, 8), r2)]
            return g_refs[a].at[pl.ds(pl.multiple_of(s * rows[a][1] + which * r2, 8), r2)]

        exchange = [pltpu.make_async_remote_copy(
            src_ref=half_of(a, s, 1 - z), dst_ref=recv[a].at[s], send_sem=xs.at[a, s], recv_sem=xr.at[a, s],
            device_id=sib, device_id_type=MESH) for a in range(n) for s in range(N_CHIPS)]
        for cp in exchange:
            cp.start()
        for cp in exchange:
            cp.wait()
        for a in range(n):
            for s in range(N_CHIPS):
                part[a][s] = (half_of(a, s, z)[...] + recv[a][s]).astype(BF16)
        scatter = []
        for a in range(n):
            for k, (dx, dy) in enumerate(_CHIP_OFFSETS):
                other = 2 * _flip(x, dx) + _flip(y, dy)
                scatter.append(pltpu.make_async_remote_copy(
                    src_ref=part[a].at[other], dst_ref=land[a].at[k], send_sem=ss.at[a, k], recv_sem=sr.at[a, k],
                    device_id=(_flip(x, dx), _flip(y, dy), z), device_id_type=MESH))
        for cp in scatter:
            cp.start()
        for cp in scatter:
            cp.wait()
        for a in range(n):
            acc = part[a][chip].astype(F32)
            for k in range(3):
                acc = acc + land[a][k].astype(F32)
            mine[a][...] = acc
        swap = [pltpu.make_async_remote_copy(
            src_ref=mine[a], dst_ref=theirs[a], send_sem=ws.at[a], recv_sem=wr.at[a], device_id=sib,
            device_id_type=MESH) for a in range(n)]
        for cp in swap:
            cp.start()
        for cp in swap:
            cp.wait()
        rider.finish(r_ins, r_outs, r_sems)

    half_shapes = [jax.ShapeDtypeStruct(h, F32) for h in halves]
    res = pl.pallas_call(
        body, name=name, in_specs=[_VMEM] * n + rider.in_specs, out_specs=[_VMEM] * (2 * n) + rider.out_specs,
        out_shape=half_shapes + half_shapes + rider.out_shapes,
        scratch_shapes=[pltpu.VMEM((N_CHIPS,) + h, F32) for h in halves] + [pltpu.VMEM((N_CHIPS,) + h, BF16) for h in halves]
        + [pltpu.VMEM((3,) + h, BF16) for h in halves]
        + [pltpu.SemaphoreType.DMA((n, N_CHIPS))] * 2 + [pltpu.SemaphoreType.DMA((n, 3))] * 2
        + [pltpu.SemaphoreType.DMA((n,))] * 2 + rider.sems,
        input_output_aliases={n + i: 2 * n + j for i, j in rider.aliases.items()},
        compiler_params=_params(None, VMEM_LIMIT),
    )(*gs, *rider.ins)
    return list(res[:n]), list(res[n:2 * n]), list(res[2 * n:])


SMALL_ROWS = 32
PACK_ROWS = 16


def _pack_small(st_post, st_ret, st_pre):
    d = st_post.shape[2]

    def body(po_ref, re_ref, pr_ref, o_ref):
        o_ref[...] = jnp.zeros(o_ref.shape, F32)
        o_ref[0:1, :] = pr_ref[0, 2:3, :] + pr_ref[1, 2:3, :] + pr_ref[2, 2:3, :]
        o_ref[1:2, :] = po_ref[0, 4:5, :] + po_ref[1, 4:5, :]
        o_ref[2:3, :] = po_ref[0, 5:6, :] + po_ref[1, 5:6, :]
        o_ref[3:4, 0:512] = re_ref[0, 0:1, :] + re_ref[1, 0:1, :]
        o_ref[4:5, :] = pr_ref[0, 3:4, :] + pr_ref[1, 3:4, :] + pr_ref[2, 3:4, :]
        o_ref[5:6, :] = pr_ref[0, 4:5, :] + pr_ref[1, 4:5, :] + pr_ref[2, 4:5, :]
        lane = lax.broadcasted_iota(jnp.int32, (1, LANES), 1)
        for row, src in ((6, 1), (10, 2)):
            acc = jnp.zeros((1, LANES), F32)
            for hd in range(HEADS):
                grp = re_ref[0, src:src + 1, hd * LANES:(hd + 1) * LANES] + re_ref[1, src:src + 1, hd * LANES:(hd + 1) * LANES]
                acc = acc + jnp.where(lane == hd, grp, 0.0)
            o_ref[row:row + 1, 0:LANES] = acc
        o_ref[7:8, :] = po_ref[0, 6:7, :] + po_ref[1, 6:7, :]
        o_ref[8:9, :] = pr_ref[2, 0:1, :]
        o_ref[9:10, :] = pr_ref[2, 1:2, :]
        for e in range(2):
            b = 12 + 6 * e
            o_ref[b:b + 1, :] = pr_ref[e, 0:1, :]
            o_ref[b + 1:b + 2, :] = pr_ref[e, 1:2, :]
            o_ref[b + 2:b + 3, :] = po_ref[e, 3:4, :]
            o_ref[b + 3:b + 4, :] = po_ref[e, 0:1, :]
            o_ref[b + 4:b + 5, :] = po_ref[e, 1:2, :]
            o_ref[b + 5:b + 6, :] = po_ref[e, 2:3, :]

    return pl.pallas_call(body, name="pack_small", out_shape=jax.ShapeDtypeStruct((SMALL_ROWS, d), F32))(st_post, st_ret, st_pre)


def _small_reduce(gathered):
    d = gathered.shape[2]

    def body(g_ref, o_ref):
        tot = g_ref[0, 0:PACK_ROWS, :]
        for dev in range(1, N_DEV):
            tot = tot + g_ref[dev, 0:PACK_ROWS, :]
        o_ref[0:PACK_ROWS, :] = tot
        for j in range(6):
            acc = g_ref[0, 12 + j:13 + j, :] + g_ref[0, 18 + j:19 + j, :]
            for dev in range(1, N_DEV):
                acc = acc + g_ref[dev, 12 + j:13 + j, :] + g_ref[dev, 18 + j:19 + j, :]
            if j < 2:
                acc = acc + o_ref[8 + j:9 + j, :]
            o_ref[PACK_ROWS + j:PACK_ROWS + j + 1, :] = acc
        o_ref[PACK_ROWS + 6:PACK_ROWS + 8, :] = jnp.zeros((2, d), F32)

    return pl.pallas_call(body, name="small_reduce", out_shape=jax.ShapeDtypeStruct((PACK_ROWS + 8, d), F32))(gathered)


_SMALL = (("g_attn", 0, 1024), ("g_ffn", 1, 1024), ("g_final", 2, 1024), ("g_ret", 3, 512), ("g_q_lora", 4, 384),
          ("g_kv_lora", 5, 256), ("ret_decay_fwd", 6, HEADS), ("ret_decay_bwd", 10, HEADS))
_SMALL_NAMES = tuple(s[0] for s in _SMALL) + ("c_ctx", "b_ada")


def _small_final(tot, dcc, sg8, ws, ms, vs):
    d = tot.shape[1]
    n = len(_SMALL_NAMES)

    def body(*refs):
        t_ref, dcc_ref, sg_ref = refs[0:3]
        w_refs, m_refs, v_refs = refs[3:3 + n], refs[3 + n:3 + 2 * n], refs[3 + 2 * n:3 + 3 * n]
        outs = refs[3 + 3 * n:]
        g_refs, d_refs, mo_refs, vo_refs = outs[0:n], outs[n:2 * n], outs[2 * n:3 * n], outs[3 * n:4 * n]
        l_ref = outs[4 * n]

        def update(i, g, sl=None):
            pick = (lambda r: r[...]) if sl is None else (lambda r: r[:, sl])
            dl, mn, vn = _adam_math(pick(w_refs[i]), g, pick(m_refs[i]), pick(v_refs[i]))
            if sl is None:
                g_refs[i][...], d_refs[i][...], mo_refs[i][...], vo_refs[i][...] = g, dl, mn, vn
            else:
                g_refs[i][:, sl], d_refs[i][:, sl], mo_refs[i][:, sl], vo_refs[i][:, sl] = g, dl, mn, vn

        for i, (name, row, width) in enumerate(_SMALL):
            g = t_ref[row:row + 1, 0:width]
            if name == "ret_decay_fwd":
                g = g * sg_ref[0:1, 0:width]
            elif name == "ret_decay_bwd":
                g = g * sg_ref[1:2, 0:width]
            update(i, g)
        i_cc, i_b = n - 2, n - 1
        cc = w_refs[i_cc][...]
        s = 1.0 / (1.0 + jnp.exp(-cc))
        dsilu = dcc_ref[0, 0:1, :] + dcc_ref[2, 0:1, :] + dcc_ref[4, 0:1, :] + dcc_ref[6, 0:1, :]
        update(i_cc, dsilu * (s * (1.0 + cc * (1.0 - s))))
        for j in range(6):
            update(i_b, t_ref[PACK_ROWS + j:PACK_ROWS + j + 1, :], pl.ds(j * d, d))
        l_ref[...] = jnp.broadcast_to((0.5 / d) * jnp.sum(t_ref[7:8, :], keepdims=True), l_ref.shape)

    shapes = [jax.ShapeDtypeStruct(a.shape, F32) for a in ws]
    outs = pl.pallas_call(
        body, name="small_final", out_shape=shapes * 4 + [jax.ShapeDtypeStruct((8, LANES), F32)],
    )(tot, dcc, sg8, *ws, *ms, *vs)
    return outs[0:n], outs[n:2 * n], outs[2 * n:3 * n], outs[3 * n:4 * n], outs[4 * n]


_WEIGHTS = ("c_ctx", "w_ada", "b_ada", "g_attn", "g_ffn", "w_in", "ret_decay_fwd", "ret_decay_bwd", "g_ret", "g_q_lora",
            "w_uq", "g_kv_lora", "w_ukv", "w_out", "w_ff1", "w_ff2", "g_final")
_BIG = ("w_in", "w_uq", "w_ukv", "w_out", "w_ff1", "w_ff2")
_TRANSPOSED = ("w_in", "w_uq")


def kernel(x, c, ctx, c_ctx, w_ada, b_ada, g_attn, g_ffn, w_in, ret_decay_fwd, ret_decay_bwd, g_ret, g_q_lora, w_uq, g_kv_lora, w_ukv, w_out, w_ff1, w_ff2, g_final, loss_target, m_c_ctx, m_w_ada, m_b_ada, m_g_attn, m_g_ffn, m_w_in, m_ret_decay_fwd, m_ret_decay_bwd, m_g_ret, m_g_q_lora, m_w_uq, m_g_kv_lora, m_w_ukv, m_w_out, m_w_ff1, m_w_ff2, m_g_final, v_c_ctx, v_w_ada, v_b_ada, v_g_attn, v_g_ffn, v_w_in, v_ret_decay_fwd, v_ret_decay_bwd, v_g_ret, v_g_q_lora, v_w_uq, v_g_kv_lora, v_w_ukv, v_w_out, v_w_ff1, v_w_ff2, v_g_final):
    w = dict(c_ctx=c_ctx, w_ada=w_ada, b_ada=b_ada, g_attn=g_attn, g_ffn=g_ffn, w_in=w_in, ret_decay_fwd=ret_decay_fwd,
             ret_decay_bwd=ret_decay_bwd, g_ret=g_ret, g_q_lora=g_q_lora, w_uq=w_uq, g_kv_lora=g_kv_lora, w_ukv=w_ukv,
             w_out=w_out, w_ff1=w_ff1, w_ff2=w_ff2, g_final=g_final)
    m = dict(c_ctx=m_c_ctx, w_ada=m_w_ada, b_ada=m_b_ada, g_attn=m_g_attn, g_ffn=m_g_ffn, w_in=m_w_in,
             ret_decay_fwd=m_ret_decay_fwd, ret_decay_bwd=m_ret_decay_bwd, g_ret=m_g_ret, g_q_lora=m_g_q_lora, w_uq=m_w_uq,
             g_kv_lora=m_g_kv_lora, w_ukv=m_w_ukv, w_out=m_w_out, w_ff1=m_w_ff1, w_ff2=m_w_ff2, g_final=m_g_final)
    v = dict(c_ctx=v_c_ctx, w_ada=v_w_ada, b_ada=v_b_ada, g_attn=v_g_attn, g_ffn=v_g_ffn, w_in=v_w_in,
             ret_decay_fwd=v_ret_decay_fwd, ret_decay_bwd=v_ret_decay_bwd, g_ret=v_g_ret, g_q_lora=v_g_q_lora, w_uq=v_w_uq,
             g_kv_lora=v_g_kv_lora, w_ukv=v_w_ukv, w_out=v_w_out, w_ff1=v_w_ff1, w_ff2=v_w_ff2, g_final=v_g_final)
    xi, yi, ci = lax.axis_index("x"), lax.axis_index("y"), lax.axis_index("c")
    chip = 2 * xi + yi
    dev = 2 * chip + ci
    nex, seq, d = x.shape
    n_ada = w_ada.shape[2]

    dec = jnp.zeros((8, LANES), F32).at[0, :HEADS].set(ret_decay_fwd[0]).at[1, :HEADS].set(ret_decay_bwd[0])
    lg8, sg8 = _decay_prep(dec)
    lg = lg8[:2, :HEADS]

    def shard_of(t, k):
        return t[k][0].T if k in _TRANSPOSED else t[k][0]

    shard = {k: shard_of(w, k) for k in _BIG}
    head_rows = MLA_NOPE + MLA_ROPE
    shard["w_uq"] = jnp.pad(shard["w_uq"], ((0, MLA_HEAD - head_rows), (0, 0)))
    slot = chip.reshape(1).astype(jnp.int32)
    core = ci.reshape(1).astype(jnp.int32)
    slots = {k: _cast_into_slot(shard[k], slot, "cast_" + k) for k in _EARLY}
    half_ff = shard["w_ff2"].shape[0] // 2
    late_pieces = [(shard["w_out"], 0, shard["w_out"].shape[0]), (shard["w_ff1"], 0, shard["w_ff1"].shape[0]),
                   (shard["w_ff2"], 0, half_ff), (shard["w_ff2"], half_ff, half_ff)]
    late_slots, (w_in_x, w_uq_x, w_ukv_x, c8) = _cast_into_slots(
        late_pieces, slot, "cast_late",
        rider=_merge_riders(_gather_ici_rider([slots[k] for k in _EARLY]),
                            _gather8_rider(jnp.pad(c, ((0, 8 - nex), (0, 0))), in_vmem=False)))

    a_in = jnp.concatenate([c8[:, :nex].reshape(N_DEV * nex, d), c_ctx.reshape(1, d), jnp.zeros((7, d), F32)], axis=0)
    b_sh = lax.dynamic_slice(b_ada, (0, chip * n_ada), (1, n_ada))
    mod_sh = _mod_fwd(a_in, w_ada[0], b_sh)
    mod8, w_in_f, w_uq_k, w_ukv_k = _run_rider(
        _merge_riders(_gather8_rider(mod_sh), _gather_d2d_rider([w_in_x, w_uq_x, w_ukv_x])), "ag_early")
    w_in_k = jnp.pad(w_in_f.reshape(IN_COLS, d), ((0, IN_PAD - IN_COLS), (0, 0)))
    mod_all = mod8[0::2].transpose(1, 0, 2).reshape(a_in.shape[0], N_CHIPS * n_ada)
    mod_me = lax.dynamic_slice(mod_all, (nex * dev, 0), (nex, N_CHIPS * n_ada)).reshape(nex, 6, d)
    mod_c = mod_all[N_DEV * nex].reshape(1, 6, d)
    modv = jnp.pad(jnp.concatenate([mod_me, mod_c], axis=0), ((0, 0), (0, 2), (0, 0)))

    gx, g_early, late, st_post, st_ret, st_pre = _local_step(
        x, ctx, loss_target, modv, lg, g_attn, g_ffn, g_final.reshape(1, d), g_ret, g_q_lora, g_kv_lora,
        w_in_k, w_uq_k, w_ukv_k, late_slots, (core, slot))

    mine, theirs, (*late_theirs, gathered) = _reduce_scatter_vmem(
        g_early, [(IN_COLS // N_CHIPS, IN_COLS // N_CHIPS), (head_rows, MLA_HEAD), (KV_LORA, KV_LORA)],
        _merge_riders(_swap_rider(late), _gather8_rider(_pack_small(st_post, st_ret, st_pre))), "rs_early")
    tot = _small_reduce(gathered)
    dm = jnp.concatenate([
        gathered[:, 12:24].reshape(N_DEV * nex, 6 * d),
        jnp.concatenate([tot[8:10].reshape(1, 2 * d), jnp.zeros((1, 4 * d), F32)], axis=1),
        jnp.zeros((7, 6 * d), F32)], axis=0)
    dm_sh = lax.dynamic_slice(dm, (0, chip * n_ada), (dm.shape[0], n_ada))
    g_ada, da = _mod_bwd(a_in, dm_sh, w_ada[0])
    dcc = _allgather8(da[N_DEV * nex:], "ag_dcc")
    halves = dict(zip(_EARLY, zip(mine, theirs)))
    halves.update(zip(_LATE, zip(late, late_theirs)))
    grad, delta, new_m, new_v = {}, {}, {}, {}
    for k in _BIG:
        a, b = halves[k]
        res = _adamw_halves(shard_of(w, k), a, b, shard_of(m, k), shard_of(v, k), core, "adamw_" + k)
        grad[k], delta[k], new_m[k], new_v[k] = [(o.T if k in _TRANSPOSED else o).reshape(w[k].shape) for o in res]

    shp = w_ada.shape
    outs, _ = _adamw(w_ada[0], g_ada, m["w_ada"][0], v["w_ada"][0], "adamw_w_ada")
    grad["w_ada"] = g_ada.reshape(shp)
    delta["w_ada"], new_m["w_ada"], new_v["w_ada"] = [o.reshape(shp) for o in outs]
    rows = [{k: t[k].reshape(1, -1) for k in _SMALL_NAMES} for t in (w, m, v)]
    small = _small_final(tot, dcc, sg8, *[[t[k] for k in _SMALL_NAMES] for t in rows])
    for res, outs in zip((grad, delta, new_m, new_v), small[:4]):
        for k, o in zip(_SMALL_NAMES, outs):
            res[k] = o.reshape(w[k].shape)
    return (small[4][0, 0], gx, *[grad[k] for k in _WEIGHTS], *[delta[k] for k in _WEIGHTS],
            *[new_m[k] for k in _WEIGHTS], *[new_v[k] for k in _WEIGHTS])
```

```python
import functools
import math

import jax
import jax.numpy as jnp
from jax import lax
from jax.experimental import pallas as pl
from jax.experimental.pallas import tpu as pltpu

F32 = jnp.float32
BF16 = jnp.bfloat16
MESH = pl.DeviceIdType.MESH

EPS = 1e-6
D_MODEL = 1024
D_FF = 4096
HEADS = 4
RET_DK = 64
RET_DV = 128
MLA_NOPE = 128
MLA_ROPE = 64
MLA_HEAD = 256
Q_LORA = 384
KV_LORA = 256
GRID_W = 64
ROPE_BASE = 10000.0
IN_COLS = 2240
IN_PAD = 2304
PG_COLS = 1152
N_CHIPS = 4
N_DEV = 8
LANES = 128
ADAM_LR = 0.001
ADAM_B1 = 0.9
ADAM_B2 = 0.999
ADAM_EPS = 1e-08
ADAM_WD = 0.01
ADAM_STEP = 10
VMEM_LIMIT = 56 * 1024 * 1024


def _dot(a, b):
    return jnp.dot(a, b, preferred_element_type=F32)


def _dot_nt(a, b):
    return lax.dot_general(a, b, (((1,), (1,)), ((), ())), preferred_element_type=F32)


def _dot_tn(a, b):
    return lax.dot_general(a, b, (((0,), (0,)), ((), ())), preferred_element_type=F32)


def _params(sem=None, vmem=None):
    return pltpu.CompilerParams(dimension_semantics=sem, vmem_limit_bytes=vmem)


def _full(shape):
    n = len(shape)
    return pl.BlockSpec(shape, lambda *_: (0,) * n)


def _once(shape):
    n = len(shape)
    return pl.BlockSpec(shape, lambda *_: (0,) * n, pipeline_mode=pl.Buffered(1))


def _rope(x, cos, sin):
    w = x.shape[-1]
    lo = (lax.broadcasted_iota(jnp.int32, (1, w), 1) % 64) < 32
    swapped = jnp.where(lo, pltpu.roll(x, w - 32, 1), pltpu.roll(x, 32, 1))
    return x * cos + swapped * sin


def _rope_t(g, cos, sin):
    w = g.shape[-1]
    lo = (lax.broadcasted_iota(jnp.int32, (1, w), 1) % 64) < 32
    t = g * sin
    swapped = jnp.where(lo, pltpu.roll(t, w - 32, 1), pltpu.roll(t, 32, 1))
    return g * cos + swapped


def _rope_tables(seq, tm):
    rows = seq // GRID_W
    row = jnp.repeat(jnp.arange(rows, dtype=F32), GRID_W)
    col = jnp.tile(jnp.arange(GRID_W, dtype=F32), rows)
    n_freq = RET_DK // 4
    freq = ROPE_BASE ** (-jnp.arange(n_freq, dtype=F32) / n_freq)
    ang = jnp.concatenate([row[:, None] * freq, col[:, None] * freq], axis=-1)
    cos, sin = jnp.cos(ang), jnp.sin(ang)
    cos_t = jnp.tile(jnp.concatenate([cos, cos], -1), (1, HEADS))
    sin_t = jnp.tile(jnp.concatenate([-sin, sin], -1), (1, HEADS))
    cos_t = jnp.concatenate([cos_t, jnp.ones((tm, 4 * RET_DK), F32)], 0)
    sin_t = jnp.concatenate([sin_t, jnp.zeros((tm, 4 * RET_DK), F32)], 0)
    return cos_t, sin_t


def _adam_math(w, g, m, v):
    mn = ADAM_B1 * m + (1.0 - ADAM_B1) * g
    vn = ADAM_B2 * v + (1.0 - ADAM_B2) * (g * g)
    m_hat = mn / (1.0 - ADAM_B1 ** ADAM_STEP)
    v_hat = vn / (1.0 - ADAM_B2 ** ADAM_STEP)
    return -ADAM_LR * (m_hat / (jnp.sqrt(v_hat) + ADAM_EPS) + ADAM_WD * w), mn, vn


def _cast_into_slots(pieces, slot, name, rider=None):
    c = pieces[0][0].shape[1]
    rb = max(b for b in range(16, 257, 16) if all(cnt % b == 0 and st % b == 0 for _, st, cnt in pieces))
    nbs = [cnt // rb for _, _, cnt in pieces]
    starts = [sum(nbs[:s]) for s in range(len(pieces))]

    def body(s_ref, *refs):
        i = pl.program_id(0)
        for s in range(len(pieces)):
            @pl.when(jnp.logical_and(i >= starts[s], i < starts[s] + nbs[s]))
            def _():
                refs[len(pieces) + s][...] = refs[s][...].astype(BF16)

    in_specs, out_specs = [], []
    for (_, first_row, _), nb, st in zip(pieces, nbs, starts):
        in_specs.append(pl.BlockSpec((rb, c), lambda i, s, nb=nb, st=st, f=first_row // rb: (f + jnp.clip(i - st, 0, nb - 1), 0)))
        out_specs.append(pl.BlockSpec((None, rb, c), lambda i, s, nb=nb, st=st: (s[0], jnp.clip(i - st, 0, nb - 1), 0)))
    return _hosted_call(
        body, [w for w, _, _ in pieces], name=name, grid=(sum(nbs),), prefetch=(slot,), in_specs=in_specs,
        out_specs=out_specs, out_shape=[jax.ShapeDtypeStruct((N_CHIPS, cnt, c), BF16) for _, _, cnt in pieces],
        sem=("arbitrary",), rider=rider)


def _cast_into_slot(w, slot, name):
    return _cast_into_slots([(w, 0, w.shape[0])], slot, name)[0][0]


def _adamw_halves(w, mine, theirs, m, v, core, name):
    r, c = w.shape
    r2 = r // 2
    rb = max(b for b in range(8, r2 + 1, 8) if r2 % b == 0 and b * c * 4 <= (1 << 21))
    nbh = r2 // rb

    def body(z_ref, w_ref, a_ref, b_ref, m_ref, v_ref, g_ref, d_ref, mo_ref, vo_ref):
        here = (pl.program_id(0) // nbh) == z_ref[0]
        gg = jnp.where(here, a_ref[...], b_ref[...])
        g_ref[...] = gg
        d_ref[...], mo_ref[...], vo_ref[...] = _adam_math(w_ref[...], gg, m_ref[...], v_ref[...])

    spec = pl.BlockSpec((rb, c), lambda i, z: (i, 0))
    a_spec = pl.BlockSpec((rb, c), lambda i, z: (jnp.clip(i - z[0] * nbh, 0, nbh - 1), 0))
    b_spec = pl.BlockSpec((rb, c), lambda i, z: (jnp.clip(i - (1 - z[0]) * nbh, 0, nbh - 1), 0))
    shp = jax.ShapeDtypeStruct((r, c), F32)
    return pl.pallas_call(
        body, name=name,
        grid_spec=pltpu.PrefetchScalarGridSpec(
            num_scalar_prefetch=1, grid=(r // rb,), in_specs=[spec, a_spec, b_spec, spec, spec], out_specs=[spec] * 4),
        out_shape=[shp] * 4,
        compiler_params=_params(("parallel",)),
    )(core, w, mine, theirs, m, v)


def _adamw(w, g, m, v, name, rider=None):
    r, c = w.shape
    rb = r
    for cand in (256, 128, 64, 32, 16, 8):
        if r % cand == 0 and cand * c * 4 <= (1 << 20):
            rb = cand
            break
    if r * c * 4 <= (1 << 20):
        rb = r

    def body(w_ref, g_ref, m_ref, v_ref, d_ref, mo_ref, vo_ref):
        d_ref[...], mo_ref[...], vo_ref[...] = _adam_math(w_ref[...], g_ref[...], m_ref[...], v_ref[...])

    spec = pl.BlockSpec((rb, c), lambda i: (i, 0))
    shp = jax.ShapeDtypeStruct((r, c), F32)
    return _hosted_call(
        body, (w, g, m, v), name=name, grid=(r // rb,), in_specs=[spec] * 4, out_specs=[spec] * 3, out_shape=[shp] * 3,
        sem=("parallel",), rider=rider)


def _decay_prep(dec):
    def body(d_ref, lg_ref, sg_ref):
        d = d_ref[...]
        lg_ref[...] = jnp.minimum(d, 0.0) - jnp.log(1.0 + jnp.exp(-jnp.abs(d)))
        sg_ref[...] = 1.0 / (1.0 + jnp.exp(d))

    shp = jax.ShapeDtypeStruct(dec.shape, F32)
    return pl.pallas_call(body, name="decay_prep", out_shape=[shp, shp])(dec)


def _mod_fwd(a_in, w_ada, b_sh):
    rows, d = a_in.shape
    n = w_ada.shape[1]
    bn = 512

    def body(a_ref, w_ref, b_ref, o_ref):
        a = a_ref[...]
        s = (a / (1.0 + jnp.exp(-a))).astype(BF16)
        o_ref[...] = _dot(s, w_ref[...].astype(BF16)) + b_ref[...]

    return pl.pallas_call(
        body, name="mod_fwd", grid=(n // bn,),
        in_specs=[_full((rows, d)), pl.BlockSpec((d, bn), lambda j: (0, j)), pl.BlockSpec((1, bn), lambda j: (0, j))],
        out_specs=pl.BlockSpec((rows, bn), lambda j: (0, j)),
        out_shape=jax.ShapeDtypeStruct((rows, n), F32),
        compiler_params=_params(("parallel",)),
    )(a_in, w_ada, b_sh)


def _mod_bwd(a_in, dm, w_ada):
    rows, d = a_in.shape
    n = w_ada.shape[1]
    bn = 512
    nb = n // bn

    def body(a_ref, dm_ref, w_ref, gw_ref, da_ref):
        j = pl.program_id(0)
        a = a_ref[...]
        s = (a / (1.0 + jnp.exp(-a))).astype(BF16)
        dmb = dm_ref[...].astype(BF16)
        gw_ref[...] = _dot_tn(s, dmb)
        part = _dot_nt(dmb, w_ref[...].astype(BF16))

        @pl.when(j == 0)
        def _():
            da_ref[...] = part

        @pl.when(j > 0)
        def _():
            da_ref[...] += part

    return pl.pallas_call(
        body, name="mod_bwd", grid=(nb,),
        in_specs=[_full((rows, d)), pl.BlockSpec((rows, bn), lambda j: (0, j)), pl.BlockSpec((d, bn), lambda j: (0, j))],
        out_specs=[pl.BlockSpec((d, bn), lambda j: (0, j)), _full((rows, d))],
        out_shape=[jax.ShapeDtypeStruct((d, n), F32), jax.ShapeDtypeStruct((rows, d), F32)],
        compiler_params=_params(("arbitrary",)),
    )(a_in, dm, w_ada)


def _pre_fwd(x2, ctx2, modv, g_attn, w_in, g_q, g_kv, w_uq, w_ukv, cos_t, sin_t, *, seq, tm, rider=None):
    t_lat, d = x2.shape
    t_ctx = ctx2.shape[0]
    nl, nc = t_lat // tm, t_ctx // tm
    n_all = t_lat + t_ctx
    tpe = seq // tm
    nex = t_lat // seq

    def body(x_ref, c_ref, mod_ref, g_ref, win_ref, gq_ref, gkv_ref, wuq_ref, wukv_ref, cos_ref, sin_ref,
             h_ref, pg_ref, rq_ref, rk_ref, rv_ref, nq_ref, nkv_ref, q_ref, k_ref, v_ref):
        i = pl.program_id(0)
        xt = jnp.where(i < nl, x_ref[...], c_ref[...])
        sh = mod_ref[0, 0:1, :]
        sc = mod_ref[0, 1:2, :]
        r = lax.rsqrt(jnp.mean(xt * xt, axis=-1, keepdims=True) + EPS)
        hb = ((xt * r) * g_ref[...] * (1.0 + sc) + sh).astype(BF16)
        h_ref[...] = hb
        p = _dot_nt(hb, win_ref[...])
        cos = cos_ref[...]
        sin = sin_ref[...]
        rq_ref[...] = _rope(p[:, 0:256], cos, sin).astype(BF16)
        rk_ref[...] = _rope(p[:, 256:512] * (RET_DK ** -0.5), cos, sin).astype(BF16)
        rv_ref[...] = p[:, 512:1024].astype(BF16)
        pg_ref[...] = p[:, 1024:2176]
        cq = p[:, 1536:1920]
        ckv = p[:, 1920:2176]
        nqb = (cq * lax.rsqrt(jnp.mean(cq * cq, axis=-1, keepdims=True) + EPS) * gq_ref[...]).astype(BF16)
        nkvb = (ckv * lax.rsqrt(jnp.mean(ckv * ckv, axis=-1, keepdims=True) + EPS) * gkv_ref[...]).astype(BF16)
        nq_ref[...] = nqb
        nkv_ref[...] = nkvb
        cos1 = cos[:, 0:LANES]
        sin1 = sin[:, 0:LANES]
        kpe = _rope(p[:, 2176:2304], cos1, sin1).astype(BF16)
        for hd in range(HEADS):
            o = hd * MLA_HEAD
            qh = _dot_nt(nqb, wuq_ref[hd]) * MLA_SCALE
            q_ref[:, o:o + 128] = qh[:, 0:128].astype(BF16)
            q_ref[:, o + 128:o + 256] = _rope(qh[:, 128:256], cos1, sin1).astype(BF16)
            kvh = _dot(nkvb, wukv_ref[hd])
            k_ref[:, o:o + 128] = kvh[:, 0:128].astype(BF16)
            k_ref[:, o + 128:o + 256] = kpe
            v_ref[:, hd * 128:(hd + 1) * 128] = kvh[:, 128:256].astype(BF16)

    def tile(width):
        return pl.BlockSpec((tm, width), lambda i: (i, 0))

    widths = (d, PG_COLS, 256, 256, 512, Q_LORA, KV_LORA, HEADS * MLA_HEAD, HEADS * MLA_HEAD, HEADS * 128)
    dtypes = (BF16, F32, BF16, BF16, BF16, BF16, BF16, BF16, BF16, BF16)
    tab = pl.BlockSpec((tm, 256), lambda i: (jnp.where(i < nl, i % tpe, tpe), 0))
    return _hosted_call(
        body, (x2, ctx2, modv, g_attn, w_in, g_q, g_kv, w_uq, w_ukv, cos_t, sin_t), name="pre_fwd", grid=(nl + nc,),
        in_specs=[
            pl.BlockSpec((tm, d), lambda i: (jnp.minimum(i, nl - 1), 0)),
            pl.BlockSpec((tm, d), lambda i: (jnp.maximum(i - nl, 0), 0)),
            pl.BlockSpec((1, 8, d), lambda i: (jnp.minimum(i // tpe, nex), 0, 0)),
            _full((1, d)), _full(w_in.shape), _full((1, Q_LORA)), _full((1, KV_LORA)),
            _full(w_uq.shape), _full(w_ukv.shape), tab, tab,
        ],
        out_specs=[tile(w) for w in widths],
        out_shape=[jax.ShapeDtypeStruct((n_all, w), dt) for w, dt in zip(widths, dtypes)],
        sem=("parallel",), rider=rider)


def _post(yret, ymla, x2, tgt2, modv, g_ffn, g_fin, w_out, w_ff1, w_ff2a, w_ff2b, *, seq, tm):
    t_lat, d = x2.shape
    nl = t_lat // tm
    tpe = seq // tm
    nex = t_lat // seq
    n_slab = w_ff1.shape[0]
    fs = w_ff1.shape[2]
    fh = w_ff2a.shape[1]

    def body(yr_ref, ym_ref, x_ref, t_ref, mod_ref, gf_ref, gl_ref, wo_ref, w1_ref, w2a_ref, w2b_ref,
             mix_ref, a_ref, du_ref, h2_ref, df_ref, dmo_ref, dmix_ref, dxm_ref, st_ref, ru_ref):
        i = pl.program_id(0)
        gt_a = mod_ref[0, 2:3, :]
        sh_f = mod_ref[0, 3:4, :]
        sc_f = mod_ref[0, 4:5, :]
        gt_f = mod_ref[0, 5:6, :]
        g_ffn_v = gf_ref[...]
        g_fin_v = gl_ref[...]
        yr = yr_ref[...]
        ym = ym_ref[...]
        mix_ref[:, 0:512] = yr
        mix_ref[:, 512:1024] = ym
        op = _dot(yr, wo_ref[0:512, :]) + _dot(ym, wo_ref[512:1024, :])
        x_mid = x_ref[...] + gt_a * op
        r2 = lax.rsqrt(jnp.mean(x_mid * x_mid, axis=-1, keepdims=True) + EPS)
        xh2 = x_mid * r2
        h2b = (xh2 * g_ffn_v * (1.0 + sc_f) + sh_f).astype(BF16)
        h2_ref[...] = h2b
        f = jnp.zeros((tm, d), F32)
        for s in range(n_slab):
            ru = jnp.maximum(_dot(h2b, w1_ref[s]), 0.0)
            ru_ref[:, s * fs:(s + 1) * fs] = ru
            ab = (ru * ru).astype(BF16)
            a_ref[:, s * fs:(s + 1) * fs] = ab
            f = f + _dot(ab[:, 0:fh], w2a_ref[s]) + _dot(ab[:, fh:fs], w2b_ref[s])
        x_out = x_mid + gt_f * f
        r3 = lax.rsqrt(jnp.mean(x_out * x_out, axis=-1, keepdims=True) + EPS)
        xh3 = x_out * r3
        err = xh3 * g_fin_v - t_ref[...]
        dy = err * (1.0 / d)
        dxh3 = dy * g_fin_v
        dx_out = r3 * (dxh3 - xh3 * jnp.mean(dxh3 * xh3, axis=-1, keepdims=True))
        dfb = (dx_out * gt_f).astype(BF16)
        df_ref[...] = dfb
        dh2 = jnp.zeros((tm, d), F32)
        for s in range(n_slab):
            da = jnp.concatenate([_dot_nt(dfb, w2a_ref[s]), _dot_nt(dfb, w2b_ref[s])], axis=1)
            dub = (da * (2.0 * ru_ref[:, s * fs:(s + 1) * fs])).astype(BF16)
            du_ref[:, s * fs:(s + 1) * fs] = dub
            dh2 = dh2 + _dot_nt(dub, w1_ref[s])
        dxh2 = dh2 * (1.0 + sc_f) * g_ffn_v
        dx_mid = dx_out + r2 * (dxh2 - xh2 * jnp.mean(dxh2 * xh2, axis=-1, keepdims=True))
        dxm_ref[...] = dx_mid
        dmob = (dx_mid * gt_a).astype(BF16)
        dmo_ref[...] = dmob
        dmix_ref[...] = _dot_nt(dmob, wo_ref[...]).astype(BF16)

        def rsum(v):
            return jnp.sum(v, axis=0, keepdims=True)

        stats = jnp.concatenate([
            rsum(dh2), rsum(dh2 * xh2 * g_ffn_v), rsum(dx_out * f), rsum(dx_mid * op),
            rsum(dh2 * (1.0 + sc_f) * xh2), rsum(dy * xh3), rsum(err * err), jnp.zeros((1, d), F32)], axis=0)

        @pl.when(i % tpe == 0)
        def _():
            st_ref[0] = stats

        @pl.when(i % tpe != 0)
        def _():
            st_ref[0] += stats

    def tile(width):
        return pl.BlockSpec((tm, width), lambda i: (i, 0))

    widths = (d, D_FF, D_FF, d, d, d, d, d)
    dtypes = (BF16, BF16, BF16, BF16, BF16, BF16, BF16, F32)
    const = pl.Buffered(1)
    return pl.pallas_call(
        body, name="post", grid=(nl,),
        in_specs=[
            tile(512), tile(512), tile(d), tile(d),
            pl.BlockSpec((1, 8, d), lambda i: (i // tpe, 0, 0)),
            _full((1, d)), _full((1, d)),
            pl.BlockSpec(w_out.shape, lambda i: (0, 0), pipeline_mode=const),
            pl.BlockSpec(w_ff1.shape, lambda i: (0, 0, 0), pipeline_mode=const),
            pl.BlockSpec(w_ff2a.shape, lambda i: (0, 0, 0), pipeline_mode=const),
            pl.BlockSpec(w_ff2b.shape, lambda i: (0, 0, 0), pipeline_mode=const),
        ],
        out_specs=[tile(w) for w in widths] + [pl.BlockSpec((1, 8, d), lambda i: (i // tpe, 0, 0))],
        out_shape=[jax.ShapeDtypeStruct((t_lat, w), dt) for w, dt in zip(widths, dtypes)]
        + [jax.ShapeDtypeStruct((nex, 8, d), F32)],
        scratch_shapes=[pltpu.VMEM((tm, D_FF), F32)],
        compiler_params=_params(("arbitrary",), VMEM_LIMIT),
    )(yret, ymla, x2, tgt2, modv, g_ffn, g_fin, w_out, w_ff1, w_ff2a, w_ff2b)


def _pre_bwd(x2, ctx2, modv, g_attn, pg, drq, drk, dkc_r, drv, dvc_r, drg, dq_m, dkl, dkc, dvl, dvc, dxm,
             w_in, g_q, g_kv, w_uq, w_ukv, cos_t, sin_t, *, seq, tm, rider=None):
    t_lat, d = x2.shape
    t_ctx = ctx2.shape[0]
    nl, nc = t_lat // tm, t_ctx // tm
    n_all = t_lat + t_ctx
    tpe = seq // tm
    nex = t_lat // seq

    def body(x_ref, c_ref, mod_ref, g_ref, pg_ref, drq_ref, drk_ref, dkcr_ref, drv_ref, dvcr_ref, drg_ref,
             dq_ref, dkl_ref, dkc_ref, dvl_ref, dvc_ref, dxm_ref, win_ref, gq_ref, gkv_ref, wuq_ref, wukv_ref,
             cos_ref, sin_ref, dpb_ref, dqf_ref, dkvf_ref, gx_ref, st_ref):
        i = pl.program_id(0)
        lat = i < nl
        latf = lat.astype(F32)
        cos = cos_ref[...]
        sin = sin_ref[...]
        cos1 = cos[:, 0:LANES]
        sin1 = sin[:, 0:LANES]
        d_rq = _rope_t(drq_ref[...] * latf, cos, sin)
        d_rk = _rope_t(jnp.where(lat, drk_ref[...], dkcr_ref[...]), cos, sin) * (RET_DK ** -0.5)
        d_rv = jnp.where(lat, drv_ref[...], dvcr_ref[...])
        d_rg = drg_ref[...] * latf
        dq_all = dq_ref[...] * (latf * MLA_SCALE)
        dk_all = jnp.where(lat, dkl_ref[...], dkc_ref[...])
        dv_all = jnp.where(lat, dvl_ref[...], dvc_ref[...])
        dnq = jnp.zeros((tm, Q_LORA), F32)
        dnkv = jnp.zeros((tm, KV_LORA), F32)
        dkpe = jnp.zeros((tm, LANES), F32)
        for hd in range(HEADS):
            o = hd * MLA_HEAD
            dqh = jnp.concatenate([dq_all[:, o:o + 128], _rope_t(dq_all[:, o + 128:o + 256], cos1, sin1)],
                                  axis=1).astype(BF16)
            dqf_ref[:, o:o + 256] = dqh
            dnq = dnq + _dot(dqh, wuq_ref[hd])
            dkpe = dkpe + dk_all[:, o + 128:o + 256]
            dkvh = jnp.concatenate([dk_all[:, o:o + 128], dv_all[:, hd * 128:(hd + 1) * 128]], axis=1).astype(BF16)
            dkvf_ref[:, o:o + 256] = dkvh
            dnkv = dnkv + _dot_nt(dkvh, wukv_ref[hd])
        d_kpe = _rope_t(dkpe, cos1, sin1)
        pgv = pg_ref[...]
        cq = pgv[:, 512:896]
        ckv = pgv[:, 896:1152]
        rq_ = lax.rsqrt(jnp.mean(cq * cq, axis=-1, keepdims=True) + EPS)
        cqh = cq * rq_
        dcqh = dnq * gq_ref[...]
        d_cq = rq_ * (dcqh - cqh * jnp.mean(dcqh * cqh, axis=-1, keepdims=True))
        rkv_ = lax.rsqrt(jnp.mean(ckv * ckv, axis=-1, keepdims=True) + EPS)
        ckvh = ckv * rkv_
        dckvh = dnkv * gkv_ref[...]
        d_ckv = rkv_ * (dckvh - ckvh * jnp.mean(dckvh * ckvh, axis=-1, keepdims=True))
        dpb = jnp.concatenate([d_rq, d_rk, d_rv, d_rg, d_cq, d_ckv, d_kpe], axis=1).astype(BF16)
        dpb_ref[...] = dpb
        dh = _dot(dpb, win_ref[...])
        xt = jnp.where(lat, x_ref[...], c_ref[...])
        sc = mod_ref[0, 1:2, :]
        g = g_ref[...]
        r = lax.rsqrt(jnp.mean(xt * xt, axis=-1, keepdims=True) + EPS)
        xh = xt * r
        dxh = dh * (1.0 + sc) * g
        dx = r * (dxh - xh * jnp.mean(dxh * xh, axis=-1, keepdims=True))

        @pl.when(lat)
        def _():
            gx_ref[...] = dxm_ref[...] + dx

        def rsum(v):
            return jnp.sum(v, axis=0, keepdims=True)

        def widen(v):
            return jnp.concatenate([v, jnp.zeros((1, d - v.shape[1]), F32)], axis=1)

        stats = jnp.concatenate([
            rsum(dh), rsum(dh * xh * g), rsum(dh * (1.0 + sc) * xh), widen(rsum(dnq * cqh)), widen(rsum(dnkv * ckvh)),
            jnp.zeros((3, d), F32)], axis=0)
        first = jnp.logical_or(jnp.logical_and(lat, i % tpe == 0), i == nl)

        @pl.when(first)
        def _():
            st_ref[0] = stats

        @pl.when(jnp.logical_not(first))
        def _():
            st_ref[0] += stats

    def lat_tile(width):
        return pl.BlockSpec((tm, width), lambda i: (jnp.minimum(i, nl - 1), 0))

    def ctx_tile(width):
        return pl.BlockSpec((tm, width), lambda i: (jnp.maximum(i - nl, 0), 0))

    def tile(width):
        return pl.BlockSpec((tm, width), lambda i: (i, 0))

    tab = pl.BlockSpec((tm, 256), lambda i: (jnp.where(i < nl, i % tpe, tpe), 0))
    ex = pl.BlockSpec((1, 8, d), lambda i: (jnp.minimum(i // tpe, nex), 0, 0))
    return _hosted_call(
        body, (x2, ctx2, modv, g_attn, pg, drq, drk, dkc_r, drv, dvc_r, drg, dq_m, dkl, dkc, dvl, dvc, dxm,
               w_in, g_q, g_kv, w_uq, w_ukv, cos_t, sin_t), name="pre_bwd", grid=(nl + nc,),
        in_specs=[
            lat_tile(d), ctx_tile(d), ex, _full((1, d)), tile(PG_COLS),
            lat_tile(256), lat_tile(256), ctx_tile(256), lat_tile(512), ctx_tile(512), lat_tile(512),
            lat_tile(1024), lat_tile(1024), ctx_tile(1024), lat_tile(512), ctx_tile(512), lat_tile(d),
            _once(w_in.shape), _full((1, Q_LORA)), _full((1, KV_LORA)), _once(w_uq.shape), _once(w_ukv.shape),
            tab, tab,
        ],
        out_specs=[tile(IN_PAD), tile(1024), tile(1024), lat_tile(d), ex],
        out_shape=[
            jax.ShapeDtypeStruct((n_all, IN_PAD), BF16), jax.ShapeDtypeStruct((n_all, 1024), BF16),
            jax.ShapeDtypeStruct((n_all, 1024), BF16), jax.ShapeDtypeStruct((t_lat, d), F32),
            jax.ShapeDtypeStruct((nex + 1, 8, d), F32),
        ],
        sem=("arbitrary",), rider=rider)


MLA_SCALE = 1.0 / math.sqrt(MLA_NOPE + MLA_ROPE)
KEY_BLOCK = 1024


def _mla_specs(t_lat, seq, ctx_len, tq, heads=1):
    nqt = seq // tq
    cb = t_lat // ctx_len
    q = pl.BlockSpec((tq, heads * MLA_HEAD), lambda b, h, j: (b * nqt + j, h))
    kl = pl.BlockSpec((seq, heads * MLA_HEAD), lambda b, h, j: (b, h))
    kc = pl.BlockSpec((ctx_len, heads * MLA_HEAD), lambda b, h, j: (cb + b, h))
    vl = pl.BlockSpec((seq, heads * 128), lambda b, h, j: (b, h))
    vc = pl.BlockSpec((ctx_len, heads * 128), lambda b, h, j: (cb + b, h))
    o = pl.BlockSpec((tq, heads * 128), lambda b, h, j: (b * nqt + j, h))
    return q, kl, kc, vl, vc, o


FWD_HEADS = 4
BWD_HEADS = 1


def _mla_fwd(q, k, v, *, t_lat, seq, ctx_len, tq, rider=None):
    nex = t_lat // seq

    def body(q_ref, kl_ref, kc_ref, vl_ref, vc_ref, o_ref, lse_ref):
        for hh in range(FWD_HEADS):
            wide = slice(hh * MLA_HEAD, (hh + 1) * MLA_HEAD)
            cols = slice(hh * 128, (hh + 1) * 128)
            qb = q_ref[:, wide]
            s = _dot_nt(qb, kl_ref[:, wide])
            sc = _dot_nt(qb, kc_ref[:, wide])
            m = jnp.maximum(jnp.max(s, axis=-1, keepdims=True), jnp.max(sc, axis=-1, keepdims=True))
            p = jnp.exp(s - m)
            pc = jnp.exp(sc - m)
            total = jnp.sum(p, axis=-1, keepdims=True) + jnp.sum(pc, axis=-1, keepdims=True)
            o = _dot(p.astype(BF16), vl_ref[:, cols]) + _dot(pc.astype(BF16), vc_ref[:, cols])
            o_ref[:, cols] = (o * (1.0 / total)).astype(BF16)
            lse_ref[:, cols] = jnp.broadcast_to(m + jnp.log(total), (tq, 128))

    qs, kl, kc, vl, vc, os_ = _mla_specs(t_lat, seq, ctx_len, tq, FWD_HEADS)
    return _hosted_call(
        body, (q, k, k, v, v), name="mla_fwd", grid=(nex, HEADS // FWD_HEADS, seq // tq),
        in_specs=[qs, kl, kc, vl, vc], out_specs=[os_, os_],
        out_shape=[jax.ShapeDtypeStruct((t_lat, HEADS * 128), BF16), jax.ShapeDtypeStruct((t_lat, HEADS * 128), F32)],
        sem=("parallel", "parallel", "arbitrary"), rider=rider)


def _mla_bwd(q, k, v, ymla, lse, dmix, *, t_lat, seq, ctx_len, tq, rider=None):
    nex = t_lat // seq
    nqt = seq // tq
    t_ctx = nex * ctx_len
    kb = min(KEY_BLOCK, seq)

    def body(q_ref, kl_ref, kc_ref, vl_ref, vc_ref, o_ref, lse_ref, do_ref, dq_ref, dkl_out, dkc_out, dvl_out, dvc_out,
             dkl_ref, dkc_ref, dvl_ref, dvc_ref):
        j = pl.program_id(2)

        @pl.when(j == 0)
        def _():
            dkl_ref[...] = jnp.zeros(dkl_ref.shape, F32)
            dkc_ref[...] = jnp.zeros(dkc_ref.shape, F32)
            dvl_ref[...] = jnp.zeros(dvl_ref.shape, F32)
            dvc_ref[...] = jnp.zeros(dvc_ref.shape, F32)

        for hh in range(BWD_HEADS):
            wide = slice(hh * MLA_HEAD, (hh + 1) * MLA_HEAD)
            cols = slice(hh * 128, (hh + 1) * 128)
            qb = q_ref[:, wide]
            dob = do_ref[:, cols]
            delta = jnp.sum(dob.astype(F32) * o_ref[:, cols].astype(F32), axis=-1, keepdims=True)
            lse_row = lse_ref[:, hh * 128:hh * 128 + 1]

            def block(k_ref, v_ref, dk_ref, dv_ref, rows):
                kbl = k_ref[rows, wide]
                vbl = v_ref[rows, cols]
                p = jnp.exp(_dot_nt(qb, kbl) - lse_row)
                ds = (p * (_dot_nt(dob, vbl) - delta)).astype(BF16)
                dk_ref[rows, wide] += _dot_tn(ds, qb)
                dv_ref[rows, cols] += _dot_tn(p.astype(BF16), dob)
                return _dot(ds, kbl)

            dq = block(kc_ref, vc_ref, dkc_ref, dvc_ref, pl.ds(0, ctx_len))
            for i in range(seq // kb):
                dq = dq + block(kl_ref, vl_ref, dkl_ref, dvl_ref, pl.ds(i * kb, kb))
            dq_ref[:, wide] = dq.astype(BF16)

        @pl.when(j == nqt - 1)
        def _():
            dkl_out[...] = dkl_ref[...].astype(BF16)
            dkc_out[...] = dkc_ref[...].astype(BF16)
            dvl_out[...] = dvl_ref[...].astype(BF16)
            dvc_out[...] = dvc_ref[...].astype(BF16)

    g = BWD_HEADS
    qs, kl, kc, vl, vc, os_ = _mla_specs(t_lat, seq, ctx_len, tq, g)
    do_spec = pl.BlockSpec((tq, g * 128), lambda b, h, j: (b * nqt + j, HEADS // g + h))
    key_blocks = [(seq, g * MLA_HEAD), (ctx_len, g * MLA_HEAD), (seq, g * 128), (ctx_len, g * 128)]
    return _hosted_call(
        body, (q, k, k, v, v, ymla, lse, dmix), name="mla_bwd", grid=(nex, HEADS // g, nqt),
        in_specs=[qs, kl, kc, vl, vc, os_, os_, do_spec],
        out_specs=[qs] + [pl.BlockSpec(blk, lambda b, h, j: (b, h)) for blk in key_blocks],
        out_shape=[
            jax.ShapeDtypeStruct((t_lat, HEADS * MLA_HEAD), BF16),
            jax.ShapeDtypeStruct((t_lat, HEADS * MLA_HEAD), BF16),
            jax.ShapeDtypeStruct((t_ctx, HEADS * MLA_HEAD), BF16),
            jax.ShapeDtypeStruct((t_lat, HEADS * 128), BF16),
            jax.ShapeDtypeStruct((t_ctx, HEADS * 128), BF16),
        ],
        scratch_shapes=[pltpu.VMEM(blk, F32) for blk in key_blocks],
        sem=("parallel", "parallel", "arbitrary"), rider=rider)


def _decay_terms(lg, chunk, forward):
    ii = lax.broadcasted_iota(jnp.int32, (chunk, chunk), 0)
    jj = lax.broadcasted_iota(jnp.int32, (chunk, chunk), 1)
    diff = (ii - jj) if forward else (jj - ii)
    dist = jnp.maximum(diff, 0).astype(F32)
    dmat = jnp.where(diff >= 0, jnp.exp(lg * dist), 0.0)
    pos = lax.broadcasted_iota(jnp.int32, (chunk, 1), 0).astype(F32)
    if forward:
        e_q = pos + 1.0
        e_k = (chunk - 1.0) - pos
    else:
        e_q = chunk - pos
        e_k = pos
    wq = jnp.exp(lg * e_q)
    wk = jnp.exp(lg * e_k)
    cd = jnp.exp(jnp.full((1, 1), lg * chunk, F32))
    return dmat, dist, wq, wk, e_q, e_k, cd


def _ctx_weights(lg, ctx_len, forward):
    pos = lax.broadcasted_iota(jnp.int32, (ctx_len, 1), 0).astype(F32)
    e = ((ctx_len - 1.0) - pos) if forward else pos
    return jnp.exp(lg * e), e


def _pair_specs(t_lat, seq, ctx_len):
    cb = t_lat // ctx_len
    qk = pl.BlockSpec((seq, 128), lambda b, p: (b, p))
    v = pl.BlockSpec((seq, 256), lambda b, p: (b, p))
    kc = pl.BlockSpec((ctx_len, 128), lambda b, p: (cb + b, p))
    vc = pl.BlockSpec((ctx_len, 256), lambda b, p: (cb + b, p))
    return qk, v, kc, vc


def _lane_masks():
    lane = lax.broadcasted_iota(jnp.int32, (1, 128), 1)
    return [(lane // RET_DK) == hh for hh in (0, 1)]


def _ret_fwd_pair(rq, rk, rv, pg, lg, g_ret, *, t_lat, seq, ctx_len, chunk, rider=None):
    nex = t_lat // seq
    n_chunk = seq // chunk

    def body(q_ref, k_ref, v_ref, kc_ref, vc_ref, rg_ref, lg_ref, g_ref, y_ref, o_ref):
        pair = pl.program_id(1)
        masks = _lane_masks()
        kcf = kc_ref[...].astype(F32)
        chains = [(forward, hh) for forward in (True, False) for hh in (0, 1)]
        terms, s0 = [], []
        for forward, hh in chains:
            lgd = lg_ref[0 if forward else 1, 2 * pair + hh]
            terms.append(_decay_terms(lgd, chunk, forward))
            wc, _ = _ctx_weights(lgd, ctx_len, forward)
            s0.append(_dot_tn((jnp.where(masks[hh], kcf, 0.0) * wc).astype(BF16), vc_ref[:, hh * 128:(hh + 1) * 128]))
        both = [terms[hh][0] + terms[2 + hh][0] for hh in (0, 1)]
        o_ref[...] = jnp.zeros(o_ref.shape, F32)

        def step(t, states):
            new = [None] * 4
            for forward in (True, False):
                n = t if forward else n_chunk - 1 - t
                sl = pl.ds(pl.multiple_of(n * chunk, chunk), chunk)
                qb = q_ref[sl, :]
                kf_all = k_ref[sl, :].astype(F32)
                for hh in (0, 1):
                    c = (0 if forward else 2) + hh
                    _, _, wq, wk, _, _, cd = terms[c]
                    cols = slice(hh * 128, (hh + 1) * 128)
                    qm = jnp.where(masks[hh], qb, jnp.zeros((), BF16))
                    kf = jnp.where(masks[hh], kf_all, 0.0)
                    vb = v_ref[sl, cols]
                    o = wq * _dot(qm, states[c].astype(BF16))
                    if forward:
                        o = o + _dot((_dot_nt(qm, kf.astype(BF16)) * both[hh]).astype(BF16), vb)
                    o_ref[sl, cols] += o
                    new[c] = cd * states[c] + _dot_tn((kf * wk).astype(BF16), vb)
            return tuple(new)

        lax.fori_loop(0, n_chunk, step, tuple(s0))

        def norm_step(n, carry):
            sl = pl.ds(pl.multiple_of(n * chunk, chunk), chunk)
            for hh in (0, 1):
                cols = slice(hh * 128, (hh + 1) * 128)
                o = o_ref[sl, cols]
                mu = jnp.mean(o, axis=-1, keepdims=True)
                oc = o - mu
                var = jnp.mean(oc * oc, axis=-1, keepdims=True)
                rg = rg_ref[sl, cols]
                y_ref[sl, cols] = (oc * lax.rsqrt(var + EPS) * g_ref[:, cols] * (rg / (1.0 + jnp.exp(-rg)))).astype(BF16)
            return carry

        lax.fori_loop(0, n_chunk, norm_step, 0)

    qk, v, kc, vc = _pair_specs(t_lat, seq, ctx_len)
    return _hosted_call(
        body, (rq, rk, rv, rk, rv, pg, lg, g_ret), name="ret_fwd", grid=(nex, HEADS // 2),
        in_specs=[qk, qk, v, kc, vc, v, pl.BlockSpec(memory_space=pltpu.SMEM), pl.BlockSpec((1, 256), lambda b, p: (0, p))],
        out_specs=[v, v],
        out_shape=[jax.ShapeDtypeStruct((t_lat, HEADS * RET_DV), BF16), jax.ShapeDtypeStruct((t_lat, HEADS * RET_DV), F32)],
        sem=("parallel", "arbitrary"), rider=rider)


def _ret_bwd_pair(rq, rk, rv, pg, osum, dmix, lg, g_ret, *, t_lat, seq, ctx_len, chunk, rider=None):
    nex = t_lat // seq
    n_chunk = seq // chunk
    t_ctx = nex * ctx_len

    def body(q_ref, k_ref, v_ref, kc_ref, vc_ref, rg_ref, o_ref, dy_ref, lg_ref, g_ref,
             dq_out, dk_out, dv_out, dkc_ref, dvc_ref, drg_ref, st_ref, do_s, s_st, dq_ref, dk_ref, dv_ref):
        pair = pl.program_id(1)
        masks = _lane_masks()
        kcf = kc_ref[...].astype(F32)

        def norm_step(n, dgains):
            sl = pl.ds(pl.multiple_of(n * chunk, chunk), chunk)
            out = []
            for hh in (0, 1):
                cols = slice(hh * 128, (hh + 1) * 128)
                gain = g_ref[:, cols]
                o = o_ref[sl, cols]
                mu = jnp.mean(o, axis=-1, keepdims=True)
                oc = o - mu
                rstd = lax.rsqrt(jnp.mean(oc * oc, axis=-1, keepdims=True) + EPS)
                ohat = oc * rstd
                rg = rg_ref[sl, cols]
                sg = 1.0 / (1.0 + jnp.exp(-rg))
                dy = dy_ref[sl, cols].astype(F32)
                don = dy * (rg * sg)
                drg_ref[sl, cols] = (dy * (ohat * gain) * (sg * (1.0 + rg * (1.0 - sg)))).astype(BF16)
                dohat = don * gain
                do_s[sl, cols] = rstd * (dohat - jnp.mean(dohat, axis=-1, keepdims=True)
                                         - ohat * jnp.mean(dohat * ohat, axis=-1, keepdims=True))
                out.append(dgains[hh] + jnp.sum(don * ohat, axis=0, keepdims=True))
            return tuple(out)

        zero_row = jnp.zeros((1, 128), F32)
        dgains = lax.fori_loop(0, n_chunk, norm_step, (zero_row, zero_row))
        dq_ref[...] = jnp.zeros(dq_ref.shape, F32)
        dk_ref[...] = jnp.zeros(dk_ref.shape, F32)
        dv_ref[...] = jnp.zeros(dv_ref.shape, F32)

        chains = [(forward, hh) for forward in (True, False) for hh in (0, 1)]
        terms, ctxw, s0 = [], [], []
        for forward, hh in chains:
            lgd = lg_ref[0 if forward else 1, 2 * pair + hh]
            terms.append(_decay_terms(lgd, chunk, forward))
            ctxw.append(_ctx_weights(lgd, ctx_len, forward))
            s0.append(_dot_tn((jnp.where(masks[hh], kcf, 0.0) * ctxw[-1][0]).astype(BF16), vc_ref[:, hh * 128:(hh + 1) * 128]))

        def chunk_at(t, ascending):
            n = t if ascending else n_chunk - 1 - t
            return n, pl.ds(pl.multiple_of(n * chunk, chunk), chunk)

        def state_step(t, states):
            new = []
            for c, (forward, hh) in enumerate(chains):
                n, sl = chunk_at(t, forward)
                wk, cd = terms[c][3], terms[c][6]
                s_st[c, n] = states[c]
                kf = jnp.where(masks[hh], k_ref[sl, :].astype(F32), 0.0)
                new.append(cd * states[c] + _dot_tn((kf * wk).astype(BF16), v_ref[sl, hh * 128:(hh + 1) * 128]))
            return tuple(new)

        lax.fori_loop(0, n_chunk, state_step, tuple(s0))

        both = [terms[hh][0] + terms[2 + hh][0] for hh in (0, 1)]

        def grad_step(t, carry):
            out = [None] * len(chains)
            in_chunk_b = [None, None]
            for forward in (True, False):
                n, sl = chunk_at(t, not forward)
                qb = q_ref[sl, :]
                kf_all = k_ref[sl, :].astype(F32)
                dq_sum = jnp.zeros((chunk, 128), F32)
                dk_sum = jnp.zeros((chunk, 128), F32)
                for hh in (0, 1):
                    c = (0 if forward else 2) + hh
                    g_next, dlg = carry[c]
                    dmat, dist, wq, wk, e_q, e_k, cd = terms[c]
                    cols = slice(hh * 128, (hh + 1) * 128)
                    qm = jnp.where(masks[hh], qb, jnp.zeros((), BF16))
                    kf = jnp.where(masks[hh], kf_all, 0.0)
                    kb = kf.astype(BF16)
                    vb = v_ref[sl, cols]
                    do = do_s[sl, cols]
                    dob = do.astype(BF16)
                    s_n = s_st[c, n]
                    s_nb = s_n.astype(BF16)
                    gb = g_next.astype(BF16)
                    dk_cross = wk * _dot_nt(vb, gb)
                    dv = _dot((kf * wk).astype(BF16), gb)
                    o_cross = wq * _dot(qm, s_nb)
                    dq_sum = dq_sum + wq * _dot_nt(dob, s_nb)
                    dk_sum = dk_sum + dk_cross
                    dlg = (dlg + chunk * cd * jnp.sum(g_next * s_n, keepdims=True)
                           + jnp.sum(e_k * jnp.sum(kf * dk_cross, axis=-1, keepdims=True), keepdims=True)
                           + jnp.sum(e_q * jnp.sum(o_cross * do, axis=-1, keepdims=True), keepdims=True))
                    if forward:
                        a_raw = _dot_nt(qm, kb)
                        da_raw = _dot_nt(dob, vb)
                        prod = a_raw * da_raw
                        dlg = dlg + jnp.sum(dist * dmat * prod, keepdims=True)
                        in_chunk_b[hh] = jnp.sum(terms[2 + hh][1] * terms[2 + hh][0] * prod, keepdims=True)
                        dab = (da_raw * both[hh]).astype(BF16)
                        dq_sum = dq_sum + _dot(dab, kb)
                        dk_sum = dk_sum + _dot_tn(dab, qm)
                        dv = dv + _dot_tn((a_raw * both[hh]).astype(BF16), dob)
                    else:
                        dlg = dlg + in_chunk_b[hh]
                    dv_ref[sl, cols] += dv
                    out[c] = (cd * g_next + _dot_tn((qm.astype(F32) * wq).astype(BF16), dob), dlg)
                dq_ref[sl, :] += dq_sum
                dk_ref[sl, :] += dk_sum
            return tuple(out)

        zero = (jnp.zeros((128, 128), F32), jnp.zeros((1, 1), F32))
        res = lax.fori_loop(0, n_chunk, grad_step, (zero,) * len(chains))
        dkc_sum = jnp.zeros((ctx_len, 128), F32)
        dvc = [jnp.zeros((ctx_len, 128), F32)] * 2
        dlgs = []
        for c, (forward, hh) in enumerate(chains):
            ds0, dlg = res[c]
            wc, e_c = ctxw[c]
            kcm = jnp.where(masks[hh], kcf, 0.0)
            ds0b = ds0.astype(BF16)
            dkc_part = wc * _dot_nt(vc_ref[:, hh * 128:(hh + 1) * 128], ds0b)
            dkc_sum = dkc_sum + dkc_part
            dvc[hh] = dvc[hh] + _dot((kcm * wc).astype(BF16), ds0b)
            dlgs.append(dlg + jnp.sum(e_c * jnp.sum(kcm * dkc_part, axis=-1, keepdims=True), keepdims=True))
        dq_out[...] = dq_ref[...].astype(BF16)
        dk_out[...] = dk_ref[...].astype(BF16)
        dv_out[...] = dv_ref[...].astype(BF16)
        dkc_ref[...] = dkc_sum
        for hh in (0, 1):
            cols = slice(hh * 128, (hh + 1) * 128)
            dvc_ref[:, cols] = dvc[hh]
            st_ref[0, :, cols] = jnp.concatenate([
                dgains[hh], jnp.broadcast_to(dlgs[hh], (1, 128)), jnp.broadcast_to(dlgs[2 + hh], (1, 128)),
                jnp.zeros((5, 128), F32)], axis=0)

    qk, v, kc, vc = _pair_specs(t_lat, seq, ctx_len)
    return _hosted_call(
        body, (rq, rk, rv, rk, rv, pg, osum, dmix, lg, g_ret), name="ret_bwd", grid=(nex, HEADS // 2),
        in_specs=[qk, qk, v, kc, vc, v, v, v, pl.BlockSpec(memory_space=pltpu.SMEM),
                  pl.BlockSpec((1, 256), lambda b, p: (0, p))],
        out_specs=[
            qk, qk, v,
            pl.BlockSpec((ctx_len, 128), lambda b, p: (b, p)),
            pl.BlockSpec((ctx_len, 256), lambda b, p: (b, p)),
            v,
            pl.BlockSpec((1, 8, 256), lambda b, p: (b, 0, p)),
        ],
        out_shape=[
            jax.ShapeDtypeStruct((t_lat, 256), BF16), jax.ShapeDtypeStruct((t_lat, 256), BF16),
            jax.ShapeDtypeStruct((t_lat, 512), BF16), jax.ShapeDtypeStruct((t_ctx, 256), F32),
            jax.ShapeDtypeStruct((t_ctx, 512), F32), jax.ShapeDtypeStruct((t_lat, 512), BF16),
            jax.ShapeDtypeStruct((nex, 8, 512), F32),
        ],
        scratch_shapes=[pltpu.VMEM((seq, 256), F32), pltpu.VMEM((4, n_chunk, 128, 128), F32),
                        pltpu.VMEM((seq, 128), F32), pltpu.VMEM((seq, 128), F32), pltpu.VMEM((seq, 256), F32)],
        sem=("parallel", "arbitrary"), rider=rider)


def _matmul_tn(a, b, *, bm, bn, bk, chip_major, name, out_dtype=F32, rider=None):
    tk, m = a.shape
    n = b.shape[1]
    slab = n // N_CHIPS
    per_block = bn // slab if chip_major else 1
    bk = max(c for c in range(LANES, min(bk, tk) + 1, LANES) if tk % c == 0)
    nk = tk // bk
    blk = (per_block, bm, slab) if chip_major else (bm, bn)

    def body(a_ref, b_ref, o_ref, acc_ref):
        k = pl.program_id(2)
        if chip_major:
            parts = [_dot_tn(a_ref[...], b_ref[:, s * slab:(s + 1) * slab]) for s in range(per_block)]
        else:
            parts = [_dot_tn(a_ref[...], b_ref[...])]

        @pl.when(k == 0)
        def _():
            for s, part in enumerate(parts):
                if chip_major:
                    acc_ref[s] = part
                else:
                    acc_ref[...] = part

        @pl.when(k > 0)
        def _():
            for s, part in enumerate(parts):
                if chip_major:
                    acc_ref[s] += part
                else:
                    acc_ref[...] += part

        @pl.when(k == nk - 1)
        def _():
            o_ref[...] = acc_ref[...].astype(out_dtype)

    if chip_major:
        out_spec = pl.BlockSpec(blk, lambda i, j, k: (j, i, 0))
        out_shape = jax.ShapeDtypeStruct((N_CHIPS, m, slab), out_dtype)
    else:
        out_spec = pl.BlockSpec(blk, lambda i, j, k: (i, j))
        out_shape = jax.ShapeDtypeStruct((m, n), out_dtype)
    (out,), carried = _hosted_call(
        body, (a, b), name=name, grid=(m // bm, n // bn, nk),
        in_specs=[pl.BlockSpec((bk, bm), lambda i, j, k: (k, i)), pl.BlockSpec((bk, bn), lambda i, j, k: (k, j))],
        out_specs=[out_spec], out_shape=[out_shape], scratch_shapes=[pltpu.VMEM(blk, F32)],
        sem=("parallel", "parallel", "arbitrary"), rider=rider)
    return out if rider is None else (out, carried)


_LATE = ("w_out", "w_ff1", "w_ff2")
_EARLY = ("w_in", "w_uq", "w_ukv")


def _local_step(x, ctx, tgt, modv, lg, g_attn, g_ffn, g_fin, g_ret, g_q, g_kv, w_in, w_uq, w_ukv, late, place=None,
                *, tm=256, tq=256, chunk=256):
    nex, seq, d = x.shape
    ctx_len = ctx.shape[1]
    t_lat = nex * seq
    tm = min(tm, seq)
    x2 = x.reshape(t_lat, d)
    ctx2 = ctx.reshape(nex * ctx_len, d)
    tgt2 = tgt.reshape(t_lat, d)
    tm_fwd = min(2 * tm, seq)
    cos_t, sin_t = _rope_tables(seq, tm)
    dims = dict(t_lat=t_lat, seq=seq, ctx_len=ctx_len)
    alone = place is None

    (hb, pg, rq, rk, rv, nq, nkv, q, k, v), crossed_a = _pre_fwd(
        x2, ctx2, modv, g_attn, w_in, g_q, g_kv, w_uq, w_ukv, *_rope_tables(seq, tm_fwd), seq=seq, tm=tm_fwd,
        rider=None if alone else _gather_ici_rider([late[2]]))
    (yret, osum), got = _ret_fwd_pair(
        rq, rk, rv, pg, lg, g_ret, chunk=min(2 * chunk, seq), **dims,
        rider=None if alone else _merge_riders(_gather_d2d_rider(crossed_a), _gather_ici_rider([late[3]])))
    (ymla, lse), got_rest = _mla_fwd(
        q, k, v, tq=tq, **dims,
        rider=None if alone else _merge_riders(_gather_rider([late[0], late[1]], staged=True), _gather_d2d_rider(got[1:])))
    w_out, w_ff1, w_ff2a, w_ff2b = late if alone else (got_rest[0], got_rest[1], got[0], got_rest[2])
    mix, act, du, h2, df, dmo, dmix, dxm, st_post = _post(yret, ymla, x2, tgt2, modv, g_ffn, g_fin, w_out.reshape(d, d),
                                                         w_ff1, w_ff2a, w_ff2b, seq=seq, tm=min(tm, 256))
    kw = dict(bm=1024, bn=1024, bk=2048, out_dtype=BF16)
    g_ff2 = _matmul_tn(act, df, chip_major=False, name="gw_ff2", **kw).reshape(N_CHIPS, D_FF // N_CHIPS, d)
    if alone:
        g_ff1 = _matmul_tn(h2, du, chip_major=True, name="gw_ff1", **kw)
        g_out = _matmul_tn(mix, dmo, chip_major=False, name="gw_out", **kw).reshape(N_CHIPS, d // N_CHIPS, d)
        (dq_m, dkl, dkc, dvl, dvc), _ = _mla_bwd(q, k, v, ymla, lse, dmix, tq=tq, **dims)
        (drq, drk, drv, dkc_r, dvc_r, drg, st_ret), _ = _ret_bwd_pair(rq, rk, rv, pg, osum, dmix, lg, g_ret, chunk=chunk,
                                                                      **dims)
        late_out = [g_out, g_ff1, g_ff2]
    else:
        core, slot = place
        g_ff1, x_ff2 = _matmul_tn(h2, du, chip_major=True, name="gw_ff1", rider=_exchange_rider([g_ff2]), **kw)
        g_out, x_ff1 = _matmul_tn(mix, dmo, chip_major=False, name="gw_out", rider=_exchange_rider([g_ff1]), **kw)
        g_out = g_out.reshape(N_CHIPS, d // N_CHIPS, d)
        p_ff2 = _add_half(g_ff2, x_ff2[0], core, "add_half_w_ff2")
        p_ff1 = _add_half(g_ff1, x_ff1[0], core, "add_half_w_ff1")
        (dq_m, dkl, dkc, dvl, dvc), (l_ff2, l_ff1, x_out) = _mla_bwd(
            q, k, v, ymla, lse, dmix, tq=min(seq, 512), **dims,
            rider=_merge_riders(_scatter_rider([p_ff2, p_ff1]), _exchange_rider([g_out])))
        p_out = _add_half(g_out, x_out, core, "add_half_w_out")
        m_ff2 = _sum_chips(p_ff2, l_ff2, slot, "sum_chips_w_ff2")
        m_ff1 = _sum_chips(p_ff1, l_ff1, slot, "sum_chips_w_ff1")
        (drq, drk, drv, dkc_r, dvc_r, drg, st_ret), (l_out,) = _ret_bwd_pair(
            rq, rk, rv, pg, osum, dmix, lg, g_ret, chunk=chunk, **dims, rider=_scatter_rider([p_out]))
        late_out = [_sum_chips(p_out, l_out, slot, "sum_chips_w_out"), m_ff1, m_ff2]
    (dpb, dqf, dkvf, gx, st_pre), _ = _pre_bwd(
        x2, ctx2, modv, g_attn, pg, drq, drk, dkc_r, drv, dvc_r, drg, dq_m, dkl, dkc, dvl, dvc, dxm, w_in, g_q, g_kv,
        w_uq, w_ukv, cos_t, sin_t, seq=seq, tm=tm)
    g_early = [
        _matmul_tn(dpb, hb, bm=IN_PAD // 2, bn=d, bk=1536, chip_major=False, name="gw_in"),
        _matmul_tn(dqf, nq, bm=HEADS * MLA_HEAD, bn=Q_LORA, bk=1536, chip_major=False, name="gw_uq"),
        _matmul_tn(nkv, dkvf, bm=KV_LORA, bn=HEADS * 256, bk=1536, chip_major=True, name="gw_ukv"),
    ]
    return gx.reshape(nex, seq, d), g_early, late_out, st_post, st_ret, st_pre


_ANY = pl.BlockSpec(memory_space=pl.ANY)
_VMEM = pl.BlockSpec(memory_space=pltpu.VMEM)
_OFFSETS = tuple((dx, dy, dc) for dx in (0, 1) for dy in (0, 1) for dc in (0, 1))[1:]
_CHIP_OFFSETS = ((1, 0), (0, 1), (1, 1))


def _place():
    return lax.axis_index("x"), lax.axis_index("y"), lax.axis_index("c")


def _flip(v, d):
    return 1 - v if d else v


def _gather8_rider(a, in_vmem=True):
    def copies(a_ref, o_ref, send, recv):
        x, y, z = _place()
        me = 4 * x + 2 * y + z
        out = []
        for k, (dx, dy, dc) in enumerate(_OFFSETS):
            peer = (_flip(x, dx), _flip(y, dy), _flip(z, dc))
            landing = o_ref.at[4 * peer[0] + 2 * peer[1] + peer[2]]
            out.append((
                pltpu.make_async_remote_copy(src_ref=a_ref, dst_ref=o_ref.at[me], send_sem=send.at[k],
                                             recv_sem=recv.at[k], device_id=peer, device_id_type=MESH),
                pltpu.make_async_remote_copy(src_ref=a_ref, dst_ref=landing, send_sem=send.at[k],
                                             recv_sem=recv.at[k], device_id=peer, device_id_type=MESH)))
        return me, out

    def start(ins, outs, sems):
        me, cps = copies(ins[0], outs[0], sems[0], sems[1])
        pltpu.make_async_copy(ins[0], outs[0].at[me], sems[2]).start()
        for out_cp, _ in cps:
            out_cp.start()

    def finish(ins, outs, sems):
        me, cps = copies(ins[0], outs[0], sems[0], sems[1])
        for out_cp, in_cp in cps:
            in_cp.wait_recv()
            out_cp.wait_send()
        pltpu.make_async_copy(ins[0], outs[0].at[me], sems[2]).wait()

    spec = [_VMEM] if in_vmem else [_ANY]
    return _Rider([a], [jax.ShapeDtypeStruct((N_DEV,) + a.shape, a.dtype)],
                  [pltpu.SemaphoreType.DMA((7,)), pltpu.SemaphoreType.DMA((7,)), pltpu.SemaphoreType.DMA],
                  start, finish, in_specs=spec, out_specs=spec)


def _merge_riders(*riders):
    ins, outs, sems, in_specs, out_specs, aliases, cuts = [], [], [], [], [], {}, []
    for r in riders:
        cuts.append((len(ins), len(outs), len(sems)))
        aliases.update({len(ins) + i: len(outs) + j for i, j in r.aliases.items()})
        ins += r.ins
        outs += r.out_shapes
        sems += r.sems
        in_specs += r.in_specs
        out_specs += r.out_specs

    def part(r, cut, r_ins, r_outs, r_sems):
        return (r_ins[cut[0]:cut[0] + len(r.ins)], r_outs[cut[1]:cut[1] + len(r.out_shapes)],
                r_sems[cut[2]:cut[2] + len(r.sems)])

    def start(r_ins, r_outs, r_sems):
        for r, cut in zip(riders, cuts):
            r.start(*part(r, cut, r_ins, r_outs, r_sems))

    def finish(r_ins, r_outs, r_sems):
        for r, cut in zip(riders, cuts):
            r.finish(*part(r, cut, r_ins, r_outs, r_sems))

    def middle(r_ins, r_outs, r_sems):
        for r, cut in zip(riders, cuts):
            if r.middle is not None:
                r.middle(*part(r, cut, r_ins, r_outs, r_sems))

    return _Rider(ins, outs, sems, start, finish, aliases=aliases, in_specs=in_specs, out_specs=out_specs,
                  middle=middle if any(r.middle is not None for r in riders) else None)


def _allgather8(a, name):
    return _run_rider(_gather8_rider(a), name)[0]


BF16_TILE_ROWS = 16


def _half(o, slot, which):
    r2 = o.shape[1] // 2
    if r2 % BF16_TILE_ROWS == 0:
        return o.at[slot, pl.ds(which * r2, r2)]
    c2 = o.shape[2] // 2
    assert c2 % LANES == 0
    return o.at[slot, :, pl.ds(which * c2, c2)]


def _gather_send(o_refs, send, recv):
    x, y, z = _place()
    chip = 2 * x + y
    for a, o in enumerate(o_refs):
        r2 = o.shape[1] // 2
        mine = _half(o, chip, z)
        for k, (dx, dy) in enumerate(_CHIP_OFFSETS):
            pltpu.make_async_remote_copy(
                src_ref=mine, dst_ref=mine, send_sem=send.at[a, k], recv_sem=recv.at[a, k],
                device_id=(_flip(x, dx), _flip(y, dy), z), device_id_type=MESH).start()


def _gather_landed(o_refs, send, recv, then=None):
    x, y, z = _place()
    chip = 2 * x + y
    for a, o in enumerate(o_refs):
        for k, (dx, dy) in enumerate(_CHIP_OFFSETS):
            landed = _half(o, 2 * _flip(x, dx) + _flip(y, dy), z)
            pltpu.make_async_remote_copy(
                src_ref=landed, dst_ref=landed, send_sem=send.at[a, k], recv_sem=recv.at[a, k],
                device_id=(_flip(x, dx), _flip(y, dy), z), device_id_type=MESH).wait_recv()
            if then is not None:
                then(a, k, landed)
    for a, o in enumerate(o_refs):
        mine = _half(o, chip, z)
        for k, (dx, dy) in enumerate(_CHIP_OFFSETS):
            pltpu.make_async_remote_copy(
                src_ref=mine, dst_ref=mine, send_sem=send.at[a, k], recv_sem=recv.at[a, k],
                device_id=(_flip(x, dx), _flip(y, dy), z), device_id_type=MESH).wait_send()


def _pass_on(o_refs, fsend, frecv, a, k, landed):
    x, y, z = _place()
    pltpu.make_async_remote_copy(
        src_ref=landed, dst_ref=landed, send_sem=fsend.at[a, k], recv_sem=frecv.at[a, k],
        device_id=(x, y, 1 - z), device_id_type=MESH).start()


def _passed_on(o_refs, fsend, frecv):
    x, y, z = _place()
    for a, o in enumerate(o_refs):
        for k, (dx, dy) in enumerate(_CHIP_OFFSETS):
            other = 2 * _flip(x, dx) + _flip(y, dy)
            got = _half(o, other, 1 - z)
            gave = _half(o, other, z)
            pltpu.make_async_remote_copy(
                src_ref=got, dst_ref=got, send_sem=fsend.at[a, k], recv_sem=frecv.at[a, k],
                device_id=(x, y, 1 - z), device_id_type=MESH).wait_recv()
            pltpu.make_async_remote_copy(
                src_ref=gave, dst_ref=gave, send_sem=fsend.at[a, k], recv_sem=frecv.at[a, k],
                device_id=(x, y, 1 - z), device_id_type=MESH).wait_send()


def _gather_finish(o_refs, send, recv, fsend, frecv):
    _gather_landed(o_refs, send, recv, functools.partial(_pass_on, o_refs, fsend, frecv))
    _passed_on(o_refs, fsend, frecv)


class _Rider:
    def __init__(self, ins, out_shapes, sems, start, finish, aliases=None, in_specs=None, out_specs=None, middle=None):
        self.ins, self.out_shapes, self.sems = list(ins), list(out_shapes), list(sems)
        self.start, self.finish, self.aliases = start, finish, dict(aliases or {})
        self.middle = middle
        self.in_specs = list(in_specs) if in_specs else [_ANY] * len(self.ins)
        self.out_specs = list(out_specs) if out_specs else [_ANY] * len(self.out_shapes)


def _run_rider(rider, name):
    r_in, r_out = len(rider.ins), len(rider.out_shapes)

    def body(*refs):
        ins, outs, sems = refs[:r_in], refs[r_in:r_in + r_out], refs[r_in + r_out:]
        rider.start(ins, outs, sems)
        if rider.middle is not None:
            rider.middle(ins, outs, sems)
        rider.finish(ins, outs, sems)

    return pl.pallas_call(
        body, name=name, in_specs=rider.in_specs, out_specs=rider.out_specs, out_shape=rider.out_shapes,
        input_output_aliases=rider.aliases, scratch_shapes=rider.sems,
    )(*rider.ins)


def _hosted_call(body, args, *, name, grid, in_specs, out_specs, out_shape, scratch_shapes=(), sem, rider=None,
                 prefetch=()):
    scratch_shapes = list(scratch_shapes)
    n_pf, n_in, n_out, n_sc = len(prefetch), len(in_specs), len(out_specs), len(scratch_shapes)
    r_in, r_out = (len(rider.ins), len(rider.out_shapes)) if rider else (0, 0)
    last = tuple(g - 1 for g in grid)

    def hosted(*refs):
        p = 0
        parts = []
        for cnt in (n_pf, n_in, r_in, n_out, r_out, n_sc):
            parts.append(refs[p:p + cnt])
            p += cnt
        pf, ins, r_ins, outs, r_outs, scratch = parts
        sems = refs[p:]
        ids = [pl.program_id(a) for a in range(len(grid))]
        is_first = functools.reduce(jnp.logical_and, [i == 0 for i in ids])
        is_last = functools.reduce(jnp.logical_and, [i == e for i, e in zip(ids, last)])

        @pl.when(is_first)
        def _():
            rider.start(r_ins, r_outs, sems)

        if rider.middle is not None:
            linear = functools.reduce(lambda acc, ig: acc * ig[1] + ig[0], zip(ids, grid), 0)

            @pl.when(linear == math.prod(grid) * 3 // 4)
            def _():
                rider.middle(r_ins, r_outs, sems)

        body(*pf, *ins, *outs, *scratch)

        @pl.when(is_last)
        def _():
            rider.finish(r_ins, r_outs, sems)

    if rider is None:
        kern, all_in, all_out, shapes, scratch, aliases, extra = body, list(in_specs), list(out_specs), list(out_shape), \
            scratch_shapes, {}, []
    else:
        kern, all_in, all_out = hosted, list(in_specs) + rider.in_specs, list(out_specs) + rider.out_specs
        shapes, scratch, extra = list(out_shape) + rider.out_shapes, scratch_shapes + rider.sems, rider.ins
        aliases = {n_pf + n_in + i: n_out + j for i, j in rider.aliases.items()}
        sem = ("arbitrary",) * len(grid)
    if prefetch:
        spec = dict(grid_spec=pltpu.PrefetchScalarGridSpec(
            num_scalar_prefetch=n_pf, grid=grid, in_specs=all_in, out_specs=all_out, scratch_shapes=scratch))
    else:
        spec = dict(grid=grid, in_specs=all_in, out_specs=all_out, scratch_shapes=scratch)
    res = pl.pallas_call(kern, name=name, out_shape=shapes, input_output_aliases=aliases,
                         compiler_params=_params(sem, VMEM_LIMIT), **spec)(*prefetch, *args, *extra)
    return list(res[:n_out]), list(res[n_out:])


def _gather_rider(ws, staged=False):
    n = len(ws)
    shapes = [jax.ShapeDtypeStruct(w.shape, w.dtype) for w in ws]
    sems = [pltpu.SemaphoreType.DMA((n, 3))] * 4
    aliases = {a: a for a in range(n)}

    def start(ins, outs, s):
        _gather_send(outs, s[0], s[1])

    if not staged:
        return _Rider(ws, shapes, sems, start, lambda ins, outs, s: _gather_finish(outs, *s), aliases=aliases)
    return _Rider(
        ws, shapes, sems, start, lambda ins, outs, s: _passed_on(outs, s[2], s[3]), aliases=aliases,
        middle=lambda ins, outs, s: _gather_landed(outs, s[0], s[1], functools.partial(_pass_on, outs, s[2], s[3])))


def _gather_ici_rider(ws):
    n = len(ws)
    return _Rider(
        ws, [jax.ShapeDtypeStruct(w.shape, w.dtype) for w in ws], [pltpu.SemaphoreType.DMA((n, 3))] * 2,
        lambda ins, outs, sems: _gather_send(outs, sems[0], sems[1]),
        lambda ins, outs, sems: _gather_landed(outs, sems[0], sems[1]),
        aliases={a: a for a in range(n)})


def _gather_d2d_rider(ws):
    n = len(ws)

    def start(ins, outs, sems):
        x, y, z = _place()
        for a, o in enumerate(outs):
            for k, (dx, dy) in enumerate(_CHIP_OFFSETS):
                _pass_on(outs, sems[0], sems[1], a, k, _half(o, 2 * _flip(x, dx) + _flip(y, dy), z))

    return _Rider(
        ws, [jax.ShapeDtypeStruct(w.shape, w.dtype) for w in ws], [pltpu.SemaphoreType.DMA((n, 3))] * 2,
        start, lambda ins, outs, sems: _passed_on(outs, sems[0], sems[1]), aliases={a: a for a in range(n)})


def _copies_rider(ins, out_shapes, sem_shape, make):
    def start(r_ins, r_outs, sems):
        for cp in make(r_ins, r_outs, sems[0], sems[1]):
            cp.start()

    def finish(r_ins, r_outs, sems):
        for cp in make(r_ins, r_outs, sems[0], sems[1]):
            cp.wait()

    return _Rider(ins, out_shapes, [pltpu.SemaphoreType.DMA(sem_shape)] * 2, start, finish)


def _exchange_rider(gs):
    def make(g_refs, r_refs, send, recv):
        x, y, z = _place()
        return [pltpu.make_async_remote_copy(
            src_ref=g.at[:, pl.ds((1 - z) * (g.shape[1] // 2), g.shape[1] // 2)], dst_ref=r, send_sem=send.at[a],
            recv_sem=recv.at[a], device_id=(x, y, 1 - z), device_id_type=MESH)
            for a, (g, r) in enumerate(zip(g_refs, r_refs))]

    shapes = [jax.ShapeDtypeStruct((g.shape[0], g.shape[1] // 2, g.shape[2]), g.dtype) for g in gs]
    return _copies_rider(gs, shapes, (len(gs),), make)


def _add_half(g, recv, core, name):
    s, r, c = g.shape
    r2 = r // 2
    rb = r2
    for cand in (256, 128, 64):
        if r2 % cand == 0:
            rb = cand
            break
    g4 = g.reshape(s, 2, r2, c)

    def body(core_ref, g_ref, r_ref, o_ref):
        o_ref[...] = (g_ref[...].astype(F32) + r_ref[...].astype(F32)).astype(BF16)

    return pl.pallas_call(
        body, name=name,
        grid_spec=pltpu.PrefetchScalarGridSpec(
            num_scalar_prefetch=1, grid=(s, r2 // rb),
            in_specs=[pl.BlockSpec((None, None, rb, c), lambda i, j, cr: (i, cr[0], j, 0)),
                      pl.BlockSpec((None, rb, c), lambda i, j, cr: (i, j, 0))],
            out_specs=pl.BlockSpec((None, rb, c), lambda i, j, cr: (i, j, 0))),
        out_shape=jax.ShapeDtypeStruct((s, r2, c), BF16),
        compiler_params=_params(("parallel", "parallel")),
    )(core, g4, recv)


def _scatter_rider(ps):
    def make(p_refs, o_refs, send, recv):
        x, y, z = _place()
        copies = []
        for a, (p, o) in enumerate(zip(p_refs, o_refs)):
            for k, (dx, dy) in enumerate(_CHIP_OFFSETS):
                other = 2 * _flip(x, dx) + _flip(y, dy)
                copies.append(pltpu.make_async_remote_copy(
                    src_ref=p.at[other], dst_ref=o.at[k], send_sem=send.at[a, k], recv_sem=recv.at[a, k],
                    device_id=(_flip(x, dx), _flip(y, dy), z), device_id_type=MESH))
        return copies

    shapes = [jax.ShapeDtypeStruct((3,) + p.shape[1:], p.dtype) for p in ps]
    return _copies_rider(ps, shapes, (len(ps), 3), make)


def _sum_chips(p, landed, chip, name):
    _, r2, c = p.shape
    rb = r2
    for cand in (256, 128, 64):
        if r2 % cand == 0:
            rb = cand
            break

    def body(s_ref, p_ref, l_ref, o_ref):
        acc = p_ref[...].astype(F32)
        for k in range(3):
            acc = acc + l_ref[k].astype(F32)
        o_ref[...] = acc

    return pl.pallas_call(
        body, name=name,
        grid_spec=pltpu.PrefetchScalarGridSpec(
            num_scalar_prefetch=1, grid=(r2 // rb,),
            in_specs=[pl.BlockSpec((None, rb, c), lambda i, s: (s[0], i, 0)),
                      pl.BlockSpec((3, rb, c), lambda i, s: (0, i, 0))],
            out_specs=pl.BlockSpec((rb, c), lambda i, s: (i, 0))),
        out_shape=jax.ShapeDtypeStruct((r2, c), F32),
        compiler_params=_params(("parallel",)),
    )(chip, p, landed)


def _swap_rider(hs):
    def make(h_refs, o_refs, send, recv):
        x, y, z = _place()
        return [pltpu.make_async_remote_copy(
            src_ref=h, dst_ref=o, send_sem=send.at[a], recv_sem=recv.at[a], device_id=(x, y, 1 - z),
            device_id_type=MESH) for a, (h, o) in enumerate(zip(h_refs, o_refs))]

    return _copies_rider(hs, [jax.ShapeDtypeStruct(h.shape, h.dtype) for h in hs], (len(hs),), make)


def _reduce_scatter_vmem(gs, rows, rider, name):
    n = len(gs)
    r_in, r_out = len(rider.ins), len(rider.out_shapes)
    halves = [(r // 2, g.shape[-1]) for g, (r, _) in zip(gs, rows)]

    def body(*refs):
        p = 0
        parts = []
        for cnt in (n, r_in, n, n, r_out, n, n, n, 6):
            parts.append(refs[p:p + cnt])
            p += cnt
        g_refs, r_ins, mine, theirs, r_outs, recv, part, land, sems = parts
        r_sems = refs[p:]
        xs, xr, ss, sr, ws, wr = sems
        x, y, z = _place()
        chip = 2 * x + y
        sib = (x, y, 1 - z)
        rider.start(r_ins, r_outs, r_sems)

        def half_of(a, s, which):
            r2 = halves[a][0]
            if len(g_refs[a].shape) == 3:
                return g_refs[a].at[s, pl.ds(pl.multiple_of(which * r2, 8), r2)]
            return g_refs[a].at[pl.ds(pl.multiple_of(s * rows[a][1] + which * r2, 8), r2)]

        exchange = [pltpu.make_async_remote_copy(
            src_ref=half_of(a, s, 1 - z), dst_ref=recv[a].at[s], send_sem=xs.at[a, s], recv_sem=xr.at[a, s],
            device_id=sib, device_id_type=MESH) for a in range(n) for s in range(N_CHIPS)]
        for cp in exchange:
            cp.start()
        for cp in exchange:
            cp.wait()
        for a in range(n):
            for s in range(N_CHIPS):
                part[a][s] = (half_of(a, s, z)[...] + recv[a][s]).astype(BF16)
        scatter = []
        for a in range(n):
            for k, (dx, dy) in enumerate(_CHIP_OFFSETS):
                other = 2 * _flip(x, dx) + _flip(y, dy)
                scatter.append(pltpu.make_async_remote_copy(
                    src_ref=part[a].at[other], dst_ref=land[a].at[k], send_sem=ss.at[a, k], recv_sem=sr.at[a, k],
                    device_id=(_flip(x, dx), _flip(y, dy), z), device_id_type=MESH))
        for cp in scatter:
            cp.start()
        for cp in scatter:
            cp.wait()
        for a in range(n):
            acc = part[a][chip].astype(F32)
            for k in range(3):
                acc = acc + land[a][k].astype(F32)
            mine[a][...] = acc
        swap = [pltpu.make_async_remote_copy(
            src_ref=mine[a], dst_ref=theirs[a], send_sem=ws.at[a], recv_sem=wr.at[a], device_id=sib,
            device_id_type=MESH) for a in range(n)]
        for cp in swap:
            cp.start()
        for cp in swap:
            cp.wait()
        rider.finish(r_ins, r_outs, r_sems)

    half_shapes = [jax.ShapeDtypeStruct(h, F32) for h in halves]
    res = pl.pallas_call(
        body, name=name, in_specs=[_VMEM] * n + rider.in_specs, out_specs=[_VMEM] * (2 * n) + rider.out_specs,
        out_shape=half_shapes + half_shapes + rider.out_shapes,
        scratch_shapes=[pltpu.VMEM((N_CHIPS,) + h, F32) for h in halves] + [pltpu.VMEM((N_CHIPS,) + h, BF16) for h in halves]
        + [pltpu.VMEM((3,) + h, BF16) for h in halves]
        + [pltpu.SemaphoreType.DMA((n, N_CHIPS))] * 2 + [pltpu.SemaphoreType.DMA((n, 3))] * 2
        + [pltpu.SemaphoreType.DMA((n,))] * 2 + rider.sems,
        input_output_aliases={n + i: 2 * n + j for i, j in rider.aliases.items()},
        compiler_params=_params(None, VMEM_LIMIT),
    )(*gs, *rider.ins)
    return list(res[:n]), list(res[n:2 * n]), list(res[2 * n:])


SMALL_ROWS = 32
PACK_ROWS = 16


def _pack_small(st_post, st_ret, st_pre):
    d = st_post.shape[2]

    def body(po_ref, re_ref, pr_ref, o_ref):
        o_ref[...] = jnp.zeros(o_ref.shape, F32)
        o_ref[0:1, :] = pr_ref[0, 2:3, :] + pr_ref[1, 2:3, :] + pr_ref[2, 2:3, :]
        o_ref[1:2, :] = po_ref[0, 4:5, :] + po_ref[1, 4:5, :]
        o_ref[2:3, :] = po_ref[0, 5:6, :] + po_ref[1, 5:6, :]
        o_ref[3:4, 0:512] = re_ref[0, 0:1, :] + re_ref[1, 0:1, :]
        o_ref[4:5, :] = pr_ref[0, 3:4, :] + pr_ref[1, 3:4, :] + pr_ref[2, 3:4, :]
        o_ref[5:6, :] = pr_ref[0, 4:5, :] + pr_ref[1, 4:5, :] + pr_ref[2, 4:5, :]
        lane = lax.broadcasted_iota(jnp.int32, (1, LANES), 1)
        for row, src in ((6, 1), (10, 2)):
            acc = jnp.zeros((1, LANES), F32)
            for hd in range(HEADS):
                grp = re_ref[0, src:src + 1, hd * LANES:(hd + 1) * LANES] + re_ref[1, src:src + 1, hd * LANES:(hd + 1) * LANES]
                acc = acc + jnp.where(lane == hd, grp, 0.0)
            o_ref[row:row + 1, 0:LANES] = acc
        o_ref[7:8, :] = po_ref[0, 6:7, :] + po_ref[1, 6:7, :]
        o_ref[8:9, :] = pr_ref[2, 0:1, :]
        o_ref[9:10, :] = pr_ref[2, 1:2, :]
        for e in range(2):
            b = 12 + 6 * e
            o_ref[b:b + 1, :] = pr_ref[e, 0:1, :]
            o_ref[b + 1:b + 2, :] = pr_ref[e, 1:2, :]
            o_ref[b + 2:b + 3, :] = po_ref[e, 3:4, :]
            o_ref[b + 3:b + 4, :] = po_ref[e, 0:1, :]
            o_ref[b + 4:b + 5, :] = po_ref[e, 1:2, :]
            o_ref[b + 5:b + 6, :] = po_ref[e, 2:3, :]

    return pl.pallas_call(body, name="pack_small", out_shape=jax.ShapeDtypeStruct((SMALL_ROWS, d), F32))(st_post, st_ret, st_pre)


def _small_reduce(gathered):
    d = gathered.shape[2]

    def body(g_ref, o_ref):
        tot = g_ref[0, 0:PACK_ROWS, :]
        for dev in range(1, N_DEV):
            tot = tot + g_ref[dev, 0:PACK_ROWS, :]
        o_ref[0:PACK_ROWS, :] = tot
        for j in range(6):
            acc = g_ref[0, 12 + j:13 + j, :] + g_ref[0, 18 + j:19 + j, :]
            for dev in range(1, N_DEV):
                acc = acc + g_ref[dev, 12 + j:13 + j, :] + g_ref[dev, 18 + j:19 + j, :]
            if j < 2:
                acc = acc + o_ref[8 + j:9 + j, :]
            o_ref[PACK_ROWS + j:PACK_ROWS + j + 1, :] = acc
        o_ref[PACK_ROWS + 6:PACK_ROWS + 8, :] = jnp.zeros((2, d), F32)

    return pl.pallas_call(body, name="small_reduce", out_shape=jax.ShapeDtypeStruct((PACK_ROWS + 8, d), F32))(gathered)


_SMALL = (("g_attn", 0, 1024), ("g_ffn", 1, 1024), ("g_final", 2, 1024), ("g_ret", 3, 512), ("g_q_lora", 4, 384),
          ("g_kv_lora", 5, 256), ("ret_decay_fwd", 6, HEADS), ("ret_decay_bwd", 10, HEADS))
_SMALL_NAMES = tuple(s[0] for s in _SMALL) + ("c_ctx", "b_ada")


def _small_final(tot, dcc, sg8, ws, ms, vs):
    d = tot.shape[1]
    n = len(_SMALL_NAMES)

    def body(*refs):
        t_ref, dcc_ref, sg_ref = refs[0:3]
        w_refs, m_refs, v_refs = refs[3:3 + n], refs[3 + n:3 + 2 * n], refs[3 + 2 * n:3 + 3 * n]
        outs = refs[3 + 3 * n:]
        g_refs, d_refs, mo_refs, vo_refs = outs[0:n], outs[n:2 * n], outs[2 * n:3 * n], outs[3 * n:4 * n]
        l_ref = outs[4 * n]

        def update(i, g, sl=None):
            pick = (lambda r: r[...]) if sl is None else (lambda r: r[:, sl])
            dl, mn, vn = _adam_math(pick(w_refs[i]), g, pick(m_refs[i]), pick(v_refs[i]))
            if sl is None:
                g_refs[i][...], d_refs[i][...], mo_refs[i][...], vo_refs[i][...] = g, dl, mn, vn
            else:
                g_refs[i][:, sl], d_refs[i][:, sl], mo_refs[i][:, sl], vo_refs[i][:, sl] = g, dl, mn, vn

        for i, (name, row, width) in enumerate(_SMALL):
            g = t_ref[row:row + 1, 0:width]
            if name == "ret_decay_fwd":
                g = g * sg_ref[0:1, 0:width]
            elif name == "ret_decay_bwd":
                g = g * sg_ref[1:2, 0:width]
            update(i, g)
        i_cc, i_b = n - 2, n - 1
        cc = w_refs[i_cc][...]
        s = 1.0 / (1.0 + jnp.exp(-cc))
        dsilu = dcc_ref[0, 0:1, :] + dcc_ref[2, 0:1, :] + dcc_ref[4, 0:1, :] + dcc_ref[6, 0:1, :]
        update(i_cc, dsilu * (s * (1.0 + cc * (1.0 - s))))
        for j in range(6):
            update(i_b, t_ref[PACK_ROWS + j:PACK_ROWS + j + 1, :], pl.ds(j * d, d))
        l_ref[...] = jnp.broadcast_to((0.5 / d) * jnp.sum(t_ref[7:8, :], keepdims=True), l_ref.shape)

    shapes = [jax.ShapeDtypeStruct(a.shape, F32) for a in ws]
    outs = pl.pallas_call(
        body, name="small_final", out_shape=shapes * 4 + [jax.ShapeDtypeStruct((8, LANES), F32)],
    )(tot, dcc, sg8, *ws, *ms, *vs)
    return outs[0:n], outs[n:2 * n], outs[2 * n:3 * n], outs[3 * n:4 * n], outs[4 * n]


_WEIGHTS = ("c_ctx", "w_ada", "b_ada", "g_attn", "g_ffn", "w_in", "ret_decay_fwd", "ret_decay_bwd", "g_ret", "g_q_lora",
            "w_uq", "g_kv_lora", "w_ukv", "w_out", "w_ff1", "w_ff2", "g_final")
_BIG = ("w_in", "w_uq", "w_ukv", "w_out", "w_ff1", "w_ff2")
_TRANSPOSED = ("w_in", "w_uq")


def kernel(x, c, ctx, c_ctx, w_ada, b_ada, g_attn, g_ffn, w_in, ret_decay_fwd, ret_decay_bwd, g_ret, g_q_lora, w_uq, g_kv_lora, w_ukv, w_out, w_ff1, w_ff2, g_final, loss_target, m_c_ctx, m_w_ada, m_b_ada, m_g_attn, m_g_ffn, m_w_in, m_ret_decay_fwd, m_ret_decay_bwd, m_g_ret, m_g_q_lora, m_w_uq, m_g_kv_lora, m_w_ukv, m_w_out, m_w_ff1, m_w_ff2, m_g_final, v_c_ctx, v_w_ada, v_b_ada, v_g_attn, v_g_ffn, v_w_in, v_ret_decay_fwd, v_ret_decay_bwd, v_g_ret, v_g_q_lora, v_w_uq, v_g_kv_lora, v_w_ukv, v_w_out, v_w_ff1, v_w_ff2, v_g_final):
    w = dict(c_ctx=c_ctx, w_ada=w_ada, b_ada=b_ada, g_attn=g_attn, g_ffn=g_ffn, w_in=w_in, ret_decay_fwd=ret_decay_fwd,
             ret_decay_bwd=ret_decay_bwd, g_ret=g_ret, g_q_lora=g_q_lora, w_uq=w_uq, g_kv_lora=g_kv_lora, w_ukv=w_ukv,
             w_out=w_out, w_ff1=w_ff1, w_ff2=w_ff2, g_final=g_final)
    m = dict(c_ctx=m_c_ctx, w_ada=m_w_ada, b_ada=m_b_ada, g_attn=m_g_attn, g_ffn=m_g_ffn, w_in=m_w_in,
             ret_decay_fwd=m_ret_decay_fwd, ret_decay_bwd=m_ret_decay_bwd, g_ret=m_g_ret, g_q_lora=m_g_q_lora, w_uq=m_w_uq,
             g_kv_lora=m_g_kv_lora, w_ukv=m_w_ukv, w_out=m_w_out, w_ff1=m_w_ff1, w_ff2=m_w_ff2, g_final=m_g_final)
    v = dict(c_ctx=v_c_ctx, w_ada=v_w_ada, b_ada=v_b_ada, g_attn=v_g_attn, g_ffn=v_g_ffn, w_in=v_w_in,
             ret_decay_fwd=v_ret_decay_fwd, ret_decay_bwd=v_ret_decay_bwd, g_ret=v_g_ret, g_q_lora=v_g_q_lora, w_uq=v_w_uq,
             g_kv_lora=v_g_kv_lora, w_ukv=v_w_ukv, w_out=v_w_out, w_ff1=v_w_ff1, w_ff2=v_w_ff2, g_final=v_g_final)
    xi, yi, ci = lax.axis_index("x"), lax.axis_index("y"), lax.axis_index("c")
    chip = 2 * xi + yi
    dev = 2 * chip + ci
    nex, seq, d = x.shape
    n_ada = w_ada.shape[2]

    dec = jnp.zeros((8, LANES), F32).at[0, :HEADS].set(ret_decay_fwd[0]).at[1, :HEADS].set(ret_decay_bwd[0])
    lg8, sg8 = _decay_prep(dec)
    lg = lg8[:2, :HEADS]

    def shard_of(t, k):
        return t[k][0].T if k in _TRANSPOSED else t[k][0]

    shard = {k: shard_of(w, k) for k in _BIG}
    head_rows = MLA_NOPE + MLA_ROPE
    shard["w_uq"] = jnp.pad(shard["w_uq"], ((0, MLA_HEAD - head_rows), (0, 0)))
    slot = chip.reshape(1).astype(jnp.int32)
    core = ci.reshape(1).astype(jnp.int32)
    slots = {k: _cast_into_slot(shard[k], slot, "cast_" + k) for k in _EARLY}
    half_ff = shard["w_ff2"].shape[0] // 2
    late_pieces = [(shard["w_out"], 0, shard["w_out"].shape[0]), (shard["w_ff1"], 0, shard["w_ff1"].shape[0]),
                   (shard["w_ff2"], 0, half_ff), (shard["w_ff2"], half_ff, half_ff)]
    late_slots, (w_in_x, w_uq_x, w_ukv_x, c8) = _cast_into_slots(
        late_pieces, slot, "cast_late",
        rider=_merge_riders(_gather_ici_rider([slots[k] for k in _EARLY]),
                            _gather8_rider(jnp.pad(c, ((0, 8 - nex), (0, 0))), in_vmem=False)))

    a_in = jnp.concatenate([c8[:, :nex].reshape(N_DEV * nex, d), c_ctx.reshape(1, d), jnp.zeros((7, d), F32)], axis=0)
    b_sh = lax.dynamic_slice(b_ada, (0, chip * n_ada), (1, n_ada))
    mod_sh = _mod_fwd(a_in, w_ada[0], b_sh)
    mod8, w_in_f, w_uq_k, w_ukv_k = _run_rider(
        _merge_riders(_gather8_rider(mod_sh), _gather_d2d_rider([w_in_x, w_uq_x, w_ukv_x])), "ag_early")
    w_in_k = jnp.pad(w_in_f.reshape(IN_COLS, d), ((0, IN_PAD - IN_COLS), (0, 0)))
    mod_all = mod8[0::2].transpose(1, 0, 2).reshape(a_in.shape[0], N_CHIPS * n_ada)
    mod_me = lax.dynamic_slice(mod_all, (nex * dev, 0), (nex, N_CHIPS * n_ada)).reshape(nex, 6, d)
    mod_c = mod_all[N_DEV * nex].reshape(1, 6, d)
    modv = jnp.pad(jnp.concatenate([mod_me, mod_c], axis=0), ((0, 0), (0, 2), (0, 0)))

    gx, g_early, late, st_post, st_ret, st_pre = _local_step(
        x, ctx, loss_target, modv, lg, g_attn, g_ffn, g_final.reshape(1, d), g_ret, g_q_lora, g_kv_lora,
        w_in_k, w_uq_k, w_ukv_k, late_slots, (core, slot))

    mine, theirs, (*late_theirs, gathered) = _reduce_scatter_vmem(
        g_early, [(IN_COLS // N_CHIPS, IN_COLS // N_CHIPS), (head_rows, MLA_HEAD), (KV_LORA, KV_LORA)],
        _merge_riders(_swap_rider(late), _gather8_rider(_pack_small(st_post, st_ret, st_pre))), "rs_early")
    tot = _small_reduce(gathered)
    dm = jnp.concatenate([
        gathered[:, 12:24].reshape(N_DEV * nex, 6 * d),
        jnp.concatenate([tot[8:10].reshape(1, 2 * d), jnp.zeros((1, 4 * d), F32)], axis=1),
        jnp.zeros((7, 6 * d), F32)], axis=0)
    dm_sh = lax.dynamic_slice(dm, (0, chip * n_ada), (dm.shape[0], n_ada))
    g_ada, da = _mod_bwd(a_in, dm_sh, w_ada[0])
    dcc = _allgather8(da[N_DEV * nex:], "ag_dcc")
    halves = dict(zip(_EARLY, zip(mine, theirs)))
    halves.update(zip(_LATE, zip(late, late_theirs)))
    grad, delta, new_m, new_v = {}, {}, {}, {}
    for k in _BIG:
        a, b = halves[k]
        res = _adamw_halves(shard_of(w, k), a, b, shard_of(m, k), shard_of(v, k), core, "adamw_" + k)
        grad[k], delta[k], new_m[k], new_v[k] = [(o.T if k in _TRANSPOSED else o).reshape(w[k].shape) for o in res]

    shp = w_ada.shape
    outs, _ = _adamw(w_ada[0], g_ada, m["w_ada"][0], v["w_ada"][0], "adamw_w_ada")
    grad["w_ada"] = g_ada.reshape(shp)
    delta["w_ada"], new_m["w_ada"], new_v["w_ada"] = [o.reshape(shp) for o in outs]
    rows = [{k: t[k].reshape(1, -1) for k in _SMALL_NAMES} for t in (w, m, v)]
    small = _small_final(tot, dcc, sg8, *[[t[k] for k in _SMALL_NAMES] for t in rows])
    for res, outs in zip((grad, delta, new_m, new_v), small[:4]):
        for k, o in zip(_SMALL_NAMES, outs):
            res[k] = o.reshape(w[k].shape)
    return (small[4][0, 0], gx, *[grad[k] for k in _WEIGHTS], *[delta[k] for k in _WEIGHTS],
            *[new_m[k] for k in _WEIGHTS], *[new_v[k] for k in _WEIGHTS])
```

```python
import functools
import math

import jax
import jax.numpy as jnp
from jax import lax
from jax.experimental import pallas as pl
from jax.experimental.pallas import tpu as pltpu

F32 = jnp.float32
BF16 = jnp.bfloat16
MESH = pl.DeviceIdType.MESH

EPS = 1e-6
D_MODEL = 1024
D_FF = 4096
HEADS = 4
RET_DK = 64
RET_DV = 128
MLA_NOPE = 128
MLA_ROPE = 64
MLA_HEAD = 256
Q_LORA = 384
KV_LORA = 256
GRID_W = 64
ROPE_BASE = 10000.0
IN_COLS = 2240
IN_PAD = 2304
PG_COLS = 1152
N_CHIPS = 4
N_DEV = 8
LANES = 128
ADAM_LR = 0.001
ADAM_B1 = 0.9
ADAM_B2 = 0.999
ADAM_EPS = 1e-08
ADAM_WD = 0.01
ADAM_STEP = 10
VMEM_LIMIT = 56 * 1024 * 1024


def _dot(a, b):
    return jnp.dot(a, b, preferred_element_type=F32)


def _dot_nt(a, b):
    return lax.dot_general(a, b, (((1,), (1,)), ((), ())), preferred_element_type=F32)


def _dot_tn(a, b):
    return lax.dot_general(a, b, (((0,), (0,)), ((), ())), preferred_element_type=F32)


def _params(sem=None, vmem=None):
    return pltpu.CompilerParams(dimension_semantics=sem, vmem_limit_bytes=vmem)


def _full(shape):
    n = len(shape)
    return pl.BlockSpec(shape, lambda *_: (0,) * n)


def _once(shape):
    n = len(shape)
    return pl.BlockSpec(shape, lambda *_: (0,) * n, pipeline_mode=pl.Buffered(1))


def _rope(x, cos, sin):
    w = x.shape[-1]
    lo = (lax.broadcasted_iota(jnp.int32, (1, w), 1) % 64) < 32
    swapped = jnp.where(lo, pltpu.roll(x, w - 32, 1), pltpu.roll(x, 32, 1))
    return x * cos + swapped * sin


def _rope_t(g, cos, sin):
    w = g.shape[-1]
    lo = (lax.broadcasted_iota(jnp.int32, (1, w), 1) % 64) < 32
    t = g * sin
    swapped = jnp.where(lo, pltpu.roll(t, w - 32, 1), pltpu.roll(t, 32, 1))
    return g * cos + swapped


def _rope_tables(seq, tm):
    rows = seq // GRID_W
    row = jnp.repeat(jnp.arange(rows, dtype=F32), GRID_W)
    col = jnp.tile(jnp.arange(GRID_W, dtype=F32), rows)
    n_freq = RET_DK // 4
    freq = ROPE_BASE ** (-jnp.arange(n_freq, dtype=F32) / n_freq)
    ang = jnp.concatenate([row[:, None] * freq, col[:, None] * freq], axis=-1)
    cos, sin = jnp.cos(ang), jnp.sin(ang)
    cos_t = jnp.tile(jnp.concatenate([cos, cos], -1), (1, HEADS))
    sin_t = jnp.tile(jnp.concatenate([-sin, sin], -1), (1, HEADS))
    cos_t = jnp.concatenate([cos_t, jnp.ones((tm, 4 * RET_DK), F32)], 0)
    sin_t = jnp.concatenate([sin_t, jnp.zeros((tm, 4 * RET_DK), F32)], 0)
    return cos_t, sin_t


def _adam_math(w, g, m, v):
    mn = ADAM_B1 * m + (1.0 - ADAM_B1) * g
    vn = ADAM_B2 * v + (1.0 - ADAM_B2) * (g * g)
    m_hat = mn / (1.0 - ADAM_B1 ** ADAM_STEP)
    v_hat = vn / (1.0 - ADAM_B2 ** ADAM_STEP)
    return -ADAM_LR * (m_hat / (jnp.sqrt(v_hat) + ADAM_EPS) + ADAM_WD * w), mn, vn


def _cast_into_slots(pieces, slot, name, rider=None):
    c = pieces[0][0].shape[1]
    rb = max(b for b in range(16, 257, 16) if all(cnt % b == 0 and st % b == 0 for _, st, cnt in pieces))
    nbs = [cnt // rb for _, _, cnt in pieces]
    starts = [sum(nbs[:s]) for s in range(len(pieces))]

    def body(s_ref, *refs):
        i = pl.program_id(0)
        for s in range(len(pieces)):
            @pl.when(jnp.logical_and(i >= starts[s], i < starts[s] + nbs[s]))
            def _():
                refs[len(pieces) + s][...] = refs[s][...].astype(BF16)

    in_specs, out_specs = [], []
    for (_, first_row, _), nb, st in zip(pieces, nbs, starts):
        in_specs.append(pl.BlockSpec((rb, c), lambda i, s, nb=nb, st=st, f=first_row // rb: (f + jnp.clip(i - st, 0, nb - 1), 0)))
        out_specs.append(pl.BlockSpec((None, rb, c), lambda i, s, nb=nb, st=st: (s[0], jnp.clip(i - st, 0, nb - 1), 0)))
    return _hosted_call(
        body, [w for w, _, _ in pieces], name=name, grid=(sum(nbs),), prefetch=(slot,), in_specs=in_specs,
        out_specs=out_specs, out_shape=[jax.ShapeDtypeStruct((N_CHIPS, cnt, c), BF16) for _, _, cnt in pieces],
        sem=("arbitrary",), rider=rider)


def _cast_into_slot(w, slot, name):
    return _cast_into_slots([(w, 0, w.shape[0])], slot, name)[0][0]


def _adamw_halves(w, mine, theirs, m, v, core, name):
    r, c = w.shape
    r2 = r // 2
    rb = max(b for b in range(8, r2 + 1, 8) if r2 % b == 0 and b * c * 4 <= (1 << 21))
    nbh = r2 // rb

    def body(z_ref, w_ref, a_ref, b_ref, m_ref, v_ref, g_ref, d_ref, mo_ref, vo_ref):
        here = (pl.program_id(0) // nbh) == z_ref[0]
        gg = jnp.where(here, a_ref[...], b_ref[...])
        g_ref[...] = gg
        d_ref[...], mo_ref[...], vo_ref[...] = _adam_math(w_ref[...], gg, m_ref[...], v_ref[...])

    spec = pl.BlockSpec((rb, c), lambda i, z: (i, 0))
    a_spec = pl.BlockSpec((rb, c), lambda i, z: (jnp.clip(i - z[0] * nbh, 0, nbh - 1), 0))
    b_spec = pl.BlockSpec((rb, c), lambda i, z: (jnp.clip(i - (1 - z[0]) * nbh, 0, nbh - 1), 0))
    shp = jax.ShapeDtypeStruct((r, c), F32)
    return pl.pallas_call(
        body, name=name,
        grid_spec=pltpu.PrefetchScalarGridSpec(
            num_scalar_prefetch=1, grid=(r // rb,), in_specs=[spec, a_spec, b_spec, spec, spec], out_specs=[spec] * 4),
        out_shape=[shp] * 4,
        compiler_params=_params(("parallel",)),
    )(core, w, mine, theirs, m, v)


def _adamw(w, g, m, v, name, rider=None):
    r, c = w.shape
    rb = r
    for cand in (256, 128, 64, 32, 16, 8):
        if r % cand == 0 and cand * c * 4 <= (1 << 20):
            rb = cand
            break
    if r * c * 4 <= (1 << 20):
        rb = r

    def body(w_ref, g_ref, m_ref, v_ref, d_ref, mo_ref, vo_ref):
        d_ref[...], mo_ref[...], vo_ref[...] = _adam_math(w_ref[...], g_ref[...], m_ref[...], v_ref[...])

    spec = pl.BlockSpec((rb, c), lambda i: (i, 0))
    shp = jax.ShapeDtypeStruct((r, c), F32)
    return _hosted_call(
        body, (w, g, m, v), name=name, grid=(r // rb,), in_specs=[spec] * 4, out_specs=[spec] * 3, out_shape=[shp] * 3,
        sem=("parallel",), rider=rider)


def _decay_prep(dec):
    def body(d_ref, lg_ref, sg_ref):
        d = d_ref[...]
        lg_ref[...] = jnp.minimum(d, 0.0) - jnp.log(1.0 + jnp.exp(-jnp.abs(d)))
        sg_ref[...] = 1.0 / (1.0 + jnp.exp(d))

    shp = jax.ShapeDtypeStruct(dec.shape, F32)
    return pl.pallas_call(body, name="decay_prep", out_shape=[shp, shp])(dec)


def _mod_fwd(a_in, w_ada, b_sh):
    rows, d = a_in.shape
    n = w_ada.shape[1]
    bn = 512

    def body(a_ref, w_ref, b_ref, o_ref):
        a = a_ref[...]
        s = (a / (1.0 + jnp.exp(-a))).astype(BF16)
        o_ref[...] = _dot(s, w_ref[...].astype(BF16)) + b_ref[...]

    return pl.pallas_call(
        body, name="mod_fwd", grid=(n // bn,),
        in_specs=[_full((rows, d)), pl.BlockSpec((d, bn), lambda j: (0, j)), pl.BlockSpec((1, bn), lambda j: (0, j))],
        out_specs=pl.BlockSpec((rows, bn), lambda j: (0, j)),
        out_shape=jax.ShapeDtypeStruct((rows, n), F32),
        compiler_params=_params(("parallel",)),
    )(a_in, w_ada, b_sh)


def _mod_bwd(a_in, dm, w_ada):
    rows, d = a_in.shape
    n = w_ada.shape[1]
    bn = 512
    nb = n // bn

    def body(a_ref, dm_ref, w_ref, gw_ref, da_ref):
        j = pl.program_id(0)
        a = a_ref[...]
        s = (a / (1.0 + jnp.exp(-a))).astype(BF16)
        dmb = dm_ref[...].astype(BF16)
        gw_ref[...] = _dot_tn(s, dmb)
        part = _dot_nt(dmb, w_ref[...].astype(BF16))

        @pl.when(j == 0)
        def _():
            da_ref[...] = part

        @pl.when(j > 0)
        def _():
            da_ref[...] += part

    return pl.pallas_call(
        body, name="mod_bwd", grid=(nb,),
        in_specs=[_full((rows, d)), pl.BlockSpec((rows, bn), lambda j: (0, j)), pl.BlockSpec((d, bn), lambda j: (0, j))],
        out_specs=[pl.BlockSpec((d, bn), lambda j: (0, j)), _full((rows, d))],
        out_shape=[jax.ShapeDtypeStruct((d, n), F32), jax.ShapeDtypeStruct((rows, d), F32)],
        compiler_params=_params(("arbitrary",)),
    )(a_in, dm, w_ada)


def _pre_fwd(x2, ctx2, modv, g_attn, w_in, g_q, g_kv, w_uq, w_ukv, cos_t, sin_t, *, seq, tm, rider=None):
    t_lat, d = x2.shape
    t_ctx = ctx2.shape[0]
    nl, nc = t_lat // tm, t_ctx // tm
    n_all = t_lat + t_ctx
    tpe = seq // tm
    nex = t_lat // seq

    def body(x_ref, c_ref, mod_ref, g_ref, win_ref, gq_ref, gkv_ref, wuq_ref, wukv_ref, cos_ref, sin_ref,
             h_ref, pg_ref, rq_ref, rk_ref, rv_ref, nq_ref, nkv_ref, q_ref, k_ref, v_ref):
        i = pl.program_id(0)
        xt = jnp.where(i < nl, x_ref[...], c_ref[...])
        sh = mod_ref[0, 0:1, :]
        sc = mod_ref[0, 1:2, :]
        r = lax.rsqrt(jnp.mean(xt * xt, axis=-1, keepdims=True) + EPS)
        hb = ((xt * r) * g_ref[...] * (1.0 + sc) + sh).astype(BF16)
        h_ref[...] = hb
        p = _dot_nt(hb, win_ref[...])
        cos = cos_ref[...]
        sin = sin_ref[...]
        rq_ref[...] = _rope(p[:, 0:256], cos, sin).astype(BF16)
        rk_ref[...] = _rope(p[:, 256:512] * (RET_DK ** -0.5), cos, sin).astype(BF16)
        rv_ref[...] = p[:, 512:1024].astype(BF16)
        pg_ref[...] = p[:, 1024:2176]
        cq = p[:, 1536:1920]
        ckv = p[:, 1920:2176]
        nqb = (cq * lax.rsqrt(jnp.mean(cq * cq, axis=-1, keepdims=True) + EPS) * gq_ref[...]).astype(BF16)
        nkvb = (ckv * lax.rsqrt(jnp.mean(ckv * ckv, axis=-1, keepdims=True) + EPS) * gkv_ref[...]).astype(BF16)
        nq_ref[...] = nqb
        nkv_ref[...] = nkvb
        cos1 = cos[:, 0:LANES]
        sin1 = sin[:, 0:LANES]
        kpe = _rope(p[:, 2176:2304], cos1, sin1).astype(BF16)
        for hd in range(HEADS):
            o = hd * MLA_HEAD
            qh = _dot_nt(nqb, wuq_ref[hd]) * MLA_SCALE
            q_ref[:, o:o + 128] = qh[:, 0:128].astype(BF16)
            q_ref[:, o + 128:o + 256] = _rope(qh[:, 128:256], cos1, sin1).astype(BF16)
            kvh = _dot(nkvb, wukv_ref[hd])
            k_ref[:, o:o + 128] = kvh[:, 0:128].astype(BF16)
            k_ref[:, o + 128:o + 256] = kpe
            v_ref[:, hd * 128:(hd + 1) * 128] = kvh[:, 128:256].astype(BF16)

    def tile(width):
        return pl.BlockSpec((tm, width), lambda i: (i, 0))

    widths = (d, PG_COLS, 256, 256, 512, Q_LORA, KV_LORA, HEADS * MLA_HEAD, HEADS * MLA_HEAD, HEADS * 128)
    dtypes = (BF16, F32, BF16, BF16, BF16, BF16, BF16, BF16, BF16, BF16)
    tab = pl.BlockSpec((tm, 256), lambda i: (jnp.where(i < nl, i % tpe, tpe), 0))
    return _hosted_call(
        body, (x2, ctx2, modv, g_attn, w_in, g_q, g_kv, w_uq, w_ukv, cos_t, sin_t), name="pre_fwd", grid=(nl + nc,),
        in_specs=[
            pl.BlockSpec((tm, d), lambda i: (jnp.minimum(i, nl - 1), 0)),
            pl.BlockSpec((tm, d), lambda i: (jnp.maximum(i - nl, 0), 0)),
            pl.BlockSpec((1, 8, d), lambda i: (jnp.minimum(i // tpe, nex), 0, 0)),
            _full((1, d)), _full(w_in.shape), _full((1, Q_LORA)), _full((1, KV_LORA)),
            _full(w_uq.shape), _full(w_ukv.shape), tab, tab,
        ],
        out_specs=[tile(w) for w in widths],
        out_shape=[jax.ShapeDtypeStruct((n_all, w), dt) for w, dt in zip(widths, dtypes)],
        sem=("parallel",), rider=rider)


def _post(yret, ymla, x2, tgt2, modv, g_ffn, g_fin, w_out, w_ff1, w_ff2a, w_ff2b, *, seq, tm):
    t_lat, d = x2.shape
    nl = t_lat // tm
    tpe = seq // tm
    nex = t_lat // seq
    n_slab = w_ff1.shape[0]
    fs = w_ff1.shape[2]
    fh = w_ff2a.shape[1]

    def body(yr_ref, ym_ref, x_ref, t_ref, mod_ref, gf_ref, gl_ref, wo_ref, w1_ref, w2a_ref, w2b_ref,
             mix_ref, a_ref, du_ref, h2_ref, df_ref, dmo_ref, dmix_ref, dxm_ref, st_ref, ru_ref):
        i = pl.program_id(0)
        gt_a = mod_ref[0, 2:3, :]
        sh_f = mod_ref[0, 3:4, :]
        sc_f = mod_ref[0, 4:5, :]
        gt_f = mod_ref[0, 5:6, :]
        g_ffn_v = gf_ref[...]
        g_fin_v = gl_ref[...]
        yr = yr_ref[...]
        ym = ym_ref[...]
        mix_ref[:, 0:512] = yr
        mix_ref[:, 512:1024] = ym
        op = _dot(yr, wo_ref[0:512, :]) + _dot(ym, wo_ref[512:1024, :])
        x_mid = x_ref[...] + gt_a * op
        r2 = lax.rsqrt(jnp.mean(x_mid * x_mid, axis=-1, keepdims=True) + EPS)
        xh2 = x_mid * r2
        h2b = (xh2 * g_ffn_v * (1.0 + sc_f) + sh_f).astype(BF16)
        h2_ref[...] = h2b
        f = jnp.zeros((tm, d), F32)
        for s in range(n_slab):
            ru = jnp.maximum(_dot(h2b, w1_ref[s]), 0.0)
            ru_ref[:, s * fs:(s + 1) * fs] = ru
            ab = (ru * ru).astype(BF16)
            a_ref[:, s * fs:(s + 1) * fs] = ab
            f = f + _dot(ab[:, 0:fh], w2a_ref[s]) + _dot(ab[:, fh:fs], w2b_ref[s])
        x_out = x_mid + gt_f * f
        r3 = lax.rsqrt(jnp.mean(x_out * x_out, axis=-1, keepdims=True) + EPS)
        xh3 = x_out * r3
        err = xh3 * g_fin_v - t_ref[...]
        dy = err * (1.0 / d)
        dxh3 = dy * g_fin_v
        dx_out = r3 * (dxh3 - xh3 * jnp.mean(dxh3 * xh3, axis=-1, keepdims=True))
        dfb = (dx_out * gt_f).astype(BF16)
        df_ref[...] = dfb
        dh2 = jnp.zeros((tm, d), F32)
        for s in range(n_slab):
            da = jnp.concatenate([_dot_nt(dfb, w2a_ref[s]), _dot_nt(dfb, w2b_ref[s])], axis=1)
            dub = (da * (2.0 * ru_ref[:, s * fs:(s + 1) * fs])).astype(BF16)
            du_ref[:, s * fs:(s + 1) * fs] = dub
            dh2 = dh2 + _dot_nt(dub, w1_ref[s])
        dxh2 = dh2 * (1.0 + sc_f) * g_ffn_v
        dx_mid = dx_out + r2 * (dxh2 - xh2 * jnp.mean(dxh2 * xh2, axis=-1, keepdims=True))
        dxm_ref[...] = dx_mid
        dmob = (dx_mid * gt_a).astype(BF16)
        dmo_ref[...] = dmob
        dmix_ref[...] = _dot_nt(dmob, wo_ref[...]).astype(BF16)

        def rsum(v):
            return jnp.sum(v, axis=0, keepdims=True)

        stats = jnp.concatenate([
            rsum(dh2), rsum(dh2 * xh2 * g_ffn_v), rsum(dx_out * f), rsum(dx_mid * op),
            rsum(dh2 * (1.0 + sc_f) * xh2), rsum(dy * xh3), rsum(err * err), jnp.zeros((1, d), F32)], axis=0)

        @pl.when(i % tpe == 0)
        def _():
            st_ref[0] = stats

        @pl.when(i % tpe != 0)
        def _():
            st_ref[0] += stats

    def tile(width):
        return pl.BlockSpec((tm, width), lambda i: (i, 0))

    widths = (d, D_FF, D_FF, d, d, d, d, d)
    dtypes = (BF16, BF16, BF16, BF16, BF16, BF16, BF16, F32)
    const = pl.Buffered(1)
    return pl.pallas_call(
        body, name="post", grid=(nl,),
        in_specs=[
            tile(512), tile(512), tile(d), tile(d),
            pl.BlockSpec((1, 8, d), lambda i: (i // tpe, 0, 0)),
            _full((1, d)), _full((1, d)),
            pl.BlockSpec(w_out.shape, lambda i: (0, 0), pipeline_mode=const),
            pl.BlockSpec(w_ff1.shape, lambda i: (0, 0, 0), pipeline_mode=const),
            pl.BlockSpec(w_ff2a.shape, lambda i: (0, 0, 0), pipeline_mode=const),
            pl.BlockSpec(w_ff2b.shape, lambda i: (0, 0, 0), pipeline_mode=const),
        ],
        out_specs=[tile(w) for w in widths] + [pl.BlockSpec((1, 8, d), lambda i: (i // tpe, 0, 0))],
        out_shape=[jax.ShapeDtypeStruct((t_lat, w), dt) for w, dt in zip(widths, dtypes)]
        + [jax.ShapeDtypeStruct((nex, 8, d), F32)],
        scratch_shapes=[pltpu.VMEM((tm, D_FF), F32)],
        compiler_params=_params(("arbitrary",), VMEM_LIMIT),
    )(yret, ymla, x2, tgt2, modv, g_ffn, g_fin, w_out, w_ff1, w_ff2a, w_ff2b)


def _pre_bwd(x2, ctx2, modv, g_attn, pg, drq, drk, dkc_r, drv, dvc_r, drg, dq_m, dkl, dkc, dvl, dvc, dxm,
             w_in, g_q, g_kv, w_uq, w_ukv, cos_t, sin_t, *, seq, tm, rider=None):
    t_lat, d = x2.shape
    t_ctx = ctx2.shape[0]
    nl, nc = t_lat // tm, t_ctx // tm
    n_all = t_lat + t_ctx
    tpe = seq // tm
    nex = t_lat // seq

    def body(x_ref, c_ref, mod_ref, g_ref, pg_ref, drq_ref, drk_ref, dkcr_ref, drv_ref, dvcr_ref, drg_ref,
             dq_ref, dkl_ref, dkc_ref, dvl_ref, dvc_ref, dxm_ref, win_ref, gq_ref, gkv_ref, wuq_ref, wukv_ref,
             cos_ref, sin_ref, dpb_ref, dqf_ref, dkvf_ref, gx_ref, st_ref):
        i = pl.program_id(0)
        lat = i < nl
        latf = lat.astype(F32)
        cos = cos_ref[...]
        sin = sin_ref[...]
        cos1 = cos[:, 0:LANES]
        sin1 = sin[:, 0:LANES]
        d_rq = _rope_t(drq_ref[...] * latf, cos, sin)
        d_rk = _rope_t(jnp.where(lat, drk_ref[...], dkcr_ref[...]), cos, sin) * (RET_DK ** -0.5)
        d_rv = jnp.where(lat, drv_ref[...], dvcr_ref[...])
        d_rg = drg_ref[...] * latf
        dq_all = dq_ref[...] * (latf * MLA_SCALE)
        dk_all = jnp.where(lat, dkl_ref[...], dkc_ref[...])
        dv_all = jnp.where(lat, dvl_ref[...], dvc_ref[...])
        dnq = jnp.zeros((tm, Q_LORA), F32)
        dnkv = jnp.zeros((tm, KV_LORA), F32)
        dkpe = jnp.zeros((tm, LANES), F32)
        for hd in range(HEADS):
            o = hd * MLA_HEAD
            dqh = jnp.concatenate([dq_all[:, o:o + 128], _rope_t(dq_all[:, o + 128:o + 256], cos1, sin1)],
                                  axis=1).astype(BF16)
            dqf_ref[:, o:o + 256] = dqh
            dnq = dnq + _dot(dqh, wuq_ref[hd])
            dkpe = dkpe + dk_all[:, o + 128:o + 256]
            dkvh = jnp.concatenate([dk_all[:, o:o + 128], dv_all[:, hd * 128:(hd + 1) * 128]], axis=1).astype(BF16)
            dkvf_ref[:, o:o + 256] = dkvh
            dnkv = dnkv + _dot_nt(dkvh, wukv_ref[hd])
        d_kpe = _rope_t(dkpe, cos1, sin1)
        pgv = pg_ref[...]
        cq = pgv[:, 512:896]
        ckv = pgv[:, 896:1152]
        rq_ = lax.rsqrt(jnp.mean(cq * cq, axis=-1, keepdims=True) + EPS)
        cqh = cq * rq_
        dcqh = dnq * gq_ref[...]
        d_cq = rq_ * (dcqh - cqh * jnp.mean(dcqh * cqh, axis=-1, keepdims=True))
        rkv_ = lax.rsqrt(jnp.mean(ckv * ckv, axis=-1, keepdims=True) + EPS)
        ckvh = ckv * rkv_
        dckvh = dnkv * gkv_ref[...]
        d_ckv = rkv_ * (dckvh - ckvh * jnp.mean(dckvh * ckvh, axis=-1, keepdims=True))
        dpb = jnp.concatenate([d_rq, d_rk, d_rv, d_rg, d_cq, d_ckv, d_kpe], axis=1).astype(BF16)
        dpb_ref[...] = dpb
        dh = _dot(dpb, win_ref[...])
        xt = jnp.where(lat, x_ref[...], c_ref[...])
        sc = mod_ref[0, 1:2, :]
        g = g_ref[...]
        r = lax.rsqrt(jnp.mean(xt * xt, axis=-1, keepdims=True) + EPS)
        xh = xt * r
        dxh = dh * (1.0 + sc) * g
        dx = r * (dxh - xh * jnp.mean(dxh * xh, axis=-1, keepdims=True))

        @pl.when(lat)
        def _():
            gx_ref[...] = dxm_ref[...] + dx

        def rsum(v):
            return jnp.sum(v, axis=0, keepdims=True)

        def widen(v):
            return jnp.concatenate([v, jnp.zeros((1, d - v.shape[1]), F32)], axis=1)

        stats = jnp.concatenate([
            rsum(dh), rsum(dh * xh * g), rsum(dh * (1.0 + sc) * xh), widen(rsum(dnq * cqh)), widen(rsum(dnkv * ckvh)),
            jnp.zeros((3, d), F32)], axis=0)
        first = jnp.logical_or(jnp.logical_and(lat, i % tpe == 0), i == nl)

        @pl.when(first)
        def _():
            st_ref[0] = stats

        @pl.when(jnp.logical_not(first))
        def _():
            st_ref[0] += stats

    def lat_tile(width):
        return pl.BlockSpec((tm, width), lambda i: (jnp.minimum(i, nl - 1), 0))

    def ctx_tile(width):
        return pl.BlockSpec((tm, width), lambda i: (jnp.maximum(i - nl, 0), 0))

    def tile(width):
        return pl.BlockSpec((tm, width), lambda i: (i, 0))

    tab = pl.BlockSpec((tm, 256), lambda i: (jnp.where(i < nl, i % tpe, tpe), 0))
    ex = pl.BlockSpec((1, 8, d), lambda i: (jnp.minimum(i // tpe, nex), 0, 0))
    return _hosted_call(
        body, (x2, ctx2, modv, g_attn, pg, drq, drk, dkc_r, drv, dvc_r, drg, dq_m, dkl, dkc, dvl, dvc, dxm,
               w_in, g_q, g_kv, w_uq, w_ukv, cos_t, sin_t), name="pre_bwd", grid=(nl + nc,),
        in_specs=[
            lat_tile(d), ctx_tile(d), ex, _full((1, d)), tile(PG_COLS),
            lat_tile(256), lat_tile(256), ctx_tile(256), lat_tile(512), ctx_tile(512), lat_tile(512),
            lat_tile(1024), lat_tile(1024), ctx_tile(1024), lat_tile(512), ctx_tile(512), lat_tile(d),
            _once(w_in.shape), _full((1, Q_LORA)), _full((1, KV_LORA)), _once(w_uq.shape), _once(w_ukv.shape),
            tab, tab,
        ],
        out_specs=[tile(IN_PAD), tile(1024), tile(1024), lat_tile(d), ex],
        out_shape=[
            jax.ShapeDtypeStruct((n_all, IN_PAD), BF16), jax.ShapeDtypeStruct((n_all, 1024), BF16),
            jax.ShapeDtypeStruct((n_all, 1024), BF16), jax.ShapeDtypeStruct((t_lat, d), F32),
            jax.ShapeDtypeStruct((nex + 1, 8, d), F32),
        ],
        sem=("arbitrary",), rider=rider)


MLA_SCALE = 1.0 / math.sqrt(MLA_NOPE + MLA_ROPE)
KEY_BLOCK = 1024


def _mla_specs(t_lat, seq, ctx_len, tq, heads=1):
    nqt = seq // tq
    cb = t_lat // ctx_len
    q = pl.BlockSpec((tq, heads * MLA_HEAD), lambda b, h, j: (b * nqt + j, h))
    kl = pl.BlockSpec((seq, heads * MLA_HEAD), lambda b, h, j: (b, h))
    kc = pl.BlockSpec((ctx_len, heads * MLA_HEAD), lambda b, h, j: (cb + b, h))
    vl = pl.BlockSpec((seq, heads * 128), lambda b, h, j: (b, h))
    vc = pl.BlockSpec((ctx_len, heads * 128), lambda b, h, j: (cb + b, h))
    o = pl.BlockSpec((tq, heads * 128), lambda b, h, j: (b * nqt + j, h))
    return q, kl, kc, vl, vc, o


FWD_HEADS = 2
BWD_HEADS = 1


def _mla_fwd(q, k, v, *, t_lat, seq, ctx_len, tq, rider=None):
    nex = t_lat // seq

    def body(q_ref, kl_ref, kc_ref, vl_ref, vc_ref, o_ref, lse_ref):
        for hh in range(FWD_HEADS):
            wide = slice(hh * MLA_HEAD, (hh + 1) * MLA_HEAD)
            cols = slice(hh * 128, (hh + 1) * 128)
            qb = q_ref[:, wide]
            s = _dot_nt(qb, kl_ref[:, wide])
            sc = _dot_nt(qb, kc_ref[:, wide])
            m = jnp.maximum(jnp.max(s, axis=-1, keepdims=True), jnp.max(sc, axis=-1, keepdims=True))
            p = jnp.exp(s - m)
            pc = jnp.exp(sc - m)
            total = jnp.sum(p, axis=-1, keepdims=True) + jnp.sum(pc, axis=-1, keepdims=True)
            o = _dot(p.astype(BF16), vl_ref[:, cols]) + _dot(pc.astype(BF16), vc_ref[:, cols])
            o_ref[:, cols] = (o * (1.0 / total)).astype(BF16)
            lse_ref[:, cols] = jnp.broadcast_to(m + jnp.log(total), (tq, 128))

    qs, kl, kc, vl, vc, os_ = _mla_specs(t_lat, seq, ctx_len, tq, FWD_HEADS)
    return _hosted_call(
        body, (q, k, k, v, v), name="mla_fwd", grid=(nex, HEADS // FWD_HEADS, seq // tq),
        in_specs=[qs, kl, kc, vl, vc], out_specs=[os_, os_],
        out_shape=[jax.ShapeDtypeStruct((t_lat, HEADS * 128), BF16), jax.ShapeDtypeStruct((t_lat, HEADS * 128), F32)],
        sem=("parallel", "parallel", "arbitrary"), rider=rider)


def _mla_bwd(q, k, v, ymla, lse, dmix, *, t_lat, seq, ctx_len, tq, rider=None):
    nex = t_lat // seq
    nqt = seq // tq
    t_ctx = nex * ctx_len
    kb = min(KEY_BLOCK, seq)

    def body(q_ref, kl_ref, kc_ref, vl_ref, vc_ref, o_ref, lse_ref, do_ref, dq_ref, dkl_out, dkc_out, dvl_out, dvc_out,
             dkl_ref, dkc_ref, dvl_ref, dvc_ref):
        j = pl.program_id(2)

        @pl.when(j == 0)
        def _():
            dkl_ref[...] = jnp.zeros(dkl_ref.shape, F32)
            dkc_ref[...] = jnp.zeros(dkc_ref.shape, F32)
            dvl_ref[...] = jnp.zeros(dvl_ref.shape, F32)
            dvc_ref[...] = jnp.zeros(dvc_ref.shape, F32)

        for hh in range(BWD_HEADS):
            wide = slice(hh * MLA_HEAD, (hh + 1) * MLA_HEAD)
            cols = slice(hh * 128, (hh + 1) * 128)
            qb = q_ref[:, wide]
            dob = do_ref[:, cols]
            delta = jnp.sum(dob.astype(F32) * o_ref[:, cols].astype(F32), axis=-1, keepdims=True)
            lse_row = lse_ref[:, hh * 128:hh * 128 + 1]

            def block(k_ref, v_ref, dk_ref, dv_ref, rows):
                kbl = k_ref[rows, wide]
                vbl = v_ref[rows, cols]
                p = jnp.exp(_dot_nt(qb, kbl) - lse_row)
                ds = (p * (_dot_nt(dob, vbl) - delta)).astype(BF16)
                dk_ref[rows, wide] += _dot_tn(ds, qb)
                dv_ref[rows, cols] += _dot_tn(p.astype(BF16), dob)
                return _dot(ds, kbl)

            dq = block(kc_ref, vc_ref, dkc_ref, dvc_ref, pl.ds(0, ctx_len))
            for i in range(seq // kb):
                dq = dq + block(kl_ref, vl_ref, dkl_ref, dvl_ref, pl.ds(i * kb, kb))
            dq_ref[:, wide] = dq.astype(BF16)

        @pl.when(j == nqt - 1)
        def _():
            dkl_out[...] = dkl_ref[...].astype(BF16)
            dkc_out[...] = dkc_ref[...].astype(BF16)
            dvl_out[...] = dvl_ref[...].astype(BF16)
            dvc_out[...] = dvc_ref[...].astype(BF16)

    g = BWD_HEADS
    qs, kl, kc, vl, vc, os_ = _mla_specs(t_lat, seq, ctx_len, tq, g)
    do_spec = pl.BlockSpec((tq, g * 128), lambda b, h, j: (b * nqt + j, HEADS // g + h))
    key_blocks = [(seq, g * MLA_HEAD), (ctx_len, g * MLA_HEAD), (seq, g * 128), (ctx_len, g * 128)]
    return _hosted_call(
        body, (q, k, k, v, v, ymla, lse, dmix), name="mla_bwd", grid=(nex, HEADS // g, nqt),
        in_specs=[qs, kl, kc, vl, vc, os_, os_, do_spec],
        out_specs=[qs] + [pl.BlockSpec(blk, lambda b, h, j: (b, h)) for blk in key_blocks],
        out_shape=[
            jax.ShapeDtypeStruct((t_lat, HEADS * MLA_HEAD), BF16),
            jax.ShapeDtypeStruct((t_lat, HEADS * MLA_HEAD), BF16),
            jax.ShapeDtypeStruct((t_ctx, HEADS * MLA_HEAD), BF16),
            jax.ShapeDtypeStruct((t_lat, HEADS * 128), BF16),
            jax.ShapeDtypeStruct((t_ctx, HEADS * 128), BF16),
        ],
        scratch_shapes=[pltpu.VMEM(blk, F32) for blk in key_blocks],
        sem=("parallel", "parallel", "arbitrary"), rider=rider)


def _decay_terms(lg, chunk, forward):
    ii = lax.broadcasted_iota(jnp.int32, (chunk, chunk), 0)
    jj = lax.broadcasted_iota(jnp.int32, (chunk, chunk), 1)
    diff = (ii - jj) if forward else (jj - ii)
    dist = jnp.maximum(diff, 0).astype(F32)
    dmat = jnp.where(diff >= 0, jnp.exp(lg * dist), 0.0)
    pos = lax.broadcasted_iota(jnp.int32, (chunk, 1), 0).astype(F32)
    if forward:
        e_q = pos + 1.0
        e_k = (chunk - 1.0) - pos
    else:
        e_q = chunk - pos
        e_k = pos
    wq = jnp.exp(lg * e_q)
    wk = jnp.exp(lg * e_k)
    cd = jnp.exp(jnp.full((1, 1), lg * chunk, F32))
    return dmat, dist, wq, wk, e_q, e_k, cd


def _ctx_weights(lg, ctx_len, forward):
    pos = lax.broadcasted_iota(jnp.int32, (ctx_len, 1), 0).astype(F32)
    e = ((ctx_len - 1.0) - pos) if forward else pos
    return jnp.exp(lg * e), e


def _pair_specs(t_lat, seq, ctx_len):
    cb = t_lat // ctx_len
    qk = pl.BlockSpec((seq, 128), lambda b, p: (b, p))
    v = pl.BlockSpec((seq, 256), lambda b, p: (b, p))
    kc = pl.BlockSpec((ctx_len, 128), lambda b, p: (cb + b, p))
    vc = pl.BlockSpec((ctx_len, 256), lambda b, p: (cb + b, p))
    return qk, v, kc, vc


def _lane_masks():
    lane = lax.broadcasted_iota(jnp.int32, (1, 128), 1)
    return [(lane // RET_DK) == hh for hh in (0, 1)]


def _ret_fwd_pair(rq, rk, rv, pg, lg, g_ret, *, t_lat, seq, ctx_len, chunk, rider=None):
    nex = t_lat // seq
    n_chunk = seq // chunk

    def body(q_ref, k_ref, v_ref, kc_ref, vc_ref, rg_ref, lg_ref, g_ref, y_ref, o_ref):
        pair = pl.program_id(1)
        masks = _lane_masks()
        kcf = kc_ref[...].astype(F32)
        chains = [(forward, hh) for forward in (True, False) for hh in (0, 1)]
        terms, s0 = [], []
        for forward, hh in chains:
            lgd = lg_ref[0 if forward else 1, 2 * pair + hh]
            terms.append(_decay_terms(lgd, chunk, forward))
            wc, _ = _ctx_weights(lgd, ctx_len, forward)
            s0.append(_dot_tn((jnp.where(masks[hh], kcf, 0.0) * wc).astype(BF16), vc_ref[:, hh * 128:(hh + 1) * 128]))
        both = [terms[hh][0] + terms[2 + hh][0] for hh in (0, 1)]
        o_ref[...] = jnp.zeros(o_ref.shape, F32)

        def step(t, states):
            new = [None] * 4
            for forward in (True, False):
                n = t if forward else n_chunk - 1 - t
                sl = pl.ds(pl.multiple_of(n * chunk, chunk), chunk)
                qb = q_ref[sl, :]
                kf_all = k_ref[sl, :].astype(F32)
                for hh in (0, 1):
                    c = (0 if forward else 2) + hh
                    _, _, wq, wk, _, _, cd = terms[c]
                    cols = slice(hh * 128, (hh + 1) * 128)
                    qm = jnp.where(masks[hh], qb, jnp.zeros((), BF16))
                    kf = jnp.where(masks[hh], kf_all, 0.0)
                    vb = v_ref[sl, cols]
                    o = wq * _dot(qm, states[c].astype(BF16))
                    if forward:
                        o = o + _dot((_dot_nt(qm, kf.astype(BF16)) * both[hh]).astype(BF16), vb)
                    o_ref[sl, cols] += o
                    new[c] = cd * states[c] + _dot_tn((kf * wk).astype(BF16), vb)
            return tuple(new)

        lax.fori_loop(0, n_chunk, step, tuple(s0))

        def norm_step(n, carry):
            sl = pl.ds(pl.multiple_of(n * chunk, chunk), chunk)
            for hh in (0, 1):
                cols = slice(hh * 128, (hh + 1) * 128)
                o = o_ref[sl, cols]
                mu = jnp.mean(o, axis=-1, keepdims=True)
                oc = o - mu
                var = jnp.mean(oc * oc, axis=-1, keepdims=True)
                rg = rg_ref[sl, cols]
                y_ref[sl, cols] = (oc * lax.rsqrt(var + EPS) * g_ref[:, cols] * (rg / (1.0 + jnp.exp(-rg)))).astype(BF16)
            return carry

        lax.fori_loop(0, n_chunk, norm_step, 0)

    qk, v, kc, vc = _pair_specs(t_lat, seq, ctx_len)
    return _hosted_call(
        body, (rq, rk, rv, rk, rv, pg, lg, g_ret), name="ret_fwd", grid=(nex, HEADS // 2),
        in_specs=[qk, qk, v, kc, vc, v, pl.BlockSpec(memory_space=pltpu.SMEM), pl.BlockSpec((1, 256), lambda b, p: (0, p))],
        out_specs=[v, v],
        out_shape=[jax.ShapeDtypeStruct((t_lat, HEADS * RET_DV), BF16), jax.ShapeDtypeStruct((t_lat, HEADS * RET_DV), F32)],
        sem=("parallel", "arbitrary"), rider=rider)


def _ret_bwd_pair(rq, rk, rv, pg, osum, dmix, lg, g_ret, *, t_lat, seq, ctx_len, chunk, rider=None):
    nex = t_lat // seq
    n_chunk = seq // chunk
    t_ctx = nex * ctx_len

    def body(q_ref, k_ref, v_ref, kc_ref, vc_ref, rg_ref, o_ref, dy_ref, lg_ref, g_ref,
             dq_out, dk_out, dv_out, dkc_ref, dvc_ref, drg_ref, st_ref, do_s, s_st, dq_ref, dk_ref, dv_ref):
        pair = pl.program_id(1)
        masks = _lane_masks()
        kcf = kc_ref[...].astype(F32)

        def norm_step(n, dgains):
            sl = pl.ds(pl.multiple_of(n * chunk, chunk), chunk)
            out = []
            for hh in (0, 1):
                cols = slice(hh * 128, (hh + 1) * 128)
                gain = g_ref[:, cols]
                o = o_ref[sl, cols]
                mu = jnp.mean(o, axis=-1, keepdims=True)
                oc = o - mu
                rstd = lax.rsqrt(jnp.mean(oc * oc, axis=-1, keepdims=True) + EPS)
                ohat = oc * rstd
                rg = rg_ref[sl, cols]
                sg = 1.0 / (1.0 + jnp.exp(-rg))
                dy = dy_ref[sl, cols].astype(F32)
                don = dy * (rg * sg)
                drg_ref[sl, cols] = (dy * (ohat * gain) * (sg * (1.0 + rg * (1.0 - sg)))).astype(BF16)
                dohat = don * gain
                do_s[sl, cols] = rstd * (dohat - jnp.mean(dohat, axis=-1, keepdims=True)
                                         - ohat * jnp.mean(dohat * ohat, axis=-1, keepdims=True))
                out.append(dgains[hh] + jnp.sum(don * ohat, axis=0, keepdims=True))
            return tuple(out)

        zero_row = jnp.zeros((1, 128), F32)
        dgains = lax.fori_loop(0, n_chunk, norm_step, (zero_row, zero_row))
        dq_ref[...] = jnp.zeros(dq_ref.shape, F32)
        dk_ref[...] = jnp.zeros(dk_ref.shape, F32)
        dv_ref[...] = jnp.zeros(dv_ref.shape, F32)

        chains = [(forward, hh) for forward in (True, False) for hh in (0, 1)]
        terms, ctxw, s0 = [], [], []
        for forward, hh in chains:
            lgd = lg_ref[0 if forward else 1, 2 * pair + hh]
            terms.append(_decay_terms(lgd, chunk, forward))
            ctxw.append(_ctx_weights(lgd, ctx_len, forward))
            s0.append(_dot_tn((jnp.where(masks[hh], kcf, 0.0) * ctxw[-1][0]).astype(BF16), vc_ref[:, hh * 128:(hh + 1) * 128]))

        def chunk_at(t, ascending):
            n = t if ascending else n_chunk - 1 - t
            return n, pl.ds(pl.multiple_of(n * chunk, chunk), chunk)

        def state_step(t, states):
            new = []
            for c, (forward, hh) in enumerate(chains):
                n, sl = chunk_at(t, forward)
                wk, cd = terms[c][3], terms[c][6]
                s_st[c, n] = states[c]
                kf = jnp.where(masks[hh], k_ref[sl, :].astype(F32), 0.0)
                new.append(cd * states[c] + _dot_tn((kf * wk).astype(BF16), v_ref[sl, hh * 128:(hh + 1) * 128]))
            return tuple(new)

        lax.fori_loop(0, n_chunk, state_step, tuple(s0))

        both = [terms[hh][0] + terms[2 + hh][0] for hh in (0, 1)]

        def grad_step(t, carry):
            out = [None] * len(chains)
            in_chunk_b = [None, None]
            for forward in (True, False):
                n, sl = chunk_at(t, not forward)
                qb = q_ref[sl, :]
                kf_all = k_ref[sl, :].astype(F32)
                dq_sum = jnp.zeros((chunk, 128), F32)
                dk_sum = jnp.zeros((chunk, 128), F32)
                for hh in (0, 1):
                    c = (0 if forward else 2) + hh
                    g_next, dlg = carry[c]
                    dmat, dist, wq, wk, e_q, e_k, cd = terms[c]
                    cols = slice(hh * 128, (hh + 1) * 128)
                    qm = jnp.where(masks[hh], qb, jnp.zeros((), BF16))
                    kf = jnp.where(masks[hh], kf_all, 0.0)
                    kb = kf.astype(BF16)
                    vb = v_ref[sl, cols]
                    do = do_s[sl, cols]
                    dob = do.astype(BF16)
                    s_n = s_st[c, n]
                    s_nb = s_n.astype(BF16)
                    gb = g_next.astype(BF16)
                    dk_cross = wk * _dot_nt(vb, gb)
                    dv = _dot((kf * wk).astype(BF16), gb)
                    o_cross = wq * _dot(qm, s_nb)
                    dq_sum = dq_sum + wq * _dot_nt(dob, s_nb)
                    dk_sum = dk_sum + dk_cross
                    dlg = (dlg + chunk * cd * jnp.sum(g_next * s_n, keepdims=True)
                           + jnp.sum(e_k * jnp.sum(kf * dk_cross, axis=-1, keepdims=True), keepdims=True)
                           + jnp.sum(e_q * jnp.sum(o_cross * do, axis=-1, keepdims=True), keepdims=True))
                    if forward:
                        a_raw = _dot_nt(qm, kb)
                        da_raw = _dot_nt(dob, vb)
                        prod = a_raw * da_raw
                        dlg = dlg + jnp.sum(dist * dmat * prod, keepdims=True)
                        in_chunk_b[hh] = jnp.sum(terms[2 + hh][1] * terms[2 + hh][0] * prod, keepdims=True)
                        dab = (da_raw * both[hh]).astype(BF16)
                        dq_sum = dq_sum + _dot(dab, kb)
                        dk_sum = dk_sum + _dot_tn(dab, qm)
                        dv = dv + _dot_tn((a_raw * both[hh]).astype(BF16), dob)
                    else:
                        dlg = dlg + in_chunk_b[hh]
                    dv_ref[sl, cols] += dv
                    out[c] = (cd * g_next + _dot_tn((qm.astype(F32) * wq).astype(BF16), dob), dlg)
                dq_ref[sl, :] += dq_sum
                dk_ref[sl, :] += dk_sum
            return tuple(out)

        zero = (jnp.zeros((128, 128), F32), jnp.zeros((1, 1), F32))
        res = lax.fori_loop(0, n_chunk, grad_step, (zero,) * len(chains))
        dkc_sum = jnp.zeros((ctx_len, 128), F32)
        dvc = [jnp.zeros((ctx_len, 128), F32)] * 2
        dlgs = []
        for c, (forward, hh) in enumerate(chains):
            ds0, dlg = res[c]
            wc, e_c = ctxw[c]
            kcm = jnp.where(masks[hh], kcf, 0.0)
            ds0b = ds0.astype(BF16)
            dkc_part = wc * _dot_nt(vc_ref[:, hh * 128:(hh + 1) * 128], ds0b)
            dkc_sum = dkc_sum + dkc_part
            dvc[hh] = dvc[hh] + _dot((kcm * wc).astype(BF16), ds0b)
            dlgs.append(dlg + jnp.sum(e_c * jnp.sum(kcm * dkc_part, axis=-1, keepdims=True), keepdims=True))
        dq_out[...] = dq_ref[...].astype(BF16)
        dk_out[...] = dk_ref[...].astype(BF16)
        dv_out[...] = dv_ref[...].astype(BF16)
        dkc_ref[...] = dkc_sum
        for hh in (0, 1):
            cols = slice(hh * 128, (hh + 1) * 128)
            dvc_ref[:, cols] = dvc[hh]
            st_ref[0, :, cols] = jnp.concatenate([
                dgains[hh], jnp.broadcast_to(dlgs[hh], (1, 128)), jnp.broadcast_to(dlgs[2 + hh], (1, 128)),
                jnp.zeros((5, 128), F32)], axis=0)

    qk, v, kc, vc = _pair_specs(t_lat, seq, ctx_len)
    return _hosted_call(
        body, (rq, rk, rv, rk, rv, pg, osum, dmix, lg, g_ret), name="ret_bwd", grid=(nex, HEADS // 2),
        in_specs=[qk, qk, v, kc, vc, v, v, v, pl.BlockSpec(memory_space=pltpu.SMEM),
                  pl.BlockSpec((1, 256), lambda b, p: (0, p))],
        out_specs=[
            qk, qk, v,
            pl.BlockSpec((ctx_len, 128), lambda b, p: (b, p)),
            pl.BlockSpec((ctx_len, 256), lambda b, p: (b, p)),
            v,
            pl.BlockSpec((1, 8, 256), lambda b, p: (b, 0, p)),
        ],
        out_shape=[
            jax.ShapeDtypeStruct((t_lat, 256), BF16), jax.ShapeDtypeStruct((t_lat, 256), BF16),
            jax.ShapeDtypeStruct((t_lat, 512), BF16), jax.ShapeDtypeStruct((t_ctx, 256), F32),
            jax.ShapeDtypeStruct((t_ctx, 512), F32), jax.ShapeDtypeStruct((t_lat, 512), BF16),
            jax.ShapeDtypeStruct((nex, 8, 512), F32),
        ],
        scratch_shapes=[pltpu.VMEM((seq, 256), F32), pltpu.VMEM((4, n_chunk, 128, 128), F32),
                        pltpu.VMEM((seq, 128), F32), pltpu.VMEM((seq, 128), F32), pltpu.VMEM((seq, 256), F32)],
        sem=("parallel", "arbitrary"), rider=rider)


def _matmul_tn(a, b, *, bm, bn, bk, chip_major, name, out_dtype=F32, rider=None):
    tk, m = a.shape
    n = b.shape[1]
    slab = n // N_CHIPS
    per_block = bn // slab if chip_major else 1
    bk = max(c for c in range(LANES, min(bk, tk) + 1, LANES) if tk % c == 0)
    nk = tk // bk
    blk = (per_block, bm, slab) if chip_major else (bm, bn)

    def body(a_ref, b_ref, o_ref, acc_ref):
        k = pl.program_id(2)
        if chip_major:
            parts = [_dot_tn(a_ref[...], b_ref[:, s * slab:(s + 1) * slab]) for s in range(per_block)]
        else:
            parts = [_dot_tn(a_ref[...], b_ref[...])]

        @pl.when(k == 0)
        def _():
            for s, part in enumerate(parts):
                if chip_major:
                    acc_ref[s] = part
                else:
                    acc_ref[...] = part

        @pl.when(k > 0)
        def _():
            for s, part in enumerate(parts):
                if chip_major:
                    acc_ref[s] += part
                else:
                    acc_ref[...] += part

        @pl.when(k == nk - 1)
        def _():
            o_ref[...] = acc_ref[...].astype(out_dtype)

    if chip_major:
        out_spec = pl.BlockSpec(blk, lambda i, j, k: (j, i, 0))
        out_shape = jax.ShapeDtypeStruct((N_CHIPS, m, slab), out_dtype)
    else:
        out_spec = pl.BlockSpec(blk, lambda i, j, k: (i, j))
        out_shape = jax.ShapeDtypeStruct((m, n), out_dtype)
    (out,), carried = _hosted_call(
        body, (a, b), name=name, grid=(m // bm, n // bn, nk),
        in_specs=[pl.BlockSpec((bk, bm), lambda i, j, k: (k, i)), pl.BlockSpec((bk, bn), lambda i, j, k: (k, j))],
        out_specs=[out_spec], out_shape=[out_shape], scratch_shapes=[pltpu.VMEM(blk, F32)],
        sem=("parallel", "parallel", "arbitrary"), rider=rider)
    return out if rider is None else (out, carried)


_LATE = ("w_out", "w_ff1", "w_ff2")
_EARLY = ("w_in", "w_uq", "w_ukv")


def _local_step(x, ctx, tgt, modv, lg, g_attn, g_ffn, g_fin, g_ret, g_q, g_kv, w_in, w_uq, w_ukv, late, place=None,
                *, tm=256, tq=256, chunk=256):
    nex, seq, d = x.shape
    ctx_len = ctx.shape[1]
    t_lat = nex * seq
    tm = min(tm, seq)
    x2 = x.reshape(t_lat, d)
    ctx2 = ctx.reshape(nex * ctx_len, d)
    tgt2 = tgt.reshape(t_lat, d)
    tm_fwd = min(2 * tm, seq)
    cos_t, sin_t = _rope_tables(seq, tm)
    dims = dict(t_lat=t_lat, seq=seq, ctx_len=ctx_len)
    alone = place is None

    (hb, pg, rq, rk, rv, nq, nkv, q, k, v), crossed_a = _pre_fwd(
        x2, ctx2, modv, g_attn, w_in, g_q, g_kv, w_uq, w_ukv, *_rope_tables(seq, tm_fwd), seq=seq, tm=tm_fwd,
        rider=None if alone else _gather_ici_rider([late[2]]))
    (yret, osum), got = _ret_fwd_pair(
        rq, rk, rv, pg, lg, g_ret, chunk=min(2 * chunk, seq), **dims,
        rider=None if alone else _merge_riders(_gather_d2d_rider(crossed_a), _gather_ici_rider([late[3]])))
    (ymla, lse), got_rest = _mla_fwd(
        q, k, v, tq=tq, **dims,
        rider=None if alone else _merge_riders(_gather_rider([late[0], late[1]], staged=True), _gather_d2d_rider(got[1:])))
    w_out, w_ff1, w_ff2a, w_ff2b = late if alone else (got_rest[0], got_rest[1], got[0], got_rest[2])
    mix, act, du, h2, df, dmo, dmix, dxm, st_post = _post(yret, ymla, x2, tgt2, modv, g_ffn, g_fin, w_out.reshape(d, d),
                                                         w_ff1, w_ff2a, w_ff2b, seq=seq, tm=min(tm, 256))
    kw = dict(bm=1024, bn=1024, bk=2048, out_dtype=BF16)
    g_ff2 = _matmul_tn(act, df, chip_major=False, name="gw_ff2", **kw).reshape(N_CHIPS, D_FF // N_CHIPS, d)
    if alone:
        g_ff1 = _matmul_tn(h2, du, chip_major=True, name="gw_ff1", **kw)
        g_out = _matmul_tn(mix, dmo, chip_major=False, name="gw_out", **kw).reshape(N_CHIPS, d // N_CHIPS, d)
        (dq_m, dkl, dkc, dvl, dvc), _ = _mla_bwd(q, k, v, ymla, lse, dmix, tq=tq, **dims)
        (drq, drk, drv, dkc_r, dvc_r, drg, st_ret), _ = _ret_bwd_pair(rq, rk, rv, pg, osum, dmix, lg, g_ret, chunk=chunk,
                                                                      **dims)
        late_out = [g_out, g_ff1, g_ff2]
    else:
        core, slot = place
        g_ff1, x_ff2 = _matmul_tn(h2, du, chip_major=True, name="gw_ff1", rider=_exchange_rider([g_ff2]), **kw)
        g_out, x_ff1 = _matmul_tn(mix, dmo, chip_major=False, name="gw_out", rider=_exchange_rider([g_ff1]), **kw)
        g_out = g_out.reshape(N_CHIPS, d // N_CHIPS, d)
        p_ff2 = _add_half(g_ff2, x_ff2[0], core, "add_half_w_ff2")
        p_ff1 = _add_half(g_ff1, x_ff1[0], core, "add_half_w_ff1")
        (dq_m, dkl, dkc, dvl, dvc), (l_ff2, l_ff1, x_out) = _mla_bwd(
            q, k, v, ymla, lse, dmix, tq=min(seq, 512), **dims,
            rider=_merge_riders(_scatter_rider([p_ff2, p_ff1]), _exchange_rider([g_out])))
        p_out = _add_half(g_out, x_out, core, "add_half_w_out")
        m_ff2 = _sum_chips(p_ff2, l_ff2, slot, "sum_chips_w_ff2")
        m_ff1 = _sum_chips(p_ff1, l_ff1, slot, "sum_chips_w_ff1")
        (drq, drk, drv, dkc_r, dvc_r, drg, st_ret), (l_out,) = _ret_bwd_pair(
            rq, rk, rv, pg, osum, dmix, lg, g_ret, chunk=chunk, **dims, rider=_scatter_rider([p_out]))
        late_out = [_sum_chips(p_out, l_out, slot, "sum_chips_w_out"), m_ff1, m_ff2]
    (dpb, dqf, dkvf, gx, st_pre), _ = _pre_bwd(
        x2, ctx2, modv, g_attn, pg, drq, drk, dkc_r, drv, dvc_r, drg, dq_m, dkl, dkc, dvl, dvc, dxm, w_in, g_q, g_kv,
        w_uq, w_ukv, cos_t, sin_t, seq=seq, tm=tm)
    g_early = [
        _matmul_tn(dpb, hb, bm=IN_PAD // 2, bn=d, bk=1536, chip_major=False, name="gw_in"),
        _matmul_tn(dqf, nq, bm=HEADS * MLA_HEAD, bn=Q_LORA, bk=1536, chip_major=False, name="gw_uq"),
        _matmul_tn(nkv, dkvf, bm=KV_LORA, bn=HEADS * 256, bk=1536, chip_major=True, name="gw_ukv"),
    ]
    return gx.reshape(nex, seq, d), g_early, late_out, st_post, st_ret, st_pre


_ANY = pl.BlockSpec(memory_space=pl.ANY)
_VMEM = pl.BlockSpec(memory_space=pltpu.VMEM)
_OFFSETS = tuple((dx, dy, dc) for dx in (0, 1) for dy in (0, 1) for dc in (0, 1))[1:]
_CHIP_OFFSETS = ((1, 0), (0, 1), (1, 1))


def _place():
    return lax.axis_index("x"), lax.axis_index("y"), lax.axis_index("c")


def _flip(v, d):
    return 1 - v if d else v


def _gather8_rider(a, in_vmem=True):
    def copies(a_ref, o_ref, send, recv):
        x, y, z = _place()
        me = 4 * x + 2 * y + z
        out = []
        for k, (dx, dy, dc) in enumerate(_OFFSETS):
            peer = (_flip(x, dx), _flip(y, dy), _flip(z, dc))
            landing = o_ref.at[4 * peer[0] + 2 * peer[1] + peer[2]]
            out.append((
                pltpu.make_async_remote_copy(src_ref=a_ref, dst_ref=o_ref.at[me], send_sem=send.at[k],
                                             recv_sem=recv.at[k], device_id=peer, device_id_type=MESH),
                pltpu.make_async_remote_copy(src_ref=a_ref, dst_ref=landing, send_sem=send.at[k],
                                             recv_sem=recv.at[k], device_id=peer, device_id_type=MESH)))
        return me, out

    def start(ins, outs, sems):
        me, cps = copies(ins[0], outs[0], sems[0], sems[1])
        pltpu.make_async_copy(ins[0], outs[0].at[me], sems[2]).start()
        for out_cp, _ in cps:
            out_cp.start()

    def finish(ins, outs, sems):
        me, cps = copies(ins[0], outs[0], sems[0], sems[1])
        for out_cp, in_cp in cps:
            in_cp.wait_recv()
            out_cp.wait_send()
        pltpu.make_async_copy(ins[0], outs[0].at[me], sems[2]).wait()

    spec = [_VMEM] if in_vmem else [_ANY]
    return _Rider([a], [jax.ShapeDtypeStruct((N_DEV,) + a.shape, a.dtype)],
                  [pltpu.SemaphoreType.DMA((7,)), pltpu.SemaphoreType.DMA((7,)), pltpu.SemaphoreType.DMA],
                  start, finish, in_specs=spec, out_specs=spec)


def _merge_riders(*riders):
    ins, outs, sems, in_specs, out_specs, aliases, cuts = [], [], [], [], [], {}, []
    for r in riders:
        cuts.append((len(ins), len(outs), len(sems)))
        aliases.update({len(ins) + i: len(outs) + j for i, j in r.aliases.items()})
        ins += r.ins
        outs += r.out_shapes
        sems += r.sems
        in_specs += r.in_specs
        out_specs += r.out_specs

    def part(r, cut, r_ins, r_outs, r_sems):
        return (r_ins[cut[0]:cut[0] + len(r.ins)], r_outs[cut[1]:cut[1] + len(r.out_shapes)],
                r_sems[cut[2]:cut[2] + len(r.sems)])

    def start(r_ins, r_outs, r_sems):
        for r, cut in zip(riders, cuts):
            r.start(*part(r, cut, r_ins, r_outs, r_sems))

    def finish(r_ins, r_outs, r_sems):
        for r, cut in zip(riders, cuts):
            r.finish(*part(r, cut, r_ins, r_outs, r_sems))

    def middle(r_ins, r_outs, r_sems):
        for r, cut in zip(riders, cuts):
            if r.middle is not None:
                r.middle(*part(r, cut, r_ins, r_outs, r_sems))

    return _Rider(ins, outs, sems, start, finish, aliases=aliases, in_specs=in_specs, out_specs=out_specs,
                  middle=middle if any(r.middle is not None for r in riders) else None)


def _allgather8(a, name):
    return _run_rider(_gather8_rider(a), name)[0]


BF16_TILE_ROWS = 16


def _half(o, slot, which):
    r2 = o.shape[1] // 2
    if r2 % BF16_TILE_ROWS == 0:
        return o.at[slot, pl.ds(which * r2, r2)]
    c2 = o.shape[2] // 2
    assert c2 % LANES == 0
    return o.at[slot, :, pl.ds(which * c2, c2)]


def _gather_send(o_refs, send, recv):
    x, y, z = _place()
    chip = 2 * x + y
    for a, o in enumerate(o_refs):
        r2 = o.shape[1] // 2
        mine = _half(o, chip, z)
        for k, (dx, dy) in enumerate(_CHIP_OFFSETS):
            pltpu.make_async_remote_copy(
                src_ref=mine, dst_ref=mine, send_sem=send.at[a, k], recv_sem=recv.at[a, k],
                device_id=(_flip(x, dx), _flip(y, dy), z), device_id_type=MESH).start()


def _gather_landed(o_refs, send, recv, then=None):
    x, y, z = _place()
    chip = 2 * x + y
    for a, o in enumerate(o_refs):
        for k, (dx, dy) in enumerate(_CHIP_OFFSETS):
            landed = _half(o, 2 * _flip(x, dx) + _flip(y, dy), z)
            pltpu.make_async_remote_copy(
                src_ref=landed, dst_ref=landed, send_sem=send.at[a, k], recv_sem=recv.at[a, k],
                device_id=(_flip(x, dx), _flip(y, dy), z), device_id_type=MESH).wait_recv()
            if then is not None:
                then(a, k, landed)
    for a, o in enumerate(o_refs):
        mine = _half(o, chip, z)
        for k, (dx, dy) in enumerate(_CHIP_OFFSETS):
            pltpu.make_async_remote_copy(
                src_ref=mine, dst_ref=mine, send_sem=send.at[a, k], recv_sem=recv.at[a, k],
                device_id=(_flip(x, dx), _flip(y, dy), z), device_id_type=MESH).wait_send()


def _pass_on(o_refs, fsend, frecv, a, k, landed):
    x, y, z = _place()
    pltpu.make_async_remote_copy(
        src_ref=landed, dst_ref=landed, send_sem=fsend.at[a, k], recv_sem=frecv.at[a, k],
        device_id=(x, y, 1 - z), device_id_type=MESH).start()


def _passed_on(o_refs, fsend, frecv):
    x, y, z = _place()
    for a, o in enumerate(o_refs):
        for k, (dx, dy) in enumerate(_CHIP_OFFSETS):
            other = 2 * _flip(x, dx) + _flip(y, dy)
            got = _half(o, other, 1 - z)
            gave = _half(o, other, z)
            pltpu.make_async_remote_copy(
                src_ref=got, dst_ref=got, send_sem=fsend.at[a, k], recv_sem=frecv.at[a, k],
                device_id=(x, y, 1 - z), device_id_type=MESH).wait_recv()
            pltpu.make_async_remote_copy(
                src_ref=gave, dst_ref=gave, send_sem=fsend.at[a, k], recv_sem=frecv.at[a, k],
                device_id=(x, y, 1 - z), device_id_type=MESH).wait_send()


def _gather_finish(o_refs, send, recv, fsend, frecv):
    _gather_landed(o_refs, send, recv, functools.partial(_pass_on, o_refs, fsend, frecv))
    _passed_on(o_refs, fsend, frecv)


class _Rider:
    def __init__(self, ins, out_shapes, sems, start, finish, aliases=None, in_specs=None, out_specs=None, middle=None):
        self.ins, self.out_shapes, self.sems = list(ins), list(out_shapes), list(sems)
        self.start, self.finish, self.aliases = start, finish, dict(aliases or {})
        self.middle = middle
        self.in_specs = list(in_specs) if in_specs else [_ANY] * len(self.ins)
        self.out_specs = list(out_specs) if out_specs else [_ANY] * len(self.out_shapes)


def _run_rider(rider, name):
    r_in, r_out = len(rider.ins), len(rider.out_shapes)

    def body(*refs):
        ins, outs, sems = refs[:r_in], refs[r_in:r_in + r_out], refs[r_in + r_out:]
        rider.start(ins, outs, sems)
        if rider.middle is not None:
            rider.middle(ins, outs, sems)
        rider.finish(ins, outs, sems)

    return pl.pallas_call(
        body, name=name, in_specs=rider.in_specs, out_specs=rider.out_specs, out_shape=rider.out_shapes,
        input_output_aliases=rider.aliases, scratch_shapes=rider.sems,
    )(*rider.ins)


def _hosted_call(body, args, *, name, grid, in_specs, out_specs, out_shape, scratch_shapes=(), sem, rider=None,
                 prefetch=()):
    scratch_shapes = list(scratch_shapes)
    n_pf, n_in, n_out, n_sc = len(prefetch), len(in_specs), len(out_specs), len(scratch_shapes)
    r_in, r_out = (len(rider.ins), len(rider.out_shapes)) if rider else (0, 0)
    last = tuple(g - 1 for g in grid)

    def hosted(*refs):
        p = 0
        parts = []
        for cnt in (n_pf, n_in, r_in, n_out, r_out, n_sc):
            parts.append(refs[p:p + cnt])
            p += cnt
        pf, ins, r_ins, outs, r_outs, scratch = parts
        sems = refs[p:]
        ids = [pl.program_id(a) for a in range(len(grid))]
        is_first = functools.reduce(jnp.logical_and, [i == 0 for i in ids])
        is_last = functools.reduce(jnp.logical_and, [i == e for i, e in zip(ids, last)])

        @pl.when(is_first)
        def _():
            rider.start(r_ins, r_outs, sems)

        if rider.middle is not None:
            linear = functools.reduce(lambda acc, ig: acc * ig[1] + ig[0], zip(ids, grid), 0)

            @pl.when(linear == math.prod(grid) * 3 // 4)
            def _():
                rider.middle(r_ins, r_outs, sems)

        body(*pf, *ins, *outs, *scratch)

        @pl.when(is_last)
        def _():
            rider.finish(r_ins, r_outs, sems)

    if rider is None:
        kern, all_in, all_out, shapes, scratch, aliases, extra = body, list(in_specs), list(out_specs), list(out_shape), \
            scratch_shapes, {}, []
    else:
        kern, all_in, all_out = hosted, list(in_specs) + rider.in_specs, list(out_specs) + rider.out_specs
        shapes, scratch, extra = list(out_shape) + rider.out_shapes, scratch_shapes + rider.sems, rider.ins
        aliases = {n_pf + n_in + i: n_out + j for i, j in rider.aliases.items()}
        sem = ("arbitrary",) * len(grid)
    if prefetch:
        spec = dict(grid_spec=pltpu.PrefetchScalarGridSpec(
            num_scalar_prefetch=n_pf, grid=grid, in_specs=all_in, out_specs=all_out, scratch_shapes=scratch))
    else:
        spec = dict(grid=grid, in_specs=all_in, out_specs=all_out, scratch_shapes=scratch)
    res = pl.pallas_call(kern, name=name, out_shape=shapes, input_output_aliases=aliases,
                         compiler_params=_params(sem, VMEM_LIMIT), **spec)(*prefetch, *args, *extra)
    return list(res[:n_out]), list(res[n_out:])


def _gather_rider(ws, staged=False):
    n = len(ws)
    shapes = [jax.ShapeDtypeStruct(w.shape, w.dtype) for w in ws]
    sems = [pltpu.SemaphoreType.DMA((n, 3))] * 4
    aliases = {a: a for a in range(n)}

    def start(ins, outs, s):
        _gather_send(outs, s[0], s[1])

    if not staged:
        return _Rider(ws, shapes, sems, start, lambda ins, outs, s: _gather_finish(outs, *s), aliases=aliases)
    return _Rider(
        ws, shapes, sems, start, lambda ins, outs, s: _passed_on(outs, s[2], s[3]), aliases=aliases,
        middle=lambda ins, outs, s: _gather_landed(outs, s[0], s[1], functools.partial(_pass_on, outs, s[2], s[3])))


def _gather_ici_rider(ws):
    n = len(ws)
    return _Rider(
        ws, [jax.ShapeDtypeStruct(w.shape, w.dtype) for w in ws], [pltpu.SemaphoreType.DMA((n, 3))] * 2,
        lambda ins, outs, sems: _gather_send(outs, sems[0], sems[1]),
        lambda ins, outs, sems: _gather_landed(outs, sems[0], sems[1]),
        aliases={a: a for a in range(n)})


def _gather_d2d_rider(ws):
    n = len(ws)

    def start(ins, outs, sems):
        x, y, z = _place()
        for a, o in enumerate(outs):
            for k, (dx, dy) in enumerate(_CHIP_OFFSETS):
                _pass_on(outs, sems[0], sems[1], a, k, _half(o, 2 * _flip(x, dx) + _flip(y, dy), z))

    return _Rider(
        ws, [jax.ShapeDtypeStruct(w.shape, w.dtype) for w in ws], [pltpu.SemaphoreType.DMA((n, 3))] * 2,
        start, lambda ins, outs, sems: _passed_on(outs, sems[0], sems[1]), aliases={a: a for a in range(n)})


def _copies_rider(ins, out_shapes, sem_shape, make):
    def start(r_ins, r_outs, sems):
        for cp in make(r_ins, r_outs, sems[0], sems[1]):
            cp.start()

    def finish(r_ins, r_outs, sems):
        for cp in make(r_ins, r_outs, sems[0], sems[1]):
            cp.wait()

    return _Rider(ins, out_shapes, [pltpu.SemaphoreType.DMA(sem_shape)] * 2, start, finish)


def _exchange_rider(gs):
    def make(g_refs, r_refs, send, recv):
        x, y, z = _place()
        return [pltpu.make_async_remote_copy(
            src_ref=g.at[:, pl.ds((1 - z) * (g.shape[1] // 2), g.shape[1] // 2)], dst_ref=r, send_sem=send.at[a],
            recv_sem=recv.at[a], device_id=(x, y, 1 - z), device_id_type=MESH)
            for a, (g, r) in enumerate(zip(g_refs, r_refs))]

    shapes = [jax.ShapeDtypeStruct((g.shape[0], g.shape[1] // 2, g.shape[2]), g.dtype) for g in gs]
    return _copies_rider(gs, shapes, (len(gs),), make)


def _add_half(g, recv, core, name):
    s, r, c = g.shape
    r2 = r // 2
    rb = r2
    for cand in (256, 128, 64):
        if r2 % cand == 0:
            rb = cand
            break
    g4 = g.reshape(s, 2, r2, c)

    def body(core_ref, g_ref, r_ref, o_ref):
        o_ref[...] = (g_ref[...].astype(F32) + r_ref[...].astype(F32)).astype(BF16)

    return pl.pallas_call(
        body, name=name,
        grid_spec=pltpu.PrefetchScalarGridSpec(
            num_scalar_prefetch=1, grid=(s, r2 // rb),
            in_specs=[pl.BlockSpec((None, None, rb, c), lambda i, j, cr: (i, cr[0], j, 0)),
                      pl.BlockSpec((None, rb, c), lambda i, j, cr: (i, j, 0))],
            out_specs=pl.BlockSpec((None, rb, c), lambda i, j, cr: (i, j, 0))),
        out_shape=jax.ShapeDtypeStruct((s, r2, c), BF16),
        compiler_params=_params(("parallel", "parallel")),
    )(core, g4, recv)


def _scatter_rider(ps):
    def make(p_refs, o_refs, send, recv):
        x, y, z = _place()
        copies = []
        for a, (p, o) in enumerate(zip(p_refs, o_refs)):
            for k, (dx, dy) in enumerate(_CHIP_OFFSETS):
                other = 2 * _flip(x, dx) + _flip(y, dy)
                copies.append(pltpu.make_async_remote_copy(
                    src_ref=p.at[other], dst_ref=o.at[k], send_sem=send.at[a, k], recv_sem=recv.at[a, k],
                    device_id=(_flip(x, dx), _flip(y, dy), z), device_id_type=MESH))
        return copies

    shapes = [jax.ShapeDtypeStruct((3,) + p.shape[1:], p.dtype) for p in ps]
    return _copies_rider(ps, shapes, (len(ps), 3), make)


def _sum_chips(p, landed, chip, name):
    _, r2, c = p.shape
    rb = r2
    for cand in (256, 128, 64):
        if r2 % cand == 0:
            rb = cand
            break

    def body(s_ref, p_ref, l_ref, o_ref):
        acc = p_ref[...].astype(F32)
        for k in range(3):
            acc = acc + l_ref[k].astype(F32)
        o_ref[...] = acc

    return pl.pallas_call(
        body, name=name,
        grid_spec=pltpu.PrefetchScalarGridSpec(
            num_scalar_prefetch=1, grid=(r2 // rb,),
            in_specs=[pl.BlockSpec((None, rb, c), lambda i, s: (s[0], i, 0)),
                      pl.BlockSpec((3, rb, c), lambda i, s: (0, i, 0))],
            out_specs=pl.BlockSpec((rb, c), lambda i, s: (i, 0))),
        out_shape=jax.ShapeDtypeStruct((r2, c), F32),
        compiler_params=_params(("parallel",)),
    )(chip, p, landed)


def _swap_rider(hs):
    def make(h_refs, o_refs, send, recv):
        x, y, z = _place()
        return [pltpu.make_async_remote_copy(
            src_ref=h, dst_ref=o, send_sem=send.at[a], recv_sem=recv.at[a], device_id=(x, y, 1 - z),
            device_id_type=MESH) for a, (h, o) in enumerate(zip(h_refs, o_refs))]

    return _copies_rider(hs, [jax.ShapeDtypeStruct(h.shape, h.dtype) for h in hs], (len(hs),), make)


def _reduce_scatter_vmem(gs, rows, rider, name):
    n = len(gs)
    r_in, r_out = len(rider.ins), len(rider.out_shapes)
    halves = [(r // 2, g.shape[-1]) for g, (r, _) in zip(gs, rows)]
    piece_cols = 2 * LANES
    pieces = [(a, slice(c0, min(c0 + piece_cols, h[1]))) for a, h in enumerate(halves) for c0 in range(0, h[1], piece_cols)]
    n_p = len(pieces)

    def body(*refs):
        p = 0
        parts = []
        for cnt in (n, r_in, n, n, r_out, n, n, n, 6):
            parts.append(refs[p:p + cnt])
            p += cnt
        g_refs, r_ins, mine, theirs, r_outs, recv, part, land, sems = parts
        r_sems = refs[p:]
        xs, xr, ss, sr, ws, wr = sems
        x, y, z = _place()
        chip = 2 * x + y
        sib = (x, y, 1 - z)
        rider.start(r_ins, r_outs, r_sems)

        def half_of(a, s, which):
            r2 = halves[a][0]
            if len(g_refs[a].shape) == 3:
                return g_refs[a].at[s, pl.ds(pl.multiple_of(which * r2, 8), r2)]
            return g_refs[a].at[pl.ds(pl.multiple_of(s * rows[a][1] + which * r2, 8), r2)]

        def exchange(i):
            a, cols = pieces[i]
            return [pltpu.make_async_remote_copy(
                src_ref=half_of(a, s, 1 - z).at[:, cols], dst_ref=recv[a].at[s, :, cols], send_sem=xs.at[i, s],
                recv_sem=xr.at[i, s], device_id=sib, device_id_type=MESH) for s in range(N_CHIPS)]

        def scatter(i):
            a, cols = pieces[i]
            return [pltpu.make_async_remote_copy(
                src_ref=part[a].at[2 * _flip(x, dx) + _flip(y, dy), :, cols], dst_ref=land[a].at[k, :, cols],
                send_sem=ss.at[i, k], recv_sem=sr.at[i, k], device_id=(_flip(x, dx), _flip(y, dy), z),
                device_id_type=MESH) for k, (dx, dy) in enumerate(_CHIP_OFFSETS)]

        def swap(i):
            a, cols = pieces[i]
            return pltpu.make_async_remote_copy(
                src_ref=mine[a].at[:, cols], dst_ref=theirs[a].at[:, cols], send_sem=ws.at[i], recv_sem=wr.at[i],
                device_id=sib, device_id_type=MESH)

        for i in range(len(pieces)):
            for cp in exchange(i):
                cp.start()
        for i, (a, cols) in enumerate(pieces):
            for cp in exchange(i):
                cp.wait()
            for s in range(N_CHIPS):
                part[a][s, :, cols] = (half_of(a, s, z)[:, cols] + recv[a][s, :, cols]).astype(BF16)
            for cp in scatter(i):
                cp.start()
        for i, (a, cols) in enumerate(pieces):
            for cp in scatter(i):
                cp.wait()
            acc = part[a][chip, :, cols].astype(F32)
            for k in range(3):
                acc = acc + land[a][k, :, cols].astype(F32)
            mine[a][:, cols] = acc
            swap(i).start()
        for i in range(len(pieces)):
            swap(i).wait()
        rider.finish(r_ins, r_outs, r_sems)

    half_shapes = [jax.ShapeDtypeStruct(h, F32) for h in halves]
    res = pl.pallas_call(
        body, name=name, in_specs=[_VMEM] * n + rider.in_specs, out_specs=[_VMEM] * (2 * n) + rider.out_specs,
        out_shape=half_shapes + half_shapes + rider.out_shapes,
        scratch_shapes=[pltpu.VMEM((N_CHIPS,) + h, F32) for h in halves] + [pltpu.VMEM((N_CHIPS,) + h, BF16) for h in halves]
        + [pltpu.VMEM((3,) + h, BF16) for h in halves]
        + [pltpu.SemaphoreType.DMA((n_p, N_CHIPS))] * 2 + [pltpu.SemaphoreType.DMA((n_p, 3))] * 2
        + [pltpu.SemaphoreType.DMA((n_p,))] * 2 + rider.sems,
        input_output_aliases={n + i: 2 * n + j for i, j in rider.aliases.items()},
        compiler_params=_params(None, VMEM_LIMIT),
    )(*gs, *rider.ins)
    return list(res[:n]), list(res[n:2 * n]), list(res[2 * n:])


SMALL_ROWS = 32
PACK_ROWS = 16


def _pack_small(st_post, st_ret, st_pre):
    d = st_post.shape[2]

    def body(po_ref, re_ref, pr_ref, o_ref):
        o_ref[...] = jnp.zeros(o_ref.shape, F32)
        o_ref[0:1, :] = pr_ref[0, 2:3, :] + pr_ref[1, 2:3, :] + pr_ref[2, 2:3, :]
        o_ref[1:2, :] = po_ref[0, 4:5, :] + po_ref[1, 4:5, :]
        o_ref[2:3, :] = po_ref[0, 5:6, :] + po_ref[1, 5:6, :]
        o_ref[3:4, 0:512] = re_ref[0, 0:1, :] + re_ref[1, 0:1, :]
        o_ref[4:5, :] = pr_ref[0, 3:4, :] + pr_ref[1, 3:4, :] + pr_ref[2, 3:4, :]
        o_ref[5:6, :] = pr_ref[0, 4:5, :] + pr_ref[1, 4:5, :] + pr_ref[2, 4:5, :]
        lane = lax.broadcasted_iota(jnp.int32, (1, LANES), 1)
        for row, src in ((6, 1), (10, 2)):
            acc = jnp.zeros((1, LANES), F32)
            for hd in range(HEADS):
                grp = re_ref[0, src:src + 1, hd * LANES:(hd + 1) * LANES] + re_ref[1, src:src + 1, hd * LANES:(hd + 1) * LANES]
                acc = acc + jnp.where(lane == hd, grp, 0.0)
            o_ref[row:row + 1, 0:LANES] = acc
        o_ref[7:8, :] = po_ref[0, 6:7, :] + po_ref[1, 6:7, :]
        o_ref[8:9, :] = pr_ref[2, 0:1, :]
        o_ref[9:10, :] = pr_ref[2, 1:2, :]
        for e in range(2):
            b = 12 + 6 * e
            o_ref[b:b + 1, :] = pr_ref[e, 0:1, :]
            o_ref[b + 1:b + 2, :] = pr_ref[e, 1:2, :]
            o_ref[b + 2:b + 3, :] = po_ref[e, 3:4, :]
            o_ref[b + 3:b + 4, :] = po_ref[e, 0:1, :]
            o_ref[b + 4:b + 5, :] = po_ref[e, 1:2, :]
            o_ref[b + 5:b + 6, :] = po_ref[e, 2:3, :]

    return pl.pallas_call(body, name="pack_small", out_shape=jax.ShapeDtypeStruct((SMALL_ROWS, d), F32))(st_post, st_ret, st_pre)


def _small_reduce(gathered):
    d = gathered.shape[2]

    def body(g_ref, o_ref):
        tot = g_ref[0, 0:PACK_ROWS, :]
        for dev in range(1, N_DEV):
            tot = tot + g_ref[dev, 0:PACK_ROWS, :]
        o_ref[0:PACK_ROWS, :] = tot
        for j in range(6):
            acc = g_ref[0, 12 + j:13 + j, :] + g_ref[0, 18 + j:19 + j, :]
            for dev in range(1, N_DEV):
                acc = acc + g_ref[dev, 12 + j:13 + j, :] + g_ref[dev, 18 + j:19 + j, :]
            if j < 2:
                acc = acc + o_ref[8 + j:9 + j, :]
            o_ref[PACK_ROWS + j:PACK_ROWS + j + 1, :] = acc
        o_ref[PACK_ROWS + 6:PACK_ROWS + 8, :] = jnp.zeros((2, d), F32)

    return pl.pallas_call(body, name="small_reduce", out_shape=jax.ShapeDtypeStruct((PACK_ROWS + 8, d), F32))(gathered)


_SMALL = (("g_attn", 0, 1024), ("g_ffn", 1, 1024), ("g_final", 2, 1024), ("g_ret", 3, 512), ("g_q_lora", 4, 384),
          ("g_kv_lora", 5, 256), ("ret_decay_fwd", 6, HEADS), ("ret_decay_bwd", 10, HEADS))
_SMALL_NAMES = tuple(s[0] for s in _SMALL) + ("c_ctx", "b_ada")


def _small_final(tot, dcc, sg8, ws, ms, vs):
    d = tot.shape[1]
    n = len(_SMALL_NAMES)

    def body(*refs):
        t_ref, dcc_ref, sg_ref = refs[0:3]
        w_refs, m_refs, v_refs = refs[3:3 + n], refs[3 + n:3 + 2 * n], refs[3 + 2 * n:3 + 3 * n]
        outs = refs[3 + 3 * n:]
        g_refs, d_refs, mo_refs, vo_refs = outs[0:n], outs[n:2 * n], outs[2 * n:3 * n], outs[3 * n:4 * n]
        l_ref = outs[4 * n]

        def update(i, g, sl=None):
            pick = (lambda r: r[...]) if sl is None else (lambda r: r[:, sl])
            dl, mn, vn = _adam_math(pick(w_refs[i]), g, pick(m_refs[i]), pick(v_refs[i]))
            if sl is None:
                g_refs[i][...], d_refs[i][...], mo_refs[i][...], vo_refs[i][...] = g, dl, mn, vn
            else:
                g_refs[i][:, sl], d_refs[i][:, sl], mo_refs[i][:, sl], vo_refs[i][:, sl] = g, dl, mn, vn

        for i, (name, row, width) in enumerate(_SMALL):
            g = t_ref[row:row + 1, 0:width]
            if name == "ret_decay_fwd":
                g = g * sg_ref[0:1, 0:width]
            elif name == "ret_decay_bwd":
                g = g * sg_ref[1:2, 0:width]
            update(i, g)
        i_cc, i_b = n - 2, n - 1
        cc = w_refs[i_cc][...]
        s = 1.0 / (1.0 + jnp.exp(-cc))
        dsilu = dcc_ref[0, 0:1, :] + dcc_ref[2, 0:1, :] + dcc_ref[4, 0:1, :] + dcc_ref[6, 0:1, :]
        update(i_cc, dsilu * (s * (1.0 + cc * (1.0 - s))))
        for j in range(6):
            update(i_b, t_ref[PACK_ROWS + j:PACK_ROWS + j + 1, :], pl.ds(j * d, d))
        l_ref[...] = jnp.broadcast_to((0.5 / d) * jnp.sum(t_ref[7:8, :], keepdims=True), l_ref.shape)

    shapes = [jax.ShapeDtypeStruct(a.shape, F32) for a in ws]
    outs = pl.pallas_call(
        body, name="small_final", out_shape=shapes * 4 + [jax.ShapeDtypeStruct((8, LANES), F32)],
    )(tot, dcc, sg8, *ws, *ms, *vs)
    return outs[0:n], outs[n:2 * n], outs[2 * n:3 * n], outs[3 * n:4 * n], outs[4 * n]


_WEIGHTS = ("c_ctx", "w_ada", "b_ada", "g_attn", "g_ffn", "w_in", "ret_decay_fwd", "ret_decay_bwd", "g_ret", "g_q_lora",
            "w_uq", "g_kv_lora", "w_ukv", "w_out", "w_ff1", "w_ff2", "g_final")
_BIG = ("w_in", "w_uq", "w_ukv", "w_out", "w_ff1", "w_ff2")
_TRANSPOSED = ("w_in", "w_uq")


def kernel(x, c, ctx, c_ctx, w_ada, b_ada, g_attn, g_ffn, w_in, ret_decay_fwd, ret_decay_bwd, g_ret, g_q_lora, w_uq, g_kv_lora, w_ukv, w_out, w_ff1, w_ff2, g_final, loss_target, m_c_ctx, m_w_ada, m_b_ada, m_g_attn, m_g_ffn, m_w_in, m_ret_decay_fwd, m_ret_decay_bwd, m_g_ret, m_g_q_lora, m_w_uq, m_g_kv_lora, m_w_ukv, m_w_out, m_w_ff1, m_w_ff2, m_g_final, v_c_ctx, v_w_ada, v_b_ada, v_g_attn, v_g_ffn, v_w_in, v_ret_decay_fwd, v_ret_decay_bwd, v_g_ret, v_g_q_lora, v_w_uq, v_g_kv_lora, v_w_ukv, v_w_out, v_w_ff1, v_w_ff2, v_g_final):
    w = dict(c_ctx=c_ctx, w_ada=w_ada, b_ada=b_ada, g_attn=g_attn, g_ffn=g_ffn, w_in=w_in, ret_decay_fwd=ret_decay_fwd,
             ret_decay_bwd=ret_decay_bwd, g_ret=g_ret, g_q_lora=g_q_lora, w_uq=w_uq, g_kv_lora=g_kv_lora, w_ukv=w_ukv,
             w_out=w_out, w_ff1=w_ff1, w_ff2=w_ff2, g_final=g_final)
    m = dict(c_ctx=m_c_ctx, w_ada=m_w_ada, b_ada=m_b_ada, g_attn=m_g_attn, g_ffn=m_g_ffn, w_in=m_w_in,
             ret_decay_fwd=m_ret_decay_fwd, ret_decay_bwd=m_ret_decay_bwd, g_ret=m_g_ret, g_q_lora=m_g_q_lora, w_uq=m_w_uq,
             g_kv_lora=m_g_kv_lora, w_ukv=m_w_ukv, w_out=m_w_out, w_ff1=m_w_ff1, w_ff2=m_w_ff2, g_final=m_g_final)
    v = dict(c_ctx=v_c_ctx, w_ada=v_w_ada, b_ada=v_b_ada, g_attn=v_g_attn, g_ffn=v_g_ffn, w_in=v_w_in,
             ret_decay_fwd=v_ret_decay_fwd, ret_decay_bwd=v_ret_decay_bwd, g_ret=v_g_ret, g_q_lora=v_g_q_lora, w_uq=v_w_uq,
             g_kv_lora=v_g_kv_lora, w_ukv=v_w_ukv, w_out=v_w_out, w_ff1=v_w_ff1, w_ff2=v_w_ff2, g_final=v_g_final)
    xi, yi, ci = lax.axis_index("x"), lax.axis_index("y"), lax.axis_index("c")
    chip = 2 * xi + yi
    dev = 2 * chip + ci
    nex, seq, d = x.shape
    n_ada = w_ada.shape[2]

    dec = jnp.zeros((8, LANES), F32).at[0, :HEADS].set(ret_decay_fwd[0]).at[1, :HEADS].set(ret_decay_bwd[0])
    lg8, sg8 = _decay_prep(dec)
    lg = lg8[:2, :HEADS]

    def shard_of(t, k):
        return t[k][0].T if k in _TRANSPOSED else t[k][0]

    shard = {k: shard_of(w, k) for k in _BIG}
    head_rows = MLA_NOPE + MLA_ROPE
    shard["w_uq"] = jnp.pad(shard["w_uq"], ((0, MLA_HEAD - head_rows), (0, 0)))
    slot = chip.reshape(1).astype(jnp.int32)
    core = ci.reshape(1).astype(jnp.int32)
    slots = {k: _cast_into_slot(shard[k], slot, "cast_" + k) for k in _EARLY}
    half_ff = shard["w_ff2"].shape[0] // 2
    late_pieces = [(shard["w_out"], 0, shard["w_out"].shape[0]), (shard["w_ff1"], 0, shard["w_ff1"].shape[0]),
                   (shard["w_ff2"], 0, half_ff), (shard["w_ff2"], half_ff, half_ff)]
    late_slots, (w_in_x, w_uq_x, w_ukv_x, c8) = _cast_into_slots(
        late_pieces, slot, "cast_late",
        rider=_merge_riders(_gather_ici_rider([slots[k] for k in _EARLY]),
                            _gather8_rider(jnp.pad(c, ((0, 8 - nex), (0, 0))), in_vmem=False)))

    a_in = jnp.concatenate([c8[:, :nex].reshape(N_DEV * nex, d), c_ctx.reshape(1, d), jnp.zeros((7, d), F32)], axis=0)
    b_sh = lax.dynamic_slice(b_ada, (0, chip * n_ada), (1, n_ada))
    mod_sh = _mod_fwd(a_in, w_ada[0], b_sh)
    mod8, w_in_f, w_uq_k, w_ukv_k = _run_rider(
        _merge_riders(_gather8_rider(mod_sh), _gather_d2d_rider([w_in_x, w_uq_x, w_ukv_x])), "ag_early")
    w_in_k = jnp.pad(w_in_f.reshape(IN_COLS, d), ((0, IN_PAD - IN_COLS), (0, 0)))
    mod_all = mod8[0::2].transpose(1, 0, 2).reshape(a_in.shape[0], N_CHIPS * n_ada)
    mod_me = lax.dynamic_slice(mod_all, (nex * dev, 0), (nex, N_CHIPS * n_ada)).reshape(nex, 6, d)
    mod_c = mod_all[N_DEV * nex].reshape(1, 6, d)
    modv = jnp.pad(jnp.concatenate([mod_me, mod_c], axis=0), ((0, 0), (0, 2), (0, 0)))

    gx, g_early, late, st_post, st_ret, st_pre = _local_step(
        x, ctx, loss_target, modv, lg, g_attn, g_ffn, g_final.reshape(1, d), g_ret, g_q_lora, g_kv_lora,
        w_in_k, w_uq_k, w_ukv_k, late_slots, (core, slot))

    mine, theirs, (*late_theirs, gathered) = _reduce_scatter_vmem(
        g_early, [(IN_COLS // N_CHIPS, IN_COLS // N_CHIPS), (head_rows, MLA_HEAD), (KV_LORA, KV_LORA)],
        _merge_riders(_swap_rider(late), _gather8_rider(_pack_small(st_post, st_ret, st_pre))), "rs_early")
    tot = _small_reduce(gathered)
    dm = jnp.concatenate([
        gathered[:, 12:24].reshape(N_DEV * nex, 6 * d),
        jnp.concatenate([tot[8:10].reshape(1, 2 * d), jnp.zeros((1, 4 * d), F32)], axis=1),
        jnp.zeros((7, 6 * d), F32)], axis=0)
    dm_sh = lax.dynamic_slice(dm, (0, chip * n_ada), (dm.shape[0], n_ada))
    g_ada, da = _mod_bwd(a_in, dm_sh, w_ada[0])
    dcc = _allgather8(da[N_DEV * nex:], "ag_dcc")
    halves = dict(zip(_EARLY, zip(mine, theirs)))
    halves.update(zip(_LATE, zip(late, late_theirs)))
    grad, delta, new_m, new_v = {}, {}, {}, {}
    for k in _BIG:
        a, b = halves[k]
        res = _adamw_halves(shard_of(w, k), a, b, shard_of(m, k), shard_of(v, k), core, "adamw_" + k)
        grad[k], delta[k], new_m[k], new_v[k] = [(o.T if k in _TRANSPOSED else o).reshape(w[k].shape) for o in res]

    shp = w_ada.shape
    outs, _ = _adamw(w_ada[0], g_ada, m["w_ada"][0], v["w_ada"][0], "adamw_w_ada")
    grad["w_ada"] = g_ada.reshape(shp)
    delta["w_ada"], new_m["w_ada"], new_v["w_ada"] = [o.reshape(shp) for o in outs]
    rows = [{k: t[k].reshape(1, -1) for k in _SMALL_NAMES} for t in (w, m, v)]
    small = _small_final(tot, dcc, sg8, *[[t[k] for k in _SMALL_NAMES] for t in rows])
    for res, outs in zip((grad, delta, new_m, new_v), small[:4]):
        for k, o in zip(_SMALL_NAMES, outs):
            res[k] = o.reshape(w[k].shape)
    return (small[4][0, 0], gx, *[grad[k] for k in _WEIGHTS], *[delta[k] for k in _WEIGHTS],
            *[new_m[k] for k in _WEIGHTS], *[new_v[k] for k in _WEIGHTS])
```

```python
import functools
import math

import jax
import jax.numpy as jnp
from jax import lax
from jax.experimental import pallas as pl
from jax.experimental.pallas import tpu as pltpu

F32 = jnp.float32
BF16 = jnp.bfloat16
MESH = pl.DeviceIdType.MESH

EPS = 1e-6
D_MODEL = 1024
D_FF = 4096
HEADS = 4
RET_DK = 64
RET_DV = 128
MLA_NOPE = 128
MLA_ROPE = 64
MLA_HEAD = 256
Q_LORA = 384
KV_LORA = 256
GRID_W = 64
ROPE_BASE = 10000.0
IN_COLS = 2240
IN_PAD = 2304
PG_COLS = 1152
N_CHIPS = 4
N_DEV = 8
LANES = 128
ADAM_LR = 0.001
ADAM_B1 = 0.9
ADAM_B2 = 0.999
ADAM_EPS = 1e-08
ADAM_WD = 0.01
ADAM_STEP = 10
VMEM_LIMIT = 56 * 1024 * 1024


def _dot(a, b):
    return jnp.dot(a, b, preferred_element_type=F32)


def _dot_nt(a, b):
    return lax.dot_general(a, b, (((1,), (1,)), ((), ())), preferred_element_type=F32)


def _dot_tn(a, b):
    return lax.dot_general(a, b, (((0,), (0,)), ((), ())), preferred_element_type=F32)


def _params(sem=None, vmem=None):
    return pltpu.CompilerParams(dimension_semantics=sem, vmem_limit_bytes=vmem)


def _full(shape):
    n = len(shape)
    return pl.BlockSpec(shape, lambda *_: (0,) * n)


def _once(shape):
    n = len(shape)
    return pl.BlockSpec(shape, lambda *_: (0,) * n, pipeline_mode=pl.Buffered(1))


def _rope(x, cos, sin):
    w = x.shape[-1]
    lo = (lax.broadcasted_iota(jnp.int32, (1, w), 1) % 64) < 32
    swapped = jnp.where(lo, pltpu.roll(x, w - 32, 1), pltpu.roll(x, 32, 1))
    return x * cos + swapped * sin


def _rope_t(g, cos, sin):
    w = g.shape[-1]
    lo = (lax.broadcasted_iota(jnp.int32, (1, w), 1) % 64) < 32
    t = g * sin
    swapped = jnp.where(lo, pltpu.roll(t, w - 32, 1), pltpu.roll(t, 32, 1))
    return g * cos + swapped


def _rope_tables(seq, tm):
    rows = seq // GRID_W
    row = jnp.repeat(jnp.arange(rows, dtype=F32), GRID_W)
    col = jnp.tile(jnp.arange(GRID_W, dtype=F32), rows)
    n_freq = RET_DK // 4
    freq = ROPE_BASE ** (-jnp.arange(n_freq, dtype=F32) / n_freq)
    ang = jnp.concatenate([row[:, None] * freq, col[:, None] * freq], axis=-1)
    cos, sin = jnp.cos(ang), jnp.sin(ang)
    cos_t = jnp.tile(jnp.concatenate([cos, cos], -1), (1, HEADS))
    sin_t = jnp.tile(jnp.concatenate([-sin, sin], -1), (1, HEADS))
    cos_t = jnp.concatenate([cos_t, jnp.ones((tm, 4 * RET_DK), F32)], 0)
    sin_t = jnp.concatenate([sin_t, jnp.zeros((tm, 4 * RET_DK), F32)], 0)
    return cos_t, sin_t


def _adam_math(w, g, m, v):
    mn = ADAM_B1 * m + (1.0 - ADAM_B1) * g
    vn = ADAM_B2 * v + (1.0 - ADAM_B2) * (g * g)
    m_hat = mn / (1.0 - ADAM_B1 ** ADAM_STEP)
    v_hat = vn / (1.0 - ADAM_B2 ** ADAM_STEP)
    return -ADAM_LR * (m_hat / (jnp.sqrt(v_hat) + ADAM_EPS) + ADAM_WD * w), mn, vn


def _cast_into_slots(pieces, slot, name, rider=None):
    c = pieces[0][0].shape[1]
    rb = max(b for b in range(16, 257, 16) if all(cnt % b == 0 and st % b == 0 for _, st, cnt in pieces))
    nbs = [cnt // rb for _, _, cnt in pieces]
    starts = [sum(nbs[:s]) for s in range(len(pieces))]

    def body(s_ref, *refs):
        i = pl.program_id(0)
        for s in range(len(pieces)):
            @pl.when(jnp.logical_and(i >= starts[s], i < starts[s] + nbs[s]))
            def _():
                refs[len(pieces) + s][...] = refs[s][...].astype(BF16)

    in_specs, out_specs = [], []
    for (_, first_row, _), nb, st in zip(pieces, nbs, starts):
        in_specs.append(pl.BlockSpec((rb, c), lambda i, s, nb=nb, st=st, f=first_row // rb: (f + jnp.clip(i - st, 0, nb - 1), 0)))
        out_specs.append(pl.BlockSpec((None, rb, c), lambda i, s, nb=nb, st=st: (s[0], jnp.clip(i - st, 0, nb - 1), 0)))
    return _hosted_call(
        body, [w for w, _, _ in pieces], name=name, grid=(sum(nbs),), prefetch=(slot,), in_specs=in_specs,
        out_specs=out_specs, out_shape=[jax.ShapeDtypeStruct((N_CHIPS, cnt, c), BF16) for _, _, cnt in pieces],
        sem=("arbitrary",), rider=rider)


def _cast_into_slot(w, slot, name):
    return _cast_into_slots([(w, 0, w.shape[0])], slot, name)[0][0]


def _adamw_halves(w, mine, theirs, m, v, core, name):
    r, c = w.shape
    r2 = r // 2
    rb = max(b for b in range(8, r2 + 1, 8) if r2 % b == 0 and b * c * 4 <= (1 << 21))
    nbh = r2 // rb

    def body(z_ref, w_ref, a_ref, b_ref, m_ref, v_ref, g_ref, d_ref, mo_ref, vo_ref):
        here = (pl.program_id(0) // nbh) == z_ref[0]
        gg = jnp.where(here, a_ref[...], b_ref[...])
        g_ref[...] = gg
        d_ref[...], mo_ref[...], vo_ref[...] = _adam_math(w_ref[...], gg, m_ref[...], v_ref[...])

    spec = pl.BlockSpec((rb, c), lambda i, z: (i, 0))
    a_spec = pl.BlockSpec((rb, c), lambda i, z: (jnp.clip(i - z[0] * nbh, 0, nbh - 1), 0))
    b_spec = pl.BlockSpec((rb, c), lambda i, z: (jnp.clip(i - (1 - z[0]) * nbh, 0, nbh - 1), 0))
    shp = jax.ShapeDtypeStruct((r, c), F32)
    return pl.pallas_call(
        body, name=name,
        grid_spec=pltpu.PrefetchScalarGridSpec(
            num_scalar_prefetch=1, grid=(r // rb,), in_specs=[spec, a_spec, b_spec, spec, spec], out_specs=[spec] * 4),
        out_shape=[shp] * 4,
        compiler_params=_params(("parallel",)),
    )(core, w, mine, theirs, m, v)


def _adamw(w, g, m, v, name, rider=None):
    r, c = w.shape
    rb = r
    for cand in (256, 128, 64, 32, 16, 8):
        if r % cand == 0 and cand * c * 4 <= (1 << 20):
            rb = cand
            break
    if r * c * 4 <= (1 << 20):
        rb = r

    def body(w_ref, g_ref, m_ref, v_ref, d_ref, mo_ref, vo_ref):
        d_ref[...], mo_ref[...], vo_ref[...] = _adam_math(w_ref[...], g_ref[...], m_ref[...], v_ref[...])

    spec = pl.BlockSpec((rb, c), lambda i: (i, 0))
    shp = jax.ShapeDtypeStruct((r, c), F32)
    return _hosted_call(
        body, (w, g, m, v), name=name, grid=(r // rb,), in_specs=[spec] * 4, out_specs=[spec] * 3, out_shape=[shp] * 3,
        sem=("parallel",), rider=rider)


def _decay_prep(dec):
    def body(d_ref, lg_ref, sg_ref):
        d = d_ref[...]
        lg_ref[...] = jnp.minimum(d, 0.0) - jnp.log(1.0 + jnp.exp(-jnp.abs(d)))
        sg_ref[...] = 1.0 / (1.0 + jnp.exp(d))

    shp = jax.ShapeDtypeStruct(dec.shape, F32)
    return pl.pallas_call(body, name="decay_prep", out_shape=[shp, shp])(dec)


def _mod_fwd(a_in, w_ada, b_sh):
    rows, d = a_in.shape
    n = w_ada.shape[1]
    bn = 512

    def body(a_ref, w_ref, b_ref, o_ref):
        a = a_ref[...]
        s = (a / (1.0 + jnp.exp(-a))).astype(BF16)
        o_ref[...] = _dot(s, w_ref[...].astype(BF16)) + b_ref[...]

    return pl.pallas_call(
        body, name="mod_fwd", grid=(n // bn,),
        in_specs=[_full((rows, d)), pl.BlockSpec((d, bn), lambda j: (0, j)), pl.BlockSpec((1, bn), lambda j: (0, j))],
        out_specs=pl.BlockSpec((rows, bn), lambda j: (0, j)),
        out_shape=jax.ShapeDtypeStruct((rows, n), F32),
        compiler_params=_params(("parallel",)),
    )(a_in, w_ada, b_sh)


def _mod_bwd(a_in, dm, w_ada):
    rows, d = a_in.shape
    n = w_ada.shape[1]
    bn = 512
    nb = n // bn

    def body(a_ref, dm_ref, w_ref, gw_ref, da_ref):
        j = pl.program_id(0)
        a = a_ref[...]
        s = (a / (1.0 + jnp.exp(-a))).astype(BF16)
        dmb = dm_ref[...].astype(BF16)
        gw_ref[...] = _dot_tn(s, dmb)
        part = _dot_nt(dmb, w_ref[...].astype(BF16))

        @pl.when(j == 0)
        def _():
            da_ref[...] = part

        @pl.when(j > 0)
        def _():
            da_ref[...] += part

    return pl.pallas_call(
        body, name="mod_bwd", grid=(nb,),
        in_specs=[_full((rows, d)), pl.BlockSpec((rows, bn), lambda j: (0, j)), pl.BlockSpec((d, bn), lambda j: (0, j))],
        out_specs=[pl.BlockSpec((d, bn), lambda j: (0, j)), _full((rows, d))],
        out_shape=[jax.ShapeDtypeStruct((d, n), F32), jax.ShapeDtypeStruct((rows, d), F32)],
        compiler_params=_params(("arbitrary",)),
    )(a_in, dm, w_ada)


def _pre_fwd(x2, ctx2, modv, g_attn, w_in, g_q, g_kv, w_uq, w_ukv, cos_t, sin_t, *, seq, tm, rider=None):
    t_lat, d = x2.shape
    t_ctx = ctx2.shape[0]
    nl, nc = t_lat // tm, t_ctx // tm
    n_all = t_lat + t_ctx
    tpe = seq // tm
    nex = t_lat // seq

    def body(x_ref, c_ref, mod_ref, g_ref, win_ref, gq_ref, gkv_ref, wuq_ref, wukv_ref, cos_ref, sin_ref,
             h_ref, pg_ref, rq_ref, rk_ref, rv_ref, nq_ref, nkv_ref, q_ref, k_ref, v_ref):
        i = pl.program_id(0)
        xt = jnp.where(i < nl, x_ref[...], c_ref[...])
        sh = mod_ref[0, 0:1, :]
        sc = mod_ref[0, 1:2, :]
        r = lax.rsqrt(jnp.mean(xt * xt, axis=-1, keepdims=True) + EPS)
        hb = ((xt * r) * g_ref[...] * (1.0 + sc) + sh).astype(BF16)
        h_ref[...] = hb
        p = _dot_nt(hb, win_ref[...])
        cos = cos_ref[...]
        sin = sin_ref[...]
        rq_ref[...] = _rope(p[:, 0:256], cos, sin).astype(BF16)
        rk_ref[...] = _rope(p[:, 256:512] * (RET_DK ** -0.5), cos, sin).astype(BF16)
        rv_ref[...] = p[:, 512:1024].astype(BF16)
        pg_ref[...] = p[:, 1024:2176]
        cq = p[:, 1536:1920]
        ckv = p[:, 1920:2176]
        nqb = (cq * lax.rsqrt(jnp.mean(cq * cq, axis=-1, keepdims=True) + EPS) * gq_ref[...]).astype(BF16)
        nkvb = (ckv * lax.rsqrt(jnp.mean(ckv * ckv, axis=-1, keepdims=True) + EPS) * gkv_ref[...]).astype(BF16)
        nq_ref[...] = nqb
        nkv_ref[...] = nkvb
        cos1 = cos[:, 0:LANES]
        sin1 = sin[:, 0:LANES]
        kpe = _rope(p[:, 2176:2304], cos1, sin1).astype(BF16)
        for hd in range(HEADS):
            o = hd * MLA_HEAD
            qh = _dot_nt(nqb, wuq_ref[hd]) * MLA_SCALE
            q_ref[:, o:o + 128] = qh[:, 0:128].astype(BF16)
            q_ref[:, o + 128:o + 256] = _rope(qh[:, 128:256], cos1, sin1).astype(BF16)
            kvh = _dot(nkvb, wukv_ref[hd])
            k_ref[:, o:o + 128] = kvh[:, 0:128].astype(BF16)
            k_ref[:, o + 128:o + 256] = kpe
            v_ref[:, hd * 128:(hd + 1) * 128] = kvh[:, 128:256].astype(BF16)

    def tile(width):
        return pl.BlockSpec((tm, width), lambda i: (i, 0))

    widths = (d, PG_COLS, 256, 256, 512, Q_LORA, KV_LORA, HEADS * MLA_HEAD, HEADS * MLA_HEAD, HEADS * 128)
    dtypes = (BF16, F32, BF16, BF16, BF16, BF16, BF16, BF16, BF16, BF16)
    tab = pl.BlockSpec((tm, 256), lambda i: (jnp.where(i < nl, i % tpe, tpe), 0))
    return _hosted_call(
        body, (x2, ctx2, modv, g_attn, w_in, g_q, g_kv, w_uq, w_ukv, cos_t, sin_t), name="pre_fwd", grid=(nl + nc,),
        in_specs=[
            pl.BlockSpec((tm, d), lambda i: (jnp.minimum(i, nl - 1), 0)),
            pl.BlockSpec((tm, d), lambda i: (jnp.maximum(i - nl, 0), 0)),
            pl.BlockSpec((1, 8, d), lambda i: (jnp.minimum(i // tpe, nex), 0, 0)),
            _full((1, d)), _full(w_in.shape), _full((1, Q_LORA)), _full((1, KV_LORA)),
            _full(w_uq.shape), _full(w_ukv.shape), tab, tab,
        ],
        out_specs=[tile(w) for w in widths],
        out_shape=[jax.ShapeDtypeStruct((n_all, w), dt) for w, dt in zip(widths, dtypes)],
        sem=("parallel",), rider=rider)


def _post(yret, ymla, x2, tgt2, modv, g_ffn, g_fin, w_out, w_ff1, w_ff2a, w_ff2b, *, seq, tm):
    t_lat, d = x2.shape
    nl = t_lat // tm
    tpe = seq // tm
    nex = t_lat // seq
    n_slab = w_ff1.shape[0]
    fs = w_ff1.shape[2]
    fh = w_ff2a.shape[1]

    def body(yr_ref, ym_ref, x_ref, t_ref, mod_ref, gf_ref, gl_ref, wo_ref, w1_ref, w2a_ref, w2b_ref,
             mix_ref, a_ref, du_ref, h2_ref, df_ref, dmo_ref, dmix_ref, dxm_ref, st_ref, ru_ref):
        i = pl.program_id(0)
        gt_a = mod_ref[0, 2:3, :]
        sh_f = mod_ref[0, 3:4, :]
        sc_f = mod_ref[0, 4:5, :]
        gt_f = mod_ref[0, 5:6, :]
        g_ffn_v = gf_ref[...]
        g_fin_v = gl_ref[...]
        yr = yr_ref[...]
        ym = ym_ref[...]
        mix_ref[:, 0:512] = yr
        mix_ref[:, 512:1024] = ym
        op = _dot(yr, wo_ref[0:512, :]) + _dot(ym, wo_ref[512:1024, :])
        x_mid = x_ref[...] + gt_a * op
        r2 = lax.rsqrt(jnp.mean(x_mid * x_mid, axis=-1, keepdims=True) + EPS)
        xh2 = x_mid * r2
        h2b = (xh2 * g_ffn_v * (1.0 + sc_f) + sh_f).astype(BF16)
        h2_ref[...] = h2b
        f = jnp.zeros((tm, d), F32)
        for s in range(n_slab):
            ru = jnp.maximum(_dot(h2b, w1_ref[s]), 0.0)
            ru_ref[:, s * fs:(s + 1) * fs] = ru
            ab = (ru * ru).astype(BF16)
            a_ref[:, s * fs:(s + 1) * fs] = ab
            f = f + _dot(ab[:, 0:fh], w2a_ref[s]) + _dot(ab[:, fh:fs], w2b_ref[s])
        x_out = x_mid + gt_f * f
        r3 = lax.rsqrt(jnp.mean(x_out * x_out, axis=-1, keepdims=True) + EPS)
        xh3 = x_out * r3
        err = xh3 * g_fin_v - t_ref[...]
        dy = err * (1.0 / d)
        dxh3 = dy * g_fin_v
        dx_out = r3 * (dxh3 - xh3 * jnp.mean(dxh3 * xh3, axis=-1, keepdims=True))
        dfb = (dx_out * gt_f).astype(BF16)
        df_ref[...] = dfb
        dh2 = jnp.zeros((tm, d), F32)
        for s in range(n_slab):
            da = jnp.concatenate([_dot_nt(dfb, w2a_ref[s]), _dot_nt(dfb, w2b_ref[s])], axis=1)
            dub = (da * (2.0 * ru_ref[:, s * fs:(s + 1) * fs])).astype(BF16)
            du_ref[:, s * fs:(s + 1) * fs] = dub
            dh2 = dh2 + _dot_nt(dub, w1_ref[s])
        dxh2 = dh2 * (1.0 + sc_f) * g_ffn_v
        dx_mid = dx_out + r2 * (dxh2 - xh2 * jnp.mean(dxh2 * xh2, axis=-1, keepdims=True))
        dxm_ref[...] = dx_mid
        dmob = (dx_mid * gt_a).astype(BF16)
        dmo_ref[...] = dmob
        dmix_ref[...] = _dot_nt(dmob, wo_ref[...]).astype(BF16)

        def rsum(v):
            return jnp.sum(v, axis=0, keepdims=True)

        stats = jnp.concatenate([
            rsum(dh2), rsum(dh2 * xh2 * g_ffn_v), rsum(dx_out * f), rsum(dx_mid * op),
            rsum(dh2 * (1.0 + sc_f) * xh2), rsum(dy * xh3), rsum(err * err), jnp.zeros((1, d), F32)], axis=0)

        @pl.when(i % tpe == 0)
        def _():
            st_ref[0] = stats

        @pl.when(i % tpe != 0)
        def _():
            st_ref[0] += stats

    def tile(width):
        return pl.BlockSpec((tm, width), lambda i: (i, 0))

    widths = (d, D_FF, D_FF, d, d, d, d, d)
    dtypes = (BF16, BF16, BF16, BF16, BF16, BF16, BF16, F32)
    const = pl.Buffered(1)
    return pl.pallas_call(
        body, name="post", grid=(nl,),
        in_specs=[
            tile(512), tile(512), tile(d), tile(d),
            pl.BlockSpec((1, 8, d), lambda i: (i // tpe, 0, 0)),
            _full((1, d)), _full((1, d)),
            pl.BlockSpec(w_out.shape, lambda i: (0, 0), pipeline_mode=const),
            pl.BlockSpec(w_ff1.shape, lambda i: (0, 0, 0), pipeline_mode=const),
            pl.BlockSpec(w_ff2a.shape, lambda i: (0, 0, 0), pipeline_mode=const),
            pl.BlockSpec(w_ff2b.shape, lambda i: (0, 0, 0), pipeline_mode=const),
        ],
        out_specs=[tile(w) for w in widths] + [pl.BlockSpec((1, 8, d), lambda i: (i // tpe, 0, 0))],
        out_shape=[jax.ShapeDtypeStruct((t_lat, w), dt) for w, dt in zip(widths, dtypes)]
        + [jax.ShapeDtypeStruct((nex, 8, d), F32)],
        scratch_shapes=[pltpu.VMEM((tm, D_FF), F32)],
        compiler_params=_params(("arbitrary",), VMEM_LIMIT),
    )(yret, ymla, x2, tgt2, modv, g_ffn, g_fin, w_out, w_ff1, w_ff2a, w_ff2b)


def _pre_bwd(x2, ctx2, modv, g_attn, pg, drq, drk, dkc_r, drv, dvc_r, drg, dq_m, dkl, dkc, dvl, dvc, dxm,
             w_in, g_q, g_kv, w_uq, w_ukv, cos_t, sin_t, *, seq, tm, rider=None):
    t_lat, d = x2.shape
    t_ctx = ctx2.shape[0]
    nl, nc = t_lat // tm, t_ctx // tm
    n_all = t_lat + t_ctx
    tpe = seq // tm
    nex = t_lat // seq

    def body(x_ref, c_ref, mod_ref, g_ref, pg_ref, drq_ref, drk_ref, dkcr_ref, drv_ref, dvcr_ref, drg_ref,
             dq_ref, dkl_ref, dkc_ref, dvl_ref, dvc_ref, dxm_ref, win_ref, gq_ref, gkv_ref, wuq_ref, wukv_ref,
             cos_ref, sin_ref, dpb_ref, dqf_ref, dkvf_ref, gx_ref, st_ref):
        i = pl.program_id(0)
        lat = i < nl
        latf = lat.astype(F32)
        cos = cos_ref[...]
        sin = sin_ref[...]
        cos1 = cos[:, 0:LANES]
        sin1 = sin[:, 0:LANES]
        d_rq = _rope_t(drq_ref[...] * latf, cos, sin)
        d_rk = _rope_t(jnp.where(lat, drk_ref[...], dkcr_ref[...]), cos, sin) * (RET_DK ** -0.5)
        d_rv = jnp.where(lat, drv_ref[...], dvcr_ref[...])
        d_rg = drg_ref[...] * latf
        dq_all = dq_ref[...] * (latf * MLA_SCALE)
        dk_all = jnp.where(lat, dkl_ref[...], dkc_ref[...])
        dv_all = jnp.where(lat, dvl_ref[...], dvc_ref[...])
        dnq = jnp.zeros((tm, Q_LORA), F32)
        dnkv = jnp.zeros((tm, KV_LORA), F32)
        dkpe = jnp.zeros((tm, LANES), F32)
        for hd in range(HEADS):
            o = hd * MLA_HEAD
            dqh = jnp.concatenate([dq_all[:, o:o + 128], _rope_t(dq_all[:, o + 128:o + 256], cos1, sin1)],
                                  axis=1).astype(BF16)
            dqf_ref[:, o:o + 256] = dqh
            dnq = dnq + _dot(dqh, wuq_ref[hd])
            dkpe = dkpe + dk_all[:, o + 128:o + 256]
            dkvh = jnp.concatenate([dk_all[:, o:o + 128], dv_all[:, hd * 128:(hd + 1) * 128]], axis=1).astype(BF16)
            dkvf_ref[:, o:o + 256] = dkvh
            dnkv = dnkv + _dot_nt(dkvh, wukv_ref[hd])
        d_kpe = _rope_t(dkpe, cos1, sin1)
        pgv = pg_ref[...]
        cq = pgv[:, 512:896]
        ckv = pgv[:, 896:1152]
        rq_ = lax.rsqrt(jnp.mean(cq * cq, axis=-1, keepdims=True) + EPS)
        cqh = cq * rq_
        dcqh = dnq * gq_ref[...]
        d_cq = rq_ * (dcqh - cqh * jnp.mean(dcqh * cqh, axis=-1, keepdims=True))
        rkv_ = lax.rsqrt(jnp.mean(ckv * ckv, axis=-1, keepdims=True) + EPS)
        ckvh = ckv * rkv_
        dckvh = dnkv * gkv_ref[...]
        d_ckv = rkv_ * (dckvh - ckvh * jnp.mean(dckvh * ckvh, axis=-1, keepdims=True))
        dpb = jnp.concatenate([d_rq, d_rk, d_rv, d_rg, d_cq, d_ckv, d_kpe], axis=1).astype(BF16)
        dpb_ref[...] = dpb
        dh = _dot(dpb, win_ref[...])
        xt = jnp.where(lat, x_ref[...], c_ref[...])
        sc = mod_ref[0, 1:2, :]
        g = g_ref[...]
        r = lax.rsqrt(jnp.mean(xt * xt, axis=-1, keepdims=True) + EPS)
        xh = xt * r
        dxh = dh * (1.0 + sc) * g
        dx = r * (dxh - xh * jnp.mean(dxh * xh, axis=-1, keepdims=True))

        @pl.when(lat)
        def _():
            gx_ref[...] = dxm_ref[...] + dx

        def rsum(v):
            return jnp.sum(v, axis=0, keepdims=True)

        def widen(v):
            return jnp.concatenate([v, jnp.zeros((1, d - v.shape[1]), F32)], axis=1)

        stats = jnp.concatenate([
            rsum(dh), rsum(dh * xh * g), rsum(dh * (1.0 + sc) * xh), widen(rsum(dnq * cqh)), widen(rsum(dnkv * ckvh)),
            jnp.zeros((3, d), F32)], axis=0)
        first = jnp.logical_or(jnp.logical_and(lat, i % tpe == 0), i == nl)

        @pl.when(first)
        def _():
            st_ref[0] = stats

        @pl.when(jnp.logical_not(first))
        def _():
            st_ref[0] += stats

    def lat_tile(width):
        return pl.BlockSpec((tm, width), lambda i: (jnp.minimum(i, nl - 1), 0))

    def ctx_tile(width):
        return pl.BlockSpec((tm, width), lambda i: (jnp.maximum(i - nl, 0), 0))

    def tile(width):
        return pl.BlockSpec((tm, width), lambda i: (i, 0))

    tab = pl.BlockSpec((tm, 256), lambda i: (jnp.where(i < nl, i % tpe, tpe), 0))
    ex = pl.BlockSpec((1, 8, d), lambda i: (jnp.minimum(i // tpe, nex), 0, 0))
    return _hosted_call(
        body, (x2, ctx2, modv, g_attn, pg, drq, drk, dkc_r, drv, dvc_r, drg, dq_m, dkl, dkc, dvl, dvc, dxm,
               w_in, g_q, g_kv, w_uq, w_ukv, cos_t, sin_t), name="pre_bwd", grid=(nl + nc,),
        in_specs=[
            lat_tile(d), ctx_tile(d), ex, _full((1, d)), tile(PG_COLS),
            lat_tile(256), lat_tile(256), ctx_tile(256), lat_tile(512), ctx_tile(512), lat_tile(512),
            lat_tile(1024), lat_tile(1024), ctx_tile(1024), lat_tile(512), ctx_tile(512), lat_tile(d),
            _once(w_in.shape), _full((1, Q_LORA)), _full((1, KV_LORA)), _once(w_uq.shape), _once(w_ukv.shape),
            tab, tab,
        ],
        out_specs=[tile(IN_PAD), tile(1024), tile(1024), lat_tile(d), ex],
        out_shape=[
            jax.ShapeDtypeStruct((n_all, IN_PAD), BF16), jax.ShapeDtypeStruct((n_all, 1024), BF16),
            jax.ShapeDtypeStruct((n_all, 1024), BF16), jax.ShapeDtypeStruct((t_lat, d), F32),
            jax.ShapeDtypeStruct((nex + 1, 8, d), F32),
        ],
        sem=("arbitrary",), rider=rider)


MLA_SCALE = 1.0 / math.sqrt(MLA_NOPE + MLA_ROPE)
KEY_BLOCK = 1024


def _mla_specs(t_lat, seq, ctx_len, tq, heads=1):
    nqt = seq // tq
    cb = t_lat // ctx_len
    q = pl.BlockSpec((tq, heads * MLA_HEAD), lambda b, h, j: (b * nqt + j, h))
    kl = pl.BlockSpec((seq, heads * MLA_HEAD), lambda b, h, j: (b, h))
    kc = pl.BlockSpec((ctx_len, heads * MLA_HEAD), lambda b, h, j: (cb + b, h))
    vl = pl.BlockSpec((seq, heads * 128), lambda b, h, j: (b, h))
    vc = pl.BlockSpec((ctx_len, heads * 128), lambda b, h, j: (cb + b, h))
    o = pl.BlockSpec((tq, heads * 128), lambda b, h, j: (b * nqt + j, h))
    return q, kl, kc, vl, vc, o


FWD_HEADS = 2
BWD_HEADS = 1


def _mla_fwd(q, k, v, *, t_lat, seq, ctx_len, tq, rider=None):
    nex = t_lat // seq

    def body(q_ref, kl_ref, kc_ref, vl_ref, vc_ref, o_ref, lse_ref):
        for hh in range(FWD_HEADS):
            wide = slice(hh * MLA_HEAD, (hh + 1) * MLA_HEAD)
            cols = slice(hh * 128, (hh + 1) * 128)
            qb = q_ref[:, wide]
            s = _dot_nt(qb, kl_ref[:, wide])
            sc = _dot_nt(qb, kc_ref[:, wide])
            m = jnp.maximum(jnp.max(s, axis=-1, keepdims=True), jnp.max(sc, axis=-1, keepdims=True))
            p = jnp.exp(s - m)
            pc = jnp.exp(sc - m)
            total = jnp.sum(p, axis=-1, keepdims=True) + jnp.sum(pc, axis=-1, keepdims=True)
            o = _dot(p.astype(BF16), vl_ref[:, cols]) + _dot(pc.astype(BF16), vc_ref[:, cols])
            o_ref[:, cols] = (o * (1.0 / total)).astype(BF16)
            lse_ref[:, cols] = jnp.broadcast_to(m + jnp.log(total), (tq, 128))

    qs, kl, kc, vl, vc, os_ = _mla_specs(t_lat, seq, ctx_len, tq, FWD_HEADS)
    return _hosted_call(
        body, (q, k, k, v, v), name="mla_fwd", grid=(nex, HEADS // FWD_HEADS, seq // tq),
        in_specs=[qs, kl, kc, vl, vc], out_specs=[os_, os_],
        out_shape=[jax.ShapeDtypeStruct((t_lat, HEADS * 128), BF16), jax.ShapeDtypeStruct((t_lat, HEADS * 128), F32)],
        sem=("parallel", "parallel", "arbitrary"), rider=rider)


def _mla_bwd(q, k, v, ymla, lse, dmix, *, t_lat, seq, ctx_len, tq, rider=None):
    nex = t_lat // seq
    nqt = seq // tq
    t_ctx = nex * ctx_len
    kb = min(KEY_BLOCK, seq)

    def body(q_ref, kl_ref, kc_ref, vl_ref, vc_ref, o_ref, lse_ref, do_ref, dq_ref, dkl_out, dkc_out, dvl_out, dvc_out,
             dkl_ref, dkc_ref, dvl_ref, dvc_ref):
        j = pl.program_id(2)

        @pl.when(j == 0)
        def _():
            dkl_ref[...] = jnp.zeros(dkl_ref.shape, F32)
            dkc_ref[...] = jnp.zeros(dkc_ref.shape, F32)
            dvl_ref[...] = jnp.zeros(dvl_ref.shape, F32)
            dvc_ref[...] = jnp.zeros(dvc_ref.shape, F32)

        for hh in range(BWD_HEADS):
            wide = slice(hh * MLA_HEAD, (hh + 1) * MLA_HEAD)
            cols = slice(hh * 128, (hh + 1) * 128)
            qb = q_ref[:, wide]
            dob = do_ref[:, cols]
            delta = jnp.sum(dob.astype(F32) * o_ref[:, cols].astype(F32), axis=-1, keepdims=True)
            lse_row = lse_ref[:, hh * 128:hh * 128 + 1]

            def block(k_ref, v_ref, dk_ref, dv_ref, rows):
                kbl = k_ref[rows, wide]
                vbl = v_ref[rows, cols]
                p = jnp.exp(_dot_nt(qb, kbl) - lse_row)
                ds = (p * (_dot_nt(dob, vbl) - delta)).astype(BF16)
                dk_ref[rows, wide] += _dot_tn(ds, qb)
                dv_ref[rows, cols] += _dot_tn(p.astype(BF16), dob)
                return _dot(ds, kbl)

            dq = block(kc_ref, vc_ref, dkc_ref, dvc_ref, pl.ds(0, ctx_len))
            for i in range(seq // kb):
                dq = dq + block(kl_ref, vl_ref, dkl_ref, dvl_ref, pl.ds(i * kb, kb))
            dq_ref[:, wide] = dq.astype(BF16)

        @pl.when(j == nqt - 1)
        def _():
            dkl_out[...] = dkl_ref[...].astype(BF16)
            dkc_out[...] = dkc_ref[...].astype(BF16)
            dvl_out[...] = dvl_ref[...].astype(BF16)
            dvc_out[...] = dvc_ref[...].astype(BF16)

    g = BWD_HEADS
    qs, kl, kc, vl, vc, os_ = _mla_specs(t_lat, seq, ctx_len, tq, g)
    do_spec = pl.BlockSpec((tq, g * 128), lambda b, h, j: (b * nqt + j, HEADS // g + h))
    key_blocks = [(seq, g * MLA_HEAD), (ctx_len, g * MLA_HEAD), (seq, g * 128), (ctx_len, g * 128)]
    return _hosted_call(
        body, (q, k, k, v, v, ymla, lse, dmix), name="mla_bwd", grid=(nex, HEADS // g, nqt),
        in_specs=[qs, kl, kc, vl, vc, os_, os_, do_spec],
        out_specs=[qs] + [pl.BlockSpec(blk, lambda b, h, j: (b, h)) for blk in key_blocks],
        out_shape=[
            jax.ShapeDtypeStruct((t_lat, HEADS * MLA_HEAD), BF16),
            jax.ShapeDtypeStruct((t_lat, HEADS * MLA_HEAD), BF16),
            jax.ShapeDtypeStruct((t_ctx, HEADS * MLA_HEAD), BF16),
            jax.ShapeDtypeStruct((t_lat, HEADS * 128), BF16),
            jax.ShapeDtypeStruct((t_ctx, HEADS * 128), BF16),
        ],
        scratch_shapes=[pltpu.VMEM(blk, F32) for blk in key_blocks],
        sem=("parallel", "parallel", "arbitrary"), rider=rider)


def _decay_terms(lg, chunk, forward):
    ii = lax.broadcasted_iota(jnp.int32, (chunk, chunk), 0)
    jj = lax.broadcasted_iota(jnp.int32, (chunk, chunk), 1)
    diff = (ii - jj) if forward else (jj - ii)
    dist = jnp.maximum(diff, 0).astype(F32)
    dmat = jnp.where(diff >= 0, jnp.exp(lg * dist), 0.0)
    pos = lax.broadcasted_iota(jnp.int32, (chunk, 1), 0).astype(F32)
    if forward:
        e_q = pos + 1.0
        e_k = (chunk - 1.0) - pos
    else:
        e_q = chunk - pos
        e_k = pos
    wq = jnp.exp(lg * e_q)
    wk = jnp.exp(lg * e_k)
    cd = jnp.exp(jnp.full((1, 1), lg * chunk, F32))
    return dmat, dist, wq, wk, e_q, e_k, cd


def _ctx_weights(lg, ctx_len, forward):
    pos = lax.broadcasted_iota(jnp.int32, (ctx_len, 1), 0).astype(F32)
    e = ((ctx_len - 1.0) - pos) if forward else pos
    return jnp.exp(lg * e), e


def _pair_specs(t_lat, seq, ctx_len):
    cb = t_lat // ctx_len
    qk = pl.BlockSpec((seq, 128), lambda b, p: (b, p))
    v = pl.BlockSpec((seq, 256), lambda b, p: (b, p))
    kc = pl.BlockSpec((ctx_len, 128), lambda b, p: (cb + b, p))
    vc = pl.BlockSpec((ctx_len, 256), lambda b, p: (cb + b, p))
    return qk, v, kc, vc


def _lane_masks():
    lane = lax.broadcasted_iota(jnp.int32, (1, 128), 1)
    return [(lane // RET_DK) == hh for hh in (0, 1)]


def _ret_fwd_pair(rq, rk, rv, pg, lg, g_ret, *, t_lat, seq, ctx_len, chunk, rider=None):
    nex = t_lat // seq
    n_chunk = seq // chunk

    def body(q_ref, k_ref, v_ref, kc_ref, vc_ref, rg_ref, lg_ref, g_ref, y_ref, o_ref):
        pair = pl.program_id(1)
        masks = _lane_masks()
        kcf = kc_ref[...].astype(F32)
        chains = [(forward, hh) for forward in (True, False) for hh in (0, 1)]
        terms, s0 = [], []
        for forward, hh in chains:
            lgd = lg_ref[0 if forward else 1, 2 * pair + hh]
            terms.append(_decay_terms(lgd, chunk, forward))
            wc, _ = _ctx_weights(lgd, ctx_len, forward)
            s0.append(_dot_tn((jnp.where(masks[hh], kcf, 0.0) * wc).astype(BF16), vc_ref[:, hh * 128:(hh + 1) * 128]))
        both = [terms[hh][0] + terms[2 + hh][0] for hh in (0, 1)]
        o_ref[...] = jnp.zeros(o_ref.shape, F32)

        def step(t, states):
            new = [None] * 4
            for forward in (True, False):
                n = t if forward else n_chunk - 1 - t
                sl = pl.ds(pl.multiple_of(n * chunk, chunk), chunk)
                qb = q_ref[sl, :]
                kf_all = k_ref[sl, :].astype(F32)
                for hh in (0, 1):
                    c = (0 if forward else 2) + hh
                    _, _, wq, wk, _, _, cd = terms[c]
                    cols = slice(hh * 128, (hh + 1) * 128)
                    qm = jnp.where(masks[hh], qb, jnp.zeros((), BF16))
                    kf = jnp.where(masks[hh], kf_all, 0.0)
                    vb = v_ref[sl, cols]
                    o = wq * _dot(qm, states[c].astype(BF16))
                    if forward:
                        o = o + _dot((_dot_nt(qm, kf.astype(BF16)) * both[hh]).astype(BF16), vb)
                    o_ref[sl, cols] += o
                    new[c] = cd * states[c] + _dot_tn((kf * wk).astype(BF16), vb)
            return tuple(new)

        lax.fori_loop(0, n_chunk, step, tuple(s0))

        def norm_step(n, carry):
            sl = pl.ds(pl.multiple_of(n * chunk, chunk), chunk)
            for hh in (0, 1):
                cols = slice(hh * 128, (hh + 1) * 128)
                o = o_ref[sl, cols]
                mu = jnp.mean(o, axis=-1, keepdims=True)
                oc = o - mu
                var = jnp.mean(oc * oc, axis=-1, keepdims=True)
                rg = rg_ref[sl, cols]
                y_ref[sl, cols] = (oc * lax.rsqrt(var + EPS) * g_ref[:, cols] * (rg / (1.0 + jnp.exp(-rg)))).astype(BF16)
            return carry

        lax.fori_loop(0, n_chunk, norm_step, 0)

    qk, v, kc, vc = _pair_specs(t_lat, seq, ctx_len)
    return _hosted_call(
        body, (rq, rk, rv, rk, rv, pg, lg, g_ret), name="ret_fwd", grid=(nex, HEADS // 2),
        in_specs=[qk, qk, v, kc, vc, v, pl.BlockSpec(memory_space=pltpu.SMEM), pl.BlockSpec((1, 256), lambda b, p: (0, p))],
        out_specs=[v, v],
        out_shape=[jax.ShapeDtypeStruct((t_lat, HEADS * RET_DV), BF16), jax.ShapeDtypeStruct((t_lat, HEADS * RET_DV), F32)],
        sem=("parallel", "arbitrary"), rider=rider)


def _ret_bwd_pair(rq, rk, rv, pg, osum, dmix, lg, g_ret, *, t_lat, seq, ctx_len, chunk, rider=None):
    nex = t_lat // seq
    n_chunk = seq // chunk
    t_ctx = nex * ctx_len

    def body(q_ref, k_ref, v_ref, kc_ref, vc_ref, rg_ref, o_ref, dy_ref, lg_ref, g_ref,
             dq_out, dk_out, dv_out, dkc_ref, dvc_ref, drg_ref, st_ref, do_s, s_st, dq_ref, dk_ref, dv_ref):
        pair = pl.program_id(1)
        masks = _lane_masks()
        kcf = kc_ref[...].astype(F32)

        def norm_step(n, dgains):
            sl = pl.ds(pl.multiple_of(n * chunk, chunk), chunk)
            out = []
            for hh in (0, 1):
                cols = slice(hh * 128, (hh + 1) * 128)
                gain = g_ref[:, cols]
                o = o_ref[sl, cols]
                mu = jnp.mean(o, axis=-1, keepdims=True)
                oc = o - mu
                rstd = lax.rsqrt(jnp.mean(oc * oc, axis=-1, keepdims=True) + EPS)
                ohat = oc * rstd
                rg = rg_ref[sl, cols]
                sg = 1.0 / (1.0 + jnp.exp(-rg))
                dy = dy_ref[sl, cols].astype(F32)
                don = dy * (rg * sg)
                drg_ref[sl, cols] = (dy * (ohat * gain) * (sg * (1.0 + rg * (1.0 - sg)))).astype(BF16)
                dohat = don * gain
                do_s[sl, cols] = rstd * (dohat - jnp.mean(dohat, axis=-1, keepdims=True)
                                         - ohat * jnp.mean(dohat * ohat, axis=-1, keepdims=True))
                out.append(dgains[hh] + jnp.sum(don * ohat, axis=0, keepdims=True))
            return tuple(out)

        zero_row = jnp.zeros((1, 128), F32)
        dgains = lax.fori_loop(0, n_chunk, norm_step, (zero_row, zero_row))
        dq_ref[...] = jnp.zeros(dq_ref.shape, F32)
        dk_ref[...] = jnp.zeros(dk_ref.shape, F32)
        dv_ref[...] = jnp.zeros(dv_ref.shape, F32)

        chains = [(forward, hh) for forward in (True, False) for hh in (0, 1)]
        terms, ctxw, s0 = [], [], []
        for forward, hh in chains:
            lgd = lg_ref[0 if forward else 1, 2 * pair + hh]
            terms.append(_decay_terms(lgd, chunk, forward))
            ctxw.append(_ctx_weights(lgd, ctx_len, forward))
            s0.append(_dot_tn((jnp.where(masks[hh], kcf, 0.0) * ctxw[-1][0]).astype(BF16), vc_ref[:, hh * 128:(hh + 1) * 128]))

        def chunk_at(t, ascending):
            n = t if ascending else n_chunk - 1 - t
            return n, pl.ds(pl.multiple_of(n * chunk, chunk), chunk)

        def state_step(t, states):
            new = []
            for c, (forward, hh) in enumerate(chains):
                n, sl = chunk_at(t, forward)
                wk, cd = terms[c][3], terms[c][6]
                s_st[c, n] = states[c]
                kf = jnp.where(masks[hh], k_ref[sl, :].astype(F32), 0.0)
                new.append(cd * states[c] + _dot_tn((kf * wk).astype(BF16), v_ref[sl, hh * 128:(hh + 1) * 128]))
            return tuple(new)

        lax.fori_loop(0, n_chunk, state_step, tuple(s0))

        both = [terms[hh][0] + terms[2 + hh][0] for hh in (0, 1)]

        def grad_step(t, carry):
            out = [None] * len(chains)
            in_chunk_b = [None, None]
            for forward in (True, False):
                n, sl = chunk_at(t, not forward)
                qb = q_ref[sl, :]
                kf_all = k_ref[sl, :].astype(F32)
                dq_sum = jnp.zeros((chunk, 128), F32)
                dk_sum = jnp.zeros((chunk, 128), F32)
                for hh in (0, 1):
                    c = (0 if forward else 2) + hh
                    g_next, dlg = carry[c]
                    dmat, dist, wq, wk, e_q, e_k, cd = terms[c]
                    cols = slice(hh * 128, (hh + 1) * 128)
                    qm = jnp.where(masks[hh], qb, jnp.zeros((), BF16))
                    kf = jnp.where(masks[hh], kf_all, 0.0)
                    kb = kf.astype(BF16)
                    vb = v_ref[sl, cols]
                    do = do_s[sl, cols]
                    dob = do.astype(BF16)
                    s_n = s_st[c, n]
                    s_nb = s_n.astype(BF16)
                    gb = g_next.astype(BF16)
                    dk_cross = wk * _dot_nt(vb, gb)
                    dv = _dot((kf * wk).astype(BF16), gb)
                    o_cross = wq * _dot(qm, s_nb)
                    dq_sum = dq_sum + wq * _dot_nt(dob, s_nb)
                    dk_sum = dk_sum + dk_cross
                    dlg = (dlg + chunk * cd * jnp.sum(g_next * s_n, keepdims=True)
                           + jnp.sum(e_k * jnp.sum(kf * dk_cross, axis=-1, keepdims=True), keepdims=True)
                           + jnp.sum(e_q * jnp.sum(o_cross * do, axis=-1, keepdims=True), keepdims=True))
                    if forward:
                        a_raw = _dot_nt(qm, kb)
                        da_raw = _dot_nt(dob, vb)
                        prod = a_raw * da_raw
                        dlg = dlg + jnp.sum(dist * dmat * prod, keepdims=True)
                        in_chunk_b[hh] = jnp.sum(terms[2 + hh][1] * terms[2 + hh][0] * prod, keepdims=True)
                        dab = (da_raw * both[hh]).astype(BF16)
                        dq_sum = dq_sum + _dot(dab, kb)
                        dk_sum = dk_sum + _dot_tn(dab, qm)
                        dv = dv + _dot_tn((a_raw * both[hh]).astype(BF16), dob)
                    else:
                        dlg = dlg + in_chunk_b[hh]
                    dv_ref[sl, cols] += dv
                    out[c] = (cd * g_next + _dot_tn((qm.astype(F32) * wq).astype(BF16), dob), dlg)
                dq_ref[sl, :] += dq_sum
                dk_ref[sl, :] += dk_sum
            return tuple(out)

        zero = (jnp.zeros((128, 128), F32), jnp.zeros((1, 1), F32))
        res = lax.fori_loop(0, n_chunk, grad_step, (zero,) * len(chains))
        dkc_sum = jnp.zeros((ctx_len, 128), F32)
        dvc = [jnp.zeros((ctx_len, 128), F32)] * 2
        dlgs = []
        for c, (forward, hh) in enumerate(chains):
            ds0, dlg = res[c]
            wc, e_c = ctxw[c]
            kcm = jnp.where(masks[hh], kcf, 0.0)
            ds0b = ds0.astype(BF16)
            dkc_part = wc * _dot_nt(vc_ref[:, hh * 128:(hh + 1) * 128], ds0b)
            dkc_sum = dkc_sum + dkc_part
            dvc[hh] = dvc[hh] + _dot((kcm * wc).astype(BF16), ds0b)
            dlgs.append(dlg + jnp.sum(e_c * jnp.sum(kcm * dkc_part, axis=-1, keepdims=True), keepdims=True))
        dq_out[...] = dq_ref[...].astype(BF16)
        dk_out[...] = dk_ref[...].astype(BF16)
        dv_out[...] = dv_ref[...].astype(BF16)
        dkc_ref[...] = dkc_sum
        for hh in (0, 1):
            cols = slice(hh * 128, (hh + 1) * 128)
            dvc_ref[:, cols] = dvc[hh]
            st_ref[0, :, cols] = jnp.concatenate([
                dgains[hh], jnp.broadcast_to(dlgs[hh], (1, 128)), jnp.broadcast_to(dlgs[2 + hh], (1, 128)),
                jnp.zeros((5, 128), F32)], axis=0)

    qk, v, kc, vc = _pair_specs(t_lat, seq, ctx_len)
    return _hosted_call(
        body, (rq, rk, rv, rk, rv, pg, osum, dmix, lg, g_ret), name="ret_bwd", grid=(nex, HEADS // 2),
        in_specs=[qk, qk, v, kc, vc, v, v, v, pl.BlockSpec(memory_space=pltpu.SMEM),
                  pl.BlockSpec((1, 256), lambda b, p: (0, p))],
        out_specs=[
            qk, qk, v,
            pl.BlockSpec((ctx_len, 128), lambda b, p: (b, p)),
            pl.BlockSpec((ctx_len, 256), lambda b, p: (b, p)),
            v,
            pl.BlockSpec((1, 8, 256), lambda b, p: (b, 0, p)),
        ],
        out_shape=[
            jax.ShapeDtypeStruct((t_lat, 256), BF16), jax.ShapeDtypeStruct((t_lat, 256), BF16),
            jax.ShapeDtypeStruct((t_lat, 512), BF16), jax.ShapeDtypeStruct((t_ctx, 256), F32),
            jax.ShapeDtypeStruct((t_ctx, 512), F32), jax.ShapeDtypeStruct((t_lat, 512), BF16),
            jax.ShapeDtypeStruct((nex, 8, 512), F32),
        ],
        scratch_shapes=[pltpu.VMEM((seq, 256), F32), pltpu.VMEM((4, n_chunk, 128, 128), F32),
                        pltpu.VMEM((seq, 128), F32), pltpu.VMEM((seq, 128), F32), pltpu.VMEM((seq, 256), F32)],
        sem=("parallel", "arbitrary"), rider=rider)


def _matmul_tn(a, b, *, bm, bn, bk, chip_major, name, out_dtype=F32, rider=None):
    tk, m = a.shape
    n = b.shape[1]
    slab = n // N_CHIPS
    per_block = bn // slab if chip_major else 1
    bk = max(c for c in range(LANES, min(bk, tk) + 1, LANES) if tk % c == 0)
    nk = tk // bk
    blk = (per_block, bm, slab) if chip_major else (bm, bn)

    def body(a_ref, b_ref, o_ref, acc_ref):
        k = pl.program_id(2)
        if chip_major:
            parts = [_dot_tn(a_ref[...], b_ref[:, s * slab:(s + 1) * slab]) for s in range(per_block)]
        else:
            parts = [_dot_tn(a_ref[...], b_ref[...])]

        @pl.when(k == 0)
        def _():
            for s, part in enumerate(parts):
                if chip_major:
                    acc_ref[s] = part
                else:
                    acc_ref[...] = part

        @pl.when(k > 0)
        def _():
            for s, part in enumerate(parts):
                if chip_major:
                    acc_ref[s] += part
                else:
                    acc_ref[...] += part

        @pl.when(k == nk - 1)
        def _():
            o_ref[...] = acc_ref[...].astype(out_dtype)

    if chip_major:
        out_spec = pl.BlockSpec(blk, lambda i, j, k: (j, i, 0))
        out_shape = jax.ShapeDtypeStruct((N_CHIPS, m, slab), out_dtype)
    else:
        out_spec = pl.BlockSpec(blk, lambda i, j, k: (i, j))
        out_shape = jax.ShapeDtypeStruct((m, n), out_dtype)
    (out,), carried = _hosted_call(
        body, (a, b), name=name, grid=(m // bm, n // bn, nk),
        in_specs=[pl.BlockSpec((bk, bm), lambda i, j, k: (k, i)), pl.BlockSpec((bk, bn), lambda i, j, k: (k, j))],
        out_specs=[out_spec], out_shape=[out_shape], scratch_shapes=[pltpu.VMEM(blk, F32)],
        sem=("parallel", "parallel", "arbitrary"), rider=rider)
    return out if rider is None else (out, carried)


_LATE = ("w_out", "w_ff1", "w_ff2")
_EARLY = ("w_in", "w_uq", "w_ukv")


def _local_step(x, ctx, tgt, modv, lg, g_attn, g_ffn, g_fin, g_ret, g_q, g_kv, w_in, w_uq, w_ukv, late, place=None,
                *, tm=256, tq=256, chunk=256):
    nex, seq, d = x.shape
    ctx_len = ctx.shape[1]
    t_lat = nex * seq
    tm = min(tm, seq)
    x2 = x.reshape(t_lat, d)
    ctx2 = ctx.reshape(nex * ctx_len, d)
    tgt2 = tgt.reshape(t_lat, d)
    tm_fwd = min(2 * tm, seq)
    cos_t, sin_t = _rope_tables(seq, tm)
    dims = dict(t_lat=t_lat, seq=seq, ctx_len=ctx_len)
    alone = place is None

    (hb, pg, rq, rk, rv, nq, nkv, q, k, v), crossed_a = _pre_fwd(
        x2, ctx2, modv, g_attn, w_in, g_q, g_kv, w_uq, w_ukv, *_rope_tables(seq, tm_fwd), seq=seq, tm=tm_fwd,
        rider=None if alone else _gather_ici_rider([late[2]]))
    (yret, osum), got = _ret_fwd_pair(
        rq, rk, rv, pg, lg, g_ret, chunk=min(2 * chunk, seq), **dims,
        rider=None if alone else _merge_riders(_gather_d2d_rider(crossed_a), _gather_ici_rider([late[3]])))
    (ymla, lse), got_rest = _mla_fwd(
        q, k, v, tq=tq, **dims,
        rider=None if alone else _merge_riders(_gather_rider([late[0], late[1]], staged=True), _gather_d2d_rider(got[1:])))
    w_out, w_ff1, w_ff2a, w_ff2b = late if alone else (got_rest[0], got_rest[1], got[0], got_rest[2])
    mix, act, du, h2, df, dmo, dmix, dxm, st_post = _post(yret, ymla, x2, tgt2, modv, g_ffn, g_fin, w_out.reshape(d, d),
                                                         w_ff1, w_ff2a, w_ff2b, seq=seq, tm=min(tm, 256))
    kw = dict(bm=1024, bn=1024, bk=2048, out_dtype=BF16)
    g_ff2 = _matmul_tn(act, df, chip_major=False, name="gw_ff2", **kw).reshape(N_CHIPS, D_FF // N_CHIPS, d)
    if alone:
        g_ff1 = _matmul_tn(h2, du, chip_major=True, name="gw_ff1", **kw)
        g_out = _matmul_tn(mix, dmo, chip_major=False, name="gw_out", **kw).reshape(N_CHIPS, d // N_CHIPS, d)
        (dq_m, dkl, dkc, dvl, dvc), _ = _mla_bwd(q, k, v, ymla, lse, dmix, tq=tq, **dims)
        (drq, drk, drv, dkc_r, dvc_r, drg, st_ret), _ = _ret_bwd_pair(rq, rk, rv, pg, osum, dmix, lg, g_ret, chunk=chunk,
                                                                      **dims)
        late_out = [g_out, g_ff1, g_ff2]
    else:
        core, slot = place
        g_ff1, x_ff2 = _matmul_tn(h2, du, chip_major=True, name="gw_ff1", rider=_exchange_rider([g_ff2]), **kw)
        g_out, x_ff1 = _matmul_tn(mix, dmo, chip_major=False, name="gw_out", rider=_exchange_rider([g_ff1]), **kw)
        g_out = g_out.reshape(N_CHIPS, d // N_CHIPS, d)
        p_ff2 = _add_half(g_ff2, x_ff2[0], core, "add_half_w_ff2")
        p_ff1 = _add_half(g_ff1, x_ff1[0], core, "add_half_w_ff1")
        (dq_m, dkl, dkc, dvl, dvc), (l_ff2, l_ff1, x_out) = _mla_bwd(
            q, k, v, ymla, lse, dmix, tq=min(seq, 512), **dims,
            rider=_merge_riders(_scatter_rider([p_ff2, p_ff1]), _exchange_rider([g_out])))
        p_out = _add_half(g_out, x_out, core, "add_half_w_out")
        m_ff2 = _sum_chips(p_ff2, l_ff2, slot, "sum_chips_w_ff2")
        m_ff1 = _sum_chips(p_ff1, l_ff1, slot, "sum_chips_w_ff1")
        (drq, drk, drv, dkc_r, dvc_r, drg, st_ret), (l_out,) = _ret_bwd_pair(
            rq, rk, rv, pg, osum, dmix, lg, g_ret, chunk=chunk, **dims, rider=_scatter_rider([p_out]))
        late_out = [_sum_chips(p_out, l_out, slot, "sum_chips_w_out"), m_ff1, m_ff2]
    (dpb, dqf, dkvf, gx, st_pre), _ = _pre_bwd(
        x2, ctx2, modv, g_attn, pg, drq, drk, dkc_r, drv, dvc_r, drg, dq_m, dkl, dkc, dvl, dvc, dxm, w_in, g_q, g_kv,
        w_uq, w_ukv, cos_t, sin_t, seq=seq, tm=tm)
    g_early = [
        _matmul_tn(dpb, hb, bm=IN_PAD // 2, bn=d, bk=1536, chip_major=False, name="gw_in"),
        _matmul_tn(dqf, nq, bm=HEADS * MLA_HEAD, bn=Q_LORA, bk=1536, chip_major=False, name="gw_uq"),
        _matmul_tn(nkv, dkvf, bm=KV_LORA, bn=HEADS * 256, bk=1536, chip_major=True, name="gw_ukv"),
    ]
    return gx.reshape(nex, seq, d), g_early, late_out, st_post, st_ret, st_pre


_ANY = pl.BlockSpec(memory_space=pl.ANY)
_VMEM = pl.BlockSpec(memory_space=pltpu.VMEM)
_OFFSETS = tuple((dx, dy, dc) for dx in (0, 1) for dy in (0, 1) for dc in (0, 1))[1:]
_CHIP_OFFSETS = ((1, 0), (0, 1), (1, 1))


def _place():
    return lax.axis_index("x"), lax.axis_index("y"), lax.axis_index("c")


def _flip(v, d):
    return 1 - v if d else v


def _gather8_rider(a, in_vmem=True):
    def copies(a_ref, o_ref, send, recv):
        x, y, z = _place()
        me = 4 * x + 2 * y + z
        out = []
        for k, (dx, dy, dc) in enumerate(_OFFSETS):
            peer = (_flip(x, dx), _flip(y, dy), _flip(z, dc))
            landing = o_ref.at[4 * peer[0] + 2 * peer[1] + peer[2]]
            out.append((
                pltpu.make_async_remote_copy(src_ref=a_ref, dst_ref=o_ref.at[me], send_sem=send.at[k],
                                             recv_sem=recv.at[k], device_id=peer, device_id_type=MESH),
                pltpu.make_async_remote_copy(src_ref=a_ref, dst_ref=landing, send_sem=send.at[k],
                                             recv_sem=recv.at[k], device_id=peer, device_id_type=MESH)))
        return me, out

    def start(ins, outs, sems):
        me, cps = copies(ins[0], outs[0], sems[0], sems[1])
        pltpu.make_async_copy(ins[0], outs[0].at[me], sems[2]).start()
        for out_cp, _ in cps:
            out_cp.start()

    def finish(ins, outs, sems):
        me, cps = copies(ins[0], outs[0], sems[0], sems[1])
        for out_cp, in_cp in cps:
            in_cp.wait_recv()
            out_cp.wait_send()
        pltpu.make_async_copy(ins[0], outs[0].at[me], sems[2]).wait()

    spec = [_VMEM] if in_vmem else [_ANY]
    return _Rider([a], [jax.ShapeDtypeStruct((N_DEV,) + a.shape, a.dtype)],
                  [pltpu.SemaphoreType.DMA((7,)), pltpu.SemaphoreType.DMA((7,)), pltpu.SemaphoreType.DMA],
                  start, finish, in_specs=spec, out_specs=spec)


def _merge_riders(*riders):
    ins, outs, sems, in_specs, out_specs, aliases, cuts = [], [], [], [], [], {}, []
    for r in riders:
        cuts.append((len(ins), len(outs), len(sems)))
        aliases.update({len(ins) + i: len(outs) + j for i, j in r.aliases.items()})
        ins += r.ins
        outs += r.out_shapes
        sems += r.sems
        in_specs += r.in_specs
        out_specs += r.out_specs

    def part(r, cut, r_ins, r_outs, r_sems):
        return (r_ins[cut[0]:cut[0] + len(r.ins)], r_outs[cut[1]:cut[1] + len(r.out_shapes)],
                r_sems[cut[2]:cut[2] + len(r.sems)])

    def start(r_ins, r_outs, r_sems):
        for r, cut in zip(riders, cuts):
            r.start(*part(r, cut, r_ins, r_outs, r_sems))

    def finish(r_ins, r_outs, r_sems):
        for r, cut in zip(riders, cuts):
            r.finish(*part(r, cut, r_ins, r_outs, r_sems))

    def middle(r_ins, r_outs, r_sems):
        for r, cut in zip(riders, cuts):
            if r.middle is not None:
                r.middle(*part(r, cut, r_ins, r_outs, r_sems))

    return _Rider(ins, outs, sems, start, finish, aliases=aliases, in_specs=in_specs, out_specs=out_specs,
                  middle=middle if any(r.middle is not None for r in riders) else None)


def _allgather8(a, name):
    return _run_rider(_gather8_rider(a), name)[0]


BF16_TILE_ROWS = 16


def _half(o, slot, which):
    r2 = o.shape[1] // 2
    if r2 % BF16_TILE_ROWS == 0:
        return o.at[slot, pl.ds(which * r2, r2)]
    c2 = o.shape[2] // 2
    assert c2 % LANES == 0
    return o.at[slot, :, pl.ds(which * c2, c2)]


def _gather_send(o_refs, send, recv):
    x, y, z = _place()
    chip = 2 * x + y
    for a, o in enumerate(o_refs):
        r2 = o.shape[1] // 2
        mine = _half(o, chip, z)
        for k, (dx, dy) in enumerate(_CHIP_OFFSETS):
            pltpu.make_async_remote_copy(
                src_ref=mine, dst_ref=mine, send_sem=send.at[a, k], recv_sem=recv.at[a, k],
                device_id=(_flip(x, dx), _flip(y, dy), z), device_id_type=MESH).start()


def _gather_landed(o_refs, send, recv, then=None):
    x, y, z = _place()
    chip = 2 * x + y
    for a, o in enumerate(o_refs):
        for k, (dx, dy) in enumerate(_CHIP_OFFSETS):
            landed = _half(o, 2 * _flip(x, dx) + _flip(y, dy), z)
            pltpu.make_async_remote_copy(
                src_ref=landed, dst_ref=landed, send_sem=send.at[a, k], recv_sem=recv.at[a, k],
                device_id=(_flip(x, dx), _flip(y, dy), z), device_id_type=MESH).wait_recv()
            if then is not None:
                then(a, k, landed)
    for a, o in enumerate(o_refs):
        mine = _half(o, chip, z)
        for k, (dx, dy) in enumerate(_CHIP_OFFSETS):
            pltpu.make_async_remote_copy(
                src_ref=mine, dst_ref=mine, send_sem=send.at[a, k], recv_sem=recv.at[a, k],
                device_id=(_flip(x, dx), _flip(y, dy), z), device_id_type=MESH).wait_send()


def _pass_on(o_refs, fsend, frecv, a, k, landed):
    x, y, z = _place()
    pltpu.make_async_remote_copy(
        src_ref=landed, dst_ref=landed, send_sem=fsend.at[a, k], recv_sem=frecv.at[a, k],
        device_id=(x, y, 1 - z), device_id_type=MESH).start()


def _passed_on(o_refs, fsend, frecv):
    x, y, z = _place()
    for a, o in enumerate(o_refs):
        for k, (dx, dy) in enumerate(_CHIP_OFFSETS):
            other = 2 * _flip(x, dx) + _flip(y, dy)
            got = _half(o, other, 1 - z)
            gave = _half(o, other, z)
            pltpu.make_async_remote_copy(
                src_ref=got, dst_ref=got, send_sem=fsend.at[a, k], recv_sem=frecv.at[a, k],
                device_id=(x, y, 1 - z), device_id_type=MESH).wait_recv()
            pltpu.make_async_remote_copy(
                src_ref=gave, dst_ref=gave, send_sem=fsend.at[a, k], recv_sem=frecv.at[a, k],
                device_id=(x, y, 1 - z), device_id_type=MESH).wait_send()


def _gather_finish(o_refs, send, recv, fsend, frecv):
    _gather_landed(o_refs, send, recv, functools.partial(_pass_on, o_refs, fsend, frecv))
    _passed_on(o_refs, fsend, frecv)


class _Rider:
    def __init__(self, ins, out_shapes, sems, start, finish, aliases=None, in_specs=None, out_specs=None, middle=None):
        self.ins, self.out_shapes, self.sems = list(ins), list(out_shapes), list(sems)
        self.start, self.finish, self.aliases = start, finish, dict(aliases or {})
        self.middle = middle
        self.in_specs = list(in_specs) if in_specs else [_ANY] * len(self.ins)
        self.out_specs = list(out_specs) if out_specs else [_ANY] * len(self.out_shapes)


def _run_rider(rider, name):
    r_in, r_out = len(rider.ins), len(rider.out_shapes)

    def body(*refs):
        ins, outs, sems = refs[:r_in], refs[r_in:r_in + r_out], refs[r_in + r_out:]
        rider.start(ins, outs, sems)
        if rider.middle is not None:
            rider.middle(ins, outs, sems)
        rider.finish(ins, outs, sems)

    return pl.pallas_call(
        body, name=name, in_specs=rider.in_specs, out_specs=rider.out_specs, out_shape=rider.out_shapes,
        input_output_aliases=rider.aliases, scratch_shapes=rider.sems,
    )(*rider.ins)


def _hosted_call(body, args, *, name, grid, in_specs, out_specs, out_shape, scratch_shapes=(), sem, rider=None,
                 prefetch=()):
    scratch_shapes = list(scratch_shapes)
    n_pf, n_in, n_out, n_sc = len(prefetch), len(in_specs), len(out_specs), len(scratch_shapes)
    r_in, r_out = (len(rider.ins), len(rider.out_shapes)) if rider else (0, 0)
    last = tuple(g - 1 for g in grid)

    def hosted(*refs):
        p = 0
        parts = []
        for cnt in (n_pf, n_in, r_in, n_out, r_out, n_sc):
            parts.append(refs[p:p + cnt])
            p += cnt
        pf, ins, r_ins, outs, r_outs, scratch = parts
        sems = refs[p:]
        ids = [pl.program_id(a) for a in range(len(grid))]
        is_first = functools.reduce(jnp.logical_and, [i == 0 for i in ids])
        is_last = functools.reduce(jnp.logical_and, [i == e for i, e in zip(ids, last)])

        @pl.when(is_first)
        def _():
            rider.start(r_ins, r_outs, sems)

        if rider.middle is not None:
            linear = functools.reduce(lambda acc, ig: acc * ig[1] + ig[0], zip(ids, grid), 0)

            @pl.when(linear == math.prod(grid) * 3 // 4)
            def _():
                rider.middle(r_ins, r_outs, sems)

        body(*pf, *ins, *outs, *scratch)

        @pl.when(is_last)
        def _():
            rider.finish(r_ins, r_outs, sems)

    if rider is None:
        kern, all_in, all_out, shapes, scratch, aliases, extra = body, list(in_specs), list(out_specs), list(out_shape), \
            scratch_shapes, {}, []
    else:
        kern, all_in, all_out = hosted, list(in_specs) + rider.in_specs, list(out_specs) + rider.out_specs
        shapes, scratch, extra = list(out_shape) + rider.out_shapes, scratch_shapes + rider.sems, rider.ins
        aliases = {n_pf + n_in + i: n_out + j for i, j in rider.aliases.items()}
        sem = ("arbitrary",) * len(grid)
    if prefetch:
        spec = dict(grid_spec=pltpu.PrefetchScalarGridSpec(
            num_scalar_prefetch=n_pf, grid=grid, in_specs=all_in, out_specs=all_out, scratch_shapes=scratch))
    else:
        spec = dict(grid=grid, in_specs=all_in, out_specs=all_out, scratch_shapes=scratch)
    res = pl.pallas_call(kern, name=name, out_shape=shapes, input_output_aliases=aliases,
                         compiler_params=_params(sem, VMEM_LIMIT), **spec)(*prefetch, *args, *extra)
    return list(res[:n_out]), list(res[n_out:])


def _gather_rider(ws, staged=False):
    n = len(ws)
    shapes = [jax.ShapeDtypeStruct(w.shape, w.dtype) for w in ws]
    sems = [pltpu.SemaphoreType.DMA((n, 3))] * 4
    aliases = {a: a for a in range(n)}

    def start(ins, outs, s):
        _gather_send(outs, s[0], s[1])

    if not staged:
        return _Rider(ws, shapes, sems, start, lambda ins, outs, s: _gather_finish(outs, *s), aliases=aliases)
    return _Rider(
        ws, shapes, sems, start, lambda ins, outs, s: _passed_on(outs, s[2], s[3]), aliases=aliases,
        middle=lambda ins, outs, s: _gather_landed(outs, s[0], s[1], functools.partial(_pass_on, outs, s[2], s[3])))


def _gather_ici_rider(ws):
    n = len(ws)
    return _Rider(
        ws, [jax.ShapeDtypeStruct(w.shape, w.dtype) for w in ws], [pltpu.SemaphoreType.DMA((n, 3))] * 2,
        lambda ins, outs, sems: _gather_send(outs, sems[0], sems[1]),
        lambda ins, outs, sems: _gather_landed(outs, sems[0], sems[1]),
        aliases={a: a for a in range(n)})


def _gather_d2d_rider(ws):
    n = len(ws)

    def start(ins, outs, sems):
        x, y, z = _place()
        for a, o in enumerate(outs):
            for k, (dx, dy) in enumerate(_CHIP_OFFSETS):
                _pass_on(outs, sems[0], sems[1], a, k, _half(o, 2 * _flip(x, dx) + _flip(y, dy), z))

    return _Rider(
        ws, [jax.ShapeDtypeStruct(w.shape, w.dtype) for w in ws], [pltpu.SemaphoreType.DMA((n, 3))] * 2,
        start, lambda ins, outs, sems: _passed_on(outs, sems[0], sems[1]), aliases={a: a for a in range(n)})


def _copies_rider(ins, out_shapes, sem_shape, make):
    def start(r_ins, r_outs, sems):
        for cp in make(r_ins, r_outs, sems[0], sems[1]):
            cp.start()

    def finish(r_ins, r_outs, sems):
        for cp in make(r_ins, r_outs, sems[0], sems[1]):
            cp.wait()

    return _Rider(ins, out_shapes, [pltpu.SemaphoreType.DMA(sem_shape)] * 2, start, finish)


def _exchange_rider(gs):
    def make(g_refs, r_refs, send, recv):
        x, y, z = _place()
        return [pltpu.make_async_remote_copy(
            src_ref=g.at[:, pl.ds((1 - z) * (g.shape[1] // 2), g.shape[1] // 2)], dst_ref=r, send_sem=send.at[a],
            recv_sem=recv.at[a], device_id=(x, y, 1 - z), device_id_type=MESH)
            for a, (g, r) in enumerate(zip(g_refs, r_refs))]

    shapes = [jax.ShapeDtypeStruct((g.shape[0], g.shape[1] // 2, g.shape[2]), g.dtype) for g in gs]
    return _copies_rider(gs, shapes, (len(gs),), make)


def _add_half(g, recv, core, name):
    s, r, c = g.shape
    r2 = r // 2
    rb = r2
    for cand in (256, 128, 64):
        if r2 % cand == 0:
            rb = cand
            break
    g4 = g.reshape(s, 2, r2, c)

    def body(core_ref, g_ref, r_ref, o_ref):
        o_ref[...] = (g_ref[...].astype(F32) + r_ref[...].astype(F32)).astype(BF16)

    return pl.pallas_call(
        body, name=name,
        grid_spec=pltpu.PrefetchScalarGridSpec(
            num_scalar_prefetch=1, grid=(s, r2 // rb),
            in_specs=[pl.BlockSpec((None, None, rb, c), lambda i, j, cr: (i, cr[0], j, 0)),
                      pl.BlockSpec((None, rb, c), lambda i, j, cr: (i, j, 0))],
            out_specs=pl.BlockSpec((None, rb, c), lambda i, j, cr: (i, j, 0))),
        out_shape=jax.ShapeDtypeStruct((s, r2, c), BF16),
        compiler_params=_params(("parallel", "parallel")),
    )(core, g4, recv)


def _scatter_rider(ps):
    def make(p_refs, o_refs, send, recv):
        x, y, z = _place()
        copies = []
        for a, (p, o) in enumerate(zip(p_refs, o_refs)):
            for k, (dx, dy) in enumerate(_CHIP_OFFSETS):
                other = 2 * _flip(x, dx) + _flip(y, dy)
                copies.append(pltpu.make_async_remote_copy(
                    src_ref=p.at[other], dst_ref=o.at[k], send_sem=send.at[a, k], recv_sem=recv.at[a, k],
                    device_id=(_flip(x, dx), _flip(y, dy), z), device_id_type=MESH))
        return copies

    shapes = [jax.ShapeDtypeStruct((3,) + p.shape[1:], p.dtype) for p in ps]
    return _copies_rider(ps, shapes, (len(ps), 3), make)


def _sum_chips(p, landed, chip, name):
    _, r2, c = p.shape
    rb = r2
    for cand in (256, 128, 64):
        if r2 % cand == 0:
            rb = cand
            break

    def body(s_ref, p_ref, l_ref, o_ref):
        acc = p_ref[...].astype(F32)
        for k in range(3):
            acc = acc + l_ref[k].astype(F32)
        o_ref[...] = acc

    return pl.pallas_call(
        body, name=name,
        grid_spec=pltpu.PrefetchScalarGridSpec(
            num_scalar_prefetch=1, grid=(r2 // rb,),
            in_specs=[pl.BlockSpec((None, rb, c), lambda i, s: (s[0], i, 0)),
                      pl.BlockSpec((3, rb, c), lambda i, s: (0, i, 0))],
            out_specs=pl.BlockSpec((rb, c), lambda i, s: (i, 0))),
        out_shape=jax.ShapeDtypeStruct((r2, c), F32),
        compiler_params=_params(("parallel",)),
    )(chip, p, landed)


def _swap_rider(hs):
    def make(h_refs, o_refs, send, recv):
        x, y, z = _place()
        return [pltpu.make_async_remote_copy(
            src_ref=h, dst_ref=o, send_sem=send.at[a], recv_sem=recv.at[a], device_id=(x, y, 1 - z),
            device_id_type=MESH) for a, (h, o) in enumerate(zip(h_refs, o_refs))]

    return _copies_rider(hs, [jax.ShapeDtypeStruct(h.shape, h.dtype) for h in hs], (len(hs),), make)


def _reduce_scatter_vmem(gs, rows, rider, name):
    n = len(gs)
    r_in, r_out = len(rider.ins), len(rider.out_shapes)
    halves = [(r // 2, g.shape[-1]) for g, (r, _) in zip(gs, rows)]
    piece_cols = 2 * LANES
    pieces = [(a, slice(c0, min(c0 + piece_cols, h[1]))) for a, h in enumerate(halves) for c0 in range(0, h[1], piece_cols)]
    n_p = len(pieces)

    def body(*refs):
        p = 0
        parts = []
        for cnt in (n, r_in, n, n, r_out, n, n, n, 6):
            parts.append(refs[p:p + cnt])
            p += cnt
        g_refs, r_ins, mine, theirs, r_outs, recv, part, land, sems = parts
        r_sems = refs[p:]
        xs, xr, ss, sr, ws, wr = sems
        x, y, z = _place()
        chip = 2 * x + y
        sib = (x, y, 1 - z)
        rider.start(r_ins, r_outs, r_sems)

        def half_of(a, s, which):
            r2 = halves[a][0]
            if len(g_refs[a].shape) == 3:
                return g_refs[a].at[s, pl.ds(pl.multiple_of(which * r2, 8), r2)]
            return g_refs[a].at[pl.ds(pl.multiple_of(s * rows[a][1] + which * r2, 8), r2)]

        def exchange(i):
            a, cols = pieces[i]
            return [pltpu.make_async_remote_copy(
                src_ref=half_of(a, s, 1 - z).at[:, cols], dst_ref=recv[a].at[s, :, cols], send_sem=xs.at[i, s],
                recv_sem=xr.at[i, s], device_id=sib, device_id_type=MESH) for s in range(N_CHIPS)]

        def scatter(i):
            a, cols = pieces[i]
            return [pltpu.make_async_remote_copy(
                src_ref=part[a].at[2 * _flip(x, dx) + _flip(y, dy), :, cols], dst_ref=land[a].at[k, :, cols],
                send_sem=ss.at[i, k], recv_sem=sr.at[i, k], device_id=(_flip(x, dx), _flip(y, dy), z),
                device_id_type=MESH) for k, (dx, dy) in enumerate(_CHIP_OFFSETS)]

        def swap(i):
            a, cols = pieces[i]
            return pltpu.make_async_remote_copy(
                src_ref=mine[a].at[:, cols], dst_ref=theirs[a].at[:, cols], send_sem=ws.at[i], recv_sem=wr.at[i],
                device_id=sib, device_id_type=MESH)

        for i in range(len(pieces)):
            for cp in exchange(i):
                cp.start()
        for i, (a, cols) in enumerate(pieces):
            for cp in exchange(i):
                cp.wait()
            for s in range(N_CHIPS):
                part[a][s, :, cols] = (half_of(a, s, z)[:, cols] + recv[a][s, :, cols]).astype(BF16)
            for cp in scatter(i):
                cp.start()
        for i, (a, cols) in enumerate(pieces):
            for cp in scatter(i):
                cp.wait()
            acc = part[a][chip, :, cols].astype(F32)
            for k in range(3):
                acc = acc + land[a][k, :, cols].astype(F32)
            mine[a][:, cols] = acc
            swap(i).start()
        for i in range(len(pieces)):
            swap(i).wait()
        rider.finish(r_ins, r_outs, r_sems)

    half_shapes = [jax.ShapeDtypeStruct(h, F32) for h in halves]
    res = pl.pallas_call(
        body, name=name, in_specs=[_VMEM] * n + rider.in_specs, out_specs=[_VMEM] * (2 * n) + rider.out_specs,
        out_shape=half_shapes + half_shapes + rider.out_shapes,
        scratch_shapes=[pltpu.VMEM((N_CHIPS,) + h, F32) for h in halves] + [pltpu.VMEM((N_CHIPS,) + h, BF16) for h in halves]
        + [pltpu.VMEM((3,) + h, BF16) for h in halves]
        + [pltpu.SemaphoreType.DMA((n_p, N_CHIPS))] * 2 + [pltpu.SemaphoreType.DMA((n_p, 3))] * 2
        + [pltpu.SemaphoreType.DMA((n_p,))] * 2 + rider.sems,
        input_output_aliases={n + i: 2 * n + j for i, j in rider.aliases.items()},
        compiler_params=_params(None, VMEM_LIMIT),
    )(*gs, *rider.ins)
    return list(res[:n]), list(res[n:2 * n]), list(res[2 * n:])


SMALL_ROWS = 32
PACK_ROWS = 16


def _pack_small(st_post, st_ret, st_pre):
    d = st_post.shape[2]

    def body(po_ref, re_ref, pr_ref, o_ref):
        o_ref[...] = jnp.zeros(o_ref.shape, F32)
        o_ref[0:1, :] = pr_ref[0, 2:3, :] + pr_ref[1, 2:3, :] + pr_ref[2, 2:3, :]
        o_ref[1:2, :] = po_ref[0, 4:5, :] + po_ref[1, 4:5, :]
        o_ref[2:3, :] = po_ref[0, 5:6, :] + po_ref[1, 5:6, :]
        o_ref[3:4, 0:512] = re_ref[0, 0:1, :] + re_ref[1, 0:1, :]
        o_ref[4:5, :] = pr_ref[0, 3:4, :] + pr_ref[1, 3:4, :] + pr_ref[2, 3:4, :]
        o_ref[5:6, :] = pr_ref[0, 4:5, :] + pr_ref[1, 4:5, :] + pr_ref[2, 4:5, :]
        lane = lax.broadcasted_iota(jnp.int32, (1, LANES), 1)
        for row, src in ((6, 1), (10, 2)):
            acc = jnp.zeros((1, LANES), F32)
            for hd in range(HEADS):
                grp = re_ref[0, src:src + 1, hd * LANES:(hd + 1) * LANES] + re_ref[1, src:src + 1, hd * LANES:(hd + 1) * LANES]
                acc = acc + jnp.where(lane == hd, grp, 0.0)
            o_ref[row:row + 1, 0:LANES] = acc
        o_ref[7:8, :] = po_ref[0, 6:7, :] + po_ref[1, 6:7, :]
        o_ref[8:9, :] = pr_ref[2, 0:1, :]
        o_ref[9:10, :] = pr_ref[2, 1:2, :]
        for e in range(2):
            b = 12 + 6 * e
            o_ref[b:b + 1, :] = pr_ref[e, 0:1, :]
            o_ref[b + 1:b + 2, :] = pr_ref[e, 1:2, :]
            o_ref[b + 2:b + 3, :] = po_ref[e, 3:4, :]
            o_ref[b + 3:b + 4, :] = po_ref[e, 0:1, :]
            o_ref[b + 4:b + 5, :] = po_ref[e, 1:2, :]
            o_ref[b + 5:b + 6, :] = po_ref[e, 2:3, :]

    return pl.pallas_call(body, name="pack_small", out_shape=jax.ShapeDtypeStruct((SMALL_ROWS, d), F32))(st_post, st_ret, st_pre)


def _small_reduce(gathered):
    d = gathered.shape[2]

    def body(g_ref, o_ref):
        tot = g_ref[0, 0:PACK_ROWS, :]
        for dev in range(1, N_DEV):
            tot = tot + g_ref[dev, 0:PACK_ROWS, :]
        o_ref[0:PACK_ROWS, :] = tot
        for j in range(6):
            acc = g_ref[0, 12 + j:13 + j, :] + g_ref[0, 18 + j:19 + j, :]
            for dev in range(1, N_DEV):
                acc = acc + g_ref[dev, 12 + j:13 + j, :] + g_ref[dev, 18 + j:19 + j, :]
            if j < 2:
                acc = acc + o_ref[8 + j:9 + j, :]
            o_ref[PACK_ROWS + j:PACK_ROWS + j + 1, :] = acc
        o_ref[PACK_ROWS + 6:PACK_ROWS + 8, :] = jnp.zeros((2, d), F32)

    return pl.pallas_call(body, name="small_reduce", out_shape=jax.ShapeDtypeStruct((PACK_ROWS + 8, d), F32))(gathered)


_SMALL = (("g_attn", 0, 1024), ("g_ffn", 1, 1024), ("g_final", 2, 1024), ("g_ret", 3, 512), ("g_q_lora", 4, 384),
          ("g_kv_lora", 5, 256), ("ret_decay_fwd", 6, HEADS), ("ret_decay_bwd", 10, HEADS))
_SMALL_NAMES = tuple(s[0] for s in _SMALL) + ("c_ctx", "b_ada")


def _small_final(tot, dcc, sg8, ws, ms, vs):
    d = tot.shape[1]
    n = len(_SMALL_NAMES)

    def body(*refs):
        t_ref, dcc_ref, sg_ref = refs[0:3]
        w_refs, m_refs, v_refs = refs[3:3 + n], refs[3 + n:3 + 2 * n], refs[3 + 2 * n:3 + 3 * n]
        outs = refs[3 + 3 * n:]
        g_refs, d_refs, mo_refs, vo_refs = outs[0:n], outs[n:2 * n], outs[2 * n:3 * n], outs[3 * n:4 * n]
        l_ref = outs[4 * n]

        def update(i, g, sl=None):
            pick = (lambda r: r[...]) if sl is None else (lambda r: r[:, sl])
            dl, mn, vn = _adam_math(pick(w_refs[i]), g, pick(m_refs[i]), pick(v_refs[i]))
            if sl is None:
                g_refs[i][...], d_refs[i][...], mo_refs[i][...], vo_refs[i][...] = g, dl, mn, vn
            else:
                g_refs[i][:, sl], d_refs[i][:, sl], mo_refs[i][:, sl], vo_refs[i][:, sl] = g, dl, mn, vn

        for i, (name, row, width) in enumerate(_SMALL):
            g = t_ref[row:row + 1, 0:width]
            if name == "ret_decay_fwd":
                g = g * sg_ref[0:1, 0:width]
            elif name == "ret_decay_bwd":
                g = g * sg_ref[1:2, 0:width]
            update(i, g)
        i_cc, i_b = n - 2, n - 1
        cc = w_refs[i_cc][...]
        s = 1.0 / (1.0 + jnp.exp(-cc))
        dsilu = dcc_ref[0, 0:1, :] + dcc_ref[2, 0:1, :] + dcc_ref[4, 0:1, :] + dcc_ref[6, 0:1, :]
        update(i_cc, dsilu * (s * (1.0 + cc * (1.0 - s))))
        for j in range(6):
            update(i_b, t_ref[PACK_ROWS + j:PACK_ROWS + j + 1, :], pl.ds(j * d, d))
        l_ref[...] = jnp.broadcast_to((0.5 / d) * jnp.sum(t_ref[7:8, :], keepdims=True), l_ref.shape)

    shapes = [jax.ShapeDtypeStruct(a.shape, F32) for a in ws]
    outs = pl.pallas_call(
        body, name="small_final", out_shape=shapes * 4 + [jax.ShapeDtypeStruct((8, LANES), F32)],
    )(tot, dcc, sg8, *ws, *ms, *vs)
    return outs[0:n], outs[n:2 * n], outs[2 * n:3 * n], outs[3 * n:4 * n], outs[4 * n]


_WEIGHTS = ("c_ctx", "w_ada", "b_ada", "g_attn", "g_ffn", "w_in", "ret_decay_fwd", "ret_decay_bwd", "g_ret", "g_q_lora",
            "w_uq", "g_kv_lora", "w_ukv", "w_out", "w_ff1", "w_ff2", "g_final")
_BIG = ("w_in", "w_uq", "w_ukv", "w_out", "w_ff1", "w_ff2")
_TRANSPOSED = ("w_in", "w_uq")


def kernel(x, c, ctx, c_ctx, w_ada, b_ada, g_attn, g_ffn, w_in, ret_decay_fwd, ret_decay_bwd, g_ret, g_q_lora, w_uq, g_kv_lora, w_ukv, w_out, w_ff1, w_ff2, g_final, loss_target, m_c_ctx, m_w_ada, m_b_ada, m_g_attn, m_g_ffn, m_w_in, m_ret_decay_fwd, m_ret_decay_bwd, m_g_ret, m_g_q_lora, m_w_uq, m_g_kv_lora, m_w_ukv, m_w_out, m_w_ff1, m_w_ff2, m_g_final, v_c_ctx, v_w_ada, v_b_ada, v_g_attn, v_g_ffn, v_w_in, v_ret_decay_fwd, v_ret_decay_bwd, v_g_ret, v_g_q_lora, v_w_uq, v_g_kv_lora, v_w_ukv, v_w_out, v_w_ff1, v_w_ff2, v_g_final):
    w = dict(c_ctx=c_ctx, w_ada=w_ada, b_ada=b_ada, g_attn=g_attn, g_ffn=g_ffn, w_in=w_in, ret_decay_fwd=ret_decay_fwd,
             ret_decay_bwd=ret_decay_bwd, g_ret=g_ret, g_q_lora=g_q_lora, w_uq=w_uq, g_kv_lora=g_kv_lora, w_ukv=w_ukv,
             w_out=w_out, w_ff1=w_ff1, w_ff2=w_ff2, g_final=g_final)
    m = dict(c_ctx=m_c_ctx, w_ada=m_w_ada, b_ada=m_b_ada, g_attn=m_g_attn, g_ffn=m_g_ffn, w_in=m_w_in,
             ret_decay_fwd=m_ret_decay_fwd, ret_decay_bwd=m_ret_decay_bwd, g_ret=m_g_ret, g_q_lora=m_g_q_lora, w_uq=m_w_uq,
             g_kv_lora=m_g_kv_lora, w_ukv=m_w_ukv, w_out=m_w_out, w_ff1=m_w_ff1, w_ff2=m_w_ff2, g_final=m_g_final)
    v = dict(c_ctx=v_c_ctx, w_ada=v_w_ada, b_ada=v_b_ada, g_attn=v_g_attn, g_ffn=v_g_ffn, w_in=v_w_in,
             ret_decay_fwd=v_ret_decay_fwd, ret_decay_bwd=v_ret_decay_bwd, g_ret=v_g_ret, g_q_lora=v_g_q_lora, w_uq=v_w_uq,
             g_kv_lora=v_g_kv_lora, w_ukv=v_w_ukv, w_out=v_w_out, w_ff1=v_w_ff1, w_ff2=v_w_ff2, g_final=v_g_final)
    xi, yi, ci = lax.axis_index("x"), lax.axis_index("y"), lax.axis_index("c")
    chip = 2 * xi + yi
    dev = 2 * chip + ci
    nex, seq, d = x.shape
    n_ada = w_ada.shape[2]

    dec = jnp.zeros((8, LANES), F32).at[0, :HEADS].set(ret_decay_fwd[0]).at[1, :HEADS].set(ret_decay_bwd[0])
    lg8, sg8 = _decay_prep(dec)
    lg = lg8[:2, :HEADS]

    def shard_of(t, k):
        return t[k][0].T if k in _TRANSPOSED else t[k][0]

    shard = {k: shard_of(w, k) for k in _BIG}
    head_rows = MLA_NOPE + MLA_ROPE
    shard["w_uq"] = jnp.pad(shard["w_uq"], ((0, MLA_HEAD - head_rows), (0, 0)))
    slot = chip.reshape(1).astype(jnp.int32)
    core = ci.reshape(1).astype(jnp.int32)
    slots = {k: _cast_into_slot(shard[k], slot, "cast_" + k) for k in _EARLY}
    half_ff = shard["w_ff2"].shape[0] // 2
    late_pieces = [(shard["w_out"], 0, shard["w_out"].shape[0]), (shard["w_ff1"], 0, shard["w_ff1"].shape[0]),
                   (shard["w_ff2"], 0, half_ff), (shard["w_ff2"], half_ff, half_ff)]
    late_slots, (w_in_f, w_uq_k, w_ukv_k, c8) = _cast_into_slots(
        late_pieces, slot, "cast_late",
        rider=_merge_riders(_gather_rider([slots[k] for k in _EARLY]),
                            _gather8_rider(jnp.pad(c, ((0, 8 - nex), (0, 0))), in_vmem=False)))

    a_in = jnp.concatenate([c8[:, :nex].reshape(N_DEV * nex, d), c_ctx.reshape(1, d), jnp.zeros((7, d), F32)], axis=0)
    b_sh = lax.dynamic_slice(b_ada, (0, chip * n_ada), (1, n_ada))
    mod_sh = _mod_fwd(a_in, w_ada[0], b_sh)
    mod8 = _allgather8(mod_sh, "ag_mod")
    w_in_k = jnp.pad(w_in_f.reshape(IN_COLS, d), ((0, IN_PAD - IN_COLS), (0, 0)))
    mod_all = mod8[0::2].transpose(1, 0, 2).reshape(a_in.shape[0], N_CHIPS * n_ada)
    mod_me = lax.dynamic_slice(mod_all, (nex * dev, 0), (nex, N_CHIPS * n_ada)).reshape(nex, 6, d)
    mod_c = mod_all[N_DEV * nex].reshape(1, 6, d)
    modv = jnp.pad(jnp.concatenate([mod_me, mod_c], axis=0), ((0, 0), (0, 2), (0, 0)))

    gx, g_early, late, st_post, st_ret, st_pre = _local_step(
        x, ctx, loss_target, modv, lg, g_attn, g_ffn, g_final.reshape(1, d), g_ret, g_q_lora, g_kv_lora,
        w_in_k, w_uq_k, w_ukv_k, late_slots, (core, slot))

    mine, theirs, (*late_theirs, gathered) = _reduce_scatter_vmem(
        g_early, [(IN_COLS // N_CHIPS, IN_COLS // N_CHIPS), (head_rows, MLA_HEAD), (KV_LORA, KV_LORA)],
        _merge_riders(_swap_rider(late), _gather8_rider(_pack_small(st_post, st_ret, st_pre))), "rs_early")
    tot = _small_reduce(gathered)
    dm = jnp.concatenate([
        gathered[:, 12:24].reshape(N_DEV * nex, 6 * d),
        jnp.concatenate([tot[8:10].reshape(1, 2 * d), jnp.zeros((1, 4 * d), F32)], axis=1),
        jnp.zeros((7, 6 * d), F32)], axis=0)
    dm_sh = lax.dynamic_slice(dm, (0, chip * n_ada), (dm.shape[0], n_ada))
    g_ada, da = _mod_bwd(a_in, dm_sh, w_ada[0])
    dcc = _allgather8(da[N_DEV * nex:], "ag_dcc")
    halves = dict(zip(_EARLY, zip(mine, theirs)))
    halves.update(zip(_LATE, zip(late, late_theirs)))
    grad, delta, new_m, new_v = {}, {}, {}, {}
    for k in _BIG:
        a, b = halves[k]
        res = _adamw_halves(shard_of(w, k), a, b, shard_of(m, k), shard_of(v, k), core, "adamw_" + k)
        grad[k], delta[k], new_m[k], new_v[k] = [(o.T if k in _TRANSPOSED else o).reshape(w[k].shape) for o in res]

    shp = w_ada.shape
    outs, _ = _adamw(w_ada[0], g_ada, m["w_ada"][0], v["w_ada"][0], "adamw_w_ada")
    grad["w_ada"] = g_ada.reshape(shp)
    delta["w_ada"], new_m["w_ada"], new_v["w_ada"] = [o.reshape(shp) for o in outs]
    rows = [{k: t[k].reshape(1, -1) for k in _SMALL_NAMES} for t in (w, m, v)]
    small = _small_final(tot, dcc, sg8, *[[t[k] for k in _SMALL_NAMES] for t in rows])
    for res, outs in zip((grad, delta, new_m, new_v), small[:4]):
        for k, o in zip(_SMALL_NAMES, outs):
            res[k] = o.reshape(w[k].shape)
    return (small[4][0, 0], gx, *[grad[k] for k in _WEIGHTS], *[delta[k] for k in _WEIGHTS],
            *[new_m[k] for k in _WEIGHTS], *[new_v[k] for k in _WEIGHTS])
```

```python
import functools
import math

import jax
import jax.numpy as jnp
from jax import lax
from jax.experimental import pallas as pl
from jax.experimental.pallas import tpu as pltpu

F32 = jnp.float32
BF16 = jnp.bfloat16
MESH = pl.DeviceIdType.MESH

EPS = 1e-6
D_MODEL = 1024
D_FF = 4096
HEADS = 4
RET_DK = 64
RET_DV = 128
MLA_NOPE = 128
MLA_ROPE = 64
MLA_HEAD = 256
Q_LORA = 384
KV_LORA = 256
GRID_W = 64
ROPE_BASE = 10000.0
IN_COLS = 2240
IN_PAD = 2304
PG_COLS = 1152
N_CHIPS = 4
N_DEV = 8
LANES = 128
ADAM_LR = 0.001
ADAM_B1 = 0.9
ADAM_B2 = 0.999
ADAM_EPS = 1e-08
ADAM_WD = 0.01
ADAM_STEP = 10
VMEM_LIMIT = 56 * 1024 * 1024


def _dot(a, b):
    return jnp.dot(a, b, preferred_element_type=F32)


def _dot_nt(a, b):
    return lax.dot_general(a, b, (((1,), (1,)), ((), ())), preferred_element_type=F32)


def _dot_tn(a, b):
    return lax.dot_general(a, b, (((0,), (0,)), ((), ())), preferred_element_type=F32)


def _params(sem=None, vmem=None):
    return pltpu.CompilerParams(dimension_semantics=sem, vmem_limit_bytes=vmem)


def _full(shape):
    n = len(shape)
    return pl.BlockSpec(shape, lambda *_: (0,) * n)


def _once(shape):
    n = len(shape)
    return pl.BlockSpec(shape, lambda *_: (0,) * n, pipeline_mode=pl.Buffered(1))


def _rope(x, cos, sin):
    w = x.shape[-1]
    lo = (lax.broadcasted_iota(jnp.int32, (1, w), 1) % 64) < 32
    swapped = jnp.where(lo, pltpu.roll(x, w - 32, 1), pltpu.roll(x, 32, 1))
    return x * cos + swapped * sin


def _rope_t(g, cos, sin):
    w = g.shape[-1]
    lo = (lax.broadcasted_iota(jnp.int32, (1, w), 1) % 64) < 32
    t = g * sin
    swapped = jnp.where(lo, pltpu.roll(t, w - 32, 1), pltpu.roll(t, 32, 1))
    return g * cos + swapped


def _rope_tables(seq, tm):
    rows = seq // GRID_W
    row = jnp.repeat(jnp.arange(rows, dtype=F32), GRID_W)
    col = jnp.tile(jnp.arange(GRID_W, dtype=F32), rows)
    n_freq = RET_DK // 4
    freq = ROPE_BASE ** (-jnp.arange(n_freq, dtype=F32) / n_freq)
    ang = jnp.concatenate([row[:, None] * freq, col[:, None] * freq], axis=-1)
    cos, sin = jnp.cos(ang), jnp.sin(ang)
    cos_t = jnp.tile(jnp.concatenate([cos, cos], -1), (1, HEADS))
    sin_t = jnp.tile(jnp.concatenate([-sin, sin], -1), (1, HEADS))
    cos_t = jnp.concatenate([cos_t, jnp.ones((tm, 4 * RET_DK), F32)], 0)
    sin_t = jnp.concatenate([sin_t, jnp.zeros((tm, 4 * RET_DK), F32)], 0)
    return cos_t, sin_t


def _adam_math(w, g, m, v):
    mn = ADAM_B1 * m + (1.0 - ADAM_B1) * g
    vn = ADAM_B2 * v + (1.0 - ADAM_B2) * (g * g)
    m_hat = mn / (1.0 - ADAM_B1 ** ADAM_STEP)
    v_hat = vn / (1.0 - ADAM_B2 ** ADAM_STEP)
    return -ADAM_LR * (m_hat / (jnp.sqrt(v_hat) + ADAM_EPS) + ADAM_WD * w), mn, vn


def _cast_into_slots(pieces, slot, name, rider=None):
    c = pieces[0][0].shape[1]
    rb = max(b for b in range(16, 257, 16) if all(cnt % b == 0 and st % b == 0 for _, st, cnt in pieces))
    nbs = [cnt // rb for _, _, cnt in pieces]
    starts = [sum(nbs[:s]) for s in range(len(pieces))]

    def body(s_ref, *refs):
        i = pl.program_id(0)
        for s in range(len(pieces)):
            @pl.when(jnp.logical_and(i >= starts[s], i < starts[s] + nbs[s]))
            def _():
                refs[len(pieces) + s][...] = refs[s][...].astype(BF16)

    in_specs, out_specs = [], []
    for (_, first_row, _), nb, st in zip(pieces, nbs, starts):
        in_specs.append(pl.BlockSpec((rb, c), lambda i, s, nb=nb, st=st, f=first_row // rb: (f + jnp.clip(i - st, 0, nb - 1), 0)))
        out_specs.append(pl.BlockSpec((None, rb, c), lambda i, s, nb=nb, st=st: (s[0], jnp.clip(i - st, 0, nb - 1), 0)))
    return _hosted_call(
        body, [w for w, _, _ in pieces], name=name, grid=(sum(nbs),), prefetch=(slot,), in_specs=in_specs,
        out_specs=out_specs, out_shape=[jax.ShapeDtypeStruct((N_CHIPS, cnt, c), BF16) for _, _, cnt in pieces],
        sem=("arbitrary",), rider=rider)


def _cast_into_slot(w, slot, name):
    return _cast_into_slots([(w, 0, w.shape[0])], slot, name)[0][0]


def _adamw_halves(w, mine, theirs, m, v, core, name):
    r, c = w.shape
    r2 = r // 2
    rb = max(b for b in range(8, r2 + 1, 8) if r2 % b == 0 and b * c * 4 <= (1 << 21))
    nbh = r2 // rb

    def body(z_ref, w_ref, a_ref, b_ref, m_ref, v_ref, g_ref, d_ref, mo_ref, vo_ref):
        here = (pl.program_id(0) // nbh) == z_ref[0]
        gg = jnp.where(here, a_ref[...], b_ref[...])
        g_ref[...] = gg
        d_ref[...], mo_ref[...], vo_ref[...] = _adam_math(w_ref[...], gg, m_ref[...], v_ref[...])

    spec = pl.BlockSpec((rb, c), lambda i, z: (i, 0))
    a_spec = pl.BlockSpec((rb, c), lambda i, z: (jnp.clip(i - z[0] * nbh, 0, nbh - 1), 0))
    b_spec = pl.BlockSpec((rb, c), lambda i, z: (jnp.clip(i - (1 - z[0]) * nbh, 0, nbh - 1), 0))
    shp = jax.ShapeDtypeStruct((r, c), F32)
    return pl.pallas_call(
        body, name=name,
        grid_spec=pltpu.PrefetchScalarGridSpec(
            num_scalar_prefetch=1, grid=(r // rb,), in_specs=[spec, a_spec, b_spec, spec, spec], out_specs=[spec] * 4),
        out_shape=[shp] * 4,
        compiler_params=_params(("parallel",)),
    )(core, w, mine, theirs, m, v)


def _adamw(w, g, m, v, name, rider=None):
    r, c = w.shape
    rb = r
    for cand in (256, 128, 64, 32, 16, 8):
        if r % cand == 0 and cand * c * 4 <= (1 << 20):
            rb = cand
            break
    if r * c * 4 <= (1 << 20):
        rb = r

    def body(w_ref, g_ref, m_ref, v_ref, d_ref, mo_ref, vo_ref):
        d_ref[...], mo_ref[...], vo_ref[...] = _adam_math(w_ref[...], g_ref[...], m_ref[...], v_ref[...])

    spec = pl.BlockSpec((rb, c), lambda i: (i, 0))
    shp = jax.ShapeDtypeStruct((r, c), F32)
    return _hosted_call(
        body, (w, g, m, v), name=name, grid=(r // rb,), in_specs=[spec] * 4, out_specs=[spec] * 3, out_shape=[shp] * 3,
        sem=("parallel",), rider=rider)


def _decay_prep(dec):
    def body(d_ref, lg_ref, sg_ref):
        d = d_ref[...]
        lg_ref[...] = jnp.minimum(d, 0.0) - jnp.log(1.0 + jnp.exp(-jnp.abs(d)))
        sg_ref[...] = 1.0 / (1.0 + jnp.exp(d))

    shp = jax.ShapeDtypeStruct(dec.shape, F32)
    return pl.pallas_call(body, name="decay_prep", out_shape=[shp, shp])(dec)


def _mod_fwd(a_in, w_ada, b_sh):
    rows, d = a_in.shape
    n = w_ada.shape[1]
    bn = 512

    def body(a_ref, w_ref, b_ref, o_ref):
        a = a_ref[...]
        s = (a / (1.0 + jnp.exp(-a))).astype(BF16)
        o_ref[...] = _dot(s, w_ref[...].astype(BF16)) + b_ref[...]

    return pl.pallas_call(
        body, name="mod_fwd", grid=(n // bn,),
        in_specs=[_full((rows, d)), pl.BlockSpec((d, bn), lambda j: (0, j)), pl.BlockSpec((1, bn), lambda j: (0, j))],
        out_specs=pl.BlockSpec((rows, bn), lambda j: (0, j)),
        out_shape=jax.ShapeDtypeStruct((rows, n), F32),
        compiler_params=_params(("parallel",)),
    )(a_in, w_ada, b_sh)


def _mod_bwd(a_in, dm, w_ada):
    rows, d = a_in.shape
    n = w_ada.shape[1]
    bn = 512
    nb = n // bn

    def body(a_ref, dm_ref, w_ref, gw_ref, da_ref):
        j = pl.program_id(0)
        a = a_ref[...]
        s = (a / (1.0 + jnp.exp(-a))).astype(BF16)
        dmb = dm_ref[...].astype(BF16)
        gw_ref[...] = _dot_tn(s, dmb)
        part = _dot_nt(dmb, w_ref[...].astype(BF16))

        @pl.when(j == 0)
        def _():
            da_ref[...] = part

        @pl.when(j > 0)
        def _():
            da_ref[...] += part

    return pl.pallas_call(
        body, name="mod_bwd", grid=(nb,),
        in_specs=[_full((rows, d)), pl.BlockSpec((rows, bn), lambda j: (0, j)), pl.BlockSpec((d, bn), lambda j: (0, j))],
        out_specs=[pl.BlockSpec((d, bn), lambda j: (0, j)), _full((rows, d))],
        out_shape=[jax.ShapeDtypeStruct((d, n), F32), jax.ShapeDtypeStruct((rows, d), F32)],
        compiler_params=_params(("arbitrary",)),
    )(a_in, dm, w_ada)


def _pre_fwd(x2, ctx2, modv, g_attn, w_in, g_q, g_kv, w_uq, w_ukv, cos_t, sin_t, *, seq, tm, rider=None):
    t_lat, d = x2.shape
    t_ctx = ctx2.shape[0]
    nl, nc = t_lat // tm, t_ctx // tm
    n_all = t_lat + t_ctx
    tpe = seq // tm
    nex = t_lat // seq

    def body(x_ref, c_ref, mod_ref, g_ref, win_ref, gq_ref, gkv_ref, wuq_ref, wukv_ref, cos_ref, sin_ref,
             h_ref, pg_ref, rq_ref, rk_ref, rv_ref, nq_ref, nkv_ref, q_ref, k_ref, v_ref):
        i = pl.program_id(0)
        xt = jnp.where(i < nl, x_ref[...], c_ref[...])
        sh = mod_ref[0, 0:1, :]
        sc = mod_ref[0, 1:2, :]
        r = lax.rsqrt(jnp.mean(xt * xt, axis=-1, keepdims=True) + EPS)
        hb = ((xt * r) * g_ref[...] * (1.0 + sc) + sh).astype(BF16)
        h_ref[...] = hb
        p = _dot_nt(hb, win_ref[...])
        cos = cos_ref[...]
        sin = sin_ref[...]
        rq_ref[...] = _rope(p[:, 0:256], cos, sin).astype(BF16)
        rk_ref[...] = _rope(p[:, 256:512] * (RET_DK ** -0.5), cos, sin).astype(BF16)
        rv_ref[...] = p[:, 512:1024].astype(BF16)
        pg_ref[...] = p[:, 1024:2176]
        cq = p[:, 1536:1920]
        ckv = p[:, 1920:2176]
        nqb = (cq * lax.rsqrt(jnp.mean(cq * cq, axis=-1, keepdims=True) + EPS) * gq_ref[...]).astype(BF16)
        nkvb = (ckv * lax.rsqrt(jnp.mean(ckv * ckv, axis=-1, keepdims=True) + EPS) * gkv_ref[...]).astype(BF16)
        nq_ref[...] = nqb
        nkv_ref[...] = nkvb
        cos1 = cos[:, 0:LANES]
        sin1 = sin[:, 0:LANES]
        kpe = _rope(p[:, 2176:2304], cos1, sin1).astype(BF16)
        for hd in range(HEADS):
            o = hd * MLA_HEAD
            qh = _dot_nt(nqb, wuq_ref[hd]) * MLA_SCALE
            q_ref[:, o:o + 128] = qh[:, 0:128].astype(BF16)
            q_ref[:, o + 128:o + 256] = _rope(qh[:, 128:256], cos1, sin1).astype(BF16)
            kvh = _dot(nkvb, wukv_ref[hd])
            k_ref[:, o:o + 128] = kvh[:, 0:128].astype(BF16)
            k_ref[:, o + 128:o + 256] = kpe
            v_ref[:, hd * 128:(hd + 1) * 128] = kvh[:, 128:256].astype(BF16)

    def tile(width):
        return pl.BlockSpec((tm, width), lambda i: (i, 0))

    widths = (d, PG_COLS, 256, 256, 512, Q_LORA, KV_LORA, HEADS * MLA_HEAD, HEADS * MLA_HEAD, HEADS * 128)
    dtypes = (BF16, F32, BF16, BF16, BF16, BF16, BF16, BF16, BF16, BF16)
    tab = pl.BlockSpec((tm, 256), lambda i: (jnp.where(i < nl, i % tpe, tpe), 0))
    return _hosted_call(
        body, (x2, ctx2, modv, g_attn, w_in, g_q, g_kv, w_uq, w_ukv, cos_t, sin_t), name="pre_fwd", grid=(nl + nc,),
        in_specs=[
            pl.BlockSpec((tm, d), lambda i: (jnp.minimum(i, nl - 1), 0)),
            pl.BlockSpec((tm, d), lambda i: (jnp.maximum(i - nl, 0), 0)),
            pl.BlockSpec((1, 8, d), lambda i: (jnp.minimum(i // tpe, nex), 0, 0)),
            _full((1, d)), _full(w_in.shape), _full((1, Q_LORA)), _full((1, KV_LORA)),
            _full(w_uq.shape), _full(w_ukv.shape), tab, tab,
        ],
        out_specs=[tile(w) for w in widths],
        out_shape=[jax.ShapeDtypeStruct((n_all, w), dt) for w, dt in zip(widths, dtypes)],
        sem=("parallel",), rider=rider)


def _post(yret, ymla, x2, tgt2, modv, g_ffn, g_fin, w_out, w_ff1, w_ff2a, w_ff2b, *, seq, tm):
    t_lat, d = x2.shape
    nl = t_lat // tm
    tpe = seq // tm
    nex = t_lat // seq
    n_slab = w_ff1.shape[0]
    fs = w_ff1.shape[2]
    fh = w_ff2a.shape[1]

    def body(yr_ref, ym_ref, x_ref, t_ref, mod_ref, gf_ref, gl_ref, wo_ref, w1_ref, w2a_ref, w2b_ref,
             mix_ref, a_ref, du_ref, h2_ref, df_ref, dmo_ref, dmix_ref, dxm_ref, st_ref, ru_ref):
        i = pl.program_id(0)
        gt_a = mod_ref[0, 2:3, :]
        sh_f = mod_ref[0, 3:4, :]
        sc_f = mod_ref[0, 4:5, :]
        gt_f = mod_ref[0, 5:6, :]
        g_ffn_v = gf_ref[...]
        g_fin_v = gl_ref[...]
        yr = yr_ref[...]
        ym = ym_ref[...]
        mix_ref[:, 0:512] = yr
        mix_ref[:, 512:1024] = ym
        op = _dot(yr, wo_ref[0:512, :]) + _dot(ym, wo_ref[512:1024, :])
        x_mid = x_ref[...] + gt_a * op
        r2 = lax.rsqrt(jnp.mean(x_mid * x_mid, axis=-1, keepdims=True) + EPS)
        xh2 = x_mid * r2
        h2b = (xh2 * g_ffn_v * (1.0 + sc_f) + sh_f).astype(BF16)
        h2_ref[...] = h2b
        f = jnp.zeros((tm, d), F32)
        for s in range(n_slab):
            ru = jnp.maximum(_dot(h2b, w1_ref[s]), 0.0)
            ru_ref[:, s * fs:(s + 1) * fs] = ru
            ab = (ru * ru).astype(BF16)
            a_ref[:, s * fs:(s + 1) * fs] = ab
            f = f + _dot(ab[:, 0:fh], w2a_ref[s]) + _dot(ab[:, fh:fs], w2b_ref[s])
        x_out = x_mid + gt_f * f
        r3 = lax.rsqrt(jnp.mean(x_out * x_out, axis=-1, keepdims=True) + EPS)
        xh3 = x_out * r3
        err = xh3 * g_fin_v - t_ref[...]
        dy = err * (1.0 / d)
        dxh3 = dy * g_fin_v
        dx_out = r3 * (dxh3 - xh3 * jnp.mean(dxh3 * xh3, axis=-1, keepdims=True))
        dfb = (dx_out * gt_f).astype(BF16)
        df_ref[...] = dfb
        dh2 = jnp.zeros((tm, d), F32)
        for s in range(n_slab):
            da = jnp.concatenate([_dot_nt(dfb, w2a_ref[s]), _dot_nt(dfb, w2b_ref[s])], axis=1)
            dub = (da * (2.0 * ru_ref[:, s * fs:(s + 1) * fs])).astype(BF16)
            du_ref[:, s * fs:(s + 1) * fs] = dub
            dh2 = dh2 + _dot_nt(dub, w1_ref[s])
        dxh2 = dh2 * (1.0 + sc_f) * g_ffn_v
        dx_mid = dx_out + r2 * (dxh2 - xh2 * jnp.mean(dxh2 * xh2, axis=-1, keepdims=True))
        dxm_ref[...] = dx_mid
        dmob = (dx_mid * gt_a).astype(BF16)
        dmo_ref[...] = dmob
        dmix_ref[...] = _dot_nt(dmob, wo_ref[...]).astype(BF16)

        def rsum(v):
            return jnp.sum(v, axis=0, keepdims=True)

        stats = jnp.concatenate([
            rsum(dh2), rsum(dh2 * xh2 * g_ffn_v), rsum(dx_out * f), rsum(dx_mid * op),
            rsum(dh2 * (1.0 + sc_f) * xh2), rsum(dy * xh3), rsum(err * err), jnp.zeros((1, d), F32)], axis=0)

        @pl.when(i % tpe == 0)
        def _():
            st_ref[0] = stats

        @pl.when(i % tpe != 0)
        def _():
            st_ref[0] += stats

    def tile(width):
        return pl.BlockSpec((tm, width), lambda i: (i, 0))

    widths = (d, D_FF, D_FF, d, d, d, d, d)
    dtypes = (BF16, BF16, BF16, BF16, BF16, BF16, BF16, F32)
    const = pl.Buffered(1)
    return pl.pallas_call(
        body, name="post", grid=(nl,),
        in_specs=[
            tile(512), tile(512), tile(d), tile(d),
            pl.BlockSpec((1, 8, d), lambda i: (i // tpe, 0, 0)),
            _full((1, d)), _full((1, d)),
            pl.BlockSpec(w_out.shape, lambda i: (0, 0), pipeline_mode=const),
            pl.BlockSpec(w_ff1.shape, lambda i: (0, 0, 0), pipeline_mode=const),
            pl.BlockSpec(w_ff2a.shape, lambda i: (0, 0, 0), pipeline_mode=const),
            pl.BlockSpec(w_ff2b.shape, lambda i: (0, 0, 0), pipeline_mode=const),
        ],
        out_specs=[tile(w) for w in widths] + [pl.BlockSpec((1, 8, d), lambda i: (i // tpe, 0, 0))],
        out_shape=[jax.ShapeDtypeStruct((t_lat, w), dt) for w, dt in zip(widths, dtypes)]
        + [jax.ShapeDtypeStruct((nex, 8, d), F32)],
        scratch_shapes=[pltpu.VMEM((tm, D_FF), F32)],
        compiler_params=_params(("arbitrary",), VMEM_LIMIT),
    )(yret, ymla, x2, tgt2, modv, g_ffn, g_fin, w_out, w_ff1, w_ff2a, w_ff2b)


def _pre_bwd(x2, ctx2, modv, g_attn, pg, drq, drk, dkc_r, drv, dvc_r, drg, dq_m, dkl, dkc, dvl, dvc, dxm,
             w_in, g_q, g_kv, w_uq, w_ukv, cos_t, sin_t, *, seq, tm, rider=None):
    t_lat, d = x2.shape
    t_ctx = ctx2.shape[0]
    nl, nc = t_lat // tm, t_ctx // tm
    n_all = t_lat + t_ctx
    tpe = seq // tm
    nex = t_lat // seq

    def body(x_ref, c_ref, mod_ref, g_ref, pg_ref, drq_ref, drk_ref, dkcr_ref, drv_ref, dvcr_ref, drg_ref,
             dq_ref, dkl_ref, dkc_ref, dvl_ref, dvc_ref, dxm_ref, win_ref, gq_ref, gkv_ref, wuq_ref, wukv_ref,
             cos_ref, sin_ref, dpb_ref, dqf_ref, dkvf_ref, gx_ref, st_ref):
        i = pl.program_id(0)
        lat = i < nl
        latf = lat.astype(F32)
        cos = cos_ref[...]
        sin = sin_ref[...]
        cos1 = cos[:, 0:LANES]
        sin1 = sin[:, 0:LANES]
        d_rq = _rope_t(drq_ref[...] * latf, cos, sin)
        d_rk = _rope_t(jnp.where(lat, drk_ref[...], dkcr_ref[...]), cos, sin) * (RET_DK ** -0.5)
        d_rv = jnp.where(lat, drv_ref[...], dvcr_ref[...])
        d_rg = drg_ref[...] * latf
        dq_all = dq_ref[...] * (latf * MLA_SCALE)
        dk_all = jnp.where(lat, dkl_ref[...], dkc_ref[...])
        dv_all = jnp.where(lat, dvl_ref[...], dvc_ref[...])
        dnq = jnp.zeros((tm, Q_LORA), F32)
        dnkv = jnp.zeros((tm, KV_LORA), F32)
        dkpe = jnp.zeros((tm, LANES), F32)
        for hd in range(HEADS):
            o = hd * MLA_HEAD
            dqh = jnp.concatenate([dq_all[:, o:o + 128], _rope_t(dq_all[:, o + 128:o + 256], cos1, sin1)],
                                  axis=1).astype(BF16)
            dqf_ref[:, o:o + 256] = dqh
            dnq = dnq + _dot(dqh, wuq_ref[hd])
            dkpe = dkpe + dk_all[:, o + 128:o + 256]
            dkvh = jnp.concatenate([dk_all[:, o:o + 128], dv_all[:, hd * 128:(hd + 1) * 128]], axis=1).astype(BF16)
            dkvf_ref[:, o:o + 256] = dkvh
            dnkv = dnkv + _dot_nt(dkvh, wukv_ref[hd])
        d_kpe = _rope_t(dkpe, cos1, sin1)
        pgv = pg_ref[...]
        cq = pgv[:, 512:896]
        ckv = pgv[:, 896:1152]
        rq_ = lax.rsqrt(jnp.mean(cq * cq, axis=-1, keepdims=True) + EPS)
        cqh = cq * rq_
        dcqh = dnq * gq_ref[...]
        d_cq = rq_ * (dcqh - cqh * jnp.mean(dcqh * cqh, axis=-1, keepdims=True))
        rkv_ = lax.rsqrt(jnp.mean(ckv * ckv, axis=-1, keepdims=True) + EPS)
        ckvh = ckv * rkv_
        dckvh = dnkv * gkv_ref[...]
        d_ckv = rkv_ * (dckvh - ckvh * jnp.mean(dckvh * ckvh, axis=-1, keepdims=True))
        dpb = jnp.concatenate([d_rq, d_rk, d_rv, d_rg, d_cq, d_ckv, d_kpe], axis=1).astype(BF16)
        dpb_ref[...] = dpb
        dh = _dot(dpb, win_ref[...])
        xt = jnp.where(lat, x_ref[...], c_ref[...])
        sc = mod_ref[0, 1:2, :]
        g = g_ref[...]
        r = lax.rsqrt(jnp.mean(xt * xt, axis=-1, keepdims=True) + EPS)
        xh = xt * r
        dxh = dh * (1.0 + sc) * g
        dx = r * (dxh - xh * jnp.mean(dxh * xh, axis=-1, keepdims=True))

        @pl.when(lat)
        def _():
            gx_ref[...] = dxm_ref[...] + dx

        def rsum(v):
            return jnp.sum(v, axis=0, keepdims=True)

        def widen(v):
            return jnp.concatenate([v, jnp.zeros((1, d - v.shape[1]), F32)], axis=1)

        stats = jnp.concatenate([
            rsum(dh), rsum(dh * xh * g), rsum(dh * (1.0 + sc) * xh), widen(rsum(dnq * cqh)), widen(rsum(dnkv * ckvh)),
            jnp.zeros((3, d), F32)], axis=0)
        first = jnp.logical_or(jnp.logical_and(lat, i % tpe == 0), i == nl)

        @pl.when(first)
        def _():
            st_ref[0] = stats

        @pl.when(jnp.logical_not(first))
        def _():
            st_ref[0] += stats

    def lat_tile(width):
        return pl.BlockSpec((tm, width), lambda i: (jnp.minimum(i, nl - 1), 0))

    def ctx_tile(width):
        return pl.BlockSpec((tm, width), lambda i: (jnp.maximum(i - nl, 0), 0))

    def tile(width):
        return pl.BlockSpec((tm, width), lambda i: (i, 0))

    tab = pl.BlockSpec((tm, 256), lambda i: (jnp.where(i < nl, i % tpe, tpe), 0))
    ex = pl.BlockSpec((1, 8, d), lambda i: (jnp.minimum(i // tpe, nex), 0, 0))
    return _hosted_call(
        body, (x2, ctx2, modv, g_attn, pg, drq, drk, dkc_r, drv, dvc_r, drg, dq_m, dkl, dkc, dvl, dvc, dxm,
               w_in, g_q, g_kv, w_uq, w_ukv, cos_t, sin_t), name="pre_bwd", grid=(nl + nc,),
        in_specs=[
            lat_tile(d), ctx_tile(d), ex, _full((1, d)), tile(PG_COLS),
            lat_tile(256), lat_tile(256), ctx_tile(256), lat_tile(512), ctx_tile(512), lat_tile(512),
            lat_tile(1024), lat_tile(1024), ctx_tile(1024), lat_tile(512), ctx_tile(512), lat_tile(d),
            _once(w_in.shape), _full((1, Q_LORA)), _full((1, KV_LORA)), _once(w_uq.shape), _once(w_ukv.shape),
            tab, tab,
        ],
        out_specs=[tile(IN_PAD), tile(1024), tile(1024), lat_tile(d), ex],
        out_shape=[
            jax.ShapeDtypeStruct((n_all, IN_PAD), BF16), jax.ShapeDtypeStruct((n_all, 1024), BF16),
            jax.ShapeDtypeStruct((n_all, 1024), BF16), jax.ShapeDtypeStruct((t_lat, d), F32),
            jax.ShapeDtypeStruct((nex + 1, 8, d), F32),
        ],
        sem=("arbitrary",), rider=rider)


MLA_SCALE = 1.0 / math.sqrt(MLA_NOPE + MLA_ROPE)
KEY_BLOCK = 1024


def _mla_specs(t_lat, seq, ctx_len, tq, heads=1):
    nqt = seq // tq
    cb = t_lat // ctx_len
    q = pl.BlockSpec((tq, heads * MLA_HEAD), lambda b, h, j: (b * nqt + j, h))
    kl = pl.BlockSpec((seq, heads * MLA_HEAD), lambda b, h, j: (b, h))
    kc = pl.BlockSpec((ctx_len, heads * MLA_HEAD), lambda b, h, j: (cb + b, h))
    vl = pl.BlockSpec((seq, heads * 128), lambda b, h, j: (b, h))
    vc = pl.BlockSpec((ctx_len, heads * 128), lambda b, h, j: (cb + b, h))
    o = pl.BlockSpec((tq, heads * 128), lambda b, h, j: (b * nqt + j, h))
    return q, kl, kc, vl, vc, o


FWD_HEADS = 2
BWD_HEADS = 1


def _mla_fwd(q, k, v, *, t_lat, seq, ctx_len, tq, rider=None):
    nex = t_lat // seq

    def body(q_ref, kl_ref, kc_ref, vl_ref, vc_ref, o_ref, lse_ref):
        for hh in range(FWD_HEADS):
            wide = slice(hh * MLA_HEAD, (hh + 1) * MLA_HEAD)
            cols = slice(hh * 128, (hh + 1) * 128)
            qb = q_ref[:, wide]
            s = _dot_nt(qb, kl_ref[:, wide])
            sc = _dot_nt(qb, kc_ref[:, wide])
            m = jnp.maximum(jnp.max(s, axis=-1, keepdims=True), jnp.max(sc, axis=-1, keepdims=True))
            p = jnp.exp(s - m)
            pc = jnp.exp(sc - m)
            total = jnp.sum(p, axis=-1, keepdims=True) + jnp.sum(pc, axis=-1, keepdims=True)
            o = _dot(p.astype(BF16), vl_ref[:, cols]) + _dot(pc.astype(BF16), vc_ref[:, cols])
            o_ref[:, cols] = (o * (1.0 / total)).astype(BF16)
            lse_ref[:, cols] = jnp.broadcast_to(m + jnp.log(total), (tq, 128))

    qs, kl, kc, vl, vc, os_ = _mla_specs(t_lat, seq, ctx_len, tq, FWD_HEADS)
    return _hosted_call(
        body, (q, k, k, v, v), name="mla_fwd", grid=(nex, HEADS // FWD_HEADS, seq // tq),
        in_specs=[qs, kl, kc, vl, vc], out_specs=[os_, os_],
        out_shape=[jax.ShapeDtypeStruct((t_lat, HEADS * 128), BF16), jax.ShapeDtypeStruct((t_lat, HEADS * 128), F32)],
        sem=("parallel", "parallel", "arbitrary"), rider=rider)


def _mla_bwd(q, k, v, ymla, lse, dmix, *, t_lat, seq, ctx_len, tq, rider=None):
    nex = t_lat // seq
    nqt = seq // tq
    t_ctx = nex * ctx_len
    kb = min(KEY_BLOCK, seq)

    def body(q_ref, kl_ref, kc_ref, vl_ref, vc_ref, o_ref, lse_ref, do_ref, dq_ref, dkl_out, dkc_out, dvl_out, dvc_out,
             dkl_ref, dkc_ref, dvl_ref, dvc_ref):
        j = pl.program_id(2)

        @pl.when(j == 0)
        def _():
            dkl_ref[...] = jnp.zeros(dkl_ref.shape, F32)
            dkc_ref[...] = jnp.zeros(dkc_ref.shape, F32)
            dvl_ref[...] = jnp.zeros(dvl_ref.shape, F32)
            dvc_ref[...] = jnp.zeros(dvc_ref.shape, F32)

        for hh in range(BWD_HEADS):
            wide = slice(hh * MLA_HEAD, (hh + 1) * MLA_HEAD)
            cols = slice(hh * 128, (hh + 1) * 128)
            qb = q_ref[:, wide]
            dob = do_ref[:, cols]
            delta = jnp.sum(dob.astype(F32) * o_ref[:, cols].astype(F32), axis=-1, keepdims=True)
            lse_row = lse_ref[:, hh * 128:hh * 128 + 1]

            def block(k_ref, v_ref, dk_ref, dv_ref, rows):
                kbl = k_ref[rows, wide]
                vbl = v_ref[rows, cols]
                p = jnp.exp(_dot_nt(qb, kbl) - lse_row)
                ds = (p * (_dot_nt(dob, vbl) - delta)).astype(BF16)
                dk_ref[rows, wide] += _dot_tn(ds, qb)
                dv_ref[rows, cols] += _dot_tn(p.astype(BF16), dob)
                return _dot(ds, kbl)

            dq = block(kc_ref, vc_ref, dkc_ref, dvc_ref, pl.ds(0, ctx_len))
            for i in range(seq // kb):
                dq = dq + block(kl_ref, vl_ref, dkl_ref, dvl_ref, pl.ds(i * kb, kb))
            dq_ref[:, wide] = dq.astype(BF16)

        @pl.when(j == nqt - 1)
        def _():
            dkl_out[...] = dkl_ref[...].astype(BF16)
            dkc_out[...] = dkc_ref[...].astype(BF16)
            dvl_out[...] = dvl_ref[...].astype(BF16)
            dvc_out[...] = dvc_ref[...].astype(BF16)

    g = BWD_HEADS
    qs, kl, kc, vl, vc, os_ = _mla_specs(t_lat, seq, ctx_len, tq, g)
    do_spec = pl.BlockSpec((tq, g * 128), lambda b, h, j: (b * nqt + j, HEADS // g + h))
    key_blocks = [(seq, g * MLA_HEAD), (ctx_len, g * MLA_HEAD), (seq, g * 128), (ctx_len, g * 128)]
    return _hosted_call(
        body, (q, k, k, v, v, ymla, lse, dmix), name="mla_bwd", grid=(nex, HEADS // g, nqt),
        in_specs=[qs, kl, kc, vl, vc, os_, os_, do_spec],
        out_specs=[qs] + [pl.BlockSpec(blk, lambda b, h, j: (b, h)) for blk in key_blocks],
        out_shape=[
            jax.ShapeDtypeStruct((t_lat, HEADS * MLA_HEAD), BF16),
            jax.ShapeDtypeStruct((t_lat, HEADS * MLA_HEAD), BF16),
            jax.ShapeDtypeStruct((t_ctx, HEADS * MLA_HEAD), BF16),
            jax.ShapeDtypeStruct((t_lat, HEADS * 128), BF16),
            jax.ShapeDtypeStruct((t_ctx, HEADS * 128), BF16),
        ],
        scratch_shapes=[pltpu.VMEM(blk, F32) for blk in key_blocks],
        sem=("parallel", "parallel", "arbitrary"), rider=rider)


def _decay_terms(lg, chunk, forward):
    ii = lax.broadcasted_iota(jnp.int32, (chunk, chunk), 0)
    jj = lax.broadcasted_iota(jnp.int32, (chunk, chunk), 1)
    diff = (ii - jj) if forward else (jj - ii)
    dist = jnp.maximum(diff, 0).astype(F32)
    dmat = jnp.where(diff >= 0, jnp.exp(lg * dist), 0.0)
    pos = lax.broadcasted_iota(jnp.int32, (chunk, 1), 0).astype(F32)
    if forward:
        e_q = pos + 1.0
        e_k = (chunk - 1.0) - pos
    else:
        e_q = chunk - pos
        e_k = pos
    wq = jnp.exp(lg * e_q)
    wk = jnp.exp(lg * e_k)
    cd = jnp.exp(jnp.full((1, 1), lg * chunk, F32))
    return dmat, dist, wq, wk, e_q, e_k, cd


def _ctx_weights(lg, ctx_len, forward):
    pos = lax.broadcasted_iota(jnp.int32, (ctx_len, 1), 0).astype(F32)
    e = ((ctx_len - 1.0) - pos) if forward else pos
    return jnp.exp(lg * e), e


def _pair_specs(t_lat, seq, ctx_len):
    cb = t_lat // ctx_len
    qk = pl.BlockSpec((seq, 128), lambda b, p: (b, p))
    v = pl.BlockSpec((seq, 256), lambda b, p: (b, p))
    kc = pl.BlockSpec((ctx_len, 128), lambda b, p: (cb + b, p))
    vc = pl.BlockSpec((ctx_len, 256), lambda b, p: (cb + b, p))
    return qk, v, kc, vc


def _lane_masks():
    lane = lax.broadcasted_iota(jnp.int32, (1, 128), 1)
    return [(lane // RET_DK) == hh for hh in (0, 1)]


def _ret_fwd_pair(rq, rk, rv, pg, lg, g_ret, *, t_lat, seq, ctx_len, chunk, rider=None):
    nex = t_lat // seq
    n_chunk = seq // chunk

    def body(q_ref, k_ref, v_ref, kc_ref, vc_ref, rg_ref, lg_ref, g_ref, y_ref, o_ref):
        pair = pl.program_id(1)
        masks = _lane_masks()
        kcf = kc_ref[...].astype(F32)
        chains = [(forward, hh) for forward in (True, False) for hh in (0, 1)]
        terms, s0 = [], []
        for forward, hh in chains:
            lgd = lg_ref[0 if forward else 1, 2 * pair + hh]
            terms.append(_decay_terms(lgd, chunk, forward))
            wc, _ = _ctx_weights(lgd, ctx_len, forward)
            s0.append(_dot_tn((jnp.where(masks[hh], kcf, 0.0) * wc).astype(BF16), vc_ref[:, hh * 128:(hh + 1) * 128]))
        both = [terms[hh][0] + terms[2 + hh][0] for hh in (0, 1)]
        o_ref[...] = jnp.zeros(o_ref.shape, F32)

        def step(t, states):
            new = [None] * 4
            for forward in (True, False):
                n = t if forward else n_chunk - 1 - t
                sl = pl.ds(pl.multiple_of(n * chunk, chunk), chunk)
                qb = q_ref[sl, :]
                kf_all = k_ref[sl, :].astype(F32)
                for hh in (0, 1):
                    c = (0 if forward else 2) + hh
                    _, _, wq, wk, _, _, cd = terms[c]
                    cols = slice(hh * 128, (hh + 1) * 128)
                    qm = jnp.where(masks[hh], qb, jnp.zeros((), BF16))
                    kf = jnp.where(masks[hh], kf_all, 0.0)
                    vb = v_ref[sl, cols]
                    o = wq * _dot(qm, states[c].astype(BF16))
                    if forward:
                        o = o + _dot((_dot_nt(qm, kf.astype(BF16)) * both[hh]).astype(BF16), vb)
                    o_ref[sl, cols] += o
                    new[c] = cd * states[c] + _dot_tn((kf * wk).astype(BF16), vb)
            return tuple(new)

        lax.fori_loop(0, n_chunk, step, tuple(s0))

        def norm_step(n, carry):
            sl = pl.ds(pl.multiple_of(n * chunk, chunk), chunk)
            for hh in (0, 1):
                cols = slice(hh * 128, (hh + 1) * 128)
                o = o_ref[sl, cols]
                mu = jnp.mean(o, axis=-1, keepdims=True)
                oc = o - mu
                var = jnp.mean(oc * oc, axis=-1, keepdims=True)
                rg = rg_ref[sl, cols]
                y_ref[sl, cols] = (oc * lax.rsqrt(var + EPS) * g_ref[:, cols] * (rg / (1.0 + jnp.exp(-rg)))).astype(BF16)
            return carry

        lax.fori_loop(0, n_chunk, norm_step, 0)

    qk, v, kc, vc = _pair_specs(t_lat, seq, ctx_len)
    return _hosted_call(
        body, (rq, rk, rv, rk, rv, pg, lg, g_ret), name="ret_fwd", grid=(nex, HEADS // 2),
        in_specs=[qk, qk, v, kc, vc, v, pl.BlockSpec(memory_space=pltpu.SMEM), pl.BlockSpec((1, 256), lambda b, p: (0, p))],
        out_specs=[v, v],
        out_shape=[jax.ShapeDtypeStruct((t_lat, HEADS * RET_DV), BF16), jax.ShapeDtypeStruct((t_lat, HEADS * RET_DV), F32)],
        sem=("parallel", "arbitrary"), rider=rider)


def _ret_bwd_pair(rq, rk, rv, pg, osum, dmix, lg, g_ret, *, t_lat, seq, ctx_len, chunk, rider=None):
    nex = t_lat // seq
    n_chunk = seq // chunk
    t_ctx = nex * ctx_len

    def body(q_ref, k_ref, v_ref, kc_ref, vc_ref, rg_ref, o_ref, dy_ref, lg_ref, g_ref,
             dq_out, dk_out, dv_out, dkc_ref, dvc_ref, drg_ref, st_ref, do_s, s_st, dq_ref, dk_ref, dv_ref):
        pair = pl.program_id(1)
        masks = _lane_masks()
        kcf = kc_ref[...].astype(F32)

        def norm_step(n, dgains):
            sl = pl.ds(pl.multiple_of(n * chunk, chunk), chunk)
            out = []
            for hh in (0, 1):
                cols = slice(hh * 128, (hh + 1) * 128)
                gain = g_ref[:, cols]
                o = o_ref[sl, cols]
                mu = jnp.mean(o, axis=-1, keepdims=True)
                oc = o - mu
                rstd = lax.rsqrt(jnp.mean(oc * oc, axis=-1, keepdims=True) + EPS)
                ohat = oc * rstd
                rg = rg_ref[sl, cols]
                sg = 1.0 / (1.0 + jnp.exp(-rg))
                dy = dy_ref[sl, cols].astype(F32)
                don = dy * (rg * sg)
                drg_ref[sl, cols] = (dy * (ohat * gain) * (sg * (1.0 + rg * (1.0 - sg)))).astype(BF16)
                dohat = don * gain
                do_s[sl, cols] = rstd * (dohat - jnp.mean(dohat, axis=-1, keepdims=True)
                                         - ohat * jnp.mean(dohat * ohat, axis=-1, keepdims=True))
                out.append(dgains[hh] + jnp.sum(don * ohat, axis=0, keepdims=True))
            return tuple(out)

        zero_row = jnp.zeros((1, 128), F32)
        dgains = lax.fori_loop(0, n_chunk, norm_step, (zero_row, zero_row))
        dq_ref[...] = jnp.zeros(dq_ref.shape, F32)
        dk_ref[...] = jnp.zeros(dk_ref.shape, F32)
        dv_ref[...] = jnp.zeros(dv_ref.shape, F32)

        chains = [(forward, hh) for forward in (True, False) for hh in (0, 1)]
        terms, ctxw, s0 = [], [], []
        for forward, hh in chains:
            lgd = lg_ref[0 if forward else 1, 2 * pair + hh]
            terms.append(_decay_terms(lgd, chunk, forward))
            ctxw.append(_ctx_weights(lgd, ctx_len, forward))
            s0.append(_dot_tn((jnp.where(masks[hh], kcf, 0.0) * ctxw[-1][0]).astype(BF16), vc_ref[:, hh * 128:(hh + 1) * 128]))

        def chunk_at(t, ascending):
            n = t if ascending else n_chunk - 1 - t
            return n, pl.ds(pl.multiple_of(n * chunk, chunk), chunk)

        def state_step(t, states):
            new = []
            for c, (forward, hh) in enumerate(chains):
                n, sl = chunk_at(t, forward)
                wk, cd = terms[c][3], terms[c][6]
                s_st[c, n] = states[c]
                kf = jnp.where(masks[hh], k_ref[sl, :].astype(F32), 0.0)
                new.append(cd * states[c] + _dot_tn((kf * wk).astype(BF16), v_ref[sl, hh * 128:(hh + 1) * 128]))
            return tuple(new)

        lax.fori_loop(0, n_chunk, state_step, tuple(s0))

        both = [terms[hh][0] + terms[2 + hh][0] for hh in (0, 1)]

        def grad_step(t, carry):
            out = [None] * len(chains)
            in_chunk_b = [None, None]
            for forward in (True, False):
                n, sl = chunk_at(t, not forward)
                qb = q_ref[sl, :]
                kf_all = k_ref[sl, :].astype(F32)
                dq_sum = jnp.zeros((chunk, 128), F32)
                dk_sum = jnp.zeros((chunk, 128), F32)
                for hh in (0, 1):
                    c = (0 if forward else 2) + hh
                    g_next, dlg = carry[c]
                    dmat, dist, wq, wk, e_q, e_k, cd = terms[c]
                    cols = slice(hh * 128, (hh + 1) * 128)
                    qm = jnp.where(masks[hh], qb, jnp.zeros((), BF16))
                    kf = jnp.where(masks[hh], kf_all, 0.0)
                    kb = kf.astype(BF16)
                    vb = v_ref[sl, cols]
                    do = do_s[sl, cols]
                    dob = do.astype(BF16)
                    s_n = s_st[c, n]
                    s_nb = s_n.astype(BF16)
                    gb = g_next.astype(BF16)
                    dk_cross = wk * _dot_nt(vb, gb)
                    dv = _dot((kf * wk).astype(BF16), gb)
                    o_cross = wq * _dot(qm, s_nb)
                    dq_sum = dq_sum + wq * _dot_nt(dob, s_nb)
                    dk_sum = dk_sum + dk_cross
                    dlg = (dlg + chunk * cd * jnp.sum(g_next * s_n, keepdims=True)
                           + jnp.sum(e_k * jnp.sum(kf * dk_cross, axis=-1, keepdims=True), keepdims=True)
                           + jnp.sum(e_q * jnp.sum(o_cross * do, axis=-1, keepdims=True), keepdims=True))
                    if forward:
                        a_raw = _dot_nt(qm, kb)
                        da_raw = _dot_nt(dob, vb)
                        prod = a_raw * da_raw
                        dlg = dlg + jnp.sum(dist * dmat * prod, keepdims=True)
                        in_chunk_b[hh] = jnp.sum(terms[2 + hh][1] * terms[2 + hh][0] * prod, keepdims=True)
                        dab = (da_raw * both[hh]).astype(BF16)
                        dq_sum = dq_sum + _dot(dab, kb)
                        dk_sum = dk_sum + _dot_tn(dab, qm)
                        dv = dv + _dot_tn((a_raw * both[hh]).astype(BF16), dob)
                    else:
                        dlg = dlg + in_chunk_b[hh]
                    dv_ref[sl, cols] += dv
                    out[c] = (cd * g_next + _dot_tn((qm.astype(F32) * wq).astype(BF16), dob), dlg)
                dq_ref[sl, :] += dq_sum
                dk_ref[sl, :] += dk_sum
            return tuple(out)

        zero = (jnp.zeros((128, 128), F32), jnp.zeros((1, 1), F32))
        res = lax.fori_loop(0, n_chunk, grad_step, (zero,) * len(chains))
        dkc_sum = jnp.zeros((ctx_len, 128), F32)
        dvc = [jnp.zeros((ctx_len, 128), F32)] * 2
        dlgs = []
        for c, (forward, hh) in enumerate(chains):
            ds0, dlg = res[c]
            wc, e_c = ctxw[c]
            kcm = jnp.where(masks[hh], kcf, 0.0)
            ds0b = ds0.astype(BF16)
            dkc_part = wc * _dot_nt(vc_ref[:, hh * 128:(hh + 1) * 128], ds0b)
            dkc_sum = dkc_sum + dkc_part
            dvc[hh] = dvc[hh] + _dot((kcm * wc).astype(BF16), ds0b)
            dlgs.append(dlg + jnp.sum(e_c * jnp.sum(kcm * dkc_part, axis=-1, keepdims=True), keepdims=True))
        dq_out[...] = dq_ref[...].astype(BF16)
        dk_out[...] = dk_ref[...].astype(BF16)
        dv_out[...] = dv_ref[...].astype(BF16)
        dkc_ref[...] = dkc_sum
        for hh in (0, 1):
            cols = slice(hh * 128, (hh + 1) * 128)
            dvc_ref[:, cols] = dvc[hh]
            st_ref[0, :, cols] = jnp.concatenate([
                dgains[hh], jnp.broadcast_to(dlgs[hh], (1, 128)), jnp.broadcast_to(dlgs[2 + hh], (1, 128)),
                jnp.zeros((5, 128), F32)], axis=0)

    qk, v, kc, vc = _pair_specs(t_lat, seq, ctx_len)
    return _hosted_call(
        body, (rq, rk, rv, rk, rv, pg, osum, dmix, lg, g_ret), name="ret_bwd", grid=(nex, HEADS // 2),
        in_specs=[qk, qk, v, kc, vc, v, v, v, pl.BlockSpec(memory_space=pltpu.SMEM),
                  pl.BlockSpec((1, 256), lambda b, p: (0, p))],
        out_specs=[
            qk, qk, v,
            pl.BlockSpec((ctx_len, 128), lambda b, p: (b, p)),
            pl.BlockSpec((ctx_len, 256), lambda b, p: (b, p)),
            v,
            pl.BlockSpec((1, 8, 256), lambda b, p: (b, 0, p)),
        ],
        out_shape=[
            jax.ShapeDtypeStruct((t_lat, 256), BF16), jax.ShapeDtypeStruct((t_lat, 256), BF16),
            jax.ShapeDtypeStruct((t_lat, 512), BF16), jax.ShapeDtypeStruct((t_ctx, 256), F32),
            jax.ShapeDtypeStruct((t_ctx, 512), F32), jax.ShapeDtypeStruct((t_lat, 512), BF16),
            jax.ShapeDtypeStruct((nex, 8, 512), F32),
        ],
        scratch_shapes=[pltpu.VMEM((seq, 256), F32), pltpu.VMEM((4, n_chunk, 128, 128), F32),
                        pltpu.VMEM((seq, 128), F32), pltpu.VMEM((seq, 128), F32), pltpu.VMEM((seq, 256), F32)],
        sem=("parallel", "arbitrary"), rider=rider)


def _matmul_tn(a, b, *, bm, bn, bk, chip_major, name, out_dtype=F32, rider=None):
    tk, m = a.shape
    n = b.shape[1]
    slab = n // N_CHIPS
    per_block = bn // slab if chip_major else 1
    bk = max(c for c in range(LANES, min(bk, tk) + 1, LANES) if tk % c == 0)
    nk = tk // bk
    blk = (per_block, bm, slab) if chip_major else (bm, bn)

    def body(a_ref, b_ref, o_ref, acc_ref):
        k = pl.program_id(2)
        if chip_major:
            parts = [_dot_tn(a_ref[...], b_ref[:, s * slab:(s + 1) * slab]) for s in range(per_block)]
        else:
            parts = [_dot_tn(a_ref[...], b_ref[...])]

        @pl.when(k == 0)
        def _():
            for s, part in enumerate(parts):
                if chip_major:
                    acc_ref[s] = part
                else:
                    acc_ref[...] = part

        @pl.when(k > 0)
        def _():
            for s, part in enumerate(parts):
                if chip_major:
                    acc_ref[s] += part
                else:
                    acc_ref[...] += part

        @pl.when(k == nk - 1)
        def _():
            o_ref[...] = acc_ref[...].astype(out_dtype)

    if chip_major:
        out_spec = pl.BlockSpec(blk, lambda i, j, k: (j, i, 0))
        out_shape = jax.ShapeDtypeStruct((N_CHIPS, m, slab), out_dtype)
    else:
        out_spec = pl.BlockSpec(blk, lambda i, j, k: (i, j))
        out_shape = jax.ShapeDtypeStruct((m, n), out_dtype)
    (out,), carried = _hosted_call(
        body, (a, b), name=name, grid=(m // bm, n // bn, nk),
        in_specs=[pl.BlockSpec((bk, bm), lambda i, j, k: (k, i)), pl.BlockSpec((bk, bn), lambda i, j, k: (k, j))],
        out_specs=[out_spec], out_shape=[out_shape], scratch_shapes=[pltpu.VMEM(blk, F32)],
        sem=("parallel", "parallel", "arbitrary"), rider=rider)
    return out if rider is None else (out, carried)


_LATE = ("w_out", "w_ff1", "w_ff2")
_EARLY = ("w_in", "w_uq", "w_ukv")


def _local_step(x, ctx, tgt, modv, lg, g_attn, g_ffn, g_fin, g_ret, g_q, g_kv, w_in, w_uq, w_ukv, late, place=None,
                *, tm=256, tq=256, chunk=256):
    nex, seq, d = x.shape
    ctx_len = ctx.shape[1]
    t_lat = nex * seq
    tm = min(tm, seq)
    x2 = x.reshape(t_lat, d)
    ctx2 = ctx.reshape(nex * ctx_len, d)
    tgt2 = tgt.reshape(t_lat, d)
    tm_fwd = min(2 * tm, seq)
    cos_t, sin_t = _rope_tables(seq, tm_fwd)
    dims = dict(t_lat=t_lat, seq=seq, ctx_len=ctx_len)
    alone = place is None

    (hb, pg, rq, rk, rv, nq, nkv, q, k, v), crossed_a = _pre_fwd(
        x2, ctx2, modv, g_attn, w_in, g_q, g_kv, w_uq, w_ukv, cos_t, sin_t, seq=seq, tm=tm_fwd,
        rider=None if alone else _gather_ici_rider([late[2]]))
    (yret, osum), got = _ret_fwd_pair(
        rq, rk, rv, pg, lg, g_ret, chunk=min(2 * chunk, seq), **dims,
        rider=None if alone else _merge_riders(_gather_d2d_rider(crossed_a), _gather_ici_rider([late[3]])))
    (ymla, lse), got_rest = _mla_fwd(
        q, k, v, tq=tq, **dims,
        rider=None if alone else _merge_riders(_gather_rider([late[0], late[1]], staged=True), _gather_d2d_rider(got[1:])))
    w_out, w_ff1, w_ff2a, w_ff2b = late if alone else (got_rest[0], got_rest[1], got[0], got_rest[2])
    mix, act, du, h2, df, dmo, dmix, dxm, st_post = _post(yret, ymla, x2, tgt2, modv, g_ffn, g_fin, w_out.reshape(d, d),
                                                         w_ff1, w_ff2a, w_ff2b, seq=seq, tm=min(tm, 256))
    kw = dict(bm=1024, bn=1024, bk=2048, out_dtype=BF16)
    g_ff2 = _matmul_tn(act, df, chip_major=False, name="gw_ff2", **kw).reshape(N_CHIPS, D_FF // N_CHIPS, d)
    if alone:
        g_ff1 = _matmul_tn(h2, du, chip_major=True, name="gw_ff1", **kw)
        g_out = _matmul_tn(mix, dmo, chip_major=False, name="gw_out", **kw).reshape(N_CHIPS, d // N_CHIPS, d)
        (dq_m, dkl, dkc, dvl, dvc), _ = _mla_bwd(q, k, v, ymla, lse, dmix, tq=tq, **dims)
        (drq, drk, drv, dkc_r, dvc_r, drg, st_ret), _ = _ret_bwd_pair(rq, rk, rv, pg, osum, dmix, lg, g_ret, chunk=chunk,
                                                                      **dims)
        late_out = [g_out, g_ff1, g_ff2]
    else:
        core, slot = place
        g_ff1, x_ff2 = _matmul_tn(h2, du, chip_major=True, name="gw_ff1", rider=_exchange_rider([g_ff2]), **kw)
        g_out, x_ff1 = _matmul_tn(mix, dmo, chip_major=False, name="gw_out", rider=_exchange_rider([g_ff1]), **kw)
        g_out = g_out.reshape(N_CHIPS, d // N_CHIPS, d)
        p_ff2 = _add_half(g_ff2, x_ff2[0], core, "add_half_w_ff2")
        p_ff1 = _add_half(g_ff1, x_ff1[0], core, "add_half_w_ff1")
        (dq_m, dkl, dkc, dvl, dvc), (l_ff2, l_ff1, x_out) = _mla_bwd(
            q, k, v, ymla, lse, dmix, tq=min(seq, 512), **dims,
            rider=_merge_riders(_scatter_rider([p_ff2, p_ff1]), _exchange_rider([g_out])))
        p_out = _add_half(g_out, x_out, core, "add_half_w_out")
        m_ff2 = _sum_chips(p_ff2, l_ff2, slot, "sum_chips_w_ff2")
        m_ff1 = _sum_chips(p_ff1, l_ff1, slot, "sum_chips_w_ff1")
        (drq, drk, drv, dkc_r, dvc_r, drg, st_ret), (l_out,) = _ret_bwd_pair(
            rq, rk, rv, pg, osum, dmix, lg, g_ret, chunk=chunk, **dims, rider=_scatter_rider([p_out]))
        late_out = [_sum_chips(p_out, l_out, slot, "sum_chips_w_out"), m_ff1, m_ff2]
    (dpb, dqf, dkvf, gx, st_pre), _ = _pre_bwd(
        x2, ctx2, modv, g_attn, pg, drq, drk, dkc_r, drv, dvc_r, drg, dq_m, dkl, dkc, dvl, dvc, dxm, w_in, g_q, g_kv,
        w_uq, w_ukv, cos_t, sin_t, seq=seq, tm=tm)
    g_early = [
        _matmul_tn(dpb, hb, bm=IN_PAD // 2, bn=d, bk=1536, chip_major=False, name="gw_in"),
        _matmul_tn(dqf, nq, bm=HEADS * MLA_HEAD, bn=Q_LORA, bk=1536, chip_major=False, name="gw_uq"),
        _matmul_tn(nkv, dkvf, bm=KV_LORA, bn=HEADS * 256, bk=1536, chip_major=True, name="gw_ukv"),
    ]
    return gx.reshape(nex, seq, d), g_early, late_out, st_post, st_ret, st_pre


_ANY = pl.BlockSpec(memory_space=pl.ANY)
_VMEM = pl.BlockSpec(memory_space=pltpu.VMEM)
_OFFSETS = tuple((dx, dy, dc) for dx in (0, 1) for dy in (0, 1) for dc in (0, 1))[1:]
_CHIP_OFFSETS = ((1, 0), (0, 1), (1, 1))


def _place():
    return lax.axis_index("x"), lax.axis_index("y"), lax.axis_index("c")


def _flip(v, d):
    return 1 - v if d else v


def _gather8_rider(a, in_vmem=True):
    def copies(a_ref, o_ref, send, recv):
        x, y, z = _place()
        me = 4 * x + 2 * y + z
        out = []
        for k, (dx, dy, dc) in enumerate(_OFFSETS):
            peer = (_flip(x, dx), _flip(y, dy), _flip(z, dc))
            landing = o_ref.at[4 * peer[0] + 2 * peer[1] + peer[2]]
            out.append((
                pltpu.make_async_remote_copy(src_ref=a_ref, dst_ref=o_ref.at[me], send_sem=send.at[k],
                                             recv_sem=recv.at[k], device_id=peer, device_id_type=MESH),
                pltpu.make_async_remote_copy(src_ref=a_ref, dst_ref=landing, send_sem=send.at[k],
                                             recv_sem=recv.at[k], device_id=peer, device_id_type=MESH)))
        return me, out

    def start(ins, outs, sems):
        me, cps = copies(ins[0], outs[0], sems[0], sems[1])
        pltpu.make_async_copy(ins[0], outs[0].at[me], sems[2]).start()
        for out_cp, _ in cps:
            out_cp.start()

    def finish(ins, outs, sems):
        me, cps = copies(ins[0], outs[0], sems[0], sems[1])
        for out_cp, in_cp in cps:
            in_cp.wait_recv()
            out_cp.wait_send()
        pltpu.make_async_copy(ins[0], outs[0].at[me], sems[2]).wait()

    spec = [_VMEM] if in_vmem else [_ANY]
    return _Rider([a], [jax.ShapeDtypeStruct((N_DEV,) + a.shape, a.dtype)],
                  [pltpu.SemaphoreType.DMA((7,)), pltpu.SemaphoreType.DMA((7,)), pltpu.SemaphoreType.DMA],
                  start, finish, in_specs=spec, out_specs=spec)


def _merge_riders(*riders):
    ins, outs, sems, in_specs, out_specs, aliases, cuts = [], [], [], [], [], {}, []
    for r in riders:
        cuts.append((len(ins), len(outs), len(sems)))
        aliases.update({len(ins) + i: len(outs) + j for i, j in r.aliases.items()})
        ins += r.ins
        outs += r.out_shapes
        sems += r.sems
        in_specs += r.in_specs
        out_specs += r.out_specs

    def part(r, cut, r_ins, r_outs, r_sems):
        return (r_ins[cut[0]:cut[0] + len(r.ins)], r_outs[cut[1]:cut[1] + len(r.out_shapes)],
                r_sems[cut[2]:cut[2] + len(r.sems)])

    def start(r_ins, r_outs, r_sems):
        for r, cut in zip(riders, cuts):
            r.start(*part(r, cut, r_ins, r_outs, r_sems))

    def finish(r_ins, r_outs, r_sems):
        for r, cut in zip(riders, cuts):
            r.finish(*part(r, cut, r_ins, r_outs, r_sems))

    def middle(r_ins, r_outs, r_sems):
        for r, cut in zip(riders, cuts):
            if r.middle is not None:
                r.middle(*part(r, cut, r_ins, r_outs, r_sems))

    return _Rider(ins, outs, sems, start, finish, aliases=aliases, in_specs=in_specs, out_specs=out_specs,
                  middle=middle if any(r.middle is not None for r in riders) else None)


def _allgather8(a, name):
    return _run_rider(_gather8_rider(a), name)[0]


BF16_TILE_ROWS = 16


def _half(o, slot, which):
    r2 = o.shape[1] // 2
    if r2 % BF16_TILE_ROWS == 0:
        return o.at[slot, pl.ds(which * r2, r2)]
    c2 = o.shape[2] // 2
    assert c2 % LANES == 0
    return o.at[slot, :, pl.ds(which * c2, c2)]


def _gather_send(o_refs, send, recv):
    x, y, z = _place()
    chip = 2 * x + y
    for a, o in enumerate(o_refs):
        r2 = o.shape[1] // 2
        mine = _half(o, chip, z)
        for k, (dx, dy) in enumerate(_CHIP_OFFSETS):
            pltpu.make_async_remote_copy(
                src_ref=mine, dst_ref=mine, send_sem=send.at[a, k], recv_sem=recv.at[a, k],
                device_id=(_flip(x, dx), _flip(y, dy), z), device_id_type=MESH).start()


def _gather_landed(o_refs, send, recv, then=None):
    x, y, z = _place()
    chip = 2 * x + y
    for a, o in enumerate(o_refs):
        for k, (dx, dy) in enumerate(_CHIP_OFFSETS):
            landed = _half(o, 2 * _flip(x, dx) + _flip(y, dy), z)
            pltpu.make_async_remote_copy(
                src_ref=landed, dst_ref=landed, send_sem=send.at[a, k], recv_sem=recv.at[a, k],
                device_id=(_flip(x, dx), _flip(y, dy), z), device_id_type=MESH).wait_recv()
            if then is not None:
                then(a, k, landed)
    for a, o in enumerate(o_refs):
        mine = _half(o, chip, z)
        for k, (dx, dy) in enumerate(_CHIP_OFFSETS):
            pltpu.make_async_remote_copy(
                src_ref=mine, dst_ref=mine, send_sem=send.at[a, k], recv_sem=recv.at[a, k],
                device_id=(_flip(x, dx), _flip(y, dy), z), device_id_type=MESH).wait_send()


def _pass_on(o_refs, fsend, frecv, a, k, landed):
    x, y, z = _place()
    pltpu.make_async_remote_copy(
        src_ref=landed, dst_ref=landed, send_sem=fsend.at[a, k], recv_sem=frecv.at[a, k],
        device_id=(x, y, 1 - z), device_id_type=MESH).start()


def _passed_on(o_refs, fsend, frecv):
    x, y, z = _place()
    for a, o in enumerate(o_refs):
        for k, (dx, dy) in enumerate(_CHIP_OFFSETS):
            other = 2 * _flip(x, dx) + _flip(y, dy)
            got = _half(o, other, 1 - z)
            gave = _half(o, other, z)
            pltpu.make_async_remote_copy(
                src_ref=got, dst_ref=got, send_sem=fsend.at[a, k], recv_sem=frecv.at[a, k],
                device_id=(x, y, 1 - z), device_id_type=MESH).wait_recv()
            pltpu.make_async_remote_copy(
                src_ref=gave, dst_ref=gave, send_sem=fsend.at[a, k], recv_sem=frecv.at[a, k],
                device_id=(x, y, 1 - z), device_id_type=MESH).wait_send()


def _gather_finish(o_refs, send, recv, fsend, frecv):
    _gather_landed(o_refs, send, recv, functools.partial(_pass_on, o_refs, fsend, frecv))
    _passed_on(o_refs, fsend, frecv)


class _Rider:
    def __init__(self, ins, out_shapes, sems, start, finish, aliases=None, in_specs=None, out_specs=None, middle=None):
        self.ins, self.out_shapes, self.sems = list(ins), list(out_shapes), list(sems)
        self.start, self.finish, self.aliases = start, finish, dict(aliases or {})
        self.middle = middle
        self.in_specs = list(in_specs) if in_specs else [_ANY] * len(self.ins)
        self.out_specs = list(out_specs) if out_specs else [_ANY] * len(self.out_shapes)


def _run_rider(rider, name):
    r_in, r_out = len(rider.ins), len(rider.out_shapes)

    def body(*refs):
        ins, outs, sems = refs[:r_in], refs[r_in:r_in + r_out], refs[r_in + r_out:]
        rider.start(ins, outs, sems)
        if rider.middle is not None:
            rider.middle(ins, outs, sems)
        rider.finish(ins, outs, sems)

    return pl.pallas_call(
        body, name=name, in_specs=rider.in_specs, out_specs=rider.out_specs, out_shape=rider.out_shapes,
        input_output_aliases=rider.aliases, scratch_shapes=rider.sems,
    )(*rider.ins)


def _hosted_call(body, args, *, name, grid, in_specs, out_specs, out_shape, scratch_shapes=(), sem, rider=None,
                 prefetch=()):
    scratch_shapes = list(scratch_shapes)
    n_pf, n_in, n_out, n_sc = len(prefetch), len(in_specs), len(out_specs), len(scratch_shapes)
    r_in, r_out = (len(rider.ins), len(rider.out_shapes)) if rider else (0, 0)
    last = tuple(g - 1 for g in grid)

    def hosted(*refs):
        p = 0
        parts = []
        for cnt in (n_pf, n_in, r_in, n_out, r_out, n_sc):
            parts.append(refs[p:p + cnt])
            p += cnt
        pf, ins, r_ins, outs, r_outs, scratch = parts
        sems = refs[p:]
        ids = [pl.program_id(a) for a in range(len(grid))]
        is_first = functools.reduce(jnp.logical_and, [i == 0 for i in ids])
        is_last = functools.reduce(jnp.logical_and, [i == e for i, e in zip(ids, last)])

        @pl.when(is_first)
        def _():
            rider.start(r_ins, r_outs, sems)

        if rider.middle is not None:
            linear = functools.reduce(lambda acc, ig: acc * ig[1] + ig[0], zip(ids, grid), 0)

            @pl.when(linear == math.prod(grid) * 3 // 4)
            def _():
                rider.middle(r_ins, r_outs, sems)

        body(*pf, *ins, *outs, *scratch)

        @pl.when(is_last)
        def _():
            rider.finish(r_ins, r_outs, sems)

    if rider is None:
        kern, all_in, all_out, shapes, scratch, aliases, extra = body, list(in_specs), list(out_specs), list(out_shape), \
            scratch_shapes, {}, []
    else:
        kern, all_in, all_out = hosted, list(in_specs) + rider.in_specs, list(out_specs) + rider.out_specs
        shapes, scratch, extra = list(out_shape) + rider.out_shapes, scratch_shapes + rider.sems, rider.ins
        aliases = {n_pf + n_in + i: n_out + j for i, j in rider.aliases.items()}
        sem = ("arbitrary",) * len(grid)
    if prefetch:
        spec = dict(grid_spec=pltpu.PrefetchScalarGridSpec(
            num_scalar_prefetch=n_pf, grid=grid, in_specs=all_in, out_specs=all_out, scratch_shapes=scratch))
    else:
        spec = dict(grid=grid, in_specs=all_in, out_specs=all_out, scratch_shapes=scratch)
    res = pl.pallas_call(kern, name=name, out_shape=shapes, input_output_aliases=aliases,
                         compiler_params=_params(sem, VMEM_LIMIT), **spec)(*prefetch, *args, *extra)
    return list(res[:n_out]), list(res[n_out:])


def _gather_rider(ws, staged=False):
    n = len(ws)
    shapes = [jax.ShapeDtypeStruct(w.shape, w.dtype) for w in ws]
    sems = [pltpu.SemaphoreType.DMA((n, 3))] * 4
    aliases = {a: a for a in range(n)}

    def start(ins, outs, s):
        _gather_send(outs, s[0], s[1])

    if not staged:
        return _Rider(ws, shapes, sems, start, lambda ins, outs, s: _gather_finish(outs, *s), aliases=aliases)
    return _Rider(
        ws, shapes, sems, start, lambda ins, outs, s: _passed_on(outs, s[2], s[3]), aliases=aliases,
        middle=lambda ins, outs, s: _gather_landed(outs, s[0], s[1], functools.partial(_pass_on, outs, s[2], s[3])))


def _gather_ici_rider(ws):
    n = len(ws)
    return _Rider(
        ws, [jax.ShapeDtypeStruct(w.shape, w.dtype) for w in ws], [pltpu.SemaphoreType.DMA((n, 3))] * 2,
        lambda ins, outs, sems: _gather_send(outs, sems[0], sems[1]),
        lambda ins, outs, sems: _gather_landed(outs, sems[0], sems[1]),
        aliases={a: a for a in range(n)})


def _gather_d2d_rider(ws):
    n = len(ws)

    def start(ins, outs, sems):
        x, y, z = _place()
        for a, o in enumerate(outs):
            for k, (dx, dy) in enumerate(_CHIP_OFFSETS):
                _pass_on(outs, sems[0], sems[1], a, k, _half(o, 2 * _flip(x, dx) + _flip(y, dy), z))

    return _Rider(
        ws, [jax.ShapeDtypeStruct(w.shape, w.dtype) for w in ws], [pltpu.SemaphoreType.DMA((n, 3))] * 2,
        start, lambda ins, outs, sems: _passed_on(outs, sems[0], sems[1]), aliases={a: a for a in range(n)})


def _copies_rider(ins, out_shapes, sem_shape, make):
    def start(r_ins, r_outs, sems):
        for cp in make(r_ins, r_outs, sems[0], sems[1]):
            cp.start()

    def finish(r_ins, r_outs, sems):
        for cp in make(r_ins, r_outs, sems[0], sems[1]):
            cp.wait()

    return _Rider(ins, out_shapes, [pltpu.SemaphoreType.DMA(sem_shape)] * 2, start, finish)


def _exchange_rider(gs):
    def make(g_refs, r_refs, send, recv):
        x, y, z = _place()
        return [pltpu.make_async_remote_copy(
            src_ref=g.at[:, pl.ds((1 - z) * (g.shape[1] // 2), g.shape[1] // 2)], dst_ref=r, send_sem=send.at[a],
            recv_sem=recv.at[a], device_id=(x, y, 1 - z), device_id_type=MESH)
            for a, (g, r) in enumerate(zip(g_refs, r_refs))]

    shapes = [jax.ShapeDtypeStruct((g.shape[0], g.shape[1] // 2, g.shape[2]), g.dtype) for g in gs]
    return _copies_rider(gs, shapes, (len(gs),), make)


def _add_half(g, recv, core, name):
    s, r, c = g.shape
    r2 = r // 2
    rb = r2
    for cand in (256, 128, 64):
        if r2 % cand == 0:
            rb = cand
            break
    g4 = g.reshape(s, 2, r2, c)

    def body(core_ref, g_ref, r_ref, o_ref):
        o_ref[...] = (g_ref[...].astype(F32) + r_ref[...].astype(F32)).astype(BF16)

    return pl.pallas_call(
        body, name=name,
        grid_spec=pltpu.PrefetchScalarGridSpec(
            num_scalar_prefetch=1, grid=(s, r2 // rb),
            in_specs=[pl.BlockSpec((None, None, rb, c), lambda i, j, cr: (i, cr[0], j, 0)),
                      pl.BlockSpec((None, rb, c), lambda i, j, cr: (i, j, 0))],
            out_specs=pl.BlockSpec((None, rb, c), lambda i, j, cr: (i, j, 0))),
        out_shape=jax.ShapeDtypeStruct((s, r2, c), BF16),
        compiler_params=_params(("parallel", "parallel")),
    )(core, g4, recv)


def _scatter_rider(ps):
    def make(p_refs, o_refs, send, recv):
        x, y, z = _place()
        copies = []
        for a, (p, o) in enumerate(zip(p_refs, o_refs)):
            for k, (dx, dy) in enumerate(_CHIP_OFFSETS):
                other = 2 * _flip(x, dx) + _flip(y, dy)
                copies.append(pltpu.make_async_remote_copy(
                    src_ref=p.at[other], dst_ref=o.at[k], send_sem=send.at[a, k], recv_sem=recv.at[a, k],
                    device_id=(_flip(x, dx), _flip(y, dy), z), device_id_type=MESH))
        return copies

    shapes = [jax.ShapeDtypeStruct((3,) + p.shape[1:], p.dtype) for p in ps]
    return _copies_rider(ps, shapes, (len(ps), 3), make)


def _sum_chips(p, landed, chip, name):
    _, r2, c = p.shape
    rb = r2
    for cand in (256, 128, 64):
        if r2 % cand == 0:
            rb = cand
            break

    def body(s_ref, p_ref, l_ref, o_ref):
        acc = p_ref[...].astype(F32)
        for k in range(3):
            acc = acc + l_ref[k].astype(F32)
        o_ref[...] = acc

    return pl.pallas_call(
        body, name=name,
        grid_spec=pltpu.PrefetchScalarGridSpec(
            num_scalar_prefetch=1, grid=(r2 // rb,),
            in_specs=[pl.BlockSpec((None, rb, c), lambda i, s: (s[0], i, 0)),
                      pl.BlockSpec((3, rb, c), lambda i, s: (0, i, 0))],
            out_specs=pl.BlockSpec((rb, c), lambda i, s: (i, 0))),
        out_shape=jax.ShapeDtypeStruct((r2, c), F32),
        compiler_params=_params(("parallel",)),
    )(chip, p, landed)


def _swap_rider(hs):
    def make(h_refs, o_refs, send, recv):
        x, y, z = _place()
        return [pltpu.make_async_remote_copy(
            src_ref=h, dst_ref=o, send_sem=send.at[a], recv_sem=recv.at[a], device_id=(x, y, 1 - z),
            device_id_type=MESH) for a, (h, o) in enumerate(zip(h_refs, o_refs))]

    return _copies_rider(hs, [jax.ShapeDtypeStruct(h.shape, h.dtype) for h in hs], (len(hs),), make)


def _reduce_scatter_vmem(gs, rows, rider, name):
    n = len(gs)
    r_in, r_out = len(rider.ins), len(rider.out_shapes)
    halves = [(r // 2, g.shape[-1]) for g, (r, _) in zip(gs, rows)]
    piece_cols = 2 * LANES
    pieces = [(a, slice(c0, min(c0 + piece_cols, h[1]))) for a, h in enumerate(halves) for c0 in range(0, h[1], piece_cols)]
    n_p = len(pieces)

    def body(*refs):
        p = 0
        parts = []
        for cnt in (n, r_in, n, n, r_out, n, n, n, 6):
            parts.append(refs[p:p + cnt])
            p += cnt
        g_refs, r_ins, mine, theirs, r_outs, recv, part, land, sems = parts
        r_sems = refs[p:]
        xs, xr, ss, sr, ws, wr = sems
        x, y, z = _place()
        chip = 2 * x + y
        sib = (x, y, 1 - z)
        rider.start(r_ins, r_outs, r_sems)

        def half_of(a, s, which):
            r2 = halves[a][0]
            if len(g_refs[a].shape) == 3:
                return g_refs[a].at[s, pl.ds(pl.multiple_of(which * r2, 8), r2)]
            return g_refs[a].at[pl.ds(pl.multiple_of(s * rows[a][1] + which * r2, 8), r2)]

        def exchange(i):
            a, cols = pieces[i]
            return [pltpu.make_async_remote_copy(
                src_ref=half_of(a, s, 1 - z).at[:, cols], dst_ref=recv[a].at[s, :, cols], send_sem=xs.at[i, s],
                recv_sem=xr.at[i, s], device_id=sib, device_id_type=MESH) for s in range(N_CHIPS)]

        def scatter(i):
            a, cols = pieces[i]
            return [pltpu.make_async_remote_copy(
                src_ref=part[a].at[2 * _flip(x, dx) + _flip(y, dy), :, cols], dst_ref=land[a].at[k, :, cols],
                send_sem=ss.at[i, k], recv_sem=sr.at[i, k], device_id=(_flip(x, dx), _flip(y, dy), z),
                device_id_type=MESH) for k, (dx, dy) in enumerate(_CHIP_OFFSETS)]

        def swap(i):
            a, cols = pieces[i]
            return pltpu.make_async_remote_copy(
                src_ref=mine[a].at[:, cols], dst_ref=theirs[a].at[:, cols], send_sem=ws.at[i], recv_sem=wr.at[i],
                device_id=sib, device_id_type=MESH)

        for i in range(len(pieces)):
            for cp in exchange(i):
                cp.start()
        for i, (a, cols) in enumerate(pieces):
            for cp in exchange(i):
                cp.wait()
            for s in range(N_CHIPS):
                part[a][s, :, cols] = (half_of(a, s, z)[:, cols] + recv[a][s, :, cols]).astype(BF16)
            for cp in scatter(i):
                cp.start()
        for i, (a, cols) in enumerate(pieces):
            for cp in scatter(i):
                cp.wait()
            acc = part[a][chip, :, cols].astype(F32)
            for k in range(3):
                acc = acc + land[a][k, :, cols].astype(F32)
            mine[a][:, cols] = acc
            swap(i).start()
        for i in range(len(pieces)):
            swap(i).wait()
        rider.finish(r_ins, r_outs, r_sems)

    half_shapes = [jax.ShapeDtypeStruct(h, F32) for h in halves]
    res = pl.pallas_call(
        body, name=name, in_specs=[_VMEM] * n + rider.in_specs, out_specs=[_VMEM] * (2 * n) + rider.out_specs,
        out_shape=half_shapes + half_shapes + rider.out_shapes,
        scratch_shapes=[pltpu.VMEM((N_CHIPS,) + h, F32) for h in halves] + [pltpu.VMEM((N_CHIPS,) + h, BF16) for h in halves]
        + [pltpu.VMEM((3,) + h, BF16) for h in halves]
        + [pltpu.SemaphoreType.DMA((n_p, N_CHIPS))] * 2 + [pltpu.SemaphoreType.DMA((n_p, 3))] * 2
        + [pltpu.SemaphoreType.DMA((n_p,))] * 2 + rider.sems,
        input_output_aliases={n + i: 2 * n + j for i, j in rider.aliases.items()},
        compiler_params=_params(None, VMEM_LIMIT),
    )(*gs, *rider.ins)
    return list(res[:n]), list(res[n:2 * n]), list(res[2 * n:])


SMALL_ROWS = 32
PACK_ROWS = 16


def _pack_small(st_post, st_ret, st_pre):
    d = st_post.shape[2]

    def body(po_ref, re_ref, pr_ref, o_ref):
        o_ref[...] = jnp.zeros(o_ref.shape, F32)
        o_ref[0:1, :] = pr_ref[0, 2:3, :] + pr_ref[1, 2:3, :] + pr_ref[2, 2:3, :]
        o_ref[1:2, :] = po_ref[0, 4:5, :] + po_ref[1, 4:5, :]
        o_ref[2:3, :] = po_ref[0, 5:6, :] + po_ref[1, 5:6, :]
        o_ref[3:4, 0:512] = re_ref[0, 0:1, :] + re_ref[1, 0:1, :]
        o_ref[4:5, :] = pr_ref[0, 3:4, :] + pr_ref[1, 3:4, :] + pr_ref[2, 3:4, :]
        o_ref[5:6, :] = pr_ref[0, 4:5, :] + pr_ref[1, 4:5, :] + pr_ref[2, 4:5, :]
        lane = lax.broadcasted_iota(jnp.int32, (1, LANES), 1)
        for row, src in ((6, 1), (10, 2)):
            acc = jnp.zeros((1, LANES), F32)
            for hd in range(HEADS):
                grp = re_ref[0, src:src + 1, hd * LANES:(hd + 1) * LANES] + re_ref[1, src:src + 1, hd * LANES:(hd + 1) * LANES]
                acc = acc + jnp.where(lane == hd, grp, 0.0)
            o_ref[row:row + 1, 0:LANES] = acc
        o_ref[7:8, :] = po_ref[0, 6:7, :] + po_ref[1, 6:7, :]
        o_ref[8:9, :] = pr_ref[2, 0:1, :]
        o_ref[9:10, :] = pr_ref[2, 1:2, :]
        for e in range(2):
            b = 12 + 6 * e
            o_ref[b:b + 1, :] = pr_ref[e, 0:1, :]
            o_ref[b + 1:b + 2, :] = pr_ref[e, 1:2, :]
            o_ref[b + 2:b + 3, :] = po_ref[e, 3:4, :]
            o_ref[b + 3:b + 4, :] = po_ref[e, 0:1, :]
            o_ref[b + 4:b + 5, :] = po_ref[e, 1:2, :]
            o_ref[b + 5:b + 6, :] = po_ref[e, 2:3, :]

    return pl.pallas_call(body, name="pack_small", out_shape=jax.ShapeDtypeStruct((SMALL_ROWS, d), F32))(st_post, st_ret, st_pre)


def _small_reduce(gathered):
    d = gathered.shape[2]

    def body(g_ref, o_ref):
        tot = g_ref[0, 0:PACK_ROWS, :]
        for dev in range(1, N_DEV):
            tot = tot + g_ref[dev, 0:PACK_ROWS, :]
        o_ref[0:PACK_ROWS, :] = tot
        for j in range(6):
            acc = g_ref[0, 12 + j:13 + j, :] + g_ref[0, 18 + j:19 + j, :]
            for dev in range(1, N_DEV):
                acc = acc + g_ref[dev, 12 + j:13 + j, :] + g_ref[dev, 18 + j:19 + j, :]
            if j < 2:
                acc = acc + o_ref[8 + j:9 + j, :]
            o_ref[PACK_ROWS + j:PACK_ROWS + j + 1, :] = acc
        o_ref[PACK_ROWS + 6:PACK_ROWS + 8, :] = jnp.zeros((2, d), F32)

    return pl.pallas_call(body, name="small_reduce", out_shape=jax.ShapeDtypeStruct((PACK_ROWS + 8, d), F32))(gathered)


_SMALL = (("g_attn", 0, 1024), ("g_ffn", 1, 1024), ("g_final", 2, 1024), ("g_ret", 3, 512), ("g_q_lora", 4, 384),
          ("g_kv_lora", 5, 256), ("ret_decay_fwd", 6, HEADS), ("ret_decay_bwd", 10, HEADS))
_SMALL_NAMES = tuple(s[0] for s in _SMALL) + ("c_ctx", "b_ada")


def _small_final(tot, dcc, sg8, ws, ms, vs):
    d = tot.shape[1]
    n = len(_SMALL_NAMES)

    def body(*refs):
        t_ref, dcc_ref, sg_ref = refs[0:3]
        w_refs, m_refs, v_refs = refs[3:3 + n], refs[3 + n:3 + 2 * n], refs[3 + 2 * n:3 + 3 * n]
        outs = refs[3 + 3 * n:]
        g_refs, d_refs, mo_refs, vo_refs = outs[0:n], outs[n:2 * n], outs[2 * n:3 * n], outs[3 * n:4 * n]
        l_ref = outs[4 * n]

        def update(i, g, sl=None):
            pick = (lambda r: r[...]) if sl is None else (lambda r: r[:, sl])
            dl, mn, vn = _adam_math(pick(w_refs[i]), g, pick(m_refs[i]), pick(v_refs[i]))
            if sl is None:
                g_refs[i][...], d_refs[i][...], mo_refs[i][...], vo_refs[i][...] = g, dl, mn, vn
            else:
                g_refs[i][:, sl], d_refs[i][:, sl], mo_refs[i][:, sl], vo_refs[i][:, sl] = g, dl, mn, vn

        for i, (name, row, width) in enumerate(_SMALL):
            g = t_ref[row:row + 1, 0:width]
            if name == "ret_decay_fwd":
                g = g * sg_ref[0:1, 0:width]
            elif name == "ret_decay_bwd":
                g = g * sg_ref[1:2, 0:width]
            update(i, g)
        i_cc, i_b = n - 2, n - 1
        cc = w_refs[i_cc][...]
        s = 1.0 / (1.0 + jnp.exp(-cc))
        dsilu = dcc_ref[0, 0:1, :] + dcc_ref[2, 0:1, :] + dcc_ref[4, 0:1, :] + dcc_ref[6, 0:1, :]
        update(i_cc, dsilu * (s * (1.0 + cc * (1.0 - s))))
        for j in range(6):
            update(i_b, t_ref[PACK_ROWS + j:PACK_ROWS + j + 1, :], pl.ds(j * d, d))
        l_ref[...] = jnp.broadcast_to((0.5 / d) * jnp.sum(t_ref[7:8, :], keepdims=True), l_ref.shape)

    shapes = [jax.ShapeDtypeStruct(a.shape, F32) for a in ws]
    outs = pl.pallas_call(
        body, name="small_final", out_shape=shapes * 4 + [jax.ShapeDtypeStruct((8, LANES), F32)],
    )(tot, dcc, sg8, *ws, *ms, *vs)
    return outs[0:n], outs[n:2 * n], outs[2 * n:3 * n], outs[3 * n:4 * n], outs[4 * n]


_WEIGHTS = ("c_ctx", "w_ada", "b_ada", "g_attn", "g_ffn", "w_in", "ret_decay_fwd", "ret_decay_bwd", "g_ret", "g_q_lora",
            "w_uq", "g_kv_lora", "w_ukv", "w_out", "w_ff1", "w_ff2", "g_final")
_BIG = ("w_in", "w_uq", "w_ukv", "w_out", "w_ff1", "w_ff2")
_TRANSPOSED = ("w_in", "w_uq")


def kernel(x, c, ctx, c_ctx, w_ada, b_ada, g_attn, g_ffn, w_in, ret_decay_fwd, ret_decay_bwd, g_ret, g_q_lora, w_uq, g_kv_lora, w_ukv, w_out, w_ff1, w_ff2, g_final, loss_target, m_c_ctx, m_w_ada, m_b_ada, m_g_attn, m_g_ffn, m_w_in, m_ret_decay_fwd, m_ret_decay_bwd, m_g_ret, m_g_q_lora, m_w_uq, m_g_kv_lora, m_w_ukv, m_w_out, m_w_ff1, m_w_ff2, m_g_final, v_c_ctx, v_w_ada, v_b_ada, v_g_attn, v_g_ffn, v_w_in, v_ret_decay_fwd, v_ret_decay_bwd, v_g_ret, v_g_q_lora, v_w_uq, v_g_kv_lora, v_w_ukv, v_w_out, v_w_ff1, v_w_ff2, v_g_final):
    w = dict(c_ctx=c_ctx, w_ada=w_ada, b_ada=b_ada, g_attn=g_attn, g_ffn=g_ffn, w_in=w_in, ret_decay_fwd=ret_decay_fwd,
             ret_decay_bwd=ret_decay_bwd, g_ret=g_ret, g_q_lora=g_q_lora, w_uq=w_uq, g_kv_lora=g_kv_lora, w_ukv=w_ukv,
             w_out=w_out, w_ff1=w_ff1, w_ff2=w_ff2, g_final=g_final)
    m = dict(c_ctx=m_c_ctx, w_ada=m_w_ada, b_ada=m_b_ada, g_attn=m_g_attn, g_ffn=m_g_ffn, w_in=m_w_in,
             ret_decay_fwd=m_ret_decay_fwd, ret_decay_bwd=m_ret_decay_bwd, g_ret=m_g_ret, g_q_lora=m_g_q_lora, w_uq=m_w_uq,
             g_kv_lora=m_g_kv_lora, w_ukv=m_w_ukv, w_out=m_w_out, w_ff1=m_w_ff1, w_ff2=m_w_ff2, g_final=m_g_final)
    v = dict(c_ctx=v_c_ctx, w_ada=v_w_ada, b_ada=v_b_ada, g_attn=v_g_attn, g_ffn=v_g_ffn, w_in=v_w_in,
             ret_decay_fwd=v_ret_decay_fwd, ret_decay_bwd=v_ret_decay_bwd, g_ret=v_g_ret, g_q_lora=v_g_q_lora, w_uq=v_w_uq,
             g_kv_lora=v_g_kv_lora, w_ukv=v_w_ukv, w_out=v_w_out, w_ff1=v_w_ff1, w_ff2=v_w_ff2, g_final=v_g_final)
    xi, yi, ci = lax.axis_index("x"), lax.axis_index("y"), lax.axis_index("c")
    chip = 2 * xi + yi
    dev = 2 * chip + ci
    nex, seq, d = x.shape
    n_ada = w_ada.shape[2]

    dec = jnp.zeros((8, LANES), F32).at[0, :HEADS].set(ret_decay_fwd[0]).at[1, :HEADS].set(ret_decay_bwd[0])
    lg8, sg8 = _decay_prep(dec)
    lg = lg8[:2, :HEADS]

    def shard_of(t, k):
        return t[k][0].T if k in _TRANSPOSED else t[k][0]

    shard = {k: shard_of(w, k) for k in _BIG}
    head_rows = MLA_NOPE + MLA_ROPE
    shard["w_uq"] = jnp.pad(shard["w_uq"], ((0, MLA_HEAD - head_rows), (0, 0)))
    slot = chip.reshape(1).astype(jnp.int32)
    core = ci.reshape(1).astype(jnp.int32)
    slots = {k: _cast_into_slot(shard[k], slot, "cast_" + k) for k in _EARLY}
    half_ff = shard["w_ff2"].shape[0] // 2
    late_pieces = [(shard["w_out"], 0, shard["w_out"].shape[0]), (shard["w_ff1"], 0, shard["w_ff1"].shape[0]),
                   (shard["w_ff2"], 0, half_ff), (shard["w_ff2"], half_ff, half_ff)]
    late_slots, (w_in_f, w_uq_k, w_ukv_k, c8) = _cast_into_slots(
        late_pieces, slot, "cast_late",
        rider=_merge_riders(_gather_rider([slots[k] for k in _EARLY]),
                            _gather8_rider(jnp.pad(c, ((0, 8 - nex), (0, 0))), in_vmem=False)))

    a_in = jnp.concatenate([c8[:, :nex].reshape(N_DEV * nex, d), c_ctx.reshape(1, d), jnp.zeros((7, d), F32)], axis=0)
    b_sh = lax.dynamic_slice(b_ada, (0, chip * n_ada), (1, n_ada))
    mod_sh = _mod_fwd(a_in, w_ada[0], b_sh)
    mod8 = _allgather8(mod_sh, "ag_mod")
    w_in_k = jnp.pad(w_in_f.reshape(IN_COLS, d), ((0, IN_PAD - IN_COLS), (0, 0)))
    mod_all = mod8[0::2].transpose(1, 0, 2).reshape(a_in.shape[0], N_CHIPS * n_ada)
    mod_me = lax.dynamic_slice(mod_all, (nex * dev, 0), (nex, N_CHIPS * n_ada)).reshape(nex, 6, d)
    mod_c = mod_all[N_DEV * nex].reshape(1, 6, d)
    modv = jnp.pad(jnp.concatenate([mod_me, mod_c], axis=0), ((0, 0), (0, 2), (0, 0)))

    gx, g_early, late, st_post, st_ret, st_pre = _local_step(
        x, ctx, loss_target, modv, lg, g_attn, g_ffn, g_final.reshape(1, d), g_ret, g_q_lora, g_kv_lora,
        w_in_k, w_uq_k, w_ukv_k, late_slots, (core, slot))

    mine, theirs, (*late_theirs, gathered) = _reduce_scatter_vmem(
        g_early, [(IN_COLS // N_CHIPS, IN_COLS // N_CHIPS), (head_rows, MLA_HEAD), (KV_LORA, KV_LORA)],
        _merge_riders(_swap_rider(late), _gather8_rider(_pack_small(st_post, st_ret, st_pre))), "rs_early")
    tot = _small_reduce(gathered)
    dm = jnp.concatenate([
        gathered[:, 12:24].reshape(N_DEV * nex, 6 * d),
        jnp.concatenate([tot[8:10].reshape(1, 2 * d), jnp.zeros((1, 4 * d), F32)], axis=1),
        jnp.zeros((7, 6 * d), F32)], axis=0)
    dm_sh = lax.dynamic_slice(dm, (0, chip * n_ada), (dm.shape[0], n_ada))
    g_ada, da = _mod_bwd(a_in, dm_sh, w_ada[0])
    dcc = _allgather8(da[N_DEV * nex:], "ag_dcc")
    halves = dict(zip(_EARLY, zip(mine, theirs)))
    halves.update(zip(_LATE, zip(late, late_theirs)))
    grad, delta, new_m, new_v = {}, {}, {}, {}
    for k in _BIG:
        a, b = halves[k]
        res = _adamw_halves(shard_of(w, k), a, b, shard_of(m, k), shard_of(v, k), core, "adamw_" + k)
        grad[k], delta[k], new_m[k], new_v[k] = [(o.T if k in _TRANSPOSED else o).reshape(w[k].shape) for o in res]

    shp = w_ada.shape
    outs, _ = _adamw(w_ada[0], g_ada, m["w_ada"][0], v["w_ada"][0], "adamw_w_ada")
    grad["w_ada"] = g_ada.reshape(shp)
    delta["w_ada"], new_m["w_ada"], new_v["w_ada"] = [o.reshape(shp) for o in outs]
    rows = [{k: t[k].reshape(1, -1) for k in _SMALL_NAMES} for t in (w, m, v)]
    small = _small_final(tot, dcc, sg8, *[[t[k] for k in _SMALL_NAMES] for t in rows])
    for res, outs in zip((grad, delta, new_m, new_v), small[:4]):
        for k, o in zip(_SMALL_NAMES, outs):
            res[k] = o.reshape(w[k].shape)
    return (small[4][0, 0], gx, *[grad[k] for k in _WEIGHTS], *[delta[k] for k in _WEIGHTS],
            *[new_m[k] for k in _WEIGHTS], *[new_v[k] for k in _WEIGHTS])
```

```python
import functools
import math

import jax
import jax.numpy as jnp
from jax import lax
from jax.experimental import pallas as pl
from jax.experimental.pallas import tpu as pltpu

F32 = jnp.float32
BF16 = jnp.bfloat16
MESH = pl.DeviceIdType.MESH

EPS = 1e-6
D_MODEL = 1024
D_FF = 4096
HEADS = 4
RET_DK = 64
RET_DV = 128
MLA_NOPE = 128
MLA_ROPE = 64
MLA_HEAD = 256
Q_LORA = 384
KV_LORA = 256
GRID_W = 64
ROPE_BASE = 10000.0
IN_COLS = 2240
IN_PAD = 2304
PG_COLS = 1152
N_CHIPS = 4
N_DEV = 8
LANES = 128
ADAM_LR = 0.001
ADAM_B1 = 0.9
ADAM_B2 = 0.999
ADAM_EPS = 1e-08
ADAM_WD = 0.01
ADAM_STEP = 10
VMEM_LIMIT = 56 * 1024 * 1024


def _dot(a, b):
    return jnp.dot(a, b, preferred_element_type=F32)


def _dot_nt(a, b):
    return lax.dot_general(a, b, (((1,), (1,)), ((), ())), preferred_element_type=F32)


def _dot_tn(a, b):
    return lax.dot_general(a, b, (((0,), (0,)), ((), ())), preferred_element_type=F32)


def _params(sem=None, vmem=None):
    return pltpu.CompilerParams(dimension_semantics=sem, vmem_limit_bytes=vmem)


def _full(shape):
    n = len(shape)
    return pl.BlockSpec(shape, lambda *_: (0,) * n)


def _once(shape):
    n = len(shape)
    return pl.BlockSpec(shape, lambda *_: (0,) * n, pipeline_mode=pl.Buffered(1))


def _rope(x, cos, sin):
    w = x.shape[-1]
    lo = (lax.broadcasted_iota(jnp.int32, (1, w), 1) % 64) < 32
    swapped = jnp.where(lo, pltpu.roll(x, w - 32, 1), pltpu.roll(x, 32, 1))
    return x * cos + swapped * sin


def _rope_t(g, cos, sin):
    w = g.shape[-1]
    lo = (lax.broadcasted_iota(jnp.int32, (1, w), 1) % 64) < 32
    t = g * sin
    swapped = jnp.where(lo, pltpu.roll(t, w - 32, 1), pltpu.roll(t, 32, 1))
    return g * cos + swapped


def _rope_tables(seq, tm):
    rows = seq // GRID_W
    row = jnp.repeat(jnp.arange(rows, dtype=F32), GRID_W)
    col = jnp.tile(jnp.arange(GRID_W, dtype=F32), rows)
    n_freq = RET_DK // 4
    freq = ROPE_BASE ** (-jnp.arange(n_freq, dtype=F32) / n_freq)
    ang = jnp.concatenate([row[:, None] * freq, col[:, None] * freq], axis=-1)
    cos, sin = jnp.cos(ang), jnp.sin(ang)
    cos_t = jnp.tile(jnp.concatenate([cos, cos], -1), (1, HEADS))
    sin_t = jnp.tile(jnp.concatenate([-sin, sin], -1), (1, HEADS))
    cos_t = jnp.concatenate([cos_t, jnp.ones((tm, 4 * RET_DK), F32)], 0)
    sin_t = jnp.concatenate([sin_t, jnp.zeros((tm, 4 * RET_DK), F32)], 0)
    return cos_t, sin_t


def _adam_math(w, g, m, v):
    mn = ADAM_B1 * m + (1.0 - ADAM_B1) * g
    vn = ADAM_B2 * v + (1.0 - ADAM_B2) * (g * g)
    m_hat = mn / (1.0 - ADAM_B1 ** ADAM_STEP)
    v_hat = vn / (1.0 - ADAM_B2 ** ADAM_STEP)
    return -ADAM_LR * (m_hat / (jnp.sqrt(v_hat) + ADAM_EPS) + ADAM_WD * w), mn, vn


def _cast_into_slots(pieces, slot, name, rider=None):
    c = pieces[0][0].shape[1]
    rb = max(b for b in range(16, 257, 16) if all(cnt % b == 0 and st % b == 0 for _, st, cnt in pieces))
    nbs = [cnt // rb for _, _, cnt in pieces]
    starts = [sum(nbs[:s]) for s in range(len(pieces))]

    def body(s_ref, *refs):
        i = pl.program_id(0)
        for s in range(len(pieces)):
            @pl.when(jnp.logical_and(i >= starts[s], i < starts[s] + nbs[s]))
            def _():
                refs[len(pieces) + s][...] = refs[s][...].astype(BF16)

    in_specs, out_specs = [], []
    for (_, first_row, _), nb, st in zip(pieces, nbs, starts):
        in_specs.append(pl.BlockSpec((rb, c), lambda i, s, nb=nb, st=st, f=first_row // rb: (f + jnp.clip(i - st, 0, nb - 1), 0)))
        out_specs.append(pl.BlockSpec((None, rb, c), lambda i, s, nb=nb, st=st: (s[0], jnp.clip(i - st, 0, nb - 1), 0)))
    return _hosted_call(
        body, [w for w, _, _ in pieces], name=name, grid=(sum(nbs),), prefetch=(slot,), in_specs=in_specs,
        out_specs=out_specs, out_shape=[jax.ShapeDtypeStruct((N_CHIPS, cnt, c), BF16) for _, _, cnt in pieces],
        sem=("arbitrary",), rider=rider)


def _cast_into_slot(w, slot, name):
    return _cast_into_slots([(w, 0, w.shape[0])], slot, name)[0][0]


def _adamw_halves(w, mine, theirs, m, v, core, name):
    r, c = w.shape
    r2 = r // 2
    rb = max(b for b in range(8, r2 + 1, 8) if r2 % b == 0 and b * c * 4 <= (1 << 19))
    nbh = r2 // rb

    def body(z_ref, w_ref, a_ref, b_ref, m_ref, v_ref, g_ref, d_ref, mo_ref, vo_ref):
        here = (pl.program_id(0) // nbh) == z_ref[0]
        gg = jnp.where(here, a_ref[...], b_ref[...])
        g_ref[...] = gg
        d_ref[...], mo_ref[...], vo_ref[...] = _adam_math(w_ref[...], gg, m_ref[...], v_ref[...])

    spec = pl.BlockSpec((rb, c), lambda i, z: (i, 0))
    a_spec = pl.BlockSpec((rb, c), lambda i, z: (jnp.clip(i - z[0] * nbh, 0, nbh - 1), 0))
    b_spec = pl.BlockSpec((rb, c), lambda i, z: (jnp.clip(i - (1 - z[0]) * nbh, 0, nbh - 1), 0))
    shp = jax.ShapeDtypeStruct((r, c), F32)
    return pl.pallas_call(
        body, name=name,
        grid_spec=pltpu.PrefetchScalarGridSpec(
            num_scalar_prefetch=1, grid=(r // rb,), in_specs=[spec, a_spec, b_spec, spec, spec], out_specs=[spec] * 4),
        out_shape=[shp] * 4,
        compiler_params=_params(("parallel",)),
    )(core, w, mine, theirs, m, v)


def _adamw(w, g, m, v, name, rider=None):
    r, c = w.shape
    rb = r
    for cand in (256, 128, 64, 32, 16, 8):
        if r % cand == 0 and cand * c * 4 <= (1 << 20):
            rb = cand
            break
    if r * c * 4 <= (1 << 20):
        rb = r

    def body(w_ref, g_ref, m_ref, v_ref, d_ref, mo_ref, vo_ref):
        d_ref[...], mo_ref[...], vo_ref[...] = _adam_math(w_ref[...], g_ref[...], m_ref[...], v_ref[...])

    spec = pl.BlockSpec((rb, c), lambda i: (i, 0))
    shp = jax.ShapeDtypeStruct((r, c), F32)
    return _hosted_call(
        body, (w, g, m, v), name=name, grid=(r // rb,), in_specs=[spec] * 4, out_specs=[spec] * 3, out_shape=[shp] * 3,
        sem=("parallel",), rider=rider)


def _decay_prep(dec):
    def body(d_ref, lg_ref, sg_ref):
        d = d_ref[...]
        lg_ref[...] = jnp.minimum(d, 0.0) - jnp.log(1.0 + jnp.exp(-jnp.abs(d)))
        sg_ref[...] = 1.0 / (1.0 + jnp.exp(d))

    shp = jax.ShapeDtypeStruct(dec.shape, F32)
    return pl.pallas_call(body, name="decay_prep", out_shape=[shp, shp])(dec)


def _mod_fwd(a_in, w_ada, b_sh):
    rows, d = a_in.shape
    n = w_ada.shape[1]
    bn = 512

    def body(a_ref, w_ref, b_ref, o_ref):
        a = a_ref[...]
        s = (a / (1.0 + jnp.exp(-a))).astype(BF16)
        o_ref[...] = _dot(s, w_ref[...].astype(BF16)) + b_ref[...]

    return pl.pallas_call(
        body, name="mod_fwd", grid=(n // bn,),
        in_specs=[_full((rows, d)), pl.BlockSpec((d, bn), lambda j: (0, j)), pl.BlockSpec((1, bn), lambda j: (0, j))],
        out_specs=pl.BlockSpec((rows, bn), lambda j: (0, j)),
        out_shape=jax.ShapeDtypeStruct((rows, n), F32),
        compiler_params=_params(("parallel",)),
    )(a_in, w_ada, b_sh)


def _mod_bwd(a_in, dm, w_ada):
    rows, d = a_in.shape
    n = w_ada.shape[1]
    bn = 512
    nb = n // bn

    def body(a_ref, dm_ref, w_ref, gw_ref, da_ref):
        j = pl.program_id(0)
        a = a_ref[...]
        s = (a / (1.0 + jnp.exp(-a))).astype(BF16)
        dmb = dm_ref[...].astype(BF16)
        gw_ref[...] = _dot_tn(s, dmb)
        part = _dot_nt(dmb, w_ref[...].astype(BF16))

        @pl.when(j == 0)
        def _():
            da_ref[...] = part

        @pl.when(j > 0)
        def _():
            da_ref[...] += part

    return pl.pallas_call(
        body, name="mod_bwd", grid=(nb,),
        in_specs=[_full((rows, d)), pl.BlockSpec((rows, bn), lambda j: (0, j)), pl.BlockSpec((d, bn), lambda j: (0, j))],
        out_specs=[pl.BlockSpec((d, bn), lambda j: (0, j)), _full((rows, d))],
        out_shape=[jax.ShapeDtypeStruct((d, n), F32), jax.ShapeDtypeStruct((rows, d), F32)],
        compiler_params=_params(("arbitrary",)),
    )(a_in, dm, w_ada)


def _pre_fwd(x2, ctx2, modv, g_attn, w_in, g_q, g_kv, w_uq, w_ukv, cos_t, sin_t, *, seq, tm, rider=None):
    t_lat, d = x2.shape
    t_ctx = ctx2.shape[0]
    nl, nc = t_lat // tm, t_ctx // tm
    n_all = t_lat + t_ctx
    tpe = seq // tm
    nex = t_lat // seq

    def body(x_ref, c_ref, mod_ref, g_ref, win_ref, gq_ref, gkv_ref, wuq_ref, wukv_ref, cos_ref, sin_ref,
             h_ref, pg_ref, rq_ref, rk_ref, rv_ref, nq_ref, nkv_ref, q_ref, k_ref, v_ref):
        i = pl.program_id(0)
        xt = jnp.where(i < nl, x_ref[...], c_ref[...])
        sh = mod_ref[0, 0:1, :]
        sc = mod_ref[0, 1:2, :]
        r = lax.rsqrt(jnp.mean(xt * xt, axis=-1, keepdims=True) + EPS)
        hb = ((xt * r) * g_ref[...] * (1.0 + sc) + sh).astype(BF16)
        h_ref[...] = hb
        p = _dot_nt(hb, win_ref[...])
        cos = cos_ref[...]
        sin = sin_ref[...]
        rq_ref[...] = _rope(p[:, 0:256], cos, sin).astype(BF16)
        rk_ref[...] = _rope(p[:, 256:512] * (RET_DK ** -0.5), cos, sin).astype(BF16)
        rv_ref[...] = p[:, 512:1024].astype(BF16)
        pg_ref[...] = p[:, 1024:2176]
        cq = p[:, 1536:1920]
        ckv = p[:, 1920:2176]
        nqb = (cq * lax.rsqrt(jnp.mean(cq * cq, axis=-1, keepdims=True) + EPS) * gq_ref[...]).astype(BF16)
        nkvb = (ckv * lax.rsqrt(jnp.mean(ckv * ckv, axis=-1, keepdims=True) + EPS) * gkv_ref[...]).astype(BF16)
        nq_ref[...] = nqb
        nkv_ref[...] = nkvb
        cos1 = cos[:, 0:LANES]
        sin1 = sin[:, 0:LANES]
        kpe = _rope(p[:, 2176:2304], cos1, sin1).astype(BF16)
        for hd in range(HEADS):
            o = hd * MLA_HEAD
            qh = _dot_nt(nqb, wuq_ref[hd]) * MLA_SCALE
            q_ref[:, o:o + 128] = qh[:, 0:128].astype(BF16)
            q_ref[:, o + 128:o + 256] = _rope(qh[:, 128:256], cos1, sin1).astype(BF16)
            kvh = _dot(nkvb, wukv_ref[hd])
            k_ref[:, o:o + 128] = kvh[:, 0:128].astype(BF16)
            k_ref[:, o + 128:o + 256] = kpe
            v_ref[:, hd * 128:(hd + 1) * 128] = kvh[:, 128:256].astype(BF16)

    def tile(width):
        return pl.BlockSpec((tm, width), lambda i: (i, 0))

    widths = (d, PG_COLS, 256, 256, 512, Q_LORA, KV_LORA, HEADS * MLA_HEAD, HEADS * MLA_HEAD, HEADS * 128)
    dtypes = (BF16, F32, BF16, BF16, BF16, BF16, BF16, BF16, BF16, BF16)
    tab = pl.BlockSpec((tm, 256), lambda i: (jnp.where(i < nl, i % tpe, tpe), 0))
    return _hosted_call(
        body, (x2, ctx2, modv, g_attn, w_in, g_q, g_kv, w_uq, w_ukv, cos_t, sin_t), name="pre_fwd", grid=(nl + nc,),
        in_specs=[
            pl.BlockSpec((tm, d), lambda i: (jnp.minimum(i, nl - 1), 0)),
            pl.BlockSpec((tm, d), lambda i: (jnp.maximum(i - nl, 0), 0)),
            pl.BlockSpec((1, 8, d), lambda i: (jnp.minimum(i // tpe, nex), 0, 0)),
            _full((1, d)), _full(w_in.shape), _full((1, Q_LORA)), _full((1, KV_LORA)),
            _full(w_uq.shape), _full(w_ukv.shape), tab, tab,
        ],
        out_specs=[tile(w) for w in widths],
        out_shape=[jax.ShapeDtypeStruct((n_all, w), dt) for w, dt in zip(widths, dtypes)],
        sem=("parallel",), rider=rider)


def _post(yret, ymla, x2, tgt2, modv, g_ffn, g_fin, w_out, w_ff1, w_ff2a, w_ff2b, *, seq, tm):
    t_lat, d = x2.shape
    nl = t_lat // tm
    tpe = seq // tm
    nex = t_lat // seq
    n_slab = w_ff1.shape[0]
    fs = w_ff1.shape[2]
    fh = w_ff2a.shape[1]

    def body(yr_ref, ym_ref, x_ref, t_ref, mod_ref, gf_ref, gl_ref, wo_ref, w1_ref, w2a_ref, w2b_ref,
             mix_ref, a_ref, du_ref, h2_ref, df_ref, dmo_ref, dmix_ref, dxm_ref, st_ref, ru_ref):
        i = pl.program_id(0)
        gt_a = mod_ref[0, 2:3, :]
        sh_f = mod_ref[0, 3:4, :]
        sc_f = mod_ref[0, 4:5, :]
        gt_f = mod_ref[0, 5:6, :]
        g_ffn_v = gf_ref[...]
        g_fin_v = gl_ref[...]
        yr = yr_ref[...]
        ym = ym_ref[...]
        mix_ref[:, 0:512] = yr
        mix_ref[:, 512:1024] = ym
        op = _dot(yr, wo_ref[0:512, :]) + _dot(ym, wo_ref[512:1024, :])
        x_mid = x_ref[...] + gt_a * op
        r2 = lax.rsqrt(jnp.mean(x_mid * x_mid, axis=-1, keepdims=True) + EPS)
        xh2 = x_mid * r2
        h2b = (xh2 * g_ffn_v * (1.0 + sc_f) + sh_f).astype(BF16)
        h2_ref[...] = h2b
        f = jnp.zeros((tm, d), F32)
        for s in range(n_slab):
            ru = jnp.maximum(_dot(h2b, w1_ref[s]), 0.0)
            ru_ref[:, s * fs:(s + 1) * fs] = ru
            ab = (ru * ru).astype(BF16)
            a_ref[:, s * fs:(s + 1) * fs] = ab
            f = f + _dot(ab[:, 0:fh], w2a_ref[s]) + _dot(ab[:, fh:fs], w2b_ref[s])
        x_out = x_mid + gt_f * f
        r3 = lax.rsqrt(jnp.mean(x_out * x_out, axis=-1, keepdims=True) + EPS)
        xh3 = x_out * r3
        err = xh3 * g_fin_v - t_ref[...]
        dy = err * (1.0 / d)
        dxh3 = dy * g_fin_v
        dx_out = r3 * (dxh3 - xh3 * jnp.mean(dxh3 * xh3, axis=-1, keepdims=True))
        dfb = (dx_out * gt_f).astype(BF16)
        df_ref[...] = dfb
        dh2 = jnp.zeros((tm, d), F32)
        for s in range(n_slab):
            da = jnp.concatenate([_dot_nt(dfb, w2a_ref[s]), _dot_nt(dfb, w2b_ref[s])], axis=1)
            dub = (da * (2.0 * ru_ref[:, s * fs:(s + 1) * fs])).astype(BF16)
            du_ref[:, s * fs:(s + 1) * fs] = dub
            dh2 = dh2 + _dot_nt(dub, w1_ref[s])
        dxh2 = dh2 * (1.0 + sc_f) * g_ffn_v
        dx_mid = dx_out + r2 * (dxh2 - xh2 * jnp.mean(dxh2 * xh2, axis=-1, keepdims=True))
        dxm_ref[...] = dx_mid
        dmob = (dx_mid * gt_a).astype(BF16)
        dmo_ref[...] = dmob
        dmix_ref[...] = _dot_nt(dmob, wo_ref[...]).astype(BF16)

        def rsum(v):
            return jnp.sum(v, axis=0, keepdims=True)

        stats = jnp.concatenate([
            rsum(dh2), rsum(dh2 * xh2 * g_ffn_v), rsum(dx_out * f), rsum(dx_mid * op),
            rsum(dh2 * (1.0 + sc_f) * xh2), rsum(dy * xh3), rsum(err * err), jnp.zeros((1, d), F32)], axis=0)

        @pl.when(i % tpe == 0)
        def _():
            st_ref[0] = stats

        @pl.when(i % tpe != 0)
        def _():
            st_ref[0] += stats

    def tile(width):
        return pl.BlockSpec((tm, width), lambda i: (i, 0))

    widths = (d, D_FF, D_FF, d, d, d, d, d)
    dtypes = (BF16, BF16, BF16, BF16, BF16, BF16, BF16, F32)
    const = pl.Buffered(1)
    return pl.pallas_call(
        body, name="post", grid=(nl,),
        in_specs=[
            tile(512), tile(512), tile(d), tile(d),
            pl.BlockSpec((1, 8, d), lambda i: (i // tpe, 0, 0)),
            _full((1, d)), _full((1, d)),
            pl.BlockSpec(w_out.shape, lambda i: (0, 0), pipeline_mode=const),
            pl.BlockSpec(w_ff1.shape, lambda i: (0, 0, 0), pipeline_mode=const),
            pl.BlockSpec(w_ff2a.shape, lambda i: (0, 0, 0), pipeline_mode=const),
            pl.BlockSpec(w_ff2b.shape, lambda i: (0, 0, 0), pipeline_mode=const),
        ],
        out_specs=[tile(w) for w in widths] + [pl.BlockSpec((1, 8, d), lambda i: (i // tpe, 0, 0))],
        out_shape=[jax.ShapeDtypeStruct((t_lat, w), dt) for w, dt in zip(widths, dtypes)]
        + [jax.ShapeDtypeStruct((nex, 8, d), F32)],
        scratch_shapes=[pltpu.VMEM((tm, D_FF), F32)],
        compiler_params=_params(("arbitrary",), VMEM_LIMIT),
    )(yret, ymla, x2, tgt2, modv, g_ffn, g_fin, w_out, w_ff1, w_ff2a, w_ff2b)


def _pre_bwd(x2, ctx2, modv, g_attn, pg, drq, drk, dkc_r, drv, dvc_r, drg, dq_m, dkl, dkc, dvl, dvc, dxm,
             w_in, g_q, g_kv, w_uq, w_ukv, cos_t, sin_t, *, seq, tm, rider=None):
    t_lat, d = x2.shape
    t_ctx = ctx2.shape[0]
    nl, nc = t_lat // tm, t_ctx // tm
    n_all = t_lat + t_ctx
    tpe = seq // tm
    nex = t_lat // seq

    def body(x_ref, c_ref, mod_ref, g_ref, pg_ref, drq_ref, drk_ref, dkcr_ref, drv_ref, dvcr_ref, drg_ref,
             dq_ref, dkl_ref, dkc_ref, dvl_ref, dvc_ref, dxm_ref, win_ref, gq_ref, gkv_ref, wuq_ref, wukv_ref,
             cos_ref, sin_ref, dpb_ref, dqf_ref, dkvf_ref, gx_ref, st_ref):
        i = pl.program_id(0)
        lat = i < nl
        latf = lat.astype(F32)
        cos = cos_ref[...]
        sin = sin_ref[...]
        cos1 = cos[:, 0:LANES]
        sin1 = sin[:, 0:LANES]
        d_rq = _rope_t(drq_ref[...] * latf, cos, sin)
        d_rk = _rope_t(jnp.where(lat, drk_ref[...], dkcr_ref[...]), cos, sin) * (RET_DK ** -0.5)
        d_rv = jnp.where(lat, drv_ref[...], dvcr_ref[...])
        d_rg = drg_ref[...] * latf
        dq_all = dq_ref[...] * (latf * MLA_SCALE)
        dk_all = jnp.where(lat, dkl_ref[...], dkc_ref[...])
        dv_all = jnp.where(lat, dvl_ref[...], dvc_ref[...])
        dnq = jnp.zeros((tm, Q_LORA), F32)
        dnkv = jnp.zeros((tm, KV_LORA), F32)
        dkpe = jnp.zeros((tm, LANES), F32)
        for hd in range(HEADS):
            o = hd * MLA_HEAD
            dqh = jnp.concatenate([dq_all[:, o:o + 128], _rope_t(dq_all[:, o + 128:o + 256], cos1, sin1)],
                                  axis=1).astype(BF16)
            dqf_ref[:, o:o + 256] = dqh
            dnq = dnq + _dot(dqh, wuq_ref[hd])
            dkpe = dkpe + dk_all[:, o + 128:o + 256]
            dkvh = jnp.concatenate([dk_all[:, o:o + 128], dv_all[:, hd * 128:(hd + 1) * 128]], axis=1).astype(BF16)
            dkvf_ref[:, o:o + 256] = dkvh
            dnkv = dnkv + _dot_nt(dkvh, wukv_ref[hd])
        d_kpe = _rope_t(dkpe, cos1, sin1)
        pgv = pg_ref[...]
        cq = pgv[:, 512:896]
        ckv = pgv[:, 896:1152]
        rq_ = lax.rsqrt(jnp.mean(cq * cq, axis=-1, keepdims=True) + EPS)
        cqh = cq * rq_
        dcqh = dnq * gq_ref[...]
        d_cq = rq_ * (dcqh - cqh * jnp.mean(dcqh * cqh, axis=-1, keepdims=True))
        rkv_ = lax.rsqrt(jnp.mean(ckv * ckv, axis=-1, keepdims=True) + EPS)
        ckvh = ckv * rkv_
        dckvh = dnkv * gkv_ref[...]
        d_ckv = rkv_ * (dckvh - ckvh * jnp.mean(dckvh * ckvh, axis=-1, keepdims=True))
        dpb = jnp.concatenate([d_rq, d_rk, d_rv, d_rg, d_cq, d_ckv, d_kpe], axis=1).astype(BF16)
        dpb_ref[...] = dpb
        dh = _dot(dpb, win_ref[...])
        xt = jnp.where(lat, x_ref[...], c_ref[...])
        sc = mod_ref[0, 1:2, :]
        g = g_ref[...]
        r = lax.rsqrt(jnp.mean(xt * xt, axis=-1, keepdims=True) + EPS)
        xh = xt * r
        dxh = dh * (1.0 + sc) * g
        dx = r * (dxh - xh * jnp.mean(dxh * xh, axis=-1, keepdims=True))

        @pl.when(lat)
        def _():
            gx_ref[...] = dxm_ref[...] + dx

        def rsum(v):
            return jnp.sum(v, axis=0, keepdims=True)

        def widen(v):
            return jnp.concatenate([v, jnp.zeros((1, d - v.shape[1]), F32)], axis=1)

        stats = jnp.concatenate([
            rsum(dh), rsum(dh * xh * g), rsum(dh * (1.0 + sc) * xh), widen(rsum(dnq * cqh)), widen(rsum(dnkv * ckvh)),
            jnp.zeros((3, d), F32)], axis=0)
        first = jnp.logical_or(jnp.logical_and(lat, i % tpe == 0), i == nl)

        @pl.when(first)
        def _():
            st_ref[0] = stats

        @pl.when(jnp.logical_not(first))
        def _():
            st_ref[0] += stats

    def lat_tile(width):
        return pl.BlockSpec((tm, width), lambda i: (jnp.minimum(i, nl - 1), 0))

    def ctx_tile(width):
        return pl.BlockSpec((tm, width), lambda i: (jnp.maximum(i - nl, 0), 0))

    def tile(width):
        return pl.BlockSpec((tm, width), lambda i: (i, 0))

    tab = pl.BlockSpec((tm, 256), lambda i: (jnp.where(i < nl, i % tpe, tpe), 0))
    ex = pl.BlockSpec((1, 8, d), lambda i: (jnp.minimum(i // tpe, nex), 0, 0))
    return _hosted_call(
        body, (x2, ctx2, modv, g_attn, pg, drq, drk, dkc_r, drv, dvc_r, drg, dq_m, dkl, dkc, dvl, dvc, dxm,
               w_in, g_q, g_kv, w_uq, w_ukv, cos_t, sin_t), name="pre_bwd", grid=(nl + nc,),
        in_specs=[
            lat_tile(d), ctx_tile(d), ex, _full((1, d)), tile(PG_COLS),
            lat_tile(256), lat_tile(256), ctx_tile(256), lat_tile(512), ctx_tile(512), lat_tile(512),
            lat_tile(1024), lat_tile(1024), ctx_tile(1024), lat_tile(512), ctx_tile(512), lat_tile(d),
            _once(w_in.shape), _full((1, Q_LORA)), _full((1, KV_LORA)), _once(w_uq.shape), _once(w_ukv.shape),
            tab, tab,
        ],
        out_specs=[tile(IN_PAD), tile(1024), tile(1024), lat_tile(d), ex],
        out_shape=[
            jax.ShapeDtypeStruct((n_all, IN_PAD), BF16), jax.ShapeDtypeStruct((n_all, 1024), BF16),
            jax.ShapeDtypeStruct((n_all, 1024), BF16), jax.ShapeDtypeStruct((t_lat, d), F32),
            jax.ShapeDtypeStruct((nex + 1, 8, d), F32),
        ],
        sem=("arbitrary",), rider=rider)


MLA_SCALE = 1.0 / math.sqrt(MLA_NOPE + MLA_ROPE)
KEY_BLOCK = 1024


def _mla_specs(t_lat, seq, ctx_len, tq, heads=1):
    nqt = seq // tq
    cb = t_lat // ctx_len
    q = pl.BlockSpec((tq, heads * MLA_HEAD), lambda b, h, j: (b * nqt + j, h))
    kl = pl.BlockSpec((seq, heads * MLA_HEAD), lambda b, h, j: (b, h))
    kc = pl.BlockSpec((ctx_len, heads * MLA_HEAD), lambda b, h, j: (cb + b, h))
    vl = pl.BlockSpec((seq, heads * 128), lambda b, h, j: (b, h))
    vc = pl.BlockSpec((ctx_len, heads * 128), lambda b, h, j: (cb + b, h))
    o = pl.BlockSpec((tq, heads * 128), lambda b, h, j: (b * nqt + j, h))
    return q, kl, kc, vl, vc, o


FWD_HEADS = 2
BWD_HEADS = 1


def _mla_fwd(q, k, v, *, t_lat, seq, ctx_len, tq, rider=None):
    nex = t_lat // seq

    def body(q_ref, kl_ref, kc_ref, vl_ref, vc_ref, o_ref, lse_ref):
        for hh in range(FWD_HEADS):
            wide = slice(hh * MLA_HEAD, (hh + 1) * MLA_HEAD)
            cols = slice(hh * 128, (hh + 1) * 128)
            qb = q_ref[:, wide]
            s = _dot_nt(qb, kl_ref[:, wide])
            sc = _dot_nt(qb, kc_ref[:, wide])
            m = jnp.maximum(jnp.max(s, axis=-1, keepdims=True), jnp.max(sc, axis=-1, keepdims=True))
            p = jnp.exp(s - m)
            pc = jnp.exp(sc - m)
            total = jnp.sum(p, axis=-1, keepdims=True) + jnp.sum(pc, axis=-1, keepdims=True)
            o = _dot(p.astype(BF16), vl_ref[:, cols]) + _dot(pc.astype(BF16), vc_ref[:, cols])
            o_ref[:, cols] = (o * (1.0 / total)).astype(BF16)
            lse_ref[:, cols] = jnp.broadcast_to(m + jnp.log(total), (tq, 128))

    qs, kl, kc, vl, vc, os_ = _mla_specs(t_lat, seq, ctx_len, tq, FWD_HEADS)
    return _hosted_call(
        body, (q, k, k, v, v), name="mla_fwd", grid=(nex, HEADS // FWD_HEADS, seq // tq),
        in_specs=[qs, kl, kc, vl, vc], out_specs=[os_, os_],
        out_shape=[jax.ShapeDtypeStruct((t_lat, HEADS * 128), BF16), jax.ShapeDtypeStruct((t_lat, HEADS * 128), F32)],
        sem=("parallel", "parallel", "arbitrary"), rider=rider)


def _mla_bwd(q, k, v, ymla, lse, dmix, *, t_lat, seq, ctx_len, tq, rider=None):
    nex = t_lat // seq
    nqt = seq // tq
    t_ctx = nex * ctx_len
    kb = min(KEY_BLOCK, seq)

    def body(q_ref, kl_ref, kc_ref, vl_ref, vc_ref, o_ref, lse_ref, do_ref, dq_ref, dkl_out, dkc_out, dvl_out, dvc_out,
             dkl_ref, dkc_ref, dvl_ref, dvc_ref):
        j = pl.program_id(2)

        @pl.when(j == 0)
        def _():
            dkl_ref[...] = jnp.zeros(dkl_ref.shape, F32)
            dkc_ref[...] = jnp.zeros(dkc_ref.shape, F32)
            dvl_ref[...] = jnp.zeros(dvl_ref.shape, F32)
            dvc_ref[...] = jnp.zeros(dvc_ref.shape, F32)

        for hh in range(BWD_HEADS):
            wide = slice(hh * MLA_HEAD, (hh + 1) * MLA_HEAD)
            cols = slice(hh * 128, (hh + 1) * 128)
            qb = q_ref[:, wide]
            dob = do_ref[:, cols]
            delta = jnp.sum(dob.astype(F32) * o_ref[:, cols].astype(F32), axis=-1, keepdims=True)
            lse_row = lse_ref[:, hh * 128:hh * 128 + 1]

            def block(k_ref, v_ref, dk_ref, dv_ref, rows):
                kbl = k_ref[rows, wide]
                vbl = v_ref[rows, cols]
                p = jnp.exp(_dot_nt(qb, kbl) - lse_row)
                ds = (p * (_dot_nt(dob, vbl) - delta)).astype(BF16)
                dk_ref[rows, wide] += _dot_tn(ds, qb)
                dv_ref[rows, cols] += _dot_tn(p.astype(BF16), dob)
                return _dot(ds, kbl)

            dq = block(kc_ref, vc_ref, dkc_ref, dvc_ref, pl.ds(0, ctx_len))
            for i in range(seq // kb):
                dq = dq + block(kl_ref, vl_ref, dkl_ref, dvl_ref, pl.ds(i * kb, kb))
            dq_ref[:, wide] = dq.astype(BF16)

        @pl.when(j == nqt - 1)
        def _():
            dkl_out[...] = dkl_ref[...].astype(BF16)
            dkc_out[...] = dkc_ref[...].astype(BF16)
            dvl_out[...] = dvl_ref[...].astype(BF16)
            dvc_out[...] = dvc_ref[...].astype(BF16)

    g = BWD_HEADS
    qs, kl, kc, vl, vc, os_ = _mla_specs(t_lat, seq, ctx_len, tq, g)
    do_spec = pl.BlockSpec((tq, g * 128), lambda b, h, j: (b * nqt + j, HEADS // g + h))
    key_blocks = [(seq, g * MLA_HEAD), (ctx_len, g * MLA_HEAD), (seq, g * 128), (ctx_len, g * 128)]
    return _hosted_call(
        body, (q, k, k, v, v, ymla, lse, dmix), name="mla_bwd", grid=(nex, HEADS // g, nqt),
        in_specs=[qs, kl, kc, vl, vc, os_, os_, do_spec],
        out_specs=[qs] + [pl.BlockSpec(blk, lambda b, h, j: (b, h)) for blk in key_blocks],
        out_shape=[
            jax.ShapeDtypeStruct((t_lat, HEADS * MLA_HEAD), BF16),
            jax.ShapeDtypeStruct((t_lat, HEADS * MLA_HEAD), BF16),
            jax.ShapeDtypeStruct((t_ctx, HEADS * MLA_HEAD), BF16),
            jax.ShapeDtypeStruct((t_lat, HEADS * 128), BF16),
            jax.ShapeDtypeStruct((t_ctx, HEADS * 128), BF16),
        ],
        scratch_shapes=[pltpu.VMEM(blk, F32) for blk in key_blocks],
        sem=("parallel", "parallel", "arbitrary"), rider=rider)


def _decay_terms(lg, chunk, forward):
    ii = lax.broadcasted_iota(jnp.int32, (chunk, chunk), 0)
    jj = lax.broadcasted_iota(jnp.int32, (chunk, chunk), 1)
    diff = (ii - jj) if forward else (jj - ii)
    dist = jnp.maximum(diff, 0).astype(F32)
    dmat = jnp.where(diff >= 0, jnp.exp(lg * dist), 0.0)
    pos = lax.broadcasted_iota(jnp.int32, (chunk, 1), 0).astype(F32)
    if forward:
        e_q = pos + 1.0
        e_k = (chunk - 1.0) - pos
    else:
        e_q = chunk - pos
        e_k = pos
    wq = jnp.exp(lg * e_q)
    wk = jnp.exp(lg * e_k)
    cd = jnp.exp(jnp.full((1, 1), lg * chunk, F32))
    return dmat, dist, wq, wk, e_q, e_k, cd


def _ctx_weights(lg, ctx_len, forward):
    pos = lax.broadcasted_iota(jnp.int32, (ctx_len, 1), 0).astype(F32)
    e = ((ctx_len - 1.0) - pos) if forward else pos
    return jnp.exp(lg * e), e


def _pair_specs(t_lat, seq, ctx_len):
    cb = t_lat // ctx_len
    qk = pl.BlockSpec((seq, 128), lambda b, p: (b, p))
    v = pl.BlockSpec((seq, 256), lambda b, p: (b, p))
    kc = pl.BlockSpec((ctx_len, 128), lambda b, p: (cb + b, p))
    vc = pl.BlockSpec((ctx_len, 256), lambda b, p: (cb + b, p))
    return qk, v, kc, vc


def _lane_masks():
    lane = lax.broadcasted_iota(jnp.int32, (1, 128), 1)
    return [(lane // RET_DK) == hh for hh in (0, 1)]


def _ret_fwd_pair(rq, rk, rv, pg, lg, g_ret, *, t_lat, seq, ctx_len, chunk, rider=None):
    nex = t_lat // seq
    n_chunk = seq // chunk

    def body(q_ref, k_ref, v_ref, kc_ref, vc_ref, rg_ref, lg_ref, g_ref, y_ref, o_ref):
        pair = pl.program_id(1)
        masks = _lane_masks()
        kcf = kc_ref[...].astype(F32)
        chains = [(forward, hh) for forward in (True, False) for hh in (0, 1)]
        terms, s0 = [], []
        for forward, hh in chains:
            lgd = lg_ref[0 if forward else 1, 2 * pair + hh]
            terms.append(_decay_terms(lgd, chunk, forward))
            wc, _ = _ctx_weights(lgd, ctx_len, forward)
            s0.append(_dot_tn((jnp.where(masks[hh], kcf, 0.0) * wc).astype(BF16), vc_ref[:, hh * 128:(hh + 1) * 128]))
        both = [terms[hh][0] + terms[2 + hh][0] for hh in (0, 1)]
        o_ref[...] = jnp.zeros(o_ref.shape, F32)

        def step(t, states):
            new = [None] * 4
            for forward in (True, False):
                n = t if forward else n_chunk - 1 - t
                sl = pl.ds(pl.multiple_of(n * chunk, chunk), chunk)
                qb = q_ref[sl, :]
                kf_all = k_ref[sl, :].astype(F32)
                for hh in (0, 1):
                    c = (0 if forward else 2) + hh
                    _, _, wq, wk, _, _, cd = terms[c]
                    cols = slice(hh * 128, (hh + 1) * 128)
                    qm = jnp.where(masks[hh], qb, jnp.zeros((), BF16))
                    kf = jnp.where(masks[hh], kf_all, 0.0)
                    vb = v_ref[sl, cols]
                    o = wq * _dot(qm, states[c].astype(BF16))
                    if forward:
                        o = o + _dot((_dot_nt(qm, kf.astype(BF16)) * both[hh]).astype(BF16), vb)
                    o_ref[sl, cols] += o
                    new[c] = cd * states[c] + _dot_tn((kf * wk).astype(BF16), vb)
            return tuple(new)

        lax.fori_loop(0, n_chunk, step, tuple(s0))

        def norm_step(n, carry):
            sl = pl.ds(pl.multiple_of(n * chunk, chunk), chunk)
            for hh in (0, 1):
                cols = slice(hh * 128, (hh + 1) * 128)
                o = o_ref[sl, cols]
                mu = jnp.mean(o, axis=-1, keepdims=True)
                oc = o - mu
                var = jnp.mean(oc * oc, axis=-1, keepdims=True)
                rg = rg_ref[sl, cols]
                y_ref[sl, cols] = (oc * lax.rsqrt(var + EPS) * g_ref[:, cols] * (rg / (1.0 + jnp.exp(-rg)))).astype(BF16)
            return carry

        lax.fori_loop(0, n_chunk, norm_step, 0)

    qk, v, kc, vc = _pair_specs(t_lat, seq, ctx_len)
    return _hosted_call(
        body, (rq, rk, rv, rk, rv, pg, lg, g_ret), name="ret_fwd", grid=(nex, HEADS // 2),
        in_specs=[qk, qk, v, kc, vc, v, pl.BlockSpec(memory_space=pltpu.SMEM), pl.BlockSpec((1, 256), lambda b, p: (0, p))],
        out_specs=[v, v],
        out_shape=[jax.ShapeDtypeStruct((t_lat, HEADS * RET_DV), BF16), jax.ShapeDtypeStruct((t_lat, HEADS * RET_DV), F32)],
        sem=("parallel", "arbitrary"), rider=rider)


def _ret_bwd_pair(rq, rk, rv, pg, osum, dmix, lg, g_ret, *, t_lat, seq, ctx_len, chunk, rider=None):
    nex = t_lat // seq
    n_chunk = seq // chunk
    t_ctx = nex * ctx_len

    def body(q_ref, k_ref, v_ref, kc_ref, vc_ref, rg_ref, o_ref, dy_ref, lg_ref, g_ref,
             dq_out, dk_out, dv_out, dkc_ref, dvc_ref, drg_ref, st_ref, do_s, s_st, dq_ref, dk_ref, dv_ref):
        pair = pl.program_id(1)
        masks = _lane_masks()
        kcf = kc_ref[...].astype(F32)

        def norm_step(n, dgains):
            sl = pl.ds(pl.multiple_of(n * chunk, chunk), chunk)
            out = []
            for hh in (0, 1):
                cols = slice(hh * 128, (hh + 1) * 128)
                gain = g_ref[:, cols]
                o = o_ref[sl, cols]
                mu = jnp.mean(o, axis=-1, keepdims=True)
                oc = o - mu
                rstd = lax.rsqrt(jnp.mean(oc * oc, axis=-1, keepdims=True) + EPS)
                ohat = oc * rstd
                rg = rg_ref[sl, cols]
                sg = 1.0 / (1.0 + jnp.exp(-rg))
                dy = dy_ref[sl, cols].astype(F32)
                don = dy * (rg * sg)
                drg_ref[sl, cols] = (dy * (ohat * gain) * (sg * (1.0 + rg * (1.0 - sg)))).astype(BF16)
                dohat = don * gain
                do_s[sl, cols] = rstd * (dohat - jnp.mean(dohat, axis=-1, keepdims=True)
                                         - ohat * jnp.mean(dohat * ohat, axis=-1, keepdims=True))
                out.append(dgains[hh] + jnp.sum(don * ohat, axis=0, keepdims=True))
            return tuple(out)

        zero_row = jnp.zeros((1, 128), F32)
        dgains = lax.fori_loop(0, n_chunk, norm_step, (zero_row, zero_row))
        dq_ref[...] = jnp.zeros(dq_ref.shape, F32)
        dk_ref[...] = jnp.zeros(dk_ref.shape, F32)
        dv_ref[...] = jnp.zeros(dv_ref.shape, F32)

        chains = [(forward, hh) for forward in (True, False) for hh in (0, 1)]
        terms, ctxw, s0 = [], [], []
        for forward, hh in chains:
            lgd = lg_ref[0 if forward else 1, 2 * pair + hh]
            terms.append(_decay_terms(lgd, chunk, forward))
            ctxw.append(_ctx_weights(lgd, ctx_len, forward))
            s0.append(_dot_tn((jnp.where(masks[hh], kcf, 0.0) * ctxw[-1][0]).astype(BF16), vc_ref[:, hh * 128:(hh + 1) * 128]))

        def chunk_at(t, ascending):
            n = t if ascending else n_chunk - 1 - t
            return n, pl.ds(pl.multiple_of(n * chunk, chunk), chunk)

        def state_step(t, states):
            new = []
            for c, (forward, hh) in enumerate(chains):
                n, sl = chunk_at(t, forward)
                wk, cd = terms[c][3], terms[c][6]
                s_st[c, n] = states[c]
                kf = jnp.where(masks[hh], k_ref[sl, :].astype(F32), 0.0)
                new.append(cd * states[c] + _dot_tn((kf * wk).astype(BF16), v_ref[sl, hh * 128:(hh + 1) * 128]))
            return tuple(new)

        lax.fori_loop(0, n_chunk, state_step, tuple(s0))

        both = [terms[hh][0] + terms[2 + hh][0] for hh in (0, 1)]

        def grad_step(t, carry):
            out = [None] * len(chains)
            in_chunk_b = [None, None]
            for forward in (True, False):
                n, sl = chunk_at(t, not forward)
                qb = q_ref[sl, :]
                kf_all = k_ref[sl, :].astype(F32)
                dq_sum = jnp.zeros((chunk, 128), F32)
                dk_sum = jnp.zeros((chunk, 128), F32)
                for hh in (0, 1):
                    c = (0 if forward else 2) + hh
                    g_next, dlg = carry[c]
                    dmat, dist, wq, wk, e_q, e_k, cd = terms[c]
                    cols = slice(hh * 128, (hh + 1) * 128)
                    qm = jnp.where(masks[hh], qb, jnp.zeros((), BF16))
                    kf = jnp.where(masks[hh], kf_all, 0.0)
                    kb = kf.astype(BF16)
                    vb = v_ref[sl, cols]
                    do = do_s[sl, cols]
                    dob = do.astype(BF16)
                    s_n = s_st[c, n]
                    s_nb = s_n.astype(BF16)
                    gb = g_next.astype(BF16)
                    dk_cross = wk * _dot_nt(vb, gb)
                    dv = _dot((kf * wk).astype(BF16), gb)
                    o_cross = wq * _dot(qm, s_nb)
                    dq_sum = dq_sum + wq * _dot_nt(dob, s_nb)
                    dk_sum = dk_sum + dk_cross
                    dlg = (dlg + chunk * cd * jnp.sum(g_next * s_n, keepdims=True)
                           + jnp.sum(e_k * jnp.sum(kf * dk_cross, axis=-1, keepdims=True), keepdims=True)
                           + jnp.sum(e_q * jnp.sum(o_cross * do, axis=-1, keepdims=True), keepdims=True))
                    if forward:
                        a_raw = _dot_nt(qm, kb)
                        da_raw = _dot_nt(dob, vb)
                        prod = a_raw * da_raw
                        dlg = dlg + jnp.sum(dist * dmat * prod, keepdims=True)
                        in_chunk_b[hh] = jnp.sum(terms[2 + hh][1] * terms[2 + hh][0] * prod, keepdims=True)
                        dab = (da_raw * both[hh]).astype(BF16)
                        dq_sum = dq_sum + _dot(dab, kb)
                        dk_sum = dk_sum + _dot_tn(dab, qm)
                        dv = dv + _dot_tn((a_raw * both[hh]).astype(BF16), dob)
                    else:
                        dlg = dlg + in_chunk_b[hh]
                    dv_ref[sl, cols] += dv
                    out[c] = (cd * g_next + _dot_tn((qm.astype(F32) * wq).astype(BF16), dob), dlg)
                dq_ref[sl, :] += dq_sum
                dk_ref[sl, :] += dk_sum
            return tuple(out)

        zero = (jnp.zeros((128, 128), F32), jnp.zeros((1, 1), F32))
        res = lax.fori_loop(0, n_chunk, grad_step, (zero,) * len(chains))
        dkc_sum = jnp.zeros((ctx_len, 128), F32)
        dvc = [jnp.zeros((ctx_len, 128), F32)] * 2
        dlgs = []
        for c, (forward, hh) in enumerate(chains):
            ds0, dlg = res[c]
            wc, e_c = ctxw[c]
            kcm = jnp.where(masks[hh], kcf, 0.0)
            ds0b = ds0.astype(BF16)
            dkc_part = wc * _dot_nt(vc_ref[:, hh * 128:(hh + 1) * 128], ds0b)
            dkc_sum = dkc_sum + dkc_part
            dvc[hh] = dvc[hh] + _dot((kcm * wc).astype(BF16), ds0b)
            dlgs.append(dlg + jnp.sum(e_c * jnp.sum(kcm * dkc_part, axis=-1, keepdims=True), keepdims=True))
        dq_out[...] = dq_ref[...].astype(BF16)
        dk_out[...] = dk_ref[...].astype(BF16)
        dv_out[...] = dv_ref[...].astype(BF16)
        dkc_ref[...] = dkc_sum
        for hh in (0, 1):
            cols = slice(hh * 128, (hh + 1) * 128)
            dvc_ref[:, cols] = dvc[hh]
            st_ref[0, :, cols] = jnp.concatenate([
                dgains[hh], jnp.broadcast_to(dlgs[hh], (1, 128)), jnp.broadcast_to(dlgs[2 + hh], (1, 128)),
                jnp.zeros((5, 128), F32)], axis=0)

    qk, v, kc, vc = _pair_specs(t_lat, seq, ctx_len)
    return _hosted_call(
        body, (rq, rk, rv, rk, rv, pg, osum, dmix, lg, g_ret), name="ret_bwd", grid=(nex, HEADS // 2),
        in_specs=[qk, qk, v, kc, vc, v, v, v, pl.BlockSpec(memory_space=pltpu.SMEM),
                  pl.BlockSpec((1, 256), lambda b, p: (0, p))],
        out_specs=[
            qk, qk, v,
            pl.BlockSpec((ctx_len, 128), lambda b, p: (b, p)),
            pl.BlockSpec((ctx_len, 256), lambda b, p: (b, p)),
            v,
            pl.BlockSpec((1, 8, 256), lambda b, p: (b, 0, p)),
        ],
        out_shape=[
            jax.ShapeDtypeStruct((t_lat, 256), BF16), jax.ShapeDtypeStruct((t_lat, 256), BF16),
            jax.ShapeDtypeStruct((t_lat, 512), BF16), jax.ShapeDtypeStruct((t_ctx, 256), F32),
            jax.ShapeDtypeStruct((t_ctx, 512), F32), jax.ShapeDtypeStruct((t_lat, 512), BF16),
            jax.ShapeDtypeStruct((nex, 8, 512), F32),
        ],
        scratch_shapes=[pltpu.VMEM((seq, 256), F32), pltpu.VMEM((4, n_chunk, 128, 128), F32),
                        pltpu.VMEM((seq, 128), F32), pltpu.VMEM((seq, 128), F32), pltpu.VMEM((seq, 256), F32)],
        sem=("parallel", "arbitrary"), rider=rider)


def _matmul_tn(a, b, *, bm, bn, bk, chip_major, name, out_dtype=F32, rider=None):
    tk, m = a.shape
    n = b.shape[1]
    slab = n // N_CHIPS
    per_block = bn // slab if chip_major else 1
    bk = max(c for c in range(LANES, min(bk, tk) + 1, LANES) if tk % c == 0)
    nk = tk // bk
    blk = (per_block, bm, slab) if chip_major else (bm, bn)

    def body(a_ref, b_ref, o_ref, acc_ref):
        k = pl.program_id(2)
        if chip_major:
            parts = [_dot_tn(a_ref[...], b_ref[:, s * slab:(s + 1) * slab]) for s in range(per_block)]
        else:
            parts = [_dot_tn(a_ref[...], b_ref[...])]

        @pl.when(k == 0)
        def _():
            for s, part in enumerate(parts):
                if chip_major:
                    acc_ref[s] = part
                else:
                    acc_ref[...] = part

        @pl.when(k > 0)
        def _():
            for s, part in enumerate(parts):
                if chip_major:
                    acc_ref[s] += part
                else:
                    acc_ref[...] += part

        @pl.when(k == nk - 1)
        def _():
            o_ref[...] = acc_ref[...].astype(out_dtype)

    if chip_major:
        out_spec = pl.BlockSpec(blk, lambda i, j, k: (j, i, 0))
        out_shape = jax.ShapeDtypeStruct((N_CHIPS, m, slab), out_dtype)
    else:
        out_spec = pl.BlockSpec(blk, lambda i, j, k: (i, j))
        out_shape = jax.ShapeDtypeStruct((m, n), out_dtype)
    (out,), carried = _hosted_call(
        body, (a, b), name=name, grid=(m // bm, n // bn, nk),
        in_specs=[pl.BlockSpec((bk, bm), lambda i, j, k: (k, i)), pl.BlockSpec((bk, bn), lambda i, j, k: (k, j))],
        out_specs=[out_spec], out_shape=[out_shape], scratch_shapes=[pltpu.VMEM(blk, F32)],
        sem=("parallel", "parallel", "arbitrary"), rider=rider)
    return out if rider is None else (out, carried)


_LATE = ("w_out", "w_ff1", "w_ff2")
_EARLY = ("w_in", "w_uq", "w_ukv")


def _local_step(x, ctx, tgt, modv, lg, g_attn, g_ffn, g_fin, g_ret, g_q, g_kv, w_in, w_uq, w_ukv, late, place=None,
                *, tm=256, tq=256, chunk=256):
    nex, seq, d = x.shape
    ctx_len = ctx.shape[1]
    t_lat = nex * seq
    tm = min(tm, seq)
    x2 = x.reshape(t_lat, d)
    ctx2 = ctx.reshape(nex * ctx_len, d)
    tgt2 = tgt.reshape(t_lat, d)
    tm_fwd = min(2 * tm, seq)
    cos_t, sin_t = _rope_tables(seq, tm_fwd)
    dims = dict(t_lat=t_lat, seq=seq, ctx_len=ctx_len)
    alone = place is None

    (hb, pg, rq, rk, rv, nq, nkv, q, k, v), crossed_a = _pre_fwd(
        x2, ctx2, modv, g_attn, w_in, g_q, g_kv, w_uq, w_ukv, cos_t, sin_t, seq=seq, tm=tm_fwd,
        rider=None if alone else _gather_ici_rider([late[2]]))
    (yret, osum), got = _ret_fwd_pair(
        rq, rk, rv, pg, lg, g_ret, chunk=min(2 * chunk, seq), **dims,
        rider=None if alone else _merge_riders(_gather_d2d_rider(crossed_a), _gather_ici_rider([late[3]])))
    (ymla, lse), got_rest = _mla_fwd(
        q, k, v, tq=tq, **dims,
        rider=None if alone else _merge_riders(_gather_rider([late[0], late[1]], staged=True), _gather_d2d_rider(got[1:])))
    w_out, w_ff1, w_ff2a, w_ff2b = late if alone else (got_rest[0], got_rest[1], got[0], got_rest[2])
    mix, act, du, h2, df, dmo, dmix, dxm, st_post = _post(yret, ymla, x2, tgt2, modv, g_ffn, g_fin, w_out.reshape(d, d),
                                                         w_ff1, w_ff2a, w_ff2b, seq=seq, tm=min(tm, 256))
    kw = dict(bm=1024, bn=1024, bk=2048, out_dtype=BF16)
    g_ff2 = _matmul_tn(act, df, chip_major=False, name="gw_ff2", **kw).reshape(N_CHIPS, D_FF // N_CHIPS, d)
    if alone:
        g_ff1 = _matmul_tn(h2, du, chip_major=True, name="gw_ff1", **kw)
        g_out = _matmul_tn(mix, dmo, chip_major=False, name="gw_out", **kw).reshape(N_CHIPS, d // N_CHIPS, d)
        (dq_m, dkl, dkc, dvl, dvc), _ = _mla_bwd(q, k, v, ymla, lse, dmix, tq=tq, **dims)
        (drq, drk, drv, dkc_r, dvc_r, drg, st_ret), _ = _ret_bwd_pair(rq, rk, rv, pg, osum, dmix, lg, g_ret, chunk=chunk,
                                                                      **dims)
        late_out = [g_out, g_ff1, g_ff2]
    else:
        core, slot = place
        g_ff1, x_ff2 = _matmul_tn(h2, du, chip_major=True, name="gw_ff1", rider=_exchange_rider([g_ff2]), **kw)
        g_out, x_ff1 = _matmul_tn(mix, dmo, chip_major=False, name="gw_out", rider=_exchange_rider([g_ff1]), **kw)
        g_out = g_out.reshape(N_CHIPS, d // N_CHIPS, d)
        p_ff2 = _add_half(g_ff2, x_ff2[0], core, "add_half_w_ff2")
        p_ff1 = _add_half(g_ff1, x_ff1[0], core, "add_half_w_ff1")
        (dq_m, dkl, dkc, dvl, dvc), (l_ff2, l_ff1, x_out) = _mla_bwd(
            q, k, v, ymla, lse, dmix, tq=min(seq, 512), **dims,
            rider=_merge_riders(_scatter_rider([p_ff2, p_ff1]), _exchange_rider([g_out])))
        p_out = _add_half(g_out, x_out, core, "add_half_w_out")
        m_ff2 = _sum_chips(p_ff2, l_ff2, slot, "sum_chips_w_ff2")
        m_ff1 = _sum_chips(p_ff1, l_ff1, slot, "sum_chips_w_ff1")
        (drq, drk, drv, dkc_r, dvc_r, drg, st_ret), (l_out,) = _ret_bwd_pair(
            rq, rk, rv, pg, osum, dmix, lg, g_ret, chunk=chunk, **dims, rider=_scatter_rider([p_out]))
        late_out = [_sum_chips(p_out, l_out, slot, "sum_chips_w_out"), m_ff1, m_ff2]
    (dpb, dqf, dkvf, gx, st_pre), _ = _pre_bwd(
        x2, ctx2, modv, g_attn, pg, drq, drk, dkc_r, drv, dvc_r, drg, dq_m, dkl, dkc, dvl, dvc, dxm, w_in, g_q, g_kv,
        w_uq, w_ukv, cos_t, sin_t, seq=seq, tm=tm)
    g_early = [
        _matmul_tn(dpb, hb, bm=IN_PAD // 2, bn=d, bk=1536, chip_major=False, name="gw_in"),
        _matmul_tn(dqf, nq, bm=HEADS * MLA_HEAD, bn=Q_LORA, bk=1536, chip_major=False, name="gw_uq"),
        _matmul_tn(nkv, dkvf, bm=KV_LORA, bn=HEADS * 256, bk=1536, chip_major=True, name="gw_ukv"),
    ]
    return gx.reshape(nex, seq, d), g_early, late_out, st_post, st_ret, st_pre


_ANY = pl.BlockSpec(memory_space=pl.ANY)
_VMEM = pl.BlockSpec(memory_space=pltpu.VMEM)
_OFFSETS = tuple((dx, dy, dc) for dx in (0, 1) for dy in (0, 1) for dc in (0, 1))[1:]
_CHIP_OFFSETS = ((1, 0), (0, 1), (1, 1))


def _place():
    return lax.axis_index("x"), lax.axis_index("y"), lax.axis_index("c")


def _flip(v, d):
    return 1 - v if d else v


def _gather8_rider(a, in_vmem=True):
    def copies(a_ref, o_ref, send, recv):
        x, y, z = _place()
        me = 4 * x + 2 * y + z
        out = []
        for k, (dx, dy, dc) in enumerate(_OFFSETS):
            peer = (_flip(x, dx), _flip(y, dy), _flip(z, dc))
            landing = o_ref.at[4 * peer[0] + 2 * peer[1] + peer[2]]
            out.append((
                pltpu.make_async_remote_copy(src_ref=a_ref, dst_ref=o_ref.at[me], send_sem=send.at[k],
                                             recv_sem=recv.at[k], device_id=peer, device_id_type=MESH),
                pltpu.make_async_remote_copy(src_ref=a_ref, dst_ref=landing, send_sem=send.at[k],
                                             recv_sem=recv.at[k], device_id=peer, device_id_type=MESH)))
        return me, out

    def start(ins, outs, sems):
        me, cps = copies(ins[0], outs[0], sems[0], sems[1])
        pltpu.make_async_copy(ins[0], outs[0].at[me], sems[2]).start()
        for out_cp, _ in cps:
            out_cp.start()

    def finish(ins, outs, sems):
        me, cps = copies(ins[0], outs[0], sems[0], sems[1])
        for out_cp, in_cp in cps:
            in_cp.wait_recv()
            out_cp.wait_send()
        pltpu.make_async_copy(ins[0], outs[0].at[me], sems[2]).wait()

    spec = [_VMEM] if in_vmem else [_ANY]
    return _Rider([a], [jax.ShapeDtypeStruct((N_DEV,) + a.shape, a.dtype)],
                  [pltpu.SemaphoreType.DMA((7,)), pltpu.SemaphoreType.DMA((7,)), pltpu.SemaphoreType.DMA],
                  start, finish, in_specs=spec, out_specs=spec)


def _merge_riders(*riders):
    ins, outs, sems, in_specs, out_specs, aliases, cuts = [], [], [], [], [], {}, []
    for r in riders:
        cuts.append((len(ins), len(outs), len(sems)))
        aliases.update({len(ins) + i: len(outs) + j for i, j in r.aliases.items()})
        ins += r.ins
        outs += r.out_shapes
        sems += r.sems
        in_specs += r.in_specs
        out_specs += r.out_specs

    def part(r, cut, r_ins, r_outs, r_sems):
        return (r_ins[cut[0]:cut[0] + len(r.ins)], r_outs[cut[1]:cut[1] + len(r.out_shapes)],
                r_sems[cut[2]:cut[2] + len(r.sems)])

    def start(r_ins, r_outs, r_sems):
        for r, cut in zip(riders, cuts):
            r.start(*part(r, cut, r_ins, r_outs, r_sems))

    def finish(r_ins, r_outs, r_sems):
        for r, cut in zip(riders, cuts):
            r.finish(*part(r, cut, r_ins, r_outs, r_sems))

    def middle(r_ins, r_outs, r_sems):
        for r, cut in zip(riders, cuts):
            if r.middle is not None:
                r.middle(*part(r, cut, r_ins, r_outs, r_sems))

    return _Rider(ins, outs, sems, start, finish, aliases=aliases, in_specs=in_specs, out_specs=out_specs,
                  middle=middle if any(r.middle is not None for r in riders) else None)


def _allgather8(a, name):
    return _run_rider(_gather8_rider(a), name)[0]


BF16_TILE_ROWS = 16


def _half(o, slot, which):
    r2 = o.shape[1] // 2
    if r2 % BF16_TILE_ROWS == 0:
        return o.at[slot, pl.ds(which * r2, r2)]
    c2 = o.shape[2] // 2
    assert c2 % LANES == 0
    return o.at[slot, :, pl.ds(which * c2, c2)]


def _gather_send(o_refs, send, recv):
    x, y, z = _place()
    chip = 2 * x + y
    for a, o in enumerate(o_refs):
        r2 = o.shape[1] // 2
        mine = _half(o, chip, z)
        for k, (dx, dy) in enumerate(_CHIP_OFFSETS):
            pltpu.make_async_remote_copy(
                src_ref=mine, dst_ref=mine, send_sem=send.at[a, k], recv_sem=recv.at[a, k],
                device_id=(_flip(x, dx), _flip(y, dy), z), device_id_type=MESH).start()


def _gather_landed(o_refs, send, recv, then=None):
    x, y, z = _place()
    chip = 2 * x + y
    for a, o in enumerate(o_refs):
        for k, (dx, dy) in enumerate(_CHIP_OFFSETS):
            landed = _half(o, 2 * _flip(x, dx) + _flip(y, dy), z)
            pltpu.make_async_remote_copy(
                src_ref=landed, dst_ref=landed, send_sem=send.at[a, k], recv_sem=recv.at[a, k],
                device_id=(_flip(x, dx), _flip(y, dy), z), device_id_type=MESH).wait_recv()
            if then is not None:
                then(a, k, landed)
    for a, o in enumerate(o_refs):
        mine = _half(o, chip, z)
        for k, (dx, dy) in enumerate(_CHIP_OFFSETS):
            pltpu.make_async_remote_copy(
                src_ref=mine, dst_ref=mine, send_sem=send.at[a, k], recv_sem=recv.at[a, k],
                device_id=(_flip(x, dx), _flip(y, dy), z), device_id_type=MESH).wait_send()


def _pass_on(o_refs, fsend, frecv, a, k, landed):
    x, y, z = _place()
    pltpu.make_async_remote_copy(
        src_ref=landed, dst_ref=landed, send_sem=fsend.at[a, k], recv_sem=frecv.at[a, k],
        device_id=(x, y, 1 - z), device_id_type=MESH).start()


def _passed_on(o_refs, fsend, frecv):
    x, y, z = _place()
    for a, o in enumerate(o_refs):
        for k, (dx, dy) in enumerate(_CHIP_OFFSETS):
            other = 2 * _flip(x, dx) + _flip(y, dy)
            got = _half(o, other, 1 - z)
            gave = _half(o, other, z)
            pltpu.make_async_remote_copy(
                src_ref=got, dst_ref=got, send_sem=fsend.at[a, k], recv_sem=frecv.at[a, k],
                device_id=(x, y, 1 - z), device_id_type=MESH).wait_recv()
            pltpu.make_async_remote_copy(
                src_ref=gave, dst_ref=gave, send_sem=fsend.at[a, k], recv_sem=frecv.at[a, k],
                device_id=(x, y, 1 - z), device_id_type=MESH).wait_send()


def _gather_finish(o_refs, send, recv, fsend, frecv):
    _gather_landed(o_refs, send, recv, functools.partial(_pass_on, o_refs, fsend, frecv))
    _passed_on(o_refs, fsend, frecv)


class _Rider:
    def __init__(self, ins, out_shapes, sems, start, finish, aliases=None, in_specs=None, out_specs=None, middle=None):
        self.ins, self.out_shapes, self.sems = list(ins), list(out_shapes), list(sems)
        self.start, self.finish, self.aliases = start, finish, dict(aliases or {})
        self.middle = middle
        self.in_specs = list(in_specs) if in_specs else [_ANY] * len(self.ins)
        self.out_specs = list(out_specs) if out_specs else [_ANY] * len(self.out_shapes)


def _run_rider(rider, name):
    r_in, r_out = len(rider.ins), len(rider.out_shapes)

    def body(*refs):
        ins, outs, sems = refs[:r_in], refs[r_in:r_in + r_out], refs[r_in + r_out:]
        rider.start(ins, outs, sems)
        if rider.middle is not None:
            rider.middle(ins, outs, sems)
        rider.finish(ins, outs, sems)

    return pl.pallas_call(
        body, name=name, in_specs=rider.in_specs, out_specs=rider.out_specs, out_shape=rider.out_shapes,
        input_output_aliases=rider.aliases, scratch_shapes=rider.sems,
    )(*rider.ins)


def _hosted_call(body, args, *, name, grid, in_specs, out_specs, out_shape, scratch_shapes=(), sem, rider=None,
                 prefetch=()):
    scratch_shapes = list(scratch_shapes)
    n_pf, n_in, n_out, n_sc = len(prefetch), len(in_specs), len(out_specs), len(scratch_shapes)
    r_in, r_out = (len(rider.ins), len(rider.out_shapes)) if rider else (0, 0)
    last = tuple(g - 1 for g in grid)

    def hosted(*refs):
        p = 0
        parts = []
        for cnt in (n_pf, n_in, r_in, n_out, r_out, n_sc):
            parts.append(refs[p:p + cnt])
            p += cnt
        pf, ins, r_ins, outs, r_outs, scratch = parts
        sems = refs[p:]
        ids = [pl.program_id(a) for a in range(len(grid))]
        is_first = functools.reduce(jnp.logical_and, [i == 0 for i in ids])
        is_last = functools.reduce(jnp.logical_and, [i == e for i, e in zip(ids, last)])

        @pl.when(is_first)
        def _():
            rider.start(r_ins, r_outs, sems)

        if rider.middle is not None:
            linear = functools.reduce(lambda acc, ig: acc * ig[1] + ig[0], zip(ids, grid), 0)

            @pl.when(linear == math.prod(grid) * 3 // 4)
            def _():
                rider.middle(r_ins, r_outs, sems)

        body(*pf, *ins, *outs, *scratch)

        @pl.when(is_last)
        def _():
            rider.finish(r_ins, r_outs, sems)

    if rider is None:
        kern, all_in, all_out, shapes, scratch, aliases, extra = body, list(in_specs), list(out_specs), list(out_shape), \
            scratch_shapes, {}, []
    else:
        kern, all_in, all_out = hosted, list(in_specs) + rider.in_specs, list(out_specs) + rider.out_specs
        shapes, scratch, extra = list(out_shape) + rider.out_shapes, scratch_shapes + rider.sems, rider.ins
        aliases = {n_pf + n_in + i: n_out + j for i, j in rider.aliases.items()}
        sem = ("arbitrary",) * len(grid)
    if prefetch:
        spec = dict(grid_spec=pltpu.PrefetchScalarGridSpec(
            num_scalar_prefetch=n_pf, grid=grid, in_specs=all_in, out_specs=all_out, scratch_shapes=scratch))
    else:
        spec = dict(grid=grid, in_specs=all_in, out_specs=all_out, scratch_shapes=scratch)
    res = pl.pallas_call(kern, name=name, out_shape=shapes, input_output_aliases=aliases,
                         compiler_params=_params(sem, VMEM_LIMIT), **spec)(*prefetch, *args, *extra)
    return list(res[:n_out]), list(res[n_out:])


def _gather_rider(ws, staged=False):
    n = len(ws)
    shapes = [jax.ShapeDtypeStruct(w.shape, w.dtype) for w in ws]
    sems = [pltpu.SemaphoreType.DMA((n, 3))] * 4
    aliases = {a: a for a in range(n)}

    def start(ins, outs, s):
        _gather_send(outs, s[0], s[1])

    if not staged:
        return _Rider(ws, shapes, sems, start, lambda ins, outs, s: _gather_finish(outs, *s), aliases=aliases)
    return _Rider(
        ws, shapes, sems, start, lambda ins, outs, s: _passed_on(outs, s[2], s[3]), aliases=aliases,
        middle=lambda ins, outs, s: _gather_landed(outs, s[0], s[1], functools.partial(_pass_on, outs, s[2], s[3])))


def _gather_ici_rider(ws):
    n = len(ws)
    return _Rider(
        ws, [jax.ShapeDtypeStruct(w.shape, w.dtype) for w in ws], [pltpu.SemaphoreType.DMA((n, 3))] * 2,
        lambda ins, outs, sems: _gather_send(outs, sems[0], sems[1]),
        lambda ins, outs, sems: _gather_landed(outs, sems[0], sems[1]),
        aliases={a: a for a in range(n)})


def _gather_d2d_rider(ws):
    n = len(ws)

    def start(ins, outs, sems):
        x, y, z = _place()
        for a, o in enumerate(outs):
            for k, (dx, dy) in enumerate(_CHIP_OFFSETS):
                _pass_on(outs, sems[0], sems[1], a, k, _half(o, 2 * _flip(x, dx) + _flip(y, dy), z))

    return _Rider(
        ws, [jax.ShapeDtypeStruct(w.shape, w.dtype) for w in ws], [pltpu.SemaphoreType.DMA((n, 3))] * 2,
        start, lambda ins, outs, sems: _passed_on(outs, sems[0], sems[1]), aliases={a: a for a in range(n)})


def _copies_rider(ins, out_shapes, sem_shape, make):
    def start(r_ins, r_outs, sems):
        for cp in make(r_ins, r_outs, sems[0], sems[1]):
            cp.start()

    def finish(r_ins, r_outs, sems):
        for cp in make(r_ins, r_outs, sems[0], sems[1]):
            cp.wait()

    return _Rider(ins, out_shapes, [pltpu.SemaphoreType.DMA(sem_shape)] * 2, start, finish)


def _exchange_rider(gs):
    def make(g_refs, r_refs, send, recv):
        x, y, z = _place()
        return [pltpu.make_async_remote_copy(
            src_ref=g.at[:, pl.ds((1 - z) * (g.shape[1] // 2), g.shape[1] // 2)], dst_ref=r, send_sem=send.at[a],
            recv_sem=recv.at[a], device_id=(x, y, 1 - z), device_id_type=MESH)
            for a, (g, r) in enumerate(zip(g_refs, r_refs))]

    shapes = [jax.ShapeDtypeStruct((g.shape[0], g.shape[1] // 2, g.shape[2]), g.dtype) for g in gs]
    return _copies_rider(gs, shapes, (len(gs),), make)


def _add_half(g, recv, core, name):
    s, r, c = g.shape
    r2 = r // 2
    rb = r2
    for cand in (256, 128, 64):
        if r2 % cand == 0:
            rb = cand
            break
    g4 = g.reshape(s, 2, r2, c)

    def body(core_ref, g_ref, r_ref, o_ref):
        o_ref[...] = (g_ref[...].astype(F32) + r_ref[...].astype(F32)).astype(BF16)

    return pl.pallas_call(
        body, name=name,
        grid_spec=pltpu.PrefetchScalarGridSpec(
            num_scalar_prefetch=1, grid=(s, r2 // rb),
            in_specs=[pl.BlockSpec((None, None, rb, c), lambda i, j, cr: (i, cr[0], j, 0)),
                      pl.BlockSpec((None, rb, c), lambda i, j, cr: (i, j, 0))],
            out_specs=pl.BlockSpec((None, rb, c), lambda i, j, cr: (i, j, 0))),
        out_shape=jax.ShapeDtypeStruct((s, r2, c), BF16),
        compiler_params=_params(("parallel", "parallel")),
    )(core, g4, recv)


def _scatter_rider(ps):
    def make(p_refs, o_refs, send, recv):
        x, y, z = _place()
        copies = []
        for a, (p, o) in enumerate(zip(p_refs, o_refs)):
            for k, (dx, dy) in enumerate(_CHIP_OFFSETS):
                other = 2 * _flip(x, dx) + _flip(y, dy)
                copies.append(pltpu.make_async_remote_copy(
                    src_ref=p.at[other], dst_ref=o.at[k], send_sem=send.at[a, k], recv_sem=recv.at[a, k],
                    device_id=(_flip(x, dx), _flip(y, dy), z), device_id_type=MESH))
        return copies

    shapes = [jax.ShapeDtypeStruct((3,) + p.shape[1:], p.dtype) for p in ps]
    return _copies_rider(ps, shapes, (len(ps), 3), make)


def _sum_chips(p, landed, chip, name):
    _, r2, c = p.shape
    rb = r2
    for cand in (256, 128, 64):
        if r2 % cand == 0:
            rb = cand
            break

    def body(s_ref, p_ref, l_ref, o_ref):
        acc = p_ref[...].astype(F32)
        for k in range(3):
            acc = acc + l_ref[k].astype(F32)
        o_ref[...] = acc

    return pl.pallas_call(
        body, name=name,
        grid_spec=pltpu.PrefetchScalarGridSpec(
            num_scalar_prefetch=1, grid=(r2 // rb,),
            in_specs=[pl.BlockSpec((None, rb, c), lambda i, s: (s[0], i, 0)),
                      pl.BlockSpec((3, rb, c), lambda i, s: (0, i, 0))],
            out_specs=pl.BlockSpec((rb, c), lambda i, s: (i, 0))),
        out_shape=jax.ShapeDtypeStruct((r2, c), F32),
        compiler_params=_params(("parallel",)),
    )(chip, p, landed)


def _swap_rider(hs):
    def make(h_refs, o_refs, send, recv):
        x, y, z = _place()
        return [pltpu.make_async_remote_copy(
            src_ref=h, dst_ref=o, send_sem=send.at[a], recv_sem=recv.at[a], device_id=(x, y, 1 - z),
            device_id_type=MESH) for a, (h, o) in enumerate(zip(h_refs, o_refs))]

    return _copies_rider(hs, [jax.ShapeDtypeStruct(h.shape, h.dtype) for h in hs], (len(hs),), make)


def _reduce_scatter_vmem(gs, rows, rider, name):
    n = len(gs)
    r_in, r_out = len(rider.ins), len(rider.out_shapes)
    halves = [(r // 2, g.shape[-1]) for g, (r, _) in zip(gs, rows)]
    piece_cols = 2 * LANES
    pieces = [(a, slice(c0, min(c0 + piece_cols, h[1]))) for a, h in enumerate(halves) for c0 in range(0, h[1], piece_cols)]
    n_p = len(pieces)

    def body(*refs):
        p = 0
        parts = []
        for cnt in (n, r_in, n, n, r_out, n, n, n, 6):
            parts.append(refs[p:p + cnt])
            p += cnt
        g_refs, r_ins, mine, theirs, r_outs, recv, part, land, sems = parts
        r_sems = refs[p:]
        xs, xr, ss, sr, ws, wr = sems
        x, y, z = _place()
        chip = 2 * x + y
        sib = (x, y, 1 - z)
        rider.start(r_ins, r_outs, r_sems)

        def half_of(a, s, which):
            r2 = halves[a][0]
            if len(g_refs[a].shape) == 3:
                return g_refs[a].at[s, pl.ds(pl.multiple_of(which * r2, 8), r2)]
            return g_refs[a].at[pl.ds(pl.multiple_of(s * rows[a][1] + which * r2, 8), r2)]

        def exchange(i):
            a, cols = pieces[i]
            return [pltpu.make_async_remote_copy(
                src_ref=half_of(a, s, 1 - z).at[:, cols], dst_ref=recv[a].at[s, :, cols], send_sem=xs.at[i, s],
                recv_sem=xr.at[i, s], device_id=sib, device_id_type=MESH) for s in range(N_CHIPS)]

        def scatter(i):
            a, cols = pieces[i]
            return [pltpu.make_async_remote_copy(
                src_ref=part[a].at[2 * _flip(x, dx) + _flip(y, dy), :, cols], dst_ref=land[a].at[k, :, cols],
                send_sem=ss.at[i, k], recv_sem=sr.at[i, k], device_id=(_flip(x, dx), _flip(y, dy), z),
                device_id_type=MESH) for k, (dx, dy) in enumerate(_CHIP_OFFSETS)]

        def swap(i):
            a, cols = pieces[i]
            return pltpu.make_async_remote_copy(
                src_ref=mine[a].at[:, cols], dst_ref=theirs[a].at[:, cols], send_sem=ws.at[i], recv_sem=wr.at[i],
                device_id=sib, device_id_type=MESH)

        for i in range(len(pieces)):
            for cp in exchange(i):
                cp.start()
        for i, (a, cols) in enumerate(pieces):
            for cp in exchange(i):
                cp.wait()
            for s in range(N_CHIPS):
                part[a][s, :, cols] = (half_of(a, s, z)[:, cols] + recv[a][s, :, cols]).astype(BF16)
            for cp in scatter(i):
                cp.start()
        for i, (a, cols) in enumerate(pieces):
            for cp in scatter(i):
                cp.wait()
            acc = part[a][chip, :, cols].astype(F32)
            for k in range(3):
                acc = acc + land[a][k, :, cols].astype(F32)
            mine[a][:, cols] = acc
            swap(i).start()
        for i in range(len(pieces)):
            swap(i).wait()
        rider.finish(r_ins, r_outs, r_sems)

    half_shapes = [jax.ShapeDtypeStruct(h, F32) for h in halves]
    res = pl.pallas_call(
        body, name=name, in_specs=[_VMEM] * n + rider.in_specs, out_specs=[_VMEM] * (2 * n) + rider.out_specs,
        out_shape=half_shapes + half_shapes + rider.out_shapes,
        scratch_shapes=[pltpu.VMEM((N_CHIPS,) + h, F32) for h in halves] + [pltpu.VMEM((N_CHIPS,) + h, BF16) for h in halves]
        + [pltpu.VMEM((3,) + h, BF16) for h in halves]
        + [pltpu.SemaphoreType.DMA((n_p, N_CHIPS))] * 2 + [pltpu.SemaphoreType.DMA((n_p, 3))] * 2
        + [pltpu.SemaphoreType.DMA((n_p,))] * 2 + rider.sems,
        input_output_aliases={n + i: 2 * n + j for i, j in rider.aliases.items()},
        compiler_params=_params(None, VMEM_LIMIT),
    )(*gs, *rider.ins)
    return list(res[:n]), list(res[n:2 * n]), list(res[2 * n:])


SMALL_ROWS = 32
PACK_ROWS = 16


def _pack_small(st_post, st_ret, st_pre):
    d = st_post.shape[2]

    def body(po_ref, re_ref, pr_ref, o_ref):
        o_ref[...] = jnp.zeros(o_ref.shape, F32)
        o_ref[0:1, :] = pr_ref[0, 2:3, :] + pr_ref[1, 2:3, :] + pr_ref[2, 2:3, :]
        o_ref[1:2, :] = po_ref[0, 4:5, :] + po_ref[1, 4:5, :]
        o_ref[2:3, :] = po_ref[0, 5:6, :] + po_ref[1, 5:6, :]
        o_ref[3:4, 0:512] = re_ref[0, 0:1, :] + re_ref[1, 0:1, :]
        o_ref[4:5, :] = pr_ref[0, 3:4, :] + pr_ref[1, 3:4, :] + pr_ref[2, 3:4, :]
        o_ref[5:6, :] = pr_ref[0, 4:5, :] + pr_ref[1, 4:5, :] + pr_ref[2, 4:5, :]
        lane = lax.broadcasted_iota(jnp.int32, (1, LANES), 1)
        for row, src in ((6, 1), (10, 2)):
            acc = jnp.zeros((1, LANES), F32)
            for hd in range(HEADS):
                grp = re_ref[0, src:src + 1, hd * LANES:(hd + 1) * LANES] + re_ref[1, src:src + 1, hd * LANES:(hd + 1) * LANES]
                acc = acc + jnp.where(lane == hd, grp, 0.0)
            o_ref[row:row + 1, 0:LANES] = acc
        o_ref[7:8, :] = po_ref[0, 6:7, :] + po_ref[1, 6:7, :]
        o_ref[8:9, :] = pr_ref[2, 0:1, :]
        o_ref[9:10, :] = pr_ref[2, 1:2, :]
        for e in range(2):
            b = 12 + 6 * e
            o_ref[b:b + 1, :] = pr_ref[e, 0:1, :]
            o_ref[b + 1:b + 2, :] = pr_ref[e, 1:2, :]
            o_ref[b + 2:b + 3, :] = po_ref[e, 3:4, :]
            o_ref[b + 3:b + 4, :] = po_ref[e, 0:1, :]
            o_ref[b + 4:b + 5, :] = po_ref[e, 1:2, :]
            o_ref[b + 5:b + 6, :] = po_ref[e, 2:3, :]

    return pl.pallas_call(body, name="pack_small", out_shape=jax.ShapeDtypeStruct((SMALL_ROWS, d), F32))(st_post, st_ret, st_pre)


def _small_reduce(gathered):
    d = gathered.shape[2]

    def body(g_ref, o_ref):
        tot = g_ref[0, 0:PACK_ROWS, :]
        for dev in range(1, N_DEV):
            tot = tot + g_ref[dev, 0:PACK_ROWS, :]
        o_ref[0:PACK_ROWS, :] = tot
        for j in range(6):
            acc = g_ref[0, 12 + j:13 + j, :] + g_ref[0, 18 + j:19 + j, :]
            for dev in range(1, N_DEV):
                acc = acc + g_ref[dev, 12 + j:13 + j, :] + g_ref[dev, 18 + j:19 + j, :]
            if j < 2:
                acc = acc + o_ref[8 + j:9 + j, :]
            o_ref[PACK_ROWS + j:PACK_ROWS + j + 1, :] = acc
        o_ref[PACK_ROWS + 6:PACK_ROWS + 8, :] = jnp.zeros((2, d), F32)

    return pl.pallas_call(body, name="small_reduce", out_shape=jax.ShapeDtypeStruct((PACK_ROWS + 8, d), F32))(gathered)


_SMALL = (("g_attn", 0, 1024), ("g_ffn", 1, 1024), ("g_final", 2, 1024), ("g_ret", 3, 512), ("g_q_lora", 4, 384),
          ("g_kv_lora", 5, 256), ("ret_decay_fwd", 6, HEADS), ("ret_decay_bwd", 10, HEADS))
_SMALL_NAMES = tuple(s[0] for s in _SMALL) + ("c_ctx", "b_ada")


def _small_final(tot, dcc, sg8, ws, ms, vs):
    d = tot.shape[1]
    n = len(_SMALL_NAMES)

    def body(*refs):
        t_ref, dcc_ref, sg_ref = refs[0:3]
        w_refs, m_refs, v_refs = refs[3:3 + n], refs[3 + n:3 + 2 * n], refs[3 + 2 * n:3 + 3 * n]
        outs = refs[3 + 3 * n:]
        g_refs, d_refs, mo_refs, vo_refs = outs[0:n], outs[n:2 * n], outs[2 * n:3 * n], outs[3 * n:4 * n]
        l_ref = outs[4 * n]

        def update(i, g, sl=None):
            pick = (lambda r: r[...]) if sl is None else (lambda r: r[:, sl])
            dl, mn, vn = _adam_math(pick(w_refs[i]), g, pick(m_refs[i]), pick(v_refs[i]))
            if sl is None:
                g_refs[i][...], d_refs[i][...], mo_refs[i][...], vo_refs[i][...] = g, dl, mn, vn
            else:
                g_refs[i][:, sl], d_refs[i][:, sl], mo_refs[i][:, sl], vo_refs[i][:, sl] = g, dl, mn, vn

        for i, (name, row, width) in enumerate(_SMALL):
            g = t_ref[row:row + 1, 0:width]
            if name == "ret_decay_fwd":
                g = g * sg_ref[0:1, 0:width]
            elif name == "ret_decay_bwd":
                g = g * sg_ref[1:2, 0:width]
            update(i, g)
        i_cc, i_b = n - 2, n - 1
        cc = w_refs[i_cc][...]
        s = 1.0 / (1.0 + jnp.exp(-cc))
        dsilu = dcc_ref[0, 0:1, :] + dcc_ref[2, 0:1, :] + dcc_ref[4, 0:1, :] + dcc_ref[6, 0:1, :]
        update(i_cc, dsilu * (s * (1.0 + cc * (1.0 - s))))
        for j in range(6):
            update(i_b, t_ref[PACK_ROWS + j:PACK_ROWS + j + 1, :], pl.ds(j * d, d))
        l_ref[...] = jnp.broadcast_to((0.5 / d) * jnp.sum(t_ref[7:8, :], keepdims=True), l_ref.shape)

    shapes = [jax.ShapeDtypeStruct(a.shape, F32) for a in ws]
    outs = pl.pallas_call(
        body, name="small_final", out_shape=shapes * 4 + [jax.ShapeDtypeStruct((8, LANES), F32)],
    )(tot, dcc, sg8, *ws, *ms, *vs)
    return outs[0:n], outs[n:2 * n], outs[2 * n:3 * n], outs[3 * n:4 * n], outs[4 * n]


_WEIGHTS = ("c_ctx", "w_ada", "b_ada", "g_attn", "g_ffn", "w_in", "ret_decay_fwd", "ret_decay_bwd", "g_ret", "g_q_lora",
            "w_uq", "g_kv_lora", "w_ukv", "w_out", "w_ff1", "w_ff2", "g_final")
_BIG = ("w_in", "w_uq", "w_ukv", "w_out", "w_ff1", "w_ff2")
_TRANSPOSED = ("w_in", "w_uq")


def kernel(x, c, ctx, c_ctx, w_ada, b_ada, g_attn, g_ffn, w_in, ret_decay_fwd, ret_decay_bwd, g_ret, g_q_lora, w_uq, g_kv_lora, w_ukv, w_out, w_ff1, w_ff2, g_final, loss_target, m_c_ctx, m_w_ada, m_b_ada, m_g_attn, m_g_ffn, m_w_in, m_ret_decay_fwd, m_ret_decay_bwd, m_g_ret, m_g_q_lora, m_w_uq, m_g_kv_lora, m_w_ukv, m_w_out, m_w_ff1, m_w_ff2, m_g_final, v_c_ctx, v_w_ada, v_b_ada, v_g_attn, v_g_ffn, v_w_in, v_ret_decay_fwd, v_ret_decay_bwd, v_g_ret, v_g_q_lora, v_w_uq, v_g_kv_lora, v_w_ukv, v_w_out, v_w_ff1, v_w_ff2, v_g_final):
    w = dict(c_ctx=c_ctx, w_ada=w_ada, b_ada=b_ada, g_attn=g_attn, g_ffn=g_ffn, w_in=w_in, ret_decay_fwd=ret_decay_fwd,
             ret_decay_bwd=ret_decay_bwd, g_ret=g_ret, g_q_lora=g_q_lora, w_uq=w_uq, g_kv_lora=g_kv_lora, w_ukv=w_ukv,
             w_out=w_out, w_ff1=w_ff1, w_ff2=w_ff2, g_final=g_final)
    m = dict(c_ctx=m_c_ctx, w_ada=m_w_ada, b_ada=m_b_ada, g_attn=m_g_attn, g_ffn=m_g_ffn, w_in=m_w_in,
             ret_decay_fwd=m_ret_decay_fwd, ret_decay_bwd=m_ret_decay_bwd, g_ret=m_g_ret, g_q_lora=m_g_q_lora, w_uq=m_w_uq,
             g_kv_lora=m_g_kv_lora, w_ukv=m_w_ukv, w_out=m_w_out, w_ff1=m_w_ff1, w_ff2=m_w_ff2, g_final=m_g_final)
    v = dict(c_ctx=v_c_ctx, w_ada=v_w_ada, b_ada=v_b_ada, g_attn=v_g_attn, g_ffn=v_g_ffn, w_in=v_w_in,
             ret_decay_fwd=v_ret_decay_fwd, ret_decay_bwd=v_ret_decay_bwd, g_ret=v_g_ret, g_q_lora=v_g_q_lora, w_uq=v_w_uq,
             g_kv_lora=v_g_kv_lora, w_ukv=v_w_ukv, w_out=v_w_out, w_ff1=v_w_ff1, w_ff2=v_w_ff2, g_final=v_g_final)
    xi, yi, ci = lax.axis_index("x"), lax.axis_index("y"), lax.axis_index("c")
    chip = 2 * xi + yi
    dev = 2 * chip + ci
    nex, seq, d = x.shape
    n_ada = w_ada.shape[2]

    dec = jnp.zeros((8, LANES), F32).at[0, :HEADS].set(ret_decay_fwd[0]).at[1, :HEADS].set(ret_decay_bwd[0])
    lg8, sg8 = _decay_prep(dec)
    lg = lg8[:2, :HEADS]

    def shard_of(t, k):
        return t[k][0].T if k in _TRANSPOSED else t[k][0]

    shard = {k: shard_of(w, k) for k in _BIG}
    head_rows = MLA_NOPE + MLA_ROPE
    shard["w_uq"] = jnp.pad(shard["w_uq"], ((0, MLA_HEAD - head_rows), (0, 0)))
    slot = chip.reshape(1).astype(jnp.int32)
    core = ci.reshape(1).astype(jnp.int32)
    slots = {k: _cast_into_slot(shard[k], slot, "cast_" + k) for k in _EARLY}
    half_ff = shard["w_ff2"].shape[0] // 2
    late_pieces = [(shard["w_out"], 0, shard["w_out"].shape[0]), (shard["w_ff1"], 0, shard["w_ff1"].shape[0]),
                   (shard["w_ff2"], 0, half_ff), (shard["w_ff2"], half_ff, half_ff)]
    late_slots, (w_in_f, w_uq_k, w_ukv_k, c8) = _cast_into_slots(
        late_pieces, slot, "cast_late",
        rider=_merge_riders(_gather_rider([slots[k] for k in _EARLY]),
                            _gather8_rider(jnp.pad(c, ((0, 8 - nex), (0, 0))), in_vmem=False)))

    a_in = jnp.concatenate([c8[:, :nex].reshape(N_DEV * nex, d), c_ctx.reshape(1, d), jnp.zeros((7, d), F32)], axis=0)
    b_sh = lax.dynamic_slice(b_ada, (0, chip * n_ada), (1, n_ada))
    mod_sh = _mod_fwd(a_in, w_ada[0], b_sh)
    mod8 = _allgather8(mod_sh, "ag_mod")
    w_in_k = jnp.pad(w_in_f.reshape(IN_COLS, d), ((0, IN_PAD - IN_COLS), (0, 0)))
    mod_all = mod8[0::2].transpose(1, 0, 2).reshape(a_in.shape[0], N_CHIPS * n_ada)
    mod_me = lax.dynamic_slice(mod_all, (nex * dev, 0), (nex, N_CHIPS * n_ada)).reshape(nex, 6, d)
    mod_c = mod_all[N_DEV * nex].reshape(1, 6, d)
    modv = jnp.pad(jnp.concatenate([mod_me, mod_c], axis=0), ((0, 0), (0, 2), (0, 0)))

    gx, g_early, late, st_post, st_ret, st_pre = _local_step(
        x, ctx, loss_target, modv, lg, g_attn, g_ffn, g_final.reshape(1, d), g_ret, g_q_lora, g_kv_lora,
        w_in_k, w_uq_k, w_ukv_k, late_slots, (core, slot))

    mine, theirs, (*late_theirs, gathered) = _reduce_scatter_vmem(
        g_early, [(IN_COLS // N_CHIPS, IN_COLS // N_CHIPS), (head_rows, MLA_HEAD), (KV_LORA, KV_LORA)],
        _merge_riders(_swap_rider(late), _gather8_rider(_pack_small(st_post, st_ret, st_pre))), "rs_early")
    tot = _small_reduce(gathered)
    dm = jnp.concatenate([
        gathered[:, 12:24].reshape(N_DEV * nex, 6 * d),
        jnp.concatenate([tot[8:10].reshape(1, 2 * d), jnp.zeros((1, 4 * d), F32)], axis=1),
        jnp.zeros((7, 6 * d), F32)], axis=0)
    dm_sh = lax.dynamic_slice(dm, (0, chip * n_ada), (dm.shape[0], n_ada))
    g_ada, da = _mod_bwd(a_in, dm_sh, w_ada[0])
    dcc = _allgather8(da[N_DEV * nex:], "ag_dcc")
    halves = dict(zip(_EARLY, zip(mine, theirs)))
    halves.update(zip(_LATE, zip(late, late_theirs)))
    grad, delta, new_m, new_v = {}, {}, {}, {}
    for k in _BIG:
        a, b = halves[k]
        res = _adamw_halves(shard_of(w, k), a, b, shard_of(m, k), shard_of(v, k), core, "adamw_" + k)
        grad[k], delta[k], new_m[k], new_v[k] = [(o.T if k in _TRANSPOSED else o).reshape(w[k].shape) for o in res]

    shp = w_ada.shape
    outs, _ = _adamw(w_ada[0], g_ada, m["w_ada"][0], v["w_ada"][0], "adamw_w_ada")
    grad["w_ada"] = g_ada.reshape(shp)
    delta["w_ada"], new_m["w_ada"], new_v["w_ada"] = [o.reshape(shp) for o in outs]
    rows = [{k: t[k].reshape(1, -1) for k in _SMALL_NAMES} for t in (w, m, v)]
    small = _small_final(tot, dcc, sg8, *[[t[k] for k in _SMALL_NAMES] for t in rows])
    for res, outs in zip((grad, delta, new_m, new_v), small[:4]):
        for k, o in zip(_SMALL_NAMES, outs):
            res[k] = o.reshape(w[k].shape)
    return (small[4][0, 0], gx, *[grad[k] for k in _WEIGHTS], *[delta[k] for k in _WEIGHTS],
            *[new_m[k] for k in _WEIGHTS], *[new_v[k] for k in _WEIGHTS])
```

```python
import functools
import math

import jax
import jax.numpy as jnp
from jax import lax
from jax.experimental import pallas as pl
from jax.experimental.pallas import tpu as pltpu

F32 = jnp.float32
BF16 = jnp.bfloat16
MESH = pl.DeviceIdType.MESH

EPS = 1e-6
D_MODEL = 1024
D_FF = 4096
HEADS = 4
RET_DK = 64
RET_DV = 128
MLA_NOPE = 128
MLA_ROPE = 64
MLA_HEAD = 256
Q_LORA = 384
KV_LORA = 256
GRID_W = 64
ROPE_BASE = 10000.0
IN_COLS = 2240
IN_PAD = 2304
PG_COLS = 1152
N_CHIPS = 4
N_DEV = 8
LANES = 128
ADAM_LR = 0.001
ADAM_B1 = 0.9
ADAM_B2 = 0.999
ADAM_EPS = 1e-08
ADAM_WD = 0.01
ADAM_STEP = 10
VMEM_LIMIT = 56 * 1024 * 1024


def _dot(a, b):
    return jnp.dot(a, b, preferred_element_type=F32)


def _dot_nt(a, b):
    return lax.dot_general(a, b, (((1,), (1,)), ((), ())), preferred_element_type=F32)


def _dot_tn(a, b):
    return lax.dot_general(a, b, (((0,), (0,)), ((), ())), preferred_element_type=F32)


def _params(sem=None, vmem=None):
    return pltpu.CompilerParams(dimension_semantics=sem, vmem_limit_bytes=vmem)


def _full(shape):
    n = len(shape)
    return pl.BlockSpec(shape, lambda *_: (0,) * n)


def _once(shape):
    n = len(shape)
    return pl.BlockSpec(shape, lambda *_: (0,) * n, pipeline_mode=pl.Buffered(1))


def _rope(x, cos, sin):
    w = x.shape[-1]
    lo = (lax.broadcasted_iota(jnp.int32, (1, w), 1) % 64) < 32
    swapped = jnp.where(lo, pltpu.roll(x, w - 32, 1), pltpu.roll(x, 32, 1))
    return x * cos + swapped * sin


def _rope_t(g, cos, sin):
    w = g.shape[-1]
    lo = (lax.broadcasted_iota(jnp.int32, (1, w), 1) % 64) < 32
    t = g * sin
    swapped = jnp.where(lo, pltpu.roll(t, w - 32, 1), pltpu.roll(t, 32, 1))
    return g * cos + swapped


def _rope_tables(seq, tm):
    rows = seq // GRID_W
    row = jnp.repeat(jnp.arange(rows, dtype=F32), GRID_W)
    col = jnp.tile(jnp.arange(GRID_W, dtype=F32), rows)
    n_freq = RET_DK // 4
    freq = ROPE_BASE ** (-jnp.arange(n_freq, dtype=F32) / n_freq)
    ang = jnp.concatenate([row[:, None] * freq, col[:, None] * freq], axis=-1)
    cos, sin = jnp.cos(ang), jnp.sin(ang)
    cos_t = jnp.tile(jnp.concatenate([cos, cos], -1), (1, HEADS))
    sin_t = jnp.tile(jnp.concatenate([-sin, sin], -1), (1, HEADS))
    cos_t = jnp.concatenate([cos_t, jnp.ones((tm, 4 * RET_DK), F32)], 0)
    sin_t = jnp.concatenate([sin_t, jnp.zeros((tm, 4 * RET_DK), F32)], 0)
    return cos_t, sin_t


def _adam_math(w, g, m, v):
    mn = ADAM_B1 * m + (1.0 - ADAM_B1) * g
    vn = ADAM_B2 * v + (1.0 - ADAM_B2) * (g * g)
    m_hat = mn / (1.0 - ADAM_B1 ** ADAM_STEP)
    v_hat = vn / (1.0 - ADAM_B2 ** ADAM_STEP)
    return -ADAM_LR * (m_hat / (jnp.sqrt(v_hat) + ADAM_EPS) + ADAM_WD * w), mn, vn


def _cast_into_slots(pieces, slot, name, rider=None):
    c = pieces[0][0].shape[1]
    rb = max(b for b in range(16, 257, 16) if all(cnt % b == 0 and st % b == 0 for _, st, cnt in pieces))
    nbs = [cnt // rb for _, _, cnt in pieces]
    starts = [sum(nbs[:s]) for s in range(len(pieces))]

    def body(s_ref, *refs):
        i = pl.program_id(0)
        for s in range(len(pieces)):
            @pl.when(jnp.logical_and(i >= starts[s], i < starts[s] + nbs[s]))
            def _():
                refs[len(pieces) + s][...] = refs[s][...].astype(BF16)

    in_specs, out_specs = [], []
    for (_, first_row, _), nb, st in zip(pieces, nbs, starts):
        in_specs.append(pl.BlockSpec((rb, c), lambda i, s, nb=nb, st=st, f=first_row // rb: (f + jnp.clip(i - st, 0, nb - 1), 0)))
        out_specs.append(pl.BlockSpec((None, rb, c), lambda i, s, nb=nb, st=st: (s[0], jnp.clip(i - st, 0, nb - 1), 0)))
    return _hosted_call(
        body, [w for w, _, _ in pieces], name=name, grid=(sum(nbs),), prefetch=(slot,), in_specs=in_specs,
        out_specs=out_specs, out_shape=[jax.ShapeDtypeStruct((N_CHIPS, cnt, c), BF16) for _, _, cnt in pieces],
        sem=("arbitrary",), rider=rider)


def _cast_into_slot(w, slot, name):
    return _cast_into_slots([(w, 0, w.shape[0])], slot, name)[0][0]


def _adamw_halves(w, mine, theirs, m, v, core, name):
    r, c = w.shape
    r2 = r // 2
    rb = max(b for b in range(8, r2 + 1, 8) if r2 % b == 0 and b * c * 4 <= (1 << 21))
    nbh = r2 // rb

    def body(z_ref, w_ref, a_ref, b_ref, m_ref, v_ref, g_ref, d_ref, mo_ref, vo_ref):
        here = (pl.program_id(0) // nbh) == z_ref[0]
        gg = jnp.where(here, a_ref[...], b_ref[...])
        g_ref[...] = gg
        d_ref[...], mo_ref[...], vo_ref[...] = _adam_math(w_ref[...], gg, m_ref[...], v_ref[...])

    spec = pl.BlockSpec((rb, c), lambda i, z: (i, 0))
    a_spec = pl.BlockSpec((rb, c), lambda i, z: (jnp.clip(i - z[0] * nbh, 0, nbh - 1), 0))
    b_spec = pl.BlockSpec((rb, c), lambda i, z: (jnp.clip(i - (1 - z[0]) * nbh, 0, nbh - 1), 0))
    shp = jax.ShapeDtypeStruct((r, c), F32)
    return pl.pallas_call(
        body, name=name,
        grid_spec=pltpu.PrefetchScalarGridSpec(
            num_scalar_prefetch=1, grid=(r // rb,), in_specs=[spec, a_spec, b_spec, spec, spec], out_specs=[spec] * 4),
        out_shape=[shp] * 4,
        compiler_params=_params(("parallel",)),
    )(core, w, mine, theirs, m, v)


def _adamw(w, g, m, v, name, rider=None):
    r, c = w.shape
    rb = r
    for cand in (512, 256, 128, 64, 32, 16, 8):
        if r % cand == 0 and cand * c * 4 <= (1 << 21):
            rb = cand
            break
    if r * c * 4 <= (1 << 21):
        rb = r

    def body(w_ref, g_ref, m_ref, v_ref, d_ref, mo_ref, vo_ref):
        d_ref[...], mo_ref[...], vo_ref[...] = _adam_math(w_ref[...], g_ref[...], m_ref[...], v_ref[...])

    spec = pl.BlockSpec((rb, c), lambda i: (i, 0))
    shp = jax.ShapeDtypeStruct((r, c), F32)
    return _hosted_call(
        body, (w, g, m, v), name=name, grid=(r // rb,), in_specs=[spec] * 4, out_specs=[spec] * 3, out_shape=[shp] * 3,
        sem=("parallel",), rider=rider)


def _decay_prep(dec):
    def body(d_ref, lg_ref, sg_ref):
        d = d_ref[...]
        lg_ref[...] = jnp.minimum(d, 0.0) - jnp.log(1.0 + jnp.exp(-jnp.abs(d)))
        sg_ref[...] = 1.0 / (1.0 + jnp.exp(d))

    shp = jax.ShapeDtypeStruct(dec.shape, F32)
    return pl.pallas_call(body, name="decay_prep", out_shape=[shp, shp])(dec)


def _mod_fwd(a_in, w_ada, b_sh):
    rows, d = a_in.shape
    n = w_ada.shape[1]
    bn = 512

    def body(a_ref, w_ref, b_ref, o_ref):
        a = a_ref[...]
        s = (a / (1.0 + jnp.exp(-a))).astype(BF16)
        o_ref[...] = _dot(s, w_ref[...].astype(BF16)) + b_ref[...]

    return pl.pallas_call(
        body, name="mod_fwd", grid=(n // bn,),
        in_specs=[_full((rows, d)), pl.BlockSpec((d, bn), lambda j: (0, j)), pl.BlockSpec((1, bn), lambda j: (0, j))],
        out_specs=pl.BlockSpec((rows, bn), lambda j: (0, j)),
        out_shape=jax.ShapeDtypeStruct((rows, n), F32),
        compiler_params=_params(("parallel",)),
    )(a_in, w_ada, b_sh)


def _mod_bwd(a_in, dm, w_ada):
    rows, d = a_in.shape
    n = w_ada.shape[1]
    bn = 512
    nb = n // bn

    def body(a_ref, dm_ref, w_ref, gw_ref, da_ref):
        j = pl.program_id(0)
        a = a_ref[...]
        s = (a / (1.0 + jnp.exp(-a))).astype(BF16)
        dmb = dm_ref[...].astype(BF16)
        gw_ref[...] = _dot_tn(s, dmb)
        part = _dot_nt(dmb, w_ref[...].astype(BF16))

        @pl.when(j == 0)
        def _():
            da_ref[...] = part

        @pl.when(j > 0)
        def _():
            da_ref[...] += part

    return pl.pallas_call(
        body, name="mod_bwd", grid=(nb,),
        in_specs=[_full((rows, d)), pl.BlockSpec((rows, bn), lambda j: (0, j)), pl.BlockSpec((d, bn), lambda j: (0, j))],
        out_specs=[pl.BlockSpec((d, bn), lambda j: (0, j)), _full((rows, d))],
        out_shape=[jax.ShapeDtypeStruct((d, n), F32), jax.ShapeDtypeStruct((rows, d), F32)],
        compiler_params=_params(("arbitrary",)),
    )(a_in, dm, w_ada)


def _pre_fwd(x2, ctx2, modv, g_attn, w_in, g_q, g_kv, w_uq, w_ukv, cos_t, sin_t, *, seq, tm, rider=None):
    t_lat, d = x2.shape
    t_ctx = ctx2.shape[0]
    nl, nc = t_lat // tm, t_ctx // tm
    n_all = t_lat + t_ctx
    tpe = seq // tm
    nex = t_lat // seq

    def body(x_ref, c_ref, mod_ref, g_ref, win_ref, gq_ref, gkv_ref, wuq_ref, wukv_ref, cos_ref, sin_ref,
             h_ref, pg_ref, rq_ref, rk_ref, rv_ref, nq_ref, nkv_ref, q_ref, k_ref, v_ref):
        i = pl.program_id(0)
        xt = jnp.where(i < nl, x_ref[...], c_ref[...])
        sh = mod_ref[0, 0:1, :]
        sc = mod_ref[0, 1:2, :]
        r = lax.rsqrt(jnp.mean(xt * xt, axis=-1, keepdims=True) + EPS)
        hb = ((xt * r) * g_ref[...] * (1.0 + sc) + sh).astype(BF16)
        h_ref[...] = hb
        p = _dot_nt(hb, win_ref[...])
        cos = cos_ref[...]
        sin = sin_ref[...]
        rq_ref[...] = _rope(p[:, 0:256], cos, sin).astype(BF16)
        rk_ref[...] = _rope(p[:, 256:512] * (RET_DK ** -0.5), cos, sin).astype(BF16)
        rv_ref[...] = p[:, 512:1024].astype(BF16)
        pg_ref[...] = p[:, 1024:2176]
        cq = p[:, 1536:1920]
        ckv = p[:, 1920:2176]
        nqb = (cq * lax.rsqrt(jnp.mean(cq * cq, axis=-1, keepdims=True) + EPS) * gq_ref[...]).astype(BF16)
        nkvb = (ckv * lax.rsqrt(jnp.mean(ckv * ckv, axis=-1, keepdims=True) + EPS) * gkv_ref[...]).astype(BF16)
        nq_ref[...] = nqb
        nkv_ref[...] = nkvb
        cos1 = cos[:, 0:LANES]
        sin1 = sin[:, 0:LANES]
        kpe = _rope(p[:, 2176:2304], cos1, sin1).astype(BF16)
        for hd in range(HEADS):
            o = hd * MLA_HEAD
            qh = _dot_nt(nqb, wuq_ref[hd]) * MLA_SCALE
            q_ref[:, o:o + 128] = qh[:, 0:128].astype(BF16)
            q_ref[:, o + 128:o + 256] = _rope(qh[:, 128:256], cos1, sin1).astype(BF16)
            kvh = _dot(nkvb, wukv_ref[hd])
            k_ref[:, o:o + 128] = kvh[:, 0:128].astype(BF16)
            k_ref[:, o + 128:o + 256] = kpe
            v_ref[:, hd * 128:(hd + 1) * 128] = kvh[:, 128:256].astype(BF16)

    def tile(width):
        return pl.BlockSpec((tm, width), lambda i: (i, 0))

    widths = (d, PG_COLS, 256, 256, 512, Q_LORA, KV_LORA, HEADS * MLA_HEAD, HEADS * MLA_HEAD, HEADS * 128)
    dtypes = (BF16, F32, BF16, BF16, BF16, BF16, BF16, BF16, BF16, BF16)
    tab = pl.BlockSpec((tm, 256), lambda i: (jnp.where(i < nl, i % tpe, tpe), 0))
    return _hosted_call(
        body, (x2, ctx2, modv, g_attn, w_in, g_q, g_kv, w_uq, w_ukv, cos_t, sin_t), name="pre_fwd", grid=(nl + nc,),
        in_specs=[
            pl.BlockSpec((tm, d), lambda i: (jnp.minimum(i, nl - 1), 0)),
            pl.BlockSpec((tm, d), lambda i: (jnp.maximum(i - nl, 0), 0)),
            pl.BlockSpec((1, 8, d), lambda i: (jnp.minimum(i // tpe, nex), 0, 0)),
            _full((1, d)), _full(w_in.shape), _full((1, Q_LORA)), _full((1, KV_LORA)),
            _full(w_uq.shape), _full(w_ukv.shape), tab, tab,
        ],
        out_specs=[tile(w) for w in widths],
        out_shape=[jax.ShapeDtypeStruct((n_all, w), dt) for w, dt in zip(widths, dtypes)],
        sem=("parallel",), rider=rider)


def _post(yret, ymla, x2, tgt2, modv, g_ffn, g_fin, w_out, w_ff1, w_ff2a, w_ff2b, *, seq, tm):
    t_lat, d = x2.shape
    nl = t_lat // tm
    tpe = seq // tm
    nex = t_lat // seq
    n_slab = w_ff1.shape[0]
    fs = w_ff1.shape[2]
    fh = w_ff2a.shape[1]

    def body(yr_ref, ym_ref, x_ref, t_ref, mod_ref, gf_ref, gl_ref, wo_ref, w1_ref, w2a_ref, w2b_ref,
             mix_ref, a_ref, du_ref, h2_ref, df_ref, dmo_ref, dmix_ref, dxm_ref, st_ref, ru_ref):
        i = pl.program_id(0)
        gt_a = mod_ref[0, 2:3, :]
        sh_f = mod_ref[0, 3:4, :]
        sc_f = mod_ref[0, 4:5, :]
        gt_f = mod_ref[0, 5:6, :]
        g_ffn_v = gf_ref[...]
        g_fin_v = gl_ref[...]
        yr = yr_ref[...]
        ym = ym_ref[...]
        mix_ref[:, 0:512] = yr
        mix_ref[:, 512:1024] = ym
        op = _dot(yr, wo_ref[0:512, :]) + _dot(ym, wo_ref[512:1024, :])
        x_mid = x_ref[...] + gt_a * op
        r2 = lax.rsqrt(jnp.mean(x_mid * x_mid, axis=-1, keepdims=True) + EPS)
        xh2 = x_mid * r2
        h2b = (xh2 * g_ffn_v * (1.0 + sc_f) + sh_f).astype(BF16)
        h2_ref[...] = h2b
        f = jnp.zeros((tm, d), F32)
        for s in range(n_slab):
            ru = jnp.maximum(_dot(h2b, w1_ref[s]), 0.0)
            ru_ref[:, s * fs:(s + 1) * fs] = ru
            ab = (ru * ru).astype(BF16)
            a_ref[:, s * fs:(s + 1) * fs] = ab
            f = f + _dot(ab[:, 0:fh], w2a_ref[s]) + _dot(ab[:, fh:fs], w2b_ref[s])
        x_out = x_mid + gt_f * f
        r3 = lax.rsqrt(jnp.mean(x_out * x_out, axis=-1, keepdims=True) + EPS)
        xh3 = x_out * r3
        err = xh3 * g_fin_v - t_ref[...]
        dy = err * (1.0 / d)
        dxh3 = dy * g_fin_v
        dx_out = r3 * (dxh3 - xh3 * jnp.mean(dxh3 * xh3, axis=-1, keepdims=True))
        dfb = (dx_out * gt_f).astype(BF16)
        df_ref[...] = dfb
        dh2 = jnp.zeros((tm, d), F32)
        for s in range(n_slab):
            da = jnp.concatenate([_dot_nt(dfb, w2a_ref[s]), _dot_nt(dfb, w2b_ref[s])], axis=1)
            dub = (da * (2.0 * ru_ref[:, s * fs:(s + 1) * fs])).astype(BF16)
            du_ref[:, s * fs:(s + 1) * fs] = dub
            dh2 = dh2 + _dot_nt(dub, w1_ref[s])
        dxh2 = dh2 * (1.0 + sc_f) * g_ffn_v
        dx_mid = dx_out + r2 * (dxh2 - xh2 * jnp.mean(dxh2 * xh2, axis=-1, keepdims=True))
        dxm_ref[...] = dx_mid
        dmob = (dx_mid * gt_a).astype(BF16)
        dmo_ref[...] = dmob
        dmix_ref[...] = _dot_nt(dmob, wo_ref[...]).astype(BF16)

        def rsum(v):
            return jnp.sum(v, axis=0, keepdims=True)

        stats = jnp.concatenate([
            rsum(dh2), rsum(dh2 * xh2 * g_ffn_v), rsum(dx_out * f), rsum(dx_mid * op),
            rsum(dh2 * (1.0 + sc_f) * xh2), rsum(dy * xh3), rsum(err * err), jnp.zeros((1, d), F32)], axis=0)

        @pl.when(i % tpe == 0)
        def _():
            st_ref[0] = stats

        @pl.when(i % tpe != 0)
        def _():
            st_ref[0] += stats

    def tile(width):
        return pl.BlockSpec((tm, width), lambda i: (i, 0))

    widths = (d, D_FF, D_FF, d, d, d, d, d)
    dtypes = (BF16, BF16, BF16, BF16, BF16, BF16, BF16, F32)
    const = pl.Buffered(1)
    return pl.pallas_call(
        body, name="post", grid=(nl,),
        in_specs=[
            tile(512), tile(512), tile(d), tile(d),
            pl.BlockSpec((1, 8, d), lambda i: (i // tpe, 0, 0)),
            _full((1, d)), _full((1, d)),
            pl.BlockSpec(w_out.shape, lambda i: (0, 0), pipeline_mode=const),
            pl.BlockSpec(w_ff1.shape, lambda i: (0, 0, 0), pipeline_mode=const),
            pl.BlockSpec(w_ff2a.shape, lambda i: (0, 0, 0), pipeline_mode=const),
            pl.BlockSpec(w_ff2b.shape, lambda i: (0, 0, 0), pipeline_mode=const),
        ],
        out_specs=[tile(w) for w in widths] + [pl.BlockSpec((1, 8, d), lambda i: (i // tpe, 0, 0))],
        out_shape=[jax.ShapeDtypeStruct((t_lat, w), dt) for w, dt in zip(widths, dtypes)]
        + [jax.ShapeDtypeStruct((nex, 8, d), F32)],
        scratch_shapes=[pltpu.VMEM((tm, D_FF), F32)],
        compiler_params=_params(("arbitrary",), VMEM_LIMIT),
    )(yret, ymla, x2, tgt2, modv, g_ffn, g_fin, w_out, w_ff1, w_ff2a, w_ff2b)


def _pre_bwd(x2, ctx2, modv, g_attn, pg, drq, drk, dkc_r, drv, dvc_r, drg, dq_m, dkl, dkc, dvl, dvc, dxm,
             w_in, g_q, g_kv, w_uq, w_ukv, cos_t, sin_t, *, seq, tm, rider=None):
    t_lat, d = x2.shape
    t_ctx = ctx2.shape[0]
    nl, nc = t_lat // tm, t_ctx // tm
    n_all = t_lat + t_ctx
    tpe = seq // tm
    nex = t_lat // seq

    def body(x_ref, c_ref, mod_ref, g_ref, pg_ref, drq_ref, drk_ref, dkcr_ref, drv_ref, dvcr_ref, drg_ref,
             dq_ref, dkl_ref, dkc_ref, dvl_ref, dvc_ref, dxm_ref, win_ref, gq_ref, gkv_ref, wuq_ref, wukv_ref,
             cos_ref, sin_ref, dpb_ref, dqf_ref, dkvf_ref, gx_ref, st_ref):
        i = pl.program_id(0)
        lat = i < nl
        latf = lat.astype(F32)
        cos = cos_ref[...]
        sin = sin_ref[...]
        cos1 = cos[:, 0:LANES]
        sin1 = sin[:, 0:LANES]
        d_rq = _rope_t(drq_ref[...] * latf, cos, sin)
        d_rk = _rope_t(jnp.where(lat, drk_ref[...], dkcr_ref[...]), cos, sin) * (RET_DK ** -0.5)
        d_rv = jnp.where(lat, drv_ref[...], dvcr_ref[...])
        d_rg = drg_ref[...] * latf
        dq_all = dq_ref[...] * (latf * MLA_SCALE)
        dk_all = jnp.where(lat, dkl_ref[...], dkc_ref[...])
        dv_all = jnp.where(lat, dvl_ref[...], dvc_ref[...])
        dnq = jnp.zeros((tm, Q_LORA), F32)
        dnkv = jnp.zeros((tm, KV_LORA), F32)
        dkpe = jnp.zeros((tm, LANES), F32)
        for hd in range(HEADS):
            o = hd * MLA_HEAD
            dqh = jnp.concatenate([dq_all[:, o:o + 128], _rope_t(dq_all[:, o + 128:o + 256], cos1, sin1)],
                                  axis=1).astype(BF16)
            dqf_ref[:, o:o + 256] = dqh
            dnq = dnq + _dot(dqh, wuq_ref[hd])
            dkpe = dkpe + dk_all[:, o + 128:o + 256]
            dkvh = jnp.concatenate([dk_all[:, o:o + 128], dv_all[:, hd * 128:(hd + 1) * 128]], axis=1).astype(BF16)
            dkvf_ref[:, o:o + 256] = dkvh
            dnkv = dnkv + _dot_nt(dkvh, wukv_ref[hd])
        d_kpe = _rope_t(dkpe, cos1, sin1)
        pgv = pg_ref[...]
        cq = pgv[:, 512:896]
        ckv = pgv[:, 896:1152]
        rq_ = lax.rsqrt(jnp.mean(cq * cq, axis=-1, keepdims=True) + EPS)
        cqh = cq * rq_
        dcqh = dnq * gq_ref[...]
        d_cq = rq_ * (dcqh - cqh * jnp.mean(dcqh * cqh, axis=-1, keepdims=True))
        rkv_ = lax.rsqrt(jnp.mean(ckv * ckv, axis=-1, keepdims=True) + EPS)
        ckvh = ckv * rkv_
        dckvh = dnkv * gkv_ref[...]
        d_ckv = rkv_ * (dckvh - ckvh * jnp.mean(dckvh * ckvh, axis=-1, keepdims=True))
        dpb = jnp.concatenate([d_rq, d_rk, d_rv, d_rg, d_cq, d_ckv, d_kpe], axis=1).astype(BF16)
        dpb_ref[...] = dpb
        dh = _dot(dpb, win_ref[...])
        xt = jnp.where(lat, x_ref[...], c_ref[...])
        sc = mod_ref[0, 1:2, :]
        g = g_ref[...]
        r = lax.rsqrt(jnp.mean(xt * xt, axis=-1, keepdims=True) + EPS)
        xh = xt * r
        dxh = dh * (1.0 + sc) * g
        dx = r * (dxh - xh * jnp.mean(dxh * xh, axis=-1, keepdims=True))

        @pl.when(lat)
        def _():
            gx_ref[...] = dxm_ref[...] + dx

        def rsum(v):
            return jnp.sum(v, axis=0, keepdims=True)

        def widen(v):
            return jnp.concatenate([v, jnp.zeros((1, d - v.shape[1]), F32)], axis=1)

        stats = jnp.concatenate([
            rsum(dh), rsum(dh * xh * g), rsum(dh * (1.0 + sc) * xh), widen(rsum(dnq * cqh)), widen(rsum(dnkv * ckvh)),
            jnp.zeros((3, d), F32)], axis=0)
        first = jnp.logical_or(jnp.logical_and(lat, i % tpe == 0), i == nl)

        @pl.when(first)
        def _():
            st_ref[0] = stats

        @pl.when(jnp.logical_not(first))
        def _():
            st_ref[0] += stats

    def lat_tile(width):
        return pl.BlockSpec((tm, width), lambda i: (jnp.minimum(i, nl - 1), 0))

    def ctx_tile(width):
        return pl.BlockSpec((tm, width), lambda i: (jnp.maximum(i - nl, 0), 0))

    def tile(width):
        return pl.BlockSpec((tm, width), lambda i: (i, 0))

    tab = pl.BlockSpec((tm, 256), lambda i: (jnp.where(i < nl, i % tpe, tpe), 0))
    ex = pl.BlockSpec((1, 8, d), lambda i: (jnp.minimum(i // tpe, nex), 0, 0))
    return _hosted_call(
        body, (x2, ctx2, modv, g_attn, pg, drq, drk, dkc_r, drv, dvc_r, drg, dq_m, dkl, dkc, dvl, dvc, dxm,
               w_in, g_q, g_kv, w_uq, w_ukv, cos_t, sin_t), name="pre_bwd", grid=(nl + nc,),
        in_specs=[
            lat_tile(d), ctx_tile(d), ex, _full((1, d)), tile(PG_COLS),
            lat_tile(256), lat_tile(256), ctx_tile(256), lat_tile(512), ctx_tile(512), lat_tile(512),
            lat_tile(1024), lat_tile(1024), ctx_tile(1024), lat_tile(512), ctx_tile(512), lat_tile(d),
            _once(w_in.shape), _full((1, Q_LORA)), _full((1, KV_LORA)), _once(w_uq.shape), _once(w_ukv.shape),
            tab, tab,
        ],
        out_specs=[tile(IN_PAD), tile(1024), tile(1024), lat_tile(d), ex],
        out_shape=[
            jax.ShapeDtypeStruct((n_all, IN_PAD), BF16), jax.ShapeDtypeStruct((n_all, 1024), BF16),
            jax.ShapeDtypeStruct((n_all, 1024), BF16), jax.ShapeDtypeStruct((t_lat, d), F32),
            jax.ShapeDtypeStruct((nex + 1, 8, d), F32),
        ],
        sem=("arbitrary",), rider=rider)


MLA_SCALE = 1.0 / math.sqrt(MLA_NOPE + MLA_ROPE)
KEY_BLOCK = 1024


def _mla_specs(t_lat, seq, ctx_len, tq, heads=1):
    nqt = seq // tq
    cb = t_lat // ctx_len
    q = pl.BlockSpec((tq, heads * MLA_HEAD), lambda b, h, j: (b * nqt + j, h))
    kl = pl.BlockSpec((seq, heads * MLA_HEAD), lambda b, h, j: (b, h))
    kc = pl.BlockSpec((ctx_len, heads * MLA_HEAD), lambda b, h, j: (cb + b, h))
    vl = pl.BlockSpec((seq, heads * 128), lambda b, h, j: (b, h))
    vc = pl.BlockSpec((ctx_len, heads * 128), lambda b, h, j: (cb + b, h))
    o = pl.BlockSpec((tq, heads * 128), lambda b, h, j: (b * nqt + j, h))
    return q, kl, kc, vl, vc, o


FWD_HEADS = 2
BWD_HEADS = 1


def _mla_fwd(q, k, v, *, t_lat, seq, ctx_len, tq, rider=None):
    nex = t_lat // seq

    def body(q_ref, kl_ref, kc_ref, vl_ref, vc_ref, o_ref, lse_ref):
        for hh in range(FWD_HEADS):
            wide = slice(hh * MLA_HEAD, (hh + 1) * MLA_HEAD)
            cols = slice(hh * 128, (hh + 1) * 128)
            qb = q_ref[:, wide]
            s = _dot_nt(qb, kl_ref[:, wide])
            sc = _dot_nt(qb, kc_ref[:, wide])
            m = jnp.maximum(jnp.max(s, axis=-1, keepdims=True), jnp.max(sc, axis=-1, keepdims=True))
            p = jnp.exp(s - m)
            pc = jnp.exp(sc - m)
            total = jnp.sum(p, axis=-1, keepdims=True) + jnp.sum(pc, axis=-1, keepdims=True)
            o = _dot(p.astype(BF16), vl_ref[:, cols]) + _dot(pc.astype(BF16), vc_ref[:, cols])
            o_ref[:, cols] = (o * (1.0 / total)).astype(BF16)
            lse_ref[:, cols] = jnp.broadcast_to(m + jnp.log(total), (tq, 128))

    qs, kl, kc, vl, vc, os_ = _mla_specs(t_lat, seq, ctx_len, tq, FWD_HEADS)
    return _hosted_call(
        body, (q, k, k, v, v), name="mla_fwd", grid=(nex, HEADS // FWD_HEADS, seq // tq),
        in_specs=[qs, kl, kc, vl, vc], out_specs=[os_, os_],
        out_shape=[jax.ShapeDtypeStruct((t_lat, HEADS * 128), BF16), jax.ShapeDtypeStruct((t_lat, HEADS * 128), F32)],
        sem=("parallel", "parallel", "arbitrary"), rider=rider)


def _mla_bwd(q, k, v, ymla, lse, dmix, *, t_lat, seq, ctx_len, tq, rider=None):
    nex = t_lat // seq
    nqt = seq // tq
    t_ctx = nex * ctx_len
    kb = min(KEY_BLOCK, seq)

    def body(q_ref, kl_ref, kc_ref, vl_ref, vc_ref, o_ref, lse_ref, do_ref, dq_ref, dkl_out, dkc_out, dvl_out, dvc_out,
             dkl_ref, dkc_ref, dvl_ref, dvc_ref):
        j = pl.program_id(2)

        @pl.when(j == 0)
        def _():
            dkl_ref[...] = jnp.zeros(dkl_ref.shape, F32)
            dkc_ref[...] = jnp.zeros(dkc_ref.shape, F32)
            dvl_ref[...] = jnp.zeros(dvl_ref.shape, F32)
            dvc_ref[...] = jnp.zeros(dvc_ref.shape, F32)

        for hh in range(BWD_HEADS):
            wide = slice(hh * MLA_HEAD, (hh + 1) * MLA_HEAD)
            cols = slice(hh * 128, (hh + 1) * 128)
            qb = q_ref[:, wide]
            dob = do_ref[:, cols]
            delta = jnp.sum(dob.astype(F32) * o_ref[:, cols].astype(F32), axis=-1, keepdims=True)
            lse_row = lse_ref[:, hh * 128:hh * 128 + 1]

            def block(k_ref, v_ref, dk_ref, dv_ref, rows):
                kbl = k_ref[rows, wide]
                vbl = v_ref[rows, cols]
                p = jnp.exp(_dot_nt(qb, kbl) - lse_row)
                ds = (p * (_dot_nt(dob, vbl) - delta)).astype(BF16)
                dk_ref[rows, wide] += _dot_tn(ds, qb)
                dv_ref[rows, cols] += _dot_tn(p.astype(BF16), dob)
                return _dot(ds, kbl)

            dq = block(kc_ref, vc_ref, dkc_ref, dvc_ref, pl.ds(0, ctx_len))
            for i in range(seq // kb):
                dq = dq + block(kl_ref, vl_ref, dkl_ref, dvl_ref, pl.ds(i * kb, kb))
            dq_ref[:, wide] = dq.astype(BF16)

        @pl.when(j == nqt - 1)
        def _():
            dkl_out[...] = dkl_ref[...].astype(BF16)
            dkc_out[...] = dkc_ref[...].astype(BF16)
            dvl_out[...] = dvl_ref[...].astype(BF16)
            dvc_out[...] = dvc_ref[...].astype(BF16)

    g = BWD_HEADS
    qs, kl, kc, vl, vc, os_ = _mla_specs(t_lat, seq, ctx_len, tq, g)
    do_spec = pl.BlockSpec((tq, g * 128), lambda b, h, j: (b * nqt + j, HEADS // g + h))
    key_blocks = [(seq, g * MLA_HEAD), (ctx_len, g * MLA_HEAD), (seq, g * 128), (ctx_len, g * 128)]
    return _hosted_call(
        body, (q, k, k, v, v, ymla, lse, dmix), name="mla_bwd", grid=(nex, HEADS // g, nqt),
        in_specs=[qs, kl, kc, vl, vc, os_, os_, do_spec],
        out_specs=[qs] + [pl.BlockSpec(blk, lambda b, h, j: (b, h)) for blk in key_blocks],
        out_shape=[
            jax.ShapeDtypeStruct((t_lat, HEADS * MLA_HEAD), BF16),
            jax.ShapeDtypeStruct((t_lat, HEADS * MLA_HEAD), BF16),
            jax.ShapeDtypeStruct((t_ctx, HEADS * MLA_HEAD), BF16),
            jax.ShapeDtypeStruct((t_lat, HEADS * 128), BF16),
            jax.ShapeDtypeStruct((t_ctx, HEADS * 128), BF16),
        ],
        scratch_shapes=[pltpu.VMEM(blk, F32) for blk in key_blocks],
        sem=("parallel", "parallel", "arbitrary"), rider=rider)


def _decay_terms(lg, chunk, forward):
    ii = lax.broadcasted_iota(jnp.int32, (chunk, chunk), 0)
    jj = lax.broadcasted_iota(jnp.int32, (chunk, chunk), 1)
    diff = (ii - jj) if forward else (jj - ii)
    dist = jnp.maximum(diff, 0).astype(F32)
    dmat = jnp.where(diff >= 0, jnp.exp(lg * dist), 0.0)
    pos = lax.broadcasted_iota(jnp.int32, (chunk, 1), 0).astype(F32)
    if forward:
        e_q = pos + 1.0
        e_k = (chunk - 1.0) - pos
    else:
        e_q = chunk - pos
        e_k = pos
    wq = jnp.exp(lg * e_q)
    wk = jnp.exp(lg * e_k)
    cd = jnp.exp(jnp.full((1, 1), lg * chunk, F32))
    return dmat, dist, wq, wk, e_q, e_k, cd


def _ctx_weights(lg, ctx_len, forward):
    pos = lax.broadcasted_iota(jnp.int32, (ctx_len, 1), 0).astype(F32)
    e = ((ctx_len - 1.0) - pos) if forward else pos
    return jnp.exp(lg * e), e


def _pair_specs(t_lat, seq, ctx_len):
    cb = t_lat // ctx_len
    qk = pl.BlockSpec((seq, 128), lambda b, p: (b, p))
    v = pl.BlockSpec((seq, 256), lambda b, p: (b, p))
    kc = pl.BlockSpec((ctx_len, 128), lambda b, p: (cb + b, p))
    vc = pl.BlockSpec((ctx_len, 256), lambda b, p: (cb + b, p))
    return qk, v, kc, vc


def _lane_masks():
    lane = lax.broadcasted_iota(jnp.int32, (1, 128), 1)
    return [(lane // RET_DK) == hh for hh in (0, 1)]


def _ret_fwd_pair(rq, rk, rv, pg, lg, g_ret, *, t_lat, seq, ctx_len, chunk, rider=None):
    nex = t_lat // seq
    n_chunk = seq // chunk

    def body(q_ref, k_ref, v_ref, kc_ref, vc_ref, rg_ref, lg_ref, g_ref, y_ref, o_ref):
        pair = pl.program_id(1)
        masks = _lane_masks()
        kcf = kc_ref[...].astype(F32)
        chains = [(forward, hh) for forward in (True, False) for hh in (0, 1)]
        terms, s0 = [], []
        for forward, hh in chains:
            lgd = lg_ref[0 if forward else 1, 2 * pair + hh]
            terms.append(_decay_terms(lgd, chunk, forward))
            wc, _ = _ctx_weights(lgd, ctx_len, forward)
            s0.append(_dot_tn((jnp.where(masks[hh], kcf, 0.0) * wc).astype(BF16), vc_ref[:, hh * 128:(hh + 1) * 128]))
        both = [terms[hh][0] + terms[2 + hh][0] for hh in (0, 1)]
        o_ref[...] = jnp.zeros(o_ref.shape, F32)

        def step(t, states):
            new = [None] * 4
            for forward in (True, False):
                n = t if forward else n_chunk - 1 - t
                sl = pl.ds(pl.multiple_of(n * chunk, chunk), chunk)
                qb = q_ref[sl, :]
                kf_all = k_ref[sl, :].astype(F32)
                for hh in (0, 1):
                    c = (0 if forward else 2) + hh
                    _, _, wq, wk, _, _, cd = terms[c]
                    cols = slice(hh * 128, (hh + 1) * 128)
                    qm = jnp.where(masks[hh], qb, jnp.zeros((), BF16))
                    kf = jnp.where(masks[hh], kf_all, 0.0)
                    vb = v_ref[sl, cols]
                    o = wq * _dot(qm, states[c].astype(BF16))
                    if forward:
                        o = o + _dot((_dot_nt(qm, kf.astype(BF16)) * both[hh]).astype(BF16), vb)
                    o_ref[sl, cols] += o
                    new[c] = cd * states[c] + _dot_tn((kf * wk).astype(BF16), vb)
            return tuple(new)

        lax.fori_loop(0, n_chunk, step, tuple(s0))

        def norm_step(n, carry):
            sl = pl.ds(pl.multiple_of(n * chunk, chunk), chunk)
            for hh in (0, 1):
                cols = slice(hh * 128, (hh + 1) * 128)
                o = o_ref[sl, cols]
                mu = jnp.mean(o, axis=-1, keepdims=True)
                oc = o - mu
                var = jnp.mean(oc * oc, axis=-1, keepdims=True)
                rg = rg_ref[sl, cols]
                y_ref[sl, cols] = (oc * lax.rsqrt(var + EPS) * g_ref[:, cols] * (rg / (1.0 + jnp.exp(-rg)))).astype(BF16)
            return carry

        lax.fori_loop(0, n_chunk, norm_step, 0)

    qk, v, kc, vc = _pair_specs(t_lat, seq, ctx_len)
    return _hosted_call(
        body, (rq, rk, rv, rk, rv, pg, lg, g_ret), name="ret_fwd", grid=(nex, HEADS // 2),
        in_specs=[qk, qk, v, kc, vc, v, pl.BlockSpec(memory_space=pltpu.SMEM), pl.BlockSpec((1, 256), lambda b, p: (0, p))],
        out_specs=[v, v],
        out_shape=[jax.ShapeDtypeStruct((t_lat, HEADS * RET_DV), BF16), jax.ShapeDtypeStruct((t_lat, HEADS * RET_DV), F32)],
        sem=("parallel", "arbitrary"), rider=rider)


def _ret_bwd_pair(rq, rk, rv, pg, osum, dmix, lg, g_ret, *, t_lat, seq, ctx_len, chunk, rider=None):
    nex = t_lat // seq
    n_chunk = seq // chunk
    t_ctx = nex * ctx_len

    def body(q_ref, k_ref, v_ref, kc_ref, vc_ref, rg_ref, o_ref, dy_ref, lg_ref, g_ref,
             dq_out, dk_out, dv_out, dkc_ref, dvc_ref, drg_ref, st_ref, do_s, s_st, dq_ref, dk_ref, dv_ref):
        pair = pl.program_id(1)
        masks = _lane_masks()
        kcf = kc_ref[...].astype(F32)

        def norm_step(n, dgains):
            sl = pl.ds(pl.multiple_of(n * chunk, chunk), chunk)
            out = []
            for hh in (0, 1):
                cols = slice(hh * 128, (hh + 1) * 128)
                gain = g_ref[:, cols]
                o = o_ref[sl, cols]
                mu = jnp.mean(o, axis=-1, keepdims=True)
                oc = o - mu
                rstd = lax.rsqrt(jnp.mean(oc * oc, axis=-1, keepdims=True) + EPS)
                ohat = oc * rstd
                rg = rg_ref[sl, cols]
                sg = 1.0 / (1.0 + jnp.exp(-rg))
                dy = dy_ref[sl, cols].astype(F32)
                don = dy * (rg * sg)
                drg_ref[sl, cols] = (dy * (ohat * gain) * (sg * (1.0 + rg * (1.0 - sg)))).astype(BF16)
                dohat = don * gain
                do_s[sl, cols] = rstd * (dohat - jnp.mean(dohat, axis=-1, keepdims=True)
                                         - ohat * jnp.mean(dohat * ohat, axis=-1, keepdims=True))
                out.append(dgains[hh] + jnp.sum(don * ohat, axis=0, keepdims=True))
            return tuple(out)

        zero_row = jnp.zeros((1, 128), F32)
        dgains = lax.fori_loop(0, n_chunk, norm_step, (zero_row, zero_row))
        dq_ref[...] = jnp.zeros(dq_ref.shape, F32)
        dk_ref[...] = jnp.zeros(dk_ref.shape, F32)
        dv_ref[...] = jnp.zeros(dv_ref.shape, F32)

        chains = [(forward, hh) for forward in (True, False) for hh in (0, 1)]
        terms, ctxw, s0 = [], [], []
        for forward, hh in chains:
            lgd = lg_ref[0 if forward else 1, 2 * pair + hh]
            terms.append(_decay_terms(lgd, chunk, forward))
            ctxw.append(_ctx_weights(lgd, ctx_len, forward))
            s0.append(_dot_tn((jnp.where(masks[hh], kcf, 0.0) * ctxw[-1][0]).astype(BF16), vc_ref[:, hh * 128:(hh + 1) * 128]))

        def chunk_at(t, ascending):
            n = t if ascending else n_chunk - 1 - t
            return n, pl.ds(pl.multiple_of(n * chunk, chunk), chunk)

        def state_step(t, states):
            new = []
            for c, (forward, hh) in enumerate(chains):
                n, sl = chunk_at(t, forward)
                wk, cd = terms[c][3], terms[c][6]
                s_st[c, n] = states[c]
                kf = jnp.where(masks[hh], k_ref[sl, :].astype(F32), 0.0)
                new.append(cd * states[c] + _dot_tn((kf * wk).astype(BF16), v_ref[sl, hh * 128:(hh + 1) * 128]))
            return tuple(new)

        lax.fori_loop(0, n_chunk, state_step, tuple(s0))

        both = [terms[hh][0] + terms[2 + hh][0] for hh in (0, 1)]

        def grad_step(t, carry):
            out = [None] * len(chains)
            in_chunk_b = [None, None]
            for forward in (True, False):
                n, sl = chunk_at(t, not forward)
                qb = q_ref[sl, :]
                kf_all = k_ref[sl, :].astype(F32)
                dq_sum = jnp.zeros((chunk, 128), F32)
                dk_sum = jnp.zeros((chunk, 128), F32)
                for hh in (0, 1):
                    c = (0 if forward else 2) + hh
                    g_next, dlg = carry[c]
                    dmat, dist, wq, wk, e_q, e_k, cd = terms[c]
                    cols = slice(hh * 128, (hh + 1) * 128)
                    qm = jnp.where(masks[hh], qb, jnp.zeros((), BF16))
                    kf = jnp.where(masks[hh], kf_all, 0.0)
                    kb = kf.astype(BF16)
                    vb = v_ref[sl, cols]
                    do = do_s[sl, cols]
                    dob = do.astype(BF16)
                    s_n = s_st[c, n]
                    s_nb = s_n.astype(BF16)
                    gb = g_next.astype(BF16)
                    dk_cross = wk * _dot_nt(vb, gb)
                    dv = _dot((kf * wk).astype(BF16), gb)
                    o_cross = wq * _dot(qm, s_nb)
                    dq_sum = dq_sum + wq * _dot_nt(dob, s_nb)
                    dk_sum = dk_sum + dk_cross
                    dlg = (dlg + chunk * cd * jnp.sum(g_next * s_n, keepdims=True)
                           + jnp.sum(e_k * jnp.sum(kf * dk_cross, axis=-1, keepdims=True), keepdims=True)
                           + jnp.sum(e_q * jnp.sum(o_cross * do, axis=-1, keepdims=True), keepdims=True))
                    if forward:
                        a_raw = _dot_nt(qm, kb)
                        da_raw = _dot_nt(dob, vb)
                        prod = a_raw * da_raw
                        dlg = dlg + jnp.sum(dist * dmat * prod, keepdims=True)
                        in_chunk_b[hh] = jnp.sum(terms[2 + hh][1] * terms[2 + hh][0] * prod, keepdims=True)
                        dab = (da_raw * both[hh]).astype(BF16)
                        dq_sum = dq_sum + _dot(dab, kb)
                        dk_sum = dk_sum + _dot_tn(dab, qm)
                        dv = dv + _dot_tn((a_raw * both[hh]).astype(BF16), dob)
                    else:
                        dlg = dlg + in_chunk_b[hh]
                    dv_ref[sl, cols] += dv
                    out[c] = (cd * g_next + _dot_tn((qm.astype(F32) * wq).astype(BF16), dob), dlg)
                dq_ref[sl, :] += dq_sum
                dk_ref[sl, :] += dk_sum
            return tuple(out)

        zero = (jnp.zeros((128, 128), F32), jnp.zeros((1, 1), F32))
        res = lax.fori_loop(0, n_chunk, grad_step, (zero,) * len(chains))
        dkc_sum = jnp.zeros((ctx_len, 128), F32)
        dvc = [jnp.zeros((ctx_len, 128), F32)] * 2
        dlgs = []
        for c, (forward, hh) in enumerate(chains):
            ds0, dlg = res[c]
            wc, e_c = ctxw[c]
            kcm = jnp.where(masks[hh], kcf, 0.0)
            ds0b = ds0.astype(BF16)
            dkc_part = wc * _dot_nt(vc_ref[:, hh * 128:(hh + 1) * 128], ds0b)
            dkc_sum = dkc_sum + dkc_part
            dvc[hh] = dvc[hh] + _dot((kcm * wc).astype(BF16), ds0b)
            dlgs.append(dlg + jnp.sum(e_c * jnp.sum(kcm * dkc_part, axis=-1, keepdims=True), keepdims=True))
        dq_out[...] = dq_ref[...].astype(BF16)
        dk_out[...] = dk_ref[...].astype(BF16)
        dv_out[...] = dv_ref[...].astype(BF16)
        dkc_ref[...] = dkc_sum
        for hh in (0, 1):
            cols = slice(hh * 128, (hh + 1) * 128)
            dvc_ref[:, cols] = dvc[hh]
            st_ref[0, :, cols] = jnp.concatenate([
                dgains[hh], jnp.broadcast_to(dlgs[hh], (1, 128)), jnp.broadcast_to(dlgs[2 + hh], (1, 128)),
                jnp.zeros((5, 128), F32)], axis=0)

    qk, v, kc, vc = _pair_specs(t_lat, seq, ctx_len)
    return _hosted_call(
        body, (rq, rk, rv, rk, rv, pg, osum, dmix, lg, g_ret), name="ret_bwd", grid=(nex, HEADS // 2),
        in_specs=[qk, qk, v, kc, vc, v, v, v, pl.BlockSpec(memory_space=pltpu.SMEM),
                  pl.BlockSpec((1, 256), lambda b, p: (0, p))],
        out_specs=[
            qk, qk, v,
            pl.BlockSpec((ctx_len, 128), lambda b, p: (b, p)),
            pl.BlockSpec((ctx_len, 256), lambda b, p: (b, p)),
            v,
            pl.BlockSpec((1, 8, 256), lambda b, p: (b, 0, p)),
        ],
        out_shape=[
            jax.ShapeDtypeStruct((t_lat, 256), BF16), jax.ShapeDtypeStruct((t_lat, 256), BF16),
            jax.ShapeDtypeStruct((t_lat, 512), BF16), jax.ShapeDtypeStruct((t_ctx, 256), F32),
            jax.ShapeDtypeStruct((t_ctx, 512), F32), jax.ShapeDtypeStruct((t_lat, 512), BF16),
            jax.ShapeDtypeStruct((nex, 8, 512), F32),
        ],
        scratch_shapes=[pltpu.VMEM((seq, 256), F32), pltpu.VMEM((4, n_chunk, 128, 128), F32),
                        pltpu.VMEM((seq, 128), F32), pltpu.VMEM((seq, 128), F32), pltpu.VMEM((seq, 256), F32)],
        sem=("parallel", "arbitrary"), rider=rider)


def _matmul_tn(a, b, *, bm, bn, bk, chip_major, name, out_dtype=F32, rider=None):
    tk, m = a.shape
    n = b.shape[1]
    slab = n // N_CHIPS
    per_block = bn // slab if chip_major else 1
    bk = max(c for c in range(LANES, min(bk, tk) + 1, LANES) if tk % c == 0)
    nk = tk // bk
    blk = (per_block, bm, slab) if chip_major else (bm, bn)

    def body(a_ref, b_ref, o_ref, acc_ref):
        k = pl.program_id(2)
        if chip_major:
            parts = [_dot_tn(a_ref[...], b_ref[:, s * slab:(s + 1) * slab]) for s in range(per_block)]
        else:
            parts = [_dot_tn(a_ref[...], b_ref[...])]

        @pl.when(k == 0)
        def _():
            for s, part in enumerate(parts):
                if chip_major:
                    acc_ref[s] = part
                else:
                    acc_ref[...] = part

        @pl.when(k > 0)
        def _():
            for s, part in enumerate(parts):
                if chip_major:
                    acc_ref[s] += part
                else:
                    acc_ref[...] += part

        @pl.when(k == nk - 1)
        def _():
            o_ref[...] = acc_ref[...].astype(out_dtype)

    if chip_major:
        out_spec = pl.BlockSpec(blk, lambda i, j, k: (j, i, 0))
        out_shape = jax.ShapeDtypeStruct((N_CHIPS, m, slab), out_dtype)
    else:
        out_spec = pl.BlockSpec(blk, lambda i, j, k: (i, j))
        out_shape = jax.ShapeDtypeStruct((m, n), out_dtype)
    (out,), carried = _hosted_call(
        body, (a, b), name=name, grid=(m // bm, n // bn, nk),
        in_specs=[pl.BlockSpec((bk, bm), lambda i, j, k: (k, i)), pl.BlockSpec((bk, bn), lambda i, j, k: (k, j))],
        out_specs=[out_spec], out_shape=[out_shape], scratch_shapes=[pltpu.VMEM(blk, F32)],
        sem=("parallel", "parallel", "arbitrary"), rider=rider)
    return out if rider is None else (out, carried)


_LATE = ("w_out", "w_ff1", "w_ff2")
_EARLY = ("w_in", "w_uq", "w_ukv")


def _local_step(x, ctx, tgt, modv, lg, g_attn, g_ffn, g_fin, g_ret, g_q, g_kv, w_in, w_uq, w_ukv, late, place=None,
                *, tm=256, tq=256, chunk=256):
    nex, seq, d = x.shape
    ctx_len = ctx.shape[1]
    t_lat = nex * seq
    tm = min(tm, seq)
    x2 = x.reshape(t_lat, d)
    ctx2 = ctx.reshape(nex * ctx_len, d)
    tgt2 = tgt.reshape(t_lat, d)
    tm_fwd = min(2 * tm, seq)
    cos_t, sin_t = _rope_tables(seq, tm)
    dims = dict(t_lat=t_lat, seq=seq, ctx_len=ctx_len)
    alone = place is None

    (hb, pg, rq, rk, rv, nq, nkv, q, k, v), crossed_a = _pre_fwd(
        x2, ctx2, modv, g_attn, w_in, g_q, g_kv, w_uq, w_ukv, *_rope_tables(seq, tm_fwd), seq=seq, tm=tm_fwd,
        rider=None if alone else _gather_ici_rider([late[2]]))
    (yret, osum), got = _ret_fwd_pair(
        rq, rk, rv, pg, lg, g_ret, chunk=min(2 * chunk, seq), **dims,
        rider=None if alone else _merge_riders(_gather_d2d_rider(crossed_a), _gather_ici_rider([late[3]])))
    (ymla, lse), got_rest = _mla_fwd(
        q, k, v, tq=tq, **dims,
        rider=None if alone else _merge_riders(_gather_rider([late[0], late[1]], staged=True), _gather_d2d_rider(got[1:])))
    w_out, w_ff1, w_ff2a, w_ff2b = late if alone else (got_rest[0], got_rest[1], got[0], got_rest[2])
    mix, act, du, h2, df, dmo, dmix, dxm, st_post = _post(yret, ymla, x2, tgt2, modv, g_ffn, g_fin, w_out.reshape(d, d),
                                                         w_ff1, w_ff2a, w_ff2b, seq=seq, tm=min(tm, 256))
    kw = dict(bm=1024, bn=1024, bk=2048, out_dtype=BF16)
    g_ff2 = _matmul_tn(act, df, chip_major=False, name="gw_ff2", **kw).reshape(N_CHIPS, D_FF // N_CHIPS, d)
    if alone:
        g_ff1 = _matmul_tn(h2, du, chip_major=True, name="gw_ff1", **kw)
        g_out = _matmul_tn(mix, dmo, chip_major=False, name="gw_out", **kw).reshape(N_CHIPS, d // N_CHIPS, d)
        (dq_m, dkl, dkc, dvl, dvc), _ = _mla_bwd(q, k, v, ymla, lse, dmix, tq=tq, **dims)
        (drq, drk, drv, dkc_r, dvc_r, drg, st_ret), _ = _ret_bwd_pair(rq, rk, rv, pg, osum, dmix, lg, g_ret, chunk=chunk,
                                                                      **dims)
        late_out = [g_out, g_ff1, g_ff2]
    else:
        core, slot = place
        g_ff1, x_ff2 = _matmul_tn(h2, du, chip_major=True, name="gw_ff1", rider=_exchange_rider([g_ff2]), **kw)
        g_out, x_ff1 = _matmul_tn(mix, dmo, chip_major=False, name="gw_out", rider=_exchange_rider([g_ff1]), **kw)
        g_out = g_out.reshape(N_CHIPS, d // N_CHIPS, d)
        p_ff2 = _add_half(g_ff2, x_ff2[0], core, "add_half_w_ff2")
        p_ff1 = _add_half(g_ff1, x_ff1[0], core, "add_half_w_ff1")
        (dq_m, dkl, dkc, dvl, dvc), (l_ff2, l_ff1, x_out) = _mla_bwd(
            q, k, v, ymla, lse, dmix, tq=min(seq, 512), **dims,
            rider=_merge_riders(_scatter_rider([p_ff2, p_ff1]), _exchange_rider([g_out])))
        p_out = _add_half(g_out, x_out, core, "add_half_w_out")
        m_ff2 = _sum_chips(p_ff2, l_ff2, slot, "sum_chips_w_ff2")
        m_ff1 = _sum_chips(p_ff1, l_ff1, slot, "sum_chips_w_ff1")
        (drq, drk, drv, dkc_r, dvc_r, drg, st_ret), (l_out,) = _ret_bwd_pair(
            rq, rk, rv, pg, osum, dmix, lg, g_ret, chunk=chunk, **dims, rider=_scatter_rider([p_out]))
        late_out = [_sum_chips(p_out, l_out, slot, "sum_chips_w_out"), m_ff1, m_ff2]
    (dpb, dqf, dkvf, gx, st_pre), _ = _pre_bwd(
        x2, ctx2, modv, g_attn, pg, drq, drk, dkc_r, drv, dvc_r, drg, dq_m, dkl, dkc, dvl, dvc, dxm, w_in, g_q, g_kv,
        w_uq, w_ukv, cos_t, sin_t, seq=seq, tm=tm)
    g_early = [
        _matmul_tn(dpb, hb, bm=IN_PAD // 2, bn=d, bk=1536, chip_major=False, name="gw_in"),
        _matmul_tn(dqf, nq, bm=HEADS * MLA_HEAD, bn=Q_LORA, bk=1536, chip_major=False, name="gw_uq"),
        _matmul_tn(nkv, dkvf, bm=KV_LORA, bn=HEADS * 256, bk=1536, chip_major=True, name="gw_ukv"),
    ]
    return gx.reshape(nex, seq, d), g_early, late_out, st_post, st_ret, st_pre


_ANY = pl.BlockSpec(memory_space=pl.ANY)
_VMEM = pl.BlockSpec(memory_space=pltpu.VMEM)
_OFFSETS = tuple((dx, dy, dc) for dx in (0, 1) for dy in (0, 1) for dc in (0, 1))[1:]
_CHIP_OFFSETS = ((1, 0), (0, 1), (1, 1))


def _place():
    return lax.axis_index("x"), lax.axis_index("y"), lax.axis_index("c")


def _flip(v, d):
    return 1 - v if d else v


def _gather8_rider(a, in_vmem=True):
    def copies(a_ref, o_ref, send, recv):
        x, y, z = _place()
        me = 4 * x + 2 * y + z
        out = []
        for k, (dx, dy, dc) in enumerate(_OFFSETS):
            peer = (_flip(x, dx), _flip(y, dy), _flip(z, dc))
            landing = o_ref.at[4 * peer[0] + 2 * peer[1] + peer[2]]
            out.append((
                pltpu.make_async_remote_copy(src_ref=a_ref, dst_ref=o_ref.at[me], send_sem=send.at[k],
                                             recv_sem=recv.at[k], device_id=peer, device_id_type=MESH),
                pltpu.make_async_remote_copy(src_ref=a_ref, dst_ref=landing, send_sem=send.at[k],
                                             recv_sem=recv.at[k], device_id=peer, device_id_type=MESH)))
        return me, out

    def start(ins, outs, sems):
        me, cps = copies(ins[0], outs[0], sems[0], sems[1])
        pltpu.make_async_copy(ins[0], outs[0].at[me], sems[2]).start()
        for out_cp, _ in cps:
            out_cp.start()

    def finish(ins, outs, sems):
        me, cps = copies(ins[0], outs[0], sems[0], sems[1])
        for out_cp, in_cp in cps:
            in_cp.wait_recv()
            out_cp.wait_send()
        pltpu.make_async_copy(ins[0], outs[0].at[me], sems[2]).wait()

    spec = [_VMEM] if in_vmem else [_ANY]
    return _Rider([a], [jax.ShapeDtypeStruct((N_DEV,) + a.shape, a.dtype)],
                  [pltpu.SemaphoreType.DMA((7,)), pltpu.SemaphoreType.DMA((7,)), pltpu.SemaphoreType.DMA],
                  start, finish, in_specs=spec, out_specs=spec)


def _merge_riders(*riders):
    ins, outs, sems, in_specs, out_specs, aliases, cuts = [], [], [], [], [], {}, []
    for r in riders:
        cuts.append((len(ins), len(outs), len(sems)))
        aliases.update({len(ins) + i: len(outs) + j for i, j in r.aliases.items()})
        ins += r.ins
        outs += r.out_shapes
        sems += r.sems
        in_specs += r.in_specs
        out_specs += r.out_specs

    def part(r, cut, r_ins, r_outs, r_sems):
        return (r_ins[cut[0]:cut[0] + len(r.ins)], r_outs[cut[1]:cut[1] + len(r.out_shapes)],
                r_sems[cut[2]:cut[2] + len(r.sems)])

    def start(r_ins, r_outs, r_sems):
        for r, cut in zip(riders, cuts):
            r.start(*part(r, cut, r_ins, r_outs, r_sems))

    def finish(r_ins, r_outs, r_sems):
        for r, cut in zip(riders, cuts):
            r.finish(*part(r, cut, r_ins, r_outs, r_sems))

    def middle(r_ins, r_outs, r_sems):
        for r, cut in zip(riders, cuts):
            if r.middle is not None:
                r.middle(*part(r, cut, r_ins, r_outs, r_sems))

    return _Rider(ins, outs, sems, start, finish, aliases=aliases, in_specs=in_specs, out_specs=out_specs,
                  middle=middle if any(r.middle is not None for r in riders) else None)


def _allgather8(a, name):
    return _run_rider(_gather8_rider(a), name)[0]


BF16_TILE_ROWS = 16


def _half(o, slot, which):
    r2 = o.shape[1] // 2
    if r2 % BF16_TILE_ROWS == 0:
        return o.at[slot, pl.ds(which * r2, r2)]
    c2 = o.shape[2] // 2
    assert c2 % LANES == 0
    return o.at[slot, :, pl.ds(which * c2, c2)]


def _gather_send(o_refs, send, recv):
    x, y, z = _place()
    chip = 2 * x + y
    for a, o in enumerate(o_refs):
        r2 = o.shape[1] // 2
        mine = _half(o, chip, z)
        for k, (dx, dy) in enumerate(_CHIP_OFFSETS):
            pltpu.make_async_remote_copy(
                src_ref=mine, dst_ref=mine, send_sem=send.at[a, k], recv_sem=recv.at[a, k],
                device_id=(_flip(x, dx), _flip(y, dy), z), device_id_type=MESH).start()


def _gather_landed(o_refs, send, recv, then=None):
    x, y, z = _place()
    chip = 2 * x + y
    for a, o in enumerate(o_refs):
        for k, (dx, dy) in enumerate(_CHIP_OFFSETS):
            landed = _half(o, 2 * _flip(x, dx) + _flip(y, dy), z)
            pltpu.make_async_remote_copy(
                src_ref=landed, dst_ref=landed, send_sem=send.at[a, k], recv_sem=recv.at[a, k],
                device_id=(_flip(x, dx), _flip(y, dy), z), device_id_type=MESH).wait_recv()
            if then is not None:
                then(a, k, landed)
    for a, o in enumerate(o_refs):
        mine = _half(o, chip, z)
        for k, (dx, dy) in enumerate(_CHIP_OFFSETS):
            pltpu.make_async_remote_copy(
                src_ref=mine, dst_ref=mine, send_sem=send.at[a, k], recv_sem=recv.at[a, k],
                device_id=(_flip(x, dx), _flip(y, dy), z), device_id_type=MESH).wait_send()


def _pass_on(o_refs, fsend, frecv, a, k, landed):
    x, y, z = _place()
    pltpu.make_async_remote_copy(
        src_ref=landed, dst_ref=landed, send_sem=fsend.at[a, k], recv_sem=frecv.at[a, k],
        device_id=(x, y, 1 - z), device_id_type=MESH).start()


def _passed_on(o_refs, fsend, frecv):
    x, y, z = _place()
    for a, o in enumerate(o_refs):
        for k, (dx, dy) in enumerate(_CHIP_OFFSETS):
            other = 2 * _flip(x, dx) + _flip(y, dy)
            got = _half(o, other, 1 - z)
            gave = _half(o, other, z)
            pltpu.make_async_remote_copy(
                src_ref=got, dst_ref=got, send_sem=fsend.at[a, k], recv_sem=frecv.at[a, k],
                device_id=(x, y, 1 - z), device_id_type=MESH).wait_recv()
            pltpu.make_async_remote_copy(
                src_ref=gave, dst_ref=gave, send_sem=fsend.at[a, k], recv_sem=frecv.at[a, k],
                device_id=(x, y, 1 - z), device_id_type=MESH).wait_send()


def _gather_finish(o_refs, send, recv, fsend, frecv):
    _gather_landed(o_refs, send, recv, functools.partial(_pass_on, o_refs, fsend, frecv))
    _passed_on(o_refs, fsend, frecv)


class _Rider:
    def __init__(self, ins, out_shapes, sems, start, finish, aliases=None, in_specs=None, out_specs=None, middle=None):
        self.ins, self.out_shapes, self.sems = list(ins), list(out_shapes), list(sems)
        self.start, self.finish, self.aliases = start, finish, dict(aliases or {})
        self.middle = middle
        self.in_specs = list(in_specs) if in_specs else [_ANY] * len(self.ins)
        self.out_specs = list(out_specs) if out_specs else [_ANY] * len(self.out_shapes)


def _run_rider(rider, name):
    r_in, r_out = len(rider.ins), len(rider.out_shapes)

    def body(*refs):
        ins, outs, sems = refs[:r_in], refs[r_in:r_in + r_out], refs[r_in + r_out:]
        rider.start(ins, outs, sems)
        if rider.middle is not None:
            rider.middle(ins, outs, sems)
        rider.finish(ins, outs, sems)

    return pl.pallas_call(
        body, name=name, in_specs=rider.in_specs, out_specs=rider.out_specs, out_shape=rider.out_shapes,
        input_output_aliases=rider.aliases, scratch_shapes=rider.sems,
    )(*rider.ins)


def _hosted_call(body, args, *, name, grid, in_specs, out_specs, out_shape, scratch_shapes=(), sem, rider=None,
                 prefetch=()):
    scratch_shapes = list(scratch_shapes)
    n_pf, n_in, n_out, n_sc = len(prefetch), len(in_specs), len(out_specs), len(scratch_shapes)
    r_in, r_out = (len(rider.ins), len(rider.out_shapes)) if rider else (0, 0)
    last = tuple(g - 1 for g in grid)

    def hosted(*refs):
        p = 0
        parts = []
        for cnt in (n_pf, n_in, r_in, n_out, r_out, n_sc):
            parts.append(refs[p:p + cnt])
            p += cnt
        pf, ins, r_ins, outs, r_outs, scratch = parts
        sems = refs[p:]
        ids = [pl.program_id(a) for a in range(len(grid))]
        is_first = functools.reduce(jnp.logical_and, [i == 0 for i in ids])
        is_last = functools.reduce(jnp.logical_and, [i == e for i, e in zip(ids, last)])

        @pl.when(is_first)
        def _():
            rider.start(r_ins, r_outs, sems)

        if rider.middle is not None:
            linear = functools.reduce(lambda acc, ig: acc * ig[1] + ig[0], zip(ids, grid), 0)

            @pl.when(linear == math.prod(grid) * 3 // 4)
            def _():
                rider.middle(r_ins, r_outs, sems)

        body(*pf, *ins, *outs, *scratch)

        @pl.when(is_last)
        def _():
            rider.finish(r_ins, r_outs, sems)

    if rider is None:
        kern, all_in, all_out, shapes, scratch, aliases, extra = body, list(in_specs), list(out_specs), list(out_shape), \
            scratch_shapes, {}, []
    else:
        kern, all_in, all_out = hosted, list(in_specs) + rider.in_specs, list(out_specs) + rider.out_specs
        shapes, scratch, extra = list(out_shape) + rider.out_shapes, scratch_shapes + rider.sems, rider.ins
        aliases = {n_pf + n_in + i: n_out + j for i, j in rider.aliases.items()}
        sem = ("arbitrary",) * len(grid)
    if prefetch:
        spec = dict(grid_spec=pltpu.PrefetchScalarGridSpec(
            num_scalar_prefetch=n_pf, grid=grid, in_specs=all_in, out_specs=all_out, scratch_shapes=scratch))
    else:
        spec = dict(grid=grid, in_specs=all_in, out_specs=all_out, scratch_shapes=scratch)
    res = pl.pallas_call(kern, name=name, out_shape=shapes, input_output_aliases=aliases,
                         compiler_params=_params(sem, VMEM_LIMIT), **spec)(*prefetch, *args, *extra)
    return list(res[:n_out]), list(res[n_out:])


def _gather_rider(ws, staged=False):
    n = len(ws)
    shapes = [jax.ShapeDtypeStruct(w.shape, w.dtype) for w in ws]
    sems = [pltpu.SemaphoreType.DMA((n, 3))] * 4
    aliases = {a: a for a in range(n)}

    def start(ins, outs, s):
        _gather_send(outs, s[0], s[1])

    if not staged:
        return _Rider(ws, shapes, sems, start, lambda ins, outs, s: _gather_finish(outs, *s), aliases=aliases)
    return _Rider(
        ws, shapes, sems, start, lambda ins, outs, s: _passed_on(outs, s[2], s[3]), aliases=aliases,
        middle=lambda ins, outs, s: _gather_landed(outs, s[0], s[1], functools.partial(_pass_on, outs, s[2], s[3])))


def _gather_ici_rider(ws):
    n = len(ws)
    return _Rider(
        ws, [jax.ShapeDtypeStruct(w.shape, w.dtype) for w in ws], [pltpu.SemaphoreType.DMA((n, 3))] * 2,
        lambda ins, outs, sems: _gather_send(outs, sems[0], sems[1]),
        lambda ins, outs, sems: _gather_landed(outs, sems[0], sems[1]),
        aliases={a: a for a in range(n)})


def _gather_d2d_rider(ws):
    n = len(ws)

    def start(ins, outs, sems):
        x, y, z = _place()
        for a, o in enumerate(outs):
            for k, (dx, dy) in enumerate(_CHIP_OFFSETS):
                _pass_on(outs, sems[0], sems[1], a, k, _half(o, 2 * _flip(x, dx) + _flip(y, dy), z))

    return _Rider(
        ws, [jax.ShapeDtypeStruct(w.shape, w.dtype) for w in ws], [pltpu.SemaphoreType.DMA((n, 3))] * 2,
        start, lambda ins, outs, sems: _passed_on(outs, sems[0], sems[1]), aliases={a: a for a in range(n)})


def _copies_rider(ins, out_shapes, sem_shape, make):
    def start(r_ins, r_outs, sems):
        for cp in make(r_ins, r_outs, sems[0], sems[1]):
            cp.start()

    def finish(r_ins, r_outs, sems):
        for cp in make(r_ins, r_outs, sems[0], sems[1]):
            cp.wait()

    return _Rider(ins, out_shapes, [pltpu.SemaphoreType.DMA(sem_shape)] * 2, start, finish)


def _exchange_rider(gs):
    def make(g_refs, r_refs, send, recv):
        x, y, z = _place()
        return [pltpu.make_async_remote_copy(
            src_ref=g.at[:, pl.ds((1 - z) * (g.shape[1] // 2), g.shape[1] // 2)], dst_ref=r, send_sem=send.at[a],
            recv_sem=recv.at[a], device_id=(x, y, 1 - z), device_id_type=MESH)
            for a, (g, r) in enumerate(zip(g_refs, r_refs))]

    shapes = [jax.ShapeDtypeStruct((g.shape[0], g.shape[1] // 2, g.shape[2]), g.dtype) for g in gs]
    return _copies_rider(gs, shapes, (len(gs),), make)


def _add_half(g, recv, core, name):
    s, r, c = g.shape
    r2 = r // 2
    rb = r2
    for cand in (512, 256, 128, 64):
        if r2 % cand == 0:
            rb = cand
            break
    g4 = g.reshape(s, 2, r2, c)

    def body(core_ref, g_ref, r_ref, o_ref):
        o_ref[...] = (g_ref[...].astype(F32) + r_ref[...].astype(F32)).astype(BF16)

    return pl.pallas_call(
        body, name=name,
        grid_spec=pltpu.PrefetchScalarGridSpec(
            num_scalar_prefetch=1, grid=(s, r2 // rb),
            in_specs=[pl.BlockSpec((None, None, rb, c), lambda i, j, cr: (i, cr[0], j, 0)),
                      pl.BlockSpec((None, rb, c), lambda i, j, cr: (i, j, 0))],
            out_specs=pl.BlockSpec((None, rb, c), lambda i, j, cr: (i, j, 0))),
        out_shape=jax.ShapeDtypeStruct((s, r2, c), BF16),
        compiler_params=_params(("parallel", "parallel")),
    )(core, g4, recv)


def _scatter_rider(ps):
    def make(p_refs, o_refs, send, recv):
        x, y, z = _place()
        copies = []
        for a, (p, o) in enumerate(zip(p_refs, o_refs)):
            for k, (dx, dy) in enumerate(_CHIP_OFFSETS):
                other = 2 * _flip(x, dx) + _flip(y, dy)
                copies.append(pltpu.make_async_remote_copy(
                    src_ref=p.at[other], dst_ref=o.at[k], send_sem=send.at[a, k], recv_sem=recv.at[a, k],
                    device_id=(_flip(x, dx), _flip(y, dy), z), device_id_type=MESH))
        return copies

    shapes = [jax.ShapeDtypeStruct((3,) + p.shape[1:], p.dtype) for p in ps]
    return _copies_rider(ps, shapes, (len(ps), 3), make)


def _sum_chips(p, landed, chip, name):
    _, r2, c = p.shape
    rb = r2
    for cand in (512, 256, 128, 64):
        if r2 % cand == 0:
            rb = cand
            break

    def body(s_ref, p_ref, l_ref, o_ref):
        acc = p_ref[...].astype(F32)
        for k in range(3):
            acc = acc + l_ref[k].astype(F32)
        o_ref[...] = acc

    return pl.pallas_call(
        body, name=name,
        grid_spec=pltpu.PrefetchScalarGridSpec(
            num_scalar_prefetch=1, grid=(r2 // rb,),
            in_specs=[pl.BlockSpec((None, rb, c), lambda i, s: (s[0], i, 0)),
                      pl.BlockSpec((3, rb, c), lambda i, s: (0, i, 0))],
            out_specs=pl.BlockSpec((rb, c), lambda i, s: (i, 0))),
        out_shape=jax.ShapeDtypeStruct((r2, c), F32),
        compiler_params=_params(("parallel",)),
    )(chip, p, landed)


def _swap_rider(hs):
    def make(h_refs, o_refs, send, recv):
        x, y, z = _place()
        return [pltpu.make_async_remote_copy(
            src_ref=h, dst_ref=o, send_sem=send.at[a], recv_sem=recv.at[a], device_id=(x, y, 1 - z),
            device_id_type=MESH) for a, (h, o) in enumerate(zip(h_refs, o_refs))]

    return _copies_rider(hs, [jax.ShapeDtypeStruct(h.shape, h.dtype) for h in hs], (len(hs),), make)


def _reduce_scatter_vmem(gs, rows, rider, name):
    n = len(gs)
    r_in, r_out = len(rider.ins), len(rider.out_shapes)
    halves = [(r // 2, g.shape[-1]) for g, (r, _) in zip(gs, rows)]
    piece_cols = 2 * LANES
    pieces = [(a, slice(c0, min(c0 + piece_cols, h[1]))) for a, h in enumerate(halves) for c0 in range(0, h[1], piece_cols)]
    n_p = len(pieces)

    def body(*refs):
        p = 0
        parts = []
        for cnt in (n, r_in, n, n, r_out, n, n, n, 6):
            parts.append(refs[p:p + cnt])
            p += cnt
        g_refs, r_ins, mine, theirs, r_outs, recv, part, land, sems = parts
        r_sems = refs[p:]
        xs, xr, ss, sr, ws, wr = sems
        x, y, z = _place()
        chip = 2 * x + y
        sib = (x, y, 1 - z)
        rider.start(r_ins, r_outs, r_sems)

        def half_of(a, s, which):
            r2 = halves[a][0]
            if len(g_refs[a].shape) == 3:
                return g_refs[a].at[s, pl.ds(pl.multiple_of(which * r2, 8), r2)]
            return g_refs[a].at[pl.ds(pl.multiple_of(s * rows[a][1] + which * r2, 8), r2)]

        def exchange(i):
            a, cols = pieces[i]
            return [pltpu.make_async_remote_copy(
                src_ref=half_of(a, s, 1 - z).at[:, cols], dst_ref=recv[a].at[s, :, cols], send_sem=xs.at[i, s],
                recv_sem=xr.at[i, s], device_id=sib, device_id_type=MESH) for s in range(N_CHIPS)]

        def scatter(i):
            a, cols = pieces[i]
            return [pltpu.make_async_remote_copy(
                src_ref=part[a].at[2 * _flip(x, dx) + _flip(y, dy), :, cols], dst_ref=land[a].at[k, :, cols],
                send_sem=ss.at[i, k], recv_sem=sr.at[i, k], device_id=(_flip(x, dx), _flip(y, dy), z),
                device_id_type=MESH) for k, (dx, dy) in enumerate(_CHIP_OFFSETS)]

        def swap(i):
            a, cols = pieces[i]
            return pltpu.make_async_remote_copy(
                src_ref=mine[a].at[:, cols], dst_ref=theirs[a].at[:, cols], send_sem=ws.at[i], recv_sem=wr.at[i],
                device_id=sib, device_id_type=MESH)

        for i in range(len(pieces)):
            for cp in exchange(i):
                cp.start()
        for i, (a, cols) in enumerate(pieces):
            for cp in exchange(i):
                cp.wait()
            for s in range(N_CHIPS):
                part[a][s, :, cols] = (half_of(a, s, z)[:, cols] + recv[a][s, :, cols]).astype(BF16)
            for cp in scatter(i):
                cp.start()
        for i, (a, cols) in enumerate(pieces):
            for cp in scatter(i):
                cp.wait()
            acc = part[a][chip, :, cols].astype(F32)
            for k in range(3):
                acc = acc + land[a][k, :, cols].astype(F32)
            mine[a][:, cols] = acc
            swap(i).start()
        for i in range(len(pieces)):
            swap(i).wait()
        rider.finish(r_ins, r_outs, r_sems)

    half_shapes = [jax.ShapeDtypeStruct(h, F32) for h in halves]
    res = pl.pallas_call(
        body, name=name, in_specs=[_VMEM] * n + rider.in_specs, out_specs=[_VMEM] * (2 * n) + rider.out_specs,
        out_shape=half_shapes + half_shapes + rider.out_shapes,
        scratch_shapes=[pltpu.VMEM((N_CHIPS,) + h, F32) for h in halves] + [pltpu.VMEM((N_CHIPS,) + h, BF16) for h in halves]
        + [pltpu.VMEM((3,) + h, BF16) for h in halves]
        + [pltpu.SemaphoreType.DMA((n_p, N_CHIPS))] * 2 + [pltpu.SemaphoreType.DMA((n_p, 3))] * 2
        + [pltpu.SemaphoreType.DMA((n_p,))] * 2 + rider.sems,
        input_output_aliases={n + i: 2 * n + j for i, j in rider.aliases.items()},
        compiler_params=_params(None, VMEM_LIMIT),
    )(*gs, *rider.ins)
    return list(res[:n]), list(res[n:2 * n]), list(res[2 * n:])


SMALL_ROWS = 32
PACK_ROWS = 16


def _pack_small(st_post, st_ret, st_pre):
    d = st_post.shape[2]

    def body(po_ref, re_ref, pr_ref, o_ref):
        o_ref[...] = jnp.zeros(o_ref.shape, F32)
        o_ref[0:1, :] = pr_ref[0, 2:3, :] + pr_ref[1, 2:3, :] + pr_ref[2, 2:3, :]
        o_ref[1:2, :] = po_ref[0, 4:5, :] + po_ref[1, 4:5, :]
        o_ref[2:3, :] = po_ref[0, 5:6, :] + po_ref[1, 5:6, :]
        o_ref[3:4, 0:512] = re_ref[0, 0:1, :] + re_ref[1, 0:1, :]
        o_ref[4:5, :] = pr_ref[0, 3:4, :] + pr_ref[1, 3:4, :] + pr_ref[2, 3:4, :]
        o_ref[5:6, :] = pr_ref[0, 4:5, :] + pr_ref[1, 4:5, :] + pr_ref[2, 4:5, :]
        lane = lax.broadcasted_iota(jnp.int32, (1, LANES), 1)
        for row, src in ((6, 1), (10, 2)):
            acc = jnp.zeros((1, LANES), F32)
            for hd in range(HEADS):
                grp = re_ref[0, src:src + 1, hd * LANES:(hd + 1) * LANES] + re_ref[1, src:src + 1, hd * LANES:(hd + 1) * LANES]
                acc = acc + jnp.where(lane == hd, grp, 0.0)
            o_ref[row:row + 1, 0:LANES] = acc
        o_ref[7:8, :] = po_ref[0, 6:7, :] + po_ref[1, 6:7, :]
        o_ref[8:9, :] = pr_ref[2, 0:1, :]
        o_ref[9:10, :] = pr_ref[2, 1:2, :]
        for e in range(2):
            b = 12 + 6 * e
            o_ref[b:b + 1, :] = pr_ref[e, 0:1, :]
            o_ref[b + 1:b + 2, :] = pr_ref[e, 1:2, :]
            o_ref[b + 2:b + 3, :] = po_ref[e, 3:4, :]
            o_ref[b + 3:b + 4, :] = po_ref[e, 0:1, :]
            o_ref[b + 4:b + 5, :] = po_ref[e, 1:2, :]
            o_ref[b + 5:b + 6, :] = po_ref[e, 2:3, :]

    return pl.pallas_call(body, name="pack_small", out_shape=jax.ShapeDtypeStruct((SMALL_ROWS, d), F32))(st_post, st_ret, st_pre)


def _small_reduce(gathered):
    d = gathered.shape[2]

    def body(g_ref, o_ref):
        tot = g_ref[0, 0:PACK_ROWS, :]
        for dev in range(1, N_DEV):
            tot = tot + g_ref[dev, 0:PACK_ROWS, :]
        o_ref[0:PACK_ROWS, :] = tot
        for j in range(6):
            acc = g_ref[0, 12 + j:13 + j, :] + g_ref[0, 18 + j:19 + j, :]
            for dev in range(1, N_DEV):
                acc = acc + g_ref[dev, 12 + j:13 + j, :] + g_ref[dev, 18 + j:19 + j, :]
            if j < 2:
                acc = acc + o_ref[8 + j:9 + j, :]
            o_ref[PACK_ROWS + j:PACK_ROWS + j + 1, :] = acc
        o_ref[PACK_ROWS + 6:PACK_ROWS + 8, :] = jnp.zeros((2, d), F32)

    return pl.pallas_call(body, name="small_reduce", out_shape=jax.ShapeDtypeStruct((PACK_ROWS + 8, d), F32))(gathered)


_SMALL = (("g_attn", 0, 1024), ("g_ffn", 1, 1024), ("g_final", 2, 1024), ("g_ret", 3, 512), ("g_q_lora", 4, 384),
          ("g_kv_lora", 5, 256), ("ret_decay_fwd", 6, HEADS), ("ret_decay_bwd", 10, HEADS))
_SMALL_NAMES = tuple(s[0] for s in _SMALL) + ("c_ctx", "b_ada")


def _small_final(tot, dcc, sg8, ws, ms, vs):
    d = tot.shape[1]
    n = len(_SMALL_NAMES)

    def body(*refs):
        t_ref, dcc_ref, sg_ref = refs[0:3]
        w_refs, m_refs, v_refs = refs[3:3 + n], refs[3 + n:3 + 2 * n], refs[3 + 2 * n:3 + 3 * n]
        outs = refs[3 + 3 * n:]
        g_refs, d_refs, mo_refs, vo_refs = outs[0:n], outs[n:2 * n], outs[2 * n:3 * n], outs[3 * n:4 * n]
        l_ref = outs[4 * n]

        def update(i, g, sl=None):
            pick = (lambda r: r[...]) if sl is None else (lambda r: r[:, sl])
            dl, mn, vn = _adam_math(pick(w_refs[i]), g, pick(m_refs[i]), pick(v_refs[i]))
            if sl is None:
                g_refs[i][...], d_refs[i][...], mo_refs[i][...], vo_refs[i][...] = g, dl, mn, vn
            else:
                g_refs[i][:, sl], d_refs[i][:, sl], mo_refs[i][:, sl], vo_refs[i][:, sl] = g, dl, mn, vn

        for i, (name, row, width) in enumerate(_SMALL):
            g = t_ref[row:row + 1, 0:width]
            if name == "ret_decay_fwd":
                g = g * sg_ref[0:1, 0:width]
            elif name == "ret_decay_bwd":
                g = g * sg_ref[1:2, 0:width]
            update(i, g)
        i_cc, i_b = n - 2, n - 1
        cc = w_refs[i_cc][...]
        s = 1.0 / (1.0 + jnp.exp(-cc))
        dsilu = dcc_ref[0, 0:1, :] + dcc_ref[2, 0:1, :] + dcc_ref[4, 0:1, :] + dcc_ref[6, 0:1, :]
        update(i_cc, dsilu * (s * (1.0 + cc * (1.0 - s))))
        for j in range(6):
            update(i_b, t_ref[PACK_ROWS + j:PACK_ROWS + j + 1, :], pl.ds(j * d, d))
        l_ref[...] = jnp.broadcast_to((0.5 / d) * jnp.sum(t_ref[7:8, :], keepdims=True), l_ref.shape)

    shapes = [jax.ShapeDtypeStruct(a.shape, F32) for a in ws]
    outs = pl.pallas_call(
        body, name="small_final", out_shape=shapes * 4 + [jax.ShapeDtypeStruct((8, LANES), F32)],
    )(tot, dcc, sg8, *ws, *ms, *vs)
    return outs[0:n], outs[n:2 * n], outs[2 * n:3 * n], outs[3 * n:4 * n], outs[4 * n]


_WEIGHTS = ("c_ctx", "w_ada", "b_ada", "g_attn", "g_ffn", "w_in", "ret_decay_fwd", "ret_decay_bwd", "g_ret", "g_q_lora",
            "w_uq", "g_kv_lora", "w_ukv", "w_out", "w_ff1", "w_ff2", "g_final")
_BIG = ("w_in", "w_uq", "w_ukv", "w_out", "w_ff1", "w_ff2")
_TRANSPOSED = ("w_in", "w_uq")


def kernel(x, c, ctx, c_ctx, w_ada, b_ada, g_attn, g_ffn, w_in, ret_decay_fwd, ret_decay_bwd, g_ret, g_q_lora, w_uq, g_kv_lora, w_ukv, w_out, w_ff1, w_ff2, g_final, loss_target, m_c_ctx, m_w_ada, m_b_ada, m_g_attn, m_g_ffn, m_w_in, m_ret_decay_fwd, m_ret_decay_bwd, m_g_ret, m_g_q_lora, m_w_uq, m_g_kv_lora, m_w_ukv, m_w_out, m_w_ff1, m_w_ff2, m_g_final, v_c_ctx, v_w_ada, v_b_ada, v_g_attn, v_g_ffn, v_w_in, v_ret_decay_fwd, v_ret_decay_bwd, v_g_ret, v_g_q_lora, v_w_uq, v_g_kv_lora, v_w_ukv, v_w_out, v_w_ff1, v_w_ff2, v_g_final):
    w = dict(c_ctx=c_ctx, w_ada=w_ada, b_ada=b_ada, g_attn=g_attn, g_ffn=g_ffn, w_in=w_in, ret_decay_fwd=ret_decay_fwd,
             ret_decay_bwd=ret_decay_bwd, g_ret=g_ret, g_q_lora=g_q_lora, w_uq=w_uq, g_kv_lora=g_kv_lora, w_ukv=w_ukv,
             w_out=w_out, w_ff1=w_ff1, w_ff2=w_ff2, g_final=g_final)
    m = dict(c_ctx=m_c_ctx, w_ada=m_w_ada, b_ada=m_b_ada, g_attn=m_g_attn, g_ffn=m_g_ffn, w_in=m_w_in,
             ret_decay_fwd=m_ret_decay_fwd, ret_decay_bwd=m_ret_decay_bwd, g_ret=m_g_ret, g_q_lora=m_g_q_lora, w_uq=m_w_uq,
             g_kv_lora=m_g_kv_lora, w_ukv=m_w_ukv, w_out=m_w_out, w_ff1=m_w_ff1, w_ff2=m_w_ff2, g_final=m_g_final)
    v = dict(c_ctx=v_c_ctx, w_ada=v_w_ada, b_ada=v_b_ada, g_attn=v_g_attn, g_ffn=v_g_ffn, w_in=v_w_in,
             ret_decay_fwd=v_ret_decay_fwd, ret_decay_bwd=v_ret_decay_bwd, g_ret=v_g_ret, g_q_lora=v_g_q_lora, w_uq=v_w_uq,
             g_kv_lora=v_g_kv_lora, w_ukv=v_w_ukv, w_out=v_w_out, w_ff1=v_w_ff1, w_ff2=v_w_ff2, g_final=v_g_final)
    xi, yi, ci = lax.axis_index("x"), lax.axis_index("y"), lax.axis_index("c")
    chip = 2 * xi + yi
    dev = 2 * chip + ci
    nex, seq, d = x.shape
    n_ada = w_ada.shape[2]

    dec = jnp.zeros((8, LANES), F32).at[0, :HEADS].set(ret_decay_fwd[0]).at[1, :HEADS].set(ret_decay_bwd[0])
    lg8, sg8 = _decay_prep(dec)
    lg = lg8[:2, :HEADS]

    def shard_of(t, k):
        return t[k][0].T if k in _TRANSPOSED else t[k][0]

    shard = {k: shard_of(w, k) for k in _BIG}
    head_rows = MLA_NOPE + MLA_ROPE
    shard["w_uq"] = jnp.pad(shard["w_uq"], ((0, MLA_HEAD - head_rows), (0, 0)))
    slot = chip.reshape(1).astype(jnp.int32)
    core = ci.reshape(1).astype(jnp.int32)
    slots = {k: _cast_into_slot(shard[k], slot, "cast_" + k) for k in _EARLY}
    half_ff = shard["w_ff2"].shape[0] // 2
    late_pieces = [(shard["w_out"], 0, shard["w_out"].shape[0]), (shard["w_ff1"], 0, shard["w_ff1"].shape[0]),
                   (shard["w_ff2"], 0, half_ff), (shard["w_ff2"], half_ff, half_ff)]
    late_slots, (w_in_f, w_uq_k, w_ukv_k, c8) = _cast_into_slots(
        late_pieces, slot, "cast_late",
        rider=_merge_riders(_gather_rider([slots[k] for k in _EARLY]),
                            _gather8_rider(jnp.pad(c, ((0, 8 - nex), (0, 0))), in_vmem=False)))

    a_in = jnp.concatenate([c8[:, :nex].reshape(N_DEV * nex, d), c_ctx.reshape(1, d), jnp.zeros((7, d), F32)], axis=0)
    b_sh = lax.dynamic_slice(b_ada, (0, chip * n_ada), (1, n_ada))
    mod_sh = _mod_fwd(a_in, w_ada[0], b_sh)
    mod8 = _allgather8(mod_sh, "ag_mod")
    w_in_k = jnp.pad(w_in_f.reshape(IN_COLS, d), ((0, IN_PAD - IN_COLS), (0, 0)))
    mod_all = mod8[0::2].transpose(1, 0, 2).reshape(a_in.shape[0], N_CHIPS * n_ada)
    mod_me = lax.dynamic_slice(mod_all, (nex * dev, 0), (nex, N_CHIPS * n_ada)).reshape(nex, 6, d)
    mod_c = mod_all[N_DEV * nex].reshape(1, 6, d)
    modv = jnp.pad(jnp.concatenate([mod_me, mod_c], axis=0), ((0, 0), (0, 2), (0, 0)))

    gx, g_early, late, st_post, st_ret, st_pre = _local_step(
        x, ctx, loss_target, modv, lg, g_attn, g_ffn, g_final.reshape(1, d), g_ret, g_q_lora, g_kv_lora,
        w_in_k, w_uq_k, w_ukv_k, late_slots, (core, slot))

    mine, theirs, (*late_theirs, gathered) = _reduce_scatter_vmem(
        g_early, [(IN_COLS // N_CHIPS, IN_COLS // N_CHIPS), (head_rows, MLA_HEAD), (KV_LORA, KV_LORA)],
        _merge_riders(_swap_rider(late), _gather8_rider(_pack_small(st_post, st_ret, st_pre))), "rs_early")
    tot = _small_reduce(gathered)
    dm = jnp.concatenate([
        gathered[:, 12:24].reshape(N_DEV * nex, 6 * d),
        jnp.concatenate([tot[8:10].reshape(1, 2 * d), jnp.zeros((1, 4 * d), F32)], axis=1),
        jnp.zeros((7, 6 * d), F32)], axis=0)
    dm_sh = lax.dynamic_slice(dm, (0, chip * n_ada), (dm.shape[0], n_ada))
    g_ada, da = _mod_bwd(a_in, dm_sh, w_ada[0])
    dcc = _allgather8(da[N_DEV * nex:], "ag_dcc")
    halves = dict(zip(_EARLY, zip(mine, theirs)))
    halves.update(zip(_LATE, zip(late, late_theirs)))
    grad, delta, new_m, new_v = {}, {}, {}, {}
    for k in _BIG:
        a, b = halves[k]
        res = _adamw_halves(shard_of(w, k), a, b, shard_of(m, k), shard_of(v, k), core, "adamw_" + k)
        grad[k], delta[k], new_m[k], new_v[k] = [(o.T if k in _TRANSPOSED else o).reshape(w[k].shape) for o in res]

    shp = w_ada.shape
    outs, _ = _adamw(w_ada[0], g_ada, m["w_ada"][0], v["w_ada"][0], "adamw_w_ada")
    grad["w_ada"] = g_ada.reshape(shp)
    delta["w_ada"], new_m["w_ada"], new_v["w_ada"] = [o.reshape(shp) for o in outs]
    rows = [{k: t[k].reshape(1, -1) for k in _SMALL_NAMES} for t in (w, m, v)]
    small = _small_final(tot, dcc, sg8, *[[t[k] for k in _SMALL_NAMES] for t in rows])
    for res, outs in zip((grad, delta, new_m, new_v), small[:4]):
        for k, o in zip(_SMALL_NAMES, outs):
            res[k] = o.reshape(w[k].shape)
    return (small[4][0, 0], gx, *[grad[k] for k in _WEIGHTS], *[delta[k] for k in _WEIGHTS],
            *[new_m[k] for k in _WEIGHTS], *[new_v[k] for k in _WEIGHTS])
```

```python
import functools
import math

import jax
import jax.numpy as jnp
from jax import lax
from jax.experimental import pallas as pl
from jax.experimental.pallas import tpu as pltpu

F32 = jnp.float32
BF16 = jnp.bfloat16
MESH = pl.DeviceIdType.MESH

EPS = 1e-6
D_MODEL = 1024
D_FF = 4096
HEADS = 4
RET_DK = 64
RET_DV = 128
MLA_NOPE = 128
MLA_ROPE = 64
MLA_HEAD = 256
Q_LORA = 384
KV_LORA = 256
GRID_W = 64
ROPE_BASE = 10000.0
IN_COLS = 2240
IN_PAD = 2304
PG_COLS = 1152
N_CHIPS = 4
N_DEV = 8
LANES = 128
ADAM_LR = 0.001
ADAM_B1 = 0.9
ADAM_B2 = 0.999
ADAM_EPS = 1e-08
ADAM_WD = 0.01
ADAM_STEP = 10
VMEM_LIMIT = 56 * 1024 * 1024


def _dot(a, b):
    return jnp.dot(a, b, preferred_element_type=F32)


def _dot_nt(a, b):
    return lax.dot_general(a, b, (((1,), (1,)), ((), ())), preferred_element_type=F32)


def _dot_tn(a, b):
    return lax.dot_general(a, b, (((0,), (0,)), ((), ())), preferred_element_type=F32)


def _params(sem=None, vmem=None):
    return pltpu.CompilerParams(dimension_semantics=sem, vmem_limit_bytes=vmem)


def _full(shape):
    n = len(shape)
    return pl.BlockSpec(shape, lambda *_: (0,) * n)


def _once(shape):
    n = len(shape)
    return pl.BlockSpec(shape, lambda *_: (0,) * n, pipeline_mode=pl.Buffered(1))


def _rope(x, cos, sin):
    w = x.shape[-1]
    lo = (lax.broadcasted_iota(jnp.int32, (1, w), 1) % 64) < 32
    swapped = jnp.where(lo, pltpu.roll(x, w - 32, 1), pltpu.roll(x, 32, 1))
    return x * cos + swapped * sin


def _rope_t(g, cos, sin):
    w = g.shape[-1]
    lo = (lax.broadcasted_iota(jnp.int32, (1, w), 1) % 64) < 32
    t = g * sin
    swapped = jnp.where(lo, pltpu.roll(t, w - 32, 1), pltpu.roll(t, 32, 1))
    return g * cos + swapped


def _rope_tables(seq, tm):
    rows = seq // GRID_W
    row = jnp.repeat(jnp.arange(rows, dtype=F32), GRID_W)
    col = jnp.tile(jnp.arange(GRID_W, dtype=F32), rows)
    n_freq = RET_DK // 4
    freq = ROPE_BASE ** (-jnp.arange(n_freq, dtype=F32) / n_freq)
    ang = jnp.concatenate([row[:, None] * freq, col[:, None] * freq], axis=-1)
    cos, sin = jnp.cos(ang), jnp.sin(ang)
    cos_t = jnp.tile(jnp.concatenate([cos, cos], -1), (1, HEADS))
    sin_t = jnp.tile(jnp.concatenate([-sin, sin], -1), (1, HEADS))
    cos_t = jnp.concatenate([cos_t, jnp.ones((tm, 4 * RET_DK), F32)], 0)
    sin_t = jnp.concatenate([sin_t, jnp.zeros((tm, 4 * RET_DK), F32)], 0)
    return cos_t, sin_t


def _adam_math(w, g, m, v):
    mn = ADAM_B1 * m + (1.0 - ADAM_B1) * g
    vn = ADAM_B2 * v + (1.0 - ADAM_B2) * (g * g)
    m_hat = mn / (1.0 - ADAM_B1 ** ADAM_STEP)
    v_hat = vn / (1.0 - ADAM_B2 ** ADAM_STEP)
    return -ADAM_LR * (m_hat / (jnp.sqrt(v_hat) + ADAM_EPS) + ADAM_WD * w), mn, vn


def _cast_into_slots(pieces, slot, name, rider=None):
    c = pieces[0][0].shape[1]
    rb = max(b for b in range(16, 1025, 16) if b * c * 4 <= (5 << 19)
             and all(cnt % b == 0 and st % b == 0 for _, st, cnt in pieces))
    nbs = [cnt // rb for _, _, cnt in pieces]
    starts = [sum(nbs[:s]) for s in range(len(pieces))]

    def body(s_ref, *refs):
        i = pl.program_id(0)
        for s in range(len(pieces)):
            @pl.when(jnp.logical_and(i >= starts[s], i < starts[s] + nbs[s]))
            def _():
                refs[len(pieces) + s][...] = refs[s][...].astype(BF16)

    in_specs, out_specs = [], []
    for (_, first_row, _), nb, st in zip(pieces, nbs, starts):
        in_specs.append(pl.BlockSpec((rb, c), lambda i, s, nb=nb, st=st, f=first_row // rb: (f + jnp.clip(i - st, 0, nb - 1), 0)))
        out_specs.append(pl.BlockSpec((None, rb, c), lambda i, s, nb=nb, st=st: (s[0], jnp.clip(i - st, 0, nb - 1), 0)))
    return _hosted_call(
        body, [w for w, _, _ in pieces], name=name, grid=(sum(nbs),), prefetch=(slot,), in_specs=in_specs,
        out_specs=out_specs, out_shape=[jax.ShapeDtypeStruct((N_CHIPS, cnt, c), BF16) for _, _, cnt in pieces],
        sem=("arbitrary",), rider=rider)


def _cast_into_slot(w, slot, name):
    return _cast_into_slots([(w, 0, w.shape[0])], slot, name)[0][0]


def _adamw_halves(w, mine, theirs, m, v, core, name):
    r, c = w.shape
    r2 = r // 2
    rb = max(b for b in range(8, r2 + 1, 8) if r2 % b == 0 and b * c * 4 <= (1 << 21))
    nbh = r2 // rb

    def body(z_ref, w_ref, a_ref, b_ref, m_ref, v_ref, g_ref, d_ref, mo_ref, vo_ref):
        here = (pl.program_id(0) // nbh) == z_ref[0]
        gg = jnp.where(here, a_ref[...], b_ref[...])
        g_ref[...] = gg
        d_ref[...], mo_ref[...], vo_ref[...] = _adam_math(w_ref[...], gg, m_ref[...], v_ref[...])

    spec = pl.BlockSpec((rb, c), lambda i, z: (i, 0))
    a_spec = pl.BlockSpec((rb, c), lambda i, z: (jnp.clip(i - z[0] * nbh, 0, nbh - 1), 0))
    b_spec = pl.BlockSpec((rb, c), lambda i, z: (jnp.clip(i - (1 - z[0]) * nbh, 0, nbh - 1), 0))
    shp = jax.ShapeDtypeStruct((r, c), F32)
    return pl.pallas_call(
        body, name=name,
        grid_spec=pltpu.PrefetchScalarGridSpec(
            num_scalar_prefetch=1, grid=(r // rb,), in_specs=[spec, a_spec, b_spec, spec, spec], out_specs=[spec] * 4),
        out_shape=[shp] * 4,
        compiler_params=_params(("parallel",)),
    )(core, w, mine, theirs, m, v)


def _adamw(w, g, m, v, name, rider=None):
    r, c = w.shape
    rb = r
    for cand in (512, 256, 128, 64, 32, 16, 8):
        if r % cand == 0 and cand * c * 4 <= (1 << 21):
            rb = cand
            break
    if r * c * 4 <= (1 << 21):
        rb = r

    def body(w_ref, g_ref, m_ref, v_ref, d_ref, mo_ref, vo_ref):
        d_ref[...], mo_ref[...], vo_ref[...] = _adam_math(w_ref[...], g_ref[...], m_ref[...], v_ref[...])

    spec = pl.BlockSpec((rb, c), lambda i: (i, 0))
    shp = jax.ShapeDtypeStruct((r, c), F32)
    return _hosted_call(
        body, (w, g, m, v), name=name, grid=(r // rb,), in_specs=[spec] * 4, out_specs=[spec] * 3, out_shape=[shp] * 3,
        sem=("parallel",), rider=rider)


def _decay_prep(dec):
    def body(d_ref, lg_ref, sg_ref):
        d = d_ref[...]
        lg_ref[...] = jnp.minimum(d, 0.0) - jnp.log(1.0 + jnp.exp(-jnp.abs(d)))
        sg_ref[...] = 1.0 / (1.0 + jnp.exp(d))

    shp = jax.ShapeDtypeStruct(dec.shape, F32)
    return pl.pallas_call(body, name="decay_prep", out_shape=[shp, shp])(dec)


def _mod_fwd(a_in, w_ada, b_sh):
    rows, d = a_in.shape
    n = w_ada.shape[1]
    bn = 512

    def body(a_ref, w_ref, b_ref, o_ref):
        a = a_ref[...]
        s = (a / (1.0 + jnp.exp(-a))).astype(BF16)
        o_ref[...] = _dot(s, w_ref[...].astype(BF16)) + b_ref[...]

    return pl.pallas_call(
        body, name="mod_fwd", grid=(n // bn,),
        in_specs=[_full((rows, d)), pl.BlockSpec((d, bn), lambda j: (0, j)), pl.BlockSpec((1, bn), lambda j: (0, j))],
        out_specs=pl.BlockSpec((rows, bn), lambda j: (0, j)),
        out_shape=jax.ShapeDtypeStruct((rows, n), F32),
        compiler_params=_params(("parallel",)),
    )(a_in, w_ada, b_sh)


def _mod_bwd(a_in, dm, w_ada):
    rows, d = a_in.shape
    n = w_ada.shape[1]
    bn = 512
    nb = n // bn

    def body(a_ref, dm_ref, w_ref, gw_ref, da_ref):
        j = pl.program_id(0)
        a = a_ref[...]
        s = (a / (1.0 + jnp.exp(-a))).astype(BF16)
        dmb = dm_ref[...].astype(BF16)
        gw_ref[...] = _dot_tn(s, dmb)
        part = _dot_nt(dmb, w_ref[...].astype(BF16))

        @pl.when(j == 0)
        def _():
            da_ref[...] = part

        @pl.when(j > 0)
        def _():
            da_ref[...] += part

    return pl.pallas_call(
        body, name="mod_bwd", grid=(nb,),
        in_specs=[_full((rows, d)), pl.BlockSpec((rows, bn), lambda j: (0, j)), pl.BlockSpec((d, bn), lambda j: (0, j))],
        out_specs=[pl.BlockSpec((d, bn), lambda j: (0, j)), _full((rows, d))],
        out_shape=[jax.ShapeDtypeStruct((d, n), F32), jax.ShapeDtypeStruct((rows, d), F32)],
        compiler_params=_params(("arbitrary",)),
    )(a_in, dm, w_ada)


def _pre_fwd(x2, ctx2, modv, g_attn, w_in, g_q, g_kv, w_uq, w_ukv, cos_t, sin_t, *, seq, tm, rider=None):
    t_lat, d = x2.shape
    t_ctx = ctx2.shape[0]
    nl, nc = t_lat // tm, t_ctx // tm
    n_all = t_lat + t_ctx
    tpe = seq // tm
    nex = t_lat // seq

    def body(x_ref, c_ref, mod_ref, g_ref, win_ref, gq_ref, gkv_ref, wuq_ref, wukv_ref, cos_ref, sin_ref,
             h_ref, pg_ref, rq_ref, rk_ref, rv_ref, nq_ref, nkv_ref, q_ref, k_ref, v_ref):
        i = pl.program_id(0)
        xt = jnp.where(i < nl, x_ref[...], c_ref[...])
        sh = mod_ref[0, 0:1, :]
        sc = mod_ref[0, 1:2, :]
        r = lax.rsqrt(jnp.mean(xt * xt, axis=-1, keepdims=True) + EPS)
        hb = ((xt * r) * g_ref[...] * (1.0 + sc) + sh).astype(BF16)
        h_ref[...] = hb
        p = _dot_nt(hb, win_ref[...])
        cos = cos_ref[...]
        sin = sin_ref[...]
        rq_ref[...] = _rope(p[:, 0:256], cos, sin).astype(BF16)
        rk_ref[...] = _rope(p[:, 256:512] * (RET_DK ** -0.5), cos, sin).astype(BF16)
        rv_ref[...] = p[:, 512:1024].astype(BF16)
        pg_ref[...] = p[:, 1024:2176]
        cq = p[:, 1536:1920]
        ckv = p[:, 1920:2176]
        nqb = (cq * lax.rsqrt(jnp.mean(cq * cq, axis=-1, keepdims=True) + EPS) * gq_ref[...]).astype(BF16)
        nkvb = (ckv * lax.rsqrt(jnp.mean(ckv * ckv, axis=-1, keepdims=True) + EPS) * gkv_ref[...]).astype(BF16)
        nq_ref[...] = nqb
        nkv_ref[...] = nkvb
        cos1 = cos[:, 0:LANES]
        sin1 = sin[:, 0:LANES]
        kpe = _rope(p[:, 2176:2304], cos1, sin1).astype(BF16)
        for hd in range(HEADS):
            o = hd * MLA_HEAD
            qh = _dot_nt(nqb, wuq_ref[hd]) * MLA_SCALE
            q_ref[:, o:o + 128] = qh[:, 0:128].astype(BF16)
            q_ref[:, o + 128:o + 256] = _rope(qh[:, 128:256], cos1, sin1).astype(BF16)
            kvh = _dot(nkvb, wukv_ref[hd])
            k_ref[:, o:o + 128] = kvh[:, 0:128].astype(BF16)
            k_ref[:, o + 128:o + 256] = kpe
            v_ref[:, hd * 128:(hd + 1) * 128] = kvh[:, 128:256].astype(BF16)

    def tile(width):
        return pl.BlockSpec((tm, width), lambda i: (i, 0))

    widths = (d, PG_COLS, 256, 256, 512, Q_LORA, KV_LORA, HEADS * MLA_HEAD, HEADS * MLA_HEAD, HEADS * 128)
    dtypes = (BF16, F32, BF16, BF16, BF16, BF16, BF16, BF16, BF16, BF16)
    tab = pl.BlockSpec((tm, 256), lambda i: (jnp.where(i < nl, i % tpe, tpe), 0))
    return _hosted_call(
        body, (x2, ctx2, modv, g_attn, w_in, g_q, g_kv, w_uq, w_ukv, cos_t, sin_t), name="pre_fwd", grid=(nl + nc,),
        in_specs=[
            pl.BlockSpec((tm, d), lambda i: (jnp.minimum(i, nl - 1), 0)),
            pl.BlockSpec((tm, d), lambda i: (jnp.maximum(i - nl, 0), 0)),
            pl.BlockSpec((1, 8, d), lambda i: (jnp.minimum(i // tpe, nex), 0, 0)),
            _full((1, d)), _full(w_in.shape), _full((1, Q_LORA)), _full((1, KV_LORA)),
            _full(w_uq.shape), _full(w_ukv.shape), tab, tab,
        ],
        out_specs=[tile(w) for w in widths],
        out_shape=[jax.ShapeDtypeStruct((n_all, w), dt) for w, dt in zip(widths, dtypes)],
        sem=("parallel",), rider=rider)


def _post(yret, ymla, x2, tgt2, modv, g_ffn, g_fin, w_out, w_ff1, w_ff2a, w_ff2b, *, seq, tm):
    t_lat, d = x2.shape
    nl = t_lat // tm
    tpe = seq // tm
    nex = t_lat // seq
    n_slab = w_ff1.shape[0]
    fs = w_ff1.shape[2]
    fh = w_ff2a.shape[1]

    def body(yr_ref, ym_ref, x_ref, t_ref, mod_ref, gf_ref, gl_ref, wo_ref, w1_ref, w2a_ref, w2b_ref,
             mix_ref, a_ref, du_ref, h2_ref, df_ref, dmo_ref, dmix_ref, dxm_ref, st_ref, ru_ref):
        i = pl.program_id(0)
        gt_a = mod_ref[0, 2:3, :]
        sh_f = mod_ref[0, 3:4, :]
        sc_f = mod_ref[0, 4:5, :]
        gt_f = mod_ref[0, 5:6, :]
        g_ffn_v = gf_ref[...]
        g_fin_v = gl_ref[...]
        yr = yr_ref[...]
        ym = ym_ref[...]
        mix_ref[:, 0:512] = yr
        mix_ref[:, 512:1024] = ym
        op = _dot(yr, wo_ref[0:512, :]) + _dot(ym, wo_ref[512:1024, :])
        x_mid = x_ref[...] + gt_a * op
        r2 = lax.rsqrt(jnp.mean(x_mid * x_mid, axis=-1, keepdims=True) + EPS)
        xh2 = x_mid * r2
        h2b = (xh2 * g_ffn_v * (1.0 + sc_f) + sh_f).astype(BF16)
        h2_ref[...] = h2b
        f = jnp.zeros((tm, d), F32)
        for s in range(n_slab):
            ru = jnp.maximum(_dot(h2b, w1_ref[s]), 0.0)
            ru_ref[:, s * fs:(s + 1) * fs] = ru
            ab = (ru * ru).astype(BF16)
            a_ref[:, s * fs:(s + 1) * fs] = ab
            f = f + _dot(ab[:, 0:fh], w2a_ref[s]) + _dot(ab[:, fh:fs], w2b_ref[s])
        x_out = x_mid + gt_f * f
        r3 = lax.rsqrt(jnp.mean(x_out * x_out, axis=-1, keepdims=True) + EPS)
        xh3 = x_out * r3
        err = xh3 * g_fin_v - t_ref[...]
        dy = err * (1.0 / d)
        dxh3 = dy * g_fin_v
        dx_out = r3 * (dxh3 - xh3 * jnp.mean(dxh3 * xh3, axis=-1, keepdims=True))
        dfb = (dx_out * gt_f).astype(BF16)
        df_ref[...] = dfb
        dh2 = jnp.zeros((tm, d), F32)
        for s in range(n_slab):
            da = jnp.concatenate([_dot_nt(dfb, w2a_ref[s]), _dot_nt(dfb, w2b_ref[s])], axis=1)
            dub = (da * (2.0 * ru_ref[:, s * fs:(s + 1) * fs])).astype(BF16)
            du_ref[:, s * fs:(s + 1) * fs] = dub
            dh2 = dh2 + _dot_nt(dub, w1_ref[s])
        dxh2 = dh2 * (1.0 + sc_f) * g_ffn_v
        dx_mid = dx_out + r2 * (dxh2 - xh2 * jnp.mean(dxh2 * xh2, axis=-1, keepdims=True))
        dxm_ref[...] = dx_mid
        dmob = (dx_mid * gt_a).astype(BF16)
        dmo_ref[...] = dmob
        dmix_ref[...] = _dot_nt(dmob, wo_ref[...]).astype(BF16)

        def rsum(v):
            return jnp.sum(v, axis=0, keepdims=True)

        stats = jnp.concatenate([
            rsum(dh2), rsum(dh2 * xh2 * g_ffn_v), rsum(dx_out * f), rsum(dx_mid * op),
            rsum(dh2 * (1.0 + sc_f) * xh2), rsum(dy * xh3), rsum(err * err), jnp.zeros((1, d), F32)], axis=0)

        @pl.when(i % tpe == 0)
        def _():
            st_ref[0] = stats

        @pl.when(i % tpe != 0)
        def _():
            st_ref[0] += stats

    def tile(width):
        return pl.BlockSpec((tm, width), lambda i: (i, 0))

    widths = (d, D_FF, D_FF, d, d, d, d, d)
    dtypes = (BF16, BF16, BF16, BF16, BF16, BF16, BF16, F32)
    const = pl.Buffered(1)
    return pl.pallas_call(
        body, name="post", grid=(nl,),
        in_specs=[
            tile(512), tile(512), tile(d), tile(d),
            pl.BlockSpec((1, 8, d), lambda i: (i // tpe, 0, 0)),
            _full((1, d)), _full((1, d)),
            pl.BlockSpec(w_out.shape, lambda i: (0, 0), pipeline_mode=const),
            pl.BlockSpec(w_ff1.shape, lambda i: (0, 0, 0), pipeline_mode=const),
            pl.BlockSpec(w_ff2a.shape, lambda i: (0, 0, 0), pipeline_mode=const),
            pl.BlockSpec(w_ff2b.shape, lambda i: (0, 0, 0), pipeline_mode=const),
        ],
        out_specs=[tile(w) for w in widths] + [pl.BlockSpec((1, 8, d), lambda i: (i // tpe, 0, 0))],
        out_shape=[jax.ShapeDtypeStruct((t_lat, w), dt) for w, dt in zip(widths, dtypes)]
        + [jax.ShapeDtypeStruct((nex, 8, d), F32)],
        scratch_shapes=[pltpu.VMEM((tm, D_FF), F32)],
        compiler_params=_params(("arbitrary",), VMEM_LIMIT),
    )(yret, ymla, x2, tgt2, modv, g_ffn, g_fin, w_out, w_ff1, w_ff2a, w_ff2b)


def _pre_bwd(x2, ctx2, modv, g_attn, pg, drq, drk, dkc_r, drv, dvc_r, drg, dq_m, dkl, dkc, dvl, dvc, dxm,
             w_in, g_q, g_kv, w_uq, w_ukv, cos_t, sin_t, *, seq, tm, rider=None):
    t_lat, d = x2.shape
    t_ctx = ctx2.shape[0]
    nl, nc = t_lat // tm, t_ctx // tm
    n_all = t_lat + t_ctx
    tpe = seq // tm
    nex = t_lat // seq

    def body(x_ref, c_ref, mod_ref, g_ref, pg_ref, drq_ref, drk_ref, dkcr_ref, drv_ref, dvcr_ref, drg_ref,
             dq_ref, dkl_ref, dkc_ref, dvl_ref, dvc_ref, dxm_ref, win_ref, gq_ref, gkv_ref, wuq_ref, wukv_ref,
             cos_ref, sin_ref, dpb_ref, dqf_ref, dkvf_ref, gx_ref, st_ref):
        i = pl.program_id(0)
        lat = i < nl
        latf = lat.astype(F32)
        cos = cos_ref[...]
        sin = sin_ref[...]
        cos1 = cos[:, 0:LANES]
        sin1 = sin[:, 0:LANES]
        d_rq = _rope_t(drq_ref[...] * latf, cos, sin)
        d_rk = _rope_t(jnp.where(lat, drk_ref[...], dkcr_ref[...]), cos, sin) * (RET_DK ** -0.5)
        d_rv = jnp.where(lat, drv_ref[...], dvcr_ref[...])
        d_rg = drg_ref[...] * latf
        dq_all = dq_ref[...] * (latf * MLA_SCALE)
        dk_all = jnp.where(lat, dkl_ref[...], dkc_ref[...])
        dv_all = jnp.where(lat, dvl_ref[...], dvc_ref[...])
        dnq = jnp.zeros((tm, Q_LORA), F32)
        dnkv = jnp.zeros((tm, KV_LORA), F32)
        dkpe = jnp.zeros((tm, LANES), F32)
        for hd in range(HEADS):
            o = hd * MLA_HEAD
            dqh = jnp.concatenate([dq_all[:, o:o + 128], _rope_t(dq_all[:, o + 128:o + 256], cos1, sin1)],
                                  axis=1).astype(BF16)
            dqf_ref[:, o:o + 256] = dqh
            dnq = dnq + _dot(dqh, wuq_ref[hd])
            dkpe = dkpe + dk_all[:, o + 128:o + 256]
            dkvh = jnp.concatenate([dk_all[:, o:o + 128], dv_all[:, hd * 128:(hd + 1) * 128]], axis=1).astype(BF16)
            dkvf_ref[:, o:o + 256] = dkvh
            dnkv = dnkv + _dot_nt(dkvh, wukv_ref[hd])
        d_kpe = _rope_t(dkpe, cos1, sin1)
        pgv = pg_ref[...]
        cq = pgv[:, 512:896]
        ckv = pgv[:, 896:1152]
        rq_ = lax.rsqrt(jnp.mean(cq * cq, axis=-1, keepdims=True) + EPS)
        cqh = cq * rq_
        dcqh = dnq * gq_ref[...]
        d_cq = rq_ * (dcqh - cqh * jnp.mean(dcqh * cqh, axis=-1, keepdims=True))
        rkv_ = lax.rsqrt(jnp.mean(ckv * ckv, axis=-1, keepdims=True) + EPS)
        ckvh = ckv * rkv_
        dckvh = dnkv * gkv_ref[...]
        d_ckv = rkv_ * (dckvh - ckvh * jnp.mean(dckvh * ckvh, axis=-1, keepdims=True))
        dpb = jnp.concatenate([d_rq, d_rk, d_rv, d_rg, d_cq, d_ckv, d_kpe], axis=1).astype(BF16)
        dpb_ref[...] = dpb
        dh = _dot(dpb, win_ref[...])
        xt = jnp.where(lat, x_ref[...], c_ref[...])
        sc = mod_ref[0, 1:2, :]
        g = g_ref[...]
        r = lax.rsqrt(jnp.mean(xt * xt, axis=-1, keepdims=True) + EPS)
        xh = xt * r
        dxh = dh * (1.0 + sc) * g
        dx = r * (dxh - xh * jnp.mean(dxh * xh, axis=-1, keepdims=True))

        @pl.when(lat)
        def _():
            gx_ref[...] = dxm_ref[...] + dx

        def rsum(v):
            return jnp.sum(v, axis=0, keepdims=True)

        def widen(v):
            return jnp.concatenate([v, jnp.zeros((1, d - v.shape[1]), F32)], axis=1)

        stats = jnp.concatenate([
            rsum(dh), rsum(dh * xh * g), rsum(dh * (1.0 + sc) * xh), widen(rsum(dnq * cqh)), widen(rsum(dnkv * ckvh)),
            jnp.zeros((3, d), F32)], axis=0)
        first = jnp.logical_or(jnp.logical_and(lat, i % tpe == 0), i == nl)

        @pl.when(first)
        def _():
            st_ref[0] = stats

        @pl.when(jnp.logical_not(first))
        def _():
            st_ref[0] += stats

    def lat_tile(width):
        return pl.BlockSpec((tm, width), lambda i: (jnp.minimum(i, nl - 1), 0))

    def ctx_tile(width):
        return pl.BlockSpec((tm, width), lambda i: (jnp.maximum(i - nl, 0), 0))

    def tile(width):
        return pl.BlockSpec((tm, width), lambda i: (i, 0))

    tab = pl.BlockSpec((tm, 256), lambda i: (jnp.where(i < nl, i % tpe, tpe), 0))
    ex = pl.BlockSpec((1, 8, d), lambda i: (jnp.minimum(i // tpe, nex), 0, 0))
    return _hosted_call(
        body, (x2, ctx2, modv, g_attn, pg, drq, drk, dkc_r, drv, dvc_r, drg, dq_m, dkl, dkc, dvl, dvc, dxm,
               w_in, g_q, g_kv, w_uq, w_ukv, cos_t, sin_t), name="pre_bwd", grid=(nl + nc,),
        in_specs=[
            lat_tile(d), ctx_tile(d), ex, _full((1, d)), tile(PG_COLS),
            lat_tile(256), lat_tile(256), ctx_tile(256), lat_tile(512), ctx_tile(512), lat_tile(512),
            lat_tile(1024), lat_tile(1024), ctx_tile(1024), lat_tile(512), ctx_tile(512), lat_tile(d),
            _once(w_in.shape), _full((1, Q_LORA)), _full((1, KV_LORA)), _once(w_uq.shape), _once(w_ukv.shape),
            tab, tab,
        ],
        out_specs=[tile(IN_PAD), tile(1024), tile(1024), lat_tile(d), ex],
        out_shape=[
            jax.ShapeDtypeStruct((n_all, IN_PAD), BF16), jax.ShapeDtypeStruct((n_all, 1024), BF16),
            jax.ShapeDtypeStruct((n_all, 1024), BF16), jax.ShapeDtypeStruct((t_lat, d), F32),
            jax.ShapeDtypeStruct((nex + 1, 8, d), F32),
        ],
        sem=("arbitrary",), rider=rider)


MLA_SCALE = 1.0 / math.sqrt(MLA_NOPE + MLA_ROPE)
KEY_BLOCK = 1024


def _mla_specs(t_lat, seq, ctx_len, tq, heads=1):
    nqt = seq // tq
    cb = t_lat // ctx_len
    q = pl.BlockSpec((tq, heads * MLA_HEAD), lambda b, h, j: (b * nqt + j, h))
    kl = pl.BlockSpec((seq, heads * MLA_HEAD), lambda b, h, j: (b, h))
    kc = pl.BlockSpec((ctx_len, heads * MLA_HEAD), lambda b, h, j: (cb + b, h))
    vl = pl.BlockSpec((seq, heads * 128), lambda b, h, j: (b, h))
    vc = pl.BlockSpec((ctx_len, heads * 128), lambda b, h, j: (cb + b, h))
    o = pl.BlockSpec((tq, heads * 128), lambda b, h, j: (b * nqt + j, h))
    return q, kl, kc, vl, vc, o


FWD_HEADS = 2
BWD_HEADS = 1


def _mla_fwd(q, k, v, *, t_lat, seq, ctx_len, tq, rider=None):
    nex = t_lat // seq

    def body(q_ref, kl_ref, kc_ref, vl_ref, vc_ref, o_ref, lse_ref):
        for hh in range(FWD_HEADS):
            wide = slice(hh * MLA_HEAD, (hh + 1) * MLA_HEAD)
            cols = slice(hh * 128, (hh + 1) * 128)
            qb = q_ref[:, wide]
            s = _dot_nt(qb, kl_ref[:, wide])
            sc = _dot_nt(qb, kc_ref[:, wide])
            m = jnp.maximum(jnp.max(s, axis=-1, keepdims=True), jnp.max(sc, axis=-1, keepdims=True))
            p = jnp.exp(s - m)
            pc = jnp.exp(sc - m)
            total = jnp.sum(p, axis=-1, keepdims=True) + jnp.sum(pc, axis=-1, keepdims=True)
            o = _dot(p.astype(BF16), vl_ref[:, cols]) + _dot(pc.astype(BF16), vc_ref[:, cols])
            o_ref[:, cols] = (o * (1.0 / total)).astype(BF16)
            lse_ref[:, cols] = jnp.broadcast_to(m + jnp.log(total), (tq, 128))

    qs, kl, kc, vl, vc, os_ = _mla_specs(t_lat, seq, ctx_len, tq, FWD_HEADS)
    return _hosted_call(
        body, (q, k, k, v, v), name="mla_fwd", grid=(nex, HEADS // FWD_HEADS, seq // tq),
        in_specs=[qs, kl, kc, vl, vc], out_specs=[os_, os_],
        out_shape=[jax.ShapeDtypeStruct((t_lat, HEADS * 128), BF16), jax.ShapeDtypeStruct((t_lat, HEADS * 128), F32)],
        sem=("parallel", "parallel", "arbitrary"), rider=rider)


def _mla_bwd(q, k, v, ymla, lse, dmix, *, t_lat, seq, ctx_len, tq, rider=None):
    nex = t_lat // seq
    nqt = seq // tq
    t_ctx = nex * ctx_len
    kb = min(KEY_BLOCK, seq)

    def body(q_ref, kl_ref, kc_ref, vl_ref, vc_ref, o_ref, lse_ref, do_ref, dq_ref, dkl_out, dkc_out, dvl_out, dvc_out,
             dkl_ref, dkc_ref, dvl_ref, dvc_ref):
        j = pl.program_id(2)

        @pl.when(j == 0)
        def _():
            dkl_ref[...] = jnp.zeros(dkl_ref.shape, F32)
            dkc_ref[...] = jnp.zeros(dkc_ref.shape, F32)
            dvl_ref[...] = jnp.zeros(dvl_ref.shape, F32)
            dvc_ref[...] = jnp.zeros(dvc_ref.shape, F32)

        for hh in range(BWD_HEADS):
            wide = slice(hh * MLA_HEAD, (hh + 1) * MLA_HEAD)
            cols = slice(hh * 128, (hh + 1) * 128)
            qb = q_ref[:, wide]
            dob = do_ref[:, cols]
            delta = jnp.sum(dob.astype(F32) * o_ref[:, cols].astype(F32), axis=-1, keepdims=True)
            lse_row = lse_ref[:, hh * 128:hh * 128 + 1]

            def block(k_ref, v_ref, dk_ref, dv_ref, rows):
                kbl = k_ref[rows, wide]
                vbl = v_ref[rows, cols]
                p = jnp.exp(_dot_nt(qb, kbl) - lse_row)
                ds = (p * (_dot_nt(dob, vbl) - delta)).astype(BF16)
                dk_ref[rows, wide] += _dot_tn(ds, qb)
                dv_ref[rows, cols] += _dot_tn(p.astype(BF16), dob)
                return _dot(ds, kbl)

            dq = block(kc_ref, vc_ref, dkc_ref, dvc_ref, pl.ds(0, ctx_len))
            for i in range(seq // kb):
                dq = dq + block(kl_ref, vl_ref, dkl_ref, dvl_ref, pl.ds(i * kb, kb))
            dq_ref[:, wide] = dq.astype(BF16)

        @pl.when(j == nqt - 1)
        def _():
            dkl_out[...] = dkl_ref[...].astype(BF16)
            dkc_out[...] = dkc_ref[...].astype(BF16)
            dvl_out[...] = dvl_ref[...].astype(BF16)
            dvc_out[...] = dvc_ref[...].astype(BF16)

    g = BWD_HEADS
    qs, kl, kc, vl, vc, os_ = _mla_specs(t_lat, seq, ctx_len, tq, g)
    do_spec = pl.BlockSpec((tq, g * 128), lambda b, h, j: (b * nqt + j, HEADS // g + h))
    key_blocks = [(seq, g * MLA_HEAD), (ctx_len, g * MLA_HEAD), (seq, g * 128), (ctx_len, g * 128)]
    return _hosted_call(
        body, (q, k, k, v, v, ymla, lse, dmix), name="mla_bwd", grid=(nex, HEADS // g, nqt),
        in_specs=[qs, kl, kc, vl, vc, os_, os_, do_spec],
        out_specs=[qs] + [pl.BlockSpec(blk, lambda b, h, j: (b, h)) for blk in key_blocks],
        out_shape=[
            jax.ShapeDtypeStruct((t_lat, HEADS * MLA_HEAD), BF16),
            jax.ShapeDtypeStruct((t_lat, HEADS * MLA_HEAD), BF16),
            jax.ShapeDtypeStruct((t_ctx, HEADS * MLA_HEAD), BF16),
            jax.ShapeDtypeStruct((t_lat, HEADS * 128), BF16),
            jax.ShapeDtypeStruct((t_ctx, HEADS * 128), BF16),
        ],
        scratch_shapes=[pltpu.VMEM(blk, F32) for blk in key_blocks],
        sem=("parallel", "parallel", "arbitrary"), rider=rider)


def _decay_terms(lg, chunk, forward):
    ii = lax.broadcasted_iota(jnp.int32, (chunk, chunk), 0)
    jj = lax.broadcasted_iota(jnp.int32, (chunk, chunk), 1)
    diff = (ii - jj) if forward else (jj - ii)
    dist = jnp.maximum(diff, 0).astype(F32)
    dmat = jnp.where(diff >= 0, jnp.exp(lg * dist), 0.0)
    pos = lax.broadcasted_iota(jnp.int32, (chunk, 1), 0).astype(F32)
    if forward:
        e_q = pos + 1.0
        e_k = (chunk - 1.0) - pos
    else:
        e_q = chunk - pos
        e_k = pos
    wq = jnp.exp(lg * e_q)
    wk = jnp.exp(lg * e_k)
    cd = jnp.exp(jnp.full((1, 1), lg * chunk, F32))
    return dmat, dist, wq, wk, e_q, e_k, cd


def _ctx_weights(lg, ctx_len, forward):
    pos = lax.broadcasted_iota(jnp.int32, (ctx_len, 1), 0).astype(F32)
    e = ((ctx_len - 1.0) - pos) if forward else pos
    return jnp.exp(lg * e), e


def _pair_specs(t_lat, seq, ctx_len):
    cb = t_lat // ctx_len
    qk = pl.BlockSpec((seq, 128), lambda b, p: (b, p))
    v = pl.BlockSpec((seq, 256), lambda b, p: (b, p))
    kc = pl.BlockSpec((ctx_len, 128), lambda b, p: (cb + b, p))
    vc = pl.BlockSpec((ctx_len, 256), lambda b, p: (cb + b, p))
    return qk, v, kc, vc


def _lane_masks():
    lane = lax.broadcasted_iota(jnp.int32, (1, 128), 1)
    return [(lane // RET_DK) == hh for hh in (0, 1)]


def _ret_fwd_pair(rq, rk, rv, pg, lg, g_ret, *, t_lat, seq, ctx_len, chunk, rider=None):
    nex = t_lat // seq
    n_chunk = seq // chunk

    def body(q_ref, k_ref, v_ref, kc_ref, vc_ref, rg_ref, lg_ref, g_ref, y_ref, o_ref):
        pair = pl.program_id(1)
        masks = _lane_masks()
        kcf = kc_ref[...].astype(F32)
        chains = [(forward, hh) for forward in (True, False) for hh in (0, 1)]
        terms, s0 = [], []
        for forward, hh in chains:
            lgd = lg_ref[0 if forward else 1, 2 * pair + hh]
            terms.append(_decay_terms(lgd, chunk, forward))
            wc, _ = _ctx_weights(lgd, ctx_len, forward)
            s0.append(_dot_tn((jnp.where(masks[hh], kcf, 0.0) * wc).astype(BF16), vc_ref[:, hh * 128:(hh + 1) * 128]))
        both = [terms[hh][0] + terms[2 + hh][0] for hh in (0, 1)]
        o_ref[...] = jnp.zeros(o_ref.shape, F32)

        def step(t, states):
            new = [None] * 4
            for forward in (True, False):
                n = t if forward else n_chunk - 1 - t
                sl = pl.ds(pl.multiple_of(n * chunk, chunk), chunk)
                qb = q_ref[sl, :]
                kf_all = k_ref[sl, :].astype(F32)
                for hh in (0, 1):
                    c = (0 if forward else 2) + hh
                    _, _, wq, wk, _, _, cd = terms[c]
                    cols = slice(hh * 128, (hh + 1) * 128)
                    qm = jnp.where(masks[hh], qb, jnp.zeros((), BF16))
                    kf = jnp.where(masks[hh], kf_all, 0.0)
                    vb = v_ref[sl, cols]
                    o = wq * _dot(qm, states[c].astype(BF16))
                    if forward:
                        o = o + _dot((_dot_nt(qm, kf.astype(BF16)) * both[hh]).astype(BF16), vb)
                    o_ref[sl, cols] += o
                    new[c] = cd * states[c] + _dot_tn((kf * wk).astype(BF16), vb)
            return tuple(new)

        lax.fori_loop(0, n_chunk, step, tuple(s0))

        def norm_step(n, carry):
            sl = pl.ds(pl.multiple_of(n * chunk, chunk), chunk)
            for hh in (0, 1):
                cols = slice(hh * 128, (hh + 1) * 128)
                o = o_ref[sl, cols]
                mu = jnp.mean(o, axis=-1, keepdims=True)
                oc = o - mu
                var = jnp.mean(oc * oc, axis=-1, keepdims=True)
                rg = rg_ref[sl, cols]
                y_ref[sl, cols] = (oc * lax.rsqrt(var + EPS) * g_ref[:, cols] * (rg / (1.0 + jnp.exp(-rg)))).astype(BF16)
            return carry

        lax.fori_loop(0, n_chunk, norm_step, 0)

    qk, v, kc, vc = _pair_specs(t_lat, seq, ctx_len)
    return _hosted_call(
        body, (rq, rk, rv, rk, rv, pg, lg, g_ret), name="ret_fwd", grid=(nex, HEADS // 2),
        in_specs=[qk, qk, v, kc, vc, v, pl.BlockSpec(memory_space=pltpu.SMEM), pl.BlockSpec((1, 256), lambda b, p: (0, p))],
        out_specs=[v, v],
        out_shape=[jax.ShapeDtypeStruct((t_lat, HEADS * RET_DV), BF16), jax.ShapeDtypeStruct((t_lat, HEADS * RET_DV), F32)],
        sem=("parallel", "arbitrary"), rider=rider)


def _ret_bwd_pair(rq, rk, rv, pg, osum, dmix, lg, g_ret, *, t_lat, seq, ctx_len, chunk, rider=None):
    nex = t_lat // seq
    n_chunk = seq // chunk
    t_ctx = nex * ctx_len

    def body(q_ref, k_ref, v_ref, kc_ref, vc_ref, rg_ref, o_ref, dy_ref, lg_ref, g_ref,
             dq_out, dk_out, dv_out, dkc_ref, dvc_ref, drg_ref, st_ref, do_s, s_st, dq_ref, dk_ref, dv_ref):
        pair = pl.program_id(1)
        masks = _lane_masks()
        kcf = kc_ref[...].astype(F32)

        def norm_step(n, dgains):
            sl = pl.ds(pl.multiple_of(n * chunk, chunk), chunk)
            out = []
            for hh in (0, 1):
                cols = slice(hh * 128, (hh + 1) * 128)
                gain = g_ref[:, cols]
                o = o_ref[sl, cols]
                mu = jnp.mean(o, axis=-1, keepdims=True)
                oc = o - mu
                rstd = lax.rsqrt(jnp.mean(oc * oc, axis=-1, keepdims=True) + EPS)
                ohat = oc * rstd
                rg = rg_ref[sl, cols]
                sg = 1.0 / (1.0 + jnp.exp(-rg))
                dy = dy_ref[sl, cols].astype(F32)
                don = dy * (rg * sg)
                drg_ref[sl, cols] = (dy * (ohat * gain) * (sg * (1.0 + rg * (1.0 - sg)))).astype(BF16)
                dohat = don * gain
                do_s[sl, cols] = rstd * (dohat - jnp.mean(dohat, axis=-1, keepdims=True)
                                         - ohat * jnp.mean(dohat * ohat, axis=-1, keepdims=True))
                out.append(dgains[hh] + jnp.sum(don * ohat, axis=0, keepdims=True))
            return tuple(out)

        zero_row = jnp.zeros((1, 128), F32)
        dgains = lax.fori_loop(0, n_chunk, norm_step, (zero_row, zero_row))
        dq_ref[...] = jnp.zeros(dq_ref.shape, F32)
        dk_ref[...] = jnp.zeros(dk_ref.shape, F32)
        dv_ref[...] = jnp.zeros(dv_ref.shape, F32)

        chains = [(forward, hh) for forward in (True, False) for hh in (0, 1)]
        terms, ctxw, s0 = [], [], []
        for forward, hh in chains:
            lgd = lg_ref[0 if forward else 1, 2 * pair + hh]
            terms.append(_decay_terms(lgd, chunk, forward))
            ctxw.append(_ctx_weights(lgd, ctx_len, forward))
            s0.append(_dot_tn((jnp.where(masks[hh], kcf, 0.0) * ctxw[-1][0]).astype(BF16), vc_ref[:, hh * 128:(hh + 1) * 128]))

        def chunk_at(t, ascending):
            n = t if ascending else n_chunk - 1 - t
            return n, pl.ds(pl.multiple_of(n * chunk, chunk), chunk)

        def state_step(t, states):
            new = []
            for c, (forward, hh) in enumerate(chains):
                n, sl = chunk_at(t, forward)
                wk, cd = terms[c][3], terms[c][6]
                s_st[c, n] = states[c]
                kf = jnp.where(masks[hh], k_ref[sl, :].astype(F32), 0.0)
                new.append(cd * states[c] + _dot_tn((kf * wk).astype(BF16), v_ref[sl, hh * 128:(hh + 1) * 128]))
            return tuple(new)

        lax.fori_loop(0, n_chunk, state_step, tuple(s0))

        both = [terms[hh][0] + terms[2 + hh][0] for hh in (0, 1)]

        def grad_step(t, carry):
            out = [None] * len(chains)
            in_chunk_b = [None, None]
            for forward in (True, False):
                n, sl = chunk_at(t, not forward)
                qb = q_ref[sl, :]
                kf_all = k_ref[sl, :].astype(F32)
                dq_sum = jnp.zeros((chunk, 128), F32)
                dk_sum = jnp.zeros((chunk, 128), F32)
                for hh in (0, 1):
                    c = (0 if forward else 2) + hh
                    g_next, dlg = carry[c]
                    dmat, dist, wq, wk, e_q, e_k, cd = terms[c]
                    cols = slice(hh * 128, (hh + 1) * 128)
                    qm = jnp.where(masks[hh], qb, jnp.zeros((), BF16))
                    kf = jnp.where(masks[hh], kf_all, 0.0)
                    kb = kf.astype(BF16)
                    vb = v_ref[sl, cols]
                    do = do_s[sl, cols]
                    dob = do.astype(BF16)
                    s_n = s_st[c, n]
                    s_nb = s_n.astype(BF16)
                    gb = g_next.astype(BF16)
                    dk_cross = wk * _dot_nt(vb, gb)
                    dv = _dot((kf * wk).astype(BF16), gb)
                    o_cross = wq * _dot(qm, s_nb)
                    dq_sum = dq_sum + wq * _dot_nt(dob, s_nb)
                    dk_sum = dk_sum + dk_cross
                    dlg = (dlg + chunk * cd * jnp.sum(g_next * s_n, keepdims=True)
                           + jnp.sum(e_k * jnp.sum(kf * dk_cross, axis=-1, keepdims=True), keepdims=True)
                           + jnp.sum(e_q * jnp.sum(o_cross * do, axis=-1, keepdims=True), keepdims=True))
                    if forward:
                        a_raw = _dot_nt(qm, kb)
                        da_raw = _dot_nt(dob, vb)
                        prod = a_raw * da_raw
                        dlg = dlg + jnp.sum(dist * dmat * prod, keepdims=True)
                        in_chunk_b[hh] = jnp.sum(terms[2 + hh][1] * terms[2 + hh][0] * prod, keepdims=True)
                        dab = (da_raw * both[hh]).astype(BF16)
                        dq_sum = dq_sum + _dot(dab, kb)
                        dk_sum = dk_sum + _dot_tn(dab, qm)
                        dv = dv + _dot_tn((a_raw * both[hh]).astype(BF16), dob)
                    else:
                        dlg = dlg + in_chunk_b[hh]
                    dv_ref[sl, cols] += dv
                    out[c] = (cd * g_next + _dot_tn((qm.astype(F32) * wq).astype(BF16), dob), dlg)
                dq_ref[sl, :] += dq_sum
                dk_ref[sl, :] += dk_sum
            return tuple(out)

        zero = (jnp.zeros((128, 128), F32), jnp.zeros((1, 1), F32))
        res = lax.fori_loop(0, n_chunk, grad_step, (zero,) * len(chains))
        dkc_sum = jnp.zeros((ctx_len, 128), F32)
        dvc = [jnp.zeros((ctx_len, 128), F32)] * 2
        dlgs = []
        for c, (forward, hh) in enumerate(chains):
            ds0, dlg = res[c]
            wc, e_c = ctxw[c]
            kcm = jnp.where(masks[hh], kcf, 0.0)
            ds0b = ds0.astype(BF16)
            dkc_part = wc * _dot_nt(vc_ref[:, hh * 128:(hh + 1) * 128], ds0b)
            dkc_sum = dkc_sum + dkc_part
            dvc[hh] = dvc[hh] + _dot((kcm * wc).astype(BF16), ds0b)
            dlgs.append(dlg + jnp.sum(e_c * jnp.sum(kcm * dkc_part, axis=-1, keepdims=True), keepdims=True))
        dq_out[...] = dq_ref[...].astype(BF16)
        dk_out[...] = dk_ref[...].astype(BF16)
        dv_out[...] = dv_ref[...].astype(BF16)
        dkc_ref[...] = dkc_sum
        for hh in (0, 1):
            cols = slice(hh * 128, (hh + 1) * 128)
            dvc_ref[:, cols] = dvc[hh]
            st_ref[0, :, cols] = jnp.concatenate([
                dgains[hh], jnp.broadcast_to(dlgs[hh], (1, 128)), jnp.broadcast_to(dlgs[2 + hh], (1, 128)),
                jnp.zeros((5, 128), F32)], axis=0)

    qk, v, kc, vc = _pair_specs(t_lat, seq, ctx_len)
    return _hosted_call(
        body, (rq, rk, rv, rk, rv, pg, osum, dmix, lg, g_ret), name="ret_bwd", grid=(nex, HEADS // 2),
        in_specs=[qk, qk, v, kc, vc, v, v, v, pl.BlockSpec(memory_space=pltpu.SMEM),
                  pl.BlockSpec((1, 256), lambda b, p: (0, p))],
        out_specs=[
            qk, qk, v,
            pl.BlockSpec((ctx_len, 128), lambda b, p: (b, p)),
            pl.BlockSpec((ctx_len, 256), lambda b, p: (b, p)),
            v,
            pl.BlockSpec((1, 8, 256), lambda b, p: (b, 0, p)),
        ],
        out_shape=[
            jax.ShapeDtypeStruct((t_lat, 256), BF16), jax.ShapeDtypeStruct((t_lat, 256), BF16),
            jax.ShapeDtypeStruct((t_lat, 512), BF16), jax.ShapeDtypeStruct((t_ctx, 256), F32),
            jax.ShapeDtypeStruct((t_ctx, 512), F32), jax.ShapeDtypeStruct((t_lat, 512), BF16),
            jax.ShapeDtypeStruct((nex, 8, 512), F32),
        ],
        scratch_shapes=[pltpu.VMEM((seq, 256), F32), pltpu.VMEM((4, n_chunk, 128, 128), F32),
                        pltpu.VMEM((seq, 128), F32), pltpu.VMEM((seq, 128), F32), pltpu.VMEM((seq, 256), F32)],
        sem=("parallel", "arbitrary"), rider=rider)


def _matmul_tn(a, b, *, bm, bn, bk, chip_major, name, out_dtype=F32, rider=None):
    tk, m = a.shape
    n = b.shape[1]
    slab = n // N_CHIPS
    per_block = bn // slab if chip_major else 1
    bk = max(c for c in range(LANES, min(bk, tk) + 1, LANES) if tk % c == 0)
    nk = tk // bk
    blk = (per_block, bm, slab) if chip_major else (bm, bn)

    def body(a_ref, b_ref, o_ref, acc_ref):
        k = pl.program_id(2)
        if chip_major:
            parts = [_dot_tn(a_ref[...], b_ref[:, s * slab:(s + 1) * slab]) for s in range(per_block)]
        else:
            parts = [_dot_tn(a_ref[...], b_ref[...])]

        @pl.when(k == 0)
        def _():
            for s, part in enumerate(parts):
                if chip_major:
                    acc_ref[s] = part
                else:
                    acc_ref[...] = part

        @pl.when(k > 0)
        def _():
            for s, part in enumerate(parts):
                if chip_major:
                    acc_ref[s] += part
                else:
                    acc_ref[...] += part

        @pl.when(k == nk - 1)
        def _():
            o_ref[...] = acc_ref[...].astype(out_dtype)

    if chip_major:
        out_spec = pl.BlockSpec(blk, lambda i, j, k: (j, i, 0))
        out_shape = jax.ShapeDtypeStruct((N_CHIPS, m, slab), out_dtype)
    else:
        out_spec = pl.BlockSpec(blk, lambda i, j, k: (i, j))
        out_shape = jax.ShapeDtypeStruct((m, n), out_dtype)
    (out,), carried = _hosted_call(
        body, (a, b), name=name, grid=(m // bm, n // bn, nk),
        in_specs=[pl.BlockSpec((bk, bm), lambda i, j, k: (k, i)), pl.BlockSpec((bk, bn), lambda i, j, k: (k, j))],
        out_specs=[out_spec], out_shape=[out_shape], scratch_shapes=[pltpu.VMEM(blk, F32)],
        sem=("parallel", "parallel", "arbitrary"), rider=rider)
    return out if rider is None else (out, carried)


_LATE = ("w_out", "w_ff1", "w_ff2")
_EARLY = ("w_in", "w_uq", "w_ukv")


def _local_step(x, ctx, tgt, modv, lg, g_attn, g_ffn, g_fin, g_ret, g_q, g_kv, w_in, w_uq, w_ukv, late, place=None,
                *, tm=256, tq=256, chunk=256):
    nex, seq, d = x.shape
    ctx_len = ctx.shape[1]
    t_lat = nex * seq
    tm = min(tm, seq)
    x2 = x.reshape(t_lat, d)
    ctx2 = ctx.reshape(nex * ctx_len, d)
    tgt2 = tgt.reshape(t_lat, d)
    tm_fwd = min(2 * tm, seq)
    cos_t, sin_t = _rope_tables(seq, tm)
    dims = dict(t_lat=t_lat, seq=seq, ctx_len=ctx_len)
    alone = place is None

    (hb, pg, rq, rk, rv, nq, nkv, q, k, v), crossed_a = _pre_fwd(
        x2, ctx2, modv, g_attn, w_in, g_q, g_kv, w_uq, w_ukv, *_rope_tables(seq, tm_fwd), seq=seq, tm=tm_fwd,
        rider=None if alone else _gather_ici_rider([late[2]]))
    (yret, osum), got = _ret_fwd_pair(
        rq, rk, rv, pg, lg, g_ret, chunk=min(2 * chunk, seq), **dims,
        rider=None if alone else _merge_riders(_gather_d2d_rider(crossed_a), _gather_ici_rider([late[3]])))
    (ymla, lse), got_rest = _mla_fwd(
        q, k, v, tq=tq, **dims,
        rider=None if alone else _merge_riders(_gather_rider([late[0], late[1]], staged=True), _gather_d2d_rider(got[1:])))
    w_out, w_ff1, w_ff2a, w_ff2b = late if alone else (got_rest[0], got_rest[1], got[0], got_rest[2])
    mix, act, du, h2, df, dmo, dmix, dxm, st_post = _post(yret, ymla, x2, tgt2, modv, g_ffn, g_fin, w_out.reshape(d, d),
                                                         w_ff1, w_ff2a, w_ff2b, seq=seq, tm=min(tm, 256))
    kw = dict(bm=1024, bn=1024, bk=2048, out_dtype=BF16)
    g_ff2 = _matmul_tn(act, df, chip_major=False, name="gw_ff2", **kw).reshape(N_CHIPS, D_FF // N_CHIPS, d)
    if alone:
        g_ff1 = _matmul_tn(h2, du, chip_major=True, name="gw_ff1", **kw)
        g_out = _matmul_tn(mix, dmo, chip_major=False, name="gw_out", **kw).reshape(N_CHIPS, d // N_CHIPS, d)
        (dq_m, dkl, dkc, dvl, dvc), _ = _mla_bwd(q, k, v, ymla, lse, dmix, tq=tq, **dims)
        (drq, drk, drv, dkc_r, dvc_r, drg, st_ret), _ = _ret_bwd_pair(rq, rk, rv, pg, osum, dmix, lg, g_ret, chunk=chunk,
                                                                      **dims)
        late_out = [g_out, g_ff1, g_ff2]
    else:
        core, slot = place
        g_ff1, x_ff2 = _matmul_tn(h2, du, chip_major=True, name="gw_ff1", rider=_exchange_rider([g_ff2]), **kw)
        g_out, x_ff1 = _matmul_tn(mix, dmo, chip_major=False, name="gw_out", rider=_exchange_rider([g_ff1]), **kw)
        g_out = g_out.reshape(N_CHIPS, d // N_CHIPS, d)
        p_ff2 = _add_half(g_ff2, x_ff2[0], core, "add_half_w_ff2")
        p_ff1 = _add_half(g_ff1, x_ff1[0], core, "add_half_w_ff1")
        (dq_m, dkl, dkc, dvl, dvc), (l_ff2, l_ff1, x_out) = _mla_bwd(
            q, k, v, ymla, lse, dmix, tq=min(seq, 512), **dims,
            rider=_merge_riders(_scatter_rider([p_ff2, p_ff1]), _exchange_rider([g_out])))
        p_out = _add_half(g_out, x_out, core, "add_half_w_out")
        m_ff2 = _sum_chips(p_ff2, l_ff2, slot, "sum_chips_w_ff2")
        m_ff1 = _sum_chips(p_ff1, l_ff1, slot, "sum_chips_w_ff1")
        (drq, drk, drv, dkc_r, dvc_r, drg, st_ret), (l_out,) = _ret_bwd_pair(
            rq, rk, rv, pg, osum, dmix, lg, g_ret, chunk=chunk, **dims, rider=_scatter_rider([p_out]))
        late_out = [_sum_chips(p_out, l_out, slot, "sum_chips_w_out"), m_ff1, m_ff2]
    (dpb, dqf, dkvf, gx, st_pre), _ = _pre_bwd(
        x2, ctx2, modv, g_attn, pg, drq, drk, dkc_r, drv, dvc_r, drg, dq_m, dkl, dkc, dvl, dvc, dxm, w_in, g_q, g_kv,
        w_uq, w_ukv, cos_t, sin_t, seq=seq, tm=tm)
    g_early = [
        _matmul_tn(dpb, hb, bm=IN_PAD // 2, bn=d, bk=1536, chip_major=False, name="gw_in"),
        _matmul_tn(dqf, nq, bm=HEADS * MLA_HEAD, bn=Q_LORA, bk=1536, chip_major=False, name="gw_uq"),
        _matmul_tn(nkv, dkvf, bm=KV_LORA, bn=HEADS * 256, bk=1536, chip_major=True, name="gw_ukv"),
    ]
    return gx.reshape(nex, seq, d), g_early, late_out, st_post, st_ret, st_pre


_ANY = pl.BlockSpec(memory_space=pl.ANY)
_VMEM = pl.BlockSpec(memory_space=pltpu.VMEM)
_OFFSETS = tuple((dx, dy, dc) for dx in (0, 1) for dy in (0, 1) for dc in (0, 1))[1:]
_CHIP_OFFSETS = ((1, 0), (0, 1), (1, 1))


def _place():
    return lax.axis_index("x"), lax.axis_index("y"), lax.axis_index("c")


def _flip(v, d):
    return 1 - v if d else v


def _gather8_rider(a, in_vmem=True):
    def copies(a_ref, o_ref, send, recv):
        x, y, z = _place()
        me = 4 * x + 2 * y + z
        out = []
        for k, (dx, dy, dc) in enumerate(_OFFSETS):
            peer = (_flip(x, dx), _flip(y, dy), _flip(z, dc))
            landing = o_ref.at[4 * peer[0] + 2 * peer[1] + peer[2]]
            out.append((
                pltpu.make_async_remote_copy(src_ref=a_ref, dst_ref=o_ref.at[me], send_sem=send.at[k],
                                             recv_sem=recv.at[k], device_id=peer, device_id_type=MESH),
                pltpu.make_async_remote_copy(src_ref=a_ref, dst_ref=landing, send_sem=send.at[k],
                                             recv_sem=recv.at[k], device_id=peer, device_id_type=MESH)))
        return me, out

    def start(ins, outs, sems):
        me, cps = copies(ins[0], outs[0], sems[0], sems[1])
        pltpu.make_async_copy(ins[0], outs[0].at[me], sems[2]).start()
        for out_cp, _ in cps:
            out_cp.start()

    def finish(ins, outs, sems):
        me, cps = copies(ins[0], outs[0], sems[0], sems[1])
        for out_cp, in_cp in cps:
            in_cp.wait_recv()
            out_cp.wait_send()
        pltpu.make_async_copy(ins[0], outs[0].at[me], sems[2]).wait()

    spec = [_VMEM] if in_vmem else [_ANY]
    return _Rider([a], [jax.ShapeDtypeStruct((N_DEV,) + a.shape, a.dtype)],
                  [pltpu.SemaphoreType.DMA((7,)), pltpu.SemaphoreType.DMA((7,)), pltpu.SemaphoreType.DMA],
                  start, finish, in_specs=spec, out_specs=spec)


def _merge_riders(*riders):
    ins, outs, sems, in_specs, out_specs, aliases, cuts = [], [], [], [], [], {}, []
    for r in riders:
        cuts.append((len(ins), len(outs), len(sems)))
        aliases.update({len(ins) + i: len(outs) + j for i, j in r.aliases.items()})
        ins += r.ins
        outs += r.out_shapes
        sems += r.sems
        in_specs += r.in_specs
        out_specs += r.out_specs

    def part(r, cut, r_ins, r_outs, r_sems):
        return (r_ins[cut[0]:cut[0] + len(r.ins)], r_outs[cut[1]:cut[1] + len(r.out_shapes)],
                r_sems[cut[2]:cut[2] + len(r.sems)])

    def start(r_ins, r_outs, r_sems):
        for r, cut in zip(riders, cuts):
            r.start(*part(r, cut, r_ins, r_outs, r_sems))

    def finish(r_ins, r_outs, r_sems):
        for r, cut in zip(riders, cuts):
            r.finish(*part(r, cut, r_ins, r_outs, r_sems))

    def middle(r_ins, r_outs, r_sems):
        for r, cut in zip(riders, cuts):
            if r.middle is not None:
                r.middle(*part(r, cut, r_ins, r_outs, r_sems))

    return _Rider(ins, outs, sems, start, finish, aliases=aliases, in_specs=in_specs, out_specs=out_specs,
                  middle=middle if any(r.middle is not None for r in riders) else None)


def _allgather8(a, name):
    return _run_rider(_gather8_rider(a), name)[0]


BF16_TILE_ROWS = 16


def _half(o, slot, which):
    r2 = o.shape[1] // 2
    if r2 % BF16_TILE_ROWS == 0:
        return o.at[slot, pl.ds(which * r2, r2)]
    c2 = o.shape[2] // 2
    assert c2 % LANES == 0
    return o.at[slot, :, pl.ds(which * c2, c2)]


def _gather_send(o_refs, send, recv):
    x, y, z = _place()
    chip = 2 * x + y
    for a, o in enumerate(o_refs):
        r2 = o.shape[1] // 2
        mine = _half(o, chip, z)
        for k, (dx, dy) in enumerate(_CHIP_OFFSETS):
            pltpu.make_async_remote_copy(
                src_ref=mine, dst_ref=mine, send_sem=send.at[a, k], recv_sem=recv.at[a, k],
                device_id=(_flip(x, dx), _flip(y, dy), z), device_id_type=MESH).start()


def _gather_landed(o_refs, send, recv, then=None):
    x, y, z = _place()
    chip = 2 * x + y
    for a, o in enumerate(o_refs):
        for k, (dx, dy) in enumerate(_CHIP_OFFSETS):
            landed = _half(o, 2 * _flip(x, dx) + _flip(y, dy), z)
            pltpu.make_async_remote_copy(
                src_ref=landed, dst_ref=landed, send_sem=send.at[a, k], recv_sem=recv.at[a, k],
                device_id=(_flip(x, dx), _flip(y, dy), z), device_id_type=MESH).wait_recv()
            if then is not None:
                then(a, k, landed)
    for a, o in enumerate(o_refs):
        mine = _half(o, chip, z)
        for k, (dx, dy) in enumerate(_CHIP_OFFSETS):
            pltpu.make_async_remote_copy(
                src_ref=mine, dst_ref=mine, send_sem=send.at[a, k], recv_sem=recv.at[a, k],
                device_id=(_flip(x, dx), _flip(y, dy), z), device_id_type=MESH).wait_send()


def _pass_on(o_refs, fsend, frecv, a, k, landed):
    x, y, z = _place()
    pltpu.make_async_remote_copy(
        src_ref=landed, dst_ref=landed, send_sem=fsend.at[a, k], recv_sem=frecv.at[a, k],
        device_id=(x, y, 1 - z), device_id_type=MESH).start()


def _passed_on(o_refs, fsend, frecv):
    x, y, z = _place()
    for a, o in enumerate(o_refs):
        for k, (dx, dy) in enumerate(_CHIP_OFFSETS):
            other = 2 * _flip(x, dx) + _flip(y, dy)
            got = _half(o, other, 1 - z)
            gave = _half(o, other, z)
            pltpu.make_async_remote_copy(
                src_ref=got, dst_ref=got, send_sem=fsend.at[a, k], recv_sem=frecv.at[a, k],
                device_id=(x, y, 1 - z), device_id_type=MESH).wait_recv()
            pltpu.make_async_remote_copy(
                src_ref=gave, dst_ref=gave, send_sem=fsend.at[a, k], recv_sem=frecv.at[a, k],
                device_id=(x, y, 1 - z), device_id_type=MESH).wait_send()


def _gather_finish(o_refs, send, recv, fsend, frecv):
    _gather_landed(o_refs, send, recv, functools.partial(_pass_on, o_refs, fsend, frecv))
    _passed_on(o_refs, fsend, frecv)


class _Rider:
    def __init__(self, ins, out_shapes, sems, start, finish, aliases=None, in_specs=None, out_specs=None, middle=None):
        self.ins, self.out_shapes, self.sems = list(ins), list(out_shapes), list(sems)
        self.start, self.finish, self.aliases = start, finish, dict(aliases or {})
        self.middle = middle
        self.in_specs = list(in_specs) if in_specs else [_ANY] * len(self.ins)
        self.out_specs = list(out_specs) if out_specs else [_ANY] * len(self.out_shapes)


def _run_rider(rider, name):
    r_in, r_out = len(rider.ins), len(rider.out_shapes)

    def body(*refs):
        ins, outs, sems = refs[:r_in], refs[r_in:r_in + r_out], refs[r_in + r_out:]
        rider.start(ins, outs, sems)
        if rider.middle is not None:
            rider.middle(ins, outs, sems)
        rider.finish(ins, outs, sems)

    return pl.pallas_call(
        body, name=name, in_specs=rider.in_specs, out_specs=rider.out_specs, out_shape=rider.out_shapes,
        input_output_aliases=rider.aliases, scratch_shapes=rider.sems,
    )(*rider.ins)


def _hosted_call(body, args, *, name, grid, in_specs, out_specs, out_shape, scratch_shapes=(), sem, rider=None,
                 prefetch=()):
    scratch_shapes = list(scratch_shapes)
    n_pf, n_in, n_out, n_sc = len(prefetch), len(in_specs), len(out_specs), len(scratch_shapes)
    r_in, r_out = (len(rider.ins), len(rider.out_shapes)) if rider else (0, 0)
    last = tuple(g - 1 for g in grid)

    def hosted(*refs):
        p = 0
        parts = []
        for cnt in (n_pf, n_in, r_in, n_out, r_out, n_sc):
            parts.append(refs[p:p + cnt])
            p += cnt
        pf, ins, r_ins, outs, r_outs, scratch = parts
        sems = refs[p:]
        ids = [pl.program_id(a) for a in range(len(grid))]
        is_first = functools.reduce(jnp.logical_and, [i == 0 for i in ids])
        is_last = functools.reduce(jnp.logical_and, [i == e for i, e in zip(ids, last)])

        @pl.when(is_first)
        def _():
            rider.start(r_ins, r_outs, sems)

        if rider.middle is not None:
            linear = functools.reduce(lambda acc, ig: acc * ig[1] + ig[0], zip(ids, grid), 0)

            @pl.when(linear == math.prod(grid) * 3 // 4)
            def _():
                rider.middle(r_ins, r_outs, sems)

        body(*pf, *ins, *outs, *scratch)

        @pl.when(is_last)
        def _():
            rider.finish(r_ins, r_outs, sems)

    if rider is None:
        kern, all_in, all_out, shapes, scratch, aliases, extra = body, list(in_specs), list(out_specs), list(out_shape), \
            scratch_shapes, {}, []
    else:
        kern, all_in, all_out = hosted, list(in_specs) + rider.in_specs, list(out_specs) + rider.out_specs
        shapes, scratch, extra = list(out_shape) + rider.out_shapes, scratch_shapes + rider.sems, rider.ins
        aliases = {n_pf + n_in + i: n_out + j for i, j in rider.aliases.items()}
        sem = ("arbitrary",) * len(grid)
    if prefetch:
        spec = dict(grid_spec=pltpu.PrefetchScalarGridSpec(
            num_scalar_prefetch=n_pf, grid=grid, in_specs=all_in, out_specs=all_out, scratch_shapes=scratch))
    else:
        spec = dict(grid=grid, in_specs=all_in, out_specs=all_out, scratch_shapes=scratch)
    res = pl.pallas_call(kern, name=name, out_shape=shapes, input_output_aliases=aliases,
                         compiler_params=_params(sem, VMEM_LIMIT), **spec)(*prefetch, *args, *extra)
    return list(res[:n_out]), list(res[n_out:])


def _gather_rider(ws, staged=False):
    n = len(ws)
    shapes = [jax.ShapeDtypeStruct(w.shape, w.dtype) for w in ws]
    sems = [pltpu.SemaphoreType.DMA((n, 3))] * 4
    aliases = {a: a for a in range(n)}

    def start(ins, outs, s):
        _gather_send(outs, s[0], s[1])

    if not staged:
        return _Rider(ws, shapes, sems, start, lambda ins, outs, s: _gather_finish(outs, *s), aliases=aliases)
    return _Rider(
        ws, shapes, sems, start, lambda ins, outs, s: _passed_on(outs, s[2], s[3]), aliases=aliases,
        middle=lambda ins, outs, s: _gather_landed(outs, s[0], s[1], functools.partial(_pass_on, outs, s[2], s[3])))


def _gather_ici_rider(ws):
    n = len(ws)
    return _Rider(
        ws, [jax.ShapeDtypeStruct(w.shape, w.dtype) for w in ws], [pltpu.SemaphoreType.DMA((n, 3))] * 2,
        lambda ins, outs, sems: _gather_send(outs, sems[0], sems[1]),
        lambda ins, outs, sems: _gather_landed(outs, sems[0], sems[1]),
        aliases={a: a for a in range(n)})


def _gather_d2d_rider(ws):
    n = len(ws)

    def start(ins, outs, sems):
        x, y, z = _place()
        for a, o in enumerate(outs):
            for k, (dx, dy) in enumerate(_CHIP_OFFSETS):
                _pass_on(outs, sems[0], sems[1], a, k, _half(o, 2 * _flip(x, dx) + _flip(y, dy), z))

    return _Rider(
        ws, [jax.ShapeDtypeStruct(w.shape, w.dtype) for w in ws], [pltpu.SemaphoreType.DMA((n, 3))] * 2,
        start, lambda ins, outs, sems: _passed_on(outs, sems[0], sems[1]), aliases={a: a for a in range(n)})


def _copies_rider(ins, out_shapes, sem_shape, make):
    def start(r_ins, r_outs, sems):
        for cp in make(r_ins, r_outs, sems[0], sems[1]):
            cp.start()

    def finish(r_ins, r_outs, sems):
        for cp in make(r_ins, r_outs, sems[0], sems[1]):
            cp.wait()

    return _Rider(ins, out_shapes, [pltpu.SemaphoreType.DMA(sem_shape)] * 2, start, finish)


def _exchange_rider(gs):
    def make(g_refs, r_refs, send, recv):
        x, y, z = _place()
        return [pltpu.make_async_remote_copy(
            src_ref=g.at[:, pl.ds((1 - z) * (g.shape[1] // 2), g.shape[1] // 2)], dst_ref=r, send_sem=send.at[a],
            recv_sem=recv.at[a], device_id=(x, y, 1 - z), device_id_type=MESH)
            for a, (g, r) in enumerate(zip(g_refs, r_refs))]

    shapes = [jax.ShapeDtypeStruct((g.shape[0], g.shape[1] // 2, g.shape[2]), g.dtype) for g in gs]
    return _copies_rider(gs, shapes, (len(gs),), make)


def _add_half(g, recv, core, name):
    s, r, c = g.shape
    r2 = r // 2
    rb = r2
    for cand in (512, 256, 128, 64):
        if r2 % cand == 0:
            rb = cand
            break
    g4 = g.reshape(s, 2, r2, c)

    def body(core_ref, g_ref, r_ref, o_ref):
        o_ref[...] = (g_ref[...].astype(F32) + r_ref[...].astype(F32)).astype(BF16)

    return pl.pallas_call(
        body, name=name,
        grid_spec=pltpu.PrefetchScalarGridSpec(
            num_scalar_prefetch=1, grid=(s, r2 // rb),
            in_specs=[pl.BlockSpec((None, None, rb, c), lambda i, j, cr: (i, cr[0], j, 0)),
                      pl.BlockSpec((None, rb, c), lambda i, j, cr: (i, j, 0))],
            out_specs=pl.BlockSpec((None, rb, c), lambda i, j, cr: (i, j, 0))),
        out_shape=jax.ShapeDtypeStruct((s, r2, c), BF16),
        compiler_params=_params(("parallel", "parallel")),
    )(core, g4, recv)


def _scatter_rider(ps):
    def make(p_refs, o_refs, send, recv):
        x, y, z = _place()
        copies = []
        for a, (p, o) in enumerate(zip(p_refs, o_refs)):
            for k, (dx, dy) in enumerate(_CHIP_OFFSETS):
                other = 2 * _flip(x, dx) + _flip(y, dy)
                copies.append(pltpu.make_async_remote_copy(
                    src_ref=p.at[other], dst_ref=o.at[k], send_sem=send.at[a, k], recv_sem=recv.at[a, k],
                    device_id=(_flip(x, dx), _flip(y, dy), z), device_id_type=MESH))
        return copies

    shapes = [jax.ShapeDtypeStruct((3,) + p.shape[1:], p.dtype) for p in ps]
    return _copies_rider(ps, shapes, (len(ps), 3), make)


def _sum_chips(p, landed, chip, name):
    _, r2, c = p.shape
    rb = r2
    for cand in (256, 128, 64):
        if r2 % cand == 0:
            rb = cand
            break

    def body(s_ref, p_ref, l_ref, o_ref):
        acc = p_ref[...].astype(F32)
        for k in range(3):
            acc = acc + l_ref[k].astype(F32)
        o_ref[...] = acc

    return pl.pallas_call(
        body, name=name,
        grid_spec=pltpu.PrefetchScalarGridSpec(
            num_scalar_prefetch=1, grid=(r2 // rb,),
            in_specs=[pl.BlockSpec((None, rb, c), lambda i, s: (s[0], i, 0)),
                      pl.BlockSpec((3, rb, c), lambda i, s: (0, i, 0))],
            out_specs=pl.BlockSpec((rb, c), lambda i, s: (i, 0))),
        out_shape=jax.ShapeDtypeStruct((r2, c), F32),
        compiler_params=_params(("parallel",)),
    )(chip, p, landed)


def _swap_rider(hs):
    def make(h_refs, o_refs, send, recv):
        x, y, z = _place()
        return [pltpu.make_async_remote_copy(
            src_ref=h, dst_ref=o, send_sem=send.at[a], recv_sem=recv.at[a], device_id=(x, y, 1 - z),
            device_id_type=MESH) for a, (h, o) in enumerate(zip(h_refs, o_refs))]

    return _copies_rider(hs, [jax.ShapeDtypeStruct(h.shape, h.dtype) for h in hs], (len(hs),), make)


def _reduce_scatter_vmem(gs, rows, rider, name):
    n = len(gs)
    r_in, r_out = len(rider.ins), len(rider.out_shapes)
    halves = [(r // 2, g.shape[-1]) for g, (r, _) in zip(gs, rows)]
    piece_cols = 2 * LANES
    pieces = [(a, slice(c0, min(c0 + piece_cols, h[1]))) for a, h in enumerate(halves) for c0 in range(0, h[1], piece_cols)]
    n_p = len(pieces)

    def body(*refs):
        p = 0
        parts = []
        for cnt in (n, r_in, n, n, r_out, n, n, n, 6):
            parts.append(refs[p:p + cnt])
            p += cnt
        g_refs, r_ins, mine, theirs, r_outs, recv, part, land, sems = parts
        r_sems = refs[p:]
        xs, xr, ss, sr, ws, wr = sems
        x, y, z = _place()
        chip = 2 * x + y
        sib = (x, y, 1 - z)
        rider.start(r_ins, r_outs, r_sems)

        def half_of(a, s, which):
            r2 = halves[a][0]
            if len(g_refs[a].shape) == 3:
                return g_refs[a].at[s, pl.ds(pl.multiple_of(which * r2, 8), r2)]
            return g_refs[a].at[pl.ds(pl.multiple_of(s * rows[a][1] + which * r2, 8), r2)]

        def exchange(i):
            a, cols = pieces[i]
            return [pltpu.make_async_remote_copy(
                src_ref=half_of(a, s, 1 - z).at[:, cols], dst_ref=recv[a].at[s, :, cols], send_sem=xs.at[i, s],
                recv_sem=xr.at[i, s], device_id=sib, device_id_type=MESH) for s in range(N_CHIPS)]

        def scatter(i):
            a, cols = pieces[i]
            return [pltpu.make_async_remote_copy(
                src_ref=part[a].at[2 * _flip(x, dx) + _flip(y, dy), :, cols], dst_ref=land[a].at[k, :, cols],
                send_sem=ss.at[i, k], recv_sem=sr.at[i, k], device_id=(_flip(x, dx), _flip(y, dy), z),
                device_id_type=MESH) for k, (dx, dy) in enumerate(_CHIP_OFFSETS)]

        def swap(i):
            a, cols = pieces[i]
            return pltpu.make_async_remote_copy(
                src_ref=mine[a].at[:, cols], dst_ref=theirs[a].at[:, cols], send_sem=ws.at[i], recv_sem=wr.at[i],
                device_id=sib, device_id_type=MESH)

        for i in range(len(pieces)):
            for cp in exchange(i):
                cp.start()
        for i, (a, cols) in enumerate(pieces):
            for cp in exchange(i):
                cp.wait()
            for s in range(N_CHIPS):
                part[a][s, :, cols] = (half_of(a, s, z)[:, cols] + recv[a][s, :, cols]).astype(BF16)
            for cp in scatter(i):
                cp.start()
        for i, (a, cols) in enumerate(pieces):
            for cp in scatter(i):
                cp.wait()
            acc = part[a][chip, :, cols].astype(F32)
            for k in range(3):
                acc = acc + land[a][k, :, cols].astype(F32)
            mine[a][:, cols] = acc
            swap(i).start()
        for i in range(len(pieces)):
            swap(i).wait()
        rider.finish(r_ins, r_outs, r_sems)

    half_shapes = [jax.ShapeDtypeStruct(h, F32) for h in halves]
    res = pl.pallas_call(
        body, name=name, in_specs=[_VMEM] * n + rider.in_specs, out_specs=[_VMEM] * (2 * n) + rider.out_specs,
        out_shape=half_shapes + half_shapes + rider.out_shapes,
        scratch_shapes=[pltpu.VMEM((N_CHIPS,) + h, F32) for h in halves] + [pltpu.VMEM((N_CHIPS,) + h, BF16) for h in halves]
        + [pltpu.VMEM((3,) + h, BF16) for h in halves]
        + [pltpu.SemaphoreType.DMA((n_p, N_CHIPS))] * 2 + [pltpu.SemaphoreType.DMA((n_p, 3))] * 2
        + [pltpu.SemaphoreType.DMA((n_p,))] * 2 + rider.sems,
        input_output_aliases={n + i: 2 * n + j for i, j in rider.aliases.items()},
        compiler_params=_params(None, VMEM_LIMIT),
    )(*gs, *rider.ins)
    return list(res[:n]), list(res[n:2 * n]), list(res[2 * n:])


SMALL_ROWS = 32
PACK_ROWS = 16


def _pack_small(st_post, st_ret, st_pre):
    d = st_post.shape[2]

    def body(po_ref, re_ref, pr_ref, o_ref):
        o_ref[...] = jnp.zeros(o_ref.shape, F32)
        o_ref[0:1, :] = pr_ref[0, 2:3, :] + pr_ref[1, 2:3, :] + pr_ref[2, 2:3, :]
        o_ref[1:2, :] = po_ref[0, 4:5, :] + po_ref[1, 4:5, :]
        o_ref[2:3, :] = po_ref[0, 5:6, :] + po_ref[1, 5:6, :]
        o_ref[3:4, 0:512] = re_ref[0, 0:1, :] + re_ref[1, 0:1, :]
        o_ref[4:5, :] = pr_ref[0, 3:4, :] + pr_ref[1, 3:4, :] + pr_ref[2, 3:4, :]
        o_ref[5:6, :] = pr_ref[0, 4:5, :] + pr_ref[1, 4:5, :] + pr_ref[2, 4:5, :]
        lane = lax.broadcasted_iota(jnp.int32, (1, LANES), 1)
        for row, src in ((6, 1), (10, 2)):
            acc = jnp.zeros((1, LANES), F32)
            for hd in range(HEADS):
                grp = re_ref[0, src:src + 1, hd * LANES:(hd + 1) * LANES] + re_ref[1, src:src + 1, hd * LANES:(hd + 1) * LANES]
                acc = acc + jnp.where(lane == hd, grp, 0.0)
            o_ref[row:row + 1, 0:LANES] = acc
        o_ref[7:8, :] = po_ref[0, 6:7, :] + po_ref[1, 6:7, :]
        o_ref[8:9, :] = pr_ref[2, 0:1, :]
        o_ref[9:10, :] = pr_ref[2, 1:2, :]
        for e in range(2):
            b = 12 + 6 * e
            o_ref[b:b + 1, :] = pr_ref[e, 0:1, :]
            o_ref[b + 1:b + 2, :] = pr_ref[e, 1:2, :]
            o_ref[b + 2:b + 3, :] = po_ref[e, 3:4, :]
            o_ref[b + 3:b + 4, :] = po_ref[e, 0:1, :]
            o_ref[b + 4:b + 5, :] = po_ref[e, 1:2, :]
            o_ref[b + 5:b + 6, :] = po_ref[e, 2:3, :]

    return pl.pallas_call(body, name="pack_small", out_shape=jax.ShapeDtypeStruct((SMALL_ROWS, d), F32))(st_post, st_ret, st_pre)


def _small_reduce(gathered):
    d = gathered.shape[2]

    def body(g_ref, o_ref):
        tot = g_ref[0, 0:PACK_ROWS, :]
        for dev in range(1, N_DEV):
            tot = tot + g_ref[dev, 0:PACK_ROWS, :]
        o_ref[0:PACK_ROWS, :] = tot
        for j in range(6):
            acc = g_ref[0, 12 + j:13 + j, :] + g_ref[0, 18 + j:19 + j, :]
            for dev in range(1, N_DEV):
                acc = acc + g_ref[dev, 12 + j:13 + j, :] + g_ref[dev, 18 + j:19 + j, :]
            if j < 2:
                acc = acc + o_ref[8 + j:9 + j, :]
            o_ref[PACK_ROWS + j:PACK_ROWS + j + 1, :] = acc
        o_ref[PACK_ROWS + 6:PACK_ROWS + 8, :] = jnp.zeros((2, d), F32)

    return pl.pallas_call(body, name="small_reduce", out_shape=jax.ShapeDtypeStruct((PACK_ROWS + 8, d), F32))(gathered)


_SMALL = (("g_attn", 0, 1024), ("g_ffn", 1, 1024), ("g_final", 2, 1024), ("g_ret", 3, 512), ("g_q_lora", 4, 384),
          ("g_kv_lora", 5, 256), ("ret_decay_fwd", 6, HEADS), ("ret_decay_bwd", 10, HEADS))
_SMALL_NAMES = tuple(s[0] for s in _SMALL) + ("c_ctx", "b_ada")


def _small_final(tot, dcc, sg8, ws, ms, vs):
    d = tot.shape[1]
    n = len(_SMALL_NAMES)

    def body(*refs):
        t_ref, dcc_ref, sg_ref = refs[0:3]
        w_refs, m_refs, v_refs = refs[3:3 + n], refs[3 + n:3 + 2 * n], refs[3 + 2 * n:3 + 3 * n]
        outs = refs[3 + 3 * n:]
        g_refs, d_refs, mo_refs, vo_refs = outs[0:n], outs[n:2 * n], outs[2 * n:3 * n], outs[3 * n:4 * n]
        l_ref = outs[4 * n]

        def update(i, g, sl=None):
            pick = (lambda r: r[...]) if sl is None else (lambda r: r[:, sl])
            dl, mn, vn = _adam_math(pick(w_refs[i]), g, pick(m_refs[i]), pick(v_refs[i]))
            if sl is None:
                g_refs[i][...], d_refs[i][...], mo_refs[i][...], vo_refs[i][...] = g, dl, mn, vn
            else:
                g_refs[i][:, sl], d_refs[i][:, sl], mo_refs[i][:, sl], vo_refs[i][:, sl] = g, dl, mn, vn

        for i, (name, row, width) in enumerate(_SMALL):
            g = t_ref[row:row + 1, 0:width]
            if name == "ret_decay_fwd":
                g = g * sg_ref[0:1, 0:width]
            elif name == "ret_decay_bwd":
                g = g * sg_ref[1:2, 0:width]
            update(i, g)
        i_cc, i_b = n - 2, n - 1
        cc = w_refs[i_cc][...]
        s = 1.0 / (1.0 + jnp.exp(-cc))
        dsilu = dcc_ref[0, 0:1, :] + dcc_ref[2, 0:1, :] + dcc_ref[4, 0:1, :] + dcc_ref[6, 0:1, :]
        update(i_cc, dsilu * (s * (1.0 + cc * (1.0 - s))))
        for j in range(6):
            update(i_b, t_ref[PACK_ROWS + j:PACK_ROWS + j + 1, :], pl.ds(j * d, d))
        l_ref[...] = jnp.broadcast_to((0.5 / d) * jnp.sum(t_ref[7:8, :], keepdims=True), l_ref.shape)

    shapes = [jax.ShapeDtypeStruct(a.shape, F32) for a in ws]
    outs = pl.pallas_call(
        body, name="small_final", out_shape=shapes * 4 + [jax.ShapeDtypeStruct((8, LANES), F32)],
    )(tot, dcc, sg8, *ws, *ms, *vs)
    return outs[0:n], outs[n:2 * n], outs[2 * n:3 * n], outs[3 * n:4 * n], outs[4 * n]


_WEIGHTS = ("c_ctx", "w_ada", "b_ada", "g_attn", "g_ffn", "w_in", "ret_decay_fwd", "ret_decay_bwd", "g_ret", "g_q_lora",
            "w_uq", "g_kv_lora", "w_ukv", "w_out", "w_ff1", "w_ff2", "g_final")
_BIG = ("w_in", "w_uq", "w_ukv", "w_out", "w_ff1", "w_ff2")
_TRANSPOSED = ("w_in", "w_uq")


def kernel(x, c, ctx, c_ctx, w_ada, b_ada, g_attn, g_ffn, w_in, ret_decay_fwd, ret_decay_bwd, g_ret, g_q_lora, w_uq, g_kv_lora, w_ukv, w_out, w_ff1, w_ff2, g_final, loss_target, m_c_ctx, m_w_ada, m_b_ada, m_g_attn, m_g_ffn, m_w_in, m_ret_decay_fwd, m_ret_decay_bwd, m_g_ret, m_g_q_lora, m_w_uq, m_g_kv_lora, m_w_ukv, m_w_out, m_w_ff1, m_w_ff2, m_g_final, v_c_ctx, v_w_ada, v_b_ada, v_g_attn, v_g_ffn, v_w_in, v_ret_decay_fwd, v_ret_decay_bwd, v_g_ret, v_g_q_lora, v_w_uq, v_g_kv_lora, v_w_ukv, v_w_out, v_w_ff1, v_w_ff2, v_g_final):
    w = dict(c_ctx=c_ctx, w_ada=w_ada, b_ada=b_ada, g_attn=g_attn, g_ffn=g_ffn, w_in=w_in, ret_decay_fwd=ret_decay_fwd,
             ret_decay_bwd=ret_decay_bwd, g_ret=g_ret, g_q_lora=g_q_lora, w_uq=w_uq, g_kv_lora=g_kv_lora, w_ukv=w_ukv,
             w_out=w_out, w_ff1=w_ff1, w_ff2=w_ff2, g_final=g_final)
    m = dict(c_ctx=m_c_ctx, w_ada=m_w_ada, b_ada=m_b_ada, g_attn=m_g_attn, g_ffn=m_g_ffn, w_in=m_w_in,
             ret_decay_fwd=m_ret_decay_fwd, ret_decay_bwd=m_ret_decay_bwd, g_ret=m_g_ret, g_q_lora=m_g_q_lora, w_uq=m_w_uq,
             g_kv_lora=m_g_kv_lora, w_ukv=m_w_ukv, w_out=m_w_out, w_ff1=m_w_ff1, w_ff2=m_w_ff2, g_final=m_g_final)
    v = dict(c_ctx=v_c_ctx, w_ada=v_w_ada, b_ada=v_b_ada, g_attn=v_g_attn, g_ffn=v_g_ffn, w_in=v_w_in,
             ret_decay_fwd=v_ret_decay_fwd, ret_decay_bwd=v_ret_decay_bwd, g_ret=v_g_ret, g_q_lora=v_g_q_lora, w_uq=v_w_uq,
             g_kv_lora=v_g_kv_lora, w_ukv=v_w_ukv, w_out=v_w_out, w_ff1=v_w_ff1, w_ff2=v_w_ff2, g_final=v_g_final)
    xi, yi, ci = lax.axis_index("x"), lax.axis_index("y"), lax.axis_index("c")
    chip = 2 * xi + yi
    dev = 2 * chip + ci
    nex, seq, d = x.shape
    n_ada = w_ada.shape[2]

    dec = jnp.zeros((8, LANES), F32).at[0, :HEADS].set(ret_decay_fwd[0]).at[1, :HEADS].set(ret_decay_bwd[0])
    lg8, sg8 = _decay_prep(dec)
    lg = lg8[:2, :HEADS]

    def shard_of(t, k):
        return t[k][0].T if k in _TRANSPOSED else t[k][0]

    shard = {k: shard_of(w, k) for k in _BIG}
    head_rows = MLA_NOPE + MLA_ROPE
    shard["w_uq"] = jnp.pad(shard["w_uq"], ((0, MLA_HEAD - head_rows), (0, 0)))
    slot = chip.reshape(1).astype(jnp.int32)
    core = ci.reshape(1).astype(jnp.int32)
    slots = {k: _cast_into_slot(shard[k], slot, "cast_" + k) for k in _EARLY}
    half_ff = shard["w_ff2"].shape[0] // 2
    late_pieces = [(shard["w_out"], 0, shard["w_out"].shape[0]), (shard["w_ff1"], 0, shard["w_ff1"].shape[0]),
                   (shard["w_ff2"], 0, half_ff), (shard["w_ff2"], half_ff, half_ff)]
    late_slots, (w_in_f, w_uq_k, w_ukv_k, c8) = _cast_into_slots(
        late_pieces, slot, "cast_late",
        rider=_merge_riders(_gather_rider([slots[k] for k in _EARLY]),
                            _gather8_rider(jnp.pad(c, ((0, 8 - nex), (0, 0))), in_vmem=False)))

    a_in = jnp.concatenate([c8[:, :nex].reshape(N_DEV * nex, d), c_ctx.reshape(1, d), jnp.zeros((7, d), F32)], axis=0)
    b_sh = lax.dynamic_slice(b_ada, (0, chip * n_ada), (1, n_ada))
    mod_sh = _mod_fwd(a_in, w_ada[0], b_sh)
    mod8 = _allgather8(mod_sh, "ag_mod")
    w_in_k = jnp.pad(w_in_f.reshape(IN_COLS, d), ((0, IN_PAD - IN_COLS), (0, 0)))
    mod_all = mod8[0::2].transpose(1, 0, 2).reshape(a_in.shape[0], N_CHIPS * n_ada)
    mod_me = lax.dynamic_slice(mod_all, (nex * dev, 0), (nex, N_CHIPS * n_ada)).reshape(nex, 6, d)
    mod_c = mod_all[N_DEV * nex].reshape(1, 6, d)
    modv = jnp.pad(jnp.concatenate([mod_me, mod_c], axis=0), ((0, 0), (0, 2), (0, 0)))

    gx, g_early, late, st_post, st_ret, st_pre = _local_step(
        x, ctx, loss_target, modv, lg, g_attn, g_ffn, g_final.reshape(1, d), g_ret, g_q_lora, g_kv_lora,
        w_in_k, w_uq_k, w_ukv_k, late_slots, (core, slot))

    mine, theirs, (*late_theirs, gathered) = _reduce_scatter_vmem(
        g_early, [(IN_COLS // N_CHIPS, IN_COLS // N_CHIPS), (head_rows, MLA_HEAD), (KV_LORA, KV_LORA)],
        _merge_riders(_swap_rider(late), _gather8_rider(_pack_small(st_post, st_ret, st_pre))), "rs_early")
    tot = _small_reduce(gathered)
    dm = jnp.concatenate([
        gathered[:, 12:24].reshape(N_DEV * nex, 6 * d),
        jnp.concatenate([tot[8:10].reshape(1, 2 * d), jnp.zeros((1, 4 * d), F32)], axis=1),
        jnp.zeros((7, 6 * d), F32)], axis=0)
    dm_sh = lax.dynamic_slice(dm, (0, chip * n_ada), (dm.shape[0], n_ada))
    g_ada, da = _mod_bwd(a_in, dm_sh, w_ada[0])
    dcc = _allgather8(da[N_DEV * nex:], "ag_dcc")
    halves = dict(zip(_EARLY, zip(mine, theirs)))
    halves.update(zip(_LATE, zip(late, late_theirs)))
    grad, delta, new_m, new_v = {}, {}, {}, {}
    for k in _BIG:
        a, b = halves[k]
        res = _adamw_halves(shard_of(w, k), a, b, shard_of(m, k), shard_of(v, k), core, "adamw_" + k)
        grad[k], delta[k], new_m[k], new_v[k] = [(o.T if k in _TRANSPOSED else o).reshape(w[k].shape) for o in res]

    shp = w_ada.shape
    outs, _ = _adamw(w_ada[0], g_ada, m["w_ada"][0], v["w_ada"][0], "adamw_w_ada")
    grad["w_ada"] = g_ada.reshape(shp)
    delta["w_ada"], new_m["w_ada"], new_v["w_ada"] = [o.reshape(shp) for o in outs]
    rows = [{k: t[k].reshape(1, -1) for k in _SMALL_NAMES} for t in (w, m, v)]
    small = _small_final(tot, dcc, sg8, *[[t[k] for k in _SMALL_NAMES] for t in rows])
    for res, outs in zip((grad, delta, new_m, new_v), small[:4]):
        for k, o in zip(_SMALL_NAMES, outs):
            res[k] = o.reshape(w[k].shape)
    return (small[4][0, 0], gx, *[grad[k] for k in _WEIGHTS], *[delta[k] for k in _WEIGHTS],
            *[new_m[k] for k in _WEIGHTS], *[new_v[k] for k in _WEIGHTS])
```

```python
import functools
import math

import jax
import jax.numpy as jnp
from jax import lax
from jax.experimental import pallas as pl
from jax.experimental.pallas import tpu as pltpu

F32 = jnp.float32
BF16 = jnp.bfloat16
MESH = pl.DeviceIdType.MESH

EPS = 1e-6
D_MODEL = 1024
D_FF = 4096
HEADS = 4
RET_DK = 64
RET_DV = 128
MLA_NOPE = 128
MLA_ROPE = 64
MLA_HEAD = 256
Q_LORA = 384
KV_LORA = 256
GRID_W = 64
ROPE_BASE = 10000.0
IN_COLS = 2240
IN_PAD = 2304
PG_COLS = 1152
N_CHIPS = 4
N_DEV = 8
LANES = 128
ADAM_LR = 0.001
ADAM_B1 = 0.9
ADAM_B2 = 0.999
ADAM_EPS = 1e-08
ADAM_WD = 0.01
ADAM_STEP = 10
VMEM_LIMIT = 56 * 1024 * 1024


def _dot(a, b):
    return jnp.dot(a, b, preferred_element_type=F32)


def _dot_nt(a, b):
    return lax.dot_general(a, b, (((1,), (1,)), ((), ())), preferred_element_type=F32)


def _dot_tn(a, b):
    return lax.dot_general(a, b, (((0,), (0,)), ((), ())), preferred_element_type=F32)


def _params(sem=None, vmem=None):
    return pltpu.CompilerParams(dimension_semantics=sem, vmem_limit_bytes=vmem)


def _full(shape):
    n = len(shape)
    return pl.BlockSpec(shape, lambda *_: (0,) * n)


def _once(shape):
    n = len(shape)
    return pl.BlockSpec(shape, lambda *_: (0,) * n, pipeline_mode=pl.Buffered(1))


def _rope(x, cos, sin):
    w = x.shape[-1]
    lo = (lax.broadcasted_iota(jnp.int32, (1, w), 1) % 64) < 32
    swapped = jnp.where(lo, pltpu.roll(x, w - 32, 1), pltpu.roll(x, 32, 1))
    return x * cos + swapped * sin


def _rope_t(g, cos, sin):
    w = g.shape[-1]
    lo = (lax.broadcasted_iota(jnp.int32, (1, w), 1) % 64) < 32
    t = g * sin
    swapped = jnp.where(lo, pltpu.roll(t, w - 32, 1), pltpu.roll(t, 32, 1))
    return g * cos + swapped


def _rope_tables(seq, tm):
    rows = seq // GRID_W
    row = jnp.repeat(jnp.arange(rows, dtype=F32), GRID_W)
    col = jnp.tile(jnp.arange(GRID_W, dtype=F32), rows)
    n_freq = RET_DK // 4
    freq = ROPE_BASE ** (-jnp.arange(n_freq, dtype=F32) / n_freq)
    ang = jnp.concatenate([row[:, None] * freq, col[:, None] * freq], axis=-1)
    cos, sin = jnp.cos(ang), jnp.sin(ang)
    cos_t = jnp.tile(jnp.concatenate([cos, cos], -1), (1, HEADS))
    sin_t = jnp.tile(jnp.concatenate([-sin, sin], -1), (1, HEADS))
    cos_t = jnp.concatenate([cos_t, jnp.ones((tm, 4 * RET_DK), F32)], 0)
    sin_t = jnp.concatenate([sin_t, jnp.zeros((tm, 4 * RET_DK), F32)], 0)
    return cos_t, sin_t


def _adam_math(w, g, m, v):
    mn = ADAM_B1 * m + (1.0 - ADAM_B1) * g
    vn = ADAM_B2 * v + (1.0 - ADAM_B2) * (g * g)
    m_hat = mn / (1.0 - ADAM_B1 ** ADAM_STEP)
    v_hat = vn / (1.0 - ADAM_B2 ** ADAM_STEP)
    return -ADAM_LR * (m_hat / (jnp.sqrt(v_hat) + ADAM_EPS) + ADAM_WD * w), mn, vn


def _cast_into_slots(pieces, slot, name, rider=None):
    c = pieces[0][0].shape[1]
    rb = max(b for b in range(16, 1025, 16) if b * c * 4 <= (5 << 19)
             and all(cnt % b == 0 and st % b == 0 for _, st, cnt in pieces))
    nbs = [cnt // rb for _, _, cnt in pieces]
    starts = [sum(nbs[:s]) for s in range(len(pieces))]

    def body(s_ref, *refs):
        i = pl.program_id(0)
        for s in range(len(pieces)):
            @pl.when(jnp.logical_and(i >= starts[s], i < starts[s] + nbs[s]))
            def _():
                refs[len(pieces) + s][...] = refs[s][...].astype(BF16)

    in_specs, out_specs = [], []
    for (_, first_row, _), nb, st in zip(pieces, nbs, starts):
        in_specs.append(pl.BlockSpec((rb, c), lambda i, s, nb=nb, st=st, f=first_row // rb: (f + jnp.clip(i - st, 0, nb - 1), 0)))
        out_specs.append(pl.BlockSpec((None, rb, c), lambda i, s, nb=nb, st=st: (s[0], jnp.clip(i - st, 0, nb - 1), 0)))
    return _hosted_call(
        body, [w for w, _, _ in pieces], name=name, grid=(sum(nbs),), prefetch=(slot,), in_specs=in_specs,
        out_specs=out_specs, out_shape=[jax.ShapeDtypeStruct((N_CHIPS, cnt, c), BF16) for _, _, cnt in pieces],
        sem=("arbitrary",), rider=rider)


def _cast_into_slot(w, slot, name):
    return _cast_into_slots([(w, 0, w.shape[0])], slot, name)[0][0]


def _adamw_halves(w, mine, theirs, m, v, core, name):
    r, c = w.shape
    r2 = r // 2
    rb = max(b for b in range(8, r2 + 1, 8) if r2 % b == 0 and b * c * 4 <= (1 << 21))
    nbh = r2 // rb

    def body(z_ref, w_ref, a_ref, b_ref, m_ref, v_ref, g_ref, d_ref, mo_ref, vo_ref):
        here = (pl.program_id(0) // nbh) == z_ref[0]
        gg = jnp.where(here, a_ref[...], b_ref[...])
        g_ref[...] = gg
        d_ref[...], mo_ref[...], vo_ref[...] = _adam_math(w_ref[...], gg, m_ref[...], v_ref[...])

    spec = pl.BlockSpec((rb, c), lambda i, z: (i, 0))
    a_spec = pl.BlockSpec((rb, c), lambda i, z: (jnp.clip(i - z[0] * nbh, 0, nbh - 1), 0))
    b_spec = pl.BlockSpec((rb, c), lambda i, z: (jnp.clip(i - (1 - z[0]) * nbh, 0, nbh - 1), 0))
    shp = jax.ShapeDtypeStruct((r, c), F32)
    return pl.pallas_call(
        body, name=name,
        grid_spec=pltpu.PrefetchScalarGridSpec(
            num_scalar_prefetch=1, grid=(r // rb,), in_specs=[spec, a_spec, b_spec, spec, spec], out_specs=[spec] * 4),
        out_shape=[shp] * 4,
        compiler_params=_params(("parallel",)),
    )(core, w, mine, theirs, m, v)


def _adamw(w, g, m, v, name, rider=None):
    r, c = w.shape
    rb = r
    for cand in (512, 256, 128, 64, 32, 16, 8):
        if r % cand == 0 and cand * c * 4 <= (1 << 21):
            rb = cand
            break
    if r * c * 4 <= (1 << 21):
        rb = r

    def body(w_ref, g_ref, m_ref, v_ref, d_ref, mo_ref, vo_ref):
        d_ref[...], mo_ref[...], vo_ref[...] = _adam_math(w_ref[...], g_ref[...], m_ref[...], v_ref[...])

    spec = pl.BlockSpec((rb, c), lambda i: (i, 0))
    shp = jax.ShapeDtypeStruct((r, c), F32)
    return _hosted_call(
        body, (w, g, m, v), name=name, grid=(r // rb,), in_specs=[spec] * 4, out_specs=[spec] * 3, out_shape=[shp] * 3,
        sem=("parallel",), rider=rider)


def _decay_prep(dec):
    def body(d_ref, lg_ref, sg_ref):
        d = d_ref[...]
        lg_ref[...] = jnp.minimum(d, 0.0) - jnp.log(1.0 + jnp.exp(-jnp.abs(d)))
        sg_ref[...] = 1.0 / (1.0 + jnp.exp(d))

    shp = jax.ShapeDtypeStruct(dec.shape, F32)
    return pl.pallas_call(body, name="decay_prep", out_shape=[shp, shp])(dec)


def _mod_fwd(a_in, w_ada, b_sh):
    rows, d = a_in.shape
    n = w_ada.shape[1]
    bn = 512

    def body(a_ref, w_ref, b_ref, o_ref):
        a = a_ref[...]
        s = (a / (1.0 + jnp.exp(-a))).astype(BF16)
        o_ref[...] = _dot(s, w_ref[...].astype(BF16)) + b_ref[...]

    return pl.pallas_call(
        body, name="mod_fwd", grid=(n // bn,),
        in_specs=[_full((rows, d)), pl.BlockSpec((d, bn), lambda j: (0, j)), pl.BlockSpec((1, bn), lambda j: (0, j))],
        out_specs=pl.BlockSpec((rows, bn), lambda j: (0, j)),
        out_shape=jax.ShapeDtypeStruct((rows, n), F32),
        compiler_params=_params(("parallel",)),
    )(a_in, w_ada, b_sh)


def _mod_bwd(a_in, dm, w_ada):
    rows, d = a_in.shape
    n = w_ada.shape[1]
    bn = 512
    nb = n // bn

    def body(a_ref, dm_ref, w_ref, gw_ref, da_ref):
        j = pl.program_id(0)
        a = a_ref[...]
        s = (a / (1.0 + jnp.exp(-a))).astype(BF16)
        dmb = dm_ref[...].astype(BF16)
        gw_ref[...] = _dot_tn(s, dmb)
        part = _dot_nt(dmb, w_ref[...].astype(BF16))

        @pl.when(j == 0)
        def _():
            da_ref[...] = part

        @pl.when(j > 0)
        def _():
            da_ref[...] += part

    return pl.pallas_call(
        body, name="mod_bwd", grid=(nb,),
        in_specs=[_full((rows, d)), pl.BlockSpec((rows, bn), lambda j: (0, j)), pl.BlockSpec((d, bn), lambda j: (0, j))],
        out_specs=[pl.BlockSpec((d, bn), lambda j: (0, j)), _full((rows, d))],
        out_shape=[jax.ShapeDtypeStruct((d, n), F32), jax.ShapeDtypeStruct((rows, d), F32)],
        compiler_params=_params(("arbitrary",)),
    )(a_in, dm, w_ada)


def _pre_fwd(x2, ctx2, modv, g_attn, w_in, g_q, g_kv, w_uq, w_ukv, cos_t, sin_t, *, seq, tm, rider=None):
    t_lat, d = x2.shape
    t_ctx = ctx2.shape[0]
    nl, nc = t_lat // tm, t_ctx // tm
    n_all = t_lat + t_ctx
    tpe = seq // tm
    nex = t_lat // seq

    def body(x_ref, c_ref, mod_ref, g_ref, win_ref, gq_ref, gkv_ref, wuq_ref, wukv_ref, cos_ref, sin_ref,
             h_ref, pg_ref, rq_ref, rk_ref, rv_ref, nq_ref, nkv_ref, q_ref, k_ref, v_ref):
        i = pl.program_id(0)
        xt = jnp.where(i < nl, x_ref[...], c_ref[...])
        sh = mod_ref[0, 0:1, :]
        sc = mod_ref[0, 1:2, :]
        r = lax.rsqrt(jnp.mean(xt * xt, axis=-1, keepdims=True) + EPS)
        hb = ((xt * r) * g_ref[...] * (1.0 + sc) + sh).astype(BF16)
        h_ref[...] = hb
        p = _dot_nt(hb, win_ref[...])
        cos = cos_ref[...]
        sin = sin_ref[...]
        rq_ref[...] = _rope(p[:, 0:256], cos, sin).astype(BF16)
        rk_ref[...] = _rope(p[:, 256:512] * (RET_DK ** -0.5), cos, sin).astype(BF16)
        rv_ref[...] = p[:, 512:1024].astype(BF16)
        pg_ref[...] = p[:, 1024:2176]
        cq = p[:, 1536:1920]
        ckv = p[:, 1920:2176]
        nqb = (cq * lax.rsqrt(jnp.mean(cq * cq, axis=-1, keepdims=True) + EPS) * gq_ref[...]).astype(BF16)
        nkvb = (ckv * lax.rsqrt(jnp.mean(ckv * ckv, axis=-1, keepdims=True) + EPS) * gkv_ref[...]).astype(BF16)
        nq_ref[...] = nqb
        nkv_ref[...] = nkvb
        cos1 = cos[:, 0:LANES]
        sin1 = sin[:, 0:LANES]
        kpe = _rope(p[:, 2176:2304], cos1, sin1).astype(BF16)
        for hd in range(HEADS):
            o = hd * MLA_HEAD
            qh = _dot_nt(nqb, wuq_ref[hd]) * MLA_SCALE
            q_ref[:, o:o + 128] = qh[:, 0:128].astype(BF16)
            q_ref[:, o + 128:o + 256] = _rope(qh[:, 128:256], cos1, sin1).astype(BF16)
            kvh = _dot(nkvb, wukv_ref[hd])
            k_ref[:, o:o + 128] = kvh[:, 0:128].astype(BF16)
            k_ref[:, o + 128:o + 256] = kpe
            v_ref[:, hd * 128:(hd + 1) * 128] = kvh[:, 128:256].astype(BF16)

    def tile(width):
        return pl.BlockSpec((tm, width), lambda i: (i, 0))

    widths = (d, PG_COLS, 256, 256, 512, Q_LORA, KV_LORA, HEADS * MLA_HEAD, HEADS * MLA_HEAD, HEADS * 128)
    dtypes = (BF16, F32, BF16, BF16, BF16, BF16, BF16, BF16, BF16, BF16)
    tab = pl.BlockSpec((tm, 256), lambda i: (jnp.where(i < nl, i % tpe, tpe), 0))
    return _hosted_call(
        body, (x2, ctx2, modv, g_attn, w_in, g_q, g_kv, w_uq, w_ukv, cos_t, sin_t), name="pre_fwd", grid=(nl + nc,),
        in_specs=[
            pl.BlockSpec((tm, d), lambda i: (jnp.minimum(i, nl - 1), 0)),
            pl.BlockSpec((tm, d), lambda i: (jnp.maximum(i - nl, 0), 0)),
            pl.BlockSpec((1, 8, d), lambda i: (jnp.minimum(i // tpe, nex), 0, 0)),
            _full((1, d)), _full(w_in.shape), _full((1, Q_LORA)), _full((1, KV_LORA)),
            _full(w_uq.shape), _full(w_ukv.shape), tab, tab,
        ],
        out_specs=[tile(w) for w in widths],
        out_shape=[jax.ShapeDtypeStruct((n_all, w), dt) for w, dt in zip(widths, dtypes)],
        sem=("parallel",), rider=rider)


def _post(yret, ymla, x2, tgt2, modv, g_ffn, g_fin, w_out, w_ff1, w_ff2a, w_ff2b, *, seq, tm):
    t_lat, d = x2.shape
    nl = t_lat // tm
    tpe = seq // tm
    nex = t_lat // seq
    n_slab = w_ff1.shape[0]
    fs = w_ff1.shape[2]
    fh = w_ff2a.shape[1]

    def body(yr_ref, ym_ref, x_ref, t_ref, mod_ref, gf_ref, gl_ref, wo_ref, w1_ref, w2a_ref, w2b_ref,
             mix_ref, a_ref, du_ref, h2_ref, df_ref, dmo_ref, dmix_ref, dxm_ref, st_ref, ru_ref):
        i = pl.program_id(0)
        gt_a = mod_ref[0, 2:3, :]
        sh_f = mod_ref[0, 3:4, :]
        sc_f = mod_ref[0, 4:5, :]
        gt_f = mod_ref[0, 5:6, :]
        g_ffn_v = gf_ref[...]
        g_fin_v = gl_ref[...]
        yr = yr_ref[...]
        ym = ym_ref[...]
        mix_ref[:, 0:512] = yr
        mix_ref[:, 512:1024] = ym
        op = _dot(yr, wo_ref[0:512, :]) + _dot(ym, wo_ref[512:1024, :])
        x_mid = x_ref[...] + gt_a * op
        r2 = lax.rsqrt(jnp.mean(x_mid * x_mid, axis=-1, keepdims=True) + EPS)
        xh2 = x_mid * r2
        h2b = (xh2 * g_ffn_v * (1.0 + sc_f) + sh_f).astype(BF16)
        h2_ref[...] = h2b
        f = jnp.zeros((tm, d), F32)
        for s in range(n_slab):
            ru = jnp.maximum(_dot(h2b, w1_ref[s]), 0.0)
            ru_ref[:, s * fs:(s + 1) * fs] = ru
            ab = (ru * ru).astype(BF16)
            a_ref[:, s * fs:(s + 1) * fs] = ab
            f = f + _dot(ab[:, 0:fh], w2a_ref[s]) + _dot(ab[:, fh:fs], w2b_ref[s])
        x_out = x_mid + gt_f * f
        r3 = lax.rsqrt(jnp.mean(x_out * x_out, axis=-1, keepdims=True) + EPS)
        xh3 = x_out * r3
        err = xh3 * g_fin_v - t_ref[...]
        dy = err * (1.0 / d)
        dxh3 = dy * g_fin_v
        dx_out = r3 * (dxh3 - xh3 * jnp.mean(dxh3 * xh3, axis=-1, keepdims=True))
        dfb = (dx_out * gt_f).astype(BF16)
        df_ref[...] = dfb
        dh2 = jnp.zeros((tm, d), F32)
        for s in range(n_slab):
            da = jnp.concatenate([_dot_nt(dfb, w2a_ref[s]), _dot_nt(dfb, w2b_ref[s])], axis=1)
            dub = (da * (2.0 * ru_ref[:, s * fs:(s + 1) * fs])).astype(BF16)
            du_ref[:, s * fs:(s + 1) * fs] = dub
            dh2 = dh2 + _dot_nt(dub, w1_ref[s])
        dxh2 = dh2 * (1.0 + sc_f) * g_ffn_v
        dx_mid = dx_out + r2 * (dxh2 - xh2 * jnp.mean(dxh2 * xh2, axis=-1, keepdims=True))
        dxm_ref[...] = dx_mid
        dmob = (dx_mid * gt_a).astype(BF16)
        dmo_ref[...] = dmob
        dmix_ref[...] = _dot_nt(dmob, wo_ref[...]).astype(BF16)

        def rsum(v):
            return jnp.sum(v, axis=0, keepdims=True)

        stats = jnp.concatenate([
            rsum(dh2), rsum(dh2 * xh2 * g_ffn_v), rsum(dx_out * f), rsum(dx_mid * op),
            rsum(dh2 * (1.0 + sc_f) * xh2), rsum(dy * xh3), rsum(err * err), jnp.zeros((1, d), F32)], axis=0)

        @pl.when(i % tpe == 0)
        def _():
            st_ref[0] = stats

        @pl.when(i % tpe != 0)
        def _():
            st_ref[0] += stats

    def tile(width):
        return pl.BlockSpec((tm, width), lambda i: (i, 0))

    widths = (d, D_FF, D_FF, d, d, d, d, d)
    dtypes = (BF16, BF16, BF16, BF16, BF16, BF16, BF16, F32)
    const = pl.Buffered(1)
    return pl.pallas_call(
        body, name="post", grid=(nl,),
        in_specs=[
            tile(512), tile(512), tile(d), tile(d),
            pl.BlockSpec((1, 8, d), lambda i: (i // tpe, 0, 0)),
            _full((1, d)), _full((1, d)),
            pl.BlockSpec(w_out.shape, lambda i: (0, 0), pipeline_mode=const),
            pl.BlockSpec(w_ff1.shape, lambda i: (0, 0, 0), pipeline_mode=const),
            pl.BlockSpec(w_ff2a.shape, lambda i: (0, 0, 0), pipeline_mode=const),
            pl.BlockSpec(w_ff2b.shape, lambda i: (0, 0, 0), pipeline_mode=const),
        ],
        out_specs=[tile(w) for w in widths] + [pl.BlockSpec((1, 8, d), lambda i: (i // tpe, 0, 0))],
        out_shape=[jax.ShapeDtypeStruct((t_lat, w), dt) for w, dt in zip(widths, dtypes)]
        + [jax.ShapeDtypeStruct((nex, 8, d), F32)],
        scratch_shapes=[pltpu.VMEM((tm, D_FF), F32)],
        compiler_params=_params(("arbitrary",), VMEM_LIMIT),
    )(yret, ymla, x2, tgt2, modv, g_ffn, g_fin, w_out, w_ff1, w_ff2a, w_ff2b)


def _pre_bwd(x2, ctx2, modv, g_attn, pg, drq, drk, dkc_r, drv, dvc_r, drg, dq_m, dkl, dkc, dvl, dvc, dxm,
             w_in, g_q, g_kv, w_uq, w_ukv, cos_t, sin_t, *, seq, tm, rider=None):
    t_lat, d = x2.shape
    t_ctx = ctx2.shape[0]
    nl, nc = t_lat // tm, t_ctx // tm
    n_all = t_lat + t_ctx
    tpe = seq // tm
    nex = t_lat // seq

    def body(x_ref, c_ref, mod_ref, g_ref, pg_ref, drq_ref, drk_ref, dkcr_ref, drv_ref, dvcr_ref, drg_ref,
             dq_ref, dkl_ref, dkc_ref, dvl_ref, dvc_ref, dxm_ref, win_ref, gq_ref, gkv_ref, wuq_ref, wukv_ref,
             cos_ref, sin_ref, dpb_ref, dqf_ref, dkvf_ref, gx_ref, st_ref):
        i = pl.program_id(0)
        lat = i < nl
        latf = lat.astype(F32)
        cos = cos_ref[...]
        sin = sin_ref[...]
        cos1 = cos[:, 0:LANES]
        sin1 = sin[:, 0:LANES]
        d_rq = _rope_t(drq_ref[...] * latf, cos, sin)
        d_rk = _rope_t(jnp.where(lat, drk_ref[...], dkcr_ref[...]), cos, sin) * (RET_DK ** -0.5)
        d_rv = jnp.where(lat, drv_ref[...], dvcr_ref[...])
        d_rg = drg_ref[...] * latf
        dq_all = dq_ref[...] * (latf * MLA_SCALE)
        dk_all = jnp.where(lat, dkl_ref[...], dkc_ref[...])
        dv_all = jnp.where(lat, dvl_ref[...], dvc_ref[...])
        dnq = jnp.zeros((tm, Q_LORA), F32)
        dnkv = jnp.zeros((tm, KV_LORA), F32)
        dkpe = jnp.zeros((tm, LANES), F32)
        for hd in range(HEADS):
            o = hd * MLA_HEAD
            dqh = jnp.concatenate([dq_all[:, o:o + 128], _rope_t(dq_all[:, o + 128:o + 256], cos1, sin1)],
                                  axis=1).astype(BF16)
            dqf_ref[:, o:o + 256] = dqh
            dnq = dnq + _dot(dqh, wuq_ref[hd])
            dkpe = dkpe + dk_all[:, o + 128:o + 256]
            dkvh = jnp.concatenate([dk_all[:, o:o + 128], dv_all[:, hd * 128:(hd + 1) * 128]], axis=1).astype(BF16)
            dkvf_ref[:, o:o + 256] = dkvh
            dnkv = dnkv + _dot_nt(dkvh, wukv_ref[hd])
        d_kpe = _rope_t(dkpe, cos1, sin1)
        pgv = pg_ref[...]
        cq = pgv[:, 512:896]
        ckv = pgv[:, 896:1152]
        rq_ = lax.rsqrt(jnp.mean(cq * cq, axis=-1, keepdims=True) + EPS)
        cqh = cq * rq_
        dcqh = dnq * gq_ref[...]
        d_cq = rq_ * (dcqh - cqh * jnp.mean(dcqh * cqh, axis=-1, keepdims=True))
        rkv_ = lax.rsqrt(jnp.mean(ckv * ckv, axis=-1, keepdims=True) + EPS)
        ckvh = ckv * rkv_
        dckvh = dnkv * gkv_ref[...]
        d_ckv = rkv_ * (dckvh - ckvh * jnp.mean(dckvh * ckvh, axis=-1, keepdims=True))
        dpb = jnp.concatenate([d_rq, d_rk, d_rv, d_rg, d_cq, d_ckv, d_kpe], axis=1).astype(BF16)
        dpb_ref[...] = dpb
        dh = _dot(dpb, win_ref[...])
        xt = jnp.where(lat, x_ref[...], c_ref[...])
        sc = mod_ref[0, 1:2, :]
        g = g_ref[...]
        r = lax.rsqrt(jnp.mean(xt * xt, axis=-1, keepdims=True) + EPS)
        xh = xt * r
        dxh = dh * (1.0 + sc) * g
        dx = r * (dxh - xh * jnp.mean(dxh * xh, axis=-1, keepdims=True))

        @pl.when(lat)
        def _():
            gx_ref[...] = dxm_ref[...] + dx

        def rsum(v):
            return jnp.sum(v, axis=0, keepdims=True)

        def widen(v):
            return jnp.concatenate([v, jnp.zeros((1, d - v.shape[1]), F32)], axis=1)

        stats = jnp.concatenate([
            rsum(dh), rsum(dh * xh * g), rsum(dh * (1.0 + sc) * xh), widen(rsum(dnq * cqh)), widen(rsum(dnkv * ckvh)),
            jnp.zeros((3, d), F32)], axis=0)
        first = jnp.logical_or(jnp.logical_and(lat, i % tpe == 0), i == nl)

        @pl.when(first)
        def _():
            st_ref[0] = stats

        @pl.when(jnp.logical_not(first))
        def _():
            st_ref[0] += stats

    def lat_tile(width):
        return pl.BlockSpec((tm, width), lambda i: (jnp.minimum(i, nl - 1), 0))

    def ctx_tile(width):
        return pl.BlockSpec((tm, width), lambda i: (jnp.maximum(i - nl, 0), 0))

    def tile(width):
        return pl.BlockSpec((tm, width), lambda i: (i, 0))

    tab = pl.BlockSpec((tm, 256), lambda i: (jnp.where(i < nl, i % tpe, tpe), 0))
    ex = pl.BlockSpec((1, 8, d), lambda i: (jnp.minimum(i // tpe, nex), 0, 0))
    return _hosted_call(
        body, (x2, ctx2, modv, g_attn, pg, drq, drk, dkc_r, drv, dvc_r, drg, dq_m, dkl, dkc, dvl, dvc, dxm,
               w_in, g_q, g_kv, w_uq, w_ukv, cos_t, sin_t), name="pre_bwd", grid=(nl + nc,),
        in_specs=[
            lat_tile(d), ctx_tile(d), ex, _full((1, d)), tile(PG_COLS),
            lat_tile(256), lat_tile(256), ctx_tile(256), lat_tile(512), ctx_tile(512), lat_tile(512),
            lat_tile(1024), lat_tile(1024), ctx_tile(1024), lat_tile(512), ctx_tile(512), lat_tile(d),
            _once(w_in.shape), _full((1, Q_LORA)), _full((1, KV_LORA)), _once(w_uq.shape), _once(w_ukv.shape),
            tab, tab,
        ],
        out_specs=[tile(IN_PAD), tile(1024), tile(1024), lat_tile(d), ex],
        out_shape=[
            jax.ShapeDtypeStruct((n_all, IN_PAD), BF16), jax.ShapeDtypeStruct((n_all, 1024), BF16),
            jax.ShapeDtypeStruct((n_all, 1024), BF16), jax.ShapeDtypeStruct((t_lat, d), F32),
            jax.ShapeDtypeStruct((nex + 1, 8, d), F32),
        ],
        sem=("arbitrary",), rider=rider)


MLA_SCALE = 1.0 / math.sqrt(MLA_NOPE + MLA_ROPE)
KEY_BLOCK = 1024


def _mla_specs(t_lat, seq, ctx_len, tq, heads=1):
    nqt = seq // tq
    cb = t_lat // ctx_len
    q = pl.BlockSpec((tq, heads * MLA_HEAD), lambda b, h, j: (b * nqt + j, h))
    kl = pl.BlockSpec((seq, heads * MLA_HEAD), lambda b, h, j: (b, h))
    kc = pl.BlockSpec((ctx_len, heads * MLA_HEAD), lambda b, h, j: (cb + b, h))
    vl = pl.BlockSpec((seq, heads * 128), lambda b, h, j: (b, h))
    vc = pl.BlockSpec((ctx_len, heads * 128), lambda b, h, j: (cb + b, h))
    o = pl.BlockSpec((tq, heads * 128), lambda b, h, j: (b * nqt + j, h))
    return q, kl, kc, vl, vc, o


FWD_HEADS = 2
BWD_HEADS = 1


def _mla_fwd(q, k, v, *, t_lat, seq, ctx_len, tq, rider=None):
    nex = t_lat // seq

    def body(q_ref, kl_ref, kc_ref, vl_ref, vc_ref, o_ref, lse_ref):
        for hh in range(FWD_HEADS):
            wide = slice(hh * MLA_HEAD, (hh + 1) * MLA_HEAD)
            cols = slice(hh * 128, (hh + 1) * 128)
            qb = q_ref[:, wide]
            s = _dot_nt(qb, kl_ref[:, wide])
            sc = _dot_nt(qb, kc_ref[:, wide])
            m = jnp.maximum(jnp.max(s, axis=-1, keepdims=True), jnp.max(sc, axis=-1, keepdims=True))
            p = jnp.exp(s - m)
            pc = jnp.exp(sc - m)
            total = jnp.sum(p, axis=-1, keepdims=True) + jnp.sum(pc, axis=-1, keepdims=True)
            o = _dot(p.astype(BF16), vl_ref[:, cols]) + _dot(pc.astype(BF16), vc_ref[:, cols])
            o_ref[:, cols] = (o * (1.0 / total)).astype(BF16)
            lse_ref[:, cols] = jnp.broadcast_to(m + jnp.log(total), (tq, 128))

    qs, kl, kc, vl, vc, os_ = _mla_specs(t_lat, seq, ctx_len, tq, FWD_HEADS)
    return _hosted_call(
        body, (q, k, k, v, v), name="mla_fwd", grid=(nex, HEADS // FWD_HEADS, seq // tq),
        in_specs=[qs, kl, kc, vl, vc], out_specs=[os_, os_],
        out_shape=[jax.ShapeDtypeStruct((t_lat, HEADS * 128), BF16), jax.ShapeDtypeStruct((t_lat, HEADS * 128), F32)],
        sem=("parallel", "parallel", "arbitrary"), rider=rider)


def _mla_bwd(q, k, v, ymla, lse, dmix, *, t_lat, seq, ctx_len, tq, rider=None):
    nex = t_lat // seq
    nqt = seq // tq
    t_ctx = nex * ctx_len
    kb = min(KEY_BLOCK, seq)

    def body(q_ref, kl_ref, kc_ref, vl_ref, vc_ref, o_ref, lse_ref, do_ref, dq_ref, dkl_out, dkc_out, dvl_out, dvc_out,
             dkl_ref, dkc_ref, dvl_ref, dvc_ref):
        j = pl.program_id(2)

        @pl.when(j == 0)
        def _():
            dkl_ref[...] = jnp.zeros(dkl_ref.shape, F32)
            dkc_ref[...] = jnp.zeros(dkc_ref.shape, F32)
            dvl_ref[...] = jnp.zeros(dvl_ref.shape, F32)
            dvc_ref[...] = jnp.zeros(dvc_ref.shape, F32)

        for hh in range(BWD_HEADS):
            wide = slice(hh * MLA_HEAD, (hh + 1) * MLA_HEAD)
            cols = slice(hh * 128, (hh + 1) * 128)
            qb = q_ref[:, wide]
            dob = do_ref[:, cols]
            delta = jnp.sum(dob.astype(F32) * o_ref[:, cols].astype(F32), axis=-1, keepdims=True)
            lse_row = lse_ref[:, hh * 128:hh * 128 + 1]

            def block(k_ref, v_ref, dk_ref, dv_ref, rows):
                kbl = k_ref[rows, wide]
                vbl = v_ref[rows, cols]
                p = jnp.exp(_dot_nt(qb, kbl) - lse_row)
                ds = (p * (_dot_nt(dob, vbl) - delta)).astype(BF16)
                dk_ref[rows, wide] += _dot_tn(ds, qb)
                dv_ref[rows, cols] += _dot_tn(p.astype(BF16), dob)
                return _dot(ds, kbl)

            dq = block(kc_ref, vc_ref, dkc_ref, dvc_ref, pl.ds(0, ctx_len))
            for i in range(seq // kb):
                dq = dq + block(kl_ref, vl_ref, dkl_ref, dvl_ref, pl.ds(i * kb, kb))
            dq_ref[:, wide] = dq.astype(BF16)

        @pl.when(j == nqt - 1)
        def _():
            dkl_out[...] = dkl_ref[...].astype(BF16)
            dkc_out[...] = dkc_ref[...].astype(BF16)
            dvl_out[...] = dvl_ref[...].astype(BF16)
            dvc_out[...] = dvc_ref[...].astype(BF16)

    g = BWD_HEADS
    qs, kl, kc, vl, vc, os_ = _mla_specs(t_lat, seq, ctx_len, tq, g)
    do_spec = pl.BlockSpec((tq, g * 128), lambda b, h, j: (b * nqt + j, HEADS // g + h))
    key_blocks = [(seq, g * MLA_HEAD), (ctx_len, g * MLA_HEAD), (seq, g * 128), (ctx_len, g * 128)]
    return _hosted_call(
        body, (q, k, k, v, v, ymla, lse, dmix), name="mla_bwd", grid=(nex, HEADS // g, nqt),
        in_specs=[qs, kl, kc, vl, vc, os_, os_, do_spec],
        out_specs=[qs] + [pl.BlockSpec(blk, lambda b, h, j: (b, h)) for blk in key_blocks],
        out_shape=[
            jax.ShapeDtypeStruct((t_lat, HEADS * MLA_HEAD), BF16),
            jax.ShapeDtypeStruct((t_lat, HEADS * MLA_HEAD), BF16),
            jax.ShapeDtypeStruct((t_ctx, HEADS * MLA_HEAD), BF16),
            jax.ShapeDtypeStruct((t_lat, HEADS * 128), BF16),
            jax.ShapeDtypeStruct((t_ctx, HEADS * 128), BF16),
        ],
        scratch_shapes=[pltpu.VMEM(blk, F32) for blk in key_blocks],
        sem=("parallel", "parallel", "arbitrary"), rider=rider)


def _decay_terms(lg, chunk, forward):
    ii = lax.broadcasted_iota(jnp.int32, (chunk, chunk), 0)
    jj = lax.broadcasted_iota(jnp.int32, (chunk, chunk), 1)
    diff = (ii - jj) if forward else (jj - ii)
    dist = jnp.maximum(diff, 0).astype(F32)
    dmat = jnp.where(diff >= 0, jnp.exp(lg * dist), 0.0)
    pos = lax.broadcasted_iota(jnp.int32, (chunk, 1), 0).astype(F32)
    if forward:
        e_q = pos + 1.0
        e_k = (chunk - 1.0) - pos
    else:
        e_q = chunk - pos
        e_k = pos
    wq = jnp.exp(lg * e_q)
    wk = jnp.exp(lg * e_k)
    cd = jnp.exp(jnp.full((1, 1), lg * chunk, F32))
    return dmat, dist, wq, wk, e_q, e_k, cd


def _ctx_weights(lg, ctx_len, forward):
    pos = lax.broadcasted_iota(jnp.int32, (ctx_len, 1), 0).astype(F32)
    e = ((ctx_len - 1.0) - pos) if forward else pos
    return jnp.exp(lg * e), e


def _pair_specs(t_lat, seq, ctx_len):
    cb = t_lat // ctx_len
    qk = pl.BlockSpec((seq, 128), lambda b, p: (b, p))
    v = pl.BlockSpec((seq, 256), lambda b, p: (b, p))
    kc = pl.BlockSpec((ctx_len, 128), lambda b, p: (cb + b, p))
    vc = pl.BlockSpec((ctx_len, 256), lambda b, p: (cb + b, p))
    return qk, v, kc, vc


def _lane_masks():
    lane = lax.broadcasted_iota(jnp.int32, (1, 128), 1)
    return [(lane // RET_DK) == hh for hh in (0, 1)]


def _ret_fwd_pair(rq, rk, rv, pg, lg, g_ret, *, t_lat, seq, ctx_len, chunk, rider=None):
    nex = t_lat // seq
    n_chunk = seq // chunk

    def body(q_ref, k_ref, v_ref, kc_ref, vc_ref, rg_ref, lg_ref, g_ref, y_ref, o_ref):
        pair = pl.program_id(1)
        masks = _lane_masks()
        kcf = kc_ref[...].astype(F32)
        chains = [(forward, hh) for forward in (True, False) for hh in (0, 1)]
        terms, s0 = [], []
        for forward, hh in chains:
            lgd = lg_ref[0 if forward else 1, 2 * pair + hh]
            terms.append(_decay_terms(lgd, chunk, forward))
            wc, _ = _ctx_weights(lgd, ctx_len, forward)
            s0.append(_dot_tn((jnp.where(masks[hh], kcf, 0.0) * wc).astype(BF16), vc_ref[:, hh * 128:(hh + 1) * 128]))
        both = [terms[hh][0] + terms[2 + hh][0] for hh in (0, 1)]
        o_ref[...] = jnp.zeros(o_ref.shape, F32)

        def step(t, states):
            new = [None] * 4
            for forward in (True, False):
                n = t if forward else n_chunk - 1 - t
                sl = pl.ds(pl.multiple_of(n * chunk, chunk), chunk)
                qb = q_ref[sl, :]
                kf_all = k_ref[sl, :].astype(F32)
                for hh in (0, 1):
                    c = (0 if forward else 2) + hh
                    _, _, wq, wk, _, _, cd = terms[c]
                    cols = slice(hh * 128, (hh + 1) * 128)
                    qm = jnp.where(masks[hh], qb, jnp.zeros((), BF16))
                    kf = jnp.where(masks[hh], kf_all, 0.0)
                    vb = v_ref[sl, cols]
                    o = wq * _dot(qm, states[c].astype(BF16))
                    if forward:
                        o = o + _dot((_dot_nt(qm, kf.astype(BF16)) * both[hh]).astype(BF16), vb)
                    o_ref[sl, cols] += o
                    new[c] = cd * states[c] + _dot_tn((kf * wk).astype(BF16), vb)
            return tuple(new)

        lax.fori_loop(0, n_chunk, step, tuple(s0))

        def norm_step(n, carry):
            sl = pl.ds(pl.multiple_of(n * chunk, chunk), chunk)
            for hh in (0, 1):
                cols = slice(hh * 128, (hh + 1) * 128)
                o = o_ref[sl, cols]
                mu = jnp.mean(o, axis=-1, keepdims=True)
                oc = o - mu
                var = jnp.mean(oc * oc, axis=-1, keepdims=True)
                rg = rg_ref[sl, cols]
                y_ref[sl, cols] = (oc * lax.rsqrt(var + EPS) * g_ref[:, cols] * (rg / (1.0 + jnp.exp(-rg)))).astype(BF16)
            return carry

        lax.fori_loop(0, n_chunk, norm_step, 0)

    qk, v, kc, vc = _pair_specs(t_lat, seq, ctx_len)
    return _hosted_call(
        body, (rq, rk, rv, rk, rv, pg, lg, g_ret), name="ret_fwd", grid=(nex, HEADS // 2),
        in_specs=[qk, qk, v, kc, vc, v, pl.BlockSpec(memory_space=pltpu.SMEM), pl.BlockSpec((1, 256), lambda b, p: (0, p))],
        out_specs=[v, v],
        out_shape=[jax.ShapeDtypeStruct((t_lat, HEADS * RET_DV), BF16), jax.ShapeDtypeStruct((t_lat, HEADS * RET_DV), F32)],
        sem=("parallel", "arbitrary"), rider=rider)


def _ret_bwd_pair(rq, rk, rv, pg, osum, dmix, lg, g_ret, *, t_lat, seq, ctx_len, chunk, rider=None):
    nex = t_lat // seq
    n_chunk = seq // chunk
    t_ctx = nex * ctx_len

    def body(q_ref, k_ref, v_ref, kc_ref, vc_ref, rg_ref, o_ref, dy_ref, lg_ref, g_ref,
             dq_out, dk_out, dv_out, dkc_ref, dvc_ref, drg_ref, st_ref, do_s, s_st, dq_ref, dk_ref, dv_ref):
        pair = pl.program_id(1)
        masks = _lane_masks()
        kcf = kc_ref[...].astype(F32)

        def norm_step(n, dgains):
            sl = pl.ds(pl.multiple_of(n * chunk, chunk), chunk)
            out = []
            for hh in (0, 1):
                cols = slice(hh * 128, (hh + 1) * 128)
                gain = g_ref[:, cols]
                o = o_ref[sl, cols]
                mu = jnp.mean(o, axis=-1, keepdims=True)
                oc = o - mu
                rstd = lax.rsqrt(jnp.mean(oc * oc, axis=-1, keepdims=True) + EPS)
                ohat = oc * rstd
                rg = rg_ref[sl, cols]
                sg = 1.0 / (1.0 + jnp.exp(-rg))
                dy = dy_ref[sl, cols].astype(F32)
                don = dy * (rg * sg)
                drg_ref[sl, cols] = (dy * (ohat * gain) * (sg * (1.0 + rg * (1.0 - sg)))).astype(BF16)
                dohat = don * gain
                do_s[sl, cols] = rstd * (dohat - jnp.mean(dohat, axis=-1, keepdims=True)
                                         - ohat * jnp.mean(dohat * ohat, axis=-1, keepdims=True))
                out.append(dgains[hh] + jnp.sum(don * ohat, axis=0, keepdims=True))
            return tuple(out)

        zero_row = jnp.zeros((1, 128), F32)
        dgains = lax.fori_loop(0, n_chunk, norm_step, (zero_row, zero_row))
        dq_ref[...] = jnp.zeros(dq_ref.shape, F32)
        dk_ref[...] = jnp.zeros(dk_ref.shape, F32)
        dv_ref[...] = jnp.zeros(dv_ref.shape, F32)

        chains = [(forward, hh) for forward in (True, False) for hh in (0, 1)]
        terms, ctxw, s0 = [], [], []
        for forward, hh in chains:
            lgd = lg_ref[0 if forward else 1, 2 * pair + hh]
            terms.append(_decay_terms(lgd, chunk, forward))
            ctxw.append(_ctx_weights(lgd, ctx_len, forward))
            s0.append(_dot_tn((jnp.where(masks[hh], kcf, 0.0) * ctxw[-1][0]).astype(BF16), vc_ref[:, hh * 128:(hh + 1) * 128]))

        def chunk_at(t, ascending):
            n = t if ascending else n_chunk - 1 - t
            return n, pl.ds(pl.multiple_of(n * chunk, chunk), chunk)

        def state_step(t, states):
            new = []
            for c, (forward, hh) in enumerate(chains):
                n, sl = chunk_at(t, forward)
                wk, cd = terms[c][3], terms[c][6]
                s_st[c, n] = states[c]
                kf = jnp.where(masks[hh], k_ref[sl, :].astype(F32), 0.0)
                new.append(cd * states[c] + _dot_tn((kf * wk).astype(BF16), v_ref[sl, hh * 128:(hh + 1) * 128]))
            return tuple(new)

        lax.fori_loop(0, n_chunk, state_step, tuple(s0))

        both = [terms[hh][0] + terms[2 + hh][0] for hh in (0, 1)]

        def grad_step(t, carry):
            out = [None] * len(chains)
            in_chunk_b = [None, None]
            for forward in (True, False):
                n, sl = chunk_at(t, not forward)
                qb = q_ref[sl, :]
                kf_all = k_ref[sl, :].astype(F32)
                dq_sum = jnp.zeros((chunk, 128), F32)
                dk_sum = jnp.zeros((chunk, 128), F32)
                for hh in (0, 1):
                    c = (0 if forward else 2) + hh
                    g_next, dlg = carry[c]
                    dmat, dist, wq, wk, e_q, e_k, cd = terms[c]
                    cols = slice(hh * 128, (hh + 1) * 128)
                    qm = jnp.where(masks[hh], qb, jnp.zeros((), BF16))
                    kf = jnp.where(masks[hh], kf_all, 0.0)
                    kb = kf.astype(BF16)
                    vb = v_ref[sl, cols]
                    do = do_s[sl, cols]
                    dob = do.astype(BF16)
                    s_n = s_st[c, n]
                    s_nb = s_n.astype(BF16)
                    gb = g_next.astype(BF16)
                    dk_cross = wk * _dot_nt(vb, gb)
                    dv = _dot((kf * wk).astype(BF16), gb)
                    o_cross = wq * _dot(qm, s_nb)
                    dq_sum = dq_sum + wq * _dot_nt(dob, s_nb)
                    dk_sum = dk_sum + dk_cross
                    dlg = (dlg + chunk * cd * jnp.sum(g_next * s_n, keepdims=True)
                           + jnp.sum(e_k * jnp.sum(kf * dk_cross, axis=-1, keepdims=True), keepdims=True)
                           + jnp.sum(e_q * jnp.sum(o_cross * do, axis=-1, keepdims=True), keepdims=True))
                    if forward:
                        a_raw = _dot_nt(qm, kb)
                        da_raw = _dot_nt(dob, vb)
                        prod = a_raw * da_raw
                        dlg = dlg + jnp.sum(dist * dmat * prod, keepdims=True)
                        in_chunk_b[hh] = jnp.sum(terms[2 + hh][1] * terms[2 + hh][0] * prod, keepdims=True)
                        dab = (da_raw * both[hh]).astype(BF16)
                        dq_sum = dq_sum + _dot(dab, kb)
                        dk_sum = dk_sum + _dot_tn(dab, qm)
                        dv = dv + _dot_tn((a_raw * both[hh]).astype(BF16), dob)
                    else:
                        dlg = dlg + in_chunk_b[hh]
                    dv_ref[sl, cols] += dv
                    out[c] = (cd * g_next + _dot_tn((qm.astype(F32) * wq).astype(BF16), dob), dlg)
                dq_ref[sl, :] += dq_sum
                dk_ref[sl, :] += dk_sum
            return tuple(out)

        zero = (jnp.zeros((128, 128), F32), jnp.zeros((1, 1), F32))
        res = lax.fori_loop(0, n_chunk, grad_step, (zero,) * len(chains))
        dkc_sum = jnp.zeros((ctx_len, 128), F32)
        dvc = [jnp.zeros((ctx_len, 128), F32)] * 2
        dlgs = []
        for c, (forward, hh) in enumerate(chains):
            ds0, dlg = res[c]
            wc, e_c = ctxw[c]
            kcm = jnp.where(masks[hh], kcf, 0.0)
            ds0b = ds0.astype(BF16)
            dkc_part = wc * _dot_nt(vc_ref[:, hh * 128:(hh + 1) * 128], ds0b)
            dkc_sum = dkc_sum + dkc_part
            dvc[hh] = dvc[hh] + _dot((kcm * wc).astype(BF16), ds0b)
            dlgs.append(dlg + jnp.sum(e_c * jnp.sum(kcm * dkc_part, axis=-1, keepdims=True), keepdims=True))
        dq_out[...] = dq_ref[...].astype(BF16)
        dk_out[...] = dk_ref[...].astype(BF16)
        dv_out[...] = dv_ref[...].astype(BF16)
        dkc_ref[...] = dkc_sum
        for hh in (0, 1):
            cols = slice(hh * 128, (hh + 1) * 128)
            dvc_ref[:, cols] = dvc[hh]
            st_ref[0, :, cols] = jnp.concatenate([
                dgains[hh], jnp.broadcast_to(dlgs[hh], (1, 128)), jnp.broadcast_to(dlgs[2 + hh], (1, 128)),
                jnp.zeros((5, 128), F32)], axis=0)

    qk, v, kc, vc = _pair_specs(t_lat, seq, ctx_len)
    return _hosted_call(
        body, (rq, rk, rv, rk, rv, pg, osum, dmix, lg, g_ret), name="ret_bwd", grid=(nex, HEADS // 2),
        in_specs=[qk, qk, v, kc, vc, v, v, v, pl.BlockSpec(memory_space=pltpu.SMEM),
                  pl.BlockSpec((1, 256), lambda b, p: (0, p))],
        out_specs=[
            qk, qk, v,
            pl.BlockSpec((ctx_len, 128), lambda b, p: (b, p)),
            pl.BlockSpec((ctx_len, 256), lambda b, p: (b, p)),
            v,
            pl.BlockSpec((1, 8, 256), lambda b, p: (b, 0, p)),
        ],
        out_shape=[
            jax.ShapeDtypeStruct((t_lat, 256), BF16), jax.ShapeDtypeStruct((t_lat, 256), BF16),
            jax.ShapeDtypeStruct((t_lat, 512), BF16), jax.ShapeDtypeStruct((t_ctx, 256), F32),
            jax.ShapeDtypeStruct((t_ctx, 512), F32), jax.ShapeDtypeStruct((t_lat, 512), BF16),
            jax.ShapeDtypeStruct((nex, 8, 512), F32),
        ],
        scratch_shapes=[pltpu.VMEM((seq, 256), F32), pltpu.VMEM((4, n_chunk, 128, 128), F32),
                        pltpu.VMEM((seq, 128), F32), pltpu.VMEM((seq, 128), F32), pltpu.VMEM((seq, 256), F32)],
        sem=("parallel", "arbitrary"), rider=rider)


def _matmul_tn(a, b, *, bm, bn, bk, chip_major, name, out_dtype=F32, rider=None):
    tk, m = a.shape
    n = b.shape[1]
    slab = n // N_CHIPS
    per_block = bn // slab if chip_major else 1
    bk = max(c for c in range(LANES, min(bk, tk) + 1, LANES) if tk % c == 0)
    nk = tk // bk
    blk = (per_block, bm, slab) if chip_major else (bm, bn)

    def body(a_ref, b_ref, o_ref, acc_ref):
        k = pl.program_id(2)
        if chip_major:
            parts = [_dot_tn(a_ref[...], b_ref[:, s * slab:(s + 1) * slab]) for s in range(per_block)]
        else:
            parts = [_dot_tn(a_ref[...], b_ref[...])]

        @pl.when(k == 0)
        def _():
            for s, part in enumerate(parts):
                if chip_major:
                    acc_ref[s] = part
                else:
                    acc_ref[...] = part

        @pl.when(k > 0)
        def _():
            for s, part in enumerate(parts):
                if chip_major:
                    acc_ref[s] += part
                else:
                    acc_ref[...] += part

        @pl.when(k == nk - 1)
        def _():
            o_ref[...] = acc_ref[...].astype(out_dtype)

    if chip_major:
        out_spec = pl.BlockSpec(blk, lambda i, j, k: (j, i, 0))
        out_shape = jax.ShapeDtypeStruct((N_CHIPS, m, slab), out_dtype)
    else:
        out_spec = pl.BlockSpec(blk, lambda i, j, k: (i, j))
        out_shape = jax.ShapeDtypeStruct((m, n), out_dtype)
    (out,), carried = _hosted_call(
        body, (a, b), name=name, grid=(m // bm, n // bn, nk),
        in_specs=[pl.BlockSpec((bk, bm), lambda i, j, k: (k, i)), pl.BlockSpec((bk, bn), lambda i, j, k: (k, j))],
        out_specs=[out_spec], out_shape=[out_shape], scratch_shapes=[pltpu.VMEM(blk, F32)],
        sem=("parallel", "parallel", "arbitrary"), rider=rider)
    return out if rider is None else (out, carried)


_LATE = ("w_out", "w_ff1", "w_ff2")
_EARLY = ("w_in", "w_uq", "w_ukv")


def _local_step(x, ctx, tgt, modv, lg, g_attn, g_ffn, g_fin, g_ret, g_q, g_kv, w_in, w_uq, w_ukv, late, place=None,
                *, tm=256, tq=256, chunk=256):
    nex, seq, d = x.shape
    ctx_len = ctx.shape[1]
    t_lat = nex * seq
    tm = min(tm, seq)
    x2 = x.reshape(t_lat, d)
    ctx2 = ctx.reshape(nex * ctx_len, d)
    tgt2 = tgt.reshape(t_lat, d)
    tm_fwd = min(2 * tm, seq)
    cos_t, sin_t = _rope_tables(seq, tm)
    dims = dict(t_lat=t_lat, seq=seq, ctx_len=ctx_len)
    alone = place is None

    (hb, pg, rq, rk, rv, nq, nkv, q, k, v), crossed_a = _pre_fwd(
        x2, ctx2, modv, g_attn, w_in, g_q, g_kv, w_uq, w_ukv, *_rope_tables(seq, tm_fwd), seq=seq, tm=tm_fwd,
        rider=None if alone else _gather_ici_rider([late[2]]))
    (yret, osum), got = _ret_fwd_pair(
        rq, rk, rv, pg, lg, g_ret, chunk=min(2 * chunk, seq), **dims,
        rider=None if alone else _merge_riders(_gather_d2d_rider(crossed_a), _gather_ici_rider([late[3]])))
    (ymla, lse), got_rest = _mla_fwd(
        q, k, v, tq=tq, **dims,
        rider=None if alone else _merge_riders(_gather_rider([late[0], late[1]], staged=True), _gather_d2d_rider(got[1:])))
    w_out, w_ff1, w_ff2a, w_ff2b = late if alone else (got_rest[0], got_rest[1], got[0], got_rest[2])
    mix, act, du, h2, df, dmo, dmix, dxm, st_post = _post(yret, ymla, x2, tgt2, modv, g_ffn, g_fin, w_out.reshape(d, d),
                                                         w_ff1, w_ff2a, w_ff2b, seq=seq, tm=min(tm, 256))
    kw = dict(bm=1024, bn=1024, bk=2048, out_dtype=BF16)
    g_ff2 = _matmul_tn(act, df, chip_major=False, name="gw_ff2", **kw).reshape(N_CHIPS, D_FF // N_CHIPS, d)
    if alone:
        g_ff1 = _matmul_tn(h2, du, chip_major=True, name="gw_ff1", **kw)
        g_out = _matmul_tn(mix, dmo, chip_major=False, name="gw_out", **kw).reshape(N_CHIPS, d // N_CHIPS, d)
        (dq_m, dkl, dkc, dvl, dvc), _ = _mla_bwd(q, k, v, ymla, lse, dmix, tq=tq, **dims)
        (drq, drk, drv, dkc_r, dvc_r, drg, st_ret), _ = _ret_bwd_pair(rq, rk, rv, pg, osum, dmix, lg, g_ret, chunk=chunk,
                                                                      **dims)
        late_out = [g_out, g_ff1, g_ff2]
    else:
        core, slot = place
        g_ff1, x_ff2 = _matmul_tn(h2, du, chip_major=True, name="gw_ff1", rider=_exchange_rider([g_ff2]), **kw)
        g_out, x_ff1 = _matmul_tn(mix, dmo, chip_major=False, name="gw_out", rider=_exchange_rider([g_ff1]), **kw)
        g_out = g_out.reshape(N_CHIPS, d // N_CHIPS, d)
        p_ff2 = _add_half(g_ff2, x_ff2[0], core, "add_half_w_ff2")
        p_ff1 = _add_half(g_ff1, x_ff1[0], core, "add_half_w_ff1")
        (dq_m, dkl, dkc, dvl, dvc), (l_ff2, l_ff1, x_out) = _mla_bwd(
            q, k, v, ymla, lse, dmix, tq=min(seq, 512), **dims,
            rider=_merge_riders(_scatter_rider([p_ff2, p_ff1]), _exchange_rider([g_out])))
        p_out = _add_half(g_out, x_out, core, "add_half_w_out")
        m_ff2 = _sum_chips(p_ff2, l_ff2, slot, "sum_chips_w_ff2")
        m_ff1 = _sum_chips(p_ff1, l_ff1, slot, "sum_chips_w_ff1")
        (drq, drk, drv, dkc_r, dvc_r, drg, st_ret), (l_out,) = _ret_bwd_pair(
            rq, rk, rv, pg, osum, dmix, lg, g_ret, chunk=chunk, **dims, rider=_scatter_rider([p_out]))
        late_out = [_sum_chips(p_out, l_out, slot, "sum_chips_w_out"), m_ff1, m_ff2]
    (dpb, dqf, dkvf, gx, st_pre), _ = _pre_bwd(
        x2, ctx2, modv, g_attn, pg, drq, drk, dkc_r, drv, dvc_r, drg, dq_m, dkl, dkc, dvl, dvc, dxm, w_in, g_q, g_kv,
        w_uq, w_ukv, cos_t, sin_t, seq=seq, tm=tm)
    g_early = [
        _matmul_tn(dpb, hb, bm=IN_PAD // 2, bn=d, bk=1536, chip_major=False, name="gw_in"),
        _matmul_tn(dqf, nq, bm=HEADS * MLA_HEAD, bn=Q_LORA, bk=2304, chip_major=False, name="gw_uq"),
        _matmul_tn(nkv, dkvf, bm=KV_LORA, bn=HEADS * 256, bk=2304, chip_major=True, name="gw_ukv"),
    ]
    return gx.reshape(nex, seq, d), g_early, late_out, st_post, st_ret, st_pre


_ANY = pl.BlockSpec(memory_space=pl.ANY)
_VMEM = pl.BlockSpec(memory_space=pltpu.VMEM)
_OFFSETS = tuple((dx, dy, dc) for dx in (0, 1) for dy in (0, 1) for dc in (0, 1))[1:]
_CHIP_OFFSETS = ((1, 0), (0, 1), (1, 1))


def _place():
    return lax.axis_index("x"), lax.axis_index("y"), lax.axis_index("c")


def _flip(v, d):
    return 1 - v if d else v


def _gather8_rider(a, in_vmem=True):
    def copies(a_ref, o_ref, send, recv):
        x, y, z = _place()
        me = 4 * x + 2 * y + z
        out = []
        for k, (dx, dy, dc) in enumerate(_OFFSETS):
            peer = (_flip(x, dx), _flip(y, dy), _flip(z, dc))
            landing = o_ref.at[4 * peer[0] + 2 * peer[1] + peer[2]]
            out.append((
                pltpu.make_async_remote_copy(src_ref=a_ref, dst_ref=o_ref.at[me], send_sem=send.at[k],
                                             recv_sem=recv.at[k], device_id=peer, device_id_type=MESH),
                pltpu.make_async_remote_copy(src_ref=a_ref, dst_ref=landing, send_sem=send.at[k],
                                             recv_sem=recv.at[k], device_id=peer, device_id_type=MESH)))
        return me, out

    def start(ins, outs, sems):
        me, cps = copies(ins[0], outs[0], sems[0], sems[1])
        pltpu.make_async_copy(ins[0], outs[0].at[me], sems[2]).start()
        for out_cp, _ in cps:
            out_cp.start()

    def finish(ins, outs, sems):
        me, cps = copies(ins[0], outs[0], sems[0], sems[1])
        for out_cp, in_cp in cps:
            in_cp.wait_recv()
            out_cp.wait_send()
        pltpu.make_async_copy(ins[0], outs[0].at[me], sems[2]).wait()

    spec = [_VMEM] if in_vmem else [_ANY]
    return _Rider([a], [jax.ShapeDtypeStruct((N_DEV,) + a.shape, a.dtype)],
                  [pltpu.SemaphoreType.DMA((7,)), pltpu.SemaphoreType.DMA((7,)), pltpu.SemaphoreType.DMA],
                  start, finish, in_specs=spec, out_specs=spec)


def _merge_riders(*riders):
    ins, outs, sems, in_specs, out_specs, aliases, cuts = [], [], [], [], [], {}, []
    for r in riders:
        cuts.append((len(ins), len(outs), len(sems)))
        aliases.update({len(ins) + i: len(outs) + j for i, j in r.aliases.items()})
        ins += r.ins
        outs += r.out_shapes
        sems += r.sems
        in_specs += r.in_specs
        out_specs += r.out_specs

    def part(r, cut, r_ins, r_outs, r_sems):
        return (r_ins[cut[0]:cut[0] + len(r.ins)], r_outs[cut[1]:cut[1] + len(r.out_shapes)],
                r_sems[cut[2]:cut[2] + len(r.sems)])

    def start(r_ins, r_outs, r_sems):
        for r, cut in zip(riders, cuts):
            r.start(*part(r, cut, r_ins, r_outs, r_sems))

    def finish(r_ins, r_outs, r_sems):
        for r, cut in zip(riders, cuts):
            r.finish(*part(r, cut, r_ins, r_outs, r_sems))

    def middle(r_ins, r_outs, r_sems):
        for r, cut in zip(riders, cuts):
            if r.middle is not None:
                r.middle(*part(r, cut, r_ins, r_outs, r_sems))

    return _Rider(ins, outs, sems, start, finish, aliases=aliases, in_specs=in_specs, out_specs=out_specs,
                  middle=middle if any(r.middle is not None for r in riders) else None)


def _allgather8(a, name):
    return _run_rider(_gather8_rider(a), name)[0]


BF16_TILE_ROWS = 16


def _half(o, slot, which):
    r2 = o.shape[1] // 2
    if r2 % BF16_TILE_ROWS == 0:
        return o.at[slot, pl.ds(which * r2, r2)]
    c2 = o.shape[2] // 2
    assert c2 % LANES == 0
    return o.at[slot, :, pl.ds(which * c2, c2)]


def _gather_send(o_refs, send, recv):
    x, y, z = _place()
    chip = 2 * x + y
    for a, o in enumerate(o_refs):
        r2 = o.shape[1] // 2
        mine = _half(o, chip, z)
        for k, (dx, dy) in enumerate(_CHIP_OFFSETS):
            pltpu.make_async_remote_copy(
                src_ref=mine, dst_ref=mine, send_sem=send.at[a, k], recv_sem=recv.at[a, k],
                device_id=(_flip(x, dx), _flip(y, dy), z), device_id_type=MESH).start()


def _gather_landed(o_refs, send, recv, then=None):
    x, y, z = _place()
    chip = 2 * x + y
    for a, o in enumerate(o_refs):
        for k, (dx, dy) in enumerate(_CHIP_OFFSETS):
            landed = _half(o, 2 * _flip(x, dx) + _flip(y, dy), z)
            pltpu.make_async_remote_copy(
                src_ref=landed, dst_ref=landed, send_sem=send.at[a, k], recv_sem=recv.at[a, k],
                device_id=(_flip(x, dx), _flip(y, dy), z), device_id_type=MESH).wait_recv()
            if then is not None:
                then(a, k, landed)
    for a, o in enumerate(o_refs):
        mine = _half(o, chip, z)
        for k, (dx, dy) in enumerate(_CHIP_OFFSETS):
            pltpu.make_async_remote_copy(
                src_ref=mine, dst_ref=mine, send_sem=send.at[a, k], recv_sem=recv.at[a, k],
                device_id=(_flip(x, dx), _flip(y, dy), z), device_id_type=MESH).wait_send()


def _pass_on(o_refs, fsend, frecv, a, k, landed):
    x, y, z = _place()
    pltpu.make_async_remote_copy(
        src_ref=landed, dst_ref=landed, send_sem=fsend.at[a, k], recv_sem=frecv.at[a, k],
        device_id=(x, y, 1 - z), device_id_type=MESH).start()


def _passed_on(o_refs, fsend, frecv):
    x, y, z = _place()
    for a, o in enumerate(o_refs):
        for k, (dx, dy) in enumerate(_CHIP_OFFSETS):
            other = 2 * _flip(x, dx) + _flip(y, dy)
            got = _half(o, other, 1 - z)
            gave = _half(o, other, z)
            pltpu.make_async_remote_copy(
                src_ref=got, dst_ref=got, send_sem=fsend.at[a, k], recv_sem=frecv.at[a, k],
                device_id=(x, y, 1 - z), device_id_type=MESH).wait_recv()
            pltpu.make_async_remote_copy(
                src_ref=gave, dst_ref=gave, send_sem=fsend.at[a, k], recv_sem=frecv.at[a, k],
                device_id=(x, y, 1 - z), device_id_type=MESH).wait_send()


def _gather_finish(o_refs, send, recv, fsend, frecv):
    _gather_landed(o_refs, send, recv, functools.partial(_pass_on, o_refs, fsend, frecv))
    _passed_on(o_refs, fsend, frecv)


class _Rider:
    def __init__(self, ins, out_shapes, sems, start, finish, aliases=None, in_specs=None, out_specs=None, middle=None):
        self.ins, self.out_shapes, self.sems = list(ins), list(out_shapes), list(sems)
        self.start, self.finish, self.aliases = start, finish, dict(aliases or {})
        self.middle = middle
        self.in_specs = list(in_specs) if in_specs else [_ANY] * len(self.ins)
        self.out_specs = list(out_specs) if out_specs else [_ANY] * len(self.out_shapes)


def _run_rider(rider, name):
    r_in, r_out = len(rider.ins), len(rider.out_shapes)

    def body(*refs):
        ins, outs, sems = refs[:r_in], refs[r_in:r_in + r_out], refs[r_in + r_out:]
        rider.start(ins, outs, sems)
        if rider.middle is not None:
            rider.middle(ins, outs, sems)
        rider.finish(ins, outs, sems)

    return pl.pallas_call(
        body, name=name, in_specs=rider.in_specs, out_specs=rider.out_specs, out_shape=rider.out_shapes,
        input_output_aliases=rider.aliases, scratch_shapes=rider.sems,
    )(*rider.ins)


def _hosted_call(body, args, *, name, grid, in_specs, out_specs, out_shape, scratch_shapes=(), sem, rider=None,
                 prefetch=()):
    scratch_shapes = list(scratch_shapes)
    n_pf, n_in, n_out, n_sc = len(prefetch), len(in_specs), len(out_specs), len(scratch_shapes)
    r_in, r_out = (len(rider.ins), len(rider.out_shapes)) if rider else (0, 0)
    last = tuple(g - 1 for g in grid)

    def hosted(*refs):
        p = 0
        parts = []
        for cnt in (n_pf, n_in, r_in, n_out, r_out, n_sc):
            parts.append(refs[p:p + cnt])
            p += cnt
        pf, ins, r_ins, outs, r_outs, scratch = parts
        sems = refs[p:]
        ids = [pl.program_id(a) for a in range(len(grid))]
        is_first = functools.reduce(jnp.logical_and, [i == 0 for i in ids])
        is_last = functools.reduce(jnp.logical_and, [i == e for i, e in zip(ids, last)])

        @pl.when(is_first)
        def _():
            rider.start(r_ins, r_outs, sems)

        if rider.middle is not None:
            linear = functools.reduce(lambda acc, ig: acc * ig[1] + ig[0], zip(ids, grid), 0)

            @pl.when(linear == math.prod(grid) * 3 // 4)
            def _():
                rider.middle(r_ins, r_outs, sems)

        body(*pf, *ins, *outs, *scratch)

        @pl.when(is_last)
        def _():
            rider.finish(r_ins, r_outs, sems)

    if rider is None:
        kern, all_in, all_out, shapes, scratch, aliases, extra = body, list(in_specs), list(out_specs), list(out_shape), \
            scratch_shapes, {}, []
    else:
        kern, all_in, all_out = hosted, list(in_specs) + rider.in_specs, list(out_specs) + rider.out_specs
        shapes, scratch, extra = list(out_shape) + rider.out_shapes, scratch_shapes + rider.sems, rider.ins
        aliases = {n_pf + n_in + i: n_out + j for i, j in rider.aliases.items()}
        sem = ("arbitrary",) * len(grid)
    if prefetch:
        spec = dict(grid_spec=pltpu.PrefetchScalarGridSpec(
            num_scalar_prefetch=n_pf, grid=grid, in_specs=all_in, out_specs=all_out, scratch_shapes=scratch))
    else:
        spec = dict(grid=grid, in_specs=all_in, out_specs=all_out, scratch_shapes=scratch)
    res = pl.pallas_call(kern, name=name, out_shape=shapes, input_output_aliases=aliases,
                         compiler_params=_params(sem, VMEM_LIMIT), **spec)(*prefetch, *args, *extra)
    return list(res[:n_out]), list(res[n_out:])


def _gather_rider(ws, staged=False):
    n = len(ws)
    shapes = [jax.ShapeDtypeStruct(w.shape, w.dtype) for w in ws]
    sems = [pltpu.SemaphoreType.DMA((n, 3))] * 4
    aliases = {a: a for a in range(n)}

    def start(ins, outs, s):
        _gather_send(outs, s[0], s[1])

    if not staged:
        return _Rider(ws, shapes, sems, start, lambda ins, outs, s: _gather_finish(outs, *s), aliases=aliases)
    return _Rider(
        ws, shapes, sems, start, lambda ins, outs, s: _passed_on(outs, s[2], s[3]), aliases=aliases,
        middle=lambda ins, outs, s: _gather_landed(outs, s[0], s[1], functools.partial(_pass_on, outs, s[2], s[3])))


def _gather_ici_rider(ws):
    n = len(ws)
    return _Rider(
        ws, [jax.ShapeDtypeStruct(w.shape, w.dtype) for w in ws], [pltpu.SemaphoreType.DMA((n, 3))] * 2,
        lambda ins, outs, sems: _gather_send(outs, sems[0], sems[1]),
        lambda ins, outs, sems: _gather_landed(outs, sems[0], sems[1]),
        aliases={a: a for a in range(n)})


def _gather_d2d_rider(ws):
    n = len(ws)

    def start(ins, outs, sems):
        x, y, z = _place()
        for a, o in enumerate(outs):
            for k, (dx, dy) in enumerate(_CHIP_OFFSETS):
                _pass_on(outs, sems[0], sems[1], a, k, _half(o, 2 * _flip(x, dx) + _flip(y, dy), z))

    return _Rider(
        ws, [jax.ShapeDtypeStruct(w.shape, w.dtype) for w in ws], [pltpu.SemaphoreType.DMA((n, 3))] * 2,
        start, lambda ins, outs, sems: _passed_on(outs, sems[0], sems[1]), aliases={a: a for a in range(n)})


def _copies_rider(ins, out_shapes, sem_shape, make):
    def start(r_ins, r_outs, sems):
        for cp in make(r_ins, r_outs, sems[0], sems[1]):
            cp.start()

    def finish(r_ins, r_outs, sems):
        for cp in make(r_ins, r_outs, sems[0], sems[1]):
            cp.wait()

    return _Rider(ins, out_shapes, [pltpu.SemaphoreType.DMA(sem_shape)] * 2, start, finish)


def _exchange_rider(gs):
    def make(g_refs, r_refs, send, recv):
        x, y, z = _place()
        return [pltpu.make_async_remote_copy(
            src_ref=g.at[:, pl.ds((1 - z) * (g.shape[1] // 2), g.shape[1] // 2)], dst_ref=r, send_sem=send.at[a],
            recv_sem=recv.at[a], device_id=(x, y, 1 - z), device_id_type=MESH)
            for a, (g, r) in enumerate(zip(g_refs, r_refs))]

    shapes = [jax.ShapeDtypeStruct((g.shape[0], g.shape[1] // 2, g.shape[2]), g.dtype) for g in gs]
    return _copies_rider(gs, shapes, (len(gs),), make)


def _add_half(g, recv, core, name):
    s, r, c = g.shape
    r2 = r // 2
    rb = r2
    for cand in (512, 256, 128, 64):
        if r2 % cand == 0:
            rb = cand
            break
    g4 = g.reshape(s, 2, r2, c)

    def body(core_ref, g_ref, r_ref, o_ref):
        o_ref[...] = (g_ref[...].astype(F32) + r_ref[...].astype(F32)).astype(BF16)

    return pl.pallas_call(
        body, name=name,
        grid_spec=pltpu.PrefetchScalarGridSpec(
            num_scalar_prefetch=1, grid=(s, r2 // rb),
            in_specs=[pl.BlockSpec((None, None, rb, c), lambda i, j, cr: (i, cr[0], j, 0)),
                      pl.BlockSpec((None, rb, c), lambda i, j, cr: (i, j, 0))],
            out_specs=pl.BlockSpec((None, rb, c), lambda i, j, cr: (i, j, 0))),
        out_shape=jax.ShapeDtypeStruct((s, r2, c), BF16),
        compiler_params=_params(("parallel", "parallel")),
    )(core, g4, recv)


def _scatter_rider(ps):
    def make(p_refs, o_refs, send, recv):
        x, y, z = _place()
        copies = []
        for a, (p, o) in enumerate(zip(p_refs, o_refs)):
            for k, (dx, dy) in enumerate(_CHIP_OFFSETS):
                other = 2 * _flip(x, dx) + _flip(y, dy)
                copies.append(pltpu.make_async_remote_copy(
                    src_ref=p.at[other], dst_ref=o.at[k], send_sem=send.at[a, k], recv_sem=recv.at[a, k],
                    device_id=(_flip(x, dx), _flip(y, dy), z), device_id_type=MESH))
        return copies

    shapes = [jax.ShapeDtypeStruct((3,) + p.shape[1:], p.dtype) for p in ps]
    return _copies_rider(ps, shapes, (len(ps), 3), make)


def _sum_chips(p, landed, chip, name):
    _, r2, c = p.shape
    rb = r2
    for cand in (256, 128, 64):
        if r2 % cand == 0:
            rb = cand
            break

    def body(s_ref, p_ref, l_ref, o_ref):
        acc = p_ref[...].astype(F32)
        for k in range(3):
            acc = acc + l_ref[k].astype(F32)
        o_ref[...] = acc

    return pl.pallas_call(
        body, name=name,
        grid_spec=pltpu.PrefetchScalarGridSpec(
            num_scalar_prefetch=1, grid=(r2 // rb,),
            in_specs=[pl.BlockSpec((None, rb, c), lambda i, s: (s[0], i, 0)),
                      pl.BlockSpec((3, rb, c), lambda i, s: (0, i, 0))],
            out_specs=pl.BlockSpec((rb, c), lambda i, s: (i, 0))),
        out_shape=jax.ShapeDtypeStruct((r2, c), F32),
        compiler_params=_params(("parallel",)),
    )(chip, p, landed)


def _swap_rider(hs):
    def make(h_refs, o_refs, send, recv):
        x, y, z = _place()
        return [pltpu.make_async_remote_copy(
            src_ref=h, dst_ref=o, send_sem=send.at[a], recv_sem=recv.at[a], device_id=(x, y, 1 - z),
            device_id_type=MESH) for a, (h, o) in enumerate(zip(h_refs, o_refs))]

    return _copies_rider(hs, [jax.ShapeDtypeStruct(h.shape, h.dtype) for h in hs], (len(hs),), make)


def _reduce_scatter_vmem(gs, rows, rider, name):
    n = len(gs)
    r_in, r_out = len(rider.ins), len(rider.out_shapes)
    halves = [(r // 2, g.shape[-1]) for g, (r, _) in zip(gs, rows)]
    piece_cols = 2 * LANES
    pieces = [(a, slice(c0, min(c0 + piece_cols, h[1]))) for a, h in enumerate(halves) for c0 in range(0, h[1], piece_cols)]
    n_p = len(pieces)

    def body(*refs):
        p = 0
        parts = []
        for cnt in (n, r_in, n, n, r_out, n, n, n, 6):
            parts.append(refs[p:p + cnt])
            p += cnt
        g_refs, r_ins, mine, theirs, r_outs, recv, part, land, sems = parts
        r_sems = refs[p:]
        xs, xr, ss, sr, ws, wr = sems
        x, y, z = _place()
        chip = 2 * x + y
        sib = (x, y, 1 - z)
        rider.start(r_ins, r_outs, r_sems)

        def half_of(a, s, which):
            r2 = halves[a][0]
            if len(g_refs[a].shape) == 3:
                return g_refs[a].at[s, pl.ds(pl.multiple_of(which * r2, 8), r2)]
            return g_refs[a].at[pl.ds(pl.multiple_of(s * rows[a][1] + which * r2, 8), r2)]

        def exchange(i):
            a, cols = pieces[i]
            return [pltpu.make_async_remote_copy(
                src_ref=half_of(a, s, 1 - z).at[:, cols], dst_ref=recv[a].at[s, :, cols], send_sem=xs.at[i, s],
                recv_sem=xr.at[i, s], device_id=sib, device_id_type=MESH) for s in range(N_CHIPS)]

        def scatter(i):
            a, cols = pieces[i]
            return [pltpu.make_async_remote_copy(
                src_ref=part[a].at[2 * _flip(x, dx) + _flip(y, dy), :, cols], dst_ref=land[a].at[k, :, cols],
                send_sem=ss.at[i, k], recv_sem=sr.at[i, k], device_id=(_flip(x, dx), _flip(y, dy), z),
                device_id_type=MESH) for k, (dx, dy) in enumerate(_CHIP_OFFSETS)]

        def swap(i):
            a, cols = pieces[i]
            return pltpu.make_async_remote_copy(
                src_ref=mine[a].at[:, cols], dst_ref=theirs[a].at[:, cols], send_sem=ws.at[i], recv_sem=wr.at[i],
                device_id=sib, device_id_type=MESH)

        for i in range(len(pieces)):
            for cp in exchange(i):
                cp.start()
        for i, (a, cols) in enumerate(pieces):
            for cp in exchange(i):
                cp.wait()
            for s in range(N_CHIPS):
                part[a][s, :, cols] = (half_of(a, s, z)[:, cols] + recv[a][s, :, cols]).astype(BF16)
            for cp in scatter(i):
                cp.start()
        for i, (a, cols) in enumerate(pieces):
            for cp in scatter(i):
                cp.wait()
            acc = part[a][chip, :, cols].astype(F32)
            for k in range(3):
                acc = acc + land[a][k, :, cols].astype(F32)
            mine[a][:, cols] = acc
            swap(i).start()
        for i in range(len(pieces)):
            swap(i).wait()
        rider.finish(r_ins, r_outs, r_sems)

    half_shapes = [jax.ShapeDtypeStruct(h, F32) for h in halves]
    res = pl.pallas_call(
        body, name=name, in_specs=[_VMEM] * n + rider.in_specs, out_specs=[_VMEM] * (2 * n) + rider.out_specs,
        out_shape=half_shapes + half_shapes + rider.out_shapes,
        scratch_shapes=[pltpu.VMEM((N_CHIPS,) + h, F32) for h in halves] + [pltpu.VMEM((N_CHIPS,) + h, BF16) for h in halves]
        + [pltpu.VMEM((3,) + h, BF16) for h in halves]
        + [pltpu.SemaphoreType.DMA((n_p, N_CHIPS))] * 2 + [pltpu.SemaphoreType.DMA((n_p, 3))] * 2
        + [pltpu.SemaphoreType.DMA((n_p,))] * 2 + rider.sems,
        input_output_aliases={n + i: 2 * n + j for i, j in rider.aliases.items()},
        compiler_params=_params(None, VMEM_LIMIT),
    )(*gs, *rider.ins)
    return list(res[:n]), list(res[n:2 * n]), list(res[2 * n:])


SMALL_ROWS = 32
PACK_ROWS = 16


def _pack_small(st_post, st_ret, st_pre):
    d = st_post.shape[2]

    def body(po_ref, re_ref, pr_ref, o_ref):
        o_ref[...] = jnp.zeros(o_ref.shape, F32)
        o_ref[0:1, :] = pr_ref[0, 2:3, :] + pr_ref[1, 2:3, :] + pr_ref[2, 2:3, :]
        o_ref[1:2, :] = po_ref[0, 4:5, :] + po_ref[1, 4:5, :]
        o_ref[2:3, :] = po_ref[0, 5:6, :] + po_ref[1, 5:6, :]
        o_ref[3:4, 0:512] = re_ref[0, 0:1, :] + re_ref[1, 0:1, :]
        o_ref[4:5, :] = pr_ref[0, 3:4, :] + pr_ref[1, 3:4, :] + pr_ref[2, 3:4, :]
        o_ref[5:6, :] = pr_ref[0, 4:5, :] + pr_ref[1, 4:5, :] + pr_ref[2, 4:5, :]
        lane = lax.broadcasted_iota(jnp.int32, (1, LANES), 1)
        for row, src in ((6, 1), (10, 2)):
            acc = jnp.zeros((1, LANES), F32)
            for hd in range(HEADS):
                grp = re_ref[0, src:src + 1, hd * LANES:(hd + 1) * LANES] + re_ref[1, src:src + 1, hd * LANES:(hd + 1) * LANES]
                acc = acc + jnp.where(lane == hd, grp, 0.0)
            o_ref[row:row + 1, 0:LANES] = acc
        o_ref[7:8, :] = po_ref[0, 6:7, :] + po_ref[1, 6:7, :]
        o_ref[8:9, :] = pr_ref[2, 0:1, :]
        o_ref[9:10, :] = pr_ref[2, 1:2, :]
        for e in range(2):
            b = 12 + 6 * e
            o_ref[b:b + 1, :] = pr_ref[e, 0:1, :]
            o_ref[b + 1:b + 2, :] = pr_ref[e, 1:2, :]
            o_ref[b + 2:b + 3, :] = po_ref[e, 3:4, :]
            o_ref[b + 3:b + 4, :] = po_ref[e, 0:1, :]
            o_ref[b + 4:b + 5, :] = po_ref[e, 1:2, :]
            o_ref[b + 5:b + 6, :] = po_ref[e, 2:3, :]

    return pl.pallas_call(body, name="pack_small", out_shape=jax.ShapeDtypeStruct((SMALL_ROWS, d), F32))(st_post, st_ret, st_pre)


def _small_reduce(gathered):
    d = gathered.shape[2]

    def body(g_ref, o_ref):
        tot = g_ref[0, 0:PACK_ROWS, :]
        for dev in range(1, N_DEV):
            tot = tot + g_ref[dev, 0:PACK_ROWS, :]
        o_ref[0:PACK_ROWS, :] = tot
        for j in range(6):
            acc = g_ref[0, 12 + j:13 + j, :] + g_ref[0, 18 + j:19 + j, :]
            for dev in range(1, N_DEV):
                acc = acc + g_ref[dev, 12 + j:13 + j, :] + g_ref[dev, 18 + j:19 + j, :]
            if j < 2:
                acc = acc + o_ref[8 + j:9 + j, :]
            o_ref[PACK_ROWS + j:PACK_ROWS + j + 1, :] = acc
        o_ref[PACK_ROWS + 6:PACK_ROWS + 8, :] = jnp.zeros((2, d), F32)

    return pl.pallas_call(body, name="small_reduce", out_shape=jax.ShapeDtypeStruct((PACK_ROWS + 8, d), F32))(gathered)


_SMALL = (("g_attn", 0, 1024), ("g_ffn", 1, 1024), ("g_final", 2, 1024), ("g_ret", 3, 512), ("g_q_lora", 4, 384),
          ("g_kv_lora", 5, 256), ("ret_decay_fwd", 6, HEADS), ("ret_decay_bwd", 10, HEADS))
_SMALL_NAMES = tuple(s[0] for s in _SMALL) + ("c_ctx", "b_ada")


def _small_final(tot, dcc, sg8, ws, ms, vs):
    d = tot.shape[1]
    n = len(_SMALL_NAMES)

    def body(*refs):
        t_ref, dcc_ref, sg_ref = refs[0:3]
        w_refs, m_refs, v_refs = refs[3:3 + n], refs[3 + n:3 + 2 * n], refs[3 + 2 * n:3 + 3 * n]
        outs = refs[3 + 3 * n:]
        g_refs, d_refs, mo_refs, vo_refs = outs[0:n], outs[n:2 * n], outs[2 * n:3 * n], outs[3 * n:4 * n]
        l_ref = outs[4 * n]

        def update(i, g, sl=None):
            pick = (lambda r: r[...]) if sl is None else (lambda r: r[:, sl])
            dl, mn, vn = _adam_math(pick(w_refs[i]), g, pick(m_refs[i]), pick(v_refs[i]))
            if sl is None:
                g_refs[i][...], d_refs[i][...], mo_refs[i][...], vo_refs[i][...] = g, dl, mn, vn
            else:
                g_refs[i][:, sl], d_refs[i][:, sl], mo_refs[i][:, sl], vo_refs[i][:, sl] = g, dl, mn, vn

        for i, (name, row, width) in enumerate(_SMALL):
            g = t_ref[row:row + 1, 0:width]
            if name == "ret_decay_fwd":
                g = g * sg_ref[0:1, 0:width]
            elif name == "ret_decay_bwd":
                g = g * sg_ref[1:2, 0:width]
            update(i, g)
        i_cc, i_b = n - 2, n - 1
        cc = w_refs[i_cc][...]
        s = 1.0 / (1.0 + jnp.exp(-cc))
        dsilu = dcc_ref[0, 0:1, :] + dcc_ref[2, 0:1, :] + dcc_ref[4, 0:1, :] + dcc_ref[6, 0:1, :]
        update(i_cc, dsilu * (s * (1.0 + cc * (1.0 - s))))
        for j in range(6):
            update(i_b, t_ref[PACK_ROWS + j:PACK_ROWS + j + 1, :], pl.ds(j * d, d))
        l_ref[...] = jnp.broadcast_to((0.5 / d) * jnp.sum(t_ref[7:8, :], keepdims=True), l_ref.shape)

    shapes = [jax.ShapeDtypeStruct(a.shape, F32) for a in ws]
    outs = pl.pallas_call(
        body, name="small_final", out_shape=shapes * 4 + [jax.ShapeDtypeStruct((8, LANES), F32)],
    )(tot, dcc, sg8, *ws, *ms, *vs)
    return outs[0:n], outs[n:2 * n], outs[2 * n:3 * n], outs[3 * n:4 * n], outs[4 * n]


_WEIGHTS = ("c_ctx", "w_ada", "b_ada", "g_attn", "g_ffn", "w_in", "ret_decay_fwd", "ret_decay_bwd", "g_ret", "g_q_lora",
            "w_uq", "g_kv_lora", "w_ukv", "w_out", "w_ff1", "w_ff2", "g_final")
_BIG = ("w_in", "w_uq", "w_ukv", "w_out", "w_ff1", "w_ff2")
_TRANSPOSED = ("w_in", "w_uq")


def kernel(x, c, ctx, c_ctx, w_ada, b_ada, g_attn, g_ffn, w_in, ret_decay_fwd, ret_decay_bwd, g_ret, g_q_lora, w_uq, g_kv_lora, w_ukv, w_out, w_ff1, w_ff2, g_final, loss_target, m_c_ctx, m_w_ada, m_b_ada, m_g_attn, m_g_ffn, m_w_in, m_ret_decay_fwd, m_ret_decay_bwd, m_g_ret, m_g_q_lora, m_w_uq, m_g_kv_lora, m_w_ukv, m_w_out, m_w_ff1, m_w_ff2, m_g_final, v_c_ctx, v_w_ada, v_b_ada, v_g_attn, v_g_ffn, v_w_in, v_ret_decay_fwd, v_ret_decay_bwd, v_g_ret, v_g_q_lora, v_w_uq, v_g_kv_lora, v_w_ukv, v_w_out, v_w_ff1, v_w_ff2, v_g_final):
    w = dict(c_ctx=c_ctx, w_ada=w_ada, b_ada=b_ada, g_attn=g_attn, g_ffn=g_ffn, w_in=w_in, ret_decay_fwd=ret_decay_fwd,
             ret_decay_bwd=ret_decay_bwd, g_ret=g_ret, g_q_lora=g_q_lora, w_uq=w_uq, g_kv_lora=g_kv_lora, w_ukv=w_ukv,
             w_out=w_out, w_ff1=w_ff1, w_ff2=w_ff2, g_final=g_final)
    m = dict(c_ctx=m_c_ctx, w_ada=m_w_ada, b_ada=m_b_ada, g_attn=m_g_attn, g_ffn=m_g_ffn, w_in=m_w_in,
             ret_decay_fwd=m_ret_decay_fwd, ret_decay_bwd=m_ret_decay_bwd, g_ret=m_g_ret, g_q_lora=m_g_q_lora, w_uq=m_w_uq,
             g_kv_lora=m_g_kv_lora, w_ukv=m_w_ukv, w_out=m_w_out, w_ff1=m_w_ff1, w_ff2=m_w_ff2, g_final=m_g_final)
    v = dict(c_ctx=v_c_ctx, w_ada=v_w_ada, b_ada=v_b_ada, g_attn=v_g_attn, g_ffn=v_g_ffn, w_in=v_w_in,
             ret_decay_fwd=v_ret_decay_fwd, ret_decay_bwd=v_ret_decay_bwd, g_ret=v_g_ret, g_q_lora=v_g_q_lora, w_uq=v_w_uq,
             g_kv_lora=v_g_kv_lora, w_ukv=v_w_ukv, w_out=v_w_out, w_ff1=v_w_ff1, w_ff2=v_w_ff2, g_final=v_g_final)
    xi, yi, ci = lax.axis_index("x"), lax.axis_index("y"), lax.axis_index("c")
    chip = 2 * xi + yi
    dev = 2 * chip + ci
    nex, seq, d = x.shape
    n_ada = w_ada.shape[2]

    dec = jnp.zeros((8, LANES), F32).at[0, :HEADS].set(ret_decay_fwd[0]).at[1, :HEADS].set(ret_decay_bwd[0])
    lg8, sg8 = _decay_prep(dec)
    lg = lg8[:2, :HEADS]

    def shard_of(t, k):
        return t[k][0].T if k in _TRANSPOSED else t[k][0]

    shard = {k: shard_of(w, k) for k in _BIG}
    head_rows = MLA_NOPE + MLA_ROPE
    shard["w_uq"] = jnp.pad(shard["w_uq"], ((0, MLA_HEAD - head_rows), (0, 0)))
    slot = chip.reshape(1).astype(jnp.int32)
    core = ci.reshape(1).astype(jnp.int32)
    slots = {k: _cast_into_slot(shard[k], slot, "cast_" + k) for k in _EARLY}
    half_ff = shard["w_ff2"].shape[0] // 2
    late_pieces = [(shard["w_out"], 0, shard["w_out"].shape[0]), (shard["w_ff1"], 0, shard["w_ff1"].shape[0]),
                   (shard["w_ff2"], 0, half_ff), (shard["w_ff2"], half_ff, half_ff)]
    late_slots, (w_in_f, w_uq_k, w_ukv_k, c8) = _cast_into_slots(
        late_pieces, slot, "cast_late",
        rider=_merge_riders(_gather_rider([slots[k] for k in _EARLY]),
                            _gather8_rider(jnp.pad(c, ((0, 8 - nex), (0, 0))), in_vmem=False)))

    a_in = jnp.concatenate([c8[:, :nex].reshape(N_DEV * nex, d), c_ctx.reshape(1, d), jnp.zeros((7, d), F32)], axis=0)
    b_sh = lax.dynamic_slice(b_ada, (0, chip * n_ada), (1, n_ada))
    mod_sh = _mod_fwd(a_in, w_ada[0], b_sh)
    mod8 = _allgather8(mod_sh, "ag_mod")
    w_in_k = jnp.pad(w_in_f.reshape(IN_COLS, d), ((0, IN_PAD - IN_COLS), (0, 0)))
    mod_all = mod8[0::2].transpose(1, 0, 2).reshape(a_in.shape[0], N_CHIPS * n_ada)
    mod_me = lax.dynamic_slice(mod_all, (nex * dev, 0), (nex, N_CHIPS * n_ada)).reshape(nex, 6, d)
    mod_c = mod_all[N_DEV * nex].reshape(1, 6, d)
    modv = jnp.pad(jnp.concatenate([mod_me, mod_c], axis=0), ((0, 0), (0, 2), (0, 0)))

    gx, g_early, late, st_post, st_ret, st_pre = _local_step(
        x, ctx, loss_target, modv, lg, g_attn, g_ffn, g_final.reshape(1, d), g_ret, g_q_lora, g_kv_lora,
        w_in_k, w_uq_k, w_ukv_k, late_slots, (core, slot))

    mine, theirs, (*late_theirs, gathered) = _reduce_scatter_vmem(
        g_early, [(IN_COLS // N_CHIPS, IN_COLS // N_CHIPS), (head_rows, MLA_HEAD), (KV_LORA, KV_LORA)],
        _merge_riders(_swap_rider(late), _gather8_rider(_pack_small(st_post, st_ret, st_pre))), "rs_early")
    tot = _small_reduce(gathered)
    dm = jnp.concatenate([
        gathered[:, 12:24].reshape(N_DEV * nex, 6 * d),
        jnp.concatenate([tot[8:10].reshape(1, 2 * d), jnp.zeros((1, 4 * d), F32)], axis=1),
        jnp.zeros((7, 6 * d), F32)], axis=0)
    dm_sh = lax.dynamic_slice(dm, (0, chip * n_ada), (dm.shape[0], n_ada))
    g_ada, da = _mod_bwd(a_in, dm_sh, w_ada[0])
    dcc = _allgather8(da[N_DEV * nex:], "ag_dcc")
    halves = dict(zip(_EARLY, zip(mine, theirs)))
    halves.update(zip(_LATE, zip(late, late_theirs)))
    grad, delta, new_m, new_v = {}, {}, {}, {}
    for k in _BIG:
        a, b = halves[k]
        res = _adamw_halves(shard_of(w, k), a, b, shard_of(m, k), shard_of(v, k), core, "adamw_" + k)
        grad[k], delta[k], new_m[k], new_v[k] = [(o.T if k in _TRANSPOSED else o).reshape(w[k].shape) for o in res]

    shp = w_ada.shape
    outs, _ = _adamw(w_ada[0], g_ada, m["w_ada"][0], v["w_ada"][0], "adamw_w_ada")
    grad["w_ada"] = g_ada.reshape(shp)
    delta["w_ada"], new_m["w_ada"], new_v["w_ada"] = [o.reshape(shp) for o in outs]
    rows = [{k: t[k].reshape(1, -1) for k in _SMALL_NAMES} for t in (w, m, v)]
    small = _small_final(tot, dcc, sg8, *[[t[k] for k in _SMALL_NAMES] for t in rows])
    for res, outs in zip((grad, delta, new_m, new_v), small[:4]):
        for k, o in zip(_SMALL_NAMES, outs):
            res[k] = o.reshape(w[k].shape)
    return (small[4][0, 0], gx, *[grad[k] for k in _WEIGHTS], *[delta[k] for k in _WEIGHTS],
            *[new_m[k] for k in _WEIGHTS], *[new_v[k] for k in _WEIGHTS])
```

```python
import functools
import math

import jax
import jax.numpy as jnp
from jax import lax
from jax.experimental import pallas as pl
from jax.experimental.pallas import tpu as pltpu

F32 = jnp.float32
BF16 = jnp.bfloat16
MESH = pl.DeviceIdType.MESH

EPS = 1e-6
D_MODEL = 1024
D_FF = 4096
HEADS = 4
RET_DK = 64
RET_DV = 128
MLA_NOPE = 128
MLA_ROPE = 64
MLA_HEAD = 256
Q_LORA = 384
KV_LORA = 256
GRID_W = 64
ROPE_BASE = 10000.0
IN_COLS = 2240
IN_PAD = 2304
PG_COLS = 1152
N_CHIPS = 4
N_DEV = 8
LANES = 128
ADAM_LR = 0.001
ADAM_B1 = 0.9
ADAM_B2 = 0.999
ADAM_EPS = 1e-08
ADAM_WD = 0.01
ADAM_STEP = 10
VMEM_LIMIT = 56 * 1024 * 1024


def _dot(a, b):
    return jnp.dot(a, b, preferred_element_type=F32)


def _dot_nt(a, b):
    return lax.dot_general(a, b, (((1,), (1,)), ((), ())), preferred_element_type=F32)


def _dot_tn(a, b):
    return lax.dot_general(a, b, (((0,), (0,)), ((), ())), preferred_element_type=F32)


def _params(sem=None, vmem=None):
    return pltpu.CompilerParams(dimension_semantics=sem, vmem_limit_bytes=vmem)


def _full(shape):
    n = len(shape)
    return pl.BlockSpec(shape, lambda *_: (0,) * n)


def _once(shape):
    n = len(shape)
    return pl.BlockSpec(shape, lambda *_: (0,) * n, pipeline_mode=pl.Buffered(1))


def _rope(x, cos, sin):
    w = x.shape[-1]
    lo = (lax.broadcasted_iota(jnp.int32, (1, w), 1) % 64) < 32
    swapped = jnp.where(lo, pltpu.roll(x, w - 32, 1), pltpu.roll(x, 32, 1))
    return x * cos + swapped * sin


def _rope_t(g, cos, sin):
    w = g.shape[-1]
    lo = (lax.broadcasted_iota(jnp.int32, (1, w), 1) % 64) < 32
    t = g * sin
    swapped = jnp.where(lo, pltpu.roll(t, w - 32, 1), pltpu.roll(t, 32, 1))
    return g * cos + swapped


def _rope_tables(seq, tm):
    rows = seq // GRID_W
    row = jnp.repeat(jnp.arange(rows, dtype=F32), GRID_W)
    col = jnp.tile(jnp.arange(GRID_W, dtype=F32), rows)
    n_freq = RET_DK // 4
    freq = ROPE_BASE ** (-jnp.arange(n_freq, dtype=F32) / n_freq)
    ang = jnp.concatenate([row[:, None] * freq, col[:, None] * freq], axis=-1)
    cos, sin = jnp.cos(ang), jnp.sin(ang)
    cos_t = jnp.tile(jnp.concatenate([cos, cos], -1), (1, HEADS))
    sin_t = jnp.tile(jnp.concatenate([-sin, sin], -1), (1, HEADS))
    cos_t = jnp.concatenate([cos_t, jnp.ones((tm, 4 * RET_DK), F32)], 0)
    sin_t = jnp.concatenate([sin_t, jnp.zeros((tm, 4 * RET_DK), F32)], 0)
    return cos_t, sin_t


def _adam_math(w, g, m, v):
    mn = ADAM_B1 * m + (1.0 - ADAM_B1) * g
    vn = ADAM_B2 * v + (1.0 - ADAM_B2) * (g * g)
    m_hat = mn / (1.0 - ADAM_B1 ** ADAM_STEP)
    v_hat = vn / (1.0 - ADAM_B2 ** ADAM_STEP)
    return -ADAM_LR * (m_hat / (jnp.sqrt(v_hat) + ADAM_EPS) + ADAM_WD * w), mn, vn


def _cast_into_slots(pieces, slot, name, rider=None):
    c = pieces[0][0].shape[1]
    rb = max(b for b in range(16, 1025, 16) if b * c * 4 <= (5 << 19)
             and all(cnt % b == 0 and st % b == 0 for _, st, cnt in pieces))
    nbs = [cnt // rb for _, _, cnt in pieces]
    starts = [sum(nbs[:s]) for s in range(len(pieces))]

    def body(s_ref, *refs):
        i = pl.program_id(0)
        for s in range(len(pieces)):
            @pl.when(jnp.logical_and(i >= starts[s], i < starts[s] + nbs[s]))
            def _():
                refs[len(pieces) + s][...] = refs[s][...].astype(BF16)

    in_specs, out_specs = [], []
    for (_, first_row, _), nb, st in zip(pieces, nbs, starts):
        in_specs.append(pl.BlockSpec((rb, c), lambda i, s, nb=nb, st=st, f=first_row // rb: (f + jnp.clip(i - st, 0, nb - 1), 0)))
        out_specs.append(pl.BlockSpec((None, rb, c), lambda i, s, nb=nb, st=st: (s[0], jnp.clip(i - st, 0, nb - 1), 0)))
    return _hosted_call(
        body, [w for w, _, _ in pieces], name=name, grid=(sum(nbs),), prefetch=(slot,), in_specs=in_specs,
        out_specs=out_specs, out_shape=[jax.ShapeDtypeStruct((N_CHIPS, cnt, c), BF16) for _, _, cnt in pieces],
        sem=("arbitrary",), rider=rider)


def _cast_into_slot(w, slot, name):
    return _cast_into_slots([(w, 0, w.shape[0])], slot, name)[0][0]


def _adamw_halves(w, mine, theirs, m, v, core, name):
    r, c = w.shape
    r2 = r // 2
    rb = max(b for b in range(8, r2 + 1, 8) if r2 % b == 0 and b * c * 4 <= (1 << 21))
    nbh = r2 // rb

    def body(z_ref, w_ref, a_ref, b_ref, m_ref, v_ref, g_ref, d_ref, mo_ref, vo_ref):
        here = (pl.program_id(0) // nbh) == z_ref[0]
        gg = jnp.where(here, a_ref[...], b_ref[...])
        g_ref[...] = gg
        d_ref[...], mo_ref[...], vo_ref[...] = _adam_math(w_ref[...], gg, m_ref[...], v_ref[...])

    spec = pl.BlockSpec((rb, c), lambda i, z: (i, 0))
    a_spec = pl.BlockSpec((rb, c), lambda i, z: (jnp.clip(i - z[0] * nbh, 0, nbh - 1), 0))
    b_spec = pl.BlockSpec((rb, c), lambda i, z: (jnp.clip(i - (1 - z[0]) * nbh, 0, nbh - 1), 0))
    shp = jax.ShapeDtypeStruct((r, c), F32)
    return pl.pallas_call(
        body, name=name,
        grid_spec=pltpu.PrefetchScalarGridSpec(
            num_scalar_prefetch=1, grid=(r // rb,), in_specs=[spec, a_spec, b_spec, spec, spec], out_specs=[spec] * 4),
        out_shape=[shp] * 4,
        compiler_params=_params(("parallel",)),
    )(core, w, mine, theirs, m, v)


def _adamw(w, g, m, v, name, rider=None):
    r, c = w.shape
    rb = r
    for cand in (512, 256, 128, 64, 32, 16, 8):
        if r % cand == 0 and cand * c * 4 <= (1 << 21):
            rb = cand
            break
    if r * c * 4 <= (1 << 21):
        rb = r

    def body(w_ref, g_ref, m_ref, v_ref, d_ref, mo_ref, vo_ref):
        d_ref[...], mo_ref[...], vo_ref[...] = _adam_math(w_ref[...], g_ref[...], m_ref[...], v_ref[...])

    spec = pl.BlockSpec((rb, c), lambda i: (i, 0))
    shp = jax.ShapeDtypeStruct((r, c), F32)
    return _hosted_call(
        body, (w, g, m, v), name=name, grid=(r // rb,), in_specs=[spec] * 4, out_specs=[spec] * 3, out_shape=[shp] * 3,
        sem=("parallel",), rider=rider)


def _decay_prep(dec):
    def body(d_ref, lg_ref, sg_ref):
        d = d_ref[...]
        lg_ref[...] = jnp.minimum(d, 0.0) - jnp.log(1.0 + jnp.exp(-jnp.abs(d)))
        sg_ref[...] = 1.0 / (1.0 + jnp.exp(d))

    shp = jax.ShapeDtypeStruct(dec.shape, F32)
    return pl.pallas_call(body, name="decay_prep", out_shape=[shp, shp])(dec)


def _mod_fwd(a_in, w_ada, b_sh, slot):
    rows, d = a_in.shape
    n = w_ada.shape[1]
    bn = 512

    def body(s_ref, a_ref, w_ref, b_ref, o_ref):
        a = a_ref[...]
        s = (a / (1.0 + jnp.exp(-a))).astype(BF16)
        o_ref[...] = _dot(s, w_ref[...].astype(BF16)) + b_ref[...]

    (out,), _ = _hosted_call(
        body, (a_in, w_ada, b_sh), name="mod_fwd", grid=(n // bn,), prefetch=(slot,),
        in_specs=[pl.BlockSpec((rows, d), lambda j, s: (0, 0)), pl.BlockSpec((d, bn), lambda j, s: (0, j)),
                  pl.BlockSpec((1, bn), lambda j, s: (0, j))],
        out_specs=[pl.BlockSpec((None, rows, bn), lambda j, s: (s[0], 0, j))],
        out_shape=[jax.ShapeDtypeStruct((N_CHIPS, rows, n), F32)], sem=("parallel",))
    return out


def _mod_bwd(a_in, dm, w_ada):
    rows, d = a_in.shape
    n = w_ada.shape[1]
    bn = 512
    nb = n // bn

    def body(a_ref, dm_ref, w_ref, gw_ref, da_ref):
        j = pl.program_id(0)
        a = a_ref[...]
        s = (a / (1.0 + jnp.exp(-a))).astype(BF16)
        dmb = dm_ref[...].astype(BF16)
        gw_ref[...] = _dot_tn(s, dmb)
        part = _dot_nt(dmb, w_ref[...].astype(BF16))

        @pl.when(j == 0)
        def _():
            da_ref[...] = part

        @pl.when(j > 0)
        def _():
            da_ref[...] += part

    return pl.pallas_call(
        body, name="mod_bwd", grid=(nb,),
        in_specs=[_full((rows, d)), pl.BlockSpec((rows, bn), lambda j: (0, j)), pl.BlockSpec((d, bn), lambda j: (0, j))],
        out_specs=[pl.BlockSpec((d, bn), lambda j: (0, j)), _full((rows, d))],
        out_shape=[jax.ShapeDtypeStruct((d, n), F32), jax.ShapeDtypeStruct((rows, d), F32)],
        compiler_params=_params(("arbitrary",)),
    )(a_in, dm, w_ada)


def _pre_fwd(x2, ctx2, modv, g_attn, w_in, g_q, g_kv, w_uq, w_ukv, cos_t, sin_t, *, seq, tm, rider=None):
    t_lat, d = x2.shape
    t_ctx = ctx2.shape[0]
    nl, nc = t_lat // tm, t_ctx // tm
    n_all = t_lat + t_ctx
    tpe = seq // tm
    nex = t_lat // seq

    def body(x_ref, c_ref, mod_ref, g_ref, win_ref, gq_ref, gkv_ref, wuq_ref, wukv_ref, cos_ref, sin_ref,
             h_ref, pg_ref, rq_ref, rk_ref, rv_ref, nq_ref, nkv_ref, q_ref, k_ref, v_ref):
        i = pl.program_id(0)
        xt = jnp.where(i < nl, x_ref[...], c_ref[...])
        sh = mod_ref[0, 0:1, :]
        sc = mod_ref[0, 1:2, :]
        r = lax.rsqrt(jnp.mean(xt * xt, axis=-1, keepdims=True) + EPS)
        hb = ((xt * r) * g_ref[...] * (1.0 + sc) + sh).astype(BF16)
        h_ref[...] = hb
        p = _dot_nt(hb, win_ref[...])
        cos = cos_ref[...]
        sin = sin_ref[...]
        rq_ref[...] = _rope(p[:, 0:256], cos, sin).astype(BF16)
        rk_ref[...] = _rope(p[:, 256:512] * (RET_DK ** -0.5), cos, sin).astype(BF16)
        rv_ref[...] = p[:, 512:1024].astype(BF16)
        pg_ref[...] = p[:, 1024:2176]
        cq = p[:, 1536:1920]
        ckv = p[:, 1920:2176]
        nqb = (cq * lax.rsqrt(jnp.mean(cq * cq, axis=-1, keepdims=True) + EPS) * gq_ref[...]).astype(BF16)
        nkvb = (ckv * lax.rsqrt(jnp.mean(ckv * ckv, axis=-1, keepdims=True) + EPS) * gkv_ref[...]).astype(BF16)
        nq_ref[...] = nqb
        nkv_ref[...] = nkvb
        cos1 = cos[:, 0:LANES]
        sin1 = sin[:, 0:LANES]
        kpe = _rope(p[:, 2176:2304], cos1, sin1).astype(BF16)
        for hd in range(HEADS):
            o = hd * MLA_HEAD
            qh = _dot_nt(nqb, wuq_ref[hd]) * MLA_SCALE
            q_ref[:, o:o + 128] = qh[:, 0:128].astype(BF16)
            q_ref[:, o + 128:o + 256] = _rope(qh[:, 128:256], cos1, sin1).astype(BF16)
            kvh = _dot(nkvb, wukv_ref[hd])
            k_ref[:, o:o + 128] = kvh[:, 0:128].astype(BF16)
            k_ref[:, o + 128:o + 256] = kpe
            v_ref[:, hd * 128:(hd + 1) * 128] = kvh[:, 128:256].astype(BF16)

    def tile(width):
        return pl.BlockSpec((tm, width), lambda i: (i, 0))

    widths = (d, PG_COLS, 256, 256, 512, Q_LORA, KV_LORA, HEADS * MLA_HEAD, HEADS * MLA_HEAD, HEADS * 128)
    dtypes = (BF16, F32, BF16, BF16, BF16, BF16, BF16, BF16, BF16, BF16)
    tab = pl.BlockSpec((tm, 256), lambda i: (jnp.where(i < nl, i % tpe, tpe), 0))
    return _hosted_call(
        body, (x2, ctx2, modv, g_attn, w_in, g_q, g_kv, w_uq, w_ukv, cos_t, sin_t), name="pre_fwd", grid=(nl + nc,),
        in_specs=[
            pl.BlockSpec((tm, d), lambda i: (jnp.minimum(i, nl - 1), 0)),
            pl.BlockSpec((tm, d), lambda i: (jnp.maximum(i - nl, 0), 0)),
            pl.BlockSpec((1, 8, d), lambda i: (jnp.minimum(i // tpe, nex), 0, 0)),
            _full((1, d)), _full(w_in.shape), _full((1, Q_LORA)), _full((1, KV_LORA)),
            _full(w_uq.shape), _full(w_ukv.shape), tab, tab,
        ],
        out_specs=[tile(w) for w in widths],
        out_shape=[jax.ShapeDtypeStruct((n_all, w), dt) for w, dt in zip(widths, dtypes)],
        sem=("parallel",), rider=rider)


def _post(yret, ymla, x2, tgt2, modv, g_ffn, g_fin, w_out, w_ff1, w_ff2a, w_ff2b, *, seq, tm):
    t_lat, d = x2.shape
    nl = t_lat // tm
    tpe = seq // tm
    nex = t_lat // seq
    n_slab = w_ff1.shape[0]
    fs = w_ff1.shape[2]
    fh = w_ff2a.shape[1]

    def body(yr_ref, ym_ref, x_ref, t_ref, mod_ref, gf_ref, gl_ref, wo_ref, w1_ref, w2a_ref, w2b_ref,
             mix_ref, a_ref, du_ref, h2_ref, df_ref, dmo_ref, dmix_ref, dxm_ref, st_ref, ru_ref):
        i = pl.program_id(0)
        gt_a = mod_ref[0, 2:3, :]
        sh_f = mod_ref[0, 3:4, :]
        sc_f = mod_ref[0, 4:5, :]
        gt_f = mod_ref[0, 5:6, :]
        g_ffn_v = gf_ref[...]
        g_fin_v = gl_ref[...]
        yr = yr_ref[...]
        ym = ym_ref[...]
        mix_ref[:, 0:512] = yr
        mix_ref[:, 512:1024] = ym
        op = _dot(yr, wo_ref[0:512, :]) + _dot(ym, wo_ref[512:1024, :])
        x_mid = x_ref[...] + gt_a * op
        r2 = lax.rsqrt(jnp.mean(x_mid * x_mid, axis=-1, keepdims=True) + EPS)
        xh2 = x_mid * r2
        h2b = (xh2 * g_ffn_v * (1.0 + sc_f) + sh_f).astype(BF16)
        h2_ref[...] = h2b
        f = jnp.zeros((tm, d), F32)
        for s in range(n_slab):
            ru = jnp.maximum(_dot(h2b, w1_ref[s]), 0.0)
            ru_ref[:, s * fs:(s + 1) * fs] = ru
            ab = (ru * ru).astype(BF16)
            a_ref[:, s * fs:(s + 1) * fs] = ab
            f = f + _dot(ab[:, 0:fh], w2a_ref[s]) + _dot(ab[:, fh:fs], w2b_ref[s])
        x_out = x_mid + gt_f * f
        r3 = lax.rsqrt(jnp.mean(x_out * x_out, axis=-1, keepdims=True) + EPS)
        xh3 = x_out * r3
        err = xh3 * g_fin_v - t_ref[...]
        dy = err * (1.0 / d)
        dxh3 = dy * g_fin_v
        dx_out = r3 * (dxh3 - xh3 * jnp.mean(dxh3 * xh3, axis=-1, keepdims=True))
        dfb = (dx_out * gt_f).astype(BF16)
        df_ref[...] = dfb
        dh2 = jnp.zeros((tm, d), F32)
        for s in range(n_slab):
            da = jnp.concatenate([_dot_nt(dfb, w2a_ref[s]), _dot_nt(dfb, w2b_ref[s])], axis=1)
            dub = (da * (2.0 * ru_ref[:, s * fs:(s + 1) * fs])).astype(BF16)
            du_ref[:, s * fs:(s + 1) * fs] = dub
            dh2 = dh2 + _dot_nt(dub, w1_ref[s])
        dxh2 = dh2 * (1.0 + sc_f) * g_ffn_v
        dx_mid = dx_out + r2 * (dxh2 - xh2 * jnp.mean(dxh2 * xh2, axis=-1, keepdims=True))
        dxm_ref[...] = dx_mid
        dmob = (dx_mid * gt_a).astype(BF16)
        dmo_ref[...] = dmob
        dmix_ref[...] = _dot_nt(dmob, wo_ref[...]).astype(BF16)

        def rsum(v):
            return jnp.sum(v, axis=0, keepdims=True)

        stats = jnp.concatenate([
            rsum(dh2), rsum(dh2 * xh2 * g_ffn_v), rsum(dx_out * f), rsum(dx_mid * op),
            rsum(dh2 * (1.0 + sc_f) * xh2), rsum(dy * xh3), rsum(err * err), jnp.zeros((1, d), F32)], axis=0)

        @pl.when(i % tpe == 0)
        def _():
            st_ref[0] = stats

        @pl.when(i % tpe != 0)
        def _():
            st_ref[0] += stats

    def tile(width):
        return pl.BlockSpec((tm, width), lambda i: (i, 0))

    widths = (d, D_FF, D_FF, d, d, d, d, d)
    dtypes = (BF16, BF16, BF16, BF16, BF16, BF16, BF16, F32)
    const = pl.Buffered(1)
    return pl.pallas_call(
        body, name="post", grid=(nl,),
        in_specs=[
            tile(512), tile(512), tile(d), tile(d),
            pl.BlockSpec((1, 8, d), lambda i: (i // tpe, 0, 0)),
            _full((1, d)), _full((1, d)),
            pl.BlockSpec(w_out.shape, lambda i: (0, 0), pipeline_mode=const),
            pl.BlockSpec(w_ff1.shape, lambda i: (0, 0, 0), pipeline_mode=const),
            pl.BlockSpec(w_ff2a.shape, lambda i: (0, 0, 0), pipeline_mode=const),
            pl.BlockSpec(w_ff2b.shape, lambda i: (0, 0, 0), pipeline_mode=const),
        ],
        out_specs=[tile(w) for w in widths] + [pl.BlockSpec((1, 8, d), lambda i: (i // tpe, 0, 0))],
        out_shape=[jax.ShapeDtypeStruct((t_lat, w), dt) for w, dt in zip(widths, dtypes)]
        + [jax.ShapeDtypeStruct((nex, 8, d), F32)],
        scratch_shapes=[pltpu.VMEM((tm, D_FF), F32)],
        compiler_params=_params(("arbitrary",), VMEM_LIMIT),
    )(yret, ymla, x2, tgt2, modv, g_ffn, g_fin, w_out, w_ff1, w_ff2a, w_ff2b)


def _pre_bwd(x2, ctx2, modv, g_attn, pg, drq, drk, dkc_r, drv, dvc_r, drg, dq_m, dkl, dkc, dvl, dvc, dxm,
             w_in, g_q, g_kv, w_uq, w_ukv, cos_t, sin_t, *, seq, tm, rider=None):
    t_lat, d = x2.shape
    t_ctx = ctx2.shape[0]
    nl, nc = t_lat // tm, t_ctx // tm
    n_all = t_lat + t_ctx
    tpe = seq // tm
    nex = t_lat // seq

    def body(x_ref, c_ref, mod_ref, g_ref, pg_ref, drq_ref, drk_ref, dkcr_ref, drv_ref, dvcr_ref, drg_ref,
             dq_ref, dkl_ref, dkc_ref, dvl_ref, dvc_ref, dxm_ref, win_ref, gq_ref, gkv_ref, wuq_ref, wukv_ref,
             cos_ref, sin_ref, dpb_ref, dqf_ref, dkvf_ref, gx_ref, st_ref):
        i = pl.program_id(0)
        lat = i < nl
        latf = lat.astype(F32)
        cos = cos_ref[...]
        sin = sin_ref[...]
        cos1 = cos[:, 0:LANES]
        sin1 = sin[:, 0:LANES]
        d_rq = _rope_t(drq_ref[...] * latf, cos, sin)
        d_rk = _rope_t(jnp.where(lat, drk_ref[...], dkcr_ref[...]), cos, sin) * (RET_DK ** -0.5)
        d_rv = jnp.where(lat, drv_ref[...], dvcr_ref[...])
        d_rg = drg_ref[...] * latf
        dq_all = dq_ref[...] * (latf * MLA_SCALE)
        dk_all = jnp.where(lat, dkl_ref[...], dkc_ref[...])
        dv_all = jnp.where(lat, dvl_ref[...], dvc_ref[...])
        dnq = jnp.zeros((tm, Q_LORA), F32)
        dnkv = jnp.zeros((tm, KV_LORA), F32)
        dkpe = jnp.zeros((tm, LANES), F32)
        for hd in range(HEADS):
            o = hd * MLA_HEAD
            dqh = jnp.concatenate([dq_all[:, o:o + 128], _rope_t(dq_all[:, o + 128:o + 256], cos1, sin1)],
                                  axis=1).astype(BF16)
            dqf_ref[:, o:o + 256] = dqh
            dnq = dnq + _dot(dqh, wuq_ref[hd])
            dkpe = dkpe + dk_all[:, o + 128:o + 256]
            dkvh = jnp.concatenate([dk_all[:, o:o + 128], dv_all[:, hd * 128:(hd + 1) * 128]], axis=1).astype(BF16)
            dkvf_ref[:, o:o + 256] = dkvh
            dnkv = dnkv + _dot_nt(dkvh, wukv_ref[hd])
        d_kpe = _rope_t(dkpe, cos1, sin1)
        pgv = pg_ref[...]
        cq = pgv[:, 512:896]
        ckv = pgv[:, 896:1152]
        rq_ = lax.rsqrt(jnp.mean(cq * cq, axis=-1, keepdims=True) + EPS)
        cqh = cq * rq_
        dcqh = dnq * gq_ref[...]
        d_cq = rq_ * (dcqh - cqh * jnp.mean(dcqh * cqh, axis=-1, keepdims=True))
        rkv_ = lax.rsqrt(jnp.mean(ckv * ckv, axis=-1, keepdims=True) + EPS)
        ckvh = ckv * rkv_
        dckvh = dnkv * gkv_ref[...]
        d_ckv = rkv_ * (dckvh - ckvh * jnp.mean(dckvh * ckvh, axis=-1, keepdims=True))
        dpb = jnp.concatenate([d_rq, d_rk, d_rv, d_rg, d_cq, d_ckv, d_kpe], axis=1).astype(BF16)
        dpb_ref[...] = dpb
        dh = _dot(dpb, win_ref[...])
        xt = jnp.where(lat, x_ref[...], c_ref[...])
        sc = mod_ref[0, 1:2, :]
        g = g_ref[...]
        r = lax.rsqrt(jnp.mean(xt * xt, axis=-1, keepdims=True) + EPS)
        xh = xt * r
        dxh = dh * (1.0 + sc) * g
        dx = r * (dxh - xh * jnp.mean(dxh * xh, axis=-1, keepdims=True))

        @pl.when(lat)
        def _():
            gx_ref[...] = dxm_ref[...] + dx

        def rsum(v):
            return jnp.sum(v, axis=0, keepdims=True)

        def widen(v):
            return jnp.concatenate([v, jnp.zeros((1, d - v.shape[1]), F32)], axis=1)

        stats = jnp.concatenate([
            rsum(dh), rsum(dh * xh * g), rsum(dh * (1.0 + sc) * xh), widen(rsum(dnq * cqh)), widen(rsum(dnkv * ckvh)),
            jnp.zeros((3, d), F32)], axis=0)
        first = jnp.logical_or(jnp.logical_and(lat, i % tpe == 0), i == nl)

        @pl.when(first)
        def _():
            st_ref[0] = stats

        @pl.when(jnp.logical_not(first))
        def _():
            st_ref[0] += stats

    def lat_tile(width):
        return pl.BlockSpec((tm, width), lambda i: (jnp.minimum(i, nl - 1), 0))

    def ctx_tile(width):
        return pl.BlockSpec((tm, width), lambda i: (jnp.maximum(i - nl, 0), 0))

    def tile(width):
        return pl.BlockSpec((tm, width), lambda i: (i, 0))

    tab = pl.BlockSpec((tm, 256), lambda i: (jnp.where(i < nl, i % tpe, tpe), 0))
    ex = pl.BlockSpec((1, 8, d), lambda i: (jnp.minimum(i // tpe, nex), 0, 0))
    return _hosted_call(
        body, (x2, ctx2, modv, g_attn, pg, drq, drk, dkc_r, drv, dvc_r, drg, dq_m, dkl, dkc, dvl, dvc, dxm,
               w_in, g_q, g_kv, w_uq, w_ukv, cos_t, sin_t), name="pre_bwd", grid=(nl + nc,),
        in_specs=[
            lat_tile(d), ctx_tile(d), ex, _full((1, d)), tile(PG_COLS),
            lat_tile(256), lat_tile(256), ctx_tile(256), lat_tile(512), ctx_tile(512), lat_tile(512),
            lat_tile(1024), lat_tile(1024), ctx_tile(1024), lat_tile(512), ctx_tile(512), lat_tile(d),
            _once(w_in.shape), _full((1, Q_LORA)), _full((1, KV_LORA)), _once(w_uq.shape), _once(w_ukv.shape),
            tab, tab,
        ],
        out_specs=[tile(IN_PAD), tile(1024), tile(1024), lat_tile(d), ex],
        out_shape=[
            jax.ShapeDtypeStruct((n_all, IN_PAD), BF16), jax.ShapeDtypeStruct((n_all, 1024), BF16),
            jax.ShapeDtypeStruct((n_all, 1024), BF16), jax.ShapeDtypeStruct((t_lat, d), F32),
            jax.ShapeDtypeStruct((nex + 1, 8, d), F32),
        ],
        sem=("arbitrary",), rider=rider)


MLA_SCALE = 1.0 / math.sqrt(MLA_NOPE + MLA_ROPE)
KEY_BLOCK = 1024


def _mla_specs(t_lat, seq, ctx_len, tq, heads=1):
    nqt = seq // tq
    cb = t_lat // ctx_len
    q = pl.BlockSpec((tq, heads * MLA_HEAD), lambda b, h, j: (b * nqt + j, h))
    kl = pl.BlockSpec((seq, heads * MLA_HEAD), lambda b, h, j: (b, h))
    kc = pl.BlockSpec((ctx_len, heads * MLA_HEAD), lambda b, h, j: (cb + b, h))
    vl = pl.BlockSpec((seq, heads * 128), lambda b, h, j: (b, h))
    vc = pl.BlockSpec((ctx_len, heads * 128), lambda b, h, j: (cb + b, h))
    o = pl.BlockSpec((tq, heads * 128), lambda b, h, j: (b * nqt + j, h))
    return q, kl, kc, vl, vc, o


FWD_HEADS = 2
BWD_HEADS = 1


def _mla_fwd(q, k, v, *, t_lat, seq, ctx_len, tq, rider=None):
    nex = t_lat // seq

    def body(q_ref, kl_ref, kc_ref, vl_ref, vc_ref, o_ref, lse_ref):
        for hh in range(FWD_HEADS):
            wide = slice(hh * MLA_HEAD, (hh + 1) * MLA_HEAD)
            cols = slice(hh * 128, (hh + 1) * 128)
            qb = q_ref[:, wide]
            s = _dot_nt(qb, kl_ref[:, wide])
            sc = _dot_nt(qb, kc_ref[:, wide])
            m = jnp.maximum(jnp.max(s, axis=-1, keepdims=True), jnp.max(sc, axis=-1, keepdims=True))
            p = jnp.exp(s - m)
            pc = jnp.exp(sc - m)
            total = jnp.sum(p, axis=-1, keepdims=True) + jnp.sum(pc, axis=-1, keepdims=True)
            o = _dot(p.astype(BF16), vl_ref[:, cols]) + _dot(pc.astype(BF16), vc_ref[:, cols])
            o_ref[:, cols] = (o * (1.0 / total)).astype(BF16)
            lse_ref[:, cols] = jnp.broadcast_to(m + jnp.log(total), (tq, 128))

    qs, kl, kc, vl, vc, os_ = _mla_specs(t_lat, seq, ctx_len, tq, FWD_HEADS)
    return _hosted_call(
        body, (q, k, k, v, v), name="mla_fwd", grid=(nex, HEADS // FWD_HEADS, seq // tq),
        in_specs=[qs, kl, kc, vl, vc], out_specs=[os_, os_],
        out_shape=[jax.ShapeDtypeStruct((t_lat, HEADS * 128), BF16), jax.ShapeDtypeStruct((t_lat, HEADS * 128), F32)],
        sem=("parallel", "parallel", "arbitrary"), rider=rider)


def _mla_bwd(q, k, v, ymla, lse, dmix, *, t_lat, seq, ctx_len, tq, rider=None):
    nex = t_lat // seq
    nqt = seq // tq
    t_ctx = nex * ctx_len
    kb = min(KEY_BLOCK, seq)

    def body(q_ref, kl_ref, kc_ref, vl_ref, vc_ref, o_ref, lse_ref, do_ref, dq_ref, dkl_out, dkc_out, dvl_out, dvc_out,
             dkl_ref, dkc_ref, dvl_ref, dvc_ref):
        j = pl.program_id(2)

        @pl.when(j == 0)
        def _():
            dkl_ref[...] = jnp.zeros(dkl_ref.shape, F32)
            dkc_ref[...] = jnp.zeros(dkc_ref.shape, F32)
            dvl_ref[...] = jnp.zeros(dvl_ref.shape, F32)
            dvc_ref[...] = jnp.zeros(dvc_ref.shape, F32)

        for hh in range(BWD_HEADS):
            wide = slice(hh * MLA_HEAD, (hh + 1) * MLA_HEAD)
            cols = slice(hh * 128, (hh + 1) * 128)
            qb = q_ref[:, wide]
            dob = do_ref[:, cols]
            delta = jnp.sum(dob.astype(F32) * o_ref[:, cols].astype(F32), axis=-1, keepdims=True)
            lse_row = lse_ref[:, hh * 128:hh * 128 + 1]

            def block(k_ref, v_ref, dk_ref, dv_ref, rows):
                kbl = k_ref[rows, wide]
                vbl = v_ref[rows, cols]
                p = jnp.exp(_dot_nt(qb, kbl) - lse_row)
                ds = (p * (_dot_nt(dob, vbl) - delta)).astype(BF16)
                dk_ref[rows, wide] += _dot_tn(ds, qb)
                dv_ref[rows, cols] += _dot_tn(p.astype(BF16), dob)
                return _dot(ds, kbl)

            dq = block(kc_ref, vc_ref, dkc_ref, dvc_ref, pl.ds(0, ctx_len))
            for i in range(seq // kb):
                dq = dq + block(kl_ref, vl_ref, dkl_ref, dvl_ref, pl.ds(i * kb, kb))
            dq_ref[:, wide] = dq.astype(BF16)

        @pl.when(j == nqt - 1)
        def _():
            dkl_out[...] = dkl_ref[...].astype(BF16)
            dkc_out[...] = dkc_ref[...].astype(BF16)
            dvl_out[...] = dvl_ref[...].astype(BF16)
            dvc_out[...] = dvc_ref[...].astype(BF16)

    g = BWD_HEADS
    qs, kl, kc, vl, vc, os_ = _mla_specs(t_lat, seq, ctx_len, tq, g)
    do_spec = pl.BlockSpec((tq, g * 128), lambda b, h, j: (b * nqt + j, HEADS // g + h))
    key_blocks = [(seq, g * MLA_HEAD), (ctx_len, g * MLA_HEAD), (seq, g * 128), (ctx_len, g * 128)]
    return _hosted_call(
        body, (q, k, k, v, v, ymla, lse, dmix), name="mla_bwd", grid=(nex, HEADS // g, nqt),
        in_specs=[qs, kl, kc, vl, vc, os_, os_, do_spec],
        out_specs=[qs] + [pl.BlockSpec(blk, lambda b, h, j: (b, h)) for blk in key_blocks],
        out_shape=[
            jax.ShapeDtypeStruct((t_lat, HEADS * MLA_HEAD), BF16),
            jax.ShapeDtypeStruct((t_lat, HEADS * MLA_HEAD), BF16),
            jax.ShapeDtypeStruct((t_ctx, HEADS * MLA_HEAD), BF16),
            jax.ShapeDtypeStruct((t_lat, HEADS * 128), BF16),
            jax.ShapeDtypeStruct((t_ctx, HEADS * 128), BF16),
        ],
        scratch_shapes=[pltpu.VMEM(blk, F32) for blk in key_blocks],
        sem=("parallel", "parallel", "arbitrary"), rider=rider)


def _decay_terms(lg, chunk, forward):
    ii = lax.broadcasted_iota(jnp.int32, (chunk, chunk), 0)
    jj = lax.broadcasted_iota(jnp.int32, (chunk, chunk), 1)
    diff = (ii - jj) if forward else (jj - ii)
    dist = jnp.maximum(diff, 0).astype(F32)
    dmat = jnp.where(diff >= 0, jnp.exp(lg * dist), 0.0)
    pos = lax.broadcasted_iota(jnp.int32, (chunk, 1), 0).astype(F32)
    if forward:
        e_q = pos + 1.0
        e_k = (chunk - 1.0) - pos
    else:
        e_q = chunk - pos
        e_k = pos
    wq = jnp.exp(lg * e_q)
    wk = jnp.exp(lg * e_k)
    cd = jnp.exp(jnp.full((1, 1), lg * chunk, F32))
    return dmat, dist, wq, wk, e_q, e_k, cd


def _ctx_weights(lg, ctx_len, forward):
    pos = lax.broadcasted_iota(jnp.int32, (ctx_len, 1), 0).astype(F32)
    e = ((ctx_len - 1.0) - pos) if forward else pos
    return jnp.exp(lg * e), e


def _pair_specs(t_lat, seq, ctx_len):
    cb = t_lat // ctx_len
    qk = pl.BlockSpec((seq, 128), lambda b, p: (b, p))
    v = pl.BlockSpec((seq, 256), lambda b, p: (b, p))
    kc = pl.BlockSpec((ctx_len, 128), lambda b, p: (cb + b, p))
    vc = pl.BlockSpec((ctx_len, 256), lambda b, p: (cb + b, p))
    return qk, v, kc, vc


def _lane_masks():
    lane = lax.broadcasted_iota(jnp.int32, (1, 128), 1)
    return [(lane // RET_DK) == hh for hh in (0, 1)]


def _ret_fwd_pair(rq, rk, rv, pg, lg, g_ret, *, t_lat, seq, ctx_len, chunk, rider=None):
    nex = t_lat // seq
    n_chunk = seq // chunk

    def body(q_ref, k_ref, v_ref, kc_ref, vc_ref, rg_ref, lg_ref, g_ref, y_ref, o_ref):
        pair = pl.program_id(1)
        masks = _lane_masks()
        kcf = kc_ref[...].astype(F32)
        chains = [(forward, hh) for forward in (True, False) for hh in (0, 1)]
        terms, s0 = [], []
        for forward, hh in chains:
            lgd = lg_ref[0 if forward else 1, 2 * pair + hh]
            terms.append(_decay_terms(lgd, chunk, forward))
            wc, _ = _ctx_weights(lgd, ctx_len, forward)
            s0.append(_dot_tn((jnp.where(masks[hh], kcf, 0.0) * wc).astype(BF16), vc_ref[:, hh * 128:(hh + 1) * 128]))
        both = [terms[hh][0] + terms[2 + hh][0] for hh in (0, 1)]
        o_ref[...] = jnp.zeros(o_ref.shape, F32)

        def step(t, states):
            new = [None] * 4
            for forward in (True, False):
                n = t if forward else n_chunk - 1 - t
                sl = pl.ds(pl.multiple_of(n * chunk, chunk), chunk)
                qb = q_ref[sl, :]
                kf_all = k_ref[sl, :].astype(F32)
                for hh in (0, 1):
                    c = (0 if forward else 2) + hh
                    _, _, wq, wk, _, _, cd = terms[c]
                    cols = slice(hh * 128, (hh + 1) * 128)
                    qm = jnp.where(masks[hh], qb, jnp.zeros((), BF16))
                    kf = jnp.where(masks[hh], kf_all, 0.0)
                    vb = v_ref[sl, cols]
                    o = wq * _dot(qm, states[c].astype(BF16))
                    if forward:
                        o = o + _dot((_dot_nt(qm, kf.astype(BF16)) * both[hh]).astype(BF16), vb)
                    o_ref[sl, cols] += o
                    new[c] = cd * states[c] + _dot_tn((kf * wk).astype(BF16), vb)
            return tuple(new)

        lax.fori_loop(0, n_chunk, step, tuple(s0))

        def norm_step(n, carry):
            sl = pl.ds(pl.multiple_of(n * chunk, chunk), chunk)
            for hh in (0, 1):
                cols = slice(hh * 128, (hh + 1) * 128)
                o = o_ref[sl, cols]
                mu = jnp.mean(o, axis=-1, keepdims=True)
                oc = o - mu
                var = jnp.mean(oc * oc, axis=-1, keepdims=True)
                rg = rg_ref[sl, cols]
                y_ref[sl, cols] = (oc * lax.rsqrt(var + EPS) * g_ref[:, cols] * (rg / (1.0 + jnp.exp(-rg)))).astype(BF16)
            return carry

        lax.fori_loop(0, n_chunk, norm_step, 0)

    qk, v, kc, vc = _pair_specs(t_lat, seq, ctx_len)
    return _hosted_call(
        body, (rq, rk, rv, rk, rv, pg, lg, g_ret), name="ret_fwd", grid=(nex, HEADS // 2),
        in_specs=[qk, qk, v, kc, vc, v, pl.BlockSpec(memory_space=pltpu.SMEM), pl.BlockSpec((1, 256), lambda b, p: (0, p))],
        out_specs=[v, v],
        out_shape=[jax.ShapeDtypeStruct((t_lat, HEADS * RET_DV), BF16), jax.ShapeDtypeStruct((t_lat, HEADS * RET_DV), F32)],
        sem=("parallel", "arbitrary"), rider=rider)


def _ret_bwd_pair(rq, rk, rv, pg, osum, dmix, lg, g_ret, *, t_lat, seq, ctx_len, chunk, rider=None):
    nex = t_lat // seq
    n_chunk = seq // chunk
    t_ctx = nex * ctx_len

    def body(q_ref, k_ref, v_ref, kc_ref, vc_ref, rg_ref, o_ref, dy_ref, lg_ref, g_ref,
             dq_out, dk_out, dv_out, dkc_ref, dvc_ref, drg_ref, st_ref, do_s, s_st, dq_ref, dk_ref, dv_ref):
        pair = pl.program_id(1)
        masks = _lane_masks()
        kcf = kc_ref[...].astype(F32)

        def norm_step(n, dgains):
            sl = pl.ds(pl.multiple_of(n * chunk, chunk), chunk)
            out = []
            for hh in (0, 1):
                cols = slice(hh * 128, (hh + 1) * 128)
                gain = g_ref[:, cols]
                o = o_ref[sl, cols]
                mu = jnp.mean(o, axis=-1, keepdims=True)
                oc = o - mu
                rstd = lax.rsqrt(jnp.mean(oc * oc, axis=-1, keepdims=True) + EPS)
                ohat = oc * rstd
                rg = rg_ref[sl, cols]
                sg = 1.0 / (1.0 + jnp.exp(-rg))
                dy = dy_ref[sl, cols].astype(F32)
                don = dy * (rg * sg)
                drg_ref[sl, cols] = (dy * (ohat * gain) * (sg * (1.0 + rg * (1.0 - sg)))).astype(BF16)
                dohat = don * gain
                do_s[sl, cols] = rstd * (dohat - jnp.mean(dohat, axis=-1, keepdims=True)
                                         - ohat * jnp.mean(dohat * ohat, axis=-1, keepdims=True))
                out.append(dgains[hh] + jnp.sum(don * ohat, axis=0, keepdims=True))
            return tuple(out)

        zero_row = jnp.zeros((1, 128), F32)
        dgains = lax.fori_loop(0, n_chunk, norm_step, (zero_row, zero_row))
        dq_ref[...] = jnp.zeros(dq_ref.shape, F32)
        dk_ref[...] = jnp.zeros(dk_ref.shape, F32)
        dv_ref[...] = jnp.zeros(dv_ref.shape, F32)

        chains = [(forward, hh) for forward in (True, False) for hh in (0, 1)]
        terms, ctxw, s0 = [], [], []
        for forward, hh in chains:
            lgd = lg_ref[0 if forward else 1, 2 * pair + hh]
            terms.append(_decay_terms(lgd, chunk, forward))
            ctxw.append(_ctx_weights(lgd, ctx_len, forward))
            s0.append(_dot_tn((jnp.where(masks[hh], kcf, 0.0) * ctxw[-1][0]).astype(BF16), vc_ref[:, hh * 128:(hh + 1) * 128]))

        def chunk_at(t, ascending):
            n = t if ascending else n_chunk - 1 - t
            return n, pl.ds(pl.multiple_of(n * chunk, chunk), chunk)

        def state_step(t, states):
            new = []
            for c, (forward, hh) in enumerate(chains):
                n, sl = chunk_at(t, forward)
                wk, cd = terms[c][3], terms[c][6]
                s_st[c, n] = states[c]
                kf = jnp.where(masks[hh], k_ref[sl, :].astype(F32), 0.0)
                new.append(cd * states[c] + _dot_tn((kf * wk).astype(BF16), v_ref[sl, hh * 128:(hh + 1) * 128]))
            return tuple(new)

        lax.fori_loop(0, n_chunk, state_step, tuple(s0))

        both = [terms[hh][0] + terms[2 + hh][0] for hh in (0, 1)]

        def grad_step(t, carry):
            out = [None] * len(chains)
            in_chunk_b = [None, None]
            for forward in (True, False):
                n, sl = chunk_at(t, not forward)
                qb = q_ref[sl, :]
                kf_all = k_ref[sl, :].astype(F32)
                dq_sum = jnp.zeros((chunk, 128), F32)
                dk_sum = jnp.zeros((chunk, 128), F32)
                for hh in (0, 1):
                    c = (0 if forward else 2) + hh
                    g_next, dlg = carry[c]
                    dmat, dist, wq, wk, e_q, e_k, cd = terms[c]
                    cols = slice(hh * 128, (hh + 1) * 128)
                    qm = jnp.where(masks[hh], qb, jnp.zeros((), BF16))
                    kf = jnp.where(masks[hh], kf_all, 0.0)
                    kb = kf.astype(BF16)
                    vb = v_ref[sl, cols]
                    do = do_s[sl, cols]
                    dob = do.astype(BF16)
                    s_n = s_st[c, n]
                    s_nb = s_n.astype(BF16)
                    gb = g_next.astype(BF16)
                    dk_cross = wk * _dot_nt(vb, gb)
                    dv = _dot((kf * wk).astype(BF16), gb)
                    o_cross = wq * _dot(qm, s_nb)
                    dq_sum = dq_sum + wq * _dot_nt(dob, s_nb)
                    dk_sum = dk_sum + dk_cross
                    dlg = (dlg + chunk * cd * jnp.sum(g_next * s_n, keepdims=True)
                           + jnp.sum(e_k * jnp.sum(kf * dk_cross, axis=-1, keepdims=True), keepdims=True)
                           + jnp.sum(e_q * jnp.sum(o_cross * do, axis=-1, keepdims=True), keepdims=True))
                    if forward:
                        a_raw = _dot_nt(qm, kb)
                        da_raw = _dot_nt(dob, vb)
                        prod = a_raw * da_raw
                        dlg = dlg + jnp.sum(dist * dmat * prod, keepdims=True)
                        in_chunk_b[hh] = jnp.sum(terms[2 + hh][1] * terms[2 + hh][0] * prod, keepdims=True)
                        dab = (da_raw * both[hh]).astype(BF16)
                        dq_sum = dq_sum + _dot(dab, kb)
                        dk_sum = dk_sum + _dot_tn(dab, qm)
                        dv = dv + _dot_tn((a_raw * both[hh]).astype(BF16), dob)
                    else:
                        dlg = dlg + in_chunk_b[hh]
                    dv_ref[sl, cols] += dv
                    out[c] = (cd * g_next + _dot_tn((qm.astype(F32) * wq).astype(BF16), dob), dlg)
                dq_ref[sl, :] += dq_sum
                dk_ref[sl, :] += dk_sum
            return tuple(out)

        zero = (jnp.zeros((128, 128), F32), jnp.zeros((1, 1), F32))
        res = lax.fori_loop(0, n_chunk, grad_step, (zero,) * len(chains))
        dkc_sum = jnp.zeros((ctx_len, 128), F32)
        dvc = [jnp.zeros((ctx_len, 128), F32)] * 2
        dlgs = []
        for c, (forward, hh) in enumerate(chains):
            ds0, dlg = res[c]
            wc, e_c = ctxw[c]
            kcm = jnp.where(masks[hh], kcf, 0.0)
            ds0b = ds0.astype(BF16)
            dkc_part = wc * _dot_nt(vc_ref[:, hh * 128:(hh + 1) * 128], ds0b)
            dkc_sum = dkc_sum + dkc_part
            dvc[hh] = dvc[hh] + _dot((kcm * wc).astype(BF16), ds0b)
            dlgs.append(dlg + jnp.sum(e_c * jnp.sum(kcm * dkc_part, axis=-1, keepdims=True), keepdims=True))
        dq_out[...] = dq_ref[...].astype(BF16)
        dk_out[...] = dk_ref[...].astype(BF16)
        dv_out[...] = dv_ref[...].astype(BF16)
        dkc_ref[...] = dkc_sum
        for hh in (0, 1):
            cols = slice(hh * 128, (hh + 1) * 128)
            dvc_ref[:, cols] = dvc[hh]
            st_ref[0, :, cols] = jnp.concatenate([
                dgains[hh], jnp.broadcast_to(dlgs[hh], (1, 128)), jnp.broadcast_to(dlgs[2 + hh], (1, 128)),
                jnp.zeros((5, 128), F32)], axis=0)

    qk, v, kc, vc = _pair_specs(t_lat, seq, ctx_len)
    return _hosted_call(
        body, (rq, rk, rv, rk, rv, pg, osum, dmix, lg, g_ret), name="ret_bwd", grid=(nex, HEADS // 2),
        in_specs=[qk, qk, v, kc, vc, v, v, v, pl.BlockSpec(memory_space=pltpu.SMEM),
                  pl.BlockSpec((1, 256), lambda b, p: (0, p))],
        out_specs=[
            qk, qk, v,
            pl.BlockSpec((ctx_len, 128), lambda b, p: (b, p)),
            pl.BlockSpec((ctx_len, 256), lambda b, p: (b, p)),
            v,
            pl.BlockSpec((1, 8, 256), lambda b, p: (b, 0, p)),
        ],
        out_shape=[
            jax.ShapeDtypeStruct((t_lat, 256), BF16), jax.ShapeDtypeStruct((t_lat, 256), BF16),
            jax.ShapeDtypeStruct((t_lat, 512), BF16), jax.ShapeDtypeStruct((t_ctx, 256), F32),
            jax.ShapeDtypeStruct((t_ctx, 512), F32), jax.ShapeDtypeStruct((t_lat, 512), BF16),
            jax.ShapeDtypeStruct((nex, 8, 512), F32),
        ],
        scratch_shapes=[pltpu.VMEM((seq, 256), F32), pltpu.VMEM((4, n_chunk, 128, 128), F32),
                        pltpu.VMEM((seq, 128), F32), pltpu.VMEM((seq, 128), F32), pltpu.VMEM((seq, 256), F32)],
        sem=("parallel", "arbitrary"), rider=rider)


def _matmul_tn(a, b, *, bm, bn, bk, chip_major, name, out_dtype=F32, rider=None):
    tk, m = a.shape
    n = b.shape[1]
    slab = n // N_CHIPS
    per_block = bn // slab if chip_major else 1
    bk = max(c for c in range(LANES, min(bk, tk) + 1, LANES) if tk % c == 0)
    nk = tk // bk
    blk = (per_block, bm, slab) if chip_major else (bm, bn)

    def body(a_ref, b_ref, o_ref, acc_ref):
        k = pl.program_id(2)
        if chip_major:
            parts = [_dot_tn(a_ref[...], b_ref[:, s * slab:(s + 1) * slab]) for s in range(per_block)]
        else:
            parts = [_dot_tn(a_ref[...], b_ref[...])]

        @pl.when(k == 0)
        def _():
            for s, part in enumerate(parts):
                if chip_major:
                    acc_ref[s] = part
                else:
                    acc_ref[...] = part

        @pl.when(k > 0)
        def _():
            for s, part in enumerate(parts):
                if chip_major:
                    acc_ref[s] += part
                else:
                    acc_ref[...] += part

        @pl.when(k == nk - 1)
        def _():
            o_ref[...] = acc_ref[...].astype(out_dtype)

    if chip_major:
        out_spec = pl.BlockSpec(blk, lambda i, j, k: (j, i, 0))
        out_shape = jax.ShapeDtypeStruct((N_CHIPS, m, slab), out_dtype)
    else:
        out_spec = pl.BlockSpec(blk, lambda i, j, k: (i, j))
        out_shape = jax.ShapeDtypeStruct((m, n), out_dtype)
    (out,), carried = _hosted_call(
        body, (a, b), name=name, grid=(m // bm, n // bn, nk),
        in_specs=[pl.BlockSpec((bk, bm), lambda i, j, k: (k, i)), pl.BlockSpec((bk, bn), lambda i, j, k: (k, j))],
        out_specs=[out_spec], out_shape=[out_shape], scratch_shapes=[pltpu.VMEM(blk, F32)],
        sem=("parallel", "parallel", "arbitrary"), rider=rider)
    return out if rider is None else (out, carried)


_LATE = ("w_out", "w_ff1", "w_ff2")
_EARLY = ("w_in", "w_uq", "w_ukv")


def _local_step(x, ctx, tgt, modv, lg, g_attn, g_ffn, g_fin, g_ret, g_q, g_kv, w_in, w_uq, w_ukv, late, place=None,
                *, tm=256, tq=256, chunk=256):
    nex, seq, d = x.shape
    ctx_len = ctx.shape[1]
    t_lat = nex * seq
    tm = min(tm, seq)
    x2 = x.reshape(t_lat, d)
    ctx2 = ctx.reshape(nex * ctx_len, d)
    tgt2 = tgt.reshape(t_lat, d)
    tm_fwd = min(2 * tm, seq)
    cos_t, sin_t = _rope_tables(seq, tm)
    dims = dict(t_lat=t_lat, seq=seq, ctx_len=ctx_len)
    alone = place is None

    (hb, pg, rq, rk, rv, nq, nkv, q, k, v), crossed_a = _pre_fwd(
        x2, ctx2, modv, g_attn, w_in, g_q, g_kv, w_uq, w_ukv, *_rope_tables(seq, tm_fwd), seq=seq, tm=tm_fwd,
        rider=None if alone else _gather_ici_rider([late[2]]))
    (yret, osum), got = _ret_fwd_pair(
        rq, rk, rv, pg, lg, g_ret, chunk=min(2 * chunk, seq), **dims,
        rider=None if alone else _merge_riders(_gather_d2d_rider(crossed_a), _gather_ici_rider([late[3]])))
    (ymla, lse), got_rest = _mla_fwd(
        q, k, v, tq=tq, **dims,
        rider=None if alone else _merge_riders(_gather_rider([late[0], late[1]], staged=True), _gather_d2d_rider(got[1:])))
    w_out, w_ff1, w_ff2a, w_ff2b = late if alone else (got_rest[0], got_rest[1], got[0], got_rest[2])
    mix, act, du, h2, df, dmo, dmix, dxm, st_post = _post(yret, ymla, x2, tgt2, modv, g_ffn, g_fin, w_out.reshape(d, d),
                                                         w_ff1, w_ff2a, w_ff2b, seq=seq, tm=min(tm, 256))
    kw = dict(bm=1024, bn=1024, bk=2048, out_dtype=BF16)
    g_ff2 = _matmul_tn(act, df, chip_major=False, name="gw_ff2", **kw).reshape(N_CHIPS, D_FF // N_CHIPS, d)
    if alone:
        g_ff1 = _matmul_tn(h2, du, chip_major=True, name="gw_ff1", **kw)
        g_out = _matmul_tn(mix, dmo, chip_major=False, name="gw_out", **kw).reshape(N_CHIPS, d // N_CHIPS, d)
        (dq_m, dkl, dkc, dvl, dvc), _ = _mla_bwd(q, k, v, ymla, lse, dmix, tq=tq, **dims)
        (drq, drk, drv, dkc_r, dvc_r, drg, st_ret), _ = _ret_bwd_pair(rq, rk, rv, pg, osum, dmix, lg, g_ret, chunk=chunk,
                                                                      **dims)
        late_out = [g_out, g_ff1, g_ff2]
    else:
        core, slot = place
        g_ff1, x_ff2 = _matmul_tn(h2, du, chip_major=True, name="gw_ff1", rider=_exchange_rider([g_ff2]), **kw)
        g_out, x_ff1 = _matmul_tn(mix, dmo, chip_major=False, name="gw_out", rider=_exchange_rider([g_ff1]), **kw)
        g_out = g_out.reshape(N_CHIPS, d // N_CHIPS, d)
        p_ff2 = _add_half(g_ff2, x_ff2[0], core, "add_half_w_ff2")
        p_ff1 = _add_half(g_ff1, x_ff1[0], core, "add_half_w_ff1")
        (dq_m, dkl, dkc, dvl, dvc), (l_ff2, l_ff1, x_out) = _mla_bwd(
            q, k, v, ymla, lse, dmix, tq=min(seq, 512), **dims,
            rider=_merge_riders(_scatter_rider([p_ff2, p_ff1]), _exchange_rider([g_out])))
        p_out = _add_half(g_out, x_out, core, "add_half_w_out")
        m_ff2 = _sum_chips(p_ff2, l_ff2, slot, "sum_chips_w_ff2")
        m_ff1 = _sum_chips(p_ff1, l_ff1, slot, "sum_chips_w_ff1")
        (drq, drk, drv, dkc_r, dvc_r, drg, st_ret), (l_out,) = _ret_bwd_pair(
            rq, rk, rv, pg, osum, dmix, lg, g_ret, chunk=chunk, **dims, rider=_scatter_rider([p_out]))
        late_out = [_sum_chips(p_out, l_out, slot, "sum_chips_w_out"), m_ff1, m_ff2]
    (dpb, dqf, dkvf, gx, st_pre), _ = _pre_bwd(
        x2, ctx2, modv, g_attn, pg, drq, drk, dkc_r, drv, dvc_r, drg, dq_m, dkl, dkc, dvl, dvc, dxm, w_in, g_q, g_kv,
        w_uq, w_ukv, cos_t, sin_t, seq=seq, tm=tm)
    g_early = [
        _matmul_tn(dpb, hb, bm=IN_PAD // 2, bn=d, bk=1536, chip_major=False, name="gw_in"),
        _matmul_tn(dqf, nq, bm=HEADS * MLA_HEAD, bn=Q_LORA, bk=1536, chip_major=False, name="gw_uq"),
        _matmul_tn(nkv, dkvf, bm=KV_LORA, bn=HEADS * 256, bk=1536, chip_major=True, name="gw_ukv"),
    ]
    return gx.reshape(nex, seq, d), g_early, late_out, st_post, st_ret, st_pre


_ANY = pl.BlockSpec(memory_space=pl.ANY)
_VMEM = pl.BlockSpec(memory_space=pltpu.VMEM)
_OFFSETS = tuple((dx, dy, dc) for dx in (0, 1) for dy in (0, 1) for dc in (0, 1))[1:]
_CHIP_OFFSETS = ((1, 0), (0, 1), (1, 1))


def _place():
    return lax.axis_index("x"), lax.axis_index("y"), lax.axis_index("c")


def _flip(v, d):
    return 1 - v if d else v


def _gather8_rider(a, in_vmem=True):
    def copies(a_ref, o_ref, send, recv):
        x, y, z = _place()
        me = 4 * x + 2 * y + z
        out = []
        for k, (dx, dy, dc) in enumerate(_OFFSETS):
            peer = (_flip(x, dx), _flip(y, dy), _flip(z, dc))
            landing = o_ref.at[4 * peer[0] + 2 * peer[1] + peer[2]]
            out.append((
                pltpu.make_async_remote_copy(src_ref=a_ref, dst_ref=o_ref.at[me], send_sem=send.at[k],
                                             recv_sem=recv.at[k], device_id=peer, device_id_type=MESH),
                pltpu.make_async_remote_copy(src_ref=a_ref, dst_ref=landing, send_sem=send.at[k],
                                             recv_sem=recv.at[k], device_id=peer, device_id_type=MESH)))
        return me, out

    def start(ins, outs, sems):
        me, cps = copies(ins[0], outs[0], sems[0], sems[1])
        pltpu.make_async_copy(ins[0], outs[0].at[me], sems[2]).start()
        for out_cp, _ in cps:
            out_cp.start()

    def finish(ins, outs, sems):
        me, cps = copies(ins[0], outs[0], sems[0], sems[1])
        for out_cp, in_cp in cps:
            in_cp.wait_recv()
            out_cp.wait_send()
        pltpu.make_async_copy(ins[0], outs[0].at[me], sems[2]).wait()

    spec = [_VMEM] if in_vmem else [_ANY]
    return _Rider([a], [jax.ShapeDtypeStruct((N_DEV,) + a.shape, a.dtype)],
                  [pltpu.SemaphoreType.DMA((7,)), pltpu.SemaphoreType.DMA((7,)), pltpu.SemaphoreType.DMA],
                  start, finish, in_specs=spec, out_specs=spec)


def _merge_riders(*riders):
    ins, outs, sems, in_specs, out_specs, aliases, cuts = [], [], [], [], [], {}, []
    for r in riders:
        cuts.append((len(ins), len(outs), len(sems)))
        aliases.update({len(ins) + i: len(outs) + j for i, j in r.aliases.items()})
        ins += r.ins
        outs += r.out_shapes
        sems += r.sems
        in_specs += r.in_specs
        out_specs += r.out_specs

    def part(r, cut, r_ins, r_outs, r_sems):
        return (r_ins[cut[0]:cut[0] + len(r.ins)], r_outs[cut[1]:cut[1] + len(r.out_shapes)],
                r_sems[cut[2]:cut[2] + len(r.sems)])

    def start(r_ins, r_outs, r_sems):
        for r, cut in zip(riders, cuts):
            r.start(*part(r, cut, r_ins, r_outs, r_sems))

    def finish(r_ins, r_outs, r_sems):
        for r, cut in zip(riders, cuts):
            r.finish(*part(r, cut, r_ins, r_outs, r_sems))

    def middle(r_ins, r_outs, r_sems):
        for r, cut in zip(riders, cuts):
            if r.middle is not None:
                r.middle(*part(r, cut, r_ins, r_outs, r_sems))

    return _Rider(ins, outs, sems, start, finish, aliases=aliases, in_specs=in_specs, out_specs=out_specs,
                  middle=middle if any(r.middle is not None for r in riders) else None)


def _allgather8(a, name):
    return _run_rider(_gather8_rider(a), name)[0]


BF16_TILE_ROWS = 16


def _half(o, slot, which):
    r2 = o.shape[1] // 2
    if r2 % BF16_TILE_ROWS == 0:
        return o.at[slot, pl.ds(which * r2, r2)]
    c2 = o.shape[2] // 2
    assert c2 % LANES == 0
    return o.at[slot, :, pl.ds(which * c2, c2)]


def _gather_send(o_refs, send, recv):
    x, y, z = _place()
    chip = 2 * x + y
    for a, o in enumerate(o_refs):
        r2 = o.shape[1] // 2
        mine = _half(o, chip, z)
        for k, (dx, dy) in enumerate(_CHIP_OFFSETS):
            pltpu.make_async_remote_copy(
                src_ref=mine, dst_ref=mine, send_sem=send.at[a, k], recv_sem=recv.at[a, k],
                device_id=(_flip(x, dx), _flip(y, dy), z), device_id_type=MESH).start()


def _gather_landed(o_refs, send, recv, then=None):
    x, y, z = _place()
    chip = 2 * x + y
    for a, o in enumerate(o_refs):
        for k, (dx, dy) in enumerate(_CHIP_OFFSETS):
            landed = _half(o, 2 * _flip(x, dx) + _flip(y, dy), z)
            pltpu.make_async_remote_copy(
                src_ref=landed, dst_ref=landed, send_sem=send.at[a, k], recv_sem=recv.at[a, k],
                device_id=(_flip(x, dx), _flip(y, dy), z), device_id_type=MESH).wait_recv()
            if then is not None:
                then(a, k, landed)
    for a, o in enumerate(o_refs):
        mine = _half(o, chip, z)
        for k, (dx, dy) in enumerate(_CHIP_OFFSETS):
            pltpu.make_async_remote_copy(
                src_ref=mine, dst_ref=mine, send_sem=send.at[a, k], recv_sem=recv.at[a, k],
                device_id=(_flip(x, dx), _flip(y, dy), z), device_id_type=MESH).wait_send()


def _pass_on(o_refs, fsend, frecv, a, k, landed):
    x, y, z = _place()
    pltpu.make_async_remote_copy(
        src_ref=landed, dst_ref=landed, send_sem=fsend.at[a, k], recv_sem=frecv.at[a, k],
        device_id=(x, y, 1 - z), device_id_type=MESH).start()


def _passed_on(o_refs, fsend, frecv):
    x, y, z = _place()
    for a, o in enumerate(o_refs):
        for k, (dx, dy) in enumerate(_CHIP_OFFSETS):
            other = 2 * _flip(x, dx) + _flip(y, dy)
            got = _half(o, other, 1 - z)
            gave = _half(o, other, z)
            pltpu.make_async_remote_copy(
                src_ref=got, dst_ref=got, send_sem=fsend.at[a, k], recv_sem=frecv.at[a, k],
                device_id=(x, y, 1 - z), device_id_type=MESH).wait_recv()
            pltpu.make_async_remote_copy(
                src_ref=gave, dst_ref=gave, send_sem=fsend.at[a, k], recv_sem=frecv.at[a, k],
                device_id=(x, y, 1 - z), device_id_type=MESH).wait_send()


def _gather_finish(o_refs, send, recv, fsend, frecv):
    _gather_landed(o_refs, send, recv, functools.partial(_pass_on, o_refs, fsend, frecv))
    _passed_on(o_refs, fsend, frecv)


class _Rider:
    def __init__(self, ins, out_shapes, sems, start, finish, aliases=None, in_specs=None, out_specs=None, middle=None):
        self.ins, self.out_shapes, self.sems = list(ins), list(out_shapes), list(sems)
        self.start, self.finish, self.aliases = start, finish, dict(aliases or {})
        self.middle = middle
        self.in_specs = list(in_specs) if in_specs else [_ANY] * len(self.ins)
        self.out_specs = list(out_specs) if out_specs else [_ANY] * len(self.out_shapes)


def _run_rider(rider, name):
    r_in, r_out = len(rider.ins), len(rider.out_shapes)

    def body(*refs):
        ins, outs, sems = refs[:r_in], refs[r_in:r_in + r_out], refs[r_in + r_out:]
        rider.start(ins, outs, sems)
        if rider.middle is not None:
            rider.middle(ins, outs, sems)
        rider.finish(ins, outs, sems)

    return pl.pallas_call(
        body, name=name, in_specs=rider.in_specs, out_specs=rider.out_specs, out_shape=rider.out_shapes,
        input_output_aliases=rider.aliases, scratch_shapes=rider.sems,
    )(*rider.ins)


def _hosted_call(body, args, *, name, grid, in_specs, out_specs, out_shape, scratch_shapes=(), sem, rider=None,
                 prefetch=()):
    scratch_shapes = list(scratch_shapes)
    n_pf, n_in, n_out, n_sc = len(prefetch), len(in_specs), len(out_specs), len(scratch_shapes)
    r_in, r_out = (len(rider.ins), len(rider.out_shapes)) if rider else (0, 0)
    last = tuple(g - 1 for g in grid)

    def hosted(*refs):
        p = 0
        parts = []
        for cnt in (n_pf, n_in, r_in, n_out, r_out, n_sc):
            parts.append(refs[p:p + cnt])
            p += cnt
        pf, ins, r_ins, outs, r_outs, scratch = parts
        sems = refs[p:]
        ids = [pl.program_id(a) for a in range(len(grid))]
        is_first = functools.reduce(jnp.logical_and, [i == 0 for i in ids])
        is_last = functools.reduce(jnp.logical_and, [i == e for i, e in zip(ids, last)])

        @pl.when(is_first)
        def _():
            rider.start(r_ins, r_outs, sems)

        if rider.middle is not None:
            linear = functools.reduce(lambda acc, ig: acc * ig[1] + ig[0], zip(ids, grid), 0)

            @pl.when(linear == math.prod(grid) * 3 // 4)
            def _():
                rider.middle(r_ins, r_outs, sems)

        body(*pf, *ins, *outs, *scratch)

        @pl.when(is_last)
        def _():
            rider.finish(r_ins, r_outs, sems)

    if rider is None:
        kern, all_in, all_out, shapes, scratch, aliases, extra = body, list(in_specs), list(out_specs), list(out_shape), \
            scratch_shapes, {}, []
    else:
        kern, all_in, all_out = hosted, list(in_specs) + rider.in_specs, list(out_specs) + rider.out_specs
        shapes, scratch, extra = list(out_shape) + rider.out_shapes, scratch_shapes + rider.sems, rider.ins
        aliases = {n_pf + n_in + i: n_out + j for i, j in rider.aliases.items()}
        sem = ("arbitrary",) * len(grid)
    if prefetch:
        spec = dict(grid_spec=pltpu.PrefetchScalarGridSpec(
            num_scalar_prefetch=n_pf, grid=grid, in_specs=all_in, out_specs=all_out, scratch_shapes=scratch))
    else:
        spec = dict(grid=grid, in_specs=all_in, out_specs=all_out, scratch_shapes=scratch)
    res = pl.pallas_call(kern, name=name, out_shape=shapes, input_output_aliases=aliases,
                         compiler_params=_params(sem, VMEM_LIMIT), **spec)(*prefetch, *args, *extra)
    return list(res[:n_out]), list(res[n_out:])


def _gather_rider(ws, staged=False):
    n = len(ws)
    shapes = [jax.ShapeDtypeStruct(w.shape, w.dtype) for w in ws]
    sems = [pltpu.SemaphoreType.DMA((n, 3))] * 4
    aliases = {a: a for a in range(n)}

    def start(ins, outs, s):
        _gather_send(outs, s[0], s[1])

    if not staged:
        return _Rider(ws, shapes, sems, start, lambda ins, outs, s: _gather_finish(outs, *s), aliases=aliases)
    return _Rider(
        ws, shapes, sems, start, lambda ins, outs, s: _passed_on(outs, s[2], s[3]), aliases=aliases,
        middle=lambda ins, outs, s: _gather_landed(outs, s[0], s[1], functools.partial(_pass_on, outs, s[2], s[3])))


def _gather_ici_rider(ws):
    n = len(ws)
    return _Rider(
        ws, [jax.ShapeDtypeStruct(w.shape, w.dtype) for w in ws], [pltpu.SemaphoreType.DMA((n, 3))] * 2,
        lambda ins, outs, sems: _gather_send(outs, sems[0], sems[1]),
        lambda ins, outs, sems: _gather_landed(outs, sems[0], sems[1]),
        aliases={a: a for a in range(n)})


def _gather_d2d_rider(ws):
    n = len(ws)

    def start(ins, outs, sems):
        x, y, z = _place()
        for a, o in enumerate(outs):
            for k, (dx, dy) in enumerate(_CHIP_OFFSETS):
                _pass_on(outs, sems[0], sems[1], a, k, _half(o, 2 * _flip(x, dx) + _flip(y, dy), z))

    return _Rider(
        ws, [jax.ShapeDtypeStruct(w.shape, w.dtype) for w in ws], [pltpu.SemaphoreType.DMA((n, 3))] * 2,
        start, lambda ins, outs, sems: _passed_on(outs, sems[0], sems[1]), aliases={a: a for a in range(n)})


def _copies_rider(ins, out_shapes, sem_shape, make):
    def start(r_ins, r_outs, sems):
        for cp in make(r_ins, r_outs, sems[0], sems[1]):
            cp.start()

    def finish(r_ins, r_outs, sems):
        for cp in make(r_ins, r_outs, sems[0], sems[1]):
            cp.wait()

    return _Rider(ins, out_shapes, [pltpu.SemaphoreType.DMA(sem_shape)] * 2, start, finish)


def _exchange_rider(gs):
    def make(g_refs, r_refs, send, recv):
        x, y, z = _place()
        return [pltpu.make_async_remote_copy(
            src_ref=g.at[:, pl.ds((1 - z) * (g.shape[1] // 2), g.shape[1] // 2)], dst_ref=r, send_sem=send.at[a],
            recv_sem=recv.at[a], device_id=(x, y, 1 - z), device_id_type=MESH)
            for a, (g, r) in enumerate(zip(g_refs, r_refs))]

    shapes = [jax.ShapeDtypeStruct((g.shape[0], g.shape[1] // 2, g.shape[2]), g.dtype) for g in gs]
    return _copies_rider(gs, shapes, (len(gs),), make)


def _add_half(g, recv, core, name):
    s, r, c = g.shape
    r2 = r // 2
    rb = r2
    for cand in (512, 256, 128, 64):
        if r2 % cand == 0:
            rb = cand
            break
    g4 = g.reshape(s, 2, r2, c)

    def body(core_ref, g_ref, r_ref, o_ref):
        o_ref[...] = (g_ref[...].astype(F32) + r_ref[...].astype(F32)).astype(BF16)

    return pl.pallas_call(
        body, name=name,
        grid_spec=pltpu.PrefetchScalarGridSpec(
            num_scalar_prefetch=1, grid=(s, r2 // rb),
            in_specs=[pl.BlockSpec((None, None, rb, c), lambda i, j, cr: (i, cr[0], j, 0)),
                      pl.BlockSpec((None, rb, c), lambda i, j, cr: (i, j, 0))],
            out_specs=pl.BlockSpec((None, rb, c), lambda i, j, cr: (i, j, 0))),
        out_shape=jax.ShapeDtypeStruct((s, r2, c), BF16),
        compiler_params=_params(("parallel", "parallel")),
    )(core, g4, recv)


def _scatter_rider(ps):
    def make(p_refs, o_refs, send, recv):
        x, y, z = _place()
        copies = []
        for a, (p, o) in enumerate(zip(p_refs, o_refs)):
            for k, (dx, dy) in enumerate(_CHIP_OFFSETS):
                other = 2 * _flip(x, dx) + _flip(y, dy)
                copies.append(pltpu.make_async_remote_copy(
                    src_ref=p.at[other], dst_ref=o.at[k], send_sem=send.at[a, k], recv_sem=recv.at[a, k],
                    device_id=(_flip(x, dx), _flip(y, dy), z), device_id_type=MESH))
        return copies

    shapes = [jax.ShapeDtypeStruct((3,) + p.shape[1:], p.dtype) for p in ps]
    return _copies_rider(ps, shapes, (len(ps), 3), make)


def _sum_chips(p, landed, chip, name):
    _, r2, c = p.shape
    rb = r2
    for cand in (256, 128, 64):
        if r2 % cand == 0:
            rb = cand
            break

    def body(s_ref, p_ref, l_ref, o_ref):
        acc = p_ref[...].astype(F32)
        for k in range(3):
            acc = acc + l_ref[k].astype(F32)
        o_ref[...] = acc

    return pl.pallas_call(
        body, name=name,
        grid_spec=pltpu.PrefetchScalarGridSpec(
            num_scalar_prefetch=1, grid=(r2 // rb,),
            in_specs=[pl.BlockSpec((None, rb, c), lambda i, s: (s[0], i, 0)),
                      pl.BlockSpec((3, rb, c), lambda i, s: (0, i, 0))],
            out_specs=pl.BlockSpec((rb, c), lambda i, s: (i, 0))),
        out_shape=jax.ShapeDtypeStruct((r2, c), F32),
        compiler_params=_params(("parallel",)),
    )(chip, p, landed)


def _swap_rider(hs):
    def make(h_refs, o_refs, send, recv):
        x, y, z = _place()
        return [pltpu.make_async_remote_copy(
            src_ref=h, dst_ref=o, send_sem=send.at[a], recv_sem=recv.at[a], device_id=(x, y, 1 - z),
            device_id_type=MESH) for a, (h, o) in enumerate(zip(h_refs, o_refs))]

    return _copies_rider(hs, [jax.ShapeDtypeStruct(h.shape, h.dtype) for h in hs], (len(hs),), make)


def _reduce_scatter_vmem(gs, rows, rider, name):
    n = len(gs)
    r_in, r_out = len(rider.ins), len(rider.out_shapes)
    halves = [(r // 2, g.shape[-1]) for g, (r, _) in zip(gs, rows)]
    piece_cols = 2 * LANES
    pieces = [(a, slice(c0, min(c0 + piece_cols, h[1]))) for a, h in enumerate(halves) for c0 in range(0, h[1], piece_cols)]
    n_p = len(pieces)

    def body(*refs):
        p = 0
        parts = []
        for cnt in (n, r_in, n, n, r_out, n, n, n, 6):
            parts.append(refs[p:p + cnt])
            p += cnt
        g_refs, r_ins, mine, theirs, r_outs, recv, part, land, sems = parts
        r_sems = refs[p:]
        xs, xr, ss, sr, ws, wr = sems
        x, y, z = _place()
        chip = 2 * x + y
        sib = (x, y, 1 - z)
        rider.start(r_ins, r_outs, r_sems)

        def half_of(a, s, which):
            r2 = halves[a][0]
            if len(g_refs[a].shape) == 3:
                return g_refs[a].at[s, pl.ds(pl.multiple_of(which * r2, 8), r2)]
            return g_refs[a].at[pl.ds(pl.multiple_of(s * rows[a][1] + which * r2, 8), r2)]

        def exchange(i):
            a, cols = pieces[i]
            return [pltpu.make_async_remote_copy(
                src_ref=half_of(a, s, 1 - z).at[:, cols], dst_ref=recv[a].at[s, :, cols], send_sem=xs.at[i, s],
                recv_sem=xr.at[i, s], device_id=sib, device_id_type=MESH) for s in range(N_CHIPS)]

        def scatter(i):
            a, cols = pieces[i]
            return [pltpu.make_async_remote_copy(
                src_ref=part[a].at[2 * _flip(x, dx) + _flip(y, dy), :, cols], dst_ref=land[a].at[k, :, cols],
                send_sem=ss.at[i, k], recv_sem=sr.at[i, k], device_id=(_flip(x, dx), _flip(y, dy), z),
                device_id_type=MESH) for k, (dx, dy) in enumerate(_CHIP_OFFSETS)]

        def swap(i):
            a, cols = pieces[i]
            return pltpu.make_async_remote_copy(
                src_ref=mine[a].at[:, cols], dst_ref=theirs[a].at[:, cols], send_sem=ws.at[i], recv_sem=wr.at[i],
                device_id=sib, device_id_type=MESH)

        for i in range(len(pieces)):
            for cp in exchange(i):
                cp.start()
        for i, (a, cols) in enumerate(pieces):
            for cp in exchange(i):
                cp.wait()
            for s in range(N_CHIPS):
                part[a][s, :, cols] = (half_of(a, s, z)[:, cols] + recv[a][s, :, cols]).astype(BF16)
            for cp in scatter(i):
                cp.start()
        for i, (a, cols) in enumerate(pieces):
            for cp in scatter(i):
                cp.wait()
            acc = part[a][chip, :, cols].astype(F32)
            for k in range(3):
                acc = acc + land[a][k, :, cols].astype(F32)
            mine[a][:, cols] = acc
            swap(i).start()
        for i in range(len(pieces)):
            swap(i).wait()
        rider.finish(r_ins, r_outs, r_sems)

    half_shapes = [jax.ShapeDtypeStruct(h, F32) for h in halves]
    res = pl.pallas_call(
        body, name=name, in_specs=[_VMEM] * n + rider.in_specs, out_specs=[_VMEM] * (2 * n) + rider.out_specs,
        out_shape=half_shapes + half_shapes + rider.out_shapes,
        scratch_shapes=[pltpu.VMEM((N_CHIPS,) + h, F32) for h in halves] + [pltpu.VMEM((N_CHIPS,) + h, BF16) for h in halves]
        + [pltpu.VMEM((3,) + h, BF16) for h in halves]
        + [pltpu.SemaphoreType.DMA((n_p, N_CHIPS))] * 2 + [pltpu.SemaphoreType.DMA((n_p, 3))] * 2
        + [pltpu.SemaphoreType.DMA((n_p,))] * 2 + rider.sems,
        input_output_aliases={n + i: 2 * n + j for i, j in rider.aliases.items()},
        compiler_params=_params(None, VMEM_LIMIT),
    )(*gs, *rider.ins)
    return list(res[:n]), list(res[n:2 * n]), list(res[2 * n:])


SMALL_ROWS = 32
PACK_ROWS = 16


def _pack_small(st_post, st_ret, st_pre):
    d = st_post.shape[2]

    def body(po_ref, re_ref, pr_ref, o_ref):
        o_ref[...] = jnp.zeros(o_ref.shape, F32)
        o_ref[0:1, :] = pr_ref[0, 2:3, :] + pr_ref[1, 2:3, :] + pr_ref[2, 2:3, :]
        o_ref[1:2, :] = po_ref[0, 4:5, :] + po_ref[1, 4:5, :]
        o_ref[2:3, :] = po_ref[0, 5:6, :] + po_ref[1, 5:6, :]
        o_ref[3:4, 0:512] = re_ref[0, 0:1, :] + re_ref[1, 0:1, :]
        o_ref[4:5, :] = pr_ref[0, 3:4, :] + pr_ref[1, 3:4, :] + pr_ref[2, 3:4, :]
        o_ref[5:6, :] = pr_ref[0, 4:5, :] + pr_ref[1, 4:5, :] + pr_ref[2, 4:5, :]
        lane = lax.broadcasted_iota(jnp.int32, (1, LANES), 1)
        for row, src in ((6, 1), (10, 2)):
            acc = jnp.zeros((1, LANES), F32)
            for hd in range(HEADS):
                grp = re_ref[0, src:src + 1, hd * LANES:(hd + 1) * LANES] + re_ref[1, src:src + 1, hd * LANES:(hd + 1) * LANES]
                acc = acc + jnp.where(lane == hd, grp, 0.0)
            o_ref[row:row + 1, 0:LANES] = acc
        o_ref[7:8, :] = po_ref[0, 6:7, :] + po_ref[1, 6:7, :]
        o_ref[8:9, :] = pr_ref[2, 0:1, :]
        o_ref[9:10, :] = pr_ref[2, 1:2, :]
        for e in range(2):
            b = 12 + 6 * e
            o_ref[b:b + 1, :] = pr_ref[e, 0:1, :]
            o_ref[b + 1:b + 2, :] = pr_ref[e, 1:2, :]
            o_ref[b + 2:b + 3, :] = po_ref[e, 3:4, :]
            o_ref[b + 3:b + 4, :] = po_ref[e, 0:1, :]
            o_ref[b + 4:b + 5, :] = po_ref[e, 1:2, :]
            o_ref[b + 5:b + 6, :] = po_ref[e, 2:3, :]

    return pl.pallas_call(body, name="pack_small", out_shape=jax.ShapeDtypeStruct((SMALL_ROWS, d), F32))(st_post, st_ret, st_pre)


def _small_reduce(gathered):
    d = gathered.shape[2]

    def body(g_ref, o_ref):
        tot = g_ref[0, 0:PACK_ROWS, :]
        for dev in range(1, N_DEV):
            tot = tot + g_ref[dev, 0:PACK_ROWS, :]
        o_ref[0:PACK_ROWS, :] = tot
        for j in range(6):
            acc = g_ref[0, 12 + j:13 + j, :] + g_ref[0, 18 + j:19 + j, :]
            for dev in range(1, N_DEV):
                acc = acc + g_ref[dev, 12 + j:13 + j, :] + g_ref[dev, 18 + j:19 + j, :]
            if j < 2:
                acc = acc + o_ref[8 + j:9 + j, :]
            o_ref[PACK_ROWS + j:PACK_ROWS + j + 1, :] = acc
        o_ref[PACK_ROWS + 6:PACK_ROWS + 8, :] = jnp.zeros((2, d), F32)

    return pl.pallas_call(body, name="small_reduce", out_shape=jax.ShapeDtypeStruct((PACK_ROWS + 8, d), F32))(gathered)


_SMALL = (("g_attn", 0, 1024), ("g_ffn", 1, 1024), ("g_final", 2, 1024), ("g_ret", 3, 512), ("g_q_lora", 4, 384),
          ("g_kv_lora", 5, 256), ("ret_decay_fwd", 6, HEADS), ("ret_decay_bwd", 10, HEADS))
_SMALL_NAMES = tuple(s[0] for s in _SMALL) + ("c_ctx", "b_ada")


def _small_final(tot, dcc, sg8, ws, ms, vs):
    d = tot.shape[1]
    n = len(_SMALL_NAMES)

    def body(*refs):
        t_ref, dcc_ref, sg_ref = refs[0:3]
        w_refs, m_refs, v_refs = refs[3:3 + n], refs[3 + n:3 + 2 * n], refs[3 + 2 * n:3 + 3 * n]
        outs = refs[3 + 3 * n:]
        g_refs, d_refs, mo_refs, vo_refs = outs[0:n], outs[n:2 * n], outs[2 * n:3 * n], outs[3 * n:4 * n]
        l_ref = outs[4 * n]

        def update(i, g, sl=None):
            pick = (lambda r: r[...]) if sl is None else (lambda r: r[:, sl])
            dl, mn, vn = _adam_math(pick(w_refs[i]), g, pick(m_refs[i]), pick(v_refs[i]))
            if sl is None:
                g_refs[i][...], d_refs[i][...], mo_refs[i][...], vo_refs[i][...] = g, dl, mn, vn
            else:
                g_refs[i][:, sl], d_refs[i][:, sl], mo_refs[i][:, sl], vo_refs[i][:, sl] = g, dl, mn, vn

        for i, (name, row, width) in enumerate(_SMALL):
            g = t_ref[row:row + 1, 0:width]
            if name == "ret_decay_fwd":
                g = g * sg_ref[0:1, 0:width]
            elif name == "ret_decay_bwd":
                g = g * sg_ref[1:2, 0:width]
            update(i, g)
        i_cc, i_b = n - 2, n - 1
        cc = w_refs[i_cc][...]
        s = 1.0 / (1.0 + jnp.exp(-cc))
        dsilu = dcc_ref[0, 0:1, :] + dcc_ref[2, 0:1, :] + dcc_ref[4, 0:1, :] + dcc_ref[6, 0:1, :]
        update(i_cc, dsilu * (s * (1.0 + cc * (1.0 - s))))
        for j in range(6):
            update(i_b, t_ref[PACK_ROWS + j:PACK_ROWS + j + 1, :], pl.ds(j * d, d))
        l_ref[...] = jnp.broadcast_to((0.5 / d) * jnp.sum(t_ref[7:8, :], keepdims=True), l_ref.shape)

    shapes = [jax.ShapeDtypeStruct(a.shape, F32) for a in ws]
    outs = pl.pallas_call(
        body, name="small_final", out_shape=shapes * 4 + [jax.ShapeDtypeStruct((8, LANES), F32)],
    )(tot, dcc, sg8, *ws, *ms, *vs)
    return outs[0:n], outs[n:2 * n], outs[2 * n:3 * n], outs[3 * n:4 * n], outs[4 * n]


_WEIGHTS = ("c_ctx", "w_ada", "b_ada", "g_attn", "g_ffn", "w_in", "ret_decay_fwd", "ret_decay_bwd", "g_ret", "g_q_lora",
            "w_uq", "g_kv_lora", "w_ukv", "w_out", "w_ff1", "w_ff2", "g_final")
_BIG = ("w_in", "w_uq", "w_ukv", "w_out", "w_ff1", "w_ff2")
_TRANSPOSED = ("w_in", "w_uq")


def kernel(x, c, ctx, c_ctx, w_ada, b_ada, g_attn, g_ffn, w_in, ret_decay_fwd, ret_decay_bwd, g_ret, g_q_lora, w_uq, g_kv_lora, w_ukv, w_out, w_ff1, w_ff2, g_final, loss_target, m_c_ctx, m_w_ada, m_b_ada, m_g_attn, m_g_ffn, m_w_in, m_ret_decay_fwd, m_ret_decay_bwd, m_g_ret, m_g_q_lora, m_w_uq, m_g_kv_lora, m_w_ukv, m_w_out, m_w_ff1, m_w_ff2, m_g_final, v_c_ctx, v_w_ada, v_b_ada, v_g_attn, v_g_ffn, v_w_in, v_ret_decay_fwd, v_ret_decay_bwd, v_g_ret, v_g_q_lora, v_w_uq, v_g_kv_lora, v_w_ukv, v_w_out, v_w_ff1, v_w_ff2, v_g_final):
    w = dict(c_ctx=c_ctx, w_ada=w_ada, b_ada=b_ada, g_attn=g_attn, g_ffn=g_ffn, w_in=w_in, ret_decay_fwd=ret_decay_fwd,
             ret_decay_bwd=ret_decay_bwd, g_ret=g_ret, g_q_lora=g_q_lora, w_uq=w_uq, g_kv_lora=g_kv_lora, w_ukv=w_ukv,
             w_out=w_out, w_ff1=w_ff1, w_ff2=w_ff2, g_final=g_final)
    m = dict(c_ctx=m_c_ctx, w_ada=m_w_ada, b_ada=m_b_ada, g_attn=m_g_attn, g_ffn=m_g_ffn, w_in=m_w_in,
             ret_decay_fwd=m_ret_decay_fwd, ret_decay_bwd=m_ret_decay_bwd, g_ret=m_g_ret, g_q_lora=m_g_q_lora, w_uq=m_w_uq,
             g_kv_lora=m_g_kv_lora, w_ukv=m_w_ukv, w_out=m_w_out, w_ff1=m_w_ff1, w_ff2=m_w_ff2, g_final=m_g_final)
    v = dict(c_ctx=v_c_ctx, w_ada=v_w_ada, b_ada=v_b_ada, g_attn=v_g_attn, g_ffn=v_g_ffn, w_in=v_w_in,
             ret_decay_fwd=v_ret_decay_fwd, ret_decay_bwd=v_ret_decay_bwd, g_ret=v_g_ret, g_q_lora=v_g_q_lora, w_uq=v_w_uq,
             g_kv_lora=v_g_kv_lora, w_ukv=v_w_ukv, w_out=v_w_out, w_ff1=v_w_ff1, w_ff2=v_w_ff2, g_final=v_g_final)
    xi, yi, ci = lax.axis_index("x"), lax.axis_index("y"), lax.axis_index("c")
    chip = 2 * xi + yi
    dev = 2 * chip + ci
    nex, seq, d = x.shape
    n_ada = w_ada.shape[2]

    dec = jnp.zeros((8, LANES), F32).at[0, :HEADS].set(ret_decay_fwd[0]).at[1, :HEADS].set(ret_decay_bwd[0])
    lg8, sg8 = _decay_prep(dec)
    lg = lg8[:2, :HEADS]

    def shard_of(t, k):
        return t[k][0].T if k in _TRANSPOSED else t[k][0]

    shard = {k: shard_of(w, k) for k in _BIG}
    head_rows = MLA_NOPE + MLA_ROPE
    shard["w_uq"] = jnp.pad(shard["w_uq"], ((0, MLA_HEAD - head_rows), (0, 0)))
    slot = chip.reshape(1).astype(jnp.int32)
    core = ci.reshape(1).astype(jnp.int32)
    slots = {k: _cast_into_slot(shard[k], slot, "cast_" + k) for k in _EARLY}
    half_ff = shard["w_ff2"].shape[0] // 2
    late_pieces = [(shard["w_out"], 0, shard["w_out"].shape[0]), (shard["w_ff1"], 0, shard["w_ff1"].shape[0]),
                   (shard["w_ff2"], 0, half_ff), (shard["w_ff2"], half_ff, half_ff)]
    late_slots, (w_in_f, w_uq_k, w_ukv_k, c8) = _cast_into_slots(
        late_pieces, slot, "cast_late",
        rider=_merge_riders(_gather_rider([slots[k] for k in _EARLY]),
                            _gather8_rider(jnp.pad(c, ((0, 8 - nex), (0, 0))), in_vmem=False)))

    a_in = jnp.concatenate([c8[:, :nex].reshape(N_DEV * nex, d), c_ctx.reshape(1, d), jnp.zeros((7, d), F32)], axis=0)
    b_sh = lax.dynamic_slice(b_ada, (0, chip * n_ada), (1, n_ada))
    mod4 = _run_rider(_gather_rider([_mod_fwd(a_in, w_ada[0], b_sh, slot)]), "ag_mod")[0]
    w_in_k = jnp.pad(w_in_f.reshape(IN_COLS, d), ((0, IN_PAD - IN_COLS), (0, 0)))
    mod_all = mod4.transpose(1, 0, 2).reshape(a_in.shape[0], N_CHIPS * n_ada)
    mod_me = lax.dynamic_slice(mod_all, (nex * dev, 0), (nex, N_CHIPS * n_ada)).reshape(nex, 6, d)
    mod_c = mod_all[N_DEV * nex].reshape(1, 6, d)
    modv = jnp.pad(jnp.concatenate([mod_me, mod_c], axis=0), ((0, 0), (0, 2), (0, 0)))

    gx, g_early, late, st_post, st_ret, st_pre = _local_step(
        x, ctx, loss_target, modv, lg, g_attn, g_ffn, g_final.reshape(1, d), g_ret, g_q_lora, g_kv_lora,
        w_in_k, w_uq_k, w_ukv_k, late_slots, (core, slot))

    mine, theirs, (*late_theirs, gathered) = _reduce_scatter_vmem(
        g_early, [(IN_COLS // N_CHIPS, IN_COLS // N_CHIPS), (head_rows, MLA_HEAD), (KV_LORA, KV_LORA)],
        _merge_riders(_swap_rider(late), _gather8_rider(_pack_small(st_post, st_ret, st_pre))), "rs_early")
    tot = _small_reduce(gathered)
    dm = jnp.concatenate([
        gathered[:, 12:24].reshape(N_DEV * nex, 6 * d),
        jnp.concatenate([tot[8:10].reshape(1, 2 * d), jnp.zeros((1, 4 * d), F32)], axis=1),
        jnp.zeros((7, 6 * d), F32)], axis=0)
    dm_sh = lax.dynamic_slice(dm, (0, chip * n_ada), (dm.shape[0], n_ada))
    g_ada, da = _mod_bwd(a_in, dm_sh, w_ada[0])
    dcc = _allgather8(da[N_DEV * nex:], "ag_dcc")
    halves = dict(zip(_EARLY, zip(mine, theirs)))
    halves.update(zip(_LATE, zip(late, late_theirs)))
    grad, delta, new_m, new_v = {}, {}, {}, {}
    for k in _BIG:
        a, b = halves[k]
        res = _adamw_halves(shard_of(w, k), a, b, shard_of(m, k), shard_of(v, k), core, "adamw_" + k)
        grad[k], delta[k], new_m[k], new_v[k] = [(o.T if k in _TRANSPOSED else o).reshape(w[k].shape) for o in res]

    shp = w_ada.shape
    outs, _ = _adamw(w_ada[0], g_ada, m["w_ada"][0], v["w_ada"][0], "adamw_w_ada")
    grad["w_ada"] = g_ada.reshape(shp)
    delta["w_ada"], new_m["w_ada"], new_v["w_ada"] = [o.reshape(shp) for o in outs]
    rows = [{k: t[k].reshape(1, -1) for k in _SMALL_NAMES} for t in (w, m, v)]
    small = _small_final(tot, dcc, sg8, *[[t[k] for k in _SMALL_NAMES] for t in rows])
    for res, outs in zip((grad, delta, new_m, new_v), small[:4]):
        for k, o in zip(_SMALL_NAMES, outs):
            res[k] = o.reshape(w[k].shape)
    return (small[4][0, 0], gx, *[grad[k] for k in _WEIGHTS], *[delta[k] for k in _WEIGHTS],
            *[new_m[k] for k in _WEIGHTS], *[new_v[k] for k in _WEIGHTS])
```

```python
import functools
import math

import jax
import jax.numpy as jnp
from jax import lax
from jax.experimental import pallas as pl
from jax.experimental.pallas import tpu as pltpu

F32 = jnp.float32
BF16 = jnp.bfloat16
MESH = pl.DeviceIdType.MESH

EPS = 1e-6
D_MODEL = 1024
D_FF = 4096
HEADS = 4
RET_DK = 64
RET_DV = 128
MLA_NOPE = 128
MLA_ROPE = 64
MLA_HEAD = 256
Q_LORA = 384
KV_LORA = 256
GRID_W = 64
ROPE_BASE = 10000.0
IN_COLS = 2240
IN_PAD = 2304
PG_COLS = 1152
N_CHIPS = 4
N_DEV = 8
LANES = 128
ADAM_LR = 0.001
ADAM_B1 = 0.9
ADAM_B2 = 0.999
ADAM_EPS = 1e-08
ADAM_WD = 0.01
ADAM_STEP = 10
VMEM_LIMIT = 56 * 1024 * 1024


def _dot(a, b):
    return jnp.dot(a, b, preferred_element_type=F32)


def _dot_nt(a, b):
    return lax.dot_general(a, b, (((1,), (1,)), ((), ())), preferred_element_type=F32)


def _dot_tn(a, b):
    return lax.dot_general(a, b, (((0,), (0,)), ((), ())), preferred_element_type=F32)


def _params(sem=None, vmem=None):
    return pltpu.CompilerParams(dimension_semantics=sem, vmem_limit_bytes=vmem)


def _full(shape):
    n = len(shape)
    return pl.BlockSpec(shape, lambda *_: (0,) * n)


def _once(shape):
    n = len(shape)
    return pl.BlockSpec(shape, lambda *_: (0,) * n, pipeline_mode=pl.Buffered(1))


def _rope(x, cos, sin):
    w = x.shape[-1]
    lo = (lax.broadcasted_iota(jnp.int32, (1, w), 1) % 64) < 32
    swapped = jnp.where(lo, pltpu.roll(x, w - 32, 1), pltpu.roll(x, 32, 1))
    return x * cos + swapped * sin


def _rope_t(g, cos, sin):
    w = g.shape[-1]
    lo = (lax.broadcasted_iota(jnp.int32, (1, w), 1) % 64) < 32
    t = g * sin
    swapped = jnp.where(lo, pltpu.roll(t, w - 32, 1), pltpu.roll(t, 32, 1))
    return g * cos + swapped


def _rope_tables(seq, tm):
    rows = seq // GRID_W
    row = jnp.repeat(jnp.arange(rows, dtype=F32), GRID_W)
    col = jnp.tile(jnp.arange(GRID_W, dtype=F32), rows)
    n_freq = RET_DK // 4
    freq = ROPE_BASE ** (-jnp.arange(n_freq, dtype=F32) / n_freq)
    ang = jnp.concatenate([row[:, None] * freq, col[:, None] * freq], axis=-1)
    cos, sin = jnp.cos(ang), jnp.sin(ang)
    cos_t = jnp.tile(jnp.concatenate([cos, cos], -1), (1, HEADS))
    sin_t = jnp.tile(jnp.concatenate([-sin, sin], -1), (1, HEADS))
    cos_t = jnp.concatenate([cos_t, jnp.ones((tm, 4 * RET_DK), F32)], 0)
    sin_t = jnp.concatenate([sin_t, jnp.zeros((tm, 4 * RET_DK), F32)], 0)
    return cos_t, sin_t


def _adam_math(w, g, m, v):
    mn = ADAM_B1 * m + (1.0 - ADAM_B1) * g
    vn = ADAM_B2 * v + (1.0 - ADAM_B2) * (g * g)
    m_hat = mn / (1.0 - ADAM_B1 ** ADAM_STEP)
    v_hat = vn / (1.0 - ADAM_B2 ** ADAM_STEP)
    return -ADAM_LR * (m_hat / (jnp.sqrt(v_hat) + ADAM_EPS) + ADAM_WD * w), mn, vn


def _cast_into_slots(pieces, slot, name, rider=None):
    c = pieces[0][0].shape[1]
    rb = max(b for b in range(16, 1025, 16) if b * c * 4 <= (5 << 19)
             and all(cnt % b == 0 and st % b == 0 for _, st, cnt in pieces))
    nbs = [cnt // rb for _, _, cnt in pieces]
    starts = [sum(nbs[:s]) for s in range(len(pieces))]

    def body(s_ref, *refs):
        i = pl.program_id(0)
        for s in range(len(pieces)):
            @pl.when(jnp.logical_and(i >= starts[s], i < starts[s] + nbs[s]))
            def _():
                refs[len(pieces) + s][...] = refs[s][...].astype(BF16)

    in_specs, out_specs = [], []
    for (_, first_row, _), nb, st in zip(pieces, nbs, starts):
        in_specs.append(pl.BlockSpec((rb, c), lambda i, s, nb=nb, st=st, f=first_row // rb: (f + jnp.clip(i - st, 0, nb - 1), 0)))
        out_specs.append(pl.BlockSpec((None, rb, c), lambda i, s, nb=nb, st=st: (s[0], jnp.clip(i - st, 0, nb - 1), 0)))
    return _hosted_call(
        body, [w for w, _, _ in pieces], name=name, grid=(sum(nbs),), prefetch=(slot,), in_specs=in_specs,
        out_specs=out_specs, out_shape=[jax.ShapeDtypeStruct((N_CHIPS, cnt, c), BF16) for _, _, cnt in pieces],
        sem=("arbitrary",), rider=rider)


def _cast_into_slot(w, slot, name):
    return _cast_into_slots([(w, 0, w.shape[0])], slot, name)[0][0]


def _adamw_halves(w, mine, theirs, m, v, core, name):
    r, c = w.shape
    r2 = r // 2
    rb = max(b for b in range(8, r2 + 1, 8) if r2 % b == 0 and b * c * 4 <= (1 << 21))
    nbh = r2 // rb

    def body(z_ref, w_ref, a_ref, b_ref, m_ref, v_ref, g_ref, d_ref, mo_ref, vo_ref):
        here = (pl.program_id(0) // nbh) == z_ref[0]
        gg = jnp.where(here, a_ref[...], b_ref[...])
        g_ref[...] = gg
        d_ref[...], mo_ref[...], vo_ref[...] = _adam_math(w_ref[...], gg, m_ref[...], v_ref[...])

    spec = pl.BlockSpec((rb, c), lambda i, z: (i, 0))
    a_spec = pl.BlockSpec((rb, c), lambda i, z: (jnp.clip(i - z[0] * nbh, 0, nbh - 1), 0))
    b_spec = pl.BlockSpec((rb, c), lambda i, z: (jnp.clip(i - (1 - z[0]) * nbh, 0, nbh - 1), 0))
    shp = jax.ShapeDtypeStruct((r, c), F32)
    return pl.pallas_call(
        body, name=name,
        grid_spec=pltpu.PrefetchScalarGridSpec(
            num_scalar_prefetch=1, grid=(r // rb,), in_specs=[spec, a_spec, b_spec, spec, spec], out_specs=[spec] * 4),
        out_shape=[shp] * 4,
        compiler_params=_params(("parallel",)),
    )(core, w, mine, theirs, m, v)


def _decay_prep(dec):
    def body(d_ref, lg_ref, sg_ref):
        d = d_ref[...]
        lg_ref[...] = jnp.minimum(d, 0.0) - jnp.log(1.0 + jnp.exp(-jnp.abs(d)))
        sg_ref[...] = 1.0 / (1.0 + jnp.exp(d))

    shp = jax.ShapeDtypeStruct(dec.shape, F32)
    return pl.pallas_call(body, name="decay_prep", out_shape=[shp, shp])(dec)


def _mod_fwd(a_in, w_ada, b_sh, slot):
    rows, d = a_in.shape
    n = w_ada.shape[1]
    bn = 512

    def body(s_ref, a_ref, w_ref, b_ref, o_ref):
        a = a_ref[...]
        s = (a / (1.0 + jnp.exp(-a))).astype(BF16)
        o_ref[...] = _dot(s, w_ref[...].astype(BF16)) + b_ref[...]

    (out,), _ = _hosted_call(
        body, (a_in, w_ada, b_sh), name="mod_fwd", grid=(n // bn,), prefetch=(slot,),
        in_specs=[pl.BlockSpec((rows, d), lambda j, s: (0, 0)), pl.BlockSpec((d, bn), lambda j, s: (0, j)),
                  pl.BlockSpec((1, bn), lambda j, s: (0, j))],
        out_specs=[pl.BlockSpec((None, rows, bn), lambda j, s: (s[0], 0, j))],
        out_shape=[jax.ShapeDtypeStruct((N_CHIPS, rows, n), F32)], sem=("parallel",))
    return out


def _mod_bwd_adamw(a_in, dm, w_ada, m, v):
    rows, d = a_in.shape
    n = w_ada.shape[1]
    bn = 3 * LANES
    nb = n // bn

    def body(a_ref, dm_ref, w_ref, m_ref, v_ref, gw_ref, da_ref, d_ref, mo_ref, vo_ref):
        j = pl.program_id(0)
        a = a_ref[...]
        s = (a / (1.0 + jnp.exp(-a))).astype(BF16)
        dmb = dm_ref[...].astype(BF16)
        w = w_ref[...]
        g = _dot_tn(s, dmb)
        gw_ref[...] = g
        d_ref[...], mo_ref[...], vo_ref[...] = _adam_math(w, g, m_ref[...], v_ref[...])
        part = _dot_nt(dmb, w.astype(BF16))

        @pl.when(j == 0)
        def _():
            da_ref[...] = part

        @pl.when(j > 0)
        def _():
            da_ref[...] += part

    blk = pl.BlockSpec((d, bn), lambda j: (0, j))
    shp = jax.ShapeDtypeStruct((d, n), F32)
    return pl.pallas_call(
        body, name="mod_bwd_adamw", grid=(nb,),
        in_specs=[_full((rows, d)), pl.BlockSpec((rows, bn), lambda j: (0, j)), blk, blk, blk],
        out_specs=[blk, _full((rows, d)), blk, blk, blk],
        out_shape=[shp, jax.ShapeDtypeStruct((rows, d), F32), shp, shp, shp],
        compiler_params=_params(("arbitrary",), vmem=3 * 7 * d * bn * 4),
    )(a_in, dm, w_ada, m, v)


def _pre_fwd(x2, ctx2, modv, g_attn, w_in, g_q, g_kv, w_uq, w_ukv, cos_t, sin_t, *, seq, tm, rider=None):
    t_lat, d = x2.shape
    t_ctx = ctx2.shape[0]
    nl, nc = t_lat // tm, t_ctx // tm
    n_all = t_lat + t_ctx
    tpe = seq // tm
    nex = t_lat // seq

    def body(x_ref, c_ref, mod_ref, g_ref, win_ref, gq_ref, gkv_ref, wuq_ref, wukv_ref, cos_ref, sin_ref,
             h_ref, pg_ref, rq_ref, rk_ref, rv_ref, nq_ref, nkv_ref, q_ref, k_ref, v_ref):
        i = pl.program_id(0)
        xt = jnp.where(i < nl, x_ref[...], c_ref[...])
        sh = mod_ref[0, 0:1, :]
        sc = mod_ref[0, 1:2, :]
        r = lax.rsqrt(jnp.mean(xt * xt, axis=-1, keepdims=True) + EPS)
        hb = ((xt * r) * g_ref[...] * (1.0 + sc) + sh).astype(BF16)
        h_ref[...] = hb
        p = _dot_nt(hb, win_ref[...])
        cos = cos_ref[...]
        sin = sin_ref[...]
        rq_ref[...] = _rope(p[:, 0:256], cos, sin).astype(BF16)
        rk_ref[...] = _rope(p[:, 256:512] * (RET_DK ** -0.5), cos, sin).astype(BF16)
        rv_ref[...] = p[:, 512:1024].astype(BF16)
        pg_ref[...] = p[:, 1024:2176]
        cq = p[:, 1536:1920]
        ckv = p[:, 1920:2176]
        nqb = (cq * lax.rsqrt(jnp.mean(cq * cq, axis=-1, keepdims=True) + EPS) * gq_ref[...]).astype(BF16)
        nkvb = (ckv * lax.rsqrt(jnp.mean(ckv * ckv, axis=-1, keepdims=True) + EPS) * gkv_ref[...]).astype(BF16)
        nq_ref[...] = nqb
        nkv_ref[...] = nkvb
        cos1 = cos[:, 0:LANES]
        sin1 = sin[:, 0:LANES]
        kpe = _rope(p[:, 2176:2304], cos1, sin1).astype(BF16)
        for hd in range(HEADS):
            o = hd * MLA_HEAD
            qh = _dot_nt(nqb, wuq_ref[hd]) * MLA_SCALE
            q_ref[:, o:o + 128] = qh[:, 0:128].astype(BF16)
            q_ref[:, o + 128:o + 256] = _rope(qh[:, 128:256], cos1, sin1).astype(BF16)
            kvh = _dot(nkvb, wukv_ref[hd])
            k_ref[:, o:o + 128] = kvh[:, 0:128].astype(BF16)
            k_ref[:, o + 128:o + 256] = kpe
            v_ref[:, hd * 128:(hd + 1) * 128] = kvh[:, 128:256].astype(BF16)

    def tile(width):
        return pl.BlockSpec((tm, width), lambda i: (i, 0))

    widths = (d, PG_COLS, 256, 256, 512, Q_LORA, KV_LORA, HEADS * MLA_HEAD, HEADS * MLA_HEAD, HEADS * 128)
    dtypes = (BF16, F32, BF16, BF16, BF16, BF16, BF16, BF16, BF16, BF16)
    tab = pl.BlockSpec((tm, 256), lambda i: (jnp.where(i < nl, i % tpe, tpe), 0))
    return _hosted_call(
        body, (x2, ctx2, modv, g_attn, w_in, g_q, g_kv, w_uq, w_ukv, cos_t, sin_t), name="pre_fwd", grid=(nl + nc,),
        in_specs=[
            pl.BlockSpec((tm, d), lambda i: (jnp.minimum(i, nl - 1), 0)),
            pl.BlockSpec((tm, d), lambda i: (jnp.maximum(i - nl, 0), 0)),
            pl.BlockSpec((1, 8, d), lambda i: (jnp.minimum(i // tpe, nex), 0, 0)),
            _full((1, d)), _full(w_in.shape), _full((1, Q_LORA)), _full((1, KV_LORA)),
            _full(w_uq.shape), _full(w_ukv.shape), tab, tab,
        ],
        out_specs=[tile(w) for w in widths],
        out_shape=[jax.ShapeDtypeStruct((n_all, w), dt) for w, dt in zip(widths, dtypes)],
        sem=("parallel",), rider=rider)


def _post(yret, ymla, x2, tgt2, modv, g_ffn, g_fin, w_out, w_ff1, w_ff2a, w_ff2b, *, seq, tm):
    t_lat, d = x2.shape
    nl = t_lat // tm
    tpe = seq // tm
    nex = t_lat // seq
    n_slab = w_ff1.shape[0]
    fs = w_ff1.shape[2]
    fh = w_ff2a.shape[1]

    def body(yr_ref, ym_ref, x_ref, t_ref, mod_ref, gf_ref, gl_ref, wo_ref, w1_ref, w2a_ref, w2b_ref,
             mix_ref, a_ref, du_ref, h2_ref, df_ref, dmo_ref, dmix_ref, dxm_ref, st_ref, ru_ref):
        i = pl.program_id(0)
        gt_a = mod_ref[0, 2:3, :]
        sh_f = mod_ref[0, 3:4, :]
        sc_f = mod_ref[0, 4:5, :]
        gt_f = mod_ref[0, 5:6, :]
        g_ffn_v = gf_ref[...]
        g_fin_v = gl_ref[...]
        yr = yr_ref[...]
        ym = ym_ref[...]
        mix_ref[:, 0:512] = yr
        mix_ref[:, 512:1024] = ym
        op = _dot(yr, wo_ref[0:512, :]) + _dot(ym, wo_ref[512:1024, :])
        x_mid = x_ref[...] + gt_a * op
        r2 = lax.rsqrt(jnp.mean(x_mid * x_mid, axis=-1, keepdims=True) + EPS)
        xh2 = x_mid * r2
        h2b = (xh2 * g_ffn_v * (1.0 + sc_f) + sh_f).astype(BF16)
        h2_ref[...] = h2b
        f = jnp.zeros((tm, d), F32)
        for s in range(n_slab):
            ru = jnp.maximum(_dot(h2b, w1_ref[s]), 0.0)
            ru_ref[:, s * fs:(s + 1) * fs] = ru
            ab = (ru * ru).astype(BF16)
            a_ref[:, s * fs:(s + 1) * fs] = ab
            f = f + _dot(ab[:, 0:fh], w2a_ref[s]) + _dot(ab[:, fh:fs], w2b_ref[s])
        x_out = x_mid + gt_f * f
        r3 = lax.rsqrt(jnp.mean(x_out * x_out, axis=-1, keepdims=True) + EPS)
        xh3 = x_out * r3
        err = xh3 * g_fin_v - t_ref[...]
        dy = err * (1.0 / d)
        dxh3 = dy * g_fin_v
        dx_out = r3 * (dxh3 - xh3 * jnp.mean(dxh3 * xh3, axis=-1, keepdims=True))
        dfb = (dx_out * gt_f).astype(BF16)
        df_ref[...] = dfb
        dh2 = jnp.zeros((tm, d), F32)
        for s in range(n_slab):
            da = jnp.concatenate([_dot_nt(dfb, w2a_ref[s]), _dot_nt(dfb, w2b_ref[s])], axis=1)
            dub = (da * (2.0 * ru_ref[:, s * fs:(s + 1) * fs])).astype(BF16)
            du_ref[:, s * fs:(s + 1) * fs] = dub
            dh2 = dh2 + _dot_nt(dub, w1_ref[s])
        dxh2 = dh2 * (1.0 + sc_f) * g_ffn_v
        dx_mid = dx_out + r2 * (dxh2 - xh2 * jnp.mean(dxh2 * xh2, axis=-1, keepdims=True))
        dxm_ref[...] = dx_mid
        dmob = (dx_mid * gt_a).astype(BF16)
        dmo_ref[...] = dmob
        dmix_ref[...] = _dot_nt(dmob, wo_ref[...]).astype(BF16)

        def rsum(v):
            return jnp.sum(v, axis=0, keepdims=True)

        stats = jnp.concatenate([
            rsum(dh2), rsum(dh2 * xh2 * g_ffn_v), rsum(dx_out * f), rsum(dx_mid * op),
            rsum(dh2 * (1.0 + sc_f) * xh2), rsum(dy * xh3), rsum(err * err), jnp.zeros((1, d), F32)], axis=0)

        @pl.when(i % tpe == 0)
        def _():
            st_ref[0] = stats

        @pl.when(i % tpe != 0)
        def _():
            st_ref[0] += stats

    def tile(width):
        return pl.BlockSpec((tm, width), lambda i: (i, 0))

    widths = (d, D_FF, D_FF, d, d, d, d, d)
    dtypes = (BF16, BF16, BF16, BF16, BF16, BF16, BF16, F32)
    const = pl.Buffered(1)
    return pl.pallas_call(
        body, name="post", grid=(nl,),
        in_specs=[
            tile(512), tile(512), tile(d), tile(d),
            pl.BlockSpec((1, 8, d), lambda i: (i // tpe, 0, 0)),
            _full((1, d)), _full((1, d)),
            pl.BlockSpec(w_out.shape, lambda i: (0, 0), pipeline_mode=const),
            pl.BlockSpec(w_ff1.shape, lambda i: (0, 0, 0), pipeline_mode=const),
            pl.BlockSpec(w_ff2a.shape, lambda i: (0, 0, 0), pipeline_mode=const),
            pl.BlockSpec(w_ff2b.shape, lambda i: (0, 0, 0), pipeline_mode=const),
        ],
        out_specs=[tile(w) for w in widths] + [pl.BlockSpec((1, 8, d), lambda i: (i // tpe, 0, 0))],
        out_shape=[jax.ShapeDtypeStruct((t_lat, w), dt) for w, dt in zip(widths, dtypes)]
        + [jax.ShapeDtypeStruct((nex, 8, d), F32)],
        scratch_shapes=[pltpu.VMEM((tm, D_FF), F32)],
        compiler_params=_params(("arbitrary",), VMEM_LIMIT),
    )(yret, ymla, x2, tgt2, modv, g_ffn, g_fin, w_out, w_ff1, w_ff2a, w_ff2b)


def _pre_bwd(x2, ctx2, modv, g_attn, pg, drq, drk, dkc_r, drv, dvc_r, drg, dq_m, dkl, dkc, dvl, dvc, dxm,
             w_in, g_q, g_kv, w_uq, w_ukv, cos_t, sin_t, *, seq, tm, rider=None):
    t_lat, d = x2.shape
    t_ctx = ctx2.shape[0]
    nl, nc = t_lat // tm, t_ctx // tm
    n_all = t_lat + t_ctx
    tpe = seq // tm
    nex = t_lat // seq

    def body(x_ref, c_ref, mod_ref, g_ref, pg_ref, drq_ref, drk_ref, dkcr_ref, drv_ref, dvcr_ref, drg_ref,
             dq_ref, dkl_ref, dkc_ref, dvl_ref, dvc_ref, dxm_ref, win_ref, gq_ref, gkv_ref, wuq_ref, wukv_ref,
             cos_ref, sin_ref, dpb_ref, dqf_ref, dkvf_ref, gx_ref, st_ref):
        i = pl.program_id(0)
        lat = i < nl
        latf = lat.astype(F32)
        cos = cos_ref[...]
        sin = sin_ref[...]
        cos1 = cos[:, 0:LANES]
        sin1 = sin[:, 0:LANES]
        d_rq = _rope_t(drq_ref[...] * latf, cos, sin)
        d_rk = _rope_t(jnp.where(lat, drk_ref[...], dkcr_ref[...]), cos, sin) * (RET_DK ** -0.5)
        d_rv = jnp.where(lat, drv_ref[...], dvcr_ref[...])
        d_rg = drg_ref[...] * latf
        dq_all = dq_ref[...] * (latf * MLA_SCALE)
        dk_all = jnp.where(lat, dkl_ref[...], dkc_ref[...])
        dv_all = jnp.where(lat, dvl_ref[...], dvc_ref[...])
        dnq = jnp.zeros((tm, Q_LORA), F32)
        dnkv = jnp.zeros((tm, KV_LORA), F32)
        dkpe = jnp.zeros((tm, LANES), F32)
        for hd in range(HEADS):
            o = hd * MLA_HEAD
            dqh = jnp.concatenate([dq_all[:, o:o + 128], _rope_t(dq_all[:, o + 128:o + 256], cos1, sin1)],
                                  axis=1).astype(BF16)
            dqf_ref[:, o:o + 256] = dqh
            dnq = dnq + _dot(dqh, wuq_ref[hd])
            dkpe = dkpe + dk_all[:, o + 128:o + 256]
            dkvh = jnp.concatenate([dk_all[:, o:o + 128], dv_all[:, hd * 128:(hd + 1) * 128]], axis=1).astype(BF16)
            dkvf_ref[:, o:o + 256] = dkvh
            dnkv = dnkv + _dot_nt(dkvh, wukv_ref[hd])
        d_kpe = _rope_t(dkpe, cos1, sin1)
        pgv = pg_ref[...]
        cq = pgv[:, 512:896]
        ckv = pgv[:, 896:1152]
        rq_ = lax.rsqrt(jnp.mean(cq * cq, axis=-1, keepdims=True) + EPS)
        cqh = cq * rq_
        dcqh = dnq * gq_ref[...]
        d_cq = rq_ * (dcqh - cqh * jnp.mean(dcqh * cqh, axis=-1, keepdims=True))
        rkv_ = lax.rsqrt(jnp.mean(ckv * ckv, axis=-1, keepdims=True) + EPS)
        ckvh = ckv * rkv_
        dckvh = dnkv * gkv_ref[...]
        d_ckv = rkv_ * (dckvh - ckvh * jnp.mean(dckvh * ckvh, axis=-1, keepdims=True))
        dpb = jnp.concatenate([d_rq, d_rk, d_rv, d_rg, d_cq, d_ckv, d_kpe], axis=1).astype(BF16)
        dpb_ref[...] = dpb
        dh = _dot(dpb, win_ref[...])
        xt = jnp.where(lat, x_ref[...], c_ref[...])
        sc = mod_ref[0, 1:2, :]
        g = g_ref[...]
        r = lax.rsqrt(jnp.mean(xt * xt, axis=-1, keepdims=True) + EPS)
        xh = xt * r
        dxh = dh * (1.0 + sc) * g
        dx = r * (dxh - xh * jnp.mean(dxh * xh, axis=-1, keepdims=True))

        @pl.when(lat)
        def _():
            gx_ref[...] = dxm_ref[...] + dx

        def rsum(v):
            return jnp.sum(v, axis=0, keepdims=True)

        def widen(v):
            return jnp.concatenate([v, jnp.zeros((1, d - v.shape[1]), F32)], axis=1)

        stats = jnp.concatenate([
            rsum(dh), rsum(dh * xh * g), rsum(dh * (1.0 + sc) * xh), widen(rsum(dnq * cqh)), widen(rsum(dnkv * ckvh)),
            jnp.zeros((3, d), F32)], axis=0)
        first = jnp.logical_or(jnp.logical_and(lat, i % tpe == 0), i == nl)

        @pl.when(first)
        def _():
            st_ref[0] = stats

        @pl.when(jnp.logical_not(first))
        def _():
            st_ref[0] += stats

    def lat_tile(width):
        return pl.BlockSpec((tm, width), lambda i: (jnp.minimum(i, nl - 1), 0))

    def ctx_tile(width):
        return pl.BlockSpec((tm, width), lambda i: (jnp.maximum(i - nl, 0), 0))

    def tile(width):
        return pl.BlockSpec((tm, width), lambda i: (i, 0))

    tab = pl.BlockSpec((tm, 256), lambda i: (jnp.where(i < nl, i % tpe, tpe), 0))
    ex = pl.BlockSpec((1, 8, d), lambda i: (jnp.minimum(i // tpe, nex), 0, 0))
    return _hosted_call(
        body, (x2, ctx2, modv, g_attn, pg, drq, drk, dkc_r, drv, dvc_r, drg, dq_m, dkl, dkc, dvl, dvc, dxm,
               w_in, g_q, g_kv, w_uq, w_ukv, cos_t, sin_t), name="pre_bwd", grid=(nl + nc,),
        in_specs=[
            lat_tile(d), ctx_tile(d), ex, _full((1, d)), tile(PG_COLS),
            lat_tile(256), lat_tile(256), ctx_tile(256), lat_tile(512), ctx_tile(512), lat_tile(512),
            lat_tile(1024), lat_tile(1024), ctx_tile(1024), lat_tile(512), ctx_tile(512), lat_tile(d),
            _once(w_in.shape), _full((1, Q_LORA)), _full((1, KV_LORA)), _once(w_uq.shape), _once(w_ukv.shape),
            tab, tab,
        ],
        out_specs=[tile(IN_PAD), tile(1024), tile(1024), lat_tile(d), ex],
        out_shape=[
            jax.ShapeDtypeStruct((n_all, IN_PAD), BF16), jax.ShapeDtypeStruct((n_all, 1024), BF16),
            jax.ShapeDtypeStruct((n_all, 1024), BF16), jax.ShapeDtypeStruct((t_lat, d), F32),
            jax.ShapeDtypeStruct((nex + 1, 8, d), F32),
        ],
        sem=("arbitrary",), rider=rider)


MLA_SCALE = 1.0 / math.sqrt(MLA_NOPE + MLA_ROPE)
KEY_BLOCK = 1024


def _mla_specs(t_lat, seq, ctx_len, tq, heads=1):
    nqt = seq // tq
    cb = t_lat // ctx_len
    q = pl.BlockSpec((tq, heads * MLA_HEAD), lambda b, h, j: (b * nqt + j, h))
    kl = pl.BlockSpec((seq, heads * MLA_HEAD), lambda b, h, j: (b, h))
    kc = pl.BlockSpec((ctx_len, heads * MLA_HEAD), lambda b, h, j: (cb + b, h))
    vl = pl.BlockSpec((seq, heads * 128), lambda b, h, j: (b, h))
    vc = pl.BlockSpec((ctx_len, heads * 128), lambda b, h, j: (cb + b, h))
    o = pl.BlockSpec((tq, heads * 128), lambda b, h, j: (b * nqt + j, h))
    return q, kl, kc, vl, vc, o


FWD_HEADS = 2
BWD_HEADS = 1


def _mla_fwd(q, k, v, *, t_lat, seq, ctx_len, tq, rider=None):
    nex = t_lat // seq

    def body(q_ref, kl_ref, kc_ref, vl_ref, vc_ref, o_ref, lse_ref):
        for hh in range(FWD_HEADS):
            wide = slice(hh * MLA_HEAD, (hh + 1) * MLA_HEAD)
            cols = slice(hh * 128, (hh + 1) * 128)
            qb = q_ref[:, wide]
            s = _dot_nt(qb, kl_ref[:, wide])
            sc = _dot_nt(qb, kc_ref[:, wide])
            m = jnp.maximum(jnp.max(s, axis=-1, keepdims=True), jnp.max(sc, axis=-1, keepdims=True))
            p = jnp.exp(s - m)
            pc = jnp.exp(sc - m)
            total = jnp.sum(p, axis=-1, keepdims=True) + jnp.sum(pc, axis=-1, keepdims=True)
            o = _dot(p.astype(BF16), vl_ref[:, cols]) + _dot(pc.astype(BF16), vc_ref[:, cols])
            o_ref[:, cols] = (o * (1.0 / total)).astype(BF16)
            lse_ref[:, cols] = jnp.broadcast_to(m + jnp.log(total), (tq, 128))

    qs, kl, kc, vl, vc, os_ = _mla_specs(t_lat, seq, ctx_len, tq, FWD_HEADS)
    return _hosted_call(
        body, (q, k, k, v, v), name="mla_fwd", grid=(nex, HEADS // FWD_HEADS, seq // tq),
        in_specs=[qs, kl, kc, vl, vc], out_specs=[os_, os_],
        out_shape=[jax.ShapeDtypeStruct((t_lat, HEADS * 128), BF16), jax.ShapeDtypeStruct((t_lat, HEADS * 128), F32)],
        sem=("parallel", "parallel", "arbitrary"), rider=rider)


def _mla_bwd(q, k, v, ymla, lse, dmix, *, t_lat, seq, ctx_len, tq, rider=None):
    nex = t_lat // seq
    nqt = seq // tq
    t_ctx = nex * ctx_len
    kb = min(KEY_BLOCK, seq)

    def body(q_ref, kl_ref, kc_ref, vl_ref, vc_ref, o_ref, lse_ref, do_ref, dq_ref, dkl_out, dkc_out, dvl_out, dvc_out,
             dkl_ref, dkc_ref, dvl_ref, dvc_ref):
        j = pl.program_id(2)

        @pl.when(j == 0)
        def _():
            dkl_ref[...] = jnp.zeros(dkl_ref.shape, F32)
            dkc_ref[...] = jnp.zeros(dkc_ref.shape, F32)
            dvl_ref[...] = jnp.zeros(dvl_ref.shape, F32)
            dvc_ref[...] = jnp.zeros(dvc_ref.shape, F32)

        for hh in range(BWD_HEADS):
            wide = slice(hh * MLA_HEAD, (hh + 1) * MLA_HEAD)
            cols = slice(hh * 128, (hh + 1) * 128)
            qb = q_ref[:, wide]
            dob = do_ref[:, cols]
            delta = jnp.sum(dob.astype(F32) * o_ref[:, cols].astype(F32), axis=-1, keepdims=True)
            lse_row = lse_ref[:, hh * 128:hh * 128 + 1]

            def block(k_ref, v_ref, dk_ref, dv_ref, rows):
                kbl = k_ref[rows, wide]
                vbl = v_ref[rows, cols]
                p = jnp.exp(_dot_nt(qb, kbl) - lse_row)
                ds = (p * (_dot_nt(dob, vbl) - delta)).astype(BF16)
                dk_ref[rows, wide] += _dot_tn(ds, qb)
                dv_ref[rows, cols] += _dot_tn(p.astype(BF16), dob)
                return _dot(ds, kbl)

            dq = block(kc_ref, vc_ref, dkc_ref, dvc_ref, pl.ds(0, ctx_len))
            for i in range(seq // kb):
                dq = dq + block(kl_ref, vl_ref, dkl_ref, dvl_ref, pl.ds(i * kb, kb))
            dq_ref[:, wide] = dq.astype(BF16)

        @pl.when(j == nqt - 1)
        def _():
            dkl_out[...] = dkl_ref[...].astype(BF16)
            dkc_out[...] = dkc_ref[...].astype(BF16)
            dvl_out[...] = dvl_ref[...].astype(BF16)
            dvc_out[...] = dvc_ref[...].astype(BF16)

    g = BWD_HEADS
    qs, kl, kc, vl, vc, os_ = _mla_specs(t_lat, seq, ctx_len, tq, g)
    do_spec = pl.BlockSpec((tq, g * 128), lambda b, h, j: (b * nqt + j, HEADS // g + h))
    key_blocks = [(seq, g * MLA_HEAD), (ctx_len, g * MLA_HEAD), (seq, g * 128), (ctx_len, g * 128)]
    return _hosted_call(
        body, (q, k, k, v, v, ymla, lse, dmix), name="mla_bwd", grid=(nex, HEADS // g, nqt),
        in_specs=[qs, kl, kc, vl, vc, os_, os_, do_spec],
        out_specs=[qs] + [pl.BlockSpec(blk, lambda b, h, j: (b, h)) for blk in key_blocks],
        out_shape=[
            jax.ShapeDtypeStruct((t_lat, HEADS * MLA_HEAD), BF16),
            jax.ShapeDtypeStruct((t_lat, HEADS * MLA_HEAD), BF16),
            jax.ShapeDtypeStruct((t_ctx, HEADS * MLA_HEAD), BF16),
            jax.ShapeDtypeStruct((t_lat, HEADS * 128), BF16),
            jax.ShapeDtypeStruct((t_ctx, HEADS * 128), BF16),
        ],
        scratch_shapes=[pltpu.VMEM(blk, F32) for blk in key_blocks],
        sem=("parallel", "parallel", "arbitrary"), rider=rider)


def _decay_terms(lg, chunk, forward):
    ii = lax.broadcasted_iota(jnp.int32, (chunk, chunk), 0)
    jj = lax.broadcasted_iota(jnp.int32, (chunk, chunk), 1)
    diff = (ii - jj) if forward else (jj - ii)
    dist = jnp.maximum(diff, 0).astype(F32)
    dmat = jnp.where(diff >= 0, jnp.exp(lg * dist), 0.0)
    pos = lax.broadcasted_iota(jnp.int32, (chunk, 1), 0).astype(F32)
    if forward:
        e_q = pos + 1.0
        e_k = (chunk - 1.0) - pos
    else:
        e_q = chunk - pos
        e_k = pos
    wq = jnp.exp(lg * e_q)
    wk = jnp.exp(lg * e_k)
    cd = jnp.exp(jnp.full((1, 1), lg * chunk, F32))
    return dmat, dist, wq, wk, e_q, e_k, cd


def _ctx_weights(lg, ctx_len, forward):
    pos = lax.broadcasted_iota(jnp.int32, (ctx_len, 1), 0).astype(F32)
    e = ((ctx_len - 1.0) - pos) if forward else pos
    return jnp.exp(lg * e), e


def _pair_specs(t_lat, seq, ctx_len):
    cb = t_lat // ctx_len
    qk = pl.BlockSpec((seq, 128), lambda b, p: (b, p))
    v = pl.BlockSpec((seq, 256), lambda b, p: (b, p))
    kc = pl.BlockSpec((ctx_len, 128), lambda b, p: (cb + b, p))
    vc = pl.BlockSpec((ctx_len, 256), lambda b, p: (cb + b, p))
    return qk, v, kc, vc


def _lane_masks():
    lane = lax.broadcasted_iota(jnp.int32, (1, 128), 1)
    return [(lane // RET_DK) == hh for hh in (0, 1)]


def _ret_fwd_pair(rq, rk, rv, pg, lg, g_ret, *, t_lat, seq, ctx_len, chunk, rider=None):
    nex = t_lat // seq
    n_chunk = seq // chunk

    def body(q_ref, k_ref, v_ref, kc_ref, vc_ref, rg_ref, lg_ref, g_ref, y_ref, o_ref):
        pair = pl.program_id(1)
        masks = _lane_masks()
        kcf = kc_ref[...].astype(F32)
        chains = [(forward, hh) for forward in (True, False) for hh in (0, 1)]
        terms, s0 = [], []
        for forward, hh in chains:
            lgd = lg_ref[0 if forward else 1, 2 * pair + hh]
            terms.append(_decay_terms(lgd, chunk, forward))
            wc, _ = _ctx_weights(lgd, ctx_len, forward)
            s0.append(_dot_tn((jnp.where(masks[hh], kcf, 0.0) * wc).astype(BF16), vc_ref[:, hh * 128:(hh + 1) * 128]))
        both = [terms[hh][0] + terms[2 + hh][0] for hh in (0, 1)]
        o_ref[...] = jnp.zeros(o_ref.shape, F32)

        def step(t, states):
            new = [None] * 4
            for forward in (True, False):
                n = t if forward else n_chunk - 1 - t
                sl = pl.ds(pl.multiple_of(n * chunk, chunk), chunk)
                qb = q_ref[sl, :]
                kf_all = k_ref[sl, :].astype(F32)
                for hh in (0, 1):
                    c = (0 if forward else 2) + hh
                    _, _, wq, wk, _, _, cd = terms[c]
                    cols = slice(hh * 128, (hh + 1) * 128)
                    qm = jnp.where(masks[hh], qb, jnp.zeros((), BF16))
                    kf = jnp.where(masks[hh], kf_all, 0.0)
                    vb = v_ref[sl, cols]
                    o = wq * _dot(qm, states[c].astype(BF16))
                    if forward:
                        o = o + _dot((_dot_nt(qm, kf.astype(BF16)) * both[hh]).astype(BF16), vb)
                    o_ref[sl, cols] += o
                    new[c] = cd * states[c] + _dot_tn((kf * wk).astype(BF16), vb)
            return tuple(new)

        lax.fori_loop(0, n_chunk, step, tuple(s0))

        def norm_step(n, carry):
            sl = pl.ds(pl.multiple_of(n * chunk, chunk), chunk)
            for hh in (0, 1):
                cols = slice(hh * 128, (hh + 1) * 128)
                o = o_ref[sl, cols]
                mu = jnp.mean(o, axis=-1, keepdims=True)
                oc = o - mu
                var = jnp.mean(oc * oc, axis=-1, keepdims=True)
                rg = rg_ref[sl, cols]
                y_ref[sl, cols] = (oc * lax.rsqrt(var + EPS) * g_ref[:, cols] * (rg / (1.0 + jnp.exp(-rg)))).astype(BF16)
            return carry

        lax.fori_loop(0, n_chunk, norm_step, 0)

    qk, v, kc, vc = _pair_specs(t_lat, seq, ctx_len)
    return _hosted_call(
        body, (rq, rk, rv, rk, rv, pg, lg, g_ret), name="ret_fwd", grid=(nex, HEADS // 2),
        in_specs=[qk, qk, v, kc, vc, v, pl.BlockSpec(memory_space=pltpu.SMEM), pl.BlockSpec((1, 256), lambda b, p: (0, p))],
        out_specs=[v, v],
        out_shape=[jax.ShapeDtypeStruct((t_lat, HEADS * RET_DV), BF16), jax.ShapeDtypeStruct((t_lat, HEADS * RET_DV), F32)],
        sem=("parallel", "arbitrary"), rider=rider)


def _ret_bwd_pair(rq, rk, rv, pg, osum, dmix, lg, g_ret, *, t_lat, seq, ctx_len, chunk, rider=None):
    nex = t_lat // seq
    n_chunk = seq // chunk
    t_ctx = nex * ctx_len

    def body(q_ref, k_ref, v_ref, kc_ref, vc_ref, rg_ref, o_ref, dy_ref, lg_ref, g_ref,
             dq_out, dk_out, dv_out, dkc_ref, dvc_ref, drg_ref, st_ref, do_s, s_st, dq_ref, dk_ref, dv_ref):
        pair = pl.program_id(1)
        masks = _lane_masks()
        kcf = kc_ref[...].astype(F32)

        def norm_step(n, dgains):
            sl = pl.ds(pl.multiple_of(n * chunk, chunk), chunk)
            out = []
            for hh in (0, 1):
                cols = slice(hh * 128, (hh + 1) * 128)
                gain = g_ref[:, cols]
                o = o_ref[sl, cols]
                mu = jnp.mean(o, axis=-1, keepdims=True)
                oc = o - mu
                rstd = lax.rsqrt(jnp.mean(oc * oc, axis=-1, keepdims=True) + EPS)
                ohat = oc * rstd
                rg = rg_ref[sl, cols]
                sg = 1.0 / (1.0 + jnp.exp(-rg))
                dy = dy_ref[sl, cols].astype(F32)
                don = dy * (rg * sg)
                drg_ref[sl, cols] = (dy * (ohat * gain) * (sg * (1.0 + rg * (1.0 - sg)))).astype(BF16)
                dohat = don * gain
                do_s[sl, cols] = rstd * (dohat - jnp.mean(dohat, axis=-1, keepdims=True)
                                         - ohat * jnp.mean(dohat * ohat, axis=-1, keepdims=True))
                out.append(dgains[hh] + jnp.sum(don * ohat, axis=0, keepdims=True))
            return tuple(out)

        zero_row = jnp.zeros((1, 128), F32)
        dgains = lax.fori_loop(0, n_chunk, norm_step, (zero_row, zero_row))
        dq_ref[...] = jnp.zeros(dq_ref.shape, F32)
        dk_ref[...] = jnp.zeros(dk_ref.shape, F32)
        dv_ref[...] = jnp.zeros(dv_ref.shape, F32)

        chains = [(forward, hh) for forward in (True, False) for hh in (0, 1)]
        terms, ctxw, s0 = [], [], []
        for forward, hh in chains:
            lgd = lg_ref[0 if forward else 1, 2 * pair + hh]
            terms.append(_decay_terms(lgd, chunk, forward))
            ctxw.append(_ctx_weights(lgd, ctx_len, forward))
            s0.append(_dot_tn((jnp.where(masks[hh], kcf, 0.0) * ctxw[-1][0]).astype(BF16), vc_ref[:, hh * 128:(hh + 1) * 128]))

        def chunk_at(t, ascending):
            n = t if ascending else n_chunk - 1 - t
            return n, pl.ds(pl.multiple_of(n * chunk, chunk), chunk)

        def state_step(t, states):
            new = []
            for c, (forward, hh) in enumerate(chains):
                n, sl = chunk_at(t, forward)
                wk, cd = terms[c][3], terms[c][6]
                s_st[c, n] = states[c]
                kf = jnp.where(masks[hh], k_ref[sl, :].astype(F32), 0.0)
                new.append(cd * states[c] + _dot_tn((kf * wk).astype(BF16), v_ref[sl, hh * 128:(hh + 1) * 128]))
            return tuple(new)

        lax.fori_loop(0, n_chunk, state_step, tuple(s0))

        both = [terms[hh][0] + terms[2 + hh][0] for hh in (0, 1)]

        def grad_step(t, carry):
            out = [None] * len(chains)
            in_chunk_b = [None, None]
            for forward in (True, False):
                n, sl = chunk_at(t, not forward)
                qb = q_ref[sl, :]
                kf_all = k_ref[sl, :].astype(F32)
                dq_sum = jnp.zeros((chunk, 128), F32)
                dk_sum = jnp.zeros((chunk, 128), F32)
                for hh in (0, 1):
                    c = (0 if forward else 2) + hh
                    g_next, dlg = carry[c]
                    dmat, dist, wq, wk, e_q, e_k, cd = terms[c]
                    cols = slice(hh * 128, (hh + 1) * 128)
                    qm = jnp.where(masks[hh], qb, jnp.zeros((), BF16))
                    kf = jnp.where(masks[hh], kf_all, 0.0)
                    kb = kf.astype(BF16)
                    vb = v_ref[sl, cols]
                    do = do_s[sl, cols]
                    dob = do.astype(BF16)
                    s_n = s_st[c, n]
                    s_nb = s_n.astype(BF16)
                    gb = g_next.astype(BF16)
                    dk_cross = wk * _dot_nt(vb, gb)
                    dv = _dot((kf * wk).astype(BF16), gb)
                    o_cross = wq * _dot(qm, s_nb)
                    dq_sum = dq_sum + wq * _dot_nt(dob, s_nb)
                    dk_sum = dk_sum + dk_cross
                    dlg = (dlg + chunk * cd * jnp.sum(g_next * s_n, keepdims=True)
                           + jnp.sum(e_k * jnp.sum(kf * dk_cross, axis=-1, keepdims=True), keepdims=True)
                           + jnp.sum(e_q * jnp.sum(o_cross * do, axis=-1, keepdims=True), keepdims=True))
                    if forward:
                        a_raw = _dot_nt(qm, kb)
                        da_raw = _dot_nt(dob, vb)
                        prod = a_raw * da_raw
                        dlg = dlg + jnp.sum(dist * dmat * prod, keepdims=True)
                        in_chunk_b[hh] = jnp.sum(terms[2 + hh][1] * terms[2 + hh][0] * prod, keepdims=True)
                        dab = (da_raw * both[hh]).astype(BF16)
                        dq_sum = dq_sum + _dot(dab, kb)
                        dk_sum = dk_sum + _dot_tn(dab, qm)
                        dv = dv + _dot_tn((a_raw * both[hh]).astype(BF16), dob)
                    else:
                        dlg = dlg + in_chunk_b[hh]
                    dv_ref[sl, cols] += dv
                    out[c] = (cd * g_next + _dot_tn((qm.astype(F32) * wq).astype(BF16), dob), dlg)
                dq_ref[sl, :] += dq_sum
                dk_ref[sl, :] += dk_sum
            return tuple(out)

        zero = (jnp.zeros((128, 128), F32), jnp.zeros((1, 1), F32))
        res = lax.fori_loop(0, n_chunk, grad_step, (zero,) * len(chains))
        dkc_sum = jnp.zeros((ctx_len, 128), F32)
        dvc = [jnp.zeros((ctx_len, 128), F32)] * 2
        dlgs = []
        for c, (forward, hh) in enumerate(chains):
            ds0, dlg = res[c]
            wc, e_c = ctxw[c]
            kcm = jnp.where(masks[hh], kcf, 0.0)
            ds0b = ds0.astype(BF16)
            dkc_part = wc * _dot_nt(vc_ref[:, hh * 128:(hh + 1) * 128], ds0b)
            dkc_sum = dkc_sum + dkc_part
            dvc[hh] = dvc[hh] + _dot((kcm * wc).astype(BF16), ds0b)
            dlgs.append(dlg + jnp.sum(e_c * jnp.sum(kcm * dkc_part, axis=-1, keepdims=True), keepdims=True))
        dq_out[...] = dq_ref[...].astype(BF16)
        dk_out[...] = dk_ref[...].astype(BF16)
        dv_out[...] = dv_ref[...].astype(BF16)
        dkc_ref[...] = dkc_sum
        for hh in (0, 1):
            cols = slice(hh * 128, (hh + 1) * 128)
            dvc_ref[:, cols] = dvc[hh]
            st_ref[0, :, cols] = jnp.concatenate([
                dgains[hh], jnp.broadcast_to(dlgs[hh], (1, 128)), jnp.broadcast_to(dlgs[2 + hh], (1, 128)),
                jnp.zeros((5, 128), F32)], axis=0)

    qk, v, kc, vc = _pair_specs(t_lat, seq, ctx_len)
    return _hosted_call(
        body, (rq, rk, rv, rk, rv, pg, osum, dmix, lg, g_ret), name="ret_bwd", grid=(nex, HEADS // 2),
        in_specs=[qk, qk, v, kc, vc, v, v, v, pl.BlockSpec(memory_space=pltpu.SMEM),
                  pl.BlockSpec((1, 256), lambda b, p: (0, p))],
        out_specs=[
            qk, qk, v,
            pl.BlockSpec((ctx_len, 128), lambda b, p: (b, p)),
            pl.BlockSpec((ctx_len, 256), lambda b, p: (b, p)),
            v,
            pl.BlockSpec((1, 8, 256), lambda b, p: (b, 0, p)),
        ],
        out_shape=[
            jax.ShapeDtypeStruct((t_lat, 256), BF16), jax.ShapeDtypeStruct((t_lat, 256), BF16),
            jax.ShapeDtypeStruct((t_lat, 512), BF16), jax.ShapeDtypeStruct((t_ctx, 256), F32),
            jax.ShapeDtypeStruct((t_ctx, 512), F32), jax.ShapeDtypeStruct((t_lat, 512), BF16),
            jax.ShapeDtypeStruct((nex, 8, 512), F32),
        ],
        scratch_shapes=[pltpu.VMEM((seq, 256), F32), pltpu.VMEM((4, n_chunk, 128, 128), F32),
                        pltpu.VMEM((seq, 128), F32), pltpu.VMEM((seq, 128), F32), pltpu.VMEM((seq, 256), F32)],
        sem=("parallel", "arbitrary"), rider=rider)


def _matmul_tn(a, b, *, bm, bn, bk, chip_major, name, out_dtype=F32, rider=None):
    tk, m = a.shape
    n = b.shape[1]
    slab = n // N_CHIPS
    per_block = bn // slab if chip_major else 1
    bk = max(c for c in range(LANES, min(bk, tk) + 1, LANES) if tk % c == 0)
    nk = tk // bk
    blk = (per_block, bm, slab) if chip_major else (bm, bn)

    def body(a_ref, b_ref, o_ref, acc_ref):
        k = pl.program_id(2)
        if chip_major:
            parts = [_dot_tn(a_ref[...], b_ref[:, s * slab:(s + 1) * slab]) for s in range(per_block)]
        else:
            parts = [_dot_tn(a_ref[...], b_ref[...])]

        @pl.when(k == 0)
        def _():
            for s, part in enumerate(parts):
                if chip_major:
                    acc_ref[s] = part
                else:
                    acc_ref[...] = part

        @pl.when(k > 0)
        def _():
            for s, part in enumerate(parts):
                if chip_major:
                    acc_ref[s] += part
                else:
                    acc_ref[...] += part

        @pl.when(k == nk - 1)
        def _():
            o_ref[...] = acc_ref[...].astype(out_dtype)

    if chip_major:
        out_spec = pl.BlockSpec(blk, lambda i, j, k: (j, i, 0))
        out_shape = jax.ShapeDtypeStruct((N_CHIPS, m, slab), out_dtype)
    else:
        out_spec = pl.BlockSpec(blk, lambda i, j, k: (i, j))
        out_shape = jax.ShapeDtypeStruct((m, n), out_dtype)
    (out,), carried = _hosted_call(
        body, (a, b), name=name, grid=(m // bm, n // bn, nk),
        in_specs=[pl.BlockSpec((bk, bm), lambda i, j, k: (k, i)), pl.BlockSpec((bk, bn), lambda i, j, k: (k, j))],
        out_specs=[out_spec], out_shape=[out_shape], scratch_shapes=[pltpu.VMEM(blk, F32)],
        sem=("parallel", "parallel", "arbitrary"), rider=rider)
    return out if rider is None else (out, carried)


_LATE = ("w_out", "w_ff1", "w_ff2")
_EARLY = ("w_in", "w_uq", "w_ukv")


def _local_step(x, ctx, tgt, modv, lg, g_attn, g_ffn, g_fin, g_ret, g_q, g_kv, w_in, w_uq, w_ukv, late, place=None,
                *, tm=256, tq=256, chunk=256):
    nex, seq, d = x.shape
    ctx_len = ctx.shape[1]
    t_lat = nex * seq
    tm = min(tm, seq)
    x2 = x.reshape(t_lat, d)
    ctx2 = ctx.reshape(nex * ctx_len, d)
    tgt2 = tgt.reshape(t_lat, d)
    tm_fwd = min(2 * tm, seq)
    cos_t, sin_t = _rope_tables(seq, tm)
    dims = dict(t_lat=t_lat, seq=seq, ctx_len=ctx_len)
    alone = place is None

    (hb, pg, rq, rk, rv, nq, nkv, q, k, v), crossed_a = _pre_fwd(
        x2, ctx2, modv, g_attn, w_in, g_q, g_kv, w_uq, w_ukv, *_rope_tables(seq, tm_fwd), seq=seq, tm=tm_fwd,
        rider=None if alone else _gather_ici_rider([late[2]]))
    (yret, osum), got = _ret_fwd_pair(
        rq, rk, rv, pg, lg, g_ret, chunk=min(2 * chunk, seq), **dims,
        rider=None if alone else _merge_riders(_gather_d2d_rider(crossed_a), _gather_ici_rider([late[3]])))
    (ymla, lse), got_rest = _mla_fwd(
        q, k, v, tq=tq, **dims,
        rider=None if alone else _merge_riders(_gather_rider([late[0], late[1]], staged=True), _gather_d2d_rider(got[1:])))
    w_out, w_ff1, w_ff2a, w_ff2b = late if alone else (got_rest[0], got_rest[1], got[0], got_rest[2])
    mix, act, du, h2, df, dmo, dmix, dxm, st_post = _post(yret, ymla, x2, tgt2, modv, g_ffn, g_fin, w_out.reshape(d, d),
                                                         w_ff1, w_ff2a, w_ff2b, seq=seq, tm=min(tm, 256))
    kw = dict(bm=1024, bn=1024, bk=2048, out_dtype=BF16)
    g_ff2 = _matmul_tn(act, df, chip_major=False, name="gw_ff2", **kw).reshape(N_CHIPS, D_FF // N_CHIPS, d)
    if alone:
        g_ff1 = _matmul_tn(h2, du, chip_major=True, name="gw_ff1", **kw)
        g_out = _matmul_tn(mix, dmo, chip_major=False, name="gw_out", **kw).reshape(N_CHIPS, d // N_CHIPS, d)
        (dq_m, dkl, dkc, dvl, dvc), _ = _mla_bwd(q, k, v, ymla, lse, dmix, tq=tq, **dims)
        (drq, drk, drv, dkc_r, dvc_r, drg, st_ret), _ = _ret_bwd_pair(rq, rk, rv, pg, osum, dmix, lg, g_ret, chunk=chunk,
                                                                      **dims)
        late_out = [g_out, g_ff1, g_ff2]
    else:
        core, slot = place
        g_ff1, x_ff2 = _matmul_tn(h2, du, chip_major=True, name="gw_ff1", rider=_exchange_rider([g_ff2]), **kw)
        g_out, x_ff1 = _matmul_tn(mix, dmo, chip_major=False, name="gw_out", rider=_exchange_rider([g_ff1]), **kw)
        g_out = g_out.reshape(N_CHIPS, d // N_CHIPS, d)
        p_ff2 = _add_half(g_ff2, x_ff2[0], core, "add_half_w_ff2")
        p_ff1 = _add_half(g_ff1, x_ff1[0], core, "add_half_w_ff1")
        (dq_m, dkl, dkc, dvl, dvc), (l_ff2, l_ff1, x_out) = _mla_bwd(
            q, k, v, ymla, lse, dmix, tq=min(seq, 512), **dims,
            rider=_merge_riders(_scatter_rider([p_ff2, p_ff1]), _exchange_rider([g_out])))
        p_out = _add_half(g_out, x_out, core, "add_half_w_out")
        m_ff2 = _sum_chips(p_ff2, l_ff2, slot, "sum_chips_w_ff2")
        m_ff1 = _sum_chips(p_ff1, l_ff1, slot, "sum_chips_w_ff1")
        (drq, drk, drv, dkc_r, dvc_r, drg, st_ret), (l_out,) = _ret_bwd_pair(
            rq, rk, rv, pg, osum, dmix, lg, g_ret, chunk=chunk, **dims, rider=_scatter_rider([p_out]))
        late_out = [_sum_chips(p_out, l_out, slot, "sum_chips_w_out"), m_ff1, m_ff2]
    (dpb, dqf, dkvf, gx, st_pre), _ = _pre_bwd(
        x2, ctx2, modv, g_attn, pg, drq, drk, dkc_r, drv, dvc_r, drg, dq_m, dkl, dkc, dvl, dvc, dxm, w_in, g_q, g_kv,
        w_uq, w_ukv, cos_t, sin_t, seq=seq, tm=tm)
    g_early = [
        _matmul_tn(dpb, hb, bm=IN_PAD // 2, bn=d, bk=1536, chip_major=False, name="gw_in"),
        _matmul_tn(dqf, nq, bm=HEADS * MLA_HEAD, bn=Q_LORA, bk=1536, chip_major=False, name="gw_uq"),
        _matmul_tn(nkv, dkvf, bm=KV_LORA, bn=HEADS * 256, bk=1536, chip_major=True, name="gw_ukv"),
    ]
    return gx.reshape(nex, seq, d), g_early, late_out, st_post, st_ret, st_pre


_ANY = pl.BlockSpec(memory_space=pl.ANY)
_VMEM = pl.BlockSpec(memory_space=pltpu.VMEM)
_OFFSETS = tuple((dx, dy, dc) for dx in (0, 1) for dy in (0, 1) for dc in (0, 1))[1:]
_CHIP_OFFSETS = ((1, 0), (0, 1), (1, 1))


def _place():
    return lax.axis_index("x"), lax.axis_index("y"), lax.axis_index("c")


def _flip(v, d):
    return 1 - v if d else v


def _gather8_rider(a, in_vmem=True):
    def copies(a_ref, o_ref, send, recv):
        x, y, z = _place()
        me = 4 * x + 2 * y + z
        out = []
        for k, (dx, dy, dc) in enumerate(_OFFSETS):
            peer = (_flip(x, dx), _flip(y, dy), _flip(z, dc))
            landing = o_ref.at[4 * peer[0] + 2 * peer[1] + peer[2]]
            out.append((
                pltpu.make_async_remote_copy(src_ref=a_ref, dst_ref=o_ref.at[me], send_sem=send.at[k],
                                             recv_sem=recv.at[k], device_id=peer, device_id_type=MESH),
                pltpu.make_async_remote_copy(src_ref=a_ref, dst_ref=landing, send_sem=send.at[k],
                                             recv_sem=recv.at[k], device_id=peer, device_id_type=MESH)))
        return me, out

    def start(ins, outs, sems):
        me, cps = copies(ins[0], outs[0], sems[0], sems[1])
        pltpu.make_async_copy(ins[0], outs[0].at[me], sems[2]).start()
        for out_cp, _ in cps:
            out_cp.start()

    def finish(ins, outs, sems):
        me, cps = copies(ins[0], outs[0], sems[0], sems[1])
        for out_cp, in_cp in cps:
            in_cp.wait_recv()
            out_cp.wait_send()
        pltpu.make_async_copy(ins[0], outs[0].at[me], sems[2]).wait()

    spec = [_VMEM] if in_vmem else [_ANY]
    return _Rider([a], [jax.ShapeDtypeStruct((N_DEV,) + a.shape, a.dtype)],
                  [pltpu.SemaphoreType.DMA((7,)), pltpu.SemaphoreType.DMA((7,)), pltpu.SemaphoreType.DMA],
                  start, finish, in_specs=spec, out_specs=spec)


def _merge_riders(*riders):
    ins, outs, sems, in_specs, out_specs, aliases, cuts = [], [], [], [], [], {}, []
    for r in riders:
        cuts.append((len(ins), len(outs), len(sems)))
        aliases.update({len(ins) + i: len(outs) + j for i, j in r.aliases.items()})
        ins += r.ins
        outs += r.out_shapes
        sems += r.sems
        in_specs += r.in_specs
        out_specs += r.out_specs

    def part(r, cut, r_ins, r_outs, r_sems):
        return (r_ins[cut[0]:cut[0] + len(r.ins)], r_outs[cut[1]:cut[1] + len(r.out_shapes)],
                r_sems[cut[2]:cut[2] + len(r.sems)])

    def start(r_ins, r_outs, r_sems):
        for r, cut in zip(riders, cuts):
            r.start(*part(r, cut, r_ins, r_outs, r_sems))

    def finish(r_ins, r_outs, r_sems):
        for r, cut in zip(riders, cuts):
            r.finish(*part(r, cut, r_ins, r_outs, r_sems))

    def middle(r_ins, r_outs, r_sems):
        for r, cut in zip(riders, cuts):
            if r.middle is not None:
                r.middle(*part(r, cut, r_ins, r_outs, r_sems))

    return _Rider(ins, outs, sems, start, finish, aliases=aliases, in_specs=in_specs, out_specs=out_specs,
                  middle=middle if any(r.middle is not None for r in riders) else None)


def _allgather8(a, name):
    return _run_rider(_gather8_rider(a), name)[0]


BF16_TILE_ROWS = 16


def _half(o, slot, which):
    r2 = o.shape[1] // 2
    if r2 % BF16_TILE_ROWS == 0:
        return o.at[slot, pl.ds(which * r2, r2)]
    c2 = o.shape[2] // 2
    assert c2 % LANES == 0
    return o.at[slot, :, pl.ds(which * c2, c2)]


def _gather_send(o_refs, send, recv):
    x, y, z = _place()
    chip = 2 * x + y
    for a, o in enumerate(o_refs):
        r2 = o.shape[1] // 2
        mine = _half(o, chip, z)
        for k, (dx, dy) in enumerate(_CHIP_OFFSETS):
            pltpu.make_async_remote_copy(
                src_ref=mine, dst_ref=mine, send_sem=send.at[a, k], recv_sem=recv.at[a, k],
                device_id=(_flip(x, dx), _flip(y, dy), z), device_id_type=MESH).start()


def _gather_landed(o_refs, send, recv, then=None):
    x, y, z = _place()
    chip = 2 * x + y
    for a, o in enumerate(o_refs):
        for k, (dx, dy) in enumerate(_CHIP_OFFSETS):
            landed = _half(o, 2 * _flip(x, dx) + _flip(y, dy), z)
            pltpu.make_async_remote_copy(
                src_ref=landed, dst_ref=landed, send_sem=send.at[a, k], recv_sem=recv.at[a, k],
                device_id=(_flip(x, dx), _flip(y, dy), z), device_id_type=MESH).wait_recv()
            if then is not None:
                then(a, k, landed)
    for a, o in enumerate(o_refs):
        mine = _half(o, chip, z)
        for k, (dx, dy) in enumerate(_CHIP_OFFSETS):
            pltpu.make_async_remote_copy(
                src_ref=mine, dst_ref=mine, send_sem=send.at[a, k], recv_sem=recv.at[a, k],
                device_id=(_flip(x, dx), _flip(y, dy), z), device_id_type=MESH).wait_send()


def _pass_on(o_refs, fsend, frecv, a, k, landed):
    x, y, z = _place()
    pltpu.make_async_remote_copy(
        src_ref=landed, dst_ref=landed, send_sem=fsend.at[a, k], recv_sem=frecv.at[a, k],
        device_id=(x, y, 1 - z), device_id_type=MESH).start()


def _passed_on(o_refs, fsend, frecv):
    x, y, z = _place()
    for a, o in enumerate(o_refs):
        for k, (dx, dy) in enumerate(_CHIP_OFFSETS):
            other = 2 * _flip(x, dx) + _flip(y, dy)
            got = _half(o, other, 1 - z)
            gave = _half(o, other, z)
            pltpu.make_async_remote_copy(
                src_ref=got, dst_ref=got, send_sem=fsend.at[a, k], recv_sem=frecv.at[a, k],
                device_id=(x, y, 1 - z), device_id_type=MESH).wait_recv()
            pltpu.make_async_remote_copy(
                src_ref=gave, dst_ref=gave, send_sem=fsend.at[a, k], recv_sem=frecv.at[a, k],
                device_id=(x, y, 1 - z), device_id_type=MESH).wait_send()


def _gather_finish(o_refs, send, recv, fsend, frecv):
    _gather_landed(o_refs, send, recv, functools.partial(_pass_on, o_refs, fsend, frecv))
    _passed_on(o_refs, fsend, frecv)


class _Rider:
    def __init__(self, ins, out_shapes, sems, start, finish, aliases=None, in_specs=None, out_specs=None, middle=None):
        self.ins, self.out_shapes, self.sems = list(ins), list(out_shapes), list(sems)
        self.start, self.finish, self.aliases = start, finish, dict(aliases or {})
        self.middle = middle
        self.in_specs = list(in_specs) if in_specs else [_ANY] * len(self.ins)
        self.out_specs = list(out_specs) if out_specs else [_ANY] * len(self.out_shapes)


def _run_rider(rider, name):
    r_in, r_out = len(rider.ins), len(rider.out_shapes)

    def body(*refs):
        ins, outs, sems = refs[:r_in], refs[r_in:r_in + r_out], refs[r_in + r_out:]
        rider.start(ins, outs, sems)
        if rider.middle is not None:
            rider.middle(ins, outs, sems)
        rider.finish(ins, outs, sems)

    return pl.pallas_call(
        body, name=name, in_specs=rider.in_specs, out_specs=rider.out_specs, out_shape=rider.out_shapes,
        input_output_aliases=rider.aliases, scratch_shapes=rider.sems,
    )(*rider.ins)


def _hosted_call(body, args, *, name, grid, in_specs, out_specs, out_shape, scratch_shapes=(), sem, rider=None,
                 prefetch=()):
    scratch_shapes = list(scratch_shapes)
    n_pf, n_in, n_out, n_sc = len(prefetch), len(in_specs), len(out_specs), len(scratch_shapes)
    r_in, r_out = (len(rider.ins), len(rider.out_shapes)) if rider else (0, 0)
    last = tuple(g - 1 for g in grid)

    def hosted(*refs):
        p = 0
        parts = []
        for cnt in (n_pf, n_in, r_in, n_out, r_out, n_sc):
            parts.append(refs[p:p + cnt])
            p += cnt
        pf, ins, r_ins, outs, r_outs, scratch = parts
        sems = refs[p:]
        ids = [pl.program_id(a) for a in range(len(grid))]
        is_first = functools.reduce(jnp.logical_and, [i == 0 for i in ids])
        is_last = functools.reduce(jnp.logical_and, [i == e for i, e in zip(ids, last)])

        @pl.when(is_first)
        def _():
            rider.start(r_ins, r_outs, sems)

        if rider.middle is not None:
            linear = functools.reduce(lambda acc, ig: acc * ig[1] + ig[0], zip(ids, grid), 0)

            @pl.when(linear == math.prod(grid) * 3 // 4)
            def _():
                rider.middle(r_ins, r_outs, sems)

        body(*pf, *ins, *outs, *scratch)

        @pl.when(is_last)
        def _():
            rider.finish(r_ins, r_outs, sems)

    if rider is None:
        kern, all_in, all_out, shapes, scratch, aliases, extra = body, list(in_specs), list(out_specs), list(out_shape), \
            scratch_shapes, {}, []
    else:
        kern, all_in, all_out = hosted, list(in_specs) + rider.in_specs, list(out_specs) + rider.out_specs
        shapes, scratch, extra = list(out_shape) + rider.out_shapes, scratch_shapes + rider.sems, rider.ins
        aliases = {n_pf + n_in + i: n_out + j for i, j in rider.aliases.items()}
        sem = ("arbitrary",) * len(grid)
    if prefetch:
        spec = dict(grid_spec=pltpu.PrefetchScalarGridSpec(
            num_scalar_prefetch=n_pf, grid=grid, in_specs=all_in, out_specs=all_out, scratch_shapes=scratch))
    else:
        spec = dict(grid=grid, in_specs=all_in, out_specs=all_out, scratch_shapes=scratch)
    res = pl.pallas_call(kern, name=name, out_shape=shapes, input_output_aliases=aliases,
                         compiler_params=_params(sem, VMEM_LIMIT), **spec)(*prefetch, *args, *extra)
    return list(res[:n_out]), list(res[n_out:])


def _gather_rider(ws, staged=False):
    n = len(ws)
    shapes = [jax.ShapeDtypeStruct(w.shape, w.dtype) for w in ws]
    sems = [pltpu.SemaphoreType.DMA((n, 3))] * 4
    aliases = {a: a for a in range(n)}

    def start(ins, outs, s):
        _gather_send(outs, s[0], s[1])

    if not staged:
        return _Rider(ws, shapes, sems, start, lambda ins, outs, s: _gather_finish(outs, *s), aliases=aliases)
    return _Rider(
        ws, shapes, sems, start, lambda ins, outs, s: _passed_on(outs, s[2], s[3]), aliases=aliases,
        middle=lambda ins, outs, s: _gather_landed(outs, s[0], s[1], functools.partial(_pass_on, outs, s[2], s[3])))


def _gather_ici_rider(ws):
    n = len(ws)
    return _Rider(
        ws, [jax.ShapeDtypeStruct(w.shape, w.dtype) for w in ws], [pltpu.SemaphoreType.DMA((n, 3))] * 2,
        lambda ins, outs, sems: _gather_send(outs, sems[0], sems[1]),
        lambda ins, outs, sems: _gather_landed(outs, sems[0], sems[1]),
        aliases={a: a for a in range(n)})


def _gather_d2d_rider(ws):
    n = len(ws)

    def start(ins, outs, sems):
        x, y, z = _place()
        for a, o in enumerate(outs):
            for k, (dx, dy) in enumerate(_CHIP_OFFSETS):
                _pass_on(outs, sems[0], sems[1], a, k, _half(o, 2 * _flip(x, dx) + _flip(y, dy), z))

    return _Rider(
        ws, [jax.ShapeDtypeStruct(w.shape, w.dtype) for w in ws], [pltpu.SemaphoreType.DMA((n, 3))] * 2,
        start, lambda ins, outs, sems: _passed_on(outs, sems[0], sems[1]), aliases={a: a for a in range(n)})


def _copies_rider(ins, out_shapes, sem_shape, make):
    def start(r_ins, r_outs, sems):
        for cp in make(r_ins, r_outs, sems[0], sems[1]):
            cp.start()

    def finish(r_ins, r_outs, sems):
        for cp in make(r_ins, r_outs, sems[0], sems[1]):
            cp.wait()

    return _Rider(ins, out_shapes, [pltpu.SemaphoreType.DMA(sem_shape)] * 2, start, finish)


def _exchange_rider(gs):
    def make(g_refs, r_refs, send, recv):
        x, y, z = _place()
        return [pltpu.make_async_remote_copy(
            src_ref=g.at[:, pl.ds((1 - z) * (g.shape[1] // 2), g.shape[1] // 2)], dst_ref=r, send_sem=send.at[a],
            recv_sem=recv.at[a], device_id=(x, y, 1 - z), device_id_type=MESH)
            for a, (g, r) in enumerate(zip(g_refs, r_refs))]

    shapes = [jax.ShapeDtypeStruct((g.shape[0], g.shape[1] // 2, g.shape[2]), g.dtype) for g in gs]
    return _copies_rider(gs, shapes, (len(gs),), make)


def _add_half(g, recv, core, name):
    s, r, c = g.shape
    r2 = r // 2
    rb = r2
    for cand in (512, 256, 128, 64):
        if r2 % cand == 0:
            rb = cand
            break
    g4 = g.reshape(s, 2, r2, c)

    def body(core_ref, g_ref, r_ref, o_ref):
        o_ref[...] = (g_ref[...].astype(F32) + r_ref[...].astype(F32)).astype(BF16)

    return pl.pallas_call(
        body, name=name,
        grid_spec=pltpu.PrefetchScalarGridSpec(
            num_scalar_prefetch=1, grid=(s, r2 // rb),
            in_specs=[pl.BlockSpec((None, None, rb, c), lambda i, j, cr: (i, cr[0], j, 0)),
                      pl.BlockSpec((None, rb, c), lambda i, j, cr: (i, j, 0))],
            out_specs=pl.BlockSpec((None, rb, c), lambda i, j, cr: (i, j, 0))),
        out_shape=jax.ShapeDtypeStruct((s, r2, c), BF16),
        compiler_params=_params(("parallel", "parallel")),
    )(core, g4, recv)


def _scatter_rider(ps):
    def make(p_refs, o_refs, send, recv):
        x, y, z = _place()
        copies = []
        for a, (p, o) in enumerate(zip(p_refs, o_refs)):
            for k, (dx, dy) in enumerate(_CHIP_OFFSETS):
                other = 2 * _flip(x, dx) + _flip(y, dy)
                copies.append(pltpu.make_async_remote_copy(
                    src_ref=p.at[other], dst_ref=o.at[k], send_sem=send.at[a, k], recv_sem=recv.at[a, k],
                    device_id=(_flip(x, dx), _flip(y, dy), z), device_id_type=MESH))
        return copies

    shapes = [jax.ShapeDtypeStruct((3,) + p.shape[1:], p.dtype) for p in ps]
    return _copies_rider(ps, shapes, (len(ps), 3), make)


def _sum_chips(p, landed, chip, name):
    _, r2, c = p.shape
    rb = r2
    for cand in (256, 128, 64):
        if r2 % cand == 0:
            rb = cand
            break

    def body(s_ref, p_ref, l_ref, o_ref):
        acc = p_ref[...].astype(F32)
        for k in range(3):
            acc = acc + l_ref[k].astype(F32)
        o_ref[...] = acc

    return pl.pallas_call(
        body, name=name,
        grid_spec=pltpu.PrefetchScalarGridSpec(
            num_scalar_prefetch=1, grid=(r2 // rb,),
            in_specs=[pl.BlockSpec((None, rb, c), lambda i, s: (s[0], i, 0)),
                      pl.BlockSpec((3, rb, c), lambda i, s: (0, i, 0))],
            out_specs=pl.BlockSpec((rb, c), lambda i, s: (i, 0))),
        out_shape=jax.ShapeDtypeStruct((r2, c), F32),
        compiler_params=_params(("parallel",)),
    )(chip, p, landed)


def _swap_rider(hs):
    def make(h_refs, o_refs, send, recv):
        x, y, z = _place()
        return [pltpu.make_async_remote_copy(
            src_ref=h, dst_ref=o, send_sem=send.at[a], recv_sem=recv.at[a], device_id=(x, y, 1 - z),
            device_id_type=MESH) for a, (h, o) in enumerate(zip(h_refs, o_refs))]

    return _copies_rider(hs, [jax.ShapeDtypeStruct(h.shape, h.dtype) for h in hs], (len(hs),), make)


def _reduce_scatter_vmem(gs, rows, rider, name):
    n = len(gs)
    r_in, r_out = len(rider.ins), len(rider.out_shapes)
    halves = [(r // 2, g.shape[-1]) for g, (r, _) in zip(gs, rows)]
    piece_cols = 2 * LANES
    pieces = [(a, slice(c0, min(c0 + piece_cols, h[1]))) for a, h in enumerate(halves) for c0 in range(0, h[1], piece_cols)]
    n_p = len(pieces)

    def body(*refs):
        p = 0
        parts = []
        for cnt in (n, r_in, n, n, r_out, n, n, n, 6):
            parts.append(refs[p:p + cnt])
            p += cnt
        g_refs, r_ins, mine, theirs, r_outs, recv, part, land, sems = parts
        r_sems = refs[p:]
        xs, xr, ss, sr, ws, wr = sems
        x, y, z = _place()
        chip = 2 * x + y
        sib = (x, y, 1 - z)
        rider.start(r_ins, r_outs, r_sems)

        def half_of(a, s, which):
            r2 = halves[a][0]
            if len(g_refs[a].shape) == 3:
                return g_refs[a].at[s, pl.ds(pl.multiple_of(which * r2, 8), r2)]
            return g_refs[a].at[pl.ds(pl.multiple_of(s * rows[a][1] + which * r2, 8), r2)]

        def exchange(i):
            a, cols = pieces[i]
            return [pltpu.make_async_remote_copy(
                src_ref=half_of(a, s, 1 - z).at[:, cols], dst_ref=recv[a].at[s, :, cols], send_sem=xs.at[i, s],
                recv_sem=xr.at[i, s], device_id=sib, device_id_type=MESH) for s in range(N_CHIPS)]

        def scatter(i):
            a, cols = pieces[i]
            return [pltpu.make_async_remote_copy(
                src_ref=part[a].at[2 * _flip(x, dx) + _flip(y, dy), :, cols], dst_ref=land[a].at[k, :, cols],
                send_sem=ss.at[i, k], recv_sem=sr.at[i, k], device_id=(_flip(x, dx), _flip(y, dy), z),
                device_id_type=MESH) for k, (dx, dy) in enumerate(_CHIP_OFFSETS)]

        def swap(i):
            a, cols = pieces[i]
            return pltpu.make_async_remote_copy(
                src_ref=mine[a].at[:, cols], dst_ref=theirs[a].at[:, cols], send_sem=ws.at[i], recv_sem=wr.at[i],
                device_id=sib, device_id_type=MESH)

        for i in range(len(pieces)):
            for cp in exchange(i):
                cp.start()
        for i, (a, cols) in enumerate(pieces):
            for cp in exchange(i):
                cp.wait()
            for s in range(N_CHIPS):
                part[a][s, :, cols] = (half_of(a, s, z)[:, cols] + recv[a][s, :, cols]).astype(BF16)
            for cp in scatter(i):
                cp.start()
        for i, (a, cols) in enumerate(pieces):
            for cp in scatter(i):
                cp.wait()
            acc = part[a][chip, :, cols].astype(F32)
            for k in range(3):
                acc = acc + land[a][k, :, cols].astype(F32)
            mine[a][:, cols] = acc
            swap(i).start()
        for i in range(len(pieces)):
            swap(i).wait()
        rider.finish(r_ins, r_outs, r_sems)

    half_shapes = [jax.ShapeDtypeStruct(h, F32) for h in halves]
    res = pl.pallas_call(
        body, name=name, in_specs=[_VMEM] * n + rider.in_specs, out_specs=[_VMEM] * (2 * n) + rider.out_specs,
        out_shape=half_shapes + half_shapes + rider.out_shapes,
        scratch_shapes=[pltpu.VMEM((N_CHIPS,) + h, F32) for h in halves] + [pltpu.VMEM((N_CHIPS,) + h, BF16) for h in halves]
        + [pltpu.VMEM((3,) + h, BF16) for h in halves]
        + [pltpu.SemaphoreType.DMA((n_p, N_CHIPS))] * 2 + [pltpu.SemaphoreType.DMA((n_p, 3))] * 2
        + [pltpu.SemaphoreType.DMA((n_p,))] * 2 + rider.sems,
        input_output_aliases={n + i: 2 * n + j for i, j in rider.aliases.items()},
        compiler_params=_params(None, VMEM_LIMIT),
    )(*gs, *rider.ins)
    return list(res[:n]), list(res[n:2 * n]), list(res[2 * n:])


SMALL_ROWS = 32
PACK_ROWS = 16


def _pack_small(st_post, st_ret, st_pre):
    d = st_post.shape[2]

    def body(po_ref, re_ref, pr_ref, o_ref):
        o_ref[...] = jnp.zeros(o_ref.shape, F32)
        o_ref[0:1, :] = pr_ref[0, 2:3, :] + pr_ref[1, 2:3, :] + pr_ref[2, 2:3, :]
        o_ref[1:2, :] = po_ref[0, 4:5, :] + po_ref[1, 4:5, :]
        o_ref[2:3, :] = po_ref[0, 5:6, :] + po_ref[1, 5:6, :]
        o_ref[3:4, 0:512] = re_ref[0, 0:1, :] + re_ref[1, 0:1, :]
        o_ref[4:5, :] = pr_ref[0, 3:4, :] + pr_ref[1, 3:4, :] + pr_ref[2, 3:4, :]
        o_ref[5:6, :] = pr_ref[0, 4:5, :] + pr_ref[1, 4:5, :] + pr_ref[2, 4:5, :]
        lane = lax.broadcasted_iota(jnp.int32, (1, LANES), 1)
        for row, src in ((6, 1), (10, 2)):
            acc = jnp.zeros((1, LANES), F32)
            for hd in range(HEADS):
                grp = re_ref[0, src:src + 1, hd * LANES:(hd + 1) * LANES] + re_ref[1, src:src + 1, hd * LANES:(hd + 1) * LANES]
                acc = acc + jnp.where(lane == hd, grp, 0.0)
            o_ref[row:row + 1, 0:LANES] = acc
        o_ref[7:8, :] = po_ref[0, 6:7, :] + po_ref[1, 6:7, :]
        o_ref[8:9, :] = pr_ref[2, 0:1, :]
        o_ref[9:10, :] = pr_ref[2, 1:2, :]
        for e in range(2):
            b = 12 + 6 * e
            o_ref[b:b + 1, :] = pr_ref[e, 0:1, :]
            o_ref[b + 1:b + 2, :] = pr_ref[e, 1:2, :]
            o_ref[b + 2:b + 3, :] = po_ref[e, 3:4, :]
            o_ref[b + 3:b + 4, :] = po_ref[e, 0:1, :]
            o_ref[b + 4:b + 5, :] = po_ref[e, 1:2, :]
            o_ref[b + 5:b + 6, :] = po_ref[e, 2:3, :]

    return pl.pallas_call(body, name="pack_small", out_shape=jax.ShapeDtypeStruct((SMALL_ROWS, d), F32))(st_post, st_ret, st_pre)


def _small_reduce(gathered):
    d = gathered.shape[2]

    def body(g_ref, o_ref):
        tot = g_ref[0, 0:PACK_ROWS, :]
        for dev in range(1, N_DEV):
            tot = tot + g_ref[dev, 0:PACK_ROWS, :]
        o_ref[0:PACK_ROWS, :] = tot
        for j in range(6):
            acc = g_ref[0, 12 + j:13 + j, :] + g_ref[0, 18 + j:19 + j, :]
            for dev in range(1, N_DEV):
                acc = acc + g_ref[dev, 12 + j:13 + j, :] + g_ref[dev, 18 + j:19 + j, :]
            if j < 2:
                acc = acc + o_ref[8 + j:9 + j, :]
            o_ref[PACK_ROWS + j:PACK_ROWS + j + 1, :] = acc
        o_ref[PACK_ROWS + 6:PACK_ROWS + 8, :] = jnp.zeros((2, d), F32)

    return pl.pallas_call(body, name="small_reduce", out_shape=jax.ShapeDtypeStruct((PACK_ROWS + 8, d), F32))(gathered)


_SMALL = (("g_attn", 0, 1024), ("g_ffn", 1, 1024), ("g_final", 2, 1024), ("g_ret", 3, 512), ("g_q_lora", 4, 384),
          ("g_kv_lora", 5, 256), ("ret_decay_fwd", 6, HEADS), ("ret_decay_bwd", 10, HEADS))
_SMALL_NAMES = tuple(s[0] for s in _SMALL) + ("c_ctx", "b_ada")


def _small_final(tot, dcc, sg8, ws, ms, vs):
    d = tot.shape[1]
    n = len(_SMALL_NAMES)

    def body(*refs):
        t_ref, dcc_ref, sg_ref = refs[0:3]
        w_refs, m_refs, v_refs = refs[3:3 + n], refs[3 + n:3 + 2 * n], refs[3 + 2 * n:3 + 3 * n]
        outs = refs[3 + 3 * n:]
        g_refs, d_refs, mo_refs, vo_refs = outs[0:n], outs[n:2 * n], outs[2 * n:3 * n], outs[3 * n:4 * n]
        l_ref = outs[4 * n]

        def update(i, g, sl=None):
            pick = (lambda r: r[...]) if sl is None else (lambda r: r[:, sl])
            dl, mn, vn = _adam_math(pick(w_refs[i]), g, pick(m_refs[i]), pick(v_refs[i]))
            if sl is None:
                g_refs[i][...], d_refs[i][...], mo_refs[i][...], vo_refs[i][...] = g, dl, mn, vn
            else:
                g_refs[i][:, sl], d_refs[i][:, sl], mo_refs[i][:, sl], vo_refs[i][:, sl] = g, dl, mn, vn

        for i, (name, row, width) in enumerate(_SMALL):
            g = t_ref[row:row + 1, 0:width]
            if name == "ret_decay_fwd":
                g = g * sg_ref[0:1, 0:width]
            elif name == "ret_decay_bwd":
                g = g * sg_ref[1:2, 0:width]
            update(i, g)
        i_cc, i_b = n - 2, n - 1
        cc = w_refs[i_cc][...]
        s = 1.0 / (1.0 + jnp.exp(-cc))
        dsilu = dcc_ref[0, 0:1, :] + dcc_ref[2, 0:1, :] + dcc_ref[4, 0:1, :] + dcc_ref[6, 0:1, :]
        update(i_cc, dsilu * (s * (1.0 + cc * (1.0 - s))))
        for j in range(6):
            update(i_b, t_ref[PACK_ROWS + j:PACK_ROWS + j + 1, :], pl.ds(j * d, d))
        l_ref[...] = jnp.broadcast_to((0.5 / d) * jnp.sum(t_ref[7:8, :], keepdims=True), l_ref.shape)

    shapes = [jax.ShapeDtypeStruct(a.shape, F32) for a in ws]
    outs = pl.pallas_call(
        body, name="small_final", out_shape=shapes * 4 + [jax.ShapeDtypeStruct((8, LANES), F32)],
    )(tot, dcc, sg8, *ws, *ms, *vs)
    return outs[0:n], outs[n:2 * n], outs[2 * n:3 * n], outs[3 * n:4 * n], outs[4 * n]


_WEIGHTS = ("c_ctx", "w_ada", "b_ada", "g_attn", "g_ffn", "w_in", "ret_decay_fwd", "ret_decay_bwd", "g_ret", "g_q_lora",
            "w_uq", "g_kv_lora", "w_ukv", "w_out", "w_ff1", "w_ff2", "g_final")
_BIG = ("w_in", "w_uq", "w_ukv", "w_out", "w_ff1", "w_ff2")
_TRANSPOSED = ("w_in", "w_uq")


def kernel(x, c, ctx, c_ctx, w_ada, b_ada, g_attn, g_ffn, w_in, ret_decay_fwd, ret_decay_bwd, g_ret, g_q_lora, w_uq, g_kv_lora, w_ukv, w_out, w_ff1, w_ff2, g_final, loss_target, m_c_ctx, m_w_ada, m_b_ada, m_g_attn, m_g_ffn, m_w_in, m_ret_decay_fwd, m_ret_decay_bwd, m_g_ret, m_g_q_lora, m_w_uq, m_g_kv_lora, m_w_ukv, m_w_out, m_w_ff1, m_w_ff2, m_g_final, v_c_ctx, v_w_ada, v_b_ada, v_g_attn, v_g_ffn, v_w_in, v_ret_decay_fwd, v_ret_decay_bwd, v_g_ret, v_g_q_lora, v_w_uq, v_g_kv_lora, v_w_ukv, v_w_out, v_w_ff1, v_w_ff2, v_g_final):
    w = dict(c_ctx=c_ctx, w_ada=w_ada, b_ada=b_ada, g_attn=g_attn, g_ffn=g_ffn, w_in=w_in, ret_decay_fwd=ret_decay_fwd,
             ret_decay_bwd=ret_decay_bwd, g_ret=g_ret, g_q_lora=g_q_lora, w_uq=w_uq, g_kv_lora=g_kv_lora, w_ukv=w_ukv,
             w_out=w_out, w_ff1=w_ff1, w_ff2=w_ff2, g_final=g_final)
    m = dict(c_ctx=m_c_ctx, w_ada=m_w_ada, b_ada=m_b_ada, g_attn=m_g_attn, g_ffn=m_g_ffn, w_in=m_w_in,
             ret_decay_fwd=m_ret_decay_fwd, ret_decay_bwd=m_ret_decay_bwd, g_ret=m_g_ret, g_q_lora=m_g_q_lora, w_uq=m_w_uq,
             g_kv_lora=m_g_kv_lora, w_ukv=m_w_ukv, w_out=m_w_out, w_ff1=m_w_ff1, w_ff2=m_w_ff2, g_final=m_g_final)
    v = dict(c_ctx=v_c_ctx, w_ada=v_w_ada, b_ada=v_b_ada, g_attn=v_g_attn, g_ffn=v_g_ffn, w_in=v_w_in,
             ret_decay_fwd=v_ret_decay_fwd, ret_decay_bwd=v_ret_decay_bwd, g_ret=v_g_ret, g_q_lora=v_g_q_lora, w_uq=v_w_uq,
             g_kv_lora=v_g_kv_lora, w_ukv=v_w_ukv, w_out=v_w_out, w_ff1=v_w_ff1, w_ff2=v_w_ff2, g_final=v_g_final)
    xi, yi, ci = lax.axis_index("x"), lax.axis_index("y"), lax.axis_index("c")
    chip = 2 * xi + yi
    dev = 2 * chip + ci
    nex, seq, d = x.shape
    n_ada = w_ada.shape[2]

    dec = jnp.zeros((8, LANES), F32).at[0, :HEADS].set(ret_decay_fwd[0]).at[1, :HEADS].set(ret_decay_bwd[0])
    lg8, sg8 = _decay_prep(dec)
    lg = lg8[:2, :HEADS]

    def shard_of(t, k):
        return t[k][0].T if k in _TRANSPOSED else t[k][0]

    shard = {k: shard_of(w, k) for k in _BIG}
    head_rows = MLA_NOPE + MLA_ROPE
    shard["w_uq"] = jnp.pad(shard["w_uq"], ((0, MLA_HEAD - head_rows), (0, 0)))
    slot = chip.reshape(1).astype(jnp.int32)
    core = ci.reshape(1).astype(jnp.int32)
    slots = {k: _cast_into_slot(shard[k], slot, "cast_" + k) for k in _EARLY}
    half_ff = shard["w_ff2"].shape[0] // 2
    late_pieces = [(shard["w_out"], 0, shard["w_out"].shape[0]), (shard["w_ff1"], 0, shard["w_ff1"].shape[0]),
                   (shard["w_ff2"], 0, half_ff), (shard["w_ff2"], half_ff, half_ff)]
    late_slots, (w_in_f, w_uq_k, w_ukv_k, c8) = _cast_into_slots(
        late_pieces, slot, "cast_late",
        rider=_merge_riders(_gather_rider([slots[k] for k in _EARLY]),
                            _gather8_rider(jnp.pad(c, ((0, 8 - nex), (0, 0))), in_vmem=False)))

    a_in = jnp.concatenate([c8[:, :nex].reshape(N_DEV * nex, d), c_ctx.reshape(1, d), jnp.zeros((7, d), F32)], axis=0)
    b_sh = lax.dynamic_slice(b_ada, (0, chip * n_ada), (1, n_ada))
    mod4 = _run_rider(_gather_rider([_mod_fwd(a_in, w_ada[0], b_sh, slot)]), "ag_mod")[0]
    w_in_k = jnp.pad(w_in_f.reshape(IN_COLS, d), ((0, IN_PAD - IN_COLS), (0, 0)))
    mod_all = mod4.transpose(1, 0, 2).reshape(a_in.shape[0], N_CHIPS * n_ada)
    mod_me = lax.dynamic_slice(mod_all, (nex * dev, 0), (nex, N_CHIPS * n_ada)).reshape(nex, 6, d)
    mod_c = mod_all[N_DEV * nex].reshape(1, 6, d)
    modv = jnp.pad(jnp.concatenate([mod_me, mod_c], axis=0), ((0, 0), (0, 2), (0, 0)))

    gx, g_early, late, st_post, st_ret, st_pre = _local_step(
        x, ctx, loss_target, modv, lg, g_attn, g_ffn, g_final.reshape(1, d), g_ret, g_q_lora, g_kv_lora,
        w_in_k, w_uq_k, w_ukv_k, late_slots, (core, slot))

    mine, theirs, (*late_theirs, gathered) = _reduce_scatter_vmem(
        g_early, [(IN_COLS // N_CHIPS, IN_COLS // N_CHIPS), (head_rows, MLA_HEAD), (KV_LORA, KV_LORA)],
        _merge_riders(_swap_rider(late), _gather8_rider(_pack_small(st_post, st_ret, st_pre))), "rs_early")
    tot = _small_reduce(gathered)
    dm = jnp.concatenate([
        gathered[:, 12:24].reshape(N_DEV * nex, 6 * d),
        jnp.concatenate([tot[8:10].reshape(1, 2 * d), jnp.zeros((1, 4 * d), F32)], axis=1),
        jnp.zeros((7, 6 * d), F32)], axis=0)
    dm_sh = lax.dynamic_slice(dm, (0, chip * n_ada), (dm.shape[0], n_ada))
    g_ada, da, *ada_outs = _mod_bwd_adamw(a_in, dm_sh, w_ada[0], m["w_ada"][0], v["w_ada"][0])
    dcc = _allgather8(da[N_DEV * nex:], "ag_dcc")
    halves = dict(zip(_EARLY, zip(mine, theirs)))
    halves.update(zip(_LATE, zip(late, late_theirs)))
    grad, delta, new_m, new_v = {}, {}, {}, {}
    for k in _BIG:
        a, b = halves[k]
        res = _adamw_halves(shard_of(w, k), a, b, shard_of(m, k), shard_of(v, k), core, "adamw_" + k)
        grad[k], delta[k], new_m[k], new_v[k] = [(o.T if k in _TRANSPOSED else o).reshape(w[k].shape) for o in res]

    shp = w_ada.shape
    grad["w_ada"] = g_ada.reshape(shp)
    delta["w_ada"], new_m["w_ada"], new_v["w_ada"] = [o.reshape(shp) for o in ada_outs]
    rows = [{k: t[k].reshape(1, -1) for k in _SMALL_NAMES} for t in (w, m, v)]
    small = _small_final(tot, dcc, sg8, *[[t[k] for k in _SMALL_NAMES] for t in rows])
    for res, outs in zip((grad, delta, new_m, new_v), small[:4]):
        for k, o in zip(_SMALL_NAMES, outs):
            res[k] = o.reshape(w[k].shape)
    return (small[4][0, 0], gx, *[grad[k] for k in _WEIGHTS], *[delta[k] for k in _WEIGHTS],
            *[new_m[k] for k in _WEIGHTS], *[new_v[k] for k in _WEIGHTS])
```

```python
import functools
import math

import jax
import jax.numpy as jnp
from jax import lax
from jax.experimental import pallas as pl
from jax.experimental.pallas import tpu as pltpu

F32 = jnp.float32
BF16 = jnp.bfloat16
MESH = pl.DeviceIdType.MESH

EPS = 1e-6
D_MODEL = 1024
D_FF = 4096
HEADS = 4
RET_DK = 64
RET_DV = 128
MLA_NOPE = 128
MLA_ROPE = 64
MLA_HEAD = 256
Q_LORA = 384
KV_LORA = 256
GRID_W = 64
ROPE_BASE = 10000.0
IN_COLS = 2240
IN_PAD = 2304
PG_COLS = 1152
N_CHIPS = 4
N_DEV = 8
LANES = 128
ADAM_LR = 0.001
ADAM_B1 = 0.9
ADAM_B2 = 0.999
ADAM_EPS = 1e-08
ADAM_WD = 0.01
ADAM_STEP = 10
VMEM_LIMIT = 56 * 1024 * 1024


def _dot(a, b):
    return jnp.dot(a, b, preferred_element_type=F32)


def _dot_nt(a, b):
    return lax.dot_general(a, b, (((1,), (1,)), ((), ())), preferred_element_type=F32)


def _dot_tn(a, b):
    return lax.dot_general(a, b, (((0,), (0,)), ((), ())), preferred_element_type=F32)


def _params(sem=None, vmem=None):
    return pltpu.CompilerParams(dimension_semantics=sem, vmem_limit_bytes=vmem)


def _full(shape):
    n = len(shape)
    return pl.BlockSpec(shape, lambda *_: (0,) * n)


def _once(shape):
    n = len(shape)
    return pl.BlockSpec(shape, lambda *_: (0,) * n, pipeline_mode=pl.Buffered(1))


def _rope(x, cos, sin):
    w = x.shape[-1]
    lo = (lax.broadcasted_iota(jnp.int32, (1, w), 1) % 64) < 32
    swapped = jnp.where(lo, pltpu.roll(x, w - 32, 1), pltpu.roll(x, 32, 1))
    return x * cos + swapped * sin


def _rope_t(g, cos, sin):
    w = g.shape[-1]
    lo = (lax.broadcasted_iota(jnp.int32, (1, w), 1) % 64) < 32
    t = g * sin
    swapped = jnp.where(lo, pltpu.roll(t, w - 32, 1), pltpu.roll(t, 32, 1))
    return g * cos + swapped


def _rope_tables(seq, tm):
    rows = seq // GRID_W
    row = jnp.repeat(jnp.arange(rows, dtype=F32), GRID_W)
    col = jnp.tile(jnp.arange(GRID_W, dtype=F32), rows)
    n_freq = RET_DK // 4
    freq = ROPE_BASE ** (-jnp.arange(n_freq, dtype=F32) / n_freq)
    ang = jnp.concatenate([row[:, None] * freq, col[:, None] * freq], axis=-1)
    cos, sin = jnp.cos(ang), jnp.sin(ang)
    cos_t = jnp.tile(jnp.concatenate([cos, cos], -1), (1, HEADS))
    sin_t = jnp.tile(jnp.concatenate([-sin, sin], -1), (1, HEADS))
    cos_t = jnp.concatenate([cos_t, jnp.ones((tm, 4 * RET_DK), F32)], 0)
    sin_t = jnp.concatenate([sin_t, jnp.zeros((tm, 4 * RET_DK), F32)], 0)
    return cos_t, sin_t


def _adam_math(w, g, m, v):
    mn = ADAM_B1 * m + (1.0 - ADAM_B1) * g
    vn = ADAM_B2 * v + (1.0 - ADAM_B2) * (g * g)
    m_hat = mn / (1.0 - ADAM_B1 ** ADAM_STEP)
    v_hat = vn / (1.0 - ADAM_B2 ** ADAM_STEP)
    return -ADAM_LR * (m_hat / (jnp.sqrt(v_hat) + ADAM_EPS) + ADAM_WD * w), mn, vn


def _cast_into_slots(pieces, slot, name, rider=None):
    c = pieces[0][0].shape[1]
    rb = max(b for b in range(16, 1025, 16) if b * c * 4 <= (5 << 19)
             and all(cnt % b == 0 and st % b == 0 for _, st, cnt in pieces))
    nbs = [cnt // rb for _, _, cnt in pieces]
    starts = [sum(nbs[:s]) for s in range(len(pieces))]

    def body(s_ref, *refs):
        i = pl.program_id(0)
        for s in range(len(pieces)):
            @pl.when(jnp.logical_and(i >= starts[s], i < starts[s] + nbs[s]))
            def _():
                refs[len(pieces) + s][...] = refs[s][...].astype(BF16)

    in_specs, out_specs = [], []
    for (_, first_row, _), nb, st in zip(pieces, nbs, starts):
        in_specs.append(pl.BlockSpec((rb, c), lambda i, s, nb=nb, st=st, f=first_row // rb: (f + jnp.clip(i - st, 0, nb - 1), 0)))
        out_specs.append(pl.BlockSpec((None, rb, c), lambda i, s, nb=nb, st=st: (s[0], jnp.clip(i - st, 0, nb - 1), 0)))
    return _hosted_call(
        body, [w for w, _, _ in pieces], name=name, grid=(sum(nbs),), prefetch=(slot,), in_specs=in_specs,
        out_specs=out_specs, out_shape=[jax.ShapeDtypeStruct((N_CHIPS, cnt, c), BF16) for _, _, cnt in pieces],
        sem=("arbitrary",), rider=rider)


def _cast_into_slot(w, slot, name):
    return _cast_into_slots([(w, 0, w.shape[0])], slot, name)[0][0]


def _adamw_halves(w, mine, theirs, m, v, core, name):
    r, c = w.shape
    r2 = r // 2
    rb = max(b for b in range(8, r2 + 1, 8) if r2 % b == 0 and b * c * 4 <= (1 << 21))
    nbh = r2 // rb

    def body(z_ref, w_ref, a_ref, b_ref, m_ref, v_ref, g_ref, d_ref, mo_ref, vo_ref):
        here = (pl.program_id(0) // nbh) == z_ref[0]
        gg = jnp.where(here, a_ref[...], b_ref[...])
        g_ref[...] = gg
        d_ref[...], mo_ref[...], vo_ref[...] = _adam_math(w_ref[...], gg, m_ref[...], v_ref[...])

    spec = pl.BlockSpec((rb, c), lambda i, z: (i, 0))
    a_spec = pl.BlockSpec((rb, c), lambda i, z: (jnp.clip(i - z[0] * nbh, 0, nbh - 1), 0))
    b_spec = pl.BlockSpec((rb, c), lambda i, z: (jnp.clip(i - (1 - z[0]) * nbh, 0, nbh - 1), 0))
    shp = jax.ShapeDtypeStruct((r, c), F32)
    return pl.pallas_call(
        body, name=name,
        grid_spec=pltpu.PrefetchScalarGridSpec(
            num_scalar_prefetch=1, grid=(r // rb,), in_specs=[spec, a_spec, b_spec, spec, spec], out_specs=[spec] * 4),
        out_shape=[shp] * 4,
        compiler_params=_params(("parallel",)),
    )(core, w, mine, theirs, m, v)


def _decay_prep(dec):
    def body(d_ref, lg_ref, sg_ref):
        d = d_ref[...]
        lg_ref[...] = jnp.minimum(d, 0.0) - jnp.log(1.0 + jnp.exp(-jnp.abs(d)))
        sg_ref[...] = 1.0 / (1.0 + jnp.exp(d))

    shp = jax.ShapeDtypeStruct(dec.shape, F32)
    return pl.pallas_call(body, name="decay_prep", out_shape=[shp, shp])(dec)


def _mod_fwd(a_in, w_ada, b_sh, slot):
    rows, d = a_in.shape
    n = w_ada.shape[1]
    bn = 512

    def body(s_ref, a_ref, w_ref, b_ref, o_ref):
        a = a_ref[...]
        s = (a / (1.0 + jnp.exp(-a))).astype(BF16)
        o_ref[...] = _dot(s, w_ref[...].astype(BF16)) + b_ref[...]

    (out,), _ = _hosted_call(
        body, (a_in, w_ada, b_sh), name="mod_fwd", grid=(n // bn,), prefetch=(slot,),
        in_specs=[pl.BlockSpec((rows, d), lambda j, s: (0, 0)), pl.BlockSpec((d, bn), lambda j, s: (0, j)),
                  pl.BlockSpec((1, bn), lambda j, s: (0, j))],
        out_specs=[pl.BlockSpec((None, rows, bn), lambda j, s: (s[0], 0, j))],
        out_shape=[jax.ShapeDtypeStruct((N_CHIPS, rows, n), F32)], sem=("parallel",))
    return out


def _mod_bwd_adamw(a_in, dm, w_ada, m, v):
    rows, d = a_in.shape
    n = w_ada.shape[1]
    bn = 3 * LANES
    nb = n // bn

    def body(a_ref, dm_ref, w_ref, m_ref, v_ref, gw_ref, da_ref, d_ref, mo_ref, vo_ref):
        j = pl.program_id(0)
        a = a_ref[...]
        s = (a / (1.0 + jnp.exp(-a))).astype(BF16)
        dmb = dm_ref[...].astype(BF16)
        w = w_ref[...]
        g = _dot_tn(s, dmb)
        gw_ref[...] = g
        d_ref[...], mo_ref[...], vo_ref[...] = _adam_math(w, g, m_ref[...], v_ref[...])
        part = _dot_nt(dmb, w.astype(BF16))

        @pl.when(j == 0)
        def _():
            da_ref[...] = part

        @pl.when(j > 0)
        def _():
            da_ref[...] += part

    blk = pl.BlockSpec((d, bn), lambda j: (0, j))
    shp = jax.ShapeDtypeStruct((d, n), F32)
    return pl.pallas_call(
        body, name="mod_bwd_adamw", grid=(nb,),
        in_specs=[_full((rows, d)), pl.BlockSpec((rows, bn), lambda j: (0, j)), blk, blk, blk],
        out_specs=[blk, _full((rows, d)), blk, blk, blk],
        out_shape=[shp, jax.ShapeDtypeStruct((rows, d), F32), shp, shp, shp],
        compiler_params=_params(("arbitrary",), VMEM_LIMIT),
    )(a_in, dm, w_ada, m, v)


def _pre_fwd(x2, ctx2, modv, g_attn, w_in, g_q, g_kv, w_uq, w_ukv, cos_t, sin_t, *, seq, tm, rider=None):
    t_lat, d = x2.shape
    t_ctx = ctx2.shape[0]
    nl, nc = t_lat // tm, t_ctx // tm
    n_all = t_lat + t_ctx
    tpe = seq // tm
    nex = t_lat // seq

    def body(x_ref, c_ref, mod_ref, g_ref, win_ref, gq_ref, gkv_ref, wuq_ref, wukv_ref, cos_ref, sin_ref,
             h_ref, pg_ref, rq_ref, rk_ref, rv_ref, nq_ref, nkv_ref, q_ref, k_ref, v_ref):
        i = pl.program_id(0)
        xt = jnp.where(i < nl, x_ref[...], c_ref[...])
        sh = mod_ref[0, 0:1, :]
        sc = mod_ref[0, 1:2, :]
        r = lax.rsqrt(jnp.mean(xt * xt, axis=-1, keepdims=True) + EPS)
        hb = ((xt * r) * g_ref[...] * (1.0 + sc) + sh).astype(BF16)
        h_ref[...] = hb
        p = _dot_nt(hb, win_ref[...])
        cos = cos_ref[...]
        sin = sin_ref[...]
        rq_ref[...] = _rope(p[:, 0:256], cos, sin).astype(BF16)
        rk_ref[...] = _rope(p[:, 256:512] * (RET_DK ** -0.5), cos, sin).astype(BF16)
        rv_ref[...] = p[:, 512:1024].astype(BF16)
        pg_ref[...] = p[:, 1024:2176]
        cq = p[:, 1536:1920]
        ckv = p[:, 1920:2176]
        nqb = (cq * lax.rsqrt(jnp.mean(cq * cq, axis=-1, keepdims=True) + EPS) * gq_ref[...]).astype(BF16)
        nkvb = (ckv * lax.rsqrt(jnp.mean(ckv * ckv, axis=-1, keepdims=True) + EPS) * gkv_ref[...]).astype(BF16)
        nq_ref[...] = nqb
        nkv_ref[...] = nkvb
        cos1 = cos[:, 0:LANES]
        sin1 = sin[:, 0:LANES]
        kpe = _rope(p[:, 2176:2304], cos1, sin1).astype(BF16)
        for hd in range(HEADS):
            o = hd * MLA_HEAD
            qh = _dot_nt(nqb, wuq_ref[hd]) * MLA_SCALE
            q_ref[:, o:o + 128] = qh[:, 0:128].astype(BF16)
            q_ref[:, o + 128:o + 256] = _rope(qh[:, 128:256], cos1, sin1).astype(BF16)
            kvh = _dot(nkvb, wukv_ref[hd])
            k_ref[:, o:o + 128] = kvh[:, 0:128].astype(BF16)
            k_ref[:, o + 128:o + 256] = kpe
            v_ref[:, hd * 128:(hd + 1) * 128] = kvh[:, 128:256].astype(BF16)

    def tile(width):
        return pl.BlockSpec((tm, width), lambda i: (i, 0))

    widths = (d, PG_COLS, 256, 256, 512, Q_LORA, KV_LORA, HEADS * MLA_HEAD, HEADS * MLA_HEAD, HEADS * 128)
    dtypes = (BF16, F32, BF16, BF16, BF16, BF16, BF16, BF16, BF16, BF16)
    tab = pl.BlockSpec((tm, 256), lambda i: (jnp.where(i < nl, i % tpe, tpe), 0))
    return _hosted_call(
        body, (x2, ctx2, modv, g_attn, w_in, g_q, g_kv, w_uq, w_ukv, cos_t, sin_t), name="pre_fwd", grid=(nl + nc,),
        in_specs=[
            pl.BlockSpec((tm, d), lambda i: (jnp.minimum(i, nl - 1), 0)),
            pl.BlockSpec((tm, d), lambda i: (jnp.maximum(i - nl, 0), 0)),
            pl.BlockSpec((1, 8, d), lambda i: (jnp.minimum(i // tpe, nex), 0, 0)),
            _full((1, d)), _full(w_in.shape), _full((1, Q_LORA)), _full((1, KV_LORA)),
            _full(w_uq.shape), _full(w_ukv.shape), tab, tab,
        ],
        out_specs=[tile(w) for w in widths],
        out_shape=[jax.ShapeDtypeStruct((n_all, w), dt) for w, dt in zip(widths, dtypes)],
        sem=("parallel",), rider=rider)


def _post(yret, ymla, x2, tgt2, modv, g_ffn, g_fin, w_out, w_ff1, w_ff2a, w_ff2b, *, seq, tm):
    t_lat, d = x2.shape
    nl = t_lat // tm
    tpe = seq // tm
    nex = t_lat // seq
    n_slab = w_ff1.shape[0]
    fs = w_ff1.shape[2]
    fh = w_ff2a.shape[1]

    def body(yr_ref, ym_ref, x_ref, t_ref, mod_ref, gf_ref, gl_ref, wo_ref, w1_ref, w2a_ref, w2b_ref,
             mix_ref, a_ref, du_ref, h2_ref, df_ref, dmo_ref, dmix_ref, dxm_ref, st_ref, ru_ref):
        i = pl.program_id(0)
        gt_a = mod_ref[0, 2:3, :]
        sh_f = mod_ref[0, 3:4, :]
        sc_f = mod_ref[0, 4:5, :]
        gt_f = mod_ref[0, 5:6, :]
        g_ffn_v = gf_ref[...]
        g_fin_v = gl_ref[...]
        yr = yr_ref[...]
        ym = ym_ref[...]
        mix_ref[:, 0:512] = yr
        mix_ref[:, 512:1024] = ym
        op = _dot(yr, wo_ref[0:512, :]) + _dot(ym, wo_ref[512:1024, :])
        x_mid = x_ref[...] + gt_a * op
        r2 = lax.rsqrt(jnp.mean(x_mid * x_mid, axis=-1, keepdims=True) + EPS)
        xh2 = x_mid * r2
        h2b = (xh2 * g_ffn_v * (1.0 + sc_f) + sh_f).astype(BF16)
        h2_ref[...] = h2b
        f = jnp.zeros((tm, d), F32)
        for s in range(n_slab):
            ru = jnp.maximum(_dot(h2b, w1_ref[s]), 0.0)
            ru_ref[:, s * fs:(s + 1) * fs] = ru
            ab = (ru * ru).astype(BF16)
            a_ref[:, s * fs:(s + 1) * fs] = ab
            f = f + _dot(ab[:, 0:fh], w2a_ref[s]) + _dot(ab[:, fh:fs], w2b_ref[s])
        x_out = x_mid + gt_f * f
        r3 = lax.rsqrt(jnp.mean(x_out * x_out, axis=-1, keepdims=True) + EPS)
        xh3 = x_out * r3
        err = xh3 * g_fin_v - t_ref[...]
        dy = err * (1.0 / d)
        dxh3 = dy * g_fin_v
        dx_out = r3 * (dxh3 - xh3 * jnp.mean(dxh3 * xh3, axis=-1, keepdims=True))
        dfb = (dx_out * gt_f).astype(BF16)
        df_ref[...] = dfb
        dh2 = jnp.zeros((tm, d), F32)
        for s in range(n_slab):
            da = jnp.concatenate([_dot_nt(dfb, w2a_ref[s]), _dot_nt(dfb, w2b_ref[s])], axis=1)
            dub = (da * (2.0 * ru_ref[:, s * fs:(s + 1) * fs])).astype(BF16)
            du_ref[:, s * fs:(s + 1) * fs] = dub
            dh2 = dh2 + _dot_nt(dub, w1_ref[s])
        dxh2 = dh2 * (1.0 + sc_f) * g_ffn_v
        dx_mid = dx_out + r2 * (dxh2 - xh2 * jnp.mean(dxh2 * xh2, axis=-1, keepdims=True))
        dxm_ref[...] = dx_mid
        dmob = (dx_mid * gt_a).astype(BF16)
        dmo_ref[...] = dmob
        dmix_ref[...] = _dot_nt(dmob, wo_ref[...]).astype(BF16)

        def rsum(v):
            return jnp.sum(v, axis=0, keepdims=True)

        stats = jnp.concatenate([
            rsum(dh2), rsum(dh2 * xh2 * g_ffn_v), rsum(dx_out * f), rsum(dx_mid * op),
            rsum(dh2 * (1.0 + sc_f) * xh2), rsum(dy * xh3), rsum(err * err), jnp.zeros((1, d), F32)], axis=0)

        @pl.when(i % tpe == 0)
        def _():
            st_ref[0] = stats

        @pl.when(i % tpe != 0)
        def _():
            st_ref[0] += stats

    def tile(width):
        return pl.BlockSpec((tm, width), lambda i: (i, 0))

    widths = (d, D_FF, D_FF, d, d, d, d, d)
    dtypes = (BF16, BF16, BF16, BF16, BF16, BF16, BF16, F32)
    const = pl.Buffered(1)
    return pl.pallas_call(
        body, name="post", grid=(nl,),
        in_specs=[
            tile(512), tile(512), tile(d), tile(d),
            pl.BlockSpec((1, 8, d), lambda i: (i // tpe, 0, 0)),
            _full((1, d)), _full((1, d)),
            pl.BlockSpec(w_out.shape, lambda i: (0, 0), pipeline_mode=const),
            pl.BlockSpec(w_ff1.shape, lambda i: (0, 0, 0), pipeline_mode=const),
            pl.BlockSpec(w_ff2a.shape, lambda i: (0, 0, 0), pipeline_mode=const),
            pl.BlockSpec(w_ff2b.shape, lambda i: (0, 0, 0), pipeline_mode=const),
        ],
        out_specs=[tile(w) for w in widths] + [pl.BlockSpec((1, 8, d), lambda i: (i // tpe, 0, 0))],
        out_shape=[jax.ShapeDtypeStruct((t_lat, w), dt) for w, dt in zip(widths, dtypes)]
        + [jax.ShapeDtypeStruct((nex, 8, d), F32)],
        scratch_shapes=[pltpu.VMEM((tm, D_FF), F32)],
        compiler_params=_params(("arbitrary",), VMEM_LIMIT),
    )(yret, ymla, x2, tgt2, modv, g_ffn, g_fin, w_out, w_ff1, w_ff2a, w_ff2b)


def _pre_bwd(x2, ctx2, modv, g_attn, pg, drq, drk, dkc_r, drv, dvc_r, drg, dq_m, dkl, dkc, dvl, dvc, dxm,
             w_in, g_q, g_kv, w_uq, w_ukv, cos_t, sin_t, *, seq, tm, rider=None):
    t_lat, d = x2.shape
    t_ctx = ctx2.shape[0]
    nl, nc = t_lat // tm, t_ctx // tm
    n_all = t_lat + t_ctx
    tpe = seq // tm
    nex = t_lat // seq

    def body(x_ref, c_ref, mod_ref, g_ref, pg_ref, drq_ref, drk_ref, dkcr_ref, drv_ref, dvcr_ref, drg_ref,
             dq_ref, dkl_ref, dkc_ref, dvl_ref, dvc_ref, dxm_ref, win_ref, gq_ref, gkv_ref, wuq_ref, wukv_ref,
             cos_ref, sin_ref, dpb_ref, dqf_ref, dkvf_ref, gx_ref, st_ref):
        i = pl.program_id(0)
        lat = i < nl
        latf = lat.astype(F32)
        cos = cos_ref[...]
        sin = sin_ref[...]
        cos1 = cos[:, 0:LANES]
        sin1 = sin[:, 0:LANES]
        d_rq = _rope_t(drq_ref[...] * latf, cos, sin)
        d_rk = _rope_t(jnp.where(lat, drk_ref[...], dkcr_ref[...]), cos, sin) * (RET_DK ** -0.5)
        d_rv = jnp.where(lat, drv_ref[...], dvcr_ref[...])
        d_rg = drg_ref[...] * latf
        dq_all = dq_ref[...] * (latf * MLA_SCALE)
        dk_all = jnp.where(lat, dkl_ref[...], dkc_ref[...])
        dv_all = jnp.where(lat, dvl_ref[...], dvc_ref[...])
        dnq = jnp.zeros((tm, Q_LORA), F32)
        dnkv = jnp.zeros((tm, KV_LORA), F32)
        dkpe = jnp.zeros((tm, LANES), F32)
        for hd in range(HEADS):
            o = hd * MLA_HEAD
            dqh = jnp.concatenate([dq_all[:, o:o + 128], _rope_t(dq_all[:, o + 128:o + 256], cos1, sin1)],
                                  axis=1).astype(BF16)
            dqf_ref[:, o:o + 256] = dqh
            dnq = dnq + _dot(dqh, wuq_ref[hd])
            dkpe = dkpe + dk_all[:, o + 128:o + 256]
            dkvh = jnp.concatenate([dk_all[:, o:o + 128], dv_all[:, hd * 128:(hd + 1) * 128]], axis=1).astype(BF16)
            dkvf_ref[:, o:o + 256] = dkvh
            dnkv = dnkv + _dot_nt(dkvh, wukv_ref[hd])
        d_kpe = _rope_t(dkpe, cos1, sin1)
        pgv = pg_ref[...]
        cq = pgv[:, 512:896]
        ckv = pgv[:, 896:1152]
        rq_ = lax.rsqrt(jnp.mean(cq * cq, axis=-1, keepdims=True) + EPS)
        cqh = cq * rq_
        dcqh = dnq * gq_ref[...]
        d_cq = rq_ * (dcqh - cqh * jnp.mean(dcqh * cqh, axis=-1, keepdims=True))
        rkv_ = lax.rsqrt(jnp.mean(ckv * ckv, axis=-1, keepdims=True) + EPS)
        ckvh = ckv * rkv_
        dckvh = dnkv * gkv_ref[...]
        d_ckv = rkv_ * (dckvh - ckvh * jnp.mean(dckvh * ckvh, axis=-1, keepdims=True))
        dpb = jnp.concatenate([d_rq, d_rk, d_rv, d_rg, d_cq, d_ckv, d_kpe], axis=1).astype(BF16)
        dpb_ref[...] = dpb
        dh = _dot(dpb, win_ref[...])
        xt = jnp.where(lat, x_ref[...], c_ref[...])
        sc = mod_ref[0, 1:2, :]
        g = g_ref[...]
        r = lax.rsqrt(jnp.mean(xt * xt, axis=-1, keepdims=True) + EPS)
        xh = xt * r
        dxh = dh * (1.0 + sc) * g
        dx = r * (dxh - xh * jnp.mean(dxh * xh, axis=-1, keepdims=True))

        @pl.when(lat)
        def _():
            gx_ref[...] = dxm_ref[...] + dx

        def rsum(v):
            return jnp.sum(v, axis=0, keepdims=True)

        def widen(v):
            return jnp.concatenate([v, jnp.zeros((1, d - v.shape[1]), F32)], axis=1)

        stats = jnp.concatenate([
            rsum(dh), rsum(dh * xh * g), rsum(dh * (1.0 + sc) * xh), widen(rsum(dnq * cqh)), widen(rsum(dnkv * ckvh)),
            jnp.zeros((3, d), F32)], axis=0)
        first = jnp.logical_or(jnp.logical_and(lat, i % tpe == 0), i == nl)

        @pl.when(first)
        def _():
            st_ref[0] = stats

        @pl.when(jnp.logical_not(first))
        def _():
            st_ref[0] += stats

    def lat_tile(width):
        return pl.BlockSpec((tm, width), lambda i: (jnp.minimum(i, nl - 1), 0))

    def ctx_tile(width):
        return pl.BlockSpec((tm, width), lambda i: (jnp.maximum(i - nl, 0), 0))

    def tile(width):
        return pl.BlockSpec((tm, width), lambda i: (i, 0))

    tab = pl.BlockSpec((tm, 256), lambda i: (jnp.where(i < nl, i % tpe, tpe), 0))
    ex = pl.BlockSpec((1, 8, d), lambda i: (jnp.minimum(i // tpe, nex), 0, 0))
    return _hosted_call(
        body, (x2, ctx2, modv, g_attn, pg, drq, drk, dkc_r, drv, dvc_r, drg, dq_m, dkl, dkc, dvl, dvc, dxm,
               w_in, g_q, g_kv, w_uq, w_ukv, cos_t, sin_t), name="pre_bwd", grid=(nl + nc,),
        in_specs=[
            lat_tile(d), ctx_tile(d), ex, _full((1, d)), tile(PG_COLS),
            lat_tile(256), lat_tile(256), ctx_tile(256), lat_tile(512), ctx_tile(512), lat_tile(512),
            lat_tile(1024), lat_tile(1024), ctx_tile(1024), lat_tile(512), ctx_tile(512), lat_tile(d),
            _once(w_in.shape), _full((1, Q_LORA)), _full((1, KV_LORA)), _once(w_uq.shape), _once(w_ukv.shape),
            tab, tab,
        ],
        out_specs=[tile(IN_PAD), tile(1024), tile(1024), lat_tile(d), ex],
        out_shape=[
            jax.ShapeDtypeStruct((n_all, IN_PAD), BF16), jax.ShapeDtypeStruct((n_all, 1024), BF16),
            jax.ShapeDtypeStruct((n_all, 1024), BF16), jax.ShapeDtypeStruct((t_lat, d), F32),
            jax.ShapeDtypeStruct((nex + 1, 8, d), F32),
        ],
        sem=("arbitrary",), rider=rider)


MLA_SCALE = 1.0 / math.sqrt(MLA_NOPE + MLA_ROPE)
KEY_BLOCK = 1024


def _mla_specs(t_lat, seq, ctx_len, tq, heads=1):
    nqt = seq // tq
    cb = t_lat // ctx_len
    q = pl.BlockSpec((tq, heads * MLA_HEAD), lambda b, h, j: (b * nqt + j, h))
    kl = pl.BlockSpec((seq, heads * MLA_HEAD), lambda b, h, j: (b, h))
    kc = pl.BlockSpec((ctx_len, heads * MLA_HEAD), lambda b, h, j: (cb + b, h))
    vl = pl.BlockSpec((seq, heads * 128), lambda b, h, j: (b, h))
    vc = pl.BlockSpec((ctx_len, heads * 128), lambda b, h, j: (cb + b, h))
    o = pl.BlockSpec((tq, heads * 128), lambda b, h, j: (b * nqt + j, h))
    return q, kl, kc, vl, vc, o


FWD_HEADS = 2
BWD_HEADS = 1


def _mla_fwd(q, k, v, *, t_lat, seq, ctx_len, tq, rider=None):
    nex = t_lat // seq

    def body(q_ref, kl_ref, kc_ref, vl_ref, vc_ref, o_ref, lse_ref):
        for hh in range(FWD_HEADS):
            wide = slice(hh * MLA_HEAD, (hh + 1) * MLA_HEAD)
            cols = slice(hh * 128, (hh + 1) * 128)
            qb = q_ref[:, wide]
            s = _dot_nt(qb, kl_ref[:, wide])
            sc = _dot_nt(qb, kc_ref[:, wide])
            m = jnp.maximum(jnp.max(s, axis=-1, keepdims=True), jnp.max(sc, axis=-1, keepdims=True))
            p = jnp.exp(s - m)
            pc = jnp.exp(sc - m)
            total = jnp.sum(p, axis=-1, keepdims=True) + jnp.sum(pc, axis=-1, keepdims=True)
            o = _dot(p.astype(BF16), vl_ref[:, cols]) + _dot(pc.astype(BF16), vc_ref[:, cols])
            o_ref[:, cols] = (o * (1.0 / total)).astype(BF16)
            lse_ref[:, cols] = jnp.broadcast_to(m + jnp.log(total), (tq, 128))

    qs, kl, kc, vl, vc, os_ = _mla_specs(t_lat, seq, ctx_len, tq, FWD_HEADS)
    return _hosted_call(
        body, (q, k, k, v, v), name="mla_fwd", grid=(nex, HEADS // FWD_HEADS, seq // tq),
        in_specs=[qs, kl, kc, vl, vc], out_specs=[os_, os_],
        out_shape=[jax.ShapeDtypeStruct((t_lat, HEADS * 128), BF16), jax.ShapeDtypeStruct((t_lat, HEADS * 128), F32)],
        sem=("parallel", "parallel", "arbitrary"), rider=rider)


def _mla_bwd(q, k, v, ymla, lse, dmix, *, t_lat, seq, ctx_len, tq, rider=None):
    nex = t_lat // seq
    nqt = seq // tq
    t_ctx = nex * ctx_len
    kb = min(KEY_BLOCK, seq)

    def body(q_ref, kl_ref, kc_ref, vl_ref, vc_ref, o_ref, lse_ref, do_ref, dq_ref, dkl_out, dkc_out, dvl_out, dvc_out,
             dkl_ref, dkc_ref, dvl_ref, dvc_ref):
        j = pl.program_id(2)

        @pl.when(j == 0)
        def _():
            dkl_ref[...] = jnp.zeros(dkl_ref.shape, F32)
            dkc_ref[...] = jnp.zeros(dkc_ref.shape, F32)
            dvl_ref[...] = jnp.zeros(dvl_ref.shape, F32)
            dvc_ref[...] = jnp.zeros(dvc_ref.shape, F32)

        for hh in range(BWD_HEADS):
            wide = slice(hh * MLA_HEAD, (hh + 1) * MLA_HEAD)
            cols = slice(hh * 128, (hh + 1) * 128)
            qb = q_ref[:, wide]
            dob = do_ref[:, cols]
            delta = jnp.sum(dob.astype(F32) * o_ref[:, cols].astype(F32), axis=-1, keepdims=True)
            lse_row = lse_ref[:, hh * 128:hh * 128 + 1]

            def block(k_ref, v_ref, dk_ref, dv_ref, rows):
                kbl = k_ref[rows, wide]
                vbl = v_ref[rows, cols]
                p = jnp.exp(_dot_nt(qb, kbl) - lse_row)
                ds = (p * (_dot_nt(dob, vbl) - delta)).astype(BF16)
                dk_ref[rows, wide] += _dot_tn(ds, qb)
                dv_ref[rows, cols] += _dot_tn(p.astype(BF16), dob)
                return _dot(ds, kbl)

            dq = block(kc_ref, vc_ref, dkc_ref, dvc_ref, pl.ds(0, ctx_len))
            for i in range(seq // kb):
                dq = dq + block(kl_ref, vl_ref, dkl_ref, dvl_ref, pl.ds(i * kb, kb))
            dq_ref[:, wide] = dq.astype(BF16)

        @pl.when(j == nqt - 1)
        def _():
            dkl_out[...] = dkl_ref[...].astype(BF16)
            dkc_out[...] = dkc_ref[...].astype(BF16)
            dvl_out[...] = dvl_ref[...].astype(BF16)
            dvc_out[...] = dvc_ref[...].astype(BF16)

    g = BWD_HEADS
    qs, kl, kc, vl, vc, os_ = _mla_specs(t_lat, seq, ctx_len, tq, g)
    do_spec = pl.BlockSpec((tq, g * 128), lambda b, h, j: (b * nqt + j, HEADS // g + h))
    key_blocks = [(seq, g * MLA_HEAD), (ctx_len, g * MLA_HEAD), (seq, g * 128), (ctx_len, g * 128)]
    return _hosted_call(
        body, (q, k, k, v, v, ymla, lse, dmix), name="mla_bwd", grid=(nex, HEADS // g, nqt),
        in_specs=[qs, kl, kc, vl, vc, os_, os_, do_spec],
        out_specs=[qs] + [pl.BlockSpec(blk, lambda b, h, j: (b, h)) for blk in key_blocks],
        out_shape=[
            jax.ShapeDtypeStruct((t_lat, HEADS * MLA_HEAD), BF16),
            jax.ShapeDtypeStruct((t_lat, HEADS * MLA_HEAD), BF16),
            jax.ShapeDtypeStruct((t_ctx, HEADS * MLA_HEAD), BF16),
            jax.ShapeDtypeStruct((t_lat, HEADS * 128), BF16),
            jax.ShapeDtypeStruct((t_ctx, HEADS * 128), BF16),
        ],
        scratch_shapes=[pltpu.VMEM(blk, F32) for blk in key_blocks],
        sem=("parallel", "parallel", "arbitrary"), rider=rider)


def _decay_terms(lg, chunk, forward):
    ii = lax.broadcasted_iota(jnp.int32, (chunk, chunk), 0)
    jj = lax.broadcasted_iota(jnp.int32, (chunk, chunk), 1)
    diff = (ii - jj) if forward else (jj - ii)
    dist = jnp.maximum(diff, 0).astype(F32)
    dmat = jnp.where(diff >= 0, jnp.exp(lg * dist), 0.0)
    pos = lax.broadcasted_iota(jnp.int32, (chunk, 1), 0).astype(F32)
    if forward:
        e_q = pos + 1.0
        e_k = (chunk - 1.0) - pos
    else:
        e_q = chunk - pos
        e_k = pos
    wq = jnp.exp(lg * e_q)
    wk = jnp.exp(lg * e_k)
    cd = jnp.exp(jnp.full((1, 1), lg * chunk, F32))
    return dmat, dist, wq, wk, e_q, e_k, cd


def _ctx_weights(lg, ctx_len, forward):
    pos = lax.broadcasted_iota(jnp.int32, (ctx_len, 1), 0).astype(F32)
    e = ((ctx_len - 1.0) - pos) if forward else pos
    return jnp.exp(lg * e), e


def _pair_specs(t_lat, seq, ctx_len):
    cb = t_lat // ctx_len
    qk = pl.BlockSpec((seq, 128), lambda b, p: (b, p))
    v = pl.BlockSpec((seq, 256), lambda b, p: (b, p))
    kc = pl.BlockSpec((ctx_len, 128), lambda b, p: (cb + b, p))
    vc = pl.BlockSpec((ctx_len, 256), lambda b, p: (cb + b, p))
    return qk, v, kc, vc


def _lane_masks():
    lane = lax.broadcasted_iota(jnp.int32, (1, 128), 1)
    return [(lane // RET_DK) == hh for hh in (0, 1)]


def _ret_fwd_pair(rq, rk, rv, pg, lg, g_ret, *, t_lat, seq, ctx_len, chunk, rider=None):
    nex = t_lat // seq
    n_chunk = seq // chunk

    def body(q_ref, k_ref, v_ref, kc_ref, vc_ref, rg_ref, lg_ref, g_ref, y_ref, o_ref):
        pair = pl.program_id(1)
        masks = _lane_masks()
        kcf = kc_ref[...].astype(F32)
        chains = [(forward, hh) for forward in (True, False) for hh in (0, 1)]
        terms, s0 = [], []
        for forward, hh in chains:
            lgd = lg_ref[0 if forward else 1, 2 * pair + hh]
            terms.append(_decay_terms(lgd, chunk, forward))
            wc, _ = _ctx_weights(lgd, ctx_len, forward)
            s0.append(_dot_tn((jnp.where(masks[hh], kcf, 0.0) * wc).astype(BF16), vc_ref[:, hh * 128:(hh + 1) * 128]))
        both = [terms[hh][0] + terms[2 + hh][0] for hh in (0, 1)]
        o_ref[...] = jnp.zeros(o_ref.shape, F32)

        def step(t, states):
            new = [None] * 4
            for forward in (True, False):
                n = t if forward else n_chunk - 1 - t
                sl = pl.ds(pl.multiple_of(n * chunk, chunk), chunk)
                qb = q_ref[sl, :]
                kf_all = k_ref[sl, :].astype(F32)
                for hh in (0, 1):
                    c = (0 if forward else 2) + hh
                    _, _, wq, wk, _, _, cd = terms[c]
                    cols = slice(hh * 128, (hh + 1) * 128)
                    qm = jnp.where(masks[hh], qb, jnp.zeros((), BF16))
                    kf = jnp.where(masks[hh], kf_all, 0.0)
                    vb = v_ref[sl, cols]
                    o = wq * _dot(qm, states[c].astype(BF16))
                    if forward:
                        o = o + _dot((_dot_nt(qm, kf.astype(BF16)) * both[hh]).astype(BF16), vb)
                    o_ref[sl, cols] += o
                    new[c] = cd * states[c] + _dot_tn((kf * wk).astype(BF16), vb)
            return tuple(new)

        lax.fori_loop(0, n_chunk, step, tuple(s0))

        def norm_step(n, carry):
            sl = pl.ds(pl.multiple_of(n * chunk, chunk), chunk)
            for hh in (0, 1):
                cols = slice(hh * 128, (hh + 1) * 128)
                o = o_ref[sl, cols]
                mu = jnp.mean(o, axis=-1, keepdims=True)
                oc = o - mu
                var = jnp.mean(oc * oc, axis=-1, keepdims=True)
                rg = rg_ref[sl, cols]
                y_ref[sl, cols] = (oc * lax.rsqrt(var + EPS) * g_ref[:, cols] * (rg / (1.0 + jnp.exp(-rg)))).astype(BF16)
            return carry

        lax.fori_loop(0, n_chunk, norm_step, 0)

    qk, v, kc, vc = _pair_specs(t_lat, seq, ctx_len)
    return _hosted_call(
        body, (rq, rk, rv, rk, rv, pg, lg, g_ret), name="ret_fwd", grid=(nex, HEADS // 2),
        in_specs=[qk, qk, v, kc, vc, v, pl.BlockSpec(memory_space=pltpu.SMEM), pl.BlockSpec((1, 256), lambda b, p: (0, p))],
        out_specs=[v, v],
        out_shape=[jax.ShapeDtypeStruct((t_lat, HEADS * RET_DV), BF16), jax.ShapeDtypeStruct((t_lat, HEADS * RET_DV), F32)],
        sem=("parallel", "arbitrary"), rider=rider)


def _ret_bwd_pair(rq, rk, rv, pg, osum, dmix, lg, g_ret, *, t_lat, seq, ctx_len, chunk, rider=None):
    nex = t_lat // seq
    n_chunk = seq // chunk
    t_ctx = nex * ctx_len

    def body(q_ref, k_ref, v_ref, kc_ref, vc_ref, rg_ref, o_ref, dy_ref, lg_ref, g_ref,
             dq_out, dk_out, dv_out, dkc_ref, dvc_ref, drg_ref, st_ref, do_s, s_st, dq_ref, dk_ref, dv_ref):
        pair = pl.program_id(1)
        masks = _lane_masks()
        kcf = kc_ref[...].astype(F32)

        def norm_step(n, dgains):
            sl = pl.ds(pl.multiple_of(n * chunk, chunk), chunk)
            out = []
            for hh in (0, 1):
                cols = slice(hh * 128, (hh + 1) * 128)
                gain = g_ref[:, cols]
                o = o_ref[sl, cols]
                mu = jnp.mean(o, axis=-1, keepdims=True)
                oc = o - mu
                rstd = lax.rsqrt(jnp.mean(oc * oc, axis=-1, keepdims=True) + EPS)
                ohat = oc * rstd
                rg = rg_ref[sl, cols]
                sg = 1.0 / (1.0 + jnp.exp(-rg))
                dy = dy_ref[sl, cols].astype(F32)
                don = dy * (rg * sg)
                drg_ref[sl, cols] = (dy * (ohat * gain) * (sg * (1.0 + rg * (1.0 - sg)))).astype(BF16)
                dohat = don * gain
                do_s[sl, cols] = rstd * (dohat - jnp.mean(dohat, axis=-1, keepdims=True)
                                         - ohat * jnp.mean(dohat * ohat, axis=-1, keepdims=True))
                out.append(dgains[hh] + jnp.sum(don * ohat, axis=0, keepdims=True))
            return tuple(out)

        zero_row = jnp.zeros((1, 128), F32)
        dgains = lax.fori_loop(0, n_chunk, norm_step, (zero_row, zero_row))
        dq_ref[...] = jnp.zeros(dq_ref.shape, F32)
        dk_ref[...] = jnp.zeros(dk_ref.shape, F32)
        dv_ref[...] = jnp.zeros(dv_ref.shape, F32)

        chains = [(forward, hh) for forward in (True, False) for hh in (0, 1)]
        terms, ctxw, s0 = [], [], []
        for forward, hh in chains:
            lgd = lg_ref[0 if forward else 1, 2 * pair + hh]
            terms.append(_decay_terms(lgd, chunk, forward))
            ctxw.append(_ctx_weights(lgd, ctx_len, forward))
            s0.append(_dot_tn((jnp.where(masks[hh], kcf, 0.0) * ctxw[-1][0]).astype(BF16), vc_ref[:, hh * 128:(hh + 1) * 128]))

        def chunk_at(t, ascending):
            n = t if ascending else n_chunk - 1 - t
            return n, pl.ds(pl.multiple_of(n * chunk, chunk), chunk)

        def state_step(t, states):
            new = []
            for c, (forward, hh) in enumerate(chains):
                n, sl = chunk_at(t, forward)
                wk, cd = terms[c][3], terms[c][6]
                s_st[c, n] = states[c]
                kf = jnp.where(masks[hh], k_ref[sl, :].astype(F32), 0.0)
                new.append(cd * states[c] + _dot_tn((kf * wk).astype(BF16), v_ref[sl, hh * 128:(hh + 1) * 128]))
            return tuple(new)

        lax.fori_loop(0, n_chunk, state_step, tuple(s0))

        both = [terms[hh][0] + terms[2 + hh][0] for hh in (0, 1)]

        def grad_step(t, carry):
            out = [None] * len(chains)
            in_chunk_b = [None, None]
            for forward in (True, False):
                n, sl = chunk_at(t, not forward)
                qb = q_ref[sl, :]
                kf_all = k_ref[sl, :].astype(F32)
                dq_sum = jnp.zeros((chunk, 128), F32)
                dk_sum = jnp.zeros((chunk, 128), F32)
                for hh in (0, 1):
                    c = (0 if forward else 2) + hh
                    g_next, dlg = carry[c]
                    dmat, dist, wq, wk, e_q, e_k, cd = terms[c]
                    cols = slice(hh * 128, (hh + 1) * 128)
                    qm = jnp.where(masks[hh], qb, jnp.zeros((), BF16))
                    kf = jnp.where(masks[hh], kf_all, 0.0)
                    kb = kf.astype(BF16)
                    vb = v_ref[sl, cols]
                    do = do_s[sl, cols]
                    dob = do.astype(BF16)
                    s_n = s_st[c, n]
                    s_nb = s_n.astype(BF16)
                    gb = g_next.astype(BF16)
                    dk_cross = wk * _dot_nt(vb, gb)
                    dv = _dot((kf * wk).astype(BF16), gb)
                    o_cross = wq * _dot(qm, s_nb)
                    dq_sum = dq_sum + wq * _dot_nt(dob, s_nb)
                    dk_sum = dk_sum + dk_cross
                    dlg = (dlg + chunk * cd * jnp.sum(g_next * s_n, keepdims=True)
                           + jnp.sum(e_k * jnp.sum(kf * dk_cross, axis=-1, keepdims=True), keepdims=True)
                           + jnp.sum(e_q * jnp.sum(o_cross * do, axis=-1, keepdims=True), keepdims=True))
                    if forward:
                        a_raw = _dot_nt(qm, kb)
                        da_raw = _dot_nt(dob, vb)
                        prod = a_raw * da_raw
                        dlg = dlg + jnp.sum(dist * dmat * prod, keepdims=True)
                        in_chunk_b[hh] = jnp.sum(terms[2 + hh][1] * terms[2 + hh][0] * prod, keepdims=True)
                        dab = (da_raw * both[hh]).astype(BF16)
                        dq_sum = dq_sum + _dot(dab, kb)
                        dk_sum = dk_sum + _dot_tn(dab, qm)
                        dv = dv + _dot_tn((a_raw * both[hh]).astype(BF16), dob)
                    else:
                        dlg = dlg + in_chunk_b[hh]
                    dv_ref[sl, cols] += dv
                    out[c] = (cd * g_next + _dot_tn((qm.astype(F32) * wq).astype(BF16), dob), dlg)
                dq_ref[sl, :] += dq_sum
                dk_ref[sl, :] += dk_sum
            return tuple(out)

        zero = (jnp.zeros((128, 128), F32), jnp.zeros((1, 1), F32))
        res = lax.fori_loop(0, n_chunk, grad_step, (zero,) * len(chains))
        dkc_sum = jnp.zeros((ctx_len, 128), F32)
        dvc = [jnp.zeros((ctx_len, 128), F32)] * 2
        dlgs = []
        for c, (forward, hh) in enumerate(chains):
            ds0, dlg = res[c]
            wc, e_c = ctxw[c]
            kcm = jnp.where(masks[hh], kcf, 0.0)
            ds0b = ds0.astype(BF16)
            dkc_part = wc * _dot_nt(vc_ref[:, hh * 128:(hh + 1) * 128], ds0b)
            dkc_sum = dkc_sum + dkc_part
            dvc[hh] = dvc[hh] + _dot((kcm * wc).astype(BF16), ds0b)
            dlgs.append(dlg + jnp.sum(e_c * jnp.sum(kcm * dkc_part, axis=-1, keepdims=True), keepdims=True))
        dq_out[...] = dq_ref[...].astype(BF16)
        dk_out[...] = dk_ref[...].astype(BF16)
        dv_out[...] = dv_ref[...].astype(BF16)
        dkc_ref[...] = dkc_sum
        for hh in (0, 1):
            cols = slice(hh * 128, (hh + 1) * 128)
            dvc_ref[:, cols] = dvc[hh]
            st_ref[0, :, cols] = jnp.concatenate([
                dgains[hh], jnp.broadcast_to(dlgs[hh], (1, 128)), jnp.broadcast_to(dlgs[2 + hh], (1, 128)),
                jnp.zeros((5, 128), F32)], axis=0)

    qk, v, kc, vc = _pair_specs(t_lat, seq, ctx_len)
    return _hosted_call(
        body, (rq, rk, rv, rk, rv, pg, osum, dmix, lg, g_ret), name="ret_bwd", grid=(nex, HEADS // 2),
        in_specs=[qk, qk, v, kc, vc, v, v, v, pl.BlockSpec(memory_space=pltpu.SMEM),
                  pl.BlockSpec((1, 256), lambda b, p: (0, p))],
        out_specs=[
            qk, qk, v,
            pl.BlockSpec((ctx_len, 128), lambda b, p: (b, p)),
            pl.BlockSpec((ctx_len, 256), lambda b, p: (b, p)),
            v,
            pl.BlockSpec((1, 8, 256), lambda b, p: (b, 0, p)),
        ],
        out_shape=[
            jax.ShapeDtypeStruct((t_lat, 256), BF16), jax.ShapeDtypeStruct((t_lat, 256), BF16),
            jax.ShapeDtypeStruct((t_lat, 512), BF16), jax.ShapeDtypeStruct((t_ctx, 256), F32),
            jax.ShapeDtypeStruct((t_ctx, 512), F32), jax.ShapeDtypeStruct((t_lat, 512), BF16),
            jax.ShapeDtypeStruct((nex, 8, 512), F32),
        ],
        scratch_shapes=[pltpu.VMEM((seq, 256), F32), pltpu.VMEM((4, n_chunk, 128, 128), F32),
                        pltpu.VMEM((seq, 128), F32), pltpu.VMEM((seq, 128), F32), pltpu.VMEM((seq, 256), F32)],
        sem=("parallel", "arbitrary"), rider=rider)


def _matmul_tn(a, b, *, bm, bn, bk, chip_major, name, out_dtype=F32, rider=None):
    tk, m = a.shape
    n = b.shape[1]
    slab = n // N_CHIPS
    per_block = bn // slab if chip_major else 1
    bk = max(c for c in range(LANES, min(bk, tk) + 1, LANES) if tk % c == 0)
    nk = tk // bk
    blk = (per_block, bm, slab) if chip_major else (bm, bn)

    def body(a_ref, b_ref, o_ref, acc_ref):
        k = pl.program_id(2)
        if chip_major:
            parts = [_dot_tn(a_ref[...], b_ref[:, s * slab:(s + 1) * slab]) for s in range(per_block)]
        else:
            parts = [_dot_tn(a_ref[...], b_ref[...])]

        @pl.when(k == 0)
        def _():
            for s, part in enumerate(parts):
                if chip_major:
                    acc_ref[s] = part
                else:
                    acc_ref[...] = part

        @pl.when(k > 0)
        def _():
            for s, part in enumerate(parts):
                if chip_major:
                    acc_ref[s] += part
                else:
                    acc_ref[...] += part

        @pl.when(k == nk - 1)
        def _():
            o_ref[...] = acc_ref[...].astype(out_dtype)

    if chip_major:
        out_spec = pl.BlockSpec(blk, lambda i, j, k: (j, i, 0))
        out_shape = jax.ShapeDtypeStruct((N_CHIPS, m, slab), out_dtype)
    else:
        out_spec = pl.BlockSpec(blk, lambda i, j, k: (i, j))
        out_shape = jax.ShapeDtypeStruct((m, n), out_dtype)
    (out,), carried = _hosted_call(
        body, (a, b), name=name, grid=(m // bm, n // bn, nk),
        in_specs=[pl.BlockSpec((bk, bm), lambda i, j, k: (k, i)), pl.BlockSpec((bk, bn), lambda i, j, k: (k, j))],
        out_specs=[out_spec], out_shape=[out_shape], scratch_shapes=[pltpu.VMEM(blk, F32)],
        sem=("parallel", "parallel", "arbitrary"), rider=rider)
    return out if rider is None else (out, carried)


_LATE = ("w_out", "w_ff1", "w_ff2")
_EARLY = ("w_in", "w_uq", "w_ukv")


def _local_step(x, ctx, tgt, modv, lg, g_attn, g_ffn, g_fin, g_ret, g_q, g_kv, w_in, w_uq, w_ukv, late, place=None,
                *, tm=256, tq=256, chunk=256):
    nex, seq, d = x.shape
    ctx_len = ctx.shape[1]
    t_lat = nex * seq
    tm = min(tm, seq)
    x2 = x.reshape(t_lat, d)
    ctx2 = ctx.reshape(nex * ctx_len, d)
    tgt2 = tgt.reshape(t_lat, d)
    tm_fwd = min(2 * tm, seq)
    cos_t, sin_t = _rope_tables(seq, tm)
    dims = dict(t_lat=t_lat, seq=seq, ctx_len=ctx_len)
    alone = place is None

    (hb, pg, rq, rk, rv, nq, nkv, q, k, v), crossed_a = _pre_fwd(
        x2, ctx2, modv, g_attn, w_in, g_q, g_kv, w_uq, w_ukv, *_rope_tables(seq, tm_fwd), seq=seq, tm=tm_fwd,
        rider=None if alone else _gather_ici_rider([late[2]]))
    (yret, osum), got = _ret_fwd_pair(
        rq, rk, rv, pg, lg, g_ret, chunk=min(2 * chunk, seq), **dims,
        rider=None if alone else _merge_riders(_gather_d2d_rider(crossed_a), _gather_ici_rider([late[3]])))
    (ymla, lse), got_rest = _mla_fwd(
        q, k, v, tq=tq, **dims,
        rider=None if alone else _merge_riders(_gather_rider([late[0], late[1]], staged=True), _gather_d2d_rider(got[1:])))
    w_out, w_ff1, w_ff2a, w_ff2b = late if alone else (got_rest[0], got_rest[1], got[0], got_rest[2])
    mix, act, du, h2, df, dmo, dmix, dxm, st_post = _post(yret, ymla, x2, tgt2, modv, g_ffn, g_fin, w_out.reshape(d, d),
                                                         w_ff1, w_ff2a, w_ff2b, seq=seq, tm=min(tm, 256))
    kw = dict(bm=1024, bn=1024, bk=2048, out_dtype=BF16)
    g_ff2 = _matmul_tn(act, df, chip_major=False, name="gw_ff2", **kw).reshape(N_CHIPS, D_FF // N_CHIPS, d)
    if alone:
        g_ff1 = _matmul_tn(h2, du, chip_major=True, name="gw_ff1", **kw)
        g_out = _matmul_tn(mix, dmo, chip_major=False, name="gw_out", **kw).reshape(N_CHIPS, d // N_CHIPS, d)
        (dq_m, dkl, dkc, dvl, dvc), _ = _mla_bwd(q, k, v, ymla, lse, dmix, tq=tq, **dims)
        (drq, drk, drv, dkc_r, dvc_r, drg, st_ret), _ = _ret_bwd_pair(rq, rk, rv, pg, osum, dmix, lg, g_ret, chunk=chunk,
                                                                      **dims)
        late_out = [g_out, g_ff1, g_ff2]
    else:
        core, slot = place
        g_ff1, x_ff2 = _matmul_tn(h2, du, chip_major=True, name="gw_ff1", rider=_exchange_rider([g_ff2]), **kw)
        g_out, x_ff1 = _matmul_tn(mix, dmo, chip_major=False, name="gw_out", rider=_exchange_rider([g_ff1]), **kw)
        g_out = g_out.reshape(N_CHIPS, d // N_CHIPS, d)
        p_ff2 = _add_half(g_ff2, x_ff2[0], core, "add_half_w_ff2")
        p_ff1 = _add_half(g_ff1, x_ff1[0], core, "add_half_w_ff1")
        (dq_m, dkl, dkc, dvl, dvc), (l_ff2, l_ff1, x_out) = _mla_bwd(
            q, k, v, ymla, lse, dmix, tq=min(seq, 512), **dims,
            rider=_merge_riders(_scatter_rider([p_ff2, p_ff1]), _exchange_rider([g_out])))
        p_out = _add_half(g_out, x_out, core, "add_half_w_out")
        m_ff2 = _sum_chips(p_ff2, l_ff2, slot, "sum_chips_w_ff2")
        m_ff1 = _sum_chips(p_ff1, l_ff1, slot, "sum_chips_w_ff1")
        (drq, drk, drv, dkc_r, dvc_r, drg, st_ret), (l_out,) = _ret_bwd_pair(
            rq, rk, rv, pg, osum, dmix, lg, g_ret, chunk=chunk, **dims, rider=_scatter_rider([p_out]))
        late_out = [_sum_chips(p_out, l_out, slot, "sum_chips_w_out"), m_ff1, m_ff2]
    (dpb, dqf, dkvf, gx, st_pre), _ = _pre_bwd(
        x2, ctx2, modv, g_attn, pg, drq, drk, dkc_r, drv, dvc_r, drg, dq_m, dkl, dkc, dvl, dvc, dxm, w_in, g_q, g_kv,
        w_uq, w_ukv, cos_t, sin_t, seq=seq, tm=tm)
    g_early = [
        _matmul_tn(dpb, hb, bm=IN_PAD // 2, bn=d, bk=1536, chip_major=False, name="gw_in"),
        _matmul_tn(dqf, nq, bm=HEADS * MLA_HEAD, bn=Q_LORA, bk=1536, chip_major=False, name="gw_uq"),
        _matmul_tn(nkv, dkvf, bm=KV_LORA, bn=HEADS * 256, bk=1536, chip_major=True, name="gw_ukv"),
    ]
    return gx.reshape(nex, seq, d), g_early, late_out, st_post, st_ret, st_pre


_ANY = pl.BlockSpec(memory_space=pl.ANY)
_VMEM = pl.BlockSpec(memory_space=pltpu.VMEM)
_OFFSETS = tuple((dx, dy, dc) for dx in (0, 1) for dy in (0, 1) for dc in (0, 1))[1:]
_CHIP_OFFSETS = ((1, 0), (0, 1), (1, 1))


def _place():
    return lax.axis_index("x"), lax.axis_index("y"), lax.axis_index("c")


def _flip(v, d):
    return 1 - v if d else v


def _gather8_rider(a, in_vmem=True):
    def copies(a_ref, o_ref, send, recv):
        x, y, z = _place()
        me = 4 * x + 2 * y + z
        out = []
        for k, (dx, dy, dc) in enumerate(_OFFSETS):
            peer = (_flip(x, dx), _flip(y, dy), _flip(z, dc))
            landing = o_ref.at[4 * peer[0] + 2 * peer[1] + peer[2]]
            out.append((
                pltpu.make_async_remote_copy(src_ref=a_ref, dst_ref=o_ref.at[me], send_sem=send.at[k],
                                             recv_sem=recv.at[k], device_id=peer, device_id_type=MESH),
                pltpu.make_async_remote_copy(src_ref=a_ref, dst_ref=landing, send_sem=send.at[k],
                                             recv_sem=recv.at[k], device_id=peer, device_id_type=MESH)))
        return me, out

    def start(ins, outs, sems):
        me, cps = copies(ins[0], outs[0], sems[0], sems[1])
        pltpu.make_async_copy(ins[0], outs[0].at[me], sems[2]).start()
        for out_cp, _ in cps:
            out_cp.start()

    def finish(ins, outs, sems):
        me, cps = copies(ins[0], outs[0], sems[0], sems[1])
        for out_cp, in_cp in cps:
            in_cp.wait_recv()
            out_cp.wait_send()
        pltpu.make_async_copy(ins[0], outs[0].at[me], sems[2]).wait()

    spec = [_VMEM] if in_vmem else [_ANY]
    return _Rider([a], [jax.ShapeDtypeStruct((N_DEV,) + a.shape, a.dtype)],
                  [pltpu.SemaphoreType.DMA((7,)), pltpu.SemaphoreType.DMA((7,)), pltpu.SemaphoreType.DMA],
                  start, finish, in_specs=spec, out_specs=spec)


def _merge_riders(*riders):
    ins, outs, sems, in_specs, out_specs, aliases, cuts = [], [], [], [], [], {}, []
    for r in riders:
        cuts.append((len(ins), len(outs), len(sems)))
        aliases.update({len(ins) + i: len(outs) + j for i, j in r.aliases.items()})
        ins += r.ins
        outs += r.out_shapes
        sems += r.sems
        in_specs += r.in_specs
        out_specs += r.out_specs

    def part(r, cut, r_ins, r_outs, r_sems):
        return (r_ins[cut[0]:cut[0] + len(r.ins)], r_outs[cut[1]:cut[1] + len(r.out_shapes)],
                r_sems[cut[2]:cut[2] + len(r.sems)])

    def start(r_ins, r_outs, r_sems):
        for r, cut in zip(riders, cuts):
            r.start(*part(r, cut, r_ins, r_outs, r_sems))

    def finish(r_ins, r_outs, r_sems):
        for r, cut in zip(riders, cuts):
            r.finish(*part(r, cut, r_ins, r_outs, r_sems))

    def middle(r_ins, r_outs, r_sems):
        for r, cut in zip(riders, cuts):
            if r.middle is not None:
                r.middle(*part(r, cut, r_ins, r_outs, r_sems))

    return _Rider(ins, outs, sems, start, finish, aliases=aliases, in_specs=in_specs, out_specs=out_specs,
                  middle=middle if any(r.middle is not None for r in riders) else None)


def _allgather8(a, name):
    return _run_rider(_gather8_rider(a), name)[0]


BF16_TILE_ROWS = 16


def _half(o, slot, which):
    r2 = o.shape[1] // 2
    if r2 % BF16_TILE_ROWS == 0:
        return o.at[slot, pl.ds(which * r2, r2)]
    c2 = o.shape[2] // 2
    assert c2 % LANES == 0
    return o.at[slot, :, pl.ds(which * c2, c2)]


def _gather_send(o_refs, send, recv):
    x, y, z = _place()
    chip = 2 * x + y
    for a, o in enumerate(o_refs):
        r2 = o.shape[1] // 2
        mine = _half(o, chip, z)
        for k, (dx, dy) in enumerate(_CHIP_OFFSETS):
            pltpu.make_async_remote_copy(
                src_ref=mine, dst_ref=mine, send_sem=send.at[a, k], recv_sem=recv.at[a, k],
                device_id=(_flip(x, dx), _flip(y, dy), z), device_id_type=MESH).start()


def _gather_landed(o_refs, send, recv, then=None):
    x, y, z = _place()
    chip = 2 * x + y
    for a, o in enumerate(o_refs):
        for k, (dx, dy) in enumerate(_CHIP_OFFSETS):
            landed = _half(o, 2 * _flip(x, dx) + _flip(y, dy), z)
            pltpu.make_async_remote_copy(
                src_ref=landed, dst_ref=landed, send_sem=send.at[a, k], recv_sem=recv.at[a, k],
                device_id=(_flip(x, dx), _flip(y, dy), z), device_id_type=MESH).wait_recv()
            if then is not None:
                then(a, k, landed)
    for a, o in enumerate(o_refs):
        mine = _half(o, chip, z)
        for k, (dx, dy) in enumerate(_CHIP_OFFSETS):
            pltpu.make_async_remote_copy(
                src_ref=mine, dst_ref=mine, send_sem=send.at[a, k], recv_sem=recv.at[a, k],
                device_id=(_flip(x, dx), _flip(y, dy), z), device_id_type=MESH).wait_send()


def _pass_on(o_refs, fsend, frecv, a, k, landed):
    x, y, z = _place()
    pltpu.make_async_remote_copy(
        src_ref=landed, dst_ref=landed, send_sem=fsend.at[a, k], recv_sem=frecv.at[a, k],
        device_id=(x, y, 1 - z), device_id_type=MESH).start()


def _passed_on(o_refs, fsend, frecv):
    x, y, z = _place()
    for a, o in enumerate(o_refs):
        for k, (dx, dy) in enumerate(_CHIP_OFFSETS):
            other = 2 * _flip(x, dx) + _flip(y, dy)
            got = _half(o, other, 1 - z)
            gave = _half(o, other, z)
            pltpu.make_async_remote_copy(
                src_ref=got, dst_ref=got, send_sem=fsend.at[a, k], recv_sem=frecv.at[a, k],
                device_id=(x, y, 1 - z), device_id_type=MESH).wait_recv()
            pltpu.make_async_remote_copy(
                src_ref=gave, dst_ref=gave, send_sem=fsend.at[a, k], recv_sem=frecv.at[a, k],
                device_id=(x, y, 1 - z), device_id_type=MESH).wait_send()


def _gather_finish(o_refs, send, recv, fsend, frecv):
    _gather_landed(o_refs, send, recv, functools.partial(_pass_on, o_refs, fsend, frecv))
    _passed_on(o_refs, fsend, frecv)


class _Rider:
    def __init__(self, ins, out_shapes, sems, start, finish, aliases=None, in_specs=None, out_specs=None, middle=None):
        self.ins, self.out_shapes, self.sems = list(ins), list(out_shapes), list(sems)
        self.start, self.finish, self.aliases = start, finish, dict(aliases or {})
        self.middle = middle
        self.in_specs = list(in_specs) if in_specs else [_ANY] * len(self.ins)
        self.out_specs = list(out_specs) if out_specs else [_ANY] * len(self.out_shapes)


def _run_rider(rider, name):
    r_in, r_out = len(rider.ins), len(rider.out_shapes)

    def body(*refs):
        ins, outs, sems = refs[:r_in], refs[r_in:r_in + r_out], refs[r_in + r_out:]
        rider.start(ins, outs, sems)
        if rider.middle is not None:
            rider.middle(ins, outs, sems)
        rider.finish(ins, outs, sems)

    return pl.pallas_call(
        body, name=name, in_specs=rider.in_specs, out_specs=rider.out_specs, out_shape=rider.out_shapes,
        input_output_aliases=rider.aliases, scratch_shapes=rider.sems,
    )(*rider.ins)


def _hosted_call(body, args, *, name, grid, in_specs, out_specs, out_shape, scratch_shapes=(), sem, rider=None,
                 prefetch=()):
    scratch_shapes = list(scratch_shapes)
    n_pf, n_in, n_out, n_sc = len(prefetch), len(in_specs), len(out_specs), len(scratch_shapes)
    r_in, r_out = (len(rider.ins), len(rider.out_shapes)) if rider else (0, 0)
    last = tuple(g - 1 for g in grid)

    def hosted(*refs):
        p = 0
        parts = []
        for cnt in (n_pf, n_in, r_in, n_out, r_out, n_sc):
            parts.append(refs[p:p + cnt])
            p += cnt
        pf, ins, r_ins, outs, r_outs, scratch = parts
        sems = refs[p:]
        ids = [pl.program_id(a) for a in range(len(grid))]
        is_first = functools.reduce(jnp.logical_and, [i == 0 for i in ids])
        is_last = functools.reduce(jnp.logical_and, [i == e for i, e in zip(ids, last)])

        @pl.when(is_first)
        def _():
            rider.start(r_ins, r_outs, sems)

        if rider.middle is not None:
            linear = functools.reduce(lambda acc, ig: acc * ig[1] + ig[0], zip(ids, grid), 0)

            @pl.when(linear == math.prod(grid) * 3 // 4)
            def _():
                rider.middle(r_ins, r_outs, sems)

        body(*pf, *ins, *outs, *scratch)

        @pl.when(is_last)
        def _():
            rider.finish(r_ins, r_outs, sems)

    if rider is None:
        kern, all_in, all_out, shapes, scratch, aliases, extra = body, list(in_specs), list(out_specs), list(out_shape), \
            scratch_shapes, {}, []
    else:
        kern, all_in, all_out = hosted, list(in_specs) + rider.in_specs, list(out_specs) + rider.out_specs
        shapes, scratch, extra = list(out_shape) + rider.out_shapes, scratch_shapes + rider.sems, rider.ins
        aliases = {n_pf + n_in + i: n_out + j for i, j in rider.aliases.items()}
        sem = ("arbitrary",) * len(grid)
    if prefetch:
        spec = dict(grid_spec=pltpu.PrefetchScalarGridSpec(
            num_scalar_prefetch=n_pf, grid=grid, in_specs=all_in, out_specs=all_out, scratch_shapes=scratch))
    else:
        spec = dict(grid=grid, in_specs=all_in, out_specs=all_out, scratch_shapes=scratch)
    res = pl.pallas_call(kern, name=name, out_shape=shapes, input_output_aliases=aliases,
                         compiler_params=_params(sem, VMEM_LIMIT), **spec)(*prefetch, *args, *extra)
    return list(res[:n_out]), list(res[n_out:])


def _gather_rider(ws, staged=False):
    n = len(ws)
    shapes = [jax.ShapeDtypeStruct(w.shape, w.dtype) for w in ws]
    sems = [pltpu.SemaphoreType.DMA((n, 3))] * 4
    aliases = {a: a for a in range(n)}

    def start(ins, outs, s):
        _gather_send(outs, s[0], s[1])

    if not staged:
        return _Rider(ws, shapes, sems, start, lambda ins, outs, s: _gather_finish(outs, *s), aliases=aliases)
    return _Rider(
        ws, shapes, sems, start, lambda ins, outs, s: _passed_on(outs, s[2], s[3]), aliases=aliases,
        middle=lambda ins, outs, s: _gather_landed(outs, s[0], s[1], functools.partial(_pass_on, outs, s[2], s[3])))


def _gather_ici_rider(ws):
    n = len(ws)
    return _Rider(
        ws, [jax.ShapeDtypeStruct(w.shape, w.dtype) for w in ws], [pltpu.SemaphoreType.DMA((n, 3))] * 2,
        lambda ins, outs, sems: _gather_send(outs, sems[0], sems[1]),
        lambda ins, outs, sems: _gather_landed(outs, sems[0], sems[1]),
        aliases={a: a for a in range(n)})


def _gather_d2d_rider(ws):
    n = len(ws)

    def start(ins, outs, sems):
        x, y, z = _place()
        for a, o in enumerate(outs):
            for k, (dx, dy) in enumerate(_CHIP_OFFSETS):
                _pass_on(outs, sems[0], sems[1], a, k, _half(o, 2 * _flip(x, dx) + _flip(y, dy), z))

    return _Rider(
        ws, [jax.ShapeDtypeStruct(w.shape, w.dtype) for w in ws], [pltpu.SemaphoreType.DMA((n, 3))] * 2,
        start, lambda ins, outs, sems: _passed_on(outs, sems[0], sems[1]), aliases={a: a for a in range(n)})


def _copies_rider(ins, out_shapes, sem_shape, make):
    def start(r_ins, r_outs, sems):
        for cp in make(r_ins, r_outs, sems[0], sems[1]):
            cp.start()

    def finish(r_ins, r_outs, sems):
        for cp in make(r_ins, r_outs, sems[0], sems[1]):
            cp.wait()

    return _Rider(ins, out_shapes, [pltpu.SemaphoreType.DMA(sem_shape)] * 2, start, finish)


def _exchange_rider(gs):
    def make(g_refs, r_refs, send, recv):
        x, y, z = _place()
        return [pltpu.make_async_remote_copy(
            src_ref=g.at[:, pl.ds((1 - z) * (g.shape[1] // 2), g.shape[1] // 2)], dst_ref=r, send_sem=send.at[a],
            recv_sem=recv.at[a], device_id=(x, y, 1 - z), device_id_type=MESH)
            for a, (g, r) in enumerate(zip(g_refs, r_refs))]

    shapes = [jax.ShapeDtypeStruct((g.shape[0], g.shape[1] // 2, g.shape[2]), g.dtype) for g in gs]
    return _copies_rider(gs, shapes, (len(gs),), make)


def _add_half(g, recv, core, name):
    s, r, c = g.shape
    r2 = r // 2
    rb = r2
    for cand in (512, 256, 128, 64):
        if r2 % cand == 0:
            rb = cand
            break
    g4 = g.reshape(s, 2, r2, c)

    def body(core_ref, g_ref, r_ref, o_ref):
        o_ref[...] = (g_ref[...].astype(F32) + r_ref[...].astype(F32)).astype(BF16)

    return pl.pallas_call(
        body, name=name,
        grid_spec=pltpu.PrefetchScalarGridSpec(
            num_scalar_prefetch=1, grid=(s, r2 // rb),
            in_specs=[pl.BlockSpec((None, None, rb, c), lambda i, j, cr: (i, cr[0], j, 0)),
                      pl.BlockSpec((None, rb, c), lambda i, j, cr: (i, j, 0))],
            out_specs=pl.BlockSpec((None, rb, c), lambda i, j, cr: (i, j, 0))),
        out_shape=jax.ShapeDtypeStruct((s, r2, c), BF16),
        compiler_params=_params(("parallel", "parallel")),
    )(core, g4, recv)


def _scatter_rider(ps):
    def make(p_refs, o_refs, send, recv):
        x, y, z = _place()
        copies = []
        for a, (p, o) in enumerate(zip(p_refs, o_refs)):
            for k, (dx, dy) in enumerate(_CHIP_OFFSETS):
                other = 2 * _flip(x, dx) + _flip(y, dy)
                copies.append(pltpu.make_async_remote_copy(
                    src_ref=p.at[other], dst_ref=o.at[k], send_sem=send.at[a, k], recv_sem=recv.at[a, k],
                    device_id=(_flip(x, dx), _flip(y, dy), z), device_id_type=MESH))
        return copies

    shapes = [jax.ShapeDtypeStruct((3,) + p.shape[1:], p.dtype) for p in ps]
    return _copies_rider(ps, shapes, (len(ps), 3), make)


def _sum_chips(p, landed, chip, name):
    _, r2, c = p.shape
    rb = r2
    for cand in (256, 128, 64):
        if r2 % cand == 0:
            rb = cand
            break

    def body(s_ref, p_ref, l_ref, o_ref):
        acc = p_ref[...].astype(F32)
        for k in range(3):
            acc = acc + l_ref[k].astype(F32)
        o_ref[...] = acc

    return pl.pallas_call(
        body, name=name,
        grid_spec=pltpu.PrefetchScalarGridSpec(
            num_scalar_prefetch=1, grid=(r2 // rb,),
            in_specs=[pl.BlockSpec((None, rb, c), lambda i, s: (s[0], i, 0)),
                      pl.BlockSpec((3, rb, c), lambda i, s: (0, i, 0))],
            out_specs=pl.BlockSpec((rb, c), lambda i, s: (i, 0))),
        out_shape=jax.ShapeDtypeStruct((r2, c), F32),
        compiler_params=_params(("parallel",)),
    )(chip, p, landed)


def _swap_rider(hs):
    def make(h_refs, o_refs, send, recv):
        x, y, z = _place()
        return [pltpu.make_async_remote_copy(
            src_ref=h, dst_ref=o, send_sem=send.at[a], recv_sem=recv.at[a], device_id=(x, y, 1 - z),
            device_id_type=MESH) for a, (h, o) in enumerate(zip(h_refs, o_refs))]

    return _copies_rider(hs, [jax.ShapeDtypeStruct(h.shape, h.dtype) for h in hs], (len(hs),), make)


def _reduce_scatter_vmem(gs, rows, rider, name):
    n = len(gs)
    r_in, r_out = len(rider.ins), len(rider.out_shapes)
    halves = [(r // 2, g.shape[-1]) for g, (r, _) in zip(gs, rows)]
    piece_cols = 2 * LANES
    pieces = [(a, slice(c0, min(c0 + piece_cols, h[1]))) for a, h in enumerate(halves) for c0 in range(0, h[1], piece_cols)]
    n_p = len(pieces)

    def body(*refs):
        p = 0
        parts = []
        for cnt in (n, r_in, n, n, r_out, n, n, n, 6):
            parts.append(refs[p:p + cnt])
            p += cnt
        g_refs, r_ins, mine, theirs, r_outs, recv, part, land, sems = parts
        r_sems = refs[p:]
        xs, xr, ss, sr, ws, wr = sems
        x, y, z = _place()
        chip = 2 * x + y
        sib = (x, y, 1 - z)
        rider.start(r_ins, r_outs, r_sems)

        def half_of(a, s, which):
            r2 = halves[a][0]
            if len(g_refs[a].shape) == 3:
                return g_refs[a].at[s, pl.ds(pl.multiple_of(which * r2, 8), r2)]
            return g_refs[a].at[pl.ds(pl.multiple_of(s * rows[a][1] + which * r2, 8), r2)]

        def exchange(i):
            a, cols = pieces[i]
            return [pltpu.make_async_remote_copy(
                src_ref=half_of(a, s, 1 - z).at[:, cols], dst_ref=recv[a].at[s, :, cols], send_sem=xs.at[i, s],
                recv_sem=xr.at[i, s], device_id=sib, device_id_type=MESH) for s in range(N_CHIPS)]

        def scatter(i):
            a, cols = pieces[i]
            return [pltpu.make_async_remote_copy(
                src_ref=part[a].at[2 * _flip(x, dx) + _flip(y, dy), :, cols], dst_ref=land[a].at[k, :, cols],
                send_sem=ss.at[i, k], recv_sem=sr.at[i, k], device_id=(_flip(x, dx), _flip(y, dy), z),
                device_id_type=MESH) for k, (dx, dy) in enumerate(_CHIP_OFFSETS)]

        def swap(i):
            a, cols = pieces[i]
            return pltpu.make_async_remote_copy(
                src_ref=mine[a].at[:, cols], dst_ref=theirs[a].at[:, cols], send_sem=ws.at[i], recv_sem=wr.at[i],
                device_id=sib, device_id_type=MESH)

        for i in range(len(pieces)):
            for cp in exchange(i):
                cp.start()
        for i, (a, cols) in enumerate(pieces):
            for cp in exchange(i):
                cp.wait()
            for s in range(N_CHIPS):
                part[a][s, :, cols] = (half_of(a, s, z)[:, cols] + recv[a][s, :, cols]).astype(BF16)
            for cp in scatter(i):
                cp.start()
        for i, (a, cols) in enumerate(pieces):
            for cp in scatter(i):
                cp.wait()
            acc = part[a][chip, :, cols].astype(F32)
            for k in range(3):
                acc = acc + land[a][k, :, cols].astype(F32)
            mine[a][:, cols] = acc
            swap(i).start()
        for i in range(len(pieces)):
            swap(i).wait()
        rider.finish(r_ins, r_outs, r_sems)

    half_shapes = [jax.ShapeDtypeStruct(h, F32) for h in halves]
    res = pl.pallas_call(
        body, name=name, in_specs=[_VMEM] * n + rider.in_specs, out_specs=[_VMEM] * (2 * n) + rider.out_specs,
        out_shape=half_shapes + half_shapes + rider.out_shapes,
        scratch_shapes=[pltpu.VMEM((N_CHIPS,) + h, F32) for h in halves] + [pltpu.VMEM((N_CHIPS,) + h, BF16) for h in halves]
        + [pltpu.VMEM((3,) + h, BF16) for h in halves]
        + [pltpu.SemaphoreType.DMA((n_p, N_CHIPS))] * 2 + [pltpu.SemaphoreType.DMA((n_p, 3))] * 2
        + [pltpu.SemaphoreType.DMA((n_p,))] * 2 + rider.sems,
        input_output_aliases={n + i: 2 * n + j for i, j in rider.aliases.items()},
        compiler_params=_params(None, VMEM_LIMIT),
    )(*gs, *rider.ins)
    return list(res[:n]), list(res[n:2 * n]), list(res[2 * n:])


SMALL_ROWS = 32
PACK_ROWS = 16


def _pack_small(st_post, st_ret, st_pre):
    d = st_post.shape[2]

    def body(po_ref, re_ref, pr_ref, o_ref):
        o_ref[...] = jnp.zeros(o_ref.shape, F32)
        o_ref[0:1, :] = pr_ref[0, 2:3, :] + pr_ref[1, 2:3, :] + pr_ref[2, 2:3, :]
        o_ref[1:2, :] = po_ref[0, 4:5, :] + po_ref[1, 4:5, :]
        o_ref[2:3, :] = po_ref[0, 5:6, :] + po_ref[1, 5:6, :]
        o_ref[3:4, 0:512] = re_ref[0, 0:1, :] + re_ref[1, 0:1, :]
        o_ref[4:5, :] = pr_ref[0, 3:4, :] + pr_ref[1, 3:4, :] + pr_ref[2, 3:4, :]
        o_ref[5:6, :] = pr_ref[0, 4:5, :] + pr_ref[1, 4:5, :] + pr_ref[2, 4:5, :]
        lane = lax.broadcasted_iota(jnp.int32, (1, LANES), 1)
        for row, src in ((6, 1), (10, 2)):
            acc = jnp.zeros((1, LANES), F32)
            for hd in range(HEADS):
                grp = re_ref[0, src:src + 1, hd * LANES:(hd + 1) * LANES] + re_ref[1, src:src + 1, hd * LANES:(hd + 1) * LANES]
                acc = acc + jnp.where(lane == hd, grp, 0.0)
            o_ref[row:row + 1, 0:LANES] = acc
        o_ref[7:8, :] = po_ref[0, 6:7, :] + po_ref[1, 6:7, :]
        o_ref[8:9, :] = pr_ref[2, 0:1, :]
        o_ref[9:10, :] = pr_ref[2, 1:2, :]
        for e in range(2):
            b = 12 + 6 * e
            o_ref[b:b + 1, :] = pr_ref[e, 0:1, :]
            o_ref[b + 1:b + 2, :] = pr_ref[e, 1:2, :]
            o_ref[b + 2:b + 3, :] = po_ref[e, 3:4, :]
            o_ref[b + 3:b + 4, :] = po_ref[e, 0:1, :]
            o_ref[b + 4:b + 5, :] = po_ref[e, 1:2, :]
            o_ref[b + 5:b + 6, :] = po_ref[e, 2:3, :]

    return pl.pallas_call(body, name="pack_small", out_shape=jax.ShapeDtypeStruct((SMALL_ROWS, d), F32))(st_post, st_ret, st_pre)


def _small_reduce(gathered):
    d = gathered.shape[2]

    def body(g_ref, o_ref):
        tot = g_ref[0, 0:PACK_ROWS, :]
        for dev in range(1, N_DEV):
            tot = tot + g_ref[dev, 0:PACK_ROWS, :]
        o_ref[0:PACK_ROWS, :] = tot
        for j in range(6):
            acc = g_ref[0, 12 + j:13 + j, :] + g_ref[0, 18 + j:19 + j, :]
            for dev in range(1, N_DEV):
                acc = acc + g_ref[dev, 12 + j:13 + j, :] + g_ref[dev, 18 + j:19 + j, :]
            if j < 2:
                acc = acc + o_ref[8 + j:9 + j, :]
            o_ref[PACK_ROWS + j:PACK_ROWS + j + 1, :] = acc
        o_ref[PACK_ROWS + 6:PACK_ROWS + 8, :] = jnp.zeros((2, d), F32)

    return pl.pallas_call(body, name="small_reduce", out_shape=jax.ShapeDtypeStruct((PACK_ROWS + 8, d), F32))(gathered)


_SMALL = (("g_attn", 0, 1024), ("g_ffn", 1, 1024), ("g_final", 2, 1024), ("g_ret", 3, 512), ("g_q_lora", 4, 384),
          ("g_kv_lora", 5, 256), ("ret_decay_fwd", 6, HEADS), ("ret_decay_bwd", 10, HEADS))
_SMALL_NAMES = tuple(s[0] for s in _SMALL) + ("c_ctx", "b_ada")


def _small_final(tot, dcc, sg8, ws, ms, vs):
    d = tot.shape[1]
    n = len(_SMALL_NAMES)

    def body(*refs):
        t_ref, dcc_ref, sg_ref = refs[0:3]
        w_refs, m_refs, v_refs = refs[3:3 + n], refs[3 + n:3 + 2 * n], refs[3 + 2 * n:3 + 3 * n]
        outs = refs[3 + 3 * n:]
        g_refs, d_refs, mo_refs, vo_refs = outs[0:n], outs[n:2 * n], outs[2 * n:3 * n], outs[3 * n:4 * n]
        l_ref = outs[4 * n]

        def update(i, g, sl=None):
            pick = (lambda r: r[...]) if sl is None else (lambda r: r[:, sl])
            dl, mn, vn = _adam_math(pick(w_refs[i]), g, pick(m_refs[i]), pick(v_refs[i]))
            if sl is None:
                g_refs[i][...], d_refs[i][...], mo_refs[i][...], vo_refs[i][...] = g, dl, mn, vn
            else:
                g_refs[i][:, sl], d_refs[i][:, sl], mo_refs[i][:, sl], vo_refs[i][:, sl] = g, dl, mn, vn

        for i, (name, row, width) in enumerate(_SMALL):
            g = t_ref[row:row + 1, 0:width]
            if name == "ret_decay_fwd":
                g = g * sg_ref[0:1, 0:width]
            elif name == "ret_decay_bwd":
                g = g * sg_ref[1:2, 0:width]
            update(i, g)
        i_cc, i_b = n - 2, n - 1
        cc = w_refs[i_cc][...]
        s = 1.0 / (1.0 + jnp.exp(-cc))
        dsilu = dcc_ref[0, 0:1, :] + dcc_ref[2, 0:1, :] + dcc_ref[4, 0:1, :] + dcc_ref[6, 0:1, :]
        update(i_cc, dsilu * (s * (1.0 + cc * (1.0 - s))))
        for j in range(6):
            update(i_b, t_ref[PACK_ROWS + j:PACK_ROWS + j + 1, :], pl.ds(j * d, d))
        l_ref[...] = jnp.broadcast_to((0.5 / d) * jnp.sum(t_ref[7:8, :], keepdims=True), l_ref.shape)

    shapes = [jax.ShapeDtypeStruct(a.shape, F32) for a in ws]
    outs = pl.pallas_call(
        body, name="small_final", out_shape=shapes * 4 + [jax.ShapeDtypeStruct((8, LANES), F32)],
    )(tot, dcc, sg8, *ws, *ms, *vs)
    return outs[0:n], outs[n:2 * n], outs[2 * n:3 * n], outs[3 * n:4 * n], outs[4 * n]


_WEIGHTS = ("c_ctx", "w_ada", "b_ada", "g_attn", "g_ffn", "w_in", "ret_decay_fwd", "ret_decay_bwd", "g_ret", "g_q_lora",
            "w_uq", "g_kv_lora", "w_ukv", "w_out", "w_ff1", "w_ff2", "g_final")
_BIG = ("w_in", "w_uq", "w_ukv", "w_out", "w_ff1", "w_ff2")
_TRANSPOSED = ("w_in", "w_uq")


def kernel(x, c, ctx, c_ctx, w_ada, b_ada, g_attn, g_ffn, w_in, ret_decay_fwd, ret_decay_bwd, g_ret, g_q_lora, w_uq, g_kv_lora, w_ukv, w_out, w_ff1, w_ff2, g_final, loss_target, m_c_ctx, m_w_ada, m_b_ada, m_g_attn, m_g_ffn, m_w_in, m_ret_decay_fwd, m_ret_decay_bwd, m_g_ret, m_g_q_lora, m_w_uq, m_g_kv_lora, m_w_ukv, m_w_out, m_w_ff1, m_w_ff2, m_g_final, v_c_ctx, v_w_ada, v_b_ada, v_g_attn, v_g_ffn, v_w_in, v_ret_decay_fwd, v_ret_decay_bwd, v_g_ret, v_g_q_lora, v_w_uq, v_g_kv_lora, v_w_ukv, v_w_out, v_w_ff1, v_w_ff2, v_g_final):
    w = dict(c_ctx=c_ctx, w_ada=w_ada, b_ada=b_ada, g_attn=g_attn, g_ffn=g_ffn, w_in=w_in, ret_decay_fwd=ret_decay_fwd,
             ret_decay_bwd=ret_decay_bwd, g_ret=g_ret, g_q_lora=g_q_lora, w_uq=w_uq, g_kv_lora=g_kv_lora, w_ukv=w_ukv,
             w_out=w_out, w_ff1=w_ff1, w_ff2=w_ff2, g_final=g_final)
    m = dict(c_ctx=m_c_ctx, w_ada=m_w_ada, b_ada=m_b_ada, g_attn=m_g_attn, g_ffn=m_g_ffn, w_in=m_w_in,
             ret_decay_fwd=m_ret_decay_fwd, ret_decay_bwd=m_ret_decay_bwd, g_ret=m_g_ret, g_q_lora=m_g_q_lora, w_uq=m_w_uq,
             g_kv_lora=m_g_kv_lora, w_ukv=m_w_ukv, w_out=m_w_out, w_ff1=m_w_ff1, w_ff2=m_w_ff2, g_final=m_g_final)
    v = dict(c_ctx=v_c_ctx, w_ada=v_w_ada, b_ada=v_b_ada, g_attn=v_g_attn, g_ffn=v_g_ffn, w_in=v_w_in,
             ret_decay_fwd=v_ret_decay_fwd, ret_decay_bwd=v_ret_decay_bwd, g_ret=v_g_ret, g_q_lora=v_g_q_lora, w_uq=v_w_uq,
             g_kv_lora=v_g_kv_lora, w_ukv=v_w_ukv, w_out=v_w_out, w_ff1=v_w_ff1, w_ff2=v_w_ff2, g_final=v_g_final)
    xi, yi, ci = lax.axis_index("x"), lax.axis_index("y"), lax.axis_index("c")
    chip = 2 * xi + yi
    dev = 2 * chip + ci
    nex, seq, d = x.shape
    n_ada = w_ada.shape[2]

    dec = jnp.zeros((8, LANES), F32).at[0, :HEADS].set(ret_decay_fwd[0]).at[1, :HEADS].set(ret_decay_bwd[0])
    lg8, sg8 = _decay_prep(dec)
    lg = lg8[:2, :HEADS]

    def shard_of(t, k):
        return t[k][0].T if k in _TRANSPOSED else t[k][0]

    shard = {k: shard_of(w, k) for k in _BIG}
    head_rows = MLA_NOPE + MLA_ROPE
    shard["w_uq"] = jnp.pad(shard["w_uq"], ((0, MLA_HEAD - head_rows), (0, 0)))
    slot = chip.reshape(1).astype(jnp.int32)
    core = ci.reshape(1).astype(jnp.int32)
    slots = {k: _cast_into_slot(shard[k], slot, "cast_" + k) for k in _EARLY}
    half_ff = shard["w_ff2"].shape[0] // 2
    late_pieces = [(shard["w_out"], 0, shard["w_out"].shape[0]), (shard["w_ff1"], 0, shard["w_ff1"].shape[0]),
                   (shard["w_ff2"], 0, half_ff), (shard["w_ff2"], half_ff, half_ff)]
    late_slots, (w_in_f, w_uq_k, w_ukv_k, c8) = _cast_into_slots(
        late_pieces, slot, "cast_late",
        rider=_merge_riders(_gather_rider([slots[k] for k in _EARLY]),
                            _gather8_rider(jnp.pad(c, ((0, 8 - nex), (0, 0))), in_vmem=False)))

    a_in = jnp.concatenate([c8[:, :nex].reshape(N_DEV * nex, d), c_ctx.reshape(1, d), jnp.zeros((7, d), F32)], axis=0)
    b_sh = lax.dynamic_slice(b_ada, (0, chip * n_ada), (1, n_ada))
    mod4 = _run_rider(_gather_rider([_mod_fwd(a_in, w_ada[0], b_sh, slot)]), "ag_mod")[0]
    w_in_k = jnp.pad(w_in_f.reshape(IN_COLS, d), ((0, IN_PAD - IN_COLS), (0, 0)))
    mod_all = mod4.transpose(1, 0, 2).reshape(a_in.shape[0], N_CHIPS * n_ada)
    mod_me = lax.dynamic_slice(mod_all, (nex * dev, 0), (nex, N_CHIPS * n_ada)).reshape(nex, 6, d)
    mod_c = mod_all[N_DEV * nex].reshape(1, 6, d)
    modv = jnp.pad(jnp.concatenate([mod_me, mod_c], axis=0), ((0, 0), (0, 2), (0, 0)))

    gx, g_early, late, st_post, st_ret, st_pre = _local_step(
        x, ctx, loss_target, modv, lg, g_attn, g_ffn, g_final.reshape(1, d), g_ret, g_q_lora, g_kv_lora,
        w_in_k, w_uq_k, w_ukv_k, late_slots, (core, slot))

    mine, theirs, (*late_theirs, gathered) = _reduce_scatter_vmem(
        g_early, [(IN_COLS // N_CHIPS, IN_COLS // N_CHIPS), (head_rows, MLA_HEAD), (KV_LORA, KV_LORA)],
        _merge_riders(_swap_rider(late), _gather8_rider(_pack_small(st_post, st_ret, st_pre))), "rs_early")
    tot = _small_reduce(gathered)
    dm = jnp.concatenate([
        gathered[:, 12:24].reshape(N_DEV * nex, 6 * d),
        jnp.concatenate([tot[8:10].reshape(1, 2 * d), jnp.zeros((1, 4 * d), F32)], axis=1),
        jnp.zeros((7, 6 * d), F32)], axis=0)
    dm_sh = lax.dynamic_slice(dm, (0, chip * n_ada), (dm.shape[0], n_ada))
    g_ada, da, *ada_outs = _mod_bwd_adamw(a_in, dm_sh, w_ada[0], m["w_ada"][0], v["w_ada"][0])
    dcc = _allgather8(da[N_DEV * nex:], "ag_dcc")
    halves = dict(zip(_EARLY, zip(mine, theirs)))
    halves.update(zip(_LATE, zip(late, late_theirs)))
    grad, delta, new_m, new_v = {}, {}, {}, {}
    for k in _BIG:
        a, b = halves[k]
        res = _adamw_halves(shard_of(w, k), a, b, shard_of(m, k), shard_of(v, k), core, "adamw_" + k)
        grad[k], delta[k], new_m[k], new_v[k] = [(o.T if k in _TRANSPOSED else o).reshape(w[k].shape) for o in res]

    shp = w_ada.shape
    grad["w_ada"] = g_ada.reshape(shp)
    delta["w_ada"], new_m["w_ada"], new_v["w_ada"] = [o.reshape(shp) for o in ada_outs]
    rows = [{k: t[k].reshape(1, -1) for k in _SMALL_NAMES} for t in (w, m, v)]
    small = _small_final(tot, dcc, sg8, *[[t[k] for k in _SMALL_NAMES] for t in rows])
    for res, outs in zip((grad, delta, new_m, new_v), small[:4]):
        for k, o in zip(_SMALL_NAMES, outs):
            res[k] = o.reshape(w[k].shape)
    return (small[4][0, 0], gx, *[grad[k] for k in _WEIGHTS], *[delta[k] for k in _WEIGHTS],
            *[new_m[k] for k in _WEIGHTS], *[new_v[k] for k in _WEIGHTS])
```

```python
import functools
import math

import jax
import jax.numpy as jnp
from jax import lax
from jax.experimental import pallas as pl
from jax.experimental.pallas import tpu as pltpu

F32 = jnp.float32
BF16 = jnp.bfloat16
MESH = pl.DeviceIdType.MESH

EPS = 1e-6
D_MODEL = 1024
D_FF = 4096
HEADS = 4
RET_DK = 64
RET_DV = 128
MLA_NOPE = 128
MLA_ROPE = 64
MLA_HEAD = 256
Q_LORA = 384
KV_LORA = 256
GRID_W = 64
ROPE_BASE = 10000.0
IN_COLS = 2240
IN_PAD = 2304
PG_COLS = 1152
N_CHIPS = 4
N_DEV = 8
LANES = 128
ADAM_LR = 0.001
ADAM_B1 = 0.9
ADAM_B2 = 0.999
ADAM_EPS = 1e-08
ADAM_WD = 0.01
ADAM_STEP = 10
VMEM_LIMIT = 56 * 1024 * 1024


def _dot(a, b):
    return jnp.dot(a, b, preferred_element_type=F32)


def _dot_nt(a, b):
    return lax.dot_general(a, b, (((1,), (1,)), ((), ())), preferred_element_type=F32)


def _dot_tn(a, b):
    return lax.dot_general(a, b, (((0,), (0,)), ((), ())), preferred_element_type=F32)


def _params(sem=None, vmem=None):
    return pltpu.CompilerParams(dimension_semantics=sem, vmem_limit_bytes=vmem)


def _full(shape):
    n = len(shape)
    return pl.BlockSpec(shape, lambda *_: (0,) * n)


def _once(shape):
    n = len(shape)
    return pl.BlockSpec(shape, lambda *_: (0,) * n, pipeline_mode=pl.Buffered(1))


def _rope(x, cos, sin):
    w = x.shape[-1]
    lo = (lax.broadcasted_iota(jnp.int32, (1, w), 1) % 64) < 32
    swapped = jnp.where(lo, pltpu.roll(x, w - 32, 1), pltpu.roll(x, 32, 1))
    return x * cos + swapped * sin


def _rope_t(g, cos, sin):
    w = g.shape[-1]
    lo = (lax.broadcasted_iota(jnp.int32, (1, w), 1) % 64) < 32
    t = g * sin
    swapped = jnp.where(lo, pltpu.roll(t, w - 32, 1), pltpu.roll(t, 32, 1))
    return g * cos + swapped


def _rope_tables(seq, tm):
    rows = seq // GRID_W
    row = jnp.repeat(jnp.arange(rows, dtype=F32), GRID_W)
    col = jnp.tile(jnp.arange(GRID_W, dtype=F32), rows)
    n_freq = RET_DK // 4
    freq = ROPE_BASE ** (-jnp.arange(n_freq, dtype=F32) / n_freq)
    ang = jnp.concatenate([row[:, None] * freq, col[:, None] * freq], axis=-1)
    cos, sin = jnp.cos(ang), jnp.sin(ang)
    cos_t = jnp.tile(jnp.concatenate([cos, cos], -1), (1, HEADS))
    sin_t = jnp.tile(jnp.concatenate([-sin, sin], -1), (1, HEADS))
    cos_t = jnp.concatenate([cos_t, jnp.ones((tm, 4 * RET_DK), F32)], 0)
    sin_t = jnp.concatenate([sin_t, jnp.zeros((tm, 4 * RET_DK), F32)], 0)
    return cos_t, sin_t


def _adam_math(w, g, m, v):
    mn = ADAM_B1 * m + (1.0 - ADAM_B1) * g
    vn = ADAM_B2 * v + (1.0 - ADAM_B2) * (g * g)
    m_hat = mn / (1.0 - ADAM_B1 ** ADAM_STEP)
    v_hat = vn / (1.0 - ADAM_B2 ** ADAM_STEP)
    return -ADAM_LR * (m_hat / (jnp.sqrt(v_hat) + ADAM_EPS) + ADAM_WD * w), mn, vn


def _cast_into_slots(pieces, slot, name, rider=None):
    c = pieces[0][0].shape[1]
    rb = max(b for b in range(16, 1025, 16) if b * c * 4 <= (5 << 19)
             and all(cnt % b == 0 and st % b == 0 for _, st, cnt in pieces))
    nbs = [cnt // rb for _, _, cnt in pieces]
    starts = [sum(nbs[:s]) for s in range(len(pieces))]

    def body(s_ref, *refs):
        i = pl.program_id(0)
        for s in range(len(pieces)):
            @pl.when(jnp.logical_and(i >= starts[s], i < starts[s] + nbs[s]))
            def _():
                refs[len(pieces) + s][...] = refs[s][...].astype(BF16)

    in_specs, out_specs = [], []
    for (_, first_row, _), nb, st in zip(pieces, nbs, starts):
        in_specs.append(pl.BlockSpec((rb, c), lambda i, s, nb=nb, st=st, f=first_row // rb: (f + jnp.clip(i - st, 0, nb - 1), 0)))
        out_specs.append(pl.BlockSpec((None, rb, c), lambda i, s, nb=nb, st=st: (s[0], jnp.clip(i - st, 0, nb - 1), 0)))
    return _hosted_call(
        body, [w for w, _, _ in pieces], name=name, grid=(sum(nbs),), prefetch=(slot,), in_specs=in_specs,
        out_specs=out_specs, out_shape=[jax.ShapeDtypeStruct((N_CHIPS, cnt, c), BF16) for _, _, cnt in pieces],
        sem=("arbitrary",), rider=rider)


def _cast_into_slot(w, slot, name):
    return _cast_into_slots([(w, 0, w.shape[0])], slot, name)[0][0]


def _adamw_halves(w, mine, theirs, m, v, core, name):
    r, c = w.shape
    r2 = r // 2
    rb = max(b for b in range(8, r2 + 1, 8) if r2 % b == 0 and b * c * 4 <= (1 << 21))
    nbh = r2 // rb

    def body(z_ref, w_ref, a_ref, b_ref, m_ref, v_ref, g_ref, d_ref, mo_ref, vo_ref):
        here = (pl.program_id(0) // nbh) == z_ref[0]
        gg = jnp.where(here, a_ref[...], b_ref[...])
        g_ref[...] = gg
        d_ref[...], mo_ref[...], vo_ref[...] = _adam_math(w_ref[...], gg, m_ref[...], v_ref[...])

    spec = pl.BlockSpec((rb, c), lambda i, z: (i, 0))
    a_spec = pl.BlockSpec((rb, c), lambda i, z: (jnp.clip(i - z[0] * nbh, 0, nbh - 1), 0))
    b_spec = pl.BlockSpec((rb, c), lambda i, z: (jnp.clip(i - (1 - z[0]) * nbh, 0, nbh - 1), 0))
    shp = jax.ShapeDtypeStruct((r, c), F32)
    return pl.pallas_call(
        body, name=name,
        grid_spec=pltpu.PrefetchScalarGridSpec(
            num_scalar_prefetch=1, grid=(r // rb,), in_specs=[spec, a_spec, b_spec, spec, spec], out_specs=[spec] * 4),
        out_shape=[shp] * 4,
        compiler_params=_params(("parallel",)),
    )(core, w, mine, theirs, m, v)


def _decay_prep(dec):
    def body(d_ref, lg_ref, sg_ref):
        d = d_ref[...]
        lg_ref[...] = jnp.minimum(d, 0.0) - jnp.log(1.0 + jnp.exp(-jnp.abs(d)))
        sg_ref[...] = 1.0 / (1.0 + jnp.exp(d))

    shp = jax.ShapeDtypeStruct(dec.shape, F32)
    return pl.pallas_call(body, name="decay_prep", out_shape=[shp, shp])(dec)


def _mod_fwd(a_in, w_ada, b_sh, slot):
    rows, d = a_in.shape
    n = w_ada.shape[1]
    bn = 512

    def body(s_ref, a_ref, w_ref, b_ref, o_ref):
        a = a_ref[...]
        s = (a / (1.0 + jnp.exp(-a))).astype(BF16)
        o_ref[...] = _dot(s, w_ref[...].astype(BF16)) + b_ref[...]

    (out,), _ = _hosted_call(
        body, (a_in, w_ada, b_sh), name="mod_fwd", grid=(n // bn,), prefetch=(slot,),
        in_specs=[pl.BlockSpec((rows, d), lambda j, s: (0, 0)), pl.BlockSpec((d, bn), lambda j, s: (0, j)),
                  pl.BlockSpec((1, bn), lambda j, s: (0, j))],
        out_specs=[pl.BlockSpec((None, rows, bn), lambda j, s: (s[0], 0, j))],
        out_shape=[jax.ShapeDtypeStruct((N_CHIPS, rows, n), F32)], sem=("parallel",))
    return out


def _mod_bwd_adamw(a_in, dm, w_ada, m, v):
    rows, d = a_in.shape
    n = w_ada.shape[1]
    rb = 2 * LANES

    def body(a_ref, dm_ref, w_ref, m_ref, v_ref, gw_ref, da_ref, d_ref, mo_ref, vo_ref):
        a = a_ref[...]
        s = (a / (1.0 + jnp.exp(-a))).astype(BF16)
        dmb = dm_ref[...].astype(BF16)
        w = w_ref[...]
        g = _dot_tn(s, dmb)
        gw_ref[...] = g
        d_ref[...], mo_ref[...], vo_ref[...] = _adam_math(w, g, m_ref[...], v_ref[...])
        da_ref[...] = _dot_nt(dmb, w.astype(BF16))

    blk = pl.BlockSpec((rb, n), lambda i: (i, 0))
    cols = pl.BlockSpec((rows, rb), lambda i: (0, i))
    shp = jax.ShapeDtypeStruct((d, n), F32)
    return pl.pallas_call(
        body, name="mod_bwd_adamw", grid=(d // rb,),
        in_specs=[cols, _full((rows, n)), blk, blk, blk],
        out_specs=[blk, cols, blk, blk, blk],
        out_shape=[shp, jax.ShapeDtypeStruct((rows, d), F32), shp, shp, shp],
        compiler_params=_params(("parallel",), VMEM_LIMIT),
    )(a_in, dm, w_ada, m, v)


def _pre_fwd(x2, ctx2, modv, g_attn, w_in, g_q, g_kv, w_uq, w_ukv, cos_t, sin_t, *, seq, tm, rider=None):
    t_lat, d = x2.shape
    t_ctx = ctx2.shape[0]
    nl, nc = t_lat // tm, t_ctx // tm
    n_all = t_lat + t_ctx
    tpe = seq // tm
    nex = t_lat // seq

    def body(x_ref, c_ref, mod_ref, g_ref, win_ref, gq_ref, gkv_ref, wuq_ref, wukv_ref, cos_ref, sin_ref,
             h_ref, pg_ref, rq_ref, rk_ref, rv_ref, nq_ref, nkv_ref, q_ref, k_ref, v_ref):
        i = pl.program_id(0)
        xt = jnp.where(i < nl, x_ref[...], c_ref[...])
        sh = mod_ref[0, 0:1, :]
        sc = mod_ref[0, 1:2, :]
        r = lax.rsqrt(jnp.mean(xt * xt, axis=-1, keepdims=True) + EPS)
        hb = ((xt * r) * g_ref[...] * (1.0 + sc) + sh).astype(BF16)
        h_ref[...] = hb
        p = _dot_nt(hb, win_ref[...])
        cos = cos_ref[...]
        sin = sin_ref[...]
        rq_ref[...] = _rope(p[:, 0:256], cos, sin).astype(BF16)
        rk_ref[...] = _rope(p[:, 256:512] * (RET_DK ** -0.5), cos, sin).astype(BF16)
        rv_ref[...] = p[:, 512:1024].astype(BF16)
        pg_ref[...] = p[:, 1024:2176]
        cq = p[:, 1536:1920]
        ckv = p[:, 1920:2176]
        nqb = (cq * lax.rsqrt(jnp.mean(cq * cq, axis=-1, keepdims=True) + EPS) * gq_ref[...]).astype(BF16)
        nkvb = (ckv * lax.rsqrt(jnp.mean(ckv * ckv, axis=-1, keepdims=True) + EPS) * gkv_ref[...]).astype(BF16)
        nq_ref[...] = nqb
        nkv_ref[...] = nkvb
        cos1 = cos[:, 0:LANES]
        sin1 = sin[:, 0:LANES]
        kpe = _rope(p[:, 2176:2304], cos1, sin1).astype(BF16)
        for hd in range(HEADS):
            o = hd * MLA_HEAD
            qh = _dot_nt(nqb, wuq_ref[hd]) * MLA_SCALE
            q_ref[:, o:o + 128] = qh[:, 0:128].astype(BF16)
            q_ref[:, o + 128:o + 256] = _rope(qh[:, 128:256], cos1, sin1).astype(BF16)
            kvh = _dot(nkvb, wukv_ref[hd])
            k_ref[:, o:o + 128] = kvh[:, 0:128].astype(BF16)
            k_ref[:, o + 128:o + 256] = kpe
            v_ref[:, hd * 128:(hd + 1) * 128] = kvh[:, 128:256].astype(BF16)

    def tile(width):
        return pl.BlockSpec((tm, width), lambda i: (i, 0))

    widths = (d, PG_COLS, 256, 256, 512, Q_LORA, KV_LORA, HEADS * MLA_HEAD, HEADS * MLA_HEAD, HEADS * 128)
    dtypes = (BF16, F32, BF16, BF16, BF16, BF16, BF16, BF16, BF16, BF16)
    tab = pl.BlockSpec((tm, 256), lambda i: (jnp.where(i < nl, i % tpe, tpe), 0))
    return _hosted_call(
        body, (x2, ctx2, modv, g_attn, w_in, g_q, g_kv, w_uq, w_ukv, cos_t, sin_t), name="pre_fwd", grid=(nl + nc,),
        in_specs=[
            pl.BlockSpec((tm, d), lambda i: (jnp.minimum(i, nl - 1), 0)),
            pl.BlockSpec((tm, d), lambda i: (jnp.maximum(i - nl, 0), 0)),
            pl.BlockSpec((1, 8, d), lambda i: (jnp.minimum(i // tpe, nex), 0, 0)),
            _full((1, d)), _full(w_in.shape), _full((1, Q_LORA)), _full((1, KV_LORA)),
            _full(w_uq.shape), _full(w_ukv.shape), tab, tab,
        ],
        out_specs=[tile(w) for w in widths],
        out_shape=[jax.ShapeDtypeStruct((n_all, w), dt) for w, dt in zip(widths, dtypes)],
        sem=("parallel",), rider=rider)


def _post(yret, ymla, x2, tgt2, modv, g_ffn, g_fin, w_out, w_ff1, w_ff2a, w_ff2b, *, seq, tm):
    t_lat, d = x2.shape
    nl = t_lat // tm
    tpe = seq // tm
    nex = t_lat // seq
    n_slab = w_ff1.shape[0]
    fs = w_ff1.shape[2]
    fh = w_ff2a.shape[1]

    def body(yr_ref, ym_ref, x_ref, t_ref, mod_ref, gf_ref, gl_ref, wo_ref, w1_ref, w2a_ref, w2b_ref,
             mix_ref, a_ref, du_ref, h2_ref, df_ref, dmo_ref, dmix_ref, dxm_ref, st_ref, ru_ref):
        i = pl.program_id(0)
        gt_a = mod_ref[0, 2:3, :]
        sh_f = mod_ref[0, 3:4, :]
        sc_f = mod_ref[0, 4:5, :]
        gt_f = mod_ref[0, 5:6, :]
        g_ffn_v = gf_ref[...]
        g_fin_v = gl_ref[...]
        yr = yr_ref[...]
        ym = ym_ref[...]
        mix_ref[:, 0:512] = yr
        mix_ref[:, 512:1024] = ym
        op = _dot(yr, wo_ref[0:512, :]) + _dot(ym, wo_ref[512:1024, :])
        x_mid = x_ref[...] + gt_a * op
        r2 = lax.rsqrt(jnp.mean(x_mid * x_mid, axis=-1, keepdims=True) + EPS)
        xh2 = x_mid * r2
        h2b = (xh2 * g_ffn_v * (1.0 + sc_f) + sh_f).astype(BF16)
        h2_ref[...] = h2b
        f = jnp.zeros((tm, d), F32)
        for s in range(n_slab):
            ru = jnp.maximum(_dot(h2b, w1_ref[s]), 0.0)
            ru_ref[:, s * fs:(s + 1) * fs] = ru
            ab = (ru * ru).astype(BF16)
            a_ref[:, s * fs:(s + 1) * fs] = ab
            f = f + _dot(ab[:, 0:fh], w2a_ref[s]) + _dot(ab[:, fh:fs], w2b_ref[s])
        x_out = x_mid + gt_f * f
        r3 = lax.rsqrt(jnp.mean(x_out * x_out, axis=-1, keepdims=True) + EPS)
        xh3 = x_out * r3
        err = xh3 * g_fin_v - t_ref[...]
        dy = err * (1.0 / d)
        dxh3 = dy * g_fin_v
        dx_out = r3 * (dxh3 - xh3 * jnp.mean(dxh3 * xh3, axis=-1, keepdims=True))
        dfb = (dx_out * gt_f).astype(BF16)
        df_ref[...] = dfb
        dh2 = jnp.zeros((tm, d), F32)
        for s in range(n_slab):
            da = jnp.concatenate([_dot_nt(dfb, w2a_ref[s]), _dot_nt(dfb, w2b_ref[s])], axis=1)
            dub = (da * (2.0 * ru_ref[:, s * fs:(s + 1) * fs])).astype(BF16)
            du_ref[:, s * fs:(s + 1) * fs] = dub
            dh2 = dh2 + _dot_nt(dub, w1_ref[s])
        dxh2 = dh2 * (1.0 + sc_f) * g_ffn_v
        dx_mid = dx_out + r2 * (dxh2 - xh2 * jnp.mean(dxh2 * xh2, axis=-1, keepdims=True))
        dxm_ref[...] = dx_mid
        dmob = (dx_mid * gt_a).astype(BF16)
        dmo_ref[...] = dmob
        dmix_ref[...] = _dot_nt(dmob, wo_ref[...]).astype(BF16)

        def rsum(v):
            return jnp.sum(v, axis=0, keepdims=True)

        stats = jnp.concatenate([
            rsum(dh2), rsum(dh2 * xh2 * g_ffn_v), rsum(dx_out * f), rsum(dx_mid * op),
            rsum(dh2 * (1.0 + sc_f) * xh2), rsum(dy * xh3), rsum(err * err), jnp.zeros((1, d), F32)], axis=0)

        @pl.when(i % tpe == 0)
        def _():
            st_ref[0] = stats

        @pl.when(i % tpe != 0)
        def _():
            st_ref[0] += stats

    def tile(width):
        return pl.BlockSpec((tm, width), lambda i: (i, 0))

    widths = (d, D_FF, D_FF, d, d, d, d, d)
    dtypes = (BF16, BF16, BF16, BF16, BF16, BF16, BF16, F32)
    const = pl.Buffered(1)
    return pl.pallas_call(
        body, name="post", grid=(nl,),
        in_specs=[
            tile(512), tile(512), tile(d), tile(d),
            pl.BlockSpec((1, 8, d), lambda i: (i // tpe, 0, 0)),
            _full((1, d)), _full((1, d)),
            pl.BlockSpec(w_out.shape, lambda i: (0, 0), pipeline_mode=const),
            pl.BlockSpec(w_ff1.shape, lambda i: (0, 0, 0), pipeline_mode=const),
            pl.BlockSpec(w_ff2a.shape, lambda i: (0, 0, 0), pipeline_mode=const),
            pl.BlockSpec(w_ff2b.shape, lambda i: (0, 0, 0), pipeline_mode=const),
        ],
        out_specs=[tile(w) for w in widths] + [pl.BlockSpec((1, 8, d), lambda i: (i // tpe, 0, 0))],
        out_shape=[jax.ShapeDtypeStruct((t_lat, w), dt) for w, dt in zip(widths, dtypes)]
        + [jax.ShapeDtypeStruct((nex, 8, d), F32)],
        scratch_shapes=[pltpu.VMEM((tm, D_FF), F32)],
        compiler_params=_params(("arbitrary",), VMEM_LIMIT),
    )(yret, ymla, x2, tgt2, modv, g_ffn, g_fin, w_out, w_ff1, w_ff2a, w_ff2b)


def _pre_bwd(x2, ctx2, modv, g_attn, pg, drq, drk, dkc_r, drv, dvc_r, drg, dq_m, dkl, dkc, dvl, dvc, dxm,
             w_in, g_q, g_kv, w_uq, w_ukv, cos_t, sin_t, *, seq, tm, rider=None):
    t_lat, d = x2.shape
    t_ctx = ctx2.shape[0]
    nl, nc = t_lat // tm, t_ctx // tm
    n_all = t_lat + t_ctx
    tpe = seq // tm
    nex = t_lat // seq

    def body(x_ref, c_ref, mod_ref, g_ref, pg_ref, drq_ref, drk_ref, dkcr_ref, drv_ref, dvcr_ref, drg_ref,
             dq_ref, dkl_ref, dkc_ref, dvl_ref, dvc_ref, dxm_ref, win_ref, gq_ref, gkv_ref, wuq_ref, wukv_ref,
             cos_ref, sin_ref, dpb_ref, dqf_ref, dkvf_ref, gx_ref, st_ref):
        i = pl.program_id(0)
        lat = i < nl
        latf = lat.astype(F32)
        cos = cos_ref[...]
        sin = sin_ref[...]
        cos1 = cos[:, 0:LANES]
        sin1 = sin[:, 0:LANES]
        d_rq = _rope_t(drq_ref[...] * latf, cos, sin)
        d_rk = _rope_t(jnp.where(lat, drk_ref[...], dkcr_ref[...]), cos, sin) * (RET_DK ** -0.5)
        d_rv = jnp.where(lat, drv_ref[...], dvcr_ref[...])
        d_rg = drg_ref[...] * latf
        dq_all = dq_ref[...] * (latf * MLA_SCALE)
        dk_all = jnp.where(lat, dkl_ref[...], dkc_ref[...])
        dv_all = jnp.where(lat, dvl_ref[...], dvc_ref[...])
        dnq = jnp.zeros((tm, Q_LORA), F32)
        dnkv = jnp.zeros((tm, KV_LORA), F32)
        dkpe = jnp.zeros((tm, LANES), F32)
        for hd in range(HEADS):
            o = hd * MLA_HEAD
            dqh = jnp.concatenate([dq_all[:, o:o + 128], _rope_t(dq_all[:, o + 128:o + 256], cos1, sin1)],
                                  axis=1).astype(BF16)
            dqf_ref[:, o:o + 256] = dqh
            dnq = dnq + _dot(dqh, wuq_ref[hd])
            dkpe = dkpe + dk_all[:, o + 128:o + 256]
            dkvh = jnp.concatenate([dk_all[:, o:o + 128], dv_all[:, hd * 128:(hd + 1) * 128]], axis=1).astype(BF16)
            dkvf_ref[:, o:o + 256] = dkvh
            dnkv = dnkv + _dot_nt(dkvh, wukv_ref[hd])
        d_kpe = _rope_t(dkpe, cos1, sin1)
        pgv = pg_ref[...]
        cq = pgv[:, 512:896]
        ckv = pgv[:, 896:1152]
        rq_ = lax.rsqrt(jnp.mean(cq * cq, axis=-1, keepdims=True) + EPS)
        cqh = cq * rq_
        dcqh = dnq * gq_ref[...]
        d_cq = rq_ * (dcqh - cqh * jnp.mean(dcqh * cqh, axis=-1, keepdims=True))
        rkv_ = lax.rsqrt(jnp.mean(ckv * ckv, axis=-1, keepdims=True) + EPS)
        ckvh = ckv * rkv_
        dckvh = dnkv * gkv_ref[...]
        d_ckv = rkv_ * (dckvh - ckvh * jnp.mean(dckvh * ckvh, axis=-1, keepdims=True))
        dpb = jnp.concatenate([d_rq, d_rk, d_rv, d_rg, d_cq, d_ckv, d_kpe], axis=1).astype(BF16)
        dpb_ref[...] = dpb
        dh = _dot(dpb, win_ref[...])
        xt = jnp.where(lat, x_ref[...], c_ref[...])
        sc = mod_ref[0, 1:2, :]
        g = g_ref[...]
        r = lax.rsqrt(jnp.mean(xt * xt, axis=-1, keepdims=True) + EPS)
        xh = xt * r
        dxh = dh * (1.0 + sc) * g
        dx = r * (dxh - xh * jnp.mean(dxh * xh, axis=-1, keepdims=True))

        @pl.when(lat)
        def _():
            gx_ref[...] = dxm_ref[...] + dx

        def rsum(v):
            return jnp.sum(v, axis=0, keepdims=True)

        def widen(v):
            return jnp.concatenate([v, jnp.zeros((1, d - v.shape[1]), F32)], axis=1)

        stats = jnp.concatenate([
            rsum(dh), rsum(dh * xh * g), rsum(dh * (1.0 + sc) * xh), widen(rsum(dnq * cqh)), widen(rsum(dnkv * ckvh)),
            jnp.zeros((3, d), F32)], axis=0)
        first = jnp.logical_or(jnp.logical_and(lat, i % tpe == 0), i == nl)

        @pl.when(first)
        def _():
            st_ref[0] = stats

        @pl.when(jnp.logical_not(first))
        def _():
            st_ref[0] += stats

    def lat_tile(width):
        return pl.BlockSpec((tm, width), lambda i: (jnp.minimum(i, nl - 1), 0))

    def ctx_tile(width):
        return pl.BlockSpec((tm, width), lambda i: (jnp.maximum(i - nl, 0), 0))

    def tile(width):
        return pl.BlockSpec((tm, width), lambda i: (i, 0))

    tab = pl.BlockSpec((tm, 256), lambda i: (jnp.where(i < nl, i % tpe, tpe), 0))
    ex = pl.BlockSpec((1, 8, d), lambda i: (jnp.minimum(i // tpe, nex), 0, 0))
    return _hosted_call(
        body, (x2, ctx2, modv, g_attn, pg, drq, drk, dkc_r, drv, dvc_r, drg, dq_m, dkl, dkc, dvl, dvc, dxm,
               w_in, g_q, g_kv, w_uq, w_ukv, cos_t, sin_t), name="pre_bwd", grid=(nl + nc,),
        in_specs=[
            lat_tile(d), ctx_tile(d), ex, _full((1, d)), tile(PG_COLS),
            lat_tile(256), lat_tile(256), ctx_tile(256), lat_tile(512), ctx_tile(512), lat_tile(512),
            lat_tile(1024), lat_tile(1024), ctx_tile(1024), lat_tile(512), ctx_tile(512), lat_tile(d),
            _once(w_in.shape), _full((1, Q_LORA)), _full((1, KV_LORA)), _once(w_uq.shape), _once(w_ukv.shape),
            tab, tab,
        ],
        out_specs=[tile(IN_PAD), tile(1024), tile(1024), lat_tile(d), ex],
        out_shape=[
            jax.ShapeDtypeStruct((n_all, IN_PAD), BF16), jax.ShapeDtypeStruct((n_all, 1024), BF16),
            jax.ShapeDtypeStruct((n_all, 1024), BF16), jax.ShapeDtypeStruct((t_lat, d), F32),
            jax.ShapeDtypeStruct((nex + 1, 8, d), F32),
        ],
        sem=("arbitrary",), rider=rider)


MLA_SCALE = 1.0 / math.sqrt(MLA_NOPE + MLA_ROPE)
KEY_BLOCK = 1024


def _mla_specs(t_lat, seq, ctx_len, tq, heads=1):
    nqt = seq // tq
    cb = t_lat // ctx_len
    q = pl.BlockSpec((tq, heads * MLA_HEAD), lambda b, h, j: (b * nqt + j, h))
    kl = pl.BlockSpec((seq, heads * MLA_HEAD), lambda b, h, j: (b, h))
    kc = pl.BlockSpec((ctx_len, heads * MLA_HEAD), lambda b, h, j: (cb + b, h))
    vl = pl.BlockSpec((seq, heads * 128), lambda b, h, j: (b, h))
    vc = pl.BlockSpec((ctx_len, heads * 128), lambda b, h, j: (cb + b, h))
    o = pl.BlockSpec((tq, heads * 128), lambda b, h, j: (b * nqt + j, h))
    return q, kl, kc, vl, vc, o


FWD_HEADS = 2
BWD_HEADS = 1


def _mla_fwd(q, k, v, *, t_lat, seq, ctx_len, tq, rider=None):
    nex = t_lat // seq

    def body(q_ref, kl_ref, kc_ref, vl_ref, vc_ref, o_ref, lse_ref):
        for hh in range(FWD_HEADS):
            wide = slice(hh * MLA_HEAD, (hh + 1) * MLA_HEAD)
            cols = slice(hh * 128, (hh + 1) * 128)
            qb = q_ref[:, wide]
            s = _dot_nt(qb, kl_ref[:, wide])
            sc = _dot_nt(qb, kc_ref[:, wide])
            m = jnp.maximum(jnp.max(s, axis=-1, keepdims=True), jnp.max(sc, axis=-1, keepdims=True))
            p = jnp.exp(s - m)
            pc = jnp.exp(sc - m)
            total = jnp.sum(p, axis=-1, keepdims=True) + jnp.sum(pc, axis=-1, keepdims=True)
            o = _dot(p.astype(BF16), vl_ref[:, cols]) + _dot(pc.astype(BF16), vc_ref[:, cols])
            o_ref[:, cols] = (o * (1.0 / total)).astype(BF16)
            lse_ref[:, cols] = jnp.broadcast_to(m + jnp.log(total), (tq, 128))

    qs, kl, kc, vl, vc, os_ = _mla_specs(t_lat, seq, ctx_len, tq, FWD_HEADS)
    return _hosted_call(
        body, (q, k, k, v, v), name="mla_fwd", grid=(nex, HEADS // FWD_HEADS, seq // tq),
        in_specs=[qs, kl, kc, vl, vc], out_specs=[os_, os_],
        out_shape=[jax.ShapeDtypeStruct((t_lat, HEADS * 128), BF16), jax.ShapeDtypeStruct((t_lat, HEADS * 128), F32)],
        sem=("parallel", "parallel", "arbitrary"), rider=rider)


def _mla_bwd(q, k, v, ymla, lse, dmix, *, t_lat, seq, ctx_len, tq, rider=None):
    nex = t_lat // seq
    nqt = seq // tq
    t_ctx = nex * ctx_len
    kb = min(KEY_BLOCK, seq)

    def body(q_ref, kl_ref, kc_ref, vl_ref, vc_ref, o_ref, lse_ref, do_ref, dq_ref, dkl_out, dkc_out, dvl_out, dvc_out,
             dkl_ref, dkc_ref, dvl_ref, dvc_ref):
        j = pl.program_id(2)

        @pl.when(j == 0)
        def _():
            dkl_ref[...] = jnp.zeros(dkl_ref.shape, F32)
            dkc_ref[...] = jnp.zeros(dkc_ref.shape, F32)
            dvl_ref[...] = jnp.zeros(dvl_ref.shape, F32)
            dvc_ref[...] = jnp.zeros(dvc_ref.shape, F32)

        for hh in range(BWD_HEADS):
            wide = slice(hh * MLA_HEAD, (hh + 1) * MLA_HEAD)
            cols = slice(hh * 128, (hh + 1) * 128)
            qb = q_ref[:, wide]
            dob = do_ref[:, cols]
            delta = jnp.sum(dob.astype(F32) * o_ref[:, cols].astype(F32), axis=-1, keepdims=True)
            lse_row = lse_ref[:, hh * 128:hh * 128 + 1]

            def block(k_ref, v_ref, dk_ref, dv_ref, rows):
                kbl = k_ref[rows, wide]
                vbl = v_ref[rows, cols]
                p = jnp.exp(_dot_nt(qb, kbl) - lse_row)
                ds = (p * (_dot_nt(dob, vbl) - delta)).astype(BF16)
                dk_ref[rows, wide] += _dot_tn(ds, qb)
                dv_ref[rows, cols] += _dot_tn(p.astype(BF16), dob)
                return _dot(ds, kbl)

            dq = block(kc_ref, vc_ref, dkc_ref, dvc_ref, pl.ds(0, ctx_len))
            for i in range(seq // kb):
                dq = dq + block(kl_ref, vl_ref, dkl_ref, dvl_ref, pl.ds(i * kb, kb))
            dq_ref[:, wide] = dq.astype(BF16)

        @pl.when(j == nqt - 1)
        def _():
            dkl_out[...] = dkl_ref[...].astype(BF16)
            dkc_out[...] = dkc_ref[...].astype(BF16)
            dvl_out[...] = dvl_ref[...].astype(BF16)
            dvc_out[...] = dvc_ref[...].astype(BF16)

    g = BWD_HEADS
    qs, kl, kc, vl, vc, os_ = _mla_specs(t_lat, seq, ctx_len, tq, g)
    do_spec = pl.BlockSpec((tq, g * 128), lambda b, h, j: (b * nqt + j, HEADS // g + h))
    key_blocks = [(seq, g * MLA_HEAD), (ctx_len, g * MLA_HEAD), (seq, g * 128), (ctx_len, g * 128)]
    return _hosted_call(
        body, (q, k, k, v, v, ymla, lse, dmix), name="mla_bwd", grid=(nex, HEADS // g, nqt),
        in_specs=[qs, kl, kc, vl, vc, os_, os_, do_spec],
        out_specs=[qs] + [pl.BlockSpec(blk, lambda b, h, j: (b, h)) for blk in key_blocks],
        out_shape=[
            jax.ShapeDtypeStruct((t_lat, HEADS * MLA_HEAD), BF16),
            jax.ShapeDtypeStruct((t_lat, HEADS * MLA_HEAD), BF16),
            jax.ShapeDtypeStruct((t_ctx, HEADS * MLA_HEAD), BF16),
            jax.ShapeDtypeStruct((t_lat, HEADS * 128), BF16),
            jax.ShapeDtypeStruct((t_ctx, HEADS * 128), BF16),
        ],
        scratch_shapes=[pltpu.VMEM(blk, F32) for blk in key_blocks],
        sem=("parallel", "parallel", "arbitrary"), rider=rider)


def _decay_terms(lg, chunk, forward):
    ii = lax.broadcasted_iota(jnp.int32, (chunk, chunk), 0)
    jj = lax.broadcasted_iota(jnp.int32, (chunk, chunk), 1)
    diff = (ii - jj) if forward else (jj - ii)
    dist = jnp.maximum(diff, 0).astype(F32)
    dmat = jnp.where(diff >= 0, jnp.exp(lg * dist), 0.0)
    pos = lax.broadcasted_iota(jnp.int32, (chunk, 1), 0).astype(F32)
    if forward:
        e_q = pos + 1.0
        e_k = (chunk - 1.0) - pos
    else:
        e_q = chunk - pos
        e_k = pos
    wq = jnp.exp(lg * e_q)
    wk = jnp.exp(lg * e_k)
    cd = jnp.exp(jnp.full((1, 1), lg * chunk, F32))
    return dmat, dist, wq, wk, e_q, e_k, cd


def _ctx_weights(lg, ctx_len, forward):
    pos = lax.broadcasted_iota(jnp.int32, (ctx_len, 1), 0).astype(F32)
    e = ((ctx_len - 1.0) - pos) if forward else pos
    return jnp.exp(lg * e), e


def _pair_specs(t_lat, seq, ctx_len):
    cb = t_lat // ctx_len
    qk = pl.BlockSpec((seq, 128), lambda b, p: (b, p))
    v = pl.BlockSpec((seq, 256), lambda b, p: (b, p))
    kc = pl.BlockSpec((ctx_len, 128), lambda b, p: (cb + b, p))
    vc = pl.BlockSpec((ctx_len, 256), lambda b, p: (cb + b, p))
    return qk, v, kc, vc


def _lane_masks():
    lane = lax.broadcasted_iota(jnp.int32, (1, 128), 1)
    return [(lane // RET_DK) == hh for hh in (0, 1)]


def _ret_fwd_pair(rq, rk, rv, pg, lg, g_ret, *, t_lat, seq, ctx_len, chunk, rider=None):
    nex = t_lat // seq
    n_chunk = seq // chunk

    def body(q_ref, k_ref, v_ref, kc_ref, vc_ref, rg_ref, lg_ref, g_ref, y_ref, o_ref):
        pair = pl.program_id(1)
        masks = _lane_masks()
        kcf = kc_ref[...].astype(F32)
        chains = [(forward, hh) for forward in (True, False) for hh in (0, 1)]
        terms, s0 = [], []
        for forward, hh in chains:
            lgd = lg_ref[0 if forward else 1, 2 * pair + hh]
            terms.append(_decay_terms(lgd, chunk, forward))
            wc, _ = _ctx_weights(lgd, ctx_len, forward)
            s0.append(_dot_tn((jnp.where(masks[hh], kcf, 0.0) * wc).astype(BF16), vc_ref[:, hh * 128:(hh + 1) * 128]))
        both = [terms[hh][0] + terms[2 + hh][0] for hh in (0, 1)]
        o_ref[...] = jnp.zeros(o_ref.shape, F32)

        def step(t, states):
            new = [None] * 4
            for forward in (True, False):
                n = t if forward else n_chunk - 1 - t
                sl = pl.ds(pl.multiple_of(n * chunk, chunk), chunk)
                qb = q_ref[sl, :]
                kf_all = k_ref[sl, :].astype(F32)
                for hh in (0, 1):
                    c = (0 if forward else 2) + hh
                    _, _, wq, wk, _, _, cd = terms[c]
                    cols = slice(hh * 128, (hh + 1) * 128)
                    qm = jnp.where(masks[hh], qb, jnp.zeros((), BF16))
                    kf = jnp.where(masks[hh], kf_all, 0.0)
                    vb = v_ref[sl, cols]
                    o = wq * _dot(qm, states[c].astype(BF16))
                    if forward:
                        o = o + _dot((_dot_nt(qm, kf.astype(BF16)) * both[hh]).astype(BF16), vb)
                    o_ref[sl, cols] += o
                    new[c] = cd * states[c] + _dot_tn((kf * wk).astype(BF16), vb)
            return tuple(new)

        lax.fori_loop(0, n_chunk, step, tuple(s0))

        def norm_step(n, carry):
            sl = pl.ds(pl.multiple_of(n * chunk, chunk), chunk)
            for hh in (0, 1):
                cols = slice(hh * 128, (hh + 1) * 128)
                o = o_ref[sl, cols]
                mu = jnp.mean(o, axis=-1, keepdims=True)
                oc = o - mu
                var = jnp.mean(oc * oc, axis=-1, keepdims=True)
                rg = rg_ref[sl, cols]
                y_ref[sl, cols] = (oc * lax.rsqrt(var + EPS) * g_ref[:, cols] * (rg / (1.0 + jnp.exp(-rg)))).astype(BF16)
            return carry

        lax.fori_loop(0, n_chunk, norm_step, 0)

    qk, v, kc, vc = _pair_specs(t_lat, seq, ctx_len)
    return _hosted_call(
        body, (rq, rk, rv, rk, rv, pg, lg, g_ret), name="ret_fwd", grid=(nex, HEADS // 2),
        in_specs=[qk, qk, v, kc, vc, v, pl.BlockSpec(memory_space=pltpu.SMEM), pl.BlockSpec((1, 256), lambda b, p: (0, p))],
        out_specs=[v, v],
        out_shape=[jax.ShapeDtypeStruct((t_lat, HEADS * RET_DV), BF16), jax.ShapeDtypeStruct((t_lat, HEADS * RET_DV), F32)],
        sem=("parallel", "arbitrary"), rider=rider)


def _ret_bwd_pair(rq, rk, rv, pg, osum, dmix, lg, g_ret, *, t_lat, seq, ctx_len, chunk, rider=None):
    nex = t_lat // seq
    n_chunk = seq // chunk
    t_ctx = nex * ctx_len

    def body(q_ref, k_ref, v_ref, kc_ref, vc_ref, rg_ref, o_ref, dy_ref, lg_ref, g_ref,
             dq_out, dk_out, dv_out, dkc_ref, dvc_ref, drg_ref, st_ref, do_s, s_st, dq_ref, dk_ref, dv_ref):
        pair = pl.program_id(1)
        masks = _lane_masks()
        kcf = kc_ref[...].astype(F32)

        def norm_step(n, dgains):
            sl = pl.ds(pl.multiple_of(n * chunk, chunk), chunk)
            out = []
            for hh in (0, 1):
                cols = slice(hh * 128, (hh + 1) * 128)
                gain = g_ref[:, cols]
                o = o_ref[sl, cols]
                mu = jnp.mean(o, axis=-1, keepdims=True)
                oc = o - mu
                rstd = lax.rsqrt(jnp.mean(oc * oc, axis=-1, keepdims=True) + EPS)
                ohat = oc * rstd
                rg = rg_ref[sl, cols]
                sg = 1.0 / (1.0 + jnp.exp(-rg))
                dy = dy_ref[sl, cols].astype(F32)
                don = dy * (rg * sg)
                drg_ref[sl, cols] = (dy * (ohat * gain) * (sg * (1.0 + rg * (1.0 - sg)))).astype(BF16)
                dohat = don * gain
                do_s[sl, cols] = rstd * (dohat - jnp.mean(dohat, axis=-1, keepdims=True)
                                         - ohat * jnp.mean(dohat * ohat, axis=-1, keepdims=True))
                out.append(dgains[hh] + jnp.sum(don * ohat, axis=0, keepdims=True))
            return tuple(out)

        zero_row = jnp.zeros((1, 128), F32)
        dgains = lax.fori_loop(0, n_chunk, norm_step, (zero_row, zero_row))
        dq_ref[...] = jnp.zeros(dq_ref.shape, F32)
        dk_ref[...] = jnp.zeros(dk_ref.shape, F32)
        dv_ref[...] = jnp.zeros(dv_ref.shape, F32)

        chains = [(forward, hh) for forward in (True, False) for hh in (0, 1)]
        terms, ctxw, s0 = [], [], []
        for forward, hh in chains:
            lgd = lg_ref[0 if forward else 1, 2 * pair + hh]
            terms.append(_decay_terms(lgd, chunk, forward))
            ctxw.append(_ctx_weights(lgd, ctx_len, forward))
            s0.append(_dot_tn((jnp.where(masks[hh], kcf, 0.0) * ctxw[-1][0]).astype(BF16), vc_ref[:, hh * 128:(hh + 1) * 128]))

        def chunk_at(t, ascending):
            n = t if ascending else n_chunk - 1 - t
            return n, pl.ds(pl.multiple_of(n * chunk, chunk), chunk)

        def state_step(t, states):
            new = []
            for c, (forward, hh) in enumerate(chains):
                n, sl = chunk_at(t, forward)
                wk, cd = terms[c][3], terms[c][6]
                s_st[c, n] = states[c]
                kf = jnp.where(masks[hh], k_ref[sl, :].astype(F32), 0.0)
                new.append(cd * states[c] + _dot_tn((kf * wk).astype(BF16), v_ref[sl, hh * 128:(hh + 1) * 128]))
            return tuple(new)

        lax.fori_loop(0, n_chunk, state_step, tuple(s0))

        both = [terms[hh][0] + terms[2 + hh][0] for hh in (0, 1)]

        def grad_step(t, carry):
            out = [None] * len(chains)
            in_chunk_b = [None, None]
            for forward in (True, False):
                n, sl = chunk_at(t, not forward)
                qb = q_ref[sl, :]
                kf_all = k_ref[sl, :].astype(F32)
                dq_sum = jnp.zeros((chunk, 128), F32)
                dk_sum = jnp.zeros((chunk, 128), F32)
                for hh in (0, 1):
                    c = (0 if forward else 2) + hh
                    g_next, dlg = carry[c]
                    dmat, dist, wq, wk, e_q, e_k, cd = terms[c]
                    cols = slice(hh * 128, (hh + 1) * 128)
                    qm = jnp.where(masks[hh], qb, jnp.zeros((), BF16))
                    kf = jnp.where(masks[hh], kf_all, 0.0)
                    kb = kf.astype(BF16)
                    vb = v_ref[sl, cols]
                    do = do_s[sl, cols]
                    dob = do.astype(BF16)
                    s_n = s_st[c, n]
                    s_nb = s_n.astype(BF16)
                    gb = g_next.astype(BF16)
                    dk_cross = wk * _dot_nt(vb, gb)
                    dv = _dot((kf * wk).astype(BF16), gb)
                    o_cross = wq * _dot(qm, s_nb)
                    dq_sum = dq_sum + wq * _dot_nt(dob, s_nb)
                    dk_sum = dk_sum + dk_cross
                    dlg = (dlg + chunk * cd * jnp.sum(g_next * s_n, keepdims=True)
                           + jnp.sum(e_k * jnp.sum(kf * dk_cross, axis=-1, keepdims=True), keepdims=True)
                           + jnp.sum(e_q * jnp.sum(o_cross * do, axis=-1, keepdims=True), keepdims=True))
                    if forward:
                        a_raw = _dot_nt(qm, kb)
                        da_raw = _dot_nt(dob, vb)
                        prod = a_raw * da_raw
                        dlg = dlg + jnp.sum(dist * dmat * prod, keepdims=True)
                        in_chunk_b[hh] = jnp.sum(terms[2 + hh][1] * terms[2 + hh][0] * prod, keepdims=True)
                        dab = (da_raw * both[hh]).astype(BF16)
                        dq_sum = dq_sum + _dot(dab, kb)
                        dk_sum = dk_sum + _dot_tn(dab, qm)
                        dv = dv + _dot_tn((a_raw * both[hh]).astype(BF16), dob)
                    else:
                        dlg = dlg + in_chunk_b[hh]
                    dv_ref[sl, cols] += dv
                    out[c] = (cd * g_next + _dot_tn((qm.astype(F32) * wq).astype(BF16), dob), dlg)
                dq_ref[sl, :] += dq_sum
                dk_ref[sl, :] += dk_sum
            return tuple(out)

        zero = (jnp.zeros((128, 128), F32), jnp.zeros((1, 1), F32))
        res = lax.fori_loop(0, n_chunk, grad_step, (zero,) * len(chains))
        dkc_sum = jnp.zeros((ctx_len, 128), F32)
        dvc = [jnp.zeros((ctx_len, 128), F32)] * 2
        dlgs = []
        for c, (forward, hh) in enumerate(chains):
            ds0, dlg = res[c]
            wc, e_c = ctxw[c]
            kcm = jnp.where(masks[hh], kcf, 0.0)
            ds0b = ds0.astype(BF16)
            dkc_part = wc * _dot_nt(vc_ref[:, hh * 128:(hh + 1) * 128], ds0b)
            dkc_sum = dkc_sum + dkc_part
            dvc[hh] = dvc[hh] + _dot((kcm * wc).astype(BF16), ds0b)
            dlgs.append(dlg + jnp.sum(e_c * jnp.sum(kcm * dkc_part, axis=-1, keepdims=True), keepdims=True))
        dq_out[...] = dq_ref[...].astype(BF16)
        dk_out[...] = dk_ref[...].astype(BF16)
        dv_out[...] = dv_ref[...].astype(BF16)
        dkc_ref[...] = dkc_sum
        for hh in (0, 1):
            cols = slice(hh * 128, (hh + 1) * 128)
            dvc_ref[:, cols] = dvc[hh]
            st_ref[0, :, cols] = jnp.concatenate([
                dgains[hh], jnp.broadcast_to(dlgs[hh], (1, 128)), jnp.broadcast_to(dlgs[2 + hh], (1, 128)),
                jnp.zeros((5, 128), F32)], axis=0)

    qk, v, kc, vc = _pair_specs(t_lat, seq, ctx_len)
    return _hosted_call(
        body, (rq, rk, rv, rk, rv, pg, osum, dmix, lg, g_ret), name="ret_bwd", grid=(nex, HEADS // 2),
        in_specs=[qk, qk, v, kc, vc, v, v, v, pl.BlockSpec(memory_space=pltpu.SMEM),
                  pl.BlockSpec((1, 256), lambda b, p: (0, p))],
        out_specs=[
            qk, qk, v,
            pl.BlockSpec((ctx_len, 128), lambda b, p: (b, p)),
            pl.BlockSpec((ctx_len, 256), lambda b, p: (b, p)),
            v,
            pl.BlockSpec((1, 8, 256), lambda b, p: (b, 0, p)),
        ],
        out_shape=[
            jax.ShapeDtypeStruct((t_lat, 256), BF16), jax.ShapeDtypeStruct((t_lat, 256), BF16),
            jax.ShapeDtypeStruct((t_lat, 512), BF16), jax.ShapeDtypeStruct((t_ctx, 256), F32),
            jax.ShapeDtypeStruct((t_ctx, 512), F32), jax.ShapeDtypeStruct((t_lat, 512), BF16),
            jax.ShapeDtypeStruct((nex, 8, 512), F32),
        ],
        scratch_shapes=[pltpu.VMEM((seq, 256), F32), pltpu.VMEM((4, n_chunk, 128, 128), F32),
                        pltpu.VMEM((seq, 128), F32), pltpu.VMEM((seq, 128), F32), pltpu.VMEM((seq, 256), F32)],
        sem=("parallel", "arbitrary"), rider=rider)


def _matmul_tn(a, b, *, bm, bn, bk, chip_major, name, out_dtype=F32, rider=None):
    tk, m = a.shape
    n = b.shape[1]
    slab = n // N_CHIPS
    per_block = bn // slab if chip_major else 1
    bk = max(c for c in range(LANES, min(bk, tk) + 1, LANES) if tk % c == 0)
    nk = tk // bk
    blk = (per_block, bm, slab) if chip_major else (bm, bn)

    def body(a_ref, b_ref, o_ref, acc_ref):
        k = pl.program_id(2)
        if chip_major:
            parts = [_dot_tn(a_ref[...], b_ref[:, s * slab:(s + 1) * slab]) for s in range(per_block)]
        else:
            parts = [_dot_tn(a_ref[...], b_ref[...])]

        @pl.when(k == 0)
        def _():
            for s, part in enumerate(parts):
                if chip_major:
                    acc_ref[s] = part
                else:
                    acc_ref[...] = part

        @pl.when(k > 0)
        def _():
            for s, part in enumerate(parts):
                if chip_major:
                    acc_ref[s] += part
                else:
                    acc_ref[...] += part

        @pl.when(k == nk - 1)
        def _():
            o_ref[...] = acc_ref[...].astype(out_dtype)

    if chip_major:
        out_spec = pl.BlockSpec(blk, lambda i, j, k: (j, i, 0))
        out_shape = jax.ShapeDtypeStruct((N_CHIPS, m, slab), out_dtype)
    else:
        out_spec = pl.BlockSpec(blk, lambda i, j, k: (i, j))
        out_shape = jax.ShapeDtypeStruct((m, n), out_dtype)
    (out,), carried = _hosted_call(
        body, (a, b), name=name, grid=(m // bm, n // bn, nk),
        in_specs=[pl.BlockSpec((bk, bm), lambda i, j, k: (k, i)), pl.BlockSpec((bk, bn), lambda i, j, k: (k, j))],
        out_specs=[out_spec], out_shape=[out_shape], scratch_shapes=[pltpu.VMEM(blk, F32)],
        sem=("parallel", "parallel", "arbitrary"), rider=rider)
    return out if rider is None else (out, carried)


_LATE = ("w_out", "w_ff1", "w_ff2")
_EARLY = ("w_in", "w_uq", "w_ukv")


def _local_step(x, ctx, tgt, modv, lg, g_attn, g_ffn, g_fin, g_ret, g_q, g_kv, w_in, w_uq, w_ukv, late, place=None,
                *, tm=256, tq=256, chunk=256):
    nex, seq, d = x.shape
    ctx_len = ctx.shape[1]
    t_lat = nex * seq
    tm = min(tm, seq)
    x2 = x.reshape(t_lat, d)
    ctx2 = ctx.reshape(nex * ctx_len, d)
    tgt2 = tgt.reshape(t_lat, d)
    tm_fwd = min(2 * tm, seq)
    cos_t, sin_t = _rope_tables(seq, tm)
    dims = dict(t_lat=t_lat, seq=seq, ctx_len=ctx_len)
    alone = place is None

    (hb, pg, rq, rk, rv, nq, nkv, q, k, v), crossed_a = _pre_fwd(
        x2, ctx2, modv, g_attn, w_in, g_q, g_kv, w_uq, w_ukv, *_rope_tables(seq, tm_fwd), seq=seq, tm=tm_fwd,
        rider=None if alone else _gather_ici_rider([late[2]]))
    (yret, osum), got = _ret_fwd_pair(
        rq, rk, rv, pg, lg, g_ret, chunk=min(2 * chunk, seq), **dims,
        rider=None if alone else _merge_riders(_gather_d2d_rider(crossed_a), _gather_ici_rider([late[3]])))
    (ymla, lse), got_rest = _mla_fwd(
        q, k, v, tq=tq, **dims,
        rider=None if alone else _merge_riders(_gather_rider([late[0], late[1]], staged=True), _gather_d2d_rider(got[1:])))
    w_out, w_ff1, w_ff2a, w_ff2b = late if alone else (got_rest[0], got_rest[1], got[0], got_rest[2])
    mix, act, du, h2, df, dmo, dmix, dxm, st_post = _post(yret, ymla, x2, tgt2, modv, g_ffn, g_fin, w_out.reshape(d, d),
                                                         w_ff1, w_ff2a, w_ff2b, seq=seq, tm=min(tm, 256))
    kw = dict(bm=1024, bn=1024, bk=2048, out_dtype=BF16)
    g_ff2 = _matmul_tn(act, df, chip_major=False, name="gw_ff2", **kw).reshape(N_CHIPS, D_FF // N_CHIPS, d)
    if alone:
        g_ff1 = _matmul_tn(h2, du, chip_major=True, name="gw_ff1", **kw)
        g_out = _matmul_tn(mix, dmo, chip_major=False, name="gw_out", **kw).reshape(N_CHIPS, d // N_CHIPS, d)
        (dq_m, dkl, dkc, dvl, dvc), _ = _mla_bwd(q, k, v, ymla, lse, dmix, tq=tq, **dims)
        (drq, drk, drv, dkc_r, dvc_r, drg, st_ret), _ = _ret_bwd_pair(rq, rk, rv, pg, osum, dmix, lg, g_ret, chunk=chunk,
                                                                      **dims)
        late_out = [g_out, g_ff1, g_ff2]
    else:
        core, slot = place
        g_ff1, x_ff2 = _matmul_tn(h2, du, chip_major=True, name="gw_ff1", rider=_exchange_rider([g_ff2]), **kw)
        g_out, x_ff1 = _matmul_tn(mix, dmo, chip_major=False, name="gw_out", rider=_exchange_rider([g_ff1]), **kw)
        g_out = g_out.reshape(N_CHIPS, d // N_CHIPS, d)
        p_ff2 = _add_half(g_ff2, x_ff2[0], core, "add_half_w_ff2")
        p_ff1 = _add_half(g_ff1, x_ff1[0], core, "add_half_w_ff1")
        (dq_m, dkl, dkc, dvl, dvc), (l_ff2, l_ff1, x_out) = _mla_bwd(
            q, k, v, ymla, lse, dmix, tq=min(seq, 512), **dims,
            rider=_merge_riders(_scatter_rider([p_ff2, p_ff1]), _exchange_rider([g_out])))
        p_out = _add_half(g_out, x_out, core, "add_half_w_out")
        m_ff2 = _sum_chips(p_ff2, l_ff2, slot, "sum_chips_w_ff2")
        m_ff1 = _sum_chips(p_ff1, l_ff1, slot, "sum_chips_w_ff1")
        (drq, drk, drv, dkc_r, dvc_r, drg, st_ret), (l_out,) = _ret_bwd_pair(
            rq, rk, rv, pg, osum, dmix, lg, g_ret, chunk=chunk, **dims, rider=_scatter_rider([p_out]))
        late_out = [_sum_chips(p_out, l_out, slot, "sum_chips_w_out"), m_ff1, m_ff2]
    (dpb, dqf, dkvf, gx, st_pre), _ = _pre_bwd(
        x2, ctx2, modv, g_attn, pg, drq, drk, dkc_r, drv, dvc_r, drg, dq_m, dkl, dkc, dvl, dvc, dxm, w_in, g_q, g_kv,
        w_uq, w_ukv, cos_t, sin_t, seq=seq, tm=tm)
    g_early = [
        _matmul_tn(dpb, hb, bm=IN_PAD // 2, bn=d, bk=1536, chip_major=False, name="gw_in"),
        _matmul_tn(dqf, nq, bm=HEADS * MLA_HEAD, bn=Q_LORA, bk=1536, chip_major=False, name="gw_uq"),
        _matmul_tn(nkv, dkvf, bm=KV_LORA, bn=HEADS * 256, bk=1536, chip_major=True, name="gw_ukv"),
    ]
    return gx.reshape(nex, seq, d), g_early, late_out, st_post, st_ret, st_pre


_ANY = pl.BlockSpec(memory_space=pl.ANY)
_VMEM = pl.BlockSpec(memory_space=pltpu.VMEM)
_OFFSETS = tuple((dx, dy, dc) for dx in (0, 1) for dy in (0, 1) for dc in (0, 1))[1:]
_CHIP_OFFSETS = ((1, 0), (0, 1), (1, 1))


def _place():
    return lax.axis_index("x"), lax.axis_index("y"), lax.axis_index("c")


def _flip(v, d):
    return 1 - v if d else v


def _gather8_rider(a, in_vmem=True):
    def copies(a_ref, o_ref, send, recv):
        x, y, z = _place()
        me = 4 * x + 2 * y + z
        out = []
        for k, (dx, dy, dc) in enumerate(_OFFSETS):
            peer = (_flip(x, dx), _flip(y, dy), _flip(z, dc))
            landing = o_ref.at[4 * peer[0] + 2 * peer[1] + peer[2]]
            out.append((
                pltpu.make_async_remote_copy(src_ref=a_ref, dst_ref=o_ref.at[me], send_sem=send.at[k],
                                             recv_sem=recv.at[k], device_id=peer, device_id_type=MESH),
                pltpu.make_async_remote_copy(src_ref=a_ref, dst_ref=landing, send_sem=send.at[k],
                                             recv_sem=recv.at[k], device_id=peer, device_id_type=MESH)))
        return me, out

    def start(ins, outs, sems):
        me, cps = copies(ins[0], outs[0], sems[0], sems[1])
        pltpu.make_async_copy(ins[0], outs[0].at[me], sems[2]).start()
        for out_cp, _ in cps:
            out_cp.start()

    def finish(ins, outs, sems):
        me, cps = copies(ins[0], outs[0], sems[0], sems[1])
        for out_cp, in_cp in cps:
            in_cp.wait_recv()
            out_cp.wait_send()
        pltpu.make_async_copy(ins[0], outs[0].at[me], sems[2]).wait()

    spec = [_VMEM] if in_vmem else [_ANY]
    return _Rider([a], [jax.ShapeDtypeStruct((N_DEV,) + a.shape, a.dtype)],
                  [pltpu.SemaphoreType.DMA((7,)), pltpu.SemaphoreType.DMA((7,)), pltpu.SemaphoreType.DMA],
                  start, finish, in_specs=spec, out_specs=spec)


def _merge_riders(*riders):
    ins, outs, sems, in_specs, out_specs, aliases, cuts = [], [], [], [], [], {}, []
    for r in riders:
        cuts.append((len(ins), len(outs), len(sems)))
        aliases.update({len(ins) + i: len(outs) + j for i, j in r.aliases.items()})
        ins += r.ins
        outs += r.out_shapes
        sems += r.sems
        in_specs += r.in_specs
        out_specs += r.out_specs

    def part(r, cut, r_ins, r_outs, r_sems):
        return (r_ins[cut[0]:cut[0] + len(r.ins)], r_outs[cut[1]:cut[1] + len(r.out_shapes)],
                r_sems[cut[2]:cut[2] + len(r.sems)])

    def start(r_ins, r_outs, r_sems):
        for r, cut in zip(riders, cuts):
            r.start(*part(r, cut, r_ins, r_outs, r_sems))

    def finish(r_ins, r_outs, r_sems):
        for r, cut in zip(riders, cuts):
            r.finish(*part(r, cut, r_ins, r_outs, r_sems))

    def middle(r_ins, r_outs, r_sems):
        for r, cut in zip(riders, cuts):
            if r.middle is not None:
                r.middle(*part(r, cut, r_ins, r_outs, r_sems))

    return _Rider(ins, outs, sems, start, finish, aliases=aliases, in_specs=in_specs, out_specs=out_specs,
                  middle=middle if any(r.middle is not None for r in riders) else None)


def _allgather8(a, name):
    return _run_rider(_gather8_rider(a), name)[0]


BF16_TILE_ROWS = 16


def _half(o, slot, which):
    r2 = o.shape[1] // 2
    if r2 % BF16_TILE_ROWS == 0:
        return o.at[slot, pl.ds(which * r2, r2)]
    c2 = o.shape[2] // 2
    assert c2 % LANES == 0
    return o.at[slot, :, pl.ds(which * c2, c2)]


def _gather_send(o_refs, send, recv):
    x, y, z = _place()
    chip = 2 * x + y
    for a, o in enumerate(o_refs):
        r2 = o.shape[1] // 2
        mine = _half(o, chip, z)
        for k, (dx, dy) in enumerate(_CHIP_OFFSETS):
            pltpu.make_async_remote_copy(
                src_ref=mine, dst_ref=mine, send_sem=send.at[a, k], recv_sem=recv.at[a, k],
                device_id=(_flip(x, dx), _flip(y, dy), z), device_id_type=MESH).start()


def _gather_landed(o_refs, send, recv, then=None):
    x, y, z = _place()
    chip = 2 * x + y
    for a, o in enumerate(o_refs):
        for k, (dx, dy) in enumerate(_CHIP_OFFSETS):
            landed = _half(o, 2 * _flip(x, dx) + _flip(y, dy), z)
            pltpu.make_async_remote_copy(
                src_ref=landed, dst_ref=landed, send_sem=send.at[a, k], recv_sem=recv.at[a, k],
                device_id=(_flip(x, dx), _flip(y, dy), z), device_id_type=MESH).wait_recv()
            if then is not None:
                then(a, k, landed)
    for a, o in enumerate(o_refs):
        mine = _half(o, chip, z)
        for k, (dx, dy) in enumerate(_CHIP_OFFSETS):
            pltpu.make_async_remote_copy(
                src_ref=mine, dst_ref=mine, send_sem=send.at[a, k], recv_sem=recv.at[a, k],
                device_id=(_flip(x, dx), _flip(y, dy), z), device_id_type=MESH).wait_send()


def _pass_on(o_refs, fsend, frecv, a, k, landed):
    x, y, z = _place()
    pltpu.make_async_remote_copy(
        src_ref=landed, dst_ref=landed, send_sem=fsend.at[a, k], recv_sem=frecv.at[a, k],
        device_id=(x, y, 1 - z), device_id_type=MESH).start()


def _passed_on(o_refs, fsend, frecv):
    x, y, z = _place()
    for a, o in enumerate(o_refs):
        for k, (dx, dy) in enumerate(_CHIP_OFFSETS):
            other = 2 * _flip(x, dx) + _flip(y, dy)
            got = _half(o, other, 1 - z)
            gave = _half(o, other, z)
            pltpu.make_async_remote_copy(
                src_ref=got, dst_ref=got, send_sem=fsend.at[a, k], recv_sem=frecv.at[a, k],
                device_id=(x, y, 1 - z), device_id_type=MESH).wait_recv()
            pltpu.make_async_remote_copy(
                src_ref=gave, dst_ref=gave, send_sem=fsend.at[a, k], recv_sem=frecv.at[a, k],
                device_id=(x, y, 1 - z), device_id_type=MESH).wait_send()


def _gather_finish(o_refs, send, recv, fsend, frecv):
    _gather_landed(o_refs, send, recv, functools.partial(_pass_on, o_refs, fsend, frecv))
    _passed_on(o_refs, fsend, frecv)


class _Rider:
    def __init__(self, ins, out_shapes, sems, start, finish, aliases=None, in_specs=None, out_specs=None, middle=None):
        self.ins, self.out_shapes, self.sems = list(ins), list(out_shapes), list(sems)
        self.start, self.finish, self.aliases = start, finish, dict(aliases or {})
        self.middle = middle
        self.in_specs = list(in_specs) if in_specs else [_ANY] * len(self.ins)
        self.out_specs = list(out_specs) if out_specs else [_ANY] * len(self.out_shapes)


def _run_rider(rider, name):
    r_in, r_out = len(rider.ins), len(rider.out_shapes)

    def body(*refs):
        ins, outs, sems = refs[:r_in], refs[r_in:r_in + r_out], refs[r_in + r_out:]
        rider.start(ins, outs, sems)
        if rider.middle is not None:
            rider.middle(ins, outs, sems)
        rider.finish(ins, outs, sems)

    return pl.pallas_call(
        body, name=name, in_specs=rider.in_specs, out_specs=rider.out_specs, out_shape=rider.out_shapes,
        input_output_aliases=rider.aliases, scratch_shapes=rider.sems,
    )(*rider.ins)


def _hosted_call(body, args, *, name, grid, in_specs, out_specs, out_shape, scratch_shapes=(), sem, rider=None,
                 prefetch=()):
    scratch_shapes = list(scratch_shapes)
    n_pf, n_in, n_out, n_sc = len(prefetch), len(in_specs), len(out_specs), len(scratch_shapes)
    r_in, r_out = (len(rider.ins), len(rider.out_shapes)) if rider else (0, 0)
    last = tuple(g - 1 for g in grid)

    def hosted(*refs):
        p = 0
        parts = []
        for cnt in (n_pf, n_in, r_in, n_out, r_out, n_sc):
            parts.append(refs[p:p + cnt])
            p += cnt
        pf, ins, r_ins, outs, r_outs, scratch = parts
        sems = refs[p:]
        ids = [pl.program_id(a) for a in range(len(grid))]
        is_first = functools.reduce(jnp.logical_and, [i == 0 for i in ids])
        is_last = functools.reduce(jnp.logical_and, [i == e for i, e in zip(ids, last)])

        @pl.when(is_first)
        def _():
            rider.start(r_ins, r_outs, sems)

        if rider.middle is not None:
            linear = functools.reduce(lambda acc, ig: acc * ig[1] + ig[0], zip(ids, grid), 0)

            @pl.when(linear == math.prod(grid) * 3 // 4)
            def _():
                rider.middle(r_ins, r_outs, sems)

        body(*pf, *ins, *outs, *scratch)

        @pl.when(is_last)
        def _():
            rider.finish(r_ins, r_outs, sems)

    if rider is None:
        kern, all_in, all_out, shapes, scratch, aliases, extra = body, list(in_specs), list(out_specs), list(out_shape), \
            scratch_shapes, {}, []
    else:
        kern, all_in, all_out = hosted, list(in_specs) + rider.in_specs, list(out_specs) + rider.out_specs
        shapes, scratch, extra = list(out_shape) + rider.out_shapes, scratch_shapes + rider.sems, rider.ins
        aliases = {n_pf + n_in + i: n_out + j for i, j in rider.aliases.items()}
        sem = ("arbitrary",) * len(grid)
    if prefetch:
        spec = dict(grid_spec=pltpu.PrefetchScalarGridSpec(
            num_scalar_prefetch=n_pf, grid=grid, in_specs=all_in, out_specs=all_out, scratch_shapes=scratch))
    else:
        spec = dict(grid=grid, in_specs=all_in, out_specs=all_out, scratch_shapes=scratch)
    res = pl.pallas_call(kern, name=name, out_shape=shapes, input_output_aliases=aliases,
                         compiler_params=_params(sem, VMEM_LIMIT), **spec)(*prefetch, *args, *extra)
    return list(res[:n_out]), list(res[n_out:])


def _gather_rider(ws, staged=False):
    n = len(ws)
    shapes = [jax.ShapeDtypeStruct(w.shape, w.dtype) for w in ws]
    sems = [pltpu.SemaphoreType.DMA((n, 3))] * 4
    aliases = {a: a for a in range(n)}

    def start(ins, outs, s):
        _gather_send(outs, s[0], s[1])

    if not staged:
        return _Rider(ws, shapes, sems, start, lambda ins, outs, s: _gather_finish(outs, *s), aliases=aliases)
    return _Rider(
        ws, shapes, sems, start, lambda ins, outs, s: _passed_on(outs, s[2], s[3]), aliases=aliases,
        middle=lambda ins, outs, s: _gather_landed(outs, s[0], s[1], functools.partial(_pass_on, outs, s[2], s[3])))


def _gather_ici_rider(ws):
    n = len(ws)
    return _Rider(
        ws, [jax.ShapeDtypeStruct(w.shape, w.dtype) for w in ws], [pltpu.SemaphoreType.DMA((n, 3))] * 2,
        lambda ins, outs, sems: _gather_send(outs, sems[0], sems[1]),
        lambda ins, outs, sems: _gather_landed(outs, sems[0], sems[1]),
        aliases={a: a for a in range(n)})


def _gather_d2d_rider(ws):
    n = len(ws)

    def start(ins, outs, sems):
        x, y, z = _place()
        for a, o in enumerate(outs):
            for k, (dx, dy) in enumerate(_CHIP_OFFSETS):
                _pass_on(outs, sems[0], sems[1], a, k, _half(o, 2 * _flip(x, dx) + _flip(y, dy), z))

    return _Rider(
        ws, [jax.ShapeDtypeStruct(w.shape, w.dtype) for w in ws], [pltpu.SemaphoreType.DMA((n, 3))] * 2,
        start, lambda ins, outs, sems: _passed_on(outs, sems[0], sems[1]), aliases={a: a for a in range(n)})


def _copies_rider(ins, out_shapes, sem_shape, make):
    def start(r_ins, r_outs, sems):
        for cp in make(r_ins, r_outs, sems[0], sems[1]):
            cp.start()

    def finish(r_ins, r_outs, sems):
        for cp in make(r_ins, r_outs, sems[0], sems[1]):
            cp.wait()

    return _Rider(ins, out_shapes, [pltpu.SemaphoreType.DMA(sem_shape)] * 2, start, finish)


def _exchange_rider(gs):
    def make(g_refs, r_refs, send, recv):
        x, y, z = _place()
        return [pltpu.make_async_remote_copy(
            src_ref=g.at[:, pl.ds((1 - z) * (g.shape[1] // 2), g.shape[1] // 2)], dst_ref=r, send_sem=send.at[a],
            recv_sem=recv.at[a], device_id=(x, y, 1 - z), device_id_type=MESH)
            for a, (g, r) in enumerate(zip(g_refs, r_refs))]

    shapes = [jax.ShapeDtypeStruct((g.shape[0], g.shape[1] // 2, g.shape[2]), g.dtype) for g in gs]
    return _copies_rider(gs, shapes, (len(gs),), make)


def _add_half(g, recv, core, name):
    s, r, c = g.shape
    r2 = r // 2
    rb = r2
    for cand in (512, 256, 128, 64):
        if r2 % cand == 0:
            rb = cand
            break
    g4 = g.reshape(s, 2, r2, c)

    def body(core_ref, g_ref, r_ref, o_ref):
        o_ref[...] = (g_ref[...].astype(F32) + r_ref[...].astype(F32)).astype(BF16)

    return pl.pallas_call(
        body, name=name,
        grid_spec=pltpu.PrefetchScalarGridSpec(
            num_scalar_prefetch=1, grid=(s, r2 // rb),
            in_specs=[pl.BlockSpec((None, None, rb, c), lambda i, j, cr: (i, cr[0], j, 0)),
                      pl.BlockSpec((None, rb, c), lambda i, j, cr: (i, j, 0))],
            out_specs=pl.BlockSpec((None, rb, c), lambda i, j, cr: (i, j, 0))),
        out_shape=jax.ShapeDtypeStruct((s, r2, c), BF16),
        compiler_params=_params(("parallel", "parallel")),
    )(core, g4, recv)


def _scatter_rider(ps):
    def make(p_refs, o_refs, send, recv):
        x, y, z = _place()
        copies = []
        for a, (p, o) in enumerate(zip(p_refs, o_refs)):
            for k, (dx, dy) in enumerate(_CHIP_OFFSETS):
                other = 2 * _flip(x, dx) + _flip(y, dy)
                copies.append(pltpu.make_async_remote_copy(
                    src_ref=p.at[other], dst_ref=o.at[k], send_sem=send.at[a, k], recv_sem=recv.at[a, k],
                    device_id=(_flip(x, dx), _flip(y, dy), z), device_id_type=MESH))
        return copies

    shapes = [jax.ShapeDtypeStruct((3,) + p.shape[1:], p.dtype) for p in ps]
    return _copies_rider(ps, shapes, (len(ps), 3), make)


def _sum_chips(p, landed, chip, name):
    _, r2, c = p.shape
    rb = r2
    for cand in (256, 128, 64):
        if r2 % cand == 0:
            rb = cand
            break

    def body(s_ref, p_ref, l_ref, o_ref):
        acc = p_ref[...].astype(F32)
        for k in range(3):
            acc = acc + l_ref[k].astype(F32)
        o_ref[...] = acc

    return pl.pallas_call(
        body, name=name,
        grid_spec=pltpu.PrefetchScalarGridSpec(
            num_scalar_prefetch=1, grid=(r2 // rb,),
            in_specs=[pl.BlockSpec((None, rb, c), lambda i, s: (s[0], i, 0)),
                      pl.BlockSpec((3, rb, c), lambda i, s: (0, i, 0))],
            out_specs=pl.BlockSpec((rb, c), lambda i, s: (i, 0))),
        out_shape=jax.ShapeDtypeStruct((r2, c), F32),
        compiler_params=_params(("parallel",)),
    )(chip, p, landed)


def _swap_rider(hs):
    def make(h_refs, o_refs, send, recv):
        x, y, z = _place()
        return [pltpu.make_async_remote_copy(
            src_ref=h, dst_ref=o, send_sem=send.at[a], recv_sem=recv.at[a], device_id=(x, y, 1 - z),
            device_id_type=MESH) for a, (h, o) in enumerate(zip(h_refs, o_refs))]

    return _copies_rider(hs, [jax.ShapeDtypeStruct(h.shape, h.dtype) for h in hs], (len(hs),), make)


def _reduce_scatter_vmem(gs, rows, rider, name):
    n = len(gs)
    r_in, r_out = len(rider.ins), len(rider.out_shapes)
    halves = [(r // 2, g.shape[-1]) for g, (r, _) in zip(gs, rows)]
    piece_cols = 2 * LANES
    pieces = [(a, slice(c0, min(c0 + piece_cols, h[1]))) for a, h in enumerate(halves) for c0 in range(0, h[1], piece_cols)]
    n_p = len(pieces)

    def body(*refs):
        p = 0
        parts = []
        for cnt in (n, r_in, n, n, r_out, n, n, n, 6):
            parts.append(refs[p:p + cnt])
            p += cnt
        g_refs, r_ins, mine, theirs, r_outs, recv, part, land, sems = parts
        r_sems = refs[p:]
        xs, xr, ss, sr, ws, wr = sems
        x, y, z = _place()
        chip = 2 * x + y
        sib = (x, y, 1 - z)
        rider.start(r_ins, r_outs, r_sems)

        def half_of(a, s, which):
            r2 = halves[a][0]
            if len(g_refs[a].shape) == 3:
                return g_refs[a].at[s, pl.ds(pl.multiple_of(which * r2, 8), r2)]
            return g_refs[a].at[pl.ds(pl.multiple_of(s * rows[a][1] + which * r2, 8), r2)]

        def exchange(i):
            a, cols = pieces[i]
            return [pltpu.make_async_remote_copy(
                src_ref=half_of(a, s, 1 - z).at[:, cols], dst_ref=recv[a].at[s, :, cols], send_sem=xs.at[i, s],
                recv_sem=xr.at[i, s], device_id=sib, device_id_type=MESH) for s in range(N_CHIPS)]

        def scatter(i):
            a, cols = pieces[i]
            return [pltpu.make_async_remote_copy(
                src_ref=part[a].at[2 * _flip(x, dx) + _flip(y, dy), :, cols], dst_ref=land[a].at[k, :, cols],
                send_sem=ss.at[i, k], recv_sem=sr.at[i, k], device_id=(_flip(x, dx), _flip(y, dy), z),
                device_id_type=MESH) for k, (dx, dy) in enumerate(_CHIP_OFFSETS)]

        def swap(i):
            a, cols = pieces[i]
            return pltpu.make_async_remote_copy(
                src_ref=mine[a].at[:, cols], dst_ref=theirs[a].at[:, cols], send_sem=ws.at[i], recv_sem=wr.at[i],
                device_id=sib, device_id_type=MESH)

        for i in range(len(pieces)):
            for cp in exchange(i):
                cp.start()
        for i, (a, cols) in enumerate(pieces):
            for cp in exchange(i):
                cp.wait()
            for s in range(N_CHIPS):
                part[a][s, :, cols] = (half_of(a, s, z)[:, cols] + recv[a][s, :, cols]).astype(BF16)
            for cp in scatter(i):
                cp.start()
        for i, (a, cols) in enumerate(pieces):
            for cp in scatter(i):
                cp.wait()
            acc = part[a][chip, :, cols].astype(F32)
            for k in range(3):
                acc = acc + land[a][k, :, cols].astype(F32)
            mine[a][:, cols] = acc
            swap(i).start()
        for i in range(len(pieces)):
            swap(i).wait()
        rider.finish(r_ins, r_outs, r_sems)

    half_shapes = [jax.ShapeDtypeStruct(h, F32) for h in halves]
    res = pl.pallas_call(
        body, name=name, in_specs=[_VMEM] * n + rider.in_specs, out_specs=[_VMEM] * (2 * n) + rider.out_specs,
        out_shape=half_shapes + half_shapes + rider.out_shapes,
        scratch_shapes=[pltpu.VMEM((N_CHIPS,) + h, F32) for h in halves] + [pltpu.VMEM((N_CHIPS,) + h, BF16) for h in halves]
        + [pltpu.VMEM((3,) + h, BF16) for h in halves]
        + [pltpu.SemaphoreType.DMA((n_p, N_CHIPS))] * 2 + [pltpu.SemaphoreType.DMA((n_p, 3))] * 2
        + [pltpu.SemaphoreType.DMA((n_p,))] * 2 + rider.sems,
        input_output_aliases={n + i: 2 * n + j for i, j in rider.aliases.items()},
        compiler_params=_params(None, VMEM_LIMIT),
    )(*gs, *rider.ins)
    return list(res[:n]), list(res[n:2 * n]), list(res[2 * n:])


SMALL_ROWS = 32
PACK_ROWS = 16


def _pack_small(st_post, st_ret, st_pre):
    d = st_post.shape[2]

    def body(po_ref, re_ref, pr_ref, o_ref):
        o_ref[...] = jnp.zeros(o_ref.shape, F32)
        o_ref[0:1, :] = pr_ref[0, 2:3, :] + pr_ref[1, 2:3, :] + pr_ref[2, 2:3, :]
        o_ref[1:2, :] = po_ref[0, 4:5, :] + po_ref[1, 4:5, :]
        o_ref[2:3, :] = po_ref[0, 5:6, :] + po_ref[1, 5:6, :]
        o_ref[3:4, 0:512] = re_ref[0, 0:1, :] + re_ref[1, 0:1, :]
        o_ref[4:5, :] = pr_ref[0, 3:4, :] + pr_ref[1, 3:4, :] + pr_ref[2, 3:4, :]
        o_ref[5:6, :] = pr_ref[0, 4:5, :] + pr_ref[1, 4:5, :] + pr_ref[2, 4:5, :]
        lane = lax.broadcasted_iota(jnp.int32, (1, LANES), 1)
        for row, src in ((6, 1), (10, 2)):
            acc = jnp.zeros((1, LANES), F32)
            for hd in range(HEADS):
                grp = re_ref[0, src:src + 1, hd * LANES:(hd + 1) * LANES] + re_ref[1, src:src + 1, hd * LANES:(hd + 1) * LANES]
                acc = acc + jnp.where(lane == hd, grp, 0.0)
            o_ref[row:row + 1, 0:LANES] = acc
        o_ref[7:8, :] = po_ref[0, 6:7, :] + po_ref[1, 6:7, :]
        o_ref[8:9, :] = pr_ref[2, 0:1, :]
        o_ref[9:10, :] = pr_ref[2, 1:2, :]
        for e in range(2):
            b = 12 + 6 * e
            o_ref[b:b + 1, :] = pr_ref[e, 0:1, :]
            o_ref[b + 1:b + 2, :] = pr_ref[e, 1:2, :]
            o_ref[b + 2:b + 3, :] = po_ref[e, 3:4, :]
            o_ref[b + 3:b + 4, :] = po_ref[e, 0:1, :]
            o_ref[b + 4:b + 5, :] = po_ref[e, 1:2, :]
            o_ref[b + 5:b + 6, :] = po_ref[e, 2:3, :]

    return pl.pallas_call(body, name="pack_small", out_shape=jax.ShapeDtypeStruct((SMALL_ROWS, d), F32))(st_post, st_ret, st_pre)


def _small_reduce(gathered):
    d = gathered.shape[2]

    def body(g_ref, o_ref):
        tot = g_ref[0, 0:PACK_ROWS, :]
        for dev in range(1, N_DEV):
            tot = tot + g_ref[dev, 0:PACK_ROWS, :]
        o_ref[0:PACK_ROWS, :] = tot
        for j in range(6):
            acc = g_ref[0, 12 + j:13 + j, :] + g_ref[0, 18 + j:19 + j, :]
            for dev in range(1, N_DEV):
                acc = acc + g_ref[dev, 12 + j:13 + j, :] + g_ref[dev, 18 + j:19 + j, :]
            if j < 2:
                acc = acc + o_ref[8 + j:9 + j, :]
            o_ref[PACK_ROWS + j:PACK_ROWS + j + 1, :] = acc
        o_ref[PACK_ROWS + 6:PACK_ROWS + 8, :] = jnp.zeros((2, d), F32)

    return pl.pallas_call(body, name="small_reduce", out_shape=jax.ShapeDtypeStruct((PACK_ROWS + 8, d), F32))(gathered)


_SMALL = (("g_attn", 0, 1024), ("g_ffn", 1, 1024), ("g_final", 2, 1024), ("g_ret", 3, 512), ("g_q_lora", 4, 384),
          ("g_kv_lora", 5, 256), ("ret_decay_fwd", 6, HEADS), ("ret_decay_bwd", 10, HEADS))
_SMALL_NAMES = tuple(s[0] for s in _SMALL) + ("c_ctx", "b_ada")


def _small_final(tot, dcc, sg8, ws, ms, vs):
    d = tot.shape[1]
    n = len(_SMALL_NAMES)

    def body(*refs):
        t_ref, dcc_ref, sg_ref = refs[0:3]
        w_refs, m_refs, v_refs = refs[3:3 + n], refs[3 + n:3 + 2 * n], refs[3 + 2 * n:3 + 3 * n]
        outs = refs[3 + 3 * n:]
        g_refs, d_refs, mo_refs, vo_refs = outs[0:n], outs[n:2 * n], outs[2 * n:3 * n], outs[3 * n:4 * n]
        l_ref = outs[4 * n]

        def update(i, g, sl=None):
            pick = (lambda r: r[...]) if sl is None else (lambda r: r[:, sl])
            dl, mn, vn = _adam_math(pick(w_refs[i]), g, pick(m_refs[i]), pick(v_refs[i]))
            if sl is None:
                g_refs[i][...], d_refs[i][...], mo_refs[i][...], vo_refs[i][...] = g, dl, mn, vn
            else:
                g_refs[i][:, sl], d_refs[i][:, sl], mo_refs[i][:, sl], vo_refs[i][:, sl] = g, dl, mn, vn

        for i, (name, row, width) in enumerate(_SMALL):
            g = t_ref[row:row + 1, 0:width]
            if name == "ret_decay_fwd":
                g = g * sg_ref[0:1, 0:width]
            elif name == "ret_decay_bwd":
                g = g * sg_ref[1:2, 0:width]
            update(i, g)
        i_cc, i_b = n - 2, n - 1
        cc = w_refs[i_cc][...]
        s = 1.0 / (1.0 + jnp.exp(-cc))
        dsilu = dcc_ref[0, 0:1, :] + dcc_ref[2, 0:1, :] + dcc_ref[4, 0:1, :] + dcc_ref[6, 0:1, :]
        update(i_cc, dsilu * (s * (1.0 + cc * (1.0 - s))))
        for j in range(6):
            update(i_b, t_ref[PACK_ROWS + j:PACK_ROWS + j + 1, :], pl.ds(j * d, d))
        l_ref[...] = jnp.broadcast_to((0.5 / d) * jnp.sum(t_ref[7:8, :], keepdims=True), l_ref.shape)

    shapes = [jax.ShapeDtypeStruct(a.shape, F32) for a in ws]
    outs = pl.pallas_call(
        body, name="small_final", out_shape=shapes * 4 + [jax.ShapeDtypeStruct((8, LANES), F32)],
    )(tot, dcc, sg8, *ws, *ms, *vs)
    return outs[0:n], outs[n:2 * n], outs[2 * n:3 * n], outs[3 * n:4 * n], outs[4 * n]


_WEIGHTS = ("c_ctx", "w_ada", "b_ada", "g_attn", "g_ffn", "w_in", "ret_decay_fwd", "ret_decay_bwd", "g_ret", "g_q_lora",
            "w_uq", "g_kv_lora", "w_ukv", "w_out", "w_ff1", "w_ff2", "g_final")
_BIG = ("w_in", "w_uq", "w_ukv", "w_out", "w_ff1", "w_ff2")
_TRANSPOSED = ("w_in", "w_uq")


def kernel(x, c, ctx, c_ctx, w_ada, b_ada, g_attn, g_ffn, w_in, ret_decay_fwd, ret_decay_bwd, g_ret, g_q_lora, w_uq, g_kv_lora, w_ukv, w_out, w_ff1, w_ff2, g_final, loss_target, m_c_ctx, m_w_ada, m_b_ada, m_g_attn, m_g_ffn, m_w_in, m_ret_decay_fwd, m_ret_decay_bwd, m_g_ret, m_g_q_lora, m_w_uq, m_g_kv_lora, m_w_ukv, m_w_out, m_w_ff1, m_w_ff2, m_g_final, v_c_ctx, v_w_ada, v_b_ada, v_g_attn, v_g_ffn, v_w_in, v_ret_decay_fwd, v_ret_decay_bwd, v_g_ret, v_g_q_lora, v_w_uq, v_g_kv_lora, v_w_ukv, v_w_out, v_w_ff1, v_w_ff2, v_g_final):
    w = dict(c_ctx=c_ctx, w_ada=w_ada, b_ada=b_ada, g_attn=g_attn, g_ffn=g_ffn, w_in=w_in, ret_decay_fwd=ret_decay_fwd,
             ret_decay_bwd=ret_decay_bwd, g_ret=g_ret, g_q_lora=g_q_lora, w_uq=w_uq, g_kv_lora=g_kv_lora, w_ukv=w_ukv,
             w_out=w_out, w_ff1=w_ff1, w_ff2=w_ff2, g_final=g_final)
    m = dict(c_ctx=m_c_ctx, w_ada=m_w_ada, b_ada=m_b_ada, g_attn=m_g_attn, g_ffn=m_g_ffn, w_in=m_w_in,
             ret_decay_fwd=m_ret_decay_fwd, ret_decay_bwd=m_ret_decay_bwd, g_ret=m_g_ret, g_q_lora=m_g_q_lora, w_uq=m_w_uq,
             g_kv_lora=m_g_kv_lora, w_ukv=m_w_ukv, w_out=m_w_out, w_ff1=m_w_ff1, w_ff2=m_w_ff2, g_final=m_g_final)
    v = dict(c_ctx=v_c_ctx, w_ada=v_w_ada, b_ada=v_b_ada, g_attn=v_g_attn, g_ffn=v_g_ffn, w_in=v_w_in,
             ret_decay_fwd=v_ret_decay_fwd, ret_decay_bwd=v_ret_decay_bwd, g_ret=v_g_ret, g_q_lora=v_g_q_lora, w_uq=v_w_uq,
             g_kv_lora=v_g_kv_lora, w_ukv=v_w_ukv, w_out=v_w_out, w_ff1=v_w_ff1, w_ff2=v_w_ff2, g_final=v_g_final)
    xi, yi, ci = lax.axis_index("x"), lax.axis_index("y"), lax.axis_index("c")
    chip = 2 * xi + yi
    dev = 2 * chip + ci
    nex, seq, d = x.shape
    n_ada = w_ada.shape[2]

    dec = jnp.zeros((8, LANES), F32).at[0, :HEADS].set(ret_decay_fwd[0]).at[1, :HEADS].set(ret_decay_bwd[0])
    lg8, sg8 = _decay_prep(dec)
    lg = lg8[:2, :HEADS]

    def shard_of(t, k):
        return t[k][0].T if k in _TRANSPOSED else t[k][0]

    shard = {k: shard_of(w, k) for k in _BIG}
    head_rows = MLA_NOPE + MLA_ROPE
    shard["w_uq"] = jnp.pad(shard["w_uq"], ((0, MLA_HEAD - head_rows), (0, 0)))
    slot = chip.reshape(1).astype(jnp.int32)
    core = ci.reshape(1).astype(jnp.int32)
    slots = {k: _cast_into_slot(shard[k], slot, "cast_" + k) for k in _EARLY}
    half_ff = shard["w_ff2"].shape[0] // 2
    late_pieces = [(shard["w_out"], 0, shard["w_out"].shape[0]), (shard["w_ff1"], 0, shard["w_ff1"].shape[0]),
                   (shard["w_ff2"], 0, half_ff), (shard["w_ff2"], half_ff, half_ff)]
    late_slots, (w_in_f, w_uq_k, w_ukv_k, c8) = _cast_into_slots(
        late_pieces, slot, "cast_late",
        rider=_merge_riders(_gather_rider([slots[k] for k in _EARLY]),
                            _gather8_rider(jnp.pad(c, ((0, 8 - nex), (0, 0))), in_vmem=False)))

    a_in = jnp.concatenate([c8[:, :nex].reshape(N_DEV * nex, d), c_ctx.reshape(1, d), jnp.zeros((7, d), F32)], axis=0)
    b_sh = lax.dynamic_slice(b_ada, (0, chip * n_ada), (1, n_ada))
    mod4 = _run_rider(_gather_rider([_mod_fwd(a_in, w_ada[0], b_sh, slot)]), "ag_mod")[0]
    w_in_k = jnp.pad(w_in_f.reshape(IN_COLS, d), ((0, IN_PAD - IN_COLS), (0, 0)))
    mod_all = mod4.transpose(1, 0, 2).reshape(a_in.shape[0], N_CHIPS * n_ada)
    mod_me = lax.dynamic_slice(mod_all, (nex * dev, 0), (nex, N_CHIPS * n_ada)).reshape(nex, 6, d)
    mod_c = mod_all[N_DEV * nex].reshape(1, 6, d)
    modv = jnp.pad(jnp.concatenate([mod_me, mod_c], axis=0), ((0, 0), (0, 2), (0, 0)))

    gx, g_early, late, st_post, st_ret, st_pre = _local_step(
        x, ctx, loss_target, modv, lg, g_attn, g_ffn, g_final.reshape(1, d), g_ret, g_q_lora, g_kv_lora,
        w_in_k, w_uq_k, w_ukv_k, late_slots, (core, slot))

    mine, theirs, (*late_theirs, gathered) = _reduce_scatter_vmem(
        g_early, [(IN_COLS // N_CHIPS, IN_COLS // N_CHIPS), (head_rows, MLA_HEAD), (KV_LORA, KV_LORA)],
        _merge_riders(_swap_rider(late), _gather8_rider(_pack_small(st_post, st_ret, st_pre))), "rs_early")
    tot = _small_reduce(gathered)
    dm = jnp.concatenate([
        gathered[:, 12:24].reshape(N_DEV * nex, 6 * d),
        jnp.concatenate([tot[8:10].reshape(1, 2 * d), jnp.zeros((1, 4 * d), F32)], axis=1),
        jnp.zeros((7, 6 * d), F32)], axis=0)
    dm_sh = lax.dynamic_slice(dm, (0, chip * n_ada), (dm.shape[0], n_ada))
    g_ada, da, *ada_outs = _mod_bwd_adamw(a_in, dm_sh, w_ada[0], m["w_ada"][0], v["w_ada"][0])
    dcc = _allgather8(da[N_DEV * nex:], "ag_dcc")
    halves = dict(zip(_EARLY, zip(mine, theirs)))
    halves.update(zip(_LATE, zip(late, late_theirs)))
    grad, delta, new_m, new_v = {}, {}, {}, {}
    for k in _BIG:
        a, b = halves[k]
        res = _adamw_halves(shard_of(w, k), a, b, shard_of(m, k), shard_of(v, k), core, "adamw_" + k)
        grad[k], delta[k], new_m[k], new_v[k] = [(o.T if k in _TRANSPOSED else o).reshape(w[k].shape) for o in res]

    shp = w_ada.shape
    grad["w_ada"] = g_ada.reshape(shp)
    delta["w_ada"], new_m["w_ada"], new_v["w_ada"] = [o.reshape(shp) for o in ada_outs]
    rows = [{k: t[k].reshape(1, -1) for k in _SMALL_NAMES} for t in (w, m, v)]
    small = _small_final(tot, dcc, sg8, *[[t[k] for k in _SMALL_NAMES] for t in rows])
    for res, outs in zip((grad, delta, new_m, new_v), small[:4]):
        for k, o in zip(_SMALL_NAMES, outs):
            res[k] = o.reshape(w[k].shape)
    return (small[4][0, 0], gx, *[grad[k] for k in _WEIGHTS], *[delta[k] for k in _WEIGHTS],
            *[new_m[k] for k in _WEIGHTS], *[new_v[k] for k in _WEIGHTS])
```
